```python
import math
import jax, jax.numpy as jnp
from jax import lax
import numpy as np

D_MODEL = 1024
BATCH = 8
SEQ = 8192
DEPTH = 1

HEAD_DIM = 64
HEADS_PER_GROUP = 4
DILATED_GROUPS = ((128, 1), (512, 4), (2048, 16))
N_ATTN_GROUPS = len(DILATED_GROUPS)
N_ATTN_HEADS = N_ATTN_GROUPS * HEADS_PER_GROUP
ATTN_WIDTH = N_ATTN_HEADS * HEAD_DIM
ATTN_OUT_WIDTH = HEADS_PER_GROUP * HEAD_DIM
BLOCK = 128
N_BUCKETS = 32
MAX_DISTANCE = 2048
NEG_INF = -1e30
SSM_GROUP = 16
SSM_WIDTH = 512
SSM_GROUPS = SSM_WIDTH // SSM_GROUP
SSM_STATE = 64
DT_MIN = 1e-3
DT_MAX = 1e-1
D_FF = 2816
EPS = 1e-6
IN_WIDTH = 3 * ATTN_WIDTH + SSM_WIDTH + 2 * D_MODEL

kernel_name = "hybrid_dilated_attn_s5_macaron"


def rms_norm(x, g):
    xf = x.astype(jnp.float32)
    y = xf * lax.rsqrt(jnp.mean(xf * xf, axis=-1, keepdims=True) + EPS)
    return (y * g.astype(jnp.float32)).astype(x.dtype)


def swiglu(h, w_gate, w_up, w_down):
    return (jax.nn.silu(h @ w_gate) * (h @ w_up)) @ w_down


def t5_bucket(dist):
    max_exact = N_BUCKETS // 2
    d = jnp.maximum(dist, 1).astype(jnp.float32)
    large = max_exact + (jnp.log(d / max_exact) / math.log(MAX_DISTANCE / max_exact)
                         * (N_BUCKETS - max_exact)).astype(jnp.int32)
    large = jnp.minimum(large, N_BUCKETS - 1)
    return jnp.where(dist < max_exact, dist, large)


def dilated_group_attention(q, k, v, bias_table_g, window, dilation):
    B, L, H, Dh = q.shape
    M = L // dilation
    n_steps = window // dilation
    nb = -(-M // BLOCK)
    Mp = nb * BLOCK

    def to_sub(t):
        t = t.reshape(B, M, dilation, H, Dh).transpose(0, 2, 1, 3, 4).reshape(B * dilation, M, H, Dh)
        t = jnp.pad(t, ((0, 0), (0, Mp - M), (0, 0), (0, 0)))
        return t.reshape(B * dilation, nb, BLOCK, H, Dh)

    def with_prev(t):
        prev = jnp.pad(t, ((0, 0), (1, 0), (0, 0), (0, 0), (0, 0)))[:, :-1]
        return jnp.concatenate([prev, t], axis=2)

    qs = to_sub(q).astype(jnp.float32)
    kb = with_prev(to_sub(k)).astype(jnp.float32)
    vb = with_prev(to_sub(v)).astype(jnp.float32)

    qi = jnp.arange(BLOCK)[:, None]
    kj = jnp.arange(2 * BLOCK)[None, :]
    steps = qi + BLOCK - kj
    band = (steps >= 0) & (steps <= n_steps)
    first_ok = (jnp.arange(nb)[:, None] > 0) | (kj >= BLOCK)
    mask = band[None] & first_ok[:, None, :]
    bucket = t5_bucket(jnp.maximum(steps, 0) * dilation)
    bias = bias_table_g.astype(jnp.float32)[bucket].transpose(2, 0, 1)

    logits = jnp.einsum('bnqhd,bnkhd->bhnqk', qs, kb) + bias[None, :, None]
    logits = jnp.where(mask[None, None], logits, NEG_INF)
    m = jnp.max(logits, axis=-1, keepdims=True)
    p = jnp.exp(logits - m)
    denom = jnp.sum(p, axis=-1)
    o = jnp.einsum('bhnqk,bnkhd->bnqhd', p, vb) / denom.transpose(0, 2, 3, 1)[..., None]
    lse = m[..., 0] + jnp.log(denom)

    o = o.reshape(B * dilation, Mp, H, Dh)[:, :M]
    o = o.reshape(B, dilation, M, H, Dh).transpose(0, 2, 1, 3, 4).reshape(B, L, H, Dh)
    lse = lse.transpose(0, 2, 3, 1).reshape(B * dilation, Mp, H)[:, :M]
    lse = lse.reshape(B, dilation, M, H).transpose(0, 2, 1, 3).reshape(B, L, H)
    return o, lse


def s5_mixer(u, a_re, a_im, log_dt, b_re, b_im, c_re, c_im, d_skip):
    B, L, _ = u.shape
    uf = u.astype(jnp.float32).reshape(B, L, SSM_GROUPS, SSM_GROUP)
    lam_re = a_re.astype(jnp.float32)
    lam_im = a_im.astype(jnp.float32)
    dt = jnp.exp(log_dt.astype(jnp.float32))[:, None]
    mag = jnp.exp(lam_re * dt)
    ab_re = mag * jnp.cos(lam_im * dt)
    ab_im = mag * jnp.sin(lam_im * dt)
    den = lam_re * lam_re + lam_im * lam_im
    xr = ab_re - 1.0
    coef_re = (xr * lam_re + ab_im * lam_im) / den
    coef_im = (ab_im * lam_re - xr * lam_im) / den
    br = b_re.astype(jnp.float32)
    bi = b_im.astype(jnp.float32)
    bb_re = coef_re[..., None] * br - coef_im[..., None] * bi
    bb_im = coef_re[..., None] * bi + coef_im[..., None] * br
    bu_re = jnp.einsum('gnc,blgc->lbgn', bb_re, uf)
    bu_im = jnp.einsum('gnc,blgc->lbgn', bb_im, uf)
    a_seq_re = jnp.broadcast_to(ab_re[None, None], (L, 1, SSM_GROUPS, SSM_STATE))
    a_seq_im = jnp.broadcast_to(ab_im[None, None], (L, 1, SSM_GROUPS, SSM_STATE))

    def combine(left, right):
        al_re, al_im, bl_re, bl_im = left
        ar_re, ar_im, brr, bri = right
        return (al_re * ar_re - al_im * ar_im,
                al_re * ar_im + al_im * ar_re,
                ar_re * bl_re - ar_im * bl_im + brr,
                ar_re * bl_im + ar_im * bl_re + bri)

    _, _, s_re, s_im = lax.associative_scan(combine, (a_seq_re, a_seq_im, bu_re, bu_im), axis=0)
    y = (jnp.einsum('gcn,lbgn->blgc', c_re.astype(jnp.float32), s_re)
         - jnp.einsum('gcn,lbgn->blgc', c_im.astype(jnp.float32), s_im)
         + d_skip.astype(jnp.float32).reshape(SSM_GROUPS, SSM_GROUP) * uf)
    return y.reshape(B, L, SSM_WIDTH).astype(u.dtype)


def _fwd_setup_inputs(seed: int = 0) -> dict:
    key = jax.random.key(seed)
    ks = iter(jax.random.split(key, 32))
    f32 = jnp.float32

    def nrm(shape, scale):
        return jax.random.normal(next(ks), shape, f32) * scale

    def gain(shape):
        return 1.0 + 0.05 * jax.random.normal(next(ks), shape, f32)

    L_ = DEPTH
    n_idx = jnp.arange(SSM_STATE, dtype=f32)
    return {
        "x": jax.random.normal(next(ks), (BATCH, SEQ, D_MODEL), f32),
        "ffn1_norm": gain((L_, D_MODEL)),
        "ffn1_w_gate": nrm((L_, D_MODEL, D_FF), D_MODEL ** -0.5),
        "ffn1_w_up": nrm((L_, D_MODEL, D_FF), D_MODEL ** -0.5),
        "ffn1_w_down": nrm((L_, D_FF, D_MODEL), D_FF ** -0.5),
        "mix_norm": gain((L_, D_MODEL)),
        "w_in": nrm((L_, D_MODEL, IN_WIDTH), D_MODEL ** -0.5),
        "gate_bias": nrm((L_, 2 * D_MODEL), 0.1),
        "rel_bias_table": nrm((N_BUCKETS, N_ATTN_HEADS), 0.5),
        "ssm_a_re": -0.5 + nrm((L_, SSM_GROUPS, SSM_STATE), 0.01),
        "ssm_a_im": math.pi * n_idx + nrm((L_, SSM_GROUPS, SSM_STATE), 0.01),
        "ssm_log_dt": jax.random.uniform(next(ks), (L_, SSM_GROUPS), f32,
                                         math.log(DT_MIN), math.log(DT_MAX)),
        "ssm_b_re": nrm((L_, SSM_GROUPS, SSM_STATE, SSM_GROUP), (2 * SSM_GROUP) ** -0.5),
        "ssm_b_im": nrm((L_, SSM_GROUPS, SSM_STATE, SSM_GROUP), (2 * SSM_GROUP) ** -0.5),
        "ssm_c_re": nrm((L_, SSM_GROUPS, SSM_GROUP, SSM_STATE), (2 * SSM_STATE) ** -0.5),
        "ssm_c_im": nrm((L_, SSM_GROUPS, SSM_GROUP, SSM_STATE), (2 * SSM_STATE) ** -0.5),
        "ssm_d": nrm((L_, SSM_WIDTH), 1.0),
        "ssm_w_glu": nrm((L_, SSM_WIDTH, 2 * SSM_WIDTH), SSM_WIDTH ** -0.5),
        "w_attn_branch": nrm((L_, ATTN_OUT_WIDTH, D_MODEL), ATTN_OUT_WIDTH ** -0.5),
        "w_ssm_branch": nrm((L_, SSM_WIDTH, D_MODEL), SSM_WIDTH ** -0.5),
        "w_out": nrm((L_, D_MODEL, D_MODEL), D_MODEL ** -0.5),
        "ffn2_norm": gain((L_, D_MODEL)),
        "ffn2_w_gate": nrm((L_, D_MODEL, D_FF), D_MODEL ** -0.5),
        "ffn2_w_up": nrm((L_, D_MODEL, D_FF), D_MODEL ** -0.5),
        "ffn2_w_down": nrm((L_, D_FF, D_MODEL), D_FF ** -0.5),
        "final_norm": gain((D_MODEL,)),
    }


def _fwd_reference(x, ffn1_norm, ffn1_w_gate, ffn1_w_up, ffn1_w_down, mix_norm, w_in, gate_bias,
              rel_bias_table, ssm_a_re, ssm_a_im, ssm_log_dt, ssm_b_re, ssm_b_im, ssm_c_re,
              ssm_c_im, ssm_d, ssm_w_glu, w_attn_branch, w_ssm_branch, w_out, ffn2_norm,
              ffn2_w_gate, ffn2_w_up, ffn2_w_down, final_norm):
    B, L, _ = x.shape
    scale = HEAD_DIM ** -0.5
    for l in range(DEPTH):
        x = x + 0.5 * swiglu(rms_norm(x, ffn1_norm[l]), ffn1_w_gate[l], ffn1_w_up[l], ffn1_w_down[l])

        h = rms_norm(x, mix_norm[l])
        z = h @ w_in[l]
        c0 = ATTN_WIDTH
        q = z[..., :c0].reshape(B, L, N_ATTN_HEADS, HEAD_DIM) * scale
        k = z[..., c0:2 * c0].reshape(B, L, N_ATTN_HEADS, HEAD_DIM)
        v = z[..., 2 * c0:3 * c0].reshape(B, L, N_ATTN_HEADS, HEAD_DIM)
        c1 = 3 * c0
        u = z[..., c1:c1 + SSM_WIDTH]
        c2 = c1 + SSM_WIDTH
        g_attn = jax.nn.sigmoid(z[..., c2:c2 + D_MODEL] + gate_bias[l, :D_MODEL])
        g_ssm = jax.nn.sigmoid(z[..., c2 + D_MODEL:] + gate_bias[l, D_MODEL:])

        outs, lses = [], []
        for g, (window, dilation) in enumerate(DILATED_GROUPS):
            hs = slice(g * HEADS_PER_GROUP, (g + 1) * HEADS_PER_GROUP)
            o_g, lse_g = dilated_group_attention(q[:, :, hs], k[:, :, hs], v[:, :, hs],
                                                 rel_bias_table[:, hs], window, dilation)
            outs.append(o_g)
            lses.append(lse_g)
        o_stack = jnp.stack(outs, axis=2)
        w_grp = jax.nn.softmax(jnp.stack(lses, axis=2), axis=2)
        o_attn = jnp.sum(w_grp[..., None] * o_stack, axis=2).reshape(B, L, ATTN_OUT_WIDTH)
        y_attn = o_attn.astype(x.dtype) @ w_attn_branch[l]

        y_s = jax.nn.gelu(s5_mixer(u, ssm_a_re[l], ssm_a_im[l], ssm_log_dt[l], ssm_b_re[l],
                                   ssm_b_im[l], ssm_c_re[l], ssm_c_im[l], ssm_d[l]))
        glu = y_s @ ssm_w_glu[l]
        y_s = glu[..., :SSM_WIDTH] * jax.nn.sigmoid(glu[..., SSM_WIDTH:])
        y_ssm = y_s @ w_ssm_branch[l]

        x = x + (g_attn * y_attn + g_ssm * y_ssm) @ w_out[l]

        x = x + 0.5 * swiglu(rms_norm(x, ffn2_norm[l]), ffn2_w_gate[l], ffn2_w_up[l], ffn2_w_down[l])
    return rms_norm(x, final_norm)


import jax as _jax
import jax.numpy as _jnp

TWIN_FORMAT = 'train_step'
FWD_PARAMS = ['x', 'ffn1_norm', 'ffn1_w_gate', 'ffn1_w_up', 'ffn1_w_down', 'mix_norm', 'w_in', 'gate_bias', 'rel_bias_table', 'ssm_a_re', 'ssm_a_im', 'ssm_log_dt', 'ssm_b_re', 'ssm_b_im', 'ssm_c_re', 'ssm_c_im', 'ssm_d', 'ssm_w_glu', 'w_attn_branch', 'w_ssm_branch', 'w_out', 'ffn2_norm', 'ffn2_w_gate', 'ffn2_w_up', 'ffn2_w_down', 'final_norm']
TWIN_WEIGHTS = ['ffn1_norm', 'ffn1_w_gate', 'ffn1_w_up', 'ffn1_w_down', 'mix_norm', 'w_in', 'gate_bias', 'rel_bias_table', 'ssm_a_re', 'ssm_a_im', 'ssm_log_dt', 'ssm_b_re', 'ssm_b_im', 'ssm_c_re', 'ssm_c_im', 'ssm_d', 'ssm_w_glu', 'w_attn_branch', 'w_ssm_branch', 'w_out', 'ffn2_norm', 'ffn2_w_gate', 'ffn2_w_up', 'ffn2_w_down', 'final_norm']
TWIN_DIFF_INPUT = 'x'
TWIN_INPUTS = ['x', 'ffn1_norm', 'ffn1_w_gate', 'ffn1_w_up', 'ffn1_w_down', 'mix_norm', 'w_in', 'gate_bias', 'rel_bias_table', 'ssm_a_re', 'ssm_a_im', 'ssm_log_dt', 'ssm_b_re', 'ssm_b_im', 'ssm_c_re', 'ssm_c_im', 'ssm_d', 'ssm_w_glu', 'w_attn_branch', 'w_ssm_branch', 'w_out', 'ffn2_norm', 'ffn2_w_gate', 'ffn2_w_up', 'ffn2_w_down', 'final_norm', 'loss_target', 'm_ffn1_norm', 'm_ffn1_w_gate', 'm_ffn1_w_up', 'm_ffn1_w_down', 'm_mix_norm', 'm_w_in', 'm_gate_bias', 'm_rel_bias_table', 'm_ssm_a_re', 'm_ssm_a_im', 'm_ssm_log_dt', 'm_ssm_b_re', 'm_ssm_b_im', 'm_ssm_c_re', 'm_ssm_c_im', 'm_ssm_d', 'm_ssm_w_glu', 'm_w_attn_branch', 'm_w_ssm_branch', 'm_w_out', 'm_ffn2_norm', 'm_ffn2_w_gate', 'm_ffn2_w_up', 'm_ffn2_w_down', 'm_final_norm', 'v_ffn1_norm', 'v_ffn1_w_gate', 'v_ffn1_w_up', 'v_ffn1_w_down', 'v_mix_norm', 'v_w_in', 'v_gate_bias', 'v_rel_bias_table', 'v_ssm_a_re', 'v_ssm_a_im', 'v_ssm_log_dt', 'v_ssm_b_re', 'v_ssm_b_im', 'v_ssm_c_re', 'v_ssm_c_im', 'v_ssm_d', 'v_ssm_w_glu', 'v_w_attn_branch', 'v_w_ssm_branch', 'v_w_out', 'v_ffn2_norm', 'v_ffn2_w_gate', 'v_ffn2_w_up', 'v_ffn2_w_down', 'v_final_norm']
TWIN_OUTPUTS = ['loss', 'grad_x', 'grad_ffn1_norm', 'grad_ffn1_w_gate', 'grad_ffn1_w_up', 'grad_ffn1_w_down', 'grad_mix_norm', 'grad_w_in', 'grad_gate_bias', 'grad_rel_bias_table', 'grad_ssm_a_re', 'grad_ssm_a_im', 'grad_ssm_log_dt', 'grad_ssm_b_re', 'grad_ssm_b_im', 'grad_ssm_c_re', 'grad_ssm_c_im', 'grad_ssm_d', 'grad_ssm_w_glu', 'grad_w_attn_branch', 'grad_w_ssm_branch', 'grad_w_out', 'grad_ffn2_norm', 'grad_ffn2_w_gate', 'grad_ffn2_w_up', 'grad_ffn2_w_down', 'grad_final_norm', 'delta_ffn1_norm', 'delta_ffn1_w_gate', 'delta_ffn1_w_up', 'delta_ffn1_w_down', 'delta_mix_norm', 'delta_w_in', 'delta_gate_bias', 'delta_rel_bias_table', 'delta_ssm_a_re', 'delta_ssm_a_im', 'delta_ssm_log_dt', 'delta_ssm_b_re', 'delta_ssm_b_im', 'delta_ssm_c_re', 'delta_ssm_c_im', 'delta_ssm_d', 'delta_ssm_w_glu', 'delta_w_attn_branch', 'delta_w_ssm_branch', 'delta_w_out', 'delta_ffn2_norm', 'delta_ffn2_w_gate', 'delta_ffn2_w_up', 'delta_ffn2_w_down', 'delta_final_norm', 'new_m_ffn1_norm', 'new_m_ffn1_w_gate', 'new_m_ffn1_w_up', 'new_m_ffn1_w_down', 'new_m_mix_norm', 'new_m_w_in', 'new_m_gate_bias', 'new_m_rel_bias_table', 'new_m_ssm_a_re', 'new_m_ssm_a_im', 'new_m_ssm_log_dt', 'new_m_ssm_b_re', 'new_m_ssm_b_im', 'new_m_ssm_c_re', 'new_m_ssm_c_im', 'new_m_ssm_d', 'new_m_ssm_w_glu', 'new_m_w_attn_branch', 'new_m_w_ssm_branch', 'new_m_w_out', 'new_m_ffn2_norm', 'new_m_ffn2_w_gate', 'new_m_ffn2_w_up', 'new_m_ffn2_w_down', 'new_m_final_norm', 'new_v_ffn1_norm', 'new_v_ffn1_w_gate', 'new_v_ffn1_w_up', 'new_v_ffn1_w_down', 'new_v_mix_norm', 'new_v_w_in', 'new_v_gate_bias', 'new_v_rel_bias_table', 'new_v_ssm_a_re', 'new_v_ssm_a_im', 'new_v_ssm_log_dt', 'new_v_ssm_b_re', 'new_v_ssm_b_im', 'new_v_ssm_c_re', 'new_v_ssm_c_im', 'new_v_ssm_d', 'new_v_ssm_w_glu', 'new_v_w_attn_branch', 'new_v_w_ssm_branch', 'new_v_w_out', 'new_v_ffn2_norm', 'new_v_ffn2_w_gate', 'new_v_ffn2_w_up', 'new_v_ffn2_w_down', 'new_v_final_norm']
TWIN_LEAF_KINDS = {'loss': 'loss', 'grad_x': 'grad_x', 'grad_ffn1_norm': 'grad_w', 'grad_ffn1_w_gate': 'grad_w', 'grad_ffn1_w_up': 'grad_w', 'grad_ffn1_w_down': 'grad_w', 'grad_mix_norm': 'grad_w', 'grad_w_in': 'grad_w', 'grad_gate_bias': 'grad_w', 'grad_rel_bias_table': 'grad_w', 'grad_ssm_a_re': 'grad_w', 'grad_ssm_a_im': 'grad_w', 'grad_ssm_log_dt': 'grad_w', 'grad_ssm_b_re': 'grad_w', 'grad_ssm_b_im': 'grad_w', 'grad_ssm_c_re': 'grad_w', 'grad_ssm_c_im': 'grad_w', 'grad_ssm_d': 'grad_w', 'grad_ssm_w_glu': 'grad_w', 'grad_w_attn_branch': 'grad_w', 'grad_w_ssm_branch': 'grad_w', 'grad_w_out': 'grad_w', 'grad_ffn2_norm': 'grad_w', 'grad_ffn2_w_gate': 'grad_w', 'grad_ffn2_w_up': 'grad_w', 'grad_ffn2_w_down': 'grad_w', 'grad_final_norm': 'grad_w', 'delta_ffn1_norm': 'delta_w', 'delta_ffn1_w_gate': 'delta_w', 'delta_ffn1_w_up': 'delta_w', 'delta_ffn1_w_down': 'delta_w', 'delta_mix_norm': 'delta_w', 'delta_w_in': 'delta_w', 'delta_gate_bias': 'delta_w', 'delta_rel_bias_table': 'delta_w', 'delta_ssm_a_re': 'delta_w', 'delta_ssm_a_im': 'delta_w', 'delta_ssm_log_dt': 'delta_w', 'delta_ssm_b_re': 'delta_w', 'delta_ssm_b_im': 'delta_w', 'delta_ssm_c_re': 'delta_w', 'delta_ssm_c_im': 'delta_w', 'delta_ssm_d': 'delta_w', 'delta_ssm_w_glu': 'delta_w', 'delta_w_attn_branch': 'delta_w', 'delta_w_ssm_branch': 'delta_w', 'delta_w_out': 'delta_w', 'delta_ffn2_norm': 'delta_w', 'delta_ffn2_w_gate': 'delta_w', 'delta_ffn2_w_up': 'delta_w', 'delta_ffn2_w_down': 'delta_w', 'delta_final_norm': 'delta_w', 'new_m_ffn1_norm': 'new_m', 'new_m_ffn1_w_gate': 'new_m', 'new_m_ffn1_w_up': 'new_m', 'new_m_ffn1_w_down': 'new_m', 'new_m_mix_norm': 'new_m', 'new_m_w_in': 'new_m', 'new_m_gate_bias': 'new_m', 'new_m_rel_bias_table': 'new_m', 'new_m_ssm_a_re': 'new_m', 'new_m_ssm_a_im': 'new_m', 'new_m_ssm_log_dt': 'new_m', 'new_m_ssm_b_re': 'new_m', 'new_m_ssm_b_im': 'new_m', 'new_m_ssm_c_re': 'new_m', 'new_m_ssm_c_im': 'new_m', 'new_m_ssm_d': 'new_m', 'new_m_ssm_w_glu': 'new_m', 'new_m_w_attn_branch': 'new_m', 'new_m_w_ssm_branch': 'new_m', 'new_m_w_out': 'new_m', 'new_m_ffn2_norm': 'new_m', 'new_m_ffn2_w_gate': 'new_m', 'new_m_ffn2_w_up': 'new_m', 'new_m_ffn2_w_down': 'new_m', 'new_m_final_norm': 'new_m', 'new_v_ffn1_norm': 'new_v', 'new_v_ffn1_w_gate': 'new_v', 'new_v_ffn1_w_up': 'new_v', 'new_v_ffn1_w_down': 'new_v', 'new_v_mix_norm': 'new_v', 'new_v_w_in': 'new_v', 'new_v_gate_bias': 'new_v', 'new_v_rel_bias_table': 'new_v', 'new_v_ssm_a_re': 'new_v', 'new_v_ssm_a_im': 'new_v', 'new_v_ssm_log_dt': 'new_v', 'new_v_ssm_b_re': 'new_v', 'new_v_ssm_b_im': 'new_v', 'new_v_ssm_c_re': 'new_v', 'new_v_ssm_c_im': 'new_v', 'new_v_ssm_d': 'new_v', 'new_v_ssm_w_glu': 'new_v', 'new_v_w_attn_branch': 'new_v', 'new_v_w_ssm_branch': 'new_v', 'new_v_w_out': 'new_v', 'new_v_ffn2_norm': 'new_v', 'new_v_ffn2_w_gate': 'new_v', 'new_v_ffn2_w_up': 'new_v', 'new_v_ffn2_w_down': 'new_v', 'new_v_final_norm': 'new_v'}


def _forward(args):
    return _fwd_reference(*[args[k] for k in FWD_PARAMS])


def _output_shape():
    def fwd():
        inp = _fwd_setup_inputs(0)
        return _fwd_reference(*[inp[k] for k in FWD_PARAMS])
    out = _jax.eval_shape(fwd)
    return out.shape, out.dtype

N_MICROBATCH = 1
ADAM_LR = 0.001
ADAM_B1 = 0.9
ADAM_B2 = 0.999
ADAM_EPS = 1e-08
ADAM_WD = 0.01
ADAM_STEP = 10
PER_EXAMPLE_BATCH_AXIS = {'x': 0, 'loss_target': 0}
SHARED_INPUTS = []
_WEIGHT_DTYPES = {'ffn1_norm': _jnp.float32, 'ffn1_w_gate': _jnp.float32, 'ffn1_w_up': _jnp.float32, 'ffn1_w_down': _jnp.float32, 'mix_norm': _jnp.float32, 'w_in': _jnp.float32, 'gate_bias': _jnp.float32, 'rel_bias_table': _jnp.float32, 'ssm_a_re': _jnp.float32, 'ssm_a_im': _jnp.float32, 'ssm_log_dt': _jnp.float32, 'ssm_b_re': _jnp.float32, 'ssm_b_im': _jnp.float32, 'ssm_c_re': _jnp.float32, 'ssm_c_im': _jnp.float32, 'ssm_d': _jnp.float32, 'ssm_w_glu': _jnp.float32, 'w_attn_branch': _jnp.float32, 'w_ssm_branch': _jnp.float32, 'w_out': _jnp.float32, 'ffn2_norm': _jnp.float32, 'ffn2_w_gate': _jnp.float32, 'ffn2_w_up': _jnp.float32, 'ffn2_w_down': _jnp.float32, 'final_norm': _jnp.float32}
MOMENT_SCALE = {'ffn1_norm': 1.172991e-01, 'ffn1_w_gate': 4.726329e-02, 'ffn1_w_up': 4.575034e-02, 'ffn1_w_down': 7.582980e-02, 'mix_norm': 6.784272e-02, 'w_in': 3.107027e-02, 'gate_bias': 1.457491e-02, 'rel_bias_table': 3.627878e-02, 'ssm_a_re': 3.665050e-03, 'ssm_a_im': 3.512392e-03, 'ssm_log_dt': 4.415778e+00, 'ssm_b_re': 2.315329e-03, 'ssm_b_im': 2.310863e-03, 'ssm_c_re': 4.902672e-03, 'ssm_c_im': 4.733011e-03, 'ssm_d': 7.876660e-02, 'ssm_w_glu': 5.257630e-02, 'w_attn_branch': 2.713962e-02, 'w_ssm_branch': 4.950365e-02, 'w_out': 5.251704e-02, 'ffn2_norm': 1.016382e-01, 'ffn2_w_gate': 4.287435e-02, 'ffn2_w_up': 4.189142e-02, 'ffn2_w_down': 6.971417e-02, 'final_norm': 6.403343e+01}


def _to_microbatches(a, axis):
    t = _jnp.moveaxis(a, axis, 0)
    t = t.reshape((N_MICROBATCH, t.shape[0] // N_MICROBATCH) + t.shape[1:])
    return _jnp.moveaxis(t, 1, axis + 1)


def setup_inputs(seed: int = 0) -> dict:
    inp = _fwd_setup_inputs(seed)
    key = _jax.random.fold_in(_jax.random.key(seed), 7919)
    shape, _ = _output_shape()
    out = dict(inp)
    out["loss_target"] = _jax.random.normal(_jax.random.fold_in(key, 0), shape, _jnp.float32)
    for i, name in enumerate(TWIN_WEIGHTS):
        w = inp[name].astype(_jnp.float32)
        if MOMENT_SCALE is None:
            s = _jnp.sqrt(_jnp.mean(_jnp.square(w)) + 1e-30)
        else:
            s = MOMENT_SCALE[name]
        km, kv = _jax.random.split(_jax.random.fold_in(key, i + 1))
        out[name] = w
        out["m_" + name] = s * _jax.random.normal(km, w.shape, _jnp.float32)
        out["v_" + name] = (s * s) * _jax.random.uniform(kv, w.shape, _jnp.float32, 0.5, 1.5)
    if N_MICROBATCH > 1:
        for name, axis in PER_EXAMPLE_BATCH_AXIS.items():
            out[name] = _to_microbatches(out[name], axis)
    return {'x': out['x'], 'ffn1_norm': out['ffn1_norm'], 'ffn1_w_gate': out['ffn1_w_gate'], 'ffn1_w_up': out['ffn1_w_up'], 'ffn1_w_down': out['ffn1_w_down'], 'mix_norm': out['mix_norm'], 'w_in': out['w_in'], 'gate_bias': out['gate_bias'], 'rel_bias_table': out['rel_bias_table'], 'ssm_a_re': out['ssm_a_re'], 'ssm_a_im': out['ssm_a_im'], 'ssm_log_dt': out['ssm_log_dt'], 'ssm_b_re': out['ssm_b_re'], 'ssm_b_im': out['ssm_b_im'], 'ssm_c_re': out['ssm_c_re'], 'ssm_c_im': out['ssm_c_im'], 'ssm_d': out['ssm_d'], 'ssm_w_glu': out['ssm_w_glu'], 'w_attn_branch': out['w_attn_branch'], 'w_ssm_branch': out['w_ssm_branch'], 'w_out': out['w_out'], 'ffn2_norm': out['ffn2_norm'], 'ffn2_w_gate': out['ffn2_w_gate'], 'ffn2_w_up': out['ffn2_w_up'], 'ffn2_w_down': out['ffn2_w_down'], 'final_norm': out['final_norm'], 'loss_target': out['loss_target'], 'm_ffn1_norm': out['m_ffn1_norm'], 'm_ffn1_w_gate': out['m_ffn1_w_gate'], 'm_ffn1_w_up': out['m_ffn1_w_up'], 'm_ffn1_w_down': out['m_ffn1_w_down'], 'm_mix_norm': out['m_mix_norm'], 'm_w_in': out['m_w_in'], 'm_gate_bias': out['m_gate_bias'], 'm_rel_bias_table': out['m_rel_bias_table'], 'm_ssm_a_re': out['m_ssm_a_re'], 'm_ssm_a_im': out['m_ssm_a_im'], 'm_ssm_log_dt': out['m_ssm_log_dt'], 'm_ssm_b_re': out['m_ssm_b_re'], 'm_ssm_b_im': out['m_ssm_b_im'], 'm_ssm_c_re': out['m_ssm_c_re'], 'm_ssm_c_im': out['m_ssm_c_im'], 'm_ssm_d': out['m_ssm_d'], 'm_ssm_w_glu': out['m_ssm_w_glu'], 'm_w_attn_branch': out['m_w_attn_branch'], 'm_w_ssm_branch': out['m_w_ssm_branch'], 'm_w_out': out['m_w_out'], 'm_ffn2_norm': out['m_ffn2_norm'], 'm_ffn2_w_gate': out['m_ffn2_w_gate'], 'm_ffn2_w_up': out['m_ffn2_w_up'], 'm_ffn2_w_down': out['m_ffn2_w_down'], 'm_final_norm': out['m_final_norm'], 'v_ffn1_norm': out['v_ffn1_norm'], 'v_ffn1_w_gate': out['v_ffn1_w_gate'], 'v_ffn1_w_up': out['v_ffn1_w_up'], 'v_ffn1_w_down': out['v_ffn1_w_down'], 'v_mix_norm': out['v_mix_norm'], 'v_w_in': out['v_w_in'], 'v_gate_bias': out['v_gate_bias'], 'v_rel_bias_table': out['v_rel_bias_table'], 'v_ssm_a_re': out['v_ssm_a_re'], 'v_ssm_a_im': out['v_ssm_a_im'], 'v_ssm_log_dt': out['v_ssm_log_dt'], 'v_ssm_b_re': out['v_ssm_b_re'], 'v_ssm_b_im': out['v_ssm_b_im'], 'v_ssm_c_re': out['v_ssm_c_re'], 'v_ssm_c_im': out['v_ssm_c_im'], 'v_ssm_d': out['v_ssm_d'], 'v_ssm_w_glu': out['v_ssm_w_glu'], 'v_w_attn_branch': out['v_w_attn_branch'], 'v_w_ssm_branch': out['v_w_ssm_branch'], 'v_w_out': out['v_w_out'], 'v_ffn2_norm': out['v_ffn2_norm'], 'v_ffn2_w_gate': out['v_ffn2_w_gate'], 'v_ffn2_w_up': out['v_ffn2_w_up'], 'v_ffn2_w_down': out['v_ffn2_w_down'], 'v_final_norm': out['v_final_norm']}


def _loss(weights, diff, rest, loss_target):
    with _jax.named_scope("forward"):
        args = {**rest, TWIN_DIFF_INPUT: diff, **{k: w.astype(_WEIGHT_DTYPES[k]) for k, w in weights.items()}}
        y = _forward(args)
    with _jax.named_scope("loss_head"):
        err = _jnp.square(y.astype(_jnp.float32) - loss_target)
        return 0.5 * _jnp.sum(_jnp.mean(err, axis=-1)) if err.ndim else 0.5 * err


def _adamw(w, g, m, v):
    m = ADAM_B1 * m + (1.0 - ADAM_B1) * g
    v = ADAM_B2 * v + (1.0 - ADAM_B2) * _jnp.square(g)
    m_hat = m / (1.0 - ADAM_B1 ** ADAM_STEP)
    v_hat = v / (1.0 - ADAM_B2 ** ADAM_STEP)
    delta = -ADAM_LR * (m_hat / (_jnp.sqrt(v_hat) + ADAM_EPS) + ADAM_WD * w)
    return delta, m, v


def reference(x, ffn1_norm, ffn1_w_gate, ffn1_w_up, ffn1_w_down, mix_norm, w_in, gate_bias, rel_bias_table, ssm_a_re, ssm_a_im, ssm_log_dt, ssm_b_re, ssm_b_im, ssm_c_re, ssm_c_im, ssm_d, ssm_w_glu, w_attn_branch, w_ssm_branch, w_out, ffn2_norm, ffn2_w_gate, ffn2_w_up, ffn2_w_down, final_norm, loss_target, m_ffn1_norm, m_ffn1_w_gate, m_ffn1_w_up, m_ffn1_w_down, m_mix_norm, m_w_in, m_gate_bias, m_rel_bias_table, m_ssm_a_re, m_ssm_a_im, m_ssm_log_dt, m_ssm_b_re, m_ssm_b_im, m_ssm_c_re, m_ssm_c_im, m_ssm_d, m_ssm_w_glu, m_w_attn_branch, m_w_ssm_branch, m_w_out, m_ffn2_norm, m_ffn2_w_gate, m_ffn2_w_up, m_ffn2_w_down, m_final_norm, v_ffn1_norm, v_ffn1_w_gate, v_ffn1_w_up, v_ffn1_w_down, v_mix_norm, v_w_in, v_gate_bias, v_rel_bias_table, v_ssm_a_re, v_ssm_a_im, v_ssm_log_dt, v_ssm_b_re, v_ssm_b_im, v_ssm_c_re, v_ssm_c_im, v_ssm_d, v_ssm_w_glu, v_w_attn_branch, v_w_ssm_branch, v_w_out, v_ffn2_norm, v_ffn2_w_gate, v_ffn2_w_up, v_ffn2_w_down, v_final_norm):
    given = dict(x=x, ffn1_norm=ffn1_norm, ffn1_w_gate=ffn1_w_gate, ffn1_w_up=ffn1_w_up, ffn1_w_down=ffn1_w_down, mix_norm=mix_norm, w_in=w_in, gate_bias=gate_bias, rel_bias_table=rel_bias_table, ssm_a_re=ssm_a_re, ssm_a_im=ssm_a_im, ssm_log_dt=ssm_log_dt, ssm_b_re=ssm_b_re, ssm_b_im=ssm_b_im, ssm_c_re=ssm_c_re, ssm_c_im=ssm_c_im, ssm_d=ssm_d, ssm_w_glu=ssm_w_glu, w_attn_branch=w_attn_branch, w_ssm_branch=w_ssm_branch, w_out=w_out, ffn2_norm=ffn2_norm, ffn2_w_gate=ffn2_w_gate, ffn2_w_up=ffn2_w_up, ffn2_w_down=ffn2_w_down, final_norm=final_norm, loss_target=loss_target, m_ffn1_norm=m_ffn1_norm, m_ffn1_w_gate=m_ffn1_w_gate, m_ffn1_w_up=m_ffn1_w_up, m_ffn1_w_down=m_ffn1_w_down, m_mix_norm=m_mix_norm, m_w_in=m_w_in, m_gate_bias=m_gate_bias, m_rel_bias_table=m_rel_bias_table, m_ssm_a_re=m_ssm_a_re, m_ssm_a_im=m_ssm_a_im, m_ssm_log_dt=m_ssm_log_dt, m_ssm_b_re=m_ssm_b_re, m_ssm_b_im=m_ssm_b_im, m_ssm_c_re=m_ssm_c_re, m_ssm_c_im=m_ssm_c_im, m_ssm_d=m_ssm_d, m_ssm_w_glu=m_ssm_w_glu, m_w_attn_branch=m_w_attn_branch, m_w_ssm_branch=m_w_ssm_branch, m_w_out=m_w_out, m_ffn2_norm=m_ffn2_norm, m_ffn2_w_gate=m_ffn2_w_gate, m_ffn2_w_up=m_ffn2_w_up, m_ffn2_w_down=m_ffn2_w_down, m_final_norm=m_final_norm, v_ffn1_norm=v_ffn1_norm, v_ffn1_w_gate=v_ffn1_w_gate, v_ffn1_w_up=v_ffn1_w_up, v_ffn1_w_down=v_ffn1_w_down, v_mix_norm=v_mix_norm, v_w_in=v_w_in, v_gate_bias=v_gate_bias, v_rel_bias_table=v_rel_bias_table, v_ssm_a_re=v_ssm_a_re, v_ssm_a_im=v_ssm_a_im, v_ssm_log_dt=v_ssm_log_dt, v_ssm_b_re=v_ssm_b_re, v_ssm_b_im=v_ssm_b_im, v_ssm_c_re=v_ssm_c_re, v_ssm_c_im=v_ssm_c_im, v_ssm_d=v_ssm_d, v_ssm_w_glu=v_ssm_w_glu, v_w_attn_branch=v_w_attn_branch, v_w_ssm_branch=v_w_ssm_branch, v_w_out=v_w_out, v_ffn2_norm=v_ffn2_norm, v_ffn2_w_gate=v_ffn2_w_gate, v_ffn2_w_up=v_ffn2_w_up, v_ffn2_w_down=v_ffn2_w_down, v_final_norm=v_final_norm)
    weights = {n: given[n] for n in TWIN_WEIGHTS}
    shared = {n: given[n] for n in SHARED_INPUTS}
    per_example = {n: given[n] for n in ['x']}
    grad_fn = _jax.value_and_grad(_loss, argnums=(0, 1))

    def one_microbatch(ex, loss_target):
        ex = dict(ex)
        diff = ex.pop(TWIN_DIFF_INPUT)
        return grad_fn(weights, diff, {**shared, **ex}, loss_target)

    if N_MICROBATCH == 1:
        loss, (grad_w, grad_x) = one_microbatch(per_example, given["loss_target"])
    else:
        def body(carry, xs):
            loss_sum, grad_sum = carry
            l_k, (gw_k, gx_k) = one_microbatch(xs[0], xs[1])
            with _jax.named_scope("update"):
                return (loss_sum + l_k, _jax.tree.map(_jnp.add, grad_sum, gw_k)), gx_k

        init = (_jnp.zeros((), _jnp.float32), _jax.tree.map(_jnp.zeros_like, weights))
        (loss, grad_w), grad_x = _jax.lax.scan(body, init, (per_example, given["loss_target"]))
    with _jax.named_scope("update"):
        delta_w, new_m, new_v = {}, {}, {}
        for n in TWIN_WEIGHTS:
            delta_w[n], new_m[n], new_v[n] = _adamw(weights[n], grad_w[n], given["m_" + n], given["v_" + n])
    return (loss, grad_x, *[grad_w[n] for n in TWIN_WEIGHTS], *[delta_w[n] for n in TWIN_WEIGHTS],
            *[new_m[n] for n in TWIN_WEIGHTS], *[new_v[n] for n in TWIN_WEIGHTS])
```

```python
import functools
import math

import numpy as np
import jax
import jax.numpy as jnp
from jax import lax
from jax.experimental import pallas as pl
from jax.experimental.pallas import tpu as pltpu

F32 = jnp.float32
BF16 = jnp.bfloat16

N_DEV = 8
D_MODEL = 1024
D_FF = 2816
HEAD_DIM = 64
HEADS_PER_GROUP = 4
DILATIONS = (1, 4, 16)
N_GROUPS = 3
ATTN_WIDTH = 768
ATTN_OUT = 256
BLOCK = 128
N_BUCKETS = 32
MAX_DISTANCE = 2048
NEG_INF = -1e30
SSM_WIDTH = 512
SSM_GROUPS = 32
SSM_GROUP = 16
SSM_STATE = 64
SSM_LANES = SSM_GROUPS * SSM_STATE
EPS = 1e-6
LR, B1, B2, ADAM_EPS, WD, STEP = 0.001, 0.9, 0.999, 1e-08, 0.01, 10

VMEM_LIMIT_BYTES = 56 * 1024 * 1024
SCAN_BLOCK = 256
SCAN_SUB = 8
SCAN_STEPS = SCAN_BLOCK // SCAN_SUB
SCAN_LANES = 512

MESH = pl.DeviceIdType.MESH


def _cparams(*sem):
    return pltpu.CompilerParams(dimension_semantics=sem, vmem_limit_bytes=VMEM_LIMIT_BYTES)


def _dot(a, b, dims):
    return lax.dot_general(a, b, (dims, ((), ())), preferred_element_type=F32)


def _dot_nn(a, b):
    return _dot(a, b, ((1,), (0,)))


def _dot_nt(a, b):
    return _dot(a, b, ((1,), (1,)))


def _dot_tn(a, b):
    return _dot(a, b, ((0,), (0,)))


def _sigmoid(x):
    return 1.0 / (1.0 + jnp.exp(-x))


def _all_gather(name, xs):
    rows, cols = xs.shape

    def body(x_ref, out_ref, send_sems, recv_sems, local_sem):
        x, y, c = lax.axis_index("x"), lax.axis_index("y"), lax.axis_index("c")
        me, sibling = (x, y, c), (x, y, 1 - c)
        chips = [(1 - x, y), (x, 1 - y), (1 - x, 1 - y)]

        def slot(px, py, pc):
            return out_ref.at[4 * px + 2 * py + pc]

        def copy(k, block, to, src=None):
            return pltpu.make_async_remote_copy(
                src_ref=slot(*block) if src is None else src, dst_ref=slot(*block),
                send_sem=send_sems.at[k], recv_sem=recv_sems.at[k], device_id=to, device_id_type=MESH)

        mine = pltpu.make_async_copy(x_ref, slot(*me), local_sem)
        mine.start()
        first = [copy(0, me, sibling, src=x_ref)]
        first += [copy(1 + j, me, (*chip, c), src=x_ref) for j, chip in enumerate(chips)]
        for cp in first:
            cp.start()
        passed = [copy(4 + j, (*chip, c), sibling) for j, chip in enumerate(chips)]
        for j, chip in enumerate(chips):
            copy(1 + j, (*chip, c), me).wait_recv()
            passed[j].start()
        copy(0, sibling, me).wait_recv()
        for j, chip in enumerate(chips):
            copy(4 + j, (*chip, 1 - c), me).wait_recv()
        for cp in first + passed:
            cp.wait_send()
        mine.wait()

    return pl.pallas_call(
        body, name=name,
        out_shape=jax.ShapeDtypeStruct((N_DEV, rows, cols), xs.dtype),
        in_specs=[pl.BlockSpec(memory_space=pl.ANY)],
        out_specs=pl.BlockSpec(memory_space=pl.ANY),
        scratch_shapes=[pltpu.SemaphoreType.DMA((7,)), pltpu.SemaphoreType.DMA((7,)), pltpu.SemaphoreType.DMA],
    )(xs)


def _all_to_all(name, xs):
    _, rows, cols = xs.shape

    def body(x_ref, out_ref, send_sems, recv_sems, local_sem):
        x, y, c = lax.axis_index("x"), lax.axis_index("y"), lax.axis_index("c")
        me = 4 * x + 2 * y + c
        mine = pltpu.make_async_copy(x_ref.at[me], out_ref.at[me], local_sem)
        mine.start()
        copies = []
        for k in range(1, N_DEV):
            px = 1 - x if k & 4 else x
            py = 1 - y if k & 2 else y
            pc = 1 - c if k & 1 else c
            cp = pltpu.make_async_remote_copy(
                src_ref=x_ref.at[4 * px + 2 * py + pc], dst_ref=out_ref.at[me],
                send_sem=send_sems.at[k - 1], recv_sem=recv_sems.at[k - 1],
                device_id=(px, py, pc), device_id_type=MESH)
            cp.start()
            copies.append(cp)
        for cp in copies:
            cp.wait()
        mine.wait()

    return pl.pallas_call(
        body, name=name,
        out_shape=jax.ShapeDtypeStruct(xs.shape, xs.dtype),
        in_specs=[pl.BlockSpec(memory_space=pl.ANY)],
        out_specs=pl.BlockSpec(memory_space=pl.ANY),
        scratch_shapes=[pltpu.SemaphoreType.DMA((7,)), pltpu.SemaphoreType.DMA((7,)), pltpu.SemaphoreType.DMA],
    )(xs)


def _mm(name, pairs, nt, n_cols, out_dtypes, epilogue=None, extras=(), tm=512, tn=256):
    rows = pairs[0][0].shape[0]
    tm = min(tm, rows)
    tn = min(tn, n_cols)
    na, ne = len(pairs), len(extras)

    def body(*refs):
        a_refs, w_refs = refs[:na], refs[na:2 * na]
        e_refs, o_refs = refs[2 * na:2 * na + ne], refs[2 * na + ne:]
        acc = None
        for a_ref, w_ref in zip(a_refs, w_refs):
            a = a_ref[...].astype(BF16)
            w = w_ref[...].astype(BF16)
            p = _dot_nt(a, w) if nt else _dot_nn(a, w)
            acc = p if acc is None else acc + p
        outs = (acc,) if epilogue is None else epilogue(acc, *[e[...] for e in e_refs])
        for o_ref, o in zip(o_refs, outs):
            o_ref[...] = o.astype(o_ref.dtype)

    in_specs = [pl.BlockSpec((tm, a.shape[1]), lambda i, j: (i, 0)) for a, _ in pairs]
    for _, w in pairs:
        if nt:
            in_specs.append(pl.BlockSpec((tn, w.shape[1]), lambda i, j: (j, 0)))
        else:
            in_specs.append(pl.BlockSpec((w.shape[0], tn), lambda i, j: (0, j)))
    for e, off in extras:
        if e.shape[0] == 1:
            in_specs.append(pl.BlockSpec((1, tn), lambda i, j, off=off: (0, j + off)))
        else:
            in_specs.append(pl.BlockSpec((tm, tn), lambda i, j, off=off: (i, j + off)))
    out_specs = [pl.BlockSpec((tm, tn), lambda i, j: (i, j)) for _ in out_dtypes]
    outs = pl.pallas_call(
        body, name=name, grid=(rows // tm, n_cols // tn),
        in_specs=in_specs, out_specs=out_specs,
        out_shape=[jax.ShapeDtypeStruct((rows, n_cols), dt) for dt in out_dtypes],
        compiler_params=_cparams("parallel", "arbitrary"),
    )(*[a for a, _ in pairs], *[w for _, w in pairs], *[e for e, _ in extras])
    return outs


def _mm_tn(name, a, b, scale=1.0, bm=256, tk=512):
    rows, m = a.shape
    n = b.shape[1]
    bm = min(bm, m)
    tk = min(tk, rows)
    nk = rows // tk

    def body(a_ref, b_ref, o_ref):
        k = pl.program_id(1)

        @pl.when(k == 0)
        def _():
            o_ref[...] = jnp.zeros_like(o_ref)

        o_ref[...] += _dot_tn(a_ref[...].astype(BF16), b_ref[...].astype(BF16))
        if scale != 1.0:
            @pl.when(k == nk - 1)
            def _():
                o_ref[...] = o_ref[...] * scale

    return pl.pallas_call(
        body, name=name, grid=(m // bm, nk),
        in_specs=[pl.BlockSpec((tk, bm), lambda i, k: (k, i)), pl.BlockSpec((tk, n), lambda i, k: (k, 0))],
        out_specs=pl.BlockSpec((bm, n), lambda i, k: (i, 0)),
        out_shape=jax.ShapeDtypeStruct((m, n), F32),
        compiler_params=_cparams("parallel", "arbitrary"),
    )(a, b)


def _colsum(name, xs, tm=512):
    rows, cols = xs.shape
    tm = min(tm, rows)

    def body(x_ref, o_ref):
        @pl.when(pl.program_id(0) == 0)
        def _():
            o_ref[...] = jnp.zeros_like(o_ref)

        o_ref[...] += jnp.sum(x_ref[...].astype(F32), axis=0, keepdims=True)

    return pl.pallas_call(
        body, name=name, grid=(rows // tm,),
        in_specs=[pl.BlockSpec((tm, cols), lambda i: (i, 0))],
        out_specs=pl.BlockSpec((1, cols), lambda i: (0, 0)),
        out_shape=jax.ShapeDtypeStruct((1, cols), F32),
        compiler_params=_cparams("arbitrary"),
    )(xs)


def _ew(name, fn, ins, out_cols, out_dtypes, tm=512):
    rows = ins[0].shape[0]
    tm = min(tm, rows)
    ni = len(ins)

    def body(*refs):
        outs = fn(*[r[...] for r in refs[:ni]])
        for o_ref, o in zip(refs[ni:], outs):
            o_ref[...] = o.astype(o_ref.dtype)

    def spec(shape):
        if shape[0] == 1:
            return pl.BlockSpec((1, shape[1]), lambda i: (0, 0))
        return pl.BlockSpec((tm, shape[1]), lambda i: (i, 0))

    return pl.pallas_call(
        body, name=name, grid=(rows // tm,),
        in_specs=[spec(a.shape) for a in ins],
        out_specs=[pl.BlockSpec((tm, c), lambda i: (i, 0)) for c in out_cols],
        out_shape=[jax.ShapeDtypeStruct((rows, c), dt) for c, dt in zip(out_cols, out_dtypes)],
        compiler_params=_cparams("parallel"),
    )(*ins)


def _rms_parts(xv):
    r = lax.rsqrt(jnp.mean(xv * xv, axis=-1, keepdims=True) + EPS)
    return r, xv * r


def _rms_bwd_dx(dh, gain, r, xh):
    dxh = dh * gain
    return r * (dxh - xh * jnp.mean(dxh * xh, axis=-1, keepdims=True))


def _rms_fwd(name, xs, gain):
    def fn(xv, g):
        _, xh = _rms_parts(xv)
        return (xh * g,)

    return _ew(name, fn, [xs, gain], [xs.shape[1]], [BF16])[0]


def _rms_bwd(name, dh, xs, gain, dres, tm=512):
    rows, d = xs.shape
    tm = min(tm, rows)

    def body(dh_ref, x_ref, g_ref, dres_ref, dx_ref, dg_ref):
        r, xh = _rms_parts(x_ref[...])
        dhv = dh_ref[...]
        dx_ref[...] = dres_ref[...] + _rms_bwd_dx(dhv, g_ref[...], r, xh)

        @pl.when(pl.program_id(0) == 0)
        def _():
            dg_ref[...] = jnp.zeros_like(dg_ref)

        dg_ref[...] += jnp.sum(dhv * xh, axis=0, keepdims=True)

    tile = pl.BlockSpec((tm, d), lambda i: (i, 0))
    row = pl.BlockSpec((1, d), lambda i: (0, 0))
    return pl.pallas_call(
        body, name=name, grid=(rows // tm,),
        in_specs=[tile, tile, row, tile], out_specs=[tile, row],
        out_shape=[jax.ShapeDtypeStruct((rows, d), F32), jax.ShapeDtypeStruct((1, d), F32)],
        compiler_params=_cparams("arbitrary"),
    )(dh, xs, gain, dres)


def _ffn_fwd(name, xs, gain, wg_t, wu_t, wd, tm=512, tf=256):
    rows, d = xs.shape
    f_all = wd.shape[0]
    tm = min(tm, rows)
    nf = f_all // tf

    def body(x_ref, g_ref, wg_ref, wu_ref, wd_ref, xo_ref, h_ref, gg_ref, uu_ref, acc_ref):
        f = pl.program_id(1)

        @pl.when(f == 0)
        def _():
            _, xh = _rms_parts(x_ref[...])
            h_ref[...] = (xh * g_ref[...]).astype(BF16)
            acc_ref[...] = jnp.zeros_like(acc_ref)

        h = h_ref[...]
        gg = _dot_nt(h, wg_ref[...])
        uu = _dot_nt(h, wu_ref[...])
        act = gg * _sigmoid(gg) * uu
        acc_ref[...] += _dot_nn(act.astype(BF16), wd_ref[...])
        gg_ref[...] = gg.astype(BF16)
        uu_ref[...] = uu.astype(BF16)

        @pl.when(f == nf - 1)
        def _():
            xo_ref[...] = x_ref[...] + 0.5 * acc_ref[...]

    tile = pl.BlockSpec((tm, d), lambda i, f: (i, 0))
    wspec = pl.BlockSpec((tf, d), lambda i, f: (f, 0))
    hid = pl.BlockSpec((tm, tf), lambda i, f: (i, f))
    return pl.pallas_call(
        body, name=name, grid=(rows // tm, nf),
        in_specs=[tile, pl.BlockSpec((1, d), lambda i, f: (0, 0)), wspec, wspec, wspec],
        out_specs=[tile, tile, hid, hid],
        out_shape=[jax.ShapeDtypeStruct((rows, d), F32), jax.ShapeDtypeStruct((rows, d), BF16),
                   jax.ShapeDtypeStruct((rows, f_all), BF16), jax.ShapeDtypeStruct((rows, f_all), BF16)],
        scratch_shapes=[pltpu.VMEM((tm, d), F32)],
        compiler_params=_cparams("parallel", "arbitrary"),
    )(xs, gain, wg_t, wu_t, wd)


def _ffn_bwd(name, dxo, xs, gain, gg_all, uu_all, wg_t, wu_t, wd, tm=512, tf=256):
    rows, d = xs.shape
    f_all = wd.shape[0]
    tm = min(tm, rows)
    nf = f_all // tf

    def body(dxo_ref, x_ref, g_ref, gg_ref, uu_ref, wg_ref, wu_ref, wd_ref,
             dx_ref, dgg_ref, duu_ref, act_ref, dgain_ref, df_ref, acc_ref):
        i, f = pl.program_id(0), pl.program_id(1)

        @pl.when(f == 0)
        def _():
            df_ref[...] = (0.5 * dxo_ref[...]).astype(BF16)
            acc_ref[...] = jnp.zeros_like(acc_ref)

        gg = gg_ref[...].astype(F32)
        uu = uu_ref[...].astype(F32)
        sg = _sigmoid(gg)
        silu = gg * sg
        dact = _dot_nt(df_ref[...], wd_ref[...])
        duu = (dact * silu).astype(BF16)
        dgg = (dact * uu * (sg * (1.0 + gg * (1.0 - sg)))).astype(BF16)
        act_ref[...] = (silu * uu).astype(BF16)
        dgg_ref[...] = dgg
        duu_ref[...] = duu
        acc_ref[...] += _dot_nn(dgg, wg_ref[...]) + _dot_nn(duu, wu_ref[...])

        @pl.when(f == nf - 1)
        def _():
            r, xh = _rms_parts(x_ref[...])
            dh = acc_ref[...]
            dx_ref[...] = dxo_ref[...] + _rms_bwd_dx(dh, g_ref[...], r, xh)
            part = jnp.sum(dh * xh, axis=0, keepdims=True)

            @pl.when(i == 0)
            def _():
                dgain_ref[...] = part

            @pl.when(i > 0)
            def _():
                dgain_ref[...] += part

    tile = pl.BlockSpec((tm, d), lambda i, f: (i, 0))
    row = pl.BlockSpec((1, d), lambda i, f: (0, 0))
    wspec = pl.BlockSpec((tf, d), lambda i, f: (f, 0))
    hid = pl.BlockSpec((tm, tf), lambda i, f: (i, f))
    hid_shape = jax.ShapeDtypeStruct((rows, f_all), BF16)
    return pl.pallas_call(
        body, name=name, grid=(rows // tm, nf),
        in_specs=[tile, tile, row, hid, hid, wspec, wspec, wspec],
        out_specs=[tile, hid, hid, hid, row],
        out_shape=[jax.ShapeDtypeStruct((rows, d), F32), hid_shape, hid_shape, hid_shape,
                   jax.ShapeDtypeStruct((1, d), F32)],
        scratch_shapes=[pltpu.VMEM((tm, d), BF16), pltpu.VMEM((tm, d), F32)],
        compiler_params=_cparams("arbitrary", "arbitrary"),
    )(dxo, xs, gain, gg_all, uu_all, wg_t, wu_t, wd)


def _final_loss(name, xs, gain, target, tm=512):
    rows, d = xs.shape
    tm = min(tm, rows)

    def body(x_ref, g_ref, t_ref, dx_ref, dg_ref, loss_ref):
        r, xh = _rms_parts(x_ref[...])
        gain_v = g_ref[...]
        err = xh * gain_v - t_ref[...]
        dy = err * (1.0 / d)
        dx_ref[...] = _rms_bwd_dx(dy, gain_v, r, xh)

        @pl.when(pl.program_id(0) == 0)
        def _():
            dg_ref[...] = jnp.zeros_like(dg_ref)
            loss_ref[...] = jnp.zeros_like(loss_ref)

        dg_ref[...] += jnp.sum(dy * xh, axis=0, keepdims=True)
        per_tok = jnp.mean(err * err, axis=-1, keepdims=True)
        loss_ref[...] += 0.5 * jnp.sum(per_tok, axis=0, keepdims=True)

    tile = pl.BlockSpec((tm, d), lambda i: (i, 0))
    row = pl.BlockSpec((1, d), lambda i: (0, 0))
    return pl.pallas_call(
        body, name=name, grid=(rows // tm,),
        in_specs=[tile, row, tile],
        out_specs=[tile, row, pl.BlockSpec((1, 1), lambda i: (0, 0))],
        out_shape=[jax.ShapeDtypeStruct((rows, d), F32), jax.ShapeDtypeStruct((1, d), F32),
                   jax.ShapeDtypeStruct((1, 1), F32)],
        compiler_params=_cparams("arbitrary"),
    )(xs, gain, target)


def _bucket_table():
    out = []
    for dil in DILATIONS:
        qi = np.arange(BLOCK)[:, None]
        kj = np.arange(2 * BLOCK)[None, :]
        dist = (np.maximum(qi + BLOCK - kj, 0) * dil).astype(np.int32)
        max_exact = N_BUCKETS // 2
        dd = np.maximum(dist, 1).astype(np.float32)
        large = max_exact + (np.log(dd / np.float32(max_exact)) / np.float32(math.log(MAX_DISTANCE / max_exact))
                             * np.float32(N_BUCKETS - max_exact)).astype(np.int32)
        large = np.minimum(large, N_BUCKETS - 1)
        out.append(np.where(dist < max_exact, dist, large).astype(np.int32))
    return np.stack(out)


def _bias_fwd(name, buckets, table):
    def body(bk_ref, tab_ref, o_ref):
        for g in range(N_GROUPS):
            bk = bk_ref[g]
            for h in range(HEADS_PER_GROUP):
                col = g * HEADS_PER_GROUP + h
                acc = jnp.zeros((BLOCK, 2 * BLOCK), F32)
                for b in range(N_BUCKETS):
                    acc = jnp.where(bk == b, tab_ref[b, col], acc)
                o_ref[col] = acc

    return pl.pallas_call(
        body, name=name,
        in_specs=[pl.BlockSpec(memory_space=pltpu.VMEM), pl.BlockSpec(memory_space=pltpu.SMEM)],
        out_specs=pl.BlockSpec(memory_space=pltpu.VMEM),
        out_shape=jax.ShapeDtypeStruct((N_GROUPS * HEADS_PER_GROUP, BLOCK, 2 * BLOCK), F32),
    )(buckets, table)


def _bias_bwd(name, buckets, dbias):
    def body(bk_ref, db_ref, o_ref):
        row_id = lax.broadcasted_iota(jnp.int32, (N_BUCKETS, 128), 0)
        col_id = lax.broadcasted_iota(jnp.int32, (N_BUCKETS, 128), 1)
        acc = jnp.zeros((N_BUCKETS, 128), F32)
        for g in range(N_GROUPS):
            bk = bk_ref[g]
            for h in range(HEADS_PER_GROUP):
                col = g * HEADS_PER_GROUP + h
                db = db_ref[col]
                for b in range(N_BUCKETS):
                    part = jnp.sum(jnp.where(bk == b, db, 0.0), axis=0, keepdims=True)
                    tot = jnp.sum(part, axis=1, keepdims=True)
                    acc = jnp.where((row_id == b) & (col_id == col), tot, acc)
        o_ref[...] = acc

    return pl.pallas_call(
        body, name=name,
        in_specs=[pl.BlockSpec(memory_space=pltpu.VMEM), pl.BlockSpec(memory_space=pltpu.VMEM)],
        out_specs=pl.BlockSpec(memory_space=pltpu.VMEM),
        out_shape=jax.ShapeDtypeStruct((N_BUCKETS, 128), F32),
    )(buckets, dbias)


def _band_mask(n):
    qi = lax.broadcasted_iota(jnp.int32, (BLOCK, 2 * BLOCK), 0)
    kj = lax.broadcasted_iota(jnp.int32, (BLOCK, 2 * BLOCK), 1)
    return (kj >= qi) & (kj <= qi + BLOCK) & ((kj >= BLOCK) | (n > 0))


def _attn_fwd(name, q, k, v, bias):
    dil, nh, m, dh = q.shape
    nb = m // BLOCK

    def body(q_ref, kc_ref, kp_ref, vc_ref, vp_ref, b_ref, o_ref, lse_ref):
        mask = _band_mask(pl.program_id(1))
        for h in range(nh):
            k2 = jnp.concatenate([kp_ref[0, h], kc_ref[0, h]], axis=0)
            v2 = jnp.concatenate([vp_ref[0, h], vc_ref[0, h]], axis=0)
            s = _dot_nt(q_ref[0, h], k2) + b_ref[h]
            s = jnp.where(mask, s, NEG_INF)
            mx = jnp.max(s, axis=-1, keepdims=True)
            p = jnp.exp(s - mx)
            den = jnp.sum(p, axis=-1, keepdims=True)
            o_ref[0, h] = _dot_nn(p.astype(BF16), v2) / den
            lse_ref[0, h] = jnp.broadcast_to(mx + jnp.log(den), (BLOCK, dh))

    cur = pl.BlockSpec((1, nh, BLOCK, dh), lambda r, n: (r, 0, n, 0))
    prev = pl.BlockSpec((1, nh, BLOCK, dh), lambda r, n: (r, 0, jnp.maximum(n - 1, 0), 0))
    return pl.pallas_call(
        body, name=name, grid=(dil, nb),
        in_specs=[cur, cur, prev, cur, prev, pl.BlockSpec((nh, BLOCK, 2 * BLOCK), lambda r, n: (0, 0, 0))],
        out_specs=[cur, cur],
        out_shape=[jax.ShapeDtypeStruct(q.shape, F32), jax.ShapeDtypeStruct(q.shape, F32)],
        compiler_params=_cparams("parallel", "arbitrary"),
    )(q, k, k, v, v, bias)


def _attn_bwd(name, q, k, v, do, lse, cvec, bias):
    dil, nh, m, dh = q.shape
    nb = m // BLOCK

    def body(q_ref, kc_ref, kp_ref, vc_ref, vp_ref, do_ref, lse_ref, c_ref, b_ref,
             dq_ref, dk_ref, dv_ref, db_ref, kcar_ref, vcar_ref):
        r, n = pl.program_id(0), pl.program_id(1)
        valid = n < nb
        mask = _band_mask(n) & valid

        @pl.when((r == 0) & (n == 0))
        def _():
            kcar_ref[...] = jnp.zeros_like(kcar_ref)
            vcar_ref[...] = jnp.zeros_like(vcar_ref)
            db_ref[...] = jnp.zeros_like(db_ref)

        for h in range(nh):
            qh = q_ref[0, h]
            k2 = jnp.concatenate([kp_ref[0, h], kc_ref[0, h]], axis=0)
            v2 = jnp.concatenate([vp_ref[0, h], vc_ref[0, h]], axis=0)
            doh = do_ref[0, h].astype(BF16)
            s = _dot_nt(qh, k2) + b_ref[h]
            p = jnp.where(mask, jnp.exp(s - lse_ref[0, h][:, :1]), 0.0)
            dp = _dot_nt(doh, v2)
            ds = p * (dp + c_ref[0, h][:, :1])
            ds_b = ds.astype(BF16)

            @pl.when(valid)
            def _():
                dq_ref[0, h] = _dot_nn(ds_b, k2)

            dk2 = _dot_tn(ds_b, qh)
            dv2 = _dot_tn(p.astype(BF16), doh)
            dk_ref[0, h] = kcar_ref[h] + dk2[:BLOCK]
            dv_ref[0, h] = vcar_ref[h] + dv2[:BLOCK]
            kcar_ref[h] = dk2[BLOCK:]
            vcar_ref[h] = dv2[BLOCK:]
            db_ref[h] += ds

    def qmap(r, n):
        return (r, 0, jnp.minimum(n, nb - 1), 0)

    def pmap(r, n):
        return (r, 0, jnp.maximum(jnp.minimum(n, nb - 1) - 1, 0), 0)

    def kvout(r, n):
        return (r, 0, jnp.maximum(n - 1, 0), 0)

    blk = (1, nh, BLOCK, dh)
    cur, prev = pl.BlockSpec(blk, qmap), pl.BlockSpec(blk, pmap)
    bias_spec = pl.BlockSpec((nh, BLOCK, 2 * BLOCK), lambda r, n: (0, 0, 0))
    full = jax.ShapeDtypeStruct(q.shape, F32)
    return pl.pallas_call(
        body, name=name, grid=(dil, nb + 1),
        in_specs=[cur, cur, prev, cur, prev, cur, cur, cur, bias_spec],
        out_specs=[cur, pl.BlockSpec(blk, kvout), pl.BlockSpec(blk, kvout), bias_spec],
        out_shape=[full, full, full, jax.ShapeDtypeStruct(bias.shape, F32)],
        scratch_shapes=[pltpu.VMEM((nh, BLOCK, dh), F32), pltpu.VMEM((nh, BLOCK, dh), F32)],
        compiler_params=_cparams("arbitrary", "arbitrary"),
    )(q, k, k, v, v, do, lse, cvec, bias)


def _group_weights(lses):
    mx = jnp.maximum(jnp.maximum(lses[0], lses[1]), lses[2])
    es = [jnp.exp(l - mx) for l in lses]
    den = es[0] + es[1] + es[2]
    return [e / den for e in es]


def _combine_fwd(name, os_, lses, tm=512):
    nh, rows, dh = os_[0].shape
    tm = min(tm, rows)

    def body(o0, o1, o2, l0, l1, l2, out_ref):
        ws = _group_weights([l0[...], l1[...], l2[...]])
        out_ref[...] = ws[0] * o0[...] + ws[1] * o1[...] + ws[2] * o2[...]

    spec = pl.BlockSpec((nh, tm, dh), lambda i: (0, i, 0))
    return pl.pallas_call(
        body, name=name, grid=(rows // tm,), in_specs=[spec] * 6, out_specs=spec,
        out_shape=jax.ShapeDtypeStruct((nh, rows, dh), F32),
        compiler_params=_cparams("parallel"),
    )(*os_, *lses)


def _combine_bwd(name, do, oa, lses, tm=512):
    nh, rows, dh = do.shape
    tm = min(tm, rows)

    def body(do_ref, oa_ref, l0, l1, l2, d0, d1, d2, c0, c1, c2):
        ws = _group_weights([l0[...], l1[...], l2[...]])
        dov = do_ref[...]
        bar = jnp.sum(dov * oa_ref[...], axis=-1, keepdims=True)
        for w, d_ref, c_ref in zip(ws, (d0, d1, d2), (c0, c1, c2)):
            d_ref[...] = w * dov
            c_ref[...] = -w * bar

    spec = pl.BlockSpec((nh, tm, dh), lambda i: (0, i, 0))
    shape = jax.ShapeDtypeStruct((nh, rows, dh), F32)
    return pl.pallas_call(
        body, name=name, grid=(rows // tm,), in_specs=[spec] * 5, out_specs=[spec] * 6,
        out_shape=[shape] * 6, compiler_params=_cparams("parallel"),
    )(do, oa, *lses)


def _ssm_disc(a_re, a_im, log_dt, b_re, b_im):
    dt = jnp.exp(log_dt)
    mag = jnp.exp(a_re * dt)
    ab_re = mag * jnp.cos(a_im * dt)
    ab_im = mag * jnp.sin(a_im * dt)
    den = a_re * a_re + a_im * a_im
    xr = ab_re - 1.0
    coef_re = (xr * a_re + ab_im * a_im) / den
    coef_im = (ab_im * a_re - xr * a_im) / den
    bb_re = coef_re[None] * b_re - coef_im[None] * b_im
    bb_im = coef_re[None] * b_im + coef_im[None] * b_re
    return ab_re, ab_im, bb_re, bb_im


def _cpow2(re, im, times):
    for _ in range(times):
        re, im = re * re - im * im, 2.0 * re * im
    return re, im


def _ssm_params_fwd(name, a_re, a_im, log_dt, b_re, b_im):
    gn = jax.ShapeDtypeStruct(a_re.shape, F32)
    cgn = jax.ShapeDtypeStruct(b_re.shape, F32)

    def body(ar, ai, ld, br, bi, o_abr, o_abi, o_apr, o_api, o_bbr, o_bbi):
        ab_re, ab_im, bb_re, bb_im = _ssm_disc(ar[...], ai[...], ld[...], br[...], bi[...])
        o_abr[...] = ab_re
        o_abi[...] = ab_im
        pr, pi = _cpow2(ab_re, ab_im, int(math.log2(SCAN_STEPS)))
        o_apr[...] = pr
        o_api[...] = pi
        o_bbr[...] = bb_re
        o_bbi[...] = bb_im

    vm = pl.BlockSpec(memory_space=pltpu.VMEM)
    return pl.pallas_call(body, name=name, in_specs=[vm] * 5, out_specs=[vm] * 6,
                          out_shape=[gn, gn, gn, gn, cgn, cgn])(a_re, a_im, log_dt, b_re, b_im)


def _ssm_params_bwd(name, a_re, a_im, log_dt, b_re, b_im, d_ab_re, d_ab_im, d_bb_re, d_bb_im):
    gn = jax.ShapeDtypeStruct(a_re.shape, F32)
    cgn = jax.ShapeDtypeStruct(b_re.shape, F32)

    def body(ar, ai, ld, br, bi, g0, g1, g2, g3, o_ar, o_ai, o_ld, o_br, o_bi):
        _, vjp = jax.vjp(_ssm_disc, ar[...], ai[...], ld[...], br[...], bi[...])
        outs = vjp((g0[...], g1[...], g2[...], g3[...]))
        for o_ref, o in zip((o_ar, o_ai, o_ld, o_br, o_bi), outs):
            o_ref[...] = o

    vm = pl.BlockSpec(memory_space=pltpu.VMEM)
    return pl.pallas_call(body, name=name, in_specs=[vm] * 9, out_specs=[vm] * 5,
                          out_shape=[gn, gn, jax.ShapeDtypeStruct(log_dt.shape, F32), cgn, cgn],
                          )(a_re, a_im, log_dt, b_re, b_im, d_ab_re, d_ab_im, d_bb_re, d_bb_im)


def _scan_block(s_ref, carry_ref, tmp_ref, ab_ref, ap_ref, reverse, sprev=None):
    nl = SSM_LANES
    js = range(SCAN_STEPS)
    for lc in range(nl // SCAN_LANES):
        re_l = pl.ds(lc * SCAN_LANES, SCAN_LANES)
        im_l = pl.ds(nl + lc * SCAN_LANES, SCAN_LANES)
        are, aim = ab_ref[:, re_l], ab_ref[:, im_l]

        def rows_of(j):
            jj = SCAN_STEPS - 1 - j if reverse else j
            return pl.ds(pl.multiple_of(jj * SCAN_SUB, SCAN_SUB), SCAN_SUB)

        def pass1(j, st):
            sr, si = st
            rows = rows_of(j)
            nr = are * sr - aim * si + s_ref[rows, re_l]
            ni = are * si + aim * sr + s_ref[rows, im_l]
            s_ref[rows, re_l] = nr
            s_ref[rows, im_l] = ni
            return nr, ni

        zero = jnp.zeros((SCAN_SUB, SCAN_LANES), F32)
        er, ei = lax.fori_loop(0, SCAN_STEPS, pass1, (zero, zero), unroll=2)
        tmp_ref[0:SCAN_SUB, re_l] = er
        tmp_ref[0:SCAN_SUB, im_l] = ei
        apr, api = ap_ref[0:1, re_l], ap_ref[0:1, im_l]
        sr, si = carry_ref[0:1, re_l], carry_ref[0:1, im_l]
        for step in range(SCAN_SUB):
            c = SCAN_SUB - 1 - step if reverse else step
            tmp_ref[SCAN_SUB + c:SCAN_SUB + c + 1, re_l] = sr
            tmp_ref[SCAN_SUB + c:SCAN_SUB + c + 1, im_l] = si
            e_r, e_i = tmp_ref[c:c + 1, re_l], tmp_ref[c:c + 1, im_l]
            sr, si = apr * sr - api * si + e_r, apr * si + api * sr + e_i
        carry_ref[0:1, re_l] = sr
        carry_ref[0:1, im_l] = si
        cr = tmp_ref[SCAN_SUB:2 * SCAN_SUB, re_l]
        ci = tmp_ref[SCAN_SUB:2 * SCAN_SUB, im_l]

        if sprev is None:
            def pass2(j, st):
                pr, pi = st
                rows = rows_of(j)
                s_ref[rows, re_l] += pr * cr - pi * ci
                s_ref[rows, im_l] += pr * ci + pi * cr
                return pr * are - pi * aim, pr * aim + pi * are

            lax.fori_loop(0, SCAN_STEPS, pass2, (are, aim), unroll=2)
        else:
            st_ref, prev_ref, have_prev, dab_ref = sprev

            def corrected(j, pr, pi):
                rows = rows_of(j)
                gr = s_ref[rows, re_l] + pr * cr - pi * ci
                gi = s_ref[rows, im_l] + pr * ci + pi * cr
                s_ref[rows, re_l] = gr
                s_ref[rows, im_l] = gi
                return gr, gi

            def pass2(j, st):
                pr, pi, dr, di = st
                gr, gi = corrected(j, pr, pi)
                before = pl.ds(pl.multiple_of((SCAN_STEPS - 2 - j) * SCAN_SUB, SCAN_SUB), SCAN_SUB)
                qr, qi = st_ref[before, re_l], st_ref[before, im_l]
                return (pr * are - pi * aim, pr * aim + pi * are,
                        dr + gr * qr + gi * qi, di + gi * qr - gr * qi)

            pr, pi, dr, di = lax.fori_loop(0, SCAN_STEPS - 1, pass2, (are, aim, zero, zero), unroll=2)
            gr, gi = corrected(SCAN_STEPS - 1, pr, pi)
            last = pl.ds((SCAN_STEPS - 1) * SCAN_SUB, SCAN_SUB)
            sub = lax.broadcasted_iota(jnp.int32, (SCAN_SUB, SCAN_LANES), 0)
            pv_r = jnp.broadcast_to(prev_ref[SCAN_SUB - 1:SCAN_SUB, re_l], (SCAN_SUB, SCAN_LANES)) * have_prev
            pv_i = jnp.broadcast_to(prev_ref[SCAN_SUB - 1:SCAN_SUB, im_l], (SCAN_SUB, SCAN_LANES)) * have_prev
            qr = jnp.where(sub == 0, pv_r, pltpu.roll(st_ref[last, re_l], 1, 0))
            qi = jnp.where(sub == 0, pv_i, pltpu.roll(st_ref[last, im_l], 1, 0))
            dab_ref[:, re_l] += dr + gr * qr + gi * qi
            dab_ref[:, im_l] += di + gi * qr - gr * qi
    del js


def _ssm_fwd(name, u_perm, bb_mat, c_mat, ab_rows, ap_rows, d_skip):
    rows = u_perm.shape[0]
    nl2 = 2 * SSM_LANES

    def body(u_ref, bb_ref, c_ref, ab_ref, ap_ref, d_ref, y_ref, s_ref, carry_ref, tmp_ref):
        @pl.when(pl.program_id(0) == 0)
        def _():
            carry_ref[...] = jnp.zeros_like(carry_ref)

        uv = u_ref[...]
        s_ref[...] = _dot_nn(uv.astype(BF16), bb_ref[...])
        _scan_block(s_ref, carry_ref, tmp_ref, ab_ref, ap_ref, reverse=False)
        y_ref[...] = _dot_nn(s_ref[...].astype(BF16), c_ref[...]) + d_ref[...] * uv

    const = lambda shape: pl.BlockSpec(shape, lambda i: (0, 0))
    return pl.pallas_call(
        body, name=name, grid=(rows // SCAN_BLOCK,),
        in_specs=[pl.BlockSpec((SCAN_BLOCK, SSM_WIDTH), lambda i: (i, 0)), const((SSM_WIDTH, nl2)),
                  const((nl2, SSM_WIDTH)), const((SCAN_SUB, nl2)), const((SCAN_SUB, nl2)), const((1, SSM_WIDTH))],
        out_specs=[pl.BlockSpec((SCAN_BLOCK, SSM_WIDTH), lambda i: (i, 0)),
                   pl.BlockSpec((SCAN_BLOCK, nl2), lambda i: (i, 0))],
        out_shape=[jax.ShapeDtypeStruct((rows, SSM_WIDTH), F32), jax.ShapeDtypeStruct((rows, nl2), F32)],
        scratch_shapes=[pltpu.VMEM((SCAN_SUB, nl2), F32), pltpu.VMEM((2 * SCAN_SUB, nl2), F32)],
        compiler_params=_cparams("arbitrary"),
    )(u_perm, bb_mat, c_mat, ab_rows, ap_rows, d_skip)


def _ssm_bwd(name, dy_perm, u_perm, states, c_mat_t, bb_mat_t, abc_rows, apc_rows, d_skip):
    rows = u_perm.shape[0]
    nl2 = 2 * SSM_LANES
    nblk = rows // SCAN_BLOCK

    def body(dy_ref, u_ref, st_ref, prev_ref, ct_ref, bt_ref, ab_ref, ap_ref, d_ref,
             du_ref, g_ref, dab_ref, dd_ref, carry_ref, tmp_ref):
        i = pl.program_id(0)

        @pl.when(i == 0)
        def _():
            carry_ref[...] = jnp.zeros_like(carry_ref)
            dab_ref[...] = jnp.zeros_like(dab_ref)
            dd_ref[...] = jnp.zeros_like(dd_ref)

        dyv = dy_ref[...]
        g_ref[...] = _dot_nn(dyv.astype(BF16), ct_ref[...])
        have_prev = (i < nblk - 1).astype(F32)
        _scan_block(g_ref, carry_ref, tmp_ref, ab_ref, ap_ref, reverse=True,
                    sprev=(st_ref, prev_ref, have_prev, dab_ref))
        du_ref[...] = _dot_nn(g_ref[...].astype(BF16), bt_ref[...]) + d_ref[...] * dyv
        dd_ref[...] += jnp.sum(dyv * u_ref[...], axis=0, keepdims=True)

    const = lambda shape: pl.BlockSpec(shape, lambda i: (0, 0))
    blk = lambda cols: pl.BlockSpec((SCAN_BLOCK, cols), lambda i: (nblk - 1 - i, 0))
    per8 = SCAN_BLOCK // SCAN_SUB
    prev_spec = pl.BlockSpec((SCAN_SUB, nl2), lambda i: (jnp.maximum((nblk - 1 - i) * per8 - 1, 0), 0))
    return pl.pallas_call(
        body, name=name, grid=(nblk,),
        in_specs=[blk(SSM_WIDTH), blk(SSM_WIDTH), blk(nl2), prev_spec, const((SSM_WIDTH, nl2)),
                  const((nl2, SSM_WIDTH)), const((SCAN_SUB, nl2)), const((SCAN_SUB, nl2)), const((1, SSM_WIDTH))],
        out_specs=[blk(SSM_WIDTH), blk(nl2), const((SCAN_SUB, nl2)), const((1, SSM_WIDTH))],
        out_shape=[jax.ShapeDtypeStruct((rows, SSM_WIDTH), F32), jax.ShapeDtypeStruct((rows, nl2), F32),
                   jax.ShapeDtypeStruct((SCAN_SUB, nl2), F32), jax.ShapeDtypeStruct((1, SSM_WIDTH), F32)],
        scratch_shapes=[pltpu.VMEM((SCAN_SUB, nl2), F32), pltpu.VMEM((2 * SCAN_SUB, nl2), F32)],
        compiler_params=_cparams("arbitrary"),
    )(dy_perm, u_perm, states, states, c_mat_t, bb_mat_t, abc_rows, apc_rows, d_skip)


def _scan_order(a):
    rows, cols = a.shape
    return a.reshape(rows // SCAN_BLOCK, SCAN_SUB, SCAN_STEPS, cols).transpose(0, 2, 1, 3).reshape(rows, cols)


def _time_order(a):
    rows, cols = a.shape
    return a.reshape(rows // SCAN_BLOCK, SCAN_STEPS, SCAN_SUB, cols).transpose(0, 2, 1, 3).reshape(rows, cols)


def _adamw(name, w, m, v, gparts, tr):
    rows, cols = w.shape

    def body(w_ref, m_ref, v_ref, g_ref, og_ref, od_ref, om_ref, ov_ref):
        g = g_ref[0].astype(F32)
        for i in range(1, N_DEV):
            g = g + g_ref[i].astype(F32)
        m_new = B1 * m_ref[...] + (1.0 - B1) * g
        v_new = B2 * v_ref[...] + (1.0 - B2) * (g * g)
        m_hat = m_new / (1.0 - B1 ** STEP)
        v_hat = v_new / (1.0 - B2 ** STEP)
        og_ref[...] = g
        od_ref[...] = -LR * (m_hat / (jnp.sqrt(v_hat) + ADAM_EPS) + WD * w_ref[...])
        om_ref[...] = m_new
        ov_ref[...] = v_new

    spec = pl.BlockSpec((tr, cols), lambda i: (i, 0))
    shape = jax.ShapeDtypeStruct((rows, cols), F32)
    return pl.pallas_call(
        body, name=name, grid=(rows // tr,),
        in_specs=[spec, spec, spec, pl.BlockSpec((N_DEV, tr, cols), lambda i: (0, i, 0))],
        out_specs=[spec] * 4, out_shape=[shape] * 4,
        compiler_params=_cparams("parallel"),
    )(w, m, v, gparts)


_SHARDED = (
    ("ffn1_w_gate", True, (352, 1024)), ("ffn1_w_up", True, (352, 1024)), ("ffn1_w_down", False, (352, 1024)),
    ("w_in", True, (608, 1024)), ("ssm_w_glu", True, (128, 512)), ("w_attn_branch", True, (128, 256)),
    ("w_ssm_branch", True, (128, 512)), ("w_out", False, (128, 1024)),
    ("ffn2_w_gate", True, (352, 1024)), ("ffn2_w_up", True, (352, 1024)), ("ffn2_w_down", False, (352, 1024)),
)
_SMALL = ("ffn1_norm", "mix_norm", "gate_bias", "rel_bias_table", "ssm_a_re", "ssm_a_im", "ssm_log_dt",
          "ssm_b_re", "ssm_b_im", "ssm_c_re", "ssm_c_im", "ssm_d", "ffn2_norm", "final_norm")
_ORDER = ("ffn1_norm", "ffn1_w_gate", "ffn1_w_up", "ffn1_w_down", "mix_norm", "w_in", "gate_bias",
          "rel_bias_table", "ssm_a_re", "ssm_a_im", "ssm_log_dt", "ssm_b_re", "ssm_b_im", "ssm_c_re",
          "ssm_c_im", "ssm_d", "ssm_w_glu", "w_attn_branch", "w_ssm_branch", "w_out", "ffn2_norm",
          "ffn2_w_gate", "ffn2_w_up", "ffn2_w_down", "final_norm")


def _pack_rows(shape):
    return shape[0] * shape[1] // D_MODEL


def _pack_sharded(ws):
    parts = []
    for nm, tr, shape in _SHARDED:
        a = ws[nm].T if tr else ws[nm]
        parts.append(a.reshape(_pack_rows(shape), D_MODEL))
    return jnp.concatenate(parts, axis=0)


def _unpack_sharded(pack):
    out, r0 = {}, 0
    for nm, tr, shape in _SHARDED:
        n = _pack_rows(shape)
        a = pack[r0:r0 + n].reshape(shape)
        out[nm] = a.T if tr else a
        r0 += n
    return out


def _unpack_gathered(gath):
    out, r0 = {}, 0
    for nm, _, shape in _SHARDED:
        n = _pack_rows(shape)
        out[nm] = gath[:, r0:r0 + n].reshape(N_DEV * shape[0], shape[1])
        r0 += n
    return out


def _pack_grads(gs):
    parts = []
    for nm, _, shape in _SHARDED:
        parts.append(gs[nm].astype(BF16).reshape(N_DEV, _pack_rows(shape), D_MODEL))
    return jnp.concatenate(parts, axis=1)


def _pack_small(ws):
    flat = jnp.concatenate([ws[nm].reshape(-1) for nm in _SMALL])
    pad = (-flat.shape[0]) % (8 * 128)
    return jnp.pad(flat, (0, pad)).reshape(-1, 128)


def _unpack_small(pack, like):
    flat, out, p0 = pack.reshape(-1), {}, 0
    for nm in _SMALL:
        n = like[nm].size
        out[nm] = flat[p0:p0 + n].reshape(like[nm].shape)
        p0 += n
    return out


def _to_dilated(a, dil):
    rows = a.shape[0]
    return a.reshape(rows // dil, dil, HEADS_PER_GROUP, HEAD_DIM).transpose(1, 2, 0, 3)


def _dilated_to_heads(a):
    dil, nh, m, dh = a.shape
    return a.transpose(1, 2, 0, 3).reshape(nh, m * dil, dh)


def _heads_to_dilated(a, dil):
    nh, rows, dh = a.shape
    return a.reshape(nh, rows // dil, dil, dh).transpose(2, 0, 1, 3)


def _block_diag(blocks_gab):
    g, a, b = blocks_gab.shape
    eye = jnp.eye(g, dtype=blocks_gab.dtype)
    return (blocks_gab[:, :, None, :] * eye[:, None, :, None]).reshape(g * a, g * b)


def _diag_blocks(mat, a, b):
    g = mat.shape[0] // a
    m4 = mat.reshape(g, a, g, b)
    return jnp.stack([m4[i, :, i, :] for i in range(g)])


def _local_step(xs, target, wf, small):
    rows = xs.shape[0]
    gfull, gsmall = {}, {}

    x1, h1, gg1, uu1 = _ffn_fwd("ffn1_fwd", xs, small["ffn1_norm"], wf["ffn1_w_gate"], wf["ffn1_w_up"],
                                wf["ffn1_w_down"])
    hmix = _rms_fwd("mix_norm_fwd", x1, small["mix_norm"])
    w_in = wf["w_in"]
    w_qkv, w_u, w_g = w_in[:3 * ATTN_WIDTH], w_in[3 * ATTN_WIDTH:3 * ATTN_WIDTH + SSM_WIDTH], w_in[3 * ATTN_WIDTH + SSM_WIDTH:]
    qscale = jnp.concatenate([jnp.full((1, ATTN_WIDTH), HEAD_DIM ** -0.5, F32), jnp.ones((1, 2 * ATTN_WIDTH), F32)], axis=1)
    qkv, = _mm("in_qkv", [(hmix, w_qkv)], True, 3 * ATTN_WIDTH, [BF16],
               epilogue=lambda acc, sc: (acc * sc,), extras=[(qscale, 0)])
    u, = _mm("in_u", [(hmix, w_u)], True, SSM_WIDTH, [F32])
    gates, = _mm("in_gates", [(hmix, w_g)], True, 2 * D_MODEL, [F32],
                 epilogue=lambda acc, b: (_sigmoid(acc + b),), extras=[(small["gate_bias"], 0)])

    buckets = jnp.asarray(_bucket_table())
    bias = _bias_fwd("rel_bias_fwd", buckets, small["rel_bias_table"])
    qkv_d, o_h, lse_h = [], [], []
    for g, dil in enumerate(DILATIONS):
        cols = [qkv[:, s * ATTN_WIDTH + g * ATTN_OUT:s * ATTN_WIDTH + (g + 1) * ATTN_OUT] for s in range(3)]
        qd, kd, vd = [_to_dilated(c, dil) for c in cols]
        bias_g = bias[g * HEADS_PER_GROUP:(g + 1) * HEADS_PER_GROUP]
        o_g, lse_g = _attn_fwd(f"attn_fwd_{g}", qd, kd, vd, bias_g)
        qkv_d.append((qd, kd, vd, bias_g))
        o_h.append(_dilated_to_heads(o_g))
        lse_h.append(_dilated_to_heads(lse_g))
    oa_h = _combine_fwd("attn_combine_fwd", o_h, lse_h)
    oa = oa_h.transpose(1, 0, 2).reshape(rows, ATTN_OUT).astype(BF16)
    y_attn, = _mm("attn_branch", [(oa, wf["w_attn_branch"])], True, D_MODEL, [F32])

    ab_re, ab_im, ap_re, ap_im, bb_re, bb_im = _ssm_params_fwd(
        "ssm_params_fwd", small["ssm_a_re"], small["ssm_a_im"], small["ssm_log_dt"].reshape(SSM_GROUPS, 1),
        small["ssm_b_re"].transpose(2, 0, 1), small["ssm_b_im"].transpose(2, 0, 1))

    def lanes(re, im, sign=1.0):
        row = jnp.concatenate([re.reshape(1, SSM_LANES), sign * im.reshape(1, SSM_LANES)], axis=1)
        return jnp.broadcast_to(row, (SCAN_SUB, 2 * SSM_LANES))

    bb_mat = jnp.concatenate([_block_diag(bb_re.transpose(1, 0, 2)), _block_diag(bb_im.transpose(1, 0, 2))], axis=1)
    c_mat_t = jnp.concatenate([_block_diag(small["ssm_c_re"]), -_block_diag(small["ssm_c_im"])], axis=1)
    bb_mat, c_mat_t = bb_mat.astype(BF16), c_mat_t.astype(BF16)
    d_skip = small["ssm_d"].reshape(1, SSM_WIDTH)
    u_perm = _scan_order(u)
    y_perm, states = _ssm_fwd("ssm_fwd", u_perm, bb_mat, c_mat_t.T, lanes(ab_re, ab_im), lanes(ap_re, ap_im), d_skip)
    y_raw = _time_order(y_perm)

    def gelu_fn(yv):
        return (jax.nn.gelu(yv),)

    ygelu, = _ew("ssm_gelu", gelu_fn, [y_raw], [SSM_WIDTH], [BF16])
    glu, = _mm("ssm_glu", [(ygelu, wf["ssm_w_glu"])], True, 2 * SSM_WIDTH, [F32])
    ysg, = _ew("ssm_glu_act", lambda gv: (gv[:, :SSM_WIDTH] * _sigmoid(gv[:, SSM_WIDTH:]),), [glu], [SSM_WIDTH], [BF16])
    nb_d = D_MODEL // 256
    y_ssm, merged = _mm("ssm_branch_merge", [(ysg, wf["w_ssm_branch"])], True, D_MODEL, [F32, BF16],
                        epilogue=lambda acc, ga, gs, ya: (acc, ga * ya + gs * acc),
                        extras=[(gates, 0), (gates, nb_d), (y_attn, 0)])
    x2, = _mm("mix_out", [(merged, wf["w_out"])], False, D_MODEL, [F32],
              epilogue=lambda acc, res: (res + acc,), extras=[(x1, 0)])
    x3, h2, gg2, uu2 = _ffn_fwd("ffn2_fwd", x2, small["ffn2_norm"], wf["ffn2_w_gate"], wf["ffn2_w_up"],
                                wf["ffn2_w_down"])
    dx3, gsmall["final_norm"], loss = _final_loss("final_loss", x3, small["final_norm"].reshape(1, D_MODEL), target)

    dx2, dgg2, duu2, act2, gsmall["ffn2_norm"] = _ffn_bwd(
        "ffn2_bwd", dx3, x2, small["ffn2_norm"], gg2, uu2, wf["ffn2_w_gate"], wf["ffn2_w_up"], wf["ffn2_w_down"])
    gfull["ffn2_w_gate"] = _mm_tn("ffn2_dwg", dgg2, h2)
    gfull["ffn2_w_up"] = _mm_tn("ffn2_dwu", duu2, h2)
    gfull["ffn2_w_down"] = _mm_tn("ffn2_dwd", act2, dx3, scale=0.5)

    def merge_bwd(dm, ga, gs, ya, ys):
        return (dm * ga, dm * gs, dm * ya * ga * (1.0 - ga), dm * ys * gs * (1.0 - gs))

    dya, dys, dzga, dzgs = _mm("mix_out_bwd", [(dx2, wf["w_out"])], True, D_MODEL, [BF16] * 4, epilogue=merge_bwd,
                               extras=[(gates, 0), (gates, nb_d), (y_attn, 0), (y_ssm, 0)])
    gfull["w_out"] = _mm_tn("dw_out", merged, dx2)
    gsmall["gate_bias"] = jnp.concatenate([_colsum("dgate_bias_a", dzga), _colsum("dgate_bias_s", dzgs)], axis=1)

    gfull["w_ssm_branch"] = _mm_tn("dw_ssm_branch", dys, ysg)
    nb_s = SSM_WIDTH // 256

    def glu_bwd(dysg, av, bv):
        sb = _sigmoid(bv)
        return (dysg * sb, dysg * av * sb * (1.0 - sb))

    dglu_a, dglu_b = _mm("ssm_branch_bwd", [(dys, wf["w_ssm_branch"])], False, SSM_WIDTH, [BF16, BF16],
                         epilogue=glu_bwd, extras=[(glu, 0), (glu, nb_s)])
    w_glu = wf["ssm_w_glu"]
    gfull["ssm_w_glu"] = jnp.concatenate([_mm_tn("dw_glu_a", dglu_a, ygelu), _mm_tn("dw_glu_b", dglu_b, ygelu)], axis=0)

    def gelu_bwd(acc, yv):
        _, vjp = jax.vjp(jax.nn.gelu, yv)
        return (vjp(acc)[0],)

    dy_raw, = _mm("ssm_glu_bwd", [(dglu_a, w_glu[:SSM_WIDTH]), (dglu_b, w_glu[SSM_WIDTH:])], False, SSM_WIDTH, [F32],
                  epilogue=gelu_bwd, extras=[(y_raw, 0)])
    dy_perm = _scan_order(dy_raw)
    du_perm, g_states, dab_rows, gsmall_d = _ssm_bwd(
        "ssm_bwd", dy_perm, u_perm, states, c_mat_t, bb_mat.T, lanes(ab_re, ab_im, -1.0), lanes(ap_re, ap_im, -1.0), d_skip)
    du = _time_order(du_perm)
    gsmall["ssm_d"] = gsmall_d
    dbb_acc = _mm_tn("ssm_dbb", u_perm, g_states, bm=SSM_WIDTH)
    dc_acc = _mm_tn("ssm_dc", dy_perm, states, bm=SSM_WIDTH)
    dbb_re = _diag_blocks(dbb_acc[:, :SSM_LANES], SSM_GROUP, SSM_STATE).transpose(1, 0, 2)
    dbb_im = _diag_blocks(dbb_acc[:, SSM_LANES:], SSM_GROUP, SSM_STATE).transpose(1, 0, 2)
    gsmall["ssm_c_re"] = _diag_blocks(dc_acc[:, :SSM_LANES], SSM_GROUP, SSM_STATE)
    gsmall["ssm_c_im"] = -_diag_blocks(dc_acc[:, SSM_LANES:], SSM_GROUP, SSM_STATE)
    dab = _colsum("ssm_dab", dab_rows)
    d_ar, d_ai, d_ld, d_br, d_bi = _ssm_params_bwd(
        "ssm_params_bwd", small["ssm_a_re"], small["ssm_a_im"], small["ssm_log_dt"].reshape(SSM_GROUPS, 1),
        small["ssm_b_re"].transpose(2, 0, 1), small["ssm_b_im"].transpose(2, 0, 1),
        dab[:, :SSM_LANES].reshape(SSM_GROUPS, SSM_STATE), dab[:, SSM_LANES:].reshape(SSM_GROUPS, SSM_STATE),
        dbb_re, dbb_im)
    gsmall["ssm_a_re"], gsmall["ssm_a_im"], gsmall["ssm_log_dt"] = d_ar, d_ai, d_ld.reshape(SSM_GROUPS)
    gsmall["ssm_b_re"], gsmall["ssm_b_im"] = d_br.transpose(1, 2, 0), d_bi.transpose(1, 2, 0)

    gfull["w_attn_branch"] = _mm_tn("dw_attn_branch", dya, oa)
    doa, = _mm("attn_branch_bwd", [(dya, wf["w_attn_branch"])], False, ATTN_OUT, [F32])
    do_h = doa.reshape(rows, HEADS_PER_GROUP, HEAD_DIM).transpose(1, 0, 2)
    dc = _combine_bwd("attn_combine_bwd", do_h, oa_h, lse_h)
    dqkv_cols = [None] * 9
    dbias = []
    for g, dil in enumerate(DILATIONS):
        qd, kd, vd, bias_g = qkv_d[g]
        dq, dk, dv, db = _attn_bwd(f"attn_bwd_{g}", qd, kd, vd, _heads_to_dilated(dc[g], dil),
                                   _heads_to_dilated(lse_h[g], dil), _heads_to_dilated(dc[3 + g], dil), bias_g)
        dbias.append(db)
        for s, (arr, sc) in enumerate(((dq, HEAD_DIM ** -0.5), (dk, 1.0), (dv, 1.0))):
            tok = _dilated_to_heads(arr).transpose(1, 0, 2).reshape(rows, ATTN_OUT)
            dqkv_cols[3 * s + g] = (tok * sc).astype(BF16)
    dqkv = jnp.concatenate(dqkv_cols, axis=1)
    gsmall["rel_bias_table"] = _bias_bwd("rel_bias_bwd", buckets, jnp.concatenate(dbias, axis=0))[:, :N_GROUPS * HEADS_PER_GROUP]

    gfull["w_in"] = jnp.concatenate([
        _mm_tn("dw_in_qkv", dqkv, hmix), _mm_tn("dw_in_u", du, hmix),
        _mm_tn("dw_in_ga", dzga, hmix), _mm_tn("dw_in_gs", dzgs, hmix)], axis=0)
    dhmix, = _mm("in_bwd", [(dqkv, w_qkv), (du, w_u), (dzga, w_g[:D_MODEL]), (dzgs, w_g[D_MODEL:])], False, D_MODEL, [F32])
    dx1, gsmall["mix_norm"] = _rms_bwd("mix_norm_bwd", dhmix, x1, small["mix_norm"], dx2)

    dx, dgg1, duu1, act1, gsmall["ffn1_norm"] = _ffn_bwd(
        "ffn1_bwd", dx1, xs, small["ffn1_norm"], gg1, uu1, wf["ffn1_w_gate"], wf["ffn1_w_up"], wf["ffn1_w_down"])
    gfull["ffn1_w_gate"] = _mm_tn("ffn1_dwg", dgg1, h1)
    gfull["ffn1_w_up"] = _mm_tn("ffn1_dwu", duu1, h1)
    gfull["ffn1_w_down"] = _mm_tn("ffn1_dwd", act1, dx1, scale=0.5)
    return loss[0, 0], dx, gfull, gsmall


def kernel(x, ffn1_norm, ffn1_w_gate, ffn1_w_up, ffn1_w_down, mix_norm, w_in, gate_bias, rel_bias_table, ssm_a_re, ssm_a_im, ssm_log_dt, ssm_b_re, ssm_b_im, ssm_c_re, ssm_c_im, ssm_d, ssm_w_glu, w_attn_branch, w_ssm_branch, w_out, ffn2_norm, ffn2_w_gate, ffn2_w_up, ffn2_w_down, final_norm, loss_target, m_ffn1_norm, m_ffn1_w_gate, m_ffn1_w_up, m_ffn1_w_down, m_mix_norm, m_w_in, m_gate_bias, m_rel_bias_table, m_ssm_a_re, m_ssm_a_im, m_ssm_log_dt, m_ssm_b_re, m_ssm_b_im, m_ssm_c_re, m_ssm_c_im, m_ssm_d, m_ssm_w_glu, m_w_attn_branch, m_w_ssm_branch, m_w_out, m_ffn2_norm, m_ffn2_w_gate, m_ffn2_w_up, m_ffn2_w_down, m_final_norm, v_ffn1_norm, v_ffn1_w_gate, v_ffn1_w_up, v_ffn1_w_down, v_mix_norm, v_w_in, v_gate_bias, v_rel_bias_table, v_ssm_a_re, v_ssm_a_im, v_ssm_log_dt, v_ssm_b_re, v_ssm_b_im, v_ssm_c_re, v_ssm_c_im, v_ssm_d, v_ssm_w_glu, v_w_attn_branch, v_w_ssm_branch, v_w_out, v_ffn2_norm, v_ffn2_w_gate, v_ffn2_w_up, v_ffn2_w_down, v_final_norm):
    given = dict(locals())
    shapes = {nm: given[nm].shape for nm in _ORDER}

    def strip(a):
        return a[0] if a.ndim >= 2 and a.shape[0] == 1 else a

    w = {nm: strip(given[nm]) for nm in _ORDER}
    m = {nm: strip(given["m_" + nm]) for nm in _ORDER}
    v = {nm: strip(given["v_" + nm]) for nm in _ORDER}
    for d in (w, m, v):
        d["rel_bias_table"] = d["rel_bias_table"].reshape(N_BUCKETS, N_GROUPS * HEADS_PER_GROUP)

    w_pack = _pack_sharded(w)
    gathered = _all_gather("gather_weights", w_pack.astype(BF16))
    wf = _unpack_gathered(gathered)
    small = {nm: w[nm] for nm in _SMALL}
    small_in = dict(small)
    for nm in ("ffn1_norm", "mix_norm", "ffn2_norm", "gate_bias"):
        small_in[nm] = small[nm].reshape(1, -1)

    loss, dx, gfull, gsmall = _local_step(x[0], loss_target[0], wf, small_in)

    recv = _all_to_all("scatter_grads", _pack_grads(gfull))
    tr = 64
    g_pack, d_pack, m_pack, v_pack = _adamw("adamw_sharded", w_pack, _pack_sharded(m), _pack_sharded(v), recv, tr)
    gs_pack = _pack_small({nm: gsmall[nm].reshape(small[nm].shape) for nm in _SMALL})
    gs_all = _all_gather("gather_small_grads", gs_pack)
    sm = _adamw("adamw_small", _pack_small(small), _pack_small({nm: m[nm] for nm in _SMALL}),
                _pack_small({nm: v[nm] for nm in _SMALL}), gs_all, gs_pack.shape[0])

    loss = lax.psum(loss, ("x", "y", "c"))
    outs = []
    for pack_big, pack_small in zip((g_pack, d_pack, m_pack, v_pack), sm):
        big = _unpack_sharded(pack_big)
        sml = _unpack_small(pack_small, small)
        outs.append([(big[nm] if nm in big else sml[nm]).reshape(shapes[nm]) for nm in _ORDER])
    return (loss, dx[None], *outs[0], *outs[1], *outs[2], *outs[3])
```

```python
import functools
import math

import numpy as np
import jax
import jax.numpy as jnp
from jax import lax
from jax.experimental import pallas as pl
from jax.experimental.pallas import tpu as pltpu

F32 = jnp.float32
BF16 = jnp.bfloat16

N_DEV = 8
D_MODEL = 1024
D_FF = 2816
HEAD_DIM = 64
HEADS_PER_GROUP = 4
DILATIONS = (1, 4, 16)
N_GROUPS = 3
ATTN_WIDTH = 768
ATTN_OUT = 256
BLOCK = 128
N_BUCKETS = 32
MAX_DISTANCE = 2048
NEG_INF = -1e30
SSM_WIDTH = 512
SSM_GROUPS = 32
SSM_GROUP = 16
SSM_STATE = 64
SSM_LANES = SSM_GROUPS * SSM_STATE
EPS = 1e-6
LR, B1, B2, ADAM_EPS, WD, STEP = 0.001, 0.9, 0.999, 1e-08, 0.01, 10

VMEM_LIMIT_BYTES = 56 * 1024 * 1024
SCAN_BLOCK = 256
SCAN_SUB = 8
SCAN_STEPS = SCAN_BLOCK // SCAN_SUB
SCAN_LANES = 512

MESH = pl.DeviceIdType.MESH


def _cparams(*sem):
    return pltpu.CompilerParams(dimension_semantics=sem, vmem_limit_bytes=VMEM_LIMIT_BYTES)


def _dot(a, b, dims):
    return lax.dot_general(a, b, (dims, ((), ())), preferred_element_type=F32)


def _dot_nn(a, b):
    return _dot(a, b, ((1,), (0,)))


def _dot_nt(a, b):
    return _dot(a, b, ((1,), (1,)))


def _dot_tn(a, b):
    return _dot(a, b, ((0,), (0,)))


def _sigmoid(x):
    return 1.0 / (1.0 + jnp.exp(-x))


def _all_gather(name, xs):
    rows, cols = xs.shape

    def body(x_ref, out_ref, send_sems, recv_sems, local_sem):
        x, y, c = lax.axis_index("x"), lax.axis_index("y"), lax.axis_index("c")
        me, sibling = (x, y, c), (x, y, 1 - c)
        chips = [(1 - x, y), (x, 1 - y), (1 - x, 1 - y)]

        def slot(px, py, pc):
            return out_ref.at[4 * px + 2 * py + pc]

        def copy(k, block, to, src=None):
            return pltpu.make_async_remote_copy(
                src_ref=slot(*block) if src is None else src, dst_ref=slot(*block),
                send_sem=send_sems.at[k], recv_sem=recv_sems.at[k], device_id=to, device_id_type=MESH)

        mine = pltpu.make_async_copy(x_ref, slot(*me), local_sem)
        mine.start()
        first = [copy(0, me, sibling, src=x_ref)]
        first += [copy(1 + j, me, (*chip, c), src=x_ref) for j, chip in enumerate(chips)]
        for cp in first:
            cp.start()
        passed = [copy(4 + j, (*chip, c), sibling) for j, chip in enumerate(chips)]
        for j, chip in enumerate(chips):
            copy(1 + j, (*chip, c), me).wait_recv()
            passed[j].start()
        copy(0, sibling, me).wait_recv()
        for j, chip in enumerate(chips):
            copy(4 + j, (*chip, 1 - c), me).wait_recv()
        for cp in first + passed:
            cp.wait_send()
        mine.wait()

    return pl.pallas_call(
        body, name=name,
        out_shape=jax.ShapeDtypeStruct((N_DEV, rows, cols), xs.dtype),
        in_specs=[pl.BlockSpec(memory_space=pl.ANY)],
        out_specs=pl.BlockSpec(memory_space=pl.ANY),
        scratch_shapes=[pltpu.SemaphoreType.DMA((7,)), pltpu.SemaphoreType.DMA((7,)), pltpu.SemaphoreType.DMA],
    )(xs)


def _all_to_all(name, xs):
    _, rows, cols = xs.shape

    def body(x_ref, out_ref, send_sems, recv_sems, local_sem):
        x, y, c = lax.axis_index("x"), lax.axis_index("y"), lax.axis_index("c")
        me = 4 * x + 2 * y + c
        mine = pltpu.make_async_copy(x_ref.at[me], out_ref.at[me], local_sem)
        mine.start()
        copies = []
        for k in range(1, N_DEV):
            px = 1 - x if k & 4 else x
            py = 1 - y if k & 2 else y
            pc = 1 - c if k & 1 else c
            cp = pltpu.make_async_remote_copy(
                src_ref=x_ref.at[4 * px + 2 * py + pc], dst_ref=out_ref.at[me],
                send_sem=send_sems.at[k - 1], recv_sem=recv_sems.at[k - 1],
                device_id=(px, py, pc), device_id_type=MESH)
            cp.start()
            copies.append(cp)
        for cp in copies:
            cp.wait()
        mine.wait()

    return pl.pallas_call(
        body, name=name,
        out_shape=jax.ShapeDtypeStruct(xs.shape, xs.dtype),
        in_specs=[pl.BlockSpec(memory_space=pl.ANY)],
        out_specs=pl.BlockSpec(memory_space=pl.ANY),
        scratch_shapes=[pltpu.SemaphoreType.DMA((7,)), pltpu.SemaphoreType.DMA((7,)), pltpu.SemaphoreType.DMA],
    )(xs)


def _mm(name, pairs, nt, n_cols, out_dtypes, epilogue=None, extras=(), tm=1024, tn=512):
    rows = pairs[0][0].shape[0]
    tm = min(tm, rows)
    tn = min(tn, n_cols)
    na, ne = len(pairs), len(extras)

    def body(*refs):
        a_refs, w_refs = refs[:na], refs[na:2 * na]
        e_refs, o_refs = refs[2 * na:2 * na + ne], refs[2 * na + ne:]
        acc = None
        for a_ref, w_ref in zip(a_refs, w_refs):
            a = a_ref[...].astype(BF16)
            w = w_ref[...].astype(BF16)
            p = _dot_nt(a, w) if nt else _dot_nn(a, w)
            acc = p if acc is None else acc + p
        outs = (acc,) if epilogue is None else epilogue(acc, *[e[...] for e in e_refs])
        for o_ref, o in zip(o_refs, outs):
            o_ref[...] = o.astype(o_ref.dtype)

    in_specs = [pl.BlockSpec((tm, a.shape[1]), lambda i, j: (i, 0)) for a, _ in pairs]
    for _, w in pairs:
        if nt:
            in_specs.append(pl.BlockSpec((tn, w.shape[1]), lambda i, j: (j, 0)))
        else:
            in_specs.append(pl.BlockSpec((w.shape[0], tn), lambda i, j: (0, j)))
    for e, col_off in extras:
        off = col_off // tn
        if e.shape[0] == 1:
            in_specs.append(pl.BlockSpec((1, tn), lambda i, j, off=off: (0, j + off)))
        else:
            in_specs.append(pl.BlockSpec((tm, tn), lambda i, j, off=off: (i, j + off)))
    out_specs = [pl.BlockSpec((tm, tn), lambda i, j: (i, j)) for _ in out_dtypes]
    outs = pl.pallas_call(
        body, name=name, grid=(rows // tm, n_cols // tn),
        in_specs=in_specs, out_specs=out_specs,
        out_shape=[jax.ShapeDtypeStruct((rows, n_cols), dt) for dt in out_dtypes],
        compiler_params=_cparams("parallel", "arbitrary"),
    )(*[a for a, _ in pairs], *[w for _, w in pairs], *[e for e, _ in extras])
    return outs


def _tn_rows(m):
    return max(b for b in range(128, min(m, 1408) + 1, 128) if m % b == 0)


def _mm_tn(name, a, b, scale=1.0, bm=None, tk=1024):
    rows, m = a.shape
    n = b.shape[1]
    bm = _tn_rows(m) if bm is None else bm
    tk = min(tk, rows)
    nk = rows // tk

    def body(a_ref, b_ref, o_ref):
        k = pl.program_id(1)

        @pl.when(k == 0)
        def _():
            o_ref[...] = jnp.zeros_like(o_ref)

        o_ref[...] += _dot_tn(a_ref[...].astype(BF16), b_ref[...].astype(BF16))
        if scale != 1.0:
            @pl.when(k == nk - 1)
            def _():
                o_ref[...] = o_ref[...] * scale

    return pl.pallas_call(
        body, name=name, grid=(m // bm, nk),
        in_specs=[pl.BlockSpec((tk, bm), lambda i, k: (k, i)), pl.BlockSpec((tk, n), lambda i, k: (k, 0))],
        out_specs=pl.BlockSpec((bm, n), lambda i, k: (i, 0)),
        out_shape=jax.ShapeDtypeStruct((m, n), F32),
        compiler_params=_cparams("parallel", "arbitrary"),
    )(a, b)


def _colsum(name, xs, tm=512):
    rows, cols = xs.shape
    tm = min(tm, rows)

    def body(x_ref, o_ref):
        @pl.when(pl.program_id(0) == 0)
        def _():
            o_ref[...] = jnp.zeros_like(o_ref)

        o_ref[...] += jnp.sum(x_ref[...].astype(F32), axis=0, keepdims=True)

    return pl.pallas_call(
        body, name=name, grid=(rows // tm,),
        in_specs=[pl.BlockSpec((tm, cols), lambda i: (i, 0))],
        out_specs=pl.BlockSpec((1, cols), lambda i: (0, 0)),
        out_shape=jax.ShapeDtypeStruct((1, cols), F32),
        compiler_params=_cparams("arbitrary"),
    )(xs)


def _ew(name, fn, ins, out_cols, out_dtypes, tm=512):
    rows = ins[0].shape[0]
    tm = min(tm, rows)
    ni = len(ins)

    def body(*refs):
        outs = fn(*[r[...] for r in refs[:ni]])
        for o_ref, o in zip(refs[ni:], outs):
            o_ref[...] = o.astype(o_ref.dtype)

    def spec(shape):
        if shape[0] == 1:
            return pl.BlockSpec((1, shape[1]), lambda i: (0, 0))
        return pl.BlockSpec((tm, shape[1]), lambda i: (i, 0))

    return pl.pallas_call(
        body, name=name, grid=(rows // tm,),
        in_specs=[spec(a.shape) for a in ins],
        out_specs=[pl.BlockSpec((tm, c), lambda i: (i, 0)) for c in out_cols],
        out_shape=[jax.ShapeDtypeStruct((rows, c), dt) for c, dt in zip(out_cols, out_dtypes)],
        compiler_params=_cparams("parallel"),
    )(*ins)


def _rms_parts(xv):
    r = lax.rsqrt(jnp.mean(xv * xv, axis=-1, keepdims=True) + EPS)
    return r, xv * r


def _rms_bwd_dx(dh, gain, r, xh):
    dxh = dh * gain
    return r * (dxh - xh * jnp.mean(dxh * xh, axis=-1, keepdims=True))


def _rms_fwd(name, xs, gain):
    def fn(xv, g):
        _, xh = _rms_parts(xv)
        return (xh * g,)

    return _ew(name, fn, [xs, gain], [xs.shape[1]], [BF16])[0]


def _rms_bwd(name, dh, xs, gain, dres, tm=512):
    rows, d = xs.shape
    tm = min(tm, rows)

    def body(dh_ref, x_ref, g_ref, dres_ref, dx_ref, dg_ref):
        r, xh = _rms_parts(x_ref[...])
        dhv = dh_ref[...]
        dx_ref[...] = dres_ref[...] + _rms_bwd_dx(dhv, g_ref[...], r, xh)

        @pl.when(pl.program_id(0) == 0)
        def _():
            dg_ref[...] = jnp.zeros_like(dg_ref)

        dg_ref[...] += jnp.sum(dhv * xh, axis=0, keepdims=True)

    tile = pl.BlockSpec((tm, d), lambda i: (i, 0))
    row = pl.BlockSpec((1, d), lambda i: (0, 0))
    return pl.pallas_call(
        body, name=name, grid=(rows // tm,),
        in_specs=[tile, tile, row, tile], out_specs=[tile, row],
        out_shape=[jax.ShapeDtypeStruct((rows, d), F32), jax.ShapeDtypeStruct((1, d), F32)],
        compiler_params=_cparams("arbitrary"),
    )(dh, xs, gain, dres)


def _ffn_fwd(name, xs, gain, wg_t, wu_t, wd, tm=512, tf=1408):
    rows, d = xs.shape
    f_all = wd.shape[0]
    tm = min(tm, rows)
    nf = f_all // tf

    def body(x_ref, g_ref, wg_ref, wu_ref, wd_ref, xo_ref, h_ref, gg_ref, uu_ref, acc_ref):
        f = pl.program_id(1)

        @pl.when(f == 0)
        def _():
            _, xh = _rms_parts(x_ref[...])
            h_ref[...] = (xh * g_ref[...]).astype(BF16)
            acc_ref[...] = jnp.zeros_like(acc_ref)

        h = h_ref[...]
        gg = _dot_nt(h, wg_ref[...])
        uu = _dot_nt(h, wu_ref[...])
        act = gg * _sigmoid(gg) * uu
        acc_ref[...] += _dot_nn(act.astype(BF16), wd_ref[...])
        gg_ref[...] = gg.astype(BF16)
        uu_ref[...] = uu.astype(BF16)

        @pl.when(f == nf - 1)
        def _():
            xo_ref[...] = x_ref[...] + 0.5 * acc_ref[...]

    tile = pl.BlockSpec((tm, d), lambda i, f: (i, 0))
    wspec = pl.BlockSpec((tf, d), lambda i, f: (f, 0))
    hid = pl.BlockSpec((tm, tf), lambda i, f: (i, f))
    return pl.pallas_call(
        body, name=name, grid=(rows // tm, nf),
        in_specs=[tile, pl.BlockSpec((1, d), lambda i, f: (0, 0)), wspec, wspec, wspec],
        out_specs=[tile, tile, hid, hid],
        out_shape=[jax.ShapeDtypeStruct((rows, d), F32), jax.ShapeDtypeStruct((rows, d), BF16),
                   jax.ShapeDtypeStruct((rows, f_all), BF16), jax.ShapeDtypeStruct((rows, f_all), BF16)],
        scratch_shapes=[pltpu.VMEM((tm, d), F32)],
        compiler_params=_cparams("parallel", "arbitrary"),
    )(xs, gain, wg_t, wu_t, wd)


def _ffn_bwd(name, dxo, xs, gain, gg_all, uu_all, wg_t, wu_t, wd, tm=256, tf=1408):
    rows, d = xs.shape
    f_all = wd.shape[0]
    tm = min(tm, rows)
    nf = f_all // tf

    def body(dxo_ref, x_ref, g_ref, gg_ref, uu_ref, wg_ref, wu_ref, wd_ref,
             dx_ref, dgg_ref, duu_ref, act_ref, dgain_ref, df_ref, acc_ref):
        i, f = pl.program_id(0), pl.program_id(1)

        @pl.when(f == 0)
        def _():
            df_ref[...] = (0.5 * dxo_ref[...]).astype(BF16)
            acc_ref[...] = jnp.zeros_like(acc_ref)

        gg = gg_ref[...].astype(F32)
        uu = uu_ref[...].astype(F32)
        sg = _sigmoid(gg)
        silu = gg * sg
        dact = _dot_nt(df_ref[...], wd_ref[...])
        duu = (dact * silu).astype(BF16)
        dgg = (dact * uu * (sg * (1.0 + gg * (1.0 - sg)))).astype(BF16)
        act_ref[...] = (silu * uu).astype(BF16)
        dgg_ref[...] = dgg
        duu_ref[...] = duu
        acc_ref[...] += _dot_nn(dgg, wg_ref[...]) + _dot_nn(duu, wu_ref[...])

        @pl.when(f == nf - 1)
        def _():
            r, xh = _rms_parts(x_ref[...])
            dh = acc_ref[...]
            dx_ref[...] = dxo_ref[...] + _rms_bwd_dx(dh, g_ref[...], r, xh)
            part = jnp.sum(dh * xh, axis=0, keepdims=True)

            @pl.when(i == 0)
            def _():
                dgain_ref[...] = part

            @pl.when(i > 0)
            def _():
                dgain_ref[...] += part

    tile = pl.BlockSpec((tm, d), lambda i, f: (i, 0))
    row = pl.BlockSpec((1, d), lambda i, f: (0, 0))
    wspec = pl.BlockSpec((tf, d), lambda i, f: (f, 0))
    hid = pl.BlockSpec((tm, tf), lambda i, f: (i, f))
    hid_shape = jax.ShapeDtypeStruct((rows, f_all), BF16)
    return pl.pallas_call(
        body, name=name, grid=(rows // tm, nf),
        in_specs=[tile, tile, row, hid, hid, wspec, wspec, wspec],
        out_specs=[tile, hid, hid, hid, row],
        out_shape=[jax.ShapeDtypeStruct((rows, d), F32), hid_shape, hid_shape, hid_shape,
                   jax.ShapeDtypeStruct((1, d), F32)],
        scratch_shapes=[pltpu.VMEM((tm, d), BF16), pltpu.VMEM((tm, d), F32)],
        compiler_params=_cparams("arbitrary", "arbitrary"),
    )(dxo, xs, gain, gg_all, uu_all, wg_t, wu_t, wd)


def _final_loss(name, xs, gain, target, tm=512):
    rows, d = xs.shape
    tm = min(tm, rows)

    def body(x_ref, g_ref, t_ref, dx_ref, dg_ref, loss_ref):
        r, xh = _rms_parts(x_ref[...])
        gain_v = g_ref[...]
        err = xh * gain_v - t_ref[...]
        dy = err * (1.0 / d)
        dx_ref[...] = _rms_bwd_dx(dy, gain_v, r, xh)

        @pl.when(pl.program_id(0) == 0)
        def _():
            dg_ref[...] = jnp.zeros_like(dg_ref)
            loss_ref[...] = jnp.zeros_like(loss_ref)

        dg_ref[...] += jnp.sum(dy * xh, axis=0, keepdims=True)
        per_tok = jnp.mean(err * err, axis=-1, keepdims=True)
        loss_ref[...] += 0.5 * jnp.sum(per_tok, axis=0, keepdims=True)

    tile = pl.BlockSpec((tm, d), lambda i: (i, 0))
    row = pl.BlockSpec((1, d), lambda i: (0, 0))
    return pl.pallas_call(
        body, name=name, grid=(rows // tm,),
        in_specs=[tile, row, tile],
        out_specs=[tile, row, pl.BlockSpec((1, 1), lambda i: (0, 0))],
        out_shape=[jax.ShapeDtypeStruct((rows, d), F32), jax.ShapeDtypeStruct((1, d), F32),
                   jax.ShapeDtypeStruct((1, 1), F32)],
        compiler_params=_cparams("arbitrary"),
    )(xs, gain, target)


def _bucket_table():
    out = []
    for dil in DILATIONS:
        qi = np.arange(BLOCK)[:, None]
        kj = np.arange(2 * BLOCK)[None, :]
        dist = (np.maximum(qi + BLOCK - kj, 0) * dil).astype(np.int32)
        max_exact = N_BUCKETS // 2
        dd = np.maximum(dist, 1).astype(np.float32)
        large = max_exact + (np.log(dd / np.float32(max_exact)) / np.float32(math.log(MAX_DISTANCE / max_exact))
                             * np.float32(N_BUCKETS - max_exact)).astype(np.int32)
        large = np.minimum(large, N_BUCKETS - 1)
        out.append(np.where(dist < max_exact, dist, large).astype(np.int32))
    return np.stack(out)


def _bias_fwd(name, buckets, table):
    def body(bk_ref, tab_ref, o_ref):
        for g in range(N_GROUPS):
            bk = bk_ref[g]
            for h in range(HEADS_PER_GROUP):
                col = g * HEADS_PER_GROUP + h
                acc = jnp.zeros((BLOCK, 2 * BLOCK), F32)
                for b in range(N_BUCKETS):
                    acc = jnp.where(bk == b, tab_ref[b, col], acc)
                o_ref[col] = acc

    return pl.pallas_call(
        body, name=name,
        in_specs=[pl.BlockSpec(memory_space=pltpu.VMEM), pl.BlockSpec(memory_space=pltpu.SMEM)],
        out_specs=pl.BlockSpec(memory_space=pltpu.VMEM),
        out_shape=jax.ShapeDtypeStruct((N_GROUPS * HEADS_PER_GROUP, BLOCK, 2 * BLOCK), F32),
    )(buckets, table)


def _bias_bwd(name, buckets, dbias):
    def body(bk_ref, db_ref, o_ref):
        row_id = lax.broadcasted_iota(jnp.int32, (N_BUCKETS, 128), 0)
        col_id = lax.broadcasted_iota(jnp.int32, (N_BUCKETS, 128), 1)
        acc = jnp.zeros((N_BUCKETS, 128), F32)
        for g in range(N_GROUPS):
            bk = bk_ref[g]
            for h in range(HEADS_PER_GROUP):
                col = g * HEADS_PER_GROUP + h
                db = db_ref[col]
                for b in range(N_BUCKETS):
                    part = jnp.sum(jnp.where(bk == b, db, 0.0), axis=0, keepdims=True)
                    tot = jnp.sum(part, axis=1, keepdims=True)
                    acc = jnp.where((row_id == b) & (col_id == col), tot, acc)
        o_ref[...] = acc

    return pl.pallas_call(
        body, name=name,
        in_specs=[pl.BlockSpec(memory_space=pltpu.VMEM), pl.BlockSpec(memory_space=pltpu.VMEM)],
        out_specs=pl.BlockSpec(memory_space=pltpu.VMEM),
        out_shape=jax.ShapeDtypeStruct((N_BUCKETS, 128), F32),
    )(buckets, dbias)


def _band_mask(n):
    qi = lax.broadcasted_iota(jnp.int32, (BLOCK, 2 * BLOCK), 0)
    kj = lax.broadcasted_iota(jnp.int32, (BLOCK, 2 * BLOCK), 1)
    return (kj >= qi) & (kj <= qi + BLOCK) & ((kj >= BLOCK) | (n > 0))


def _attn_fwd(name, q, k, v, bias):
    dil, nh, m, dh = q.shape
    nb = m // BLOCK

    def body(q_ref, kc_ref, kp_ref, vc_ref, vp_ref, b_ref, o_ref, lse_ref):
        mask = _band_mask(pl.program_id(1))
        for h in range(nh):
            k2 = jnp.concatenate([kp_ref[0, h], kc_ref[0, h]], axis=0)
            v2 = jnp.concatenate([vp_ref[0, h], vc_ref[0, h]], axis=0)
            s = _dot_nt(q_ref[0, h], k2) + b_ref[h]
            s = jnp.where(mask, s, NEG_INF)
            mx = jnp.max(s, axis=-1, keepdims=True)
            p = jnp.exp(s - mx)
            den = jnp.sum(p, axis=-1, keepdims=True)
            o_ref[0, h] = _dot_nn(p.astype(BF16), v2) / den
            lse_ref[0, h] = jnp.broadcast_to(mx + jnp.log(den), (BLOCK, dh))

    cur = pl.BlockSpec((1, nh, BLOCK, dh), lambda r, n: (r, 0, n, 0))
    prev = pl.BlockSpec((1, nh, BLOCK, dh), lambda r, n: (r, 0, jnp.maximum(n - 1, 0), 0))
    return pl.pallas_call(
        body, name=name, grid=(dil, nb),
        in_specs=[cur, cur, prev, cur, prev, pl.BlockSpec((nh, BLOCK, 2 * BLOCK), lambda r, n: (0, 0, 0))],
        out_specs=[cur, cur],
        out_shape=[jax.ShapeDtypeStruct(q.shape, F32), jax.ShapeDtypeStruct(q.shape, F32)],
        compiler_params=_cparams("parallel", "arbitrary"),
    )(q, k, k, v, v, bias)


def _attn_bwd(name, q, k, v, do, lse, cvec, bias):
    dil, nh, m, dh = q.shape
    nb = m // BLOCK

    def body(q_ref, kc_ref, kp_ref, vc_ref, vp_ref, do_ref, lse_ref, c_ref, b_ref,
             dq_ref, dk_ref, dv_ref, db_ref, kcar_ref, vcar_ref):
        r, n = pl.program_id(0), pl.program_id(1)
        valid = n < nb
        mask = _band_mask(n) & valid

        @pl.when((r == 0) & (n == 0))
        def _():
            kcar_ref[...] = jnp.zeros_like(kcar_ref)
            vcar_ref[...] = jnp.zeros_like(vcar_ref)
            db_ref[...] = jnp.zeros_like(db_ref)

        for h in range(nh):
            qh = q_ref[0, h]
            k2 = jnp.concatenate([kp_ref[0, h], kc_ref[0, h]], axis=0)
            v2 = jnp.concatenate([vp_ref[0, h], vc_ref[0, h]], axis=0)
            doh = do_ref[0, h].astype(BF16)
            s = _dot_nt(qh, k2) + b_ref[h]
            p = jnp.where(mask, jnp.exp(s - lse_ref[0, h][:, :1]), 0.0)
            dp = _dot_nt(doh, v2)
            ds = p * (dp + c_ref[0, h][:, :1])
            ds_b = ds.astype(BF16)

            @pl.when(valid)
            def _():
                dq_ref[0, h] = _dot_nn(ds_b, k2)

            dk2 = _dot_tn(ds_b, qh)
            dv2 = _dot_tn(p.astype(BF16), doh)
            dk_ref[0, h] = kcar_ref[h] + dk2[:BLOCK]
            dv_ref[0, h] = vcar_ref[h] + dv2[:BLOCK]
            kcar_ref[h] = dk2[BLOCK:]
            vcar_ref[h] = dv2[BLOCK:]
            db_ref[h] += ds

    def qmap(r, n):
        return (r, 0, jnp.minimum(n, nb - 1), 0)

    def pmap(r, n):
        return (r, 0, jnp.maximum(jnp.minimum(n, nb - 1) - 1, 0), 0)

    def kvout(r, n):
        return (r, 0, jnp.maximum(n - 1, 0), 0)

    blk = (1, nh, BLOCK, dh)
    cur, prev = pl.BlockSpec(blk, qmap), pl.BlockSpec(blk, pmap)
    bias_spec = pl.BlockSpec((nh, BLOCK, 2 * BLOCK), lambda r, n: (0, 0, 0))
    full = jax.ShapeDtypeStruct(q.shape, F32)
    return pl.pallas_call(
        body, name=name, grid=(dil, nb + 1),
        in_specs=[cur, cur, prev, cur, prev, cur, cur, cur, bias_spec],
        out_specs=[cur, pl.BlockSpec(blk, kvout), pl.BlockSpec(blk, kvout), bias_spec],
        out_shape=[full, full, full, jax.ShapeDtypeStruct(bias.shape, F32)],
        scratch_shapes=[pltpu.VMEM((nh, BLOCK, dh), F32), pltpu.VMEM((nh, BLOCK, dh), F32)],
        compiler_params=_cparams("arbitrary", "arbitrary"),
    )(q, k, k, v, v, do, lse, cvec, bias)


def _group_weights(lses):
    mx = jnp.maximum(jnp.maximum(lses[0], lses[1]), lses[2])
    es = [jnp.exp(l - mx) for l in lses]
    den = es[0] + es[1] + es[2]
    return [e / den for e in es]


def _combine_fwd(name, os_, lses, tm=512):
    nh, rows, dh = os_[0].shape
    tm = min(tm, rows)

    def body(o0, o1, o2, l0, l1, l2, out_ref):
        ws = _group_weights([l0[...], l1[...], l2[...]])
        out_ref[...] = ws[0] * o0[...] + ws[1] * o1[...] + ws[2] * o2[...]

    spec = pl.BlockSpec((nh, tm, dh), lambda i: (0, i, 0))
    return pl.pallas_call(
        body, name=name, grid=(rows // tm,), in_specs=[spec] * 6, out_specs=spec,
        out_shape=jax.ShapeDtypeStruct((nh, rows, dh), F32),
        compiler_params=_cparams("parallel"),
    )(*os_, *lses)


def _combine_bwd(name, do, oa, lses, tm=512):
    nh, rows, dh = do.shape
    tm = min(tm, rows)

    def body(do_ref, oa_ref, l0, l1, l2, d0, d1, d2, c0, c1, c2):
        ws = _group_weights([l0[...], l1[...], l2[...]])
        dov = do_ref[...]
        bar = jnp.sum(dov * oa_ref[...], axis=-1, keepdims=True)
        for w, d_ref, c_ref in zip(ws, (d0, d1, d2), (c0, c1, c2)):
            d_ref[...] = w * dov
            c_ref[...] = -w * bar

    spec = pl.BlockSpec((nh, tm, dh), lambda i: (0, i, 0))
    shape = jax.ShapeDtypeStruct((nh, rows, dh), F32)
    return pl.pallas_call(
        body, name=name, grid=(rows // tm,), in_specs=[spec] * 5, out_specs=[spec] * 6,
        out_shape=[shape] * 6, compiler_params=_cparams("parallel"),
    )(do, oa, *lses)


def _ssm_disc(a_re, a_im, log_dt, b_re, b_im):
    dt = jnp.exp(log_dt)
    mag = jnp.exp(a_re * dt)
    ab_re = mag * jnp.cos(a_im * dt)
    ab_im = mag * jnp.sin(a_im * dt)
    den = a_re * a_re + a_im * a_im
    xr = ab_re - 1.0
    coef_re = (xr * a_re + ab_im * a_im) / den
    coef_im = (ab_im * a_re - xr * a_im) / den
    bb_re = coef_re[None] * b_re - coef_im[None] * b_im
    bb_im = coef_re[None] * b_im + coef_im[None] * b_re
    return ab_re, ab_im, bb_re, bb_im


def _cpow2(re, im, times):
    for _ in range(times):
        re, im = re * re - im * im, 2.0 * re * im
    return re, im


def _ssm_params_fwd(name, a_re, a_im, log_dt, b_re, b_im):
    gn = jax.ShapeDtypeStruct(a_re.shape, F32)
    cgn = jax.ShapeDtypeStruct(b_re.shape, F32)

    def body(ar, ai, ld, br, bi, o_abr, o_abi, o_apr, o_api, o_bbr, o_bbi):
        ab_re, ab_im, bb_re, bb_im = _ssm_disc(ar[...], ai[...], ld[...], br[...], bi[...])
        o_abr[...] = ab_re
        o_abi[...] = ab_im
        pr, pi = _cpow2(ab_re, ab_im, int(math.log2(SCAN_STEPS)))
        o_apr[...] = pr
        o_api[...] = pi
        o_bbr[...] = bb_re
        o_bbi[...] = bb_im

    vm = pl.BlockSpec(memory_space=pltpu.VMEM)
    return pl.pallas_call(body, name=name, in_specs=[vm] * 5, out_specs=[vm] * 6,
                          out_shape=[gn, gn, gn, gn, cgn, cgn])(a_re, a_im, log_dt, b_re, b_im)


def _ssm_params_bwd(name, a_re, a_im, log_dt, b_re, b_im, d_ab_re, d_ab_im, d_bb_re, d_bb_im):
    gn = jax.ShapeDtypeStruct(a_re.shape, F32)
    cgn = jax.ShapeDtypeStruct(b_re.shape, F32)

    def body(ar, ai, ld, br, bi, g0, g1, g2, g3, o_ar, o_ai, o_ld, o_br, o_bi):
        _, vjp = jax.vjp(_ssm_disc, ar[...], ai[...], ld[...], br[...], bi[...])
        outs = vjp((g0[...], g1[...], g2[...], g3[...]))
        for o_ref, o in zip((o_ar, o_ai, o_ld, o_br, o_bi), outs):
            o_ref[...] = o

    vm = pl.BlockSpec(memory_space=pltpu.VMEM)
    return pl.pallas_call(body, name=name, in_specs=[vm] * 9, out_specs=[vm] * 5,
                          out_shape=[gn, gn, jax.ShapeDtypeStruct(log_dt.shape, F32), cgn, cgn],
                          )(a_re, a_im, log_dt, b_re, b_im, d_ab_re, d_ab_im, d_bb_re, d_bb_im)


def _scan_block(s_ref, carry_ref, tmp_ref, ab_ref, ap_ref, reverse, sprev=None):
    nl = SSM_LANES
    for lc in range(nl // SCAN_LANES):
        re_l = pl.ds(lc * SCAN_LANES, SCAN_LANES)
        im_l = pl.ds(nl + lc * SCAN_LANES, SCAN_LANES)
        are, aim = ab_ref[:, re_l], ab_ref[:, im_l]

        def rows_of(j):
            jj = SCAN_STEPS - 1 - j if reverse else j
            return pl.ds(pl.multiple_of(jj * SCAN_SUB, SCAN_SUB), SCAN_SUB)

        def pass1(j, st):
            sr, si = st
            rows = rows_of(j)
            nr = are * sr - aim * si + s_ref[rows, re_l]
            ni = are * si + aim * sr + s_ref[rows, im_l]
            s_ref[rows, re_l] = nr
            s_ref[rows, im_l] = ni
            return nr, ni

        zero = jnp.zeros((SCAN_SUB, SCAN_LANES), F32)
        er, ei = lax.fori_loop(0, SCAN_STEPS, pass1, (zero, zero), unroll=2)
        tmp_ref[0:SCAN_SUB, re_l] = er
        tmp_ref[0:SCAN_SUB, im_l] = ei
        apr, api = ap_ref[0:1, re_l], ap_ref[0:1, im_l]
        sr, si = carry_ref[0:1, re_l], carry_ref[0:1, im_l]
        for step in range(SCAN_SUB):
            c = SCAN_SUB - 1 - step if reverse else step
            tmp_ref[SCAN_SUB + c:SCAN_SUB + c + 1, re_l] = sr
            tmp_ref[SCAN_SUB + c:SCAN_SUB + c + 1, im_l] = si
            e_r, e_i = tmp_ref[c:c + 1, re_l], tmp_ref[c:c + 1, im_l]
            sr, si = apr * sr - api * si + e_r, apr * si + api * sr + e_i
        carry_ref[0:1, re_l] = sr
        carry_ref[0:1, im_l] = si
        cr = tmp_ref[SCAN_SUB:2 * SCAN_SUB, re_l]
        ci = tmp_ref[SCAN_SUB:2 * SCAN_SUB, im_l]

        if sprev is None:
            def pass2(j, st):
                pr, pi = st
                rows = rows_of(j)
                s_ref[rows, re_l] += pr * cr - pi * ci
                s_ref[rows, im_l] += pr * ci + pi * cr
                return pr * are - pi * aim, pr * aim + pi * are

            lax.fori_loop(0, SCAN_STEPS, pass2, (are, aim), unroll=2)
        else:
            st_ref, prev_ref, have_prev, dab_ref = sprev

            def corrected(j, pr, pi):
                rows = rows_of(j)
                gr = s_ref[rows, re_l] + pr * cr - pi * ci
                gi = s_ref[rows, im_l] + pr * ci + pi * cr
                s_ref[rows, re_l] = gr
                s_ref[rows, im_l] = gi
                return gr, gi

            def pass2(j, st):
                pr, pi, dr, di = st
                gr, gi = corrected(j, pr, pi)
                before = pl.ds(pl.multiple_of((SCAN_STEPS - 2 - j) * SCAN_SUB, SCAN_SUB), SCAN_SUB)
                qr, qi = st_ref[before, re_l], st_ref[before, im_l]
                return (pr * are - pi * aim, pr * aim + pi * are,
                        dr + gr * qr + gi * qi, di + gi * qr - gr * qi)

            pr, pi, dr, di = lax.fori_loop(0, SCAN_STEPS - 1, pass2, (are, aim, zero, zero), unroll=2)
            gr, gi = corrected(SCAN_STEPS - 1, pr, pi)
            last = pl.ds((SCAN_STEPS - 1) * SCAN_SUB, SCAN_SUB)
            sub = lax.broadcasted_iota(jnp.int32, (SCAN_SUB, SCAN_LANES), 0)
            pv_r = jnp.broadcast_to(prev_ref[SCAN_SUB - 1:SCAN_SUB, re_l], (SCAN_SUB, SCAN_LANES)) * have_prev
            pv_i = jnp.broadcast_to(prev_ref[SCAN_SUB - 1:SCAN_SUB, im_l], (SCAN_SUB, SCAN_LANES)) * have_prev
            qr = jnp.where(sub == 0, pv_r, pltpu.roll(st_ref[last, re_l], 1, 0))
            qi = jnp.where(sub == 0, pv_i, pltpu.roll(st_ref[last, im_l], 1, 0))
            dab_ref[:, re_l] += dr + gr * qr + gi * qi
            dab_ref[:, im_l] += di + gi * qr - gr * qi


def _ssm_fwd(name, u_perm, bb_mat, c_mat, ab_rows, ap_rows, d_skip):
    rows = u_perm.shape[0]
    nl2 = 2 * SSM_LANES

    def body(u_ref, bb_ref, c_ref, ab_ref, ap_ref, d_ref, y_ref, s_ref, carry_ref, tmp_ref):
        @pl.when(pl.program_id(0) == 0)
        def _():
            carry_ref[...] = jnp.zeros_like(carry_ref)

        uv = u_ref[...]
        s_ref[...] = _dot_nn(uv.astype(BF16), bb_ref[...])
        _scan_block(s_ref, carry_ref, tmp_ref, ab_ref, ap_ref, reverse=False)
        y_ref[...] = _dot_nn(s_ref[...].astype(BF16), c_ref[...]) + d_ref[...] * uv

    const = lambda shape: pl.BlockSpec(shape, lambda i: (0, 0))
    return pl.pallas_call(
        body, name=name, grid=(rows // SCAN_BLOCK,),
        in_specs=[pl.BlockSpec((SCAN_BLOCK, SSM_WIDTH), lambda i: (i, 0)), const((SSM_WIDTH, nl2)),
                  const((nl2, SSM_WIDTH)), const((SCAN_SUB, nl2)), const((SCAN_SUB, nl2)), const((1, SSM_WIDTH))],
        out_specs=[pl.BlockSpec((SCAN_BLOCK, SSM_WIDTH), lambda i: (i, 0)),
                   pl.BlockSpec((SCAN_BLOCK, nl2), lambda i: (i, 0))],
        out_shape=[jax.ShapeDtypeStruct((rows, SSM_WIDTH), F32), jax.ShapeDtypeStruct((rows, nl2), F32)],
        scratch_shapes=[pltpu.VMEM((SCAN_SUB, nl2), F32), pltpu.VMEM((2 * SCAN_SUB, nl2), F32)],
        compiler_params=_cparams("arbitrary"),
    )(u_perm, bb_mat, c_mat, ab_rows, ap_rows, d_skip)


def _ssm_bwd(name, dy_perm, u_perm, states, c_mat_t, bb_mat_t, abc_rows, apc_rows, d_skip):
    rows = u_perm.shape[0]
    nl2 = 2 * SSM_LANES
    nblk = rows // SCAN_BLOCK

    def body(dy_ref, u_ref, st_ref, prev_ref, ct_ref, bt_ref, ab_ref, ap_ref, d_ref,
             du_ref, g_ref, dab_ref, dd_ref, carry_ref, tmp_ref):
        i = pl.program_id(0)

        @pl.when(i == 0)
        def _():
            carry_ref[...] = jnp.zeros_like(carry_ref)
            dab_ref[...] = jnp.zeros_like(dab_ref)
            dd_ref[...] = jnp.zeros_like(dd_ref)

        dyv = dy_ref[...]
        g_ref[...] = _dot_nn(dyv.astype(BF16), ct_ref[...])
        have_prev = (i < nblk - 1).astype(F32)
        _scan_block(g_ref, carry_ref, tmp_ref, ab_ref, ap_ref, reverse=True,
                    sprev=(st_ref, prev_ref, have_prev, dab_ref))
        du_ref[...] = _dot_nn(g_ref[...].astype(BF16), bt_ref[...]) + d_ref[...] * dyv
        dd_ref[...] += jnp.sum(dyv * u_ref[...], axis=0, keepdims=True)

    const = lambda shape: pl.BlockSpec(shape, lambda i: (0, 0))
    blk = lambda cols: pl.BlockSpec((SCAN_BLOCK, cols), lambda i: (nblk - 1 - i, 0))
    per8 = SCAN_BLOCK // SCAN_SUB
    prev_spec = pl.BlockSpec((SCAN_SUB, nl2), lambda i: (jnp.maximum((nblk - 1 - i) * per8 - 1, 0), 0))
    return pl.pallas_call(
        body, name=name, grid=(nblk,),
        in_specs=[blk(SSM_WIDTH), blk(SSM_WIDTH), blk(nl2), prev_spec, const((SSM_WIDTH, nl2)),
                  const((nl2, SSM_WIDTH)), const((SCAN_SUB, nl2)), const((SCAN_SUB, nl2)), const((1, SSM_WIDTH))],
        out_specs=[blk(SSM_WIDTH), blk(nl2), const((SCAN_SUB, nl2)), const((1, SSM_WIDTH))],
        out_shape=[jax.ShapeDtypeStruct((rows, SSM_WIDTH), F32), jax.ShapeDtypeStruct((rows, nl2), F32),
                   jax.ShapeDtypeStruct((SCAN_SUB, nl2), F32), jax.ShapeDtypeStruct((1, SSM_WIDTH), F32)],
        scratch_shapes=[pltpu.VMEM((SCAN_SUB, nl2), F32), pltpu.VMEM((2 * SCAN_SUB, nl2), F32)],
        compiler_params=_cparams("arbitrary"),
    )(dy_perm, u_perm, states, states, c_mat_t, bb_mat_t, abc_rows, apc_rows, d_skip)


def _scan_order(a):
    rows, cols = a.shape
    return a.reshape(rows // SCAN_BLOCK, SCAN_SUB, SCAN_STEPS, cols).transpose(0, 2, 1, 3).reshape(rows, cols)


def _time_order(a):
    rows, cols = a.shape
    return a.reshape(rows // SCAN_BLOCK, SCAN_STEPS, SCAN_SUB, cols).transpose(0, 2, 1, 3).reshape(rows, cols)


def _adamw(name, w, m, v, gparts, tr):
    rows, cols = w.shape

    def body(w_ref, m_ref, v_ref, g_ref, og_ref, od_ref, om_ref, ov_ref):
        g = g_ref[0].astype(F32)
        for i in range(1, N_DEV):
            g = g + g_ref[i].astype(F32)
        m_new = B1 * m_ref[...] + (1.0 - B1) * g
        v_new = B2 * v_ref[...] + (1.0 - B2) * (g * g)
        m_hat = m_new / (1.0 - B1 ** STEP)
        v_hat = v_new / (1.0 - B2 ** STEP)
        og_ref[...] = g
        od_ref[...] = -LR * (m_hat / (jnp.sqrt(v_hat) + ADAM_EPS) + WD * w_ref[...])
        om_ref[...] = m_new
        ov_ref[...] = v_new

    spec = pl.BlockSpec((tr, cols), lambda i: (i, 0))
    shape = jax.ShapeDtypeStruct((rows, cols), F32)
    return pl.pallas_call(
        body, name=name, grid=(rows // tr,),
        in_specs=[spec, spec, spec, pl.BlockSpec((N_DEV, tr, cols), lambda i: (0, i, 0))],
        out_specs=[spec] * 4, out_shape=[shape] * 4,
        compiler_params=_cparams("parallel"),
    )(w, m, v, gparts)


_SHARDED = (
    ("ffn1_w_gate", True, (352, 1024)), ("ffn1_w_up", True, (352, 1024)), ("ffn1_w_down", False, (352, 1024)),
    ("w_in", True, (608, 1024)), ("ssm_w_glu", True, (128, 512)), ("w_attn_branch", True, (128, 256)),
    ("w_ssm_branch", True, (128, 512)), ("w_out", False, (128, 1024)),
    ("ffn2_w_gate", True, (352, 1024)), ("ffn2_w_up", True, (352, 1024)), ("ffn2_w_down", False, (352, 1024)),
)
_SMALL = ("ffn1_norm", "mix_norm", "gate_bias", "rel_bias_table", "ssm_a_re", "ssm_a_im", "ssm_log_dt",
          "ssm_b_re", "ssm_b_im", "ssm_c_re", "ssm_c_im", "ssm_d", "ffn2_norm", "final_norm")
_ORDER = ("ffn1_norm", "ffn1_w_gate", "ffn1_w_up", "ffn1_w_down", "mix_norm", "w_in", "gate_bias",
          "rel_bias_table", "ssm_a_re", "ssm_a_im", "ssm_log_dt", "ssm_b_re", "ssm_b_im", "ssm_c_re",
          "ssm_c_im", "ssm_d", "ssm_w_glu", "w_attn_branch", "w_ssm_branch", "w_out", "ffn2_norm",
          "ffn2_w_gate", "ffn2_w_up", "ffn2_w_down", "final_norm")


def _pack_rows(shape):
    return shape[0] * shape[1] // D_MODEL


def _pack_sharded(ws):
    parts = []
    for nm, tr, shape in _SHARDED:
        a = ws[nm].T if tr else ws[nm]
        parts.append(a.reshape(_pack_rows(shape), D_MODEL))
    return jnp.concatenate(parts, axis=0)


def _unpack_sharded(pack):
    out, r0 = {}, 0
    for nm, tr, shape in _SHARDED:
        n = _pack_rows(shape)
        a = pack[r0:r0 + n].reshape(shape)
        out[nm] = a.T if tr else a
        r0 += n
    return out


def _unpack_gathered(gath):
    out, r0 = {}, 0
    for nm, _, shape in _SHARDED:
        n = _pack_rows(shape)
        out[nm] = gath[:, r0:r0 + n].reshape(N_DEV * shape[0], shape[1])
        r0 += n
    return out


def _pack_grads(gs):
    parts = []
    for nm, _, shape in _SHARDED:
        parts.append(gs[nm].astype(BF16).reshape(N_DEV, _pack_rows(shape), D_MODEL))
    return jnp.concatenate(parts, axis=1)


def _pack_small(ws):
    flat = jnp.concatenate([ws[nm].reshape(-1) for nm in _SMALL])
    pad = (-flat.shape[0]) % (8 * 128)
    return jnp.pad(flat, (0, pad)).reshape(-1, 128)


def _unpack_small(pack, like):
    flat, out, p0 = pack.reshape(-1), {}, 0
    for nm in _SMALL:
        n = like[nm].size
        out[nm] = flat[p0:p0 + n].reshape(like[nm].shape)
        p0 += n
    return out


def _to_dilated(a, dil):
    rows = a.shape[0]
    return a.reshape(rows // dil, dil, HEADS_PER_GROUP, HEAD_DIM).transpose(1, 2, 0, 3)


def _dilated_to_heads(a):
    dil, nh, m, dh = a.shape
    return a.transpose(1, 2, 0, 3).reshape(nh, m * dil, dh)


def _heads_to_dilated(a, dil):
    nh, rows, dh = a.shape
    return a.reshape(nh, rows // dil, dil, dh).transpose(2, 0, 1, 3)


def _block_diag(blocks_gab):
    g, a, b = blocks_gab.shape
    eye = jnp.eye(g, dtype=blocks_gab.dtype)
    return (blocks_gab[:, :, None, :] * eye[:, None, :, None]).reshape(g * a, g * b)


def _diag_blocks(mat, a, b):
    g = mat.shape[0] // a
    m4 = mat.reshape(g, a, g, b)
    return jnp.stack([m4[i, :, i, :] for i in range(g)])


def _local_step(xs, target, wf, small):
    rows = xs.shape[0]
    gfull, gsmall = {}, {}

    x1, h1, gg1, uu1 = _ffn_fwd("ffn1_fwd", xs, small["ffn1_norm"], wf["ffn1_w_gate"], wf["ffn1_w_up"],
                                wf["ffn1_w_down"])
    hmix = _rms_fwd("mix_norm_fwd", x1, small["mix_norm"])
    w_in = wf["w_in"]
    w_qkv, w_u, w_g = w_in[:3 * ATTN_WIDTH], w_in[3 * ATTN_WIDTH:3 * ATTN_WIDTH + SSM_WIDTH], w_in[3 * ATTN_WIDTH + SSM_WIDTH:]
    qscale = jnp.concatenate([jnp.full((1, ATTN_WIDTH), HEAD_DIM ** -0.5, F32), jnp.ones((1, 2 * ATTN_WIDTH), F32)], axis=1)
    qkv, = _mm("in_qkv", [(hmix, w_qkv)], True, 3 * ATTN_WIDTH, [BF16],
               epilogue=lambda acc, sc: (acc * sc,), extras=[(qscale, 0)], tn=ATTN_WIDTH)
    u, = _mm("in_u", [(hmix, w_u)], True, SSM_WIDTH, [F32])
    gates, = _mm("in_gates", [(hmix, w_g)], True, 2 * D_MODEL, [F32],
                 epilogue=lambda acc, b: (_sigmoid(acc + b),), extras=[(small["gate_bias"], 0)])

    buckets = jnp.asarray(_bucket_table())
    bias = _bias_fwd("rel_bias_fwd", buckets, small["rel_bias_table"])
    qkv_d, o_h, lse_h = [], [], []
    for g, dil in enumerate(DILATIONS):
        cols = [qkv[:, s * ATTN_WIDTH + g * ATTN_OUT:s * ATTN_WIDTH + (g + 1) * ATTN_OUT] for s in range(3)]
        qd, kd, vd = [_to_dilated(c, dil) for c in cols]
        bias_g = bias[g * HEADS_PER_GROUP:(g + 1) * HEADS_PER_GROUP]
        o_g, lse_g = _attn_fwd(f"attn_fwd_{g}", qd, kd, vd, bias_g)
        qkv_d.append((qd, kd, vd, bias_g))
        o_h.append(_dilated_to_heads(o_g))
        lse_h.append(_dilated_to_heads(lse_g))
    oa_h = _combine_fwd("attn_combine_fwd", o_h, lse_h)
    oa = oa_h.transpose(1, 0, 2).reshape(rows, ATTN_OUT).astype(BF16)
    y_attn, = _mm("attn_branch", [(oa, wf["w_attn_branch"])], True, D_MODEL, [F32])

    ab_re, ab_im, ap_re, ap_im, bb_re, bb_im = _ssm_params_fwd(
        "ssm_params_fwd", small["ssm_a_re"], small["ssm_a_im"], small["ssm_log_dt"].reshape(SSM_GROUPS, 1),
        small["ssm_b_re"].transpose(2, 0, 1), small["ssm_b_im"].transpose(2, 0, 1))

    def lanes(re, im, sign=1.0):
        row = jnp.concatenate([re.reshape(1, SSM_LANES), sign * im.reshape(1, SSM_LANES)], axis=1)
        return jnp.broadcast_to(row, (SCAN_SUB, 2 * SSM_LANES))

    bb_mat = jnp.concatenate([_block_diag(bb_re.transpose(1, 0, 2)), _block_diag(bb_im.transpose(1, 0, 2))], axis=1)
    c_mat_t = jnp.concatenate([_block_diag(small["ssm_c_re"]), -_block_diag(small["ssm_c_im"])], axis=1)
    bb_mat, c_mat_t = bb_mat.astype(BF16), c_mat_t.astype(BF16)
    d_skip = small["ssm_d"].reshape(1, SSM_WIDTH)
    u_perm = _scan_order(u)
    y_perm, states = _ssm_fwd("ssm_fwd", u_perm, bb_mat, c_mat_t.T, lanes(ab_re, ab_im), lanes(ap_re, ap_im), d_skip)
    y_raw = _time_order(y_perm)

    def gelu_fn(yv):
        return (jax.nn.gelu(yv),)

    ygelu, = _ew("ssm_gelu", gelu_fn, [y_raw], [SSM_WIDTH], [BF16])
    glu, = _mm("ssm_glu", [(ygelu, wf["ssm_w_glu"])], True, 2 * SSM_WIDTH, [F32])
    ysg, = _ew("ssm_glu_act", lambda gv: (gv[:, :SSM_WIDTH] * _sigmoid(gv[:, SSM_WIDTH:]),), [glu], [SSM_WIDTH], [BF16])
    y_ssm, merged = _mm("ssm_branch_merge", [(ysg, wf["w_ssm_branch"])], True, D_MODEL, [F32, BF16],
                        epilogue=lambda acc, ga, gs, ya: (acc, ga * ya + gs * acc),
                        extras=[(gates, 0), (gates, D_MODEL), (y_attn, 0)])
    x2, = _mm("mix_out", [(merged, wf["w_out"])], False, D_MODEL, [F32],
              epilogue=lambda acc, res: (res + acc,), extras=[(x1, 0)])
    x3, h2, gg2, uu2 = _ffn_fwd("ffn2_fwd", x2, small["ffn2_norm"], wf["ffn2_w_gate"], wf["ffn2_w_up"],
                                wf["ffn2_w_down"])
    dx3, gsmall["final_norm"], loss = _final_loss("final_loss", x3, small["final_norm"].reshape(1, D_MODEL), target)

    dx2, dgg2, duu2, act2, gsmall["ffn2_norm"] = _ffn_bwd(
        "ffn2_bwd", dx3, x2, small["ffn2_norm"], gg2, uu2, wf["ffn2_w_gate"], wf["ffn2_w_up"], wf["ffn2_w_down"])
    gfull["ffn2_w_gate"] = _mm_tn("ffn2_dwg", dgg2, h2)
    gfull["ffn2_w_up"] = _mm_tn("ffn2_dwu", duu2, h2)
    gfull["ffn2_w_down"] = _mm_tn("ffn2_dwd", act2, dx3, scale=0.5)

    def merge_bwd(dm, ga, gs, ya, ys):
        return (dm * ga, dm * gs, dm * ya * ga * (1.0 - ga), dm * ys * gs * (1.0 - gs))

    dya, dys, dzga, dzgs = _mm("mix_out_bwd", [(dx2, wf["w_out"])], True, D_MODEL, [BF16] * 4, epilogue=merge_bwd,
                               extras=[(gates, 0), (gates, D_MODEL), (y_attn, 0), (y_ssm, 0)])
    gfull["w_out"] = _mm_tn("dw_out", merged, dx2)
    gsmall["gate_bias"] = jnp.concatenate([_colsum("dgate_bias_a", dzga), _colsum("dgate_bias_s", dzgs)], axis=1)

    gfull["w_ssm_branch"] = _mm_tn("dw_ssm_branch", dys, ysg)

    def glu_bwd(dysg, av, bv):
        sb = _sigmoid(bv)
        return (dysg * sb, dysg * av * sb * (1.0 - sb))

    dglu_a, dglu_b = _mm("ssm_branch_bwd", [(dys, wf["w_ssm_branch"])], False, SSM_WIDTH, [BF16, BF16],
                         epilogue=glu_bwd, extras=[(glu, 0), (glu, SSM_WIDTH)])
    w_glu = wf["ssm_w_glu"]
    gfull["ssm_w_glu"] = jnp.concatenate([_mm_tn("dw_glu_a", dglu_a, ygelu), _mm_tn("dw_glu_b", dglu_b, ygelu)], axis=0)

    def gelu_bwd(acc, yv):
        _, vjp = jax.vjp(jax.nn.gelu, yv)
        return (vjp(acc)[0],)

    dy_raw, = _mm("ssm_glu_bwd", [(dglu_a, w_glu[:SSM_WIDTH]), (dglu_b, w_glu[SSM_WIDTH:])], False, SSM_WIDTH, [F32],
                  epilogue=gelu_bwd, extras=[(y_raw, 0)])
    dy_perm = _scan_order(dy_raw)
    du_perm, g_states, dab_rows, gsmall_d = _ssm_bwd(
        "ssm_bwd", dy_perm, u_perm, states, c_mat_t, bb_mat.T, lanes(ab_re, ab_im, -1.0), lanes(ap_re, ap_im, -1.0), d_skip)
    du = _time_order(du_perm)
    gsmall["ssm_d"] = gsmall_d
    dbb_acc = _mm_tn("ssm_dbb", u_perm, g_states, bm=SSM_WIDTH)
    dc_acc = _mm_tn("ssm_dc", dy_perm, states, bm=SSM_WIDTH)
    dbb_re = _diag_blocks(dbb_acc[:, :SSM_LANES], SSM_GROUP, SSM_STATE).transpose(1, 0, 2)
    dbb_im = _diag_blocks(dbb_acc[:, SSM_LANES:], SSM_GROUP, SSM_STATE).transpose(1, 0, 2)
    gsmall["ssm_c_re"] = _diag_blocks(dc_acc[:, :SSM_LANES], SSM_GROUP, SSM_STATE)
    gsmall["ssm_c_im"] = -_diag_blocks(dc_acc[:, SSM_LANES:], SSM_GROUP, SSM_STATE)
    dab = _colsum("ssm_dab", dab_rows)
    d_ar, d_ai, d_ld, d_br, d_bi = _ssm_params_bwd(
        "ssm_params_bwd", small["ssm_a_re"], small["ssm_a_im"], small["ssm_log_dt"].reshape(SSM_GROUPS, 1),
        small["ssm_b_re"].transpose(2, 0, 1), small["ssm_b_im"].transpose(2, 0, 1),
        dab[:, :SSM_LANES].reshape(SSM_GROUPS, SSM_STATE), dab[:, SSM_LANES:].reshape(SSM_GROUPS, SSM_STATE),
        dbb_re, dbb_im)
    gsmall["ssm_a_re"], gsmall["ssm_a_im"], gsmall["ssm_log_dt"] = d_ar, d_ai, d_ld.reshape(SSM_GROUPS)
    gsmall["ssm_b_re"], gsmall["ssm_b_im"] = d_br.transpose(1, 2, 0), d_bi.transpose(1, 2, 0)

    gfull["w_attn_branch"] = _mm_tn("dw_attn_branch", dya, oa)
    doa, = _mm("attn_branch_bwd", [(dya, wf["w_attn_branch"])], False, ATTN_OUT, [F32])
    do_h = doa.reshape(rows, HEADS_PER_GROUP, HEAD_DIM).transpose(1, 0, 2)
    dc = _combine_bwd("attn_combine_bwd", do_h, oa_h, lse_h)
    dqkv_cols = [None] * 9
    dbias = []
    for g, dil in enumerate(DILATIONS):
        qd, kd, vd, bias_g = qkv_d[g]
        dq, dk, dv, db = _attn_bwd(f"attn_bwd_{g}", qd, kd, vd, _heads_to_dilated(dc[g], dil),
                                   _heads_to_dilated(lse_h[g], dil), _heads_to_dilated(dc[3 + g], dil), bias_g)
        dbias.append(db)
        for s, (arr, sc) in enumerate(((dq, HEAD_DIM ** -0.5), (dk, 1.0), (dv, 1.0))):
            tok = _dilated_to_heads(arr).transpose(1, 0, 2).reshape(rows, ATTN_OUT)
            dqkv_cols[3 * s + g] = (tok * sc).astype(BF16)
    dqkv = jnp.concatenate(dqkv_cols, axis=1)
    gsmall["rel_bias_table"] = _bias_bwd("rel_bias_bwd", buckets, jnp.concatenate(dbias, axis=0))[:, :N_GROUPS * HEADS_PER_GROUP]

    gfull["w_in"] = jnp.concatenate([
        _mm_tn("dw_in_qkv", dqkv, hmix), _mm_tn("dw_in_u", du, hmix),
        _mm_tn("dw_in_ga", dzga, hmix), _mm_tn("dw_in_gs", dzgs, hmix)], axis=0)
    dhmix, = _mm("in_bwd", [(dqkv, w_qkv), (du, w_u), (dzga, w_g[:D_MODEL]), (dzgs, w_g[D_MODEL:])], False, D_MODEL, [F32], tm=512)
    dx1, gsmall["mix_norm"] = _rms_bwd("mix_norm_bwd", dhmix, x1, small["mix_norm"], dx2)

    dx, dgg1, duu1, act1, gsmall["ffn1_norm"] = _ffn_bwd(
        "ffn1_bwd", dx1, xs, small["ffn1_norm"], gg1, uu1, wf["ffn1_w_gate"], wf["ffn1_w_up"], wf["ffn1_w_down"])
    gfull["ffn1_w_gate"] = _mm_tn("ffn1_dwg", dgg1, h1)
    gfull["ffn1_w_up"] = _mm_tn("ffn1_dwu", duu1, h1)
    gfull["ffn1_w_down"] = _mm_tn("ffn1_dwd", act1, dx1, scale=0.5)
    return loss[0, 0], dx, gfull, gsmall


def kernel(x, ffn1_norm, ffn1_w_gate, ffn1_w_up, ffn1_w_down, mix_norm, w_in, gate_bias, rel_bias_table, ssm_a_re, ssm_a_im, ssm_log_dt, ssm_b_re, ssm_b_im, ssm_c_re, ssm_c_im, ssm_d, ssm_w_glu, w_attn_branch, w_ssm_branch, w_out, ffn2_norm, ffn2_w_gate, ffn2_w_up, ffn2_w_down, final_norm, loss_target, m_ffn1_norm, m_ffn1_w_gate, m_ffn1_w_up, m_ffn1_w_down, m_mix_norm, m_w_in, m_gate_bias, m_rel_bias_table, m_ssm_a_re, m_ssm_a_im, m_ssm_log_dt, m_ssm_b_re, m_ssm_b_im, m_ssm_c_re, m_ssm_c_im, m_ssm_d, m_ssm_w_glu, m_w_attn_branch, m_w_ssm_branch, m_w_out, m_ffn2_norm, m_ffn2_w_gate, m_ffn2_w_up, m_ffn2_w_down, m_final_norm, v_ffn1_norm, v_ffn1_w_gate, v_ffn1_w_up, v_ffn1_w_down, v_mix_norm, v_w_in, v_gate_bias, v_rel_bias_table, v_ssm_a_re, v_ssm_a_im, v_ssm_log_dt, v_ssm_b_re, v_ssm_b_im, v_ssm_c_re, v_ssm_c_im, v_ssm_d, v_ssm_w_glu, v_w_attn_branch, v_w_ssm_branch, v_w_out, v_ffn2_norm, v_ffn2_w_gate, v_ffn2_w_up, v_ffn2_w_down, v_final_norm):
    given = dict(locals())
    shapes = {nm: given[nm].shape for nm in _ORDER}

    def strip(a):
        return a[0] if a.ndim >= 2 and a.shape[0] == 1 else a

    w = {nm: strip(given[nm]) for nm in _ORDER}
    m = {nm: strip(given["m_" + nm]) for nm in _ORDER}
    v = {nm: strip(given["v_" + nm]) for nm in _ORDER}
    for d in (w, m, v):
        d["rel_bias_table"] = d["rel_bias_table"].reshape(N_BUCKETS, N_GROUPS * HEADS_PER_GROUP)

    w_pack = _pack_sharded(w)
    gathered = _all_gather("gather_weights", w_pack.astype(BF16))
    wf = _unpack_gathered(gathered)
    small = {nm: w[nm] for nm in _SMALL}
    small_in = dict(small)
    for nm in ("ffn1_norm", "mix_norm", "ffn2_norm", "gate_bias"):
        small_in[nm] = small[nm].reshape(1, -1)

    loss, dx, gfull, gsmall = _local_step(x[0], loss_target[0], wf, small_in)

    recv = _all_to_all("scatter_grads", _pack_grads(gfull))
    tr = 64
    g_pack, d_pack, m_pack, v_pack = _adamw("adamw_sharded", w_pack, _pack_sharded(m), _pack_sharded(v), recv, tr)
    gs_pack = _pack_small({nm: gsmall[nm].reshape(small[nm].shape) for nm in _SMALL})
    gs_all = _all_gather("gather_small_grads", gs_pack)
    sm = _adamw("adamw_small", _pack_small(small), _pack_small({nm: m[nm] for nm in _SMALL}),
                _pack_small({nm: v[nm] for nm in _SMALL}), gs_all, gs_pack.shape[0])

    loss = lax.psum(loss, ("x", "y", "c"))
    outs = []
    for pack_big, pack_small in zip((g_pack, d_pack, m_pack, v_pack), sm):
        big = _unpack_sharded(pack_big)
        sml = _unpack_small(pack_small, small)
        outs.append([(big[nm] if nm in big else sml[nm]).reshape(shapes[nm]) for nm in _ORDER])
    return (loss, dx[None], *outs[0], *outs[1], *outs[2], *outs[3])
```

```python
import functools
import math

import numpy as np
import jax
import jax.numpy as jnp
from jax import lax
from jax.experimental import pallas as pl
from jax.experimental.pallas import tpu as pltpu

F32 = jnp.float32
BF16 = jnp.bfloat16

N_DEV = 8
D_MODEL = 1024
D_FF = 2816
HEAD_DIM = 64
HEADS_PER_GROUP = 4
DILATIONS = (1, 4, 16)
N_GROUPS = 3
ATTN_WIDTH = 768
ATTN_OUT = 256
BLOCK = 128
N_BUCKETS = 32
MAX_DISTANCE = 2048
NEG_INF = -1e30
SSM_WIDTH = 512
SSM_GROUPS = 32
SSM_GROUP = 16
SSM_STATE = 64
SSM_LANES = SSM_GROUPS * SSM_STATE
EPS = 1e-6
LR, B1, B2, ADAM_EPS, WD, STEP = 0.001, 0.9, 0.999, 1e-08, 0.01, 10

VMEM_LIMIT_BYTES = 56 * 1024 * 1024
SCAN_BLOCK = 256
SCAN_SUB = 8
SCAN_STEPS = SCAN_BLOCK // SCAN_SUB
SCAN_LANES = 512

MESH = pl.DeviceIdType.MESH


def _cparams(*sem):
    return pltpu.CompilerParams(dimension_semantics=sem, vmem_limit_bytes=VMEM_LIMIT_BYTES)


def _dot(a, b, dims):
    return lax.dot_general(a, b, (dims, ((), ())), preferred_element_type=F32)


def _dot_nn(a, b):
    return _dot(a, b, ((1,), (0,)))


def _dot_nt(a, b):
    return _dot(a, b, ((1,), (1,)))


def _dot_tn(a, b):
    return _dot(a, b, ((0,), (0,)))


def _sigmoid(x):
    return 1.0 / (1.0 + jnp.exp(-x))


def _all_gather(name, xs):
    rows, cols = xs.shape

    def body(x_ref, out_ref, send_sems, recv_sems, local_sem):
        x, y, c = lax.axis_index("x"), lax.axis_index("y"), lax.axis_index("c")
        me, sibling = (x, y, c), (x, y, 1 - c)
        chips = [(1 - x, y), (x, 1 - y), (1 - x, 1 - y)]

        def slot(px, py, pc):
            return out_ref.at[4 * px + 2 * py + pc]

        def copy(k, block, to, src=None):
            return pltpu.make_async_remote_copy(
                src_ref=slot(*block) if src is None else src, dst_ref=slot(*block),
                send_sem=send_sems.at[k], recv_sem=recv_sems.at[k], device_id=to, device_id_type=MESH)

        mine = pltpu.make_async_copy(x_ref, slot(*me), local_sem)
        mine.start()
        first = [copy(0, me, sibling, src=x_ref)]
        first += [copy(1 + j, me, (*chip, c), src=x_ref) for j, chip in enumerate(chips)]
        for cp in first:
            cp.start()
        passed = [copy(4 + j, (*chip, c), sibling) for j, chip in enumerate(chips)]
        for j, chip in enumerate(chips):
            copy(1 + j, (*chip, c), me).wait_recv()
            passed[j].start()
        copy(0, sibling, me).wait_recv()
        for j, chip in enumerate(chips):
            copy(4 + j, (*chip, 1 - c), me).wait_recv()
        for cp in first + passed:
            cp.wait_send()
        mine.wait()

    return pl.pallas_call(
        body, name=name,
        out_shape=jax.ShapeDtypeStruct((N_DEV, rows, cols), xs.dtype),
        in_specs=[pl.BlockSpec(memory_space=pl.ANY)],
        out_specs=pl.BlockSpec(memory_space=pl.ANY),
        scratch_shapes=[pltpu.SemaphoreType.DMA((7,)), pltpu.SemaphoreType.DMA((7,)), pltpu.SemaphoreType.DMA],
    )(xs)


def _all_to_all(name, xs):
    _, rows, cols = xs.shape

    def body(x_ref, out_ref, send_sems, recv_sems, local_sem):
        x, y, c = lax.axis_index("x"), lax.axis_index("y"), lax.axis_index("c")
        me = 4 * x + 2 * y + c
        mine = pltpu.make_async_copy(x_ref.at[me], out_ref.at[me], local_sem)
        mine.start()
        copies = []
        for k in range(1, N_DEV):
            px = 1 - x if k & 4 else x
            py = 1 - y if k & 2 else y
            pc = 1 - c if k & 1 else c
            cp = pltpu.make_async_remote_copy(
                src_ref=x_ref.at[4 * px + 2 * py + pc], dst_ref=out_ref.at[me],
                send_sem=send_sems.at[k - 1], recv_sem=recv_sems.at[k - 1],
                device_id=(px, py, pc), device_id_type=MESH)
            cp.start()
            copies.append(cp)
        for cp in copies:
            cp.wait()
        mine.wait()

    return pl.pallas_call(
        body, name=name,
        out_shape=jax.ShapeDtypeStruct(xs.shape, xs.dtype),
        in_specs=[pl.BlockSpec(memory_space=pl.ANY)],
        out_specs=pl.BlockSpec(memory_space=pl.ANY),
        scratch_shapes=[pltpu.SemaphoreType.DMA((7,)), pltpu.SemaphoreType.DMA((7,)), pltpu.SemaphoreType.DMA],
    )(xs)


_HBM_SPEC = pl.BlockSpec(memory_space=pltpu.HBM)
_SEM_SPEC = pl.BlockSpec(memory_space=pltpu.SEMAPHORE)
_EFFECT = pltpu.SideEffectType.DATAFLOW_SIDE_EFFECTING


def _peers(x, y, c):
    return [(1 - x if k & 4 else x, 1 - y if k & 2 else y, 1 - c if k & 1 else c) for k in range(1, N_DEV)]


def _exchange_copies(x_ref, land_ref, send_sems, recv_sems, gather):
    x, y, c = lax.axis_index("x"), lax.axis_index("y"), lax.axis_index("c")
    me = 4 * x + 2 * y + c
    copies = []
    for k, (px, py, pc) in enumerate(_peers(x, y, c)):
        src = x_ref if gather else x_ref.at[4 * px + 2 * py + pc]
        copies.append(pltpu.make_async_remote_copy(
            src_ref=src, dst_ref=land_ref.at[me], send_sem=send_sems.at[k], recv_sem=recv_sems.at[k],
            device_id=(px, py, pc), device_id_type=MESH))
    return me, copies


def _exchange_start(name, xs, gather):
    land_shape = (N_DEV, *xs.shape) if gather else xs.shape

    def body(x_ref, land_ref, send_sems, recv_sems, x_thru, land_thru, token):
        _, copies = _exchange_copies(x_ref, land_ref, send_sems, recv_sems, gather)
        for cp in copies:
            cp.start()
        token[...] = jnp.zeros_like(token)

    return pl.pallas_call(
        body, name=name,
        out_shape=(pltpu.SemaphoreType.DMA((N_DEV - 1,)), pltpu.SemaphoreType.DMA((N_DEV - 1,)),
                   pltpu.HBM(xs.shape, xs.dtype), pltpu.HBM(land_shape, xs.dtype), jax.ShapeDtypeStruct((8, 128), F32)),
        in_specs=(_HBM_SPEC, _HBM_SPEC),
        out_specs=(_SEM_SPEC, _SEM_SPEC, _HBM_SPEC, _HBM_SPEC, pl.BlockSpec(memory_space=pltpu.VMEM)),
        input_output_aliases={0: 2, 1: 3},
        compiler_params=pltpu.CompilerParams(has_side_effects=_EFFECT),
    )(pltpu.with_memory_space_constraint(xs, pltpu.HBM),
      pltpu.with_memory_space_constraint(lax.empty(land_shape, xs.dtype), pltpu.HBM))


def _exchange_wait(name, handle, after, gather):
    send_sems, recv_sems, x_thru, land_thru, _ = handle

    def body(x_ref, land_ref, send_sems, recv_sems, after_ref, x_dead, got_ref, local_sem):
        me, copies = _exchange_copies(x_ref, land_ref, send_sems, recv_sems, gather)
        for cp in copies:
            cp.wait_send()
            cp.wait_recv()
        own = pltpu.make_async_copy(x_ref if gather else x_ref.at[me], got_ref.at[me], local_sem)
        own.start()
        own.wait()

    return pl.pallas_call(
        body, name=name,
        out_shape=(pltpu.HBM(x_thru.shape, x_thru.dtype), pltpu.HBM(land_thru.shape, land_thru.dtype)),
        in_specs=(_HBM_SPEC, _HBM_SPEC, _SEM_SPEC, _SEM_SPEC, pl.BlockSpec(memory_space=pl.ANY)),
        out_specs=(_HBM_SPEC, _HBM_SPEC), input_output_aliases={0: 0, 1: 1},
        scratch_shapes=[pltpu.SemaphoreType.DMA],
        compiler_params=pltpu.CompilerParams(has_side_effects=_EFFECT),
    )(x_thru, land_thru, send_sems, recv_sems, after)[1]


def _mm(name, pairs, nt, n_cols, out_dtypes, epilogue=None, extras=(), tm=1024, tn=512):
    rows = pairs[0][0].shape[0]
    tm = min(tm, rows)
    tn = min(tn, n_cols)
    na, ne = len(pairs), len(extras)

    def body(*refs):
        a_refs, w_refs = refs[:na], refs[na:2 * na]
        e_refs, o_refs = refs[2 * na:2 * na + ne], refs[2 * na + ne:]
        acc = None
        for a_ref, w_ref in zip(a_refs, w_refs):
            a = a_ref[...].astype(BF16)
            w = w_ref[...].astype(BF16)
            p = _dot_nt(a, w) if nt else _dot_nn(a, w)
            acc = p if acc is None else acc + p
        outs = (acc,) if epilogue is None else epilogue(acc, *[e[...] for e in e_refs])
        for o_ref, o in zip(o_refs, outs):
            o_ref[...] = o.astype(o_ref.dtype)

    in_specs = [pl.BlockSpec((tm, a.shape[1]), lambda i, j: (i, 0)) for a, _ in pairs]
    for _, w in pairs:
        if nt:
            in_specs.append(pl.BlockSpec((tn, w.shape[1]), lambda i, j: (j, 0)))
        else:
            in_specs.append(pl.BlockSpec((w.shape[0], tn), lambda i, j: (0, j)))
    for e, col_off in extras:
        off = col_off // tn
        if e.shape[0] == 1:
            in_specs.append(pl.BlockSpec((1, tn), lambda i, j, off=off: (0, j + off)))
        else:
            in_specs.append(pl.BlockSpec((tm, tn), lambda i, j, off=off: (i, j + off)))
    out_specs = [pl.BlockSpec((tm, tn), lambda i, j: (i, j)) for _ in out_dtypes]
    outs = pl.pallas_call(
        body, name=name, grid=(rows // tm, n_cols // tn),
        in_specs=in_specs, out_specs=out_specs,
        out_shape=[jax.ShapeDtypeStruct((rows, n_cols), dt) for dt in out_dtypes],
        compiler_params=_cparams("parallel", "arbitrary"),
    )(*[a for a, _ in pairs], *[w for _, w in pairs], *[e for e, _ in extras])
    return outs


def _tn_rows(m):
    return max(b for b in range(128, min(m, 1408) + 1, 128) if m % b == 0)


def _mm_tn(name, a, b, scale=1.0, bm=None, tk=1024):
    rows, m = a.shape
    n = b.shape[1]
    bm = _tn_rows(m) if bm is None else bm
    tk = min(tk, rows)
    nk = rows // tk

    def body(a_ref, b_ref, o_ref):
        k = pl.program_id(1)

        @pl.when(k == 0)
        def _():
            o_ref[...] = jnp.zeros_like(o_ref)

        o_ref[...] += _dot_tn(a_ref[...].astype(BF16), b_ref[...].astype(BF16))
        if scale != 1.0:
            @pl.when(k == nk - 1)
            def _():
                o_ref[...] = o_ref[...] * scale

    return pl.pallas_call(
        body, name=name, grid=(m // bm, nk),
        in_specs=[pl.BlockSpec((tk, bm), lambda i, k: (k, i)), pl.BlockSpec((tk, n), lambda i, k: (k, 0))],
        out_specs=pl.BlockSpec((bm, n), lambda i, k: (i, 0)),
        out_shape=jax.ShapeDtypeStruct((m, n), F32),
        compiler_params=_cparams("parallel", "arbitrary"),
    )(a, b)


def _colsum(name, xs, tm=512):
    rows, cols = xs.shape
    tm = min(tm, rows)

    def body(x_ref, o_ref):
        @pl.when(pl.program_id(0) == 0)
        def _():
            o_ref[...] = jnp.zeros_like(o_ref)

        o_ref[...] += jnp.sum(x_ref[...].astype(F32), axis=0, keepdims=True)

    return pl.pallas_call(
        body, name=name, grid=(rows // tm,),
        in_specs=[pl.BlockSpec((tm, cols), lambda i: (i, 0))],
        out_specs=pl.BlockSpec((1, cols), lambda i: (0, 0)),
        out_shape=jax.ShapeDtypeStruct((1, cols), F32),
        compiler_params=_cparams("arbitrary"),
    )(xs)


def _ew(name, fn, ins, out_cols, out_dtypes, tm=512):
    rows = ins[0].shape[0]
    tm = min(tm, rows)
    ni = len(ins)

    def body(*refs):
        outs = fn(*[r[...] for r in refs[:ni]])
        for o_ref, o in zip(refs[ni:], outs):
            o_ref[...] = o.astype(o_ref.dtype)

    def spec(shape):
        if shape[0] == 1:
            return pl.BlockSpec((1, shape[1]), lambda i: (0, 0))
        return pl.BlockSpec((tm, shape[1]), lambda i: (i, 0))

    return pl.pallas_call(
        body, name=name, grid=(rows // tm,),
        in_specs=[spec(a.shape) for a in ins],
        out_specs=[pl.BlockSpec((tm, c), lambda i: (i, 0)) for c in out_cols],
        out_shape=[jax.ShapeDtypeStruct((rows, c), dt) for c, dt in zip(out_cols, out_dtypes)],
        compiler_params=_cparams("parallel"),
    )(*ins)


def _rms_parts(xv):
    r = lax.rsqrt(jnp.mean(xv * xv, axis=-1, keepdims=True) + EPS)
    return r, xv * r


def _rms_bwd_dx(dh, gain, r, xh):
    dxh = dh * gain
    return r * (dxh - xh * jnp.mean(dxh * xh, axis=-1, keepdims=True))


def _rms_fwd(name, xs, gain):
    def fn(xv, g):
        _, xh = _rms_parts(xv)
        return (xh * g,)

    return _ew(name, fn, [xs, gain], [xs.shape[1]], [BF16])[0]


def _rms_bwd(name, dh, xs, gain, dres, tm=512):
    rows, d = xs.shape
    tm = min(tm, rows)

    def body(dh_ref, x_ref, g_ref, dres_ref, dx_ref, dg_ref):
        r, xh = _rms_parts(x_ref[...])
        dhv = dh_ref[...]
        dx_ref[...] = dres_ref[...] + _rms_bwd_dx(dhv, g_ref[...], r, xh)

        @pl.when(pl.program_id(0) == 0)
        def _():
            dg_ref[...] = jnp.zeros_like(dg_ref)

        dg_ref[...] += jnp.sum(dhv * xh, axis=0, keepdims=True)

    tile = pl.BlockSpec((tm, d), lambda i: (i, 0))
    row = pl.BlockSpec((1, d), lambda i: (0, 0))
    return pl.pallas_call(
        body, name=name, grid=(rows // tm,),
        in_specs=[tile, tile, row, tile], out_specs=[tile, row],
        out_shape=[jax.ShapeDtypeStruct((rows, d), F32), jax.ShapeDtypeStruct((1, d), F32)],
        compiler_params=_cparams("arbitrary"),
    )(dh, xs, gain, dres)


def _ffn_fwd(name, xs, gain, wg_t, wu_t, wd, tm=512, tf=1408):
    rows, d = xs.shape
    f_all = wd.shape[0]
    tm = min(tm, rows)
    nf = f_all // tf

    def body(x_ref, g_ref, wg_ref, wu_ref, wd_ref, xo_ref, h_ref, gg_ref, uu_ref, acc_ref):
        f = pl.program_id(1)

        @pl.when(f == 0)
        def _():
            _, xh = _rms_parts(x_ref[...])
            h_ref[...] = (xh * g_ref[...]).astype(BF16)
            acc_ref[...] = jnp.zeros_like(acc_ref)

        h = h_ref[...]
        gg = _dot_nt(h, wg_ref[...])
        uu = _dot_nt(h, wu_ref[...])
        act = gg * _sigmoid(gg) * uu
        acc_ref[...] += _dot_nn(act.astype(BF16), wd_ref[...])
        gg_ref[...] = gg.astype(BF16)
        uu_ref[...] = uu.astype(BF16)

        @pl.when(f == nf - 1)
        def _():
            xo_ref[...] = x_ref[...] + 0.5 * acc_ref[...]

    tile = pl.BlockSpec((tm, d), lambda i, f: (i, 0))
    wspec = pl.BlockSpec((tf, d), lambda i, f: (f, 0))
    hid = pl.BlockSpec((tm, tf), lambda i, f: (i, f))
    return pl.pallas_call(
        body, name=name, grid=(rows // tm, nf),
        in_specs=[tile, pl.BlockSpec((1, d), lambda i, f: (0, 0)), wspec, wspec, wspec],
        out_specs=[tile, tile, hid, hid],
        out_shape=[jax.ShapeDtypeStruct((rows, d), F32), jax.ShapeDtypeStruct((rows, d), BF16),
                   jax.ShapeDtypeStruct((rows, f_all), BF16), jax.ShapeDtypeStruct((rows, f_all), BF16)],
        scratch_shapes=[pltpu.VMEM((tm, d), F32)],
        compiler_params=_cparams("parallel", "arbitrary"),
    )(xs, gain, wg_t, wu_t, wd)


def _ffn_bwd(name, dxo, xs, gain, gg_all, uu_all, wg_t, wu_t, wd, tm=256, tf=1408):
    rows, d = xs.shape
    f_all = wd.shape[0]
    tm = min(tm, rows)
    nf = f_all // tf

    def body(dxo_ref, x_ref, g_ref, gg_ref, uu_ref, wg_ref, wu_ref, wd_ref,
             dx_ref, dgg_ref, duu_ref, act_ref, dgain_ref, df_ref, acc_ref):
        i, f = pl.program_id(0), pl.program_id(1)

        @pl.when(f == 0)
        def _():
            df_ref[...] = (0.5 * dxo_ref[...]).astype(BF16)
            acc_ref[...] = jnp.zeros_like(acc_ref)

        gg = gg_ref[...].astype(F32)
        uu = uu_ref[...].astype(F32)
        sg = _sigmoid(gg)
        silu = gg * sg
        dact = _dot_nt(df_ref[...], wd_ref[...])
        duu = (dact * silu).astype(BF16)
        dgg = (dact * uu * (sg * (1.0 + gg * (1.0 - sg)))).astype(BF16)
        act_ref[...] = (silu * uu).astype(BF16)
        dgg_ref[...] = dgg
        duu_ref[...] = duu
        acc_ref[...] += _dot_nn(dgg, wg_ref[...]) + _dot_nn(duu, wu_ref[...])

        @pl.when(f == nf - 1)
        def _():
            r, xh = _rms_parts(x_ref[...])
            dh = acc_ref[...]
            dx_ref[...] = dxo_ref[...] + _rms_bwd_dx(dh, g_ref[...], r, xh)
            part = jnp.sum(dh * xh, axis=0, keepdims=True)

            @pl.when(i == 0)
            def _():
                dgain_ref[...] = part

            @pl.when(i > 0)
            def _():
                dgain_ref[...] += part

    tile = pl.BlockSpec((tm, d), lambda i, f: (i, 0))
    row = pl.BlockSpec((1, d), lambda i, f: (0, 0))
    wspec = pl.BlockSpec((tf, d), lambda i, f: (f, 0))
    hid = pl.BlockSpec((tm, tf), lambda i, f: (i, f))
    hid_shape = jax.ShapeDtypeStruct((rows, f_all), BF16)
    return pl.pallas_call(
        body, name=name, grid=(rows // tm, nf),
        in_specs=[tile, tile, row, hid, hid, wspec, wspec, wspec],
        out_specs=[tile, hid, hid, hid, row],
        out_shape=[jax.ShapeDtypeStruct((rows, d), F32), hid_shape, hid_shape, hid_shape,
                   jax.ShapeDtypeStruct((1, d), F32)],
        scratch_shapes=[pltpu.VMEM((tm, d), BF16), pltpu.VMEM((tm, d), F32)],
        compiler_params=_cparams("arbitrary", "arbitrary"),
    )(dxo, xs, gain, gg_all, uu_all, wg_t, wu_t, wd)


def _final_loss(name, xs, gain, target, tm=512):
    rows, d = xs.shape
    tm = min(tm, rows)

    def body(x_ref, g_ref, t_ref, dx_ref, dg_ref, loss_ref):
        r, xh = _rms_parts(x_ref[...])
        gain_v = g_ref[...]
        err = xh * gain_v - t_ref[...]
        dy = err * (1.0 / d)
        dx_ref[...] = _rms_bwd_dx(dy, gain_v, r, xh)

        @pl.when(pl.program_id(0) == 0)
        def _():
            dg_ref[...] = jnp.zeros_like(dg_ref)
            loss_ref[...] = jnp.zeros_like(loss_ref)

        dg_ref[...] += jnp.sum(dy * xh, axis=0, keepdims=True)
        per_tok = jnp.mean(err * err, axis=-1, keepdims=True)
        loss_ref[...] += 0.5 * jnp.sum(per_tok, axis=0, keepdims=True)

    tile = pl.BlockSpec((tm, d), lambda i: (i, 0))
    row = pl.BlockSpec((1, d), lambda i: (0, 0))
    return pl.pallas_call(
        body, name=name, grid=(rows // tm,),
        in_specs=[tile, row, tile],
        out_specs=[tile, row, pl.BlockSpec((1, 1), lambda i: (0, 0))],
        out_shape=[jax.ShapeDtypeStruct((rows, d), F32), jax.ShapeDtypeStruct((1, d), F32),
                   jax.ShapeDtypeStruct((1, 1), F32)],
        compiler_params=_cparams("arbitrary"),
    )(xs, gain, target)


def _bucket_table():
    out = []
    for dil in DILATIONS:
        qi = np.arange(BLOCK)[:, None]
        kj = np.arange(2 * BLOCK)[None, :]
        dist = (np.maximum(qi + BLOCK - kj, 0) * dil).astype(np.int32)
        max_exact = N_BUCKETS // 2
        dd = np.maximum(dist, 1).astype(np.float32)
        large = max_exact + (np.log(dd / np.float32(max_exact)) / np.float32(math.log(MAX_DISTANCE / max_exact))
                             * np.float32(N_BUCKETS - max_exact)).astype(np.int32)
        large = np.minimum(large, N_BUCKETS - 1)
        out.append(np.where(dist < max_exact, dist, large).astype(np.int32))
    return np.stack(out)


def _bias_fwd(name, buckets, table):
    def body(bk_ref, tab_ref, o_ref):
        for g in range(N_GROUPS):
            bk = bk_ref[g]
            for h in range(HEADS_PER_GROUP):
                col = g * HEADS_PER_GROUP + h
                acc = jnp.zeros((BLOCK, 2 * BLOCK), F32)
                for b in range(N_BUCKETS):
                    acc = jnp.where(bk == b, tab_ref[b, col], acc)
                o_ref[col] = acc

    return pl.pallas_call(
        body, name=name,
        in_specs=[pl.BlockSpec(memory_space=pltpu.VMEM), pl.BlockSpec(memory_space=pltpu.SMEM)],
        out_specs=pl.BlockSpec(memory_space=pltpu.VMEM),
        out_shape=jax.ShapeDtypeStruct((N_GROUPS * HEADS_PER_GROUP, BLOCK, 2 * BLOCK), F32),
    )(buckets, table)


def _bias_bwd(name, buckets, dbias):
    def body(bk_ref, db_ref, o_ref):
        row_id = lax.broadcasted_iota(jnp.int32, (N_BUCKETS, 128), 0)
        col_id = lax.broadcasted_iota(jnp.int32, (N_BUCKETS, 128), 1)
        acc = jnp.zeros((N_BUCKETS, 128), F32)
        for g in range(N_GROUPS):
            bk = bk_ref[g]
            for h in range(HEADS_PER_GROUP):
                col = g * HEADS_PER_GROUP + h
                db = db_ref[col]
                for b in range(N_BUCKETS):
                    part = jnp.sum(jnp.where(bk == b, db, 0.0), axis=0, keepdims=True)
                    tot = jnp.sum(part, axis=1, keepdims=True)
                    acc = jnp.where((row_id == b) & (col_id == col), tot, acc)
        o_ref[...] = acc

    return pl.pallas_call(
        body, name=name,
        in_specs=[pl.BlockSpec(memory_space=pltpu.VMEM), pl.BlockSpec(memory_space=pltpu.VMEM)],
        out_specs=pl.BlockSpec(memory_space=pltpu.VMEM),
        out_shape=jax.ShapeDtypeStruct((N_BUCKETS, 128), F32),
    )(buckets, dbias)


def _band_mask(n):
    qi = lax.broadcasted_iota(jnp.int32, (BLOCK, 2 * BLOCK), 0)
    kj = lax.broadcasted_iota(jnp.int32, (BLOCK, 2 * BLOCK), 1)
    return (kj >= qi) & (kj <= qi + BLOCK) & ((kj >= BLOCK) | (n > 0))


def _attn_fwd(name, q, k, v, bias):
    dil, nh, m, dh = q.shape
    nb = m // BLOCK

    def body(q_ref, kc_ref, kp_ref, vc_ref, vp_ref, b_ref, o_ref, lse_ref):
        mask = _band_mask(pl.program_id(1))
        for h in range(nh):
            k2 = jnp.concatenate([kp_ref[0, h], kc_ref[0, h]], axis=0)
            v2 = jnp.concatenate([vp_ref[0, h], vc_ref[0, h]], axis=0)
            s = _dot_nt(q_ref[0, h], k2) + b_ref[h]
            s = jnp.where(mask, s, NEG_INF)
            mx = jnp.max(s, axis=-1, keepdims=True)
            p = jnp.exp(s - mx)
            den = jnp.sum(p, axis=-1, keepdims=True)
            o_ref[0, h] = _dot_nn(p.astype(BF16), v2) / den
            lse_ref[0, h] = jnp.broadcast_to(mx + jnp.log(den), (BLOCK, dh))

    cur = pl.BlockSpec((1, nh, BLOCK, dh), lambda r, n: (r, 0, n, 0))
    prev = pl.BlockSpec((1, nh, BLOCK, dh), lambda r, n: (r, 0, jnp.maximum(n - 1, 0), 0))
    return pl.pallas_call(
        body, name=name, grid=(dil, nb),
        in_specs=[cur, cur, prev, cur, prev, pl.BlockSpec((nh, BLOCK, 2 * BLOCK), lambda r, n: (0, 0, 0))],
        out_specs=[cur, cur],
        out_shape=[jax.ShapeDtypeStruct(q.shape, F32), jax.ShapeDtypeStruct(q.shape, F32)],
        compiler_params=_cparams("parallel", "arbitrary"),
    )(q, k, k, v, v, bias)


def _attn_bwd(name, q, k, v, do, lse, cvec, bias):
    dil, nh, m, dh = q.shape
    nb = m // BLOCK

    def body(q_ref, kc_ref, kp_ref, vc_ref, vp_ref, do_ref, lse_ref, c_ref, b_ref,
             dq_ref, dk_ref, dv_ref, db_ref, kcar_ref, vcar_ref):
        r, n = pl.program_id(0), pl.program_id(1)
        valid = n < nb
        mask = _band_mask(n) & valid

        @pl.when((r == 0) & (n == 0))
        def _():
            kcar_ref[...] = jnp.zeros_like(kcar_ref)
            vcar_ref[...] = jnp.zeros_like(vcar_ref)
            db_ref[...] = jnp.zeros_like(db_ref)

        for h in range(nh):
            qh = q_ref[0, h]
            k2 = jnp.concatenate([kp_ref[0, h], kc_ref[0, h]], axis=0)
            v2 = jnp.concatenate([vp_ref[0, h], vc_ref[0, h]], axis=0)
            doh = do_ref[0, h].astype(BF16)
            s = _dot_nt(qh, k2) + b_ref[h]
            p = jnp.where(mask, jnp.exp(s - lse_ref[0, h][:, :1]), 0.0)
            dp = _dot_nt(doh, v2)
            ds = p * (dp + c_ref[0, h][:, :1])
            ds_b = ds.astype(BF16)

            @pl.when(valid)
            def _():
                dq_ref[0, h] = _dot_nn(ds_b, k2)

            dk2 = _dot_tn(ds_b, qh)
            dv2 = _dot_tn(p.astype(BF16), doh)
            dk_ref[0, h] = kcar_ref[h] + dk2[:BLOCK]
            dv_ref[0, h] = vcar_ref[h] + dv2[:BLOCK]
            kcar_ref[h] = dk2[BLOCK:]
            vcar_ref[h] = dv2[BLOCK:]
            db_ref[h] += ds

    def qmap(r, n):
        return (r, 0, jnp.minimum(n, nb - 1), 0)

    def pmap(r, n):
        return (r, 0, jnp.maximum(jnp.minimum(n, nb - 1) - 1, 0), 0)

    def kvout(r, n):
        return (r, 0, jnp.maximum(n - 1, 0), 0)

    blk = (1, nh, BLOCK, dh)
    cur, prev = pl.BlockSpec(blk, qmap), pl.BlockSpec(blk, pmap)
    bias_spec = pl.BlockSpec((nh, BLOCK, 2 * BLOCK), lambda r, n: (0, 0, 0))
    full = jax.ShapeDtypeStruct(q.shape, F32)
    return pl.pallas_call(
        body, name=name, grid=(dil, nb + 1),
        in_specs=[cur, cur, prev, cur, prev, cur, cur, cur, bias_spec],
        out_specs=[cur, pl.BlockSpec(blk, kvout), pl.BlockSpec(blk, kvout), bias_spec],
        out_shape=[full, full, full, jax.ShapeDtypeStruct(bias.shape, F32)],
        scratch_shapes=[pltpu.VMEM((nh, BLOCK, dh), F32), pltpu.VMEM((nh, BLOCK, dh), F32)],
        compiler_params=_cparams("arbitrary", "arbitrary"),
    )(q, k, k, v, v, do, lse, cvec, bias)


def _group_weights(lses):
    mx = jnp.maximum(jnp.maximum(lses[0], lses[1]), lses[2])
    es = [jnp.exp(l - mx) for l in lses]
    den = es[0] + es[1] + es[2]
    return [e / den for e in es]


def _combine_fwd(name, os_, lses, tm=512):
    nh, rows, dh = os_[0].shape
    tm = min(tm, rows)

    def body(o0, o1, o2, l0, l1, l2, out_ref):
        ws = _group_weights([l0[...], l1[...], l2[...]])
        out_ref[...] = ws[0] * o0[...] + ws[1] * o1[...] + ws[2] * o2[...]

    spec = pl.BlockSpec((nh, tm, dh), lambda i: (0, i, 0))
    return pl.pallas_call(
        body, name=name, grid=(rows // tm,), in_specs=[spec] * 6, out_specs=spec,
        out_shape=jax.ShapeDtypeStruct((nh, rows, dh), F32),
        compiler_params=_cparams("parallel"),
    )(*os_, *lses)


def _combine_bwd(name, do, oa, lses, tm=512):
    nh, rows, dh = do.shape
    tm = min(tm, rows)

    def body(do_ref, oa_ref, l0, l1, l2, d0, d1, d2, c0, c1, c2):
        ws = _group_weights([l0[...], l1[...], l2[...]])
        dov = do_ref[...]
        bar = jnp.sum(dov * oa_ref[...], axis=-1, keepdims=True)
        for w, d_ref, c_ref in zip(ws, (d0, d1, d2), (c0, c1, c2)):
            d_ref[...] = w * dov
            c_ref[...] = -w * bar

    spec = pl.BlockSpec((nh, tm, dh), lambda i: (0, i, 0))
    shape = jax.ShapeDtypeStruct((nh, rows, dh), F32)
    return pl.pallas_call(
        body, name=name, grid=(rows // tm,), in_specs=[spec] * 5, out_specs=[spec] * 6,
        out_shape=[shape] * 6, compiler_params=_cparams("parallel"),
    )(do, oa, *lses)


def _ssm_disc(a_re, a_im, log_dt, b_re, b_im):
    dt = jnp.exp(log_dt)
    mag = jnp.exp(a_re * dt)
    ab_re = mag * jnp.cos(a_im * dt)
    ab_im = mag * jnp.sin(a_im * dt)
    den = a_re * a_re + a_im * a_im
    xr = ab_re - 1.0
    coef_re = (xr * a_re + ab_im * a_im) / den
    coef_im = (ab_im * a_re - xr * a_im) / den
    bb_re = coef_re[None] * b_re - coef_im[None] * b_im
    bb_im = coef_re[None] * b_im + coef_im[None] * b_re
    return ab_re, ab_im, bb_re, bb_im


def _cpow2(re, im, times):
    for _ in range(times):
        re, im = re * re - im * im, 2.0 * re * im
    return re, im


def _ssm_params_fwd(name, a_re, a_im, log_dt, b_re, b_im):
    gn = jax.ShapeDtypeStruct(a_re.shape, F32)
    cgn = jax.ShapeDtypeStruct(b_re.shape, F32)

    def body(ar, ai, ld, br, bi, o_abr, o_abi, o_apr, o_api, o_bbr, o_bbi):
        ab_re, ab_im, bb_re, bb_im = _ssm_disc(ar[...], ai[...], ld[...], br[...], bi[...])
        o_abr[...] = ab_re
        o_abi[...] = ab_im
        pr, pi = _cpow2(ab_re, ab_im, int(math.log2(SCAN_STEPS)))
        o_apr[...] = pr
        o_api[...] = pi
        o_bbr[...] = bb_re
        o_bbi[...] = bb_im

    vm = pl.BlockSpec(memory_space=pltpu.VMEM)
    return pl.pallas_call(body, name=name, in_specs=[vm] * 5, out_specs=[vm] * 6,
                          out_shape=[gn, gn, gn, gn, cgn, cgn])(a_re, a_im, log_dt, b_re, b_im)


def _ssm_params_bwd(name, a_re, a_im, log_dt, b_re, b_im, d_ab_re, d_ab_im, d_bb_re, d_bb_im):
    gn = jax.ShapeDtypeStruct(a_re.shape, F32)
    cgn = jax.ShapeDtypeStruct(b_re.shape, F32)

    def body(ar, ai, ld, br, bi, g0, g1, g2, g3, o_ar, o_ai, o_ld, o_br, o_bi):
        _, vjp = jax.vjp(_ssm_disc, ar[...], ai[...], ld[...], br[...], bi[...])
        outs = vjp((g0[...], g1[...], g2[...], g3[...]))
        for o_ref, o in zip((o_ar, o_ai, o_ld, o_br, o_bi), outs):
            o_ref[...] = o

    vm = pl.BlockSpec(memory_space=pltpu.VMEM)
    return pl.pallas_call(body, name=name, in_specs=[vm] * 9, out_specs=[vm] * 5,
                          out_shape=[gn, gn, jax.ShapeDtypeStruct(log_dt.shape, F32), cgn, cgn],
                          )(a_re, a_im, log_dt, b_re, b_im, d_ab_re, d_ab_im, d_bb_re, d_bb_im)


def _scan_block(s_ref, carry_ref, tmp_ref, ab_ref, ap_ref, reverse, sprev=None):
    nl = SSM_LANES
    for lc in range(nl // SCAN_LANES):
        re_l = pl.ds(lc * SCAN_LANES, SCAN_LANES)
        im_l = pl.ds(nl + lc * SCAN_LANES, SCAN_LANES)
        are, aim = ab_ref[:, re_l], ab_ref[:, im_l]

        def rows_of(j):
            jj = SCAN_STEPS - 1 - j if reverse else j
            return pl.ds(pl.multiple_of(jj * SCAN_SUB, SCAN_SUB), SCAN_SUB)

        def pass1(j, st):
            sr, si = st
            rows = rows_of(j)
            nr = are * sr - aim * si + s_ref[rows, re_l]
            ni = are * si + aim * sr + s_ref[rows, im_l]
            s_ref[rows, re_l] = nr
            s_ref[rows, im_l] = ni
            return nr, ni

        zero = jnp.zeros((SCAN_SUB, SCAN_LANES), F32)
        er, ei = lax.fori_loop(0, SCAN_STEPS, pass1, (zero, zero), unroll=2)
        tmp_ref[0:SCAN_SUB, re_l] = er
        tmp_ref[0:SCAN_SUB, im_l] = ei
        apr, api = ap_ref[0:1, re_l], ap_ref[0:1, im_l]
        sr, si = carry_ref[0:1, re_l], carry_ref[0:1, im_l]
        for step in range(SCAN_SUB):
            c = SCAN_SUB - 1 - step if reverse else step
            tmp_ref[SCAN_SUB + c:SCAN_SUB + c + 1, re_l] = sr
            tmp_ref[SCAN_SUB + c:SCAN_SUB + c + 1, im_l] = si
            e_r, e_i = tmp_ref[c:c + 1, re_l], tmp_ref[c:c + 1, im_l]
            sr, si = apr * sr - api * si + e_r, apr * si + api * sr + e_i
        carry_ref[0:1, re_l] = sr
        carry_ref[0:1, im_l] = si
        cr = tmp_ref[SCAN_SUB:2 * SCAN_SUB, re_l]
        ci = tmp_ref[SCAN_SUB:2 * SCAN_SUB, im_l]

        if sprev is None:
            def pass2(j, st):
                pr, pi = st
                rows = rows_of(j)
                s_ref[rows, re_l] += pr * cr - pi * ci
                s_ref[rows, im_l] += pr * ci + pi * cr
                return pr * are - pi * aim, pr * aim + pi * are

            lax.fori_loop(0, SCAN_STEPS, pass2, (are, aim), unroll=2)
        else:
            st_ref, prev_ref, have_prev, dab_ref = sprev

            def corrected(j, pr, pi):
                rows = rows_of(j)
                gr = s_ref[rows, re_l] + pr * cr - pi * ci
                gi = s_ref[rows, im_l] + pr * ci + pi * cr
                s_ref[rows, re_l] = gr
                s_ref[rows, im_l] = gi
                return gr, gi

            def pass2(j, st):
                pr, pi, dr, di = st
                gr, gi = corrected(j, pr, pi)
                before = pl.ds(pl.multiple_of((SCAN_STEPS - 2 - j) * SCAN_SUB, SCAN_SUB), SCAN_SUB)
                qr, qi = st_ref[before, re_l], st_ref[before, im_l]
                return (pr * are - pi * aim, pr * aim + pi * are,
                        dr + gr * qr + gi * qi, di + gi * qr - gr * qi)

            pr, pi, dr, di = lax.fori_loop(0, SCAN_STEPS - 1, pass2, (are, aim, zero, zero), unroll=2)
            gr, gi = corrected(SCAN_STEPS - 1, pr, pi)
            last = pl.ds((SCAN_STEPS - 1) * SCAN_SUB, SCAN_SUB)
            sub = lax.broadcasted_iota(jnp.int32, (SCAN_SUB, SCAN_LANES), 0)
            pv_r = jnp.broadcast_to(prev_ref[SCAN_SUB - 1:SCAN_SUB, re_l], (SCAN_SUB, SCAN_LANES)) * have_prev
            pv_i = jnp.broadcast_to(prev_ref[SCAN_SUB - 1:SCAN_SUB, im_l], (SCAN_SUB, SCAN_LANES)) * have_prev
            qr = jnp.where(sub == 0, pv_r, pltpu.roll(st_ref[last, re_l], 1, 0))
            qi = jnp.where(sub == 0, pv_i, pltpu.roll(st_ref[last, im_l], 1, 0))
            dab_ref[:, re_l] += dr + gr * qr + gi * qi
            dab_ref[:, im_l] += di + gi * qr - gr * qi


def _ssm_fwd(name, u_perm, bb_mat, c_mat, ab_rows, ap_rows, d_skip):
    rows = u_perm.shape[0]
    nl2 = 2 * SSM_LANES

    def body(u_ref, bb_ref, c_ref, ab_ref, ap_ref, d_ref, y_ref, s_ref, carry_ref, tmp_ref):
        @pl.when(pl.program_id(0) == 0)
        def _():
            carry_ref[...] = jnp.zeros_like(carry_ref)

        uv = u_ref[...]
        s_ref[...] = _dot_nn(uv.astype(BF16), bb_ref[...])
        _scan_block(s_ref, carry_ref, tmp_ref, ab_ref, ap_ref, reverse=False)
        y_ref[...] = _dot_nn(s_ref[...].astype(BF16), c_ref[...]) + d_ref[...] * uv

    const = lambda shape: pl.BlockSpec(shape, lambda i: (0, 0))
    return pl.pallas_call(
        body, name=name, grid=(rows // SCAN_BLOCK,),
        in_specs=[pl.BlockSpec((SCAN_BLOCK, SSM_WIDTH), lambda i: (i, 0)), const((SSM_WIDTH, nl2)),
                  const((nl2, SSM_WIDTH)), const((SCAN_SUB, nl2)), const((SCAN_SUB, nl2)), const((1, SSM_WIDTH))],
        out_specs=[pl.BlockSpec((SCAN_BLOCK, SSM_WIDTH), lambda i: (i, 0)),
                   pl.BlockSpec((SCAN_BLOCK, nl2), lambda i: (i, 0))],
        out_shape=[jax.ShapeDtypeStruct((rows, SSM_WIDTH), F32), jax.ShapeDtypeStruct((rows, nl2), F32)],
        scratch_shapes=[pltpu.VMEM((SCAN_SUB, nl2), F32), pltpu.VMEM((2 * SCAN_SUB, nl2), F32)],
        compiler_params=_cparams("arbitrary"),
    )(u_perm, bb_mat, c_mat, ab_rows, ap_rows, d_skip)


def _ssm_bwd(name, dy_perm, u_perm, states, c_mat_t, bb_mat_t, abc_rows, apc_rows, d_skip):
    rows = u_perm.shape[0]
    nl2 = 2 * SSM_LANES
    nblk = rows // SCAN_BLOCK

    def body(dy_ref, u_ref, st_ref, prev_ref, ct_ref, bt_ref, ab_ref, ap_ref, d_ref,
             du_ref, g_ref, dab_ref, dd_ref, carry_ref, tmp_ref):
        i = pl.program_id(0)

        @pl.when(i == 0)
        def _():
            carry_ref[...] = jnp.zeros_like(carry_ref)
            dab_ref[...] = jnp.zeros_like(dab_ref)
            dd_ref[...] = jnp.zeros_like(dd_ref)

        dyv = dy_ref[...]
        g_ref[...] = _dot_nn(dyv.astype(BF16), ct_ref[...])
        have_prev = (i < nblk - 1).astype(F32)
        _scan_block(g_ref, carry_ref, tmp_ref, ab_ref, ap_ref, reverse=True,
                    sprev=(st_ref, prev_ref, have_prev, dab_ref))
        du_ref[...] = _dot_nn(g_ref[...].astype(BF16), bt_ref[...]) + d_ref[...] * dyv
        dd_ref[...] += jnp.sum(dyv * u_ref[...], axis=0, keepdims=True)

    const = lambda shape: pl.BlockSpec(shape, lambda i: (0, 0))
    blk = lambda cols: pl.BlockSpec((SCAN_BLOCK, cols), lambda i: (nblk - 1 - i, 0))
    per8 = SCAN_BLOCK // SCAN_SUB
    prev_spec = pl.BlockSpec((SCAN_SUB, nl2), lambda i: (jnp.maximum((nblk - 1 - i) * per8 - 1, 0), 0))
    return pl.pallas_call(
        body, name=name, grid=(nblk,),
        in_specs=[blk(SSM_WIDTH), blk(SSM_WIDTH), blk(nl2), prev_spec, const((SSM_WIDTH, nl2)),
                  const((nl2, SSM_WIDTH)), const((SCAN_SUB, nl2)), const((SCAN_SUB, nl2)), const((1, SSM_WIDTH))],
        out_specs=[blk(SSM_WIDTH), blk(nl2), const((SCAN_SUB, nl2)), const((1, SSM_WIDTH))],
        out_shape=[jax.ShapeDtypeStruct((rows, SSM_WIDTH), F32), jax.ShapeDtypeStruct((rows, nl2), F32),
                   jax.ShapeDtypeStruct((SCAN_SUB, nl2), F32), jax.ShapeDtypeStruct((1, SSM_WIDTH), F32)],
        scratch_shapes=[pltpu.VMEM((SCAN_SUB, nl2), F32), pltpu.VMEM((2 * SCAN_SUB, nl2), F32)],
        compiler_params=_cparams("arbitrary"),
    )(dy_perm, u_perm, states, states, c_mat_t, bb_mat_t, abc_rows, apc_rows, d_skip)


def _scan_order(a):
    rows, cols = a.shape
    return a.reshape(rows // SCAN_BLOCK, SCAN_SUB, SCAN_STEPS, cols).transpose(0, 2, 1, 3).reshape(rows, cols)


def _time_order(a):
    rows, cols = a.shape
    return a.reshape(rows // SCAN_BLOCK, SCAN_STEPS, SCAN_SUB, cols).transpose(0, 2, 1, 3).reshape(rows, cols)


def _adamw(name, w, m, v, gparts, tr):
    rows, cols = w.shape

    def body(w_ref, m_ref, v_ref, g_ref, og_ref, od_ref, om_ref, ov_ref):
        g = g_ref[0].astype(F32)
        for i in range(1, N_DEV):
            g = g + g_ref[i].astype(F32)
        m_new = B1 * m_ref[...] + (1.0 - B1) * g
        v_new = B2 * v_ref[...] + (1.0 - B2) * (g * g)
        m_hat = m_new / (1.0 - B1 ** STEP)
        v_hat = v_new / (1.0 - B2 ** STEP)
        og_ref[...] = g
        od_ref[...] = -LR * (m_hat / (jnp.sqrt(v_hat) + ADAM_EPS) + WD * w_ref[...])
        om_ref[...] = m_new
        ov_ref[...] = v_new

    spec = pl.BlockSpec((tr, cols), lambda i: (i, 0))
    shape = jax.ShapeDtypeStruct((rows, cols), F32)
    return pl.pallas_call(
        body, name=name, grid=(rows // tr,),
        in_specs=[spec, spec, spec, pl.BlockSpec((N_DEV, tr, cols), lambda i: (0, i, 0))],
        out_specs=[spec] * 4, out_shape=[shape] * 4,
        compiler_params=_cparams("parallel"),
    )(w, m, v, gparts)


_SHARDED = (
    ("ffn1_w_gate", True, (352, 1024)), ("ffn1_w_up", True, (352, 1024)), ("ffn1_w_down", False, (352, 1024)),
    ("w_in", True, (608, 1024)), ("ssm_w_glu", True, (128, 512)), ("w_attn_branch", True, (128, 256)),
    ("w_ssm_branch", True, (128, 512)), ("w_out", False, (128, 1024)),
    ("ffn2_w_gate", True, (352, 1024)), ("ffn2_w_up", True, (352, 1024)), ("ffn2_w_down", False, (352, 1024)),
)
_SMALL = ("ffn1_norm", "mix_norm", "gate_bias", "rel_bias_table", "ssm_a_re", "ssm_a_im", "ssm_log_dt",
          "ssm_b_re", "ssm_b_im", "ssm_c_re", "ssm_c_im", "ssm_d", "ffn2_norm", "final_norm")
_ORDER = ("ffn1_norm", "ffn1_w_gate", "ffn1_w_up", "ffn1_w_down", "mix_norm", "w_in", "gate_bias",
          "rel_bias_table", "ssm_a_re", "ssm_a_im", "ssm_log_dt", "ssm_b_re", "ssm_b_im", "ssm_c_re",
          "ssm_c_im", "ssm_d", "ssm_w_glu", "w_attn_branch", "w_ssm_branch", "w_out", "ffn2_norm",
          "ffn2_w_gate", "ffn2_w_up", "ffn2_w_down", "final_norm")


def _pack_rows(shape):
    return shape[0] * shape[1] // D_MODEL


_SHARD_INFO = {nm: (tr, shape) for nm, tr, shape in _SHARDED}
_PHASES = {
    "f1gu": ("ffn1_w_gate", "ffn1_w_up"), "f1d": ("ffn1_w_down",),
    "mix": ("w_in", "ssm_w_glu", "w_attn_branch", "w_ssm_branch", "w_out"),
    "f2": ("ffn2_w_gate", "ffn2_w_up", "ffn2_w_down"),
}


def _pack_sharded(ws, names):
    parts = []
    for nm in names:
        tr, shape = _SHARD_INFO[nm]
        a = ws[nm].T if tr else ws[nm]
        parts.append(a.reshape(_pack_rows(shape), D_MODEL))
    return jnp.concatenate(parts, axis=0)


def _unpack_sharded(pack, names):
    out, r0 = {}, 0
    for nm in names:
        tr, shape = _SHARD_INFO[nm]
        n = _pack_rows(shape)
        a = pack[r0:r0 + n].reshape(shape)
        out[nm] = a.T if tr else a
        r0 += n
    return out


def _unpack_gathered(gath, names):
    out, r0 = {}, 0
    for nm in names:
        _, shape = _SHARD_INFO[nm]
        n = _pack_rows(shape)
        out[nm] = gath[:, r0:r0 + n].reshape(N_DEV * shape[0], shape[1])
        r0 += n
    return out


def _pack_grads(gs, names):
    parts = []
    for nm in names:
        _, shape = _SHARD_INFO[nm]
        parts.append(gs[nm].astype(BF16).reshape(N_DEV, _pack_rows(shape), D_MODEL))
    return jnp.concatenate(parts, axis=1)


def _pack_small(ws):
    flat = jnp.concatenate([ws[nm].reshape(-1) for nm in _SMALL])
    pad = (-flat.shape[0]) % (8 * 128)
    return jnp.pad(flat, (0, pad)).reshape(-1, 128)


def _unpack_small(pack, like):
    flat, out, p0 = pack.reshape(-1), {}, 0
    for nm in _SMALL:
        n = like[nm].size
        out[nm] = flat[p0:p0 + n].reshape(like[nm].shape)
        p0 += n
    return out


def _to_dilated(a, dil):
    rows = a.shape[0]
    return a.reshape(rows // dil, dil, HEADS_PER_GROUP, HEAD_DIM).transpose(1, 2, 0, 3)


def _dilated_to_heads(a):
    dil, nh, m, dh = a.shape
    return a.transpose(1, 2, 0, 3).reshape(nh, m * dil, dh)


def _heads_to_dilated(a, dil):
    nh, rows, dh = a.shape
    return a.reshape(nh, rows // dil, dil, dh).transpose(2, 0, 1, 3)


def _block_diag(blocks_gab):
    g, a, b = blocks_gab.shape
    eye = jnp.eye(g, dtype=blocks_gab.dtype)
    return (blocks_gab[:, :, None, :] * eye[:, None, :, None]).reshape(g * a, g * b)


def _diag_blocks(mat, a, b):
    g = mat.shape[0] // a
    eye = jnp.eye(g, dtype=mat.dtype)
    return jnp.einsum("gahb,gh->gab", mat.reshape(g, a, g, b), eye)


def _local_step(xs, target, small, weights_of, send_grads):
    rows = xs.shape[0]
    gfull, gsmall = {}, {}
    wf = dict(weights_of("f1", None))

    x1, h1, gg1, uu1 = _ffn_fwd("ffn1_fwd", xs, small["ffn1_norm"], wf["ffn1_w_gate"], wf["ffn1_w_up"],
                                wf["ffn1_w_down"])
    wf.update(weights_of("mix", x1))
    hmix = _rms_fwd("mix_norm_fwd", x1, small["mix_norm"])
    w_in = wf["w_in"]
    w_qkv, w_u, w_g = w_in[:3 * ATTN_WIDTH], w_in[3 * ATTN_WIDTH:3 * ATTN_WIDTH + SSM_WIDTH], w_in[3 * ATTN_WIDTH + SSM_WIDTH:]
    qscale = jnp.concatenate([jnp.full((1, ATTN_WIDTH), HEAD_DIM ** -0.5, F32), jnp.ones((1, 2 * ATTN_WIDTH), F32)], axis=1)
    qkv, = _mm("in_qkv", [(hmix, w_qkv)], True, 3 * ATTN_WIDTH, [BF16],
               epilogue=lambda acc, sc: (acc * sc,), extras=[(qscale, 0)], tn=ATTN_WIDTH)
    u, = _mm("in_u", [(hmix, w_u)], True, SSM_WIDTH, [F32])
    gates, = _mm("in_gates", [(hmix, w_g)], True, 2 * D_MODEL, [F32],
                 epilogue=lambda acc, b: (_sigmoid(acc + b),), extras=[(small["gate_bias"], 0)])

    buckets = jnp.asarray(_bucket_table())
    bias = _bias_fwd("rel_bias_fwd", buckets, small["rel_bias_table"])
    qkv_d, o_h, lse_h = [], [], []
    for g, dil in enumerate(DILATIONS):
        cols = [qkv[:, s * ATTN_WIDTH + g * ATTN_OUT:s * ATTN_WIDTH + (g + 1) * ATTN_OUT] for s in range(3)]
        qd, kd, vd = [_to_dilated(c, dil) for c in cols]
        bias_g = bias[g * HEADS_PER_GROUP:(g + 1) * HEADS_PER_GROUP]
        o_g, lse_g = _attn_fwd(f"attn_fwd_{g}", qd, kd, vd, bias_g)
        qkv_d.append((qd, kd, vd, bias_g))
        o_h.append(_dilated_to_heads(o_g))
        lse_h.append(_dilated_to_heads(lse_g))
    oa_h = _combine_fwd("attn_combine_fwd", o_h, lse_h)
    oa = oa_h.transpose(1, 0, 2).reshape(rows, ATTN_OUT).astype(BF16)
    y_attn, = _mm("attn_branch", [(oa, wf["w_attn_branch"])], True, D_MODEL, [F32])

    ab_re, ab_im, ap_re, ap_im, bb_re, bb_im = _ssm_params_fwd(
        "ssm_params_fwd", small["ssm_a_re"], small["ssm_a_im"], small["ssm_log_dt"].reshape(SSM_GROUPS, 1),
        small["ssm_b_re"].transpose(2, 0, 1), small["ssm_b_im"].transpose(2, 0, 1))

    def lanes(re, im, sign=1.0):
        row = jnp.concatenate([re.reshape(1, SSM_LANES), sign * im.reshape(1, SSM_LANES)], axis=1)
        return jnp.broadcast_to(row, (SCAN_SUB, 2 * SSM_LANES))

    bb_mat = jnp.concatenate([_block_diag(bb_re.transpose(1, 0, 2)), _block_diag(bb_im.transpose(1, 0, 2))], axis=1)
    c_mat_t = jnp.concatenate([_block_diag(small["ssm_c_re"]), -_block_diag(small["ssm_c_im"])], axis=1)
    bb_mat, c_mat_t = bb_mat.astype(BF16), c_mat_t.astype(BF16)
    d_skip = small["ssm_d"].reshape(1, SSM_WIDTH)
    u_perm = _scan_order(u)
    y_perm, states = _ssm_fwd("ssm_fwd", u_perm, bb_mat, c_mat_t.T, lanes(ab_re, ab_im), lanes(ap_re, ap_im), d_skip)
    y_raw = _time_order(y_perm)

    def gelu_fn(yv):
        return (jax.nn.gelu(yv),)

    ygelu, = _ew("ssm_gelu", gelu_fn, [y_raw], [SSM_WIDTH], [BF16])
    glu, = _mm("ssm_glu", [(ygelu, wf["ssm_w_glu"])], True, 2 * SSM_WIDTH, [F32])
    ysg, = _ew("ssm_glu_act", lambda gv: (gv[:, :SSM_WIDTH] * _sigmoid(gv[:, SSM_WIDTH:]),), [glu], [SSM_WIDTH], [BF16])
    y_ssm, merged = _mm("ssm_branch_merge", [(ysg, wf["w_ssm_branch"])], True, D_MODEL, [F32, BF16],
                        epilogue=lambda acc, ga, gs, ya: (acc, ga * ya + gs * acc),
                        extras=[(gates, 0), (gates, D_MODEL), (y_attn, 0)])
    x2, = _mm("mix_out", [(merged, wf["w_out"])], False, D_MODEL, [F32],
              epilogue=lambda acc, res: (res + acc,), extras=[(x1, 0)])
    wf.update(weights_of("f2", x2))
    x3, h2, gg2, uu2 = _ffn_fwd("ffn2_fwd", x2, small["ffn2_norm"], wf["ffn2_w_gate"], wf["ffn2_w_up"],
                                wf["ffn2_w_down"])
    dx3, gsmall["final_norm"], loss = _final_loss("final_loss", x3, small["final_norm"].reshape(1, D_MODEL), target)

    dx2, dgg2, duu2, act2, gsmall["ffn2_norm"] = _ffn_bwd(
        "ffn2_bwd", dx3, x2, small["ffn2_norm"], gg2, uu2, wf["ffn2_w_gate"], wf["ffn2_w_up"], wf["ffn2_w_down"])
    gfull["ffn2_w_gate"] = _mm_tn("ffn2_dwg", dgg2, h2)
    gfull["ffn2_w_up"] = _mm_tn("ffn2_dwu", duu2, h2)
    gfull["ffn2_w_down"] = _mm_tn("ffn2_dwd", act2, dx3, scale=0.5)
    dx2 = send_grads("f2", gfull, dx2)

    def merge_bwd(dm, ga, gs, ya, ys):
        return (dm * ga, dm * gs, dm * ya * ga * (1.0 - ga), dm * ys * gs * (1.0 - gs))

    dya, dys, dzga, dzgs = _mm("mix_out_bwd", [(dx2, wf["w_out"])], True, D_MODEL, [BF16] * 4, epilogue=merge_bwd,
                               extras=[(gates, 0), (gates, D_MODEL), (y_attn, 0), (y_ssm, 0)])
    gfull["w_out"] = _mm_tn("dw_out", merged, dx2)
    gsmall["gate_bias"] = jnp.concatenate([_colsum("dgate_bias_a", dzga), _colsum("dgate_bias_s", dzgs)], axis=1)

    gfull["w_ssm_branch"] = _mm_tn("dw_ssm_branch", dys, ysg)

    def glu_bwd(dysg, av, bv):
        sb = _sigmoid(bv)
        return (dysg * sb, dysg * av * sb * (1.0 - sb))

    dglu_a, dglu_b = _mm("ssm_branch_bwd", [(dys, wf["w_ssm_branch"])], False, SSM_WIDTH, [BF16, BF16],
                         epilogue=glu_bwd, extras=[(glu, 0), (glu, SSM_WIDTH)])
    w_glu = wf["ssm_w_glu"]
    gfull["ssm_w_glu"] = jnp.concatenate([_mm_tn("dw_glu_a", dglu_a, ygelu), _mm_tn("dw_glu_b", dglu_b, ygelu)], axis=0)

    def gelu_bwd(acc, yv):
        _, vjp = jax.vjp(jax.nn.gelu, yv)
        return (vjp(acc)[0],)

    dy_raw, = _mm("ssm_glu_bwd", [(dglu_a, w_glu[:SSM_WIDTH]), (dglu_b, w_glu[SSM_WIDTH:])], False, SSM_WIDTH, [F32],
                  epilogue=gelu_bwd, extras=[(y_raw, 0)])
    dy_perm = _scan_order(dy_raw)
    du_perm, g_states, dab_rows, gsmall_d = _ssm_bwd(
        "ssm_bwd", dy_perm, u_perm, states, c_mat_t, bb_mat.T, lanes(ab_re, ab_im, -1.0), lanes(ap_re, ap_im, -1.0), d_skip)
    du = _time_order(du_perm)
    gsmall["ssm_d"] = gsmall_d
    dbb_acc = _mm_tn("ssm_dbb", u_perm, g_states, bm=SSM_WIDTH)
    dc_acc = _mm_tn("ssm_dc", dy_perm, states, bm=SSM_WIDTH)
    dbb_re = _diag_blocks(dbb_acc[:, :SSM_LANES], SSM_GROUP, SSM_STATE).transpose(1, 0, 2)
    dbb_im = _diag_blocks(dbb_acc[:, SSM_LANES:], SSM_GROUP, SSM_STATE).transpose(1, 0, 2)
    gsmall["ssm_c_re"] = _diag_blocks(dc_acc[:, :SSM_LANES], SSM_GROUP, SSM_STATE)
    gsmall["ssm_c_im"] = -_diag_blocks(dc_acc[:, SSM_LANES:], SSM_GROUP, SSM_STATE)
    dab = _colsum("ssm_dab", dab_rows)
    d_ar, d_ai, d_ld, d_br, d_bi = _ssm_params_bwd(
        "ssm_params_bwd", small["ssm_a_re"], small["ssm_a_im"], small["ssm_log_dt"].reshape(SSM_GROUPS, 1),
        small["ssm_b_re"].transpose(2, 0, 1), small["ssm_b_im"].transpose(2, 0, 1),
        dab[:, :SSM_LANES].reshape(SSM_GROUPS, SSM_STATE), dab[:, SSM_LANES:].reshape(SSM_GROUPS, SSM_STATE),
        dbb_re, dbb_im)
    gsmall["ssm_a_re"], gsmall["ssm_a_im"], gsmall["ssm_log_dt"] = d_ar, d_ai, d_ld.reshape(SSM_GROUPS)
    gsmall["ssm_b_re"], gsmall["ssm_b_im"] = d_br.transpose(1, 2, 0), d_bi.transpose(1, 2, 0)

    gfull["w_attn_branch"] = _mm_tn("dw_attn_branch", dya, oa)
    doa, = _mm("attn_branch_bwd", [(dya, wf["w_attn_branch"])], False, ATTN_OUT, [F32])
    do_h = doa.reshape(rows, HEADS_PER_GROUP, HEAD_DIM).transpose(1, 0, 2)
    dc = _combine_bwd("attn_combine_bwd", do_h, oa_h, lse_h)
    dqkv_cols = [None] * 9
    dbias = []
    for g, dil in enumerate(DILATIONS):
        qd, kd, vd, bias_g = qkv_d[g]
        dq, dk, dv, db = _attn_bwd(f"attn_bwd_{g}", qd, kd, vd, _heads_to_dilated(dc[g], dil),
                                   _heads_to_dilated(lse_h[g], dil), _heads_to_dilated(dc[3 + g], dil), bias_g)
        dbias.append(db)
        for s, (arr, sc) in enumerate(((dq, HEAD_DIM ** -0.5), (dk, 1.0), (dv, 1.0))):
            tok = _dilated_to_heads(arr).transpose(1, 0, 2).reshape(rows, ATTN_OUT)
            dqkv_cols[3 * s + g] = (tok * sc).astype(BF16)
    dqkv = jnp.concatenate(dqkv_cols, axis=1)
    gsmall["rel_bias_table"] = _bias_bwd("rel_bias_bwd", buckets, jnp.concatenate(dbias, axis=0))[:, :N_GROUPS * HEADS_PER_GROUP]

    gfull["w_in"] = jnp.concatenate([
        _mm_tn("dw_in_qkv", dqkv, hmix), _mm_tn("dw_in_u", du, hmix),
        _mm_tn("dw_in_ga", dzga, hmix), _mm_tn("dw_in_gs", dzgs, hmix)], axis=0)
    dhmix, = _mm("in_bwd", [(dqkv, w_qkv), (du, w_u), (dzga, w_g[:D_MODEL]), (dzgs, w_g[D_MODEL:])], False, D_MODEL, [F32], tm=512)
    dhmix = send_grads("mix", gfull, dhmix)
    dx1, gsmall["mix_norm"] = _rms_bwd("mix_norm_bwd", dhmix, x1, small["mix_norm"], dx2)

    dx, dgg1, duu1, act1, gsmall["ffn1_norm"] = _ffn_bwd(
        "ffn1_bwd", dx1, xs, small["ffn1_norm"], gg1, uu1, wf["ffn1_w_gate"], wf["ffn1_w_up"], wf["ffn1_w_down"])
    gfull["ffn1_w_down"] = _mm_tn("ffn1_dwd", act1, dx1, scale=0.5)
    dgg1 = send_grads("f1d", gfull, dgg1)
    gfull["ffn1_w_gate"] = _mm_tn("ffn1_dwg", dgg1, h1)
    gfull["ffn1_w_up"] = _mm_tn("ffn1_dwu", duu1, h1)
    send_grads("f1gu", gfull, None)
    return loss[0, 0], dx, gsmall


def kernel(x, ffn1_norm, ffn1_w_gate, ffn1_w_up, ffn1_w_down, mix_norm, w_in, gate_bias, rel_bias_table, ssm_a_re, ssm_a_im, ssm_log_dt, ssm_b_re, ssm_b_im, ssm_c_re, ssm_c_im, ssm_d, ssm_w_glu, w_attn_branch, w_ssm_branch, w_out, ffn2_norm, ffn2_w_gate, ffn2_w_up, ffn2_w_down, final_norm, loss_target, m_ffn1_norm, m_ffn1_w_gate, m_ffn1_w_up, m_ffn1_w_down, m_mix_norm, m_w_in, m_gate_bias, m_rel_bias_table, m_ssm_a_re, m_ssm_a_im, m_ssm_log_dt, m_ssm_b_re, m_ssm_b_im, m_ssm_c_re, m_ssm_c_im, m_ssm_d, m_ssm_w_glu, m_w_attn_branch, m_w_ssm_branch, m_w_out, m_ffn2_norm, m_ffn2_w_gate, m_ffn2_w_up, m_ffn2_w_down, m_final_norm, v_ffn1_norm, v_ffn1_w_gate, v_ffn1_w_up, v_ffn1_w_down, v_mix_norm, v_w_in, v_gate_bias, v_rel_bias_table, v_ssm_a_re, v_ssm_a_im, v_ssm_log_dt, v_ssm_b_re, v_ssm_b_im, v_ssm_c_re, v_ssm_c_im, v_ssm_d, v_ssm_w_glu, v_w_attn_branch, v_w_ssm_branch, v_w_out, v_ffn2_norm, v_ffn2_w_gate, v_ffn2_w_up, v_ffn2_w_down, v_final_norm):
    given = dict(locals())
    shapes = {nm: given[nm].shape for nm in _ORDER}

    def strip(a):
        return a[0] if a.ndim >= 2 and a.shape[0] == 1 else a

    w = {nm: strip(given[nm]) for nm in _ORDER}
    m = {nm: strip(given["m_" + nm]) for nm in _ORDER}
    v = {nm: strip(given["v_" + nm]) for nm in _ORDER}
    for d in (w, m, v):
        d["rel_bias_table"] = d["rel_bias_table"].reshape(N_BUCKETS, N_GROUPS * HEADS_PER_GROUP)

    small = {nm: w[nm] for nm in _SMALL}
    small_in = dict(small)
    for nm in ("ffn1_norm", "mix_norm", "ffn2_norm", "gate_bias"):
        small_in[nm] = small[nm].reshape(1, -1)
    w_pack = {ph: _pack_sharded(w, names) for ph, names in _PHASES.items()}

    f1_names = _PHASES["f1gu"] + _PHASES["f1d"]
    got_f1 = _all_gather("gather_f1", jnp.concatenate([w_pack["f1gu"], w_pack["f1d"]], axis=0).astype(BF16))
    got_f1, mix_bf = lax.optimization_barrier((got_f1, w_pack["mix"].astype(BF16)))
    pending_w = {"mix": _exchange_start("gather_mix_start", mix_bf, gather=True)}
    token, f2_bf = lax.optimization_barrier((pending_w["mix"][4], w_pack["f2"].astype(BF16)))
    pending_w["f2"] = _exchange_start("gather_f2_start", f2_bf, gather=True)
    xs, _ = lax.optimization_barrier((x[0], (token, pending_w["f2"][4])))

    def weights_of(phase, after):
        if phase == "f1":
            return _unpack_gathered(got_f1, f1_names)
        return _unpack_gathered(_exchange_wait(f"gather_{phase}_wait", pending_w[phase], after, gather=True),
                                _PHASES[phase])

    pending_g, recv = {}, {}

    def send_grads(phase, grads, then):
        pack = _pack_grads(grads, _PHASES[phase])
        if then is None:
            recv[phase] = _all_to_all(f"scatter_{phase}", pack)
            return None
        pending_g[phase] = _exchange_start(f"scatter_{phase}_start", pack, gather=False)
        then, _ = lax.optimization_barrier((then, pending_g[phase][4]))
        return then

    loss, dx, gsmall = _local_step(xs, loss_target[0], small_in, weights_of, send_grads)

    for phase in pending_g:
        recv[phase] = _exchange_wait(f"scatter_{phase}_wait", pending_g[phase], recv["f1gu"], gather=False)
    packs = {}
    for phase, names in _PHASES.items():
        rows_p = w_pack[phase].shape[0]
        tr = max(t for t in range(16, 129, 16) if rows_p % t == 0)
        packs[phase] = _adamw(f"adamw_{phase}", w_pack[phase], _pack_sharded(m, names), _pack_sharded(v, names),
                              recv[phase], tr)
    gs_pack = _pack_small({nm: gsmall[nm].reshape(small[nm].shape) for nm in _SMALL})
    gs_all = _all_gather("gather_small_grads", gs_pack)
    sm = _adamw("adamw_small", _pack_small(small), _pack_small({nm: m[nm] for nm in _SMALL}),
                _pack_small({nm: v[nm] for nm in _SMALL}), gs_all, gs_pack.shape[0])

    loss = lax.psum(loss, ("x", "y", "c"))
    outs = []
    for i in range(4):
        big = {}
        for phase, names in _PHASES.items():
            big.update(_unpack_sharded(packs[phase][i], names))
        sml = _unpack_small(sm[i], small)
        outs.append([(big[nm] if nm in big else sml[nm]).reshape(shapes[nm]) for nm in _ORDER])
    return (loss, dx[None], *outs[0], *outs[1], *outs[2], *outs[3])
```

```python
import functools
import math

import numpy as np
import jax
import jax.numpy as jnp
from jax import lax
from jax.experimental import pallas as pl
from jax.experimental.pallas import tpu as pltpu

F32 = jnp.float32
BF16 = jnp.bfloat16

N_DEV = 8
D_MODEL = 1024
D_FF = 2816
HEAD_DIM = 64
HEADS_PER_GROUP = 4
DILATIONS = (1, 4, 16)
N_GROUPS = 3
ATTN_WIDTH = 768
ATTN_OUT = 256
BLOCK = 128
N_BUCKETS = 32
MAX_DISTANCE = 2048
NEG_INF = -1e30
SSM_WIDTH = 512
SSM_GROUPS = 32
SSM_GROUP = 16
SSM_STATE = 64
SSM_LANES = SSM_GROUPS * SSM_STATE
EPS = 1e-6
LR, B1, B2, ADAM_EPS, WD, STEP = 0.001, 0.9, 0.999, 1e-08, 0.01, 10

VMEM_LIMIT_BYTES = 56 * 1024 * 1024
SCAN_BLOCK = 256
SCAN_SUB = 8
SCAN_STEPS = SCAN_BLOCK // SCAN_SUB
SCAN_LANES = 512

MESH = pl.DeviceIdType.MESH


def _cparams(*sem):
    return pltpu.CompilerParams(dimension_semantics=sem, vmem_limit_bytes=VMEM_LIMIT_BYTES)


def _dot(a, b, dims):
    return lax.dot_general(a, b, (dims, ((), ())), preferred_element_type=F32)


def _dot_nn(a, b):
    return _dot(a, b, ((1,), (0,)))


def _dot_nt(a, b):
    return _dot(a, b, ((1,), (1,)))


def _dot_tn(a, b):
    return _dot(a, b, ((0,), (0,)))


def _sigmoid(x):
    return 1.0 / (1.0 + jnp.exp(-x))


def _all_gather(name, xs):
    rows, cols = xs.shape

    def body(x_ref, out_ref, send_sems, recv_sems, local_sem):
        x, y, c = lax.axis_index("x"), lax.axis_index("y"), lax.axis_index("c")
        me, sibling = (x, y, c), (x, y, 1 - c)
        chips = [(1 - x, y), (x, 1 - y), (1 - x, 1 - y)]

        def slot(px, py, pc):
            return out_ref.at[4 * px + 2 * py + pc]

        def copy(k, block, to, src=None):
            return pltpu.make_async_remote_copy(
                src_ref=slot(*block) if src is None else src, dst_ref=slot(*block),
                send_sem=send_sems.at[k], recv_sem=recv_sems.at[k], device_id=to, device_id_type=MESH)

        mine = pltpu.make_async_copy(x_ref, slot(*me), local_sem)
        mine.start()
        first = [copy(0, me, sibling, src=x_ref)]
        first += [copy(1 + j, me, (*chip, c), src=x_ref) for j, chip in enumerate(chips)]
        for cp in first:
            cp.start()
        passed = [copy(4 + j, (*chip, c), sibling) for j, chip in enumerate(chips)]
        for j, chip in enumerate(chips):
            copy(1 + j, (*chip, c), me).wait_recv()
            passed[j].start()
        copy(0, sibling, me).wait_recv()
        for j, chip in enumerate(chips):
            copy(4 + j, (*chip, 1 - c), me).wait_recv()
        for cp in first + passed:
            cp.wait_send()
        mine.wait()

    return pl.pallas_call(
        body, name=name,
        out_shape=jax.ShapeDtypeStruct((N_DEV, rows, cols), xs.dtype),
        in_specs=[pl.BlockSpec(memory_space=pl.ANY)],
        out_specs=pl.BlockSpec(memory_space=pl.ANY),
        scratch_shapes=[pltpu.SemaphoreType.DMA((7,)), pltpu.SemaphoreType.DMA((7,)), pltpu.SemaphoreType.DMA],
    )(xs)


def _all_to_all(name, xs):
    _, rows, cols = xs.shape

    def body(x_ref, out_ref, send_sems, recv_sems, local_sem):
        x, y, c = lax.axis_index("x"), lax.axis_index("y"), lax.axis_index("c")
        me = 4 * x + 2 * y + c
        mine = pltpu.make_async_copy(x_ref.at[me], out_ref.at[me], local_sem)
        mine.start()
        copies = []
        for k in range(1, N_DEV):
            px = 1 - x if k & 4 else x
            py = 1 - y if k & 2 else y
            pc = 1 - c if k & 1 else c
            cp = pltpu.make_async_remote_copy(
                src_ref=x_ref.at[4 * px + 2 * py + pc], dst_ref=out_ref.at[me],
                send_sem=send_sems.at[k - 1], recv_sem=recv_sems.at[k - 1],
                device_id=(px, py, pc), device_id_type=MESH)
            cp.start()
            copies.append(cp)
        for cp in copies:
            cp.wait()
        mine.wait()

    return pl.pallas_call(
        body, name=name,
        out_shape=jax.ShapeDtypeStruct(xs.shape, xs.dtype),
        in_specs=[pl.BlockSpec(memory_space=pl.ANY)],
        out_specs=pl.BlockSpec(memory_space=pl.ANY),
        scratch_shapes=[pltpu.SemaphoreType.DMA((7,)), pltpu.SemaphoreType.DMA((7,)), pltpu.SemaphoreType.DMA],
    )(xs)


_HBM_SPEC = pl.BlockSpec(memory_space=pltpu.HBM)
_SEM_SPEC = pl.BlockSpec(memory_space=pltpu.SEMAPHORE)
_EFFECT = pltpu.SideEffectType.DATAFLOW_SIDE_EFFECTING


def _peers(x, y, c):
    return [(1 - x if k & 4 else x, 1 - y if k & 2 else y, 1 - c if k & 1 else c) for k in range(1, N_DEV)]


def _exchange_copies(x_ref, land_ref, send_sems, recv_sems, gather):
    x, y, c = lax.axis_index("x"), lax.axis_index("y"), lax.axis_index("c")
    me = 4 * x + 2 * y + c
    copies = []
    for k, (px, py, pc) in enumerate(_peers(x, y, c)):
        src = x_ref if gather else x_ref.at[4 * px + 2 * py + pc]
        copies.append(pltpu.make_async_remote_copy(
            src_ref=src, dst_ref=land_ref.at[me], send_sem=send_sems.at[k], recv_sem=recv_sems.at[k],
            device_id=(px, py, pc), device_id_type=MESH))
    return me, copies


_ANY_SPEC = pl.BlockSpec(memory_space=pl.ANY)


def _exchange_start(name, xs, gather, deps=()):
    land_shape = (N_DEV, *xs.shape) if gather else xs.shape
    nd = len(deps)

    def body(x_ref, land_ref, *rest):
        send_sems, recv_sems, _, _, token = rest[nd:]
        _, copies = _exchange_copies(x_ref, land_ref, send_sems, recv_sems, gather)
        for cp in copies:
            cp.start()
        token[...] = jnp.zeros_like(token)

    return pl.pallas_call(
        body, name=name,
        out_shape=(pltpu.SemaphoreType.DMA((N_DEV - 1,)), pltpu.SemaphoreType.DMA((N_DEV - 1,)),
                   pltpu.HBM(xs.shape, xs.dtype), pltpu.HBM(land_shape, xs.dtype), jax.ShapeDtypeStruct((8, 128), F32)),
        in_specs=(_HBM_SPEC, _HBM_SPEC) + (_ANY_SPEC,) * nd,
        out_specs=(_SEM_SPEC, _SEM_SPEC, _HBM_SPEC, _HBM_SPEC, pl.BlockSpec(memory_space=pltpu.VMEM)),
        input_output_aliases={0: 2, 1: 3},
        compiler_params=pltpu.CompilerParams(has_side_effects=_EFFECT),
    )(pltpu.with_memory_space_constraint(xs, pltpu.HBM),
      pltpu.with_memory_space_constraint(lax.empty(land_shape, xs.dtype), pltpu.HBM), *deps)


def _exchange_wait(name, handle, after, gather):
    send_sems, recv_sems, x_thru, land_thru, _ = handle

    def body(x_ref, land_ref, send_sems, recv_sems, after_ref, x_dead, got_ref, local_sem):
        me, copies = _exchange_copies(x_ref, land_ref, send_sems, recv_sems, gather)
        for cp in copies:
            cp.wait_send()
            cp.wait_recv()
        own = pltpu.make_async_copy(x_ref if gather else x_ref.at[me], got_ref.at[me], local_sem)
        own.start()
        own.wait()

    return pl.pallas_call(
        body, name=name,
        out_shape=(pltpu.HBM(x_thru.shape, x_thru.dtype), pltpu.HBM(land_thru.shape, land_thru.dtype)),
        in_specs=(_HBM_SPEC, _HBM_SPEC, _SEM_SPEC, _SEM_SPEC, pl.BlockSpec(memory_space=pl.ANY)),
        out_specs=(_HBM_SPEC, _HBM_SPEC), input_output_aliases={0: 0, 1: 1},
        scratch_shapes=[pltpu.SemaphoreType.DMA],
        compiler_params=pltpu.CompilerParams(has_side_effects=_EFFECT),
    )(x_thru, land_thru, send_sems, recv_sems, after)[1]


def _mm(name, pairs, nt, n_cols, out_dtypes, epilogue=None, extras=(), tm=1024, tn=512, deps=()):
    rows = pairs[0][0].shape[0]
    tm = min(tm, rows)
    tn = min(tn, n_cols)
    na, ne, nd = len(pairs), len(extras), len(deps)

    def body(*refs):
        a_refs, w_refs = refs[:na], refs[na:2 * na]
        e_refs, o_refs = refs[2 * na:2 * na + ne], refs[2 * na + ne + nd:]
        acc = None
        for a_ref, w_ref in zip(a_refs, w_refs):
            a = a_ref[...].astype(BF16)
            w = w_ref[...].astype(BF16)
            p = _dot_nt(a, w) if nt else _dot_nn(a, w)
            acc = p if acc is None else acc + p
        outs = (acc,) if epilogue is None else epilogue(acc, *[e[...] for e in e_refs])
        for o_ref, o in zip(o_refs, outs):
            o_ref[...] = o.astype(o_ref.dtype)

    in_specs = [pl.BlockSpec((tm, a.shape[1]), lambda i, j: (i, 0)) for a, _ in pairs]
    for _, w in pairs:
        if nt:
            in_specs.append(pl.BlockSpec((tn, w.shape[1]), lambda i, j: (j, 0)))
        else:
            in_specs.append(pl.BlockSpec((w.shape[0], tn), lambda i, j: (0, j)))
    for e, col_off in extras:
        off = col_off // tn
        if e.shape[0] == 1:
            in_specs.append(pl.BlockSpec((1, tn), lambda i, j, off=off: (0, j + off)))
        else:
            in_specs.append(pl.BlockSpec((tm, tn), lambda i, j, off=off: (i, j + off)))
    in_specs += [_ANY_SPEC] * nd
    out_specs = [pl.BlockSpec((tm, tn), lambda i, j: (i, j)) for _ in out_dtypes]
    outs = pl.pallas_call(
        body, name=name, grid=(rows // tm, n_cols // tn),
        in_specs=in_specs, out_specs=out_specs,
        out_shape=[jax.ShapeDtypeStruct((rows, n_cols), dt) for dt in out_dtypes],
        compiler_params=_cparams("parallel", "arbitrary"),
    )(*[a for a, _ in pairs], *[w for _, w in pairs], *[e for e, _ in extras], *deps)
    return outs


def _tn_rows(m):
    return max(b for b in range(128, min(m, 1408) + 1, 128) if m % b == 0)


def _mm_tn(name, a, b, scale=1.0, bm=None, tk=1024, deps=()):
    rows, m = a.shape
    n = b.shape[1]
    bm = _tn_rows(m) if bm is None else bm
    tk = min(tk, rows)
    nk = rows // tk

    def body(a_ref, b_ref, *rest):
        o_ref = rest[-1]
        k = pl.program_id(1)

        @pl.when(k == 0)
        def _():
            o_ref[...] = jnp.zeros_like(o_ref)

        o_ref[...] += _dot_tn(a_ref[...].astype(BF16), b_ref[...].astype(BF16))
        if scale != 1.0:
            @pl.when(k == nk - 1)
            def _():
                o_ref[...] = o_ref[...] * scale

    return pl.pallas_call(
        body, name=name, grid=(m // bm, nk),
        in_specs=[pl.BlockSpec((tk, bm), lambda i, k: (k, i)), pl.BlockSpec((tk, n), lambda i, k: (k, 0))]
        + [_ANY_SPEC] * len(deps),
        out_specs=pl.BlockSpec((bm, n), lambda i, k: (i, 0)),
        out_shape=jax.ShapeDtypeStruct((m, n), F32),
        compiler_params=_cparams("parallel", "arbitrary"),
    )(a, b, *deps)


def _colsum(name, xs, tm=512):
    rows, cols = xs.shape
    tm = min(tm, rows)

    def body(x_ref, o_ref):
        @pl.when(pl.program_id(0) == 0)
        def _():
            o_ref[...] = jnp.zeros_like(o_ref)

        o_ref[...] += jnp.sum(x_ref[...].astype(F32), axis=0, keepdims=True)

    return pl.pallas_call(
        body, name=name, grid=(rows // tm,),
        in_specs=[pl.BlockSpec((tm, cols), lambda i: (i, 0))],
        out_specs=pl.BlockSpec((1, cols), lambda i: (0, 0)),
        out_shape=jax.ShapeDtypeStruct((1, cols), F32),
        compiler_params=_cparams("arbitrary"),
    )(xs)


def _ew(name, fn, ins, out_cols, out_dtypes, tm=512):
    rows = ins[0].shape[0]
    tm = min(tm, rows)
    ni = len(ins)

    def body(*refs):
        outs = fn(*[r[...] for r in refs[:ni]])
        for o_ref, o in zip(refs[ni:], outs):
            o_ref[...] = o.astype(o_ref.dtype)

    def spec(shape):
        if shape[0] == 1:
            return pl.BlockSpec((1, shape[1]), lambda i: (0, 0))
        return pl.BlockSpec((tm, shape[1]), lambda i: (i, 0))

    return pl.pallas_call(
        body, name=name, grid=(rows // tm,),
        in_specs=[spec(a.shape) for a in ins],
        out_specs=[pl.BlockSpec((tm, c), lambda i: (i, 0)) for c in out_cols],
        out_shape=[jax.ShapeDtypeStruct((rows, c), dt) for c, dt in zip(out_cols, out_dtypes)],
        compiler_params=_cparams("parallel"),
    )(*ins)


def _rms_parts(xv):
    r = lax.rsqrt(jnp.mean(xv * xv, axis=-1, keepdims=True) + EPS)
    return r, xv * r


def _rms_bwd_dx(dh, gain, r, xh):
    dxh = dh * gain
    return r * (dxh - xh * jnp.mean(dxh * xh, axis=-1, keepdims=True))


def _rms_fwd(name, xs, gain):
    def fn(xv, g):
        _, xh = _rms_parts(xv)
        return (xh * g,)

    return _ew(name, fn, [xs, gain], [xs.shape[1]], [BF16])[0]


def _rms_bwd(name, dh, xs, gain, dres, tm=512, deps=()):
    rows, d = xs.shape
    tm = min(tm, rows)

    def body(dh_ref, x_ref, g_ref, dres_ref, *rest):
        dx_ref, dg_ref = rest[-2:]
        r, xh = _rms_parts(x_ref[...])
        dhv = dh_ref[...]
        dx_ref[...] = dres_ref[...] + _rms_bwd_dx(dhv, g_ref[...], r, xh)

        @pl.when(pl.program_id(0) == 0)
        def _():
            dg_ref[...] = jnp.zeros_like(dg_ref)

        dg_ref[...] += jnp.sum(dhv * xh, axis=0, keepdims=True)

    tile = pl.BlockSpec((tm, d), lambda i: (i, 0))
    row = pl.BlockSpec((1, d), lambda i: (0, 0))
    return pl.pallas_call(
        body, name=name, grid=(rows // tm,),
        in_specs=[tile, tile, row, tile] + [_ANY_SPEC] * len(deps), out_specs=[tile, row],
        out_shape=[jax.ShapeDtypeStruct((rows, d), F32), jax.ShapeDtypeStruct((1, d), F32)],
        compiler_params=_cparams("arbitrary"),
    )(dh, xs, gain, dres, *deps)


def _ffn_fwd(name, xs, gain, wg_t, wu_t, wd, tm=512, tf=1408, deps=()):
    rows, d = xs.shape
    f_all = wd.shape[0]
    tm = min(tm, rows)
    nf = f_all // tf

    def body(x_ref, g_ref, wg_ref, wu_ref, wd_ref, *rest):
        xo_ref, h_ref, gg_ref, uu_ref, acc_ref = rest[-5:]
        f = pl.program_id(1)

        @pl.when(f == 0)
        def _():
            _, xh = _rms_parts(x_ref[...])
            h_ref[...] = (xh * g_ref[...]).astype(BF16)
            acc_ref[...] = jnp.zeros_like(acc_ref)

        h = h_ref[...]
        gg = _dot_nt(h, wg_ref[...])
        uu = _dot_nt(h, wu_ref[...])
        act = gg * _sigmoid(gg) * uu
        acc_ref[...] += _dot_nn(act.astype(BF16), wd_ref[...])
        gg_ref[...] = gg.astype(BF16)
        uu_ref[...] = uu.astype(BF16)

        @pl.when(f == nf - 1)
        def _():
            xo_ref[...] = x_ref[...] + 0.5 * acc_ref[...]

    tile = pl.BlockSpec((tm, d), lambda i, f: (i, 0))
    wspec = pl.BlockSpec((tf, d), lambda i, f: (f, 0))
    hid = pl.BlockSpec((tm, tf), lambda i, f: (i, f))
    return pl.pallas_call(
        body, name=name, grid=(rows // tm, nf),
        in_specs=[tile, pl.BlockSpec((1, d), lambda i, f: (0, 0)), wspec, wspec, wspec] + [_ANY_SPEC] * len(deps),
        out_specs=[tile, tile, hid, hid],
        out_shape=[jax.ShapeDtypeStruct((rows, d), F32), jax.ShapeDtypeStruct((rows, d), BF16),
                   jax.ShapeDtypeStruct((rows, f_all), BF16), jax.ShapeDtypeStruct((rows, f_all), BF16)],
        scratch_shapes=[pltpu.VMEM((tm, d), F32)],
        compiler_params=_cparams("parallel", "arbitrary"),
    )(xs, gain, wg_t, wu_t, wd, *deps)


def _ffn_bwd(name, dxo, xs, gain, gg_all, uu_all, wg_t, wu_t, wd, tm=256, tf=1408):
    rows, d = xs.shape
    f_all = wd.shape[0]
    tm = min(tm, rows)
    nf = f_all // tf

    def body(dxo_ref, x_ref, g_ref, gg_ref, uu_ref, wg_ref, wu_ref, wd_ref,
             dx_ref, dgg_ref, duu_ref, act_ref, dgain_ref, df_ref, acc_ref):
        i, f = pl.program_id(0), pl.program_id(1)

        @pl.when(f == 0)
        def _():
            df_ref[...] = (0.5 * dxo_ref[...]).astype(BF16)
            acc_ref[...] = jnp.zeros_like(acc_ref)

        gg = gg_ref[...].astype(F32)
        uu = uu_ref[...].astype(F32)
        sg = _sigmoid(gg)
        silu = gg * sg
        dact = _dot_nt(df_ref[...], wd_ref[...])
        duu = (dact * silu).astype(BF16)
        dgg = (dact * uu * (sg * (1.0 + gg * (1.0 - sg)))).astype(BF16)
        act_ref[...] = (silu * uu).astype(BF16)
        dgg_ref[...] = dgg
        duu_ref[...] = duu
        acc_ref[...] += _dot_nn(dgg, wg_ref[...]) + _dot_nn(duu, wu_ref[...])

        @pl.when(f == nf - 1)
        def _():
            r, xh = _rms_parts(x_ref[...])
            dh = acc_ref[...]
            dx_ref[...] = dxo_ref[...] + _rms_bwd_dx(dh, g_ref[...], r, xh)
            part = jnp.sum(dh * xh, axis=0, keepdims=True)

            @pl.when(i == 0)
            def _():
                dgain_ref[...] = part

            @pl.when(i > 0)
            def _():
                dgain_ref[...] += part

    tile = pl.BlockSpec((tm, d), lambda i, f: (i, 0))
    row = pl.BlockSpec((1, d), lambda i, f: (0, 0))
    wspec = pl.BlockSpec((tf, d), lambda i, f: (f, 0))
    hid = pl.BlockSpec((tm, tf), lambda i, f: (i, f))
    hid_shape = jax.ShapeDtypeStruct((rows, f_all), BF16)
    return pl.pallas_call(
        body, name=name, grid=(rows // tm, nf),
        in_specs=[tile, tile, row, hid, hid, wspec, wspec, wspec],
        out_specs=[tile, hid, hid, hid, row],
        out_shape=[jax.ShapeDtypeStruct((rows, d), F32), hid_shape, hid_shape, hid_shape,
                   jax.ShapeDtypeStruct((1, d), F32)],
        scratch_shapes=[pltpu.VMEM((tm, d), BF16), pltpu.VMEM((tm, d), F32)],
        compiler_params=_cparams("arbitrary", "arbitrary"),
    )(dxo, xs, gain, gg_all, uu_all, wg_t, wu_t, wd)


def _final_loss(name, xs, gain, target, tm=512):
    rows, d = xs.shape
    tm = min(tm, rows)

    def body(x_ref, g_ref, t_ref, dx_ref, dg_ref, loss_ref):
        r, xh = _rms_parts(x_ref[...])
        gain_v = g_ref[...]
        err = xh * gain_v - t_ref[...]
        dy = err * (1.0 / d)
        dx_ref[...] = _rms_bwd_dx(dy, gain_v, r, xh)

        @pl.when(pl.program_id(0) == 0)
        def _():
            dg_ref[...] = jnp.zeros_like(dg_ref)
            loss_ref[...] = jnp.zeros_like(loss_ref)

        dg_ref[...] += jnp.sum(dy * xh, axis=0, keepdims=True)
        per_tok = jnp.mean(err * err, axis=-1, keepdims=True)
        loss_ref[...] += 0.5 * jnp.sum(per_tok, axis=0, keepdims=True)

    tile = pl.BlockSpec((tm, d), lambda i: (i, 0))
    row = pl.BlockSpec((1, d), lambda i: (0, 0))
    return pl.pallas_call(
        body, name=name, grid=(rows // tm,),
        in_specs=[tile, row, tile],
        out_specs=[tile, row, pl.BlockSpec((1, 1), lambda i: (0, 0))],
        out_shape=[jax.ShapeDtypeStruct((rows, d), F32), jax.ShapeDtypeStruct((1, d), F32),
                   jax.ShapeDtypeStruct((1, 1), F32)],
        compiler_params=_cparams("arbitrary"),
    )(xs, gain, target)


def _bucket_table():
    out = []
    for dil in DILATIONS:
        qi = np.arange(BLOCK)[:, None]
        kj = np.arange(2 * BLOCK)[None, :]
        dist = (np.maximum(qi + BLOCK - kj, 0) * dil).astype(np.int32)
        max_exact = N_BUCKETS // 2
        dd = np.maximum(dist, 1).astype(np.float32)
        large = max_exact + (np.log(dd / np.float32(max_exact)) / np.float32(math.log(MAX_DISTANCE / max_exact))
                             * np.float32(N_BUCKETS - max_exact)).astype(np.int32)
        large = np.minimum(large, N_BUCKETS - 1)
        out.append(np.where(dist < max_exact, dist, large).astype(np.int32))
    return np.stack(out)


def _bias_fwd(name, buckets, table):
    def body(bk_ref, tab_ref, o_ref):
        for g in range(N_GROUPS):
            bk = bk_ref[g]
            for h in range(HEADS_PER_GROUP):
                col = g * HEADS_PER_GROUP + h
                acc = jnp.zeros((BLOCK, 2 * BLOCK), F32)
                for b in range(N_BUCKETS):
                    acc = jnp.where(bk == b, tab_ref[b, col], acc)
                o_ref[col] = acc

    return pl.pallas_call(
        body, name=name,
        in_specs=[pl.BlockSpec(memory_space=pltpu.VMEM), pl.BlockSpec(memory_space=pltpu.SMEM)],
        out_specs=pl.BlockSpec(memory_space=pltpu.VMEM),
        out_shape=jax.ShapeDtypeStruct((N_GROUPS * HEADS_PER_GROUP, BLOCK, 2 * BLOCK), F32),
    )(buckets, table)


def _bias_bwd(name, buckets, dbias):
    def body(bk_ref, db_ref, o_ref):
        row_id = lax.broadcasted_iota(jnp.int32, (N_BUCKETS, 128), 0)
        col_id = lax.broadcasted_iota(jnp.int32, (N_BUCKETS, 128), 1)
        acc = jnp.zeros((N_BUCKETS, 128), F32)
        for g in range(N_GROUPS):
            bk = bk_ref[g]
            for h in range(HEADS_PER_GROUP):
                col = g * HEADS_PER_GROUP + h
                db = db_ref[col]
                for b in range(N_BUCKETS):
                    part = jnp.sum(jnp.where(bk == b, db, 0.0), axis=0, keepdims=True)
                    tot = jnp.sum(part, axis=1, keepdims=True)
                    acc = jnp.where((row_id == b) & (col_id == col), tot, acc)
        o_ref[...] = acc

    return pl.pallas_call(
        body, name=name,
        in_specs=[pl.BlockSpec(memory_space=pltpu.VMEM), pl.BlockSpec(memory_space=pltpu.VMEM)],
        out_specs=pl.BlockSpec(memory_space=pltpu.VMEM),
        out_shape=jax.ShapeDtypeStruct((N_BUCKETS, 128), F32),
    )(buckets, dbias)


def _band_mask(n):
    qi = lax.broadcasted_iota(jnp.int32, (BLOCK, 2 * BLOCK), 0)
    kj = lax.broadcasted_iota(jnp.int32, (BLOCK, 2 * BLOCK), 1)
    return (kj >= qi) & (kj <= qi + BLOCK) & ((kj >= BLOCK) | (n > 0))


def _attn_fwd(name, q, k, v, bias):
    dil, nh, m, dh = q.shape
    nb = m // BLOCK

    def body(q_ref, kc_ref, kp_ref, vc_ref, vp_ref, b_ref, o_ref, lse_ref):
        mask = _band_mask(pl.program_id(1))
        for h in range(nh):
            k2 = jnp.concatenate([kp_ref[0, h], kc_ref[0, h]], axis=0)
            v2 = jnp.concatenate([vp_ref[0, h], vc_ref[0, h]], axis=0)
            s = _dot_nt(q_ref[0, h], k2) + b_ref[h]
            s = jnp.where(mask, s, NEG_INF)
            mx = jnp.max(s, axis=-1, keepdims=True)
            p = jnp.exp(s - mx)
            den = jnp.sum(p, axis=-1, keepdims=True)
            o_ref[0, h] = _dot_nn(p.astype(BF16), v2) / den
            lse_ref[0, h] = jnp.broadcast_to(mx + jnp.log(den), (BLOCK, dh))

    cur = pl.BlockSpec((1, nh, BLOCK, dh), lambda r, n: (r, 0, n, 0))
    prev = pl.BlockSpec((1, nh, BLOCK, dh), lambda r, n: (r, 0, jnp.maximum(n - 1, 0), 0))
    return pl.pallas_call(
        body, name=name, grid=(dil, nb),
        in_specs=[cur, cur, prev, cur, prev, pl.BlockSpec((nh, BLOCK, 2 * BLOCK), lambda r, n: (0, 0, 0))],
        out_specs=[cur, cur],
        out_shape=[jax.ShapeDtypeStruct(q.shape, F32), jax.ShapeDtypeStruct(q.shape, F32)],
        compiler_params=_cparams("parallel", "arbitrary"),
    )(q, k, k, v, v, bias)


def _attn_bwd(name, q, k, v, do, lse, cvec, bias):
    dil, nh, m, dh = q.shape
    nb = m // BLOCK

    def body(q_ref, kc_ref, kp_ref, vc_ref, vp_ref, do_ref, lse_ref, c_ref, b_ref,
             dq_ref, dk_ref, dv_ref, db_ref, kcar_ref, vcar_ref):
        r, n = pl.program_id(0), pl.program_id(1)
        valid = n < nb
        mask = _band_mask(n) & valid

        @pl.when((r == 0) & (n == 0))
        def _():
            kcar_ref[...] = jnp.zeros_like(kcar_ref)
            vcar_ref[...] = jnp.zeros_like(vcar_ref)
            db_ref[...] = jnp.zeros_like(db_ref)

        for h in range(nh):
            qh = q_ref[0, h]
            k2 = jnp.concatenate([kp_ref[0, h], kc_ref[0, h]], axis=0)
            v2 = jnp.concatenate([vp_ref[0, h], vc_ref[0, h]], axis=0)
            doh = do_ref[0, h].astype(BF16)
            s = _dot_nt(qh, k2) + b_ref[h]
            p = jnp.where(mask, jnp.exp(s - lse_ref[0, h][:, :1]), 0.0)
            dp = _dot_nt(doh, v2)
            ds = p * (dp + c_ref[0, h][:, :1])
            ds_b = ds.astype(BF16)

            @pl.when(valid)
            def _():
                dq_ref[0, h] = _dot_nn(ds_b, k2)

            dk2 = _dot_tn(ds_b, qh)
            dv2 = _dot_tn(p.astype(BF16), doh)
            dk_ref[0, h] = kcar_ref[h] + dk2[:BLOCK]
            dv_ref[0, h] = vcar_ref[h] + dv2[:BLOCK]
            kcar_ref[h] = dk2[BLOCK:]
            vcar_ref[h] = dv2[BLOCK:]
            db_ref[h] += ds

    def qmap(r, n):
        return (r, 0, jnp.minimum(n, nb - 1), 0)

    def pmap(r, n):
        return (r, 0, jnp.maximum(jnp.minimum(n, nb - 1) - 1, 0), 0)

    def kvout(r, n):
        return (r, 0, jnp.maximum(n - 1, 0), 0)

    blk = (1, nh, BLOCK, dh)
    cur, prev = pl.BlockSpec(blk, qmap), pl.BlockSpec(blk, pmap)
    bias_spec = pl.BlockSpec((nh, BLOCK, 2 * BLOCK), lambda r, n: (0, 0, 0))
    full = jax.ShapeDtypeStruct(q.shape, F32)
    return pl.pallas_call(
        body, name=name, grid=(dil, nb + 1),
        in_specs=[cur, cur, prev, cur, prev, cur, cur, cur, bias_spec],
        out_specs=[cur, pl.BlockSpec(blk, kvout), pl.BlockSpec(blk, kvout), bias_spec],
        out_shape=[full, full, full, jax.ShapeDtypeStruct(bias.shape, F32)],
        scratch_shapes=[pltpu.VMEM((nh, BLOCK, dh), F32), pltpu.VMEM((nh, BLOCK, dh), F32)],
        compiler_params=_cparams("arbitrary", "arbitrary"),
    )(q, k, k, v, v, do, lse, cvec, bias)


def _group_weights(lses):
    mx = jnp.maximum(jnp.maximum(lses[0], lses[1]), lses[2])
    es = [jnp.exp(l - mx) for l in lses]
    den = es[0] + es[1] + es[2]
    return [e / den for e in es]


def _combine_fwd(name, os_, lses, tm=512):
    nh, rows, dh = os_[0].shape
    tm = min(tm, rows)

    def body(o0, o1, o2, l0, l1, l2, out_ref):
        ws = _group_weights([l0[...], l1[...], l2[...]])
        out_ref[...] = ws[0] * o0[...] + ws[1] * o1[...] + ws[2] * o2[...]

    spec = pl.BlockSpec((nh, tm, dh), lambda i: (0, i, 0))
    return pl.pallas_call(
        body, name=name, grid=(rows // tm,), in_specs=[spec] * 6, out_specs=spec,
        out_shape=jax.ShapeDtypeStruct((nh, rows, dh), F32),
        compiler_params=_cparams("parallel"),
    )(*os_, *lses)


def _combine_bwd(name, do, oa, lses, tm=512):
    nh, rows, dh = do.shape
    tm = min(tm, rows)

    def body(do_ref, oa_ref, l0, l1, l2, d0, d1, d2, c0, c1, c2):
        ws = _group_weights([l0[...], l1[...], l2[...]])
        dov = do_ref[...]
        bar = jnp.sum(dov * oa_ref[...], axis=-1, keepdims=True)
        for w, d_ref, c_ref in zip(ws, (d0, d1, d2), (c0, c1, c2)):
            d_ref[...] = w * dov
            c_ref[...] = -w * bar

    spec = pl.BlockSpec((nh, tm, dh), lambda i: (0, i, 0))
    shape = jax.ShapeDtypeStruct((nh, rows, dh), F32)
    return pl.pallas_call(
        body, name=name, grid=(rows // tm,), in_specs=[spec] * 5, out_specs=[spec] * 6,
        out_shape=[shape] * 6, compiler_params=_cparams("parallel"),
    )(do, oa, *lses)


def _ssm_disc(a_re, a_im, log_dt, b_re, b_im):
    dt = jnp.exp(log_dt)
    mag = jnp.exp(a_re * dt)
    ab_re = mag * jnp.cos(a_im * dt)
    ab_im = mag * jnp.sin(a_im * dt)
    den = a_re * a_re + a_im * a_im
    xr = ab_re - 1.0
    coef_re = (xr * a_re + ab_im * a_im) / den
    coef_im = (ab_im * a_re - xr * a_im) / den
    bb_re = coef_re[None] * b_re - coef_im[None] * b_im
    bb_im = coef_re[None] * b_im + coef_im[None] * b_re
    return ab_re, ab_im, bb_re, bb_im


def _cpow2(re, im, times):
    for _ in range(times):
        re, im = re * re - im * im, 2.0 * re * im
    return re, im


def _ssm_params_fwd(name, a_re, a_im, log_dt, b_re, b_im):
    gn = jax.ShapeDtypeStruct(a_re.shape, F32)
    cgn = jax.ShapeDtypeStruct(b_re.shape, F32)

    def body(ar, ai, ld, br, bi, o_abr, o_abi, o_apr, o_api, o_bbr, o_bbi):
        ab_re, ab_im, bb_re, bb_im = _ssm_disc(ar[...], ai[...], ld[...], br[...], bi[...])
        o_abr[...] = ab_re
        o_abi[...] = ab_im
        pr, pi = _cpow2(ab_re, ab_im, int(math.log2(SCAN_STEPS)))
        o_apr[...] = pr
        o_api[...] = pi
        o_bbr[...] = bb_re
        o_bbi[...] = bb_im

    vm = pl.BlockSpec(memory_space=pltpu.VMEM)
    return pl.pallas_call(body, name=name, in_specs=[vm] * 5, out_specs=[vm] * 6,
                          out_shape=[gn, gn, gn, gn, cgn, cgn])(a_re, a_im, log_dt, b_re, b_im)


def _ssm_params_bwd(name, a_re, a_im, log_dt, b_re, b_im, d_ab_re, d_ab_im, d_bb_re, d_bb_im):
    gn = jax.ShapeDtypeStruct(a_re.shape, F32)
    cgn = jax.ShapeDtypeStruct(b_re.shape, F32)

    def body(ar, ai, ld, br, bi, g0, g1, g2, g3, o_ar, o_ai, o_ld, o_br, o_bi):
        _, vjp = jax.vjp(_ssm_disc, ar[...], ai[...], ld[...], br[...], bi[...])
        outs = vjp((g0[...], g1[...], g2[...], g3[...]))
        for o_ref, o in zip((o_ar, o_ai, o_ld, o_br, o_bi), outs):
            o_ref[...] = o

    vm = pl.BlockSpec(memory_space=pltpu.VMEM)
    return pl.pallas_call(body, name=name, in_specs=[vm] * 9, out_specs=[vm] * 5,
                          out_shape=[gn, gn, jax.ShapeDtypeStruct(log_dt.shape, F32), cgn, cgn],
                          )(a_re, a_im, log_dt, b_re, b_im, d_ab_re, d_ab_im, d_bb_re, d_bb_im)


def _scan_block(s_ref, carry_ref, tmp_ref, ab_ref, ap_ref, reverse, sprev=None):
    nl = SSM_LANES
    for lc in range(nl // SCAN_LANES):
        re_l = pl.ds(lc * SCAN_LANES, SCAN_LANES)
        im_l = pl.ds(nl + lc * SCAN_LANES, SCAN_LANES)
        are, aim = ab_ref[:, re_l], ab_ref[:, im_l]

        def rows_of(j):
            jj = SCAN_STEPS - 1 - j if reverse else j
            return pl.ds(pl.multiple_of(jj * SCAN_SUB, SCAN_SUB), SCAN_SUB)

        def pass1(j, st):
            sr, si = st
            rows = rows_of(j)
            nr = are * sr - aim * si + s_ref[rows, re_l]
            ni = are * si + aim * sr + s_ref[rows, im_l]
            s_ref[rows, re_l] = nr
            s_ref[rows, im_l] = ni
            return nr, ni

        zero = jnp.zeros((SCAN_SUB, SCAN_LANES), F32)
        er, ei = lax.fori_loop(0, SCAN_STEPS, pass1, (zero, zero), unroll=2)
        tmp_ref[0:SCAN_SUB, re_l] = er
        tmp_ref[0:SCAN_SUB, im_l] = ei
        apr, api = ap_ref[0:1, re_l], ap_ref[0:1, im_l]
        sr, si = carry_ref[0:1, re_l], carry_ref[0:1, im_l]
        for step in range(SCAN_SUB):
            c = SCAN_SUB - 1 - step if reverse else step
            tmp_ref[SCAN_SUB + c:SCAN_SUB + c + 1, re_l] = sr
            tmp_ref[SCAN_SUB + c:SCAN_SUB + c + 1, im_l] = si
            e_r, e_i = tmp_ref[c:c + 1, re_l], tmp_ref[c:c + 1, im_l]
            sr, si = apr * sr - api * si + e_r, apr * si + api * sr + e_i
        carry_ref[0:1, re_l] = sr
        carry_ref[0:1, im_l] = si
        cr = tmp_ref[SCAN_SUB:2 * SCAN_SUB, re_l]
        ci = tmp_ref[SCAN_SUB:2 * SCAN_SUB, im_l]

        if sprev is None:
            def pass2(j, st):
                pr, pi = st
                rows = rows_of(j)
                s_ref[rows, re_l] += pr * cr - pi * ci
                s_ref[rows, im_l] += pr * ci + pi * cr
                return pr * are - pi * aim, pr * aim + pi * are

            lax.fori_loop(0, SCAN_STEPS, pass2, (are, aim), unroll=2)
        else:
            st_ref, prev_ref, have_prev, dab_ref = sprev

            def corrected(j, pr, pi):
                rows = rows_of(j)
                gr = s_ref[rows, re_l] + pr * cr - pi * ci
                gi = s_ref[rows, im_l] + pr * ci + pi * cr
                s_ref[rows, re_l] = gr
                s_ref[rows, im_l] = gi
                return gr, gi

            def pass2(j, st):
                pr, pi, dr, di = st
                gr, gi = corrected(j, pr, pi)
                before = pl.ds(pl.multiple_of((SCAN_STEPS - 2 - j) * SCAN_SUB, SCAN_SUB), SCAN_SUB)
                qr, qi = st_ref[before, re_l], st_ref[before, im_l]
                return (pr * are - pi * aim, pr * aim + pi * are,
                        dr + gr * qr + gi * qi, di + gi * qr - gr * qi)

            pr, pi, dr, di = lax.fori_loop(0, SCAN_STEPS - 1, pass2, (are, aim, zero, zero), unroll=2)
            gr, gi = corrected(SCAN_STEPS - 1, pr, pi)
            last = pl.ds((SCAN_STEPS - 1) * SCAN_SUB, SCAN_SUB)
            sub = lax.broadcasted_iota(jnp.int32, (SCAN_SUB, SCAN_LANES), 0)
            pv_r = jnp.broadcast_to(prev_ref[SCAN_SUB - 1:SCAN_SUB, re_l], (SCAN_SUB, SCAN_LANES)) * have_prev
            pv_i = jnp.broadcast_to(prev_ref[SCAN_SUB - 1:SCAN_SUB, im_l], (SCAN_SUB, SCAN_LANES)) * have_prev
            qr = jnp.where(sub == 0, pv_r, pltpu.roll(st_ref[last, re_l], 1, 0))
            qi = jnp.where(sub == 0, pv_i, pltpu.roll(st_ref[last, im_l], 1, 0))
            dab_ref[:, re_l] += dr + gr * qr + gi * qi
            dab_ref[:, im_l] += di + gi * qr - gr * qi


def _ssm_fwd(name, u_perm, bb_mat, c_mat, ab_rows, ap_rows, d_skip):
    rows = u_perm.shape[0]
    nl2 = 2 * SSM_LANES

    def body(u_ref, bb_ref, c_ref, ab_ref, ap_ref, d_ref, y_ref, s_ref, carry_ref, tmp_ref):
        @pl.when(pl.program_id(0) == 0)
        def _():
            carry_ref[...] = jnp.zeros_like(carry_ref)

        uv = u_ref[...]
        s_ref[...] = _dot_nn(uv.astype(BF16), bb_ref[...])
        _scan_block(s_ref, carry_ref, tmp_ref, ab_ref, ap_ref, reverse=False)
        y_ref[...] = _dot_nn(s_ref[...].astype(BF16), c_ref[...]) + d_ref[...] * uv

    const = lambda shape: pl.BlockSpec(shape, lambda i: (0, 0))
    return pl.pallas_call(
        body, name=name, grid=(rows // SCAN_BLOCK,),
        in_specs=[pl.BlockSpec((SCAN_BLOCK, SSM_WIDTH), lambda i: (i, 0)), const((SSM_WIDTH, nl2)),
                  const((nl2, SSM_WIDTH)), const((SCAN_SUB, nl2)), const((SCAN_SUB, nl2)), const((1, SSM_WIDTH))],
        out_specs=[pl.BlockSpec((SCAN_BLOCK, SSM_WIDTH), lambda i: (i, 0)),
                   pl.BlockSpec((SCAN_BLOCK, nl2), lambda i: (i, 0))],
        out_shape=[jax.ShapeDtypeStruct((rows, SSM_WIDTH), F32), jax.ShapeDtypeStruct((rows, nl2), F32)],
        scratch_shapes=[pltpu.VMEM((SCAN_SUB, nl2), F32), pltpu.VMEM((2 * SCAN_SUB, nl2), F32)],
        compiler_params=_cparams("arbitrary"),
    )(u_perm, bb_mat, c_mat, ab_rows, ap_rows, d_skip)


def _ssm_bwd(name, dy_perm, u_perm, states, c_mat_t, bb_mat_t, abc_rows, apc_rows, d_skip):
    rows = u_perm.shape[0]
    nl2 = 2 * SSM_LANES
    nblk = rows // SCAN_BLOCK

    def body(dy_ref, u_ref, st_ref, prev_ref, ct_ref, bt_ref, ab_ref, ap_ref, d_ref,
             du_ref, g_ref, dab_ref, dd_ref, carry_ref, tmp_ref):
        i = pl.program_id(0)

        @pl.when(i == 0)
        def _():
            carry_ref[...] = jnp.zeros_like(carry_ref)
            dab_ref[...] = jnp.zeros_like(dab_ref)
            dd_ref[...] = jnp.zeros_like(dd_ref)

        dyv = dy_ref[...]
        g_ref[...] = _dot_nn(dyv.astype(BF16), ct_ref[...])
        have_prev = (i < nblk - 1).astype(F32)
        _scan_block(g_ref, carry_ref, tmp_ref, ab_ref, ap_ref, reverse=True,
                    sprev=(st_ref, prev_ref, have_prev, dab_ref))
        du_ref[...] = _dot_nn(g_ref[...].astype(BF16), bt_ref[...]) + d_ref[...] * dyv
        dd_ref[...] += jnp.sum(dyv * u_ref[...], axis=0, keepdims=True)

    const = lambda shape: pl.BlockSpec(shape, lambda i: (0, 0))
    blk = lambda cols: pl.BlockSpec((SCAN_BLOCK, cols), lambda i: (nblk - 1 - i, 0))
    per8 = SCAN_BLOCK // SCAN_SUB
    prev_spec = pl.BlockSpec((SCAN_SUB, nl2), lambda i: (jnp.maximum((nblk - 1 - i) * per8 - 1, 0), 0))
    return pl.pallas_call(
        body, name=name, grid=(nblk,),
        in_specs=[blk(SSM_WIDTH), blk(SSM_WIDTH), blk(nl2), prev_spec, const((SSM_WIDTH, nl2)),
                  const((nl2, SSM_WIDTH)), const((SCAN_SUB, nl2)), const((SCAN_SUB, nl2)), const((1, SSM_WIDTH))],
        out_specs=[blk(SSM_WIDTH), blk(nl2), const((SCAN_SUB, nl2)), const((1, SSM_WIDTH))],
        out_shape=[jax.ShapeDtypeStruct((rows, SSM_WIDTH), F32), jax.ShapeDtypeStruct((rows, nl2), F32),
                   jax.ShapeDtypeStruct((SCAN_SUB, nl2), F32), jax.ShapeDtypeStruct((1, SSM_WIDTH), F32)],
        scratch_shapes=[pltpu.VMEM((SCAN_SUB, nl2), F32), pltpu.VMEM((2 * SCAN_SUB, nl2), F32)],
        compiler_params=_cparams("arbitrary"),
    )(dy_perm, u_perm, states, states, c_mat_t, bb_mat_t, abc_rows, apc_rows, d_skip)


def _scan_order(a):
    rows, cols = a.shape
    return a.reshape(rows // SCAN_BLOCK, SCAN_SUB, SCAN_STEPS, cols).transpose(0, 2, 1, 3).reshape(rows, cols)


def _time_order(a):
    rows, cols = a.shape
    return a.reshape(rows // SCAN_BLOCK, SCAN_STEPS, SCAN_SUB, cols).transpose(0, 2, 1, 3).reshape(rows, cols)


def _adamw(name, w, m, v, gparts, tr):
    rows, cols = w.shape

    def body(w_ref, m_ref, v_ref, g_ref, og_ref, od_ref, om_ref, ov_ref):
        g = g_ref[0].astype(F32)
        for i in range(1, N_DEV):
            g = g + g_ref[i].astype(F32)
        m_new = B1 * m_ref[...] + (1.0 - B1) * g
        v_new = B2 * v_ref[...] + (1.0 - B2) * (g * g)
        m_hat = m_new / (1.0 - B1 ** STEP)
        v_hat = v_new / (1.0 - B2 ** STEP)
        og_ref[...] = g
        od_ref[...] = -LR * (m_hat / (jnp.sqrt(v_hat) + ADAM_EPS) + WD * w_ref[...])
        om_ref[...] = m_new
        ov_ref[...] = v_new

    spec = pl.BlockSpec((tr, cols), lambda i: (i, 0))
    shape = jax.ShapeDtypeStruct((rows, cols), F32)
    return pl.pallas_call(
        body, name=name, grid=(rows // tr,),
        in_specs=[spec, spec, spec, pl.BlockSpec((N_DEV, tr, cols), lambda i: (0, i, 0))],
        out_specs=[spec] * 4, out_shape=[shape] * 4,
        compiler_params=_cparams("parallel"),
    )(w, m, v, gparts)


_SHARDED = (
    ("ffn1_w_gate", True, (352, 1024)), ("ffn1_w_up", True, (352, 1024)), ("ffn1_w_down", False, (352, 1024)),
    ("w_in", True, (608, 1024)), ("ssm_w_glu", True, (128, 512)), ("w_attn_branch", True, (128, 256)),
    ("w_ssm_branch", True, (128, 512)), ("w_out", False, (128, 1024)),
    ("ffn2_w_gate", True, (352, 1024)), ("ffn2_w_up", True, (352, 1024)), ("ffn2_w_down", False, (352, 1024)),
)
_SMALL = ("ffn1_norm", "mix_norm", "gate_bias", "rel_bias_table", "ssm_a_re", "ssm_a_im", "ssm_log_dt",
          "ssm_b_re", "ssm_b_im", "ssm_c_re", "ssm_c_im", "ssm_d", "ffn2_norm", "final_norm")
_ORDER = ("ffn1_norm", "ffn1_w_gate", "ffn1_w_up", "ffn1_w_down", "mix_norm", "w_in", "gate_bias",
          "rel_bias_table", "ssm_a_re", "ssm_a_im", "ssm_log_dt", "ssm_b_re", "ssm_b_im", "ssm_c_re",
          "ssm_c_im", "ssm_d", "ssm_w_glu", "w_attn_branch", "w_ssm_branch", "w_out", "ffn2_norm",
          "ffn2_w_gate", "ffn2_w_up", "ffn2_w_down", "final_norm")


def _pack_rows(shape):
    return shape[0] * shape[1] // D_MODEL


_SHARD_INFO = {nm: (tr, shape) for nm, tr, shape in _SHARDED}
_PHASES = {
    "f1gu": ("ffn1_w_gate", "ffn1_w_up"), "f1d": ("ffn1_w_down",),
    "mix": ("w_in", "ssm_w_glu", "w_attn_branch", "w_ssm_branch", "w_out"),
    "f2": ("ffn2_w_gate", "ffn2_w_up", "ffn2_w_down"),
}


def _pack_sharded(ws, names):
    parts = []
    for nm in names:
        tr, shape = _SHARD_INFO[nm]
        a = ws[nm].T if tr else ws[nm]
        parts.append(a.reshape(_pack_rows(shape), D_MODEL))
    return jnp.concatenate(parts, axis=0)


def _unpack_sharded(pack, names):
    out, r0 = {}, 0
    for nm in names:
        tr, shape = _SHARD_INFO[nm]
        n = _pack_rows(shape)
        a = pack[r0:r0 + n].reshape(shape)
        out[nm] = a.T if tr else a
        r0 += n
    return out


def _unpack_gathered(gath, names):
    out, r0 = {}, 0
    for nm in names:
        _, shape = _SHARD_INFO[nm]
        n = _pack_rows(shape)
        out[nm] = gath[:, r0:r0 + n].reshape(N_DEV * shape[0], shape[1])
        r0 += n
    return out


def _pack_grads(gs, names):
    parts = []
    for nm in names:
        _, shape = _SHARD_INFO[nm]
        parts.append(gs[nm].astype(BF16).reshape(N_DEV, _pack_rows(shape), D_MODEL))
    return jnp.concatenate(parts, axis=1)


def _pack_small(ws):
    flat = jnp.concatenate([ws[nm].reshape(-1) for nm in _SMALL])
    pad = (-flat.shape[0]) % (8 * 128)
    return jnp.pad(flat, (0, pad)).reshape(-1, 128)


def _unpack_small(pack, like):
    flat, out, p0 = pack.reshape(-1), {}, 0
    for nm in _SMALL:
        n = like[nm].size
        out[nm] = flat[p0:p0 + n].reshape(like[nm].shape)
        p0 += n
    return out


def _to_dilated(a, dil):
    rows = a.shape[0]
    return a.reshape(rows // dil, dil, HEADS_PER_GROUP, HEAD_DIM).transpose(1, 2, 0, 3)


def _dilated_to_heads(a):
    dil, nh, m, dh = a.shape
    return a.transpose(1, 2, 0, 3).reshape(nh, m * dil, dh)


def _heads_to_dilated(a, dil):
    nh, rows, dh = a.shape
    return a.reshape(nh, rows // dil, dil, dh).transpose(2, 0, 1, 3)


def _block_diag(blocks_gab):
    g, a, b = blocks_gab.shape
    eye = jnp.eye(g, dtype=blocks_gab.dtype)
    return (blocks_gab[:, :, None, :] * eye[:, None, :, None]).reshape(g * a, g * b)


def _diag_blocks(mat, a, b):
    g = mat.shape[0] // a
    eye = jnp.eye(g, dtype=mat.dtype)
    return jnp.einsum("gahb,gh->gab", mat.reshape(g, a, g, b), eye)


def _local_step(xs, target, small, weights_of, send_grads, first_deps=()):
    rows = xs.shape[0]
    gfull, gsmall = {}, {}
    wf = dict(weights_of("f1", None))

    x1, h1, gg1, uu1 = _ffn_fwd("ffn1_fwd", xs, small["ffn1_norm"], wf["ffn1_w_gate"], wf["ffn1_w_up"],
                                wf["ffn1_w_down"], deps=first_deps)
    wf.update(weights_of("mix", x1))
    hmix = _rms_fwd("mix_norm_fwd", x1, small["mix_norm"])
    w_in = wf["w_in"]
    w_qkv, w_u, w_g = w_in[:3 * ATTN_WIDTH], w_in[3 * ATTN_WIDTH:3 * ATTN_WIDTH + SSM_WIDTH], w_in[3 * ATTN_WIDTH + SSM_WIDTH:]
    qscale = jnp.concatenate([jnp.full((1, ATTN_WIDTH), HEAD_DIM ** -0.5, F32), jnp.ones((1, 2 * ATTN_WIDTH), F32)], axis=1)
    qkv, = _mm("in_qkv", [(hmix, w_qkv)], True, 3 * ATTN_WIDTH, [BF16],
               epilogue=lambda acc, sc: (acc * sc,), extras=[(qscale, 0)], tn=ATTN_WIDTH)
    u, = _mm("in_u", [(hmix, w_u)], True, SSM_WIDTH, [F32])
    gates, = _mm("in_gates", [(hmix, w_g)], True, 2 * D_MODEL, [F32],
                 epilogue=lambda acc, b: (_sigmoid(acc + b),), extras=[(small["gate_bias"], 0)])

    buckets = jnp.asarray(_bucket_table())
    bias = _bias_fwd("rel_bias_fwd", buckets, small["rel_bias_table"])
    qkv_d, o_h, lse_h = [], [], []
    for g, dil in enumerate(DILATIONS):
        cols = [qkv[:, s * ATTN_WIDTH + g * ATTN_OUT:s * ATTN_WIDTH + (g + 1) * ATTN_OUT] for s in range(3)]
        qd, kd, vd = [_to_dilated(c, dil) for c in cols]
        bias_g = bias[g * HEADS_PER_GROUP:(g + 1) * HEADS_PER_GROUP]
        o_g, lse_g = _attn_fwd(f"attn_fwd_{g}", qd, kd, vd, bias_g)
        qkv_d.append((qd, kd, vd, bias_g))
        o_h.append(_dilated_to_heads(o_g))
        lse_h.append(_dilated_to_heads(lse_g))
    oa_h = _combine_fwd("attn_combine_fwd", o_h, lse_h)
    oa = oa_h.transpose(1, 0, 2).reshape(rows, ATTN_OUT).astype(BF16)
    y_attn, = _mm("attn_branch", [(oa, wf["w_attn_branch"])], True, D_MODEL, [F32])

    ab_re, ab_im, ap_re, ap_im, bb_re, bb_im = _ssm_params_fwd(
        "ssm_params_fwd", small["ssm_a_re"], small["ssm_a_im"], small["ssm_log_dt"].reshape(SSM_GROUPS, 1),
        small["ssm_b_re"].transpose(2, 0, 1), small["ssm_b_im"].transpose(2, 0, 1))

    def lanes(re, im, sign=1.0):
        row = jnp.concatenate([re.reshape(1, SSM_LANES), sign * im.reshape(1, SSM_LANES)], axis=1)
        return jnp.broadcast_to(row, (SCAN_SUB, 2 * SSM_LANES))

    bb_mat = jnp.concatenate([_block_diag(bb_re.transpose(1, 0, 2)), _block_diag(bb_im.transpose(1, 0, 2))], axis=1)
    c_mat_t = jnp.concatenate([_block_diag(small["ssm_c_re"]), -_block_diag(small["ssm_c_im"])], axis=1)
    bb_mat, c_mat_t = bb_mat.astype(BF16), c_mat_t.astype(BF16)
    d_skip = small["ssm_d"].reshape(1, SSM_WIDTH)
    u_perm = _scan_order(u)
    y_perm, states = _ssm_fwd("ssm_fwd", u_perm, bb_mat, c_mat_t.T, lanes(ab_re, ab_im), lanes(ap_re, ap_im), d_skip)
    y_raw = _time_order(y_perm)

    def gelu_fn(yv):
        return (jax.nn.gelu(yv),)

    ygelu, = _ew("ssm_gelu", gelu_fn, [y_raw], [SSM_WIDTH], [BF16])
    glu, = _mm("ssm_glu", [(ygelu, wf["ssm_w_glu"])], True, 2 * SSM_WIDTH, [F32])
    ysg, = _ew("ssm_glu_act", lambda gv: (gv[:, :SSM_WIDTH] * _sigmoid(gv[:, SSM_WIDTH:]),), [glu], [SSM_WIDTH], [BF16])
    y_ssm, merged = _mm("ssm_branch_merge", [(ysg, wf["w_ssm_branch"])], True, D_MODEL, [F32, BF16],
                        epilogue=lambda acc, ga, gs, ya: (acc, ga * ya + gs * acc),
                        extras=[(gates, 0), (gates, D_MODEL), (y_attn, 0)])
    x2, = _mm("mix_out", [(merged, wf["w_out"])], False, D_MODEL, [F32],
              epilogue=lambda acc, res: (res + acc,), extras=[(x1, 0)])
    wf.update(weights_of("f2", x2))
    x3, h2, gg2, uu2 = _ffn_fwd("ffn2_fwd", x2, small["ffn2_norm"], wf["ffn2_w_gate"], wf["ffn2_w_up"],
                                wf["ffn2_w_down"])
    dx3, gsmall["final_norm"], loss = _final_loss("final_loss", x3, small["final_norm"].reshape(1, D_MODEL), target)

    dx2, dgg2, duu2, act2, gsmall["ffn2_norm"] = _ffn_bwd(
        "ffn2_bwd", dx3, x2, small["ffn2_norm"], gg2, uu2, wf["ffn2_w_gate"], wf["ffn2_w_up"], wf["ffn2_w_down"])
    gfull["ffn2_w_gate"] = _mm_tn("ffn2_dwg", dgg2, h2)
    gfull["ffn2_w_up"] = _mm_tn("ffn2_dwu", duu2, h2)
    gfull["ffn2_w_down"] = _mm_tn("ffn2_dwd", act2, dx3, scale=0.5)
    sent = send_grads("f2", gfull)

    def merge_bwd(dm, ga, gs, ya, ys):
        return (dm * ga, dm * gs, dm * ya * ga * (1.0 - ga), dm * ys * gs * (1.0 - gs))

    dya, dys, dzga, dzgs = _mm("mix_out_bwd", [(dx2, wf["w_out"])], True, D_MODEL, [BF16] * 4, epilogue=merge_bwd,
                               extras=[(gates, 0), (gates, D_MODEL), (y_attn, 0), (y_ssm, 0)], deps=sent)
    gfull["w_out"] = _mm_tn("dw_out", merged, dx2)
    gsmall["gate_bias"] = jnp.concatenate([_colsum("dgate_bias_a", dzga), _colsum("dgate_bias_s", dzgs)], axis=1)

    gfull["w_ssm_branch"] = _mm_tn("dw_ssm_branch", dys, ysg)

    def glu_bwd(dysg, av, bv):
        sb = _sigmoid(bv)
        return (dysg * sb, dysg * av * sb * (1.0 - sb))

    dglu_a, dglu_b = _mm("ssm_branch_bwd", [(dys, wf["w_ssm_branch"])], False, SSM_WIDTH, [BF16, BF16],
                         epilogue=glu_bwd, extras=[(glu, 0), (glu, SSM_WIDTH)])
    w_glu = wf["ssm_w_glu"]
    gfull["ssm_w_glu"] = jnp.concatenate([_mm_tn("dw_glu_a", dglu_a, ygelu), _mm_tn("dw_glu_b", dglu_b, ygelu)], axis=0)

    def gelu_bwd(acc, yv):
        _, vjp = jax.vjp(jax.nn.gelu, yv)
        return (vjp(acc)[0],)

    dy_raw, = _mm("ssm_glu_bwd", [(dglu_a, w_glu[:SSM_WIDTH]), (dglu_b, w_glu[SSM_WIDTH:])], False, SSM_WIDTH, [F32],
                  epilogue=gelu_bwd, extras=[(y_raw, 0)])
    dy_perm = _scan_order(dy_raw)
    du_perm, g_states, dab_rows, gsmall_d = _ssm_bwd(
        "ssm_bwd", dy_perm, u_perm, states, c_mat_t, bb_mat.T, lanes(ab_re, ab_im, -1.0), lanes(ap_re, ap_im, -1.0), d_skip)
    du = _time_order(du_perm)
    gsmall["ssm_d"] = gsmall_d
    dbb_acc = _mm_tn("ssm_dbb", u_perm, g_states, bm=SSM_WIDTH)
    dc_acc = _mm_tn("ssm_dc", dy_perm, states, bm=SSM_WIDTH)
    dbb_re = _diag_blocks(dbb_acc[:, :SSM_LANES], SSM_GROUP, SSM_STATE).transpose(1, 0, 2)
    dbb_im = _diag_blocks(dbb_acc[:, SSM_LANES:], SSM_GROUP, SSM_STATE).transpose(1, 0, 2)
    gsmall["ssm_c_re"] = _diag_blocks(dc_acc[:, :SSM_LANES], SSM_GROUP, SSM_STATE)
    gsmall["ssm_c_im"] = -_diag_blocks(dc_acc[:, SSM_LANES:], SSM_GROUP, SSM_STATE)
    dab = _colsum("ssm_dab", dab_rows)
    d_ar, d_ai, d_ld, d_br, d_bi = _ssm_params_bwd(
        "ssm_params_bwd", small["ssm_a_re"], small["ssm_a_im"], small["ssm_log_dt"].reshape(SSM_GROUPS, 1),
        small["ssm_b_re"].transpose(2, 0, 1), small["ssm_b_im"].transpose(2, 0, 1),
        dab[:, :SSM_LANES].reshape(SSM_GROUPS, SSM_STATE), dab[:, SSM_LANES:].reshape(SSM_GROUPS, SSM_STATE),
        dbb_re, dbb_im)
    gsmall["ssm_a_re"], gsmall["ssm_a_im"], gsmall["ssm_log_dt"] = d_ar, d_ai, d_ld.reshape(SSM_GROUPS)
    gsmall["ssm_b_re"], gsmall["ssm_b_im"] = d_br.transpose(1, 2, 0), d_bi.transpose(1, 2, 0)

    gfull["w_attn_branch"] = _mm_tn("dw_attn_branch", dya, oa)
    doa, = _mm("attn_branch_bwd", [(dya, wf["w_attn_branch"])], False, ATTN_OUT, [F32])
    do_h = doa.reshape(rows, HEADS_PER_GROUP, HEAD_DIM).transpose(1, 0, 2)
    dc = _combine_bwd("attn_combine_bwd", do_h, oa_h, lse_h)
    dqkv_cols = [None] * 9
    dbias = []
    for g, dil in enumerate(DILATIONS):
        qd, kd, vd, bias_g = qkv_d[g]
        dq, dk, dv, db = _attn_bwd(f"attn_bwd_{g}", qd, kd, vd, _heads_to_dilated(dc[g], dil),
                                   _heads_to_dilated(lse_h[g], dil), _heads_to_dilated(dc[3 + g], dil), bias_g)
        dbias.append(db)
        for s, (arr, sc) in enumerate(((dq, HEAD_DIM ** -0.5), (dk, 1.0), (dv, 1.0))):
            tok = _dilated_to_heads(arr).transpose(1, 0, 2).reshape(rows, ATTN_OUT)
            dqkv_cols[3 * s + g] = (tok * sc).astype(BF16)
    dqkv = jnp.concatenate(dqkv_cols, axis=1)
    gsmall["rel_bias_table"] = _bias_bwd("rel_bias_bwd", buckets, jnp.concatenate(dbias, axis=0))[:, :N_GROUPS * HEADS_PER_GROUP]

    gfull["w_in"] = jnp.concatenate([
        _mm_tn("dw_in_qkv", dqkv, hmix), _mm_tn("dw_in_u", du, hmix),
        _mm_tn("dw_in_ga", dzga, hmix), _mm_tn("dw_in_gs", dzgs, hmix)], axis=0)
    sent = send_grads("mix", gfull)
    dhmix, = _mm("in_bwd", [(dqkv, w_qkv), (du, w_u), (dzga, w_g[:D_MODEL]), (dzgs, w_g[D_MODEL:])], False, D_MODEL,
                 [F32], tm=512, deps=sent)
    dx1, gsmall["mix_norm"] = _rms_bwd("mix_norm_bwd", dhmix, x1, small["mix_norm"], dx2)

    dx, dgg1, duu1, act1, gsmall["ffn1_norm"] = _ffn_bwd(
        "ffn1_bwd", dx1, xs, small["ffn1_norm"], gg1, uu1, wf["ffn1_w_gate"], wf["ffn1_w_up"], wf["ffn1_w_down"])
    gfull["ffn1_w_down"] = _mm_tn("ffn1_dwd", act1, dx1, scale=0.5)
    sent = send_grads("f1d", gfull)
    gfull["ffn1_w_gate"] = _mm_tn("ffn1_dwg", dgg1, h1, deps=sent)
    gfull["ffn1_w_up"] = _mm_tn("ffn1_dwu", duu1, h1)
    send_grads("f1gu", gfull)
    return loss[0, 0], dx, gsmall


def kernel(x, ffn1_norm, ffn1_w_gate, ffn1_w_up, ffn1_w_down, mix_norm, w_in, gate_bias, rel_bias_table, ssm_a_re, ssm_a_im, ssm_log_dt, ssm_b_re, ssm_b_im, ssm_c_re, ssm_c_im, ssm_d, ssm_w_glu, w_attn_branch, w_ssm_branch, w_out, ffn2_norm, ffn2_w_gate, ffn2_w_up, ffn2_w_down, final_norm, loss_target, m_ffn1_norm, m_ffn1_w_gate, m_ffn1_w_up, m_ffn1_w_down, m_mix_norm, m_w_in, m_gate_bias, m_rel_bias_table, m_ssm_a_re, m_ssm_a_im, m_ssm_log_dt, m_ssm_b_re, m_ssm_b_im, m_ssm_c_re, m_ssm_c_im, m_ssm_d, m_ssm_w_glu, m_w_attn_branch, m_w_ssm_branch, m_w_out, m_ffn2_norm, m_ffn2_w_gate, m_ffn2_w_up, m_ffn2_w_down, m_final_norm, v_ffn1_norm, v_ffn1_w_gate, v_ffn1_w_up, v_ffn1_w_down, v_mix_norm, v_w_in, v_gate_bias, v_rel_bias_table, v_ssm_a_re, v_ssm_a_im, v_ssm_log_dt, v_ssm_b_re, v_ssm_b_im, v_ssm_c_re, v_ssm_c_im, v_ssm_d, v_ssm_w_glu, v_w_attn_branch, v_w_ssm_branch, v_w_out, v_ffn2_norm, v_ffn2_w_gate, v_ffn2_w_up, v_ffn2_w_down, v_final_norm):
    given = dict(locals())
    shapes = {nm: given[nm].shape for nm in _ORDER}

    def strip(a):
        return a[0] if a.ndim >= 2 and a.shape[0] == 1 else a

    w = {nm: strip(given[nm]) for nm in _ORDER}
    m = {nm: strip(given["m_" + nm]) for nm in _ORDER}
    v = {nm: strip(given["v_" + nm]) for nm in _ORDER}
    for d in (w, m, v):
        d["rel_bias_table"] = d["rel_bias_table"].reshape(N_BUCKETS, N_GROUPS * HEADS_PER_GROUP)

    small = {nm: w[nm] for nm in _SMALL}
    small_in = dict(small)
    for nm in ("ffn1_norm", "mix_norm", "ffn2_norm", "gate_bias"):
        small_in[nm] = small[nm].reshape(1, -1)
    w_pack = {ph: _pack_sharded(w, names) for ph, names in _PHASES.items()}

    f1_names = _PHASES["f1gu"] + _PHASES["f1d"]
    got_f1 = _all_gather("gather_f1", jnp.concatenate([w_pack["f1gu"], w_pack["f1d"]], axis=0).astype(BF16))
    pending_w = {"mix": _exchange_start("gather_mix_start", w_pack["mix"].astype(BF16), gather=True, deps=[got_f1])}
    pending_w["f2"] = _exchange_start("gather_f2_start", w_pack["f2"].astype(BF16), gather=True,
                                      deps=[pending_w["mix"][4]])

    def weights_of(phase, after):
        if phase == "f1":
            return _unpack_gathered(got_f1, f1_names)
        return _unpack_gathered(_exchange_wait(f"gather_{phase}_wait", pending_w[phase], after, gather=True),
                                _PHASES[phase])

    pending_g, recv = {}, {}

    def send_grads(phase, grads):
        pack = _pack_grads(grads, _PHASES[phase])
        if phase == "f1gu":
            recv[phase] = _all_to_all(f"scatter_{phase}", pack)
            return []
        pending_g[phase] = _exchange_start(f"scatter_{phase}_start", pack, gather=False)
        return [pending_g[phase][4]]

    loss, dx, gsmall = _local_step(x[0], loss_target[0], small_in, weights_of, send_grads,
                                   first_deps=[pending_w["f2"][4]])

    for phase in pending_g:
        recv[phase] = _exchange_wait(f"scatter_{phase}_wait", pending_g[phase], recv["f1gu"], gather=False)
    packs = {}
    for phase, names in _PHASES.items():
        rows_p = w_pack[phase].shape[0]
        tr = max(t for t in range(16, 129, 16) if rows_p % t == 0)
        packs[phase] = _adamw(f"adamw_{phase}", w_pack[phase], _pack_sharded(m, names), _pack_sharded(v, names),
                              recv[phase], tr)
    gs_pack = _pack_small({nm: gsmall[nm].reshape(small[nm].shape) for nm in _SMALL})
    gs_all = _all_gather("gather_small_grads", gs_pack)
    sm = _adamw("adamw_small", _pack_small(small), _pack_small({nm: m[nm] for nm in _SMALL}),
                _pack_small({nm: v[nm] for nm in _SMALL}), gs_all, gs_pack.shape[0])

    loss = lax.psum(loss, ("x", "y", "c"))
    outs = []
    for i in range(4):
        big = {}
        for phase, names in _PHASES.items():
            big.update(_unpack_sharded(packs[phase][i], names))
        sml = _unpack_small(sm[i], small)
        outs.append([(big[nm] if nm in big else sml[nm]).reshape(shapes[nm]) for nm in _ORDER])
    return (loss, dx[None], *outs[0], *outs[1], *outs[2], *outs[3])
```

```python
import functools
import math

import numpy as np
import jax
import jax.numpy as jnp
from jax import lax
from jax.experimental import pallas as pl
from jax.experimental.pallas import tpu as pltpu

F32 = jnp.float32
BF16 = jnp.bfloat16

N_DEV = 8
D_MODEL = 1024
D_FF = 2816
HEAD_DIM = 64
HEADS_PER_GROUP = 4
DILATIONS = (1, 4, 16)
N_GROUPS = 3
ATTN_WIDTH = 768
ATTN_OUT = 256
BLOCK = 128
N_BUCKETS = 32
MAX_DISTANCE = 2048
NEG_INF = -1e30
SSM_WIDTH = 512
SSM_GROUPS = 32
SSM_GROUP = 16
SSM_STATE = 64
SSM_LANES = SSM_GROUPS * SSM_STATE
EPS = 1e-6
LR, B1, B2, ADAM_EPS, WD, STEP = 0.001, 0.9, 0.999, 1e-08, 0.01, 10

VMEM_LIMIT_BYTES = 56 * 1024 * 1024
SCAN_BLOCK = 256
SCAN_SUB = 8
SCAN_STEPS = SCAN_BLOCK // SCAN_SUB
SCAN_LANES = 512

MESH = pl.DeviceIdType.MESH


def _cparams(*sem):
    return pltpu.CompilerParams(dimension_semantics=sem, vmem_limit_bytes=VMEM_LIMIT_BYTES)


def _dot(a, b, dims):
    return lax.dot_general(a, b, (dims, ((), ())), preferred_element_type=F32)


def _dot_nn(a, b):
    return _dot(a, b, ((1,), (0,)))


def _dot_nt(a, b):
    return _dot(a, b, ((1,), (1,)))


def _dot_tn(a, b):
    return _dot(a, b, ((0,), (0,)))


def _sigmoid(x):
    return 1.0 / (1.0 + jnp.exp(-x))


def _all_gather(name, xs):
    rows, cols = xs.shape

    def body(x_ref, out_ref, send_sems, recv_sems, local_sem):
        x, y, c = lax.axis_index("x"), lax.axis_index("y"), lax.axis_index("c")
        me, sibling = (x, y, c), (x, y, 1 - c)
        chips = [(1 - x, y), (x, 1 - y), (1 - x, 1 - y)]

        def slot(px, py, pc):
            return out_ref.at[4 * px + 2 * py + pc]

        def copy(k, block, to, src=None):
            return pltpu.make_async_remote_copy(
                src_ref=slot(*block) if src is None else src, dst_ref=slot(*block),
                send_sem=send_sems.at[k], recv_sem=recv_sems.at[k], device_id=to, device_id_type=MESH)

        mine = pltpu.make_async_copy(x_ref, slot(*me), local_sem)
        mine.start()
        first = [copy(0, me, sibling, src=x_ref)]
        first += [copy(1 + j, me, (*chip, c), src=x_ref) for j, chip in enumerate(chips)]
        for cp in first:
            cp.start()
        passed = [copy(4 + j, (*chip, c), sibling) for j, chip in enumerate(chips)]
        for j, chip in enumerate(chips):
            copy(1 + j, (*chip, c), me).wait_recv()
            passed[j].start()
        copy(0, sibling, me).wait_recv()
        for j, chip in enumerate(chips):
            copy(4 + j, (*chip, 1 - c), me).wait_recv()
        for cp in first + passed:
            cp.wait_send()
        mine.wait()

    return pl.pallas_call(
        body, name=name,
        out_shape=jax.ShapeDtypeStruct((N_DEV, rows, cols), xs.dtype),
        in_specs=[pl.BlockSpec(memory_space=pl.ANY)],
        out_specs=pl.BlockSpec(memory_space=pl.ANY),
        scratch_shapes=[pltpu.SemaphoreType.DMA((7,)), pltpu.SemaphoreType.DMA((7,)), pltpu.SemaphoreType.DMA],
    )(xs)


def _all_to_all(name, xs):
    _, rows, cols = xs.shape

    def body(x_ref, out_ref, send_sems, recv_sems, local_sem):
        x, y, c = lax.axis_index("x"), lax.axis_index("y"), lax.axis_index("c")
        me = 4 * x + 2 * y + c
        mine = pltpu.make_async_copy(x_ref.at[me], out_ref.at[me], local_sem)
        mine.start()
        copies = []
        for k in range(1, N_DEV):
            px = 1 - x if k & 4 else x
            py = 1 - y if k & 2 else y
            pc = 1 - c if k & 1 else c
            cp = pltpu.make_async_remote_copy(
                src_ref=x_ref.at[4 * px + 2 * py + pc], dst_ref=out_ref.at[me],
                send_sem=send_sems.at[k - 1], recv_sem=recv_sems.at[k - 1],
                device_id=(px, py, pc), device_id_type=MESH)
            cp.start()
            copies.append(cp)
        for cp in copies:
            cp.wait()
        mine.wait()

    return pl.pallas_call(
        body, name=name,
        out_shape=jax.ShapeDtypeStruct(xs.shape, xs.dtype),
        in_specs=[pl.BlockSpec(memory_space=pl.ANY)],
        out_specs=pl.BlockSpec(memory_space=pl.ANY),
        scratch_shapes=[pltpu.SemaphoreType.DMA((7,)), pltpu.SemaphoreType.DMA((7,)), pltpu.SemaphoreType.DMA],
    )(xs)


_HBM_SPEC = pl.BlockSpec(memory_space=pltpu.HBM)
_SEM_SPEC = pl.BlockSpec(memory_space=pltpu.SEMAPHORE)
_EFFECT = pltpu.SideEffectType.DATAFLOW_SIDE_EFFECTING


def _peers(x, y, c):
    return [(1 - x if k & 4 else x, 1 - y if k & 2 else y, 1 - c if k & 1 else c) for k in range(1, N_DEV)]


def _exchange_copies(x_ref, land_ref, send_sems, recv_sems, gather):
    x, y, c = lax.axis_index("x"), lax.axis_index("y"), lax.axis_index("c")
    me = 4 * x + 2 * y + c
    copies = []
    for k, (px, py, pc) in enumerate(_peers(x, y, c)):
        src = x_ref if gather else x_ref.at[4 * px + 2 * py + pc]
        copies.append(pltpu.make_async_remote_copy(
            src_ref=src, dst_ref=land_ref.at[me], send_sem=send_sems.at[k], recv_sem=recv_sems.at[k],
            device_id=(px, py, pc), device_id_type=MESH))
    own = pltpu.make_async_copy(x_ref if gather else x_ref.at[me], land_ref.at[me], send_sems.at[N_DEV - 1])
    return own, copies


_ANY_SPEC = pl.BlockSpec(memory_space=pl.ANY)


def _exchange_start(name, xs, gather, deps=()):
    land_shape = (N_DEV, *xs.shape) if gather else xs.shape
    nd = len(deps)

    def body(x_ref, land_ref, *rest):
        send_sems, recv_sems, _, _, token = rest[nd:]
        own, copies = _exchange_copies(x_ref, land_ref, send_sems, recv_sems, gather)
        for cp in copies:
            cp.start()
        own.start()
        token[...] = jnp.zeros_like(token)

    return pl.pallas_call(
        body, name=name,
        out_shape=(pltpu.SemaphoreType.DMA((N_DEV,)), pltpu.SemaphoreType.DMA((N_DEV - 1,)),
                   pltpu.HBM(xs.shape, xs.dtype), pltpu.HBM(land_shape, xs.dtype), jax.ShapeDtypeStruct((8, 128), F32)),
        in_specs=(_HBM_SPEC, _HBM_SPEC) + (_ANY_SPEC,) * nd,
        out_specs=(_SEM_SPEC, _SEM_SPEC, _HBM_SPEC, _HBM_SPEC, pl.BlockSpec(memory_space=pltpu.VMEM)),
        input_output_aliases={0: 2, 1: 3},
        compiler_params=pltpu.CompilerParams(has_side_effects=_EFFECT),
    )(pltpu.with_memory_space_constraint(xs, pltpu.HBM),
      pltpu.with_memory_space_constraint(lax.empty(land_shape, xs.dtype), pltpu.HBM), *deps)


def _exchange_wait(name, handle, after, gather):
    send_sems, recv_sems, x_thru, land_thru, _ = handle

    def body(x_ref, land_ref, send_sems, recv_sems, after_ref, x_dead, got_ref):
        own, copies = _exchange_copies(x_ref, land_ref, send_sems, recv_sems, gather)
        for cp in copies:
            cp.wait_send()
            cp.wait_recv()
        own.wait()

    return pl.pallas_call(
        body, name=name,
        out_shape=(pltpu.HBM(x_thru.shape, x_thru.dtype), pltpu.HBM(land_thru.shape, land_thru.dtype)),
        in_specs=(_HBM_SPEC, _HBM_SPEC, _SEM_SPEC, _SEM_SPEC, pl.BlockSpec(memory_space=pl.ANY)),
        out_specs=(_HBM_SPEC, _HBM_SPEC), input_output_aliases={0: 0, 1: 1},
        compiler_params=pltpu.CompilerParams(has_side_effects=_EFFECT),
    )(x_thru, land_thru, send_sems, recv_sems, after)[1]


def _mm(name, pairs, nt, n_cols, out_dtypes, epilogue=None, extras=(), tm=1024, tn=512, deps=()):
    rows = pairs[0][0].shape[0]
    tm = min(tm, rows)
    tn = min(tn, n_cols)
    na, ne, nd = len(pairs), len(extras), len(deps)

    def body(*refs):
        a_refs, w_refs = refs[:na], refs[na:2 * na]
        e_refs, o_refs = refs[2 * na:2 * na + ne], refs[2 * na + ne + nd:]
        acc = None
        for a_ref, w_ref in zip(a_refs, w_refs):
            a = a_ref[...].astype(BF16)
            w = w_ref[...].astype(BF16)
            p = _dot_nt(a, w) if nt else _dot_nn(a, w)
            acc = p if acc is None else acc + p
        outs = (acc,) if epilogue is None else epilogue(acc, *[e[...] for e in e_refs])
        for o_ref, o in zip(o_refs, outs):
            o_ref[...] = o.astype(o_ref.dtype)

    in_specs = [pl.BlockSpec((tm, a.shape[1]), lambda i, j: (i, 0)) for a, _ in pairs]
    for _, w in pairs:
        if nt:
            in_specs.append(pl.BlockSpec((tn, w.shape[1]), lambda i, j: (j, 0)))
        else:
            in_specs.append(pl.BlockSpec((w.shape[0], tn), lambda i, j: (0, j)))
    for e, col_off in extras:
        off = col_off // tn
        if e.shape[0] == 1:
            in_specs.append(pl.BlockSpec((1, tn), lambda i, j, off=off: (0, j + off)))
        else:
            in_specs.append(pl.BlockSpec((tm, tn), lambda i, j, off=off: (i, j + off)))
    in_specs += [_ANY_SPEC] * nd
    out_specs = [pl.BlockSpec((tm, tn), lambda i, j: (i, j)) for _ in out_dtypes]
    outs = pl.pallas_call(
        body, name=name, grid=(rows // tm, n_cols // tn),
        in_specs=in_specs, out_specs=out_specs,
        out_shape=[jax.ShapeDtypeStruct((rows, n_cols), dt) for dt in out_dtypes],
        compiler_params=_cparams("parallel", "arbitrary"),
    )(*[a for a, _ in pairs], *[w for _, w in pairs], *[e for e, _ in extras], *deps)
    return outs


def _tn_rows(m):
    return max(b for b in range(128, min(m, 1408) + 1, 128) if m % b == 0)


def _mm_tn(name, a, b, scale=1.0, bm=None, tk=1024, deps=()):
    rows, m = a.shape
    n = b.shape[1]
    bm = _tn_rows(m) if bm is None else bm
    tk = min(tk, rows)
    nk = rows // tk

    def body(a_ref, b_ref, *rest):
        o_ref = rest[-1]
        k = pl.program_id(1)

        @pl.when(k == 0)
        def _():
            o_ref[...] = jnp.zeros_like(o_ref)

        o_ref[...] += _dot_tn(a_ref[...].astype(BF16), b_ref[...].astype(BF16))
        if scale != 1.0:
            @pl.when(k == nk - 1)
            def _():
                o_ref[...] = o_ref[...] * scale

    return pl.pallas_call(
        body, name=name, grid=(m // bm, nk),
        in_specs=[pl.BlockSpec((tk, bm), lambda i, k: (k, i)), pl.BlockSpec((tk, n), lambda i, k: (k, 0))]
        + [_ANY_SPEC] * len(deps),
        out_specs=pl.BlockSpec((bm, n), lambda i, k: (i, 0)),
        out_shape=jax.ShapeDtypeStruct((m, n), F32),
        compiler_params=_cparams("parallel", "arbitrary"),
    )(a, b, *deps)


def _colsum(name, xs, tm=512):
    rows, cols = xs.shape
    tm = min(tm, rows)

    def body(x_ref, o_ref):
        @pl.when(pl.program_id(0) == 0)
        def _():
            o_ref[...] = jnp.zeros_like(o_ref)

        o_ref[...] += jnp.sum(x_ref[...].astype(F32), axis=0, keepdims=True)

    return pl.pallas_call(
        body, name=name, grid=(rows // tm,),
        in_specs=[pl.BlockSpec((tm, cols), lambda i: (i, 0))],
        out_specs=pl.BlockSpec((1, cols), lambda i: (0, 0)),
        out_shape=jax.ShapeDtypeStruct((1, cols), F32),
        compiler_params=_cparams("arbitrary"),
    )(xs)


def _ew(name, fn, ins, out_cols, out_dtypes, tm=512):
    rows = ins[0].shape[0]
    tm = min(tm, rows)
    ni = len(ins)

    def body(*refs):
        outs = fn(*[r[...] for r in refs[:ni]])
        for o_ref, o in zip(refs[ni:], outs):
            o_ref[...] = o.astype(o_ref.dtype)

    def spec(shape):
        if shape[0] == 1:
            return pl.BlockSpec((1, shape[1]), lambda i: (0, 0))
        return pl.BlockSpec((tm, shape[1]), lambda i: (i, 0))

    return pl.pallas_call(
        body, name=name, grid=(rows // tm,),
        in_specs=[spec(a.shape) for a in ins],
        out_specs=[pl.BlockSpec((tm, c), lambda i: (i, 0)) for c in out_cols],
        out_shape=[jax.ShapeDtypeStruct((rows, c), dt) for c, dt in zip(out_cols, out_dtypes)],
        compiler_params=_cparams("parallel"),
    )(*ins)


def _rms_parts(xv):
    r = lax.rsqrt(jnp.mean(xv * xv, axis=-1, keepdims=True) + EPS)
    return r, xv * r


def _rms_bwd_dx(dh, gain, r, xh):
    dxh = dh * gain
    return r * (dxh - xh * jnp.mean(dxh * xh, axis=-1, keepdims=True))


def _rms_fwd(name, xs, gain):
    def fn(xv, g):
        _, xh = _rms_parts(xv)
        return (xh * g,)

    return _ew(name, fn, [xs, gain], [xs.shape[1]], [BF16])[0]


def _rms_bwd(name, dh, xs, gain, dres, tm=512, deps=()):
    rows, d = xs.shape
    tm = min(tm, rows)

    def body(dh_ref, x_ref, g_ref, dres_ref, *rest):
        dx_ref, dg_ref = rest[-2:]
        r, xh = _rms_parts(x_ref[...])
        dhv = dh_ref[...]
        dx_ref[...] = dres_ref[...] + _rms_bwd_dx(dhv, g_ref[...], r, xh)

        @pl.when(pl.program_id(0) == 0)
        def _():
            dg_ref[...] = jnp.zeros_like(dg_ref)

        dg_ref[...] += jnp.sum(dhv * xh, axis=0, keepdims=True)

    tile = pl.BlockSpec((tm, d), lambda i: (i, 0))
    row = pl.BlockSpec((1, d), lambda i: (0, 0))
    return pl.pallas_call(
        body, name=name, grid=(rows // tm,),
        in_specs=[tile, tile, row, tile] + [_ANY_SPEC] * len(deps), out_specs=[tile, row],
        out_shape=[jax.ShapeDtypeStruct((rows, d), F32), jax.ShapeDtypeStruct((1, d), F32)],
        compiler_params=_cparams("arbitrary"),
    )(dh, xs, gain, dres, *deps)


def _ffn_fwd(name, xs, gain, wg_t, wu_t, wd, tm=512, tf=1408, deps=()):
    rows, d = xs.shape
    f_all = wd.shape[0]
    tm = min(tm, rows)
    nf = f_all // tf

    def body(x_ref, g_ref, wg_ref, wu_ref, wd_ref, *rest):
        xo_ref, h_ref, gg_ref, uu_ref, acc_ref = rest[-5:]
        f = pl.program_id(1)

        @pl.when(f == 0)
        def _():
            _, xh = _rms_parts(x_ref[...])
            h_ref[...] = (xh * g_ref[...]).astype(BF16)
            acc_ref[...] = jnp.zeros_like(acc_ref)

        h = h_ref[...]
        gg = _dot_nt(h, wg_ref[...])
        uu = _dot_nt(h, wu_ref[...])
        act = gg * _sigmoid(gg) * uu
        acc_ref[...] += _dot_nn(act.astype(BF16), wd_ref[...])
        gg_ref[...] = gg.astype(BF16)
        uu_ref[...] = uu.astype(BF16)

        @pl.when(f == nf - 1)
        def _():
            xo_ref[...] = x_ref[...] + 0.5 * acc_ref[...]

    tile = pl.BlockSpec((tm, d), lambda i, f: (i, 0))
    wspec = pl.BlockSpec((tf, d), lambda i, f: (f, 0))
    hid = pl.BlockSpec((tm, tf), lambda i, f: (i, f))
    return pl.pallas_call(
        body, name=name, grid=(rows // tm, nf),
        in_specs=[tile, pl.BlockSpec((1, d), lambda i, f: (0, 0)), wspec, wspec, wspec] + [_ANY_SPEC] * len(deps),
        out_specs=[tile, tile, hid, hid],
        out_shape=[jax.ShapeDtypeStruct((rows, d), F32), jax.ShapeDtypeStruct((rows, d), BF16),
                   jax.ShapeDtypeStruct((rows, f_all), BF16), jax.ShapeDtypeStruct((rows, f_all), BF16)],
        scratch_shapes=[pltpu.VMEM((tm, d), F32)],
        compiler_params=_cparams("parallel", "arbitrary"),
    )(xs, gain, wg_t, wu_t, wd, *deps)


def _ffn_bwd(name, dxo, xs, gain, gg_all, uu_all, wg_t, wu_t, wd, tm=256, tf=1408):
    rows, d = xs.shape
    f_all = wd.shape[0]
    tm = min(tm, rows)
    nf = f_all // tf

    def body(dxo_ref, x_ref, g_ref, gg_ref, uu_ref, wg_ref, wu_ref, wd_ref,
             dx_ref, dgg_ref, duu_ref, act_ref, dgain_ref, df_ref, acc_ref):
        i, f = pl.program_id(0), pl.program_id(1)

        @pl.when(f == 0)
        def _():
            df_ref[...] = (0.5 * dxo_ref[...]).astype(BF16)
            acc_ref[...] = jnp.zeros_like(acc_ref)

        gg = gg_ref[...].astype(F32)
        uu = uu_ref[...].astype(F32)
        sg = _sigmoid(gg)
        silu = gg * sg
        dact = _dot_nt(df_ref[...], wd_ref[...])
        duu = (dact * silu).astype(BF16)
        dgg = (dact * uu * (sg * (1.0 + gg * (1.0 - sg)))).astype(BF16)
        act_ref[...] = (silu * uu).astype(BF16)
        dgg_ref[...] = dgg
        duu_ref[...] = duu
        acc_ref[...] += _dot_nn(dgg, wg_ref[...]) + _dot_nn(duu, wu_ref[...])

        @pl.when(f == nf - 1)
        def _():
            r, xh = _rms_parts(x_ref[...])
            dh = acc_ref[...]
            dx_ref[...] = dxo_ref[...] + _rms_bwd_dx(dh, g_ref[...], r, xh)
            part = jnp.sum(dh * xh, axis=0, keepdims=True)

            @pl.when(i == 0)
            def _():
                dgain_ref[...] = part

            @pl.when(i > 0)
            def _():
                dgain_ref[...] += part

    tile = pl.BlockSpec((tm, d), lambda i, f: (i, 0))
    row = pl.BlockSpec((1, d), lambda i, f: (0, 0))
    wspec = pl.BlockSpec((tf, d), lambda i, f: (f, 0))
    hid = pl.BlockSpec((tm, tf), lambda i, f: (i, f))
    hid_shape = jax.ShapeDtypeStruct((rows, f_all), BF16)
    return pl.pallas_call(
        body, name=name, grid=(rows // tm, nf),
        in_specs=[tile, tile, row, hid, hid, wspec, wspec, wspec],
        out_specs=[tile, hid, hid, hid, row],
        out_shape=[jax.ShapeDtypeStruct((rows, d), F32), hid_shape, hid_shape, hid_shape,
                   jax.ShapeDtypeStruct((1, d), F32)],
        scratch_shapes=[pltpu.VMEM((tm, d), BF16), pltpu.VMEM((tm, d), F32)],
        compiler_params=_cparams("arbitrary", "arbitrary"),
    )(dxo, xs, gain, gg_all, uu_all, wg_t, wu_t, wd)


def _final_loss(name, xs, gain, target, tm=512):
    rows, d = xs.shape
    tm = min(tm, rows)

    def body(x_ref, g_ref, t_ref, dx_ref, dg_ref, loss_ref):
        r, xh = _rms_parts(x_ref[...])
        gain_v = g_ref[...]
        err = xh * gain_v - t_ref[...]
        dy = err * (1.0 / d)
        dx_ref[...] = _rms_bwd_dx(dy, gain_v, r, xh)

        @pl.when(pl.program_id(0) == 0)
        def _():
            dg_ref[...] = jnp.zeros_like(dg_ref)
            loss_ref[...] = jnp.zeros_like(loss_ref)

        dg_ref[...] += jnp.sum(dy * xh, axis=0, keepdims=True)
        per_tok = jnp.mean(err * err, axis=-1, keepdims=True)
        loss_ref[...] += 0.5 * jnp.sum(per_tok, axis=0, keepdims=True)

    tile = pl.BlockSpec((tm, d), lambda i: (i, 0))
    row = pl.BlockSpec((1, d), lambda i: (0, 0))
    return pl.pallas_call(
        body, name=name, grid=(rows // tm,),
        in_specs=[tile, row, tile],
        out_specs=[tile, row, pl.BlockSpec((1, 1), lambda i: (0, 0))],
        out_shape=[jax.ShapeDtypeStruct((rows, d), F32), jax.ShapeDtypeStruct((1, d), F32),
                   jax.ShapeDtypeStruct((1, 1), F32)],
        compiler_params=_cparams("arbitrary"),
    )(xs, gain, target)


def _bucket_table():
    out = []
    for dil in DILATIONS:
        qi = np.arange(BLOCK)[:, None]
        kj = np.arange(2 * BLOCK)[None, :]
        dist = (np.maximum(qi + BLOCK - kj, 0) * dil).astype(np.int32)
        max_exact = N_BUCKETS // 2
        dd = np.maximum(dist, 1).astype(np.float32)
        large = max_exact + (np.log(dd / np.float32(max_exact)) / np.float32(math.log(MAX_DISTANCE / max_exact))
                             * np.float32(N_BUCKETS - max_exact)).astype(np.int32)
        large = np.minimum(large, N_BUCKETS - 1)
        out.append(np.where(dist < max_exact, dist, large).astype(np.int32))
    return np.stack(out)


def _bias_fwd(name, buckets, table):
    def body(bk_ref, tab_ref, o_ref):
        for g in range(N_GROUPS):
            bk = bk_ref[g]
            for h in range(HEADS_PER_GROUP):
                col = g * HEADS_PER_GROUP + h
                acc = jnp.zeros((BLOCK, 2 * BLOCK), F32)
                for b in range(N_BUCKETS):
                    acc = jnp.where(bk == b, tab_ref[b, col], acc)
                o_ref[col] = acc

    return pl.pallas_call(
        body, name=name,
        in_specs=[pl.BlockSpec(memory_space=pltpu.VMEM), pl.BlockSpec(memory_space=pltpu.SMEM)],
        out_specs=pl.BlockSpec(memory_space=pltpu.VMEM),
        out_shape=jax.ShapeDtypeStruct((N_GROUPS * HEADS_PER_GROUP, BLOCK, 2 * BLOCK), F32),
    )(buckets, table)


def _bias_bwd(name, buckets, dbias):
    def body(bk_ref, db_ref, o_ref):
        row_id = lax.broadcasted_iota(jnp.int32, (N_BUCKETS, 128), 0)
        col_id = lax.broadcasted_iota(jnp.int32, (N_BUCKETS, 128), 1)
        acc = jnp.zeros((N_BUCKETS, 128), F32)
        for g in range(N_GROUPS):
            bk = bk_ref[g]
            for h in range(HEADS_PER_GROUP):
                col = g * HEADS_PER_GROUP + h
                db = db_ref[col]
                for b in range(N_BUCKETS):
                    part = jnp.sum(jnp.where(bk == b, db, 0.0), axis=0, keepdims=True)
                    tot = jnp.sum(part, axis=1, keepdims=True)
                    acc = jnp.where((row_id == b) & (col_id == col), tot, acc)
        o_ref[...] = acc

    return pl.pallas_call(
        body, name=name,
        in_specs=[pl.BlockSpec(memory_space=pltpu.VMEM), pl.BlockSpec(memory_space=pltpu.VMEM)],
        out_specs=pl.BlockSpec(memory_space=pltpu.VMEM),
        out_shape=jax.ShapeDtypeStruct((N_BUCKETS, 128), F32),
    )(buckets, dbias)


def _band_mask(n):
    qi = lax.broadcasted_iota(jnp.int32, (BLOCK, 2 * BLOCK), 0)
    kj = lax.broadcasted_iota(jnp.int32, (BLOCK, 2 * BLOCK), 1)
    return (kj >= qi) & (kj <= qi + BLOCK) & ((kj >= BLOCK) | (n > 0))


def _attn_fwd(name, q, k, v, bias):
    dil, nh, m, dh = q.shape
    nb = m // BLOCK

    def body(q_ref, kc_ref, kp_ref, vc_ref, vp_ref, b_ref, o_ref, lse_ref):
        mask = _band_mask(pl.program_id(1))
        for h in range(nh):
            k2 = jnp.concatenate([kp_ref[0, h], kc_ref[0, h]], axis=0)
            v2 = jnp.concatenate([vp_ref[0, h], vc_ref[0, h]], axis=0)
            s = _dot_nt(q_ref[0, h], k2) + b_ref[h]
            s = jnp.where(mask, s, NEG_INF)
            mx = jnp.max(s, axis=-1, keepdims=True)
            p = jnp.exp(s - mx)
            den = jnp.sum(p, axis=-1, keepdims=True)
            o_ref[0, h] = _dot_nn(p.astype(BF16), v2) / den
            lse_ref[0, h] = jnp.broadcast_to(mx + jnp.log(den), (BLOCK, dh))

    cur = pl.BlockSpec((1, nh, BLOCK, dh), lambda r, n: (r, 0, n, 0))
    prev = pl.BlockSpec((1, nh, BLOCK, dh), lambda r, n: (r, 0, jnp.maximum(n - 1, 0), 0))
    return pl.pallas_call(
        body, name=name, grid=(dil, nb),
        in_specs=[cur, cur, prev, cur, prev, pl.BlockSpec((nh, BLOCK, 2 * BLOCK), lambda r, n: (0, 0, 0))],
        out_specs=[cur, cur],
        out_shape=[jax.ShapeDtypeStruct(q.shape, F32), jax.ShapeDtypeStruct(q.shape, F32)],
        compiler_params=_cparams("parallel", "arbitrary"),
    )(q, k, k, v, v, bias)


def _attn_bwd(name, q, k, v, do, lse, cvec, bias):
    dil, nh, m, dh = q.shape
    nb = m // BLOCK

    def body(q_ref, kc_ref, kp_ref, vc_ref, vp_ref, do_ref, lse_ref, c_ref, b_ref,
             dq_ref, dk_ref, dv_ref, db_ref, kcar_ref, vcar_ref):
        r, n = pl.program_id(0), pl.program_id(1)
        valid = n < nb
        mask = _band_mask(n) & valid

        @pl.when((r == 0) & (n == 0))
        def _():
            kcar_ref[...] = jnp.zeros_like(kcar_ref)
            vcar_ref[...] = jnp.zeros_like(vcar_ref)
            db_ref[...] = jnp.zeros_like(db_ref)

        for h in range(nh):
            qh = q_ref[0, h]
            k2 = jnp.concatenate([kp_ref[0, h], kc_ref[0, h]], axis=0)
            v2 = jnp.concatenate([vp_ref[0, h], vc_ref[0, h]], axis=0)
            doh = do_ref[0, h].astype(BF16)
            s = _dot_nt(qh, k2) + b_ref[h]
            p = jnp.where(mask, jnp.exp(s - lse_ref[0, h][:, :1]), 0.0)
            dp = _dot_nt(doh, v2)
            ds = p * (dp + c_ref[0, h][:, :1])
            ds_b = ds.astype(BF16)

            @pl.when(valid)
            def _():
                dq_ref[0, h] = _dot_nn(ds_b, k2)

            dk2 = _dot_tn(ds_b, qh)
            dv2 = _dot_tn(p.astype(BF16), doh)
            dk_ref[0, h] = kcar_ref[h] + dk2[:BLOCK]
            dv_ref[0, h] = vcar_ref[h] + dv2[:BLOCK]
            kcar_ref[h] = dk2[BLOCK:]
            vcar_ref[h] = dv2[BLOCK:]
            db_ref[h] += ds

    def qmap(r, n):
        return (r, 0, jnp.minimum(n, nb - 1), 0)

    def pmap(r, n):
        return (r, 0, jnp.maximum(jnp.minimum(n, nb - 1) - 1, 0), 0)

    def kvout(r, n):
        return (r, 0, jnp.maximum(n - 1, 0), 0)

    blk = (1, nh, BLOCK, dh)
    cur, prev = pl.BlockSpec(blk, qmap), pl.BlockSpec(blk, pmap)
    bias_spec = pl.BlockSpec((nh, BLOCK, 2 * BLOCK), lambda r, n: (0, 0, 0))
    full = jax.ShapeDtypeStruct(q.shape, F32)
    return pl.pallas_call(
        body, name=name, grid=(dil, nb + 1),
        in_specs=[cur, cur, prev, cur, prev, cur, cur, cur, bias_spec],
        out_specs=[cur, pl.BlockSpec(blk, kvout), pl.BlockSpec(blk, kvout), bias_spec],
        out_shape=[full, full, full, jax.ShapeDtypeStruct(bias.shape, F32)],
        scratch_shapes=[pltpu.VMEM((nh, BLOCK, dh), F32), pltpu.VMEM((nh, BLOCK, dh), F32)],
        compiler_params=_cparams("arbitrary", "arbitrary"),
    )(q, k, k, v, v, do, lse, cvec, bias)


def _group_weights(lses):
    mx = jnp.maximum(jnp.maximum(lses[0], lses[1]), lses[2])
    es = [jnp.exp(l - mx) for l in lses]
    den = es[0] + es[1] + es[2]
    return [e / den for e in es]


def _combine_fwd(name, os_, lses, tm=512):
    nh, rows, dh = os_[0].shape
    tm = min(tm, rows)

    def body(o0, o1, o2, l0, l1, l2, out_ref):
        ws = _group_weights([l0[...], l1[...], l2[...]])
        out_ref[...] = ws[0] * o0[...] + ws[1] * o1[...] + ws[2] * o2[...]

    spec = pl.BlockSpec((nh, tm, dh), lambda i: (0, i, 0))
    return pl.pallas_call(
        body, name=name, grid=(rows // tm,), in_specs=[spec] * 6, out_specs=spec,
        out_shape=jax.ShapeDtypeStruct((nh, rows, dh), F32),
        compiler_params=_cparams("parallel"),
    )(*os_, *lses)


def _combine_bwd(name, do, oa, lses, tm=512):
    nh, rows, dh = do.shape
    tm = min(tm, rows)

    def body(do_ref, oa_ref, l0, l1, l2, d0, d1, d2, c0, c1, c2):
        ws = _group_weights([l0[...], l1[...], l2[...]])
        dov = do_ref[...]
        bar = jnp.sum(dov * oa_ref[...], axis=-1, keepdims=True)
        for w, d_ref, c_ref in zip(ws, (d0, d1, d2), (c0, c1, c2)):
            d_ref[...] = w * dov
            c_ref[...] = -w * bar

    spec = pl.BlockSpec((nh, tm, dh), lambda i: (0, i, 0))
    shape = jax.ShapeDtypeStruct((nh, rows, dh), F32)
    return pl.pallas_call(
        body, name=name, grid=(rows // tm,), in_specs=[spec] * 5, out_specs=[spec] * 6,
        out_shape=[shape] * 6, compiler_params=_cparams("parallel"),
    )(do, oa, *lses)


def _ssm_disc(a_re, a_im, log_dt, b_re, b_im):
    dt = jnp.exp(log_dt)
    mag = jnp.exp(a_re * dt)
    ab_re = mag * jnp.cos(a_im * dt)
    ab_im = mag * jnp.sin(a_im * dt)
    den = a_re * a_re + a_im * a_im
    xr = ab_re - 1.0
    coef_re = (xr * a_re + ab_im * a_im) / den
    coef_im = (ab_im * a_re - xr * a_im) / den
    bb_re = coef_re[None] * b_re - coef_im[None] * b_im
    bb_im = coef_re[None] * b_im + coef_im[None] * b_re
    return ab_re, ab_im, bb_re, bb_im


def _cpow2(re, im, times):
    for _ in range(times):
        re, im = re * re - im * im, 2.0 * re * im
    return re, im


def _ssm_params_fwd(name, a_re, a_im, log_dt, b_re, b_im):
    gn = jax.ShapeDtypeStruct(a_re.shape, F32)
    cgn = jax.ShapeDtypeStruct(b_re.shape, F32)

    def body(ar, ai, ld, br, bi, o_abr, o_abi, o_apr, o_api, o_bbr, o_bbi):
        ab_re, ab_im, bb_re, bb_im = _ssm_disc(ar[...], ai[...], ld[...], br[...], bi[...])
        o_abr[...] = ab_re
        o_abi[...] = ab_im
        pr, pi = _cpow2(ab_re, ab_im, int(math.log2(SCAN_STEPS)))
        o_apr[...] = pr
        o_api[...] = pi
        o_bbr[...] = bb_re
        o_bbi[...] = bb_im

    vm = pl.BlockSpec(memory_space=pltpu.VMEM)
    return pl.pallas_call(body, name=name, in_specs=[vm] * 5, out_specs=[vm] * 6,
                          out_shape=[gn, gn, gn, gn, cgn, cgn])(a_re, a_im, log_dt, b_re, b_im)


def _ssm_params_bwd(name, a_re, a_im, log_dt, b_re, b_im, d_ab_re, d_ab_im, d_bb_re, d_bb_im):
    gn = jax.ShapeDtypeStruct(a_re.shape, F32)
    cgn = jax.ShapeDtypeStruct(b_re.shape, F32)

    def body(ar, ai, ld, br, bi, g0, g1, g2, g3, o_ar, o_ai, o_ld, o_br, o_bi):
        _, vjp = jax.vjp(_ssm_disc, ar[...], ai[...], ld[...], br[...], bi[...])
        outs = vjp((g0[...], g1[...], g2[...], g3[...]))
        for o_ref, o in zip((o_ar, o_ai, o_ld, o_br, o_bi), outs):
            o_ref[...] = o

    vm = pl.BlockSpec(memory_space=pltpu.VMEM)
    return pl.pallas_call(body, name=name, in_specs=[vm] * 9, out_specs=[vm] * 5,
                          out_shape=[gn, gn, jax.ShapeDtypeStruct(log_dt.shape, F32), cgn, cgn],
                          )(a_re, a_im, log_dt, b_re, b_im, d_ab_re, d_ab_im, d_bb_re, d_bb_im)


def _scan_block(s_ref, carry_ref, tmp_ref, ab_ref, ap_ref, reverse, sprev=None):
    nl = SSM_LANES
    for lc in range(nl // SCAN_LANES):
        re_l = pl.ds(lc * SCAN_LANES, SCAN_LANES)
        im_l = pl.ds(nl + lc * SCAN_LANES, SCAN_LANES)
        are, aim = ab_ref[:, re_l], ab_ref[:, im_l]

        def rows_of(j):
            jj = SCAN_STEPS - 1 - j if reverse else j
            return pl.ds(pl.multiple_of(jj * SCAN_SUB, SCAN_SUB), SCAN_SUB)

        def pass1(j, st):
            sr, si = st
            rows = rows_of(j)
            nr = are * sr - aim * si + s_ref[rows, re_l]
            ni = are * si + aim * sr + s_ref[rows, im_l]
            s_ref[rows, re_l] = nr
            s_ref[rows, im_l] = ni
            return nr, ni

        zero = jnp.zeros((SCAN_SUB, SCAN_LANES), F32)
        er, ei = lax.fori_loop(0, SCAN_STEPS, pass1, (zero, zero), unroll=2)
        tmp_ref[0:SCAN_SUB, re_l] = er
        tmp_ref[0:SCAN_SUB, im_l] = ei
        apr, api = ap_ref[0:1, re_l], ap_ref[0:1, im_l]
        sr, si = carry_ref[0:1, re_l], carry_ref[0:1, im_l]
        for step in range(SCAN_SUB):
            c = SCAN_SUB - 1 - step if reverse else step
            tmp_ref[SCAN_SUB + c:SCAN_SUB + c + 1, re_l] = sr
            tmp_ref[SCAN_SUB + c:SCAN_SUB + c + 1, im_l] = si
            e_r, e_i = tmp_ref[c:c + 1, re_l], tmp_ref[c:c + 1, im_l]
            sr, si = apr * sr - api * si + e_r, apr * si + api * sr + e_i
        carry_ref[0:1, re_l] = sr
        carry_ref[0:1, im_l] = si
        cr = tmp_ref[SCAN_SUB:2 * SCAN_SUB, re_l]
        ci = tmp_ref[SCAN_SUB:2 * SCAN_SUB, im_l]

        if sprev is None:
            def pass2(j, st):
                pr, pi = st
                rows = rows_of(j)
                s_ref[rows, re_l] += pr * cr - pi * ci
                s_ref[rows, im_l] += pr * ci + pi * cr
                return pr * are - pi * aim, pr * aim + pi * are

            lax.fori_loop(0, SCAN_STEPS, pass2, (are, aim), unroll=2)
        else:
            st_ref, prev_ref, have_prev, dab_ref = sprev

            def corrected(j, pr, pi):
                rows = rows_of(j)
                gr = s_ref[rows, re_l] + pr * cr - pi * ci
                gi = s_ref[rows, im_l] + pr * ci + pi * cr
                s_ref[rows, re_l] = gr
                s_ref[rows, im_l] = gi
                return gr, gi

            def pass2(j, st):
                pr, pi, dr, di = st
                gr, gi = corrected(j, pr, pi)
                before = pl.ds(pl.multiple_of((SCAN_STEPS - 2 - j) * SCAN_SUB, SCAN_SUB), SCAN_SUB)
                qr, qi = st_ref[before, re_l], st_ref[before, im_l]
                return (pr * are - pi * aim, pr * aim + pi * are,
                        dr + gr * qr + gi * qi, di + gi * qr - gr * qi)

            pr, pi, dr, di = lax.fori_loop(0, SCAN_STEPS - 1, pass2, (are, aim, zero, zero), unroll=2)
            gr, gi = corrected(SCAN_STEPS - 1, pr, pi)
            last = pl.ds((SCAN_STEPS - 1) * SCAN_SUB, SCAN_SUB)
            sub = lax.broadcasted_iota(jnp.int32, (SCAN_SUB, SCAN_LANES), 0)
            pv_r = jnp.broadcast_to(prev_ref[SCAN_SUB - 1:SCAN_SUB, re_l], (SCAN_SUB, SCAN_LANES)) * have_prev
            pv_i = jnp.broadcast_to(prev_ref[SCAN_SUB - 1:SCAN_SUB, im_l], (SCAN_SUB, SCAN_LANES)) * have_prev
            qr = jnp.where(sub == 0, pv_r, pltpu.roll(st_ref[last, re_l], 1, 0))
            qi = jnp.where(sub == 0, pv_i, pltpu.roll(st_ref[last, im_l], 1, 0))
            dab_ref[:, re_l] += dr + gr * qr + gi * qi
            dab_ref[:, im_l] += di + gi * qr - gr * qi


def _ssm_fwd(name, u_perm, bb_mat, c_mat, ab_rows, ap_rows, d_skip):
    rows = u_perm.shape[0]
    nl2 = 2 * SSM_LANES

    def body(u_ref, bb_ref, c_ref, ab_ref, ap_ref, d_ref, y_ref, s_ref, carry_ref, tmp_ref):
        @pl.when(pl.program_id(0) == 0)
        def _():
            carry_ref[...] = jnp.zeros_like(carry_ref)

        uv = u_ref[...]
        s_ref[...] = _dot_nn(uv.astype(BF16), bb_ref[...])
        _scan_block(s_ref, carry_ref, tmp_ref, ab_ref, ap_ref, reverse=False)
        y_ref[...] = _dot_nn(s_ref[...].astype(BF16), c_ref[...]) + d_ref[...] * uv

    const = lambda shape: pl.BlockSpec(shape, lambda i: (0, 0))
    return pl.pallas_call(
        body, name=name, grid=(rows // SCAN_BLOCK,),
        in_specs=[pl.BlockSpec((SCAN_BLOCK, SSM_WIDTH), lambda i: (i, 0)), const((SSM_WIDTH, nl2)),
                  const((nl2, SSM_WIDTH)), const((SCAN_SUB, nl2)), const((SCAN_SUB, nl2)), const((1, SSM_WIDTH))],
        out_specs=[pl.BlockSpec((SCAN_BLOCK, SSM_WIDTH), lambda i: (i, 0)),
                   pl.BlockSpec((SCAN_BLOCK, nl2), lambda i: (i, 0))],
        out_shape=[jax.ShapeDtypeStruct((rows, SSM_WIDTH), F32), jax.ShapeDtypeStruct((rows, nl2), F32)],
        scratch_shapes=[pltpu.VMEM((SCAN_SUB, nl2), F32), pltpu.VMEM((2 * SCAN_SUB, nl2), F32)],
        compiler_params=_cparams("arbitrary"),
    )(u_perm, bb_mat, c_mat, ab_rows, ap_rows, d_skip)


def _ssm_bwd(name, dy_perm, u_perm, states, c_mat_t, bb_mat_t, abc_rows, apc_rows, d_skip):
    rows = u_perm.shape[0]
    nl2 = 2 * SSM_LANES
    nblk = rows // SCAN_BLOCK

    def body(dy_ref, u_ref, st_ref, prev_ref, ct_ref, bt_ref, ab_ref, ap_ref, d_ref,
             du_ref, g_ref, dab_ref, dd_ref, carry_ref, tmp_ref):
        i = pl.program_id(0)

        @pl.when(i == 0)
        def _():
            carry_ref[...] = jnp.zeros_like(carry_ref)
            dab_ref[...] = jnp.zeros_like(dab_ref)
            dd_ref[...] = jnp.zeros_like(dd_ref)

        dyv = dy_ref[...]
        g_ref[...] = _dot_nn(dyv.astype(BF16), ct_ref[...])
        have_prev = (i < nblk - 1).astype(F32)
        _scan_block(g_ref, carry_ref, tmp_ref, ab_ref, ap_ref, reverse=True,
                    sprev=(st_ref, prev_ref, have_prev, dab_ref))
        du_ref[...] = _dot_nn(g_ref[...].astype(BF16), bt_ref[...]) + d_ref[...] * dyv
        dd_ref[...] += jnp.sum(dyv * u_ref[...], axis=0, keepdims=True)

    const = lambda shape: pl.BlockSpec(shape, lambda i: (0, 0))
    blk = lambda cols: pl.BlockSpec((SCAN_BLOCK, cols), lambda i: (nblk - 1 - i, 0))
    per8 = SCAN_BLOCK // SCAN_SUB
    prev_spec = pl.BlockSpec((SCAN_SUB, nl2), lambda i: (jnp.maximum((nblk - 1 - i) * per8 - 1, 0), 0))
    return pl.pallas_call(
        body, name=name, grid=(nblk,),
        in_specs=[blk(SSM_WIDTH), blk(SSM_WIDTH), blk(nl2), prev_spec, const((SSM_WIDTH, nl2)),
                  const((nl2, SSM_WIDTH)), const((SCAN_SUB, nl2)), const((SCAN_SUB, nl2)), const((1, SSM_WIDTH))],
        out_specs=[blk(SSM_WIDTH), blk(nl2), const((SCAN_SUB, nl2)), const((1, SSM_WIDTH))],
        out_shape=[jax.ShapeDtypeStruct((rows, SSM_WIDTH), F32), jax.ShapeDtypeStruct((rows, nl2), F32),
                   jax.ShapeDtypeStruct((SCAN_SUB, nl2), F32), jax.ShapeDtypeStruct((1, SSM_WIDTH), F32)],
        scratch_shapes=[pltpu.VMEM((SCAN_SUB, nl2), F32), pltpu.VMEM((2 * SCAN_SUB, nl2), F32)],
        compiler_params=_cparams("arbitrary"),
    )(dy_perm, u_perm, states, states, c_mat_t, bb_mat_t, abc_rows, apc_rows, d_skip)


def _scan_order(a):
    rows, cols = a.shape
    return a.reshape(rows // SCAN_BLOCK, SCAN_SUB, SCAN_STEPS, cols).transpose(0, 2, 1, 3).reshape(rows, cols)


def _time_order(a):
    rows, cols = a.shape
    return a.reshape(rows // SCAN_BLOCK, SCAN_STEPS, SCAN_SUB, cols).transpose(0, 2, 1, 3).reshape(rows, cols)


def _adamw(name, w, m, v, gparts, tr):
    rows, cols = w.shape

    def body(w_ref, m_ref, v_ref, g_ref, og_ref, od_ref, om_ref, ov_ref):
        g = g_ref[0].astype(F32)
        for i in range(1, N_DEV):
            g = g + g_ref[i].astype(F32)
        m_new = B1 * m_ref[...] + (1.0 - B1) * g
        v_new = B2 * v_ref[...] + (1.0 - B2) * (g * g)
        m_hat = m_new / (1.0 - B1 ** STEP)
        v_hat = v_new / (1.0 - B2 ** STEP)
        og_ref[...] = g
        od_ref[...] = -LR * (m_hat / (jnp.sqrt(v_hat) + ADAM_EPS) + WD * w_ref[...])
        om_ref[...] = m_new
        ov_ref[...] = v_new

    spec = pl.BlockSpec((tr, cols), lambda i: (i, 0))
    shape = jax.ShapeDtypeStruct((rows, cols), F32)
    return pl.pallas_call(
        body, name=name, grid=(rows // tr,),
        in_specs=[spec, spec, spec, pl.BlockSpec((N_DEV, tr, cols), lambda i: (0, i, 0))],
        out_specs=[spec] * 4, out_shape=[shape] * 4,
        compiler_params=_cparams("parallel"),
    )(w, m, v, gparts)


_SHARDED = (
    ("ffn1_w_gate", True, (352, 1024)), ("ffn1_w_up", True, (352, 1024)), ("ffn1_w_down", False, (352, 1024)),
    ("w_in", True, (608, 1024)), ("ssm_w_glu", True, (128, 512)), ("w_attn_branch", True, (128, 256)),
    ("w_ssm_branch", True, (128, 512)), ("w_out", False, (128, 1024)),
    ("ffn2_w_gate", True, (352, 1024)), ("ffn2_w_up", True, (352, 1024)), ("ffn2_w_down", False, (352, 1024)),
)
_SMALL = ("ffn1_norm", "mix_norm", "gate_bias", "rel_bias_table", "ssm_a_re", "ssm_a_im", "ssm_log_dt",
          "ssm_b_re", "ssm_b_im", "ssm_c_re", "ssm_c_im", "ssm_d", "ffn2_norm", "final_norm")
_ORDER = ("ffn1_norm", "ffn1_w_gate", "ffn1_w_up", "ffn1_w_down", "mix_norm", "w_in", "gate_bias",
          "rel_bias_table", "ssm_a_re", "ssm_a_im", "ssm_log_dt", "ssm_b_re", "ssm_b_im", "ssm_c_re",
          "ssm_c_im", "ssm_d", "ssm_w_glu", "w_attn_branch", "w_ssm_branch", "w_out", "ffn2_norm",
          "ffn2_w_gate", "ffn2_w_up", "ffn2_w_down", "final_norm")


def _pack_rows(shape):
    return shape[0] * shape[1] // D_MODEL


_SHARD_INFO = {nm: (tr, shape) for nm, tr, shape in _SHARDED}
_PHASES = {
    "f1gu": ("ffn1_w_gate", "ffn1_w_up"), "f1d": ("ffn1_w_down",),
    "mix": ("w_in", "ssm_w_glu", "w_attn_branch", "w_ssm_branch", "w_out"),
    "f2": ("ffn2_w_gate", "ffn2_w_up", "ffn2_w_down"),
}


def _pack_sharded(ws, names):
    parts = []
    for nm in names:
        tr, shape = _SHARD_INFO[nm]
        a = ws[nm].T if tr else ws[nm]
        parts.append(a.reshape(_pack_rows(shape), D_MODEL))
    return jnp.concatenate(parts, axis=0)


def _unpack_sharded(pack, names):
    out, r0 = {}, 0
    for nm in names:
        tr, shape = _SHARD_INFO[nm]
        n = _pack_rows(shape)
        a = pack[r0:r0 + n].reshape(shape)
        out[nm] = a.T if tr else a
        r0 += n
    return out


def _unpack_gathered(gath, names):
    out, r0 = {}, 0
    for nm in names:
        _, shape = _SHARD_INFO[nm]
        n = _pack_rows(shape)
        out[nm] = gath[:, r0:r0 + n].reshape(N_DEV * shape[0], shape[1])
        r0 += n
    return out


def _pack_grads(gs, names):
    parts = []
    for nm in names:
        _, shape = _SHARD_INFO[nm]
        parts.append(gs[nm].astype(BF16).reshape(N_DEV, _pack_rows(shape), D_MODEL))
    return jnp.concatenate(parts, axis=1)


def _pack_small(ws):
    flat = jnp.concatenate([ws[nm].reshape(-1) for nm in _SMALL])
    pad = (-flat.shape[0]) % (8 * 128)
    return jnp.pad(flat, (0, pad)).reshape(-1, 128)


def _unpack_small(pack, like):
    flat, out, p0 = pack.reshape(-1), {}, 0
    for nm in _SMALL:
        n = like[nm].size
        out[nm] = flat[p0:p0 + n].reshape(like[nm].shape)
        p0 += n
    return out


def _to_dilated(a, dil):
    rows = a.shape[0]
    return a.reshape(rows // dil, dil, HEADS_PER_GROUP, HEAD_DIM).transpose(1, 2, 0, 3)


def _dilated_to_heads(a):
    dil, nh, m, dh = a.shape
    return a.transpose(1, 2, 0, 3).reshape(nh, m * dil, dh)


def _heads_to_dilated(a, dil):
    nh, rows, dh = a.shape
    return a.reshape(nh, rows // dil, dil, dh).transpose(2, 0, 1, 3)


def _block_diag(blocks_gab):
    g, a, b = blocks_gab.shape
    eye = jnp.eye(g, dtype=blocks_gab.dtype)
    return (blocks_gab[:, :, None, :] * eye[:, None, :, None]).reshape(g * a, g * b)


def _diag_blocks(mat, a, b):
    g = mat.shape[0] // a
    eye = jnp.eye(g, dtype=mat.dtype)
    return jnp.einsum("gahb,gh->gab", mat.reshape(g, a, g, b), eye)


def _local_step(xs, target, small, weights_of, send_grads, first_deps=()):
    rows = xs.shape[0]
    gfull, gsmall = {}, {}
    wf = dict(weights_of("f1", None))

    x1, h1, gg1, uu1 = _ffn_fwd("ffn1_fwd", xs, small["ffn1_norm"], wf["ffn1_w_gate"], wf["ffn1_w_up"],
                                wf["ffn1_w_down"], deps=first_deps)
    wf.update(weights_of("mix", x1))
    hmix = _rms_fwd("mix_norm_fwd", x1, small["mix_norm"])
    w_in = wf["w_in"]
    w_qkv, w_u, w_g = w_in[:3 * ATTN_WIDTH], w_in[3 * ATTN_WIDTH:3 * ATTN_WIDTH + SSM_WIDTH], w_in[3 * ATTN_WIDTH + SSM_WIDTH:]
    qscale = jnp.concatenate([jnp.full((1, ATTN_WIDTH), HEAD_DIM ** -0.5, F32), jnp.ones((1, 2 * ATTN_WIDTH), F32)], axis=1)
    qkv, = _mm("in_qkv", [(hmix, w_qkv)], True, 3 * ATTN_WIDTH, [BF16],
               epilogue=lambda acc, sc: (acc * sc,), extras=[(qscale, 0)], tn=ATTN_WIDTH)
    u, = _mm("in_u", [(hmix, w_u)], True, SSM_WIDTH, [F32])
    gates, = _mm("in_gates", [(hmix, w_g)], True, 2 * D_MODEL, [F32],
                 epilogue=lambda acc, b: (_sigmoid(acc + b),), extras=[(small["gate_bias"], 0)])

    buckets = jnp.asarray(_bucket_table())
    bias = _bias_fwd("rel_bias_fwd", buckets, small["rel_bias_table"])
    qkv_d, o_h, lse_h = [], [], []
    for g, dil in enumerate(DILATIONS):
        cols = [qkv[:, s * ATTN_WIDTH + g * ATTN_OUT:s * ATTN_WIDTH + (g + 1) * ATTN_OUT] for s in range(3)]
        qd, kd, vd = [_to_dilated(c, dil) for c in cols]
        bias_g = bias[g * HEADS_PER_GROUP:(g + 1) * HEADS_PER_GROUP]
        o_g, lse_g = _attn_fwd(f"attn_fwd_{g}", qd, kd, vd, bias_g)
        qkv_d.append((qd, kd, vd, bias_g))
        o_h.append(_dilated_to_heads(o_g))
        lse_h.append(_dilated_to_heads(lse_g))
    oa_h = _combine_fwd("attn_combine_fwd", o_h, lse_h)
    oa = oa_h.transpose(1, 0, 2).reshape(rows, ATTN_OUT).astype(BF16)
    y_attn, = _mm("attn_branch", [(oa, wf["w_attn_branch"])], True, D_MODEL, [F32])

    ab_re, ab_im, ap_re, ap_im, bb_re, bb_im = _ssm_params_fwd(
        "ssm_params_fwd", small["ssm_a_re"], small["ssm_a_im"], small["ssm_log_dt"].reshape(SSM_GROUPS, 1),
        small["ssm_b_re"].transpose(2, 0, 1), small["ssm_b_im"].transpose(2, 0, 1))

    def lanes(re, im, sign=1.0):
        row = jnp.concatenate([re.reshape(1, SSM_LANES), sign * im.reshape(1, SSM_LANES)], axis=1)
        return jnp.broadcast_to(row, (SCAN_SUB, 2 * SSM_LANES))

    bb_mat = jnp.concatenate([_block_diag(bb_re.transpose(1, 0, 2)), _block_diag(bb_im.transpose(1, 0, 2))], axis=1)
    c_mat_t = jnp.concatenate([_block_diag(small["ssm_c_re"]), -_block_diag(small["ssm_c_im"])], axis=1)
    bb_mat, c_mat_t = bb_mat.astype(BF16), c_mat_t.astype(BF16)
    d_skip = small["ssm_d"].reshape(1, SSM_WIDTH)
    u_perm = _scan_order(u)
    y_perm, states = _ssm_fwd("ssm_fwd", u_perm, bb_mat, c_mat_t.T, lanes(ab_re, ab_im), lanes(ap_re, ap_im), d_skip)
    y_raw = _time_order(y_perm)

    def gelu_fn(yv):
        return (jax.nn.gelu(yv),)

    ygelu, = _ew("ssm_gelu", gelu_fn, [y_raw], [SSM_WIDTH], [BF16])
    glu, = _mm("ssm_glu", [(ygelu, wf["ssm_w_glu"])], True, 2 * SSM_WIDTH, [F32])
    ysg, = _ew("ssm_glu_act", lambda gv: (gv[:, :SSM_WIDTH] * _sigmoid(gv[:, SSM_WIDTH:]),), [glu], [SSM_WIDTH], [BF16])
    y_ssm, merged = _mm("ssm_branch_merge", [(ysg, wf["w_ssm_branch"])], True, D_MODEL, [F32, BF16],
                        epilogue=lambda acc, ga, gs, ya: (acc, ga * ya + gs * acc),
                        extras=[(gates, 0), (gates, D_MODEL), (y_attn, 0)])
    x2, = _mm("mix_out", [(merged, wf["w_out"])], False, D_MODEL, [F32],
              epilogue=lambda acc, res: (res + acc,), extras=[(x1, 0)])
    wf.update(weights_of("f2", x2))
    x3, h2, gg2, uu2 = _ffn_fwd("ffn2_fwd", x2, small["ffn2_norm"], wf["ffn2_w_gate"], wf["ffn2_w_up"],
                                wf["ffn2_w_down"])
    dx3, gsmall["final_norm"], loss = _final_loss("final_loss", x3, small["final_norm"].reshape(1, D_MODEL), target)

    dx2, dgg2, duu2, act2, gsmall["ffn2_norm"] = _ffn_bwd(
        "ffn2_bwd", dx3, x2, small["ffn2_norm"], gg2, uu2, wf["ffn2_w_gate"], wf["ffn2_w_up"], wf["ffn2_w_down"])
    gfull["ffn2_w_gate"] = _mm_tn("ffn2_dwg", dgg2, h2)
    gfull["ffn2_w_up"] = _mm_tn("ffn2_dwu", duu2, h2)
    gfull["ffn2_w_down"] = _mm_tn("ffn2_dwd", act2, dx3, scale=0.5)
    sent = send_grads("f2", gfull)

    def merge_bwd(dm, ga, gs, ya, ys):
        return (dm * ga, dm * gs, dm * ya * ga * (1.0 - ga), dm * ys * gs * (1.0 - gs))

    dya, dys, dzga, dzgs = _mm("mix_out_bwd", [(dx2, wf["w_out"])], True, D_MODEL, [BF16] * 4, epilogue=merge_bwd,
                               extras=[(gates, 0), (gates, D_MODEL), (y_attn, 0), (y_ssm, 0)], deps=sent)
    gfull["w_out"] = _mm_tn("dw_out", merged, dx2)
    gsmall["gate_bias"] = jnp.concatenate([_colsum("dgate_bias_a", dzga), _colsum("dgate_bias_s", dzgs)], axis=1)

    gfull["w_ssm_branch"] = _mm_tn("dw_ssm_branch", dys, ysg)

    def glu_bwd(dysg, av, bv):
        sb = _sigmoid(bv)
        return (dysg * sb, dysg * av * sb * (1.0 - sb))

    dglu_a, dglu_b = _mm("ssm_branch_bwd", [(dys, wf["w_ssm_branch"])], False, SSM_WIDTH, [BF16, BF16],
                         epilogue=glu_bwd, extras=[(glu, 0), (glu, SSM_WIDTH)])
    w_glu = wf["ssm_w_glu"]
    gfull["ssm_w_glu"] = jnp.concatenate([_mm_tn("dw_glu_a", dglu_a, ygelu), _mm_tn("dw_glu_b", dglu_b, ygelu)], axis=0)

    def gelu_bwd(acc, yv):
        _, vjp = jax.vjp(jax.nn.gelu, yv)
        return (vjp(acc)[0],)

    dy_raw, = _mm("ssm_glu_bwd", [(dglu_a, w_glu[:SSM_WIDTH]), (dglu_b, w_glu[SSM_WIDTH:])], False, SSM_WIDTH, [F32],
                  epilogue=gelu_bwd, extras=[(y_raw, 0)])
    dy_perm = _scan_order(dy_raw)
    du_perm, g_states, dab_rows, gsmall_d = _ssm_bwd(
        "ssm_bwd", dy_perm, u_perm, states, c_mat_t, bb_mat.T, lanes(ab_re, ab_im, -1.0), lanes(ap_re, ap_im, -1.0), d_skip)
    du = _time_order(du_perm)
    gsmall["ssm_d"] = gsmall_d
    dbb_acc = _mm_tn("ssm_dbb", u_perm, g_states, bm=SSM_WIDTH)
    dc_acc = _mm_tn("ssm_dc", dy_perm, states, bm=SSM_WIDTH)
    dbb_re = _diag_blocks(dbb_acc[:, :SSM_LANES], SSM_GROUP, SSM_STATE).transpose(1, 0, 2)
    dbb_im = _diag_blocks(dbb_acc[:, SSM_LANES:], SSM_GROUP, SSM_STATE).transpose(1, 0, 2)
    gsmall["ssm_c_re"] = _diag_blocks(dc_acc[:, :SSM_LANES], SSM_GROUP, SSM_STATE)
    gsmall["ssm_c_im"] = -_diag_blocks(dc_acc[:, SSM_LANES:], SSM_GROUP, SSM_STATE)
    dab = _colsum("ssm_dab", dab_rows)
    d_ar, d_ai, d_ld, d_br, d_bi = _ssm_params_bwd(
        "ssm_params_bwd", small["ssm_a_re"], small["ssm_a_im"], small["ssm_log_dt"].reshape(SSM_GROUPS, 1),
        small["ssm_b_re"].transpose(2, 0, 1), small["ssm_b_im"].transpose(2, 0, 1),
        dab[:, :SSM_LANES].reshape(SSM_GROUPS, SSM_STATE), dab[:, SSM_LANES:].reshape(SSM_GROUPS, SSM_STATE),
        dbb_re, dbb_im)
    gsmall["ssm_a_re"], gsmall["ssm_a_im"], gsmall["ssm_log_dt"] = d_ar, d_ai, d_ld.reshape(SSM_GROUPS)
    gsmall["ssm_b_re"], gsmall["ssm_b_im"] = d_br.transpose(1, 2, 0), d_bi.transpose(1, 2, 0)

    gfull["w_attn_branch"] = _mm_tn("dw_attn_branch", dya, oa)
    doa, = _mm("attn_branch_bwd", [(dya, wf["w_attn_branch"])], False, ATTN_OUT, [F32])
    do_h = doa.reshape(rows, HEADS_PER_GROUP, HEAD_DIM).transpose(1, 0, 2)
    dc = _combine_bwd("attn_combine_bwd", do_h, oa_h, lse_h)
    dqkv_cols = [None] * 9
    dbias = []
    for g, dil in enumerate(DILATIONS):
        qd, kd, vd, bias_g = qkv_d[g]
        dq, dk, dv, db = _attn_bwd(f"attn_bwd_{g}", qd, kd, vd, _heads_to_dilated(dc[g], dil),
                                   _heads_to_dilated(lse_h[g], dil), _heads_to_dilated(dc[3 + g], dil), bias_g)
        dbias.append(db)
        for s, (arr, sc) in enumerate(((dq, HEAD_DIM ** -0.5), (dk, 1.0), (dv, 1.0))):
            tok = _dilated_to_heads(arr).transpose(1, 0, 2).reshape(rows, ATTN_OUT)
            dqkv_cols[3 * s + g] = (tok * sc).astype(BF16)
    dqkv = jnp.concatenate(dqkv_cols, axis=1)
    gsmall["rel_bias_table"] = _bias_bwd("rel_bias_bwd", buckets, jnp.concatenate(dbias, axis=0))[:, :N_GROUPS * HEADS_PER_GROUP]

    gfull["w_in"] = jnp.concatenate([
        _mm_tn("dw_in_qkv", dqkv, hmix), _mm_tn("dw_in_u", du, hmix),
        _mm_tn("dw_in_ga", dzga, hmix), _mm_tn("dw_in_gs", dzgs, hmix)], axis=0)
    sent = send_grads("mix", gfull)
    dhmix, = _mm("in_bwd", [(dqkv, w_qkv), (du, w_u), (dzga, w_g[:D_MODEL]), (dzgs, w_g[D_MODEL:])], False, D_MODEL,
                 [F32], tm=512, deps=sent)
    dx1, gsmall["mix_norm"] = _rms_bwd("mix_norm_bwd", dhmix, x1, small["mix_norm"], dx2)

    dx, dgg1, duu1, act1, gsmall["ffn1_norm"] = _ffn_bwd(
        "ffn1_bwd", dx1, xs, small["ffn1_norm"], gg1, uu1, wf["ffn1_w_gate"], wf["ffn1_w_up"], wf["ffn1_w_down"])
    gfull["ffn1_w_down"] = _mm_tn("ffn1_dwd", act1, dx1, scale=0.5)
    sent = send_grads("f1d", gfull)
    gfull["ffn1_w_gate"] = _mm_tn("ffn1_dwg", dgg1, h1, deps=sent)
    gfull["ffn1_w_up"] = _mm_tn("ffn1_dwu", duu1, h1)
    send_grads("f1gu", gfull)
    return loss[0, 0], dx, gsmall


def kernel(x, ffn1_norm, ffn1_w_gate, ffn1_w_up, ffn1_w_down, mix_norm, w_in, gate_bias, rel_bias_table, ssm_a_re, ssm_a_im, ssm_log_dt, ssm_b_re, ssm_b_im, ssm_c_re, ssm_c_im, ssm_d, ssm_w_glu, w_attn_branch, w_ssm_branch, w_out, ffn2_norm, ffn2_w_gate, ffn2_w_up, ffn2_w_down, final_norm, loss_target, m_ffn1_norm, m_ffn1_w_gate, m_ffn1_w_up, m_ffn1_w_down, m_mix_norm, m_w_in, m_gate_bias, m_rel_bias_table, m_ssm_a_re, m_ssm_a_im, m_ssm_log_dt, m_ssm_b_re, m_ssm_b_im, m_ssm_c_re, m_ssm_c_im, m_ssm_d, m_ssm_w_glu, m_w_attn_branch, m_w_ssm_branch, m_w_out, m_ffn2_norm, m_ffn2_w_gate, m_ffn2_w_up, m_ffn2_w_down, m_final_norm, v_ffn1_norm, v_ffn1_w_gate, v_ffn1_w_up, v_ffn1_w_down, v_mix_norm, v_w_in, v_gate_bias, v_rel_bias_table, v_ssm_a_re, v_ssm_a_im, v_ssm_log_dt, v_ssm_b_re, v_ssm_b_im, v_ssm_c_re, v_ssm_c_im, v_ssm_d, v_ssm_w_glu, v_w_attn_branch, v_w_ssm_branch, v_w_out, v_ffn2_norm, v_ffn2_w_gate, v_ffn2_w_up, v_ffn2_w_down, v_final_norm):
    given = dict(locals())
    shapes = {nm: given[nm].shape for nm in _ORDER}

    def strip(a):
        return a[0] if a.ndim >= 2 and a.shape[0] == 1 else a

    w = {nm: strip(given[nm]) for nm in _ORDER}
    m = {nm: strip(given["m_" + nm]) for nm in _ORDER}
    v = {nm: strip(given["v_" + nm]) for nm in _ORDER}
    for d in (w, m, v):
        d["rel_bias_table"] = d["rel_bias_table"].reshape(N_BUCKETS, N_GROUPS * HEADS_PER_GROUP)

    small = {nm: w[nm] for nm in _SMALL}
    small_in = dict(small)
    for nm in ("ffn1_norm", "mix_norm", "ffn2_norm", "gate_bias"):
        small_in[nm] = small[nm].reshape(1, -1)
    w_pack = {ph: _pack_sharded(w, names) for ph, names in _PHASES.items()}

    f1_names = _PHASES["f1gu"] + _PHASES["f1d"]
    got_f1 = _all_gather("gather_f1", jnp.concatenate([w_pack["f1gu"], w_pack["f1d"]], axis=0).astype(BF16))
    pending_w = {"mix": _exchange_start("gather_mix_start", w_pack["mix"].astype(BF16), gather=True, deps=[got_f1])}
    pending_w["f2"] = _exchange_start("gather_f2_start", w_pack["f2"].astype(BF16), gather=True,
                                      deps=[pending_w["mix"][4]])

    def weights_of(phase, after):
        if phase == "f1":
            return _unpack_gathered(got_f1, f1_names)
        return _unpack_gathered(_exchange_wait(f"gather_{phase}_wait", pending_w[phase], after, gather=True),
                                _PHASES[phase])

    pending_g = {}

    def send_grads(phase, grads):
        pending_g[phase] = _exchange_start(f"scatter_{phase}_start", _pack_grads(grads, _PHASES[phase]), gather=False)
        return [pending_g[phase][4]]

    loss, dx, gsmall = _local_step(x[0], loss_target[0], small_in, weights_of, send_grads,
                                   first_deps=[pending_w["f2"][4]])

    packs = {}
    after = pending_g["f1gu"][4]
    for phase in ("f2", "mix", "f1d", "small", "f1gu"):
        if phase == "small":
            gs_pack = _pack_small({nm: gsmall[nm].reshape(small[nm].shape) for nm in _SMALL})
            gs_all = _all_gather("gather_small_grads", gs_pack)
            sm = _adamw("adamw_small", _pack_small(small), _pack_small({nm: m[nm] for nm in _SMALL}),
                        _pack_small({nm: v[nm] for nm in _SMALL}), gs_all, gs_pack.shape[0])
            after = sm[0]
            continue
        names = _PHASES[phase]
        recv = _exchange_wait(f"scatter_{phase}_wait", pending_g[phase], after, gather=False)
        rows_p = w_pack[phase].shape[0]
        tr = max(t for t in range(16, 129, 16) if rows_p % t == 0)
        packs[phase] = _adamw(f"adamw_{phase}", w_pack[phase], _pack_sharded(m, names), _pack_sharded(v, names),
                              recv, tr)
        after = packs[phase][0]

    loss = lax.psum(loss, ("x", "y", "c"))
    outs = []
    for i in range(4):
        big = {}
        for phase, names in _PHASES.items():
            big.update(_unpack_sharded(packs[phase][i], names))
        sml = _unpack_small(sm[i], small)
        outs.append([(big[nm] if nm in big else sml[nm]).reshape(shapes[nm]) for nm in _ORDER])
    return (loss, dx[None], *outs[0], *outs[1], *outs[2], *outs[3])
```

```python
import functools
import math

import numpy as np
import jax
import jax.numpy as jnp
from jax import lax
from jax.experimental import pallas as pl
from jax.experimental.pallas import tpu as pltpu

F32 = jnp.float32
BF16 = jnp.bfloat16

N_DEV = 8
D_MODEL = 1024
D_FF = 2816
HEAD_DIM = 64
HEADS_PER_GROUP = 4
DILATIONS = (1, 4, 16)
N_GROUPS = 3
ATTN_WIDTH = 768
ATTN_OUT = 256
BLOCK = 128
N_BUCKETS = 32
MAX_DISTANCE = 2048
NEG_INF = -1e30
SSM_WIDTH = 512
SSM_GROUPS = 32
SSM_GROUP = 16
SSM_STATE = 64
SSM_LANES = SSM_GROUPS * SSM_STATE
EPS = 1e-6
LR, B1, B2, ADAM_EPS, WD, STEP = 0.001, 0.9, 0.999, 1e-08, 0.01, 10

VMEM_LIMIT_BYTES = 56 * 1024 * 1024
SCAN_BLOCK = 256
SCAN_SUB = 8
SCAN_STEPS = SCAN_BLOCK // SCAN_SUB
SCAN_LANES = 512

MESH = pl.DeviceIdType.MESH


def _cparams(*sem):
    return pltpu.CompilerParams(dimension_semantics=sem, vmem_limit_bytes=VMEM_LIMIT_BYTES)


def _dot(a, b, dims):
    return lax.dot_general(a, b, (dims, ((), ())), preferred_element_type=F32)


def _dot_nn(a, b):
    return _dot(a, b, ((1,), (0,)))


def _dot_nt(a, b):
    return _dot(a, b, ((1,), (1,)))


def _dot_tn(a, b):
    return _dot(a, b, ((0,), (0,)))


def _sigmoid(x):
    return 1.0 / (1.0 + jnp.exp(-x))


def _all_gather(name, xs):
    rows, cols = xs.shape

    def body(x_ref, out_ref, send_sems, recv_sems, local_sem):
        x, y, c = lax.axis_index("x"), lax.axis_index("y"), lax.axis_index("c")
        me, sibling = (x, y, c), (x, y, 1 - c)
        chips = [(1 - x, y), (x, 1 - y), (1 - x, 1 - y)]

        def slot(px, py, pc):
            return out_ref.at[4 * px + 2 * py + pc]

        def copy(k, block, to, src=None):
            return pltpu.make_async_remote_copy(
                src_ref=slot(*block) if src is None else src, dst_ref=slot(*block),
                send_sem=send_sems.at[k], recv_sem=recv_sems.at[k], device_id=to, device_id_type=MESH)

        mine = pltpu.make_async_copy(x_ref, slot(*me), local_sem)
        mine.start()
        first = [copy(0, me, sibling, src=x_ref)]
        first += [copy(1 + j, me, (*chip, c), src=x_ref) for j, chip in enumerate(chips)]
        for cp in first:
            cp.start()
        passed = [copy(4 + j, (*chip, c), sibling) for j, chip in enumerate(chips)]
        for j, chip in enumerate(chips):
            copy(1 + j, (*chip, c), me).wait_recv()
            passed[j].start()
        copy(0, sibling, me).wait_recv()
        for j, chip in enumerate(chips):
            copy(4 + j, (*chip, 1 - c), me).wait_recv()
        for cp in first + passed:
            cp.wait_send()
        mine.wait()

    return pl.pallas_call(
        body, name=name,
        out_shape=jax.ShapeDtypeStruct((N_DEV, rows, cols), xs.dtype),
        in_specs=[pl.BlockSpec(memory_space=pl.ANY)],
        out_specs=pl.BlockSpec(memory_space=pl.ANY),
        scratch_shapes=[pltpu.SemaphoreType.DMA((7,)), pltpu.SemaphoreType.DMA((7,)), pltpu.SemaphoreType.DMA],
    )(xs)


_HBM_SPEC = pl.BlockSpec(memory_space=pltpu.HBM)
_SEM_SPEC = pl.BlockSpec(memory_space=pltpu.SEMAPHORE)
_EFFECT = pltpu.SideEffectType.DATAFLOW_SIDE_EFFECTING


def _peers(x, y, c):
    return [(1 - x if k & 4 else x, 1 - y if k & 2 else y, 1 - c if k & 1 else c) for k in range(1, N_DEV)]


def _exchange_copies(x_ref, land_ref, send_sems, recv_sems, gather):
    x, y, c = lax.axis_index("x"), lax.axis_index("y"), lax.axis_index("c")
    me = 4 * x + 2 * y + c
    copies = []
    for k, (px, py, pc) in enumerate(_peers(x, y, c)):
        src = x_ref if gather else x_ref.at[4 * px + 2 * py + pc]
        copies.append(pltpu.make_async_remote_copy(
            src_ref=src, dst_ref=land_ref.at[me], send_sem=send_sems.at[k], recv_sem=recv_sems.at[k],
            device_id=(px, py, pc), device_id_type=MESH))
    own = pltpu.make_async_copy(x_ref if gather else x_ref.at[me], land_ref.at[me], send_sems.at[N_DEV - 1])
    return own, copies


_ANY_SPEC = pl.BlockSpec(memory_space=pl.ANY)


def _exchange_start(name, xs, gather, deps=()):
    land_shape = (N_DEV, *xs.shape) if gather else xs.shape
    nd = len(deps)

    def body(x_ref, land_ref, *rest):
        send_sems, recv_sems, _, _, token = rest[nd:]
        own, copies = _exchange_copies(x_ref, land_ref, send_sems, recv_sems, gather)
        for cp in copies:
            cp.start()
        own.start()
        token[...] = jnp.zeros_like(token)

    return pl.pallas_call(
        body, name=name,
        out_shape=(pltpu.SemaphoreType.DMA((N_DEV,)), pltpu.SemaphoreType.DMA((N_DEV - 1,)),
                   pltpu.HBM(xs.shape, xs.dtype), pltpu.HBM(land_shape, xs.dtype), jax.ShapeDtypeStruct((8, 128), F32)),
        in_specs=(_HBM_SPEC, _HBM_SPEC) + (_ANY_SPEC,) * nd,
        out_specs=(_SEM_SPEC, _SEM_SPEC, _HBM_SPEC, _HBM_SPEC, pl.BlockSpec(memory_space=pltpu.VMEM)),
        input_output_aliases={0: 2, 1: 3},
        compiler_params=pltpu.CompilerParams(has_side_effects=_EFFECT),
    )(pltpu.with_memory_space_constraint(xs, pltpu.HBM),
      pltpu.with_memory_space_constraint(lax.empty(land_shape, xs.dtype), pltpu.HBM), *deps)


def _exchange_wait(name, handle, after, gather):
    send_sems, recv_sems, x_thru, land_thru, _ = handle

    def body(x_ref, land_ref, send_sems, recv_sems, after_ref, x_dead, got_ref):
        own, copies = _exchange_copies(x_ref, land_ref, send_sems, recv_sems, gather)
        for cp in copies:
            cp.wait_send()
            cp.wait_recv()
        own.wait()

    return pl.pallas_call(
        body, name=name,
        out_shape=(pltpu.HBM(x_thru.shape, x_thru.dtype), pltpu.HBM(land_thru.shape, land_thru.dtype)),
        in_specs=(_HBM_SPEC, _HBM_SPEC, _SEM_SPEC, _SEM_SPEC, pl.BlockSpec(memory_space=pl.ANY)),
        out_specs=(_HBM_SPEC, _HBM_SPEC), input_output_aliases={0: 0, 1: 1},
        compiler_params=pltpu.CompilerParams(has_side_effects=_EFFECT),
    )(x_thru, land_thru, send_sems, recv_sems, after)[1]


def _mm(name, pairs, nt, n_cols, out_dtypes, epilogue=None, extras=(), tm=1024, tn=512, deps=()):
    rows = pairs[0][0].shape[0]
    tm = min(tm, rows)
    tn = min(tn, n_cols)
    na, ne, nd = len(pairs), len(extras), len(deps)

    def body(*refs):
        a_refs, w_refs = refs[:na], refs[na:2 * na]
        e_refs, o_refs = refs[2 * na:2 * na + ne], refs[2 * na + ne + nd:]
        acc = None
        for a_ref, w_ref in zip(a_refs, w_refs):
            a = a_ref[...].astype(BF16)
            w = w_ref[...].astype(BF16)
            p = _dot_nt(a, w) if nt else _dot_nn(a, w)
            acc = p if acc is None else acc + p
        outs = (acc,) if epilogue is None else epilogue(acc, *[e[...] for e in e_refs])
        for o_ref, o in zip(o_refs, outs):
            o_ref[...] = o.astype(o_ref.dtype)

    in_specs = [pl.BlockSpec((tm, a.shape[1]), lambda i, j: (i, 0)) for a, _ in pairs]
    for _, w in pairs:
        if nt:
            in_specs.append(pl.BlockSpec((tn, w.shape[1]), lambda i, j: (j, 0)))
        else:
            in_specs.append(pl.BlockSpec((w.shape[0], tn), lambda i, j: (0, j)))
    for e, col_off in extras:
        off = col_off // tn
        if e.shape[0] == 1:
            in_specs.append(pl.BlockSpec((1, tn), lambda i, j, off=off: (0, j + off)))
        else:
            in_specs.append(pl.BlockSpec((tm, tn), lambda i, j, off=off: (i, j + off)))
    in_specs += [_ANY_SPEC] * nd
    out_specs = [pl.BlockSpec((tm, tn), lambda i, j: (i, j)) for _ in out_dtypes]
    outs = pl.pallas_call(
        body, name=name, grid=(rows // tm, n_cols // tn),
        in_specs=in_specs, out_specs=out_specs,
        out_shape=[jax.ShapeDtypeStruct((rows, n_cols), dt) for dt in out_dtypes],
        compiler_params=_cparams("parallel", "arbitrary"),
    )(*[a for a, _ in pairs], *[w for _, w in pairs], *[e for e, _ in extras], *deps)
    return outs


def _tn_rows(m):
    return max(b for b in range(128, min(m, 1408) + 1, 128) if m % b == 0)


def _mm_tn(name, a, b, scale=1.0, bm=None, tk=1024, deps=()):
    rows, m = a.shape
    n = b.shape[1]
    bm = _tn_rows(m) if bm is None else bm
    tk = min(tk, rows)
    nk = rows // tk

    def body(a_ref, b_ref, *rest):
        o_ref = rest[-1]
        k = pl.program_id(1)

        @pl.when(k == 0)
        def _():
            o_ref[...] = jnp.zeros_like(o_ref)

        o_ref[...] += _dot_tn(a_ref[...].astype(BF16), b_ref[...].astype(BF16))
        if scale != 1.0:
            @pl.when(k == nk - 1)
            def _():
                o_ref[...] = o_ref[...] * scale

    return pl.pallas_call(
        body, name=name, grid=(m // bm, nk),
        in_specs=[pl.BlockSpec((tk, bm), lambda i, k: (k, i)), pl.BlockSpec((tk, n), lambda i, k: (k, 0))]
        + [_ANY_SPEC] * len(deps),
        out_specs=pl.BlockSpec((bm, n), lambda i, k: (i, 0)),
        out_shape=jax.ShapeDtypeStruct((m, n), F32),
        compiler_params=_cparams("parallel", "arbitrary"),
    )(a, b, *deps)


def _colsum(name, xs, tm=512):
    rows, cols = xs.shape
    tm = min(tm, rows)

    def body(x_ref, o_ref):
        @pl.when(pl.program_id(0) == 0)
        def _():
            o_ref[...] = jnp.zeros_like(o_ref)

        o_ref[...] += jnp.sum(x_ref[...].astype(F32), axis=0, keepdims=True)

    return pl.pallas_call(
        body, name=name, grid=(rows // tm,),
        in_specs=[pl.BlockSpec((tm, cols), lambda i: (i, 0))],
        out_specs=pl.BlockSpec((1, cols), lambda i: (0, 0)),
        out_shape=jax.ShapeDtypeStruct((1, cols), F32),
        compiler_params=_cparams("arbitrary"),
    )(xs)


def _ew(name, fn, ins, out_cols, out_dtypes, tm=512):
    rows = ins[0].shape[0]
    tm = min(tm, rows)
    ni = len(ins)

    def body(*refs):
        outs = fn(*[r[...] for r in refs[:ni]])
        for o_ref, o in zip(refs[ni:], outs):
            o_ref[...] = o.astype(o_ref.dtype)

    def spec(shape):
        if shape[0] == 1:
            return pl.BlockSpec((1, shape[1]), lambda i: (0, 0))
        return pl.BlockSpec((tm, shape[1]), lambda i: (i, 0))

    return pl.pallas_call(
        body, name=name, grid=(rows // tm,),
        in_specs=[spec(a.shape) for a in ins],
        out_specs=[pl.BlockSpec((tm, c), lambda i: (i, 0)) for c in out_cols],
        out_shape=[jax.ShapeDtypeStruct((rows, c), dt) for c, dt in zip(out_cols, out_dtypes)],
        compiler_params=_cparams("parallel"),
    )(*ins)


def _rms_parts(xv):
    r = lax.rsqrt(jnp.mean(xv * xv, axis=-1, keepdims=True) + EPS)
    return r, xv * r


def _rms_bwd_dx(dh, gain, r, xh):
    dxh = dh * gain
    return r * (dxh - xh * jnp.mean(dxh * xh, axis=-1, keepdims=True))


def _rms_fwd(name, xs, gain):
    def fn(xv, g):
        _, xh = _rms_parts(xv)
        return (xh * g,)

    return _ew(name, fn, [xs, gain], [xs.shape[1]], [BF16])[0]


def _rms_bwd(name, dh, xs, gain, dres, tm=512, deps=()):
    rows, d = xs.shape
    tm = min(tm, rows)

    def body(dh_ref, x_ref, g_ref, dres_ref, *rest):
        dx_ref, dg_ref = rest[-2:]
        r, xh = _rms_parts(x_ref[...])
        dhv = dh_ref[...]
        dx_ref[...] = dres_ref[...] + _rms_bwd_dx(dhv, g_ref[...], r, xh)

        @pl.when(pl.program_id(0) == 0)
        def _():
            dg_ref[...] = jnp.zeros_like(dg_ref)

        dg_ref[...] += jnp.sum(dhv * xh, axis=0, keepdims=True)

    tile = pl.BlockSpec((tm, d), lambda i: (i, 0))
    row = pl.BlockSpec((1, d), lambda i: (0, 0))
    return pl.pallas_call(
        body, name=name, grid=(rows // tm,),
        in_specs=[tile, tile, row, tile] + [_ANY_SPEC] * len(deps), out_specs=[tile, row],
        out_shape=[jax.ShapeDtypeStruct((rows, d), F32), jax.ShapeDtypeStruct((1, d), F32)],
        compiler_params=_cparams("arbitrary"),
    )(dh, xs, gain, dres, *deps)


def _ffn_fwd(name, xs, gain, wg_t, wu_t, wd, tm=512, tf=1408, deps=()):
    rows, d = xs.shape
    f_all = wd.shape[0]
    tm = min(tm, rows)
    nf = f_all // tf

    def body(x_ref, g_ref, wg_ref, wu_ref, wd_ref, *rest):
        xo_ref, h_ref, gg_ref, uu_ref, acc_ref = rest[-5:]
        f = pl.program_id(1)

        @pl.when(f == 0)
        def _():
            _, xh = _rms_parts(x_ref[...])
            h_ref[...] = (xh * g_ref[...]).astype(BF16)
            acc_ref[...] = jnp.zeros_like(acc_ref)

        h = h_ref[...]
        gg = _dot_nt(h, wg_ref[...])
        uu = _dot_nt(h, wu_ref[...])
        act = gg * _sigmoid(gg) * uu
        acc_ref[...] += _dot_nn(act.astype(BF16), wd_ref[...])
        gg_ref[...] = gg.astype(BF16)
        uu_ref[...] = uu.astype(BF16)

        @pl.when(f == nf - 1)
        def _():
            xo_ref[...] = x_ref[...] + 0.5 * acc_ref[...]

    tile = pl.BlockSpec((tm, d), lambda i, f: (i, 0))
    wspec = pl.BlockSpec((tf, d), lambda i, f: (f, 0))
    hid = pl.BlockSpec((tm, tf), lambda i, f: (i, f))
    return pl.pallas_call(
        body, name=name, grid=(rows // tm, nf),
        in_specs=[tile, pl.BlockSpec((1, d), lambda i, f: (0, 0)), wspec, wspec, wspec] + [_ANY_SPEC] * len(deps),
        out_specs=[tile, tile, hid, hid],
        out_shape=[jax.ShapeDtypeStruct((rows, d), F32), jax.ShapeDtypeStruct((rows, d), BF16),
                   jax.ShapeDtypeStruct((rows, f_all), BF16), jax.ShapeDtypeStruct((rows, f_all), BF16)],
        scratch_shapes=[pltpu.VMEM((tm, d), F32)],
        compiler_params=_cparams("parallel", "arbitrary"),
    )(xs, gain, wg_t, wu_t, wd, *deps)


def _ffn_bwd(name, dxo, xs, gain, gg_all, uu_all, wg_t, wu_t, wd, tm=256, tf=1408):
    rows, d = xs.shape
    f_all = wd.shape[0]
    tm = min(tm, rows)
    nf = f_all // tf

    def body(dxo_ref, x_ref, g_ref, gg_ref, uu_ref, wg_ref, wu_ref, wd_ref,
             dx_ref, dgg_ref, duu_ref, act_ref, dgain_ref, df_ref, acc_ref):
        i, f = pl.program_id(0), pl.program_id(1)

        @pl.when(f == 0)
        def _():
            df_ref[...] = (0.5 * dxo_ref[...]).astype(BF16)
            acc_ref[...] = jnp.zeros_like(acc_ref)

        gg = gg_ref[...].astype(F32)
        uu = uu_ref[...].astype(F32)
        sg = _sigmoid(gg)
        silu = gg * sg
        dact = _dot_nt(df_ref[...], wd_ref[...])
        duu = (dact * silu).astype(BF16)
        dgg = (dact * uu * (sg * (1.0 + gg * (1.0 - sg)))).astype(BF16)
        act_ref[...] = (silu * uu).astype(BF16)
        dgg_ref[...] = dgg
        duu_ref[...] = duu
        acc_ref[...] += _dot_nn(dgg, wg_ref[...]) + _dot_nn(duu, wu_ref[...])

        @pl.when(f == nf - 1)
        def _():
            r, xh = _rms_parts(x_ref[...])
            dh = acc_ref[...]
            dx_ref[...] = dxo_ref[...] + _rms_bwd_dx(dh, g_ref[...], r, xh)
            part = jnp.sum(dh * xh, axis=0, keepdims=True)

            @pl.when(i == 0)
            def _():
                dgain_ref[...] = part

            @pl.when(i > 0)
            def _():
                dgain_ref[...] += part

    tile = pl.BlockSpec((tm, d), lambda i, f: (i, 0))
    row = pl.BlockSpec((1, d), lambda i, f: (0, 0))
    wspec = pl.BlockSpec((tf, d), lambda i, f: (f, 0))
    hid = pl.BlockSpec((tm, tf), lambda i, f: (i, f))
    hid_shape = jax.ShapeDtypeStruct((rows, f_all), BF16)
    return pl.pallas_call(
        body, name=name, grid=(rows // tm, nf),
        in_specs=[tile, tile, row, hid, hid, wspec, wspec, wspec],
        out_specs=[tile, hid, hid, hid, row],
        out_shape=[jax.ShapeDtypeStruct((rows, d), F32), hid_shape, hid_shape, hid_shape,
                   jax.ShapeDtypeStruct((1, d), F32)],
        scratch_shapes=[pltpu.VMEM((tm, d), BF16), pltpu.VMEM((tm, d), F32)],
        compiler_params=_cparams("arbitrary", "arbitrary"),
    )(dxo, xs, gain, gg_all, uu_all, wg_t, wu_t, wd)


def _final_loss(name, xs, gain, target, tm=512):
    rows, d = xs.shape
    tm = min(tm, rows)

    def body(x_ref, g_ref, t_ref, dx_ref, dg_ref, loss_ref):
        r, xh = _rms_parts(x_ref[...])
        gain_v = g_ref[...]
        err = xh * gain_v - t_ref[...]
        dy = err * (1.0 / d)
        dx_ref[...] = _rms_bwd_dx(dy, gain_v, r, xh)

        @pl.when(pl.program_id(0) == 0)
        def _():
            dg_ref[...] = jnp.zeros_like(dg_ref)
            loss_ref[...] = jnp.zeros_like(loss_ref)

        dg_ref[...] += jnp.sum(dy * xh, axis=0, keepdims=True)
        per_tok = jnp.mean(err * err, axis=-1, keepdims=True)
        loss_ref[...] += 0.5 * jnp.sum(per_tok, axis=0, keepdims=True)

    tile = pl.BlockSpec((tm, d), lambda i: (i, 0))
    row = pl.BlockSpec((1, d), lambda i: (0, 0))
    return pl.pallas_call(
        body, name=name, grid=(rows // tm,),
        in_specs=[tile, row, tile],
        out_specs=[tile, row, pl.BlockSpec((1, 1), lambda i: (0, 0))],
        out_shape=[jax.ShapeDtypeStruct((rows, d), F32), jax.ShapeDtypeStruct((1, d), F32),
                   jax.ShapeDtypeStruct((1, 1), F32)],
        compiler_params=_cparams("arbitrary"),
    )(xs, gain, target)


def _bucket_table():
    out = []
    for dil in DILATIONS:
        qi = np.arange(BLOCK)[:, None]
        kj = np.arange(2 * BLOCK)[None, :]
        dist = (np.maximum(qi + BLOCK - kj, 0) * dil).astype(np.int32)
        max_exact = N_BUCKETS // 2
        dd = np.maximum(dist, 1).astype(np.float32)
        large = max_exact + (np.log(dd / np.float32(max_exact)) / np.float32(math.log(MAX_DISTANCE / max_exact))
                             * np.float32(N_BUCKETS - max_exact)).astype(np.int32)
        large = np.minimum(large, N_BUCKETS - 1)
        out.append(np.where(dist < max_exact, dist, large).astype(np.int32))
    return np.stack(out)


def _bias_fwd(name, buckets, table):
    def body(bk_ref, tab_ref, o_ref):
        for g in range(N_GROUPS):
            bk = bk_ref[g]
            for h in range(HEADS_PER_GROUP):
                col = g * HEADS_PER_GROUP + h
                acc = jnp.zeros((BLOCK, 2 * BLOCK), F32)
                for b in range(N_BUCKETS):
                    acc = jnp.where(bk == b, tab_ref[b, col], acc)
                o_ref[col] = acc

    return pl.pallas_call(
        body, name=name,
        in_specs=[pl.BlockSpec(memory_space=pltpu.VMEM), pl.BlockSpec(memory_space=pltpu.SMEM)],
        out_specs=pl.BlockSpec(memory_space=pltpu.VMEM),
        out_shape=jax.ShapeDtypeStruct((N_GROUPS * HEADS_PER_GROUP, BLOCK, 2 * BLOCK), F32),
    )(buckets, table)


def _bias_bwd(name, buckets, dbias):
    def body(bk_ref, db_ref, o_ref):
        row_id = lax.broadcasted_iota(jnp.int32, (N_BUCKETS, 128), 0)
        col_id = lax.broadcasted_iota(jnp.int32, (N_BUCKETS, 128), 1)
        acc = jnp.zeros((N_BUCKETS, 128), F32)
        for g in range(N_GROUPS):
            bk = bk_ref[g]
            for h in range(HEADS_PER_GROUP):
                col = g * HEADS_PER_GROUP + h
                db = db_ref[col]
                for b in range(N_BUCKETS):
                    part = jnp.sum(jnp.where(bk == b, db, 0.0), axis=0, keepdims=True)
                    tot = jnp.sum(part, axis=1, keepdims=True)
                    acc = jnp.where((row_id == b) & (col_id == col), tot, acc)
        o_ref[...] = acc

    return pl.pallas_call(
        body, name=name,
        in_specs=[pl.BlockSpec(memory_space=pltpu.VMEM), pl.BlockSpec(memory_space=pltpu.VMEM)],
        out_specs=pl.BlockSpec(memory_space=pltpu.VMEM),
        out_shape=jax.ShapeDtypeStruct((N_BUCKETS, 128), F32),
    )(buckets, dbias)


def _band_mask(n):
    qi = lax.broadcasted_iota(jnp.int32, (BLOCK, 2 * BLOCK), 0)
    kj = lax.broadcasted_iota(jnp.int32, (BLOCK, 2 * BLOCK), 1)
    return (kj >= qi) & (kj <= qi + BLOCK) & ((kj >= BLOCK) | (n > 0))


def _attn_fwd(name, q, k, v, bias):
    dil, nh, m, dh = q.shape
    nb = m // BLOCK

    def body(q_ref, kc_ref, kp_ref, vc_ref, vp_ref, b_ref, o_ref, lse_ref):
        mask = _band_mask(pl.program_id(1))
        for h in range(nh):
            k2 = jnp.concatenate([kp_ref[0, h], kc_ref[0, h]], axis=0)
            v2 = jnp.concatenate([vp_ref[0, h], vc_ref[0, h]], axis=0)
            s = _dot_nt(q_ref[0, h], k2) + b_ref[h]
            s = jnp.where(mask, s, NEG_INF)
            mx = jnp.max(s, axis=-1, keepdims=True)
            p = jnp.exp(s - mx)
            den = jnp.sum(p, axis=-1, keepdims=True)
            o_ref[0, h] = _dot_nn(p.astype(BF16), v2) / den
            lse_ref[0, h] = jnp.broadcast_to(mx + jnp.log(den), (BLOCK, dh))

    cur = pl.BlockSpec((1, nh, BLOCK, dh), lambda r, n: (r, 0, n, 0))
    prev = pl.BlockSpec((1, nh, BLOCK, dh), lambda r, n: (r, 0, jnp.maximum(n - 1, 0), 0))
    return pl.pallas_call(
        body, name=name, grid=(dil, nb),
        in_specs=[cur, cur, prev, cur, prev, pl.BlockSpec((nh, BLOCK, 2 * BLOCK), lambda r, n: (0, 0, 0))],
        out_specs=[cur, cur],
        out_shape=[jax.ShapeDtypeStruct(q.shape, F32), jax.ShapeDtypeStruct(q.shape, F32)],
        compiler_params=_cparams("parallel", "arbitrary"),
    )(q, k, k, v, v, bias)


def _attn_bwd(name, q, k, v, do, lse, cvec, bias):
    dil, nh, m, dh = q.shape
    nb = m // BLOCK

    def body(q_ref, kc_ref, kp_ref, vc_ref, vp_ref, do_ref, lse_ref, c_ref, b_ref,
             dq_ref, dk_ref, dv_ref, db_ref, kcar_ref, vcar_ref):
        r, n = pl.program_id(0), pl.program_id(1)
        valid = n < nb
        mask = _band_mask(n) & valid

        @pl.when((r == 0) & (n == 0))
        def _():
            kcar_ref[...] = jnp.zeros_like(kcar_ref)
            vcar_ref[...] = jnp.zeros_like(vcar_ref)
            db_ref[...] = jnp.zeros_like(db_ref)

        for h in range(nh):
            qh = q_ref[0, h]
            k2 = jnp.concatenate([kp_ref[0, h], kc_ref[0, h]], axis=0)
            v2 = jnp.concatenate([vp_ref[0, h], vc_ref[0, h]], axis=0)
            doh = do_ref[0, h].astype(BF16)
            s = _dot_nt(qh, k2) + b_ref[h]
            p = jnp.where(mask, jnp.exp(s - lse_ref[0, h][:, :1]), 0.0)
            dp = _dot_nt(doh, v2)
            ds = p * (dp + c_ref[0, h][:, :1])
            ds_b = ds.astype(BF16)

            @pl.when(valid)
            def _():
                dq_ref[0, h] = _dot_nn(ds_b, k2)

            dk2 = _dot_tn(ds_b, qh)
            dv2 = _dot_tn(p.astype(BF16), doh)
            dk_ref[0, h] = kcar_ref[h] + dk2[:BLOCK]
            dv_ref[0, h] = vcar_ref[h] + dv2[:BLOCK]
            kcar_ref[h] = dk2[BLOCK:]
            vcar_ref[h] = dv2[BLOCK:]
            db_ref[h] += ds

    def qmap(r, n):
        return (r, 0, jnp.minimum(n, nb - 1), 0)

    def pmap(r, n):
        return (r, 0, jnp.maximum(jnp.minimum(n, nb - 1) - 1, 0), 0)

    def kvout(r, n):
        return (r, 0, jnp.maximum(n - 1, 0), 0)

    blk = (1, nh, BLOCK, dh)
    cur, prev = pl.BlockSpec(blk, qmap), pl.BlockSpec(blk, pmap)
    bias_spec = pl.BlockSpec((nh, BLOCK, 2 * BLOCK), lambda r, n: (0, 0, 0))
    full = jax.ShapeDtypeStruct(q.shape, F32)
    return pl.pallas_call(
        body, name=name, grid=(dil, nb + 1),
        in_specs=[cur, cur, prev, cur, prev, cur, cur, cur, bias_spec],
        out_specs=[cur, pl.BlockSpec(blk, kvout), pl.BlockSpec(blk, kvout), bias_spec],
        out_shape=[full, full, full, jax.ShapeDtypeStruct(bias.shape, F32)],
        scratch_shapes=[pltpu.VMEM((nh, BLOCK, dh), F32), pltpu.VMEM((nh, BLOCK, dh), F32)],
        compiler_params=_cparams("arbitrary", "arbitrary"),
    )(q, k, k, v, v, do, lse, cvec, bias)


def _group_weights(lses):
    mx = jnp.maximum(jnp.maximum(lses[0], lses[1]), lses[2])
    es = [jnp.exp(l - mx) for l in lses]
    den = es[0] + es[1] + es[2]
    return [e / den for e in es]


def _combine_fwd(name, os_, lses, tm=512):
    nh, rows, dh = os_[0].shape
    tm = min(tm, rows)

    def body(o0, o1, o2, l0, l1, l2, out_ref):
        ws = _group_weights([l0[...], l1[...], l2[...]])
        out_ref[...] = ws[0] * o0[...] + ws[1] * o1[...] + ws[2] * o2[...]

    spec = pl.BlockSpec((nh, tm, dh), lambda i: (0, i, 0))
    return pl.pallas_call(
        body, name=name, grid=(rows // tm,), in_specs=[spec] * 6, out_specs=spec,
        out_shape=jax.ShapeDtypeStruct((nh, rows, dh), F32),
        compiler_params=_cparams("parallel"),
    )(*os_, *lses)


def _combine_bwd(name, do, oa, lses, tm=512):
    nh, rows, dh = do.shape
    tm = min(tm, rows)

    def body(do_ref, oa_ref, l0, l1, l2, d0, d1, d2, c0, c1, c2):
        ws = _group_weights([l0[...], l1[...], l2[...]])
        dov = do_ref[...]
        bar = jnp.sum(dov * oa_ref[...], axis=-1, keepdims=True)
        for w, d_ref, c_ref in zip(ws, (d0, d1, d2), (c0, c1, c2)):
            d_ref[...] = w * dov
            c_ref[...] = -w * bar

    spec = pl.BlockSpec((nh, tm, dh), lambda i: (0, i, 0))
    shape = jax.ShapeDtypeStruct((nh, rows, dh), F32)
    return pl.pallas_call(
        body, name=name, grid=(rows // tm,), in_specs=[spec] * 5, out_specs=[spec] * 6,
        out_shape=[shape] * 6, compiler_params=_cparams("parallel"),
    )(do, oa, *lses)


def _ssm_disc(a_re, a_im, log_dt, b_re, b_im):
    dt = jnp.exp(log_dt)
    mag = jnp.exp(a_re * dt)
    ab_re = mag * jnp.cos(a_im * dt)
    ab_im = mag * jnp.sin(a_im * dt)
    den = a_re * a_re + a_im * a_im
    xr = ab_re - 1.0
    coef_re = (xr * a_re + ab_im * a_im) / den
    coef_im = (ab_im * a_re - xr * a_im) / den
    bb_re = coef_re[None] * b_re - coef_im[None] * b_im
    bb_im = coef_re[None] * b_im + coef_im[None] * b_re
    return ab_re, ab_im, bb_re, bb_im


def _cpow2(re, im, times):
    for _ in range(times):
        re, im = re * re - im * im, 2.0 * re * im
    return re, im


def _ssm_params_fwd(name, a_re, a_im, log_dt, b_re, b_im):
    gn = jax.ShapeDtypeStruct(a_re.shape, F32)
    cgn = jax.ShapeDtypeStruct(b_re.shape, F32)

    def body(ar, ai, ld, br, bi, o_abr, o_abi, o_apr, o_api, o_bbr, o_bbi):
        ab_re, ab_im, bb_re, bb_im = _ssm_disc(ar[...], ai[...], ld[...], br[...], bi[...])
        o_abr[...] = ab_re
        o_abi[...] = ab_im
        pr, pi = _cpow2(ab_re, ab_im, int(math.log2(SCAN_STEPS)))
        o_apr[...] = pr
        o_api[...] = pi
        o_bbr[...] = bb_re
        o_bbi[...] = bb_im

    vm = pl.BlockSpec(memory_space=pltpu.VMEM)
    return pl.pallas_call(body, name=name, in_specs=[vm] * 5, out_specs=[vm] * 6,
                          out_shape=[gn, gn, gn, gn, cgn, cgn])(a_re, a_im, log_dt, b_re, b_im)


def _ssm_params_bwd(name, a_re, a_im, log_dt, b_re, b_im, d_ab_re, d_ab_im, d_bb_re, d_bb_im):
    gn = jax.ShapeDtypeStruct(a_re.shape, F32)
    cgn = jax.ShapeDtypeStruct(b_re.shape, F32)

    def body(ar, ai, ld, br, bi, g0, g1, g2, g3, o_ar, o_ai, o_ld, o_br, o_bi):
        _, vjp = jax.vjp(_ssm_disc, ar[...], ai[...], ld[...], br[...], bi[...])
        outs = vjp((g0[...], g1[...], g2[...], g3[...]))
        for o_ref, o in zip((o_ar, o_ai, o_ld, o_br, o_bi), outs):
            o_ref[...] = o

    vm = pl.BlockSpec(memory_space=pltpu.VMEM)
    return pl.pallas_call(body, name=name, in_specs=[vm] * 9, out_specs=[vm] * 5,
                          out_shape=[gn, gn, jax.ShapeDtypeStruct(log_dt.shape, F32), cgn, cgn],
                          )(a_re, a_im, log_dt, b_re, b_im, d_ab_re, d_ab_im, d_bb_re, d_bb_im)


def _scan_block(s_ref, carry_ref, tmp_ref, ab_ref, ap_ref, reverse, sprev=None):
    nl = SSM_LANES
    for lc in range(nl // SCAN_LANES):
        re_l = pl.ds(lc * SCAN_LANES, SCAN_LANES)
        im_l = pl.ds(nl + lc * SCAN_LANES, SCAN_LANES)
        are, aim = ab_ref[:, re_l], ab_ref[:, im_l]

        def rows_of(j):
            jj = SCAN_STEPS - 1 - j if reverse else j
            return pl.ds(pl.multiple_of(jj * SCAN_SUB, SCAN_SUB), SCAN_SUB)

        def pass1(j, st):
            sr, si = st
            rows = rows_of(j)
            nr = are * sr - aim * si + s_ref[rows, re_l]
            ni = are * si + aim * sr + s_ref[rows, im_l]
            s_ref[rows, re_l] = nr
            s_ref[rows, im_l] = ni
            return nr, ni

        zero = jnp.zeros((SCAN_SUB, SCAN_LANES), F32)
        er, ei = lax.fori_loop(0, SCAN_STEPS, pass1, (zero, zero), unroll=2)
        tmp_ref[0:SCAN_SUB, re_l] = er
        tmp_ref[0:SCAN_SUB, im_l] = ei
        apr, api = ap_ref[0:1, re_l], ap_ref[0:1, im_l]
        sr, si = carry_ref[0:1, re_l], carry_ref[0:1, im_l]
        for step in range(SCAN_SUB):
            c = SCAN_SUB - 1 - step if reverse else step
            tmp_ref[SCAN_SUB + c:SCAN_SUB + c + 1, re_l] = sr
            tmp_ref[SCAN_SUB + c:SCAN_SUB + c + 1, im_l] = si
            e_r, e_i = tmp_ref[c:c + 1, re_l], tmp_ref[c:c + 1, im_l]
            sr, si = apr * sr - api * si + e_r, apr * si + api * sr + e_i
        carry_ref[0:1, re_l] = sr
        carry_ref[0:1, im_l] = si
        cr = tmp_ref[SCAN_SUB:2 * SCAN_SUB, re_l]
        ci = tmp_ref[SCAN_SUB:2 * SCAN_SUB, im_l]

        if sprev is None:
            def pass2(j, st):
                pr, pi = st
                rows = rows_of(j)
                s_ref[rows, re_l] += pr * cr - pi * ci
                s_ref[rows, im_l] += pr * ci + pi * cr
                return pr * are - pi * aim, pr * aim + pi * are

            lax.fori_loop(0, SCAN_STEPS, pass2, (are, aim), unroll=2)
        else:
            st_ref, prev_ref, have_prev, dab_ref = sprev

            def corrected(j, pr, pi):
                rows = rows_of(j)
                gr = s_ref[rows, re_l] + pr * cr - pi * ci
                gi = s_ref[rows, im_l] + pr * ci + pi * cr
                s_ref[rows, re_l] = gr
                s_ref[rows, im_l] = gi
                return gr, gi

            def pass2(j, st):
                pr, pi, dr, di = st
                gr, gi = corrected(j, pr, pi)
                before = pl.ds(pl.multiple_of((SCAN_STEPS - 2 - j) * SCAN_SUB, SCAN_SUB), SCAN_SUB)
                qr, qi = st_ref[before, re_l], st_ref[before, im_l]
                return (pr * are - pi * aim, pr * aim + pi * are,
                        dr + gr * qr + gi * qi, di + gi * qr - gr * qi)

            pr, pi, dr, di = lax.fori_loop(0, SCAN_STEPS - 1, pass2, (are, aim, zero, zero), unroll=2)
            gr, gi = corrected(SCAN_STEPS - 1, pr, pi)
            last = pl.ds((SCAN_STEPS - 1) * SCAN_SUB, SCAN_SUB)
            sub = lax.broadcasted_iota(jnp.int32, (SCAN_SUB, SCAN_LANES), 0)
            pv_r = jnp.broadcast_to(prev_ref[SCAN_SUB - 1:SCAN_SUB, re_l], (SCAN_SUB, SCAN_LANES)) * have_prev
            pv_i = jnp.broadcast_to(prev_ref[SCAN_SUB - 1:SCAN_SUB, im_l], (SCAN_SUB, SCAN_LANES)) * have_prev
            qr = jnp.where(sub == 0, pv_r, pltpu.roll(st_ref[last, re_l], 1, 0))
            qi = jnp.where(sub == 0, pv_i, pltpu.roll(st_ref[last, im_l], 1, 0))
            dab_ref[:, re_l] += dr + gr * qr + gi * qi
            dab_ref[:, im_l] += di + gi * qr - gr * qi


def _ssm_fwd(name, u_perm, bb_mat, c_mat, ab_rows, ap_rows, d_skip):
    rows = u_perm.shape[0]
    nl2 = 2 * SSM_LANES

    def body(u_ref, bb_ref, c_ref, ab_ref, ap_ref, d_ref, y_ref, s_ref, carry_ref, tmp_ref):
        @pl.when(pl.program_id(0) == 0)
        def _():
            carry_ref[...] = jnp.zeros_like(carry_ref)

        uv = u_ref[...]
        s_ref[...] = _dot_nn(uv.astype(BF16), bb_ref[...])
        _scan_block(s_ref, carry_ref, tmp_ref, ab_ref, ap_ref, reverse=False)
        y_ref[...] = _dot_nn(s_ref[...].astype(BF16), c_ref[...]) + d_ref[...] * uv

    const = lambda shape: pl.BlockSpec(shape, lambda i: (0, 0))
    return pl.pallas_call(
        body, name=name, grid=(rows // SCAN_BLOCK,),
        in_specs=[pl.BlockSpec((SCAN_BLOCK, SSM_WIDTH), lambda i: (i, 0)), const((SSM_WIDTH, nl2)),
                  const((nl2, SSM_WIDTH)), const((SCAN_SUB, nl2)), const((SCAN_SUB, nl2)), const((1, SSM_WIDTH))],
        out_specs=[pl.BlockSpec((SCAN_BLOCK, SSM_WIDTH), lambda i: (i, 0)),
                   pl.BlockSpec((SCAN_BLOCK, nl2), lambda i: (i, 0))],
        out_shape=[jax.ShapeDtypeStruct((rows, SSM_WIDTH), F32), jax.ShapeDtypeStruct((rows, nl2), F32)],
        scratch_shapes=[pltpu.VMEM((SCAN_SUB, nl2), F32), pltpu.VMEM((2 * SCAN_SUB, nl2), F32)],
        compiler_params=_cparams("arbitrary"),
    )(u_perm, bb_mat, c_mat, ab_rows, ap_rows, d_skip)


def _ssm_bwd(name, dy_perm, u_perm, states, c_mat_t, bb_mat_t, abc_rows, apc_rows, d_skip):
    rows = u_perm.shape[0]
    nl2 = 2 * SSM_LANES
    nblk = rows // SCAN_BLOCK

    def body(dy_ref, u_ref, st_ref, prev_ref, ct_ref, bt_ref, ab_ref, ap_ref, d_ref,
             du_ref, g_ref, dab_ref, dd_ref, carry_ref, tmp_ref):
        i = pl.program_id(0)

        @pl.when(i == 0)
        def _():
            carry_ref[...] = jnp.zeros_like(carry_ref)
            dab_ref[...] = jnp.zeros_like(dab_ref)
            dd_ref[...] = jnp.zeros_like(dd_ref)

        dyv = dy_ref[...]
        g_ref[...] = _dot_nn(dyv.astype(BF16), ct_ref[...])
        have_prev = (i < nblk - 1).astype(F32)
        _scan_block(g_ref, carry_ref, tmp_ref, ab_ref, ap_ref, reverse=True,
                    sprev=(st_ref, prev_ref, have_prev, dab_ref))
        du_ref[...] = _dot_nn(g_ref[...].astype(BF16), bt_ref[...]) + d_ref[...] * dyv
        dd_ref[...] += jnp.sum(dyv * u_ref[...], axis=0, keepdims=True)

    const = lambda shape: pl.BlockSpec(shape, lambda i: (0, 0))
    blk = lambda cols: pl.BlockSpec((SCAN_BLOCK, cols), lambda i: (nblk - 1 - i, 0))
    per8 = SCAN_BLOCK // SCAN_SUB
    prev_spec = pl.BlockSpec((SCAN_SUB, nl2), lambda i: (jnp.maximum((nblk - 1 - i) * per8 - 1, 0), 0))
    return pl.pallas_call(
        body, name=name, grid=(nblk,),
        in_specs=[blk(SSM_WIDTH), blk(SSM_WIDTH), blk(nl2), prev_spec, const((SSM_WIDTH, nl2)),
                  const((nl2, SSM_WIDTH)), const((SCAN_SUB, nl2)), const((SCAN_SUB, nl2)), const((1, SSM_WIDTH))],
        out_specs=[blk(SSM_WIDTH), blk(nl2), const((SCAN_SUB, nl2)), const((1, SSM_WIDTH))],
        out_shape=[jax.ShapeDtypeStruct((rows, SSM_WIDTH), F32), jax.ShapeDtypeStruct((rows, nl2), F32),
                   jax.ShapeDtypeStruct((SCAN_SUB, nl2), F32), jax.ShapeDtypeStruct((1, SSM_WIDTH), F32)],
        scratch_shapes=[pltpu.VMEM((SCAN_SUB, nl2), F32), pltpu.VMEM((2 * SCAN_SUB, nl2), F32)],
        compiler_params=_cparams("arbitrary"),
    )(dy_perm, u_perm, states, states, c_mat_t, bb_mat_t, abc_rows, apc_rows, d_skip)


def _scan_order(a):
    rows, cols = a.shape
    return a.reshape(rows // SCAN_BLOCK, SCAN_SUB, SCAN_STEPS, cols).transpose(0, 2, 1, 3).reshape(rows, cols)


def _time_order(a):
    rows, cols = a.shape
    return a.reshape(rows // SCAN_BLOCK, SCAN_STEPS, SCAN_SUB, cols).transpose(0, 2, 1, 3).reshape(rows, cols)


def _adamw(name, w, m, v, gparts, tr):
    rows, cols = w.shape

    def body(w_ref, m_ref, v_ref, g_ref, og_ref, od_ref, om_ref, ov_ref):
        g = g_ref[0].astype(F32)
        for i in range(1, N_DEV):
            g = g + g_ref[i].astype(F32)
        m_new = B1 * m_ref[...] + (1.0 - B1) * g
        v_new = B2 * v_ref[...] + (1.0 - B2) * (g * g)
        m_hat = m_new / (1.0 - B1 ** STEP)
        v_hat = v_new / (1.0 - B2 ** STEP)
        og_ref[...] = g
        od_ref[...] = -LR * (m_hat / (jnp.sqrt(v_hat) + ADAM_EPS) + WD * w_ref[...])
        om_ref[...] = m_new
        ov_ref[...] = v_new

    spec = pl.BlockSpec((tr, cols), lambda i: (i, 0))
    shape = jax.ShapeDtypeStruct((rows, cols), F32)
    return pl.pallas_call(
        body, name=name, grid=(rows // tr,),
        in_specs=[spec, spec, spec, pl.BlockSpec((N_DEV, tr, cols), lambda i: (0, i, 0))],
        out_specs=[spec] * 4, out_shape=[shape] * 4,
        compiler_params=_cparams("parallel"),
    )(w, m, v, gparts)


_SHARDED = (
    ("ffn1_w_gate", True, (352, 1024)), ("ffn1_w_up", True, (352, 1024)), ("ffn1_w_down", False, (352, 1024)),
    ("w_in", True, (608, 1024)), ("ssm_w_glu", True, (128, 512)), ("w_attn_branch", True, (128, 256)),
    ("w_ssm_branch", True, (128, 512)), ("w_out", False, (128, 1024)),
    ("ffn2_w_gate", True, (352, 1024)), ("ffn2_w_up", True, (352, 1024)), ("ffn2_w_down", False, (352, 1024)),
)
_SMALL = ("ffn1_norm", "mix_norm", "gate_bias", "rel_bias_table", "ssm_a_re", "ssm_a_im", "ssm_log_dt",
          "ssm_b_re", "ssm_b_im", "ssm_c_re", "ssm_c_im", "ssm_d", "ffn2_norm", "final_norm")
_ORDER = ("ffn1_norm", "ffn1_w_gate", "ffn1_w_up", "ffn1_w_down", "mix_norm", "w_in", "gate_bias",
          "rel_bias_table", "ssm_a_re", "ssm_a_im", "ssm_log_dt", "ssm_b_re", "ssm_b_im", "ssm_c_re",
          "ssm_c_im", "ssm_d", "ssm_w_glu", "w_attn_branch", "w_ssm_branch", "w_out", "ffn2_norm",
          "ffn2_w_gate", "ffn2_w_up", "ffn2_w_down", "final_norm")


def _pack_rows(shape):
    return shape[0] * shape[1] // D_MODEL


_SHARD_INFO = {nm: (tr, shape) for nm, tr, shape in _SHARDED}
_PHASES = {
    "f1gu": ("ffn1_w_gate", "ffn1_w_up"), "f1d": ("ffn1_w_down",),
    "mix": ("w_in", "ssm_w_glu", "w_attn_branch", "w_ssm_branch", "w_out"),
    "f2": ("ffn2_w_gate", "ffn2_w_up", "ffn2_w_down"),
}


def _pack_sharded(ws, names):
    parts = []
    for nm in names:
        tr, shape = _SHARD_INFO[nm]
        a = ws[nm].T if tr else ws[nm]
        parts.append(a.reshape(_pack_rows(shape), D_MODEL))
    return jnp.concatenate(parts, axis=0)


def _unpack_sharded(pack, names):
    out, r0 = {}, 0
    for nm in names:
        tr, shape = _SHARD_INFO[nm]
        n = _pack_rows(shape)
        a = pack[r0:r0 + n].reshape(shape)
        out[nm] = a.T if tr else a
        r0 += n
    return out


def _unpack_gathered(gath, names):
    out, r0 = {}, 0
    for nm in names:
        _, shape = _SHARD_INFO[nm]
        n = _pack_rows(shape)
        out[nm] = gath[:, r0:r0 + n].reshape(N_DEV * shape[0], shape[1])
        r0 += n
    return out


def _pack_grads(gs, names):
    parts = []
    for nm in names:
        _, shape = _SHARD_INFO[nm]
        parts.append(gs[nm].astype(BF16).reshape(N_DEV, _pack_rows(shape), D_MODEL))
    return jnp.concatenate(parts, axis=1)


def _pack_small(ws):
    flat = jnp.concatenate([ws[nm].reshape(-1) for nm in _SMALL])
    pad = (-flat.shape[0]) % (8 * 128)
    return jnp.pad(flat, (0, pad)).reshape(-1, 128)


def _unpack_small(pack, like):
    flat, out, p0 = pack.reshape(-1), {}, 0
    for nm in _SMALL:
        n = like[nm].size
        out[nm] = flat[p0:p0 + n].reshape(like[nm].shape)
        p0 += n
    return out


def _to_dilated(a, dil):
    rows = a.shape[0]
    return a.reshape(rows // dil, dil, HEADS_PER_GROUP, HEAD_DIM).transpose(1, 2, 0, 3)


def _dilated_to_heads(a):
    dil, nh, m, dh = a.shape
    return a.transpose(1, 2, 0, 3).reshape(nh, m * dil, dh)


def _heads_to_dilated(a, dil):
    nh, rows, dh = a.shape
    return a.reshape(nh, rows // dil, dil, dh).transpose(2, 0, 1, 3)


def _block_diag(blocks_gab):
    g, a, b = blocks_gab.shape
    eye = jnp.eye(g, dtype=blocks_gab.dtype)
    return (blocks_gab[:, :, None, :] * eye[:, None, :, None]).reshape(g * a, g * b)


def _diag_blocks(mat, a, b):
    g = mat.shape[0] // a
    eye = jnp.eye(g, dtype=mat.dtype)
    return jnp.einsum("gahb,gh->gab", mat.reshape(g, a, g, b), eye)


def _local_step(xs, target, small, weights_of, send_grads, first_deps=()):
    rows = xs.shape[0]
    gfull, gsmall = {}, {}
    wf = dict(weights_of("f1", None))

    x1, h1, gg1, uu1 = _ffn_fwd("ffn1_fwd", xs, small["ffn1_norm"], wf["ffn1_w_gate"], wf["ffn1_w_up"],
                                wf["ffn1_w_down"], deps=first_deps)
    wf.update(weights_of("mix", x1))
    hmix = _rms_fwd("mix_norm_fwd", x1, small["mix_norm"])
    w_in = wf["w_in"]
    w_qkv, w_u, w_g = w_in[:3 * ATTN_WIDTH], w_in[3 * ATTN_WIDTH:3 * ATTN_WIDTH + SSM_WIDTH], w_in[3 * ATTN_WIDTH + SSM_WIDTH:]
    qscale = jnp.concatenate([jnp.full((1, ATTN_WIDTH), HEAD_DIM ** -0.5, F32), jnp.ones((1, 2 * ATTN_WIDTH), F32)], axis=1)
    qkv, = _mm("in_qkv", [(hmix, w_qkv)], True, 3 * ATTN_WIDTH, [BF16],
               epilogue=lambda acc, sc: (acc * sc,), extras=[(qscale, 0)], tn=ATTN_WIDTH)
    u, = _mm("in_u", [(hmix, w_u)], True, SSM_WIDTH, [F32])
    gates, = _mm("in_gates", [(hmix, w_g)], True, 2 * D_MODEL, [F32],
                 epilogue=lambda acc, b: (_sigmoid(acc + b),), extras=[(small["gate_bias"], 0)])

    buckets = jnp.asarray(_bucket_table())
    bias = _bias_fwd("rel_bias_fwd", buckets, small["rel_bias_table"])
    qkv_d, o_h, lse_h = [], [], []
    for g, dil in enumerate(DILATIONS):
        cols = [qkv[:, s * ATTN_WIDTH + g * ATTN_OUT:s * ATTN_WIDTH + (g + 1) * ATTN_OUT] for s in range(3)]
        qd, kd, vd = [_to_dilated(c, dil) for c in cols]
        bias_g = bias[g * HEADS_PER_GROUP:(g + 1) * HEADS_PER_GROUP]
        o_g, lse_g = _attn_fwd(f"attn_fwd_{g}", qd, kd, vd, bias_g)
        qkv_d.append((qd, kd, vd, bias_g))
        o_h.append(_dilated_to_heads(o_g))
        lse_h.append(_dilated_to_heads(lse_g))
    oa_h = _combine_fwd("attn_combine_fwd", o_h, lse_h)
    oa = oa_h.transpose(1, 0, 2).reshape(rows, ATTN_OUT).astype(BF16)
    y_attn, = _mm("attn_branch", [(oa, wf["w_attn_branch"])], True, D_MODEL, [F32])

    ab_re, ab_im, ap_re, ap_im, bb_re, bb_im = _ssm_params_fwd(
        "ssm_params_fwd", small["ssm_a_re"], small["ssm_a_im"], small["ssm_log_dt"].reshape(SSM_GROUPS, 1),
        small["ssm_b_re"].transpose(2, 0, 1), small["ssm_b_im"].transpose(2, 0, 1))

    def lanes(re, im, sign=1.0):
        row = jnp.concatenate([re.reshape(1, SSM_LANES), sign * im.reshape(1, SSM_LANES)], axis=1)
        return jnp.broadcast_to(row, (SCAN_SUB, 2 * SSM_LANES))

    bb_mat = jnp.concatenate([_block_diag(bb_re.transpose(1, 0, 2)), _block_diag(bb_im.transpose(1, 0, 2))], axis=1)
    c_mat_t = jnp.concatenate([_block_diag(small["ssm_c_re"]), -_block_diag(small["ssm_c_im"])], axis=1)
    bb_mat, c_mat_t = bb_mat.astype(BF16), c_mat_t.astype(BF16)
    d_skip = small["ssm_d"].reshape(1, SSM_WIDTH)
    u_perm = _scan_order(u)
    y_perm, states = _ssm_fwd("ssm_fwd", u_perm, bb_mat, c_mat_t.T, lanes(ab_re, ab_im), lanes(ap_re, ap_im), d_skip)
    y_raw = _time_order(y_perm)

    def gelu_fn(yv):
        return (jax.nn.gelu(yv),)

    ygelu, = _ew("ssm_gelu", gelu_fn, [y_raw], [SSM_WIDTH], [BF16])
    glu, = _mm("ssm_glu", [(ygelu, wf["ssm_w_glu"])], True, 2 * SSM_WIDTH, [F32])
    ysg, = _ew("ssm_glu_act", lambda gv: (gv[:, :SSM_WIDTH] * _sigmoid(gv[:, SSM_WIDTH:]),), [glu], [SSM_WIDTH], [BF16])
    y_ssm, merged = _mm("ssm_branch_merge", [(ysg, wf["w_ssm_branch"])], True, D_MODEL, [F32, BF16],
                        epilogue=lambda acc, ga, gs, ya: (acc, ga * ya + gs * acc),
                        extras=[(gates, 0), (gates, D_MODEL), (y_attn, 0)])
    x2, = _mm("mix_out", [(merged, wf["w_out"])], False, D_MODEL, [F32],
              epilogue=lambda acc, res: (res + acc,), extras=[(x1, 0)])
    wf.update(weights_of("f2", x2))
    x3, h2, gg2, uu2 = _ffn_fwd("ffn2_fwd", x2, small["ffn2_norm"], wf["ffn2_w_gate"], wf["ffn2_w_up"],
                                wf["ffn2_w_down"])
    dx3, gsmall["final_norm"], loss = _final_loss("final_loss", x3, small["final_norm"].reshape(1, D_MODEL), target)

    dx2, dgg2, duu2, act2, gsmall["ffn2_norm"] = _ffn_bwd(
        "ffn2_bwd", dx3, x2, small["ffn2_norm"], gg2, uu2, wf["ffn2_w_gate"], wf["ffn2_w_up"], wf["ffn2_w_down"])
    gfull["ffn2_w_gate"] = _mm_tn("ffn2_dwg", dgg2, h2)
    gfull["ffn2_w_up"] = _mm_tn("ffn2_dwu", duu2, h2)
    gfull["ffn2_w_down"] = _mm_tn("ffn2_dwd", act2, dx3, scale=0.5)
    sent = send_grads("f2", gfull)

    def merge_bwd(dm, ga, gs, ya, ys):
        return (dm * ga, dm * gs, dm * ya * ga * (1.0 - ga), dm * ys * gs * (1.0 - gs))

    dya, dys, dzga, dzgs = _mm("mix_out_bwd", [(dx2, wf["w_out"])], True, D_MODEL, [BF16] * 4, epilogue=merge_bwd,
                               extras=[(gates, 0), (gates, D_MODEL), (y_attn, 0), (y_ssm, 0)], deps=sent)
    gfull["w_out"] = _mm_tn("dw_out", merged, dx2)
    gsmall["gate_bias"] = jnp.concatenate([_colsum("dgate_bias_a", dzga), _colsum("dgate_bias_s", dzgs)], axis=1)

    gfull["w_ssm_branch"] = _mm_tn("dw_ssm_branch", dys, ysg)

    def glu_bwd(dysg, av, bv):
        sb = _sigmoid(bv)
        return (dysg * sb, dysg * av * sb * (1.0 - sb))

    dglu_a, dglu_b = _mm("ssm_branch_bwd", [(dys, wf["w_ssm_branch"])], False, SSM_WIDTH, [BF16, BF16],
                         epilogue=glu_bwd, extras=[(glu, 0), (glu, SSM_WIDTH)])
    w_glu = wf["ssm_w_glu"]
    gfull["ssm_w_glu"] = jnp.concatenate([_mm_tn("dw_glu_a", dglu_a, ygelu), _mm_tn("dw_glu_b", dglu_b, ygelu)], axis=0)

    def gelu_bwd(acc, yv):
        _, vjp = jax.vjp(jax.nn.gelu, yv)
        return (vjp(acc)[0],)

    dy_raw, = _mm("ssm_glu_bwd", [(dglu_a, w_glu[:SSM_WIDTH]), (dglu_b, w_glu[SSM_WIDTH:])], False, SSM_WIDTH, [F32],
                  epilogue=gelu_bwd, extras=[(y_raw, 0)])
    dy_perm = _scan_order(dy_raw)
    du_perm, g_states, dab_rows, gsmall_d = _ssm_bwd(
        "ssm_bwd", dy_perm, u_perm, states, c_mat_t, bb_mat.T, lanes(ab_re, ab_im, -1.0), lanes(ap_re, ap_im, -1.0), d_skip)
    du = _time_order(du_perm)
    gsmall["ssm_d"] = gsmall_d
    dbb_acc = _mm_tn("ssm_dbb", u_perm, g_states, bm=SSM_WIDTH)
    dc_acc = _mm_tn("ssm_dc", dy_perm, states, bm=SSM_WIDTH)
    dbb_re = _diag_blocks(dbb_acc[:, :SSM_LANES], SSM_GROUP, SSM_STATE).transpose(1, 0, 2)
    dbb_im = _diag_blocks(dbb_acc[:, SSM_LANES:], SSM_GROUP, SSM_STATE).transpose(1, 0, 2)
    gsmall["ssm_c_re"] = _diag_blocks(dc_acc[:, :SSM_LANES], SSM_GROUP, SSM_STATE)
    gsmall["ssm_c_im"] = -_diag_blocks(dc_acc[:, SSM_LANES:], SSM_GROUP, SSM_STATE)
    dab = _colsum("ssm_dab", dab_rows)
    d_ar, d_ai, d_ld, d_br, d_bi = _ssm_params_bwd(
        "ssm_params_bwd", small["ssm_a_re"], small["ssm_a_im"], small["ssm_log_dt"].reshape(SSM_GROUPS, 1),
        small["ssm_b_re"].transpose(2, 0, 1), small["ssm_b_im"].transpose(2, 0, 1),
        dab[:, :SSM_LANES].reshape(SSM_GROUPS, SSM_STATE), dab[:, SSM_LANES:].reshape(SSM_GROUPS, SSM_STATE),
        dbb_re, dbb_im)
    gsmall["ssm_a_re"], gsmall["ssm_a_im"], gsmall["ssm_log_dt"] = d_ar, d_ai, d_ld.reshape(SSM_GROUPS)
    gsmall["ssm_b_re"], gsmall["ssm_b_im"] = d_br.transpose(1, 2, 0), d_bi.transpose(1, 2, 0)

    gfull["w_attn_branch"] = _mm_tn("dw_attn_branch", dya, oa)
    doa, = _mm("attn_branch_bwd", [(dya, wf["w_attn_branch"])], False, ATTN_OUT, [F32])
    do_h = doa.reshape(rows, HEADS_PER_GROUP, HEAD_DIM).transpose(1, 0, 2)
    dc = _combine_bwd("attn_combine_bwd", do_h, oa_h, lse_h)
    dqkv_cols = [None] * 9
    dbias = []
    for g, dil in enumerate(DILATIONS):
        qd, kd, vd, bias_g = qkv_d[g]
        dq, dk, dv, db = _attn_bwd(f"attn_bwd_{g}", qd, kd, vd, _heads_to_dilated(dc[g], dil),
                                   _heads_to_dilated(lse_h[g], dil), _heads_to_dilated(dc[3 + g], dil), bias_g)
        dbias.append(db)
        for s, (arr, sc) in enumerate(((dq, HEAD_DIM ** -0.5), (dk, 1.0), (dv, 1.0))):
            tok = _dilated_to_heads(arr).transpose(1, 0, 2).reshape(rows, ATTN_OUT)
            dqkv_cols[3 * s + g] = (tok * sc).astype(BF16)
    dqkv = jnp.concatenate(dqkv_cols, axis=1)
    gsmall["rel_bias_table"] = _bias_bwd("rel_bias_bwd", buckets, jnp.concatenate(dbias, axis=0))[:, :N_GROUPS * HEADS_PER_GROUP]

    gfull["w_in"] = jnp.concatenate([
        _mm_tn("dw_in_qkv", dqkv, hmix), _mm_tn("dw_in_u", du, hmix),
        _mm_tn("dw_in_ga", dzga, hmix), _mm_tn("dw_in_gs", dzgs, hmix)], axis=0)
    sent = send_grads("mix", gfull)
    dhmix, = _mm("in_bwd", [(dqkv, w_qkv), (du, w_u), (dzga, w_g[:D_MODEL]), (dzgs, w_g[D_MODEL:])], False, D_MODEL,
                 [F32], tm=512, deps=sent)
    dx1, gsmall["mix_norm"] = _rms_bwd("mix_norm_bwd", dhmix, x1, small["mix_norm"], dx2)

    dx, dgg1, duu1, act1, gsmall["ffn1_norm"] = _ffn_bwd(
        "ffn1_bwd", dx1, xs, small["ffn1_norm"], gg1, uu1, wf["ffn1_w_gate"], wf["ffn1_w_up"], wf["ffn1_w_down"])
    sent = send_grads("small", gsmall)
    gfull["ffn1_w_down"] = _mm_tn("ffn1_dwd", act1, dx1, scale=0.5, deps=sent)
    sent = send_grads("f1d", gfull)
    gfull["ffn1_w_gate"] = _mm_tn("ffn1_dwg", dgg1, h1, deps=sent)
    gfull["ffn1_w_up"] = _mm_tn("ffn1_dwu", duu1, h1)
    send_grads("f1gu", gfull)
    return loss[0, 0], dx, gsmall


def kernel(x, ffn1_norm, ffn1_w_gate, ffn1_w_up, ffn1_w_down, mix_norm, w_in, gate_bias, rel_bias_table, ssm_a_re, ssm_a_im, ssm_log_dt, ssm_b_re, ssm_b_im, ssm_c_re, ssm_c_im, ssm_d, ssm_w_glu, w_attn_branch, w_ssm_branch, w_out, ffn2_norm, ffn2_w_gate, ffn2_w_up, ffn2_w_down, final_norm, loss_target, m_ffn1_norm, m_ffn1_w_gate, m_ffn1_w_up, m_ffn1_w_down, m_mix_norm, m_w_in, m_gate_bias, m_rel_bias_table, m_ssm_a_re, m_ssm_a_im, m_ssm_log_dt, m_ssm_b_re, m_ssm_b_im, m_ssm_c_re, m_ssm_c_im, m_ssm_d, m_ssm_w_glu, m_w_attn_branch, m_w_ssm_branch, m_w_out, m_ffn2_norm, m_ffn2_w_gate, m_ffn2_w_up, m_ffn2_w_down, m_final_norm, v_ffn1_norm, v_ffn1_w_gate, v_ffn1_w_up, v_ffn1_w_down, v_mix_norm, v_w_in, v_gate_bias, v_rel_bias_table, v_ssm_a_re, v_ssm_a_im, v_ssm_log_dt, v_ssm_b_re, v_ssm_b_im, v_ssm_c_re, v_ssm_c_im, v_ssm_d, v_ssm_w_glu, v_w_attn_branch, v_w_ssm_branch, v_w_out, v_ffn2_norm, v_ffn2_w_gate, v_ffn2_w_up, v_ffn2_w_down, v_final_norm):
    given = dict(locals())
    shapes = {nm: given[nm].shape for nm in _ORDER}

    def strip(a):
        return a[0] if a.ndim >= 2 and a.shape[0] == 1 else a

    w = {nm: strip(given[nm]) for nm in _ORDER}
    m = {nm: strip(given["m_" + nm]) for nm in _ORDER}
    v = {nm: strip(given["v_" + nm]) for nm in _ORDER}
    for d in (w, m, v):
        d["rel_bias_table"] = d["rel_bias_table"].reshape(N_BUCKETS, N_GROUPS * HEADS_PER_GROUP)

    small = {nm: w[nm] for nm in _SMALL}
    small_in = dict(small)
    for nm in ("ffn1_norm", "mix_norm", "ffn2_norm", "gate_bias"):
        small_in[nm] = small[nm].reshape(1, -1)
    w_pack = {ph: _pack_sharded(w, names) for ph, names in _PHASES.items()}

    f1_names = _PHASES["f1gu"] + _PHASES["f1d"]
    got_f1 = _all_gather("gather_f1", jnp.concatenate([w_pack["f1gu"], w_pack["f1d"]], axis=0).astype(BF16))
    pending_w = {"mix": _exchange_start("gather_mix_start", w_pack["mix"].astype(BF16), gather=True, deps=[got_f1])}
    pending_w["f2"] = _exchange_start("gather_f2_start", w_pack["f2"].astype(BF16), gather=True,
                                      deps=[pending_w["mix"][4]])

    def weights_of(phase, after):
        if phase == "f1":
            return _unpack_gathered(got_f1, f1_names)
        return _unpack_gathered(_exchange_wait(f"gather_{phase}_wait", pending_w[phase], after, gather=True),
                                _PHASES[phase])

    pending_g = {}

    def send_grads(phase, grads):
        if phase == "small":
            gs_pack = _pack_small({nm: grads[nm].reshape(small[nm].shape) for nm in _SMALL})
            pending_g[phase] = _exchange_start("gather_small_start", gs_pack, gather=True)
        else:
            pending_g[phase] = _exchange_start(f"scatter_{phase}_start", _pack_grads(grads, _PHASES[phase]),
                                               gather=False)
        return [pending_g[phase][4]]

    loss, dx, gsmall = _local_step(x[0], loss_target[0], small_in, weights_of, send_grads,
                                   first_deps=[pending_w["f2"][4]])

    packs = {}
    after = pending_g["f1gu"][4]
    for phase in ("f2", "mix", "f1d", "small", "f1gu"):
        if phase == "small":
            gs_all = _exchange_wait("gather_small_wait", pending_g[phase], after, gather=True)
            sm = _adamw("adamw_small", _pack_small(small), _pack_small({nm: m[nm] for nm in _SMALL}),
                        _pack_small({nm: v[nm] for nm in _SMALL}), gs_all, gs_all.shape[1])
            after = sm[0]
            continue
        names = _PHASES[phase]
        recv = _exchange_wait(f"scatter_{phase}_wait", pending_g[phase], after, gather=False)
        rows_p = w_pack[phase].shape[0]
        tr = max(t for t in range(16, 129, 16) if rows_p % t == 0)
        packs[phase] = _adamw(f"adamw_{phase}", w_pack[phase], _pack_sharded(m, names), _pack_sharded(v, names),
                              recv, tr)
        after = packs[phase][0]

    loss = lax.psum(loss, ("x", "y", "c"))
    outs = []
    for i in range(4):
        big = {}
        for phase, names in _PHASES.items():
            big.update(_unpack_sharded(packs[phase][i], names))
        sml = _unpack_small(sm[i], small)
        outs.append([(big[nm] if nm in big else sml[nm]).reshape(shapes[nm]) for nm in _ORDER])
    return (loss, dx[None], *outs[0], *outs[1], *outs[2], *outs[3])
```

```python
import math

import numpy as np
import jax
import jax.numpy as jnp
from jax import lax
from jax.experimental import pallas as pl
from jax.experimental.pallas import tpu as pltpu

F32 = jnp.float32
BF16 = jnp.bfloat16

N_DEV = 8
D_MODEL = 1024
D_FF = 2816
HEAD_DIM = 64
HEADS_PER_GROUP = 4
DILATIONS = (1, 4, 16)
N_GROUPS = 3
ATTN_WIDTH = 768
ATTN_OUT = 256
BLOCK = 128
N_BUCKETS = 32
MAX_DISTANCE = 2048
NEG_INF = -1e30
SSM_WIDTH = 512
SSM_GROUPS = 32
SSM_GROUP = 16
SSM_STATE = 64
SSM_LANES = SSM_GROUPS * SSM_STATE
EPS = 1e-6
LR, B1, B2, ADAM_EPS, WD, STEP = 0.001, 0.9, 0.999, 1e-08, 0.01, 10

VMEM_LIMIT_BYTES = 56 * 1024 * 1024
SCAN_BLOCK = 256
SCAN_STEPS = 16
SCAN_COLS = SCAN_BLOCK // SCAN_STEPS
SCAN_SUB = 8
SCAN_LANES = 512

MESH = pl.DeviceIdType.MESH


def _cparams(*sem):
    return pltpu.CompilerParams(dimension_semantics=sem, vmem_limit_bytes=VMEM_LIMIT_BYTES)


def _dot(a, b, dims):
    return lax.dot_general(a, b, (dims, ((), ())), preferred_element_type=F32)


def _dot_nn(a, b):
    return _dot(a, b, ((1,), (0,)))


def _dot_nt(a, b):
    return _dot(a, b, ((1,), (1,)))


def _dot_tn(a, b):
    return _dot(a, b, ((0,), (0,)))


def _sigmoid(x):
    return 1.0 / (1.0 + jnp.exp(-x))


def _all_gather(name, xs):
    rows, cols = xs.shape

    def body(x_ref, out_ref, send_sems, recv_sems, local_sem):
        x, y, c = lax.axis_index("x"), lax.axis_index("y"), lax.axis_index("c")
        me, sibling = (x, y, c), (x, y, 1 - c)
        chips = [(1 - x, y), (x, 1 - y), (1 - x, 1 - y)]

        def slot(px, py, pc):
            return out_ref.at[4 * px + 2 * py + pc]

        def copy(k, block, to, src=None):
            return pltpu.make_async_remote_copy(
                src_ref=slot(*block) if src is None else src, dst_ref=slot(*block),
                send_sem=send_sems.at[k], recv_sem=recv_sems.at[k], device_id=to, device_id_type=MESH)

        mine = pltpu.make_async_copy(x_ref, slot(*me), local_sem)
        mine.start()
        first = [copy(0, me, sibling, src=x_ref)]
        first += [copy(1 + j, me, (*chip, c), src=x_ref) for j, chip in enumerate(chips)]
        for cp in first:
            cp.start()
        passed = [copy(4 + j, (*chip, c), sibling) for j, chip in enumerate(chips)]
        for j, chip in enumerate(chips):
            copy(1 + j, (*chip, c), me).wait_recv()
            passed[j].start()
        copy(0, sibling, me).wait_recv()
        for j, chip in enumerate(chips):
            copy(4 + j, (*chip, 1 - c), me).wait_recv()
        for cp in first + passed:
            cp.wait_send()
        mine.wait()

    return pl.pallas_call(
        body, name=name,
        out_shape=jax.ShapeDtypeStruct((N_DEV, rows, cols), xs.dtype),
        in_specs=[pl.BlockSpec(memory_space=pl.ANY)],
        out_specs=pl.BlockSpec(memory_space=pl.ANY),
        scratch_shapes=[pltpu.SemaphoreType.DMA((7,)), pltpu.SemaphoreType.DMA((7,)), pltpu.SemaphoreType.DMA],
    )(xs)


_HBM_SPEC = pl.BlockSpec(memory_space=pltpu.HBM)
_SEM_SPEC = pl.BlockSpec(memory_space=pltpu.SEMAPHORE)
_ANY_SPEC = pl.BlockSpec(memory_space=pl.ANY)
_EFFECT = pltpu.SideEffectType.DATAFLOW_SIDE_EFFECTING


def _peers(x, y, c):
    return [(1 - x if k & 4 else x, 1 - y if k & 2 else y, 1 - c if k & 1 else c) for k in range(1, N_DEV)]


def _exchange_copies(x_ref, land_ref, send_sems, recv_sems, gather):
    x, y, c = lax.axis_index("x"), lax.axis_index("y"), lax.axis_index("c")
    me = 4 * x + 2 * y + c
    copies = []
    for k, (px, py, pc) in enumerate(_peers(x, y, c)):
        src = x_ref if gather else x_ref.at[4 * px + 2 * py + pc]
        copies.append(pltpu.make_async_remote_copy(
            src_ref=src, dst_ref=land_ref.at[me], send_sem=send_sems.at[k], recv_sem=recv_sems.at[k],
            device_id=(px, py, pc), device_id_type=MESH))
    own = pltpu.make_async_copy(x_ref if gather else x_ref.at[me], land_ref.at[me], send_sems.at[N_DEV - 1])
    return own, copies


def _exchange_start(name, xs, gather, deps=()):
    land_shape = (N_DEV, *xs.shape) if gather else xs.shape
    nd = len(deps)

    def body(x_ref, land_ref, *rest):
        send_sems, recv_sems, _, _, token = rest[nd:]
        own, copies = _exchange_copies(x_ref, land_ref, send_sems, recv_sems, gather)
        for cp in copies:
            cp.start()
        own.start()
        token[...] = jnp.zeros_like(token)

    return pl.pallas_call(
        body, name=name,
        out_shape=(pltpu.SemaphoreType.DMA((N_DEV,)), pltpu.SemaphoreType.DMA((N_DEV - 1,)),
                   pltpu.HBM(xs.shape, xs.dtype), pltpu.HBM(land_shape, xs.dtype), jax.ShapeDtypeStruct((8, 128), F32)),
        in_specs=(_HBM_SPEC, _HBM_SPEC) + (_ANY_SPEC,) * nd,
        out_specs=(_SEM_SPEC, _SEM_SPEC, _HBM_SPEC, _HBM_SPEC, pl.BlockSpec(memory_space=pltpu.VMEM)),
        input_output_aliases={0: 2, 1: 3},
        compiler_params=pltpu.CompilerParams(has_side_effects=_EFFECT),
    )(pltpu.with_memory_space_constraint(xs, pltpu.HBM),
      pltpu.with_memory_space_constraint(lax.empty(land_shape, xs.dtype), pltpu.HBM), *deps)


def _exchange_wait(name, handle, after, gather):
    send_sems, recv_sems, x_thru, land_thru, _ = handle

    def body(x_ref, land_ref, send_sems, recv_sems, after_ref, x_dead, got_ref):
        own, copies = _exchange_copies(x_ref, land_ref, send_sems, recv_sems, gather)
        for cp in copies:
            cp.wait_send()
            cp.wait_recv()
        own.wait()

    return pl.pallas_call(
        body, name=name,
        out_shape=(pltpu.HBM(x_thru.shape, x_thru.dtype), pltpu.HBM(land_thru.shape, land_thru.dtype)),
        in_specs=(_HBM_SPEC, _HBM_SPEC, _SEM_SPEC, _SEM_SPEC, _ANY_SPEC),
        out_specs=(_HBM_SPEC, _HBM_SPEC), input_output_aliases={0: 0, 1: 1},
        compiler_params=pltpu.CompilerParams(has_side_effects=_EFFECT),
    )(x_thru, land_thru, send_sems, recv_sems, after)[1]


def _mm(name, pairs, nt, n_cols, out_dtypes, epilogue=None, extras=(), tm=1024, tn=512, deps=()):
    rows = pairs[0][0].shape[0]
    tm = min(tm, rows)
    tn = min(tn, n_cols)
    na, ne, nd = len(pairs), len(extras), len(deps)

    def body(*refs):
        a_refs, w_refs = refs[:na], refs[na:2 * na]
        e_refs, o_refs = refs[2 * na:2 * na + ne], refs[2 * na + ne + nd:]
        acc = None
        for a_ref, w_ref in zip(a_refs, w_refs):
            a = a_ref[...].astype(BF16)
            w = w_ref[...].astype(BF16)
            p = _dot_nt(a, w) if nt else _dot_nn(a, w)
            acc = p if acc is None else acc + p
        outs = (acc,) if epilogue is None else epilogue(acc, *[e[...] for e in e_refs])
        for o_ref, o in zip(o_refs, outs):
            o_ref[...] = o.astype(o_ref.dtype)

    in_specs = [pl.BlockSpec((tm, a.shape[1]), lambda i, j: (i, 0)) for a, _ in pairs]
    for _, w in pairs:
        if nt:
            in_specs.append(pl.BlockSpec((tn, w.shape[1]), lambda i, j: (j, 0)))
        else:
            in_specs.append(pl.BlockSpec((w.shape[0], tn), lambda i, j: (0, j)))
    for e, col_off in extras:
        off = col_off // tn
        if e.shape[0] == 1:
            in_specs.append(pl.BlockSpec((1, tn), lambda i, j, off=off: (0, j + off)))
        else:
            in_specs.append(pl.BlockSpec((tm, tn), lambda i, j, off=off: (i, j + off)))
    in_specs += [_ANY_SPEC] * nd
    out_specs = [pl.BlockSpec((tm, tn), lambda i, j: (i, j)) for _ in out_dtypes]
    outs = pl.pallas_call(
        body, name=name, grid=(rows // tm, n_cols // tn),
        in_specs=in_specs, out_specs=out_specs,
        out_shape=[jax.ShapeDtypeStruct((rows, n_cols), dt) for dt in out_dtypes],
        compiler_params=_cparams("parallel", "arbitrary"),
    )(*[a for a, _ in pairs], *[w for _, w in pairs], *[e for e, _ in extras], *deps)
    return outs


def _tn_rows(m):
    return max(b for b in range(128, min(m, 1408) + 1, 128) if m % b == 0)


def _mm_tn(name, a, b, scale=1.0, bm=None, tk=1024, deps=()):
    rows, m = a.shape
    n = b.shape[1]
    bm = _tn_rows(m) if bm is None else bm
    tk = min(tk, rows)
    nk = rows // tk

    def body(a_ref, b_ref, *rest):
        o_ref = rest[-1]
        k = pl.program_id(1)

        @pl.when(k == 0)
        def _():
            o_ref[...] = jnp.zeros_like(o_ref)

        o_ref[...] += _dot_tn(a_ref[...].astype(BF16), b_ref[...].astype(BF16))
        if scale != 1.0:
            @pl.when(k == nk - 1)
            def _():
                o_ref[...] = o_ref[...] * scale

    return pl.pallas_call(
        body, name=name, grid=(m // bm, nk),
        in_specs=[pl.BlockSpec((tk, bm), lambda i, k: (k, i)), pl.BlockSpec((tk, n), lambda i, k: (k, 0))]
        + [_ANY_SPEC] * len(deps),
        out_specs=pl.BlockSpec((bm, n), lambda i, k: (i, 0)),
        out_shape=jax.ShapeDtypeStruct((m, n), F32),
        compiler_params=_cparams("parallel", "arbitrary"),
    )(a, b, *deps)


def _colsum(name, xs, tm=512):
    rows, cols = xs.shape
    tm = min(tm, rows)

    def body(x_ref, o_ref):
        @pl.when(pl.program_id(0) == 0)
        def _():
            o_ref[...] = jnp.zeros_like(o_ref)

        o_ref[...] += jnp.sum(x_ref[...].astype(F32), axis=0, keepdims=True)

    return pl.pallas_call(
        body, name=name, grid=(rows // tm,),
        in_specs=[pl.BlockSpec((tm, cols), lambda i: (i, 0))],
        out_specs=pl.BlockSpec((1, cols), lambda i: (0, 0)),
        out_shape=jax.ShapeDtypeStruct((1, cols), F32),
        compiler_params=_cparams("arbitrary"),
    )(xs)


def _ew(name, fn, ins, out_cols, out_dtypes, tm=512):
    rows = ins[0].shape[0]
    tm = min(tm, rows)
    ni = len(ins)

    def body(*refs):
        outs = fn(*[r[...] for r in refs[:ni]])
        for o_ref, o in zip(refs[ni:], outs):
            o_ref[...] = o.astype(o_ref.dtype)

    def spec(shape):
        if shape[0] == 1:
            return pl.BlockSpec((1, shape[1]), lambda i: (0, 0))
        return pl.BlockSpec((tm, shape[1]), lambda i: (i, 0))

    return pl.pallas_call(
        body, name=name, grid=(rows // tm,),
        in_specs=[spec(a.shape) for a in ins],
        out_specs=[pl.BlockSpec((tm, c), lambda i: (i, 0)) for c in out_cols],
        out_shape=[jax.ShapeDtypeStruct((rows, c), dt) for c, dt in zip(out_cols, out_dtypes)],
        compiler_params=_cparams("parallel"),
    )(*ins)


def _rms_parts(xv):
    r = lax.rsqrt(jnp.mean(xv * xv, axis=-1, keepdims=True) + EPS)
    return r, xv * r


def _rms_bwd_dx(dh, gain, r, xh):
    dxh = dh * gain
    return r * (dxh - xh * jnp.mean(dxh * xh, axis=-1, keepdims=True))


def _rms_fwd(name, xs, gain):
    def fn(xv, g):
        _, xh = _rms_parts(xv)
        return (xh * g,)

    return _ew(name, fn, [xs, gain], [xs.shape[1]], [BF16])[0]


def _rms_bwd(name, dh, xs, gain, dres, tm=512):
    rows, d = xs.shape
    tm = min(tm, rows)

    def body(dh_ref, x_ref, g_ref, dres_ref, dx_ref, dg_ref):
        r, xh = _rms_parts(x_ref[...])
        dhv = dh_ref[...]
        dx_ref[...] = dres_ref[...] + _rms_bwd_dx(dhv, g_ref[...], r, xh)

        @pl.when(pl.program_id(0) == 0)
        def _():
            dg_ref[...] = jnp.zeros_like(dg_ref)

        dg_ref[...] += jnp.sum(dhv * xh, axis=0, keepdims=True)

    tile = pl.BlockSpec((tm, d), lambda i: (i, 0))
    row = pl.BlockSpec((1, d), lambda i: (0, 0))
    return pl.pallas_call(
        body, name=name, grid=(rows // tm,),
        in_specs=[tile, tile, row, tile], out_specs=[tile, row],
        out_shape=[jax.ShapeDtypeStruct((rows, d), F32), jax.ShapeDtypeStruct((1, d), F32)],
        compiler_params=_cparams("arbitrary"),
    )(dh, xs, gain, dres)


def _ffn_fwd(name, xs, gain, wg_t, wu_t, wd, tm=512, tf=1408, deps=()):
    rows, d = xs.shape
    f_all = wd.shape[0]
    tm = min(tm, rows)
    nf = f_all // tf

    def body(x_ref, g_ref, wg_ref, wu_ref, wd_ref, *rest):
        xo_ref, h_ref, gg_ref, uu_ref, acc_ref = rest[-5:]
        f = pl.program_id(1)

        @pl.when(f == 0)
        def _():
            _, xh = _rms_parts(x_ref[...])
            h_ref[...] = (xh * g_ref[...]).astype(BF16)
            acc_ref[...] = jnp.zeros_like(acc_ref)

        h = h_ref[...]
        gg = _dot_nt(h, wg_ref[...])
        uu = _dot_nt(h, wu_ref[...])
        act = gg * _sigmoid(gg) * uu
        acc_ref[...] += _dot_nn(act.astype(BF16), wd_ref[...])
        gg_ref[...] = gg.astype(BF16)
        uu_ref[...] = uu.astype(BF16)

        @pl.when(f == nf - 1)
        def _():
            xo_ref[...] = x_ref[...] + 0.5 * acc_ref[...]

    tile = pl.BlockSpec((tm, d), lambda i, f: (i, 0))
    wspec = pl.BlockSpec((tf, d), lambda i, f: (f, 0))
    hid = pl.BlockSpec((tm, tf), lambda i, f: (i, f))
    return pl.pallas_call(
        body, name=name, grid=(rows // tm, nf),
        in_specs=[tile, pl.BlockSpec((1, d), lambda i, f: (0, 0)), wspec, wspec, wspec] + [_ANY_SPEC] * len(deps),
        out_specs=[tile, tile, hid, hid],
        out_shape=[jax.ShapeDtypeStruct((rows, d), F32), jax.ShapeDtypeStruct((rows, d), BF16),
                   jax.ShapeDtypeStruct((rows, f_all), BF16), jax.ShapeDtypeStruct((rows, f_all), BF16)],
        scratch_shapes=[pltpu.VMEM((tm, d), F32)],
        compiler_params=_cparams("parallel", "arbitrary"),
    )(xs, gain, wg_t, wu_t, wd, *deps)


def _ffn_bwd(name, dxo, xs, gain, gg_all, uu_all, wg_t, wu_t, wd, tm=256, tf=1408):
    rows, d = xs.shape
    f_all = wd.shape[0]
    tm = min(tm, rows)
    nf = f_all // tf

    def body(dxo_ref, x_ref, g_ref, gg_ref, uu_ref, wg_ref, wu_ref, wd_ref,
             dx_ref, dgg_ref, duu_ref, act_ref, dgain_ref, df_ref, acc_ref):
        i, f = pl.program_id(0), pl.program_id(1)

        @pl.when(f == 0)
        def _():
            df_ref[...] = (0.5 * dxo_ref[...]).astype(BF16)
            acc_ref[...] = jnp.zeros_like(acc_ref)

        gg = gg_ref[...].astype(F32)
        uu = uu_ref[...].astype(F32)
        sg = _sigmoid(gg)
        silu = gg * sg
        dact = _dot_nt(df_ref[...], wd_ref[...])
        duu = (dact * silu).astype(BF16)
        dgg = (dact * uu * (sg * (1.0 + gg * (1.0 - sg)))).astype(BF16)
        act_ref[...] = (silu * uu).astype(BF16)
        dgg_ref[...] = dgg
        duu_ref[...] = duu
        acc_ref[...] += _dot_nn(dgg, wg_ref[...]) + _dot_nn(duu, wu_ref[...])

        @pl.when(f == nf - 1)
        def _():
            r, xh = _rms_parts(x_ref[...])
            dh = acc_ref[...]
            dx_ref[...] = dxo_ref[...] + _rms_bwd_dx(dh, g_ref[...], r, xh)
            part = jnp.sum(dh * xh, axis=0, keepdims=True)

            @pl.when(i == 0)
            def _():
                dgain_ref[...] = part

            @pl.when(i > 0)
            def _():
                dgain_ref[...] += part

    tile = pl.BlockSpec((tm, d), lambda i, f: (i, 0))
    row = pl.BlockSpec((1, d), lambda i, f: (0, 0))
    wspec = pl.BlockSpec((tf, d), lambda i, f: (f, 0))
    hid = pl.BlockSpec((tm, tf), lambda i, f: (i, f))
    hid_shape = jax.ShapeDtypeStruct((rows, f_all), BF16)
    return pl.pallas_call(
        body, name=name, grid=(rows // tm, nf),
        in_specs=[tile, tile, row, hid, hid, wspec, wspec, wspec],
        out_specs=[tile, hid, hid, hid, row],
        out_shape=[jax.ShapeDtypeStruct((rows, d), F32), hid_shape, hid_shape, hid_shape,
                   jax.ShapeDtypeStruct((1, d), F32)],
        scratch_shapes=[pltpu.VMEM((tm, d), BF16), pltpu.VMEM((tm, d), F32)],
        compiler_params=_cparams("arbitrary", "arbitrary"),
    )(dxo, xs, gain, gg_all, uu_all, wg_t, wu_t, wd)


def _final_loss(name, xs, gain, target, tm=512):
    rows, d = xs.shape
    tm = min(tm, rows)

    def body(x_ref, g_ref, t_ref, dx_ref, dg_ref, loss_ref):
        r, xh = _rms_parts(x_ref[...])
        gain_v = g_ref[...]
        err = xh * gain_v - t_ref[...]
        dy = err * (1.0 / d)
        dx_ref[...] = _rms_bwd_dx(dy, gain_v, r, xh)

        @pl.when(pl.program_id(0) == 0)
        def _():
            dg_ref[...] = jnp.zeros_like(dg_ref)
            loss_ref[...] = jnp.zeros_like(loss_ref)

        dg_ref[...] += jnp.sum(dy * xh, axis=0, keepdims=True)
        per_tok = jnp.mean(err * err, axis=-1, keepdims=True)
        loss_ref[...] += 0.5 * jnp.sum(per_tok, axis=0, keepdims=True)

    tile = pl.BlockSpec((tm, d), lambda i: (i, 0))
    row = pl.BlockSpec((1, d), lambda i: (0, 0))
    return pl.pallas_call(
        body, name=name, grid=(rows // tm,),
        in_specs=[tile, row, tile],
        out_specs=[tile, row, pl.BlockSpec((1, 1), lambda i: (0, 0))],
        out_shape=[jax.ShapeDtypeStruct((rows, d), F32), jax.ShapeDtypeStruct((1, d), F32),
                   jax.ShapeDtypeStruct((1, 1), F32)],
        compiler_params=_cparams("arbitrary"),
    )(xs, gain, target)


def _t5_bucket_np(dist):
    max_exact = N_BUCKETS // 2
    dd = np.maximum(dist, 1).astype(np.float32)
    large = max_exact + (np.log(dd / np.float32(max_exact)) / np.float32(math.log(MAX_DISTANCE / max_exact))
                         * np.float32(N_BUCKETS - max_exact)).astype(np.int32)
    large = np.minimum(large, N_BUCKETS - 1)
    return np.where(dist < max_exact, dist, large).astype(np.int32)


def _attn_geometry(g, rows):
    run = rows // 16
    dil = DILATIONS[g]
    if dil == 16:
        bq = BLOCK
        return dict(view=(16, run), block=(None, bq), grid=(16, run // bq), index=lambda r, n: (r, n),
                    pos=np.arange(bq), bq=bq)
    if dil == 4:
        per = BLOCK // 4
        pos = (4 * np.arange(per)[None, :] + np.arange(4)[:, None]).reshape(-1)
        return dict(view=(4, 4, run), block=(4, None, per), grid=(4, run // per), index=lambda r, n: (0, r, n),
                    pos=pos, bq=BLOCK)
    per = 16
    pos = (16 * np.arange(per)[None, :] + np.arange(16)[:, None]).reshape(-1)
    return dict(view=(16, run), block=(16, per), grid=(1, run // per), index=lambda r, n: (0, n),
                pos=pos, bq=16 * per)


def _attn_tables(g, rows):
    geo = _attn_geometry(g, rows)
    pos, bq = geo["pos"], geo["bq"]
    steps = pos[:, None] - np.concatenate([pos - bq, pos])[None, :]
    valid = (steps >= 0) & (steps <= BLOCK)
    bucket = _t5_bucket_np((np.maximum(steps, 0) * DILATIONS[g]).astype(np.int32))
    return bucket, valid.astype(np.int32)


def _bias_fwd(name, bucket, valid, table_t):
    bq = bucket.shape[0]

    def body(bk_ref, ok_ref, tab_ref, o_ref):
        bk = bk_ref[...]
        ok = ok_ref[...] > 0
        for h in range(HEADS_PER_GROUP):
            acc = jnp.zeros(bk.shape, F32)
            for b in range(N_BUCKETS):
                acc = jnp.where(bk == b, tab_ref[h, b], acc)
            o_ref[h] = jnp.where(ok, acc, NEG_INF)

    vm = pl.BlockSpec(memory_space=pltpu.VMEM)
    return pl.pallas_call(
        body, name=name, in_specs=[vm, vm, pl.BlockSpec(memory_space=pltpu.SMEM)], out_specs=vm,
        out_shape=jax.ShapeDtypeStruct((HEADS_PER_GROUP, bq, 2 * bq), F32),
    )(bucket, valid, table_t)


def _bias_bwd(name, bucket, dbias):
    def body(bk_ref, db_ref, o_ref):
        row_id = lax.broadcasted_iota(jnp.int32, (N_BUCKETS, 128), 0)
        col_id = lax.broadcasted_iota(jnp.int32, (N_BUCKETS, 128), 1)
        bk = bk_ref[...]
        acc = jnp.zeros((N_BUCKETS, 128), F32)
        for h in range(HEADS_PER_GROUP):
            db = db_ref[h]
            for b in range(N_BUCKETS):
                part = jnp.sum(jnp.where(bk == b, db, 0.0), axis=0, keepdims=True)
                tot = jnp.sum(part, axis=1, keepdims=True)
                acc = jnp.where((row_id == b) & (col_id == h), tot, acc)
        o_ref[...] = acc

    vm = pl.BlockSpec(memory_space=pltpu.VMEM)
    return pl.pallas_call(body, name=name, in_specs=[vm, vm], out_specs=vm,
                          out_shape=jax.ShapeDtypeStruct((N_BUCKETS, 128), F32))(bucket, dbias)


def _head_of_lane(nrows):
    return lax.broadcasted_iota(jnp.int32, (nrows, ATTN_OUT), 1) // HEAD_DIM


def _stack_heads(a, lane_head):
    zero = jnp.zeros_like(a)
    return jnp.concatenate([jnp.where(lane_head == h, a, zero) for h in range(HEADS_PER_GROUP)], axis=0)


def _unstack_heads(a4, lane_head, bq):
    out = a4[:bq]
    for h in range(1, HEADS_PER_GROUP):
        out = jnp.where(lane_head == h, a4[h * bq:(h + 1) * bq], out)
    return out


def _attn_specs(geo, cols, col_block, index):
    return pl.BlockSpec(geo["block"] + (cols,), lambda r, n: index(r, n) + (col_block,))


def _attn_fwd(name, qkv, g, bias4):
    rows = qkv.shape[0]
    geo = _attn_geometry(g, rows)
    bq, (nsub, nb), index = geo["bq"], geo["grid"], geo["index"]
    blk_shape = tuple(b for b in geo["block"] if b is not None) + (ATTN_OUT,)

    def body(q_ref, kc_ref, kp_ref, vc_ref, vp_ref, b_ref, o_ref, lse_ref):
        n = pl.program_id(1)
        lane_head = _head_of_lane(bq)
        flat = lambda ref: ref[...].reshape(bq, ATTN_OUT)
        q4 = _stack_heads(flat(q_ref), lane_head)
        k2 = jnp.concatenate([flat(kp_ref), flat(kc_ref)], axis=0)
        v2 = jnp.concatenate([flat(vp_ref), flat(vc_ref)], axis=0)
        s = _dot_nt(q4, k2) + b_ref[...]
        col = lax.broadcasted_iota(jnp.int32, s.shape, 1)
        s = jnp.where((col >= bq) | (n > 0), s, NEG_INF)
        mx = jnp.max(s, axis=-1, keepdims=True)
        p = jnp.exp(s - mx)
        den = jnp.sum(p, axis=-1, keepdims=True)
        o4 = _dot_nn(p.astype(BF16), v2) / den
        lse4 = jnp.broadcast_to(mx + jnp.log(den), (HEADS_PER_GROUP * bq, ATTN_OUT))
        o_ref[...] = _unstack_heads(o4, lane_head, bq).reshape(blk_shape)
        lse_ref[...] = _unstack_heads(lse4, lane_head, bq).reshape(blk_shape)

    prev = lambda r, n: index(r, jnp.maximum(n - 1, 0))
    view = lambda a: a.reshape(geo["view"] + (a.shape[1],))
    qkv_v = view(qkv)
    out_spec = _attn_specs(geo, ATTN_OUT, 0, index)
    out_shape = jax.ShapeDtypeStruct(geo["view"] + (ATTN_OUT,), F32)
    o, lse = pl.pallas_call(
        body, name=name, grid=(nsub, nb),
        in_specs=[_attn_specs(geo, ATTN_OUT, g, index), _attn_specs(geo, ATTN_OUT, 3 + g, index),
                  _attn_specs(geo, ATTN_OUT, 3 + g, prev), _attn_specs(geo, ATTN_OUT, 6 + g, index),
                  _attn_specs(geo, ATTN_OUT, 6 + g, prev), pl.BlockSpec(bias4.shape, lambda r, n: (0, 0))],
        out_specs=[out_spec, out_spec], out_shape=[out_shape, out_shape],
        compiler_params=_cparams("parallel", "arbitrary"),
    )(qkv_v, qkv_v, qkv_v, qkv_v, qkv_v, bias4)
    return o.reshape(rows, ATTN_OUT), lse.reshape(rows, ATTN_OUT)


def _attn_bwd(name, qkv, do, lse, cvec, g, bias4):
    rows = qkv.shape[0]
    geo = _attn_geometry(g, rows)
    bq, (nsub, nb), index = geo["bq"], geo["grid"], geo["index"]
    blk_shape = tuple(b for b in geo["block"] if b is not None) + (ATTN_OUT,)
    nlead = len(blk_shape) - 1

    def body(q_ref, kc_ref, kp_ref, vc_ref, vp_ref, do_ref, lse_ref, c_ref, b_ref,
             dq_ref, dk_ref, dv_ref, db_ref, kcar_ref, vcar_ref):
        r, n = pl.program_id(0), pl.program_id(1)
        valid = n < nb
        lane_head = _head_of_lane(bq)
        flat = lambda ref: ref[...].reshape(bq, ATTN_OUT)

        @pl.when((r == 0) & (n == 0))
        def _():
            kcar_ref[...] = jnp.zeros_like(kcar_ref)
            vcar_ref[...] = jnp.zeros_like(vcar_ref)
            db_ref[...] = jnp.zeros_like(db_ref)

        def column(ref, h):
            lead = (slice(None),) * nlead
            return ref[lead + (pl.ds(h * HEAD_DIM, 1),)].reshape(bq, 1)

        q4 = _stack_heads(flat(q_ref), lane_head)
        do4 = _stack_heads(flat(do_ref), lane_head)
        k2 = jnp.concatenate([flat(kp_ref), flat(kc_ref)], axis=0)
        v2 = jnp.concatenate([flat(vp_ref), flat(vc_ref)], axis=0)
        lse4 = jnp.concatenate([column(lse_ref, h) for h in range(HEADS_PER_GROUP)], axis=0)
        c4 = jnp.concatenate([column(c_ref, h) for h in range(HEADS_PER_GROUP)], axis=0)
        s = _dot_nt(q4, k2) + b_ref[...]
        col = lax.broadcasted_iota(jnp.int32, s.shape, 1)
        keep = ((col >= bq) | (n > 0)) & valid
        p = jnp.where(keep, jnp.exp(s - lse4), 0.0)
        ds = p * (_dot_nt(do4, v2) + c4)
        ds_b = ds.astype(BF16)

        @pl.when(valid)
        def _():
            dq = _unstack_heads(_dot_nn(ds_b, k2), lane_head, bq) * (HEAD_DIM ** -0.5)
            dq_ref[...] = dq.astype(BF16).reshape(blk_shape)

        dk2 = _dot_tn(ds_b, q4)
        dv2 = _dot_tn(p.astype(BF16), do4)
        dk_ref[...] = (kcar_ref[...] + dk2[:bq]).astype(BF16).reshape(blk_shape)
        dv_ref[...] = (vcar_ref[...] + dv2[:bq]).astype(BF16).reshape(blk_shape)
        kcar_ref[...] = dk2[bq:]
        vcar_ref[...] = dv2[bq:]
        db_ref[...] += ds

    cur = lambda r, n: index(r, jnp.minimum(n, nb - 1))
    prev = lambda r, n: index(r, jnp.maximum(jnp.minimum(n, nb - 1) - 1, 0))
    late = lambda r, n: index(r, jnp.maximum(n - 1, 0))
    view = lambda a: a.reshape(geo["view"] + (a.shape[1],))
    qkv_v = view(qkv)
    tile = _attn_specs(geo, ATTN_OUT, 0, cur)
    bias_spec = pl.BlockSpec(bias4.shape, lambda r, n: (0, 0))
    out_shape = jax.ShapeDtypeStruct(geo["view"] + (ATTN_OUT,), BF16)
    dq, dk, dv, db = pl.pallas_call(
        body, name=name, grid=(nsub, nb + 1),
        in_specs=[_attn_specs(geo, ATTN_OUT, g, cur), _attn_specs(geo, ATTN_OUT, 3 + g, cur),
                  _attn_specs(geo, ATTN_OUT, 3 + g, prev), _attn_specs(geo, ATTN_OUT, 6 + g, cur),
                  _attn_specs(geo, ATTN_OUT, 6 + g, prev), tile, tile, tile, bias_spec],
        out_specs=[tile, _attn_specs(geo, ATTN_OUT, 0, late), _attn_specs(geo, ATTN_OUT, 0, late), bias_spec],
        out_shape=[out_shape, out_shape, out_shape, jax.ShapeDtypeStruct(bias4.shape, F32)],
        scratch_shapes=[pltpu.VMEM((bq, ATTN_OUT), F32), pltpu.VMEM((bq, ATTN_OUT), F32)],
        compiler_params=_cparams("arbitrary", "arbitrary"),
    )(qkv_v, qkv_v, qkv_v, qkv_v, qkv_v, view(do), view(lse), view(cvec), bias4)
    return dq.reshape(rows, ATTN_OUT), dk.reshape(rows, ATTN_OUT), dv.reshape(rows, ATTN_OUT), db


def _group_weights(lses):
    mx = jnp.maximum(jnp.maximum(lses[0], lses[1]), lses[2])
    es = [jnp.exp(l - mx) for l in lses]
    den = es[0] + es[1] + es[2]
    return [e / den for e in es]


def _combine_fwd(name, os_, lses):
    def fn(o0, o1, o2, l0, l1, l2):
        ws = _group_weights([l0, l1, l2])
        out = ws[0] * o0 + ws[1] * o1 + ws[2] * o2
        return out, out

    return _ew(name, fn, [*os_, *lses], [ATTN_OUT, ATTN_OUT], [F32, BF16], tm=1024)


def _combine_bwd(name, do, oa, lses):
    def fn(dov, oav, l0, l1, l2):
        head_sum = (lax.broadcasted_iota(jnp.int32, (ATTN_OUT, ATTN_OUT), 0) // HEAD_DIM
                    == lax.broadcasted_iota(jnp.int32, (ATTN_OUT, ATTN_OUT), 1) // HEAD_DIM)
        ws = _group_weights([l0, l1, l2])
        prod = dov * oav
        hi = prod.astype(BF16)
        lo = (prod - hi.astype(F32)).astype(BF16)
        ones = jnp.where(head_sum, 1.0, 0.0).astype(BF16)
        bar = _dot_nn(hi, ones) + _dot_nn(lo, ones)
        return tuple(w * dov for w in ws) + tuple(-w * bar for w in ws)

    return _ew(name, fn, [do, oa, *lses], [ATTN_OUT] * 6, [BF16] * 3 + [F32] * 3, tm=1024)


def _ssm_disc(a_re, a_im, log_dt, b_re, b_im):
    dt = jnp.exp(log_dt)
    mag = jnp.exp(a_re * dt)
    ab_re = mag * jnp.cos(a_im * dt)
    ab_im = mag * jnp.sin(a_im * dt)
    den = a_re * a_re + a_im * a_im
    xr = ab_re - 1.0
    coef_re = (xr * a_re + ab_im * a_im) / den
    coef_im = (ab_im * a_re - xr * a_im) / den
    bb_re = coef_re[None] * b_re - coef_im[None] * b_im
    bb_im = coef_re[None] * b_im + coef_im[None] * b_re
    return ab_re, ab_im, bb_re, bb_im


def _cpow2(re, im, times):
    for _ in range(times):
        re, im = re * re - im * im, 2.0 * re * im
    return re, im


def _ssm_params_fwd(name, a_re, a_im, log_dt, b_re, b_im):
    gn = jax.ShapeDtypeStruct(a_re.shape, F32)
    cgn = jax.ShapeDtypeStruct(b_re.shape, F32)

    def body(ar, ai, ld, br, bi, o_abr, o_abi, o_apr, o_api, o_bbr, o_bbi):
        ab_re, ab_im, bb_re, bb_im = _ssm_disc(ar[...], ai[...], ld[...], br[...], bi[...])
        o_abr[...] = ab_re
        o_abi[...] = ab_im
        pr, pi = _cpow2(ab_re, ab_im, int(math.log2(SCAN_STEPS)))
        o_apr[...] = pr
        o_api[...] = pi
        o_bbr[...] = bb_re
        o_bbi[...] = bb_im

    vm = pl.BlockSpec(memory_space=pltpu.VMEM)
    return pl.pallas_call(body, name=name, in_specs=[vm] * 5, out_specs=[vm] * 6,
                          out_shape=[gn, gn, gn, gn, cgn, cgn])(a_re, a_im, log_dt, b_re, b_im)


def _ssm_params_bwd(name, a_re, a_im, log_dt, b_re, b_im, d_ab_re, d_ab_im, d_bb_re, d_bb_im):
    gn = jax.ShapeDtypeStruct(a_re.shape, F32)
    cgn = jax.ShapeDtypeStruct(b_re.shape, F32)

    def body(ar, ai, ld, br, bi, g0, g1, g2, g3, o_ar, o_ai, o_ld, o_br, o_bi):
        _, vjp = jax.vjp(_ssm_disc, ar[...], ai[...], ld[...], br[...], bi[...])
        outs = vjp((g0[...], g1[...], g2[...], g3[...]))
        for o_ref, o in zip((o_ar, o_ai, o_ld, o_br, o_bi), outs):
            o_ref[...] = o

    vm = pl.BlockSpec(memory_space=pltpu.VMEM)
    return pl.pallas_call(body, name=name, in_specs=[vm] * 9, out_specs=[vm] * 5,
                          out_shape=[gn, gn, jax.ShapeDtypeStruct(log_dt.shape, F32), cgn, cgn],
                          )(a_re, a_im, log_dt, b_re, b_im, d_ab_re, d_ab_im, d_bb_re, d_bb_im)


def _scan_block(s_ref, carry_ref, tmp_ref, ab_ref, ap_ref, reverse, sprev=None):
    nl = SSM_LANES
    halves = range(SCAN_COLS // SCAN_SUB)
    zero = jnp.zeros((SCAN_SUB, SCAN_LANES), F32)
    for half in (reversed(halves) if reverse else halves):
        sub_rows = pl.ds(half * SCAN_SUB, SCAN_SUB)
        for lc in range(nl // SCAN_LANES):
            re_l = pl.ds(lc * SCAN_LANES, SCAN_LANES)
            im_l = pl.ds(nl + lc * SCAN_LANES, SCAN_LANES)
            are, aim = ab_ref[:, re_l], ab_ref[:, im_l]

            def step_of(j):
                return SCAN_STEPS - 1 - j if reverse else j

            def pass1(j, st):
                sr, si = st
                jj = step_of(j)
                nr = are * sr - aim * si + s_ref[jj, sub_rows, re_l]
                ni = are * si + aim * sr + s_ref[jj, sub_rows, im_l]
                s_ref[jj, sub_rows, re_l] = nr
                s_ref[jj, sub_rows, im_l] = ni
                return nr, ni

            er, ei = lax.fori_loop(0, SCAN_STEPS, pass1, (zero, zero), unroll=2)
            tmp_ref[0:SCAN_SUB, re_l] = er
            tmp_ref[0:SCAN_SUB, im_l] = ei
            apr, api = ap_ref[0:1, re_l], ap_ref[0:1, im_l]
            sr, si = carry_ref[0:1, re_l], carry_ref[0:1, im_l]
            for step in range(SCAN_SUB):
                c = SCAN_SUB - 1 - step if reverse else step
                tmp_ref[SCAN_SUB + c:SCAN_SUB + c + 1, re_l] = sr
                tmp_ref[SCAN_SUB + c:SCAN_SUB + c + 1, im_l] = si
                e_r, e_i = tmp_ref[c:c + 1, re_l], tmp_ref[c:c + 1, im_l]
                sr, si = apr * sr - api * si + e_r, apr * si + api * sr + e_i
            carry_ref[0:1, re_l] = sr
            carry_ref[0:1, im_l] = si
            cr = tmp_ref[SCAN_SUB:2 * SCAN_SUB, re_l]
            ci = tmp_ref[SCAN_SUB:2 * SCAN_SUB, im_l]

            if sprev is None:
                def pass2(j, st):
                    pr, pi = st
                    jj = step_of(j)
                    s_ref[jj, sub_rows, re_l] += pr * cr - pi * ci
                    s_ref[jj, sub_rows, im_l] += pr * ci + pi * cr
                    return pr * are - pi * aim, pr * aim + pi * are

                lax.fori_loop(0, SCAN_STEPS, pass2, (are, aim), unroll=2)
            else:
                st_ref, prev_ref, have_prev, dab_ref = sprev

                def corrected(jj, pr, pi):
                    gr = s_ref[jj, sub_rows, re_l] + pr * cr - pi * ci
                    gi = s_ref[jj, sub_rows, im_l] + pr * ci + pi * cr
                    s_ref[jj, sub_rows, re_l] = gr
                    s_ref[jj, sub_rows, im_l] = gi
                    return gr, gi

                def pass2(j, st):
                    pr, pi, dr, di = st
                    jj = SCAN_STEPS - 1 - j
                    gr, gi = corrected(jj, pr, pi)
                    qr, qi = st_ref[jj - 1, sub_rows, re_l], st_ref[jj - 1, sub_rows, im_l]
                    return (pr * are - pi * aim, pr * aim + pi * are,
                            dr + gr * qr + gi * qi, di + gi * qr - gr * qi)

                pr, pi, dr, di = lax.fori_loop(0, SCAN_STEPS - 1, pass2, (are, aim, zero, zero), unroll=2)
                gr, gi = corrected(0, pr, pi)
                sub = lax.broadcasted_iota(jnp.int32, (SCAN_SUB, SCAN_LANES), 0)
                if half == 0:
                    pv_r = prev_ref[SCAN_SUB - 1:SCAN_SUB, re_l] * have_prev
                    pv_i = prev_ref[SCAN_SUB - 1:SCAN_SUB, im_l] * have_prev
                else:
                    before = pl.ds(half * SCAN_SUB - 1, 1)
                    pv_r, pv_i = st_ref[SCAN_STEPS - 1, before, re_l], st_ref[SCAN_STEPS - 1, before, im_l]
                shape = (SCAN_SUB, SCAN_LANES)
                qr = jnp.where(sub == 0, jnp.broadcast_to(pv_r, shape),
                               pltpu.roll(st_ref[SCAN_STEPS - 1, sub_rows, re_l], 1, 0))
                qi = jnp.where(sub == 0, jnp.broadcast_to(pv_i, shape),
                               pltpu.roll(st_ref[SCAN_STEPS - 1, sub_rows, im_l], 1, 0))
                dab_ref[:, re_l] += dr + gr * qr + gi * qi
                dab_ref[:, im_l] += di + gi * qr - gr * qi


def _scan_view(a):
    return a.reshape(16, a.shape[0] // 16, a.shape[1])


def _ssm_fwd(name, u, bb_mat, c_mat, ab_rows, ap_rows, d_skip):
    rows = u.shape[0]
    nl2 = 2 * SSM_LANES
    nblk = rows // SCAN_BLOCK

    def body(u_ref, bb_ref, c_ref, ab_ref, ap_ref, d_ref, y_ref, s_ref, carry_ref, tmp_ref):
        @pl.when(pl.program_id(0) == 0)
        def _():
            carry_ref[...] = jnp.zeros_like(carry_ref)

        uv = u_ref[...].reshape(SCAN_BLOCK, SSM_WIDTH)
        s_ref[...] = _dot_nn(uv.astype(BF16), bb_ref[...]).reshape(16, SCAN_COLS, nl2)
        _scan_block(s_ref, carry_ref, tmp_ref, ab_ref, ap_ref, reverse=False)
        sv = s_ref[...].reshape(SCAN_BLOCK, nl2)
        y_ref[...] = (_dot_nn(sv.astype(BF16), c_ref[...]) + d_ref[...] * uv).reshape(16, SCAN_COLS, SSM_WIDTH)

    const = lambda shape: pl.BlockSpec(shape, lambda i: (0, 0))
    blk = lambda cols: pl.BlockSpec((16, SCAN_COLS, cols), lambda i: (0, i, 0))
    y, s = pl.pallas_call(
        body, name=name, grid=(nblk,),
        in_specs=[blk(SSM_WIDTH), const((SSM_WIDTH, nl2)), const((nl2, SSM_WIDTH)), const((SCAN_SUB, nl2)),
                  const((SCAN_SUB, nl2)), const((1, SSM_WIDTH))],
        out_specs=[blk(SSM_WIDTH), blk(nl2)],
        out_shape=[jax.ShapeDtypeStruct((16, rows // 16, SSM_WIDTH), F32),
                   jax.ShapeDtypeStruct((16, rows // 16, nl2), F32)],
        scratch_shapes=[pltpu.VMEM((SCAN_SUB, nl2), F32), pltpu.VMEM((2 * SCAN_SUB, nl2), F32)],
        compiler_params=_cparams("arbitrary"),
    )(_scan_view(u), bb_mat, c_mat, ab_rows, ap_rows, d_skip)
    return y.reshape(rows, SSM_WIDTH), s.reshape(rows, nl2)


def _ssm_bwd(name, dy, u, states, c_mat_t, bb_mat_t, abc_rows, apc_rows, d_skip):
    rows = u.shape[0]
    nl2 = 2 * SSM_LANES
    nblk = rows // SCAN_BLOCK

    def body(dy_ref, u_ref, st_ref, prev_ref, ct_ref, bt_ref, ab_ref, ap_ref, d_ref,
             du_ref, g_ref, dab_ref, dd_ref, carry_ref, tmp_ref):
        i = pl.program_id(0)

        @pl.when(i == 0)
        def _():
            carry_ref[...] = jnp.zeros_like(carry_ref)
            dab_ref[...] = jnp.zeros_like(dab_ref)
            dd_ref[...] = jnp.zeros_like(dd_ref)

        dyv = dy_ref[...].reshape(SCAN_BLOCK, SSM_WIDTH)
        g_ref[...] = _dot_nn(dyv.astype(BF16), ct_ref[...]).reshape(16, SCAN_COLS, nl2)
        have_prev = (i < nblk - 1).astype(F32)
        _scan_block(g_ref, carry_ref, tmp_ref, ab_ref, ap_ref, reverse=True,
                    sprev=(st_ref, prev_ref, have_prev, dab_ref))
        gv = g_ref[...].reshape(SCAN_BLOCK, nl2)
        du_ref[...] = (_dot_nn(gv.astype(BF16), bt_ref[...]) + d_ref[...] * dyv).reshape(16, SCAN_COLS, SSM_WIDTH)
        dd_ref[...] += jnp.sum(dyv * u_ref[...].reshape(SCAN_BLOCK, SSM_WIDTH), axis=0, keepdims=True)

    const = lambda shape: pl.BlockSpec(shape, lambda i: (0, 0))
    blk = lambda cols: pl.BlockSpec((16, SCAN_COLS, cols), lambda i: (0, nblk - 1 - i, 0))
    per8 = SCAN_COLS // SCAN_SUB
    prev_spec = pl.BlockSpec((None, SCAN_SUB, nl2), lambda i: (15, jnp.maximum((nblk - 1 - i) * per8 - 1, 0), 0))
    sv = _scan_view(states)
    du, g, dab, dd = pl.pallas_call(
        body, name=name, grid=(nblk,),
        in_specs=[blk(SSM_WIDTH), blk(SSM_WIDTH), blk(nl2), prev_spec, const((SSM_WIDTH, nl2)),
                  const((nl2, SSM_WIDTH)), const((SCAN_SUB, nl2)), const((SCAN_SUB, nl2)), const((1, SSM_WIDTH))],
        out_specs=[blk(SSM_WIDTH), blk(nl2), const((SCAN_SUB, nl2)), const((1, SSM_WIDTH))],
        out_shape=[jax.ShapeDtypeStruct((16, rows // 16, SSM_WIDTH), F32),
                   jax.ShapeDtypeStruct((16, rows // 16, nl2), F32),
                   jax.ShapeDtypeStruct((SCAN_SUB, nl2), F32), jax.ShapeDtypeStruct((1, SSM_WIDTH), F32)],
        scratch_shapes=[pltpu.VMEM((SCAN_SUB, nl2), F32), pltpu.VMEM((2 * SCAN_SUB, nl2), F32)],
        compiler_params=_cparams("arbitrary"),
    )(_scan_view(dy), _scan_view(u), sv, sv, c_mat_t, bb_mat_t, abc_rows, apc_rows, d_skip)
    return du.reshape(rows, SSM_WIDTH), g.reshape(rows, nl2), dab, dd


def _adamw(name, w, m, v, gparts, tr):
    rows, cols = w.shape

    def body(w_ref, m_ref, v_ref, g_ref, og_ref, od_ref, om_ref, ov_ref):
        g = g_ref[0].astype(F32)
        for i in range(1, N_DEV):
            g = g + g_ref[i].astype(F32)
        m_new = B1 * m_ref[...] + (1.0 - B1) * g
        v_new = B2 * v_ref[...] + (1.0 - B2) * (g * g)
        m_hat = m_new / (1.0 - B1 ** STEP)
        v_hat = v_new / (1.0 - B2 ** STEP)
        og_ref[...] = g
        od_ref[...] = -LR * (m_hat / (jnp.sqrt(v_hat) + ADAM_EPS) + WD * w_ref[...])
        om_ref[...] = m_new
        ov_ref[...] = v_new

    spec = pl.BlockSpec((tr, cols), lambda i: (i, 0))
    shape = jax.ShapeDtypeStruct((rows, cols), F32)
    return pl.pallas_call(
        body, name=name, grid=(rows // tr,),
        in_specs=[spec, spec, spec, pl.BlockSpec((N_DEV, tr, cols), lambda i: (0, i, 0))],
        out_specs=[spec] * 4, out_shape=[shape] * 4,
        compiler_params=_cparams("parallel"),
    )(w, m, v, gparts)


_SHARDED = (
    ("ffn1_w_gate", True, (352, 1024)), ("ffn1_w_up", True, (352, 1024)), ("ffn1_w_down", False, (352, 1024)),
    ("w_in", True, (608, 1024)), ("ssm_w_glu", True, (128, 512)), ("w_attn_branch", True, (128, 256)),
    ("w_ssm_branch", True, (128, 512)), ("w_out", False, (128, 1024)),
    ("ffn2_w_gate", True, (352, 1024)), ("ffn2_w_up", True, (352, 1024)), ("ffn2_w_down", False, (352, 1024)),
)
_SMALL = ("ffn1_norm", "mix_norm", "gate_bias", "rel_bias_table", "ssm_a_re", "ssm_a_im", "ssm_log_dt",
          "ssm_b_re", "ssm_b_im", "ssm_c_re", "ssm_c_im", "ssm_d", "ffn2_norm", "final_norm")
_ORDER = ("ffn1_norm", "ffn1_w_gate", "ffn1_w_up", "ffn1_w_down", "mix_norm", "w_in", "gate_bias",
          "rel_bias_table", "ssm_a_re", "ssm_a_im", "ssm_log_dt", "ssm_b_re", "ssm_b_im", "ssm_c_re",
          "ssm_c_im", "ssm_d", "ssm_w_glu", "w_attn_branch", "w_ssm_branch", "w_out", "ffn2_norm",
          "ffn2_w_gate", "ffn2_w_up", "ffn2_w_down", "final_norm")


def _pack_rows(shape):
    return shape[0] * shape[1] // D_MODEL


_SHARD_INFO = {nm: (tr, shape) for nm, tr, shape in _SHARDED}
_PHASES = {
    "f1gu": ("ffn1_w_gate", "ffn1_w_up"), "f1d": ("ffn1_w_down",),
    "mix": ("w_in", "ssm_w_glu", "w_attn_branch", "w_ssm_branch", "w_out"),
    "f2": ("ffn2_w_gate", "ffn2_w_up", "ffn2_w_down"),
}


def _pack_sharded(ws, names):
    parts = []
    for nm in names:
        tr, shape = _SHARD_INFO[nm]
        a = ws[nm].T if tr else ws[nm]
        parts.append(a.reshape(_pack_rows(shape), D_MODEL))
    return jnp.concatenate(parts, axis=0)


def _unpack_sharded(pack, names):
    out, r0 = {}, 0
    for nm in names:
        tr, shape = _SHARD_INFO[nm]
        n = _pack_rows(shape)
        a = pack[r0:r0 + n].reshape(shape)
        out[nm] = a.T if tr else a
        r0 += n
    return out


def _unpack_gathered(gath, names):
    out, r0 = {}, 0
    for nm in names:
        _, shape = _SHARD_INFO[nm]
        n = _pack_rows(shape)
        out[nm] = gath[:, r0:r0 + n].reshape(N_DEV * shape[0], shape[1])
        r0 += n
    return out


def _pack_grads(gs, names):
    parts = []
    for nm in names:
        _, shape = _SHARD_INFO[nm]
        parts.append(gs[nm].astype(BF16).reshape(N_DEV, _pack_rows(shape), D_MODEL))
    return jnp.concatenate(parts, axis=1)


def _pack_small(ws):
    flat = jnp.concatenate([ws[nm].reshape(-1) for nm in _SMALL])
    pad = (-flat.shape[0]) % (8 * 128)
    return jnp.pad(flat, (0, pad)).reshape(-1, 128)


def _unpack_small(pack, like):
    flat, out, p0 = pack.reshape(-1), {}, 0
    for nm in _SMALL:
        n = like[nm].size
        out[nm] = flat[p0:p0 + n].reshape(like[nm].shape)
        p0 += n
    return out


def _residue_order(a):
    rows, cols = a.shape
    return a.reshape(rows // 16, 16, cols).transpose(1, 0, 2).reshape(rows, cols)


def _token_order(a):
    rows, cols = a.shape
    return a.reshape(16, rows // 16, cols).transpose(1, 0, 2).reshape(rows, cols)


def _block_diag(blocks_gab):
    g, a, b = blocks_gab.shape
    eye = jnp.eye(g, dtype=blocks_gab.dtype)
    return (blocks_gab[:, :, None, :] * eye[:, None, :, None]).reshape(g * a, g * b)


def _diag_blocks(mat, a, b):
    g = mat.shape[0] // a
    eye = jnp.eye(g, dtype=mat.dtype)
    return jnp.einsum("gahb,gh->gab", mat.reshape(g, a, g, b), eye)


def _local_step(xs, target, small, weights_of, send_grads, first_deps=()):
    rows = xs.shape[0]
    gfull, gsmall = {}, {}
    wf = dict(weights_of("f1", None))

    x1, h1, gg1, uu1 = _ffn_fwd("ffn1_fwd", xs, small["ffn1_norm"], wf["ffn1_w_gate"], wf["ffn1_w_up"],
                                wf["ffn1_w_down"], deps=first_deps)
    wf.update(weights_of("mix", x1))
    hmix = _rms_fwd("mix_norm_fwd", x1, small["mix_norm"])
    w_in = wf["w_in"]
    w_qkv, w_u, w_g = w_in[:3 * ATTN_WIDTH], w_in[3 * ATTN_WIDTH:3 * ATTN_WIDTH + SSM_WIDTH], w_in[3 * ATTN_WIDTH + SSM_WIDTH:]
    qscale = jnp.concatenate([jnp.full((1, ATTN_WIDTH), HEAD_DIM ** -0.5, F32), jnp.ones((1, 2 * ATTN_WIDTH), F32)], axis=1)
    qkv, = _mm("in_qkv", [(hmix, w_qkv)], True, 3 * ATTN_WIDTH, [BF16],
               epilogue=lambda acc, sc: (acc * sc,), extras=[(qscale, 0)], tn=ATTN_WIDTH)
    u, = _mm("in_u", [(hmix, w_u)], True, SSM_WIDTH, [F32])
    gates, = _mm("in_gates", [(hmix, w_g)], True, 2 * D_MODEL, [F32],
                 epilogue=lambda acc, b: (_sigmoid(acc + b),), extras=[(small["gate_bias"], 0)])

    table_t = small["rel_bias_table"].T
    tables, bias4, o_g, lse_g = [], [], [], []
    for g in range(N_GROUPS):
        bucket, valid = [jnp.asarray(t) for t in _attn_tables(g, rows)]
        bias_g = _bias_fwd(f"rel_bias_fwd_{g}", bucket, valid, table_t[g * HEADS_PER_GROUP:(g + 1) * HEADS_PER_GROUP])
        tables.append(bucket)
        bias4.append(bias_g.reshape(-1, bias_g.shape[-1]))
        o, lse = _attn_fwd(f"attn_fwd_{g}", qkv, g, bias4[g])
        o_g.append(o)
        lse_g.append(lse)
    oa_f32, oa = _combine_fwd("attn_combine_fwd", o_g, lse_g)
    y_attn, = _mm("attn_branch", [(oa, wf["w_attn_branch"])], True, D_MODEL, [F32])

    ab_re, ab_im, ap_re, ap_im, bb_re, bb_im = _ssm_params_fwd(
        "ssm_params_fwd", small["ssm_a_re"], small["ssm_a_im"], small["ssm_log_dt"].reshape(SSM_GROUPS, 1),
        small["ssm_b_re"].transpose(2, 0, 1), small["ssm_b_im"].transpose(2, 0, 1))

    def lanes(re, im, sign=1.0):
        row = jnp.concatenate([re.reshape(1, SSM_LANES), sign * im.reshape(1, SSM_LANES)], axis=1)
        return jnp.broadcast_to(row, (SCAN_SUB, 2 * SSM_LANES))

    bb_mat = jnp.concatenate([_block_diag(bb_re.transpose(1, 0, 2)), _block_diag(bb_im.transpose(1, 0, 2))], axis=1)
    c_mat_t = jnp.concatenate([_block_diag(small["ssm_c_re"]), -_block_diag(small["ssm_c_im"])], axis=1)
    bb_mat, c_mat_t = bb_mat.astype(BF16), c_mat_t.astype(BF16)
    d_skip = small["ssm_d"].reshape(1, SSM_WIDTH)
    y_raw, states = _ssm_fwd("ssm_fwd", u, bb_mat, c_mat_t.T, lanes(ab_re, ab_im), lanes(ap_re, ap_im), d_skip)

    def gelu_fn(yv):
        return (jax.nn.gelu(yv),)

    ygelu, = _ew("ssm_gelu", gelu_fn, [y_raw], [SSM_WIDTH], [BF16])
    glu, = _mm("ssm_glu", [(ygelu, wf["ssm_w_glu"])], True, 2 * SSM_WIDTH, [F32])
    ysg, = _ew("ssm_glu_act", lambda gv: (gv[:, :SSM_WIDTH] * _sigmoid(gv[:, SSM_WIDTH:]),), [glu], [SSM_WIDTH], [BF16])
    y_ssm, merged = _mm("ssm_branch_merge", [(ysg, wf["w_ssm_branch"])], True, D_MODEL, [F32, BF16],
                        epilogue=lambda acc, ga, gs, ya: (acc, ga * ya + gs * acc),
                        extras=[(gates, 0), (gates, D_MODEL), (y_attn, 0)])
    x2, = _mm("mix_out", [(merged, wf["w_out"])], False, D_MODEL, [F32],
              epilogue=lambda acc, res: (res + acc,), extras=[(x1, 0)])
    wf.update(weights_of("f2", x2))
    x3, h2, gg2, uu2 = _ffn_fwd("ffn2_fwd", x2, small["ffn2_norm"], wf["ffn2_w_gate"], wf["ffn2_w_up"],
                                wf["ffn2_w_down"])
    dx3, gsmall["final_norm"], loss = _final_loss("final_loss", x3, small["final_norm"].reshape(1, D_MODEL), target)

    dx2, dgg2, duu2, act2, gsmall["ffn2_norm"] = _ffn_bwd(
        "ffn2_bwd", dx3, x2, small["ffn2_norm"], gg2, uu2, wf["ffn2_w_gate"], wf["ffn2_w_up"], wf["ffn2_w_down"])
    gfull["ffn2_w_gate"] = _mm_tn("ffn2_dwg", dgg2, h2)
    gfull["ffn2_w_up"] = _mm_tn("ffn2_dwu", duu2, h2)
    gfull["ffn2_w_down"] = _mm_tn("ffn2_dwd", act2, dx3, scale=0.5)
    sent = send_grads("f2", gfull)

    def merge_bwd(dm, ga, gs, ya, ys):
        return (dm * ga, dm * gs, dm * ya * ga * (1.0 - ga), dm * ys * gs * (1.0 - gs))

    dya, dys, dzga, dzgs = _mm("mix_out_bwd", [(dx2, wf["w_out"])], True, D_MODEL, [BF16] * 4, epilogue=merge_bwd,
                               extras=[(gates, 0), (gates, D_MODEL), (y_attn, 0), (y_ssm, 0)], deps=sent)
    gfull["w_out"] = _mm_tn("dw_out", merged, dx2)
    gsmall["gate_bias"] = jnp.concatenate([_colsum("dgate_bias_a", dzga), _colsum("dgate_bias_s", dzgs)], axis=1)

    gfull["w_ssm_branch"] = _mm_tn("dw_ssm_branch", dys, ysg)

    def glu_bwd(dysg, av, bv):
        sb = _sigmoid(bv)
        return (dysg * sb, dysg * av * sb * (1.0 - sb))

    dglu_a, dglu_b = _mm("ssm_branch_bwd", [(dys, wf["w_ssm_branch"])], False, SSM_WIDTH, [BF16, BF16],
                         epilogue=glu_bwd, extras=[(glu, 0), (glu, SSM_WIDTH)])
    w_glu = wf["ssm_w_glu"]
    gfull["ssm_w_glu"] = jnp.concatenate([_mm_tn("dw_glu_a", dglu_a, ygelu), _mm_tn("dw_glu_b", dglu_b, ygelu)], axis=0)

    def gelu_bwd(acc, yv):
        _, vjp = jax.vjp(jax.nn.gelu, yv)
        return (vjp(acc)[0],)

    dy_raw, = _mm("ssm_glu_bwd", [(dglu_a, w_glu[:SSM_WIDTH]), (dglu_b, w_glu[SSM_WIDTH:])], False, SSM_WIDTH, [F32],
                  epilogue=gelu_bwd, extras=[(y_raw, 0)])
    du, g_states, dab_rows, gsmall_d = _ssm_bwd(
        "ssm_bwd", dy_raw, u, states, c_mat_t, bb_mat.T, lanes(ab_re, ab_im, -1.0), lanes(ap_re, ap_im, -1.0), d_skip)
    gsmall["ssm_d"] = gsmall_d
    dbb_acc = _mm_tn("ssm_dbb", u, g_states, bm=SSM_WIDTH)
    dc_acc = _mm_tn("ssm_dc", dy_raw, states, bm=SSM_WIDTH)
    dbb_re = _diag_blocks(dbb_acc[:, :SSM_LANES], SSM_GROUP, SSM_STATE).transpose(1, 0, 2)
    dbb_im = _diag_blocks(dbb_acc[:, SSM_LANES:], SSM_GROUP, SSM_STATE).transpose(1, 0, 2)
    gsmall["ssm_c_re"] = _diag_blocks(dc_acc[:, :SSM_LANES], SSM_GROUP, SSM_STATE)
    gsmall["ssm_c_im"] = -_diag_blocks(dc_acc[:, SSM_LANES:], SSM_GROUP, SSM_STATE)
    dab = _colsum("ssm_dab", dab_rows)
    d_ar, d_ai, d_ld, d_br, d_bi = _ssm_params_bwd(
        "ssm_params_bwd", small["ssm_a_re"], small["ssm_a_im"], small["ssm_log_dt"].reshape(SSM_GROUPS, 1),
        small["ssm_b_re"].transpose(2, 0, 1), small["ssm_b_im"].transpose(2, 0, 1),
        dab[:, :SSM_LANES].reshape(SSM_GROUPS, SSM_STATE), dab[:, SSM_LANES:].reshape(SSM_GROUPS, SSM_STATE),
        dbb_re, dbb_im)
    gsmall["ssm_a_re"], gsmall["ssm_a_im"], gsmall["ssm_log_dt"] = d_ar, d_ai, d_ld.reshape(SSM_GROUPS)
    gsmall["ssm_b_re"], gsmall["ssm_b_im"] = d_br.transpose(1, 2, 0), d_bi.transpose(1, 2, 0)

    gfull["w_attn_branch"] = _mm_tn("dw_attn_branch", dya, oa)
    doa, = _mm("attn_branch_bwd", [(dya, wf["w_attn_branch"])], False, ATTN_OUT, [F32])
    dc = _combine_bwd("attn_combine_bwd", doa, oa_f32, lse_g)
    dqkv_cols = [None] * 9
    dtable = []
    for g in range(N_GROUPS):
        dq, dk, dv, db = _attn_bwd(f"attn_bwd_{g}", qkv, dc[g], lse_g[g], dc[3 + g], g, bias4[g])
        dqkv_cols[g], dqkv_cols[3 + g], dqkv_cols[6 + g] = dq, dk, dv
        dt = _bias_bwd(f"rel_bias_bwd_{g}", tables[g], db.reshape(HEADS_PER_GROUP, -1, db.shape[-1]))
        dtable.append(dt[:, :HEADS_PER_GROUP])
    gsmall["rel_bias_table"] = jnp.concatenate(dtable, axis=1)

    gfull["w_in"] = jnp.concatenate(
        [_mm_tn(f"dw_in_qkv_{i}", c, hmix) for i, c in enumerate(dqkv_cols)]
        + [_mm_tn("dw_in_u", du, hmix), _mm_tn("dw_in_ga", dzga, hmix), _mm_tn("dw_in_gs", dzgs, hmix)], axis=0)
    sent = send_grads("mix", gfull)
    qkv_pairs = [(c, w_qkv[i * ATTN_OUT:(i + 1) * ATTN_OUT]) for i, c in enumerate(dqkv_cols)]
    dhmix, = _mm("in_bwd", qkv_pairs + [(du, w_u), (dzga, w_g[:D_MODEL]), (dzgs, w_g[D_MODEL:])], False, D_MODEL,
                 [F32], tm=512, deps=sent)
    dx1, gsmall["mix_norm"] = _rms_bwd("mix_norm_bwd", dhmix, x1, small["mix_norm"], dx2)

    dx, dgg1, duu1, act1, gsmall["ffn1_norm"] = _ffn_bwd(
        "ffn1_bwd", dx1, xs, small["ffn1_norm"], gg1, uu1, wf["ffn1_w_gate"], wf["ffn1_w_up"], wf["ffn1_w_down"])
    sent = send_grads("small", gsmall)
    gfull["ffn1_w_down"] = _mm_tn("ffn1_dwd", act1, dx1, scale=0.5, deps=sent)
    sent = send_grads("f1d", gfull)
    gfull["ffn1_w_gate"] = _mm_tn("ffn1_dwg", dgg1, h1, deps=sent)
    gfull["ffn1_w_up"] = _mm_tn("ffn1_dwu", duu1, h1)
    send_grads("f1gu", gfull)
    return loss[0, 0], dx, gsmall


def kernel(x, ffn1_norm, ffn1_w_gate, ffn1_w_up, ffn1_w_down, mix_norm, w_in, gate_bias, rel_bias_table, ssm_a_re, ssm_a_im, ssm_log_dt, ssm_b_re, ssm_b_im, ssm_c_re, ssm_c_im, ssm_d, ssm_w_glu, w_attn_branch, w_ssm_branch, w_out, ffn2_norm, ffn2_w_gate, ffn2_w_up, ffn2_w_down, final_norm, loss_target, m_ffn1_norm, m_ffn1_w_gate, m_ffn1_w_up, m_ffn1_w_down, m_mix_norm, m_w_in, m_gate_bias, m_rel_bias_table, m_ssm_a_re, m_ssm_a_im, m_ssm_log_dt, m_ssm_b_re, m_ssm_b_im, m_ssm_c_re, m_ssm_c_im, m_ssm_d, m_ssm_w_glu, m_w_attn_branch, m_w_ssm_branch, m_w_out, m_ffn2_norm, m_ffn2_w_gate, m_ffn2_w_up, m_ffn2_w_down, m_final_norm, v_ffn1_norm, v_ffn1_w_gate, v_ffn1_w_up, v_ffn1_w_down, v_mix_norm, v_w_in, v_gate_bias, v_rel_bias_table, v_ssm_a_re, v_ssm_a_im, v_ssm_log_dt, v_ssm_b_re, v_ssm_b_im, v_ssm_c_re, v_ssm_c_im, v_ssm_d, v_ssm_w_glu, v_w_attn_branch, v_w_ssm_branch, v_w_out, v_ffn2_norm, v_ffn2_w_gate, v_ffn2_w_up, v_ffn2_w_down, v_final_norm):
    given = dict(locals())
    shapes = {nm: given[nm].shape for nm in _ORDER}

    def strip(a):
        return a[0] if a.ndim >= 2 and a.shape[0] == 1 else a

    w = {nm: strip(given[nm]) for nm in _ORDER}
    m = {nm: strip(given["m_" + nm]) for nm in _ORDER}
    v = {nm: strip(given["v_" + nm]) for nm in _ORDER}
    for d in (w, m, v):
        d["rel_bias_table"] = d["rel_bias_table"].reshape(N_BUCKETS, N_GROUPS * HEADS_PER_GROUP)

    small = {nm: w[nm] for nm in _SMALL}
    small_in = dict(small)
    for nm in ("ffn1_norm", "mix_norm", "ffn2_norm", "gate_bias"):
        small_in[nm] = small[nm].reshape(1, -1)
    w_pack = {ph: _pack_sharded(w, names) for ph, names in _PHASES.items()}

    f1_names = _PHASES["f1gu"] + _PHASES["f1d"]
    got_f1 = _all_gather("gather_f1", jnp.concatenate([w_pack["f1gu"], w_pack["f1d"]], axis=0).astype(BF16))
    pending_w = {"mix": _exchange_start("gather_mix_start", w_pack["mix"].astype(BF16), gather=True, deps=[got_f1])}
    pending_w["f2"] = _exchange_start("gather_f2_start", w_pack["f2"].astype(BF16), gather=True,
                                      deps=[pending_w["mix"][4]])

    def weights_of(phase, after):
        if phase == "f1":
            return _unpack_gathered(got_f1, f1_names)
        return _unpack_gathered(_exchange_wait(f"gather_{phase}_wait", pending_w[phase], after, gather=True),
                                _PHASES[phase])

    pending_g = {}

    def send_grads(phase, grads):
        if phase == "small":
            gs_pack = _pack_small({nm: grads[nm].reshape(small[nm].shape) for nm in _SMALL})
            pending_g[phase] = _exchange_start("gather_small_start", gs_pack, gather=True)
        else:
            pending_g[phase] = _exchange_start(f"scatter_{phase}_start", _pack_grads(grads, _PHASES[phase]),
                                               gather=False)
        return [pending_g[phase][4]]

    loss, dx, gsmall = _local_step(_residue_order(x[0]), _residue_order(loss_target[0]), small_in, weights_of,
                                   send_grads, first_deps=[pending_w["f2"][4]])
    dx = _token_order(dx)

    packs = {}
    after = pending_g["f1gu"][4]
    for phase in ("f2", "mix", "f1d", "small", "f1gu"):
        if phase == "small":
            gs_all = _exchange_wait("gather_small_wait", pending_g[phase], after, gather=True)
            sm = _adamw("adamw_small", _pack_small(small), _pack_small({nm: m[nm] for nm in _SMALL}),
                        _pack_small({nm: v[nm] for nm in _SMALL}), gs_all, gs_all.shape[1])
            after = sm[0]
            continue
        names = _PHASES[phase]
        recv = _exchange_wait(f"scatter_{phase}_wait", pending_g[phase], after, gather=False)
        rows_p = w_pack[phase].shape[0]
        tr = max(t for t in range(16, 129, 16) if rows_p % t == 0)
        packs[phase] = _adamw(f"adamw_{phase}", w_pack[phase], _pack_sharded(m, names), _pack_sharded(v, names),
                              recv, tr)
        after = packs[phase][0]

    loss = lax.psum(loss, ("x", "y", "c"))
    outs = []
    for i in range(4):
        big = {}
        for phase, names in _PHASES.items():
            big.update(_unpack_sharded(packs[phase][i], names))
        sml = _unpack_small(sm[i], small)
        outs.append([(big[nm] if nm in big else sml[nm]).reshape(shapes[nm]) for nm in _ORDER])
    return (loss, dx[None], *outs[0], *outs[1], *outs[2], *outs[3])
```

```python
import math

import numpy as np
import jax
import jax.numpy as jnp
from jax import lax
from jax.experimental import pallas as pl
from jax.experimental.pallas import tpu as pltpu

F32 = jnp.float32
BF16 = jnp.bfloat16

N_DEV = 8
D_MODEL = 1024
D_FF = 2816
HEAD_DIM = 64
HEADS_PER_GROUP = 4
DILATIONS = (1, 4, 16)
N_GROUPS = 3
ATTN_WIDTH = 768
ATTN_OUT = 256
BLOCK = 128
N_BUCKETS = 32
MAX_DISTANCE = 2048
NEG_INF = -1e30
SSM_WIDTH = 512
SSM_GROUPS = 32
SSM_GROUP = 16
SSM_STATE = 64
SSM_LANES = SSM_GROUPS * SSM_STATE
EPS = 1e-6
LR, B1, B2, ADAM_EPS, WD, STEP = 0.001, 0.9, 0.999, 1e-08, 0.01, 10

VMEM_LIMIT_BYTES = 56 * 1024 * 1024
FFN_CHUNK = D_FF // 2
SCAN_BLOCK = 256
SCAN_STEPS = 16
SCAN_COLS = SCAN_BLOCK // SCAN_STEPS
SCAN_SUB = 8
SCAN_LANES = 512

MESH = pl.DeviceIdType.MESH


def _cparams(*sem):
    return pltpu.CompilerParams(dimension_semantics=sem, vmem_limit_bytes=VMEM_LIMIT_BYTES)


def _dot(a, b, dims):
    return lax.dot_general(a, b, (dims, ((), ())), preferred_element_type=F32)


def _dot_nn(a, b):
    return _dot(a, b, ((1,), (0,)))


def _dot_nt(a, b):
    return _dot(a, b, ((1,), (1,)))


def _dot_tn(a, b):
    return _dot(a, b, ((0,), (0,)))


def _sigmoid(x):
    return 1.0 / (1.0 + jnp.exp(-x))


def _all_gather(name, xs):
    rows, cols = xs.shape

    def body(x_ref, out_ref, send_sems, recv_sems, local_sem):
        x, y, c = lax.axis_index("x"), lax.axis_index("y"), lax.axis_index("c")
        me, sibling = (x, y, c), (x, y, 1 - c)
        chips = [(1 - x, y), (x, 1 - y), (1 - x, 1 - y)]

        def slot(px, py, pc):
            return out_ref.at[4 * px + 2 * py + pc]

        def copy(k, block, to, src=None):
            return pltpu.make_async_remote_copy(
                src_ref=slot(*block) if src is None else src, dst_ref=slot(*block),
                send_sem=send_sems.at[k], recv_sem=recv_sems.at[k], device_id=to, device_id_type=MESH)

        mine = pltpu.make_async_copy(x_ref, slot(*me), local_sem)
        mine.start()
        first = [copy(0, me, sibling, src=x_ref)]
        first += [copy(1 + j, me, (*chip, c), src=x_ref) for j, chip in enumerate(chips)]
        for cp in first:
            cp.start()
        passed = [copy(4 + j, (*chip, c), sibling) for j, chip in enumerate(chips)]
        for j, chip in enumerate(chips):
            copy(1 + j, (*chip, c), me).wait_recv()
            passed[j].start()
        copy(0, sibling, me).wait_recv()
        for j, chip in enumerate(chips):
            copy(4 + j, (*chip, 1 - c), me).wait_recv()
        for cp in first + passed:
            cp.wait_send()
        mine.wait()

    return pl.pallas_call(
        body, name=name,
        out_shape=jax.ShapeDtypeStruct((N_DEV, rows, cols), xs.dtype),
        in_specs=[pl.BlockSpec(memory_space=pl.ANY)],
        out_specs=pl.BlockSpec(memory_space=pl.ANY),
        scratch_shapes=[pltpu.SemaphoreType.DMA((7,)), pltpu.SemaphoreType.DMA((7,)), pltpu.SemaphoreType.DMA],
    )(xs)


_HBM_SPEC = pl.BlockSpec(memory_space=pltpu.HBM)
_SEM_SPEC = pl.BlockSpec(memory_space=pltpu.SEMAPHORE)
_ANY_SPEC = pl.BlockSpec(memory_space=pl.ANY)
_EFFECT = pltpu.SideEffectType.DATAFLOW_SIDE_EFFECTING


def _peers(x, y, c):
    return [(1 - x if k & 4 else x, 1 - y if k & 2 else y, 1 - c if k & 1 else c) for k in range(1, N_DEV)]


def _exchange_copies(x_ref, land_ref, send_sems, recv_sems, gather):
    x, y, c = lax.axis_index("x"), lax.axis_index("y"), lax.axis_index("c")
    me = 4 * x + 2 * y + c
    copies = []
    for k, (px, py, pc) in enumerate(_peers(x, y, c)):
        src = x_ref if gather else x_ref.at[4 * px + 2 * py + pc]
        copies.append(pltpu.make_async_remote_copy(
            src_ref=src, dst_ref=land_ref.at[me], send_sem=send_sems.at[k], recv_sem=recv_sems.at[k],
            device_id=(px, py, pc), device_id_type=MESH))
    own = pltpu.make_async_copy(x_ref if gather else x_ref.at[me], land_ref.at[me], send_sems.at[N_DEV - 1])
    return own, copies


def _exchange_start(name, xs, gather, deps=()):
    land_shape = (N_DEV, *xs.shape) if gather else xs.shape
    nd = len(deps)

    def body(x_ref, land_ref, *rest):
        send_sems, recv_sems, _, _, token = rest[nd:]
        own, copies = _exchange_copies(x_ref, land_ref, send_sems, recv_sems, gather)
        for cp in copies:
            cp.start()
        own.start()
        token[...] = jnp.zeros_like(token)

    return pl.pallas_call(
        body, name=name,
        out_shape=(pltpu.SemaphoreType.DMA((N_DEV,)), pltpu.SemaphoreType.DMA((N_DEV - 1,)),
                   pltpu.HBM(xs.shape, xs.dtype), pltpu.HBM(land_shape, xs.dtype), jax.ShapeDtypeStruct((8, 128), F32)),
        in_specs=(_HBM_SPEC, _HBM_SPEC) + (_ANY_SPEC,) * nd,
        out_specs=(_SEM_SPEC, _SEM_SPEC, _HBM_SPEC, _HBM_SPEC, pl.BlockSpec(memory_space=pltpu.VMEM)),
        input_output_aliases={0: 2, 1: 3},
        compiler_params=pltpu.CompilerParams(has_side_effects=_EFFECT),
    )(pltpu.with_memory_space_constraint(xs, pltpu.HBM),
      pltpu.with_memory_space_constraint(lax.empty(land_shape, xs.dtype), pltpu.HBM), *deps)


def _exchange_wait(name, handle, after, gather):
    send_sems, recv_sems, x_thru, land_thru, _ = handle

    def body(x_ref, land_ref, send_sems, recv_sems, after_ref, x_dead, got_ref):
        own, copies = _exchange_copies(x_ref, land_ref, send_sems, recv_sems, gather)
        for cp in copies:
            cp.wait_send()
            cp.wait_recv()
        own.wait()

    return pl.pallas_call(
        body, name=name,
        out_shape=(pltpu.HBM(x_thru.shape, x_thru.dtype), pltpu.HBM(land_thru.shape, land_thru.dtype)),
        in_specs=(_HBM_SPEC, _HBM_SPEC, _SEM_SPEC, _SEM_SPEC, _ANY_SPEC),
        out_specs=(_HBM_SPEC, _HBM_SPEC), input_output_aliases={0: 0, 1: 1},
        compiler_params=pltpu.CompilerParams(has_side_effects=_EFFECT),
    )(x_thru, land_thru, send_sems, recv_sems, after)[1]


def _mm(name, pairs, nt, n_cols, out_dtypes, epilogue=None, extras=(), tm=1024, tn=512, deps=()):
    rows = pairs[0][0].shape[0]
    tm = min(tm, rows)
    tn = min(tn, n_cols)
    na, ne, nd = len(pairs), len(extras), len(deps)

    def body(*refs):
        a_refs, w_refs = refs[:na], refs[na:2 * na]
        e_refs, o_refs = refs[2 * na:2 * na + ne], refs[2 * na + ne + nd:]
        acc = None
        for a_ref, w_ref in zip(a_refs, w_refs):
            a = a_ref[...].astype(BF16)
            w = w_ref[...].astype(BF16)
            p = _dot_nt(a, w) if nt else _dot_nn(a, w)
            acc = p if acc is None else acc + p
        outs = (acc,) if epilogue is None else epilogue(acc, *[e[...] for e in e_refs])
        for o_ref, o in zip(o_refs, outs):
            o_ref[...] = o.astype(o_ref.dtype)

    in_specs = [pl.BlockSpec((tm, a.shape[1]), lambda i, j: (i, 0)) for a, _ in pairs]
    for _, w in pairs:
        if nt:
            in_specs.append(pl.BlockSpec((tn, w.shape[1]), lambda i, j: (j, 0)))
        else:
            in_specs.append(pl.BlockSpec((w.shape[0], tn), lambda i, j: (0, j)))
    for e, col_off in extras:
        off = col_off // tn
        if e.shape[0] == 1:
            in_specs.append(pl.BlockSpec((1, tn), lambda i, j, off=off: (0, j + off)))
        else:
            in_specs.append(pl.BlockSpec((tm, tn), lambda i, j, off=off: (i, j + off)))
    in_specs += [_ANY_SPEC] * nd
    out_specs = [pl.BlockSpec((tm, tn), lambda i, j: (i, j)) for _ in out_dtypes]
    outs = pl.pallas_call(
        body, name=name, grid=(rows // tm, n_cols // tn),
        in_specs=in_specs, out_specs=out_specs,
        out_shape=[jax.ShapeDtypeStruct((rows, n_cols), dt) for dt in out_dtypes],
        compiler_params=_cparams("parallel", "arbitrary"),
    )(*[a for a, _ in pairs], *[w for _, w in pairs], *[e for e, _ in extras], *deps)
    return outs


def _tn_rows(m):
    return max(b for b in range(128, min(m, 1408) + 1, 128) if m % b == 0)


def _mm_tn(name, a, b, scale=1.0, bm=None, tk=1024, deps=()):
    rows, m = a.shape
    n = b.shape[1]
    bm = _tn_rows(m) if bm is None else bm
    tk = min(tk, rows)
    nk = rows // tk

    def body(a_ref, b_ref, *rest):
        o_ref = rest[-1]
        k = pl.program_id(1)

        @pl.when(k == 0)
        def _():
            o_ref[...] = jnp.zeros_like(o_ref)

        o_ref[...] += _dot_tn(a_ref[...].astype(BF16), b_ref[...].astype(BF16))
        if scale != 1.0:
            @pl.when(k == nk - 1)
            def _():
                o_ref[...] = o_ref[...] * scale

    return pl.pallas_call(
        body, name=name, grid=(m // bm, nk),
        in_specs=[pl.BlockSpec((tk, bm), lambda i, k: (k, i)), pl.BlockSpec((tk, n), lambda i, k: (k, 0))]
        + [_ANY_SPEC] * len(deps),
        out_specs=pl.BlockSpec((bm, n), lambda i, k: (i, 0)),
        out_shape=jax.ShapeDtypeStruct((m, n), F32),
        compiler_params=_cparams("parallel", "arbitrary"),
    )(a, b, *deps)


def _mm_tn_stack(name, a_list, b, tk=1024):
    rows, n = b.shape
    ms = [a.shape[1] for a in a_list]
    tk = min(tk, rows)
    na = len(a_list)

    def body(*refs):
        a_refs, b_ref, o_ref = refs[:na], refs[na], refs[na + 1]

        @pl.when(pl.program_id(0) == 0)
        def _():
            o_ref[...] = jnp.zeros_like(o_ref)

        bv = b_ref[...].astype(BF16)
        r0 = 0
        for a_ref, m in zip(a_refs, ms):
            o_ref[r0:r0 + m, :] += _dot_tn(a_ref[...].astype(BF16), bv)
            r0 += m

    return pl.pallas_call(
        body, name=name, grid=(rows // tk,),
        in_specs=[pl.BlockSpec((tk, m), lambda k: (k, 0)) for m in ms] + [pl.BlockSpec((tk, n), lambda k: (k, 0))],
        out_specs=pl.BlockSpec((sum(ms), n), lambda k: (0, 0)),
        out_shape=jax.ShapeDtypeStruct((sum(ms), n), F32),
        compiler_params=_cparams("arbitrary"),
    )(*a_list, b)


def _colsum(name, xs, tm=512):
    rows, cols = xs.shape
    tm = min(tm, rows)

    def body(x_ref, o_ref):
        @pl.when(pl.program_id(0) == 0)
        def _():
            o_ref[...] = jnp.zeros_like(o_ref)

        o_ref[...] += jnp.sum(x_ref[...].astype(F32), axis=0, keepdims=True)

    return pl.pallas_call(
        body, name=name, grid=(rows // tm,),
        in_specs=[pl.BlockSpec((tm, cols), lambda i: (i, 0))],
        out_specs=pl.BlockSpec((1, cols), lambda i: (0, 0)),
        out_shape=jax.ShapeDtypeStruct((1, cols), F32),
        compiler_params=_cparams("arbitrary"),
    )(xs)


def _ew(name, fn, ins, out_cols, out_dtypes, tm=512):
    rows = ins[0].shape[0]
    tm = min(tm, rows)
    ni = len(ins)

    def body(*refs):
        outs = fn(*[r[...] for r in refs[:ni]])
        for o_ref, o in zip(refs[ni:], outs):
            o_ref[...] = o.astype(o_ref.dtype)

    def spec(shape):
        if shape[0] == 1:
            return pl.BlockSpec((1, shape[1]), lambda i: (0, 0))
        return pl.BlockSpec((tm, shape[1]), lambda i: (i, 0))

    return pl.pallas_call(
        body, name=name, grid=(rows // tm,),
        in_specs=[spec(a.shape) for a in ins],
        out_specs=[pl.BlockSpec((tm, c), lambda i: (i, 0)) for c in out_cols],
        out_shape=[jax.ShapeDtypeStruct((rows, c), dt) for c, dt in zip(out_cols, out_dtypes)],
        compiler_params=_cparams("parallel"),
    )(*ins)


def _rms_parts(xv):
    r = lax.rsqrt(jnp.mean(xv * xv, axis=-1, keepdims=True) + EPS)
    return r, xv * r


def _rms_bwd_dx(dh, gain, r, xh):
    dxh = dh * gain
    return r * (dxh - xh * jnp.mean(dxh * xh, axis=-1, keepdims=True))


def _rms_fwd(name, xs, gain):
    def fn(xv, g):
        _, xh = _rms_parts(xv)
        return (xh * g,)

    return _ew(name, fn, [xs, gain], [xs.shape[1]], [BF16])[0]


def _rms_bwd(name, dh, xs, gain, dres, tm=512):
    rows, d = xs.shape
    tm = min(tm, rows)

    def body(dh_ref, x_ref, g_ref, dres_ref, dx_ref, dg_ref):
        r, xh = _rms_parts(x_ref[...])
        dhv = dh_ref[...]
        dx_ref[...] = dres_ref[...] + _rms_bwd_dx(dhv, g_ref[...], r, xh)

        @pl.when(pl.program_id(0) == 0)
        def _():
            dg_ref[...] = jnp.zeros_like(dg_ref)

        dg_ref[...] += jnp.sum(dhv * xh, axis=0, keepdims=True)

    tile = pl.BlockSpec((tm, d), lambda i: (i, 0))
    row = pl.BlockSpec((1, d), lambda i: (0, 0))
    return pl.pallas_call(
        body, name=name, grid=(rows // tm,),
        in_specs=[tile, tile, row, tile], out_specs=[tile, row],
        out_shape=[jax.ShapeDtypeStruct((rows, d), F32), jax.ShapeDtypeStruct((1, d), F32)],
        compiler_params=_cparams("arbitrary"),
    )(dh, xs, gain, dres)


def _ffn_chunks(f_all):
    return [slice(c, c + FFN_CHUNK) for c in range(0, f_all, FFN_CHUNK)]


def _ffn_fwd(name, xs, gain, wg_t, wu_t, wd, tm=512, deps=()):
    rows, d = xs.shape
    f_all = wd.shape[0]
    tm = min(tm, rows)

    def body(x_ref, g_ref, wg_ref, wu_ref, wd_ref, *rest):
        xo_ref, h_ref, gg_ref, uu_ref = rest[-4:]
        xv = x_ref[...]
        _, xh = _rms_parts(xv)
        h = (xh * g_ref[...]).astype(BF16)
        h_ref[...] = h
        acc = None
        for cols in _ffn_chunks(f_all):
            gg = _dot_nt(h, wg_ref[cols, :])
            uu = _dot_nt(h, wu_ref[cols, :])
            act = gg * _sigmoid(gg) * uu
            part = _dot_nn(act.astype(BF16), wd_ref[cols, :])
            acc = part if acc is None else acc + part
            gg_ref[:, cols] = gg.astype(BF16)
            uu_ref[:, cols] = uu.astype(BF16)
        xo_ref[...] = xv + 0.5 * acc

    tile = pl.BlockSpec((tm, d), lambda i: (i, 0))
    wspec = pl.BlockSpec((f_all, d), lambda i: (0, 0), pipeline_mode=pl.Buffered(1))
    hid = pl.BlockSpec((tm, f_all), lambda i: (i, 0))
    return pl.pallas_call(
        body, name=name, grid=(rows // tm,),
        in_specs=[tile, pl.BlockSpec((1, d), lambda i: (0, 0)), wspec, wspec, wspec] + [_ANY_SPEC] * len(deps),
        out_specs=[tile, tile, hid, hid],
        out_shape=[jax.ShapeDtypeStruct((rows, d), F32), jax.ShapeDtypeStruct((rows, d), BF16),
                   jax.ShapeDtypeStruct((rows, f_all), BF16), jax.ShapeDtypeStruct((rows, f_all), BF16)],
        compiler_params=_cparams("parallel"),
    )(xs, gain, wg_t, wu_t, wd, *deps)


def _ffn_bwd(name, dxo, xs, gain, gg_all, uu_all, wg_t, wu_t, wd, tm=256):
    rows, d = xs.shape
    f_all = wd.shape[0]
    tm = min(tm, rows)

    def body(dxo_ref, x_ref, g_ref, gg_ref, uu_ref, wg_ref, wu_ref, wd_ref,
             dx_ref, dgg_ref, duu_ref, act_ref, dgain_ref):
        dxo = dxo_ref[...]
        df = (0.5 * dxo).astype(BF16)
        dh = None
        for cols in _ffn_chunks(f_all):
            gg = gg_ref[:, cols].astype(F32)
            uu = uu_ref[:, cols].astype(F32)
            sg = _sigmoid(gg)
            silu = gg * sg
            dact = _dot_nt(df, wd_ref[cols, :])
            duu = (dact * silu).astype(BF16)
            dgg = (dact * uu * (sg * (1.0 + gg * (1.0 - sg)))).astype(BF16)
            act_ref[:, cols] = (silu * uu).astype(BF16)
            dgg_ref[:, cols] = dgg
            duu_ref[:, cols] = duu
            part = _dot_nn(dgg, wg_ref[cols, :]) + _dot_nn(duu, wu_ref[cols, :])
            dh = part if dh is None else dh + part
        r, xh = _rms_parts(x_ref[...])
        dx_ref[...] = dxo + _rms_bwd_dx(dh, g_ref[...], r, xh)

        @pl.when(pl.program_id(0) == 0)
        def _():
            dgain_ref[...] = jnp.zeros_like(dgain_ref)

        dgain_ref[...] += jnp.sum(dh * xh, axis=0, keepdims=True)

    tile = pl.BlockSpec((tm, d), lambda i: (i, 0))
    row = pl.BlockSpec((1, d), lambda i: (0, 0))
    wspec = pl.BlockSpec((f_all, d), lambda i: (0, 0), pipeline_mode=pl.Buffered(1))
    hid = pl.BlockSpec((tm, f_all), lambda i: (i, 0))
    hid_shape = jax.ShapeDtypeStruct((rows, f_all), BF16)
    return pl.pallas_call(
        body, name=name, grid=(rows // tm,),
        in_specs=[tile, tile, row, hid, hid, wspec, wspec, wspec],
        out_specs=[tile, hid, hid, hid, row],
        out_shape=[jax.ShapeDtypeStruct((rows, d), F32), hid_shape, hid_shape, hid_shape,
                   jax.ShapeDtypeStruct((1, d), F32)],
        compiler_params=_cparams("arbitrary"),
    )(dxo, xs, gain, gg_all, uu_all, wg_t, wu_t, wd)


def _final_loss(name, xs, gain, target, tm=512):
    rows, d = xs.shape
    tm = min(tm, rows)

    def body(x_ref, g_ref, t_ref, dx_ref, dg_ref, loss_ref):
        r, xh = _rms_parts(x_ref[...])
        gain_v = g_ref[...]
        err = xh * gain_v - t_ref[...]
        dy = err * (1.0 / d)
        dx_ref[...] = _rms_bwd_dx(dy, gain_v, r, xh)

        @pl.when(pl.program_id(0) == 0)
        def _():
            dg_ref[...] = jnp.zeros_like(dg_ref)
            loss_ref[...] = jnp.zeros_like(loss_ref)

        dg_ref[...] += jnp.sum(dy * xh, axis=0, keepdims=True)
        per_tok = jnp.mean(err * err, axis=-1, keepdims=True)
        loss_ref[...] += 0.5 * jnp.sum(per_tok, axis=0, keepdims=True)

    tile = pl.BlockSpec((tm, d), lambda i: (i, 0))
    row = pl.BlockSpec((1, d), lambda i: (0, 0))
    return pl.pallas_call(
        body, name=name, grid=(rows // tm,),
        in_specs=[tile, row, tile],
        out_specs=[tile, row, pl.BlockSpec((1, 1), lambda i: (0, 0))],
        out_shape=[jax.ShapeDtypeStruct((rows, d), F32), jax.ShapeDtypeStruct((1, d), F32),
                   jax.ShapeDtypeStruct((1, 1), F32)],
        compiler_params=_cparams("arbitrary"),
    )(xs, gain, target)


def _t5_bucket_np(dist):
    max_exact = N_BUCKETS // 2
    dd = np.maximum(dist, 1).astype(np.float32)
    large = max_exact + (np.log(dd / np.float32(max_exact)) / np.float32(math.log(MAX_DISTANCE / max_exact))
                         * np.float32(N_BUCKETS - max_exact)).astype(np.int32)
    large = np.minimum(large, N_BUCKETS - 1)
    return np.where(dist < max_exact, dist, large).astype(np.int32)


def _attn_geometry(g, rows):
    run = rows // 16
    dil = DILATIONS[g]
    if dil == 16:
        bq = BLOCK
        return dict(view=(16, run), block=(None, bq), grid=(16, run // bq), index=lambda r, n: (r, n),
                    pos=np.arange(bq), bq=bq)
    if dil == 4:
        per = BLOCK // 4
        pos = (4 * np.arange(per)[None, :] + np.arange(4)[:, None]).reshape(-1)
        return dict(view=(4, 4, run), block=(4, None, per), grid=(4, run // per), index=lambda r, n: (0, r, n),
                    pos=pos, bq=BLOCK)
    per = 16
    pos = (16 * np.arange(per)[None, :] + np.arange(16)[:, None]).reshape(-1)
    return dict(view=(16, run), block=(16, per), grid=(1, run // per), index=lambda r, n: (0, n),
                pos=pos, bq=16 * per)


def _attn_tables(g, rows):
    geo = _attn_geometry(g, rows)
    pos, bq = geo["pos"], geo["bq"]
    steps = pos[:, None] - np.concatenate([pos - bq, pos])[None, :]
    valid = (steps >= 0) & (steps <= BLOCK)
    bucket = _t5_bucket_np((np.maximum(steps, 0) * DILATIONS[g]).astype(np.int32))
    return bucket, valid.astype(np.int32)


def _bias_fwd(name, bucket, valid, table_t):
    bq = bucket.shape[0]

    def body(bk_ref, ok_ref, tab_ref, o_ref):
        bk = bk_ref[...]
        ok = ok_ref[...] > 0
        for h in range(HEADS_PER_GROUP):
            acc = jnp.zeros(bk.shape, F32)
            for b in range(N_BUCKETS):
                acc = jnp.where(bk == b, tab_ref[h, b], acc)
            o_ref[h] = jnp.where(ok, acc, NEG_INF)

    vm = pl.BlockSpec(memory_space=pltpu.VMEM)
    return pl.pallas_call(
        body, name=name, in_specs=[vm, vm, pl.BlockSpec(memory_space=pltpu.SMEM)], out_specs=vm,
        out_shape=jax.ShapeDtypeStruct((HEADS_PER_GROUP, bq, 2 * bq), F32),
    )(bucket, valid, table_t)


def _bias_bwd(name, bucket, dbias):
    def body(bk_ref, db_ref, o_ref):
        row_id = lax.broadcasted_iota(jnp.int32, (N_BUCKETS, 128), 0)
        col_id = lax.broadcasted_iota(jnp.int32, (N_BUCKETS, 128), 1)
        bk = bk_ref[...]
        acc = jnp.zeros((N_BUCKETS, 128), F32)
        for h in range(HEADS_PER_GROUP):
            db = db_ref[h]
            for b in range(N_BUCKETS):
                part = jnp.sum(jnp.where(bk == b, db, 0.0), axis=0, keepdims=True)
                tot = jnp.sum(part, axis=1, keepdims=True)
                acc = jnp.where((row_id == b) & (col_id == h), tot, acc)
        o_ref[...] = acc

    vm = pl.BlockSpec(memory_space=pltpu.VMEM)
    return pl.pallas_call(body, name=name, in_specs=[vm, vm], out_specs=vm,
                          out_shape=jax.ShapeDtypeStruct((N_BUCKETS, 128), F32))(bucket, dbias)


def _head_of_lane(nrows):
    return lax.broadcasted_iota(jnp.int32, (nrows, ATTN_OUT), 1) // HEAD_DIM


def _stack_heads(a, lane_head):
    zero = jnp.zeros_like(a)
    return jnp.concatenate([jnp.where(lane_head == h, a, zero) for h in range(HEADS_PER_GROUP)], axis=0)


def _unstack_heads(a4, lane_head, bq):
    out = a4[:bq]
    for h in range(1, HEADS_PER_GROUP):
        out = jnp.where(lane_head == h, a4[h * bq:(h + 1) * bq], out)
    return out


def _attn_specs(geo, cols, col_block, index):
    return pl.BlockSpec(geo["block"] + (cols,), lambda r, n: index(r, n) + (col_block,))


def _attn_fwd(name, qkv, g, bias4):
    rows = qkv.shape[0]
    geo = _attn_geometry(g, rows)
    bq, (nsub, nb), index = geo["bq"], geo["grid"], geo["index"]
    blk_shape = tuple(b for b in geo["block"] if b is not None) + (ATTN_OUT,)

    def body(q_ref, kc_ref, kp_ref, vc_ref, vp_ref, b_ref, o_ref, lse_ref):
        n = pl.program_id(1)
        lane_head = _head_of_lane(bq)
        flat = lambda ref: ref[...].reshape(bq, ATTN_OUT)
        q4 = _stack_heads(flat(q_ref), lane_head)
        k2 = jnp.concatenate([flat(kp_ref), flat(kc_ref)], axis=0)
        v2 = jnp.concatenate([flat(vp_ref), flat(vc_ref)], axis=0)
        s = _dot_nt(q4, k2) + b_ref[...]
        col = lax.broadcasted_iota(jnp.int32, s.shape, 1)
        s = jnp.where((col >= bq) | (n > 0), s, NEG_INF)
        mx = jnp.max(s, axis=-1, keepdims=True)
        p = jnp.exp(s - mx)
        den = jnp.sum(p, axis=-1, keepdims=True)
        o4 = _dot_nn(p.astype(BF16), v2) / den
        lse4 = jnp.broadcast_to(mx + jnp.log(den), (HEADS_PER_GROUP * bq, ATTN_OUT))
        o_ref[...] = _unstack_heads(o4, lane_head, bq).reshape(blk_shape)
        lse_ref[...] = _unstack_heads(lse4, lane_head, bq).reshape(blk_shape)

    prev = lambda r, n: index(r, jnp.maximum(n - 1, 0))
    view = lambda a: a.reshape(geo["view"] + (a.shape[1],))
    qkv_v = view(qkv)
    out_spec = _attn_specs(geo, ATTN_OUT, 0, index)
    out_shape = jax.ShapeDtypeStruct(geo["view"] + (ATTN_OUT,), F32)
    o, lse = pl.pallas_call(
        body, name=name, grid=(nsub, nb),
        in_specs=[_attn_specs(geo, ATTN_OUT, g, index), _attn_specs(geo, ATTN_OUT, 3 + g, index),
                  _attn_specs(geo, ATTN_OUT, 3 + g, prev), _attn_specs(geo, ATTN_OUT, 6 + g, index),
                  _attn_specs(geo, ATTN_OUT, 6 + g, prev), pl.BlockSpec(bias4.shape, lambda r, n: (0, 0))],
        out_specs=[out_spec, out_spec], out_shape=[out_shape, out_shape],
        compiler_params=_cparams("parallel", "arbitrary"),
    )(qkv_v, qkv_v, qkv_v, qkv_v, qkv_v, bias4)
    return o.reshape(rows, ATTN_OUT), lse.reshape(rows, ATTN_OUT)


def _attn_bwd(name, qkv, do, lse, cvec, g, bias4):
    rows = qkv.shape[0]
    geo = _attn_geometry(g, rows)
    bq, (nsub, nb), index = geo["bq"], geo["grid"], geo["index"]
    blk_shape = tuple(b for b in geo["block"] if b is not None) + (ATTN_OUT,)
    nlead = len(blk_shape) - 1

    def body(q_ref, kc_ref, kp_ref, vc_ref, vp_ref, do_ref, lse_ref, c_ref, b_ref,
             dq_ref, dk_ref, dv_ref, db_ref, kcar_ref, vcar_ref):
        r, n = pl.program_id(0), pl.program_id(1)
        valid = n < nb
        lane_head = _head_of_lane(bq)
        flat = lambda ref: ref[...].reshape(bq, ATTN_OUT)

        @pl.when((r == 0) & (n == 0))
        def _():
            kcar_ref[...] = jnp.zeros_like(kcar_ref)
            vcar_ref[...] = jnp.zeros_like(vcar_ref)
            db_ref[...] = jnp.zeros_like(db_ref)

        def column(ref, h):
            lead = (slice(None),) * nlead
            return ref[lead + (pl.ds(h * HEAD_DIM, 1),)].reshape(bq, 1)

        q4 = _stack_heads(flat(q_ref), lane_head)
        do4 = _stack_heads(flat(do_ref), lane_head)
        k2 = jnp.concatenate([flat(kp_ref), flat(kc_ref)], axis=0)
        v2 = jnp.concatenate([flat(vp_ref), flat(vc_ref)], axis=0)
        lse4 = jnp.concatenate([column(lse_ref, h) for h in range(HEADS_PER_GROUP)], axis=0)
        c4 = jnp.concatenate([column(c_ref, h) for h in range(HEADS_PER_GROUP)], axis=0)
        s = _dot_nt(q4, k2) + b_ref[...]
        col = lax.broadcasted_iota(jnp.int32, s.shape, 1)
        keep = ((col >= bq) | (n > 0)) & valid
        p = jnp.where(keep, jnp.exp(s - lse4), 0.0)
        ds = p * (_dot_nt(do4, v2) + c4)
        ds_b = ds.astype(BF16)

        @pl.when(valid)
        def _():
            dq = _unstack_heads(_dot_nn(ds_b, k2), lane_head, bq) * (HEAD_DIM ** -0.5)
            dq_ref[...] = dq.astype(BF16).reshape(blk_shape)

        dk2 = _dot_tn(ds_b, q4)
        dv2 = _dot_tn(p.astype(BF16), do4)
        dk_ref[...] = (kcar_ref[...] + dk2[:bq]).astype(BF16).reshape(blk_shape)
        dv_ref[...] = (vcar_ref[...] + dv2[:bq]).astype(BF16).reshape(blk_shape)
        kcar_ref[...] = dk2[bq:]
        vcar_ref[...] = dv2[bq:]
        db_ref[...] += ds

    cur = lambda r, n: index(r, jnp.minimum(n, nb - 1))
    prev = lambda r, n: index(r, jnp.maximum(jnp.minimum(n, nb - 1) - 1, 0))
    late = lambda r, n: index(r, jnp.maximum(n - 1, 0))
    view = lambda a: a.reshape(geo["view"] + (a.shape[1],))
    qkv_v = view(qkv)
    tile = _attn_specs(geo, ATTN_OUT, 0, cur)
    bias_spec = pl.BlockSpec(bias4.shape, lambda r, n: (0, 0))
    out_shape = jax.ShapeDtypeStruct(geo["view"] + (ATTN_OUT,), BF16)
    dq, dk, dv, db = pl.pallas_call(
        body, name=name, grid=(nsub, nb + 1),
        in_specs=[_attn_specs(geo, ATTN_OUT, g, cur), _attn_specs(geo, ATTN_OUT, 3 + g, cur),
                  _attn_specs(geo, ATTN_OUT, 3 + g, prev), _attn_specs(geo, ATTN_OUT, 6 + g, cur),
                  _attn_specs(geo, ATTN_OUT, 6 + g, prev), tile, tile, tile, bias_spec],
        out_specs=[tile, _attn_specs(geo, ATTN_OUT, 0, late), _attn_specs(geo, ATTN_OUT, 0, late), bias_spec],
        out_shape=[out_shape, out_shape, out_shape, jax.ShapeDtypeStruct(bias4.shape, F32)],
        scratch_shapes=[pltpu.VMEM((bq, ATTN_OUT), F32), pltpu.VMEM((bq, ATTN_OUT), F32)],
        compiler_params=_cparams("arbitrary", "arbitrary"),
    )(qkv_v, qkv_v, qkv_v, qkv_v, qkv_v, view(do), view(lse), view(cvec), bias4)
    return dq.reshape(rows, ATTN_OUT), dk.reshape(rows, ATTN_OUT), dv.reshape(rows, ATTN_OUT), db


def _group_weights(lses):
    mx = jnp.maximum(jnp.maximum(lses[0], lses[1]), lses[2])
    es = [jnp.exp(l - mx) for l in lses]
    den = es[0] + es[1] + es[2]
    return [e / den for e in es]


def _combine_fwd(name, os_, lses):
    def fn(o0, o1, o2, l0, l1, l2):
        ws = _group_weights([l0, l1, l2])
        out = ws[0] * o0 + ws[1] * o1 + ws[2] * o2
        return out, out

    return _ew(name, fn, [*os_, *lses], [ATTN_OUT, ATTN_OUT], [F32, BF16], tm=1024)


def _combine_bwd(name, do, oa, lses):
    def fn(dov, oav, l0, l1, l2):
        head_sum = (lax.broadcasted_iota(jnp.int32, (ATTN_OUT, ATTN_OUT), 0) // HEAD_DIM
                    == lax.broadcasted_iota(jnp.int32, (ATTN_OUT, ATTN_OUT), 1) // HEAD_DIM)
        ws = _group_weights([l0, l1, l2])
        prod = dov * oav
        hi = prod.astype(BF16)
        lo = (prod - hi.astype(F32)).astype(BF16)
        ones = jnp.where(head_sum, 1.0, 0.0).astype(BF16)
        bar = _dot_nn(hi, ones) + _dot_nn(lo, ones)
        return tuple(w * dov for w in ws) + tuple(-w * bar for w in ws)

    return _ew(name, fn, [do, oa, *lses], [ATTN_OUT] * 6, [BF16] * 3 + [F32] * 3, tm=1024)


def _ssm_disc(a_re, a_im, log_dt, b_re, b_im):
    dt = jnp.exp(log_dt)
    mag = jnp.exp(a_re * dt)
    ab_re = mag * jnp.cos(a_im * dt)
    ab_im = mag * jnp.sin(a_im * dt)
    den = a_re * a_re + a_im * a_im
    xr = ab_re - 1.0
    coef_re = (xr * a_re + ab_im * a_im) / den
    coef_im = (ab_im * a_re - xr * a_im) / den
    bb_re = coef_re[None] * b_re - coef_im[None] * b_im
    bb_im = coef_re[None] * b_im + coef_im[None] * b_re
    return ab_re, ab_im, bb_re, bb_im


def _cpow2(re, im, times):
    for _ in range(times):
        re, im = re * re - im * im, 2.0 * re * im
    return re, im


def _ssm_params_fwd(name, a_re, a_im, log_dt, b_re, b_im):
    gn = jax.ShapeDtypeStruct(a_re.shape, F32)
    cgn = jax.ShapeDtypeStruct(b_re.shape, F32)

    def body(ar, ai, ld, br, bi, o_abr, o_abi, o_apr, o_api, o_bbr, o_bbi):
        ab_re, ab_im, bb_re, bb_im = _ssm_disc(ar[...], ai[...], ld[...], br[...], bi[...])
        o_abr[...] = ab_re
        o_abi[...] = ab_im
        pr, pi = _cpow2(ab_re, ab_im, int(math.log2(SCAN_STEPS)))
        o_apr[...] = pr
        o_api[...] = pi
        o_bbr[...] = bb_re
        o_bbi[...] = bb_im

    vm = pl.BlockSpec(memory_space=pltpu.VMEM)
    return pl.pallas_call(body, name=name, in_specs=[vm] * 5, out_specs=[vm] * 6,
                          out_shape=[gn, gn, gn, gn, cgn, cgn])(a_re, a_im, log_dt, b_re, b_im)


def _ssm_params_bwd(name, a_re, a_im, log_dt, b_re, b_im, d_ab_re, d_ab_im, d_bb_re, d_bb_im):
    gn = jax.ShapeDtypeStruct(a_re.shape, F32)
    cgn = jax.ShapeDtypeStruct(b_re.shape, F32)

    def body(ar, ai, ld, br, bi, g0, g1, g2, g3, o_ar, o_ai, o_ld, o_br, o_bi):
        _, vjp = jax.vjp(_ssm_disc, ar[...], ai[...], ld[...], br[...], bi[...])
        outs = vjp((g0[...], g1[...], g2[...], g3[...]))
        for o_ref, o in zip((o_ar, o_ai, o_ld, o_br, o_bi), outs):
            o_ref[...] = o

    vm = pl.BlockSpec(memory_space=pltpu.VMEM)
    return pl.pallas_call(body, name=name, in_specs=[vm] * 9, out_specs=[vm] * 5,
                          out_shape=[gn, gn, jax.ShapeDtypeStruct(log_dt.shape, F32), cgn, cgn],
                          )(a_re, a_im, log_dt, b_re, b_im, d_ab_re, d_ab_im, d_bb_re, d_bb_im)


def _scan_block(s_ref, carry_ref, tmp_ref, ab_ref, ap_ref, reverse, sprev=None):
    nl = SSM_LANES
    halves = range(SCAN_COLS // SCAN_SUB)
    zero = jnp.zeros((SCAN_SUB, SCAN_LANES), F32)
    for half in (reversed(halves) if reverse else halves):
        sub_rows = pl.ds(half * SCAN_SUB, SCAN_SUB)
        for lc in range(nl // SCAN_LANES):
            re_l = pl.ds(lc * SCAN_LANES, SCAN_LANES)
            im_l = pl.ds(nl + lc * SCAN_LANES, SCAN_LANES)
            are, aim = ab_ref[:, re_l], ab_ref[:, im_l]

            def step_of(j):
                return SCAN_STEPS - 1 - j if reverse else j

            def pass1(j, st):
                sr, si = st
                jj = step_of(j)
                nr = are * sr - aim * si + s_ref[jj, sub_rows, re_l]
                ni = are * si + aim * sr + s_ref[jj, sub_rows, im_l]
                s_ref[jj, sub_rows, re_l] = nr
                s_ref[jj, sub_rows, im_l] = ni
                return nr, ni

            er, ei = lax.fori_loop(0, SCAN_STEPS, pass1, (zero, zero), unroll=2)
            tmp_ref[0:SCAN_SUB, re_l] = er
            tmp_ref[0:SCAN_SUB, im_l] = ei
            apr, api = ap_ref[0:1, re_l], ap_ref[0:1, im_l]
            sr, si = carry_ref[0:1, re_l], carry_ref[0:1, im_l]
            for step in range(SCAN_SUB):
                c = SCAN_SUB - 1 - step if reverse else step
                tmp_ref[SCAN_SUB + c:SCAN_SUB + c + 1, re_l] = sr
                tmp_ref[SCAN_SUB + c:SCAN_SUB + c + 1, im_l] = si
                e_r, e_i = tmp_ref[c:c + 1, re_l], tmp_ref[c:c + 1, im_l]
                sr, si = apr * sr - api * si + e_r, apr * si + api * sr + e_i
            carry_ref[0:1, re_l] = sr
            carry_ref[0:1, im_l] = si
            cr = tmp_ref[SCAN_SUB:2 * SCAN_SUB, re_l]
            ci = tmp_ref[SCAN_SUB:2 * SCAN_SUB, im_l]

            if sprev is None:
                def pass2(j, st):
                    pr, pi = st
                    jj = step_of(j)
                    s_ref[jj, sub_rows, re_l] += pr * cr - pi * ci
                    s_ref[jj, sub_rows, im_l] += pr * ci + pi * cr
                    return pr * are - pi * aim, pr * aim + pi * are

                lax.fori_loop(0, SCAN_STEPS, pass2, (are, aim), unroll=2)
            else:
                st_ref, prev_ref, have_prev, dab_ref = sprev

                def corrected(jj, pr, pi):
                    gr = s_ref[jj, sub_rows, re_l] + pr * cr - pi * ci
                    gi = s_ref[jj, sub_rows, im_l] + pr * ci + pi * cr
                    s_ref[jj, sub_rows, re_l] = gr
                    s_ref[jj, sub_rows, im_l] = gi
                    return gr, gi

                def pass2(j, st):
                    pr, pi, dr, di = st
                    jj = SCAN_STEPS - 1 - j
                    gr, gi = corrected(jj, pr, pi)
                    qr, qi = st_ref[jj - 1, sub_rows, re_l], st_ref[jj - 1, sub_rows, im_l]
                    return (pr * are - pi * aim, pr * aim + pi * are,
                            dr + gr * qr + gi * qi, di + gi * qr - gr * qi)

                pr, pi, dr, di = lax.fori_loop(0, SCAN_STEPS - 1, pass2, (are, aim, zero, zero), unroll=2)
                gr, gi = corrected(0, pr, pi)
                sub = lax.broadcasted_iota(jnp.int32, (SCAN_SUB, SCAN_LANES), 0)
                if half == 0:
                    pv_r = prev_ref[SCAN_SUB - 1:SCAN_SUB, re_l] * have_prev
                    pv_i = prev_ref[SCAN_SUB - 1:SCAN_SUB, im_l] * have_prev
                else:
                    before = pl.ds(half * SCAN_SUB - 1, 1)
                    pv_r, pv_i = st_ref[SCAN_STEPS - 1, before, re_l], st_ref[SCAN_STEPS - 1, before, im_l]
                shape = (SCAN_SUB, SCAN_LANES)
                qr = jnp.where(sub == 0, jnp.broadcast_to(pv_r, shape),
                               pltpu.roll(st_ref[SCAN_STEPS - 1, sub_rows, re_l], 1, 0))
                qi = jnp.where(sub == 0, jnp.broadcast_to(pv_i, shape),
                               pltpu.roll(st_ref[SCAN_STEPS - 1, sub_rows, im_l], 1, 0))
                dab_ref[:, re_l] += dr + gr * qr + gi * qi
                dab_ref[:, im_l] += di + gi * qr - gr * qi


def _scan_view(a):
    return a.reshape(16, a.shape[0] // 16, a.shape[1])


def _ssm_fwd(name, u, bb_mat, c_mat, ab_rows, ap_rows, d_skip):
    rows = u.shape[0]
    nl2 = 2 * SSM_LANES
    nblk = rows // SCAN_BLOCK

    def body(u_ref, bb_ref, c_ref, ab_ref, ap_ref, d_ref, y_ref, s_ref, carry_ref, tmp_ref):
        @pl.when(pl.program_id(0) == 0)
        def _():
            carry_ref[...] = jnp.zeros_like(carry_ref)

        uv = u_ref[...].reshape(SCAN_BLOCK, SSM_WIDTH)
        s_ref[...] = _dot_nn(uv.astype(BF16), bb_ref[...]).reshape(16, SCAN_COLS, nl2)
        _scan_block(s_ref, carry_ref, tmp_ref, ab_ref, ap_ref, reverse=False)
        sv = s_ref[...].reshape(SCAN_BLOCK, nl2)
        y_ref[...] = (_dot_nn(sv.astype(BF16), c_ref[...]) + d_ref[...] * uv).reshape(16, SCAN_COLS, SSM_WIDTH)

    const = lambda shape: pl.BlockSpec(shape, lambda i: (0, 0))
    blk = lambda cols: pl.BlockSpec((16, SCAN_COLS, cols), lambda i: (0, i, 0))
    y, s = pl.pallas_call(
        body, name=name, grid=(nblk,),
        in_specs=[blk(SSM_WIDTH), const((SSM_WIDTH, nl2)), const((nl2, SSM_WIDTH)), const((SCAN_SUB, nl2)),
                  const((SCAN_SUB, nl2)), const((1, SSM_WIDTH))],
        out_specs=[blk(SSM_WIDTH), blk(nl2)],
        out_shape=[jax.ShapeDtypeStruct((16, rows // 16, SSM_WIDTH), F32),
                   jax.ShapeDtypeStruct((16, rows // 16, nl2), F32)],
        scratch_shapes=[pltpu.VMEM((SCAN_SUB, nl2), F32), pltpu.VMEM((2 * SCAN_SUB, nl2), F32)],
        compiler_params=_cparams("arbitrary"),
    )(_scan_view(u), bb_mat, c_mat, ab_rows, ap_rows, d_skip)
    return y.reshape(rows, SSM_WIDTH), s.reshape(rows, nl2)


def _ssm_bwd(name, dy, u, states, c_mat_t, bb_mat_t, abc_rows, apc_rows, d_skip):
    rows = u.shape[0]
    nl2 = 2 * SSM_LANES
    nblk = rows // SCAN_BLOCK

    def body(dy_ref, u_ref, st_ref, prev_ref, ct_ref, bt_ref, ab_ref, ap_ref, d_ref,
             du_ref, g_ref, dab_ref, dd_ref, carry_ref, tmp_ref):
        i = pl.program_id(0)

        @pl.when(i == 0)
        def _():
            carry_ref[...] = jnp.zeros_like(carry_ref)
            dab_ref[...] = jnp.zeros_like(dab_ref)
            dd_ref[...] = jnp.zeros_like(dd_ref)

        dyv = dy_ref[...].reshape(SCAN_BLOCK, SSM_WIDTH)
        g_ref[...] = _dot_nn(dyv.astype(BF16), ct_ref[...]).reshape(16, SCAN_COLS, nl2)
        have_prev = (i < nblk - 1).astype(F32)
        _scan_block(g_ref, carry_ref, tmp_ref, ab_ref, ap_ref, reverse=True,
                    sprev=(st_ref, prev_ref, have_prev, dab_ref))
        gv = g_ref[...].reshape(SCAN_BLOCK, nl2)
        du_ref[...] = (_dot_nn(gv.astype(BF16), bt_ref[...]) + d_ref[...] * dyv).reshape(16, SCAN_COLS, SSM_WIDTH)
        dd_ref[...] += jnp.sum(dyv * u_ref[...].reshape(SCAN_BLOCK, SSM_WIDTH), axis=0, keepdims=True)

    const = lambda shape: pl.BlockSpec(shape, lambda i: (0, 0))
    blk = lambda cols: pl.BlockSpec((16, SCAN_COLS, cols), lambda i: (0, nblk - 1 - i, 0))
    per8 = SCAN_COLS // SCAN_SUB
    prev_spec = pl.BlockSpec((None, SCAN_SUB, nl2), lambda i: (15, jnp.maximum((nblk - 1 - i) * per8 - 1, 0), 0))
    sv = _scan_view(states)
    du, g, dab, dd = pl.pallas_call(
        body, name=name, grid=(nblk,),
        in_specs=[blk(SSM_WIDTH), blk(SSM_WIDTH), blk(nl2), prev_spec, const((SSM_WIDTH, nl2)),
                  const((nl2, SSM_WIDTH)), const((SCAN_SUB, nl2)), const((SCAN_SUB, nl2)), const((1, SSM_WIDTH))],
        out_specs=[blk(SSM_WIDTH), blk(nl2), const((SCAN_SUB, nl2)), const((1, SSM_WIDTH))],
        out_shape=[jax.ShapeDtypeStruct((16, rows // 16, SSM_WIDTH), F32),
                   jax.ShapeDtypeStruct((16, rows // 16, nl2), F32),
                   jax.ShapeDtypeStruct((SCAN_SUB, nl2), F32), jax.ShapeDtypeStruct((1, SSM_WIDTH), F32)],
        scratch_shapes=[pltpu.VMEM((SCAN_SUB, nl2), F32), pltpu.VMEM((2 * SCAN_SUB, nl2), F32)],
        compiler_params=_cparams("arbitrary"),
    )(_scan_view(dy), _scan_view(u), sv, sv, c_mat_t, bb_mat_t, abc_rows, apc_rows, d_skip)
    return du.reshape(rows, SSM_WIDTH), g.reshape(rows, nl2), dab, dd


def _adamw(name, w, m, v, gparts, tr):
    rows, cols = w.shape

    def body(w_ref, m_ref, v_ref, g_ref, og_ref, od_ref, om_ref, ov_ref):
        g = g_ref[0].astype(F32)
        for i in range(1, N_DEV):
            g = g + g_ref[i].astype(F32)
        m_new = B1 * m_ref[...] + (1.0 - B1) * g
        v_new = B2 * v_ref[...] + (1.0 - B2) * (g * g)
        m_hat = m_new / (1.0 - B1 ** STEP)
        v_hat = v_new / (1.0 - B2 ** STEP)
        og_ref[...] = g
        od_ref[...] = -LR * (m_hat / (jnp.sqrt(v_hat) + ADAM_EPS) + WD * w_ref[...])
        om_ref[...] = m_new
        ov_ref[...] = v_new

    spec = pl.BlockSpec((tr, cols), lambda i: (i, 0))
    shape = jax.ShapeDtypeStruct((rows, cols), F32)
    return pl.pallas_call(
        body, name=name, grid=(rows // tr,),
        in_specs=[spec, spec, spec, pl.BlockSpec((N_DEV, tr, cols), lambda i: (0, i, 0))],
        out_specs=[spec] * 4, out_shape=[shape] * 4,
        compiler_params=_cparams("parallel"),
    )(w, m, v, gparts)


_SHARDED = (
    ("ffn1_w_gate", True, (352, 1024)), ("ffn1_w_up", True, (352, 1024)), ("ffn1_w_down", False, (352, 1024)),
    ("w_in", True, (608, 1024)), ("ssm_w_glu", True, (128, 512)), ("w_attn_branch", True, (128, 256)),
    ("w_ssm_branch", True, (128, 512)), ("w_out", False, (128, 1024)),
    ("ffn2_w_gate", True, (352, 1024)), ("ffn2_w_up", True, (352, 1024)), ("ffn2_w_down", False, (352, 1024)),
)
_SMALL = ("ffn1_norm", "mix_norm", "gate_bias", "rel_bias_table", "ssm_a_re", "ssm_a_im", "ssm_log_dt",
          "ssm_b_re", "ssm_b_im", "ssm_c_re", "ssm_c_im", "ssm_d", "ffn2_norm", "final_norm")
_ORDER = ("ffn1_norm", "ffn1_w_gate", "ffn1_w_up", "ffn1_w_down", "mix_norm", "w_in", "gate_bias",
          "rel_bias_table", "ssm_a_re", "ssm_a_im", "ssm_log_dt", "ssm_b_re", "ssm_b_im", "ssm_c_re",
          "ssm_c_im", "ssm_d", "ssm_w_glu", "w_attn_branch", "w_ssm_branch", "w_out", "ffn2_norm",
          "ffn2_w_gate", "ffn2_w_up", "ffn2_w_down", "final_norm")


def _pack_rows(shape):
    return shape[0] * shape[1] // D_MODEL


_SHARD_INFO = {nm: (tr, shape) for nm, tr, shape in _SHARDED}
_PHASES = {
    "f1gu": ("ffn1_w_gate", "ffn1_w_up"), "f1d": ("ffn1_w_down",),
    "mix": ("w_in", "ssm_w_glu", "w_attn_branch", "w_ssm_branch", "w_out"),
    "f2": ("ffn2_w_gate", "ffn2_w_up", "ffn2_w_down"),
}


def _pack_sharded(ws, names):
    parts = []
    for nm in names:
        tr, shape = _SHARD_INFO[nm]
        a = ws[nm].T if tr else ws[nm]
        parts.append(a.reshape(_pack_rows(shape), D_MODEL))
    return jnp.concatenate(parts, axis=0)


def _unpack_sharded(pack, names):
    out, r0 = {}, 0
    for nm in names:
        tr, shape = _SHARD_INFO[nm]
        n = _pack_rows(shape)
        a = pack[r0:r0 + n].reshape(shape)
        out[nm] = a.T if tr else a
        r0 += n
    return out


def _unpack_gathered(gath, names):
    out, r0 = {}, 0
    for nm in names:
        _, shape = _SHARD_INFO[nm]
        n = _pack_rows(shape)
        out[nm] = gath[:, r0:r0 + n].reshape(N_DEV * shape[0], shape[1])
        r0 += n
    return out


def _pack_grads(gs, names):
    parts = []
    for nm in names:
        _, shape = _SHARD_INFO[nm]
        parts.append(gs[nm].astype(BF16).reshape(N_DEV, _pack_rows(shape), D_MODEL))
    return jnp.concatenate(parts, axis=1)


def _pack_small(ws):
    flat = jnp.concatenate([ws[nm].reshape(-1) for nm in _SMALL])
    pad = (-flat.shape[0]) % (8 * 128)
    return jnp.pad(flat, (0, pad)).reshape(-1, 128)


def _unpack_small(pack, like):
    flat, out, p0 = pack.reshape(-1), {}, 0
    for nm in _SMALL:
        n = like[nm].size
        out[nm] = flat[p0:p0 + n].reshape(like[nm].shape)
        p0 += n
    return out


def _residue_order(a):
    rows, cols = a.shape
    return a.reshape(rows // 16, 16, cols).transpose(1, 0, 2).reshape(rows, cols)


def _token_order(a):
    rows, cols = a.shape
    return a.reshape(16, rows // 16, cols).transpose(1, 0, 2).reshape(rows, cols)


def _block_diag(blocks_gab):
    g, a, b = blocks_gab.shape
    eye = jnp.eye(g, dtype=blocks_gab.dtype)
    return (blocks_gab[:, :, None, :] * eye[:, None, :, None]).reshape(g * a, g * b)


def _diag_blocks(mat, a, b):
    g = mat.shape[0] // a
    eye = jnp.eye(g, dtype=mat.dtype)
    return jnp.einsum("gahb,gh->gab", mat.reshape(g, a, g, b), eye)


def _local_step(xs, target, small, weights_of, send_grads, first_deps=()):
    rows = xs.shape[0]
    gfull, gsmall = {}, {}
    wf = dict(weights_of("f1", None))

    x1, h1, gg1, uu1 = _ffn_fwd("ffn1_fwd", xs, small["ffn1_norm"], wf["ffn1_w_gate"], wf["ffn1_w_up"],
                                wf["ffn1_w_down"], deps=first_deps)
    wf.update(weights_of("mix", x1))
    hmix = _rms_fwd("mix_norm_fwd", x1, small["mix_norm"])
    w_in = wf["w_in"]
    w_qkv, w_u, w_g = w_in[:3 * ATTN_WIDTH], w_in[3 * ATTN_WIDTH:3 * ATTN_WIDTH + SSM_WIDTH], w_in[3 * ATTN_WIDTH + SSM_WIDTH:]
    qscale = jnp.concatenate([jnp.full((1, ATTN_WIDTH), HEAD_DIM ** -0.5, F32), jnp.ones((1, 2 * ATTN_WIDTH), F32)], axis=1)
    qkv, = _mm("in_qkv", [(hmix, w_qkv)], True, 3 * ATTN_WIDTH, [BF16],
               epilogue=lambda acc, sc: (acc * sc,), extras=[(qscale, 0)], tn=ATTN_WIDTH)
    u, = _mm("in_u", [(hmix, w_u)], True, SSM_WIDTH, [F32])
    gates, = _mm("in_gates", [(hmix, w_g)], True, 2 * D_MODEL, [F32],
                 epilogue=lambda acc, b: (_sigmoid(acc + b),), extras=[(small["gate_bias"], 0)])

    table_t = small["rel_bias_table"].T
    tables, bias4, o_g, lse_g = [], [], [], []
    for g in range(N_GROUPS):
        bucket, valid = [jnp.asarray(t) for t in _attn_tables(g, rows)]
        bias_g = _bias_fwd(f"rel_bias_fwd_{g}", bucket, valid, table_t[g * HEADS_PER_GROUP:(g + 1) * HEADS_PER_GROUP])
        tables.append(bucket)
        bias4.append(bias_g.reshape(-1, bias_g.shape[-1]))
        o, lse = _attn_fwd(f"attn_fwd_{g}", qkv, g, bias4[g])
        o_g.append(o)
        lse_g.append(lse)
    oa_f32, oa = _combine_fwd("attn_combine_fwd", o_g, lse_g)
    y_attn, = _mm("attn_branch", [(oa, wf["w_attn_branch"])], True, D_MODEL, [F32])

    ab_re, ab_im, ap_re, ap_im, bb_re, bb_im = _ssm_params_fwd(
        "ssm_params_fwd", small["ssm_a_re"], small["ssm_a_im"], small["ssm_log_dt"].reshape(SSM_GROUPS, 1),
        small["ssm_b_re"].transpose(2, 0, 1), small["ssm_b_im"].transpose(2, 0, 1))

    def lanes(re, im, sign=1.0):
        row = jnp.concatenate([re.reshape(1, SSM_LANES), sign * im.reshape(1, SSM_LANES)], axis=1)
        return jnp.broadcast_to(row, (SCAN_SUB, 2 * SSM_LANES))

    bb_mat = jnp.concatenate([_block_diag(bb_re.transpose(1, 0, 2)), _block_diag(bb_im.transpose(1, 0, 2))], axis=1)
    c_mat_t = jnp.concatenate([_block_diag(small["ssm_c_re"]), -_block_diag(small["ssm_c_im"])], axis=1)
    bb_mat, c_mat_t = bb_mat.astype(BF16), c_mat_t.astype(BF16)
    d_skip = small["ssm_d"].reshape(1, SSM_WIDTH)
    y_raw, states = _ssm_fwd("ssm_fwd", u, bb_mat, c_mat_t.T, lanes(ab_re, ab_im), lanes(ap_re, ap_im), d_skip)

    def gelu_fn(yv):
        return (jax.nn.gelu(yv),)

    ygelu, = _ew("ssm_gelu", gelu_fn, [y_raw], [SSM_WIDTH], [BF16])
    glu, = _mm("ssm_glu", [(ygelu, wf["ssm_w_glu"])], True, 2 * SSM_WIDTH, [F32])
    ysg, = _ew("ssm_glu_act", lambda gv: (gv[:, :SSM_WIDTH] * _sigmoid(gv[:, SSM_WIDTH:]),), [glu], [SSM_WIDTH], [BF16])
    y_ssm, merged = _mm("ssm_branch_merge", [(ysg, wf["w_ssm_branch"])], True, D_MODEL, [F32, BF16],
                        epilogue=lambda acc, ga, gs, ya: (acc, ga * ya + gs * acc),
                        extras=[(gates, 0), (gates, D_MODEL), (y_attn, 0)])
    x2, = _mm("mix_out", [(merged, wf["w_out"])], False, D_MODEL, [F32],
              epilogue=lambda acc, res: (res + acc,), extras=[(x1, 0)])
    wf.update(weights_of("f2", x2))
    x3, h2, gg2, uu2 = _ffn_fwd("ffn2_fwd", x2, small["ffn2_norm"], wf["ffn2_w_gate"], wf["ffn2_w_up"],
                                wf["ffn2_w_down"])
    dx3, gsmall["final_norm"], loss = _final_loss("final_loss", x3, small["final_norm"].reshape(1, D_MODEL), target)

    dx2, dgg2, duu2, act2, gsmall["ffn2_norm"] = _ffn_bwd(
        "ffn2_bwd", dx3, x2, small["ffn2_norm"], gg2, uu2, wf["ffn2_w_gate"], wf["ffn2_w_up"], wf["ffn2_w_down"])
    gfull["ffn2_w_gate"] = _mm_tn("ffn2_dwg", dgg2, h2)
    gfull["ffn2_w_up"] = _mm_tn("ffn2_dwu", duu2, h2)
    gfull["ffn2_w_down"] = _mm_tn("ffn2_dwd", act2, dx3, scale=0.5)
    sent = send_grads("f2", gfull)

    def merge_bwd(dm, ga, gs, ya, ys):
        return (dm * ga, dm * gs, dm * ya * ga * (1.0 - ga), dm * ys * gs * (1.0 - gs))

    dya, dys, dzga, dzgs = _mm("mix_out_bwd", [(dx2, wf["w_out"])], True, D_MODEL, [BF16] * 4, epilogue=merge_bwd,
                               extras=[(gates, 0), (gates, D_MODEL), (y_attn, 0), (y_ssm, 0)], deps=sent)
    gfull["w_out"] = _mm_tn("dw_out", merged, dx2)
    gsmall["gate_bias"] = jnp.concatenate([_colsum("dgate_bias_a", dzga), _colsum("dgate_bias_s", dzgs)], axis=1)

    gfull["w_ssm_branch"] = _mm_tn("dw_ssm_branch", dys, ysg)

    def glu_bwd(dysg, av, bv):
        sb = _sigmoid(bv)
        return (dysg * sb, dysg * av * sb * (1.0 - sb))

    dglu_a, dglu_b = _mm("ssm_branch_bwd", [(dys, wf["w_ssm_branch"])], False, SSM_WIDTH, [BF16, BF16],
                         epilogue=glu_bwd, extras=[(glu, 0), (glu, SSM_WIDTH)])
    w_glu = wf["ssm_w_glu"]
    gfull["ssm_w_glu"] = jnp.concatenate([_mm_tn("dw_glu_a", dglu_a, ygelu), _mm_tn("dw_glu_b", dglu_b, ygelu)], axis=0)

    def gelu_bwd(acc, yv):
        _, vjp = jax.vjp(jax.nn.gelu, yv)
        return (vjp(acc)[0],)

    dy_raw, = _mm("ssm_glu_bwd", [(dglu_a, w_glu[:SSM_WIDTH]), (dglu_b, w_glu[SSM_WIDTH:])], False, SSM_WIDTH, [F32],
                  epilogue=gelu_bwd, extras=[(y_raw, 0)])
    du, g_states, dab_rows, gsmall_d = _ssm_bwd(
        "ssm_bwd", dy_raw, u, states, c_mat_t, bb_mat.T, lanes(ab_re, ab_im, -1.0), lanes(ap_re, ap_im, -1.0), d_skip)
    gsmall["ssm_d"] = gsmall_d
    dbb_acc = _mm_tn("ssm_dbb", u, g_states, bm=SSM_WIDTH)
    dc_acc = _mm_tn("ssm_dc", dy_raw, states, bm=SSM_WIDTH)
    dbb_re = _diag_blocks(dbb_acc[:, :SSM_LANES], SSM_GROUP, SSM_STATE).transpose(1, 0, 2)
    dbb_im = _diag_blocks(dbb_acc[:, SSM_LANES:], SSM_GROUP, SSM_STATE).transpose(1, 0, 2)
    gsmall["ssm_c_re"] = _diag_blocks(dc_acc[:, :SSM_LANES], SSM_GROUP, SSM_STATE)
    gsmall["ssm_c_im"] = -_diag_blocks(dc_acc[:, SSM_LANES:], SSM_GROUP, SSM_STATE)
    dab = _colsum("ssm_dab", dab_rows)
    d_ar, d_ai, d_ld, d_br, d_bi = _ssm_params_bwd(
        "ssm_params_bwd", small["ssm_a_re"], small["ssm_a_im"], small["ssm_log_dt"].reshape(SSM_GROUPS, 1),
        small["ssm_b_re"].transpose(2, 0, 1), small["ssm_b_im"].transpose(2, 0, 1),
        dab[:, :SSM_LANES].reshape(SSM_GROUPS, SSM_STATE), dab[:, SSM_LANES:].reshape(SSM_GROUPS, SSM_STATE),
        dbb_re, dbb_im)
    gsmall["ssm_a_re"], gsmall["ssm_a_im"], gsmall["ssm_log_dt"] = d_ar, d_ai, d_ld.reshape(SSM_GROUPS)
    gsmall["ssm_b_re"], gsmall["ssm_b_im"] = d_br.transpose(1, 2, 0), d_bi.transpose(1, 2, 0)

    gfull["w_attn_branch"] = _mm_tn("dw_attn_branch", dya, oa)
    doa, = _mm("attn_branch_bwd", [(dya, wf["w_attn_branch"])], False, ATTN_OUT, [F32])
    dc = _combine_bwd("attn_combine_bwd", doa, oa_f32, lse_g)
    dqkv_cols = [None] * 9
    dtable = []
    for g in range(N_GROUPS):
        dq, dk, dv, db = _attn_bwd(f"attn_bwd_{g}", qkv, dc[g], lse_g[g], dc[3 + g], g, bias4[g])
        dqkv_cols[g], dqkv_cols[3 + g], dqkv_cols[6 + g] = dq, dk, dv
        dt = _bias_bwd(f"rel_bias_bwd_{g}", tables[g], db.reshape(HEADS_PER_GROUP, -1, db.shape[-1]))
        dtable.append(dt[:, :HEADS_PER_GROUP])
    gsmall["rel_bias_table"] = jnp.concatenate(dtable, axis=1)

    gfull["w_in"] = jnp.concatenate([_mm_tn_stack("dw_in_qkv", dqkv_cols, hmix),
                                     _mm_tn_stack("dw_in_rest", [du, dzga, dzgs], hmix)], axis=0)
    sent = send_grads("mix", gfull)
    qkv_pairs = [(c, w_qkv[i * ATTN_OUT:(i + 1) * ATTN_OUT]) for i, c in enumerate(dqkv_cols)]
    dhmix, = _mm("in_bwd", qkv_pairs + [(du, w_u), (dzga, w_g[:D_MODEL]), (dzgs, w_g[D_MODEL:])], False, D_MODEL,
                 [F32], tm=512, deps=sent)
    dx1, gsmall["mix_norm"] = _rms_bwd("mix_norm_bwd", dhmix, x1, small["mix_norm"], dx2)

    dx, dgg1, duu1, act1, gsmall["ffn1_norm"] = _ffn_bwd(
        "ffn1_bwd", dx1, xs, small["ffn1_norm"], gg1, uu1, wf["ffn1_w_gate"], wf["ffn1_w_up"], wf["ffn1_w_down"])
    sent = send_grads("small", gsmall)
    gfull["ffn1_w_down"] = _mm_tn("ffn1_dwd", act1, dx1, scale=0.5, deps=sent)
    sent = send_grads("f1d", gfull)
    gfull["ffn1_w_gate"] = _mm_tn("ffn1_dwg", dgg1, h1, deps=sent)
    gfull["ffn1_w_up"] = _mm_tn("ffn1_dwu", duu1, h1)
    send_grads("f1gu", gfull)
    return loss[0, 0], dx, gsmall


def kernel(x, ffn1_norm, ffn1_w_gate, ffn1_w_up, ffn1_w_down, mix_norm, w_in, gate_bias, rel_bias_table, ssm_a_re, ssm_a_im, ssm_log_dt, ssm_b_re, ssm_b_im, ssm_c_re, ssm_c_im, ssm_d, ssm_w_glu, w_attn_branch, w_ssm_branch, w_out, ffn2_norm, ffn2_w_gate, ffn2_w_up, ffn2_w_down, final_norm, loss_target, m_ffn1_norm, m_ffn1_w_gate, m_ffn1_w_up, m_ffn1_w_down, m_mix_norm, m_w_in, m_gate_bias, m_rel_bias_table, m_ssm_a_re, m_ssm_a_im, m_ssm_log_dt, m_ssm_b_re, m_ssm_b_im, m_ssm_c_re, m_ssm_c_im, m_ssm_d, m_ssm_w_glu, m_w_attn_branch, m_w_ssm_branch, m_w_out, m_ffn2_norm, m_ffn2_w_gate, m_ffn2_w_up, m_ffn2_w_down, m_final_norm, v_ffn1_norm, v_ffn1_w_gate, v_ffn1_w_up, v_ffn1_w_down, v_mix_norm, v_w_in, v_gate_bias, v_rel_bias_table, v_ssm_a_re, v_ssm_a_im, v_ssm_log_dt, v_ssm_b_re, v_ssm_b_im, v_ssm_c_re, v_ssm_c_im, v_ssm_d, v_ssm_w_glu, v_w_attn_branch, v_w_ssm_branch, v_w_out, v_ffn2_norm, v_ffn2_w_gate, v_ffn2_w_up, v_ffn2_w_down, v_final_norm):
    given = dict(locals())
    shapes = {nm: given[nm].shape for nm in _ORDER}

    def strip(a):
        return a[0] if a.ndim >= 2 and a.shape[0] == 1 else a

    w = {nm: strip(given[nm]) for nm in _ORDER}
    m = {nm: strip(given["m_" + nm]) for nm in _ORDER}
    v = {nm: strip(given["v_" + nm]) for nm in _ORDER}
    for d in (w, m, v):
        d["rel_bias_table"] = d["rel_bias_table"].reshape(N_BUCKETS, N_GROUPS * HEADS_PER_GROUP)

    small = {nm: w[nm] for nm in _SMALL}
    small_in = dict(small)
    for nm in ("ffn1_norm", "mix_norm", "ffn2_norm", "gate_bias"):
        small_in[nm] = small[nm].reshape(1, -1)
    w_pack = {ph: _pack_sharded(w, names) for ph, names in _PHASES.items()}

    f1_names = _PHASES["f1gu"] + _PHASES["f1d"]
    got_f1 = _all_gather("gather_f1", jnp.concatenate([w_pack["f1gu"], w_pack["f1d"]], axis=0).astype(BF16))
    pending_w = {"mix": _exchange_start("gather_mix_start", w_pack["mix"].astype(BF16), gather=True, deps=[got_f1])}
    pending_w["f2"] = _exchange_start("gather_f2_start", w_pack["f2"].astype(BF16), gather=True,
                                      deps=[pending_w["mix"][4]])

    def weights_of(phase, after):
        if phase == "f1":
            return _unpack_gathered(got_f1, f1_names)
        return _unpack_gathered(_exchange_wait(f"gather_{phase}_wait", pending_w[phase], after, gather=True),
                                _PHASES[phase])

    pending_g = {}

    def send_grads(phase, grads):
        if phase == "small":
            gs_pack = _pack_small({nm: grads[nm].reshape(small[nm].shape) for nm in _SMALL})
            pending_g[phase] = _exchange_start("gather_small_start", gs_pack, gather=True)
        else:
            pending_g[phase] = _exchange_start(f"scatter_{phase}_start", _pack_grads(grads, _PHASES[phase]),
                                               gather=False)
        return [pending_g[phase][4]]

    loss, dx, gsmall = _local_step(_residue_order(x[0]), _residue_order(loss_target[0]), small_in, weights_of,
                                   send_grads, first_deps=[pending_w["f2"][4]])
    dx = _token_order(dx)

    packs = {}
    after = pending_g["f1gu"][4]
    for phase in ("f2", "mix", "f1d", "small", "f1gu"):
        if phase == "small":
            gs_all = _exchange_wait("gather_small_wait", pending_g[phase], after, gather=True)
            sm = _adamw("adamw_small", _pack_small(small), _pack_small({nm: m[nm] for nm in _SMALL}),
                        _pack_small({nm: v[nm] for nm in _SMALL}), gs_all, gs_all.shape[1])
            after = sm[0]
            continue
        names = _PHASES[phase]
        recv = _exchange_wait(f"scatter_{phase}_wait", pending_g[phase], after, gather=False)
        rows_p = w_pack[phase].shape[0]
        tr = max(t for t in range(16, 129, 16) if rows_p % t == 0)
        packs[phase] = _adamw(f"adamw_{phase}", w_pack[phase], _pack_sharded(m, names), _pack_sharded(v, names),
                              recv, tr)
        after = packs[phase][0]

    loss = lax.psum(loss, ("x", "y", "c"))
    outs = []
    for i in range(4):
        big = {}
        for phase, names in _PHASES.items():
            big.update(_unpack_sharded(packs[phase][i], names))
        sml = _unpack_small(sm[i], small)
        outs.append([(big[nm] if nm in big else sml[nm]).reshape(shapes[nm]) for nm in _ORDER])
    return (loss, dx[None], *outs[0], *outs[1], *outs[2], *outs[3])
```

```python
import math

import numpy as np
import jax
import jax.numpy as jnp
from jax import lax
from jax.experimental import pallas as pl
from jax.experimental.pallas import tpu as pltpu

F32 = jnp.float32
BF16 = jnp.bfloat16

N_DEV = 8
D_MODEL = 1024
D_FF = 2816
HEAD_DIM = 64
HEADS_PER_GROUP = 4
DILATIONS = (1, 4, 16)
N_GROUPS = 3
ATTN_WIDTH = 768
ATTN_OUT = 256
BLOCK = 128
N_BUCKETS = 32
MAX_DISTANCE = 2048
NEG_INF = -1e30
SSM_WIDTH = 512
SSM_GROUPS = 32
SSM_GROUP = 16
SSM_STATE = 64
SSM_LANES = SSM_GROUPS * SSM_STATE
EPS = 1e-6
LR, B1, B2, ADAM_EPS, WD, STEP = 0.001, 0.9, 0.999, 1e-08, 0.01, 10

VMEM_LIMIT_BYTES = 56 * 1024 * 1024
FFN_CHUNK = D_FF // 2
SCAN_BLOCK = 256
SCAN_STEPS = 16
SCAN_COLS = SCAN_BLOCK // SCAN_STEPS
SCAN_SUB = 8
SCAN_LANES = 512

MESH = pl.DeviceIdType.MESH


def _cparams(*sem):
    return pltpu.CompilerParams(dimension_semantics=sem, vmem_limit_bytes=VMEM_LIMIT_BYTES)


def _dot(a, b, dims):
    return lax.dot_general(a, b, (dims, ((), ())), preferred_element_type=F32)


def _dot_nn(a, b):
    return _dot(a, b, ((1,), (0,)))


def _dot_nt(a, b):
    return _dot(a, b, ((1,), (1,)))


def _dot_tn(a, b):
    return _dot(a, b, ((0,), (0,)))


def _sigmoid(x):
    return 1.0 / (1.0 + jnp.exp(-x))


def _all_gather(name, xs_list):
    n = len(xs_list)

    def body(*refs):
        x_refs, out_refs = refs[:n], refs[n:2 * n]
        send_sems, recv_sems, local_sems = refs[2 * n:]
        x, y, c = lax.axis_index("x"), lax.axis_index("y"), lax.axis_index("c")
        me, sibling = (x, y, c), (x, y, 1 - c)
        chips = [(1 - x, y), (x, 1 - y), (1 - x, 1 - y)]

        def copy(a, k, block, to, own=False):
            px, py, pc = block
            rows = out_refs[a].at[4 * px + 2 * py + pc]
            return pltpu.make_async_remote_copy(
                src_ref=x_refs[a] if own else rows, dst_ref=rows,
                send_sem=send_sems.at[7 * a + k], recv_sem=recv_sems.at[7 * a + k], device_id=to,
                device_id_type=MESH)

        mine = [pltpu.make_async_copy(x_refs[a], out_refs[a].at[4 * x + 2 * y + c], local_sems.at[a]) for a in range(n)]
        first = []
        for a in range(n):
            mine[a].start()
            first.append(copy(a, 0, me, sibling, own=True))
            first += [copy(a, 1 + j, me, (*chip, c), own=True) for j, chip in enumerate(chips)]
        for cp in first:
            cp.start()
        passed = []
        for a in range(n):
            for j, chip in enumerate(chips):
                copy(a, 1 + j, (*chip, c), me).wait_recv()
                passed.append(copy(a, 4 + j, (*chip, c), sibling))
                passed[-1].start()
        for a in range(n):
            copy(a, 0, sibling, me).wait_recv()
            for j, chip in enumerate(chips):
                copy(a, 4 + j, (*chip, 1 - c), me).wait_recv()
        for cp in first + passed:
            cp.wait_send()
        for cp in mine:
            cp.wait()

    return pl.pallas_call(
        body, name=name,
        out_shape=[jax.ShapeDtypeStruct((N_DEV, *xs.shape), xs.dtype) for xs in xs_list],
        in_specs=[_ANY_SPEC] * n, out_specs=[_ANY_SPEC] * n,
        scratch_shapes=[pltpu.SemaphoreType.DMA((7 * n,)), pltpu.SemaphoreType.DMA((7 * n,)),
                        pltpu.SemaphoreType.DMA((n,))],
    )(*xs_list)


_HBM_SPEC = pl.BlockSpec(memory_space=pltpu.HBM)
_SEM_SPEC = pl.BlockSpec(memory_space=pltpu.SEMAPHORE)
_ANY_SPEC = pl.BlockSpec(memory_space=pl.ANY)
_EFFECT = pltpu.SideEffectType.DATAFLOW_SIDE_EFFECTING


def _peers(x, y, c):
    return [(1 - x if k & 4 else x, 1 - y if k & 2 else y, 1 - c if k & 1 else c) for k in range(1, N_DEV)]


def _exchange_copies(x_refs, land_refs, send_sems, recv_sems, gather):
    x, y, c = lax.axis_index("x"), lax.axis_index("y"), lax.axis_index("c")
    me = 4 * x + 2 * y + c
    copies = []
    for a, (x_ref, land_ref) in enumerate(zip(x_refs, land_refs)):
        for k, (px, py, pc) in enumerate(_peers(x, y, c)):
            src = x_ref if gather else x_ref.at[4 * px + 2 * py + pc]
            copies.append(pltpu.make_async_remote_copy(
                src_ref=src, dst_ref=land_ref.at[me], send_sem=send_sems.at[N_DEV * a + k],
                recv_sem=recv_sems.at[(N_DEV - 1) * a + k], device_id=(px, py, pc), device_id_type=MESH))
    owns = [pltpu.make_async_copy(x_ref if gather else x_ref.at[me], land_ref.at[me],
                                  send_sems.at[N_DEV * a + N_DEV - 1])
            for a, (x_ref, land_ref) in enumerate(zip(x_refs, land_refs))]
    return owns, copies


def _exchange_start(name, xs_list, gather, deps=()):
    n, nd = len(xs_list), len(deps)
    land_shapes = [(N_DEV, *xs.shape) if gather else xs.shape for xs in xs_list]

    def body(*refs):
        x_refs, land_refs = refs[:n], refs[n:2 * n]
        send_sems, recv_sems = refs[2 * n + nd:2 * n + nd + 2]
        token = refs[-1]
        owns, copies = _exchange_copies(x_refs, land_refs, send_sems, recv_sems, gather)
        for cp in copies + owns:
            cp.start()
        token[...] = jnp.zeros_like(token)

    hbm = lambda a: pltpu.with_memory_space_constraint(a, pltpu.HBM)
    outs = pl.pallas_call(
        body, name=name,
        out_shape=(pltpu.SemaphoreType.DMA((n * N_DEV,)), pltpu.SemaphoreType.DMA((n * (N_DEV - 1),)),
                   *[pltpu.HBM(xs.shape, xs.dtype) for xs in xs_list],
                   *[pltpu.HBM(shape, xs.dtype) for shape, xs in zip(land_shapes, xs_list)],
                   jax.ShapeDtypeStruct((8, 128), F32)),
        in_specs=(_HBM_SPEC,) * (2 * n) + (_ANY_SPEC,) * nd,
        out_specs=(_SEM_SPEC, _SEM_SPEC) + (_HBM_SPEC,) * (2 * n) + (pl.BlockSpec(memory_space=pltpu.VMEM),),
        input_output_aliases={i: 2 + i for i in range(2 * n)},
        compiler_params=pltpu.CompilerParams(has_side_effects=_EFFECT),
    )(*[hbm(xs) for xs in xs_list], *[hbm(lax.empty(shape, xs.dtype)) for shape, xs in zip(land_shapes, xs_list)],
      *deps)
    return outs[0], outs[1], list(outs[2:2 + n]), list(outs[2 + n:2 + 2 * n]), outs[-1]


def _exchange_wait(name, handle, after, gather):
    send_sems, recv_sems, xs_thru, lands_thru, _ = handle
    n = len(xs_thru)

    def body(*refs):
        x_refs, land_refs = refs[:n], refs[n:2 * n]
        send_sems, recv_sems = refs[2 * n:2 * n + 2]
        owns, copies = _exchange_copies(x_refs, land_refs, send_sems, recv_sems, gather)
        for cp in copies:
            cp.wait_send()
            cp.wait_recv()
        for cp in owns:
            cp.wait()

    outs = pl.pallas_call(
        body, name=name,
        out_shape=tuple(pltpu.HBM(a.shape, a.dtype) for a in xs_thru + lands_thru),
        in_specs=(_HBM_SPEC,) * (2 * n) + (_SEM_SPEC, _SEM_SPEC, _ANY_SPEC),
        out_specs=(_HBM_SPEC,) * (2 * n), input_output_aliases={i: i for i in range(2 * n)},
        compiler_params=pltpu.CompilerParams(has_side_effects=_EFFECT),
    )(*xs_thru, *lands_thru, send_sems, recv_sems, after)
    return list(outs[n:])


def _mm(name, pairs, nt, n_cols, out_dtypes, epilogue=None, extras=(), tm=1024, tn=512, deps=()):
    rows = pairs[0][0].shape[0]
    tm = min(tm, rows)
    tn = min(tn, n_cols)
    na, ne, nd = len(pairs), len(extras), len(deps)

    def body(*refs):
        a_refs, w_refs = refs[:na], refs[na:2 * na]
        e_refs, o_refs = refs[2 * na:2 * na + ne], refs[2 * na + ne + nd:]
        acc = None
        for a_ref, w_ref in zip(a_refs, w_refs):
            a = a_ref[...].astype(BF16)
            w = w_ref[...].astype(BF16)
            p = _dot_nt(a, w) if nt else _dot_nn(a, w)
            acc = p if acc is None else acc + p
        outs = (acc,) if epilogue is None else epilogue(acc, *[e[...] for e in e_refs])
        for o_ref, o in zip(o_refs, outs):
            o_ref[...] = o.astype(o_ref.dtype)

    in_specs = [pl.BlockSpec((tm, a.shape[1]), lambda i, j: (i, 0)) for a, _ in pairs]
    for _, w in pairs:
        if nt:
            in_specs.append(pl.BlockSpec((tn, w.shape[1]), lambda i, j: (j, 0)))
        else:
            in_specs.append(pl.BlockSpec((w.shape[0], tn), lambda i, j: (0, j)))
    for e, col_off in extras:
        off = col_off // tn
        if e.shape[0] == 1:
            in_specs.append(pl.BlockSpec((1, tn), lambda i, j, off=off: (0, j + off)))
        else:
            in_specs.append(pl.BlockSpec((tm, tn), lambda i, j, off=off: (i, j + off)))
    in_specs += [_ANY_SPEC] * nd
    out_specs = [pl.BlockSpec((tm, tn), lambda i, j: (i, j)) for _ in out_dtypes]
    outs = pl.pallas_call(
        body, name=name, grid=(rows // tm, n_cols // tn),
        in_specs=in_specs, out_specs=out_specs,
        out_shape=[jax.ShapeDtypeStruct((rows, n_cols), dt) for dt in out_dtypes],
        compiler_params=_cparams("parallel", "arbitrary"),
    )(*[a for a, _ in pairs], *[w for _, w in pairs], *[e for e, _ in extras], *deps)
    return outs


def _tn_rows(m):
    return max(b for b in range(128, min(m, 1408) + 1, 128) if m % b == 0)


def _mm_tn(name, a, b, scale=1.0, bm=None, tk=1024, deps=(), out_dtype=F32):
    rows, m = a.shape
    n = b.shape[1]
    bm = _tn_rows(m) if bm is None else bm
    tk = min(tk, rows)
    nk = rows // tk

    def body(a_ref, b_ref, *rest):
        o_ref, acc_ref = rest[-2:]
        k = pl.program_id(1)

        @pl.when(k == 0)
        def _():
            acc_ref[...] = jnp.zeros_like(acc_ref)

        acc_ref[...] += _dot_tn(a_ref[...].astype(BF16), b_ref[...].astype(BF16))

        @pl.when(k == nk - 1)
        def _():
            o_ref[...] = (acc_ref[...] * scale).astype(o_ref.dtype)

    return pl.pallas_call(
        body, name=name, grid=(m // bm, nk),
        in_specs=[pl.BlockSpec((tk, bm), lambda i, k: (k, i)), pl.BlockSpec((tk, n), lambda i, k: (k, 0))]
        + [_ANY_SPEC] * len(deps),
        out_specs=pl.BlockSpec((bm, n), lambda i, k: (i, 0)),
        out_shape=jax.ShapeDtypeStruct((m, n), out_dtype),
        scratch_shapes=[pltpu.VMEM((bm, n), F32)],
        compiler_params=_cparams("parallel", "arbitrary"),
    )(a, b, *deps)


def _mm_tn_stack(name, a_list, b, tk=1024, out_dtype=F32):
    rows, n = b.shape
    ms = [a.shape[1] for a in a_list]
    tk = min(tk, rows)
    nk = rows // tk
    na = len(a_list)

    def body(*refs):
        a_refs, b_ref, o_ref, acc_ref = refs[:na], refs[na], refs[na + 1], refs[na + 2]
        k = pl.program_id(0)

        @pl.when(k == 0)
        def _():
            acc_ref[...] = jnp.zeros_like(acc_ref)

        bv = b_ref[...].astype(BF16)
        r0 = 0
        for a_ref, m in zip(a_refs, ms):
            acc_ref[r0:r0 + m, :] += _dot_tn(a_ref[...].astype(BF16), bv)
            r0 += m

        @pl.when(k == nk - 1)
        def _():
            o_ref[...] = acc_ref[...].astype(o_ref.dtype)

    return pl.pallas_call(
        body, name=name, grid=(nk,),
        in_specs=[pl.BlockSpec((tk, m), lambda k: (k, 0)) for m in ms] + [pl.BlockSpec((tk, n), lambda k: (k, 0))],
        out_specs=pl.BlockSpec((sum(ms), n), lambda k: (0, 0)),
        out_shape=jax.ShapeDtypeStruct((sum(ms), n), out_dtype),
        scratch_shapes=[pltpu.VMEM((sum(ms), n), F32)],
        compiler_params=_cparams("arbitrary"),
    )(*a_list, b)


def _colsum(name, xs, tm=512):
    rows, cols = xs.shape
    tm = min(tm, rows)

    def body(x_ref, o_ref):
        @pl.when(pl.program_id(0) == 0)
        def _():
            o_ref[...] = jnp.zeros_like(o_ref)

        o_ref[...] += jnp.sum(x_ref[...].astype(F32), axis=0, keepdims=True)

    return pl.pallas_call(
        body, name=name, grid=(rows // tm,),
        in_specs=[pl.BlockSpec((tm, cols), lambda i: (i, 0))],
        out_specs=pl.BlockSpec((1, cols), lambda i: (0, 0)),
        out_shape=jax.ShapeDtypeStruct((1, cols), F32),
        compiler_params=_cparams("arbitrary"),
    )(xs)


def _ew(name, fn, ins, out_cols, out_dtypes, tm=512):
    rows = ins[0].shape[0]
    tm = min(tm, rows)
    ni = len(ins)

    def body(*refs):
        outs = fn(*[r[...] for r in refs[:ni]])
        for o_ref, o in zip(refs[ni:], outs):
            o_ref[...] = o.astype(o_ref.dtype)

    def spec(shape):
        if shape[0] == 1:
            return pl.BlockSpec((1, shape[1]), lambda i: (0, 0))
        return pl.BlockSpec((tm, shape[1]), lambda i: (i, 0))

    return pl.pallas_call(
        body, name=name, grid=(rows // tm,),
        in_specs=[spec(a.shape) for a in ins],
        out_specs=[pl.BlockSpec((tm, c), lambda i: (i, 0)) for c in out_cols],
        out_shape=[jax.ShapeDtypeStruct((rows, c), dt) for c, dt in zip(out_cols, out_dtypes)],
        compiler_params=_cparams("parallel"),
    )(*ins)


def _rms_parts(xv):
    r = lax.rsqrt(jnp.mean(xv * xv, axis=-1, keepdims=True) + EPS)
    return r, xv * r


def _rms_bwd_dx(dh, gain, r, xh):
    dxh = dh * gain
    return r * (dxh - xh * jnp.mean(dxh * xh, axis=-1, keepdims=True))


def _rms_fwd(name, xs, gain):
    def fn(xv, g):
        _, xh = _rms_parts(xv)
        return (xh * g,)

    return _ew(name, fn, [xs, gain], [xs.shape[1]], [BF16])[0]


def _rms_bwd(name, dh, xs, gain, dres, tm=512):
    rows, d = xs.shape
    tm = min(tm, rows)

    def body(dh_ref, x_ref, g_ref, dres_ref, dx_ref, dg_ref):
        r, xh = _rms_parts(x_ref[...])
        dhv = dh_ref[...]
        dx_ref[...] = dres_ref[...] + _rms_bwd_dx(dhv, g_ref[...], r, xh)

        @pl.when(pl.program_id(0) == 0)
        def _():
            dg_ref[...] = jnp.zeros_like(dg_ref)

        dg_ref[...] += jnp.sum(dhv * xh, axis=0, keepdims=True)

    tile = pl.BlockSpec((tm, d), lambda i: (i, 0))
    row = pl.BlockSpec((1, d), lambda i: (0, 0))
    return pl.pallas_call(
        body, name=name, grid=(rows // tm,),
        in_specs=[tile, tile, row, tile], out_specs=[tile, row],
        out_shape=[jax.ShapeDtypeStruct((rows, d), F32), jax.ShapeDtypeStruct((1, d), F32)],
        compiler_params=_cparams("arbitrary"),
    )(dh, xs, gain, dres)


def _ffn_chunks(f_all):
    return [slice(c, c + FFN_CHUNK) for c in range(0, f_all, FFN_CHUNK)]


def _ffn_fwd(name, xs, gain, wg_t, wu_t, wd, tm=512, deps=()):
    rows, d = xs.shape
    f_all = wd.shape[0]
    tm = min(tm, rows)

    def body(x_ref, g_ref, wg_ref, wu_ref, wd_ref, *rest):
        xo_ref, h_ref, gg_ref, uu_ref = rest[-4:]
        xv = x_ref[...]
        _, xh = _rms_parts(xv)
        h = (xh * g_ref[...]).astype(BF16)
        h_ref[...] = h
        acc = None
        for cols in _ffn_chunks(f_all):
            gg = _dot_nt(h, wg_ref[cols, :])
            uu = _dot_nt(h, wu_ref[cols, :])
            act = gg * _sigmoid(gg) * uu
            part = _dot_nn(act.astype(BF16), wd_ref[cols, :])
            acc = part if acc is None else acc + part
            gg_ref[:, cols] = gg.astype(BF16)
            uu_ref[:, cols] = uu.astype(BF16)
        xo_ref[...] = xv + 0.5 * acc

    tile = pl.BlockSpec((tm, d), lambda i: (i, 0))
    wspec = pl.BlockSpec((f_all, d), lambda i: (0, 0), pipeline_mode=pl.Buffered(1))
    hid = pl.BlockSpec((tm, f_all), lambda i: (i, 0))
    return pl.pallas_call(
        body, name=name, grid=(rows // tm,),
        in_specs=[tile, pl.BlockSpec((1, d), lambda i: (0, 0)), wspec, wspec, wspec] + [_ANY_SPEC] * len(deps),
        out_specs=[tile, tile, hid, hid],
        out_shape=[jax.ShapeDtypeStruct((rows, d), F32), jax.ShapeDtypeStruct((rows, d), BF16),
                   jax.ShapeDtypeStruct((rows, f_all), BF16), jax.ShapeDtypeStruct((rows, f_all), BF16)],
        compiler_params=_cparams("parallel"),
    )(xs, gain, wg_t, wu_t, wd, *deps)


def _ffn_bwd(name, dxo, xs, gain, gg_all, uu_all, wg_t, wu_t, wd, tm=256):
    rows, d = xs.shape
    f_all = wd.shape[0]
    tm = min(tm, rows)

    def body(dxo_ref, x_ref, g_ref, gg_ref, uu_ref, wg_ref, wu_ref, wd_ref,
             dx_ref, dgg_ref, duu_ref, act_ref, dgain_ref):
        dxo = dxo_ref[...]
        df = (0.5 * dxo).astype(BF16)
        dh = None
        for cols in _ffn_chunks(f_all):
            gg = gg_ref[:, cols].astype(F32)
            uu = uu_ref[:, cols].astype(F32)
            sg = _sigmoid(gg)
            silu = gg * sg
            dact = _dot_nt(df, wd_ref[cols, :])
            duu = (dact * silu).astype(BF16)
            dgg = (dact * uu * (sg * (1.0 + gg * (1.0 - sg)))).astype(BF16)
            act_ref[:, cols] = (silu * uu).astype(BF16)
            dgg_ref[:, cols] = dgg
            duu_ref[:, cols] = duu
            part = _dot_nn(dgg, wg_ref[cols, :]) + _dot_nn(duu, wu_ref[cols, :])
            dh = part if dh is None else dh + part
        r, xh = _rms_parts(x_ref[...])
        dx_ref[...] = dxo + _rms_bwd_dx(dh, g_ref[...], r, xh)

        @pl.when(pl.program_id(0) == 0)
        def _():
            dgain_ref[...] = jnp.zeros_like(dgain_ref)

        dgain_ref[...] += jnp.sum(dh * xh, axis=0, keepdims=True)

    tile = pl.BlockSpec((tm, d), lambda i: (i, 0))
    row = pl.BlockSpec((1, d), lambda i: (0, 0))
    wspec = pl.BlockSpec((f_all, d), lambda i: (0, 0), pipeline_mode=pl.Buffered(1))
    hid = pl.BlockSpec((tm, f_all), lambda i: (i, 0))
    hid_shape = jax.ShapeDtypeStruct((rows, f_all), BF16)
    return pl.pallas_call(
        body, name=name, grid=(rows // tm,),
        in_specs=[tile, tile, row, hid, hid, wspec, wspec, wspec],
        out_specs=[tile, hid, hid, hid, row],
        out_shape=[jax.ShapeDtypeStruct((rows, d), F32), hid_shape, hid_shape, hid_shape,
                   jax.ShapeDtypeStruct((1, d), F32)],
        compiler_params=_cparams("arbitrary"),
    )(dxo, xs, gain, gg_all, uu_all, wg_t, wu_t, wd)


def _final_loss(name, xs, gain, target, tm=512):
    rows, d = xs.shape
    tm = min(tm, rows)

    def body(x_ref, g_ref, t_ref, dx_ref, dg_ref, loss_ref):
        r, xh = _rms_parts(x_ref[...])
        gain_v = g_ref[...]
        err = xh * gain_v - t_ref[...]
        dy = err * (1.0 / d)
        dx_ref[...] = _rms_bwd_dx(dy, gain_v, r, xh)

        @pl.when(pl.program_id(0) == 0)
        def _():
            dg_ref[...] = jnp.zeros_like(dg_ref)
            loss_ref[...] = jnp.zeros_like(loss_ref)

        dg_ref[...] += jnp.sum(dy * xh, axis=0, keepdims=True)
        per_tok = jnp.mean(err * err, axis=-1, keepdims=True)
        loss_ref[...] += 0.5 * jnp.sum(per_tok, axis=0, keepdims=True)

    tile = pl.BlockSpec((tm, d), lambda i: (i, 0))
    row = pl.BlockSpec((1, d), lambda i: (0, 0))
    return pl.pallas_call(
        body, name=name, grid=(rows // tm,),
        in_specs=[tile, row, tile],
        out_specs=[tile, row, pl.BlockSpec((1, 1), lambda i: (0, 0))],
        out_shape=[jax.ShapeDtypeStruct((rows, d), F32), jax.ShapeDtypeStruct((1, d), F32),
                   jax.ShapeDtypeStruct((1, 1), F32)],
        compiler_params=_cparams("arbitrary"),
    )(xs, gain, target)


def _t5_bucket_np(dist):
    max_exact = N_BUCKETS // 2
    dd = np.maximum(dist, 1).astype(np.float32)
    large = max_exact + (np.log(dd / np.float32(max_exact)) / np.float32(math.log(MAX_DISTANCE / max_exact))
                         * np.float32(N_BUCKETS - max_exact)).astype(np.int32)
    large = np.minimum(large, N_BUCKETS - 1)
    return np.where(dist < max_exact, dist, large).astype(np.int32)


def _attn_geometry(g, rows):
    run = rows // 16
    dil = DILATIONS[g]
    if dil == 16:
        bq = BLOCK
        return dict(view=(16, run), block=(None, bq), grid=(16, run // bq), index=lambda r, n: (r, n),
                    pos=np.arange(bq), bq=bq)
    if dil == 4:
        per = BLOCK // 4
        pos = (4 * np.arange(per)[None, :] + np.arange(4)[:, None]).reshape(-1)
        return dict(view=(4, 4, run), block=(4, None, per), grid=(4, run // per), index=lambda r, n: (0, r, n),
                    pos=pos, bq=BLOCK)
    per = 16
    pos = (16 * np.arange(per)[None, :] + np.arange(16)[:, None]).reshape(-1)
    return dict(view=(16, run), block=(16, per), grid=(1, run // per), index=lambda r, n: (0, n),
                pos=pos, bq=16 * per)


def _attn_tables(g, rows):
    geo = _attn_geometry(g, rows)
    pos, bq = geo["pos"], geo["bq"]
    steps = pos[:, None] - np.concatenate([pos - bq, pos])[None, :]
    valid = (steps >= 0) & (steps <= BLOCK)
    bucket = _t5_bucket_np((np.maximum(steps, 0) * DILATIONS[g]).astype(np.int32))
    return bucket, valid.astype(np.int32)


def _bias_fwd(name, bucket, valid, table_t):
    bq = bucket.shape[0]

    def body(bk_ref, ok_ref, tab_ref, o_ref):
        bk = bk_ref[...]
        ok = ok_ref[...] > 0
        for h in range(HEADS_PER_GROUP):
            acc = jnp.zeros(bk.shape, F32)
            for b in range(N_BUCKETS):
                acc = jnp.where(bk == b, tab_ref[h, b], acc)
            o_ref[h] = jnp.where(ok, acc, NEG_INF)

    vm = pl.BlockSpec(memory_space=pltpu.VMEM)
    return pl.pallas_call(
        body, name=name, in_specs=[vm, vm, pl.BlockSpec(memory_space=pltpu.SMEM)], out_specs=vm,
        out_shape=jax.ShapeDtypeStruct((HEADS_PER_GROUP, bq, 2 * bq), F32),
    )(bucket, valid, table_t)


def _bias_bwd(name, bucket, dbias):
    def body(bk_ref, db_ref, o_ref):
        row_id = lax.broadcasted_iota(jnp.int32, (N_BUCKETS, 128), 0)
        col_id = lax.broadcasted_iota(jnp.int32, (N_BUCKETS, 128), 1)
        bk = bk_ref[...]
        acc = jnp.zeros((N_BUCKETS, 128), F32)
        for h in range(HEADS_PER_GROUP):
            db = db_ref[h]
            for b in range(N_BUCKETS):
                part = jnp.sum(jnp.where(bk == b, db, 0.0), axis=0, keepdims=True)
                tot = jnp.sum(part, axis=1, keepdims=True)
                acc = jnp.where((row_id == b) & (col_id == h), tot, acc)
        o_ref[...] = acc

    vm = pl.BlockSpec(memory_space=pltpu.VMEM)
    return pl.pallas_call(body, name=name, in_specs=[vm, vm], out_specs=vm,
                          out_shape=jax.ShapeDtypeStruct((N_BUCKETS, 128), F32))(bucket, dbias)


def _head_of_lane(nrows):
    return lax.broadcasted_iota(jnp.int32, (nrows, ATTN_OUT), 1) // HEAD_DIM


def _stack_heads(a, lane_head):
    zero = jnp.zeros_like(a)
    return jnp.concatenate([jnp.where(lane_head == h, a, zero) for h in range(HEADS_PER_GROUP)], axis=0)


def _unstack_heads(a4, lane_head, bq):
    out = a4[:bq]
    for h in range(1, HEADS_PER_GROUP):
        out = jnp.where(lane_head == h, a4[h * bq:(h + 1) * bq], out)
    return out


def _attn_specs(geo, cols, col_block, index):
    return pl.BlockSpec(geo["block"] + (cols,), lambda r, n: index(r, n) + (col_block,))


def _attn_fwd(name, qkv, g, bias4):
    rows = qkv.shape[0]
    geo = _attn_geometry(g, rows)
    bq, (nsub, nb), index = geo["bq"], geo["grid"], geo["index"]
    blk_shape = tuple(b for b in geo["block"] if b is not None) + (ATTN_OUT,)

    def body(q_ref, kc_ref, kp_ref, vc_ref, vp_ref, b_ref, o_ref, lse_ref):
        n = pl.program_id(1)
        lane_head = _head_of_lane(bq)
        flat = lambda ref: ref[...].reshape(bq, ATTN_OUT)
        q4 = _stack_heads(flat(q_ref), lane_head)
        k2 = jnp.concatenate([flat(kp_ref), flat(kc_ref)], axis=0)
        v2 = jnp.concatenate([flat(vp_ref), flat(vc_ref)], axis=0)
        s = _dot_nt(q4, k2) + b_ref[...]
        col = lax.broadcasted_iota(jnp.int32, s.shape, 1)
        s = jnp.where((col >= bq) | (n > 0), s, NEG_INF)
        mx = jnp.max(s, axis=-1, keepdims=True)
        p = jnp.exp(s - mx)
        den = jnp.sum(p, axis=-1, keepdims=True)
        o4 = _dot_nn(p.astype(BF16), v2) / den
        lse4 = jnp.broadcast_to(mx + jnp.log(den), (HEADS_PER_GROUP * bq, ATTN_OUT))
        o_ref[...] = _unstack_heads(o4, lane_head, bq).reshape(blk_shape)
        lse_ref[...] = _unstack_heads(lse4, lane_head, bq).reshape(blk_shape)

    prev = lambda r, n: index(r, jnp.maximum(n - 1, 0))
    view = lambda a: a.reshape(geo["view"] + (a.shape[1],))
    qkv_v = view(qkv)
    out_spec = _attn_specs(geo, ATTN_OUT, 0, index)
    out_shape = jax.ShapeDtypeStruct(geo["view"] + (ATTN_OUT,), F32)
    o, lse = pl.pallas_call(
        body, name=name, grid=(nsub, nb),
        in_specs=[_attn_specs(geo, ATTN_OUT, g, index), _attn_specs(geo, ATTN_OUT, 3 + g, index),
                  _attn_specs(geo, ATTN_OUT, 3 + g, prev), _attn_specs(geo, ATTN_OUT, 6 + g, index),
                  _attn_specs(geo, ATTN_OUT, 6 + g, prev), pl.BlockSpec(bias4.shape, lambda r, n: (0, 0))],
        out_specs=[out_spec, out_spec], out_shape=[out_shape, out_shape],
        compiler_params=_cparams("parallel", "arbitrary"),
    )(qkv_v, qkv_v, qkv_v, qkv_v, qkv_v, bias4)
    return o.reshape(rows, ATTN_OUT), lse.reshape(rows, ATTN_OUT)


def _attn_bwd(name, qkv, do, lse, cvec, g, bias4):
    rows = qkv.shape[0]
    geo = _attn_geometry(g, rows)
    bq, (nsub, nb), index = geo["bq"], geo["grid"], geo["index"]
    blk_shape = tuple(b for b in geo["block"] if b is not None) + (ATTN_OUT,)
    nlead = len(blk_shape) - 1

    def body(q_ref, kc_ref, kp_ref, vc_ref, vp_ref, do_ref, lse_ref, c_ref, b_ref,
             dq_ref, dk_ref, dv_ref, db_ref, kcar_ref, vcar_ref):
        r, n = pl.program_id(0), pl.program_id(1)
        valid = n < nb
        lane_head = _head_of_lane(bq)
        flat = lambda ref: ref[...].reshape(bq, ATTN_OUT)

        @pl.when((r == 0) & (n == 0))
        def _():
            kcar_ref[...] = jnp.zeros_like(kcar_ref)
            vcar_ref[...] = jnp.zeros_like(vcar_ref)
            db_ref[...] = jnp.zeros_like(db_ref)

        def column(ref, h):
            lead = (slice(None),) * nlead
            return ref[lead + (pl.ds(h * HEAD_DIM, 1),)].reshape(bq, 1)

        q4 = _stack_heads(flat(q_ref), lane_head)
        do4 = _stack_heads(flat(do_ref), lane_head)
        k2 = jnp.concatenate([flat(kp_ref), flat(kc_ref)], axis=0)
        v2 = jnp.concatenate([flat(vp_ref), flat(vc_ref)], axis=0)
        lse4 = jnp.concatenate([column(lse_ref, h) for h in range(HEADS_PER_GROUP)], axis=0)
        c4 = jnp.concatenate([column(c_ref, h) for h in range(HEADS_PER_GROUP)], axis=0)
        s = _dot_nt(q4, k2) + b_ref[...]
        col = lax.broadcasted_iota(jnp.int32, s.shape, 1)
        keep = ((col >= bq) | (n > 0)) & valid
        p = jnp.where(keep, jnp.exp(s - lse4), 0.0)
        ds = p * (_dot_nt(do4, v2) + c4)
        ds_b = ds.astype(BF16)

        @pl.when(valid)
        def _():
            dq = _unstack_heads(_dot_nn(ds_b, k2), lane_head, bq) * (HEAD_DIM ** -0.5)
            dq_ref[...] = dq.astype(BF16).reshape(blk_shape)

        dk2 = _dot_tn(ds_b, q4)
        dv2 = _dot_tn(p.astype(BF16), do4)
        dk_ref[...] = (kcar_ref[...] + dk2[:bq]).astype(BF16).reshape(blk_shape)
        dv_ref[...] = (vcar_ref[...] + dv2[:bq]).astype(BF16).reshape(blk_shape)
        kcar_ref[...] = dk2[bq:]
        vcar_ref[...] = dv2[bq:]
        db_ref[...] += ds

    cur = lambda r, n: index(r, jnp.minimum(n, nb - 1))
    prev = lambda r, n: index(r, jnp.maximum(jnp.minimum(n, nb - 1) - 1, 0))
    late = lambda r, n: index(r, jnp.maximum(n - 1, 0))
    view = lambda a: a.reshape(geo["view"] + (a.shape[1],))
    qkv_v = view(qkv)
    tile = _attn_specs(geo, ATTN_OUT, 0, cur)
    bias_spec = pl.BlockSpec(bias4.shape, lambda r, n: (0, 0))
    out_shape = jax.ShapeDtypeStruct(geo["view"] + (ATTN_OUT,), BF16)
    dq, dk, dv, db = pl.pallas_call(
        body, name=name, grid=(nsub, nb + 1),
        in_specs=[_attn_specs(geo, ATTN_OUT, g, cur), _attn_specs(geo, ATTN_OUT, 3 + g, cur),
                  _attn_specs(geo, ATTN_OUT, 3 + g, prev), _attn_specs(geo, ATTN_OUT, 6 + g, cur),
                  _attn_specs(geo, ATTN_OUT, 6 + g, prev), tile, tile, tile, bias_spec],
        out_specs=[tile, _attn_specs(geo, ATTN_OUT, 0, late), _attn_specs(geo, ATTN_OUT, 0, late), bias_spec],
        out_shape=[out_shape, out_shape, out_shape, jax.ShapeDtypeStruct(bias4.shape, F32)],
        scratch_shapes=[pltpu.VMEM((bq, ATTN_OUT), F32), pltpu.VMEM((bq, ATTN_OUT), F32)],
        compiler_params=_cparams("arbitrary", "arbitrary"),
    )(qkv_v, qkv_v, qkv_v, qkv_v, qkv_v, view(do), view(lse), view(cvec), bias4)
    return dq.reshape(rows, ATTN_OUT), dk.reshape(rows, ATTN_OUT), dv.reshape(rows, ATTN_OUT), db


def _group_weights(lses):
    mx = jnp.maximum(jnp.maximum(lses[0], lses[1]), lses[2])
    es = [jnp.exp(l - mx) for l in lses]
    den = es[0] + es[1] + es[2]
    return [e / den for e in es]


def _combine_fwd(name, os_, lses):
    def fn(o0, o1, o2, l0, l1, l2):
        ws = _group_weights([l0, l1, l2])
        out = ws[0] * o0 + ws[1] * o1 + ws[2] * o2
        return out, out

    return _ew(name, fn, [*os_, *lses], [ATTN_OUT, ATTN_OUT], [F32, BF16], tm=1024)


def _combine_bwd(name, do, oa, lses):
    def fn(dov, oav, l0, l1, l2):
        head_sum = (lax.broadcasted_iota(jnp.int32, (ATTN_OUT, ATTN_OUT), 0) // HEAD_DIM
                    == lax.broadcasted_iota(jnp.int32, (ATTN_OUT, ATTN_OUT), 1) // HEAD_DIM)
        ws = _group_weights([l0, l1, l2])
        prod = dov * oav
        hi = prod.astype(BF16)
        lo = (prod - hi.astype(F32)).astype(BF16)
        ones = jnp.where(head_sum, 1.0, 0.0).astype(BF16)
        bar = _dot_nn(hi, ones) + _dot_nn(lo, ones)
        return tuple(w * dov for w in ws) + tuple(-w * bar for w in ws)

    return _ew(name, fn, [do, oa, *lses], [ATTN_OUT] * 6, [BF16] * 3 + [F32] * 3, tm=1024)


def _ssm_disc(a_re, a_im, log_dt, b_re, b_im):
    dt = jnp.exp(log_dt)
    mag = jnp.exp(a_re * dt)
    ab_re = mag * jnp.cos(a_im * dt)
    ab_im = mag * jnp.sin(a_im * dt)
    den = a_re * a_re + a_im * a_im
    xr = ab_re - 1.0
    coef_re = (xr * a_re + ab_im * a_im) / den
    coef_im = (ab_im * a_re - xr * a_im) / den
    bb_re = coef_re[None] * b_re - coef_im[None] * b_im
    bb_im = coef_re[None] * b_im + coef_im[None] * b_re
    return ab_re, ab_im, bb_re, bb_im


def _cpow2(re, im, times):
    for _ in range(times):
        re, im = re * re - im * im, 2.0 * re * im
    return re, im


def _ssm_params_fwd(name, a_re, a_im, log_dt, b_re, b_im):
    gn = jax.ShapeDtypeStruct(a_re.shape, F32)
    cgn = jax.ShapeDtypeStruct(b_re.shape, F32)

    def body(ar, ai, ld, br, bi, o_abr, o_abi, o_apr, o_api, o_bbr, o_bbi):
        ab_re, ab_im, bb_re, bb_im = _ssm_disc(ar[...], ai[...], ld[...], br[...], bi[...])
        o_abr[...] = ab_re
        o_abi[...] = ab_im
        pr, pi = _cpow2(ab_re, ab_im, int(math.log2(SCAN_STEPS)))
        o_apr[...] = pr
        o_api[...] = pi
        o_bbr[...] = bb_re
        o_bbi[...] = bb_im

    vm = pl.BlockSpec(memory_space=pltpu.VMEM)
    return pl.pallas_call(body, name=name, in_specs=[vm] * 5, out_specs=[vm] * 6,
                          out_shape=[gn, gn, gn, gn, cgn, cgn])(a_re, a_im, log_dt, b_re, b_im)


def _ssm_params_bwd(name, a_re, a_im, log_dt, b_re, b_im, d_ab_re, d_ab_im, d_bb_re, d_bb_im):
    gn = jax.ShapeDtypeStruct(a_re.shape, F32)
    cgn = jax.ShapeDtypeStruct(b_re.shape, F32)

    def body(ar, ai, ld, br, bi, g0, g1, g2, g3, o_ar, o_ai, o_ld, o_br, o_bi):
        _, vjp = jax.vjp(_ssm_disc, ar[...], ai[...], ld[...], br[...], bi[...])
        outs = vjp((g0[...], g1[...], g2[...], g3[...]))
        for o_ref, o in zip((o_ar, o_ai, o_ld, o_br, o_bi), outs):
            o_ref[...] = o

    vm = pl.BlockSpec(memory_space=pltpu.VMEM)
    return pl.pallas_call(body, name=name, in_specs=[vm] * 9, out_specs=[vm] * 5,
                          out_shape=[gn, gn, jax.ShapeDtypeStruct(log_dt.shape, F32), cgn, cgn],
                          )(a_re, a_im, log_dt, b_re, b_im, d_ab_re, d_ab_im, d_bb_re, d_bb_im)


def _scan_block(s_ref, carry_ref, tmp_ref, ab_ref, ap_ref, reverse, sprev=None):
    nl = SSM_LANES
    halves = range(SCAN_COLS // SCAN_SUB)
    zero = jnp.zeros((SCAN_SUB, SCAN_LANES), F32)
    for half in (reversed(halves) if reverse else halves):
        sub_rows = pl.ds(half * SCAN_SUB, SCAN_SUB)
        for lc in range(nl // SCAN_LANES):
            re_l = pl.ds(lc * SCAN_LANES, SCAN_LANES)
            im_l = pl.ds(nl + lc * SCAN_LANES, SCAN_LANES)
            are, aim = ab_ref[:, re_l], ab_ref[:, im_l]

            def step_of(j):
                return SCAN_STEPS - 1 - j if reverse else j

            def pass1(j, st):
                sr, si = st
                jj = step_of(j)
                nr = are * sr - aim * si + s_ref[jj, sub_rows, re_l]
                ni = are * si + aim * sr + s_ref[jj, sub_rows, im_l]
                s_ref[jj, sub_rows, re_l] = nr
                s_ref[jj, sub_rows, im_l] = ni
                return nr, ni

            er, ei = lax.fori_loop(0, SCAN_STEPS, pass1, (zero, zero), unroll=2)
            tmp_ref[0:SCAN_SUB, re_l] = er
            tmp_ref[0:SCAN_SUB, im_l] = ei
            apr, api = ap_ref[0:1, re_l], ap_ref[0:1, im_l]
            sr, si = carry_ref[0:1, re_l], carry_ref[0:1, im_l]
            for step in range(SCAN_SUB):
                c = SCAN_SUB - 1 - step if reverse else step
                tmp_ref[SCAN_SUB + c:SCAN_SUB + c + 1, re_l] = sr
                tmp_ref[SCAN_SUB + c:SCAN_SUB + c + 1, im_l] = si
                e_r, e_i = tmp_ref[c:c + 1, re_l], tmp_ref[c:c + 1, im_l]
                sr, si = apr * sr - api * si + e_r, apr * si + api * sr + e_i
            carry_ref[0:1, re_l] = sr
            carry_ref[0:1, im_l] = si
            cr = tmp_ref[SCAN_SUB:2 * SCAN_SUB, re_l]
            ci = tmp_ref[SCAN_SUB:2 * SCAN_SUB, im_l]

            if sprev is None:
                def pass2(j, st):
                    pr, pi = st
                    jj = step_of(j)
                    s_ref[jj, sub_rows, re_l] += pr * cr - pi * ci
                    s_ref[jj, sub_rows, im_l] += pr * ci + pi * cr
                    return pr * are - pi * aim, pr * aim + pi * are

                lax.fori_loop(0, SCAN_STEPS, pass2, (are, aim), unroll=2)
            else:
                st_ref, prev_ref, have_prev, dab_ref = sprev

                def corrected(jj, pr, pi):
                    gr = s_ref[jj, sub_rows, re_l] + pr * cr - pi * ci
                    gi = s_ref[jj, sub_rows, im_l] + pr * ci + pi * cr
                    s_ref[jj, sub_rows, re_l] = gr
                    s_ref[jj, sub_rows, im_l] = gi
                    return gr, gi

                def pass2(j, st):
                    pr, pi, dr, di = st
                    jj = SCAN_STEPS - 1 - j
                    gr, gi = corrected(jj, pr, pi)
                    qr, qi = st_ref[jj - 1, sub_rows, re_l], st_ref[jj - 1, sub_rows, im_l]
                    return (pr * are - pi * aim, pr * aim + pi * are,
                            dr + gr * qr + gi * qi, di + gi * qr - gr * qi)

                pr, pi, dr, di = lax.fori_loop(0, SCAN_STEPS - 1, pass2, (are, aim, zero, zero), unroll=2)
                gr, gi = corrected(0, pr, pi)
                sub = lax.broadcasted_iota(jnp.int32, (SCAN_SUB, SCAN_LANES), 0)
                if half == 0:
                    pv_r = prev_ref[SCAN_SUB - 1:SCAN_SUB, re_l] * have_prev
                    pv_i = prev_ref[SCAN_SUB - 1:SCAN_SUB, im_l] * have_prev
                else:
                    before = pl.ds(half * SCAN_SUB - 1, 1)
                    pv_r, pv_i = st_ref[SCAN_STEPS - 1, before, re_l], st_ref[SCAN_STEPS - 1, before, im_l]
                shape = (SCAN_SUB, SCAN_LANES)
                qr = jnp.where(sub == 0, jnp.broadcast_to(pv_r, shape),
                               pltpu.roll(st_ref[SCAN_STEPS - 1, sub_rows, re_l], 1, 0))
                qi = jnp.where(sub == 0, jnp.broadcast_to(pv_i, shape),
                               pltpu.roll(st_ref[SCAN_STEPS - 1, sub_rows, im_l], 1, 0))
                dab_ref[:, re_l] += dr + gr * qr + gi * qi
                dab_ref[:, im_l] += di + gi * qr - gr * qi


def _scan_view(a):
    return a.reshape(16, a.shape[0] // 16, a.shape[1])


def _ssm_fwd(name, u, bb_mat, c_mat_t, ab_rows, ap_rows, d_skip):
    rows = u.shape[0]
    nl2 = 2 * SSM_LANES
    nblk = rows // SCAN_BLOCK

    def body(u_ref, bb_ref, c_ref, ab_ref, ap_ref, d_ref, y_ref, s_ref, carry_ref, tmp_ref):
        @pl.when(pl.program_id(0) == 0)
        def _():
            carry_ref[...] = jnp.zeros_like(carry_ref)

        uv = u_ref[...].reshape(SCAN_BLOCK, SSM_WIDTH)
        s_ref[...] = _dot_nn(uv.astype(BF16), bb_ref[...]).reshape(16, SCAN_COLS, nl2)
        _scan_block(s_ref, carry_ref, tmp_ref, ab_ref, ap_ref, reverse=False)
        sv = s_ref[...].reshape(SCAN_BLOCK, nl2)
        y_ref[...] = (_dot_nt(sv.astype(BF16), c_ref[...]) + d_ref[...] * uv).reshape(16, SCAN_COLS, SSM_WIDTH)

    const = lambda shape: pl.BlockSpec(shape, lambda i: (0, 0))
    blk = lambda cols: pl.BlockSpec((16, SCAN_COLS, cols), lambda i: (0, i, 0))
    y, s = pl.pallas_call(
        body, name=name, grid=(nblk,),
        in_specs=[blk(SSM_WIDTH), const((SSM_WIDTH, nl2)), const((SSM_WIDTH, nl2)), const((SCAN_SUB, nl2)),
                  const((SCAN_SUB, nl2)), const((1, SSM_WIDTH))],
        out_specs=[blk(SSM_WIDTH), blk(nl2)],
        out_shape=[jax.ShapeDtypeStruct((16, rows // 16, SSM_WIDTH), F32),
                   jax.ShapeDtypeStruct((16, rows // 16, nl2), F32)],
        scratch_shapes=[pltpu.VMEM((SCAN_SUB, nl2), F32), pltpu.VMEM((2 * SCAN_SUB, nl2), F32)],
        compiler_params=_cparams("arbitrary"),
    )(_scan_view(u), bb_mat, c_mat_t, ab_rows, ap_rows, d_skip)
    return y.reshape(rows, SSM_WIDTH), s.reshape(rows, nl2)


def _ssm_bwd(name, dy, u, states, bb_mat, c_mat_t, abc_rows, apc_rows, d_skip):
    rows = u.shape[0]
    nl2 = 2 * SSM_LANES
    nblk = rows // SCAN_BLOCK

    def body(dy_ref, u_ref, st_ref, prev_ref, bb_ref, ct_ref, ab_ref, ap_ref, d_ref,
             du_ref, g_ref, dab_ref, dd_ref, carry_ref, tmp_ref):
        i = pl.program_id(0)

        @pl.when(i == 0)
        def _():
            carry_ref[...] = jnp.zeros_like(carry_ref)
            dab_ref[...] = jnp.zeros_like(dab_ref)
            dd_ref[...] = jnp.zeros_like(dd_ref)

        dyv = dy_ref[...].reshape(SCAN_BLOCK, SSM_WIDTH)
        g_ref[...] = _dot_nn(dyv.astype(BF16), ct_ref[...]).reshape(16, SCAN_COLS, nl2)
        have_prev = (i < nblk - 1).astype(F32)
        _scan_block(g_ref, carry_ref, tmp_ref, ab_ref, ap_ref, reverse=True,
                    sprev=(st_ref, prev_ref, have_prev, dab_ref))
        gv = g_ref[...].reshape(SCAN_BLOCK, nl2)
        du_ref[...] = (_dot_nt(gv.astype(BF16), bb_ref[...]) + d_ref[...] * dyv).reshape(16, SCAN_COLS, SSM_WIDTH)
        dd_ref[...] += jnp.sum(dyv * u_ref[...].reshape(SCAN_BLOCK, SSM_WIDTH), axis=0, keepdims=True)

    const = lambda shape: pl.BlockSpec(shape, lambda i: (0, 0))
    blk = lambda cols: pl.BlockSpec((16, SCAN_COLS, cols), lambda i: (0, nblk - 1 - i, 0))
    per8 = SCAN_COLS // SCAN_SUB
    prev_spec = pl.BlockSpec((None, SCAN_SUB, nl2), lambda i: (15, jnp.maximum((nblk - 1 - i) * per8 - 1, 0), 0))
    sv = _scan_view(states)
    du, g, dab, dd = pl.pallas_call(
        body, name=name, grid=(nblk,),
        in_specs=[blk(SSM_WIDTH), blk(SSM_WIDTH), blk(nl2), prev_spec, const((SSM_WIDTH, nl2)),
                  const((SSM_WIDTH, nl2)), const((SCAN_SUB, nl2)), const((SCAN_SUB, nl2)), const((1, SSM_WIDTH))],
        out_specs=[blk(SSM_WIDTH), blk(nl2), const((SCAN_SUB, nl2)), const((1, SSM_WIDTH))],
        out_shape=[jax.ShapeDtypeStruct((16, rows // 16, SSM_WIDTH), F32),
                   jax.ShapeDtypeStruct((16, rows // 16, nl2), F32),
                   jax.ShapeDtypeStruct((SCAN_SUB, nl2), F32), jax.ShapeDtypeStruct((1, SSM_WIDTH), F32)],
        scratch_shapes=[pltpu.VMEM((SCAN_SUB, nl2), F32), pltpu.VMEM((2 * SCAN_SUB, nl2), F32)],
        compiler_params=_cparams("arbitrary"),
    )(_scan_view(dy), _scan_view(u), sv, sv, bb_mat, c_mat_t, abc_rows, apc_rows, d_skip)
    return du.reshape(rows, SSM_WIDTH), g.reshape(rows, nl2), dab, dd


def _adamw(name, w, m, v, gparts, tr):
    rows, cols = w.shape

    def body(w_ref, m_ref, v_ref, g_ref, og_ref, od_ref, om_ref, ov_ref):
        g = g_ref[0].astype(F32)
        for i in range(1, N_DEV):
            g = g + g_ref[i].astype(F32)
        m_new = B1 * m_ref[...] + (1.0 - B1) * g
        v_new = B2 * v_ref[...] + (1.0 - B2) * (g * g)
        m_hat = m_new / (1.0 - B1 ** STEP)
        v_hat = v_new / (1.0 - B2 ** STEP)
        og_ref[...] = g
        od_ref[...] = -LR * (m_hat / (jnp.sqrt(v_hat) + ADAM_EPS) + WD * w_ref[...])
        om_ref[...] = m_new
        ov_ref[...] = v_new

    spec = pl.BlockSpec((tr, cols), lambda i: (i, 0))
    shape = jax.ShapeDtypeStruct((rows, cols), F32)
    return pl.pallas_call(
        body, name=name, grid=(rows // tr,),
        in_specs=[spec, spec, spec, pl.BlockSpec((N_DEV, tr, cols), lambda i: (0, i, 0))],
        out_specs=[spec] * 4, out_shape=[shape] * 4,
        compiler_params=_cparams("parallel"),
    )(w, m, v, gparts)


_SHARDED = (
    ("ffn1_w_gate", True, (352, 1024)), ("ffn1_w_up", True, (352, 1024)), ("ffn1_w_down", False, (352, 1024)),
    ("w_in", True, (608, 1024)), ("ssm_w_glu", True, (128, 512)), ("w_attn_branch", True, (128, 256)),
    ("w_ssm_branch", True, (128, 512)), ("w_out", False, (128, 1024)),
    ("ffn2_w_gate", True, (352, 1024)), ("ffn2_w_up", True, (352, 1024)), ("ffn2_w_down", False, (352, 1024)),
)
_SMALL = ("ffn1_norm", "mix_norm", "gate_bias", "rel_bias_table", "ssm_a_re", "ssm_a_im", "ssm_log_dt",
          "ssm_b_re", "ssm_b_im", "ssm_c_re", "ssm_c_im", "ssm_d", "ffn2_norm", "final_norm")
_ORDER = ("ffn1_norm", "ffn1_w_gate", "ffn1_w_up", "ffn1_w_down", "mix_norm", "w_in", "gate_bias",
          "rel_bias_table", "ssm_a_re", "ssm_a_im", "ssm_log_dt", "ssm_b_re", "ssm_b_im", "ssm_c_re",
          "ssm_c_im", "ssm_d", "ssm_w_glu", "w_attn_branch", "w_ssm_branch", "w_out", "ffn2_norm",
          "ffn2_w_gate", "ffn2_w_up", "ffn2_w_down", "final_norm")


def _pack_rows(shape):
    return shape[0] * shape[1] // D_MODEL


_SHARD_INFO = {nm: (tr, shape) for nm, tr, shape in _SHARDED}
_PHASES = {
    "f1gu": ("ffn1_w_gate", "ffn1_w_up"), "f1d": ("ffn1_w_down",),
    "mix": ("w_in", "ssm_w_glu", "w_attn_branch", "w_ssm_branch", "w_out"),
    "f2": ("ffn2_w_gate", "ffn2_w_up", "ffn2_w_down"),
}


def _to_rows(a, nm):
    tr, shape = _SHARD_INFO[nm]
    return (a.T if tr else a).reshape(_pack_rows(shape), D_MODEL)


def _from_rows(p, nm):
    tr, shape = _SHARD_INFO[nm]
    a = p.reshape(shape)
    return a.T if tr else a


def _full_weight(gathered, nm):
    _, shape = _SHARD_INFO[nm]
    return gathered.reshape(N_DEV * shape[0], shape[1])


def _grad_blocks(g, nm):
    _, shape = _SHARD_INFO[nm]
    return g.astype(BF16).reshape(N_DEV, _pack_rows(shape), D_MODEL)


_SMALL_TILE = 8 * 128


def _small_rows(a):
    flat = a.reshape(-1)
    return jnp.pad(flat, (0, (-flat.shape[0]) % _SMALL_TILE)).reshape(-1, 128)


def _pack_small(ws):
    return jnp.concatenate([_small_rows(ws[nm]) for nm in _SMALL], axis=0)


def _unpack_small(pack, like):
    out, r0 = {}, 0
    for nm in _SMALL:
        n = like[nm].size
        nr = 8 * -(-n // _SMALL_TILE)
        out[nm] = pack[r0:r0 + nr].reshape(-1)[:n].reshape(like[nm].shape)
        r0 += nr
    return out


def _residue_order(a):
    rows, cols = a.shape
    return a.reshape(rows // 16, 16, cols).transpose(1, 0, 2).reshape(rows, cols)


def _token_order(a):
    rows, cols = a.shape
    return a.reshape(16, rows // 16, cols).transpose(1, 0, 2).reshape(rows, cols)


def _block_diag_pair(re, im):
    g, a, b = re.shape
    eye = jnp.eye(g, dtype=re.dtype)
    return jnp.einsum("xgab,gh->gaxhb", jnp.stack([re, im]), eye).reshape(g * a, 2 * g * b).astype(BF16)


def _diag_blocks(mat, a, b):
    g = mat.shape[0] // a
    eye = jnp.eye(g, dtype=mat.dtype)
    return jnp.einsum("gahb,gh->gab", mat.reshape(g, a, g, b), eye)


def _local_step(xs, target, small, weights_of, send_grads, first_deps=()):
    rows = xs.shape[0]
    gfull, gsmall = {}, {}
    wf = dict(weights_of("f1", None))

    x1, h1, gg1, uu1 = _ffn_fwd("ffn1_fwd", xs, small["ffn1_norm"], wf["ffn1_w_gate"], wf["ffn1_w_up"],
                                wf["ffn1_w_down"], deps=first_deps)
    wf.update(weights_of("mix", x1))
    hmix = _rms_fwd("mix_norm_fwd", x1, small["mix_norm"])
    w_in = wf["w_in"]
    w_qkv, w_u, w_g = w_in[:3 * ATTN_WIDTH], w_in[3 * ATTN_WIDTH:3 * ATTN_WIDTH + SSM_WIDTH], w_in[3 * ATTN_WIDTH + SSM_WIDTH:]
    qscale = jnp.concatenate([jnp.full((1, ATTN_WIDTH), HEAD_DIM ** -0.5, F32), jnp.ones((1, 2 * ATTN_WIDTH), F32)], axis=1)
    qkv, = _mm("in_qkv", [(hmix, w_qkv)], True, 3 * ATTN_WIDTH, [BF16],
               epilogue=lambda acc, sc: (acc * sc,), extras=[(qscale, 0)], tn=ATTN_WIDTH)
    u, = _mm("in_u", [(hmix, w_u)], True, SSM_WIDTH, [F32])
    gates, = _mm("in_gates", [(hmix, w_g)], True, 2 * D_MODEL, [F32],
                 epilogue=lambda acc, b: (_sigmoid(acc + b),), extras=[(small["gate_bias"], 0)])

    table_t = small["rel_bias_table"].T
    tables, bias4, o_g, lse_g = [], [], [], []
    for g in range(N_GROUPS):
        bucket, valid = [jnp.asarray(t) for t in _attn_tables(g, rows)]
        bias_g = _bias_fwd(f"rel_bias_fwd_{g}", bucket, valid, table_t[g * HEADS_PER_GROUP:(g + 1) * HEADS_PER_GROUP])
        tables.append(bucket)
        bias4.append(bias_g.reshape(-1, bias_g.shape[-1]))
        o, lse = _attn_fwd(f"attn_fwd_{g}", qkv, g, bias4[g])
        o_g.append(o)
        lse_g.append(lse)
    oa_f32, oa = _combine_fwd("attn_combine_fwd", o_g, lse_g)
    y_attn, = _mm("attn_branch", [(oa, wf["w_attn_branch"])], True, D_MODEL, [F32])

    ab_re, ab_im, ap_re, ap_im, bb_re, bb_im = _ssm_params_fwd(
        "ssm_params_fwd", small["ssm_a_re"], small["ssm_a_im"], small["ssm_log_dt"].reshape(SSM_GROUPS, 1),
        small["ssm_b_re"].transpose(2, 0, 1), small["ssm_b_im"].transpose(2, 0, 1))

    def lanes(re, im, sign=1.0):
        row = jnp.concatenate([re.reshape(1, SSM_LANES), sign * im.reshape(1, SSM_LANES)], axis=1)
        return jnp.broadcast_to(row, (SCAN_SUB, 2 * SSM_LANES))

    bb_mat = _block_diag_pair(bb_re.transpose(1, 0, 2), bb_im.transpose(1, 0, 2))
    c_mat_t = _block_diag_pair(small["ssm_c_re"], -small["ssm_c_im"])
    d_skip = small["ssm_d"].reshape(1, SSM_WIDTH)
    y_raw, states = _ssm_fwd("ssm_fwd", u, bb_mat, c_mat_t, lanes(ab_re, ab_im), lanes(ap_re, ap_im), d_skip)

    def gelu_fn(yv):
        return (jax.nn.gelu(yv),)

    ygelu, = _ew("ssm_gelu", gelu_fn, [y_raw], [SSM_WIDTH], [BF16])
    glu, = _mm("ssm_glu", [(ygelu, wf["ssm_w_glu"])], True, 2 * SSM_WIDTH, [F32])
    ysg, = _ew("ssm_glu_act", lambda gv: (gv[:, :SSM_WIDTH] * _sigmoid(gv[:, SSM_WIDTH:]),), [glu], [SSM_WIDTH], [BF16])
    y_ssm, merged = _mm("ssm_branch_merge", [(ysg, wf["w_ssm_branch"])], True, D_MODEL, [F32, BF16],
                        epilogue=lambda acc, ga, gs, ya: (acc, ga * ya + gs * acc),
                        extras=[(gates, 0), (gates, D_MODEL), (y_attn, 0)])
    x2, = _mm("mix_out", [(merged, wf["w_out"])], False, D_MODEL, [F32],
              epilogue=lambda acc, res: (res + acc,), extras=[(x1, 0)])
    wf.update(weights_of("f2", x2))
    x3, h2, gg2, uu2 = _ffn_fwd("ffn2_fwd", x2, small["ffn2_norm"], wf["ffn2_w_gate"], wf["ffn2_w_up"],
                                wf["ffn2_w_down"])
    dx3, gsmall["final_norm"], loss = _final_loss("final_loss", x3, small["final_norm"].reshape(1, D_MODEL), target)

    dx2, dgg2, duu2, act2, gsmall["ffn2_norm"] = _ffn_bwd(
        "ffn2_bwd", dx3, x2, small["ffn2_norm"], gg2, uu2, wf["ffn2_w_gate"], wf["ffn2_w_up"], wf["ffn2_w_down"])
    gfull["ffn2_w_gate"] = _mm_tn("ffn2_dwg", dgg2, h2, out_dtype=BF16)
    gfull["ffn2_w_up"] = _mm_tn("ffn2_dwu", duu2, h2, out_dtype=BF16)
    gfull["ffn2_w_down"] = _mm_tn("ffn2_dwd", act2, dx3, scale=0.5, out_dtype=BF16)
    sent = send_grads("f2", gfull)

    def merge_bwd(dm, ga, gs, ya, ys):
        return (dm * ga, dm * gs, dm * ya * ga * (1.0 - ga), dm * ys * gs * (1.0 - gs))

    dya, dys, dzga, dzgs = _mm("mix_out_bwd", [(dx2, wf["w_out"])], True, D_MODEL, [BF16] * 4, epilogue=merge_bwd,
                               extras=[(gates, 0), (gates, D_MODEL), (y_attn, 0), (y_ssm, 0)], deps=sent)
    gfull["w_out"] = _mm_tn("dw_out", merged, dx2, out_dtype=BF16)
    gsmall["gate_bias"] = jnp.concatenate([_colsum("dgate_bias_a", dzga), _colsum("dgate_bias_s", dzgs)], axis=1)

    gfull["w_ssm_branch"] = _mm_tn("dw_ssm_branch", dys, ysg, out_dtype=BF16)

    def glu_bwd(dysg, av, bv):
        sb = _sigmoid(bv)
        return (dysg * sb, dysg * av * sb * (1.0 - sb))

    dglu_a, dglu_b = _mm("ssm_branch_bwd", [(dys, wf["w_ssm_branch"])], False, SSM_WIDTH, [BF16, BF16],
                         epilogue=glu_bwd, extras=[(glu, 0), (glu, SSM_WIDTH)])
    w_glu = wf["ssm_w_glu"]
    gfull["ssm_w_glu"] = _mm_tn_stack("dw_glu", [dglu_a, dglu_b], ygelu, out_dtype=BF16)

    def gelu_bwd(acc, yv):
        _, vjp = jax.vjp(jax.nn.gelu, yv)
        return (vjp(acc)[0],)

    dy_raw, = _mm("ssm_glu_bwd", [(dglu_a, w_glu[:SSM_WIDTH]), (dglu_b, w_glu[SSM_WIDTH:])], False, SSM_WIDTH, [F32],
                  epilogue=gelu_bwd, extras=[(y_raw, 0)])
    du, g_states, dab_rows, gsmall_d = _ssm_bwd(
        "ssm_bwd", dy_raw, u, states, bb_mat, c_mat_t, lanes(ab_re, ab_im, -1.0), lanes(ap_re, ap_im, -1.0), d_skip)
    gsmall["ssm_d"] = gsmall_d
    dbb_acc = _mm_tn("ssm_dbb", u, g_states, bm=SSM_WIDTH, tk=512)
    dc_acc = _mm_tn("ssm_dc", dy_raw, states, bm=SSM_WIDTH, tk=512)
    dbb_re = _diag_blocks(dbb_acc[:, :SSM_LANES], SSM_GROUP, SSM_STATE).transpose(1, 0, 2)
    dbb_im = _diag_blocks(dbb_acc[:, SSM_LANES:], SSM_GROUP, SSM_STATE).transpose(1, 0, 2)
    gsmall["ssm_c_re"] = _diag_blocks(dc_acc[:, :SSM_LANES], SSM_GROUP, SSM_STATE)
    gsmall["ssm_c_im"] = -_diag_blocks(dc_acc[:, SSM_LANES:], SSM_GROUP, SSM_STATE)
    dab = _colsum("ssm_dab", dab_rows)
    d_ar, d_ai, d_ld, d_br, d_bi = _ssm_params_bwd(
        "ssm_params_bwd", small["ssm_a_re"], small["ssm_a_im"], small["ssm_log_dt"].reshape(SSM_GROUPS, 1),
        small["ssm_b_re"].transpose(2, 0, 1), small["ssm_b_im"].transpose(2, 0, 1),
        dab[:, :SSM_LANES].reshape(SSM_GROUPS, SSM_STATE), dab[:, SSM_LANES:].reshape(SSM_GROUPS, SSM_STATE),
        dbb_re, dbb_im)
    gsmall["ssm_a_re"], gsmall["ssm_a_im"], gsmall["ssm_log_dt"] = d_ar, d_ai, d_ld.reshape(SSM_GROUPS)
    gsmall["ssm_b_re"], gsmall["ssm_b_im"] = d_br.transpose(1, 2, 0), d_bi.transpose(1, 2, 0)

    gfull["w_attn_branch"] = _mm_tn("dw_attn_branch", dya, oa, out_dtype=BF16)
    doa, = _mm("attn_branch_bwd", [(dya, wf["w_attn_branch"])], False, ATTN_OUT, [F32])
    dc = _combine_bwd("attn_combine_bwd", doa, oa_f32, lse_g)
    dqkv_cols = [None] * 9
    dtable = []
    for g in range(N_GROUPS):
        dq, dk, dv, db = _attn_bwd(f"attn_bwd_{g}", qkv, dc[g], lse_g[g], dc[3 + g], g, bias4[g])
        dqkv_cols[g], dqkv_cols[3 + g], dqkv_cols[6 + g] = dq, dk, dv
        dt = _bias_bwd(f"rel_bias_bwd_{g}", tables[g], db.reshape(HEADS_PER_GROUP, -1, db.shape[-1]))
        dtable.append(dt[:, :HEADS_PER_GROUP])
    gsmall["rel_bias_table"] = jnp.concatenate(dtable, axis=1)

    gfull["w_in"] = jnp.concatenate([_mm_tn_stack("dw_in_qkv", dqkv_cols, hmix, out_dtype=BF16),
                                     _mm_tn_stack("dw_in_rest", [du, dzga, dzgs], hmix, out_dtype=BF16)], axis=0)
    sent = send_grads("mix", gfull)
    qkv_pairs = [(c, w_qkv[i * ATTN_OUT:(i + 1) * ATTN_OUT]) for i, c in enumerate(dqkv_cols)]
    dhmix, = _mm("in_bwd", qkv_pairs + [(du, w_u), (dzga, w_g[:D_MODEL]), (dzgs, w_g[D_MODEL:])], False, D_MODEL,
                 [F32], tm=512, deps=sent)
    dx1, gsmall["mix_norm"] = _rms_bwd("mix_norm_bwd", dhmix, x1, small["mix_norm"], dx2)

    dx, dgg1, duu1, act1, gsmall["ffn1_norm"] = _ffn_bwd(
        "ffn1_bwd", dx1, xs, small["ffn1_norm"], gg1, uu1, wf["ffn1_w_gate"], wf["ffn1_w_up"], wf["ffn1_w_down"])
    sent = send_grads("small", gsmall)
    gfull["ffn1_w_down"] = _mm_tn("ffn1_dwd", act1, dx1, scale=0.5, deps=sent, out_dtype=BF16)
    sent = send_grads("f1d", gfull)
    gfull["ffn1_w_gate"] = _mm_tn("ffn1_dwg", dgg1, h1, deps=sent, out_dtype=BF16)
    gfull["ffn1_w_up"] = _mm_tn("ffn1_dwu", duu1, h1, out_dtype=BF16)
    send_grads("f1gu", gfull)
    return loss[0, 0], dx, gsmall


def kernel(x, ffn1_norm, ffn1_w_gate, ffn1_w_up, ffn1_w_down, mix_norm, w_in, gate_bias, rel_bias_table, ssm_a_re, ssm_a_im, ssm_log_dt, ssm_b_re, ssm_b_im, ssm_c_re, ssm_c_im, ssm_d, ssm_w_glu, w_attn_branch, w_ssm_branch, w_out, ffn2_norm, ffn2_w_gate, ffn2_w_up, ffn2_w_down, final_norm, loss_target, m_ffn1_norm, m_ffn1_w_gate, m_ffn1_w_up, m_ffn1_w_down, m_mix_norm, m_w_in, m_gate_bias, m_rel_bias_table, m_ssm_a_re, m_ssm_a_im, m_ssm_log_dt, m_ssm_b_re, m_ssm_b_im, m_ssm_c_re, m_ssm_c_im, m_ssm_d, m_ssm_w_glu, m_w_attn_branch, m_w_ssm_branch, m_w_out, m_ffn2_norm, m_ffn2_w_gate, m_ffn2_w_up, m_ffn2_w_down, m_final_norm, v_ffn1_norm, v_ffn1_w_gate, v_ffn1_w_up, v_ffn1_w_down, v_mix_norm, v_w_in, v_gate_bias, v_rel_bias_table, v_ssm_a_re, v_ssm_a_im, v_ssm_log_dt, v_ssm_b_re, v_ssm_b_im, v_ssm_c_re, v_ssm_c_im, v_ssm_d, v_ssm_w_glu, v_w_attn_branch, v_w_ssm_branch, v_w_out, v_ffn2_norm, v_ffn2_w_gate, v_ffn2_w_up, v_ffn2_w_down, v_final_norm):
    given = dict(locals())
    shapes = {nm: given[nm].shape for nm in _ORDER}

    def strip(a):
        return a[0] if a.ndim >= 2 and a.shape[0] == 1 else a

    w = {nm: strip(given[nm]) for nm in _ORDER}
    m = {nm: strip(given["m_" + nm]) for nm in _ORDER}
    v = {nm: strip(given["v_" + nm]) for nm in _ORDER}
    for d in (w, m, v):
        d["rel_bias_table"] = d["rel_bias_table"].reshape(N_BUCKETS, N_GROUPS * HEADS_PER_GROUP)

    small = {nm: w[nm] for nm in _SMALL}
    small_in = dict(small)
    for nm in ("ffn1_norm", "mix_norm", "ffn2_norm", "gate_bias"):
        small_in[nm] = small[nm].reshape(1, -1)
    w_rows = {nm: _to_rows(w[nm], nm) for nm in _SHARD_INFO}

    def bf16_rows(phase):
        return [w_rows[nm].astype(BF16) for nm in _PHASES[phase]]

    f1_names = _PHASES["f1gu"] + _PHASES["f1d"]
    got_f1 = _all_gather("gather_f1", bf16_rows("f1gu") + bf16_rows("f1d"))
    pending_w = {"mix": _exchange_start("gather_mix_start", bf16_rows("mix"), gather=True, deps=[got_f1[0]])}
    pending_w["f2"] = _exchange_start("gather_f2_start", bf16_rows("f2"), gather=True, deps=[pending_w["mix"][4]])

    def weights_of(phase, after):
        if phase == "f1":
            return {nm: _full_weight(got, nm) for nm, got in zip(f1_names, got_f1)}
        landed = _exchange_wait(f"gather_{phase}_wait", pending_w[phase], after, gather=True)
        return {nm: _full_weight(got, nm) for nm, got in zip(_PHASES[phase], landed)}

    pending_g = {}

    def send_grads(phase, grads):
        if phase == "small":
            gs_pack = _pack_small({nm: grads[nm].reshape(small[nm].shape) for nm in _SMALL})
            pending_g[phase] = _exchange_start("gather_small_start", [gs_pack], gather=True)
        else:
            pending_g[phase] = _exchange_start(f"scatter_{phase}_start",
                                               [_grad_blocks(grads[nm], nm) for nm in _PHASES[phase]], gather=False)
        return [pending_g[phase][4]]

    loss, dx, gsmall = _local_step(_residue_order(x[0]), _residue_order(loss_target[0]), small_in, weights_of,
                                   send_grads, first_deps=[pending_w["f2"][4]])
    dx = _token_order(dx)

    updated = {}
    after = pending_g["f1gu"][4]
    for phase in ("f2", "mix", "f1d", "small", "f1gu"):
        landed = _exchange_wait(f"exchange_{phase}_wait", pending_g[phase], after, gather=phase == "small")
        if phase == "small":
            sm = _adamw("adamw_small", _pack_small(small), _pack_small({nm: m[nm] for nm in _SMALL}),
                        _pack_small({nm: v[nm] for nm in _SMALL}), landed[0], landed[0].shape[1])
            after = sm[0]
            continue
        for nm, recv in zip(_PHASES[phase], landed):
            tr = max(t for t in range(16, 353, 16) if w_rows[nm].shape[0] % t == 0)
            updated[nm] = _adamw(f"adamw_{nm}", w_rows[nm], _to_rows(m[nm], nm), _to_rows(v[nm], nm), recv, tr)
            after = updated[nm][0]

    loss = lax.psum(loss, ("x", "y", "c"))
    outs = []
    for i in range(4):
        sml = _unpack_small(sm[i], small)
        outs.append([(_from_rows(updated[nm][i], nm) if nm in updated else sml[nm]).reshape(shapes[nm])
                     for nm in _ORDER])
    return (loss, dx[None], *outs[0], *outs[1], *outs[2], *outs[3])
```

```python
import math

import numpy as np
import jax
import jax.numpy as jnp
from jax import lax
from jax.experimental import pallas as pl
from jax.experimental.pallas import tpu as pltpu

F32 = jnp.float32
BF16 = jnp.bfloat16

N_DEV = 8
D_MODEL = 1024
D_FF = 2816
HEAD_DIM = 64
HEADS_PER_GROUP = 4
DILATIONS = (1, 4, 16)
N_GROUPS = 3
ATTN_WIDTH = 768
ATTN_OUT = 256
BLOCK = 128
N_BUCKETS = 32
MAX_DISTANCE = 2048
NEG_INF = -1e30
SSM_WIDTH = 512
SSM_GROUPS = 32
SSM_GROUP = 16
SSM_STATE = 64
SSM_LANES = SSM_GROUPS * SSM_STATE
SSM_PAIRS = SSM_GROUPS // 2
PAIR_LANES = 2 * SSM_STATE
PAIR_TILE = 256
EPS = 1e-6
LR, B1, B2, ADAM_EPS, WD, STEP = 0.001, 0.9, 0.999, 1e-08, 0.01, 10

VMEM_LIMIT_BYTES = 56 * 1024 * 1024
FFN_CHUNK = D_FF // 2
SCAN_BLOCK = 256
SCAN_STEPS = 16
SCAN_COLS = SCAN_BLOCK // SCAN_STEPS
SCAN_SUB = 8
SCAN_LANES = 512

MESH = pl.DeviceIdType.MESH


def _cparams(*sem):
    return pltpu.CompilerParams(dimension_semantics=sem, vmem_limit_bytes=VMEM_LIMIT_BYTES)


def _dot(a, b, dims):
    return lax.dot_general(a, b, (dims, ((), ())), preferred_element_type=F32)


def _dot_nn(a, b):
    return _dot(a, b, ((1,), (0,)))


def _dot_nt(a, b):
    return _dot(a, b, ((1,), (1,)))


def _dot_tn(a, b):
    return _dot(a, b, ((0,), (0,)))


def _sigmoid(x):
    return 1.0 / (1.0 + jnp.exp(-x))


def _all_gather(name, xs_list):
    n = len(xs_list)

    def body(*refs):
        x_refs, out_refs = refs[:n], refs[n:2 * n]
        send_sems, recv_sems, local_sems = refs[2 * n:]
        x, y, c = lax.axis_index("x"), lax.axis_index("y"), lax.axis_index("c")
        me, sibling = (x, y, c), (x, y, 1 - c)
        chips = [(1 - x, y), (x, 1 - y), (1 - x, 1 - y)]

        def copy(a, k, block, to, own=False):
            px, py, pc = block
            rows = out_refs[a].at[4 * px + 2 * py + pc]
            return pltpu.make_async_remote_copy(
                src_ref=x_refs[a] if own else rows, dst_ref=rows,
                send_sem=send_sems.at[7 * a + k], recv_sem=recv_sems.at[7 * a + k], device_id=to,
                device_id_type=MESH)

        mine = [pltpu.make_async_copy(x_refs[a], out_refs[a].at[4 * x + 2 * y + c], local_sems.at[a]) for a in range(n)]
        first = []
        for a in range(n):
            mine[a].start()
            first.append(copy(a, 0, me, sibling, own=True))
            first += [copy(a, 1 + j, me, (*chip, c), own=True) for j, chip in enumerate(chips)]
        for cp in first:
            cp.start()
        passed = []
        for a in range(n):
            for j, chip in enumerate(chips):
                copy(a, 1 + j, (*chip, c), me).wait_recv()
                passed.append(copy(a, 4 + j, (*chip, c), sibling))
                passed[-1].start()
        for a in range(n):
            copy(a, 0, sibling, me).wait_recv()
            for j, chip in enumerate(chips):
                copy(a, 4 + j, (*chip, 1 - c), me).wait_recv()
        for cp in first + passed:
            cp.wait_send()
        for cp in mine:
            cp.wait()

    return pl.pallas_call(
        body, name=name,
        out_shape=[jax.ShapeDtypeStruct((N_DEV, *xs.shape), xs.dtype) for xs in xs_list],
        in_specs=[_ANY_SPEC] * n, out_specs=[_ANY_SPEC] * n,
        scratch_shapes=[pltpu.SemaphoreType.DMA((7 * n,)), pltpu.SemaphoreType.DMA((7 * n,)),
                        pltpu.SemaphoreType.DMA((n,))],
    )(*xs_list)


_HBM_SPEC = pl.BlockSpec(memory_space=pltpu.HBM)
_SEM_SPEC = pl.BlockSpec(memory_space=pltpu.SEMAPHORE)
_ANY_SPEC = pl.BlockSpec(memory_space=pl.ANY)
_EFFECT = pltpu.SideEffectType.DATAFLOW_SIDE_EFFECTING


def _peers(x, y, c):
    return [(1 - x if k & 4 else x, 1 - y if k & 2 else y, 1 - c if k & 1 else c) for k in range(1, N_DEV)]


def _exchange_copies(x_refs, land_refs, send_sems, recv_sems, gather):
    x, y, c = lax.axis_index("x"), lax.axis_index("y"), lax.axis_index("c")
    me = 4 * x + 2 * y + c
    copies = []
    for a, (x_ref, land_ref) in enumerate(zip(x_refs, land_refs)):
        for k, (px, py, pc) in enumerate(_peers(x, y, c)):
            src = x_ref if gather else x_ref.at[4 * px + 2 * py + pc]
            copies.append(pltpu.make_async_remote_copy(
                src_ref=src, dst_ref=land_ref.at[me], send_sem=send_sems.at[N_DEV * a + k],
                recv_sem=recv_sems.at[(N_DEV - 1) * a + k], device_id=(px, py, pc), device_id_type=MESH))
    owns = [pltpu.make_async_copy(x_ref if gather else x_ref.at[me], land_ref.at[me],
                                  send_sems.at[N_DEV * a + N_DEV - 1])
            for a, (x_ref, land_ref) in enumerate(zip(x_refs, land_refs))]
    return owns, copies


def _exchange_start(name, xs_list, gather, deps=()):
    n, nd = len(xs_list), len(deps)
    land_shapes = [(N_DEV, *xs.shape) if gather else xs.shape for xs in xs_list]

    def body(*refs):
        x_refs, land_refs = refs[:n], refs[n:2 * n]
        send_sems, recv_sems = refs[2 * n + nd:2 * n + nd + 2]
        token = refs[-1]
        owns, copies = _exchange_copies(x_refs, land_refs, send_sems, recv_sems, gather)
        for cp in copies + owns:
            cp.start()
        token[...] = jnp.zeros_like(token)

    hbm = lambda a: pltpu.with_memory_space_constraint(a, pltpu.HBM)
    outs = pl.pallas_call(
        body, name=name,
        out_shape=(pltpu.SemaphoreType.DMA((n * N_DEV,)), pltpu.SemaphoreType.DMA((n * (N_DEV - 1),)),
                   *[pltpu.HBM(xs.shape, xs.dtype) for xs in xs_list],
                   *[pltpu.HBM(shape, xs.dtype) for shape, xs in zip(land_shapes, xs_list)],
                   jax.ShapeDtypeStruct((8, 128), F32)),
        in_specs=(_HBM_SPEC,) * (2 * n) + (_ANY_SPEC,) * nd,
        out_specs=(_SEM_SPEC, _SEM_SPEC) + (_HBM_SPEC,) * (2 * n) + (pl.BlockSpec(memory_space=pltpu.VMEM),),
        input_output_aliases={i: 2 + i for i in range(2 * n)},
        compiler_params=pltpu.CompilerParams(has_side_effects=_EFFECT),
    )(*[hbm(xs) for xs in xs_list], *[hbm(lax.empty(shape, xs.dtype)) for shape, xs in zip(land_shapes, xs_list)],
      *deps)
    return outs[0], outs[1], list(outs[2:2 + n]), list(outs[2 + n:2 + 2 * n]), outs[-1]


def _exchange_wait(name, handle, after, gather):
    send_sems, recv_sems, xs_thru, lands_thru, _ = handle
    n = len(xs_thru)

    def body(*refs):
        x_refs, land_refs = refs[:n], refs[n:2 * n]
        send_sems, recv_sems = refs[2 * n:2 * n + 2]
        owns, copies = _exchange_copies(x_refs, land_refs, send_sems, recv_sems, gather)
        for cp in copies:
            cp.wait_send()
            cp.wait_recv()
        for cp in owns:
            cp.wait()

    outs = pl.pallas_call(
        body, name=name,
        out_shape=tuple(pltpu.HBM(a.shape, a.dtype) for a in xs_thru + lands_thru),
        in_specs=(_HBM_SPEC,) * (2 * n) + (_SEM_SPEC, _SEM_SPEC, _ANY_SPEC),
        out_specs=(_HBM_SPEC,) * (2 * n), input_output_aliases={i: i for i in range(2 * n)},
        compiler_params=pltpu.CompilerParams(has_side_effects=_EFFECT),
    )(*xs_thru, *lands_thru, send_sems, recv_sems, after)
    return list(outs[n:])


def _mm(name, pairs, nt, n_cols, out_dtypes, epilogue=None, extras=(), tm=1024, tn=512, deps=()):
    rows = pairs[0][0].shape[0]
    tm = min(tm, rows)
    tn = min(tn, n_cols)
    na, ne, nd = len(pairs), len(extras), len(deps)

    def body(*refs):
        a_refs, w_refs = refs[:na], refs[na:2 * na]
        e_refs, o_refs = refs[2 * na:2 * na + ne], refs[2 * na + ne + nd:]
        acc = None
        for a_ref, w_ref in zip(a_refs, w_refs):
            a = a_ref[...].astype(BF16)
            w = w_ref[...].astype(BF16)
            p = _dot_nt(a, w) if nt else _dot_nn(a, w)
            acc = p if acc is None else acc + p
        outs = (acc,) if epilogue is None else epilogue(acc, *[e[...] for e in e_refs])
        for o_ref, o in zip(o_refs, outs):
            o_ref[...] = o.astype(o_ref.dtype)

    in_specs = [pl.BlockSpec((tm, a.shape[1]), lambda i, j: (i, 0)) for a, _ in pairs]
    for _, w in pairs:
        if nt:
            in_specs.append(pl.BlockSpec((tn, w.shape[1]), lambda i, j: (j, 0)))
        else:
            in_specs.append(pl.BlockSpec((w.shape[0], tn), lambda i, j: (0, j)))
    for e, col_off in extras:
        off = col_off // tn
        if e.shape[0] == 1:
            in_specs.append(pl.BlockSpec((1, tn), lambda i, j, off=off: (0, j + off)))
        else:
            in_specs.append(pl.BlockSpec((tm, tn), lambda i, j, off=off: (i, j + off)))
    in_specs += [_ANY_SPEC] * nd
    out_specs = [pl.BlockSpec((tm, tn), lambda i, j: (i, j)) for _ in out_dtypes]
    outs = pl.pallas_call(
        body, name=name, grid=(rows // tm, n_cols // tn),
        in_specs=in_specs, out_specs=out_specs,
        out_shape=[jax.ShapeDtypeStruct((rows, n_cols), dt) for dt in out_dtypes],
        compiler_params=_cparams("parallel", "arbitrary"),
    )(*[a for a, _ in pairs], *[w for _, w in pairs], *[e for e, _ in extras], *deps)
    return outs


def _tn_rows(m):
    return max(b for b in range(128, min(m, 1408) + 1, 128) if m % b == 0)


def _mm_tn(name, a, b, scale=1.0, bm=None, tk=1024, deps=(), out_dtype=F32):
    rows, m = a.shape
    n = b.shape[1]
    bm = _tn_rows(m) if bm is None else bm
    tk = min(tk, rows)
    nk = rows // tk

    def body(a_ref, b_ref, *rest):
        o_ref, acc_ref = rest[-2:]
        k = pl.program_id(1)

        @pl.when(k == 0)
        def _():
            acc_ref[...] = jnp.zeros_like(acc_ref)

        acc_ref[...] += _dot_tn(a_ref[...].astype(BF16), b_ref[...].astype(BF16))

        @pl.when(k == nk - 1)
        def _():
            o_ref[...] = (acc_ref[...] * scale).astype(o_ref.dtype)

    return pl.pallas_call(
        body, name=name, grid=(m // bm, nk),
        in_specs=[pl.BlockSpec((tk, bm), lambda i, k: (k, i)), pl.BlockSpec((tk, n), lambda i, k: (k, 0))]
        + [_ANY_SPEC] * len(deps),
        out_specs=pl.BlockSpec((bm, n), lambda i, k: (i, 0)),
        out_shape=jax.ShapeDtypeStruct((m, n), out_dtype),
        scratch_shapes=[pltpu.VMEM((bm, n), F32)],
        compiler_params=_cparams("parallel", "arbitrary"),
    )(a, b, *deps)


def _mm_tn_stack(name, a_list, b, tk=1024, out_dtype=F32):
    rows, n = b.shape
    ms = [a.shape[1] for a in a_list]
    tk = min(tk, rows)
    nk = rows // tk
    na = len(a_list)

    def body(*refs):
        a_refs, b_ref, o_ref, acc_ref = refs[:na], refs[na], refs[na + 1], refs[na + 2]
        k = pl.program_id(0)

        @pl.when(k == 0)
        def _():
            acc_ref[...] = jnp.zeros_like(acc_ref)

        bv = b_ref[...].astype(BF16)
        r0 = 0
        for a_ref, m in zip(a_refs, ms):
            acc_ref[r0:r0 + m, :] += _dot_tn(a_ref[...].astype(BF16), bv)
            r0 += m

        @pl.when(k == nk - 1)
        def _():
            o_ref[...] = acc_ref[...].astype(o_ref.dtype)

    return pl.pallas_call(
        body, name=name, grid=(nk,),
        in_specs=[pl.BlockSpec((tk, m), lambda k: (k, 0)) for m in ms] + [pl.BlockSpec((tk, n), lambda k: (k, 0))],
        out_specs=pl.BlockSpec((sum(ms), n), lambda k: (0, 0)),
        out_shape=jax.ShapeDtypeStruct((sum(ms), n), out_dtype),
        scratch_shapes=[pltpu.VMEM((sum(ms), n), F32)],
        compiler_params=_cparams("arbitrary"),
    )(*a_list, b)


def _colsum(name, xs, tm=512):
    rows, cols = xs.shape
    tm = min(tm, rows)

    def body(x_ref, o_ref):
        @pl.when(pl.program_id(0) == 0)
        def _():
            o_ref[...] = jnp.zeros_like(o_ref)

        o_ref[...] += jnp.sum(x_ref[...].astype(F32), axis=0, keepdims=True)

    return pl.pallas_call(
        body, name=name, grid=(rows // tm,),
        in_specs=[pl.BlockSpec((tm, cols), lambda i: (i, 0))],
        out_specs=pl.BlockSpec((1, cols), lambda i: (0, 0)),
        out_shape=jax.ShapeDtypeStruct((1, cols), F32),
        compiler_params=_cparams("arbitrary"),
    )(xs)


def _ew(name, fn, ins, out_cols, out_dtypes, tm=512):
    rows = ins[0].shape[0]
    tm = min(tm, rows)
    ni = len(ins)

    def body(*refs):
        outs = fn(*[r[...] for r in refs[:ni]])
        for o_ref, o in zip(refs[ni:], outs):
            o_ref[...] = o.astype(o_ref.dtype)

    def spec(shape):
        if shape[0] == 1:
            return pl.BlockSpec((1, shape[1]), lambda i: (0, 0))
        return pl.BlockSpec((tm, shape[1]), lambda i: (i, 0))

    return pl.pallas_call(
        body, name=name, grid=(rows // tm,),
        in_specs=[spec(a.shape) for a in ins],
        out_specs=[pl.BlockSpec((tm, c), lambda i: (i, 0)) for c in out_cols],
        out_shape=[jax.ShapeDtypeStruct((rows, c), dt) for c, dt in zip(out_cols, out_dtypes)],
        compiler_params=_cparams("parallel"),
    )(*ins)


def _rms_parts(xv):
    r = lax.rsqrt(jnp.mean(xv * xv, axis=-1, keepdims=True) + EPS)
    return r, xv * r


def _rms_bwd_dx(dh, gain, r, xh):
    dxh = dh * gain
    return r * (dxh - xh * jnp.mean(dxh * xh, axis=-1, keepdims=True))


def _rms_fwd(name, xs, gain):
    def fn(xv, g):
        _, xh = _rms_parts(xv)
        return (xh * g,)

    return _ew(name, fn, [xs, gain], [xs.shape[1]], [BF16])[0]


def _rms_bwd(name, dh, xs, gain, dres, tm=512):
    rows, d = xs.shape
    tm = min(tm, rows)

    def body(dh_ref, x_ref, g_ref, dres_ref, dx_ref, dg_ref):
        r, xh = _rms_parts(x_ref[...])
        dhv = dh_ref[...]
        dx_ref[...] = dres_ref[...] + _rms_bwd_dx(dhv, g_ref[...], r, xh)

        @pl.when(pl.program_id(0) == 0)
        def _():
            dg_ref[...] = jnp.zeros_like(dg_ref)

        dg_ref[...] += jnp.sum(dhv * xh, axis=0, keepdims=True)

    tile = pl.BlockSpec((tm, d), lambda i: (i, 0))
    row = pl.BlockSpec((1, d), lambda i: (0, 0))
    return pl.pallas_call(
        body, name=name, grid=(rows // tm,),
        in_specs=[tile, tile, row, tile], out_specs=[tile, row],
        out_shape=[jax.ShapeDtypeStruct((rows, d), F32), jax.ShapeDtypeStruct((1, d), F32)],
        compiler_params=_cparams("arbitrary"),
    )(dh, xs, gain, dres)


def _ffn_chunks(f_all):
    return [slice(c, c + FFN_CHUNK) for c in range(0, f_all, FFN_CHUNK)]


def _ffn_fwd(name, xs, gain, wg_t, wu_t, wd, tm=512, deps=()):
    rows, d = xs.shape
    f_all = wd.shape[0]
    tm = min(tm, rows)

    def body(x_ref, g_ref, wg_ref, wu_ref, wd_ref, *rest):
        xo_ref, h_ref, gg_ref, uu_ref = rest[-4:]
        xv = x_ref[...]
        _, xh = _rms_parts(xv)
        h = (xh * g_ref[...]).astype(BF16)
        h_ref[...] = h
        acc = None
        for cols in _ffn_chunks(f_all):
            gg = _dot_nt(h, wg_ref[cols, :])
            uu = _dot_nt(h, wu_ref[cols, :])
            act = gg * _sigmoid(gg) * uu
            part = _dot_nn(act.astype(BF16), wd_ref[cols, :])
            acc = part if acc is None else acc + part
            gg_ref[:, cols] = gg.astype(BF16)
            uu_ref[:, cols] = uu.astype(BF16)
        xo_ref[...] = xv + 0.5 * acc

    tile = pl.BlockSpec((tm, d), lambda i: (i, 0))
    wspec = pl.BlockSpec((f_all, d), lambda i: (0, 0), pipeline_mode=pl.Buffered(1))
    hid = pl.BlockSpec((tm, f_all), lambda i: (i, 0))
    return pl.pallas_call(
        body, name=name, grid=(rows // tm,),
        in_specs=[tile, pl.BlockSpec((1, d), lambda i: (0, 0)), wspec, wspec, wspec] + [_ANY_SPEC] * len(deps),
        out_specs=[tile, tile, hid, hid],
        out_shape=[jax.ShapeDtypeStruct((rows, d), F32), jax.ShapeDtypeStruct((rows, d), BF16),
                   jax.ShapeDtypeStruct((rows, f_all), BF16), jax.ShapeDtypeStruct((rows, f_all), BF16)],
        compiler_params=_cparams("parallel"),
    )(xs, gain, wg_t, wu_t, wd, *deps)


def _ffn_bwd(name, dxo, xs, gain, gg_all, uu_all, wg_t, wu_t, wd, tm=256):
    rows, d = xs.shape
    f_all = wd.shape[0]
    tm = min(tm, rows)

    def body(dxo_ref, x_ref, g_ref, gg_ref, uu_ref, wg_ref, wu_ref, wd_ref,
             dx_ref, dgg_ref, duu_ref, act_ref, dgain_ref):
        dxo = dxo_ref[...]
        df = (0.5 * dxo).astype(BF16)
        dh = None
        for cols in _ffn_chunks(f_all):
            gg = gg_ref[:, cols].astype(F32)
            uu = uu_ref[:, cols].astype(F32)
            sg = _sigmoid(gg)
            silu = gg * sg
            dact = _dot_nt(df, wd_ref[cols, :])
            duu = (dact * silu).astype(BF16)
            dgg = (dact * uu * (sg * (1.0 + gg * (1.0 - sg)))).astype(BF16)
            act_ref[:, cols] = (silu * uu).astype(BF16)
            dgg_ref[:, cols] = dgg
            duu_ref[:, cols] = duu
            part = _dot_nn(dgg, wg_ref[cols, :]) + _dot_nn(duu, wu_ref[cols, :])
            dh = part if dh is None else dh + part
        r, xh = _rms_parts(x_ref[...])
        dx_ref[...] = dxo + _rms_bwd_dx(dh, g_ref[...], r, xh)

        @pl.when(pl.program_id(0) == 0)
        def _():
            dgain_ref[...] = jnp.zeros_like(dgain_ref)

        dgain_ref[...] += jnp.sum(dh * xh, axis=0, keepdims=True)

    tile = pl.BlockSpec((tm, d), lambda i: (i, 0))
    row = pl.BlockSpec((1, d), lambda i: (0, 0))
    wspec = pl.BlockSpec((f_all, d), lambda i: (0, 0), pipeline_mode=pl.Buffered(1))
    hid = pl.BlockSpec((tm, f_all), lambda i: (i, 0))
    hid_shape = jax.ShapeDtypeStruct((rows, f_all), BF16)
    return pl.pallas_call(
        body, name=name, grid=(rows // tm,),
        in_specs=[tile, tile, row, hid, hid, wspec, wspec, wspec],
        out_specs=[tile, hid, hid, hid, row],
        out_shape=[jax.ShapeDtypeStruct((rows, d), F32), hid_shape, hid_shape, hid_shape,
                   jax.ShapeDtypeStruct((1, d), F32)],
        compiler_params=_cparams("arbitrary"),
    )(dxo, xs, gain, gg_all, uu_all, wg_t, wu_t, wd)


def _final_loss(name, xs, gain, target, tm=512):
    rows, d = xs.shape
    tm = min(tm, rows)

    def body(x_ref, g_ref, t_ref, dx_ref, dg_ref, loss_ref):
        r, xh = _rms_parts(x_ref[...])
        gain_v = g_ref[...]
        err = xh * gain_v - t_ref[...]
        dy = err * (1.0 / d)
        dx_ref[...] = _rms_bwd_dx(dy, gain_v, r, xh)

        @pl.when(pl.program_id(0) == 0)
        def _():
            dg_ref[...] = jnp.zeros_like(dg_ref)
            loss_ref[...] = jnp.zeros_like(loss_ref)

        dg_ref[...] += jnp.sum(dy * xh, axis=0, keepdims=True)
        per_tok = jnp.mean(err * err, axis=-1, keepdims=True)
        loss_ref[...] += 0.5 * jnp.sum(per_tok, axis=0, keepdims=True)

    tile = pl.BlockSpec((tm, d), lambda i: (i, 0))
    row = pl.BlockSpec((1, d), lambda i: (0, 0))
    return pl.pallas_call(
        body, name=name, grid=(rows // tm,),
        in_specs=[tile, row, tile],
        out_specs=[tile, row, pl.BlockSpec((1, 1), lambda i: (0, 0))],
        out_shape=[jax.ShapeDtypeStruct((rows, d), F32), jax.ShapeDtypeStruct((1, d), F32),
                   jax.ShapeDtypeStruct((1, 1), F32)],
        compiler_params=_cparams("arbitrary"),
    )(xs, gain, target)


def _t5_bucket_np(dist):
    max_exact = N_BUCKETS // 2
    dd = np.maximum(dist, 1).astype(np.float32)
    large = max_exact + (np.log(dd / np.float32(max_exact)) / np.float32(math.log(MAX_DISTANCE / max_exact))
                         * np.float32(N_BUCKETS - max_exact)).astype(np.int32)
    large = np.minimum(large, N_BUCKETS - 1)
    return np.where(dist < max_exact, dist, large).astype(np.int32)


def _attn_geometry(g, rows):
    run = rows // 16
    dil = DILATIONS[g]
    if dil == 16:
        bq = BLOCK
        return dict(view=(16, run), block=(None, bq), grid=(16, run // bq), index=lambda r, n: (r, n),
                    pos=np.arange(bq), bq=bq)
    if dil == 4:
        per = BLOCK // 4
        pos = (4 * np.arange(per)[None, :] + np.arange(4)[:, None]).reshape(-1)
        return dict(view=(4, 4, run), block=(4, None, per), grid=(4, run // per), index=lambda r, n: (0, r, n),
                    pos=pos, bq=BLOCK)
    per = 16
    pos = (16 * np.arange(per)[None, :] + np.arange(16)[:, None]).reshape(-1)
    return dict(view=(16, run), block=(16, per), grid=(1, run // per), index=lambda r, n: (0, n),
                pos=pos, bq=16 * per)


def _attn_tables(g, rows):
    geo = _attn_geometry(g, rows)
    pos, bq = geo["pos"], geo["bq"]
    steps = pos[:, None] - np.concatenate([pos - bq, pos])[None, :]
    valid = (steps >= 0) & (steps <= BLOCK)
    bucket = _t5_bucket_np((np.maximum(steps, 0) * DILATIONS[g]).astype(np.int32))
    return bucket, valid.astype(np.int32)


def _bias_fwd(name, bucket, valid, table_t):
    bq = bucket.shape[0]

    def body(bk_ref, ok_ref, tab_ref, o_ref):
        bk = bk_ref[...]
        ok = ok_ref[...] > 0
        for h in range(HEADS_PER_GROUP):
            acc = jnp.zeros(bk.shape, F32)
            for b in range(N_BUCKETS):
                acc = jnp.where(bk == b, tab_ref[h, b], acc)
            o_ref[h] = jnp.where(ok, acc, NEG_INF)

    vm = pl.BlockSpec(memory_space=pltpu.VMEM)
    return pl.pallas_call(
        body, name=name, in_specs=[vm, vm, pl.BlockSpec(memory_space=pltpu.SMEM)], out_specs=vm,
        out_shape=jax.ShapeDtypeStruct((HEADS_PER_GROUP, bq, 2 * bq), F32),
    )(bucket, valid, table_t)


def _bias_bwd(name, bucket, dbias):
    def body(bk_ref, db_ref, o_ref):
        row_id = lax.broadcasted_iota(jnp.int32, (N_BUCKETS, 128), 0)
        col_id = lax.broadcasted_iota(jnp.int32, (N_BUCKETS, 128), 1)
        bk = bk_ref[...]
        acc = jnp.zeros((N_BUCKETS, 128), F32)
        for h in range(HEADS_PER_GROUP):
            db = db_ref[h]
            for b in range(N_BUCKETS):
                part = jnp.sum(jnp.where(bk == b, db, 0.0), axis=0, keepdims=True)
                tot = jnp.sum(part, axis=1, keepdims=True)
                acc = jnp.where((row_id == b) & (col_id == h), tot, acc)
        o_ref[...] = acc

    vm = pl.BlockSpec(memory_space=pltpu.VMEM)
    return pl.pallas_call(body, name=name, in_specs=[vm, vm], out_specs=vm,
                          out_shape=jax.ShapeDtypeStruct((N_BUCKETS, 128), F32))(bucket, dbias)


def _head_of_lane(nrows):
    return lax.broadcasted_iota(jnp.int32, (nrows, ATTN_OUT), 1) // HEAD_DIM


def _stack_heads(a, lane_head):
    zero = jnp.zeros_like(a)
    return jnp.concatenate([jnp.where(lane_head == h, a, zero) for h in range(HEADS_PER_GROUP)], axis=0)


def _unstack_heads(a4, lane_head, bq):
    out = a4[:bq]
    for h in range(1, HEADS_PER_GROUP):
        out = jnp.where(lane_head == h, a4[h * bq:(h + 1) * bq], out)
    return out


def _attn_specs(geo, cols, col_block, index):
    return pl.BlockSpec(geo["block"] + (cols,), lambda r, n: index(r, n) + (col_block,))


def _attn_fwd(name, qkv, g, bias4):
    rows = qkv.shape[0]
    geo = _attn_geometry(g, rows)
    bq, (nsub, nb), index = geo["bq"], geo["grid"], geo["index"]
    blk_shape = tuple(b for b in geo["block"] if b is not None) + (ATTN_OUT,)

    def body(q_ref, kc_ref, kp_ref, vc_ref, vp_ref, b_ref, o_ref, lse_ref):
        n = pl.program_id(1)
        lane_head = _head_of_lane(bq)
        flat = lambda ref: ref[...].reshape(bq, ATTN_OUT)
        q4 = _stack_heads(flat(q_ref), lane_head)
        k2 = jnp.concatenate([flat(kp_ref), flat(kc_ref)], axis=0)
        v2 = jnp.concatenate([flat(vp_ref), flat(vc_ref)], axis=0)
        s = _dot_nt(q4, k2) + b_ref[...]
        col = lax.broadcasted_iota(jnp.int32, s.shape, 1)
        s = jnp.where((col >= bq) | (n > 0), s, NEG_INF)
        mx = jnp.max(s, axis=-1, keepdims=True)
        p = jnp.exp(s - mx)
        den = jnp.sum(p, axis=-1, keepdims=True)
        o4 = _dot_nn(p.astype(BF16), v2) / den
        lse4 = jnp.broadcast_to(mx + jnp.log(den), (HEADS_PER_GROUP * bq, ATTN_OUT))
        o_ref[...] = _unstack_heads(o4, lane_head, bq).reshape(blk_shape)
        lse_ref[...] = _unstack_heads(lse4, lane_head, bq).reshape(blk_shape)

    prev = lambda r, n: index(r, jnp.maximum(n - 1, 0))
    view = lambda a: a.reshape(geo["view"] + (a.shape[1],))
    qkv_v = view(qkv)
    out_spec = _attn_specs(geo, ATTN_OUT, 0, index)
    out_shape = jax.ShapeDtypeStruct(geo["view"] + (ATTN_OUT,), F32)
    o, lse = pl.pallas_call(
        body, name=name, grid=(nsub, nb),
        in_specs=[_attn_specs(geo, ATTN_OUT, g, index), _attn_specs(geo, ATTN_OUT, 3 + g, index),
                  _attn_specs(geo, ATTN_OUT, 3 + g, prev), _attn_specs(geo, ATTN_OUT, 6 + g, index),
                  _attn_specs(geo, ATTN_OUT, 6 + g, prev), pl.BlockSpec(bias4.shape, lambda r, n: (0, 0))],
        out_specs=[out_spec, out_spec], out_shape=[out_shape, out_shape],
        compiler_params=_cparams("parallel", "arbitrary"),
    )(qkv_v, qkv_v, qkv_v, qkv_v, qkv_v, bias4)
    return o.reshape(rows, ATTN_OUT), lse.reshape(rows, ATTN_OUT)


def _attn_bwd(name, qkv, do, lse, cvec, g, bias4):
    rows = qkv.shape[0]
    geo = _attn_geometry(g, rows)
    bq, (nsub, nb), index = geo["bq"], geo["grid"], geo["index"]
    blk_shape = tuple(b for b in geo["block"] if b is not None) + (ATTN_OUT,)
    nlead = len(blk_shape) - 1

    def body(q_ref, kc_ref, kp_ref, vc_ref, vp_ref, do_ref, lse_ref, c_ref, b_ref,
             dq_ref, dk_ref, dv_ref, db_ref, kcar_ref, vcar_ref):
        r, n = pl.program_id(0), pl.program_id(1)
        valid = n < nb
        lane_head = _head_of_lane(bq)
        flat = lambda ref: ref[...].reshape(bq, ATTN_OUT)

        @pl.when((r == 0) & (n == 0))
        def _():
            kcar_ref[...] = jnp.zeros_like(kcar_ref)
            vcar_ref[...] = jnp.zeros_like(vcar_ref)
            db_ref[...] = jnp.zeros_like(db_ref)

        def column(ref, h):
            lead = (slice(None),) * nlead
            return ref[lead + (pl.ds(h * HEAD_DIM, 1),)].reshape(bq, 1)

        q4 = _stack_heads(flat(q_ref), lane_head)
        do4 = _stack_heads(flat(do_ref), lane_head)
        k2 = jnp.concatenate([flat(kp_ref), flat(kc_ref)], axis=0)
        v2 = jnp.concatenate([flat(vp_ref), flat(vc_ref)], axis=0)
        lse4 = jnp.concatenate([column(lse_ref, h) for h in range(HEADS_PER_GROUP)], axis=0)
        c4 = jnp.concatenate([column(c_ref, h) for h in range(HEADS_PER_GROUP)], axis=0)
        s = _dot_nt(q4, k2) + b_ref[...]
        col = lax.broadcasted_iota(jnp.int32, s.shape, 1)
        keep = ((col >= bq) | (n > 0)) & valid
        p = jnp.where(keep, jnp.exp(s - lse4), 0.0)
        ds = p * (_dot_nt(do4, v2) + c4)
        ds_b = ds.astype(BF16)

        @pl.when(valid)
        def _():
            dq = _unstack_heads(_dot_nn(ds_b, k2), lane_head, bq) * (HEAD_DIM ** -0.5)
            dq_ref[...] = dq.astype(BF16).reshape(blk_shape)

        dk2 = _dot_tn(ds_b, q4)
        dv2 = _dot_tn(p.astype(BF16), do4)
        dk_ref[...] = (kcar_ref[...] + dk2[:bq]).astype(BF16).reshape(blk_shape)
        dv_ref[...] = (vcar_ref[...] + dv2[:bq]).astype(BF16).reshape(blk_shape)
        kcar_ref[...] = dk2[bq:]
        vcar_ref[...] = dv2[bq:]
        db_ref[...] += ds

    cur = lambda r, n: index(r, jnp.minimum(n, nb - 1))
    prev = lambda r, n: index(r, jnp.maximum(jnp.minimum(n, nb - 1) - 1, 0))
    late = lambda r, n: index(r, jnp.maximum(n - 1, 0))
    view = lambda a: a.reshape(geo["view"] + (a.shape[1],))
    qkv_v = view(qkv)
    tile = _attn_specs(geo, ATTN_OUT, 0, cur)
    bias_spec = pl.BlockSpec(bias4.shape, lambda r, n: (0, 0))
    out_shape = jax.ShapeDtypeStruct(geo["view"] + (ATTN_OUT,), BF16)
    dq, dk, dv, db = pl.pallas_call(
        body, name=name, grid=(nsub, nb + 1),
        in_specs=[_attn_specs(geo, ATTN_OUT, g, cur), _attn_specs(geo, ATTN_OUT, 3 + g, cur),
                  _attn_specs(geo, ATTN_OUT, 3 + g, prev), _attn_specs(geo, ATTN_OUT, 6 + g, cur),
                  _attn_specs(geo, ATTN_OUT, 6 + g, prev), tile, tile, tile, bias_spec],
        out_specs=[tile, _attn_specs(geo, ATTN_OUT, 0, late), _attn_specs(geo, ATTN_OUT, 0, late), bias_spec],
        out_shape=[out_shape, out_shape, out_shape, jax.ShapeDtypeStruct(bias4.shape, F32)],
        scratch_shapes=[pltpu.VMEM((bq, ATTN_OUT), F32), pltpu.VMEM((bq, ATTN_OUT), F32)],
        compiler_params=_cparams("arbitrary", "arbitrary"),
    )(qkv_v, qkv_v, qkv_v, qkv_v, qkv_v, view(do), view(lse), view(cvec), bias4)
    return dq.reshape(rows, ATTN_OUT), dk.reshape(rows, ATTN_OUT), dv.reshape(rows, ATTN_OUT), db


def _group_weights(lses):
    mx = jnp.maximum(jnp.maximum(lses[0], lses[1]), lses[2])
    es = [jnp.exp(l - mx) for l in lses]
    den = es[0] + es[1] + es[2]
    return [e / den for e in es]


def _combine_fwd(name, os_, lses):
    def fn(o0, o1, o2, l0, l1, l2):
        ws = _group_weights([l0, l1, l2])
        out = ws[0] * o0 + ws[1] * o1 + ws[2] * o2
        return out, out

    return _ew(name, fn, [*os_, *lses], [ATTN_OUT, ATTN_OUT], [F32, BF16], tm=1024)


def _combine_bwd(name, do, oa, lses):
    def fn(dov, oav, l0, l1, l2):
        head_sum = (lax.broadcasted_iota(jnp.int32, (ATTN_OUT, ATTN_OUT), 0) // HEAD_DIM
                    == lax.broadcasted_iota(jnp.int32, (ATTN_OUT, ATTN_OUT), 1) // HEAD_DIM)
        ws = _group_weights([l0, l1, l2])
        prod = dov * oav
        hi = prod.astype(BF16)
        lo = (prod - hi.astype(F32)).astype(BF16)
        ones = jnp.where(head_sum, 1.0, 0.0).astype(BF16)
        bar = _dot_nn(hi, ones) + _dot_nn(lo, ones)
        return tuple(w * dov for w in ws) + tuple(-w * bar for w in ws)

    return _ew(name, fn, [do, oa, *lses], [ATTN_OUT] * 6, [BF16] * 3 + [F32] * 3, tm=1024)


def _ssm_disc(a_re, a_im, log_dt, b_re, b_im):
    dt = jnp.exp(log_dt)
    mag = jnp.exp(a_re * dt)
    ab_re = mag * jnp.cos(a_im * dt)
    ab_im = mag * jnp.sin(a_im * dt)
    den = a_re * a_re + a_im * a_im
    xr = ab_re - 1.0
    coef_re = (xr * a_re + ab_im * a_im) / den
    coef_im = (ab_im * a_re - xr * a_im) / den
    bb_re = coef_re[None] * b_re - coef_im[None] * b_im
    bb_im = coef_re[None] * b_im + coef_im[None] * b_re
    return ab_re, ab_im, bb_re, bb_im


def _cpow2(re, im, times):
    for _ in range(times):
        re, im = re * re - im * im, 2.0 * re * im
    return re, im


def _ssm_params_fwd(name, a_re, a_im, log_dt, b_re, b_im):
    gn = jax.ShapeDtypeStruct(a_re.shape, F32)
    cgn = jax.ShapeDtypeStruct(b_re.shape, F32)

    def body(ar, ai, ld, br, bi, o_abr, o_abi, o_apr, o_api, o_bbr, o_bbi):
        ab_re, ab_im, bb_re, bb_im = _ssm_disc(ar[...], ai[...], ld[...], br[...], bi[...])
        o_abr[...] = ab_re
        o_abi[...] = ab_im
        pr, pi = _cpow2(ab_re, ab_im, int(math.log2(SCAN_STEPS)))
        o_apr[...] = pr
        o_api[...] = pi
        o_bbr[...] = bb_re
        o_bbi[...] = bb_im

    vm = pl.BlockSpec(memory_space=pltpu.VMEM)
    return pl.pallas_call(body, name=name, in_specs=[vm] * 5, out_specs=[vm] * 6,
                          out_shape=[gn, gn, gn, gn, cgn, cgn])(a_re, a_im, log_dt, b_re, b_im)


def _ssm_params_bwd(name, a_re, a_im, log_dt, b_re, b_im, d_ab_re, d_ab_im, d_bb_re, d_bb_im):
    gn = jax.ShapeDtypeStruct(a_re.shape, F32)
    cgn = jax.ShapeDtypeStruct(b_re.shape, F32)

    def body(ar, ai, ld, br, bi, g0, g1, g2, g3, o_ar, o_ai, o_ld, o_br, o_bi):
        _, vjp = jax.vjp(_ssm_disc, ar[...], ai[...], ld[...], br[...], bi[...])
        outs = vjp((g0[...], g1[...], g2[...], g3[...]))
        for o_ref, o in zip((o_ar, o_ai, o_ld, o_br, o_bi), outs):
            o_ref[...] = o

    vm = pl.BlockSpec(memory_space=pltpu.VMEM)
    return pl.pallas_call(body, name=name, in_specs=[vm] * 9, out_specs=[vm] * 5,
                          out_shape=[gn, gn, jax.ShapeDtypeStruct(log_dt.shape, F32), cgn, cgn],
                          )(a_re, a_im, log_dt, b_re, b_im, d_ab_re, d_ab_im, d_bb_re, d_bb_im)


def _scan_block(s_ref, carry_ref, tmp_ref, ab_ref, ap_ref, reverse, sprev=None):
    nl = SSM_LANES
    halves = range(SCAN_COLS // SCAN_SUB)
    zero = jnp.zeros((SCAN_SUB, SCAN_LANES), F32)
    for half in (reversed(halves) if reverse else halves):
        sub_rows = pl.ds(half * SCAN_SUB, SCAN_SUB)
        for lc in range(nl // SCAN_LANES):
            re_l = pl.ds(lc * SCAN_LANES, SCAN_LANES)
            im_l = pl.ds(nl + lc * SCAN_LANES, SCAN_LANES)
            are, aim = ab_ref[:, re_l], ab_ref[:, im_l]

            def step_of(j):
                return SCAN_STEPS - 1 - j if reverse else j

            def pass1(j, st):
                sr, si = st
                jj = step_of(j)
                nr = are * sr - aim * si + s_ref[jj, sub_rows, re_l]
                ni = are * si + aim * sr + s_ref[jj, sub_rows, im_l]
                s_ref[jj, sub_rows, re_l] = nr
                s_ref[jj, sub_rows, im_l] = ni
                return nr, ni

            er, ei = lax.fori_loop(0, SCAN_STEPS, pass1, (zero, zero), unroll=2)
            tmp_ref[0:SCAN_SUB, re_l] = er
            tmp_ref[0:SCAN_SUB, im_l] = ei
            apr, api = ap_ref[0:1, re_l], ap_ref[0:1, im_l]
            sr, si = carry_ref[0:1, re_l], carry_ref[0:1, im_l]
            for step in range(SCAN_SUB):
                c = SCAN_SUB - 1 - step if reverse else step
                tmp_ref[SCAN_SUB + c:SCAN_SUB + c + 1, re_l] = sr
                tmp_ref[SCAN_SUB + c:SCAN_SUB + c + 1, im_l] = si
                e_r, e_i = tmp_ref[c:c + 1, re_l], tmp_ref[c:c + 1, im_l]
                sr, si = apr * sr - api * si + e_r, apr * si + api * sr + e_i
            carry_ref[0:1, re_l] = sr
            carry_ref[0:1, im_l] = si
            cr = tmp_ref[SCAN_SUB:2 * SCAN_SUB, re_l]
            ci = tmp_ref[SCAN_SUB:2 * SCAN_SUB, im_l]

            if sprev is None:
                def pass2(j, st):
                    pr, pi = st
                    jj = step_of(j)
                    s_ref[jj, sub_rows, re_l] += pr * cr - pi * ci
                    s_ref[jj, sub_rows, im_l] += pr * ci + pi * cr
                    return pr * are - pi * aim, pr * aim + pi * are

                lax.fori_loop(0, SCAN_STEPS, pass2, (are, aim), unroll=2)
            else:
                st_ref, prev_ref, have_prev, dab_ref = sprev

                def corrected(jj, pr, pi):
                    gr = s_ref[jj, sub_rows, re_l] + pr * cr - pi * ci
                    gi = s_ref[jj, sub_rows, im_l] + pr * ci + pi * cr
                    s_ref[jj, sub_rows, re_l] = gr
                    s_ref[jj, sub_rows, im_l] = gi
                    return gr, gi

                def pass2(j, st):
                    pr, pi, dr, di = st
                    jj = SCAN_STEPS - 1 - j
                    gr, gi = corrected(jj, pr, pi)
                    qr, qi = st_ref[jj - 1, sub_rows, re_l], st_ref[jj - 1, sub_rows, im_l]
                    return (pr * are - pi * aim, pr * aim + pi * are,
                            dr + gr * qr + gi * qi, di + gi * qr - gr * qi)

                pr, pi, dr, di = lax.fori_loop(0, SCAN_STEPS - 1, pass2, (are, aim, zero, zero), unroll=2)
                gr, gi = corrected(0, pr, pi)
                sub = lax.broadcasted_iota(jnp.int32, (SCAN_SUB, SCAN_LANES), 0)
                if half == 0:
                    pv_r = prev_ref[SCAN_SUB - 1:SCAN_SUB, re_l] * have_prev
                    pv_i = prev_ref[SCAN_SUB - 1:SCAN_SUB, im_l] * have_prev
                else:
                    before = pl.ds(half * SCAN_SUB - 1, 1)
                    pv_r, pv_i = st_ref[SCAN_STEPS - 1, before, re_l], st_ref[SCAN_STEPS - 1, before, im_l]
                shape = (SCAN_SUB, SCAN_LANES)
                qr = jnp.where(sub == 0, jnp.broadcast_to(pv_r, shape),
                               pltpu.roll(st_ref[SCAN_STEPS - 1, sub_rows, re_l], 1, 0))
                qi = jnp.where(sub == 0, jnp.broadcast_to(pv_i, shape),
                               pltpu.roll(st_ref[SCAN_STEPS - 1, sub_rows, im_l], 1, 0))
                dab_ref[:, re_l] += dr + gr * qr + gi * qi
                dab_ref[:, im_l] += di + gi * qr - gr * qi


def _scan_view(a):
    return a.reshape(16, a.shape[0] // 16, a.shape[1])


def _pair_tile(p):
    start = (p * 2 * SSM_GROUP // PAIR_TILE) * PAIR_TILE
    return slice(start, start + PAIR_TILE)


def _pair_lanes(p):
    return pl.ds(p * PAIR_LANES, PAIR_LANES), pl.ds(SSM_LANES + p * PAIR_LANES, PAIR_LANES)


def _pair_store(s_ref, p, val):
    re_l, im_l = _pair_lanes(p)
    s_ref[:, :, re_l] = val[:, :PAIR_LANES].reshape(16, SCAN_COLS, PAIR_LANES)
    s_ref[:, :, im_l] = val[:, PAIR_LANES:].reshape(16, SCAN_COLS, PAIR_LANES)


def _pair_load(s_ref, p):
    re_l, im_l = _pair_lanes(p)
    parts = [s_ref[:, :, l].reshape(SCAN_BLOCK, PAIR_LANES) for l in (re_l, im_l)]
    return jnp.concatenate(parts, axis=1).astype(BF16)


def _pair_sum(fn):
    per = PAIR_TILE // (2 * SSM_GROUP)
    tiles = []
    for t in range(SSM_PAIRS // per):
        acc = None
        for p in range(t * per, (t + 1) * per):
            part = fn(p)
            acc = part if acc is None else acc + part
        tiles.append(acc)
    return jnp.concatenate(tiles, axis=1)


def _ssm_fwd(name, u, bb_mats, c_mats, ab_rows, ap_rows, d_skip):
    rows = u.shape[0]
    nl2 = 2 * SSM_LANES
    nblk = rows // SCAN_BLOCK

    def body(u_ref, bb_ref, c_ref, ab_ref, ap_ref, d_ref, y_ref, s_ref, carry_ref, tmp_ref):
        @pl.when(pl.program_id(0) == 0)
        def _():
            carry_ref[...] = jnp.zeros_like(carry_ref)

        uv = u_ref[...].reshape(SCAN_BLOCK, SSM_WIDTH)
        ub = uv.astype(BF16)
        for p in range(SSM_PAIRS):
            _pair_store(s_ref, p, _dot_nn(ub[:, _pair_tile(p)], bb_ref[p]))
        _scan_block(s_ref, carry_ref, tmp_ref, ab_ref, ap_ref, reverse=False)
        ys = _pair_sum(lambda p: _dot_nt(_pair_load(s_ref, p), c_ref[p]))
        y_ref[...] = (ys + d_ref[...] * uv).reshape(16, SCAN_COLS, SSM_WIDTH)

    const = lambda shape: pl.BlockSpec(shape, lambda i: (0,) * len(shape))
    blk = lambda cols: pl.BlockSpec((16, SCAN_COLS, cols), lambda i: (0, i, 0))
    pair_mats = const((SSM_PAIRS, PAIR_TILE, PAIR_TILE))
    y, s = pl.pallas_call(
        body, name=name, grid=(nblk,),
        in_specs=[blk(SSM_WIDTH), pair_mats, pair_mats, const((SCAN_SUB, nl2)),
                  const((SCAN_SUB, nl2)), const((1, SSM_WIDTH))],
        out_specs=[blk(SSM_WIDTH), blk(nl2)],
        out_shape=[jax.ShapeDtypeStruct((16, rows // 16, SSM_WIDTH), F32),
                   jax.ShapeDtypeStruct((16, rows // 16, nl2), F32)],
        scratch_shapes=[pltpu.VMEM((SCAN_SUB, nl2), F32), pltpu.VMEM((2 * SCAN_SUB, nl2), F32)],
        compiler_params=_cparams("arbitrary"),
    )(_scan_view(u), bb_mats, c_mats, ab_rows, ap_rows, d_skip)
    return y.reshape(rows, SSM_WIDTH), s.reshape(rows, nl2)


def _ssm_bwd(name, dy, u, states, bb_mats, c_mats, abc_rows, apc_rows, d_skip):
    rows = u.shape[0]
    nl2 = 2 * SSM_LANES
    nblk = rows // SCAN_BLOCK

    def body(dy_ref, u_ref, st_ref, prev_ref, bb_ref, c_ref, ab_ref, ap_ref, d_ref,
             du_ref, dbb_ref, dc_ref, dab_ref, dd_ref, g_ref, carry_ref, tmp_ref):
        i = pl.program_id(0)

        @pl.when(i == 0)
        def _():
            carry_ref[...] = jnp.zeros_like(carry_ref)
            for ref in (dbb_ref, dc_ref, dab_ref, dd_ref):
                ref[...] = jnp.zeros_like(ref)

        dyv = dy_ref[...].reshape(SCAN_BLOCK, SSM_WIDTH)
        uv = u_ref[...].reshape(SCAN_BLOCK, SSM_WIDTH)
        dyb, ub = dyv.astype(BF16), uv.astype(BF16)
        for p in range(SSM_PAIRS):
            _pair_store(g_ref, p, _dot_nn(dyb[:, _pair_tile(p)], c_ref[p]))
        have_prev = (i < nblk - 1).astype(F32)
        _scan_block(g_ref, carry_ref, tmp_ref, ab_ref, ap_ref, reverse=True,
                    sprev=(st_ref, prev_ref, have_prev, dab_ref))

        def pair_work(p):
            gp = _pair_load(g_ref, p)
            dbb_ref[p] += _dot_tn(ub[:, _pair_tile(p)], gp)
            dc_ref[p] += _dot_tn(dyb[:, _pair_tile(p)], _pair_load(st_ref, p))
            return _dot_nt(gp, bb_ref[p])

        du_ref[...] = (_pair_sum(pair_work) + d_ref[...] * dyv).reshape(16, SCAN_COLS, SSM_WIDTH)
        dd_ref[...] += jnp.sum(dyv * uv, axis=0, keepdims=True)

    const = lambda shape: pl.BlockSpec(shape, lambda i: (0,) * len(shape))
    blk = lambda cols: pl.BlockSpec((16, SCAN_COLS, cols), lambda i: (0, nblk - 1 - i, 0))
    per8 = SCAN_COLS // SCAN_SUB
    prev_spec = pl.BlockSpec((None, SCAN_SUB, nl2), lambda i: (15, jnp.maximum((nblk - 1 - i) * per8 - 1, 0), 0))
    pair_mats = const((SSM_PAIRS, PAIR_TILE, PAIR_TILE))
    pair_shape = jax.ShapeDtypeStruct((SSM_PAIRS, PAIR_TILE, PAIR_TILE), F32)
    sv = _scan_view(states)
    du, dbb, dc, dab, dd = pl.pallas_call(
        body, name=name, grid=(nblk,),
        in_specs=[blk(SSM_WIDTH), blk(SSM_WIDTH), blk(nl2), prev_spec, pair_mats, pair_mats,
                  const((SCAN_SUB, nl2)), const((SCAN_SUB, nl2)), const((1, SSM_WIDTH))],
        out_specs=[blk(SSM_WIDTH), pair_mats, pair_mats, const((SCAN_SUB, nl2)), const((1, SSM_WIDTH))],
        out_shape=[jax.ShapeDtypeStruct((16, rows // 16, SSM_WIDTH), F32), pair_shape, pair_shape,
                   jax.ShapeDtypeStruct((SCAN_SUB, nl2), F32), jax.ShapeDtypeStruct((1, SSM_WIDTH), F32)],
        scratch_shapes=[pltpu.VMEM((16, SCAN_COLS, nl2), F32), pltpu.VMEM((SCAN_SUB, nl2), F32),
                        pltpu.VMEM((2 * SCAN_SUB, nl2), F32)],
        compiler_params=_cparams("arbitrary"),
    )(_scan_view(dy), _scan_view(u), sv, sv, bb_mats, c_mats, abc_rows, apc_rows, d_skip)
    return du.reshape(rows, SSM_WIDTH), dbb, dc, dab, dd


def _adamw(name, w, m, v, gparts, tr):
    rows, cols = w.shape

    def body(w_ref, m_ref, v_ref, g_ref, og_ref, od_ref, om_ref, ov_ref):
        g = g_ref[0].astype(F32)
        for i in range(1, N_DEV):
            g = g + g_ref[i].astype(F32)
        m_new = B1 * m_ref[...] + (1.0 - B1) * g
        v_new = B2 * v_ref[...] + (1.0 - B2) * (g * g)
        m_hat = m_new / (1.0 - B1 ** STEP)
        v_hat = v_new / (1.0 - B2 ** STEP)
        og_ref[...] = g
        od_ref[...] = -LR * (m_hat / (jnp.sqrt(v_hat) + ADAM_EPS) + WD * w_ref[...])
        om_ref[...] = m_new
        ov_ref[...] = v_new

    spec = pl.BlockSpec((tr, cols), lambda i: (i, 0))
    shape = jax.ShapeDtypeStruct((rows, cols), F32)
    return pl.pallas_call(
        body, name=name, grid=(rows // tr,),
        in_specs=[spec, spec, spec, pl.BlockSpec((N_DEV, tr, cols), lambda i: (0, i, 0))],
        out_specs=[spec] * 4, out_shape=[shape] * 4,
        compiler_params=_cparams("parallel"),
    )(w, m, v, gparts)


_SHARDED = (
    ("ffn1_w_gate", True, (352, 1024)), ("ffn1_w_up", True, (352, 1024)), ("ffn1_w_down", False, (352, 1024)),
    ("w_in", True, (608, 1024)), ("ssm_w_glu", True, (128, 512)), ("w_attn_branch", True, (128, 256)),
    ("w_ssm_branch", True, (128, 512)), ("w_out", False, (128, 1024)),
    ("ffn2_w_gate", True, (352, 1024)), ("ffn2_w_up", True, (352, 1024)), ("ffn2_w_down", False, (352, 1024)),
)
_SMALL = ("ffn1_norm", "mix_norm", "gate_bias", "rel_bias_table", "ssm_a_re", "ssm_a_im", "ssm_log_dt",
          "ssm_b_re", "ssm_b_im", "ssm_c_re", "ssm_c_im", "ssm_d", "ffn2_norm", "final_norm")
_ORDER = ("ffn1_norm", "ffn1_w_gate", "ffn1_w_up", "ffn1_w_down", "mix_norm", "w_in", "gate_bias",
          "rel_bias_table", "ssm_a_re", "ssm_a_im", "ssm_log_dt", "ssm_b_re", "ssm_b_im", "ssm_c_re",
          "ssm_c_im", "ssm_d", "ssm_w_glu", "w_attn_branch", "w_ssm_branch", "w_out", "ffn2_norm",
          "ffn2_w_gate", "ffn2_w_up", "ffn2_w_down", "final_norm")


def _pack_rows(shape):
    return shape[0] * shape[1] // D_MODEL


_SHARD_INFO = {nm: (tr, shape) for nm, tr, shape in _SHARDED}
_PHASES = {
    "f1gu": ("ffn1_w_gate", "ffn1_w_up"), "f1d": ("ffn1_w_down",),
    "mix": ("w_in", "ssm_w_glu", "w_attn_branch", "w_ssm_branch", "w_out"),
    "f2": ("ffn2_w_gate", "ffn2_w_up", "ffn2_w_down"),
}


def _to_rows(a, nm):
    tr, shape = _SHARD_INFO[nm]
    return (a.T if tr else a).reshape(_pack_rows(shape), D_MODEL)


def _from_rows(p, nm):
    tr, shape = _SHARD_INFO[nm]
    a = p.reshape(shape)
    return a.T if tr else a


def _full_weight(gathered, nm):
    _, shape = _SHARD_INFO[nm]
    return gathered.reshape(N_DEV * shape[0], shape[1])


def _grad_blocks(g, nm):
    _, shape = _SHARD_INFO[nm]
    return g.astype(BF16).reshape(N_DEV, _pack_rows(shape), D_MODEL)


_SMALL_TILE = 8 * 128


def _small_rows(a):
    flat = a.reshape(-1)
    return jnp.pad(flat, (0, (-flat.shape[0]) % _SMALL_TILE)).reshape(-1, 128)


def _pack_small(ws):
    return jnp.concatenate([_small_rows(ws[nm]) for nm in _SMALL], axis=0)


def _unpack_small(pack, like):
    out, r0 = {}, 0
    for nm in _SMALL:
        n = like[nm].size
        nr = 8 * -(-n // _SMALL_TILE)
        out[nm] = pack[r0:r0 + nr].reshape(-1)[:n].reshape(like[nm].shape)
        r0 += nr
    return out


def _residue_order(a):
    rows, cols = a.shape
    return a.reshape(rows // 16, 16, cols).transpose(1, 0, 2).reshape(rows, cols)


def _token_order(a):
    rows, cols = a.shape
    return a.reshape(16, rows // 16, cols).transpose(1, 0, 2).reshape(rows, cols)


_PAIRS_PER_TILE = PAIR_TILE // (2 * SSM_GROUP)
_PAIR_AXES = (SSM_PAIRS // _PAIRS_PER_TILE, _PAIRS_PER_TILE, 2)


def _pair_matrices(re, im):
    six = jnp.stack([re, im]).reshape((2,) + _PAIR_AXES + (SSM_GROUP, SSM_STATE))
    eye_j, eye_l = jnp.eye(_PAIRS_PER_TILE, dtype=re.dtype), jnp.eye(2, dtype=re.dtype)
    mats = jnp.einsum("xkjlcn,jJ,lL->kjJLcxln", six, eye_j, eye_l)
    return mats.reshape(SSM_PAIRS, PAIR_TILE, PAIR_TILE).astype(BF16)


def _pair_diagonals(acc):
    k, j, l = _PAIR_AXES
    eight = acc.reshape(k, j, j, l, SSM_GROUP, 2, l, SSM_STATE)
    eye_j, eye_l = jnp.eye(j, dtype=acc.dtype), jnp.eye(l, dtype=acc.dtype)
    own = jnp.einsum("kjJLcxln,jJ,lL->xkjlcn", eight, eye_j, eye_l).reshape(2, SSM_GROUPS, SSM_GROUP, SSM_STATE)
    return own[0], own[1]


def _local_step(xs, target, small, weights_of, send_grads, first_deps=()):
    rows = xs.shape[0]
    gfull, gsmall = {}, {}
    wf = dict(weights_of("f1", None))

    x1, h1, gg1, uu1 = _ffn_fwd("ffn1_fwd", xs, small["ffn1_norm"], wf["ffn1_w_gate"], wf["ffn1_w_up"],
                                wf["ffn1_w_down"], deps=first_deps)
    wf.update(weights_of("mix", x1))
    hmix = _rms_fwd("mix_norm_fwd", x1, small["mix_norm"])
    w_in = wf["w_in"]
    w_qkv, w_u, w_g = w_in[:3 * ATTN_WIDTH], w_in[3 * ATTN_WIDTH:3 * ATTN_WIDTH + SSM_WIDTH], w_in[3 * ATTN_WIDTH + SSM_WIDTH:]
    qscale = jnp.concatenate([jnp.full((1, ATTN_WIDTH), HEAD_DIM ** -0.5, F32), jnp.ones((1, 2 * ATTN_WIDTH), F32)], axis=1)
    qkv, = _mm("in_qkv", [(hmix, w_qkv)], True, 3 * ATTN_WIDTH, [BF16],
               epilogue=lambda acc, sc: (acc * sc,), extras=[(qscale, 0)], tn=ATTN_WIDTH)
    u, = _mm("in_u", [(hmix, w_u)], True, SSM_WIDTH, [F32])
    gates, = _mm("in_gates", [(hmix, w_g)], True, 2 * D_MODEL, [F32],
                 epilogue=lambda acc, b: (_sigmoid(acc + b),), extras=[(small["gate_bias"], 0)])

    table_t = small["rel_bias_table"].T
    tables, bias4, o_g, lse_g = [], [], [], []
    for g in range(N_GROUPS):
        bucket, valid = [jnp.asarray(t) for t in _attn_tables(g, rows)]
        bias_g = _bias_fwd(f"rel_bias_fwd_{g}", bucket, valid, table_t[g * HEADS_PER_GROUP:(g + 1) * HEADS_PER_GROUP])
        tables.append(bucket)
        bias4.append(bias_g.reshape(-1, bias_g.shape[-1]))
        o, lse = _attn_fwd(f"attn_fwd_{g}", qkv, g, bias4[g])
        o_g.append(o)
        lse_g.append(lse)
    oa_f32, oa = _combine_fwd("attn_combine_fwd", o_g, lse_g)
    y_attn, = _mm("attn_branch", [(oa, wf["w_attn_branch"])], True, D_MODEL, [F32])

    ab_re, ab_im, ap_re, ap_im, bb_re, bb_im = _ssm_params_fwd(
        "ssm_params_fwd", small["ssm_a_re"], small["ssm_a_im"], small["ssm_log_dt"].reshape(SSM_GROUPS, 1),
        small["ssm_b_re"].transpose(2, 0, 1), small["ssm_b_im"].transpose(2, 0, 1))

    def lanes(re, im, sign=1.0):
        row = jnp.concatenate([re.reshape(1, SSM_LANES), sign * im.reshape(1, SSM_LANES)], axis=1)
        return jnp.broadcast_to(row, (SCAN_SUB, 2 * SSM_LANES))

    bb_mats = _pair_matrices(bb_re.transpose(1, 0, 2), bb_im.transpose(1, 0, 2))
    c_mats = _pair_matrices(small["ssm_c_re"], -small["ssm_c_im"])
    d_skip = small["ssm_d"].reshape(1, SSM_WIDTH)
    y_raw, states = _ssm_fwd("ssm_fwd", u, bb_mats, c_mats, lanes(ab_re, ab_im), lanes(ap_re, ap_im), d_skip)

    def gelu_fn(yv):
        return (jax.nn.gelu(yv),)

    ygelu, = _ew("ssm_gelu", gelu_fn, [y_raw], [SSM_WIDTH], [BF16])
    glu, = _mm("ssm_glu", [(ygelu, wf["ssm_w_glu"])], True, 2 * SSM_WIDTH, [F32])
    ysg, = _ew("ssm_glu_act", lambda gv: (gv[:, :SSM_WIDTH] * _sigmoid(gv[:, SSM_WIDTH:]),), [glu], [SSM_WIDTH], [BF16])
    y_ssm, merged = _mm("ssm_branch_merge", [(ysg, wf["w_ssm_branch"])], True, D_MODEL, [F32, BF16],
                        epilogue=lambda acc, ga, gs, ya: (acc, ga * ya + gs * acc),
                        extras=[(gates, 0), (gates, D_MODEL), (y_attn, 0)])
    x2, = _mm("mix_out", [(merged, wf["w_out"])], False, D_MODEL, [F32],
              epilogue=lambda acc, res: (res + acc,), extras=[(x1, 0)])
    wf.update(weights_of("f2", x2))
    x3, h2, gg2, uu2 = _ffn_fwd("ffn2_fwd", x2, small["ffn2_norm"], wf["ffn2_w_gate"], wf["ffn2_w_up"],
                                wf["ffn2_w_down"])
    dx3, gsmall["final_norm"], loss = _final_loss("final_loss", x3, small["final_norm"].reshape(1, D_MODEL), target)

    dx2, dgg2, duu2, act2, gsmall["ffn2_norm"] = _ffn_bwd(
        "ffn2_bwd", dx3, x2, small["ffn2_norm"], gg2, uu2, wf["ffn2_w_gate"], wf["ffn2_w_up"], wf["ffn2_w_down"])
    gfull["ffn2_w_gate"] = _mm_tn("ffn2_dwg", dgg2, h2, out_dtype=BF16)
    gfull["ffn2_w_up"] = _mm_tn("ffn2_dwu", duu2, h2, out_dtype=BF16)
    gfull["ffn2_w_down"] = _mm_tn("ffn2_dwd", act2, dx3, scale=0.5, out_dtype=BF16)
    sent = send_grads("f2", gfull)

    def merge_bwd(dm, ga, gs, ya, ys):
        return (dm * ga, dm * gs, dm * ya * ga * (1.0 - ga), dm * ys * gs * (1.0 - gs))

    dya, dys, dzga, dzgs = _mm("mix_out_bwd", [(dx2, wf["w_out"])], True, D_MODEL, [BF16] * 4, epilogue=merge_bwd,
                               extras=[(gates, 0), (gates, D_MODEL), (y_attn, 0), (y_ssm, 0)], deps=sent)
    gfull["w_out"] = _mm_tn("dw_out", merged, dx2, out_dtype=BF16)
    gsmall["gate_bias"] = jnp.concatenate([_colsum("dgate_bias_a", dzga), _colsum("dgate_bias_s", dzgs)], axis=1)

    gfull["w_ssm_branch"] = _mm_tn("dw_ssm_branch", dys, ysg, out_dtype=BF16)

    def glu_bwd(dysg, av, bv):
        sb = _sigmoid(bv)
        return (dysg * sb, dysg * av * sb * (1.0 - sb))

    dglu_a, dglu_b = _mm("ssm_branch_bwd", [(dys, wf["w_ssm_branch"])], False, SSM_WIDTH, [BF16, BF16],
                         epilogue=glu_bwd, extras=[(glu, 0), (glu, SSM_WIDTH)])
    w_glu = wf["ssm_w_glu"]
    gfull["ssm_w_glu"] = _mm_tn_stack("dw_glu", [dglu_a, dglu_b], ygelu, out_dtype=BF16)

    def gelu_bwd(acc, yv):
        _, vjp = jax.vjp(jax.nn.gelu, yv)
        return (vjp(acc)[0],)

    dy_raw, = _mm("ssm_glu_bwd", [(dglu_a, w_glu[:SSM_WIDTH]), (dglu_b, w_glu[SSM_WIDTH:])], False, SSM_WIDTH, [F32],
                  epilogue=gelu_bwd, extras=[(y_raw, 0)])
    du, dbb_acc, dc_acc, dab_rows, gsmall_d = _ssm_bwd(
        "ssm_bwd", dy_raw, u, states, bb_mats, c_mats, lanes(ab_re, ab_im, -1.0), lanes(ap_re, ap_im, -1.0), d_skip)
    gsmall["ssm_d"] = gsmall_d
    dbb_re, dbb_im = [a.transpose(1, 0, 2) for a in _pair_diagonals(dbb_acc)]
    dc_re, dc_im = _pair_diagonals(dc_acc)
    gsmall["ssm_c_re"], gsmall["ssm_c_im"] = dc_re, -dc_im
    dab = _colsum("ssm_dab", dab_rows)
    d_ar, d_ai, d_ld, d_br, d_bi = _ssm_params_bwd(
        "ssm_params_bwd", small["ssm_a_re"], small["ssm_a_im"], small["ssm_log_dt"].reshape(SSM_GROUPS, 1),
        small["ssm_b_re"].transpose(2, 0, 1), small["ssm_b_im"].transpose(2, 0, 1),
        dab[:, :SSM_LANES].reshape(SSM_GROUPS, SSM_STATE), dab[:, SSM_LANES:].reshape(SSM_GROUPS, SSM_STATE),
        dbb_re, dbb_im)
    gsmall["ssm_a_re"], gsmall["ssm_a_im"], gsmall["ssm_log_dt"] = d_ar, d_ai, d_ld.reshape(SSM_GROUPS)
    gsmall["ssm_b_re"], gsmall["ssm_b_im"] = d_br.transpose(1, 2, 0), d_bi.transpose(1, 2, 0)

    gfull["w_attn_branch"] = _mm_tn("dw_attn_branch", dya, oa, out_dtype=BF16)
    doa, = _mm("attn_branch_bwd", [(dya, wf["w_attn_branch"])], False, ATTN_OUT, [F32])
    dc = _combine_bwd("attn_combine_bwd", doa, oa_f32, lse_g)
    dqkv_cols = [None] * 9
    dtable = []
    for g in range(N_GROUPS):
        dq, dk, dv, db = _attn_bwd(f"attn_bwd_{g}", qkv, dc[g], lse_g[g], dc[3 + g], g, bias4[g])
        dqkv_cols[g], dqkv_cols[3 + g], dqkv_cols[6 + g] = dq, dk, dv
        dt = _bias_bwd(f"rel_bias_bwd_{g}", tables[g], db.reshape(HEADS_PER_GROUP, -1, db.shape[-1]))
        dtable.append(dt[:, :HEADS_PER_GROUP])
    gsmall["rel_bias_table"] = jnp.concatenate(dtable, axis=1)

    gfull["w_in"] = jnp.concatenate([_mm_tn_stack("dw_in_qkv", dqkv_cols, hmix, out_dtype=BF16),
                                     _mm_tn_stack("dw_in_rest", [du, dzga, dzgs], hmix, out_dtype=BF16)], axis=0)
    sent = send_grads("mix", gfull)
    qkv_pairs = [(c, w_qkv[i * ATTN_OUT:(i + 1) * ATTN_OUT]) for i, c in enumerate(dqkv_cols)]
    dhmix, = _mm("in_bwd", qkv_pairs + [(du, w_u), (dzga, w_g[:D_MODEL]), (dzgs, w_g[D_MODEL:])], False, D_MODEL,
                 [F32], tm=512, deps=sent)
    dx1, gsmall["mix_norm"] = _rms_bwd("mix_norm_bwd", dhmix, x1, small["mix_norm"], dx2)

    dx, dgg1, duu1, act1, gsmall["ffn1_norm"] = _ffn_bwd(
        "ffn1_bwd", dx1, xs, small["ffn1_norm"], gg1, uu1, wf["ffn1_w_gate"], wf["ffn1_w_up"], wf["ffn1_w_down"])
    sent = send_grads("small", gsmall)
    gfull["ffn1_w_down"] = _mm_tn("ffn1_dwd", act1, dx1, scale=0.5, deps=sent, out_dtype=BF16)
    sent = send_grads("f1d", gfull)
    gfull["ffn1_w_gate"] = _mm_tn("ffn1_dwg", dgg1, h1, deps=sent, out_dtype=BF16)
    gfull["ffn1_w_up"] = _mm_tn("ffn1_dwu", duu1, h1, out_dtype=BF16)
    send_grads("f1gu", gfull)
    return loss[0, 0], dx, gsmall


def kernel(x, ffn1_norm, ffn1_w_gate, ffn1_w_up, ffn1_w_down, mix_norm, w_in, gate_bias, rel_bias_table, ssm_a_re, ssm_a_im, ssm_log_dt, ssm_b_re, ssm_b_im, ssm_c_re, ssm_c_im, ssm_d, ssm_w_glu, w_attn_branch, w_ssm_branch, w_out, ffn2_norm, ffn2_w_gate, ffn2_w_up, ffn2_w_down, final_norm, loss_target, m_ffn1_norm, m_ffn1_w_gate, m_ffn1_w_up, m_ffn1_w_down, m_mix_norm, m_w_in, m_gate_bias, m_rel_bias_table, m_ssm_a_re, m_ssm_a_im, m_ssm_log_dt, m_ssm_b_re, m_ssm_b_im, m_ssm_c_re, m_ssm_c_im, m_ssm_d, m_ssm_w_glu, m_w_attn_branch, m_w_ssm_branch, m_w_out, m_ffn2_norm, m_ffn2_w_gate, m_ffn2_w_up, m_ffn2_w_down, m_final_norm, v_ffn1_norm, v_ffn1_w_gate, v_ffn1_w_up, v_ffn1_w_down, v_mix_norm, v_w_in, v_gate_bias, v_rel_bias_table, v_ssm_a_re, v_ssm_a_im, v_ssm_log_dt, v_ssm_b_re, v_ssm_b_im, v_ssm_c_re, v_ssm_c_im, v_ssm_d, v_ssm_w_glu, v_w_attn_branch, v_w_ssm_branch, v_w_out, v_ffn2_norm, v_ffn2_w_gate, v_ffn2_w_up, v_ffn2_w_down, v_final_norm):
    given = dict(locals())
    shapes = {nm: given[nm].shape for nm in _ORDER}

    def strip(a):
        return a[0] if a.ndim >= 2 and a.shape[0] == 1 else a

    w = {nm: strip(given[nm]) for nm in _ORDER}
    m = {nm: strip(given["m_" + nm]) for nm in _ORDER}
    v = {nm: strip(given["v_" + nm]) for nm in _ORDER}
    for d in (w, m, v):
        d["rel_bias_table"] = d["rel_bias_table"].reshape(N_BUCKETS, N_GROUPS * HEADS_PER_GROUP)

    small = {nm: w[nm] for nm in _SMALL}
    small_in = dict(small)
    for nm in ("ffn1_norm", "mix_norm", "ffn2_norm", "gate_bias"):
        small_in[nm] = small[nm].reshape(1, -1)
    w_rows = {nm: _to_rows(w[nm], nm) for nm in _SHARD_INFO}

    def bf16_rows(phase):
        return [w_rows[nm].astype(BF16) for nm in _PHASES[phase]]

    f1_names = _PHASES["f1gu"] + _PHASES["f1d"]
    got_f1 = _all_gather("gather_f1", bf16_rows("f1gu") + bf16_rows("f1d"))
    pending_w = {"mix": _exchange_start("gather_mix_start", bf16_rows("mix"), gather=True, deps=[got_f1[0]])}
    pending_w["f2"] = _exchange_start("gather_f2_start", bf16_rows("f2"), gather=True, deps=[pending_w["mix"][4]])

    def weights_of(phase, after):
        if phase == "f1":
            return {nm: _full_weight(got, nm) for nm, got in zip(f1_names, got_f1)}
        landed = _exchange_wait(f"gather_{phase}_wait", pending_w[phase], after, gather=True)
        return {nm: _full_weight(got, nm) for nm, got in zip(_PHASES[phase], landed)}

    pending_g = {}

    def send_grads(phase, grads):
        if phase == "small":
            gs_pack = _pack_small({nm: grads[nm].reshape(small[nm].shape) for nm in _SMALL})
            pending_g[phase] = _exchange_start("gather_small_start", [gs_pack], gather=True)
        else:
            pending_g[phase] = _exchange_start(f"scatter_{phase}_start",
                                               [_grad_blocks(grads[nm], nm) for nm in _PHASES[phase]], gather=False)
        return [pending_g[phase][4]]

    loss, dx, gsmall = _local_step(_residue_order(x[0]), _residue_order(loss_target[0]), small_in, weights_of,
                                   send_grads, first_deps=[pending_w["f2"][4]])
    dx = _token_order(dx)

    updated = {}
    after = pending_g["f1gu"][4]
    for phase in ("f2", "mix", "f1d", "small", "f1gu"):
        landed = _exchange_wait(f"exchange_{phase}_wait", pending_g[phase], after, gather=phase == "small")
        if phase == "small":
            sm = _adamw("adamw_small", _pack_small(small), _pack_small({nm: m[nm] for nm in _SMALL}),
                        _pack_small({nm: v[nm] for nm in _SMALL}), landed[0], landed[0].shape[1])
            after = sm[0]
            continue
        for nm, recv in zip(_PHASES[phase], landed):
            tr = max(t for t in range(16, 353, 16) if w_rows[nm].shape[0] % t == 0)
            updated[nm] = _adamw(f"adamw_{nm}", w_rows[nm], _to_rows(m[nm], nm), _to_rows(v[nm], nm), recv, tr)
            after = updated[nm][0]

    loss = lax.psum(loss, ("x", "y", "c"))
    outs = []
    for i in range(4):
        sml = _unpack_small(sm[i], small)
        outs.append([(_from_rows(updated[nm][i], nm) if nm in updated else sml[nm]).reshape(shapes[nm])
                     for nm in _ORDER])
    return (loss, dx[None], *outs[0], *outs[1], *outs[2], *outs[3])
```

```python
import math

import numpy as np
import jax
import jax.numpy as jnp
from jax import lax
from jax.experimental import pallas as pl
from jax.experimental.pallas import tpu as pltpu

F32 = jnp.float32
BF16 = jnp.bfloat16

N_DEV = 8
D_MODEL = 1024
D_FF = 2816
HEAD_DIM = 64
HEADS_PER_GROUP = 4
DILATIONS = (1, 4, 16)
N_GROUPS = 3
ATTN_WIDTH = 768
ATTN_OUT = 256
BLOCK = 128
N_BUCKETS = 32
MAX_DISTANCE = 2048
NEG_INF = -1e30
SSM_WIDTH = 512
SSM_GROUPS = 32
SSM_GROUP = 16
SSM_STATE = 64
SSM_LANES = SSM_GROUPS * SSM_STATE
SSM_PAIRS = SSM_GROUPS // 2
PAIR_LANES = 2 * SSM_STATE
PAIR_TILE = 256
EPS = 1e-6
LR, B1, B2, ADAM_EPS, WD, STEP = 0.001, 0.9, 0.999, 1e-08, 0.01, 10

VMEM_LIMIT_BYTES = 56 * 1024 * 1024
FFN_CHUNK = 768
SCAN_BLOCK = 256
SCAN_STEPS = 16
SCAN_COLS = SCAN_BLOCK // SCAN_STEPS
SCAN_SUB = 8
SCAN_LANES = 512

MESH = pl.DeviceIdType.MESH


def _cparams(*sem):
    return pltpu.CompilerParams(dimension_semantics=sem, vmem_limit_bytes=VMEM_LIMIT_BYTES)


def _dot(a, b, dims):
    return lax.dot_general(a, b, (dims, ((), ())), preferred_element_type=F32)


def _dot_nn(a, b):
    return _dot(a, b, ((1,), (0,)))


def _dot_nt(a, b):
    return _dot(a, b, ((1,), (1,)))


def _dot_tn(a, b):
    return _dot(a, b, ((0,), (0,)))


def _sigmoid(x):
    return 1.0 / (1.0 + jnp.exp(-x))


def _all_gather(name, xs_list):
    n = len(xs_list)

    def body(*refs):
        x_refs, out_refs = refs[:n], refs[n:2 * n]
        send_sems, recv_sems, local_sems = refs[2 * n:]
        x, y, c = lax.axis_index("x"), lax.axis_index("y"), lax.axis_index("c")
        me, sibling = (x, y, c), (x, y, 1 - c)
        chips = [(1 - x, y), (x, 1 - y), (1 - x, 1 - y)]

        def copy(a, k, block, to, own=False):
            px, py, pc = block
            rows = out_refs[a].at[4 * px + 2 * py + pc]
            return pltpu.make_async_remote_copy(
                src_ref=x_refs[a] if own else rows, dst_ref=rows,
                send_sem=send_sems.at[7 * a + k], recv_sem=recv_sems.at[7 * a + k], device_id=to,
                device_id_type=MESH)

        mine = [pltpu.make_async_copy(x_refs[a], out_refs[a].at[4 * x + 2 * y + c], local_sems.at[a]) for a in range(n)]
        first = []
        for a in range(n):
            mine[a].start()
            first.append(copy(a, 0, me, sibling, own=True))
            first += [copy(a, 1 + j, me, (*chip, c), own=True) for j, chip in enumerate(chips)]
        for cp in first:
            cp.start()
        passed = []
        for a in range(n):
            for j, chip in enumerate(chips):
                copy(a, 1 + j, (*chip, c), me).wait_recv()
                passed.append(copy(a, 4 + j, (*chip, c), sibling))
                passed[-1].start()
        for a in range(n):
            copy(a, 0, sibling, me).wait_recv()
            for j, chip in enumerate(chips):
                copy(a, 4 + j, (*chip, 1 - c), me).wait_recv()
        for cp in first + passed:
            cp.wait_send()
        for cp in mine:
            cp.wait()

    return pl.pallas_call(
        body, name=name,
        out_shape=[jax.ShapeDtypeStruct((N_DEV, *xs.shape), xs.dtype) for xs in xs_list],
        in_specs=[_ANY_SPEC] * n, out_specs=[_ANY_SPEC] * n,
        scratch_shapes=[pltpu.SemaphoreType.DMA((7 * n,)), pltpu.SemaphoreType.DMA((7 * n,)),
                        pltpu.SemaphoreType.DMA((n,))],
    )(*xs_list)


_HBM_SPEC = pl.BlockSpec(memory_space=pltpu.HBM)
_SEM_SPEC = pl.BlockSpec(memory_space=pltpu.SEMAPHORE)
_ANY_SPEC = pl.BlockSpec(memory_space=pl.ANY)
_EFFECT = pltpu.SideEffectType.DATAFLOW_SIDE_EFFECTING


def _peers(x, y, c):
    return [(1 - x if k & 4 else x, 1 - y if k & 2 else y, 1 - c if k & 1 else c) for k in range(1, N_DEV)]


def _exchange_copies(x_refs, land_refs, send_sems, recv_sems, gather):
    x, y, c = lax.axis_index("x"), lax.axis_index("y"), lax.axis_index("c")
    me = 4 * x + 2 * y + c
    copies = []
    for a, (x_ref, land_ref) in enumerate(zip(x_refs, land_refs)):
        for k, (px, py, pc) in enumerate(_peers(x, y, c)):
            src = x_ref if gather else x_ref.at[4 * px + 2 * py + pc]
            copies.append(pltpu.make_async_remote_copy(
                src_ref=src, dst_ref=land_ref.at[me], send_sem=send_sems.at[N_DEV * a + k],
                recv_sem=recv_sems.at[(N_DEV - 1) * a + k], device_id=(px, py, pc), device_id_type=MESH))
    owns = [pltpu.make_async_copy(x_ref if gather else x_ref.at[me], land_ref.at[me],
                                  send_sems.at[N_DEV * a + N_DEV - 1])
            for a, (x_ref, land_ref) in enumerate(zip(x_refs, land_refs))]
    return owns, copies


def _exchange_start(name, xs_list, gather, deps=()):
    n, nd = len(xs_list), len(deps)
    land_shapes = [(N_DEV, *xs.shape) if gather else xs.shape for xs in xs_list]

    def body(*refs):
        x_refs, land_refs = refs[:n], refs[n:2 * n]
        send_sems, recv_sems = refs[2 * n + nd:2 * n + nd + 2]
        token = refs[-1]
        owns, copies = _exchange_copies(x_refs, land_refs, send_sems, recv_sems, gather)
        for cp in copies + owns:
            cp.start()
        token[...] = jnp.zeros_like(token)

    hbm = lambda a: pltpu.with_memory_space_constraint(a, pltpu.HBM)
    outs = pl.pallas_call(
        body, name=name,
        out_shape=(pltpu.SemaphoreType.DMA((n * N_DEV,)), pltpu.SemaphoreType.DMA((n * (N_DEV - 1),)),
                   *[pltpu.HBM(xs.shape, xs.dtype) for xs in xs_list],
                   *[pltpu.HBM(shape, xs.dtype) for shape, xs in zip(land_shapes, xs_list)],
                   jax.ShapeDtypeStruct((8, 128), F32)),
        in_specs=(_HBM_SPEC,) * (2 * n) + (_ANY_SPEC,) * nd,
        out_specs=(_SEM_SPEC, _SEM_SPEC) + (_HBM_SPEC,) * (2 * n) + (pl.BlockSpec(memory_space=pltpu.VMEM),),
        input_output_aliases={i: 2 + i for i in range(2 * n)},
        compiler_params=pltpu.CompilerParams(has_side_effects=_EFFECT),
    )(*[hbm(xs) for xs in xs_list], *[hbm(lax.empty(shape, xs.dtype)) for shape, xs in zip(land_shapes, xs_list)],
      *deps)
    return outs[0], outs[1], list(outs[2:2 + n]), list(outs[2 + n:2 + 2 * n]), outs[-1]


def _exchange_wait(name, handle, after, gather):
    send_sems, recv_sems, xs_thru, lands_thru, _ = handle
    n = len(xs_thru)

    def body(*refs):
        x_refs, land_refs = refs[:n], refs[n:2 * n]
        send_sems, recv_sems = refs[2 * n:2 * n + 2]
        owns, copies = _exchange_copies(x_refs, land_refs, send_sems, recv_sems, gather)
        for cp in copies:
            cp.wait_send()
            cp.wait_recv()
        for cp in owns:
            cp.wait()

    outs = pl.pallas_call(
        body, name=name,
        out_shape=tuple(pltpu.HBM(a.shape, a.dtype) for a in xs_thru + lands_thru),
        in_specs=(_HBM_SPEC,) * (2 * n) + (_SEM_SPEC, _SEM_SPEC, _ANY_SPEC),
        out_specs=(_HBM_SPEC,) * (2 * n), input_output_aliases={i: i for i in range(2 * n)},
        compiler_params=pltpu.CompilerParams(has_side_effects=_EFFECT),
    )(*xs_thru, *lands_thru, send_sems, recv_sems, after)
    return list(outs[n:])


def _mm(name, pairs, nt, n_cols, out_dtypes, epilogue=None, extras=(), tm=1024, tn=512, deps=()):
    rows = pairs[0][0].shape[0]
    tm = min(tm, rows)
    tn = min(tn, n_cols)
    na, ne, nd = len(pairs), len(extras), len(deps)

    def body(*refs):
        a_refs, w_refs = refs[:na], refs[na:2 * na]
        e_refs, o_refs = refs[2 * na:2 * na + ne], refs[2 * na + ne + nd:]
        acc = None
        for a_ref, w_ref in zip(a_refs, w_refs):
            a = a_ref[...].astype(BF16)
            w = w_ref[...].astype(BF16)
            p = _dot_nt(a, w) if nt else _dot_nn(a, w)
            acc = p if acc is None else acc + p
        outs = (acc,) if epilogue is None else epilogue(acc, *[e[...] for e in e_refs])
        for o_ref, o in zip(o_refs, outs):
            o_ref[...] = o.astype(o_ref.dtype)

    in_specs = [pl.BlockSpec((tm, a.shape[1]), lambda i, j: (i, 0)) for a, _ in pairs]
    for _, w in pairs:
        if nt:
            in_specs.append(pl.BlockSpec((tn, w.shape[1]), lambda i, j: (j, 0)))
        else:
            in_specs.append(pl.BlockSpec((w.shape[0], tn), lambda i, j: (0, j)))
    for e, col_off in extras:
        off = col_off // tn
        if e.shape[0] == 1:
            in_specs.append(pl.BlockSpec((1, tn), lambda i, j, off=off: (0, j + off)))
        else:
            in_specs.append(pl.BlockSpec((tm, tn), lambda i, j, off=off: (i, j + off)))
    in_specs += [_ANY_SPEC] * nd
    out_specs = [pl.BlockSpec((tm, tn), lambda i, j: (i, j)) for _ in out_dtypes]
    outs = pl.pallas_call(
        body, name=name, grid=(rows // tm, n_cols // tn),
        in_specs=in_specs, out_specs=out_specs,
        out_shape=[jax.ShapeDtypeStruct((rows, n_cols), dt) for dt in out_dtypes],
        compiler_params=_cparams("parallel", "arbitrary"),
    )(*[a for a, _ in pairs], *[w for _, w in pairs], *[e for e, _ in extras], *deps)
    return outs


def _tn_rows(m):
    return max(b for b in range(128, min(m, 1408) + 1, 128) if m % b == 0)


def _mm_tn(name, a, b, scale=1.0, bm=None, tk=1024, deps=(), out_dtype=F32):
    rows, m = a.shape
    n = b.shape[1]
    bm = _tn_rows(m) if bm is None else bm
    tk = min(tk, rows)
    nk = rows // tk

    def body(a_ref, b_ref, *rest):
        o_ref, acc_ref = rest[-2:]
        k = pl.program_id(1)

        @pl.when(k == 0)
        def _():
            acc_ref[...] = jnp.zeros_like(acc_ref)

        acc_ref[...] += _dot_tn(a_ref[...].astype(BF16), b_ref[...].astype(BF16))

        @pl.when(k == nk - 1)
        def _():
            o_ref[...] = (acc_ref[...] * scale).astype(o_ref.dtype)

    return pl.pallas_call(
        body, name=name, grid=(m // bm, nk),
        in_specs=[pl.BlockSpec((tk, bm), lambda i, k: (k, i)), pl.BlockSpec((tk, n), lambda i, k: (k, 0))]
        + [_ANY_SPEC] * len(deps),
        out_specs=pl.BlockSpec((bm, n), lambda i, k: (i, 0)),
        out_shape=jax.ShapeDtypeStruct((m, n), out_dtype),
        scratch_shapes=[pltpu.VMEM((bm, n), F32)],
        compiler_params=_cparams("parallel", "arbitrary"),
    )(a, b, *deps)


def _mm_tn_stack(name, a_list, b, tk=1024, out_dtype=F32):
    rows, n = b.shape
    ms = [a.shape[1] for a in a_list]
    tk = min(tk, rows)
    nk = rows // tk
    na = len(a_list)

    def body(*refs):
        a_refs, b_ref, o_ref, acc_ref = refs[:na], refs[na], refs[na + 1], refs[na + 2]
        k = pl.program_id(0)

        @pl.when(k == 0)
        def _():
            acc_ref[...] = jnp.zeros_like(acc_ref)

        bv = b_ref[...].astype(BF16)
        r0 = 0
        for a_ref, m in zip(a_refs, ms):
            acc_ref[r0:r0 + m, :] += _dot_tn(a_ref[...].astype(BF16), bv)
            r0 += m

        @pl.when(k == nk - 1)
        def _():
            o_ref[...] = acc_ref[...].astype(o_ref.dtype)

    return pl.pallas_call(
        body, name=name, grid=(nk,),
        in_specs=[pl.BlockSpec((tk, m), lambda k: (k, 0)) for m in ms] + [pl.BlockSpec((tk, n), lambda k: (k, 0))],
        out_specs=pl.BlockSpec((sum(ms), n), lambda k: (0, 0)),
        out_shape=jax.ShapeDtypeStruct((sum(ms), n), out_dtype),
        scratch_shapes=[pltpu.VMEM((sum(ms), n), F32)],
        compiler_params=_cparams("arbitrary"),
    )(*a_list, b)


def _colsum(name, xs, tm=512):
    rows, cols = xs.shape
    tm = min(tm, rows)

    def body(x_ref, o_ref):
        @pl.when(pl.program_id(0) == 0)
        def _():
            o_ref[...] = jnp.zeros_like(o_ref)

        o_ref[...] += jnp.sum(x_ref[...].astype(F32), axis=0, keepdims=True)

    return pl.pallas_call(
        body, name=name, grid=(rows // tm,),
        in_specs=[pl.BlockSpec((tm, cols), lambda i: (i, 0))],
        out_specs=pl.BlockSpec((1, cols), lambda i: (0, 0)),
        out_shape=jax.ShapeDtypeStruct((1, cols), F32),
        compiler_params=_cparams("arbitrary"),
    )(xs)


def _ew(name, fn, ins, out_cols, out_dtypes, tm=512):
    rows = ins[0].shape[0]
    tm = min(tm, rows)
    ni = len(ins)

    def body(*refs):
        outs = fn(*[r[...] for r in refs[:ni]])
        for o_ref, o in zip(refs[ni:], outs):
            o_ref[...] = o.astype(o_ref.dtype)

    def spec(shape):
        if shape[0] == 1:
            return pl.BlockSpec((1, shape[1]), lambda i: (0, 0))
        return pl.BlockSpec((tm, shape[1]), lambda i: (i, 0))

    return pl.pallas_call(
        body, name=name, grid=(rows // tm,),
        in_specs=[spec(a.shape) for a in ins],
        out_specs=[pl.BlockSpec((tm, c), lambda i: (i, 0)) for c in out_cols],
        out_shape=[jax.ShapeDtypeStruct((rows, c), dt) for c, dt in zip(out_cols, out_dtypes)],
        compiler_params=_cparams("parallel"),
    )(*ins)


def _rms_parts(xv):
    r = lax.rsqrt(jnp.mean(xv * xv, axis=-1, keepdims=True) + EPS)
    return r, xv * r


def _rms_bwd_dx(dh, gain, r, xh):
    dxh = dh * gain
    return r * (dxh - xh * jnp.mean(dxh * xh, axis=-1, keepdims=True))


def _rms_fwd(name, xs, gain):
    def fn(xv, g):
        _, xh = _rms_parts(xv)
        return (xh * g,)

    return _ew(name, fn, [xs, gain], [xs.shape[1]], [BF16])[0]


def _rms_bwd(name, dh, xs, gain, dres, tm=512):
    rows, d = xs.shape
    tm = min(tm, rows)

    def body(dh_ref, x_ref, g_ref, dres_ref, dx_ref, dg_ref):
        r, xh = _rms_parts(x_ref[...])
        dhv = dh_ref[...]
        dx_ref[...] = dres_ref[...] + _rms_bwd_dx(dhv, g_ref[...], r, xh)

        @pl.when(pl.program_id(0) == 0)
        def _():
            dg_ref[...] = jnp.zeros_like(dg_ref)

        dg_ref[...] += jnp.sum(dhv * xh, axis=0, keepdims=True)

    tile = pl.BlockSpec((tm, d), lambda i: (i, 0))
    row = pl.BlockSpec((1, d), lambda i: (0, 0))
    return pl.pallas_call(
        body, name=name, grid=(rows // tm,),
        in_specs=[tile, tile, row, tile], out_specs=[tile, row],
        out_shape=[jax.ShapeDtypeStruct((rows, d), F32), jax.ShapeDtypeStruct((1, d), F32)],
        compiler_params=_cparams("arbitrary"),
    )(dh, xs, gain, dres)


def _ffn_chunks(f_all):
    return [slice(c, min(c + FFN_CHUNK, f_all)) for c in range(0, f_all, FFN_CHUNK)]


def _ffn_fwd(name, xs, gain, wg_t, wu_t, wd, tm=512, deps=()):
    rows, d = xs.shape
    f_all = wd.shape[0]
    tm = min(tm, rows)

    def body(x_ref, g_ref, wg_ref, wu_ref, wd_ref, *rest):
        xo_ref, h_ref, gg_ref, uu_ref = rest[-4:]
        xv = x_ref[...]
        _, xh = _rms_parts(xv)
        h = (xh * g_ref[...]).astype(BF16)
        h_ref[...] = h
        acc = None
        for cols in _ffn_chunks(f_all):
            gg = _dot_nt(h, wg_ref[cols, :])
            uu = _dot_nt(h, wu_ref[cols, :])
            act = gg * _sigmoid(gg) * uu
            part = _dot_nn(act.astype(BF16), wd_ref[cols, :])
            acc = part if acc is None else acc + part
            gg_ref[:, cols] = gg.astype(BF16)
            uu_ref[:, cols] = uu.astype(BF16)
        xo_ref[...] = xv + 0.5 * acc

    tile = pl.BlockSpec((tm, d), lambda i: (i, 0))
    wspec = pl.BlockSpec((f_all, d), lambda i: (0, 0), pipeline_mode=pl.Buffered(1))
    hid = pl.BlockSpec((tm, f_all), lambda i: (i, 0))
    return pl.pallas_call(
        body, name=name, grid=(rows // tm,),
        in_specs=[tile, pl.BlockSpec((1, d), lambda i: (0, 0)), wspec, wspec, wspec] + [_ANY_SPEC] * len(deps),
        out_specs=[tile, tile, hid, hid],
        out_shape=[jax.ShapeDtypeStruct((rows, d), F32), jax.ShapeDtypeStruct((rows, d), BF16),
                   jax.ShapeDtypeStruct((rows, f_all), BF16), jax.ShapeDtypeStruct((rows, f_all), BF16)],
        compiler_params=_cparams("parallel"),
    )(xs, gain, wg_t, wu_t, wd, *deps)


def _ffn_bwd(name, dxo, xs, gain, gg_all, uu_all, wg_t, wu_t, wd, tm=256):
    rows, d = xs.shape
    f_all = wd.shape[0]
    tm = min(tm, rows)

    def body(dxo_ref, x_ref, g_ref, gg_ref, uu_ref, wg_ref, wu_ref, wd_ref,
             dx_ref, dgg_ref, duu_ref, act_ref, dgain_ref):
        dxo = dxo_ref[...]
        df = (0.5 * dxo).astype(BF16)
        dh = None
        for cols in _ffn_chunks(f_all):
            gg = gg_ref[:, cols].astype(F32)
            uu = uu_ref[:, cols].astype(F32)
            sg = _sigmoid(gg)
            silu = gg * sg
            dact = _dot_nt(df, wd_ref[cols, :])
            duu = (dact * silu).astype(BF16)
            dgg = (dact * uu * (sg * (1.0 + gg * (1.0 - sg)))).astype(BF16)
            act_ref[:, cols] = (silu * uu).astype(BF16)
            dgg_ref[:, cols] = dgg
            duu_ref[:, cols] = duu
            part = _dot_nn(dgg, wg_ref[cols, :]) + _dot_nn(duu, wu_ref[cols, :])
            dh = part if dh is None else dh + part
        r, xh = _rms_parts(x_ref[...])
        dx_ref[...] = dxo + _rms_bwd_dx(dh, g_ref[...], r, xh)

        @pl.when(pl.program_id(0) == 0)
        def _():
            dgain_ref[...] = jnp.zeros_like(dgain_ref)

        dgain_ref[...] += jnp.sum(dh * xh, axis=0, keepdims=True)

    tile = pl.BlockSpec((tm, d), lambda i: (i, 0))
    row = pl.BlockSpec((1, d), lambda i: (0, 0))
    wspec = pl.BlockSpec((f_all, d), lambda i: (0, 0), pipeline_mode=pl.Buffered(1))
    hid = pl.BlockSpec((tm, f_all), lambda i: (i, 0))
    hid_shape = jax.ShapeDtypeStruct((rows, f_all), BF16)
    return pl.pallas_call(
        body, name=name, grid=(rows // tm,),
        in_specs=[tile, tile, row, hid, hid, wspec, wspec, wspec],
        out_specs=[tile, hid, hid, hid, row],
        out_shape=[jax.ShapeDtypeStruct((rows, d), F32), hid_shape, hid_shape, hid_shape,
                   jax.ShapeDtypeStruct((1, d), F32)],
        compiler_params=_cparams("arbitrary"),
    )(dxo, xs, gain, gg_all, uu_all, wg_t, wu_t, wd)


def _final_loss(name, xs, gain, target, tm=512):
    rows, d = xs.shape
    tm = min(tm, rows)

    def body(x_ref, g_ref, t_ref, dx_ref, dg_ref, loss_ref):
        r, xh = _rms_parts(x_ref[...])
        gain_v = g_ref[...]
        err = xh * gain_v - t_ref[...]
        dy = err * (1.0 / d)
        dx_ref[...] = _rms_bwd_dx(dy, gain_v, r, xh)

        @pl.when(pl.program_id(0) == 0)
        def _():
            dg_ref[...] = jnp.zeros_like(dg_ref)
            loss_ref[...] = jnp.zeros_like(loss_ref)

        dg_ref[...] += jnp.sum(dy * xh, axis=0, keepdims=True)
        per_tok = jnp.mean(err * err, axis=-1, keepdims=True)
        loss_ref[...] += 0.5 * jnp.sum(per_tok, axis=0, keepdims=True)

    tile = pl.BlockSpec((tm, d), lambda i: (i, 0))
    row = pl.BlockSpec((1, d), lambda i: (0, 0))
    return pl.pallas_call(
        body, name=name, grid=(rows // tm,),
        in_specs=[tile, row, tile],
        out_specs=[tile, row, pl.BlockSpec((1, 1), lambda i: (0, 0))],
        out_shape=[jax.ShapeDtypeStruct((rows, d), F32), jax.ShapeDtypeStruct((1, d), F32),
                   jax.ShapeDtypeStruct((1, 1), F32)],
        compiler_params=_cparams("arbitrary"),
    )(xs, gain, target)


def _t5_bucket_np(dist):
    max_exact = N_BUCKETS // 2
    dd = np.maximum(dist, 1).astype(np.float32)
    large = max_exact + (np.log(dd / np.float32(max_exact)) / np.float32(math.log(MAX_DISTANCE / max_exact))
                         * np.float32(N_BUCKETS - max_exact)).astype(np.int32)
    large = np.minimum(large, N_BUCKETS - 1)
    return np.where(dist < max_exact, dist, large).astype(np.int32)


def _attn_geometry(g, rows):
    run = rows // 16
    dil = DILATIONS[g]
    if dil == 16:
        bq = BLOCK
        return dict(view=(16, run), block=(None, bq), grid=(16, run // bq), index=lambda r, n: (r, n),
                    pos=np.arange(bq), bq=bq)
    if dil == 4:
        per = BLOCK // 4
        pos = (4 * np.arange(per)[None, :] + np.arange(4)[:, None]).reshape(-1)
        return dict(view=(4, 4, run), block=(4, None, per), grid=(4, run // per), index=lambda r, n: (0, r, n),
                    pos=pos, bq=BLOCK)
    per = 16
    pos = (16 * np.arange(per)[None, :] + np.arange(16)[:, None]).reshape(-1)
    return dict(view=(16, run), block=(16, per), grid=(1, run // per), index=lambda r, n: (0, n),
                pos=pos, bq=16 * per)


def _attn_tables(g, rows):
    geo = _attn_geometry(g, rows)
    pos, bq = geo["pos"], geo["bq"]
    steps = pos[:, None] - np.concatenate([pos - bq, pos])[None, :]
    valid = (steps >= 0) & (steps <= BLOCK)
    bucket = _t5_bucket_np((np.maximum(steps, 0) * DILATIONS[g]).astype(np.int32))
    return bucket, valid.astype(np.int32)


def _bias_fwd(name, bucket, valid, table_t):
    bq = bucket.shape[0]

    def body(bk_ref, ok_ref, tab_ref, o_ref):
        bk = bk_ref[...]
        ok = ok_ref[...] > 0
        for h in range(HEADS_PER_GROUP):
            acc = jnp.zeros(bk.shape, F32)
            for b in range(N_BUCKETS):
                acc = jnp.where(bk == b, tab_ref[h, b], acc)
            o_ref[h] = jnp.where(ok, acc, NEG_INF)

    vm = pl.BlockSpec(memory_space=pltpu.VMEM)
    return pl.pallas_call(
        body, name=name, in_specs=[vm, vm, pl.BlockSpec(memory_space=pltpu.SMEM)], out_specs=vm,
        out_shape=jax.ShapeDtypeStruct((HEADS_PER_GROUP, bq, 2 * bq), F32),
    )(bucket, valid, table_t)


def _bias_bwd(name, bucket, dbias):
    def body(bk_ref, db_ref, o_ref):
        row_id = lax.broadcasted_iota(jnp.int32, (N_BUCKETS, 128), 0)
        col_id = lax.broadcasted_iota(jnp.int32, (N_BUCKETS, 128), 1)
        bk = bk_ref[...]
        acc = jnp.zeros((N_BUCKETS, 128), F32)
        for h in range(HEADS_PER_GROUP):
            db = db_ref[h]
            for b in range(N_BUCKETS):
                part = jnp.sum(jnp.where(bk == b, db, 0.0), axis=0, keepdims=True)
                tot = jnp.sum(part, axis=1, keepdims=True)
                acc = jnp.where((row_id == b) & (col_id == h), tot, acc)
        o_ref[...] = acc

    vm = pl.BlockSpec(memory_space=pltpu.VMEM)
    return pl.pallas_call(body, name=name, in_specs=[vm, vm], out_specs=vm,
                          out_shape=jax.ShapeDtypeStruct((N_BUCKETS, 128), F32))(bucket, dbias)


def _head_of_lane(nrows):
    return lax.broadcasted_iota(jnp.int32, (nrows, ATTN_OUT), 1) // HEAD_DIM


def _stack_heads(a, lane_head):
    zero = jnp.zeros_like(a)
    return jnp.concatenate([jnp.where(lane_head == h, a, zero) for h in range(HEADS_PER_GROUP)], axis=0)


def _unstack_heads(a4, lane_head, bq):
    out = a4[:bq]
    for h in range(1, HEADS_PER_GROUP):
        out = jnp.where(lane_head == h, a4[h * bq:(h + 1) * bq], out)
    return out


def _attn_specs(geo, cols, col_block, index):
    return pl.BlockSpec(geo["block"] + (cols,), lambda r, n: index(r, n) + (col_block,))


def _attn_fwd(name, qkv, g, bias4):
    rows = qkv.shape[0]
    geo = _attn_geometry(g, rows)
    bq, (nsub, nb), index = geo["bq"], geo["grid"], geo["index"]
    blk_shape = tuple(b for b in geo["block"] if b is not None) + (ATTN_OUT,)

    def body(q_ref, kc_ref, kp_ref, vc_ref, vp_ref, b_ref, o_ref, lse_ref):
        n = pl.program_id(1)
        lane_head = _head_of_lane(bq)
        flat = lambda ref: ref[...].reshape(bq, ATTN_OUT)
        q4 = _stack_heads(flat(q_ref), lane_head)
        k2 = jnp.concatenate([flat(kp_ref), flat(kc_ref)], axis=0)
        v2 = jnp.concatenate([flat(vp_ref), flat(vc_ref)], axis=0)
        s = _dot_nt(q4, k2) + b_ref[...]
        col = lax.broadcasted_iota(jnp.int32, s.shape, 1)
        s = jnp.where((col >= bq) | (n > 0), s, NEG_INF)
        mx = jnp.max(s, axis=-1, keepdims=True)
        p = jnp.exp(s - mx)
        den = jnp.sum(p, axis=-1, keepdims=True)
        o4 = _dot_nn(p.astype(BF16), v2) / den
        lse4 = jnp.broadcast_to(mx + jnp.log(den), (HEADS_PER_GROUP * bq, ATTN_OUT))
        o_ref[...] = _unstack_heads(o4, lane_head, bq).reshape(blk_shape)
        lse_ref[...] = _unstack_heads(lse4, lane_head, bq).reshape(blk_shape)

    prev = lambda r, n: index(r, jnp.maximum(n - 1, 0))
    view = lambda a: a.reshape(geo["view"] + (a.shape[1],))
    qkv_v = view(qkv)
    out_spec = _attn_specs(geo, ATTN_OUT, 0, index)
    out_shape = jax.ShapeDtypeStruct(geo["view"] + (ATTN_OUT,), F32)
    o, lse = pl.pallas_call(
        body, name=name, grid=(nsub, nb),
        in_specs=[_attn_specs(geo, ATTN_OUT, g, index), _attn_specs(geo, ATTN_OUT, 3 + g, index),
                  _attn_specs(geo, ATTN_OUT, 3 + g, prev), _attn_specs(geo, ATTN_OUT, 6 + g, index),
                  _attn_specs(geo, ATTN_OUT, 6 + g, prev), pl.BlockSpec(bias4.shape, lambda r, n: (0, 0))],
        out_specs=[out_spec, out_spec], out_shape=[out_shape, out_shape],
        compiler_params=_cparams("parallel", "arbitrary"),
    )(qkv_v, qkv_v, qkv_v, qkv_v, qkv_v, bias4)
    return o.reshape(rows, ATTN_OUT), lse.reshape(rows, ATTN_OUT)


def _attn_bwd(name, qkv, do, lse, cvec, g, bias4):
    rows = qkv.shape[0]
    geo = _attn_geometry(g, rows)
    bq, (nsub, nb), index = geo["bq"], geo["grid"], geo["index"]
    blk_shape = tuple(b for b in geo["block"] if b is not None) + (ATTN_OUT,)
    nlead = len(blk_shape) - 1

    def body(q_ref, kc_ref, kp_ref, vc_ref, vp_ref, do_ref, lse_ref, c_ref, b_ref,
             dq_ref, dk_ref, dv_ref, db_ref, kcar_ref, vcar_ref):
        r, n = pl.program_id(0), pl.program_id(1)
        valid = n < nb
        lane_head = _head_of_lane(bq)
        flat = lambda ref: ref[...].reshape(bq, ATTN_OUT)

        @pl.when((r == 0) & (n == 0))
        def _():
            kcar_ref[...] = jnp.zeros_like(kcar_ref)
            vcar_ref[...] = jnp.zeros_like(vcar_ref)
            db_ref[...] = jnp.zeros_like(db_ref)

        def column(ref, h):
            lead = (slice(None),) * nlead
            return ref[lead + (pl.ds(h * HEAD_DIM, 1),)].reshape(bq, 1)

        q4 = _stack_heads(flat(q_ref), lane_head)
        do4 = _stack_heads(flat(do_ref), lane_head)
        k2 = jnp.concatenate([flat(kp_ref), flat(kc_ref)], axis=0)
        v2 = jnp.concatenate([flat(vp_ref), flat(vc_ref)], axis=0)
        lse4 = jnp.concatenate([column(lse_ref, h) for h in range(HEADS_PER_GROUP)], axis=0)
        c4 = jnp.concatenate([column(c_ref, h) for h in range(HEADS_PER_GROUP)], axis=0)
        s = _dot_nt(q4, k2) + b_ref[...]
        col = lax.broadcasted_iota(jnp.int32, s.shape, 1)
        keep = ((col >= bq) | (n > 0)) & valid
        p = jnp.where(keep, jnp.exp(s - lse4), 0.0)
        ds = p * (_dot_nt(do4, v2) + c4)
        ds_b = ds.astype(BF16)

        @pl.when(valid)
        def _():
            dq = _unstack_heads(_dot_nn(ds_b, k2), lane_head, bq) * (HEAD_DIM ** -0.5)
            dq_ref[...] = dq.astype(BF16).reshape(blk_shape)

        dk2 = _dot_tn(ds_b, q4)
        dv2 = _dot_tn(p.astype(BF16), do4)
        dk_ref[...] = (kcar_ref[...] + dk2[:bq]).astype(BF16).reshape(blk_shape)
        dv_ref[...] = (vcar_ref[...] + dv2[:bq]).astype(BF16).reshape(blk_shape)
        kcar_ref[...] = dk2[bq:]
        vcar_ref[...] = dv2[bq:]
        db_ref[...] += ds

    cur = lambda r, n: index(r, jnp.minimum(n, nb - 1))
    prev = lambda r, n: index(r, jnp.maximum(jnp.minimum(n, nb - 1) - 1, 0))
    late = lambda r, n: index(r, jnp.maximum(n - 1, 0))
    view = lambda a: a.reshape(geo["view"] + (a.shape[1],))
    qkv_v = view(qkv)
    tile = _attn_specs(geo, ATTN_OUT, 0, cur)
    bias_spec = pl.BlockSpec(bias4.shape, lambda r, n: (0, 0))
    out_shape = jax.ShapeDtypeStruct(geo["view"] + (ATTN_OUT,), BF16)
    dq, dk, dv, db = pl.pallas_call(
        body, name=name, grid=(nsub, nb + 1),
        in_specs=[_attn_specs(geo, ATTN_OUT, g, cur), _attn_specs(geo, ATTN_OUT, 3 + g, cur),
                  _attn_specs(geo, ATTN_OUT, 3 + g, prev), _attn_specs(geo, ATTN_OUT, 6 + g, cur),
                  _attn_specs(geo, ATTN_OUT, 6 + g, prev), tile, tile, tile, bias_spec],
        out_specs=[tile, _attn_specs(geo, ATTN_OUT, 0, late), _attn_specs(geo, ATTN_OUT, 0, late), bias_spec],
        out_shape=[out_shape, out_shape, out_shape, jax.ShapeDtypeStruct(bias4.shape, F32)],
        scratch_shapes=[pltpu.VMEM((bq, ATTN_OUT), F32), pltpu.VMEM((bq, ATTN_OUT), F32)],
        compiler_params=_cparams("arbitrary", "arbitrary"),
    )(qkv_v, qkv_v, qkv_v, qkv_v, qkv_v, view(do), view(lse), view(cvec), bias4)
    return dq.reshape(rows, ATTN_OUT), dk.reshape(rows, ATTN_OUT), dv.reshape(rows, ATTN_OUT), db


def _group_weights(lses):
    mx = jnp.maximum(jnp.maximum(lses[0], lses[1]), lses[2])
    es = [jnp.exp(l - mx) for l in lses]
    den = es[0] + es[1] + es[2]
    return [e / den for e in es]


def _combine_fwd(name, os_, lses):
    def fn(o0, o1, o2, l0, l1, l2):
        ws = _group_weights([l0, l1, l2])
        out = ws[0] * o0 + ws[1] * o1 + ws[2] * o2
        return out, out

    return _ew(name, fn, [*os_, *lses], [ATTN_OUT, ATTN_OUT], [F32, BF16], tm=1024)


def _combine_bwd(name, do, oa, lses):
    def fn(dov, oav, l0, l1, l2):
        head_sum = (lax.broadcasted_iota(jnp.int32, (ATTN_OUT, ATTN_OUT), 0) // HEAD_DIM
                    == lax.broadcasted_iota(jnp.int32, (ATTN_OUT, ATTN_OUT), 1) // HEAD_DIM)
        ws = _group_weights([l0, l1, l2])
        prod = dov * oav
        hi = prod.astype(BF16)
        lo = (prod - hi.astype(F32)).astype(BF16)
        ones = jnp.where(head_sum, 1.0, 0.0).astype(BF16)
        bar = _dot_nn(hi, ones) + _dot_nn(lo, ones)
        return tuple(w * dov for w in ws) + tuple(-w * bar for w in ws)

    return _ew(name, fn, [do, oa, *lses], [ATTN_OUT] * 6, [BF16] * 3 + [F32] * 3, tm=1024)


def _ssm_disc(a_re, a_im, log_dt, b_re, b_im):
    dt = jnp.exp(log_dt)
    mag = jnp.exp(a_re * dt)
    ab_re = mag * jnp.cos(a_im * dt)
    ab_im = mag * jnp.sin(a_im * dt)
    den = a_re * a_re + a_im * a_im
    xr = ab_re - 1.0
    coef_re = (xr * a_re + ab_im * a_im) / den
    coef_im = (ab_im * a_re - xr * a_im) / den
    bb_re = coef_re[None] * b_re - coef_im[None] * b_im
    bb_im = coef_re[None] * b_im + coef_im[None] * b_re
    return ab_re, ab_im, bb_re, bb_im


def _ssm_params_fwd(name, a_re, a_im, log_dt, b_re, b_im):
    pows = jax.ShapeDtypeStruct((SCAN_STEPS,) + a_re.shape, F32)
    cgn = jax.ShapeDtypeStruct(b_re.shape, F32)

    def body(ar, ai, ld, br, bi, o_pr, o_pi, o_bbr, o_bbi):
        ab_re, ab_im, bb_re, bb_im = _ssm_disc(ar[...], ai[...], ld[...], br[...], bi[...])
        pr, pi = ab_re, ab_im
        for j in range(SCAN_STEPS):
            o_pr[j] = pr
            o_pi[j] = pi
            pr, pi = pr * ab_re - pi * ab_im, pr * ab_im + pi * ab_re
        o_bbr[...] = bb_re
        o_bbi[...] = bb_im

    vm = pl.BlockSpec(memory_space=pltpu.VMEM)
    return pl.pallas_call(body, name=name, in_specs=[vm] * 5, out_specs=[vm] * 4,
                          out_shape=[pows, pows, cgn, cgn])(a_re, a_im, log_dt, b_re, b_im)


def _ssm_params_bwd(name, a_re, a_im, log_dt, b_re, b_im, d_ab_re, d_ab_im, d_bb_re, d_bb_im):
    gn = jax.ShapeDtypeStruct(a_re.shape, F32)
    cgn = jax.ShapeDtypeStruct(b_re.shape, F32)

    def body(ar, ai, ld, br, bi, g0, g1, g2, g3, o_ar, o_ai, o_ld, o_br, o_bi):
        _, vjp = jax.vjp(_ssm_disc, ar[...], ai[...], ld[...], br[...], bi[...])
        outs = vjp((g0[...], g1[...], g2[...], g3[...]))
        for o_ref, o in zip((o_ar, o_ai, o_ld, o_br, o_bi), outs):
            o_ref[...] = o

    vm = pl.BlockSpec(memory_space=pltpu.VMEM)
    return pl.pallas_call(body, name=name, in_specs=[vm] * 9, out_specs=[vm] * 5,
                          out_shape=[gn, gn, jax.ShapeDtypeStruct(log_dt.shape, F32), cgn, cgn],
                          )(a_re, a_im, log_dt, b_re, b_im, d_ab_re, d_ab_im, d_bb_re, d_bb_im)


def _scan_block(s_ref, carry_ref, tmp_ref, pw_ref, reverse, sprev=None):
    nl = SSM_LANES
    halves = range(SCAN_COLS // SCAN_SUB)
    zero = jnp.zeros((SCAN_SUB, SCAN_LANES), F32)
    for half in (reversed(halves) if reverse else halves):
        sub_rows = pl.ds(half * SCAN_SUB, SCAN_SUB)
        for lc in range(nl // SCAN_LANES):
            re_l = pl.ds(lc * SCAN_LANES, SCAN_LANES)
            im_l = pl.ds(nl + lc * SCAN_LANES, SCAN_LANES)
            are, aim = pw_ref[0, :, re_l], pw_ref[0, :, im_l]

            def step_of(j):
                return SCAN_STEPS - 1 - j if reverse else j

            def pass1(j, st):
                sr, si = st
                jj = step_of(j)
                nr = are * sr - aim * si + s_ref[jj, sub_rows, re_l]
                ni = are * si + aim * sr + s_ref[jj, sub_rows, im_l]
                s_ref[jj, sub_rows, re_l] = nr
                s_ref[jj, sub_rows, im_l] = ni
                return nr, ni

            er, ei = lax.fori_loop(0, SCAN_STEPS, pass1, (zero, zero), unroll=2)
            tmp_ref[0:SCAN_SUB, re_l] = er
            tmp_ref[0:SCAN_SUB, im_l] = ei
            apr, api = pw_ref[SCAN_STEPS - 1, 0:1, re_l], pw_ref[SCAN_STEPS - 1, 0:1, im_l]
            sr, si = carry_ref[0:1, re_l], carry_ref[0:1, im_l]
            for step in range(SCAN_SUB):
                c = SCAN_SUB - 1 - step if reverse else step
                tmp_ref[SCAN_SUB + c:SCAN_SUB + c + 1, re_l] = sr
                tmp_ref[SCAN_SUB + c:SCAN_SUB + c + 1, im_l] = si
                e_r, e_i = tmp_ref[c:c + 1, re_l], tmp_ref[c:c + 1, im_l]
                sr, si = apr * sr - api * si + e_r, apr * si + api * sr + e_i
            carry_ref[0:1, re_l] = sr
            carry_ref[0:1, im_l] = si
            cr = tmp_ref[SCAN_SUB:2 * SCAN_SUB, re_l]
            ci = tmp_ref[SCAN_SUB:2 * SCAN_SUB, im_l]

            if sprev is None:
                def pass2(j, st):
                    pr, pi = pw_ref[j, :, re_l], pw_ref[j, :, im_l]
                    jj = step_of(j)
                    s_ref[jj, sub_rows, re_l] += pr * cr - pi * ci
                    s_ref[jj, sub_rows, im_l] += pr * ci + pi * cr
                    return st

                lax.fori_loop(0, SCAN_STEPS, pass2, 0, unroll=2)
            else:
                st_ref, prev_ref, have_prev, dab_ref = sprev

                def corrected(jj, pr, pi):
                    gr = s_ref[jj, sub_rows, re_l] + pr * cr - pi * ci
                    gi = s_ref[jj, sub_rows, im_l] + pr * ci + pi * cr
                    s_ref[jj, sub_rows, re_l] = gr
                    s_ref[jj, sub_rows, im_l] = gi
                    return gr, gi

                def pass2(j, st):
                    dr, di = st
                    jj = SCAN_STEPS - 1 - j
                    gr, gi = corrected(jj, pw_ref[j, :, re_l], pw_ref[j, :, im_l])
                    qr, qi = st_ref[jj - 1, sub_rows, re_l], st_ref[jj - 1, sub_rows, im_l]
                    return dr + gr * qr + gi * qi, di + gi * qr - gr * qi

                dr, di = lax.fori_loop(0, SCAN_STEPS - 1, pass2, (zero, zero), unroll=2)
                gr, gi = corrected(0, pw_ref[SCAN_STEPS - 1, :, re_l], pw_ref[SCAN_STEPS - 1, :, im_l])
                sub = lax.broadcasted_iota(jnp.int32, (SCAN_SUB, SCAN_LANES), 0)
                if half == 0:
                    pv_r = prev_ref[SCAN_SUB - 1:SCAN_SUB, re_l] * have_prev
                    pv_i = prev_ref[SCAN_SUB - 1:SCAN_SUB, im_l] * have_prev
                else:
                    before = pl.ds(half * SCAN_SUB - 1, 1)
                    pv_r, pv_i = st_ref[SCAN_STEPS - 1, before, re_l], st_ref[SCAN_STEPS - 1, before, im_l]
                shape = (SCAN_SUB, SCAN_LANES)
                qr = jnp.where(sub == 0, jnp.broadcast_to(pv_r, shape),
                               pltpu.roll(st_ref[SCAN_STEPS - 1, sub_rows, re_l], 1, 0))
                qi = jnp.where(sub == 0, jnp.broadcast_to(pv_i, shape),
                               pltpu.roll(st_ref[SCAN_STEPS - 1, sub_rows, im_l], 1, 0))
                dab_ref[:, re_l] += dr + gr * qr + gi * qi
                dab_ref[:, im_l] += di + gi * qr - gr * qi


def _scan_view(a):
    return a.reshape(16, a.shape[0] // 16, a.shape[1])


def _pair_tile(p):
    start = (p * 2 * SSM_GROUP // PAIR_TILE) * PAIR_TILE
    return slice(start, start + PAIR_TILE)


def _pair_lanes(p):
    return pl.ds(p * PAIR_LANES, PAIR_LANES), pl.ds(SSM_LANES + p * PAIR_LANES, PAIR_LANES)


def _pair_store(s_ref, p, val):
    re_l, im_l = _pair_lanes(p)
    s_ref[:, :, re_l] = val[:, :PAIR_LANES].reshape(16, SCAN_COLS, PAIR_LANES)
    s_ref[:, :, im_l] = val[:, PAIR_LANES:].reshape(16, SCAN_COLS, PAIR_LANES)


def _pair_load(s_ref, p):
    re_l, im_l = _pair_lanes(p)
    parts = [s_ref[:, :, l].reshape(SCAN_BLOCK, PAIR_LANES) for l in (re_l, im_l)]
    return jnp.concatenate(parts, axis=1).astype(BF16)


def _pair_sum(fn):
    per = PAIR_TILE // (2 * SSM_GROUP)
    tiles = []
    for t in range(SSM_PAIRS // per):
        acc = None
        for p in range(t * per, (t + 1) * per):
            part = fn(p)
            acc = part if acc is None else acc + part
        tiles.append(acc)
    return jnp.concatenate(tiles, axis=1)


def _ssm_fwd(name, u, bb_mats, c_mats, pw_rows, d_skip):
    rows = u.shape[0]
    nl2 = 2 * SSM_LANES
    nblk = rows // SCAN_BLOCK

    def body(u_ref, bb_ref, c_ref, pw_ref, d_ref, y_ref, s_ref, carry_ref, tmp_ref):
        @pl.when(pl.program_id(0) == 0)
        def _():
            carry_ref[...] = jnp.zeros_like(carry_ref)

        uv = u_ref[...].reshape(SCAN_BLOCK, SSM_WIDTH)
        ub = uv.astype(BF16)
        for p in range(SSM_PAIRS):
            _pair_store(s_ref, p, _dot_nn(ub[:, _pair_tile(p)], bb_ref[p]))
        _scan_block(s_ref, carry_ref, tmp_ref, pw_ref, reverse=False)
        ys = _pair_sum(lambda p: _dot_nt(_pair_load(s_ref, p), c_ref[p]))
        y_ref[...] = (ys + d_ref[...] * uv).reshape(16, SCAN_COLS, SSM_WIDTH)

    const = lambda shape: pl.BlockSpec(shape, lambda i: (0,) * len(shape))
    blk = lambda cols: pl.BlockSpec((16, SCAN_COLS, cols), lambda i: (0, i, 0))
    pair_mats = const((SSM_PAIRS, PAIR_TILE, PAIR_TILE))
    y, s = pl.pallas_call(
        body, name=name, grid=(nblk,),
        in_specs=[blk(SSM_WIDTH), pair_mats, pair_mats, const((SCAN_STEPS, SCAN_SUB, nl2)), const((1, SSM_WIDTH))],
        out_specs=[blk(SSM_WIDTH), blk(nl2)],
        out_shape=[jax.ShapeDtypeStruct((16, rows // 16, SSM_WIDTH), F32),
                   jax.ShapeDtypeStruct((16, rows // 16, nl2), F32)],
        scratch_shapes=[pltpu.VMEM((SCAN_SUB, nl2), F32), pltpu.VMEM((2 * SCAN_SUB, nl2), F32)],
        compiler_params=_cparams("arbitrary"),
    )(_scan_view(u), bb_mats, c_mats, pw_rows, d_skip)
    return y.reshape(rows, SSM_WIDTH), s.reshape(rows, nl2)


def _ssm_bwd(name, dy, u, states, bb_mats, c_mats, pwc_rows, d_skip):
    rows = u.shape[0]
    nl2 = 2 * SSM_LANES
    nblk = rows // SCAN_BLOCK

    def body(dy_ref, u_ref, st_ref, prev_ref, bb_ref, c_ref, pw_ref, d_ref,
             du_ref, dbb_ref, dc_ref, dab_ref, dd_ref, g_ref, carry_ref, tmp_ref):
        i = pl.program_id(0)

        @pl.when(i == 0)
        def _():
            carry_ref[...] = jnp.zeros_like(carry_ref)
            for ref in (dbb_ref, dc_ref, dab_ref, dd_ref):
                ref[...] = jnp.zeros_like(ref)

        dyv = dy_ref[...].reshape(SCAN_BLOCK, SSM_WIDTH)
        uv = u_ref[...].reshape(SCAN_BLOCK, SSM_WIDTH)
        dyb, ub = dyv.astype(BF16), uv.astype(BF16)
        for p in range(SSM_PAIRS):
            _pair_store(g_ref, p, _dot_nn(dyb[:, _pair_tile(p)], c_ref[p]))
        have_prev = (i < nblk - 1).astype(F32)
        _scan_block(g_ref, carry_ref, tmp_ref, pw_ref, reverse=True,
                    sprev=(st_ref, prev_ref, have_prev, dab_ref))

        def pair_work(p):
            gp = _pair_load(g_ref, p)
            dbb_ref[p] += _dot_tn(ub[:, _pair_tile(p)], gp)
            dc_ref[p] += _dot_tn(dyb[:, _pair_tile(p)], _pair_load(st_ref, p))
            return _dot_nt(gp, bb_ref[p])

        du_ref[...] = (_pair_sum(pair_work) + d_ref[...] * dyv).reshape(16, SCAN_COLS, SSM_WIDTH)
        dd_ref[...] += jnp.sum(dyv * uv, axis=0, keepdims=True)

    const = lambda shape: pl.BlockSpec(shape, lambda i: (0,) * len(shape))
    blk = lambda cols: pl.BlockSpec((16, SCAN_COLS, cols), lambda i: (0, nblk - 1 - i, 0))
    per8 = SCAN_COLS // SCAN_SUB
    prev_spec = pl.BlockSpec((None, SCAN_SUB, nl2), lambda i: (15, jnp.maximum((nblk - 1 - i) * per8 - 1, 0), 0))
    pair_mats = const((SSM_PAIRS, PAIR_TILE, PAIR_TILE))
    pair_shape = jax.ShapeDtypeStruct((SSM_PAIRS, PAIR_TILE, PAIR_TILE), F32)
    sv = _scan_view(states)
    du, dbb, dc, dab, dd = pl.pallas_call(
        body, name=name, grid=(nblk,),
        in_specs=[blk(SSM_WIDTH), blk(SSM_WIDTH), blk(nl2), prev_spec, pair_mats, pair_mats,
                  const((SCAN_STEPS, SCAN_SUB, nl2)), const((1, SSM_WIDTH))],
        out_specs=[blk(SSM_WIDTH), pair_mats, pair_mats, const((SCAN_SUB, nl2)), const((1, SSM_WIDTH))],
        out_shape=[jax.ShapeDtypeStruct((16, rows // 16, SSM_WIDTH), F32), pair_shape, pair_shape,
                   jax.ShapeDtypeStruct((SCAN_SUB, nl2), F32), jax.ShapeDtypeStruct((1, SSM_WIDTH), F32)],
        scratch_shapes=[pltpu.VMEM((16, SCAN_COLS, nl2), F32), pltpu.VMEM((SCAN_SUB, nl2), F32),
                        pltpu.VMEM((2 * SCAN_SUB, nl2), F32)],
        compiler_params=_cparams("arbitrary"),
    )(_scan_view(dy), _scan_view(u), sv, sv, bb_mats, c_mats, pwc_rows, d_skip)
    return du.reshape(rows, SSM_WIDTH), dbb, dc, dab, dd


def _adamw(name, w, m, v, gparts, tr):
    rows, cols = w.shape

    def body(w_ref, m_ref, v_ref, g_ref, og_ref, od_ref, om_ref, ov_ref):
        g = g_ref[0].astype(F32)
        for i in range(1, N_DEV):
            g = g + g_ref[i].astype(F32)
        m_new = B1 * m_ref[...] + (1.0 - B1) * g
        v_new = B2 * v_ref[...] + (1.0 - B2) * (g * g)
        m_hat = m_new / (1.0 - B1 ** STEP)
        v_hat = v_new / (1.0 - B2 ** STEP)
        og_ref[...] = g
        od_ref[...] = -LR * (m_hat / (jnp.sqrt(v_hat) + ADAM_EPS) + WD * w_ref[...])
        om_ref[...] = m_new
        ov_ref[...] = v_new

    spec = pl.BlockSpec((tr, cols), lambda i: (i, 0))
    shape = jax.ShapeDtypeStruct((rows, cols), F32)
    return pl.pallas_call(
        body, name=name, grid=(rows // tr,),
        in_specs=[spec, spec, spec, pl.BlockSpec((N_DEV, tr, cols), lambda i: (0, i, 0))],
        out_specs=[spec] * 4, out_shape=[shape] * 4,
        compiler_params=_cparams("parallel"),
    )(w, m, v, gparts)


_SHARDED = (
    ("ffn1_w_gate", True, (352, 1024)), ("ffn1_w_up", True, (352, 1024)), ("ffn1_w_down", False, (352, 1024)),
    ("w_in", True, (608, 1024)), ("ssm_w_glu", True, (128, 512)), ("w_attn_branch", True, (128, 256)),
    ("w_ssm_branch", True, (128, 512)), ("w_out", False, (128, 1024)),
    ("ffn2_w_gate", True, (352, 1024)), ("ffn2_w_up", True, (352, 1024)), ("ffn2_w_down", False, (352, 1024)),
)
_SMALL = ("ffn1_norm", "mix_norm", "gate_bias", "rel_bias_table", "ssm_a_re", "ssm_a_im", "ssm_log_dt",
          "ssm_b_re", "ssm_b_im", "ssm_c_re", "ssm_c_im", "ssm_d", "ffn2_norm", "final_norm")
_ORDER = ("ffn1_norm", "ffn1_w_gate", "ffn1_w_up", "ffn1_w_down", "mix_norm", "w_in", "gate_bias",
          "rel_bias_table", "ssm_a_re", "ssm_a_im", "ssm_log_dt", "ssm_b_re", "ssm_b_im", "ssm_c_re",
          "ssm_c_im", "ssm_d", "ssm_w_glu", "w_attn_branch", "w_ssm_branch", "w_out", "ffn2_norm",
          "ffn2_w_gate", "ffn2_w_up", "ffn2_w_down", "final_norm")


def _pack_rows(shape):
    return shape[0] * shape[1] // D_MODEL


_SHARD_INFO = {nm: (tr, shape) for nm, tr, shape in _SHARDED}
_PHASES = {
    "f1gu": ("ffn1_w_gate", "ffn1_w_up"), "f1d": ("ffn1_w_down",),
    "mix": ("w_in", "ssm_w_glu", "w_attn_branch", "w_ssm_branch", "w_out"),
    "f2": ("ffn2_w_gate", "ffn2_w_up", "ffn2_w_down"),
}


def _to_rows(a, nm):
    tr, shape = _SHARD_INFO[nm]
    return (a.T if tr else a).reshape(_pack_rows(shape), D_MODEL)


def _from_rows(p, nm):
    tr, shape = _SHARD_INFO[nm]
    a = p.reshape(shape)
    return a.T if tr else a


def _full_weight(gathered, nm):
    _, shape = _SHARD_INFO[nm]
    return gathered.reshape(N_DEV * shape[0], shape[1])


def _grad_blocks(g, nm):
    _, shape = _SHARD_INFO[nm]
    return g.astype(BF16).reshape(N_DEV, _pack_rows(shape), D_MODEL)


_SMALL_TILE = 8 * 128


def _small_rows(a):
    flat = a.reshape(-1)
    return jnp.pad(flat, (0, (-flat.shape[0]) % _SMALL_TILE)).reshape(-1, 128)


def _pack_small(ws, last=None):
    tail = jnp.zeros((), F32) if last is None else last
    return jnp.concatenate([_small_rows(ws[nm]) for nm in _SMALL] + [_small_rows(tail)], axis=0)


def _unpack_small(pack, like):
    out, r0 = {}, 0
    for nm in _SMALL:
        n = like[nm].size
        nr = 8 * -(-n // _SMALL_TILE)
        out[nm] = pack[r0:r0 + nr].reshape(-1)[:n].reshape(like[nm].shape)
        r0 += nr
    return out


def _residue_order(a):
    rows, cols = a.shape
    return a.reshape(rows // 16, 16, cols).transpose(1, 0, 2).reshape(rows, cols)


def _token_order(a):
    rows, cols = a.shape
    return a.reshape(16, rows // 16, cols).transpose(1, 0, 2).reshape(rows, cols)


_PAIRS_PER_TILE = PAIR_TILE // (2 * SSM_GROUP)
_PAIR_AXES = (SSM_PAIRS // _PAIRS_PER_TILE, _PAIRS_PER_TILE, 2)


def _pair_matrices(re, im):
    six = jnp.stack([re, im]).reshape((2,) + _PAIR_AXES + (SSM_GROUP, SSM_STATE))
    eye_j, eye_l = jnp.eye(_PAIRS_PER_TILE, dtype=re.dtype), jnp.eye(2, dtype=re.dtype)
    mats = jnp.einsum("xkjlcn,jJ,lL->kjJLcxln", six, eye_j, eye_l)
    return mats.reshape(SSM_PAIRS, PAIR_TILE, PAIR_TILE).astype(BF16)


def _pair_diagonals(acc):
    k, j, l = _PAIR_AXES
    eight = acc.reshape(k, j, j, l, SSM_GROUP, 2, l, SSM_STATE)
    eye_j, eye_l = jnp.eye(j, dtype=acc.dtype), jnp.eye(l, dtype=acc.dtype)
    own = jnp.einsum("kjJLcxln,jJ,lL->xkjlcn", eight, eye_j, eye_l).reshape(2, SSM_GROUPS, SSM_GROUP, SSM_STATE)
    return own[0], own[1]


def _local_step(xs, target, small, weights_of, send_grads, first_deps=()):
    rows = xs.shape[0]
    gfull, gsmall = {}, {}
    wf = dict(weights_of("f1", None))

    x1, h1, gg1, uu1 = _ffn_fwd("ffn1_fwd", xs, small["ffn1_norm"], wf["ffn1_w_gate"], wf["ffn1_w_up"],
                                wf["ffn1_w_down"], deps=first_deps)
    wf.update(weights_of("mix", x1))
    hmix = _rms_fwd("mix_norm_fwd", x1, small["mix_norm"])
    w_in = wf["w_in"]
    w_qkv, w_u, w_g = w_in[:3 * ATTN_WIDTH], w_in[3 * ATTN_WIDTH:3 * ATTN_WIDTH + SSM_WIDTH], w_in[3 * ATTN_WIDTH + SSM_WIDTH:]
    qscale = jnp.concatenate([jnp.full((1, ATTN_WIDTH), HEAD_DIM ** -0.5, F32), jnp.ones((1, 2 * ATTN_WIDTH), F32)], axis=1)
    qkv, = _mm("in_qkv", [(hmix, w_qkv)], True, 3 * ATTN_WIDTH, [BF16],
               epilogue=lambda acc, sc: (acc * sc,), extras=[(qscale, 0)], tn=ATTN_WIDTH)
    u, = _mm("in_u", [(hmix, w_u)], True, SSM_WIDTH, [F32])
    gates, = _mm("in_gates", [(hmix, w_g)], True, 2 * D_MODEL, [F32],
                 epilogue=lambda acc, b: (_sigmoid(acc + b),), extras=[(small["gate_bias"], 0)])

    table_t = small["rel_bias_table"].T
    tables, bias4, o_g, lse_g = [], [], [], []
    for g in range(N_GROUPS):
        bucket, valid = [jnp.asarray(t) for t in _attn_tables(g, rows)]
        bias_g = _bias_fwd(f"rel_bias_fwd_{g}", bucket, valid, table_t[g * HEADS_PER_GROUP:(g + 1) * HEADS_PER_GROUP])
        tables.append(bucket)
        bias4.append(bias_g.reshape(-1, bias_g.shape[-1]))
        o, lse = _attn_fwd(f"attn_fwd_{g}", qkv, g, bias4[g])
        o_g.append(o)
        lse_g.append(lse)
    oa_f32, oa = _combine_fwd("attn_combine_fwd", o_g, lse_g)
    y_attn, = _mm("attn_branch", [(oa, wf["w_attn_branch"])], True, D_MODEL, [F32])

    pw_re, pw_im, bb_re, bb_im = _ssm_params_fwd(
        "ssm_params_fwd", small["ssm_a_re"], small["ssm_a_im"], small["ssm_log_dt"].reshape(SSM_GROUPS, 1),
        small["ssm_b_re"].transpose(2, 0, 1), small["ssm_b_im"].transpose(2, 0, 1))

    def power_rows(sign):
        row = jnp.concatenate([pw_re.reshape(SCAN_STEPS, 1, SSM_LANES), sign * pw_im.reshape(SCAN_STEPS, 1, SSM_LANES)],
                              axis=2)
        return jnp.broadcast_to(row, (SCAN_STEPS, SCAN_SUB, 2 * SSM_LANES))

    bb_mats = _pair_matrices(bb_re.transpose(1, 0, 2), bb_im.transpose(1, 0, 2))
    c_mats = _pair_matrices(small["ssm_c_re"], -small["ssm_c_im"])
    d_skip = small["ssm_d"].reshape(1, SSM_WIDTH)
    y_raw, states = _ssm_fwd("ssm_fwd", u, bb_mats, c_mats, power_rows(1.0), d_skip)

    def gelu_fn(yv):
        return (jax.nn.gelu(yv),)

    ygelu, = _ew("ssm_gelu", gelu_fn, [y_raw], [SSM_WIDTH], [BF16])
    glu, = _mm("ssm_glu", [(ygelu, wf["ssm_w_glu"])], True, 2 * SSM_WIDTH, [F32])
    ysg, = _ew("ssm_glu_act", lambda gv: (gv[:, :SSM_WIDTH] * _sigmoid(gv[:, SSM_WIDTH:]),), [glu], [SSM_WIDTH], [BF16])
    y_ssm, merged = _mm("ssm_branch_merge", [(ysg, wf["w_ssm_branch"])], True, D_MODEL, [F32, BF16],
                        epilogue=lambda acc, ga, gs, ya: (acc, ga * ya + gs * acc),
                        extras=[(gates, 0), (gates, D_MODEL), (y_attn, 0)])
    x2, = _mm("mix_out", [(merged, wf["w_out"])], False, D_MODEL, [F32],
              epilogue=lambda acc, res: (res + acc,), extras=[(x1, 0)])
    wf.update(weights_of("f2", x2))
    x3, h2, gg2, uu2 = _ffn_fwd("ffn2_fwd", x2, small["ffn2_norm"], wf["ffn2_w_gate"], wf["ffn2_w_up"],
                                wf["ffn2_w_down"])
    dx3, gsmall["final_norm"], loss = _final_loss("final_loss", x3, small["final_norm"].reshape(1, D_MODEL), target)
    gsmall["loss"] = loss

    dx2, dgg2, duu2, act2, gsmall["ffn2_norm"] = _ffn_bwd(
        "ffn2_bwd", dx3, x2, small["ffn2_norm"], gg2, uu2, wf["ffn2_w_gate"], wf["ffn2_w_up"], wf["ffn2_w_down"])
    gfull["ffn2_w_gate"] = _mm_tn("ffn2_dwg", dgg2, h2, out_dtype=BF16)
    gfull["ffn2_w_up"] = _mm_tn("ffn2_dwu", duu2, h2, out_dtype=BF16)
    gfull["ffn2_w_down"] = _mm_tn("ffn2_dwd", act2, dx3, scale=0.5, out_dtype=BF16)
    sent = send_grads("f2", gfull)

    def merge_bwd(dm, ga, gs, ya, ys):
        return (dm * ga, dm * gs, dm * ya * ga * (1.0 - ga), dm * ys * gs * (1.0 - gs))

    dya, dys, dzga, dzgs = _mm("mix_out_bwd", [(dx2, wf["w_out"])], True, D_MODEL, [BF16] * 4, epilogue=merge_bwd,
                               extras=[(gates, 0), (gates, D_MODEL), (y_attn, 0), (y_ssm, 0)], deps=sent)
    gfull["w_out"] = _mm_tn("dw_out", merged, dx2, out_dtype=BF16)
    gsmall["gate_bias"] = jnp.concatenate([_colsum("dgate_bias_a", dzga), _colsum("dgate_bias_s", dzgs)], axis=1)

    gfull["w_ssm_branch"] = _mm_tn("dw_ssm_branch", dys, ysg, out_dtype=BF16)

    def glu_bwd(dysg, av, bv):
        sb = _sigmoid(bv)
        return (dysg * sb, dysg * av * sb * (1.0 - sb))

    dglu_a, dglu_b = _mm("ssm_branch_bwd", [(dys, wf["w_ssm_branch"])], False, SSM_WIDTH, [BF16, BF16],
                         epilogue=glu_bwd, extras=[(glu, 0), (glu, SSM_WIDTH)])
    w_glu = wf["ssm_w_glu"]
    gfull["ssm_w_glu"] = _mm_tn_stack("dw_glu", [dglu_a, dglu_b], ygelu, out_dtype=BF16)

    def gelu_bwd(acc, yv):
        _, vjp = jax.vjp(jax.nn.gelu, yv)
        return (vjp(acc)[0],)

    dy_raw, = _mm("ssm_glu_bwd", [(dglu_a, w_glu[:SSM_WIDTH]), (dglu_b, w_glu[SSM_WIDTH:])], False, SSM_WIDTH, [F32],
                  epilogue=gelu_bwd, extras=[(y_raw, 0)])
    du, dbb_acc, dc_acc, dab_rows, gsmall_d = _ssm_bwd(
        "ssm_bwd", dy_raw, u, states, bb_mats, c_mats, power_rows(-1.0), d_skip)
    gsmall["ssm_d"] = gsmall_d
    dbb_re, dbb_im = [a.transpose(1, 0, 2) for a in _pair_diagonals(dbb_acc)]
    dc_re, dc_im = _pair_diagonals(dc_acc)
    gsmall["ssm_c_re"], gsmall["ssm_c_im"] = dc_re, -dc_im
    dab = _colsum("ssm_dab", dab_rows)
    d_ar, d_ai, d_ld, d_br, d_bi = _ssm_params_bwd(
        "ssm_params_bwd", small["ssm_a_re"], small["ssm_a_im"], small["ssm_log_dt"].reshape(SSM_GROUPS, 1),
        small["ssm_b_re"].transpose(2, 0, 1), small["ssm_b_im"].transpose(2, 0, 1),
        dab[:, :SSM_LANES].reshape(SSM_GROUPS, SSM_STATE), dab[:, SSM_LANES:].reshape(SSM_GROUPS, SSM_STATE),
        dbb_re, dbb_im)
    gsmall["ssm_a_re"], gsmall["ssm_a_im"], gsmall["ssm_log_dt"] = d_ar, d_ai, d_ld.reshape(SSM_GROUPS)
    gsmall["ssm_b_re"], gsmall["ssm_b_im"] = d_br.transpose(1, 2, 0), d_bi.transpose(1, 2, 0)

    gfull["w_attn_branch"] = _mm_tn("dw_attn_branch", dya, oa, out_dtype=BF16)
    doa, = _mm("attn_branch_bwd", [(dya, wf["w_attn_branch"])], False, ATTN_OUT, [F32])
    dc = _combine_bwd("attn_combine_bwd", doa, oa_f32, lse_g)
    dqkv_cols = [None] * 9
    dtable = []
    for g in range(N_GROUPS):
        dq, dk, dv, db = _attn_bwd(f"attn_bwd_{g}", qkv, dc[g], lse_g[g], dc[3 + g], g, bias4[g])
        dqkv_cols[g], dqkv_cols[3 + g], dqkv_cols[6 + g] = dq, dk, dv
        dt = _bias_bwd(f"rel_bias_bwd_{g}", tables[g], db.reshape(HEADS_PER_GROUP, -1, db.shape[-1]))
        dtable.append(dt[:, :HEADS_PER_GROUP])
    gsmall["rel_bias_table"] = jnp.concatenate(dtable, axis=1)

    gfull["w_in"] = jnp.concatenate([_mm_tn_stack("dw_in_qkv", dqkv_cols, hmix, out_dtype=BF16),
                                     _mm_tn_stack("dw_in_rest", [du, dzga, dzgs], hmix, out_dtype=BF16)], axis=0)
    sent = send_grads("mix", gfull)
    qkv_pairs = [(c, w_qkv[i * ATTN_OUT:(i + 1) * ATTN_OUT]) for i, c in enumerate(dqkv_cols)]
    dhmix, = _mm("in_bwd", qkv_pairs + [(du, w_u), (dzga, w_g[:D_MODEL]), (dzgs, w_g[D_MODEL:])], False, D_MODEL,
                 [F32], tm=512, deps=sent)
    dx1, gsmall["mix_norm"] = _rms_bwd("mix_norm_bwd", dhmix, x1, small["mix_norm"], dx2)

    dx, dgg1, duu1, act1, gsmall["ffn1_norm"] = _ffn_bwd(
        "ffn1_bwd", dx1, xs, small["ffn1_norm"], gg1, uu1, wf["ffn1_w_gate"], wf["ffn1_w_up"], wf["ffn1_w_down"])
    sent = send_grads("small", gsmall)
    gfull["ffn1_w_down"] = _mm_tn("ffn1_dwd", act1, dx1, scale=0.5, deps=sent, out_dtype=BF16)
    sent = send_grads("f1d", gfull)
    gfull["ffn1_w_gate"] = _mm_tn("ffn1_dwg", dgg1, h1, deps=sent, out_dtype=BF16)
    gfull["ffn1_w_up"] = _mm_tn("ffn1_dwu", duu1, h1, out_dtype=BF16)
    send_grads("f1gu", gfull)
    return dx, gsmall


def kernel(x, ffn1_norm, ffn1_w_gate, ffn1_w_up, ffn1_w_down, mix_norm, w_in, gate_bias, rel_bias_table, ssm_a_re, ssm_a_im, ssm_log_dt, ssm_b_re, ssm_b_im, ssm_c_re, ssm_c_im, ssm_d, ssm_w_glu, w_attn_branch, w_ssm_branch, w_out, ffn2_norm, ffn2_w_gate, ffn2_w_up, ffn2_w_down, final_norm, loss_target, m_ffn1_norm, m_ffn1_w_gate, m_ffn1_w_up, m_ffn1_w_down, m_mix_norm, m_w_in, m_gate_bias, m_rel_bias_table, m_ssm_a_re, m_ssm_a_im, m_ssm_log_dt, m_ssm_b_re, m_ssm_b_im, m_ssm_c_re, m_ssm_c_im, m_ssm_d, m_ssm_w_glu, m_w_attn_branch, m_w_ssm_branch, m_w_out, m_ffn2_norm, m_ffn2_w_gate, m_ffn2_w_up, m_ffn2_w_down, m_final_norm, v_ffn1_norm, v_ffn1_w_gate, v_ffn1_w_up, v_ffn1_w_down, v_mix_norm, v_w_in, v_gate_bias, v_rel_bias_table, v_ssm_a_re, v_ssm_a_im, v_ssm_log_dt, v_ssm_b_re, v_ssm_b_im, v_ssm_c_re, v_ssm_c_im, v_ssm_d, v_ssm_w_glu, v_w_attn_branch, v_w_ssm_branch, v_w_out, v_ffn2_norm, v_ffn2_w_gate, v_ffn2_w_up, v_ffn2_w_down, v_final_norm):
    given = dict(locals())
    shapes = {nm: given[nm].shape for nm in _ORDER}

    def strip(a):
        return a[0] if a.ndim >= 2 and a.shape[0] == 1 else a

    w = {nm: strip(given[nm]) for nm in _ORDER}
    m = {nm: strip(given["m_" + nm]) for nm in _ORDER}
    v = {nm: strip(given["v_" + nm]) for nm in _ORDER}
    for d in (w, m, v):
        d["rel_bias_table"] = d["rel_bias_table"].reshape(N_BUCKETS, N_GROUPS * HEADS_PER_GROUP)

    small = {nm: w[nm] for nm in _SMALL}
    small_in = dict(small)
    for nm in ("ffn1_norm", "mix_norm", "ffn2_norm", "gate_bias"):
        small_in[nm] = small[nm].reshape(1, -1)
    w_rows = {nm: _to_rows(w[nm], nm) for nm in _SHARD_INFO}

    def bf16_rows(phase):
        return [w_rows[nm].astype(BF16) for nm in _PHASES[phase]]

    f1_names = _PHASES["f1gu"] + _PHASES["f1d"]
    got_f1 = _all_gather("gather_f1", bf16_rows("f1gu") + bf16_rows("f1d"))
    pending_w = {"mix": _exchange_start("gather_mix_start", bf16_rows("mix"), gather=True, deps=[got_f1[0]])}
    pending_w["f2"] = _exchange_start("gather_f2_start", bf16_rows("f2"), gather=True, deps=[pending_w["mix"][4]])

    def weights_of(phase, after):
        if phase == "f1":
            return {nm: _full_weight(got, nm) for nm, got in zip(f1_names, got_f1)}
        landed = _exchange_wait(f"gather_{phase}_wait", pending_w[phase], after, gather=True)
        return {nm: _full_weight(got, nm) for nm, got in zip(_PHASES[phase], landed)}

    pending_g = {}

    def send_grads(phase, grads):
        if phase == "small":
            gs_pack = _pack_small({nm: grads[nm].reshape(small[nm].shape) for nm in _SMALL}, last=grads["loss"])
            pending_g[phase] = _exchange_start("gather_small_start", [gs_pack], gather=True)
        else:
            pending_g[phase] = _exchange_start(f"scatter_{phase}_start",
                                               [_grad_blocks(grads[nm], nm) for nm in _PHASES[phase]], gather=False)
        return [pending_g[phase][4]]

    dx, gsmall = _local_step(_residue_order(x[0]), _residue_order(loss_target[0]), small_in, weights_of,
                             send_grads, first_deps=[pending_w["f2"][4]])
    dx = _token_order(dx)

    updated = {}
    after = pending_g["f1gu"][4]
    for phase in ("f2", "mix", "f1d", "small", "f1gu"):
        landed = _exchange_wait(f"exchange_{phase}_wait", pending_g[phase], after, gather=phase == "small")
        if phase == "small":
            sm = _adamw("adamw_small", _pack_small(small), _pack_small({nm: m[nm] for nm in _SMALL}),
                        _pack_small({nm: v[nm] for nm in _SMALL}), landed[0], landed[0].shape[1])
            after = sm[0]
            continue
        for nm, recv in zip(_PHASES[phase], landed):
            tr = max(t for t in range(16, 353, 16) if w_rows[nm].shape[0] % t == 0)
            updated[nm] = _adamw(f"adamw_{nm}", w_rows[nm], _to_rows(m[nm], nm), _to_rows(v[nm], nm), recv, tr)
            after = updated[nm][0]

    loss = sm[0][-8, 0]
    outs = []
    for i in range(4):
        sml = _unpack_small(sm[i], small)
        outs.append([(_from_rows(updated[nm][i], nm) if nm in updated else sml[nm]).reshape(shapes[nm])
                     for nm in _ORDER])
    return (loss, dx[None], *outs[0], *outs[1], *outs[2], *outs[3])
```

```python
import math

import numpy as np
import jax
import jax.numpy as jnp
from jax import lax
from jax.experimental import pallas as pl
from jax.experimental.pallas import tpu as pltpu

F32 = jnp.float32
BF16 = jnp.bfloat16

N_DEV = 8
D_MODEL = 1024
D_FF = 2816
HEAD_DIM = 64
HEADS_PER_GROUP = 4
DILATIONS = (1, 4, 16)
N_GROUPS = 3
ATTN_WIDTH = 768
ATTN_OUT = 256
BLOCK = 128
N_BUCKETS = 32
MAX_DISTANCE = 2048
NEG_INF = -1e30
SSM_WIDTH = 512
SSM_GROUPS = 32
SSM_GROUP = 16
SSM_STATE = 64
SSM_LANES = SSM_GROUPS * SSM_STATE
SSM_PAIRS = SSM_GROUPS // 2
PAIR_LANES = 2 * SSM_STATE
PAIR_TILE = 256
EPS = 1e-6
LR, B1, B2, ADAM_EPS, WD, STEP = 0.001, 0.9, 0.999, 1e-08, 0.01, 10

VMEM_LIMIT_BYTES = 56 * 1024 * 1024
FFN_CHUNK = 768
SCAN_BLOCK = 256
SCAN_STEPS = 16
SCAN_COLS = SCAN_BLOCK // SCAN_STEPS
SCAN_SUB = 8
SCAN_LANES = 512

MESH = pl.DeviceIdType.MESH


def _cparams(*sem):
    return pltpu.CompilerParams(dimension_semantics=sem, vmem_limit_bytes=VMEM_LIMIT_BYTES)


def _dot(a, b, dims):
    return lax.dot_general(a, b, (dims, ((), ())), preferred_element_type=F32)


def _dot_nn(a, b):
    return _dot(a, b, ((1,), (0,)))


def _dot_nt(a, b):
    return _dot(a, b, ((1,), (1,)))


def _dot_tn(a, b):
    return _dot(a, b, ((0,), (0,)))


def _sigmoid(x):
    return 1.0 / (1.0 + jnp.exp(-x))


def _all_gather(name, xs_list):
    n = len(xs_list)

    def body(*refs):
        x_refs, out_refs = refs[:n], refs[n:2 * n]
        send_sems, recv_sems, local_sems = refs[2 * n:]
        x, y, c = lax.axis_index("x"), lax.axis_index("y"), lax.axis_index("c")
        me, sibling = (x, y, c), (x, y, 1 - c)
        chips = [(1 - x, y), (x, 1 - y), (1 - x, 1 - y)]

        def copy(a, k, block, to, own=False):
            px, py, pc = block
            rows = out_refs[a].at[4 * px + 2 * py + pc]
            return pltpu.make_async_remote_copy(
                src_ref=x_refs[a] if own else rows, dst_ref=rows,
                send_sem=send_sems.at[7 * a + k], recv_sem=recv_sems.at[7 * a + k], device_id=to,
                device_id_type=MESH)

        mine = [pltpu.make_async_copy(x_refs[a], out_refs[a].at[4 * x + 2 * y + c], local_sems.at[a]) for a in range(n)]
        first = []
        for a in range(n):
            mine[a].start()
            first.append(copy(a, 0, me, sibling, own=True))
            first += [copy(a, 1 + j, me, (*chip, c), own=True) for j, chip in enumerate(chips)]
        for cp in first:
            cp.start()
        passed = []
        for a in range(n):
            for j, chip in enumerate(chips):
                copy(a, 1 + j, (*chip, c), me).wait_recv()
                passed.append(copy(a, 4 + j, (*chip, c), sibling))
                passed[-1].start()
        for a in range(n):
            copy(a, 0, sibling, me).wait_recv()
            for j, chip in enumerate(chips):
                copy(a, 4 + j, (*chip, 1 - c), me).wait_recv()
        for cp in first + passed:
            cp.wait_send()
        for cp in mine:
            cp.wait()

    return pl.pallas_call(
        body, name=name,
        out_shape=[jax.ShapeDtypeStruct((N_DEV, *xs.shape), xs.dtype) for xs in xs_list],
        in_specs=[_ANY_SPEC] * n, out_specs=[_ANY_SPEC] * n,
        scratch_shapes=[pltpu.SemaphoreType.DMA((7 * n,)), pltpu.SemaphoreType.DMA((7 * n,)),
                        pltpu.SemaphoreType.DMA((n,))],
    )(*xs_list)


_HBM_SPEC = pl.BlockSpec(memory_space=pltpu.HBM)
_SEM_SPEC = pl.BlockSpec(memory_space=pltpu.SEMAPHORE)
_ANY_SPEC = pl.BlockSpec(memory_space=pl.ANY)
_EFFECT = pltpu.SideEffectType.DATAFLOW_SIDE_EFFECTING


def _peers(x, y, c):
    return [(1 - x if k & 4 else x, 1 - y if k & 2 else y, 1 - c if k & 1 else c) for k in range(1, N_DEV)]


def _exchange_copies(x_refs, land_refs, send_sems, recv_sems, gather):
    x, y, c = lax.axis_index("x"), lax.axis_index("y"), lax.axis_index("c")
    me = 4 * x + 2 * y + c
    copies = []
    for a, (x_ref, land_ref) in enumerate(zip(x_refs, land_refs)):
        for k, (px, py, pc) in enumerate(_peers(x, y, c)):
            src = x_ref if gather else x_ref.at[4 * px + 2 * py + pc]
            copies.append(pltpu.make_async_remote_copy(
                src_ref=src, dst_ref=land_ref.at[me], send_sem=send_sems.at[N_DEV * a + k],
                recv_sem=recv_sems.at[(N_DEV - 1) * a + k], device_id=(px, py, pc), device_id_type=MESH))
    owns = [pltpu.make_async_copy(x_ref if gather else x_ref.at[me], land_ref.at[me],
                                  send_sems.at[N_DEV * a + N_DEV - 1])
            for a, (x_ref, land_ref) in enumerate(zip(x_refs, land_refs))]
    return owns, copies


def _exchange_start(name, xs_list, gather, deps=()):
    n, nd = len(xs_list), len(deps)
    land_shapes = [(N_DEV, *xs.shape) if gather else xs.shape for xs in xs_list]

    def body(*refs):
        x_refs, land_refs = refs[:n], refs[n:2 * n]
        send_sems, recv_sems = refs[2 * n + nd:2 * n + nd + 2]
        token = refs[-1]
        owns, copies = _exchange_copies(x_refs, land_refs, send_sems, recv_sems, gather)
        for cp in copies + owns:
            cp.start()
        token[...] = jnp.zeros_like(token)

    hbm = lambda a: pltpu.with_memory_space_constraint(a, pltpu.HBM)
    outs = pl.pallas_call(
        body, name=name,
        out_shape=(pltpu.SemaphoreType.DMA((n * N_DEV,)), pltpu.SemaphoreType.DMA((n * (N_DEV - 1),)),
                   *[pltpu.HBM(xs.shape, xs.dtype) for xs in xs_list],
                   *[pltpu.HBM(shape, xs.dtype) for shape, xs in zip(land_shapes, xs_list)],
                   jax.ShapeDtypeStruct((8, 128), F32)),
        in_specs=(_HBM_SPEC,) * (2 * n) + (_ANY_SPEC,) * nd,
        out_specs=(_SEM_SPEC, _SEM_SPEC) + (_HBM_SPEC,) * (2 * n) + (pl.BlockSpec(memory_space=pltpu.VMEM),),
        input_output_aliases={i: 2 + i for i in range(2 * n)},
        compiler_params=pltpu.CompilerParams(has_side_effects=_EFFECT),
    )(*[hbm(xs) for xs in xs_list], *[hbm(lax.empty(shape, xs.dtype)) for shape, xs in zip(land_shapes, xs_list)],
      *deps)
    return outs[0], outs[1], list(outs[2:2 + n]), list(outs[2 + n:2 + 2 * n]), outs[-1]


def _exchange_wait(name, handle, after, gather):
    send_sems, recv_sems, xs_thru, lands_thru, _ = handle
    n = len(xs_thru)

    def body(*refs):
        x_refs, land_refs = refs[:n], refs[n:2 * n]
        send_sems, recv_sems = refs[2 * n:2 * n + 2]
        owns, copies = _exchange_copies(x_refs, land_refs, send_sems, recv_sems, gather)
        for cp in copies:
            cp.wait_send()
            cp.wait_recv()
        for cp in owns:
            cp.wait()

    outs = pl.pallas_call(
        body, name=name,
        out_shape=tuple(pltpu.HBM(a.shape, a.dtype) for a in xs_thru + lands_thru),
        in_specs=(_HBM_SPEC,) * (2 * n) + (_SEM_SPEC, _SEM_SPEC, _ANY_SPEC),
        out_specs=(_HBM_SPEC,) * (2 * n), input_output_aliases={i: i for i in range(2 * n)},
        compiler_params=pltpu.CompilerParams(has_side_effects=_EFFECT),
    )(*xs_thru, *lands_thru, send_sems, recv_sems, after)
    return list(outs[n:])


def _mm(name, pairs, nt, n_cols, out_dtypes, epilogue=None, extras=(), tm=1024, tn=512, deps=(), row_sums=0):
    rows = pairs[0][0].shape[0]
    tm = min(tm, rows)
    tn = min(tn, n_cols)
    na, ne, nd, no = len(pairs), len(extras), len(deps), len(out_dtypes)

    def body(*refs):
        a_refs, w_refs = refs[:na], refs[na:2 * na]
        e_refs, o_refs = refs[2 * na:2 * na + ne], refs[2 * na + ne + nd:]
        acc = None
        for a_ref, w_ref in zip(a_refs, w_refs):
            a = a_ref[...].astype(BF16)
            w = w_ref[...].astype(BF16)
            p = _dot_nt(a, w) if nt else _dot_nn(a, w)
            acc = p if acc is None else acc + p
        outs = (acc,) if epilogue is None else epilogue(acc, *[e[...] for e in e_refs])
        for o_ref, o in zip(o_refs[:no], outs[:no]):
            o_ref[...] = o.astype(o_ref.dtype)
        for r_ref, o in zip(o_refs[no:], outs[no:]):
            @pl.when(pl.program_id(0) == 0)
            def _():
                r_ref[...] = jnp.zeros_like(r_ref)

            r_ref[...] += o

    in_specs = [pl.BlockSpec((tm, a.shape[1]), lambda i, j: (i, 0)) for a, _ in pairs]
    for _, w in pairs:
        if nt:
            in_specs.append(pl.BlockSpec((tn, w.shape[1]), lambda i, j: (j, 0)))
        else:
            in_specs.append(pl.BlockSpec((w.shape[0], tn), lambda i, j: (0, j)))
    for e, col_off in extras:
        off = col_off // tn
        if e.shape[0] == 1:
            in_specs.append(pl.BlockSpec((1, tn), lambda i, j, off=off: (0, j + off)))
        else:
            in_specs.append(pl.BlockSpec((tm, tn), lambda i, j, off=off: (i, j + off)))
    in_specs += [_ANY_SPEC] * nd
    out_specs = [pl.BlockSpec((tm, tn), lambda i, j: (i, j)) for _ in out_dtypes]
    out_specs += [pl.BlockSpec((1, tn), lambda i, j: (0, j))] * row_sums
    out_shape = [jax.ShapeDtypeStruct((rows, n_cols), dt) for dt in out_dtypes]
    out_shape += [jax.ShapeDtypeStruct((1, n_cols), F32)] * row_sums
    outs = pl.pallas_call(
        body, name=name, grid=(rows // tm, n_cols // tn),
        in_specs=in_specs, out_specs=out_specs, out_shape=out_shape,
        compiler_params=_cparams("arbitrary" if row_sums else "parallel", "arbitrary"),
    )(*[a for a, _ in pairs], *[w for _, w in pairs], *[e for e, _ in extras], *deps)
    return outs


def _tn_rows(m):
    return max(b for b in range(128, min(m, 1408) + 1, 128) if m % b == 0)


def _mm_tn(name, a, b, scale=1.0, bm=None, tk=1024, deps=(), out_dtype=F32):
    rows, m = a.shape
    n = b.shape[1]
    bm = _tn_rows(m) if bm is None else bm
    tk = min(tk, rows)
    nk = rows // tk

    def body(a_ref, b_ref, *rest):
        o_ref, acc_ref = rest[-2:]
        k = pl.program_id(1)

        @pl.when(k == 0)
        def _():
            acc_ref[...] = jnp.zeros_like(acc_ref)

        acc_ref[...] += _dot_tn(a_ref[...].astype(BF16), b_ref[...].astype(BF16))

        @pl.when(k == nk - 1)
        def _():
            o_ref[...] = (acc_ref[...] * scale).astype(o_ref.dtype)

    return pl.pallas_call(
        body, name=name, grid=(m // bm, nk),
        in_specs=[pl.BlockSpec((tk, bm), lambda i, k: (k, i)), pl.BlockSpec((tk, n), lambda i, k: (k, 0))]
        + [_ANY_SPEC] * len(deps),
        out_specs=pl.BlockSpec((bm, n), lambda i, k: (i, 0)),
        out_shape=jax.ShapeDtypeStruct((m, n), out_dtype),
        scratch_shapes=[pltpu.VMEM((bm, n), F32)],
        compiler_params=_cparams("parallel", "arbitrary"),
    )(a, b, *deps)


def _mm_tn_stack(name, a_list, b, tk=1024, out_dtype=F32):
    rows, n = b.shape
    ms = [a.shape[1] for a in a_list]
    tk = min(tk, rows)
    nk = rows // tk
    na = len(a_list)

    def body(*refs):
        a_refs, b_ref, o_ref, acc_ref = refs[:na], refs[na], refs[na + 1], refs[na + 2]
        k = pl.program_id(0)

        @pl.when(k == 0)
        def _():
            acc_ref[...] = jnp.zeros_like(acc_ref)

        bv = b_ref[...].astype(BF16)
        r0 = 0
        for a_ref, m in zip(a_refs, ms):
            acc_ref[r0:r0 + m, :] += _dot_tn(a_ref[...].astype(BF16), bv)
            r0 += m

        @pl.when(k == nk - 1)
        def _():
            o_ref[...] = acc_ref[...].astype(o_ref.dtype)

    return pl.pallas_call(
        body, name=name, grid=(nk,),
        in_specs=[pl.BlockSpec((tk, m), lambda k: (k, 0)) for m in ms] + [pl.BlockSpec((tk, n), lambda k: (k, 0))],
        out_specs=pl.BlockSpec((sum(ms), n), lambda k: (0, 0)),
        out_shape=jax.ShapeDtypeStruct((sum(ms), n), out_dtype),
        scratch_shapes=[pltpu.VMEM((sum(ms), n), F32)],
        compiler_params=_cparams("arbitrary"),
    )(*a_list, b)


def _colsum(name, xs, tm=512):
    rows, cols = xs.shape
    tm = min(tm, rows)

    def body(x_ref, o_ref):
        @pl.when(pl.program_id(0) == 0)
        def _():
            o_ref[...] = jnp.zeros_like(o_ref)

        o_ref[...] += jnp.sum(x_ref[...].astype(F32), axis=0, keepdims=True)

    return pl.pallas_call(
        body, name=name, grid=(rows // tm,),
        in_specs=[pl.BlockSpec((tm, cols), lambda i: (i, 0))],
        out_specs=pl.BlockSpec((1, cols), lambda i: (0, 0)),
        out_shape=jax.ShapeDtypeStruct((1, cols), F32),
        compiler_params=_cparams("arbitrary"),
    )(xs)


def _ew(name, fn, ins, out_cols, out_dtypes, tm=512):
    rows = ins[0].shape[0]
    tm = min(tm, rows)
    ni = len(ins)

    def body(*refs):
        outs = fn(*[r[...] for r in refs[:ni]])
        for o_ref, o in zip(refs[ni:], outs):
            o_ref[...] = o.astype(o_ref.dtype)

    def spec(shape):
        if shape[0] == 1:
            return pl.BlockSpec((1, shape[1]), lambda i: (0, 0))
        return pl.BlockSpec((tm, shape[1]), lambda i: (i, 0))

    return pl.pallas_call(
        body, name=name, grid=(rows // tm,),
        in_specs=[spec(a.shape) for a in ins],
        out_specs=[pl.BlockSpec((tm, c), lambda i: (i, 0)) for c in out_cols],
        out_shape=[jax.ShapeDtypeStruct((rows, c), dt) for c, dt in zip(out_cols, out_dtypes)],
        compiler_params=_cparams("parallel"),
    )(*ins)


def _rms_parts(xv):
    r = lax.rsqrt(jnp.mean(xv * xv, axis=-1, keepdims=True) + EPS)
    return r, xv * r


def _rms_bwd_dx(dh, gain, r, xh):
    dxh = dh * gain
    return r * (dxh - xh * jnp.mean(dxh * xh, axis=-1, keepdims=True))


def _rms_fwd(name, xs, gain):
    def fn(xv, g):
        _, xh = _rms_parts(xv)
        return (xh * g,)

    return _ew(name, fn, [xs, gain], [xs.shape[1]], [BF16])[0]


def _ffn_chunks(f_all):
    return [slice(c, min(c + FFN_CHUNK, f_all)) for c in range(0, f_all, FFN_CHUNK)]


def _loss_head(xo, gain_f, target, d):
    r, xh = _rms_parts(xo)
    err = xh * gain_f - target
    dy = err * (1.0 / d)
    per_tok = jnp.mean(err * err, axis=-1, keepdims=True)
    return (_rms_bwd_dx(dy, gain_f, r, xh), jnp.sum(dy * xh, axis=0, keepdims=True),
            0.5 * jnp.sum(per_tok, axis=0, keepdims=True))


def _ffn_tile(x_ref, g_ref, wg_ref, wu_ref, wd_ref, h_ref, gg_ref, uu_ref):
    xv = x_ref[...]
    _, xh = _rms_parts(xv)
    h = (xh * g_ref[...]).astype(BF16)
    h_ref[...] = h
    acc = None
    for cols in _ffn_chunks(wd_ref.shape[0]):
        gg = _dot_nt(h, wg_ref[cols, :])
        uu = _dot_nt(h, wu_ref[cols, :])
        act = gg * _sigmoid(gg) * uu
        part = _dot_nn(act.astype(BF16), wd_ref[cols, :])
        acc = part if acc is None else acc + part
        gg_ref[:, cols] = gg.astype(BF16)
        uu_ref[:, cols] = uu.astype(BF16)
    return xv + 0.5 * acc


def _ffn_fwd(name, xs, gain, wg_t, wu_t, wd, tm=512, deps=(), head=None):
    rows, d = xs.shape
    f_all = wd.shape[0]
    tm = min(tm, rows)
    if head is not None:
        return _ffn_fwd_head(name, xs, gain, wg_t, wu_t, wd, tm, head)

    def body(x_ref, g_ref, wg_ref, wu_ref, wd_ref, *rest):
        xo_ref, h_ref, gg_ref, uu_ref = rest[-4:]
        xo_ref[...] = _ffn_tile(x_ref, g_ref, wg_ref, wu_ref, wd_ref, h_ref, gg_ref, uu_ref)

    tile = pl.BlockSpec((tm, d), lambda i: (i, 0))
    wspec = pl.BlockSpec((f_all, d), lambda i: (0, 0), pipeline_mode=pl.Buffered(1))
    hid = pl.BlockSpec((tm, f_all), lambda i: (i, 0))
    return pl.pallas_call(
        body, name=name, grid=(rows // tm,),
        in_specs=[tile, pl.BlockSpec((1, d), lambda i: (0, 0)), wspec, wspec, wspec] + [_ANY_SPEC] * len(deps),
        out_specs=[tile, tile, hid, hid],
        out_shape=[jax.ShapeDtypeStruct((rows, d), F32), jax.ShapeDtypeStruct((rows, d), BF16),
                   jax.ShapeDtypeStruct((rows, f_all), BF16), jax.ShapeDtypeStruct((rows, f_all), BF16)],
        compiler_params=_cparams("parallel"),
    )(xs, gain, wg_t, wu_t, wd, *deps)


def _ffn_fwd_head(name, xs, gain, wg_t, wu_t, wd, tm, head):
    rows, d = xs.shape
    f_all = wd.shape[0]
    gain_f, target = head

    def body(x_ref, g_ref, wg_ref, wu_ref, wd_ref, gf_ref, t_ref, dxo_ref, h_ref, gg_ref, uu_ref, dgf_ref, loss_ref):
        xo = _ffn_tile(x_ref, g_ref, wg_ref, wu_ref, wd_ref, h_ref, gg_ref, uu_ref)
        dxo, dgf, loss = _loss_head(xo, gf_ref[...], t_ref[...], d)
        dxo_ref[...] = dxo

        @pl.when(pl.program_id(0) == 0)
        def _():
            dgf_ref[...] = jnp.zeros_like(dgf_ref)
            loss_ref[...] = jnp.zeros_like(loss_ref)

        dgf_ref[...] += dgf
        loss_ref[...] += loss

    tile = pl.BlockSpec((tm, d), lambda i: (i, 0))
    row = pl.BlockSpec((1, d), lambda i: (0, 0))
    wspec = pl.BlockSpec((f_all, d), lambda i: (0, 0), pipeline_mode=pl.Buffered(1))
    hid = pl.BlockSpec((tm, f_all), lambda i: (i, 0))
    return pl.pallas_call(
        body, name=name, grid=(rows // tm,),
        in_specs=[tile, row, wspec, wspec, wspec, row, tile],
        out_specs=[tile, tile, hid, hid, row, pl.BlockSpec((1, 1), lambda i: (0, 0))],
        out_shape=[jax.ShapeDtypeStruct((rows, d), F32), jax.ShapeDtypeStruct((rows, d), BF16),
                   jax.ShapeDtypeStruct((rows, f_all), BF16), jax.ShapeDtypeStruct((rows, f_all), BF16),
                   jax.ShapeDtypeStruct((1, d), F32), jax.ShapeDtypeStruct((1, 1), F32)],
        compiler_params=_cparams("arbitrary"),
    )(xs, gain, wg_t, wu_t, wd, gain_f, target)


def _ffn_bwd(name, dxo, xs, gain, gg_all, uu_all, wg_t, wu_t, wd, tm=256):
    rows, d = xs.shape
    f_all = wd.shape[0]
    tm = min(tm, rows)

    def body(dxo_ref, x_ref, g_ref, gg_ref, uu_ref, wg_ref, wu_ref, wd_ref,
             dx_ref, dgg_ref, duu_ref, act_ref, dgain_ref):
        dxo = dxo_ref[...]
        df = (0.5 * dxo).astype(BF16)
        dh = None
        for cols in _ffn_chunks(f_all):
            gg = gg_ref[:, cols].astype(F32)
            uu = uu_ref[:, cols].astype(F32)
            sg = _sigmoid(gg)
            silu = gg * sg
            dact = _dot_nt(df, wd_ref[cols, :])
            duu = (dact * silu).astype(BF16)
            dgg = (dact * uu * (sg * (1.0 + gg * (1.0 - sg)))).astype(BF16)
            act_ref[:, cols] = (silu * uu).astype(BF16)
            dgg_ref[:, cols] = dgg
            duu_ref[:, cols] = duu
            part = _dot_nn(dgg, wg_ref[cols, :]) + _dot_nn(duu, wu_ref[cols, :])
            dh = part if dh is None else dh + part
        r, xh = _rms_parts(x_ref[...])
        dx_ref[...] = dxo + _rms_bwd_dx(dh, g_ref[...], r, xh)

        @pl.when(pl.program_id(0) == 0)
        def _():
            dgain_ref[...] = jnp.zeros_like(dgain_ref)

        dgain_ref[...] += jnp.sum(dh * xh, axis=0, keepdims=True)

    tile = pl.BlockSpec((tm, d), lambda i: (i, 0))
    row = pl.BlockSpec((1, d), lambda i: (0, 0))
    wspec = pl.BlockSpec((f_all, d), lambda i: (0, 0), pipeline_mode=pl.Buffered(1))
    hid = pl.BlockSpec((tm, f_all), lambda i: (i, 0))
    hid_shape = jax.ShapeDtypeStruct((rows, f_all), BF16)
    return pl.pallas_call(
        body, name=name, grid=(rows // tm,),
        in_specs=[tile, tile, row, hid, hid, wspec, wspec, wspec],
        out_specs=[tile, hid, hid, hid, row],
        out_shape=[jax.ShapeDtypeStruct((rows, d), F32), hid_shape, hid_shape, hid_shape,
                   jax.ShapeDtypeStruct((1, d), F32)],
        compiler_params=_cparams("arbitrary"),
    )(dxo, xs, gain, gg_all, uu_all, wg_t, wu_t, wd)


def _t5_bucket_np(dist):
    max_exact = N_BUCKETS // 2
    dd = np.maximum(dist, 1).astype(np.float32)
    large = max_exact + (np.log(dd / np.float32(max_exact)) / np.float32(math.log(MAX_DISTANCE / max_exact))
                         * np.float32(N_BUCKETS - max_exact)).astype(np.int32)
    large = np.minimum(large, N_BUCKETS - 1)
    return np.where(dist < max_exact, dist, large).astype(np.int32)


def _attn_geometry(g, rows):
    run = rows // 16
    dil = DILATIONS[g]
    if dil == 16:
        bq = BLOCK
        return dict(view=(16, run), block=(None, bq), grid=(16, run // bq), index=lambda r, n: (r, n),
                    pos=np.arange(bq), bq=bq)
    if dil == 4:
        per = BLOCK // 4
        pos = (4 * np.arange(per)[None, :] + np.arange(4)[:, None]).reshape(-1)
        return dict(view=(4, 4, run), block=(4, None, per), grid=(4, run // per), index=lambda r, n: (0, r, n),
                    pos=pos, bq=BLOCK)
    per = 16
    pos = (16 * np.arange(per)[None, :] + np.arange(16)[:, None]).reshape(-1)
    return dict(view=(16, run), block=(16, per), grid=(1, run // per), index=lambda r, n: (0, n),
                pos=pos, bq=16 * per)


def _attn_tables(g, rows):
    geo = _attn_geometry(g, rows)
    pos, bq = geo["pos"], geo["bq"]
    steps = pos[:, None] - np.concatenate([pos - bq, pos])[None, :]
    valid = (steps >= 0) & (steps <= BLOCK)
    bucket = _t5_bucket_np((np.maximum(steps, 0) * DILATIONS[g]).astype(np.int32))
    return bucket, valid.astype(np.int32)


def _bias_fwd(name, bucket, valid, table_t):
    bq = bucket.shape[0]

    def body(bk_ref, ok_ref, tab_ref, o_ref):
        bk = bk_ref[...]
        ok = ok_ref[...] > 0
        for h in range(HEADS_PER_GROUP):
            acc = jnp.zeros(bk.shape, F32)
            for b in range(N_BUCKETS):
                acc = jnp.where(bk == b, tab_ref[h, b], acc)
            o_ref[h] = jnp.where(ok, acc, NEG_INF)

    vm = pl.BlockSpec(memory_space=pltpu.VMEM)
    return pl.pallas_call(
        body, name=name, in_specs=[vm, vm, pl.BlockSpec(memory_space=pltpu.SMEM)], out_specs=vm,
        out_shape=jax.ShapeDtypeStruct((HEADS_PER_GROUP, bq, 2 * bq), F32),
    )(bucket, valid, table_t)


def _bias_bwd(name, bucket, dbias):
    def body(bk_ref, db_ref, o_ref):
        row_id = lax.broadcasted_iota(jnp.int32, (N_BUCKETS, 128), 0)
        col_id = lax.broadcasted_iota(jnp.int32, (N_BUCKETS, 128), 1)
        bk = bk_ref[...]
        acc = jnp.zeros((N_BUCKETS, 128), F32)
        for h in range(HEADS_PER_GROUP):
            db = db_ref[h]
            for b in range(N_BUCKETS):
                part = jnp.sum(jnp.where(bk == b, db, 0.0), axis=0, keepdims=True)
                tot = jnp.sum(part, axis=1, keepdims=True)
                acc = jnp.where((row_id == b) & (col_id == h), tot, acc)
        o_ref[...] = acc

    vm = pl.BlockSpec(memory_space=pltpu.VMEM)
    return pl.pallas_call(body, name=name, in_specs=[vm, vm], out_specs=vm,
                          out_shape=jax.ShapeDtypeStruct((N_BUCKETS, 128), F32))(bucket, dbias)


def _head_of_lane(nrows):
    return lax.broadcasted_iota(jnp.int32, (nrows, ATTN_OUT), 1) // HEAD_DIM


def _stack_heads(a, lane_head):
    zero = jnp.zeros_like(a)
    return jnp.concatenate([jnp.where(lane_head == h, a, zero) for h in range(HEADS_PER_GROUP)], axis=0)


def _unstack_heads(a4, lane_head, bq):
    out = a4[:bq]
    for h in range(1, HEADS_PER_GROUP):
        out = jnp.where(lane_head == h, a4[h * bq:(h + 1) * bq], out)
    return out


def _attn_specs(geo, cols, col_block, index):
    return pl.BlockSpec(geo["block"] + (cols,), lambda r, n: index(r, n) + (col_block,))


def _attn_fwd(name, qkv, g, bias4):
    rows = qkv.shape[0]
    geo = _attn_geometry(g, rows)
    bq, (nsub, nb), index = geo["bq"], geo["grid"], geo["index"]
    blk_shape = tuple(b for b in geo["block"] if b is not None) + (ATTN_OUT,)

    def body(q_ref, kc_ref, kp_ref, vc_ref, vp_ref, b_ref, o_ref, lse_ref):
        n = pl.program_id(1)
        lane_head = _head_of_lane(bq)
        flat = lambda ref: ref[...].reshape(bq, ATTN_OUT)
        q4 = _stack_heads(flat(q_ref), lane_head)
        k2 = jnp.concatenate([flat(kp_ref), flat(kc_ref)], axis=0)
        v2 = jnp.concatenate([flat(vp_ref), flat(vc_ref)], axis=0)
        s = _dot_nt(q4, k2) + b_ref[...]
        col = lax.broadcasted_iota(jnp.int32, s.shape, 1)
        s = jnp.where((col >= bq) | (n > 0), s, NEG_INF)
        mx = jnp.max(s, axis=-1, keepdims=True)
        p = jnp.exp(s - mx)
        den = jnp.sum(p, axis=-1, keepdims=True)
        o4 = _dot_nn(p.astype(BF16), v2) / den
        lse4 = jnp.broadcast_to(mx + jnp.log(den), (HEADS_PER_GROUP * bq, ATTN_OUT))
        o_ref[...] = _unstack_heads(o4, lane_head, bq).reshape(blk_shape)
        lse_ref[...] = _unstack_heads(lse4, lane_head, bq).reshape(blk_shape)

    prev = lambda r, n: index(r, jnp.maximum(n - 1, 0))
    view = lambda a: a.reshape(geo["view"] + (a.shape[1],))
    qkv_v = view(qkv)
    out_spec = _attn_specs(geo, ATTN_OUT, 0, index)
    out_shape = jax.ShapeDtypeStruct(geo["view"] + (ATTN_OUT,), F32)
    o, lse = pl.pallas_call(
        body, name=name, grid=(nsub, nb),
        in_specs=[_attn_specs(geo, ATTN_OUT, g, index), _attn_specs(geo, ATTN_OUT, 3 + g, index),
                  _attn_specs(geo, ATTN_OUT, 3 + g, prev), _attn_specs(geo, ATTN_OUT, 6 + g, index),
                  _attn_specs(geo, ATTN_OUT, 6 + g, prev), pl.BlockSpec(bias4.shape, lambda r, n: (0, 0))],
        out_specs=[out_spec, out_spec], out_shape=[out_shape, out_shape],
        compiler_params=_cparams("parallel", "arbitrary"),
    )(qkv_v, qkv_v, qkv_v, qkv_v, qkv_v, bias4)
    return o.reshape(rows, ATTN_OUT), lse.reshape(rows, ATTN_OUT)


def _attn_bwd(name, qkv, do, lse, cvec, g, bias4):
    rows = qkv.shape[0]
    geo = _attn_geometry(g, rows)
    bq, (nsub, nb), index = geo["bq"], geo["grid"], geo["index"]
    blk_shape = tuple(b for b in geo["block"] if b is not None) + (ATTN_OUT,)
    nlead = len(blk_shape) - 1

    def body(q_ref, kc_ref, kp_ref, vc_ref, vp_ref, do_ref, lse_ref, c_ref, b_ref,
             dq_ref, dk_ref, dv_ref, db_ref, kcar_ref, vcar_ref):
        r, n = pl.program_id(0), pl.program_id(1)
        valid = n < nb
        lane_head = _head_of_lane(bq)
        flat = lambda ref: ref[...].reshape(bq, ATTN_OUT)

        @pl.when((r == 0) & (n == 0))
        def _():
            kcar_ref[...] = jnp.zeros_like(kcar_ref)
            vcar_ref[...] = jnp.zeros_like(vcar_ref)
            db_ref[...] = jnp.zeros_like(db_ref)

        def column(ref, h):
            lead = (slice(None),) * nlead
            return ref[lead + (pl.ds(h * HEAD_DIM, 1),)].reshape(bq, 1)

        q4 = _stack_heads(flat(q_ref), lane_head)
        do4 = _stack_heads(flat(do_ref), lane_head)
        k2 = jnp.concatenate([flat(kp_ref), flat(kc_ref)], axis=0)
        v2 = jnp.concatenate([flat(vp_ref), flat(vc_ref)], axis=0)
        lse4 = jnp.concatenate([column(lse_ref, h) for h in range(HEADS_PER_GROUP)], axis=0)
        c4 = jnp.concatenate([column(c_ref, h) for h in range(HEADS_PER_GROUP)], axis=0)
        s = _dot_nt(q4, k2) + b_ref[...]
        col = lax.broadcasted_iota(jnp.int32, s.shape, 1)
        keep = ((col >= bq) | (n > 0)) & valid
        p = jnp.where(keep, jnp.exp(s - lse4), 0.0)
        ds = p * (_dot_nt(do4, v2) + c4)
        ds_b = ds.astype(BF16)

        @pl.when(valid)
        def _():
            dq = _unstack_heads(_dot_nn(ds_b, k2), lane_head, bq) * (HEAD_DIM ** -0.5)
            dq_ref[...] = dq.astype(BF16).reshape(blk_shape)

        dk2 = _dot_tn(ds_b, q4)
        dv2 = _dot_tn(p.astype(BF16), do4)
        dk_ref[...] = (kcar_ref[...] + dk2[:bq]).astype(BF16).reshape(blk_shape)
        dv_ref[...] = (vcar_ref[...] + dv2[:bq]).astype(BF16).reshape(blk_shape)
        kcar_ref[...] = dk2[bq:]
        vcar_ref[...] = dv2[bq:]
        db_ref[...] += ds

    cur = lambda r, n: index(r, jnp.minimum(n, nb - 1))
    prev = lambda r, n: index(r, jnp.maximum(jnp.minimum(n, nb - 1) - 1, 0))
    late = lambda r, n: index(r, jnp.maximum(n - 1, 0))
    view = lambda a: a.reshape(geo["view"] + (a.shape[1],))
    qkv_v = view(qkv)
    tile = _attn_specs(geo, ATTN_OUT, 0, cur)
    bias_spec = pl.BlockSpec(bias4.shape, lambda r, n: (0, 0))
    out_shape = jax.ShapeDtypeStruct(geo["view"] + (ATTN_OUT,), BF16)
    dq, dk, dv, db = pl.pallas_call(
        body, name=name, grid=(nsub, nb + 1),
        in_specs=[_attn_specs(geo, ATTN_OUT, g, cur), _attn_specs(geo, ATTN_OUT, 3 + g, cur),
                  _attn_specs(geo, ATTN_OUT, 3 + g, prev), _attn_specs(geo, ATTN_OUT, 6 + g, cur),
                  _attn_specs(geo, ATTN_OUT, 6 + g, prev), tile, tile, tile, bias_spec],
        out_specs=[tile, _attn_specs(geo, ATTN_OUT, 0, late), _attn_specs(geo, ATTN_OUT, 0, late), bias_spec],
        out_shape=[out_shape, out_shape, out_shape, jax.ShapeDtypeStruct(bias4.shape, F32)],
        scratch_shapes=[pltpu.VMEM((bq, ATTN_OUT), F32), pltpu.VMEM((bq, ATTN_OUT), F32)],
        compiler_params=_cparams("arbitrary", "arbitrary"),
    )(qkv_v, qkv_v, qkv_v, qkv_v, qkv_v, view(do), view(lse), view(cvec), bias4)
    return dq.reshape(rows, ATTN_OUT), dk.reshape(rows, ATTN_OUT), dv.reshape(rows, ATTN_OUT), db


def _group_weights(lses):
    mx = jnp.maximum(jnp.maximum(lses[0], lses[1]), lses[2])
    es = [jnp.exp(l - mx) for l in lses]
    den = es[0] + es[1] + es[2]
    return [e / den for e in es]


def _combine_fwd(name, os_, lses):
    def fn(o0, o1, o2, l0, l1, l2):
        ws = _group_weights([l0, l1, l2])
        out = ws[0] * o0 + ws[1] * o1 + ws[2] * o2
        return out, out

    return _ew(name, fn, [*os_, *lses], [ATTN_OUT, ATTN_OUT], [F32, BF16], tm=1024)


def _combine_bwd(name, do, oa, lses):
    def fn(dov, oav, l0, l1, l2):
        head_sum = (lax.broadcasted_iota(jnp.int32, (ATTN_OUT, ATTN_OUT), 0) // HEAD_DIM
                    == lax.broadcasted_iota(jnp.int32, (ATTN_OUT, ATTN_OUT), 1) // HEAD_DIM)
        ws = _group_weights([l0, l1, l2])
        prod = dov * oav
        hi = prod.astype(BF16)
        lo = (prod - hi.astype(F32)).astype(BF16)
        ones = jnp.where(head_sum, 1.0, 0.0).astype(BF16)
        bar = _dot_nn(hi, ones) + _dot_nn(lo, ones)
        return tuple(w * dov for w in ws) + tuple(-w * bar for w in ws)

    return _ew(name, fn, [do, oa, *lses], [ATTN_OUT] * 6, [BF16] * 3 + [F32] * 3, tm=1024)


def _ssm_disc(a_re, a_im, log_dt, b_re, b_im):
    dt = jnp.exp(log_dt)
    mag = jnp.exp(a_re * dt)
    ab_re = mag * jnp.cos(a_im * dt)
    ab_im = mag * jnp.sin(a_im * dt)
    den = a_re * a_re + a_im * a_im
    xr = ab_re - 1.0
    coef_re = (xr * a_re + ab_im * a_im) / den
    coef_im = (ab_im * a_re - xr * a_im) / den
    bb_re = coef_re[None] * b_re - coef_im[None] * b_im
    bb_im = coef_re[None] * b_im + coef_im[None] * b_re
    return ab_re, ab_im, bb_re, bb_im


def _ssm_params_fwd(name, a_re, a_im, log_dt, b_re, b_im):
    pows = jax.ShapeDtypeStruct((SCAN_STEPS,) + a_re.shape, F32)
    cgn = jax.ShapeDtypeStruct(b_re.shape, F32)

    def body(ar, ai, ld, br, bi, o_pr, o_pi, o_bbr, o_bbi):
        ab_re, ab_im, bb_re, bb_im = _ssm_disc(ar[...], ai[...], ld[...], br[...], bi[...])
        pr, pi = ab_re, ab_im
        for j in range(SCAN_STEPS):
            o_pr[j] = pr
            o_pi[j] = pi
            pr, pi = pr * ab_re - pi * ab_im, pr * ab_im + pi * ab_re
        o_bbr[...] = bb_re
        o_bbi[...] = bb_im

    vm = pl.BlockSpec(memory_space=pltpu.VMEM)
    return pl.pallas_call(body, name=name, in_specs=[vm] * 5, out_specs=[vm] * 4,
                          out_shape=[pows, pows, cgn, cgn])(a_re, a_im, log_dt, b_re, b_im)


def _ssm_params_bwd(name, a_re, a_im, log_dt, b_re, b_im, d_ab_re, d_ab_im, d_bb_re, d_bb_im):
    gn = jax.ShapeDtypeStruct(a_re.shape, F32)
    cgn = jax.ShapeDtypeStruct(b_re.shape, F32)

    def body(ar, ai, ld, br, bi, g0, g1, g2, g3, o_ar, o_ai, o_ld, o_br, o_bi):
        _, vjp = jax.vjp(_ssm_disc, ar[...], ai[...], ld[...], br[...], bi[...])
        outs = vjp((g0[...], g1[...], g2[...], g3[...]))
        for o_ref, o in zip((o_ar, o_ai, o_ld, o_br, o_bi), outs):
            o_ref[...] = o

    vm = pl.BlockSpec(memory_space=pltpu.VMEM)
    return pl.pallas_call(body, name=name, in_specs=[vm] * 9, out_specs=[vm] * 5,
                          out_shape=[gn, gn, jax.ShapeDtypeStruct(log_dt.shape, F32), cgn, cgn],
                          )(a_re, a_im, log_dt, b_re, b_im, d_ab_re, d_ab_im, d_bb_re, d_bb_im)


def _scan_block(s_ref, carry_ref, tmp_ref, pw_ref, reverse, sprev=None):
    nl = SSM_LANES
    halves = range(SCAN_COLS // SCAN_SUB)
    zero = jnp.zeros((SCAN_SUB, SCAN_LANES), F32)
    for half in (reversed(halves) if reverse else halves):
        sub_rows = pl.ds(half * SCAN_SUB, SCAN_SUB)
        for lc in range(nl // SCAN_LANES):
            re_l = pl.ds(lc * SCAN_LANES, SCAN_LANES)
            im_l = pl.ds(nl + lc * SCAN_LANES, SCAN_LANES)
            are, aim = pw_ref[0, :, re_l], pw_ref[0, :, im_l]

            def step_of(j):
                return SCAN_STEPS - 1 - j if reverse else j

            def pass1(j, st):
                sr, si = st
                jj = step_of(j)
                nr = are * sr - aim * si + s_ref[jj, sub_rows, re_l]
                ni = are * si + aim * sr + s_ref[jj, sub_rows, im_l]
                s_ref[jj, sub_rows, re_l] = nr
                s_ref[jj, sub_rows, im_l] = ni
                return nr, ni

            er, ei = lax.fori_loop(0, SCAN_STEPS, pass1, (zero, zero), unroll=2)
            tmp_ref[0:SCAN_SUB, re_l] = er
            tmp_ref[0:SCAN_SUB, im_l] = ei
            apr, api = pw_ref[SCAN_STEPS - 1, 0:1, re_l], pw_ref[SCAN_STEPS - 1, 0:1, im_l]
            sr, si = carry_ref[0:1, re_l], carry_ref[0:1, im_l]
            for step in range(SCAN_SUB):
                c = SCAN_SUB - 1 - step if reverse else step
                tmp_ref[SCAN_SUB + c:SCAN_SUB + c + 1, re_l] = sr
                tmp_ref[SCAN_SUB + c:SCAN_SUB + c + 1, im_l] = si
                e_r, e_i = tmp_ref[c:c + 1, re_l], tmp_ref[c:c + 1, im_l]
                sr, si = apr * sr - api * si + e_r, apr * si + api * sr + e_i
            carry_ref[0:1, re_l] = sr
            carry_ref[0:1, im_l] = si
            cr = tmp_ref[SCAN_SUB:2 * SCAN_SUB, re_l]
            ci = tmp_ref[SCAN_SUB:2 * SCAN_SUB, im_l]

            if sprev is None:
                def pass2(j, st):
                    pr, pi = pw_ref[j, :, re_l], pw_ref[j, :, im_l]
                    jj = step_of(j)
                    s_ref[jj, sub_rows, re_l] += pr * cr - pi * ci
                    s_ref[jj, sub_rows, im_l] += pr * ci + pi * cr
                    return st

                lax.fori_loop(0, SCAN_STEPS, pass2, 0, unroll=2)
            else:
                st_ref, prev_ref, have_prev, dab_ref = sprev

                def corrected(jj, pr, pi):
                    gr = s_ref[jj, sub_rows, re_l] + pr * cr - pi * ci
                    gi = s_ref[jj, sub_rows, im_l] + pr * ci + pi * cr
                    s_ref[jj, sub_rows, re_l] = gr
                    s_ref[jj, sub_rows, im_l] = gi
                    return gr, gi

                def pass2(j, st):
                    dr, di = st
                    jj = SCAN_STEPS - 1 - j
                    gr, gi = corrected(jj, pw_ref[j, :, re_l], pw_ref[j, :, im_l])
                    qr, qi = st_ref[jj - 1, sub_rows, re_l], st_ref[jj - 1, sub_rows, im_l]
                    return dr + gr * qr + gi * qi, di + gi * qr - gr * qi

                dr, di = lax.fori_loop(0, SCAN_STEPS - 1, pass2, (zero, zero), unroll=2)
                gr, gi = corrected(0, pw_ref[SCAN_STEPS - 1, :, re_l], pw_ref[SCAN_STEPS - 1, :, im_l])
                sub = lax.broadcasted_iota(jnp.int32, (SCAN_SUB, SCAN_LANES), 0)
                if half == 0:
                    pv_r = prev_ref[SCAN_SUB - 1:SCAN_SUB, re_l] * have_prev
                    pv_i = prev_ref[SCAN_SUB - 1:SCAN_SUB, im_l] * have_prev
                else:
                    before = pl.ds(half * SCAN_SUB - 1, 1)
                    pv_r, pv_i = st_ref[SCAN_STEPS - 1, before, re_l], st_ref[SCAN_STEPS - 1, before, im_l]
                shape = (SCAN_SUB, SCAN_LANES)
                qr = jnp.where(sub == 0, jnp.broadcast_to(pv_r, shape),
                               pltpu.roll(st_ref[SCAN_STEPS - 1, sub_rows, re_l], 1, 0))
                qi = jnp.where(sub == 0, jnp.broadcast_to(pv_i, shape),
                               pltpu.roll(st_ref[SCAN_STEPS - 1, sub_rows, im_l], 1, 0))
                dab_ref[:, re_l] += dr + gr * qr + gi * qi
                dab_ref[:, im_l] += di + gi * qr - gr * qi


def _scan_view(a):
    return a.reshape(16, a.shape[0] // 16, a.shape[1])


def _pair_tile(p):
    start = (p * 2 * SSM_GROUP // PAIR_TILE) * PAIR_TILE
    return slice(start, start + PAIR_TILE)


def _pair_lanes(p):
    return pl.ds(p * PAIR_LANES, PAIR_LANES), pl.ds(SSM_LANES + p * PAIR_LANES, PAIR_LANES)


def _pair_store(s_ref, p, val):
    re_l, im_l = _pair_lanes(p)
    s_ref[:, :, re_l] = val[:, :PAIR_LANES].reshape(16, SCAN_COLS, PAIR_LANES)
    s_ref[:, :, im_l] = val[:, PAIR_LANES:].reshape(16, SCAN_COLS, PAIR_LANES)


def _pair_load(s_ref, p):
    re_l, im_l = _pair_lanes(p)
    parts = [s_ref[:, :, l].reshape(SCAN_BLOCK, PAIR_LANES) for l in (re_l, im_l)]
    return jnp.concatenate(parts, axis=1).astype(BF16)


def _pair_sum(fn):
    per = PAIR_TILE // (2 * SSM_GROUP)
    tiles = []
    for t in range(SSM_PAIRS // per):
        acc = None
        for p in range(t * per, (t + 1) * per):
            part = fn(p)
            acc = part if acc is None else acc + part
        tiles.append(acc)
    return jnp.concatenate(tiles, axis=1)


def _ssm_fwd(name, u, bb_mats, c_mats, pw_rows, d_skip):
    rows = u.shape[0]
    nl2 = 2 * SSM_LANES
    nblk = rows // SCAN_BLOCK

    def body(u_ref, bb_ref, c_ref, pw_ref, d_ref, y_ref, yg_ref, s_ref, carry_ref, tmp_ref):
        @pl.when(pl.program_id(0) == 0)
        def _():
            carry_ref[...] = jnp.zeros_like(carry_ref)

        uv = u_ref[...].reshape(SCAN_BLOCK, SSM_WIDTH)
        ub = uv.astype(BF16)
        for p in range(SSM_PAIRS):
            _pair_store(s_ref, p, _dot_nn(ub[:, _pair_tile(p)], bb_ref[p]))
        _scan_block(s_ref, carry_ref, tmp_ref, pw_ref, reverse=False)
        ys = _pair_sum(lambda p: _dot_nt(_pair_load(s_ref, p), c_ref[p]))
        yv = ys + d_ref[...] * uv
        y_ref[...] = yv.reshape(16, SCAN_COLS, SSM_WIDTH)
        yg_ref[...] = jax.nn.gelu(yv).astype(BF16).reshape(16, SCAN_COLS, SSM_WIDTH)

    const = lambda shape: pl.BlockSpec(shape, lambda i: (0,) * len(shape))
    blk = lambda cols: pl.BlockSpec((16, SCAN_COLS, cols), lambda i: (0, i, 0))
    pair_mats = const((SSM_PAIRS, PAIR_TILE, PAIR_TILE))
    y, yg, s = pl.pallas_call(
        body, name=name, grid=(nblk,),
        in_specs=[blk(SSM_WIDTH), pair_mats, pair_mats, const((SCAN_STEPS, SCAN_SUB, nl2)), const((1, SSM_WIDTH))],
        out_specs=[blk(SSM_WIDTH), blk(SSM_WIDTH), blk(nl2)],
        out_shape=[jax.ShapeDtypeStruct((16, rows // 16, SSM_WIDTH), F32),
                   jax.ShapeDtypeStruct((16, rows // 16, SSM_WIDTH), BF16),
                   jax.ShapeDtypeStruct((16, rows // 16, nl2), F32)],
        scratch_shapes=[pltpu.VMEM((SCAN_SUB, nl2), F32), pltpu.VMEM((2 * SCAN_SUB, nl2), F32)],
        compiler_params=_cparams("arbitrary"),
    )(_scan_view(u), bb_mats, c_mats, pw_rows, d_skip)
    return y.reshape(rows, SSM_WIDTH), yg.reshape(rows, SSM_WIDTH), s.reshape(rows, nl2)


def _ssm_bwd(name, dy, u, states, bb_mats, c_mats, pwc_rows, d_skip):
    rows = u.shape[0]
    nl2 = 2 * SSM_LANES
    nblk = rows // SCAN_BLOCK

    def body(dy_ref, u_ref, st_ref, prev_ref, bb_ref, c_ref, pw_ref, d_ref,
             du_ref, dbb_ref, dc_ref, dab_ref, dd_ref, g_ref, carry_ref, tmp_ref):
        i = pl.program_id(0)

        @pl.when(i == 0)
        def _():
            carry_ref[...] = jnp.zeros_like(carry_ref)
            for ref in (dbb_ref, dc_ref, dab_ref, dd_ref):
                ref[...] = jnp.zeros_like(ref)

        dyv = dy_ref[...].reshape(SCAN_BLOCK, SSM_WIDTH)
        uv = u_ref[...].reshape(SCAN_BLOCK, SSM_WIDTH)
        dyb, ub = dyv.astype(BF16), uv.astype(BF16)
        for p in range(SSM_PAIRS):
            _pair_store(g_ref, p, _dot_nn(dyb[:, _pair_tile(p)], c_ref[p]))
        have_prev = (i < nblk - 1).astype(F32)
        _scan_block(g_ref, carry_ref, tmp_ref, pw_ref, reverse=True,
                    sprev=(st_ref, prev_ref, have_prev, dab_ref))

        def pair_work(p):
            gp = _pair_load(g_ref, p)
            dbb_ref[p] += _dot_tn(ub[:, _pair_tile(p)], gp)
            dc_ref[p] += _dot_tn(dyb[:, _pair_tile(p)], _pair_load(st_ref, p))
            return _dot_nt(gp, bb_ref[p])

        du_ref[...] = (_pair_sum(pair_work) + d_ref[...] * dyv).reshape(16, SCAN_COLS, SSM_WIDTH)
        dd_ref[...] += jnp.sum(dyv * uv, axis=0, keepdims=True)

    const = lambda shape: pl.BlockSpec(shape, lambda i: (0,) * len(shape))
    blk = lambda cols: pl.BlockSpec((16, SCAN_COLS, cols), lambda i: (0, nblk - 1 - i, 0))
    per8 = SCAN_COLS // SCAN_SUB
    prev_spec = pl.BlockSpec((None, SCAN_SUB, nl2), lambda i: (15, jnp.maximum((nblk - 1 - i) * per8 - 1, 0), 0))
    pair_mats = const((SSM_PAIRS, PAIR_TILE, PAIR_TILE))
    pair_shape = jax.ShapeDtypeStruct((SSM_PAIRS, PAIR_TILE, PAIR_TILE), F32)
    sv = _scan_view(states)
    du, dbb, dc, dab, dd = pl.pallas_call(
        body, name=name, grid=(nblk,),
        in_specs=[blk(SSM_WIDTH), blk(SSM_WIDTH), blk(nl2), prev_spec, pair_mats, pair_mats,
                  const((SCAN_STEPS, SCAN_SUB, nl2)), const((1, SSM_WIDTH))],
        out_specs=[blk(SSM_WIDTH), pair_mats, pair_mats, const((SCAN_SUB, nl2)), const((1, SSM_WIDTH))],
        out_shape=[jax.ShapeDtypeStruct((16, rows // 16, SSM_WIDTH), F32), pair_shape, pair_shape,
                   jax.ShapeDtypeStruct((SCAN_SUB, nl2), F32), jax.ShapeDtypeStruct((1, SSM_WIDTH), F32)],
        scratch_shapes=[pltpu.VMEM((16, SCAN_COLS, nl2), F32), pltpu.VMEM((SCAN_SUB, nl2), F32),
                        pltpu.VMEM((2 * SCAN_SUB, nl2), F32)],
        compiler_params=_cparams("arbitrary"),
    )(_scan_view(dy), _scan_view(u), sv, sv, bb_mats, c_mats, pwc_rows, d_skip)
    return du.reshape(rows, SSM_WIDTH), dbb, dc, dab, dd


def _adamw(name, w, m, v, gparts, tr):
    rows, cols = w.shape

    def body(w_ref, m_ref, v_ref, g_ref, og_ref, od_ref, om_ref, ov_ref):
        g = g_ref[0].astype(F32)
        for i in range(1, N_DEV):
            g = g + g_ref[i].astype(F32)
        m_new = B1 * m_ref[...] + (1.0 - B1) * g
        v_new = B2 * v_ref[...] + (1.0 - B2) * (g * g)
        m_hat = m_new / (1.0 - B1 ** STEP)
        v_hat = v_new / (1.0 - B2 ** STEP)
        og_ref[...] = g
        od_ref[...] = -LR * (m_hat / (jnp.sqrt(v_hat) + ADAM_EPS) + WD * w_ref[...])
        om_ref[...] = m_new
        ov_ref[...] = v_new

    spec = pl.BlockSpec((tr, cols), lambda i: (i, 0))
    shape = jax.ShapeDtypeStruct((rows, cols), F32)
    return pl.pallas_call(
        body, name=name, grid=(rows // tr,),
        in_specs=[spec, spec, spec, pl.BlockSpec((N_DEV, tr, cols), lambda i: (0, i, 0))],
        out_specs=[spec] * 4, out_shape=[shape] * 4,
        compiler_params=_cparams("parallel"),
    )(w, m, v, gparts)


_SHARDED = (
    ("ffn1_w_gate", True, (352, 1024)), ("ffn1_w_up", True, (352, 1024)), ("ffn1_w_down", False, (352, 1024)),
    ("w_in", True, (608, 1024)), ("ssm_w_glu", True, (128, 512)), ("w_attn_branch", True, (128, 256)),
    ("w_ssm_branch", True, (128, 512)), ("w_out", False, (128, 1024)),
    ("ffn2_w_gate", True, (352, 1024)), ("ffn2_w_up", True, (352, 1024)), ("ffn2_w_down", False, (352, 1024)),
)
_SMALL = ("ffn1_norm", "mix_norm", "gate_bias", "rel_bias_table", "ssm_a_re", "ssm_a_im", "ssm_log_dt",
          "ssm_b_re", "ssm_b_im", "ssm_c_re", "ssm_c_im", "ssm_d", "ffn2_norm", "final_norm")
_ORDER = ("ffn1_norm", "ffn1_w_gate", "ffn1_w_up", "ffn1_w_down", "mix_norm", "w_in", "gate_bias",
          "rel_bias_table", "ssm_a_re", "ssm_a_im", "ssm_log_dt", "ssm_b_re", "ssm_b_im", "ssm_c_re",
          "ssm_c_im", "ssm_d", "ssm_w_glu", "w_attn_branch", "w_ssm_branch", "w_out", "ffn2_norm",
          "ffn2_w_gate", "ffn2_w_up", "ffn2_w_down", "final_norm")


def _pack_rows(shape):
    return shape[0] * shape[1] // D_MODEL


_SHARD_INFO = {nm: (tr, shape) for nm, tr, shape in _SHARDED}
_PHASES = {
    "f1gu": ("ffn1_w_gate", "ffn1_w_up"), "f1d": ("ffn1_w_down",),
    "mix": ("w_in", "ssm_w_glu", "w_attn_branch", "w_ssm_branch", "w_out"),
    "f2": ("ffn2_w_gate", "ffn2_w_up", "ffn2_w_down"),
}


def _to_rows(a, nm):
    tr, shape = _SHARD_INFO[nm]
    return (a.T if tr else a).reshape(_pack_rows(shape), D_MODEL)


def _from_rows(p, nm):
    tr, shape = _SHARD_INFO[nm]
    a = p.reshape(shape)
    return a.T if tr else a


def _full_weight(gathered, nm):
    _, shape = _SHARD_INFO[nm]
    return gathered.reshape(N_DEV * shape[0], shape[1])


def _grad_blocks(g, nm):
    _, shape = _SHARD_INFO[nm]
    return g.astype(BF16).reshape(N_DEV, _pack_rows(shape), D_MODEL)


_SMALL_TILE = 8 * 128


def _small_rows(a):
    flat = a.reshape(-1)
    return jnp.pad(flat, (0, (-flat.shape[0]) % _SMALL_TILE)).reshape(-1, 128)


def _pack_small(ws, last=None):
    tail = jnp.zeros((), F32) if last is None else last
    return jnp.concatenate([_small_rows(ws[nm]) for nm in _SMALL] + [_small_rows(tail)], axis=0)


def _unpack_small(pack, like):
    out, r0 = {}, 0
    for nm in _SMALL:
        n = like[nm].size
        nr = 8 * -(-n // _SMALL_TILE)
        out[nm] = pack[r0:r0 + nr].reshape(-1)[:n].reshape(like[nm].shape)
        r0 += nr
    return out


def _residue_order(a):
    rows, cols = a.shape
    return a.reshape(rows // 16, 16, cols).transpose(1, 0, 2).reshape(rows, cols)


def _token_order(a):
    rows, cols = a.shape
    return a.reshape(16, rows // 16, cols).transpose(1, 0, 2).reshape(rows, cols)


_PAIRS_PER_TILE = PAIR_TILE // (2 * SSM_GROUP)
_PAIR_AXES = (SSM_PAIRS // _PAIRS_PER_TILE, _PAIRS_PER_TILE, 2)


def _pair_matrices(re, im):
    six = jnp.stack([re, im]).reshape((2,) + _PAIR_AXES + (SSM_GROUP, SSM_STATE))
    eye_j, eye_l = jnp.eye(_PAIRS_PER_TILE, dtype=re.dtype), jnp.eye(2, dtype=re.dtype)
    mats = jnp.einsum("xkjlcn,jJ,lL->kjJLcxln", six, eye_j, eye_l)
    return mats.reshape(SSM_PAIRS, PAIR_TILE, PAIR_TILE).astype(BF16)


def _pair_diagonals(acc):
    k, j, l = _PAIR_AXES
    eight = acc.reshape(k, j, j, l, SSM_GROUP, 2, l, SSM_STATE)
    eye_j, eye_l = jnp.eye(j, dtype=acc.dtype), jnp.eye(l, dtype=acc.dtype)
    own = jnp.einsum("kjJLcxln,jJ,lL->xkjlcn", eight, eye_j, eye_l).reshape(2, SSM_GROUPS, SSM_GROUP, SSM_STATE)
    return own[0], own[1]


def _local_step(xs, target, small, weights_of, send_grads, first_deps=()):
    rows = xs.shape[0]
    gfull, gsmall = {}, {}
    wf = dict(weights_of("f1", None))

    x1, h1, gg1, uu1 = _ffn_fwd("ffn1_fwd", xs, small["ffn1_norm"], wf["ffn1_w_gate"], wf["ffn1_w_up"],
                                wf["ffn1_w_down"], deps=first_deps)
    wf.update(weights_of("mix", x1))
    hmix = _rms_fwd("mix_norm_fwd", x1, small["mix_norm"])
    w_in = wf["w_in"]
    w_qkv, w_u, w_g = w_in[:3 * ATTN_WIDTH], w_in[3 * ATTN_WIDTH:3 * ATTN_WIDTH + SSM_WIDTH], w_in[3 * ATTN_WIDTH + SSM_WIDTH:]
    qscale = jnp.concatenate([jnp.full((1, ATTN_WIDTH), HEAD_DIM ** -0.5, F32), jnp.ones((1, 2 * ATTN_WIDTH), F32)], axis=1)
    qkv, = _mm("in_qkv", [(hmix, w_qkv)], True, 3 * ATTN_WIDTH, [BF16],
               epilogue=lambda acc, sc: (acc * sc,), extras=[(qscale, 0)], tn=ATTN_WIDTH)
    u, = _mm("in_u", [(hmix, w_u)], True, SSM_WIDTH, [F32])
    gates, = _mm("in_gates", [(hmix, w_g)], True, 2 * D_MODEL, [F32],
                 epilogue=lambda acc, b: (_sigmoid(acc + b),), extras=[(small["gate_bias"], 0)])

    table_t = small["rel_bias_table"].T
    tables, bias4, o_g, lse_g = [], [], [], []
    for g in range(N_GROUPS):
        bucket, valid = [jnp.asarray(t) for t in _attn_tables(g, rows)]
        bias_g = _bias_fwd(f"rel_bias_fwd_{g}", bucket, valid, table_t[g * HEADS_PER_GROUP:(g + 1) * HEADS_PER_GROUP])
        tables.append(bucket)
        bias4.append(bias_g.reshape(-1, bias_g.shape[-1]))
        o, lse = _attn_fwd(f"attn_fwd_{g}", qkv, g, bias4[g])
        o_g.append(o)
        lse_g.append(lse)
    oa_f32, oa = _combine_fwd("attn_combine_fwd", o_g, lse_g)
    y_attn, = _mm("attn_branch", [(oa, wf["w_attn_branch"])], True, D_MODEL, [F32])

    pw_re, pw_im, bb_re, bb_im = _ssm_params_fwd(
        "ssm_params_fwd", small["ssm_a_re"], small["ssm_a_im"], small["ssm_log_dt"].reshape(SSM_GROUPS, 1),
        small["ssm_b_re"].transpose(2, 0, 1), small["ssm_b_im"].transpose(2, 0, 1))

    def power_rows(sign):
        row = jnp.concatenate([pw_re.reshape(SCAN_STEPS, 1, SSM_LANES), sign * pw_im.reshape(SCAN_STEPS, 1, SSM_LANES)],
                              axis=2)
        return jnp.broadcast_to(row, (SCAN_STEPS, SCAN_SUB, 2 * SSM_LANES))

    bb_mats = _pair_matrices(bb_re.transpose(1, 0, 2), bb_im.transpose(1, 0, 2))
    c_mats = _pair_matrices(small["ssm_c_re"], -small["ssm_c_im"])
    d_skip = small["ssm_d"].reshape(1, SSM_WIDTH)
    y_raw, ygelu, states = _ssm_fwd("ssm_fwd", u, bb_mats, c_mats, power_rows(1.0), d_skip)
    glu, = _mm("ssm_glu", [(ygelu, wf["ssm_w_glu"])], True, 2 * SSM_WIDTH, [F32])
    ysg, = _ew("ssm_glu_act", lambda gv: (gv[:, :SSM_WIDTH] * _sigmoid(gv[:, SSM_WIDTH:]),), [glu], [SSM_WIDTH], [BF16])
    y_ssm, merged = _mm("ssm_branch_merge", [(ysg, wf["w_ssm_branch"])], True, D_MODEL, [F32, BF16],
                        epilogue=lambda acc, ga, gs, ya: (acc, ga * ya + gs * acc),
                        extras=[(gates, 0), (gates, D_MODEL), (y_attn, 0)])
    x2, = _mm("mix_out", [(merged, wf["w_out"])], False, D_MODEL, [F32],
              epilogue=lambda acc, res: (res + acc,), extras=[(x1, 0)])
    wf.update(weights_of("f2", x2))
    dx3, h2, gg2, uu2, gsmall["final_norm"], gsmall["loss"] = _ffn_fwd(
        "ffn2_fwd", x2, small["ffn2_norm"], wf["ffn2_w_gate"], wf["ffn2_w_up"], wf["ffn2_w_down"],
        head=(small["final_norm"].reshape(1, D_MODEL), target))

    dx2, dgg2, duu2, act2, gsmall["ffn2_norm"] = _ffn_bwd(
        "ffn2_bwd", dx3, x2, small["ffn2_norm"], gg2, uu2, wf["ffn2_w_gate"], wf["ffn2_w_up"], wf["ffn2_w_down"])
    gfull["ffn2_w_gate"] = _mm_tn("ffn2_dwg", dgg2, h2, out_dtype=BF16)
    gfull["ffn2_w_up"] = _mm_tn("ffn2_dwu", duu2, h2, out_dtype=BF16)
    gfull["ffn2_w_down"] = _mm_tn("ffn2_dwd", act2, dx3, scale=0.5, out_dtype=BF16)
    sent = send_grads("f2", gfull)

    def merge_bwd(dm, ga, gs, ya, ys):
        dza, dzs = dm * ya * ga * (1.0 - ga), dm * ys * gs * (1.0 - gs)
        return (dm * ga, dm * gs, dza, dzs, jnp.sum(dza, axis=0, keepdims=True), jnp.sum(dzs, axis=0, keepdims=True))

    dya, dys, dzga, dzgs, dba, dbs = _mm(
        "mix_out_bwd", [(dx2, wf["w_out"])], True, D_MODEL, [BF16] * 4, epilogue=merge_bwd, row_sums=2,
        extras=[(gates, 0), (gates, D_MODEL), (y_attn, 0), (y_ssm, 0)], deps=sent, tm=512, tn=D_MODEL)
    gfull["w_out"] = _mm_tn("dw_out", merged, dx2, out_dtype=BF16)
    gsmall["gate_bias"] = jnp.concatenate([dba, dbs], axis=1)

    gfull["w_ssm_branch"] = _mm_tn("dw_ssm_branch", dys, ysg, out_dtype=BF16)

    def glu_bwd(dysg, av, bv):
        sb = _sigmoid(bv)
        return (dysg * sb, dysg * av * sb * (1.0 - sb))

    dglu_a, dglu_b = _mm("ssm_branch_bwd", [(dys, wf["w_ssm_branch"])], False, SSM_WIDTH, [BF16, BF16],
                         epilogue=glu_bwd, extras=[(glu, 0), (glu, SSM_WIDTH)])
    w_glu = wf["ssm_w_glu"]
    gfull["ssm_w_glu"] = _mm_tn_stack("dw_glu", [dglu_a, dglu_b], ygelu, out_dtype=BF16)

    def gelu_bwd(acc, yv):
        _, vjp = jax.vjp(jax.nn.gelu, yv)
        return (vjp(acc)[0],)

    dy_raw, = _mm("ssm_glu_bwd", [(dglu_a, w_glu[:SSM_WIDTH]), (dglu_b, w_glu[SSM_WIDTH:])], False, SSM_WIDTH, [F32],
                  epilogue=gelu_bwd, extras=[(y_raw, 0)])
    du, dbb_acc, dc_acc, dab_rows, gsmall_d = _ssm_bwd(
        "ssm_bwd", dy_raw, u, states, bb_mats, c_mats, power_rows(-1.0), d_skip)
    gsmall["ssm_d"] = gsmall_d
    dbb_re, dbb_im = [a.transpose(1, 0, 2) for a in _pair_diagonals(dbb_acc)]
    dc_re, dc_im = _pair_diagonals(dc_acc)
    gsmall["ssm_c_re"], gsmall["ssm_c_im"] = dc_re, -dc_im
    dab = _colsum("ssm_dab", dab_rows)
    d_ar, d_ai, d_ld, d_br, d_bi = _ssm_params_bwd(
        "ssm_params_bwd", small["ssm_a_re"], small["ssm_a_im"], small["ssm_log_dt"].reshape(SSM_GROUPS, 1),
        small["ssm_b_re"].transpose(2, 0, 1), small["ssm_b_im"].transpose(2, 0, 1),
        dab[:, :SSM_LANES].reshape(SSM_GROUPS, SSM_STATE), dab[:, SSM_LANES:].reshape(SSM_GROUPS, SSM_STATE),
        dbb_re, dbb_im)
    gsmall["ssm_a_re"], gsmall["ssm_a_im"], gsmall["ssm_log_dt"] = d_ar, d_ai, d_ld.reshape(SSM_GROUPS)
    gsmall["ssm_b_re"], gsmall["ssm_b_im"] = d_br.transpose(1, 2, 0), d_bi.transpose(1, 2, 0)

    gfull["w_attn_branch"] = _mm_tn("dw_attn_branch", dya, oa, out_dtype=BF16)
    doa, = _mm("attn_branch_bwd", [(dya, wf["w_attn_branch"])], False, ATTN_OUT, [F32])
    dc = _combine_bwd("attn_combine_bwd", doa, oa_f32, lse_g)
    dqkv_cols = [None] * 9
    dtable = []
    for g in range(N_GROUPS):
        dq, dk, dv, db = _attn_bwd(f"attn_bwd_{g}", qkv, dc[g], lse_g[g], dc[3 + g], g, bias4[g])
        dqkv_cols[g], dqkv_cols[3 + g], dqkv_cols[6 + g] = dq, dk, dv
        dt = _bias_bwd(f"rel_bias_bwd_{g}", tables[g], db.reshape(HEADS_PER_GROUP, -1, db.shape[-1]))
        dtable.append(dt[:, :HEADS_PER_GROUP])
    gsmall["rel_bias_table"] = jnp.concatenate(dtable, axis=1)

    gfull["w_in"] = jnp.concatenate([_mm_tn_stack("dw_in_qkv", dqkv_cols, hmix, out_dtype=BF16),
                                     _mm_tn_stack("dw_in_rest", [du, dzga, dzgs], hmix, out_dtype=BF16)], axis=0)
    sent = send_grads("mix", gfull)
    qkv_pairs = [(c, w_qkv[i * ATTN_OUT:(i + 1) * ATTN_OUT]) for i, c in enumerate(dqkv_cols)]

    def mix_norm_bwd(dh, xv, gain, dres):
        r, xh = _rms_parts(xv)
        return dres + _rms_bwd_dx(dh, gain, r, xh), jnp.sum(dh * xh, axis=0, keepdims=True)

    dx1, gsmall["mix_norm"] = _mm(
        "in_bwd", qkv_pairs + [(du, w_u), (dzga, w_g[:D_MODEL]), (dzgs, w_g[D_MODEL:])], False, D_MODEL, [F32],
        epilogue=mix_norm_bwd, row_sums=1, extras=[(x1, 0), (small["mix_norm"], 0), (dx2, 0)], tm=512, tn=D_MODEL,
        deps=sent)

    dx, dgg1, duu1, act1, gsmall["ffn1_norm"] = _ffn_bwd(
        "ffn1_bwd", dx1, xs, small["ffn1_norm"], gg1, uu1, wf["ffn1_w_gate"], wf["ffn1_w_up"], wf["ffn1_w_down"])
    sent = send_grads("small", gsmall)
    gfull["ffn1_w_down"] = _mm_tn("ffn1_dwd", act1, dx1, scale=0.5, deps=sent, out_dtype=BF16)
    sent = send_grads("f1d", gfull)
    gfull["ffn1_w_gate"] = _mm_tn("ffn1_dwg", dgg1, h1, deps=sent, out_dtype=BF16)
    gfull["ffn1_w_up"] = _mm_tn("ffn1_dwu", duu1, h1, out_dtype=BF16)
    send_grads("f1gu", gfull)
    return dx, gsmall


def kernel(x, ffn1_norm, ffn1_w_gate, ffn1_w_up, ffn1_w_down, mix_norm, w_in, gate_bias, rel_bias_table, ssm_a_re, ssm_a_im, ssm_log_dt, ssm_b_re, ssm_b_im, ssm_c_re, ssm_c_im, ssm_d, ssm_w_glu, w_attn_branch, w_ssm_branch, w_out, ffn2_norm, ffn2_w_gate, ffn2_w_up, ffn2_w_down, final_norm, loss_target, m_ffn1_norm, m_ffn1_w_gate, m_ffn1_w_up, m_ffn1_w_down, m_mix_norm, m_w_in, m_gate_bias, m_rel_bias_table, m_ssm_a_re, m_ssm_a_im, m_ssm_log_dt, m_ssm_b_re, m_ssm_b_im, m_ssm_c_re, m_ssm_c_im, m_ssm_d, m_ssm_w_glu, m_w_attn_branch, m_w_ssm_branch, m_w_out, m_ffn2_norm, m_ffn2_w_gate, m_ffn2_w_up, m_ffn2_w_down, m_final_norm, v_ffn1_norm, v_ffn1_w_gate, v_ffn1_w_up, v_ffn1_w_down, v_mix_norm, v_w_in, v_gate_bias, v_rel_bias_table, v_ssm_a_re, v_ssm_a_im, v_ssm_log_dt, v_ssm_b_re, v_ssm_b_im, v_ssm_c_re, v_ssm_c_im, v_ssm_d, v_ssm_w_glu, v_w_attn_branch, v_w_ssm_branch, v_w_out, v_ffn2_norm, v_ffn2_w_gate, v_ffn2_w_up, v_ffn2_w_down, v_final_norm):
    given = dict(locals())
    shapes = {nm: given[nm].shape for nm in _ORDER}

    def strip(a):
        return a[0] if a.ndim >= 2 and a.shape[0] == 1 else a

    w = {nm: strip(given[nm]) for nm in _ORDER}
    m = {nm: strip(given["m_" + nm]) for nm in _ORDER}
    v = {nm: strip(given["v_" + nm]) for nm in _ORDER}
    for d in (w, m, v):
        d["rel_bias_table"] = d["rel_bias_table"].reshape(N_BUCKETS, N_GROUPS * HEADS_PER_GROUP)

    small = {nm: w[nm] for nm in _SMALL}
    small_in = dict(small)
    for nm in ("ffn1_norm", "mix_norm", "ffn2_norm", "gate_bias"):
        small_in[nm] = small[nm].reshape(1, -1)
    w_rows = {nm: _to_rows(w[nm], nm) for nm in _SHARD_INFO}

    def bf16_rows(phase):
        return [w_rows[nm].astype(BF16) for nm in _PHASES[phase]]

    f1_names = _PHASES["f1gu"] + _PHASES["f1d"]
    got_f1 = _all_gather("gather_f1", bf16_rows("f1gu") + bf16_rows("f1d"))
    pending_w = {"mix": _exchange_start("gather_mix_start", bf16_rows("mix"), gather=True, deps=[got_f1[0]])}
    pending_w["f2"] = _exchange_start("gather_f2_start", bf16_rows("f2"), gather=True, deps=[pending_w["mix"][4]])

    def weights_of(phase, after):
        if phase == "f1":
            return {nm: _full_weight(got, nm) for nm, got in zip(f1_names, got_f1)}
        landed = _exchange_wait(f"gather_{phase}_wait", pending_w[phase], after, gather=True)
        return {nm: _full_weight(got, nm) for nm, got in zip(_PHASES[phase], landed)}

    pending_g = {}

    def send_grads(phase, grads):
        if phase == "small":
            gs_pack = _pack_small({nm: grads[nm].reshape(small[nm].shape) for nm in _SMALL}, last=grads["loss"])
            pending_g[phase] = _exchange_start("gather_small_start", [gs_pack], gather=True)
        else:
            pending_g[phase] = _exchange_start(f"scatter_{phase}_start",
                                               [_grad_blocks(grads[nm], nm) for nm in _PHASES[phase]], gather=False)
        return [pending_g[phase][4]]

    dx, gsmall = _local_step(_residue_order(x[0]), _residue_order(loss_target[0]), small_in, weights_of,
                             send_grads, first_deps=[pending_w["f2"][4]])
    dx = _token_order(dx)

    updated = {}
    after = pending_g["f1gu"][4]
    for phase in ("f2", "mix", "f1d", "small", "f1gu"):
        landed = _exchange_wait(f"exchange_{phase}_wait", pending_g[phase], after, gather=phase == "small")
        if phase == "small":
            sm = _adamw("adamw_small", _pack_small(small), _pack_small({nm: m[nm] for nm in _SMALL}),
                        _pack_small({nm: v[nm] for nm in _SMALL}), landed[0], landed[0].shape[1])
            after = sm[0]
            continue
        for nm, recv in zip(_PHASES[phase], landed):
            tr = max(t for t in range(16, 353, 16) if w_rows[nm].shape[0] % t == 0)
            updated[nm] = _adamw(f"adamw_{nm}", w_rows[nm], _to_rows(m[nm], nm), _to_rows(v[nm], nm), recv, tr)
            after = updated[nm][0]

    loss = sm[0][-8, 0]
    outs = []
    for i in range(4):
        sml = _unpack_small(sm[i], small)
        outs.append([(_from_rows(updated[nm][i], nm) if nm in updated else sml[nm]).reshape(shapes[nm])
                     for nm in _ORDER])
    return (loss, dx[None], *outs[0], *outs[1], *outs[2], *outs[3])
```

```python
import math

import numpy as np
import jax
import jax.numpy as jnp
from jax import lax
from jax.experimental import pallas as pl
from jax.experimental.pallas import tpu as pltpu

F32 = jnp.float32
BF16 = jnp.bfloat16

N_DEV = 8
D_MODEL = 1024
D_FF = 2816
HEAD_DIM = 64
HEADS_PER_GROUP = 4
DILATIONS = (1, 4, 16)
N_GROUPS = 3
ATTN_WIDTH = 768
ATTN_OUT = 256
BLOCK = 128
N_BUCKETS = 32
MAX_DISTANCE = 2048
NEG_INF = -1e30
SSM_WIDTH = 512
SSM_GROUPS = 32
SSM_GROUP = 16
SSM_STATE = 64
SSM_LANES = SSM_GROUPS * SSM_STATE
SSM_PAIRS = SSM_GROUPS // 2
PAIR_LANES = 2 * SSM_STATE
PAIR_TILE = 256
EPS = 1e-6
LR, B1, B2, ADAM_EPS, WD, STEP = 0.001, 0.9, 0.999, 1e-08, 0.01, 10

VMEM_LIMIT_BYTES = 56 * 1024 * 1024
FFN_CHUNK = 768
SCAN_BLOCK = 256
SCAN_STEPS = 16
SCAN_COLS = SCAN_BLOCK // SCAN_STEPS
SCAN_SUB = 8
SCAN_LANES = 512

MESH = pl.DeviceIdType.MESH


def _cparams(*sem):
    return pltpu.CompilerParams(dimension_semantics=sem, vmem_limit_bytes=VMEM_LIMIT_BYTES)


def _dot(a, b, dims):
    return lax.dot_general(a, b, (dims, ((), ())), preferred_element_type=F32)


def _dot_nn(a, b):
    return _dot(a, b, ((1,), (0,)))


def _dot_nt(a, b):
    return _dot(a, b, ((1,), (1,)))


def _dot_tn(a, b):
    return _dot(a, b, ((0,), (0,)))


def _sigmoid(x):
    return 1.0 / (1.0 + jnp.exp(-x))


def _all_gather(name, xs_list):
    n = len(xs_list)

    def body(*refs):
        x_refs, out_refs = refs[:n], refs[n:2 * n]
        send_sems, recv_sems, local_sems = refs[2 * n:]
        x, y, c = lax.axis_index("x"), lax.axis_index("y"), lax.axis_index("c")
        me, sibling = (x, y, c), (x, y, 1 - c)
        chips = [(1 - x, y), (x, 1 - y), (1 - x, 1 - y)]

        def copy(a, k, block, to, own=False):
            px, py, pc = block
            rows = out_refs[a].at[4 * px + 2 * py + pc]
            return pltpu.make_async_remote_copy(
                src_ref=x_refs[a] if own else rows, dst_ref=rows,
                send_sem=send_sems.at[7 * a + k], recv_sem=recv_sems.at[7 * a + k], device_id=to,
                device_id_type=MESH)

        mine = [pltpu.make_async_copy(x_refs[a], out_refs[a].at[4 * x + 2 * y + c], local_sems.at[a]) for a in range(n)]
        first = []
        for a in range(n):
            mine[a].start()
            first.append(copy(a, 0, me, sibling, own=True))
            first += [copy(a, 1 + j, me, (*chip, c), own=True) for j, chip in enumerate(chips)]
        for cp in first:
            cp.start()
        passed = []
        for a in range(n):
            for j, chip in enumerate(chips):
                copy(a, 1 + j, (*chip, c), me).wait_recv()
                passed.append(copy(a, 4 + j, (*chip, c), sibling))
                passed[-1].start()
        for a in range(n):
            copy(a, 0, sibling, me).wait_recv()
            for j, chip in enumerate(chips):
                copy(a, 4 + j, (*chip, 1 - c), me).wait_recv()
        for cp in first + passed:
            cp.wait_send()
        for cp in mine:
            cp.wait()

    return pl.pallas_call(
        body, name=name,
        out_shape=[jax.ShapeDtypeStruct((N_DEV, *xs.shape), xs.dtype) for xs in xs_list],
        in_specs=[_ANY_SPEC] * n, out_specs=[_ANY_SPEC] * n,
        scratch_shapes=[pltpu.SemaphoreType.DMA((7 * n,)), pltpu.SemaphoreType.DMA((7 * n,)),
                        pltpu.SemaphoreType.DMA((n,))],
    )(*xs_list)


_HBM_SPEC = pl.BlockSpec(memory_space=pltpu.HBM)
_SEM_SPEC = pl.BlockSpec(memory_space=pltpu.SEMAPHORE)
_ANY_SPEC = pl.BlockSpec(memory_space=pl.ANY)
_EFFECT = pltpu.SideEffectType.DATAFLOW_SIDE_EFFECTING


def _peers(x, y, c):
    return [(1 - x if k & 4 else x, 1 - y if k & 2 else y, 1 - c if k & 1 else c) for k in range(1, N_DEV)]


def _exchange_copies(x_refs, land_refs, send_sems, recv_sems, gather):
    x, y, c = lax.axis_index("x"), lax.axis_index("y"), lax.axis_index("c")
    me = 4 * x + 2 * y + c
    copies = []
    for a, (x_ref, land_ref) in enumerate(zip(x_refs, land_refs)):
        for k, (px, py, pc) in enumerate(_peers(x, y, c)):
            src = x_ref if gather else x_ref.at[4 * px + 2 * py + pc]
            copies.append(pltpu.make_async_remote_copy(
                src_ref=src, dst_ref=land_ref.at[me], send_sem=send_sems.at[N_DEV * a + k],
                recv_sem=recv_sems.at[(N_DEV - 1) * a + k], device_id=(px, py, pc), device_id_type=MESH))
    owns = [pltpu.make_async_copy(x_ref if gather else x_ref.at[me], land_ref.at[me],
                                  send_sems.at[N_DEV * a + N_DEV - 1])
            for a, (x_ref, land_ref) in enumerate(zip(x_refs, land_refs))]
    return owns, copies


def _exchange_start(name, xs_list, gather, deps=()):
    n, nd = len(xs_list), len(deps)
    land_shapes = [(N_DEV, *xs.shape) if gather else xs.shape for xs in xs_list]

    def body(*refs):
        x_refs, land_refs = refs[:n], refs[n:2 * n]
        send_sems, recv_sems = refs[2 * n + nd:2 * n + nd + 2]
        token = refs[-1]
        owns, copies = _exchange_copies(x_refs, land_refs, send_sems, recv_sems, gather)
        for cp in copies + owns:
            cp.start()
        token[...] = jnp.zeros_like(token)

    hbm = lambda a: pltpu.with_memory_space_constraint(a, pltpu.HBM)
    outs = pl.pallas_call(
        body, name=name,
        out_shape=(pltpu.SemaphoreType.DMA((n * N_DEV,)), pltpu.SemaphoreType.DMA((n * (N_DEV - 1),)),
                   *[pltpu.HBM(xs.shape, xs.dtype) for xs in xs_list],
                   *[pltpu.HBM(shape, xs.dtype) for shape, xs in zip(land_shapes, xs_list)],
                   jax.ShapeDtypeStruct((8, 128), F32)),
        in_specs=(_HBM_SPEC,) * (2 * n) + (_ANY_SPEC,) * nd,
        out_specs=(_SEM_SPEC, _SEM_SPEC) + (_HBM_SPEC,) * (2 * n) + (pl.BlockSpec(memory_space=pltpu.VMEM),),
        input_output_aliases={i: 2 + i for i in range(2 * n)},
        compiler_params=pltpu.CompilerParams(has_side_effects=_EFFECT),
    )(*[hbm(xs) for xs in xs_list], *[hbm(lax.empty(shape, xs.dtype)) for shape, xs in zip(land_shapes, xs_list)],
      *deps)
    return outs[0], outs[1], list(outs[2:2 + n]), list(outs[2 + n:2 + 2 * n]), outs[-1]


def _exchange_wait(name, handle, after, gather):
    send_sems, recv_sems, xs_thru, lands_thru, _ = handle
    n = len(xs_thru)

    def body(*refs):
        x_refs, land_refs = refs[:n], refs[n:2 * n]
        send_sems, recv_sems = refs[2 * n:2 * n + 2]
        owns, copies = _exchange_copies(x_refs, land_refs, send_sems, recv_sems, gather)
        for cp in copies:
            cp.wait_send()
            cp.wait_recv()
        for cp in owns:
            cp.wait()

    outs = pl.pallas_call(
        body, name=name,
        out_shape=tuple(pltpu.HBM(a.shape, a.dtype) for a in xs_thru + lands_thru),
        in_specs=(_HBM_SPEC,) * (2 * n) + (_SEM_SPEC, _SEM_SPEC, _ANY_SPEC),
        out_specs=(_HBM_SPEC,) * (2 * n), input_output_aliases={i: i for i in range(2 * n)},
        compiler_params=pltpu.CompilerParams(has_side_effects=_EFFECT),
    )(*xs_thru, *lands_thru, send_sems, recv_sems, after)
    return list(outs[n:])


def _mm(name, pairs, nt, n_cols, out_dtypes, epilogue=None, extras=(), tm=1024, tn=512, deps=(), row_sums=0,
        out_cols=None):
    rows = pairs[0][0].shape[0]
    tm = min(tm, rows)
    tn = min(tn, n_cols)
    na, ne, nd, no = len(pairs), len(extras), len(deps), len(out_dtypes)

    def body(*refs):
        a_refs, w_refs = refs[:na], refs[na:2 * na]
        e_refs, o_refs = refs[2 * na:2 * na + ne], refs[2 * na + ne + nd:]
        acc = None
        for a_ref, w_ref in zip(a_refs, w_refs):
            a = a_ref[...].astype(BF16)
            w = w_ref[...].astype(BF16)
            p = _dot_nt(a, w) if nt else _dot_nn(a, w)
            acc = p if acc is None else acc + p
        outs = (acc,) if epilogue is None else epilogue(acc, *[e[...] for e in e_refs])
        for o_ref, o in zip(o_refs[:no], outs[:no]):
            o_ref[...] = o.astype(o_ref.dtype)
        for r_ref, o in zip(o_refs[no:], outs[no:]):
            @pl.when(pl.program_id(0) == 0)
            def _():
                r_ref[...] = jnp.zeros_like(r_ref)

            r_ref[...] += o

    in_specs = [pl.BlockSpec((tm, a.shape[1]), lambda i, j: (i, 0)) for a, _ in pairs]
    for _, w in pairs:
        if nt:
            in_specs.append(pl.BlockSpec((tn, w.shape[1]), lambda i, j: (j, 0)))
        else:
            in_specs.append(pl.BlockSpec((w.shape[0], tn), lambda i, j: (0, j)))
    for e, col_off in extras:
        off = col_off // tn
        if e.shape[0] == 1:
            in_specs.append(pl.BlockSpec((1, tn), lambda i, j, off=off: (0, j + off)))
        else:
            in_specs.append(pl.BlockSpec((tm, tn), lambda i, j, off=off: (i, j + off)))
    in_specs += [_ANY_SPEC] * nd
    if out_cols is None:
        out_cols = [n_cols] * no
    else:
        assert tn == n_cols, "outputs of other widths need the whole row in one block"
    assert not row_sums or tn == n_cols
    out_specs = [pl.BlockSpec((tm, tn * c // n_cols), lambda i, j: (i, j)) for c in out_cols]
    out_specs += [pl.BlockSpec((1, tn), lambda i, j: (0, j))] * row_sums
    out_shape = [jax.ShapeDtypeStruct((rows, c), dt) for c, dt in zip(out_cols, out_dtypes)]
    out_shape += [jax.ShapeDtypeStruct((1, n_cols), F32)] * row_sums
    outs = pl.pallas_call(
        body, name=name, grid=(rows // tm, n_cols // tn),
        in_specs=in_specs, out_specs=out_specs, out_shape=out_shape,
        compiler_params=_cparams("arbitrary" if row_sums else "parallel", "arbitrary"),
    )(*[a for a, _ in pairs], *[w for _, w in pairs], *[e for e, _ in extras], *deps)
    return outs


def _tn_rows(m):
    return max(b for b in range(128, min(m, 1408) + 1, 128) if m % b == 0)


def _mm_tn(name, a, b, scale=1.0, bm=None, tk=1024, deps=(), out_dtype=F32):
    rows, m = a.shape
    n = b.shape[1]
    bm = _tn_rows(m) if bm is None else bm
    tk = min(tk, rows)
    nk = rows // tk

    def body(a_ref, b_ref, *rest):
        o_ref, acc_ref = rest[-2:]
        k = pl.program_id(1)

        @pl.when(k == 0)
        def _():
            acc_ref[...] = jnp.zeros_like(acc_ref)

        acc_ref[...] += _dot_tn(a_ref[...].astype(BF16), b_ref[...].astype(BF16))

        @pl.when(k == nk - 1)
        def _():
            o_ref[...] = (acc_ref[...] * scale).astype(o_ref.dtype)

    return pl.pallas_call(
        body, name=name, grid=(m // bm, nk),
        in_specs=[pl.BlockSpec((tk, bm), lambda i, k: (k, i)), pl.BlockSpec((tk, n), lambda i, k: (k, 0))]
        + [_ANY_SPEC] * len(deps),
        out_specs=pl.BlockSpec((bm, n), lambda i, k: (i, 0)),
        out_shape=jax.ShapeDtypeStruct((m, n), out_dtype),
        scratch_shapes=[pltpu.VMEM((bm, n), F32)],
        compiler_params=_cparams("parallel", "arbitrary"),
    )(a, b, *deps)


def _mm_tn_stack(name, a_list, b, tk=1024, out_dtype=F32):
    rows, n = b.shape
    ms = [a.shape[1] for a in a_list]
    tk = min(tk, rows)
    nk = rows // tk
    na = len(a_list)

    def body(*refs):
        a_refs, b_ref, o_ref, acc_ref = refs[:na], refs[na], refs[na + 1], refs[na + 2]
        k = pl.program_id(0)

        @pl.when(k == 0)
        def _():
            acc_ref[...] = jnp.zeros_like(acc_ref)

        bv = b_ref[...].astype(BF16)
        r0 = 0
        for a_ref, m in zip(a_refs, ms):
            acc_ref[r0:r0 + m, :] += _dot_tn(a_ref[...].astype(BF16), bv)
            r0 += m

        @pl.when(k == nk - 1)
        def _():
            o_ref[...] = acc_ref[...].astype(o_ref.dtype)

    return pl.pallas_call(
        body, name=name, grid=(nk,),
        in_specs=[pl.BlockSpec((tk, m), lambda k: (k, 0)) for m in ms] + [pl.BlockSpec((tk, n), lambda k: (k, 0))],
        out_specs=pl.BlockSpec((sum(ms), n), lambda k: (0, 0)),
        out_shape=jax.ShapeDtypeStruct((sum(ms), n), out_dtype),
        scratch_shapes=[pltpu.VMEM((sum(ms), n), F32)],
        compiler_params=_cparams("arbitrary"),
    )(*a_list, b)


def _colsum(name, xs, tm=512):
    rows, cols = xs.shape
    tm = min(tm, rows)

    def body(x_ref, o_ref):
        @pl.when(pl.program_id(0) == 0)
        def _():
            o_ref[...] = jnp.zeros_like(o_ref)

        o_ref[...] += jnp.sum(x_ref[...].astype(F32), axis=0, keepdims=True)

    return pl.pallas_call(
        body, name=name, grid=(rows // tm,),
        in_specs=[pl.BlockSpec((tm, cols), lambda i: (i, 0))],
        out_specs=pl.BlockSpec((1, cols), lambda i: (0, 0)),
        out_shape=jax.ShapeDtypeStruct((1, cols), F32),
        compiler_params=_cparams("arbitrary"),
    )(xs)


def _ew(name, fn, ins, out_cols, out_dtypes, tm=512):
    rows = ins[0].shape[0]
    tm = min(tm, rows)
    ni = len(ins)

    def body(*refs):
        outs = fn(*[r[...] for r in refs[:ni]])
        for o_ref, o in zip(refs[ni:], outs):
            o_ref[...] = o.astype(o_ref.dtype)

    def spec(shape):
        if shape[0] == 1:
            return pl.BlockSpec((1, shape[1]), lambda i: (0, 0))
        return pl.BlockSpec((tm, shape[1]), lambda i: (i, 0))

    return pl.pallas_call(
        body, name=name, grid=(rows // tm,),
        in_specs=[spec(a.shape) for a in ins],
        out_specs=[pl.BlockSpec((tm, c), lambda i: (i, 0)) for c in out_cols],
        out_shape=[jax.ShapeDtypeStruct((rows, c), dt) for c, dt in zip(out_cols, out_dtypes)],
        compiler_params=_cparams("parallel"),
    )(*ins)


def _rms_parts(xv):
    r = lax.rsqrt(jnp.mean(xv * xv, axis=-1, keepdims=True) + EPS)
    return r, xv * r


def _rms_bwd_dx(dh, gain, r, xh):
    dxh = dh * gain
    return r * (dxh - xh * jnp.mean(dxh * xh, axis=-1, keepdims=True))


def _ffn_chunks(f_all):
    return [slice(c, min(c + FFN_CHUNK, f_all)) for c in range(0, f_all, FFN_CHUNK)]


def _loss_head(xo, gain_f, target, d):
    r, xh = _rms_parts(xo)
    err = xh * gain_f - target
    dy = err * (1.0 / d)
    per_tok = jnp.mean(err * err, axis=-1, keepdims=True)
    return (_rms_bwd_dx(dy, gain_f, r, xh), jnp.sum(dy * xh, axis=0, keepdims=True),
            0.5 * jnp.sum(per_tok, axis=0, keepdims=True))


def _ffn_tile(x_ref, g_ref, wg_ref, wu_ref, wd_ref, h_ref, gg_ref, uu_ref):
    xv = x_ref[...]
    _, xh = _rms_parts(xv)
    h = (xh * g_ref[...]).astype(BF16)
    h_ref[...] = h
    acc = None
    for cols in _ffn_chunks(wd_ref.shape[0]):
        gg = _dot_nt(h, wg_ref[cols, :])
        uu = _dot_nt(h, wu_ref[cols, :])
        act = gg * _sigmoid(gg) * uu
        part = _dot_nn(act.astype(BF16), wd_ref[cols, :])
        acc = part if acc is None else acc + part
        gg_ref[:, cols] = gg.astype(BF16)
        uu_ref[:, cols] = uu.astype(BF16)
    return xv + 0.5 * acc


def _ffn_fwd(name, xs, gain, wg_t, wu_t, wd, next_gain, tm=512, deps=()):
    rows, d = xs.shape
    f_all = wd.shape[0]
    tm = min(tm, rows)

    def body(x_ref, g_ref, wg_ref, wu_ref, wd_ref, ng_ref, *rest):
        xo_ref, h_ref, gg_ref, uu_ref, hn_ref = rest[-5:]
        xo = _ffn_tile(x_ref, g_ref, wg_ref, wu_ref, wd_ref, h_ref, gg_ref, uu_ref)
        xo_ref[...] = xo
        hn_ref[...] = (_rms_parts(xo)[1] * ng_ref[...]).astype(BF16)

    tile = pl.BlockSpec((tm, d), lambda i: (i, 0))
    row = pl.BlockSpec((1, d), lambda i: (0, 0))
    wspec = pl.BlockSpec((f_all, d), lambda i: (0, 0), pipeline_mode=pl.Buffered(1))
    hid = pl.BlockSpec((tm, f_all), lambda i: (i, 0))
    return pl.pallas_call(
        body, name=name, grid=(rows // tm,),
        in_specs=[tile, row, wspec, wspec, wspec, row] + [_ANY_SPEC] * len(deps),
        out_specs=[tile, tile, hid, hid, tile],
        out_shape=[jax.ShapeDtypeStruct((rows, d), F32), jax.ShapeDtypeStruct((rows, d), BF16),
                   jax.ShapeDtypeStruct((rows, f_all), BF16), jax.ShapeDtypeStruct((rows, f_all), BF16),
                   jax.ShapeDtypeStruct((rows, d), BF16)],
        compiler_params=_cparams("parallel"),
    )(xs, gain, wg_t, wu_t, wd, next_gain, *deps)


def _ffn_fwd_head(name, xs, gain, wg_t, wu_t, wd, gain_f, target, tm=512):
    rows, d = xs.shape
    f_all = wd.shape[0]
    tm = min(tm, rows)

    def body(x_ref, g_ref, wg_ref, wu_ref, wd_ref, gf_ref, t_ref, dxo_ref, h_ref, gg_ref, uu_ref, dgf_ref, loss_ref):
        xo = _ffn_tile(x_ref, g_ref, wg_ref, wu_ref, wd_ref, h_ref, gg_ref, uu_ref)
        dxo, dgf, loss = _loss_head(xo, gf_ref[...], t_ref[...], d)
        dxo_ref[...] = dxo

        @pl.when(pl.program_id(0) == 0)
        def _():
            dgf_ref[...] = jnp.zeros_like(dgf_ref)
            loss_ref[...] = jnp.zeros_like(loss_ref)

        dgf_ref[...] += dgf
        loss_ref[...] += loss

    tile = pl.BlockSpec((tm, d), lambda i: (i, 0))
    row = pl.BlockSpec((1, d), lambda i: (0, 0))
    wspec = pl.BlockSpec((f_all, d), lambda i: (0, 0), pipeline_mode=pl.Buffered(1))
    hid = pl.BlockSpec((tm, f_all), lambda i: (i, 0))
    return pl.pallas_call(
        body, name=name, grid=(rows // tm,),
        in_specs=[tile, row, wspec, wspec, wspec, row, tile],
        out_specs=[tile, tile, hid, hid, row, pl.BlockSpec((1, 1), lambda i: (0, 0))],
        out_shape=[jax.ShapeDtypeStruct((rows, d), F32), jax.ShapeDtypeStruct((rows, d), BF16),
                   jax.ShapeDtypeStruct((rows, f_all), BF16), jax.ShapeDtypeStruct((rows, f_all), BF16),
                   jax.ShapeDtypeStruct((1, d), F32), jax.ShapeDtypeStruct((1, 1), F32)],
        compiler_params=_cparams("arbitrary"),
    )(xs, gain, wg_t, wu_t, wd, gain_f, target)


def _ffn_bwd(name, dxo, xs, gain, gg_all, uu_all, wg_t, wu_t, wd, tm=256):
    rows, d = xs.shape
    f_all = wd.shape[0]
    tm = min(tm, rows)

    def body(dxo_ref, x_ref, g_ref, gg_ref, uu_ref, wg_ref, wu_ref, wd_ref,
             dx_ref, dgg_ref, duu_ref, act_ref, dgain_ref):
        dxo = dxo_ref[...]
        df = (0.5 * dxo).astype(BF16)
        dh = None
        for cols in _ffn_chunks(f_all):
            gg = gg_ref[:, cols].astype(F32)
            uu = uu_ref[:, cols].astype(F32)
            sg = _sigmoid(gg)
            silu = gg * sg
            dact = _dot_nt(df, wd_ref[cols, :])
            duu = (dact * silu).astype(BF16)
            dgg = (dact * uu * (sg * (1.0 + gg * (1.0 - sg)))).astype(BF16)
            act_ref[:, cols] = (silu * uu).astype(BF16)
            dgg_ref[:, cols] = dgg
            duu_ref[:, cols] = duu
            part = _dot_nn(dgg, wg_ref[cols, :]) + _dot_nn(duu, wu_ref[cols, :])
            dh = part if dh is None else dh + part
        r, xh = _rms_parts(x_ref[...])
        dx_ref[...] = dxo + _rms_bwd_dx(dh, g_ref[...], r, xh)

        @pl.when(pl.program_id(0) == 0)
        def _():
            dgain_ref[...] = jnp.zeros_like(dgain_ref)

        dgain_ref[...] += jnp.sum(dh * xh, axis=0, keepdims=True)

    tile = pl.BlockSpec((tm, d), lambda i: (i, 0))
    row = pl.BlockSpec((1, d), lambda i: (0, 0))
    wspec = pl.BlockSpec((f_all, d), lambda i: (0, 0), pipeline_mode=pl.Buffered(1))
    hid = pl.BlockSpec((tm, f_all), lambda i: (i, 0))
    hid_shape = jax.ShapeDtypeStruct((rows, f_all), BF16)
    return pl.pallas_call(
        body, name=name, grid=(rows // tm,),
        in_specs=[tile, tile, row, hid, hid, wspec, wspec, wspec],
        out_specs=[tile, hid, hid, hid, row],
        out_shape=[jax.ShapeDtypeStruct((rows, d), F32), hid_shape, hid_shape, hid_shape,
                   jax.ShapeDtypeStruct((1, d), F32)],
        compiler_params=_cparams("arbitrary"),
    )(dxo, xs, gain, gg_all, uu_all, wg_t, wu_t, wd)


def _t5_bucket_np(dist):
    max_exact = N_BUCKETS // 2
    dd = np.maximum(dist, 1).astype(np.float32)
    large = max_exact + (np.log(dd / np.float32(max_exact)) / np.float32(math.log(MAX_DISTANCE / max_exact))
                         * np.float32(N_BUCKETS - max_exact)).astype(np.int32)
    large = np.minimum(large, N_BUCKETS - 1)
    return np.where(dist < max_exact, dist, large).astype(np.int32)


def _attn_geometry(g, rows):
    run = rows // 16
    dil = DILATIONS[g]
    if dil == 16:
        bq = BLOCK
        return dict(view=(16, run), block=(None, bq), grid=(16, run // bq), index=lambda r, n: (r, n),
                    pos=np.arange(bq), bq=bq)
    if dil == 4:
        per = BLOCK // 4
        pos = (4 * np.arange(per)[None, :] + np.arange(4)[:, None]).reshape(-1)
        return dict(view=(4, 4, run), block=(4, None, per), grid=(4, run // per), index=lambda r, n: (0, r, n),
                    pos=pos, bq=BLOCK)
    per = 16
    pos = (16 * np.arange(per)[None, :] + np.arange(16)[:, None]).reshape(-1)
    return dict(view=(16, run), block=(16, per), grid=(1, run // per), index=lambda r, n: (0, n),
                pos=pos, bq=16 * per)


def _attn_tables(g, rows):
    geo = _attn_geometry(g, rows)
    pos, bq = geo["pos"], geo["bq"]
    steps = pos[:, None] - np.concatenate([pos - bq, pos])[None, :]
    valid = (steps >= 0) & (steps <= BLOCK)
    bucket = _t5_bucket_np((np.maximum(steps, 0) * DILATIONS[g]).astype(np.int32))
    return bucket, valid.astype(np.int32)


def _bias_fwd(name, bucket, valid, table_t):
    bq = bucket.shape[0]

    def body(bk_ref, ok_ref, tab_ref, o_ref):
        bk = bk_ref[...]
        ok = ok_ref[...] > 0
        for h in range(HEADS_PER_GROUP):
            acc = jnp.zeros(bk.shape, F32)
            for b in range(N_BUCKETS):
                acc = jnp.where(bk == b, tab_ref[h, b], acc)
            o_ref[h] = jnp.where(ok, acc, NEG_INF)

    vm = pl.BlockSpec(memory_space=pltpu.VMEM)
    return pl.pallas_call(
        body, name=name, in_specs=[vm, vm, pl.BlockSpec(memory_space=pltpu.SMEM)], out_specs=vm,
        out_shape=jax.ShapeDtypeStruct((HEADS_PER_GROUP, bq, 2 * bq), F32),
    )(bucket, valid, table_t)


def _bias_bwd(name, bucket, dbias):
    def body(bk_ref, db_ref, o_ref):
        row_id = lax.broadcasted_iota(jnp.int32, (N_BUCKETS, 128), 0)
        col_id = lax.broadcasted_iota(jnp.int32, (N_BUCKETS, 128), 1)
        bk = bk_ref[...]
        acc = jnp.zeros((N_BUCKETS, 128), F32)
        for h in range(HEADS_PER_GROUP):
            db = db_ref[h]
            for b in range(N_BUCKETS):
                part = jnp.sum(jnp.where(bk == b, db, 0.0), axis=0, keepdims=True)
                tot = jnp.sum(part, axis=1, keepdims=True)
                acc = jnp.where((row_id == b) & (col_id == h), tot, acc)
        o_ref[...] = acc

    vm = pl.BlockSpec(memory_space=pltpu.VMEM)
    return pl.pallas_call(body, name=name, in_specs=[vm, vm], out_specs=vm,
                          out_shape=jax.ShapeDtypeStruct((N_BUCKETS, 128), F32))(bucket, dbias)


def _head_of_lane(nrows):
    return lax.broadcasted_iota(jnp.int32, (nrows, ATTN_OUT), 1) // HEAD_DIM


def _stack_heads(a, lane_head):
    zero = jnp.zeros_like(a)
    return jnp.concatenate([jnp.where(lane_head == h, a, zero) for h in range(HEADS_PER_GROUP)], axis=0)


def _unstack_heads(a4, lane_head, bq):
    out = a4[:bq]
    for h in range(1, HEADS_PER_GROUP):
        out = jnp.where(lane_head == h, a4[h * bq:(h + 1) * bq], out)
    return out


def _attn_specs(geo, cols, col_block, index):
    return pl.BlockSpec(geo["block"] + (cols,), lambda r, n: index(r, n) + (col_block,))


def _attn_fwd(name, qkv, g, bias4):
    rows = qkv.shape[0]
    geo = _attn_geometry(g, rows)
    bq, (nsub, nb), index = geo["bq"], geo["grid"], geo["index"]
    blk_shape = tuple(b for b in geo["block"] if b is not None) + (ATTN_OUT,)

    def body(q_ref, kc_ref, kp_ref, vc_ref, vp_ref, b_ref, o_ref, lse_ref):
        n = pl.program_id(1)
        lane_head = _head_of_lane(bq)
        flat = lambda ref: ref[...].reshape(bq, ATTN_OUT)
        q4 = _stack_heads(flat(q_ref), lane_head)
        k2 = jnp.concatenate([flat(kp_ref), flat(kc_ref)], axis=0)
        v2 = jnp.concatenate([flat(vp_ref), flat(vc_ref)], axis=0)
        s = _dot_nt(q4, k2) + b_ref[...]
        col = lax.broadcasted_iota(jnp.int32, s.shape, 1)
        s = jnp.where((col >= bq) | (n > 0), s, NEG_INF)
        mx = jnp.max(s, axis=-1, keepdims=True)
        p = jnp.exp(s - mx)
        den = jnp.sum(p, axis=-1, keepdims=True)
        o4 = _dot_nn(p.astype(BF16), v2) / den
        lse4 = jnp.broadcast_to(mx + jnp.log(den), (HEADS_PER_GROUP * bq, ATTN_OUT))
        o_ref[...] = _unstack_heads(o4, lane_head, bq).reshape(blk_shape)
        lse_ref[...] = _unstack_heads(lse4, lane_head, bq).reshape(blk_shape)

    prev = lambda r, n: index(r, jnp.maximum(n - 1, 0))
    view = lambda a: a.reshape(geo["view"] + (a.shape[1],))
    qkv_v = view(qkv)
    out_spec = _attn_specs(geo, ATTN_OUT, 0, index)
    out_shape = jax.ShapeDtypeStruct(geo["view"] + (ATTN_OUT,), F32)
    o, lse = pl.pallas_call(
        body, name=name, grid=(nsub, nb),
        in_specs=[_attn_specs(geo, ATTN_OUT, g, index), _attn_specs(geo, ATTN_OUT, 3 + g, index),
                  _attn_specs(geo, ATTN_OUT, 3 + g, prev), _attn_specs(geo, ATTN_OUT, 6 + g, index),
                  _attn_specs(geo, ATTN_OUT, 6 + g, prev), pl.BlockSpec(bias4.shape, lambda r, n: (0, 0))],
        out_specs=[out_spec, out_spec], out_shape=[out_shape, out_shape],
        compiler_params=_cparams("parallel", "arbitrary"),
    )(qkv_v, qkv_v, qkv_v, qkv_v, qkv_v, bias4)
    return o.reshape(rows, ATTN_OUT), lse.reshape(rows, ATTN_OUT)


def _attn_bwd(name, qkv, do, lse, cvec, g, bias4):
    rows = qkv.shape[0]
    geo = _attn_geometry(g, rows)
    bq, (nsub, nb), index = geo["bq"], geo["grid"], geo["index"]
    blk_shape = tuple(b for b in geo["block"] if b is not None) + (ATTN_OUT,)
    nlead = len(blk_shape) - 1

    def body(q_ref, kc_ref, kp_ref, vc_ref, vp_ref, do_ref, lse_ref, c_ref, b_ref,
             dq_ref, dk_ref, dv_ref, db_ref, kcar_ref, vcar_ref):
        r, n = pl.program_id(0), pl.program_id(1)
        valid = n < nb
        lane_head = _head_of_lane(bq)
        flat = lambda ref: ref[...].reshape(bq, ATTN_OUT)

        @pl.when((r == 0) & (n == 0))
        def _():
            kcar_ref[...] = jnp.zeros_like(kcar_ref)
            vcar_ref[...] = jnp.zeros_like(vcar_ref)
            db_ref[...] = jnp.zeros_like(db_ref)

        def column(ref, h):
            lead = (slice(None),) * nlead
            return ref[lead + (pl.ds(h * HEAD_DIM, 1),)].reshape(bq, 1)

        q4 = _stack_heads(flat(q_ref), lane_head)
        do4 = _stack_heads(flat(do_ref), lane_head)
        k2 = jnp.concatenate([flat(kp_ref), flat(kc_ref)], axis=0)
        v2 = jnp.concatenate([flat(vp_ref), flat(vc_ref)], axis=0)
        lse4 = jnp.concatenate([column(lse_ref, h) for h in range(HEADS_PER_GROUP)], axis=0)
        c4 = jnp.concatenate([column(c_ref, h) for h in range(HEADS_PER_GROUP)], axis=0)
        s = _dot_nt(q4, k2) + b_ref[...]
        col = lax.broadcasted_iota(jnp.int32, s.shape, 1)
        keep = ((col >= bq) | (n > 0)) & valid
        p = jnp.where(keep, jnp.exp(s - lse4), 0.0)
        ds = p * (_dot_nt(do4, v2) + c4)
        ds_b = ds.astype(BF16)

        @pl.when(valid)
        def _():
            dq = _unstack_heads(_dot_nn(ds_b, k2), lane_head, bq) * (HEAD_DIM ** -0.5)
            dq_ref[...] = dq.astype(BF16).reshape(blk_shape)

        dk2 = _dot_tn(ds_b, q4)
        dv2 = _dot_tn(p.astype(BF16), do4)
        dk_ref[...] = (kcar_ref[...] + dk2[:bq]).astype(BF16).reshape(blk_shape)
        dv_ref[...] = (vcar_ref[...] + dv2[:bq]).astype(BF16).reshape(blk_shape)
        kcar_ref[...] = dk2[bq:]
        vcar_ref[...] = dv2[bq:]
        db_ref[...] += ds

    cur = lambda r, n: index(r, jnp.minimum(n, nb - 1))
    prev = lambda r, n: index(r, jnp.maximum(jnp.minimum(n, nb - 1) - 1, 0))
    late = lambda r, n: index(r, jnp.maximum(n - 1, 0))
    view = lambda a: a.reshape(geo["view"] + (a.shape[1],))
    qkv_v = view(qkv)
    tile = _attn_specs(geo, ATTN_OUT, 0, cur)
    bias_spec = pl.BlockSpec(bias4.shape, lambda r, n: (0, 0))
    out_shape = jax.ShapeDtypeStruct(geo["view"] + (ATTN_OUT,), BF16)
    dq, dk, dv, db = pl.pallas_call(
        body, name=name, grid=(nsub, nb + 1),
        in_specs=[_attn_specs(geo, ATTN_OUT, g, cur), _attn_specs(geo, ATTN_OUT, 3 + g, cur),
                  _attn_specs(geo, ATTN_OUT, 3 + g, prev), _attn_specs(geo, ATTN_OUT, 6 + g, cur),
                  _attn_specs(geo, ATTN_OUT, 6 + g, prev), tile, tile, tile, bias_spec],
        out_specs=[tile, _attn_specs(geo, ATTN_OUT, 0, late), _attn_specs(geo, ATTN_OUT, 0, late), bias_spec],
        out_shape=[out_shape, out_shape, out_shape, jax.ShapeDtypeStruct(bias4.shape, F32)],
        scratch_shapes=[pltpu.VMEM((bq, ATTN_OUT), F32), pltpu.VMEM((bq, ATTN_OUT), F32)],
        compiler_params=_cparams("arbitrary", "arbitrary"),
    )(qkv_v, qkv_v, qkv_v, qkv_v, qkv_v, view(do), view(lse), view(cvec), bias4)
    return dq.reshape(rows, ATTN_OUT), dk.reshape(rows, ATTN_OUT), dv.reshape(rows, ATTN_OUT), db


def _group_weights(lses):
    mx = jnp.maximum(jnp.maximum(lses[0], lses[1]), lses[2])
    es = [jnp.exp(l - mx) for l in lses]
    den = es[0] + es[1] + es[2]
    return [e / den for e in es]


def _combine_fwd(name, os_, lses):
    def fn(o0, o1, o2, l0, l1, l2):
        ws = _group_weights([l0, l1, l2])
        out = ws[0] * o0 + ws[1] * o1 + ws[2] * o2
        return out, out

    return _ew(name, fn, [*os_, *lses], [ATTN_OUT, ATTN_OUT], [F32, BF16], tm=1024)


def _combine_bwd(name, do, oa, lses):
    def fn(dov, oav, l0, l1, l2):
        head_sum = (lax.broadcasted_iota(jnp.int32, (ATTN_OUT, ATTN_OUT), 0) // HEAD_DIM
                    == lax.broadcasted_iota(jnp.int32, (ATTN_OUT, ATTN_OUT), 1) // HEAD_DIM)
        ws = _group_weights([l0, l1, l2])
        prod = dov * oav
        hi = prod.astype(BF16)
        lo = (prod - hi.astype(F32)).astype(BF16)
        ones = jnp.where(head_sum, 1.0, 0.0).astype(BF16)
        bar = _dot_nn(hi, ones) + _dot_nn(lo, ones)
        return tuple(w * dov for w in ws) + tuple(-w * bar for w in ws)

    return _ew(name, fn, [do, oa, *lses], [ATTN_OUT] * 6, [BF16] * 3 + [F32] * 3, tm=1024)


def _ssm_disc(a_re, a_im, log_dt, b_re, b_im):
    dt = jnp.exp(log_dt)
    mag = jnp.exp(a_re * dt)
    ab_re = mag * jnp.cos(a_im * dt)
    ab_im = mag * jnp.sin(a_im * dt)
    den = a_re * a_re + a_im * a_im
    xr = ab_re - 1.0
    coef_re = (xr * a_re + ab_im * a_im) / den
    coef_im = (ab_im * a_re - xr * a_im) / den
    bb_re = coef_re[None] * b_re - coef_im[None] * b_im
    bb_im = coef_re[None] * b_im + coef_im[None] * b_re
    return ab_re, ab_im, bb_re, bb_im


def _ssm_params_fwd(name, a_re, a_im, log_dt, b_re, b_im):
    pows = jax.ShapeDtypeStruct((SCAN_STEPS,) + a_re.shape, F32)
    cgn = jax.ShapeDtypeStruct(b_re.shape, F32)

    def body(ar, ai, ld, br, bi, o_pr, o_pi, o_bbr, o_bbi):
        ab_re, ab_im, bb_re, bb_im = _ssm_disc(ar[...], ai[...], ld[...], br[...], bi[...])
        pr, pi = ab_re, ab_im
        for j in range(SCAN_STEPS):
            o_pr[j] = pr
            o_pi[j] = pi
            pr, pi = pr * ab_re - pi * ab_im, pr * ab_im + pi * ab_re
        o_bbr[...] = bb_re
        o_bbi[...] = bb_im

    vm = pl.BlockSpec(memory_space=pltpu.VMEM)
    return pl.pallas_call(body, name=name, in_specs=[vm] * 5, out_specs=[vm] * 4,
                          out_shape=[pows, pows, cgn, cgn])(a_re, a_im, log_dt, b_re, b_im)


def _ssm_params_bwd(name, a_re, a_im, log_dt, b_re, b_im, d_ab_re, d_ab_im, d_bb_re, d_bb_im):
    gn = jax.ShapeDtypeStruct(a_re.shape, F32)
    cgn = jax.ShapeDtypeStruct(b_re.shape, F32)

    def body(ar, ai, ld, br, bi, g0, g1, g2, g3, o_ar, o_ai, o_ld, o_br, o_bi):
        _, vjp = jax.vjp(_ssm_disc, ar[...], ai[...], ld[...], br[...], bi[...])
        outs = vjp((g0[...], g1[...], g2[...], g3[...]))
        for o_ref, o in zip((o_ar, o_ai, o_ld, o_br, o_bi), outs):
            o_ref[...] = o

    vm = pl.BlockSpec(memory_space=pltpu.VMEM)
    return pl.pallas_call(body, name=name, in_specs=[vm] * 9, out_specs=[vm] * 5,
                          out_shape=[gn, gn, jax.ShapeDtypeStruct(log_dt.shape, F32), cgn, cgn],
                          )(a_re, a_im, log_dt, b_re, b_im, d_ab_re, d_ab_im, d_bb_re, d_bb_im)


def _scan_block(s_ref, carry_ref, tmp_ref, pw_ref, reverse, sprev=None):
    nl = SSM_LANES
    halves = range(SCAN_COLS // SCAN_SUB)
    zero = jnp.zeros((SCAN_SUB, SCAN_LANES), F32)
    for half in (reversed(halves) if reverse else halves):
        sub_rows = pl.ds(half * SCAN_SUB, SCAN_SUB)
        for lc in range(nl // SCAN_LANES):
            re_l = pl.ds(lc * SCAN_LANES, SCAN_LANES)
            im_l = pl.ds(nl + lc * SCAN_LANES, SCAN_LANES)
            are, aim = pw_ref[0, :, re_l], pw_ref[0, :, im_l]

            def step_of(j):
                return SCAN_STEPS - 1 - j if reverse else j

            def pass1(j, st):
                sr, si = st
                jj = step_of(j)
                nr = are * sr - aim * si + s_ref[jj, sub_rows, re_l]
                ni = are * si + aim * sr + s_ref[jj, sub_rows, im_l]
                s_ref[jj, sub_rows, re_l] = nr
                s_ref[jj, sub_rows, im_l] = ni
                return nr, ni

            er, ei = lax.fori_loop(0, SCAN_STEPS, pass1, (zero, zero), unroll=2)
            tmp_ref[0:SCAN_SUB, re_l] = er
            tmp_ref[0:SCAN_SUB, im_l] = ei
            apr, api = pw_ref[SCAN_STEPS - 1, 0:1, re_l], pw_ref[SCAN_STEPS - 1, 0:1, im_l]
            sr, si = carry_ref[0:1, re_l], carry_ref[0:1, im_l]
            for step in range(SCAN_SUB):
                c = SCAN_SUB - 1 - step if reverse else step
                tmp_ref[SCAN_SUB + c:SCAN_SUB + c + 1, re_l] = sr
                tmp_ref[SCAN_SUB + c:SCAN_SUB + c + 1, im_l] = si
                e_r, e_i = tmp_ref[c:c + 1, re_l], tmp_ref[c:c + 1, im_l]
                sr, si = apr * sr - api * si + e_r, apr * si + api * sr + e_i
            carry_ref[0:1, re_l] = sr
            carry_ref[0:1, im_l] = si
            cr = tmp_ref[SCAN_SUB:2 * SCAN_SUB, re_l]
            ci = tmp_ref[SCAN_SUB:2 * SCAN_SUB, im_l]

            if sprev is None:
                def pass2(j, st):
                    pr, pi = pw_ref[j, :, re_l], pw_ref[j, :, im_l]
                    jj = step_of(j)
                    s_ref[jj, sub_rows, re_l] += pr * cr - pi * ci
                    s_ref[jj, sub_rows, im_l] += pr * ci + pi * cr
                    return st

                lax.fori_loop(0, SCAN_STEPS, pass2, 0, unroll=2)
            else:
                st_ref, prev_ref, have_prev, dab_ref = sprev

                def corrected(jj, pr, pi):
                    gr = s_ref[jj, sub_rows, re_l] + pr * cr - pi * ci
                    gi = s_ref[jj, sub_rows, im_l] + pr * ci + pi * cr
                    s_ref[jj, sub_rows, re_l] = gr
                    s_ref[jj, sub_rows, im_l] = gi
                    return gr, gi

                def pass2(j, st):
                    dr, di = st
                    jj = SCAN_STEPS - 1 - j
                    gr, gi = corrected(jj, pw_ref[j, :, re_l], pw_ref[j, :, im_l])
                    qr, qi = st_ref[jj - 1, sub_rows, re_l], st_ref[jj - 1, sub_rows, im_l]
                    return dr + gr * qr + gi * qi, di + gi * qr - gr * qi

                dr, di = lax.fori_loop(0, SCAN_STEPS - 1, pass2, (zero, zero), unroll=2)
                gr, gi = corrected(0, pw_ref[SCAN_STEPS - 1, :, re_l], pw_ref[SCAN_STEPS - 1, :, im_l])
                sub = lax.broadcasted_iota(jnp.int32, (SCAN_SUB, SCAN_LANES), 0)
                if half == 0:
                    pv_r = prev_ref[SCAN_SUB - 1:SCAN_SUB, re_l] * have_prev
                    pv_i = prev_ref[SCAN_SUB - 1:SCAN_SUB, im_l] * have_prev
                else:
                    before = pl.ds(half * SCAN_SUB - 1, 1)
                    pv_r, pv_i = st_ref[SCAN_STEPS - 1, before, re_l], st_ref[SCAN_STEPS - 1, before, im_l]
                shape = (SCAN_SUB, SCAN_LANES)
                qr = jnp.where(sub == 0, jnp.broadcast_to(pv_r, shape),
                               pltpu.roll(st_ref[SCAN_STEPS - 1, sub_rows, re_l], 1, 0))
                qi = jnp.where(sub == 0, jnp.broadcast_to(pv_i, shape),
                               pltpu.roll(st_ref[SCAN_STEPS - 1, sub_rows, im_l], 1, 0))
                dab_ref[:, re_l] += dr + gr * qr + gi * qi
                dab_ref[:, im_l] += di + gi * qr - gr * qi


def _scan_view(a):
    return a.reshape(16, a.shape[0] // 16, a.shape[1])


def _pair_tile(p):
    start = (p * 2 * SSM_GROUP // PAIR_TILE) * PAIR_TILE
    return slice(start, start + PAIR_TILE)


def _pair_lanes(p):
    return pl.ds(p * PAIR_LANES, PAIR_LANES), pl.ds(SSM_LANES + p * PAIR_LANES, PAIR_LANES)


def _pair_store(s_ref, p, val):
    re_l, im_l = _pair_lanes(p)
    s_ref[:, :, re_l] = val[:, :PAIR_LANES].reshape(16, SCAN_COLS, PAIR_LANES)
    s_ref[:, :, im_l] = val[:, PAIR_LANES:].reshape(16, SCAN_COLS, PAIR_LANES)


def _pair_load(s_ref, p):
    re_l, im_l = _pair_lanes(p)
    parts = [s_ref[:, :, l].reshape(SCAN_BLOCK, PAIR_LANES) for l in (re_l, im_l)]
    return jnp.concatenate(parts, axis=1).astype(BF16)


def _pair_sum(fn):
    per = PAIR_TILE // (2 * SSM_GROUP)
    tiles = []
    for t in range(SSM_PAIRS // per):
        acc = None
        for p in range(t * per, (t + 1) * per):
            part = fn(p)
            acc = part if acc is None else acc + part
        tiles.append(acc)
    return jnp.concatenate(tiles, axis=1)


def _ssm_fwd(name, u, bb_mats, c_mats, pw_rows, d_skip):
    rows = u.shape[0]
    nl2 = 2 * SSM_LANES
    nblk = rows // SCAN_BLOCK

    def body(u_ref, bb_ref, c_ref, pw_ref, d_ref, y_ref, yg_ref, s_ref, carry_ref, tmp_ref):
        @pl.when(pl.program_id(0) == 0)
        def _():
            carry_ref[...] = jnp.zeros_like(carry_ref)

        uv = u_ref[...].reshape(SCAN_BLOCK, SSM_WIDTH)
        ub = uv.astype(BF16)
        for p in range(SSM_PAIRS):
            _pair_store(s_ref, p, _dot_nn(ub[:, _pair_tile(p)], bb_ref[p]))
        _scan_block(s_ref, carry_ref, tmp_ref, pw_ref, reverse=False)
        ys = _pair_sum(lambda p: _dot_nt(_pair_load(s_ref, p), c_ref[p]))
        yv = ys + d_ref[...] * uv
        y_ref[...] = yv.reshape(16, SCAN_COLS, SSM_WIDTH)
        yg_ref[...] = jax.nn.gelu(yv).astype(BF16).reshape(16, SCAN_COLS, SSM_WIDTH)

    const = lambda shape: pl.BlockSpec(shape, lambda i: (0,) * len(shape))
    blk = lambda cols: pl.BlockSpec((16, SCAN_COLS, cols), lambda i: (0, i, 0))
    pair_mats = const((SSM_PAIRS, PAIR_TILE, PAIR_TILE))
    y, yg, s = pl.pallas_call(
        body, name=name, grid=(nblk,),
        in_specs=[blk(SSM_WIDTH), pair_mats, pair_mats, const((SCAN_STEPS, SCAN_SUB, nl2)), const((1, SSM_WIDTH))],
        out_specs=[blk(SSM_WIDTH), blk(SSM_WIDTH), blk(nl2)],
        out_shape=[jax.ShapeDtypeStruct((16, rows // 16, SSM_WIDTH), F32),
                   jax.ShapeDtypeStruct((16, rows // 16, SSM_WIDTH), BF16),
                   jax.ShapeDtypeStruct((16, rows // 16, nl2), F32)],
        scratch_shapes=[pltpu.VMEM((SCAN_SUB, nl2), F32), pltpu.VMEM((2 * SCAN_SUB, nl2), F32)],
        compiler_params=_cparams("arbitrary"),
    )(_scan_view(u), bb_mats, c_mats, pw_rows, d_skip)
    return y.reshape(rows, SSM_WIDTH), yg.reshape(rows, SSM_WIDTH), s.reshape(rows, nl2)


def _ssm_bwd(name, dy, u, states, bb_mats, c_mats, pwc_rows, d_skip):
    rows = u.shape[0]
    nl2 = 2 * SSM_LANES
    nblk = rows // SCAN_BLOCK

    def body(dy_ref, u_ref, st_ref, prev_ref, bb_ref, c_ref, pw_ref, d_ref,
             du_ref, dbb_ref, dc_ref, dab_ref, dd_ref, g_ref, carry_ref, tmp_ref):
        i = pl.program_id(0)

        @pl.when(i == 0)
        def _():
            carry_ref[...] = jnp.zeros_like(carry_ref)
            for ref in (dbb_ref, dc_ref, dab_ref, dd_ref):
                ref[...] = jnp.zeros_like(ref)

        dyv = dy_ref[...].reshape(SCAN_BLOCK, SSM_WIDTH)
        uv = u_ref[...].reshape(SCAN_BLOCK, SSM_WIDTH)
        dyb, ub = dyv.astype(BF16), uv.astype(BF16)
        for p in range(SSM_PAIRS):
            _pair_store(g_ref, p, _dot_nn(dyb[:, _pair_tile(p)], c_ref[p]))
        have_prev = (i < nblk - 1).astype(F32)
        _scan_block(g_ref, carry_ref, tmp_ref, pw_ref, reverse=True,
                    sprev=(st_ref, prev_ref, have_prev, dab_ref))

        def pair_work(p):
            gp = _pair_load(g_ref, p)
            dbb_ref[p] += _dot_tn(ub[:, _pair_tile(p)], gp)
            dc_ref[p] += _dot_tn(dyb[:, _pair_tile(p)], _pair_load(st_ref, p))
            return _dot_nt(gp, bb_ref[p])

        du_ref[...] = (_pair_sum(pair_work) + d_ref[...] * dyv).reshape(16, SCAN_COLS, SSM_WIDTH)
        dd_ref[...] += jnp.sum(dyv * uv, axis=0, keepdims=True)

    const = lambda shape: pl.BlockSpec(shape, lambda i: (0,) * len(shape))
    blk = lambda cols: pl.BlockSpec((16, SCAN_COLS, cols), lambda i: (0, nblk - 1 - i, 0))
    per8 = SCAN_COLS // SCAN_SUB
    prev_spec = pl.BlockSpec((None, SCAN_SUB, nl2), lambda i: (15, jnp.maximum((nblk - 1 - i) * per8 - 1, 0), 0))
    pair_mats = const((SSM_PAIRS, PAIR_TILE, PAIR_TILE))
    pair_shape = jax.ShapeDtypeStruct((SSM_PAIRS, PAIR_TILE, PAIR_TILE), F32)
    sv = _scan_view(states)
    du, dbb, dc, dab, dd = pl.pallas_call(
        body, name=name, grid=(nblk,),
        in_specs=[blk(SSM_WIDTH), blk(SSM_WIDTH), blk(nl2), prev_spec, pair_mats, pair_mats,
                  const((SCAN_STEPS, SCAN_SUB, nl2)), const((1, SSM_WIDTH))],
        out_specs=[blk(SSM_WIDTH), pair_mats, pair_mats, const((SCAN_SUB, nl2)), const((1, SSM_WIDTH))],
        out_shape=[jax.ShapeDtypeStruct((16, rows // 16, SSM_WIDTH), F32), pair_shape, pair_shape,
                   jax.ShapeDtypeStruct((SCAN_SUB, nl2), F32), jax.ShapeDtypeStruct((1, SSM_WIDTH), F32)],
        scratch_shapes=[pltpu.VMEM((16, SCAN_COLS, nl2), F32), pltpu.VMEM((SCAN_SUB, nl2), F32),
                        pltpu.VMEM((2 * SCAN_SUB, nl2), F32)],
        compiler_params=_cparams("arbitrary"),
    )(_scan_view(dy), _scan_view(u), sv, sv, bb_mats, c_mats, pwc_rows, d_skip)
    return du.reshape(rows, SSM_WIDTH), dbb, dc, dab, dd


def _adamw(name, w, m, v, gparts, tr):
    rows, cols = w.shape

    def body(w_ref, m_ref, v_ref, g_ref, og_ref, od_ref, om_ref, ov_ref):
        g = g_ref[0].astype(F32)
        for i in range(1, N_DEV):
            g = g + g_ref[i].astype(F32)
        m_new = B1 * m_ref[...] + (1.0 - B1) * g
        v_new = B2 * v_ref[...] + (1.0 - B2) * (g * g)
        m_hat = m_new / (1.0 - B1 ** STEP)
        v_hat = v_new / (1.0 - B2 ** STEP)
        og_ref[...] = g
        od_ref[...] = -LR * (m_hat / (jnp.sqrt(v_hat) + ADAM_EPS) + WD * w_ref[...])
        om_ref[...] = m_new
        ov_ref[...] = v_new

    spec = pl.BlockSpec((tr, cols), lambda i: (i, 0))
    shape = jax.ShapeDtypeStruct((rows, cols), F32)
    return pl.pallas_call(
        body, name=name, grid=(rows // tr,),
        in_specs=[spec, spec, spec, pl.BlockSpec((N_DEV, tr, cols), lambda i: (0, i, 0))],
        out_specs=[spec] * 4, out_shape=[shape] * 4,
        compiler_params=_cparams("parallel"),
    )(w, m, v, gparts)


_SHARDED = (
    ("ffn1_w_gate", True, (352, 1024)), ("ffn1_w_up", True, (352, 1024)), ("ffn1_w_down", False, (352, 1024)),
    ("w_in", True, (608, 1024)), ("ssm_w_glu", True, (128, 512)), ("w_attn_branch", True, (128, 256)),
    ("w_ssm_branch", True, (128, 512)), ("w_out", False, (128, 1024)),
    ("ffn2_w_gate", True, (352, 1024)), ("ffn2_w_up", True, (352, 1024)), ("ffn2_w_down", False, (352, 1024)),
)
_SMALL = ("ffn1_norm", "mix_norm", "gate_bias", "rel_bias_table", "ssm_a_re", "ssm_a_im", "ssm_log_dt",
          "ssm_b_re", "ssm_b_im", "ssm_c_re", "ssm_c_im", "ssm_d", "ffn2_norm", "final_norm")
_ORDER = ("ffn1_norm", "ffn1_w_gate", "ffn1_w_up", "ffn1_w_down", "mix_norm", "w_in", "gate_bias",
          "rel_bias_table", "ssm_a_re", "ssm_a_im", "ssm_log_dt", "ssm_b_re", "ssm_b_im", "ssm_c_re",
          "ssm_c_im", "ssm_d", "ssm_w_glu", "w_attn_branch", "w_ssm_branch", "w_out", "ffn2_norm",
          "ffn2_w_gate", "ffn2_w_up", "ffn2_w_down", "final_norm")


def _pack_rows(shape):
    return shape[0] * shape[1] // D_MODEL


_SHARD_INFO = {nm: (tr, shape) for nm, tr, shape in _SHARDED}
_PHASES = {
    "f1gu": ("ffn1_w_gate", "ffn1_w_up"), "f1d": ("ffn1_w_down",),
    "mix": ("w_in", "ssm_w_glu", "w_attn_branch", "w_ssm_branch", "w_out"),
    "f2": ("ffn2_w_gate", "ffn2_w_up", "ffn2_w_down"),
}


def _to_rows(a, nm):
    tr, shape = _SHARD_INFO[nm]
    return (a.T if tr else a).reshape(_pack_rows(shape), D_MODEL)


def _from_rows(p, nm):
    tr, shape = _SHARD_INFO[nm]
    a = p.reshape(shape)
    return a.T if tr else a


def _full_weight(gathered, nm):
    _, shape = _SHARD_INFO[nm]
    return gathered.reshape(N_DEV * shape[0], shape[1])


def _grad_blocks(g, nm):
    _, shape = _SHARD_INFO[nm]
    return g.astype(BF16).reshape(N_DEV, _pack_rows(shape), D_MODEL)


_SMALL_TILE = 8 * 128


def _small_rows(a):
    flat = a.reshape(-1)
    return jnp.pad(flat, (0, (-flat.shape[0]) % _SMALL_TILE)).reshape(-1, 128)


def _pack_small(ws, last=None):
    tail = jnp.zeros((), F32) if last is None else last
    return jnp.concatenate([_small_rows(ws[nm]) for nm in _SMALL] + [_small_rows(tail)], axis=0)


def _unpack_small(pack, like):
    out, r0 = {}, 0
    for nm in _SMALL:
        n = like[nm].size
        nr = 8 * -(-n // _SMALL_TILE)
        out[nm] = pack[r0:r0 + nr].reshape(-1)[:n].reshape(like[nm].shape)
        r0 += nr
    return out


def _residue_order(a):
    rows, cols = a.shape
    return a.reshape(rows // 16, 16, cols).transpose(1, 0, 2).reshape(rows, cols)


def _token_order(a):
    rows, cols = a.shape
    return a.reshape(16, rows // 16, cols).transpose(1, 0, 2).reshape(rows, cols)


_PAIRS_PER_TILE = PAIR_TILE // (2 * SSM_GROUP)
_PAIR_AXES = (SSM_PAIRS // _PAIRS_PER_TILE, _PAIRS_PER_TILE, 2)


def _pair_matrices(re, im):
    six = jnp.stack([re, im]).reshape((2,) + _PAIR_AXES + (SSM_GROUP, SSM_STATE))
    eye_j, eye_l = jnp.eye(_PAIRS_PER_TILE, dtype=re.dtype), jnp.eye(2, dtype=re.dtype)
    mats = jnp.einsum("xkjlcn,jJ,lL->kjJLcxln", six, eye_j, eye_l)
    return mats.reshape(SSM_PAIRS, PAIR_TILE, PAIR_TILE).astype(BF16)


def _pair_diagonals(acc):
    k, j, l = _PAIR_AXES
    eight = acc.reshape(k, j, j, l, SSM_GROUP, 2, l, SSM_STATE)
    eye_j, eye_l = jnp.eye(j, dtype=acc.dtype), jnp.eye(l, dtype=acc.dtype)
    own = jnp.einsum("kjJLcxln,jJ,lL->xkjlcn", eight, eye_j, eye_l).reshape(2, SSM_GROUPS, SSM_GROUP, SSM_STATE)
    return own[0], own[1]


def _local_step(xs, target, small, weights_of, send_grads, first_deps=()):
    rows = xs.shape[0]
    gfull, gsmall = {}, {}
    wf = dict(weights_of("f1", None))

    x1, h1, gg1, uu1, hmix = _ffn_fwd("ffn1_fwd", xs, small["ffn1_norm"], wf["ffn1_w_gate"], wf["ffn1_w_up"],
                                      wf["ffn1_w_down"], small["mix_norm"], deps=first_deps)
    wf.update(weights_of("mix", x1))
    w_in = wf["w_in"]
    w_qkv, w_u, w_g = w_in[:3 * ATTN_WIDTH], w_in[3 * ATTN_WIDTH:3 * ATTN_WIDTH + SSM_WIDTH], w_in[3 * ATTN_WIDTH + SSM_WIDTH:]
    qscale = jnp.concatenate([jnp.full((1, ATTN_WIDTH), HEAD_DIM ** -0.5, F32), jnp.ones((1, 2 * ATTN_WIDTH), F32)], axis=1)
    qkv, = _mm("in_qkv", [(hmix, w_qkv)], True, 3 * ATTN_WIDTH, [BF16],
               epilogue=lambda acc, sc: (acc * sc,), extras=[(qscale, 0)], tn=ATTN_WIDTH)
    u, = _mm("in_u", [(hmix, w_u)], True, SSM_WIDTH, [F32])
    gates, = _mm("in_gates", [(hmix, w_g)], True, 2 * D_MODEL, [F32],
                 epilogue=lambda acc, b: (_sigmoid(acc + b),), extras=[(small["gate_bias"], 0)])

    table_t = small["rel_bias_table"].T
    tables, bias4, o_g, lse_g = [], [], [], []
    for g in range(N_GROUPS):
        bucket, valid = [jnp.asarray(t) for t in _attn_tables(g, rows)]
        bias_g = _bias_fwd(f"rel_bias_fwd_{g}", bucket, valid, table_t[g * HEADS_PER_GROUP:(g + 1) * HEADS_PER_GROUP])
        tables.append(bucket)
        bias4.append(bias_g.reshape(-1, bias_g.shape[-1]))
        o, lse = _attn_fwd(f"attn_fwd_{g}", qkv, g, bias4[g])
        o_g.append(o)
        lse_g.append(lse)
    oa_f32, oa = _combine_fwd("attn_combine_fwd", o_g, lse_g)
    y_attn, = _mm("attn_branch", [(oa, wf["w_attn_branch"])], True, D_MODEL, [F32])

    pw_re, pw_im, bb_re, bb_im = _ssm_params_fwd(
        "ssm_params_fwd", small["ssm_a_re"], small["ssm_a_im"], small["ssm_log_dt"].reshape(SSM_GROUPS, 1),
        small["ssm_b_re"].transpose(2, 0, 1), small["ssm_b_im"].transpose(2, 0, 1))

    def power_rows(sign):
        row = jnp.concatenate([pw_re.reshape(SCAN_STEPS, 1, SSM_LANES), sign * pw_im.reshape(SCAN_STEPS, 1, SSM_LANES)],
                              axis=2)
        return jnp.broadcast_to(row, (SCAN_STEPS, SCAN_SUB, 2 * SSM_LANES))

    bb_mats = _pair_matrices(bb_re.transpose(1, 0, 2), bb_im.transpose(1, 0, 2))
    c_mats = _pair_matrices(small["ssm_c_re"], -small["ssm_c_im"])
    d_skip = small["ssm_d"].reshape(1, SSM_WIDTH)
    y_raw, ygelu, states = _ssm_fwd("ssm_fwd", u, bb_mats, c_mats, power_rows(1.0), d_skip)
    glu, ysg = _mm("ssm_glu", [(ygelu, wf["ssm_w_glu"])], True, 2 * SSM_WIDTH, [F32, BF16],
                   epilogue=lambda gv: (gv, gv[:, :SSM_WIDTH] * _sigmoid(gv[:, SSM_WIDTH:])),
                   tn=2 * SSM_WIDTH, out_cols=[2 * SSM_WIDTH, SSM_WIDTH])
    y_ssm, merged = _mm("ssm_branch_merge", [(ysg, wf["w_ssm_branch"])], True, D_MODEL, [F32, BF16],
                        epilogue=lambda acc, ga, gs, ya: (acc, ga * ya + gs * acc),
                        extras=[(gates, 0), (gates, D_MODEL), (y_attn, 0)])
    x2, = _mm("mix_out", [(merged, wf["w_out"])], False, D_MODEL, [F32],
              epilogue=lambda acc, res: (res + acc,), extras=[(x1, 0)])
    wf.update(weights_of("f2", x2))
    dx3, h2, gg2, uu2, gsmall["final_norm"], gsmall["loss"] = _ffn_fwd_head(
        "ffn2_fwd", x2, small["ffn2_norm"], wf["ffn2_w_gate"], wf["ffn2_w_up"], wf["ffn2_w_down"],
        small["final_norm"].reshape(1, D_MODEL), target)

    dx2, dgg2, duu2, act2, gsmall["ffn2_norm"] = _ffn_bwd(
        "ffn2_bwd", dx3, x2, small["ffn2_norm"], gg2, uu2, wf["ffn2_w_gate"], wf["ffn2_w_up"], wf["ffn2_w_down"])
    gfull["ffn2_w_gate"] = _mm_tn("ffn2_dwg", dgg2, h2, out_dtype=BF16)
    gfull["ffn2_w_up"] = _mm_tn("ffn2_dwu", duu2, h2, out_dtype=BF16)
    gfull["ffn2_w_down"] = _mm_tn("ffn2_dwd", act2, dx3, scale=0.5, out_dtype=BF16)
    sent = send_grads("f2", gfull)

    def merge_bwd(dm, ga, gs, ya, ys):
        dza, dzs = dm * ya * ga * (1.0 - ga), dm * ys * gs * (1.0 - gs)
        return (dm * ga, dm * gs, dza, dzs, jnp.sum(dza, axis=0, keepdims=True), jnp.sum(dzs, axis=0, keepdims=True))

    dya, dys, dzga, dzgs, dba, dbs = _mm(
        "mix_out_bwd", [(dx2, wf["w_out"])], True, D_MODEL, [BF16] * 4, epilogue=merge_bwd, row_sums=2,
        extras=[(gates, 0), (gates, D_MODEL), (y_attn, 0), (y_ssm, 0)], deps=sent, tm=512, tn=D_MODEL)
    gfull["w_out"] = _mm_tn("dw_out", merged, dx2, out_dtype=BF16)
    gsmall["gate_bias"] = jnp.concatenate([dba, dbs], axis=1)

    gfull["w_ssm_branch"] = _mm_tn("dw_ssm_branch", dys, ysg, out_dtype=BF16)

    def glu_bwd(dysg, av, bv):
        sb = _sigmoid(bv)
        return (dysg * sb, dysg * av * sb * (1.0 - sb))

    dglu_a, dglu_b = _mm("ssm_branch_bwd", [(dys, wf["w_ssm_branch"])], False, SSM_WIDTH, [BF16, BF16],
                         epilogue=glu_bwd, extras=[(glu, 0), (glu, SSM_WIDTH)])
    w_glu = wf["ssm_w_glu"]
    gfull["ssm_w_glu"] = _mm_tn_stack("dw_glu", [dglu_a, dglu_b], ygelu, out_dtype=BF16)

    def gelu_bwd(acc, yv):
        _, vjp = jax.vjp(jax.nn.gelu, yv)
        return (vjp(acc)[0],)

    dy_raw, = _mm("ssm_glu_bwd", [(dglu_a, w_glu[:SSM_WIDTH]), (dglu_b, w_glu[SSM_WIDTH:])], False, SSM_WIDTH, [F32],
                  epilogue=gelu_bwd, extras=[(y_raw, 0)])
    du, dbb_acc, dc_acc, dab_rows, gsmall_d = _ssm_bwd(
        "ssm_bwd", dy_raw, u, states, bb_mats, c_mats, power_rows(-1.0), d_skip)
    gsmall["ssm_d"] = gsmall_d
    dbb_re, dbb_im = [a.transpose(1, 0, 2) for a in _pair_diagonals(dbb_acc)]
    dc_re, dc_im = _pair_diagonals(dc_acc)
    gsmall["ssm_c_re"], gsmall["ssm_c_im"] = dc_re, -dc_im
    dab = _colsum("ssm_dab", dab_rows)
    d_ar, d_ai, d_ld, d_br, d_bi = _ssm_params_bwd(
        "ssm_params_bwd", small["ssm_a_re"], small["ssm_a_im"], small["ssm_log_dt"].reshape(SSM_GROUPS, 1),
        small["ssm_b_re"].transpose(2, 0, 1), small["ssm_b_im"].transpose(2, 0, 1),
        dab[:, :SSM_LANES].reshape(SSM_GROUPS, SSM_STATE), dab[:, SSM_LANES:].reshape(SSM_GROUPS, SSM_STATE),
        dbb_re, dbb_im)
    gsmall["ssm_a_re"], gsmall["ssm_a_im"], gsmall["ssm_log_dt"] = d_ar, d_ai, d_ld.reshape(SSM_GROUPS)
    gsmall["ssm_b_re"], gsmall["ssm_b_im"] = d_br.transpose(1, 2, 0), d_bi.transpose(1, 2, 0)

    gfull["w_attn_branch"] = _mm_tn("dw_attn_branch", dya, oa, out_dtype=BF16)
    doa, = _mm("attn_branch_bwd", [(dya, wf["w_attn_branch"])], False, ATTN_OUT, [F32])
    dc = _combine_bwd("attn_combine_bwd", doa, oa_f32, lse_g)
    dqkv_cols = [None] * 9
    dtable = []
    for g in range(N_GROUPS):
        dq, dk, dv, db = _attn_bwd(f"attn_bwd_{g}", qkv, dc[g], lse_g[g], dc[3 + g], g, bias4[g])
        dqkv_cols[g], dqkv_cols[3 + g], dqkv_cols[6 + g] = dq, dk, dv
        dt = _bias_bwd(f"rel_bias_bwd_{g}", tables[g], db.reshape(HEADS_PER_GROUP, -1, db.shape[-1]))
        dtable.append(dt[:, :HEADS_PER_GROUP])
    gsmall["rel_bias_table"] = jnp.concatenate(dtable, axis=1)

    gfull["w_in"] = jnp.concatenate([_mm_tn_stack("dw_in_qkv", dqkv_cols, hmix, out_dtype=BF16),
                                     _mm_tn_stack("dw_in_rest", [du, dzga, dzgs], hmix, out_dtype=BF16)], axis=0)
    sent = send_grads("mix", gfull)
    qkv_pairs = [(c, w_qkv[i * ATTN_OUT:(i + 1) * ATTN_OUT]) for i, c in enumerate(dqkv_cols)]

    def mix_norm_bwd(dh, xv, gain, dres):
        r, xh = _rms_parts(xv)
        return dres + _rms_bwd_dx(dh, gain, r, xh), jnp.sum(dh * xh, axis=0, keepdims=True)

    dx1, gsmall["mix_norm"] = _mm(
        "in_bwd", qkv_pairs + [(du, w_u), (dzga, w_g[:D_MODEL]), (dzgs, w_g[D_MODEL:])], False, D_MODEL, [F32],
        epilogue=mix_norm_bwd, row_sums=1, extras=[(x1, 0), (small["mix_norm"], 0), (dx2, 0)], tm=512, tn=D_MODEL,
        deps=sent)

    dx, dgg1, duu1, act1, gsmall["ffn1_norm"] = _ffn_bwd(
        "ffn1_bwd", dx1, xs, small["ffn1_norm"], gg1, uu1, wf["ffn1_w_gate"], wf["ffn1_w_up"], wf["ffn1_w_down"])
    sent = send_grads("small", gsmall)
    gfull["ffn1_w_gate"] = _mm_tn("ffn1_dwg", dgg1, h1, deps=sent, out_dtype=BF16)
    gfull["ffn1_w_up"] = _mm_tn("ffn1_dwu", duu1, h1, out_dtype=BF16)
    sent = send_grads("f1gu", gfull)
    gfull["ffn1_w_down"] = _mm_tn("ffn1_dwd", act1, dx1, scale=0.5, deps=sent, out_dtype=BF16)
    send_grads("f1d", gfull)
    return dx, gsmall


def kernel(x, ffn1_norm, ffn1_w_gate, ffn1_w_up, ffn1_w_down, mix_norm, w_in, gate_bias, rel_bias_table, ssm_a_re, ssm_a_im, ssm_log_dt, ssm_b_re, ssm_b_im, ssm_c_re, ssm_c_im, ssm_d, ssm_w_glu, w_attn_branch, w_ssm_branch, w_out, ffn2_norm, ffn2_w_gate, ffn2_w_up, ffn2_w_down, final_norm, loss_target, m_ffn1_norm, m_ffn1_w_gate, m_ffn1_w_up, m_ffn1_w_down, m_mix_norm, m_w_in, m_gate_bias, m_rel_bias_table, m_ssm_a_re, m_ssm_a_im, m_ssm_log_dt, m_ssm_b_re, m_ssm_b_im, m_ssm_c_re, m_ssm_c_im, m_ssm_d, m_ssm_w_glu, m_w_attn_branch, m_w_ssm_branch, m_w_out, m_ffn2_norm, m_ffn2_w_gate, m_ffn2_w_up, m_ffn2_w_down, m_final_norm, v_ffn1_norm, v_ffn1_w_gate, v_ffn1_w_up, v_ffn1_w_down, v_mix_norm, v_w_in, v_gate_bias, v_rel_bias_table, v_ssm_a_re, v_ssm_a_im, v_ssm_log_dt, v_ssm_b_re, v_ssm_b_im, v_ssm_c_re, v_ssm_c_im, v_ssm_d, v_ssm_w_glu, v_w_attn_branch, v_w_ssm_branch, v_w_out, v_ffn2_norm, v_ffn2_w_gate, v_ffn2_w_up, v_ffn2_w_down, v_final_norm):
    given = dict(locals())
    shapes = {nm: given[nm].shape for nm in _ORDER}

    def strip(a):
        return a[0] if a.ndim >= 2 and a.shape[0] == 1 else a

    w = {nm: strip(given[nm]) for nm in _ORDER}
    m = {nm: strip(given["m_" + nm]) for nm in _ORDER}
    v = {nm: strip(given["v_" + nm]) for nm in _ORDER}
    for d in (w, m, v):
        d["rel_bias_table"] = d["rel_bias_table"].reshape(N_BUCKETS, N_GROUPS * HEADS_PER_GROUP)

    small = {nm: w[nm] for nm in _SMALL}
    small_in = dict(small)
    for nm in ("ffn1_norm", "mix_norm", "ffn2_norm", "gate_bias"):
        small_in[nm] = small[nm].reshape(1, -1)
    w_rows = {nm: _to_rows(w[nm], nm) for nm in _SHARD_INFO}

    def bf16_rows(phase):
        return [w_rows[nm].astype(BF16) for nm in _PHASES[phase]]

    f1_names = _PHASES["f1gu"] + _PHASES["f1d"]
    got_f1 = _all_gather("gather_f1", bf16_rows("f1gu") + bf16_rows("f1d"))
    pending_w = {"mix": _exchange_start("gather_mix_start", bf16_rows("mix"), gather=True, deps=[got_f1[0]])}
    pending_w["f2"] = _exchange_start("gather_f2_start", bf16_rows("f2"), gather=True, deps=[pending_w["mix"][4]])

    def weights_of(phase, after):
        if phase == "f1":
            return {nm: _full_weight(got, nm) for nm, got in zip(f1_names, got_f1)}
        landed = _exchange_wait(f"gather_{phase}_wait", pending_w[phase], after, gather=True)
        return {nm: _full_weight(got, nm) for nm, got in zip(_PHASES[phase], landed)}

    pending_g = {}

    def send_grads(phase, grads):
        if phase == "small":
            gs_pack = _pack_small({nm: grads[nm].reshape(small[nm].shape) for nm in _SMALL}, last=grads["loss"])
            pending_g[phase] = _exchange_start("gather_small_start", [gs_pack], gather=True)
        else:
            pending_g[phase] = _exchange_start(f"scatter_{phase}_start",
                                               [_grad_blocks(grads[nm], nm) for nm in _PHASES[phase]], gather=False)
        return [pending_g[phase][4]]

    dx, gsmall = _local_step(_residue_order(x[0]), _residue_order(loss_target[0]), small_in, weights_of,
                             send_grads, first_deps=[pending_w["f2"][4]])
    dx = _token_order(dx)

    updated = {}
    after = pending_g["f1d"][4]
    for phase in ("f2", "mix", "small", "f1gu", "f1d"):
        landed = _exchange_wait(f"exchange_{phase}_wait", pending_g[phase], after, gather=phase == "small")
        if phase == "small":
            sm = _adamw("adamw_small", _pack_small(small), _pack_small({nm: m[nm] for nm in _SMALL}),
                        _pack_small({nm: v[nm] for nm in _SMALL}), landed[0], landed[0].shape[1])
            after = sm[0]
            continue
        for nm, recv in zip(_PHASES[phase], landed):
            tr = max(t for t in range(16, 353, 16) if w_rows[nm].shape[0] % t == 0)
            updated[nm] = _adamw(f"adamw_{nm}", w_rows[nm], _to_rows(m[nm], nm), _to_rows(v[nm], nm), recv, tr)
            after = updated[nm][0]

    loss = sm[0][-8, 0]
    outs = []
    for i in range(4):
        sml = _unpack_small(sm[i], small)
        outs.append([(_from_rows(updated[nm][i], nm) if nm in updated else sml[nm]).reshape(shapes[nm])
                     for nm in _ORDER])
    return (loss, dx[None], *outs[0], *outs[1], *outs[2], *outs[3])
```

```python
import math

import numpy as np
import jax
import jax.numpy as jnp
from jax import lax
from jax.experimental import pallas as pl
from jax.experimental.pallas import tpu as pltpu

F32 = jnp.float32
BF16 = jnp.bfloat16

N_DEV = 8
D_MODEL = 1024
D_FF = 2816
HEAD_DIM = 64
HEADS_PER_GROUP = 4
DILATIONS = (1, 4, 16)
N_GROUPS = 3
ATTN_WIDTH = 768
ATTN_OUT = 256
BLOCK = 128
N_BUCKETS = 32
MAX_DISTANCE = 2048
NEG_INF = -1e30
SSM_WIDTH = 512
SSM_GROUPS = 32
SSM_GROUP = 16
SSM_STATE = 64
SSM_LANES = SSM_GROUPS * SSM_STATE
SSM_PAIRS = SSM_GROUPS // 2
PAIR_LANES = 2 * SSM_STATE
PAIR_TILE = 256
EPS = 1e-6
LR, B1, B2, ADAM_EPS, WD, STEP = 0.001, 0.9, 0.999, 1e-08, 0.01, 10

VMEM_LIMIT_BYTES = 56 * 1024 * 1024
FFN_CHUNK = 768
SCAN_BLOCK = 256
SCAN_STEPS = 16
SCAN_COLS = SCAN_BLOCK // SCAN_STEPS
SCAN_SUB = 8
SCAN_LANES = 512

MESH = pl.DeviceIdType.MESH


def _cparams(*sem):
    return pltpu.CompilerParams(dimension_semantics=sem, vmem_limit_bytes=VMEM_LIMIT_BYTES)


def _dot(a, b, dims):
    return lax.dot_general(a, b, (dims, ((), ())), preferred_element_type=F32)


def _dot_nn(a, b):
    return _dot(a, b, ((1,), (0,)))


def _dot_nt(a, b):
    return _dot(a, b, ((1,), (1,)))


def _dot_tn(a, b):
    return _dot(a, b, ((0,), (0,)))


def _sigmoid(x):
    return 1.0 / (1.0 + jnp.exp(-x))


_HBM_SPEC = pl.BlockSpec(memory_space=pltpu.HBM)
_SEM_SPEC = pl.BlockSpec(memory_space=pltpu.SEMAPHORE)
_ANY_SPEC = pl.BlockSpec(memory_space=pl.ANY)
_EFFECT = pltpu.SideEffectType.DATAFLOW_SIDE_EFFECTING


def _peers(x, y, c):
    return [(1 - x if k & 4 else x, 1 - y if k & 2 else y, 1 - c if k & 1 else c) for k in range(1, N_DEV)]


def _exchange_copies(x_refs, land_refs, send_sems, recv_sems, gather):
    x, y, c = lax.axis_index("x"), lax.axis_index("y"), lax.axis_index("c")
    me = 4 * x + 2 * y + c
    copies = []
    for a, (x_ref, land_ref) in enumerate(zip(x_refs, land_refs)):
        for k, (px, py, pc) in enumerate(_peers(x, y, c)):
            src = x_ref if gather else x_ref.at[4 * px + 2 * py + pc]
            copies.append(pltpu.make_async_remote_copy(
                src_ref=src, dst_ref=land_ref.at[me], send_sem=send_sems.at[N_DEV * a + k],
                recv_sem=recv_sems.at[(N_DEV - 1) * a + k], device_id=(px, py, pc), device_id_type=MESH))
    owns = [pltpu.make_async_copy(x_ref if gather else x_ref.at[me], land_ref.at[me],
                                  send_sems.at[N_DEV * a + N_DEV - 1])
            for a, (x_ref, land_ref) in enumerate(zip(x_refs, land_refs))]
    return owns, copies


def _exchange_start(name, xs_list, gather, deps=()):
    n, nd = len(xs_list), len(deps)
    land_shapes = [(N_DEV, *xs.shape) if gather else xs.shape for xs in xs_list]

    def body(*refs):
        x_refs, land_refs = refs[:n], refs[n:2 * n]
        send_sems, recv_sems = refs[2 * n + nd:2 * n + nd + 2]
        token = refs[-1]
        owns, copies = _exchange_copies(x_refs, land_refs, send_sems, recv_sems, gather)
        for cp in copies + owns:
            cp.start()
        token[...] = jnp.zeros_like(token)

    hbm = lambda a: pltpu.with_memory_space_constraint(a, pltpu.HBM)
    outs = pl.pallas_call(
        body, name=name,
        out_shape=(pltpu.SemaphoreType.DMA((n * N_DEV,)), pltpu.SemaphoreType.DMA((n * (N_DEV - 1),)),
                   *[pltpu.HBM(xs.shape, xs.dtype) for xs in xs_list],
                   *[pltpu.HBM(shape, xs.dtype) for shape, xs in zip(land_shapes, xs_list)],
                   jax.ShapeDtypeStruct((8, 128), F32)),
        in_specs=(_HBM_SPEC,) * (2 * n) + (_ANY_SPEC,) * nd,
        out_specs=(_SEM_SPEC, _SEM_SPEC) + (_HBM_SPEC,) * (2 * n) + (pl.BlockSpec(memory_space=pltpu.VMEM),),
        input_output_aliases={i: 2 + i for i in range(2 * n)},
        compiler_params=pltpu.CompilerParams(has_side_effects=_EFFECT),
    )(*[hbm(xs) for xs in xs_list], *[hbm(lax.empty(shape, xs.dtype)) for shape, xs in zip(land_shapes, xs_list)],
      *deps)
    return outs[0], outs[1], list(outs[2:2 + n]), list(outs[2 + n:2 + 2 * n]), outs[-1]


def _exchange_wait(name, handle, after, gather):
    send_sems, recv_sems, xs_thru, lands_thru, _ = handle
    n = len(xs_thru)
    after = list(after) if isinstance(after, (list, tuple)) else [after]

    def body(*refs):
        x_refs, land_refs = refs[:n], refs[n:2 * n]
        send_sems, recv_sems = refs[2 * n:2 * n + 2]
        owns, copies = _exchange_copies(x_refs, land_refs, send_sems, recv_sems, gather)
        for cp in copies:
            cp.wait_send()
            cp.wait_recv()
        for cp in owns:
            cp.wait()

    outs = pl.pallas_call(
        body, name=name,
        out_shape=tuple(pltpu.HBM(a.shape, a.dtype) for a in xs_thru + lands_thru),
        in_specs=(_HBM_SPEC,) * (2 * n) + (_SEM_SPEC, _SEM_SPEC) + (_ANY_SPEC,) * len(after),
        out_specs=(_HBM_SPEC,) * (2 * n), input_output_aliases={i: i for i in range(2 * n)},
        compiler_params=pltpu.CompilerParams(has_side_effects=_EFFECT),
    )(*xs_thru, *lands_thru, send_sems, recv_sems, *after)
    return list(outs[n:])


def _mm(name, pairs, nt, n_cols, out_dtypes, epilogue=None, extras=(), tm=1024, tn=512, deps=(), row_sums=0,
        out_cols=None):
    rows = pairs[0][0].shape[0]
    tm = min(tm, rows)
    tn = min(tn, n_cols)
    na, ne, nd, no = len(pairs), len(extras), len(deps), len(out_dtypes)

    def body(*refs):
        a_refs, w_refs = refs[:na], refs[na:2 * na]
        e_refs, o_refs = refs[2 * na:2 * na + ne], refs[2 * na + ne + nd:]
        acc = None
        for a_ref, w_ref in zip(a_refs, w_refs):
            a = a_ref[...].astype(BF16)
            w = w_ref[...].astype(BF16)
            p = _dot_nt(a, w) if nt else _dot_nn(a, w)
            acc = p if acc is None else acc + p
        outs = (acc,) if epilogue is None else epilogue(acc, *[e[...] for e in e_refs])
        for o_ref, o in zip(o_refs[:no], outs[:no]):
            o_ref[...] = o.astype(o_ref.dtype)
        for r_ref, o in zip(o_refs[no:], outs[no:]):
            @pl.when(pl.program_id(0) == 0)
            def _():
                r_ref[...] = jnp.zeros_like(r_ref)

            r_ref[...] += o

    in_specs = [pl.BlockSpec((tm, a.shape[1]), lambda i, j: (i, 0)) for a, _ in pairs]
    for _, w in pairs:
        if nt:
            in_specs.append(pl.BlockSpec((tn, w.shape[1]), lambda i, j: (j, 0)))
        else:
            in_specs.append(pl.BlockSpec((w.shape[0], tn), lambda i, j: (0, j)))
    for e, col_off in extras:
        off = col_off // tn
        if e.shape[0] == 1:
            in_specs.append(pl.BlockSpec((1, tn), lambda i, j, off=off: (0, j + off)))
        else:
            in_specs.append(pl.BlockSpec((tm, tn), lambda i, j, off=off: (i, j + off)))
    in_specs += [_ANY_SPEC] * nd
    if out_cols is None:
        out_cols = [n_cols] * no
    else:
        assert tn == n_cols, "outputs of other widths need the whole row in one block"
    assert not row_sums or tn == n_cols
    out_specs = [pl.BlockSpec((tm, tn * c // n_cols), lambda i, j: (i, j)) for c in out_cols]
    out_specs += [pl.BlockSpec((1, tn), lambda i, j: (0, j))] * row_sums
    out_shape = [jax.ShapeDtypeStruct((rows, c), dt) for c, dt in zip(out_cols, out_dtypes)]
    out_shape += [jax.ShapeDtypeStruct((1, n_cols), F32)] * row_sums
    outs = pl.pallas_call(
        body, name=name, grid=(rows // tm, n_cols // tn),
        in_specs=in_specs, out_specs=out_specs, out_shape=out_shape,
        compiler_params=_cparams("arbitrary" if row_sums else "parallel", "arbitrary"),
    )(*[a for a, _ in pairs], *[w for _, w in pairs], *[e for e, _ in extras], *deps)
    return outs


def _tn_rows(m):
    return max(b for b in range(128, min(m, 1408) + 1, 128) if m % b == 0)


def _mm_tn(name, a, b, scale=1.0, bm=None, tk=1024, deps=(), out_dtype=F32):
    rows, m = a.shape
    n = b.shape[1]
    bm = _tn_rows(m) if bm is None else bm
    tk = min(tk, rows)
    nk = rows // tk

    def body(a_ref, b_ref, *rest):
        o_ref, acc_ref = rest[-2:]
        k = pl.program_id(1)

        @pl.when(k == 0)
        def _():
            acc_ref[...] = jnp.zeros_like(acc_ref)

        acc_ref[...] += _dot_tn(a_ref[...].astype(BF16), b_ref[...].astype(BF16))

        @pl.when(k == nk - 1)
        def _():
            o_ref[...] = (acc_ref[...] * scale).astype(o_ref.dtype)

    return pl.pallas_call(
        body, name=name, grid=(m // bm, nk),
        in_specs=[pl.BlockSpec((tk, bm), lambda i, k: (k, i)), pl.BlockSpec((tk, n), lambda i, k: (k, 0))]
        + [_ANY_SPEC] * len(deps),
        out_specs=pl.BlockSpec((bm, n), lambda i, k: (i, 0)),
        out_shape=jax.ShapeDtypeStruct((m, n), out_dtype),
        scratch_shapes=[pltpu.VMEM((bm, n), F32)],
        compiler_params=_cparams("parallel", "arbitrary"),
    )(a, b, *deps)


def _mm_tn_stack(name, a_list, b, tk=1024, out_dtype=F32):
    rows, n = b.shape
    ms = [a.shape[1] for a in a_list]
    tk = min(tk, rows)
    nk = rows // tk
    na = len(a_list)

    def body(*refs):
        a_refs, b_ref, o_ref, acc_ref = refs[:na], refs[na], refs[na + 1], refs[na + 2]
        k = pl.program_id(0)

        @pl.when(k == 0)
        def _():
            acc_ref[...] = jnp.zeros_like(acc_ref)

        bv = b_ref[...].astype(BF16)
        r0 = 0
        for a_ref, m in zip(a_refs, ms):
            acc_ref[r0:r0 + m, :] += _dot_tn(a_ref[...].astype(BF16), bv)
            r0 += m

        @pl.when(k == nk - 1)
        def _():
            o_ref[...] = acc_ref[...].astype(o_ref.dtype)

    return pl.pallas_call(
        body, name=name, grid=(nk,),
        in_specs=[pl.BlockSpec((tk, m), lambda k: (k, 0)) for m in ms] + [pl.BlockSpec((tk, n), lambda k: (k, 0))],
        out_specs=pl.BlockSpec((sum(ms), n), lambda k: (0, 0)),
        out_shape=jax.ShapeDtypeStruct((sum(ms), n), out_dtype),
        scratch_shapes=[pltpu.VMEM((sum(ms), n), F32)],
        compiler_params=_cparams("arbitrary"),
    )(*a_list, b)


def _colsum(name, xs, tm=512):
    rows, cols = xs.shape
    tm = min(tm, rows)

    def body(x_ref, o_ref):
        @pl.when(pl.program_id(0) == 0)
        def _():
            o_ref[...] = jnp.zeros_like(o_ref)

        o_ref[...] += jnp.sum(x_ref[...].astype(F32), axis=0, keepdims=True)

    return pl.pallas_call(
        body, name=name, grid=(rows // tm,),
        in_specs=[pl.BlockSpec((tm, cols), lambda i: (i, 0))],
        out_specs=pl.BlockSpec((1, cols), lambda i: (0, 0)),
        out_shape=jax.ShapeDtypeStruct((1, cols), F32),
        compiler_params=_cparams("arbitrary"),
    )(xs)


def _ew(name, fn, ins, out_cols, out_dtypes, tm=512):
    rows = ins[0].shape[0]
    tm = min(tm, rows)
    ni = len(ins)

    def body(*refs):
        outs = fn(*[r[...] for r in refs[:ni]])
        for o_ref, o in zip(refs[ni:], outs):
            o_ref[...] = o.astype(o_ref.dtype)

    def spec(shape):
        if shape[0] == 1:
            return pl.BlockSpec((1, shape[1]), lambda i: (0, 0))
        return pl.BlockSpec((tm, shape[1]), lambda i: (i, 0))

    return pl.pallas_call(
        body, name=name, grid=(rows // tm,),
        in_specs=[spec(a.shape) for a in ins],
        out_specs=[pl.BlockSpec((tm, c), lambda i: (i, 0)) for c in out_cols],
        out_shape=[jax.ShapeDtypeStruct((rows, c), dt) for c, dt in zip(out_cols, out_dtypes)],
        compiler_params=_cparams("parallel"),
    )(*ins)


def _rms_parts(xv):
    r = lax.rsqrt(jnp.mean(xv * xv, axis=-1, keepdims=True) + EPS)
    return r, xv * r


def _rms_bwd_dx(dh, gain, r, xh):
    dxh = dh * gain
    return r * (dxh - xh * jnp.mean(dxh * xh, axis=-1, keepdims=True))


def _ffn_chunks(f_all):
    return [slice(c, min(c + FFN_CHUNK, f_all)) for c in range(0, f_all, FFN_CHUNK)]


def _loss_head(xo, gain_f, target, d):
    r, xh = _rms_parts(xo)
    err = xh * gain_f - target
    dy = err * (1.0 / d)
    per_tok = jnp.mean(err * err, axis=-1, keepdims=True)
    return (_rms_bwd_dx(dy, gain_f, r, xh), jnp.sum(dy * xh, axis=0, keepdims=True),
            0.5 * jnp.sum(per_tok, axis=0, keepdims=True))


def _ffn_tile(x_ref, g_ref, wg_ref, wu_ref, wd_ref, h_ref, gg_ref, uu_ref):
    xv = x_ref[...]
    _, xh = _rms_parts(xv)
    h = (xh * g_ref[...]).astype(BF16)
    h_ref[...] = h
    acc = None
    for cols in _ffn_chunks(wd_ref.shape[0]):
        gg = _dot_nt(h, wg_ref[cols, :])
        uu = _dot_nt(h, wu_ref[cols, :])
        act = gg * _sigmoid(gg) * uu
        part = _dot_nn(act.astype(BF16), wd_ref[cols, :])
        acc = part if acc is None else acc + part
        gg_ref[:, cols] = gg.astype(BF16)
        uu_ref[:, cols] = uu.astype(BF16)
    return xv + 0.5 * acc


def _ffn_fwd(name, xs, gain, wg_t, wu_t, wd, next_gain, tm=512, deps=()):
    rows, d = xs.shape
    f_all = wd.shape[0]
    tm = min(tm, rows)

    def body(x_ref, g_ref, wg_ref, wu_ref, wd_ref, ng_ref, *rest):
        xo_ref, h_ref, gg_ref, uu_ref, hn_ref = rest[-5:]
        xo = _ffn_tile(x_ref, g_ref, wg_ref, wu_ref, wd_ref, h_ref, gg_ref, uu_ref)
        xo_ref[...] = xo
        hn_ref[...] = (_rms_parts(xo)[1] * ng_ref[...]).astype(BF16)

    tile = pl.BlockSpec((tm, d), lambda i: (i, 0))
    row = pl.BlockSpec((1, d), lambda i: (0, 0))
    wspec = pl.BlockSpec((f_all, d), lambda i: (0, 0), pipeline_mode=pl.Buffered(1))
    hid = pl.BlockSpec((tm, f_all), lambda i: (i, 0))
    return pl.pallas_call(
        body, name=name, grid=(rows // tm,),
        in_specs=[tile, row, wspec, wspec, wspec, row] + [_ANY_SPEC] * len(deps),
        out_specs=[tile, tile, hid, hid, tile],
        out_shape=[jax.ShapeDtypeStruct((rows, d), F32), jax.ShapeDtypeStruct((rows, d), BF16),
                   jax.ShapeDtypeStruct((rows, f_all), BF16), jax.ShapeDtypeStruct((rows, f_all), BF16),
                   jax.ShapeDtypeStruct((rows, d), BF16)],
        compiler_params=_cparams("parallel"),
    )(xs, gain, wg_t, wu_t, wd, next_gain, *deps)


def _ffn_fwd_head(name, xs, gain, wg_t, wu_t, wd, gain_f, target, tm=512):
    rows, d = xs.shape
    f_all = wd.shape[0]
    tm = min(tm, rows)

    def body(x_ref, g_ref, wg_ref, wu_ref, wd_ref, gf_ref, t_ref, dxo_ref, h_ref, gg_ref, uu_ref, dgf_ref, loss_ref):
        xo = _ffn_tile(x_ref, g_ref, wg_ref, wu_ref, wd_ref, h_ref, gg_ref, uu_ref)
        dxo, dgf, loss = _loss_head(xo, gf_ref[...], t_ref[...], d)
        dxo_ref[...] = dxo

        @pl.when(pl.program_id(0) == 0)
        def _():
            dgf_ref[...] = jnp.zeros_like(dgf_ref)
            loss_ref[...] = jnp.zeros_like(loss_ref)

        dgf_ref[...] += dgf
        loss_ref[...] += loss

    tile = pl.BlockSpec((tm, d), lambda i: (i, 0))
    row = pl.BlockSpec((1, d), lambda i: (0, 0))
    wspec = pl.BlockSpec((f_all, d), lambda i: (0, 0), pipeline_mode=pl.Buffered(1))
    hid = pl.BlockSpec((tm, f_all), lambda i: (i, 0))
    return pl.pallas_call(
        body, name=name, grid=(rows // tm,),
        in_specs=[tile, row, wspec, wspec, wspec, row, tile],
        out_specs=[tile, tile, hid, hid, row, pl.BlockSpec((1, 1), lambda i: (0, 0))],
        out_shape=[jax.ShapeDtypeStruct((rows, d), F32), jax.ShapeDtypeStruct((rows, d), BF16),
                   jax.ShapeDtypeStruct((rows, f_all), BF16), jax.ShapeDtypeStruct((rows, f_all), BF16),
                   jax.ShapeDtypeStruct((1, d), F32), jax.ShapeDtypeStruct((1, 1), F32)],
        compiler_params=_cparams("arbitrary"),
    )(xs, gain, wg_t, wu_t, wd, gain_f, target)


def _ffn_bwd(name, dxo, xs, gain, gg_all, uu_all, wg_t, wu_t, wd, tm=256):
    rows, d = xs.shape
    f_all = wd.shape[0]
    tm = min(tm, rows)

    def body(dxo_ref, x_ref, g_ref, gg_ref, uu_ref, wg_ref, wu_ref, wd_ref,
             dx_ref, dgg_ref, duu_ref, act_ref, dgain_ref):
        dxo = dxo_ref[...]
        df = (0.5 * dxo).astype(BF16)
        dh = None
        for cols in _ffn_chunks(f_all):
            gg = gg_ref[:, cols].astype(F32)
            uu = uu_ref[:, cols].astype(F32)
            sg = _sigmoid(gg)
            silu = gg * sg
            dact = _dot_nt(df, wd_ref[cols, :])
            duu = (dact * silu).astype(BF16)
            dgg = (dact * uu * (sg * (1.0 + gg * (1.0 - sg)))).astype(BF16)
            act_ref[:, cols] = (silu * uu).astype(BF16)
            dgg_ref[:, cols] = dgg
            duu_ref[:, cols] = duu
            part = _dot_nn(dgg, wg_ref[cols, :]) + _dot_nn(duu, wu_ref[cols, :])
            dh = part if dh is None else dh + part
        r, xh = _rms_parts(x_ref[...])
        dx_ref[...] = dxo + _rms_bwd_dx(dh, g_ref[...], r, xh)

        @pl.when(pl.program_id(0) == 0)
        def _():
            dgain_ref[...] = jnp.zeros_like(dgain_ref)

        dgain_ref[...] += jnp.sum(dh * xh, axis=0, keepdims=True)

    tile = pl.BlockSpec((tm, d), lambda i: (i, 0))
    row = pl.BlockSpec((1, d), lambda i: (0, 0))
    wspec = pl.BlockSpec((f_all, d), lambda i: (0, 0), pipeline_mode=pl.Buffered(1))
    hid = pl.BlockSpec((tm, f_all), lambda i: (i, 0))
    hid_shape = jax.ShapeDtypeStruct((rows, f_all), BF16)
    return pl.pallas_call(
        body, name=name, grid=(rows // tm,),
        in_specs=[tile, tile, row, hid, hid, wspec, wspec, wspec],
        out_specs=[tile, hid, hid, hid, row],
        out_shape=[jax.ShapeDtypeStruct((rows, d), F32), hid_shape, hid_shape, hid_shape,
                   jax.ShapeDtypeStruct((1, d), F32)],
        compiler_params=_cparams("arbitrary"),
    )(dxo, xs, gain, gg_all, uu_all, wg_t, wu_t, wd)


def _t5_bucket_np(dist):
    max_exact = N_BUCKETS // 2
    dd = np.maximum(dist, 1).astype(np.float32)
    large = max_exact + (np.log(dd / np.float32(max_exact)) / np.float32(math.log(MAX_DISTANCE / max_exact))
                         * np.float32(N_BUCKETS - max_exact)).astype(np.int32)
    large = np.minimum(large, N_BUCKETS - 1)
    return np.where(dist < max_exact, dist, large).astype(np.int32)


def _attn_geometry(g, rows):
    run = rows // 16
    dil = DILATIONS[g]
    if dil == 16:
        bq = BLOCK
        return dict(view=(16, run), block=(None, bq), grid=(16, run // bq), index=lambda r, n: (r, n),
                    pos=np.arange(bq), bq=bq)
    if dil == 4:
        per = BLOCK // 4
        pos = (4 * np.arange(per)[None, :] + np.arange(4)[:, None]).reshape(-1)
        return dict(view=(4, 4, run), block=(4, None, per), grid=(4, run // per), index=lambda r, n: (0, r, n),
                    pos=pos, bq=BLOCK)
    per = 16
    pos = (16 * np.arange(per)[None, :] + np.arange(16)[:, None]).reshape(-1)
    return dict(view=(16, run), block=(16, per), grid=(1, run // per), index=lambda r, n: (0, n),
                pos=pos, bq=16 * per)


def _attn_tables(g, rows):
    geo = _attn_geometry(g, rows)
    pos, bq = geo["pos"], geo["bq"]
    steps = pos[:, None] - np.concatenate([pos - bq, pos])[None, :]
    valid = (steps >= 0) & (steps <= BLOCK)
    bucket = _t5_bucket_np((np.maximum(steps, 0) * DILATIONS[g]).astype(np.int32))
    return bucket, valid.astype(np.int32)


def _bias_fwd(name, bucket, valid, table_t):
    bq = bucket.shape[0]

    def body(bk_ref, ok_ref, tab_ref, o_ref):
        bk = bk_ref[...]
        ok = ok_ref[...] > 0
        for h in range(HEADS_PER_GROUP):
            acc = jnp.zeros(bk.shape, F32)
            for b in range(N_BUCKETS):
                acc = jnp.where(bk == b, tab_ref[h, b], acc)
            o_ref[h] = jnp.where(ok, acc, NEG_INF)

    vm = pl.BlockSpec(memory_space=pltpu.VMEM)
    return pl.pallas_call(
        body, name=name, in_specs=[vm, vm, pl.BlockSpec(memory_space=pltpu.SMEM)], out_specs=vm,
        out_shape=jax.ShapeDtypeStruct((HEADS_PER_GROUP, bq, 2 * bq), F32),
    )(bucket, valid, table_t)


def _bias_bwd(name, bucket, dbias):
    def body(bk_ref, db_ref, o_ref):
        row_id = lax.broadcasted_iota(jnp.int32, (N_BUCKETS, 128), 0)
        col_id = lax.broadcasted_iota(jnp.int32, (N_BUCKETS, 128), 1)
        bk = bk_ref[...]
        acc = jnp.zeros((N_BUCKETS, 128), F32)
        for h in range(HEADS_PER_GROUP):
            db = db_ref[h]
            for b in range(N_BUCKETS):
                part = jnp.sum(jnp.where(bk == b, db, 0.0), axis=0, keepdims=True)
                tot = jnp.sum(part, axis=1, keepdims=True)
                acc = jnp.where((row_id == b) & (col_id == h), tot, acc)
        o_ref[...] = acc

    vm = pl.BlockSpec(memory_space=pltpu.VMEM)
    return pl.pallas_call(body, name=name, in_specs=[vm, vm], out_specs=vm,
                          out_shape=jax.ShapeDtypeStruct((N_BUCKETS, 128), F32))(bucket, dbias)


def _head_of_lane(nrows):
    return lax.broadcasted_iota(jnp.int32, (nrows, ATTN_OUT), 1) // HEAD_DIM


def _stack_heads(a, lane_head):
    zero = jnp.zeros_like(a)
    return jnp.concatenate([jnp.where(lane_head == h, a, zero) for h in range(HEADS_PER_GROUP)], axis=0)


def _unstack_heads(a4, lane_head, bq):
    out = a4[:bq]
    for h in range(1, HEADS_PER_GROUP):
        out = jnp.where(lane_head == h, a4[h * bq:(h + 1) * bq], out)
    return out


def _attn_specs(geo, cols, col_block, index):
    return pl.BlockSpec(geo["block"] + (cols,), lambda r, n: index(r, n) + (col_block,))


def _attn_fwd(name, qkv, g, bias4):
    rows = qkv.shape[0]
    geo = _attn_geometry(g, rows)
    bq, (nsub, nb), index = geo["bq"], geo["grid"], geo["index"]
    blk_shape = tuple(b for b in geo["block"] if b is not None) + (ATTN_OUT,)

    def body(q_ref, kc_ref, kp_ref, vc_ref, vp_ref, b_ref, o_ref, lse_ref):
        n = pl.program_id(1)
        lane_head = _head_of_lane(bq)
        flat = lambda ref: ref[...].reshape(bq, ATTN_OUT)
        q4 = _stack_heads(flat(q_ref), lane_head)
        k2 = jnp.concatenate([flat(kp_ref), flat(kc_ref)], axis=0)
        v2 = jnp.concatenate([flat(vp_ref), flat(vc_ref)], axis=0)
        s = _dot_nt(q4, k2) + b_ref[...]
        col = lax.broadcasted_iota(jnp.int32, s.shape, 1)
        s = jnp.where((col >= bq) | (n > 0), s, NEG_INF)
        mx = jnp.max(s, axis=-1, keepdims=True)
        p = jnp.exp(s - mx)
        den = jnp.sum(p, axis=-1, keepdims=True)
        o4 = _dot_nn(p.astype(BF16), v2) / den
        lse4 = jnp.broadcast_to(mx + jnp.log(den), (HEADS_PER_GROUP * bq, ATTN_OUT))
        o_ref[...] = _unstack_heads(o4, lane_head, bq).reshape(blk_shape)
        lse_ref[...] = _unstack_heads(lse4, lane_head, bq).reshape(blk_shape)

    prev = lambda r, n: index(r, jnp.maximum(n - 1, 0))
    view = lambda a: a.reshape(geo["view"] + (a.shape[1],))
    qkv_v = view(qkv)
    out_spec = _attn_specs(geo, ATTN_OUT, 0, index)
    out_shape = jax.ShapeDtypeStruct(geo["view"] + (ATTN_OUT,), F32)
    o, lse = pl.pallas_call(
        body, name=name, grid=(nsub, nb),
        in_specs=[_attn_specs(geo, ATTN_OUT, g, index), _attn_specs(geo, ATTN_OUT, 3 + g, index),
                  _attn_specs(geo, ATTN_OUT, 3 + g, prev), _attn_specs(geo, ATTN_OUT, 6 + g, index),
                  _attn_specs(geo, ATTN_OUT, 6 + g, prev), pl.BlockSpec(bias4.shape, lambda r, n: (0, 0))],
        out_specs=[out_spec, out_spec], out_shape=[out_shape, out_shape],
        compiler_params=_cparams("parallel", "arbitrary"),
    )(qkv_v, qkv_v, qkv_v, qkv_v, qkv_v, bias4)
    return o.reshape(rows, ATTN_OUT), lse.reshape(rows, ATTN_OUT)


def _attn_bwd(name, qkv, do, lse, cvec, g, bias4):
    rows = qkv.shape[0]
    geo = _attn_geometry(g, rows)
    bq, (nsub, nb), index = geo["bq"], geo["grid"], geo["index"]
    blk_shape = tuple(b for b in geo["block"] if b is not None) + (ATTN_OUT,)
    nlead = len(blk_shape) - 1

    def body(q_ref, kc_ref, kp_ref, vc_ref, vp_ref, do_ref, lse_ref, c_ref, b_ref,
             dq_ref, dk_ref, dv_ref, db_ref, kcar_ref, vcar_ref):
        r, n = pl.program_id(0), pl.program_id(1)
        valid = n < nb
        lane_head = _head_of_lane(bq)
        flat = lambda ref: ref[...].reshape(bq, ATTN_OUT)

        @pl.when((r == 0) & (n == 0))
        def _():
            kcar_ref[...] = jnp.zeros_like(kcar_ref)
            vcar_ref[...] = jnp.zeros_like(vcar_ref)
            db_ref[...] = jnp.zeros_like(db_ref)

        def column(ref, h):
            lead = (slice(None),) * nlead
            return ref[lead + (pl.ds(h * HEAD_DIM, 1),)].reshape(bq, 1)

        q4 = _stack_heads(flat(q_ref), lane_head)
        do4 = _stack_heads(flat(do_ref), lane_head)
        k2 = jnp.concatenate([flat(kp_ref), flat(kc_ref)], axis=0)
        v2 = jnp.concatenate([flat(vp_ref), flat(vc_ref)], axis=0)
        lse4 = jnp.concatenate([column(lse_ref, h) for h in range(HEADS_PER_GROUP)], axis=0)
        c4 = jnp.concatenate([column(c_ref, h) for h in range(HEADS_PER_GROUP)], axis=0)
        s = _dot_nt(q4, k2) + b_ref[...]
        col = lax.broadcasted_iota(jnp.int32, s.shape, 1)
        keep = ((col >= bq) | (n > 0)) & valid
        p = jnp.where(keep, jnp.exp(s - lse4), 0.0)
        ds = p * (_dot_nt(do4, v2) + c4)
        ds_b = ds.astype(BF16)

        @pl.when(valid)
        def _():
            dq = _unstack_heads(_dot_nn(ds_b, k2), lane_head, bq) * (HEAD_DIM ** -0.5)
            dq_ref[...] = dq.astype(BF16).reshape(blk_shape)

        dk2 = _dot_tn(ds_b, q4)
        dv2 = _dot_tn(p.astype(BF16), do4)
        dk_ref[...] = (kcar_ref[...] + dk2[:bq]).astype(BF16).reshape(blk_shape)
        dv_ref[...] = (vcar_ref[...] + dv2[:bq]).astype(BF16).reshape(blk_shape)
        kcar_ref[...] = dk2[bq:]
        vcar_ref[...] = dv2[bq:]
        db_ref[...] += ds

    cur = lambda r, n: index(r, jnp.minimum(n, nb - 1))
    prev = lambda r, n: index(r, jnp.maximum(jnp.minimum(n, nb - 1) - 1, 0))
    late = lambda r, n: index(r, jnp.maximum(n - 1, 0))
    view = lambda a: a.reshape(geo["view"] + (a.shape[1],))
    qkv_v = view(qkv)
    tile = _attn_specs(geo, ATTN_OUT, 0, cur)
    bias_spec = pl.BlockSpec(bias4.shape, lambda r, n: (0, 0))
    out_shape = jax.ShapeDtypeStruct(geo["view"] + (ATTN_OUT,), BF16)
    dq, dk, dv, db = pl.pallas_call(
        body, name=name, grid=(nsub, nb + 1),
        in_specs=[_attn_specs(geo, ATTN_OUT, g, cur), _attn_specs(geo, ATTN_OUT, 3 + g, cur),
                  _attn_specs(geo, ATTN_OUT, 3 + g, prev), _attn_specs(geo, ATTN_OUT, 6 + g, cur),
                  _attn_specs(geo, ATTN_OUT, 6 + g, prev), tile, tile, tile, bias_spec],
        out_specs=[tile, _attn_specs(geo, ATTN_OUT, 0, late), _attn_specs(geo, ATTN_OUT, 0, late), bias_spec],
        out_shape=[out_shape, out_shape, out_shape, jax.ShapeDtypeStruct(bias4.shape, F32)],
        scratch_shapes=[pltpu.VMEM((bq, ATTN_OUT), F32), pltpu.VMEM((bq, ATTN_OUT), F32)],
        compiler_params=_cparams("arbitrary", "arbitrary"),
    )(qkv_v, qkv_v, qkv_v, qkv_v, qkv_v, view(do), view(lse), view(cvec), bias4)
    return dq.reshape(rows, ATTN_OUT), dk.reshape(rows, ATTN_OUT), dv.reshape(rows, ATTN_OUT), db


def _group_weights(lses):
    mx = jnp.maximum(jnp.maximum(lses[0], lses[1]), lses[2])
    es = [jnp.exp(l - mx) for l in lses]
    den = es[0] + es[1] + es[2]
    return [e / den for e in es]


def _combine_fwd(name, os_, lses):
    def fn(o0, o1, o2, l0, l1, l2):
        ws = _group_weights([l0, l1, l2])
        out = ws[0] * o0 + ws[1] * o1 + ws[2] * o2
        return out, out

    return _ew(name, fn, [*os_, *lses], [ATTN_OUT, ATTN_OUT], [F32, BF16], tm=1024)


def _combine_bwd(name, do, oa, lses):
    def fn(dov, oav, l0, l1, l2):
        head_sum = (lax.broadcasted_iota(jnp.int32, (ATTN_OUT, ATTN_OUT), 0) // HEAD_DIM
                    == lax.broadcasted_iota(jnp.int32, (ATTN_OUT, ATTN_OUT), 1) // HEAD_DIM)
        ws = _group_weights([l0, l1, l2])
        prod = dov * oav
        hi = prod.astype(BF16)
        lo = (prod - hi.astype(F32)).astype(BF16)
        ones = jnp.where(head_sum, 1.0, 0.0).astype(BF16)
        bar = _dot_nn(hi, ones) + _dot_nn(lo, ones)
        return tuple(w * dov for w in ws) + tuple(-w * bar for w in ws)

    return _ew(name, fn, [do, oa, *lses], [ATTN_OUT] * 6, [BF16] * 3 + [F32] * 3, tm=1024)


def _ssm_disc(a_re, a_im, log_dt, b_re, b_im):
    dt = jnp.exp(log_dt)
    mag = jnp.exp(a_re * dt)
    ab_re = mag * jnp.cos(a_im * dt)
    ab_im = mag * jnp.sin(a_im * dt)
    den = a_re * a_re + a_im * a_im
    xr = ab_re - 1.0
    coef_re = (xr * a_re + ab_im * a_im) / den
    coef_im = (ab_im * a_re - xr * a_im) / den
    bb_re = coef_re[None] * b_re - coef_im[None] * b_im
    bb_im = coef_re[None] * b_im + coef_im[None] * b_re
    return ab_re, ab_im, bb_re, bb_im


def _ssm_params_fwd(name, a_re, a_im, log_dt, b_re, b_im):
    pows = jax.ShapeDtypeStruct((SCAN_STEPS,) + a_re.shape, F32)
    cgn = jax.ShapeDtypeStruct(b_re.shape, F32)

    def body(ar, ai, ld, br, bi, o_pr, o_pi, o_bbr, o_bbi):
        ab_re, ab_im, bb_re, bb_im = _ssm_disc(ar[...], ai[...], ld[...], br[...], bi[...])
        pr, pi = ab_re, ab_im
        for j in range(SCAN_STEPS):
            o_pr[j] = pr
            o_pi[j] = pi
            pr, pi = pr * ab_re - pi * ab_im, pr * ab_im + pi * ab_re
        o_bbr[...] = bb_re
        o_bbi[...] = bb_im

    vm = pl.BlockSpec(memory_space=pltpu.VMEM)
    return pl.pallas_call(body, name=name, in_specs=[vm] * 5, out_specs=[vm] * 4,
                          out_shape=[pows, pows, cgn, cgn])(a_re, a_im, log_dt, b_re, b_im)


def _ssm_params_bwd(name, a_re, a_im, log_dt, b_re, b_im, d_ab_re, d_ab_im, d_bb_re, d_bb_im):
    gn = jax.ShapeDtypeStruct(a_re.shape, F32)
    cgn = jax.ShapeDtypeStruct(b_re.shape, F32)

    def body(ar, ai, ld, br, bi, g0, g1, g2, g3, o_ar, o_ai, o_ld, o_br, o_bi):
        _, vjp = jax.vjp(_ssm_disc, ar[...], ai[...], ld[...], br[...], bi[...])
        outs = vjp((g0[...], g1[...], g2[...], g3[...]))
        for o_ref, o in zip((o_ar, o_ai, o_ld, o_br, o_bi), outs):
            o_ref[...] = o

    vm = pl.BlockSpec(memory_space=pltpu.VMEM)
    return pl.pallas_call(body, name=name, in_specs=[vm] * 9, out_specs=[vm] * 5,
                          out_shape=[gn, gn, jax.ShapeDtypeStruct(log_dt.shape, F32), cgn, cgn],
                          )(a_re, a_im, log_dt, b_re, b_im, d_ab_re, d_ab_im, d_bb_re, d_bb_im)


def _scan_block(s_ref, carry_ref, tmp_ref, pw_ref, reverse, sprev=None):
    nl = SSM_LANES
    halves = range(SCAN_COLS // SCAN_SUB)
    zero = jnp.zeros((SCAN_SUB, SCAN_LANES), F32)
    for half in (reversed(halves) if reverse else halves):
        sub_rows = pl.ds(half * SCAN_SUB, SCAN_SUB)
        for lc in range(nl // SCAN_LANES):
            re_l = pl.ds(lc * SCAN_LANES, SCAN_LANES)
            im_l = pl.ds(nl + lc * SCAN_LANES, SCAN_LANES)
            are, aim = pw_ref[0, :, re_l], pw_ref[0, :, im_l]

            def step_of(j):
                return SCAN_STEPS - 1 - j if reverse else j

            def pass1(j, st):
                sr, si = st
                jj = step_of(j)
                nr = are * sr - aim * si + s_ref[jj, sub_rows, re_l]
                ni = are * si + aim * sr + s_ref[jj, sub_rows, im_l]
                s_ref[jj, sub_rows, re_l] = nr
                s_ref[jj, sub_rows, im_l] = ni
                return nr, ni

            er, ei = lax.fori_loop(0, SCAN_STEPS, pass1, (zero, zero), unroll=2)
            tmp_ref[0:SCAN_SUB, re_l] = er
            tmp_ref[0:SCAN_SUB, im_l] = ei
            apr, api = pw_ref[SCAN_STEPS - 1, 0:1, re_l], pw_ref[SCAN_STEPS - 1, 0:1, im_l]
            sr, si = carry_ref[0:1, re_l], carry_ref[0:1, im_l]
            for step in range(SCAN_SUB):
                c = SCAN_SUB - 1 - step if reverse else step
                tmp_ref[SCAN_SUB + c:SCAN_SUB + c + 1, re_l] = sr
                tmp_ref[SCAN_SUB + c:SCAN_SUB + c + 1, im_l] = si
                e_r, e_i = tmp_ref[c:c + 1, re_l], tmp_ref[c:c + 1, im_l]
                sr, si = apr * sr - api * si + e_r, apr * si + api * sr + e_i
            carry_ref[0:1, re_l] = sr
            carry_ref[0:1, im_l] = si
            cr = tmp_ref[SCAN_SUB:2 * SCAN_SUB, re_l]
            ci = tmp_ref[SCAN_SUB:2 * SCAN_SUB, im_l]

            if sprev is None:
                def pass2(j, st):
                    pr, pi = pw_ref[j, :, re_l], pw_ref[j, :, im_l]
                    jj = step_of(j)
                    s_ref[jj, sub_rows, re_l] += pr * cr - pi * ci
                    s_ref[jj, sub_rows, im_l] += pr * ci + pi * cr
                    return st

                lax.fori_loop(0, SCAN_STEPS, pass2, 0, unroll=2)
            else:
                st_ref, prev_ref, have_prev, dab_ref = sprev

                def corrected(jj, pr, pi):
                    gr = s_ref[jj, sub_rows, re_l] + pr * cr - pi * ci
                    gi = s_ref[jj, sub_rows, im_l] + pr * ci + pi * cr
                    s_ref[jj, sub_rows, re_l] = gr
                    s_ref[jj, sub_rows, im_l] = gi
                    return gr, gi

                def pass2(j, st):
                    dr, di = st
                    jj = SCAN_STEPS - 1 - j
                    gr, gi = corrected(jj, pw_ref[j, :, re_l], pw_ref[j, :, im_l])
                    qr, qi = st_ref[jj - 1, sub_rows, re_l], st_ref[jj - 1, sub_rows, im_l]
                    return dr + gr * qr + gi * qi, di + gi * qr - gr * qi

                dr, di = lax.fori_loop(0, SCAN_STEPS - 1, pass2, (zero, zero), unroll=2)
                gr, gi = corrected(0, pw_ref[SCAN_STEPS - 1, :, re_l], pw_ref[SCAN_STEPS - 1, :, im_l])
                sub = lax.broadcasted_iota(jnp.int32, (SCAN_SUB, SCAN_LANES), 0)
                if half == 0:
                    pv_r = prev_ref[SCAN_SUB - 1:SCAN_SUB, re_l] * have_prev
                    pv_i = prev_ref[SCAN_SUB - 1:SCAN_SUB, im_l] * have_prev
                else:
                    before = pl.ds(half * SCAN_SUB - 1, 1)
                    pv_r, pv_i = st_ref[SCAN_STEPS - 1, before, re_l], st_ref[SCAN_STEPS - 1, before, im_l]
                shape = (SCAN_SUB, SCAN_LANES)
                qr = jnp.where(sub == 0, jnp.broadcast_to(pv_r, shape),
                               pltpu.roll(st_ref[SCAN_STEPS - 1, sub_rows, re_l], 1, 0))
                qi = jnp.where(sub == 0, jnp.broadcast_to(pv_i, shape),
                               pltpu.roll(st_ref[SCAN_STEPS - 1, sub_rows, im_l], 1, 0))
                dab_ref[:, re_l] += dr + gr * qr + gi * qi
                dab_ref[:, im_l] += di + gi * qr - gr * qi


def _scan_view(a):
    return a.reshape(16, a.shape[0] // 16, a.shape[1])


def _pair_tile(p):
    start = (p * 2 * SSM_GROUP // PAIR_TILE) * PAIR_TILE
    return slice(start, start + PAIR_TILE)


def _pair_lanes(p):
    return pl.ds(p * PAIR_LANES, PAIR_LANES), pl.ds(SSM_LANES + p * PAIR_LANES, PAIR_LANES)


def _pair_store(s_ref, p, val):
    re_l, im_l = _pair_lanes(p)
    s_ref[:, :, re_l] = val[:, :PAIR_LANES].reshape(16, SCAN_COLS, PAIR_LANES)
    s_ref[:, :, im_l] = val[:, PAIR_LANES:].reshape(16, SCAN_COLS, PAIR_LANES)


def _pair_load(s_ref, p):
    re_l, im_l = _pair_lanes(p)
    parts = [s_ref[:, :, l].reshape(SCAN_BLOCK, PAIR_LANES) for l in (re_l, im_l)]
    return jnp.concatenate(parts, axis=1).astype(BF16)


def _pair_sum(fn):
    per = PAIR_TILE // (2 * SSM_GROUP)
    tiles = []
    for t in range(SSM_PAIRS // per):
        acc = None
        for p in range(t * per, (t + 1) * per):
            part = fn(p)
            acc = part if acc is None else acc + part
        tiles.append(acc)
    return jnp.concatenate(tiles, axis=1)


def _ssm_fwd(name, u, bb_mats, c_mats, pw_rows, d_skip):
    rows = u.shape[0]
    nl2 = 2 * SSM_LANES
    nblk = rows // SCAN_BLOCK

    def body(u_ref, bb_ref, c_ref, pw_ref, d_ref, y_ref, yg_ref, s_ref, carry_ref, tmp_ref):
        @pl.when(pl.program_id(0) == 0)
        def _():
            carry_ref[...] = jnp.zeros_like(carry_ref)

        uv = u_ref[...].reshape(SCAN_BLOCK, SSM_WIDTH)
        ub = uv.astype(BF16)
        for p in range(SSM_PAIRS):
            _pair_store(s_ref, p, _dot_nn(ub[:, _pair_tile(p)], bb_ref[p]))
        _scan_block(s_ref, carry_ref, tmp_ref, pw_ref, reverse=False)
        ys = _pair_sum(lambda p: _dot_nt(_pair_load(s_ref, p), c_ref[p]))
        yv = ys + d_ref[...] * uv
        y_ref[...] = yv.reshape(16, SCAN_COLS, SSM_WIDTH)
        yg_ref[...] = jax.nn.gelu(yv).astype(BF16).reshape(16, SCAN_COLS, SSM_WIDTH)

    const = lambda shape: pl.BlockSpec(shape, lambda i: (0,) * len(shape))
    blk = lambda cols: pl.BlockSpec((16, SCAN_COLS, cols), lambda i: (0, i, 0))
    pair_mats = const((SSM_PAIRS, PAIR_TILE, PAIR_TILE))
    y, yg, s = pl.pallas_call(
        body, name=name, grid=(nblk,),
        in_specs=[blk(SSM_WIDTH), pair_mats, pair_mats, const((SCAN_STEPS, SCAN_SUB, nl2)), const((1, SSM_WIDTH))],
        out_specs=[blk(SSM_WIDTH), blk(SSM_WIDTH), blk(nl2)],
        out_shape=[jax.ShapeDtypeStruct((16, rows // 16, SSM_WIDTH), F32),
                   jax.ShapeDtypeStruct((16, rows // 16, SSM_WIDTH), BF16),
                   jax.ShapeDtypeStruct((16, rows // 16, nl2), F32)],
        scratch_shapes=[pltpu.VMEM((SCAN_SUB, nl2), F32), pltpu.VMEM((2 * SCAN_SUB, nl2), F32)],
        compiler_params=_cparams("arbitrary"),
    )(_scan_view(u), bb_mats, c_mats, pw_rows, d_skip)
    return y.reshape(rows, SSM_WIDTH), yg.reshape(rows, SSM_WIDTH), s.reshape(rows, nl2)


def _ssm_bwd(name, dy, u, states, bb_mats, c_mats, pwc_rows, d_skip):
    rows = u.shape[0]
    nl2 = 2 * SSM_LANES
    nblk = rows // SCAN_BLOCK

    def body(dy_ref, u_ref, st_ref, prev_ref, bb_ref, c_ref, pw_ref, d_ref,
             du_ref, dbb_ref, dc_ref, dab_ref, dd_ref, g_ref, carry_ref, tmp_ref):
        i = pl.program_id(0)

        @pl.when(i == 0)
        def _():
            carry_ref[...] = jnp.zeros_like(carry_ref)
            for ref in (dbb_ref, dc_ref, dab_ref, dd_ref):
                ref[...] = jnp.zeros_like(ref)

        dyv = dy_ref[...].reshape(SCAN_BLOCK, SSM_WIDTH)
        uv = u_ref[...].reshape(SCAN_BLOCK, SSM_WIDTH)
        dyb, ub = dyv.astype(BF16), uv.astype(BF16)
        for p in range(SSM_PAIRS):
            _pair_store(g_ref, p, _dot_nn(dyb[:, _pair_tile(p)], c_ref[p]))
        have_prev = (i < nblk - 1).astype(F32)
        _scan_block(g_ref, carry_ref, tmp_ref, pw_ref, reverse=True,
                    sprev=(st_ref, prev_ref, have_prev, dab_ref))

        def pair_work(p):
            gp = _pair_load(g_ref, p)
            dbb_ref[p] += _dot_tn(ub[:, _pair_tile(p)], gp)
            dc_ref[p] += _dot_tn(dyb[:, _pair_tile(p)], _pair_load(st_ref, p))
            return _dot_nt(gp, bb_ref[p])

        du_ref[...] = (_pair_sum(pair_work) + d_ref[...] * dyv).reshape(16, SCAN_COLS, SSM_WIDTH)
        dd_ref[...] += jnp.sum(dyv * uv, axis=0, keepdims=True)

    const = lambda shape: pl.BlockSpec(shape, lambda i: (0,) * len(shape))
    blk = lambda cols: pl.BlockSpec((16, SCAN_COLS, cols), lambda i: (0, nblk - 1 - i, 0))
    per8 = SCAN_COLS // SCAN_SUB
    prev_spec = pl.BlockSpec((None, SCAN_SUB, nl2), lambda i: (15, jnp.maximum((nblk - 1 - i) * per8 - 1, 0), 0))
    pair_mats = const((SSM_PAIRS, PAIR_TILE, PAIR_TILE))
    pair_shape = jax.ShapeDtypeStruct((SSM_PAIRS, PAIR_TILE, PAIR_TILE), F32)
    sv = _scan_view(states)
    du, dbb, dc, dab, dd = pl.pallas_call(
        body, name=name, grid=(nblk,),
        in_specs=[blk(SSM_WIDTH), blk(SSM_WIDTH), blk(nl2), prev_spec, pair_mats, pair_mats,
                  const((SCAN_STEPS, SCAN_SUB, nl2)), const((1, SSM_WIDTH))],
        out_specs=[blk(SSM_WIDTH), pair_mats, pair_mats, const((SCAN_SUB, nl2)), const((1, SSM_WIDTH))],
        out_shape=[jax.ShapeDtypeStruct((16, rows // 16, SSM_WIDTH), F32), pair_shape, pair_shape,
                   jax.ShapeDtypeStruct((SCAN_SUB, nl2), F32), jax.ShapeDtypeStruct((1, SSM_WIDTH), F32)],
        scratch_shapes=[pltpu.VMEM((16, SCAN_COLS, nl2), F32), pltpu.VMEM((SCAN_SUB, nl2), F32),
                        pltpu.VMEM((2 * SCAN_SUB, nl2), F32)],
        compiler_params=_cparams("arbitrary"),
    )(_scan_view(dy), _scan_view(u), sv, sv, bb_mats, c_mats, pwc_rows, d_skip)
    return du.reshape(rows, SSM_WIDTH), dbb, dc, dab, dd


def _adamw(name, w, m, v, gparts, tr):
    rows, cols = w.shape

    def body(w_ref, m_ref, v_ref, g_ref, og_ref, od_ref, om_ref, ov_ref):
        g = g_ref[0].astype(F32)
        for i in range(1, N_DEV):
            g = g + g_ref[i].astype(F32)
        m_new = B1 * m_ref[...] + (1.0 - B1) * g
        v_new = B2 * v_ref[...] + (1.0 - B2) * (g * g)
        m_hat = m_new / (1.0 - B1 ** STEP)
        v_hat = v_new / (1.0 - B2 ** STEP)
        og_ref[...] = g
        od_ref[...] = -LR * (m_hat / (jnp.sqrt(v_hat) + ADAM_EPS) + WD * w_ref[...])
        om_ref[...] = m_new
        ov_ref[...] = v_new

    spec = pl.BlockSpec((tr, cols), lambda i: (i, 0))
    shape = jax.ShapeDtypeStruct((rows, cols), F32)
    return pl.pallas_call(
        body, name=name, grid=(rows // tr,),
        in_specs=[spec, spec, spec, pl.BlockSpec((N_DEV, tr, cols), lambda i: (0, i, 0))],
        out_specs=[spec] * 4, out_shape=[shape] * 4,
        compiler_params=_cparams("parallel"),
    )(w, m, v, gparts)


_SHARDED = (
    ("ffn1_w_gate", True, (352, 1024)), ("ffn1_w_up", True, (352, 1024)), ("ffn1_w_down", False, (352, 1024)),
    ("w_in", True, (608, 1024)), ("ssm_w_glu", True, (128, 512)), ("w_attn_branch", True, (128, 256)),
    ("w_ssm_branch", True, (128, 512)), ("w_out", False, (128, 1024)),
    ("ffn2_w_gate", True, (352, 1024)), ("ffn2_w_up", True, (352, 1024)), ("ffn2_w_down", False, (352, 1024)),
)
_SMALL = ("ffn1_norm", "mix_norm", "gate_bias", "rel_bias_table", "ssm_a_re", "ssm_a_im", "ssm_log_dt",
          "ssm_b_re", "ssm_b_im", "ssm_c_re", "ssm_c_im", "ssm_d", "ffn2_norm", "final_norm")
_ORDER = ("ffn1_norm", "ffn1_w_gate", "ffn1_w_up", "ffn1_w_down", "mix_norm", "w_in", "gate_bias",
          "rel_bias_table", "ssm_a_re", "ssm_a_im", "ssm_log_dt", "ssm_b_re", "ssm_b_im", "ssm_c_re",
          "ssm_c_im", "ssm_d", "ssm_w_glu", "w_attn_branch", "w_ssm_branch", "w_out", "ffn2_norm",
          "ffn2_w_gate", "ffn2_w_up", "ffn2_w_down", "final_norm")


def _pack_rows(shape):
    return shape[0] * shape[1] // D_MODEL


_SHARD_INFO = {nm: (tr, shape) for nm, tr, shape in _SHARDED}
_PHASES = {
    "f1gu": ("ffn1_w_gate", "ffn1_w_up"), "f1d": ("ffn1_w_down",),
    "mix": ("w_in", "ssm_w_glu", "w_attn_branch", "w_ssm_branch", "w_out"),
    "f2": ("ffn2_w_gate", "ffn2_w_up", "ffn2_w_down"),
}


def _to_rows(a, nm):
    tr, shape = _SHARD_INFO[nm]
    return (a.T if tr else a).reshape(_pack_rows(shape), D_MODEL)


def _from_rows(p, nm):
    tr, shape = _SHARD_INFO[nm]
    a = p.reshape(shape)
    return a.T if tr else a


def _full_weight(gathered, nm):
    _, shape = _SHARD_INFO[nm]
    return gathered.reshape(N_DEV * shape[0], shape[1])


def _grad_blocks(g, nm):
    _, shape = _SHARD_INFO[nm]
    return g.astype(BF16).reshape(N_DEV, _pack_rows(shape), D_MODEL)


_SMALL_TILE = 8 * 128


def _small_rows(a):
    flat = a.reshape(-1)
    return jnp.pad(flat, (0, (-flat.shape[0]) % _SMALL_TILE)).reshape(-1, 128)


def _pack_small(ws, last=None):
    tail = jnp.zeros((), F32) if last is None else last
    return jnp.concatenate([_small_rows(ws[nm]) for nm in _SMALL] + [_small_rows(tail)], axis=0)


def _unpack_small(pack, like):
    out, r0 = {}, 0
    for nm in _SMALL:
        n = like[nm].size
        nr = 8 * -(-n // _SMALL_TILE)
        out[nm] = pack[r0:r0 + nr].reshape(-1)[:n].reshape(like[nm].shape)
        r0 += nr
    return out


def _residue_order(a):
    rows, cols = a.shape
    return a.reshape(rows // 16, 16, cols).transpose(1, 0, 2).reshape(rows, cols)


def _token_order(a):
    rows, cols = a.shape
    return a.reshape(16, rows // 16, cols).transpose(1, 0, 2).reshape(rows, cols)


_PAIRS_PER_TILE = PAIR_TILE // (2 * SSM_GROUP)
_PAIR_AXES = (SSM_PAIRS // _PAIRS_PER_TILE, _PAIRS_PER_TILE, 2)


def _pair_matrices(re, im):
    six = jnp.stack([re, im]).reshape((2,) + _PAIR_AXES + (SSM_GROUP, SSM_STATE))
    eye_j, eye_l = jnp.eye(_PAIRS_PER_TILE, dtype=re.dtype), jnp.eye(2, dtype=re.dtype)
    mats = jnp.einsum("xkjlcn,jJ,lL->kjJLcxln", six, eye_j, eye_l)
    return mats.reshape(SSM_PAIRS, PAIR_TILE, PAIR_TILE).astype(BF16)


def _pair_diagonals(acc):
    k, j, l = _PAIR_AXES
    eight = acc.reshape(k, j, j, l, SSM_GROUP, 2, l, SSM_STATE)
    eye_j, eye_l = jnp.eye(j, dtype=acc.dtype), jnp.eye(l, dtype=acc.dtype)
    own = jnp.einsum("kjJLcxln,jJ,lL->xkjlcn", eight, eye_j, eye_l).reshape(2, SSM_GROUPS, SSM_GROUP, SSM_STATE)
    return own[0], own[1]


def _local_step(xs, target, small, weights_of, send_grads, first_deps=()):
    rows = xs.shape[0]
    gfull, gsmall = {}, {}

    table_t = small["rel_bias_table"].T
    tables, bias4 = [], []
    for g in range(N_GROUPS):
        bucket, valid = [jnp.asarray(t) for t in _attn_tables(g, rows)]
        bias_g = _bias_fwd(f"rel_bias_fwd_{g}", bucket, valid, table_t[g * HEADS_PER_GROUP:(g + 1) * HEADS_PER_GROUP])
        tables.append(bucket)
        bias4.append(bias_g.reshape(-1, bias_g.shape[-1]))
    pw_re, pw_im, bb_re, bb_im = _ssm_params_fwd(
        "ssm_params_fwd", small["ssm_a_re"], small["ssm_a_im"], small["ssm_log_dt"].reshape(SSM_GROUPS, 1),
        small["ssm_b_re"].transpose(2, 0, 1), small["ssm_b_im"].transpose(2, 0, 1))

    def power_rows(sign):
        row = jnp.concatenate([pw_re.reshape(SCAN_STEPS, 1, SSM_LANES), sign * pw_im.reshape(SCAN_STEPS, 1, SSM_LANES)],
                              axis=2)
        return jnp.broadcast_to(row, (SCAN_STEPS, SCAN_SUB, 2 * SSM_LANES))

    bb_mats = _pair_matrices(bb_re.transpose(1, 0, 2), bb_im.transpose(1, 0, 2))
    c_mats = _pair_matrices(small["ssm_c_re"], -small["ssm_c_im"])
    pw_fwd, pw_bwd = power_rows(1.0), power_rows(-1.0)
    d_skip = small["ssm_d"].reshape(1, SSM_WIDTH)
    wf = dict(weights_of("f1", [xs, target, bb_mats, c_mats, pw_fwd, pw_bwd] + bias4))

    x1, h1, gg1, uu1, hmix = _ffn_fwd("ffn1_fwd", xs, small["ffn1_norm"], wf["ffn1_w_gate"], wf["ffn1_w_up"],
                                      wf["ffn1_w_down"], small["mix_norm"], deps=first_deps)
    wf.update(weights_of("mix", x1))
    w_in = wf["w_in"]
    w_qkv, w_u, w_g = w_in[:3 * ATTN_WIDTH], w_in[3 * ATTN_WIDTH:3 * ATTN_WIDTH + SSM_WIDTH], w_in[3 * ATTN_WIDTH + SSM_WIDTH:]
    qscale = jnp.concatenate([jnp.full((1, ATTN_WIDTH), HEAD_DIM ** -0.5, F32), jnp.ones((1, 2 * ATTN_WIDTH), F32)], axis=1)
    qkv, = _mm("in_qkv", [(hmix, w_qkv)], True, 3 * ATTN_WIDTH, [BF16],
               epilogue=lambda acc, sc: (acc * sc,), extras=[(qscale, 0)], tn=ATTN_WIDTH)
    u, = _mm("in_u", [(hmix, w_u)], True, SSM_WIDTH, [F32])
    gates, = _mm("in_gates", [(hmix, w_g)], True, 2 * D_MODEL, [F32],
                 epilogue=lambda acc, b: (_sigmoid(acc + b),), extras=[(small["gate_bias"], 0)])

    o_g, lse_g = [], []
    for g in range(N_GROUPS):
        o, lse = _attn_fwd(f"attn_fwd_{g}", qkv, g, bias4[g])
        o_g.append(o)
        lse_g.append(lse)
    oa_f32, oa = _combine_fwd("attn_combine_fwd", o_g, lse_g)
    y_attn, = _mm("attn_branch", [(oa, wf["w_attn_branch"])], True, D_MODEL, [F32])
    y_raw, ygelu, states = _ssm_fwd("ssm_fwd", u, bb_mats, c_mats, pw_fwd, d_skip)
    glu, ysg = _mm("ssm_glu", [(ygelu, wf["ssm_w_glu"])], True, 2 * SSM_WIDTH, [F32, BF16],
                   epilogue=lambda gv: (gv, gv[:, :SSM_WIDTH] * _sigmoid(gv[:, SSM_WIDTH:])),
                   tn=2 * SSM_WIDTH, out_cols=[2 * SSM_WIDTH, SSM_WIDTH])
    y_ssm, merged = _mm("ssm_branch_merge", [(ysg, wf["w_ssm_branch"])], True, D_MODEL, [F32, BF16],
                        epilogue=lambda acc, ga, gs, ya: (acc, ga * ya + gs * acc),
                        extras=[(gates, 0), (gates, D_MODEL), (y_attn, 0)])
    x2, = _mm("mix_out", [(merged, wf["w_out"])], False, D_MODEL, [F32],
              epilogue=lambda acc, res: (res + acc,), extras=[(x1, 0)])
    wf.update(weights_of("f2", x2))
    dx3, h2, gg2, uu2, gsmall["final_norm"], gsmall["loss"] = _ffn_fwd_head(
        "ffn2_fwd", x2, small["ffn2_norm"], wf["ffn2_w_gate"], wf["ffn2_w_up"], wf["ffn2_w_down"],
        small["final_norm"].reshape(1, D_MODEL), target)

    dx2, dgg2, duu2, act2, gsmall["ffn2_norm"] = _ffn_bwd(
        "ffn2_bwd", dx3, x2, small["ffn2_norm"], gg2, uu2, wf["ffn2_w_gate"], wf["ffn2_w_up"], wf["ffn2_w_down"])
    gfull["ffn2_w_gate"] = _mm_tn("ffn2_dwg", dgg2, h2, out_dtype=BF16)
    gfull["ffn2_w_up"] = _mm_tn("ffn2_dwu", duu2, h2, out_dtype=BF16)
    gfull["ffn2_w_down"] = _mm_tn("ffn2_dwd", act2, dx3, scale=0.5, out_dtype=BF16)
    sent = send_grads("f2", gfull)

    def merge_bwd(dm, ga, gs, ya, ys):
        dza, dzs = dm * ya * ga * (1.0 - ga), dm * ys * gs * (1.0 - gs)
        return (dm * ga, dm * gs, dza, dzs, jnp.sum(dza, axis=0, keepdims=True), jnp.sum(dzs, axis=0, keepdims=True))

    dya, dys, dzga, dzgs, dba, dbs = _mm(
        "mix_out_bwd", [(dx2, wf["w_out"])], True, D_MODEL, [BF16] * 4, epilogue=merge_bwd, row_sums=2,
        extras=[(gates, 0), (gates, D_MODEL), (y_attn, 0), (y_ssm, 0)], deps=sent, tm=512, tn=D_MODEL)
    gfull["w_out"] = _mm_tn("dw_out", merged, dx2, out_dtype=BF16)
    gsmall["gate_bias"] = jnp.concatenate([dba, dbs], axis=1)

    gfull["w_ssm_branch"] = _mm_tn("dw_ssm_branch", dys, ysg, out_dtype=BF16)

    def glu_bwd(dysg, av, bv):
        sb = _sigmoid(bv)
        return (dysg * sb, dysg * av * sb * (1.0 - sb))

    dglu_a, dglu_b = _mm("ssm_branch_bwd", [(dys, wf["w_ssm_branch"])], False, SSM_WIDTH, [BF16, BF16],
                         epilogue=glu_bwd, extras=[(glu, 0), (glu, SSM_WIDTH)])
    w_glu = wf["ssm_w_glu"]
    gfull["ssm_w_glu"] = _mm_tn_stack("dw_glu", [dglu_a, dglu_b], ygelu, out_dtype=BF16)

    def gelu_bwd(acc, yv):
        _, vjp = jax.vjp(jax.nn.gelu, yv)
        return (vjp(acc)[0],)

    dy_raw, = _mm("ssm_glu_bwd", [(dglu_a, w_glu[:SSM_WIDTH]), (dglu_b, w_glu[SSM_WIDTH:])], False, SSM_WIDTH, [F32],
                  epilogue=gelu_bwd, extras=[(y_raw, 0)])
    du, dbb_acc, dc_acc, dab_rows, gsmall_d = _ssm_bwd(
        "ssm_bwd", dy_raw, u, states, bb_mats, c_mats, pw_bwd, d_skip)
    gsmall["ssm_d"] = gsmall_d
    dbb_re, dbb_im = [a.transpose(1, 0, 2) for a in _pair_diagonals(dbb_acc)]
    dc_re, dc_im = _pair_diagonals(dc_acc)
    gsmall["ssm_c_re"], gsmall["ssm_c_im"] = dc_re, -dc_im
    dab = _colsum("ssm_dab", dab_rows)
    d_ar, d_ai, d_ld, d_br, d_bi = _ssm_params_bwd(
        "ssm_params_bwd", small["ssm_a_re"], small["ssm_a_im"], small["ssm_log_dt"].reshape(SSM_GROUPS, 1),
        small["ssm_b_re"].transpose(2, 0, 1), small["ssm_b_im"].transpose(2, 0, 1),
        dab[:, :SSM_LANES].reshape(SSM_GROUPS, SSM_STATE), dab[:, SSM_LANES:].reshape(SSM_GROUPS, SSM_STATE),
        dbb_re, dbb_im)
    gsmall["ssm_a_re"], gsmall["ssm_a_im"], gsmall["ssm_log_dt"] = d_ar, d_ai, d_ld.reshape(SSM_GROUPS)
    gsmall["ssm_b_re"], gsmall["ssm_b_im"] = d_br.transpose(1, 2, 0), d_bi.transpose(1, 2, 0)

    gfull["w_attn_branch"] = _mm_tn("dw_attn_branch", dya, oa, out_dtype=BF16)
    doa, = _mm("attn_branch_bwd", [(dya, wf["w_attn_branch"])], False, ATTN_OUT, [F32])
    dc = _combine_bwd("attn_combine_bwd", doa, oa_f32, lse_g)
    dqkv_cols = [None] * 9
    dtable = []
    for g in range(N_GROUPS):
        dq, dk, dv, db = _attn_bwd(f"attn_bwd_{g}", qkv, dc[g], lse_g[g], dc[3 + g], g, bias4[g])
        dqkv_cols[g], dqkv_cols[3 + g], dqkv_cols[6 + g] = dq, dk, dv
        dt = _bias_bwd(f"rel_bias_bwd_{g}", tables[g], db.reshape(HEADS_PER_GROUP, -1, db.shape[-1]))
        dtable.append(dt[:, :HEADS_PER_GROUP])
    gsmall["rel_bias_table"] = jnp.concatenate(dtable, axis=1)

    gfull["w_in"] = jnp.concatenate([_mm_tn_stack("dw_in_qkv", dqkv_cols, hmix, out_dtype=BF16),
                                     _mm_tn_stack("dw_in_rest", [du, dzga, dzgs], hmix, out_dtype=BF16)], axis=0)
    sent = send_grads("mix", gfull)
    qkv_pairs = [(c, w_qkv[i * ATTN_OUT:(i + 1) * ATTN_OUT]) for i, c in enumerate(dqkv_cols)]

    def mix_norm_bwd(dh, xv, gain, dres):
        r, xh = _rms_parts(xv)
        return dres + _rms_bwd_dx(dh, gain, r, xh), jnp.sum(dh * xh, axis=0, keepdims=True)

    dx1, gsmall["mix_norm"] = _mm(
        "in_bwd", qkv_pairs + [(du, w_u), (dzga, w_g[:D_MODEL]), (dzgs, w_g[D_MODEL:])], False, D_MODEL, [F32],
        epilogue=mix_norm_bwd, row_sums=1, extras=[(x1, 0), (small["mix_norm"], 0), (dx2, 0)], tm=512, tn=D_MODEL,
        deps=sent)

    dx, dgg1, duu1, act1, gsmall["ffn1_norm"] = _ffn_bwd(
        "ffn1_bwd", dx1, xs, small["ffn1_norm"], gg1, uu1, wf["ffn1_w_gate"], wf["ffn1_w_up"], wf["ffn1_w_down"])
    sent = send_grads("small", gsmall)
    gfull["ffn1_w_gate"] = _mm_tn("ffn1_dwg", dgg1, h1, deps=sent, out_dtype=BF16)
    gfull["ffn1_w_up"] = _mm_tn("ffn1_dwu", duu1, h1, out_dtype=BF16)
    sent = send_grads("f1gu", gfull)
    gfull["ffn1_w_down"] = _mm_tn("ffn1_dwd", act1, dx1, scale=0.5, deps=sent, out_dtype=BF16)
    send_grads("f1d", gfull)
    return dx, gsmall


def kernel(x, ffn1_norm, ffn1_w_gate, ffn1_w_up, ffn1_w_down, mix_norm, w_in, gate_bias, rel_bias_table, ssm_a_re, ssm_a_im, ssm_log_dt, ssm_b_re, ssm_b_im, ssm_c_re, ssm_c_im, ssm_d, ssm_w_glu, w_attn_branch, w_ssm_branch, w_out, ffn2_norm, ffn2_w_gate, ffn2_w_up, ffn2_w_down, final_norm, loss_target, m_ffn1_norm, m_ffn1_w_gate, m_ffn1_w_up, m_ffn1_w_down, m_mix_norm, m_w_in, m_gate_bias, m_rel_bias_table, m_ssm_a_re, m_ssm_a_im, m_ssm_log_dt, m_ssm_b_re, m_ssm_b_im, m_ssm_c_re, m_ssm_c_im, m_ssm_d, m_ssm_w_glu, m_w_attn_branch, m_w_ssm_branch, m_w_out, m_ffn2_norm, m_ffn2_w_gate, m_ffn2_w_up, m_ffn2_w_down, m_final_norm, v_ffn1_norm, v_ffn1_w_gate, v_ffn1_w_up, v_ffn1_w_down, v_mix_norm, v_w_in, v_gate_bias, v_rel_bias_table, v_ssm_a_re, v_ssm_a_im, v_ssm_log_dt, v_ssm_b_re, v_ssm_b_im, v_ssm_c_re, v_ssm_c_im, v_ssm_d, v_ssm_w_glu, v_w_attn_branch, v_w_ssm_branch, v_w_out, v_ffn2_norm, v_ffn2_w_gate, v_ffn2_w_up, v_ffn2_w_down, v_final_norm):
    given = dict(locals())
    shapes = {nm: given[nm].shape for nm in _ORDER}

    def strip(a):
        return a[0] if a.ndim >= 2 and a.shape[0] == 1 else a

    w = {nm: strip(given[nm]) for nm in _ORDER}
    m = {nm: strip(given["m_" + nm]) for nm in _ORDER}
    v = {nm: strip(given["v_" + nm]) for nm in _ORDER}
    for d in (w, m, v):
        d["rel_bias_table"] = d["rel_bias_table"].reshape(N_BUCKETS, N_GROUPS * HEADS_PER_GROUP)

    weight_phases = {"f1": _PHASES["f1gu"] + _PHASES["f1d"], "mix": _PHASES["mix"], "f2": _PHASES["f2"]}
    pending_w, w_rows, deps, zero = {}, {}, [], 0.0
    for phase, names in weight_phases.items():
        w_rows.update({nm: _to_rows(w[nm] + zero, nm) for nm in names})
        pending_w[phase] = _exchange_start(f"gather_{phase}_start", [w_rows[nm].astype(BF16) for nm in names],
                                           gather=True, deps=deps)
        deps = [pending_w[phase][4]]
        zero = pending_w["f1"][4][0, 0]
    m_rows = {nm: _to_rows(m[nm] + zero, nm) for nm in _SHARD_INFO}
    v_rows = {nm: _to_rows(v[nm] + zero, nm) for nm in _SHARD_INFO}
    small = {nm: w[nm] for nm in _SMALL}
    small_in = {nm: small[nm] + zero for nm in _SMALL}
    for nm in ("ffn1_norm", "mix_norm", "ffn2_norm", "gate_bias"):
        small_in[nm] = small_in[nm].reshape(1, -1)

    def weights_of(phase, after):
        if phase == "f1":
            after = list(after) + list(m_rows.values()) + list(v_rows.values())
        landed = _exchange_wait(f"gather_{phase}_wait", pending_w[phase], after, gather=True)
        return {nm: _full_weight(got, nm) for nm, got in zip(weight_phases[phase], landed)}

    pending_g = {}

    def send_grads(phase, grads):
        if phase == "small":
            gs_pack = _pack_small({nm: grads[nm].reshape(small[nm].shape) for nm in _SMALL}, last=grads["loss"])
            pending_g[phase] = _exchange_start("gather_small_start", [gs_pack], gather=True)
        else:
            pending_g[phase] = _exchange_start(f"scatter_{phase}_start",
                                               [_grad_blocks(grads[nm], nm) for nm in _PHASES[phase]], gather=False)
        return [pending_g[phase][4]]

    dx, gsmall = _local_step(_residue_order(x[0] + zero), _residue_order(loss_target[0] + zero), small_in,
                             weights_of, send_grads, first_deps=[pending_w["f2"][4]])
    dx = _token_order(dx)

    updated = {}
    after = pending_g["f1d"][4]
    for phase in ("f2", "mix", "small", "f1gu", "f1d"):
        landed = _exchange_wait(f"exchange_{phase}_wait", pending_g[phase], after, gather=phase == "small")
        if phase == "small":
            sm = _adamw("adamw_small", _pack_small(small), _pack_small({nm: m[nm] for nm in _SMALL}),
                        _pack_small({nm: v[nm] for nm in _SMALL}), landed[0], landed[0].shape[1])
            after = sm[0]
            continue
        for nm, recv in zip(_PHASES[phase], landed):
            tr = max(t for t in range(16, 353, 16) if w_rows[nm].shape[0] % t == 0)
            updated[nm] = _adamw(f"adamw_{nm}", w_rows[nm], m_rows[nm], v_rows[nm], recv, tr)
            after = updated[nm][0]

    loss = sm[0][-8, 0]
    outs = []
    for i in range(4):
        sml = _unpack_small(sm[i], small)
        outs.append([(_from_rows(updated[nm][i], nm) if nm in updated else sml[nm]).reshape(shapes[nm])
                     for nm in _ORDER])
    return (loss, dx[None], *outs[0], *outs[1], *outs[2], *outs[3])
```

```python
import math

import numpy as np
import jax
import jax.numpy as jnp
from jax import lax
from jax.experimental import pallas as pl
from jax.experimental.pallas import tpu as pltpu

F32 = jnp.float32
BF16 = jnp.bfloat16

N_DEV = 8
D_MODEL = 1024
D_FF = 2816
HEAD_DIM = 64
HEADS_PER_GROUP = 4
DILATIONS = (1, 4, 16)
N_GROUPS = 3
ATTN_WIDTH = 768
ATTN_OUT = 256
BLOCK = 128
N_BUCKETS = 32
MAX_DISTANCE = 2048
NEG_INF = -1e30
SSM_WIDTH = 512
SSM_GROUPS = 32
SSM_GROUP = 16
SSM_STATE = 64
SSM_LANES = SSM_GROUPS * SSM_STATE
SSM_PAIRS = SSM_GROUPS // 2
PAIR_LANES = 2 * SSM_STATE
PAIR_TILE = 256
EPS = 1e-6
LR, B1, B2, ADAM_EPS, WD, STEP = 0.001, 0.9, 0.999, 1e-08, 0.01, 10

VMEM_LIMIT_BYTES = 56 * 1024 * 1024
FFN_CHUNK = 768
SCAN_BLOCK = 256
SCAN_STEPS = 16
SCAN_COLS = SCAN_BLOCK // SCAN_STEPS
SCAN_SUB = 8
SCAN_LANES = 512

MESH = pl.DeviceIdType.MESH


def _cparams(*sem):
    return pltpu.CompilerParams(dimension_semantics=sem, vmem_limit_bytes=VMEM_LIMIT_BYTES)


def _dot(a, b, dims):
    return lax.dot_general(a, b, (dims, ((), ())), preferred_element_type=F32)


def _dot_nn(a, b):
    return _dot(a, b, ((1,), (0,)))


def _dot_nt(a, b):
    return _dot(a, b, ((1,), (1,)))


def _dot_tn(a, b):
    return _dot(a, b, ((0,), (0,)))


def _sigmoid(x):
    return 1.0 / (1.0 + jnp.exp(-x))


_HBM_SPEC = pl.BlockSpec(memory_space=pltpu.HBM)
_SEM_SPEC = pl.BlockSpec(memory_space=pltpu.SEMAPHORE)
_ANY_SPEC = pl.BlockSpec(memory_space=pl.ANY)
_EFFECT = pltpu.SideEffectType.DATAFLOW_SIDE_EFFECTING


def _peers(x, y, c):
    return [(1 - x if k & 4 else x, 1 - y if k & 2 else y, 1 - c if k & 1 else c) for k in range(1, N_DEV)]


def _exchange_copies(x_refs, land_refs, send_sems, recv_sems, gather):
    x, y, c = lax.axis_index("x"), lax.axis_index("y"), lax.axis_index("c")
    me = 4 * x + 2 * y + c
    copies = []
    for a, (x_ref, land_ref) in enumerate(zip(x_refs, land_refs)):
        for k, (px, py, pc) in enumerate(_peers(x, y, c)):
            src = x_ref if gather else x_ref.at[4 * px + 2 * py + pc]
            copies.append(pltpu.make_async_remote_copy(
                src_ref=src, dst_ref=land_ref.at[me], send_sem=send_sems.at[N_DEV * a + k],
                recv_sem=recv_sems.at[(N_DEV - 1) * a + k], device_id=(px, py, pc), device_id_type=MESH))
    owns = [pltpu.make_async_copy(x_ref if gather else x_ref.at[me], land_ref.at[me],
                                  send_sems.at[N_DEV * a + N_DEV - 1])
            for a, (x_ref, land_ref) in enumerate(zip(x_refs, land_refs))]
    return owns, copies


def _exchange_start(name, xs_list, gather, deps=()):
    n, nd = len(xs_list), len(deps)
    land_shapes = [(N_DEV, *xs.shape) if gather else xs.shape for xs in xs_list]

    def body(*refs):
        x_refs, land_refs = refs[:n], refs[n:2 * n]
        send_sems, recv_sems = refs[2 * n + nd:2 * n + nd + 2]
        token = refs[-1]
        owns, copies = _exchange_copies(x_refs, land_refs, send_sems, recv_sems, gather)
        for cp in copies + owns:
            cp.start()
        token[...] = jnp.zeros_like(token)

    hbm = lambda a: pltpu.with_memory_space_constraint(a, pltpu.HBM)
    outs = pl.pallas_call(
        body, name=name,
        out_shape=(pltpu.SemaphoreType.DMA((n * N_DEV,)), pltpu.SemaphoreType.DMA((n * (N_DEV - 1),)),
                   *[pltpu.HBM(xs.shape, xs.dtype) for xs in xs_list],
                   *[pltpu.HBM(shape, xs.dtype) for shape, xs in zip(land_shapes, xs_list)],
                   jax.ShapeDtypeStruct((8, 128), F32)),
        in_specs=(_HBM_SPEC,) * (2 * n) + (_ANY_SPEC,) * nd,
        out_specs=(_SEM_SPEC, _SEM_SPEC) + (_HBM_SPEC,) * (2 * n) + (pl.BlockSpec(memory_space=pltpu.VMEM),),
        input_output_aliases={i: 2 + i for i in range(2 * n)},
        compiler_params=pltpu.CompilerParams(has_side_effects=_EFFECT),
    )(*[hbm(xs) for xs in xs_list], *[hbm(lax.empty(shape, xs.dtype)) for shape, xs in zip(land_shapes, xs_list)],
      *deps)
    return outs[0], outs[1], list(outs[2:2 + n]), list(outs[2 + n:2 + 2 * n]), outs[-1]


def _exchange_wait(name, handle, after, gather):
    send_sems, recv_sems, xs_thru, lands_thru, _ = handle
    n = len(xs_thru)
    after = list(after) if isinstance(after, (list, tuple)) else [after]

    def body(*refs):
        x_refs, land_refs = refs[:n], refs[n:2 * n]
        send_sems, recv_sems = refs[2 * n:2 * n + 2]
        owns, copies = _exchange_copies(x_refs, land_refs, send_sems, recv_sems, gather)
        for cp in copies:
            cp.wait_send()
            cp.wait_recv()
        for cp in owns:
            cp.wait()

    outs = pl.pallas_call(
        body, name=name,
        out_shape=tuple(pltpu.HBM(a.shape, a.dtype) for a in xs_thru + lands_thru),
        in_specs=(_HBM_SPEC,) * (2 * n) + (_SEM_SPEC, _SEM_SPEC) + (_ANY_SPEC,) * len(after),
        out_specs=(_HBM_SPEC,) * (2 * n), input_output_aliases={i: i for i in range(2 * n)},
        compiler_params=pltpu.CompilerParams(has_side_effects=_EFFECT),
    )(*xs_thru, *lands_thru, send_sems, recv_sems, *after)
    return list(outs[n:])


def _mm(name, pairs, nt, n_cols, out_dtypes, epilogue=None, extras=(), tm=1024, tn=512, deps=(), row_sums=0,
        out_cols=None):
    rows = pairs[0][0].shape[0]
    tm = min(tm, rows)
    tn = min(tn, n_cols)
    na, ne, nd, no = len(pairs), len(extras), len(deps), len(out_dtypes)

    def body(*refs):
        a_refs, w_refs = refs[:na], refs[na:2 * na]
        e_refs, o_refs = refs[2 * na:2 * na + ne], refs[2 * na + ne + nd:]
        acc = None
        for a_ref, w_ref in zip(a_refs, w_refs):
            a = a_ref[...].astype(BF16)
            w = w_ref[...].astype(BF16)
            p = _dot_nt(a, w) if nt else _dot_nn(a, w)
            acc = p if acc is None else acc + p
        outs = (acc,) if epilogue is None else epilogue(acc, *[e[...] for e in e_refs])
        for o_ref, o in zip(o_refs[:no], outs[:no]):
            o_ref[...] = o.astype(o_ref.dtype)
        for r_ref, o in zip(o_refs[no:], outs[no:]):
            @pl.when(pl.program_id(0) == 0)
            def _():
                r_ref[...] = jnp.zeros_like(r_ref)

            r_ref[...] += o

    in_specs = [pl.BlockSpec((tm, a.shape[1]), lambda i, j: (i, 0)) for a, _ in pairs]
    for _, w in pairs:
        if nt:
            in_specs.append(pl.BlockSpec((tn, w.shape[1]), lambda i, j: (j, 0)))
        else:
            in_specs.append(pl.BlockSpec((w.shape[0], tn), lambda i, j: (0, j)))
    for e, col_off in extras:
        off = col_off // tn
        if e.shape[0] == 1:
            in_specs.append(pl.BlockSpec((1, tn), lambda i, j, off=off: (0, j + off)))
        else:
            in_specs.append(pl.BlockSpec((tm, tn), lambda i, j, off=off: (i, j + off)))
    in_specs += [_ANY_SPEC] * nd
    if out_cols is None:
        out_cols = [n_cols] * no
    else:
        assert tn == n_cols, "outputs of other widths need the whole row in one block"
    assert not row_sums or tn == n_cols
    out_specs = [pl.BlockSpec((tm, tn * c // n_cols), lambda i, j: (i, j)) for c in out_cols]
    out_specs += [pl.BlockSpec((1, tn), lambda i, j: (0, j))] * row_sums
    out_shape = [jax.ShapeDtypeStruct((rows, c), dt) for c, dt in zip(out_cols, out_dtypes)]
    out_shape += [jax.ShapeDtypeStruct((1, n_cols), F32)] * row_sums
    outs = pl.pallas_call(
        body, name=name, grid=(rows // tm, n_cols // tn),
        in_specs=in_specs, out_specs=out_specs, out_shape=out_shape,
        compiler_params=_cparams("arbitrary" if row_sums else "parallel", "arbitrary"),
    )(*[a for a, _ in pairs], *[w for _, w in pairs], *[e for e, _ in extras], *deps)
    return outs


def _tn_rows(m):
    return max(b for b in range(128, min(m, 1408) + 1, 128) if m % b == 0)


def _mm_tn(name, a, b, scale=1.0, bm=None, tk=1024, deps=(), out_dtype=F32):
    rows, m = a.shape
    n = b.shape[1]
    bm = _tn_rows(m) if bm is None else bm
    tk = min(tk, rows)
    nk = rows // tk

    def body(a_ref, b_ref, *rest):
        o_ref, acc_ref = rest[-2:]
        k = pl.program_id(1)

        @pl.when(k == 0)
        def _():
            acc_ref[...] = jnp.zeros_like(acc_ref)

        acc_ref[...] += _dot_tn(a_ref[...].astype(BF16), b_ref[...].astype(BF16))

        @pl.when(k == nk - 1)
        def _():
            o_ref[...] = (acc_ref[...] * scale).astype(o_ref.dtype)

    return pl.pallas_call(
        body, name=name, grid=(m // bm, nk),
        in_specs=[pl.BlockSpec((tk, bm), lambda i, k: (k, i)), pl.BlockSpec((tk, n), lambda i, k: (k, 0))]
        + [_ANY_SPEC] * len(deps),
        out_specs=pl.BlockSpec((bm, n), lambda i, k: (i, 0)),
        out_shape=jax.ShapeDtypeStruct((m, n), out_dtype),
        scratch_shapes=[pltpu.VMEM((bm, n), F32)],
        compiler_params=_cparams("parallel", "arbitrary"),
    )(a, b, *deps)


def _mm_tn_stack(name, a_list, b, tk=1024, out_dtype=F32):
    rows, n = b.shape
    ms = [a.shape[1] for a in a_list]
    tk = min(tk, rows)
    nk = rows // tk
    na = len(a_list)

    def body(*refs):
        a_refs, b_ref, o_ref, acc_ref = refs[:na], refs[na], refs[na + 1], refs[na + 2]
        k = pl.program_id(0)

        @pl.when(k == 0)
        def _():
            acc_ref[...] = jnp.zeros_like(acc_ref)

        bv = b_ref[...].astype(BF16)
        r0 = 0
        for a_ref, m in zip(a_refs, ms):
            acc_ref[r0:r0 + m, :] += _dot_tn(a_ref[...].astype(BF16), bv)
            r0 += m

        @pl.when(k == nk - 1)
        def _():
            o_ref[...] = acc_ref[...].astype(o_ref.dtype)

    return pl.pallas_call(
        body, name=name, grid=(nk,),
        in_specs=[pl.BlockSpec((tk, m), lambda k: (k, 0)) for m in ms] + [pl.BlockSpec((tk, n), lambda k: (k, 0))],
        out_specs=pl.BlockSpec((sum(ms), n), lambda k: (0, 0)),
        out_shape=jax.ShapeDtypeStruct((sum(ms), n), out_dtype),
        scratch_shapes=[pltpu.VMEM((sum(ms), n), F32)],
        compiler_params=_cparams("arbitrary"),
    )(*a_list, b)


def _colsum(name, xs, tm=512):
    rows, cols = xs.shape
    tm = min(tm, rows)

    def body(x_ref, o_ref):
        @pl.when(pl.program_id(0) == 0)
        def _():
            o_ref[...] = jnp.zeros_like(o_ref)

        o_ref[...] += jnp.sum(x_ref[...].astype(F32), axis=0, keepdims=True)

    return pl.pallas_call(
        body, name=name, grid=(rows // tm,),
        in_specs=[pl.BlockSpec((tm, cols), lambda i: (i, 0))],
        out_specs=pl.BlockSpec((1, cols), lambda i: (0, 0)),
        out_shape=jax.ShapeDtypeStruct((1, cols), F32),
        compiler_params=_cparams("arbitrary"),
    )(xs)


def _ew(name, fn, ins, out_cols, out_dtypes, tm=512):
    rows = ins[0].shape[0]
    tm = min(tm, rows)
    ni = len(ins)

    def body(*refs):
        outs = fn(*[r[...] for r in refs[:ni]])
        for o_ref, o in zip(refs[ni:], outs):
            o_ref[...] = o.astype(o_ref.dtype)

    def spec(shape):
        if shape[0] == 1:
            return pl.BlockSpec((1, shape[1]), lambda i: (0, 0))
        return pl.BlockSpec((tm, shape[1]), lambda i: (i, 0))

    return pl.pallas_call(
        body, name=name, grid=(rows // tm,),
        in_specs=[spec(a.shape) for a in ins],
        out_specs=[pl.BlockSpec((tm, c), lambda i: (i, 0)) for c in out_cols],
        out_shape=[jax.ShapeDtypeStruct((rows, c), dt) for c, dt in zip(out_cols, out_dtypes)],
        compiler_params=_cparams("parallel"),
    )(*ins)


def _rms_parts(xv):
    r = lax.rsqrt(jnp.mean(xv * xv, axis=-1, keepdims=True) + EPS)
    return r, xv * r


def _rms_bwd_dx(dh, gain, r, xh):
    dxh = dh * gain
    return r * (dxh - xh * jnp.mean(dxh * xh, axis=-1, keepdims=True))


def _ffn_chunks(f_all):
    return [slice(c, min(c + FFN_CHUNK, f_all)) for c in range(0, f_all, FFN_CHUNK)]


def _loss_head(xo, gain_f, target, d):
    r, xh = _rms_parts(xo)
    err = xh * gain_f - target
    dy = err * (1.0 / d)
    per_tok = jnp.mean(err * err, axis=-1, keepdims=True)
    return (_rms_bwd_dx(dy, gain_f, r, xh), jnp.sum(dy * xh, axis=0, keepdims=True),
            0.5 * jnp.sum(per_tok, axis=0, keepdims=True))


def _ffn_tile(x_ref, g_ref, wg_ref, wu_ref, wd_ref, h_ref, gg_ref, uu_ref):
    xv = x_ref[...]
    _, xh = _rms_parts(xv)
    h = (xh * g_ref[...]).astype(BF16)
    h_ref[...] = h
    acc = None
    for cols in _ffn_chunks(wd_ref.shape[0]):
        gg = _dot_nt(h, wg_ref[cols, :])
        uu = _dot_nt(h, wu_ref[cols, :])
        act = gg * _sigmoid(gg) * uu
        part = _dot_nn(act.astype(BF16), wd_ref[cols, :])
        acc = part if acc is None else acc + part
        gg_ref[:, cols] = gg.astype(BF16)
        uu_ref[:, cols] = uu.astype(BF16)
    return xv + 0.5 * acc


def _ffn_fwd(name, xs, gain, wg_t, wu_t, wd, next_gain, tm=512, deps=()):
    rows, d = xs.shape
    f_all = wd.shape[0]
    tm = min(tm, rows)

    def body(x_ref, g_ref, wg_ref, wu_ref, wd_ref, ng_ref, *rest):
        xo_ref, h_ref, gg_ref, uu_ref, hn_ref = rest[-5:]
        xo = _ffn_tile(x_ref, g_ref, wg_ref, wu_ref, wd_ref, h_ref, gg_ref, uu_ref)
        xo_ref[...] = xo
        hn_ref[...] = (_rms_parts(xo)[1] * ng_ref[...]).astype(BF16)

    tile = pl.BlockSpec((tm, d), lambda i: (i, 0))
    row = pl.BlockSpec((1, d), lambda i: (0, 0))
    wspec = pl.BlockSpec((f_all, d), lambda i: (0, 0), pipeline_mode=pl.Buffered(1))
    hid = pl.BlockSpec((tm, f_all), lambda i: (i, 0))
    return pl.pallas_call(
        body, name=name, grid=(rows // tm,),
        in_specs=[tile, row, wspec, wspec, wspec, row] + [_ANY_SPEC] * len(deps),
        out_specs=[tile, tile, hid, hid, tile],
        out_shape=[jax.ShapeDtypeStruct((rows, d), F32), jax.ShapeDtypeStruct((rows, d), BF16),
                   jax.ShapeDtypeStruct((rows, f_all), BF16), jax.ShapeDtypeStruct((rows, f_all), BF16),
                   jax.ShapeDtypeStruct((rows, d), BF16)],
        compiler_params=_cparams("parallel"),
    )(xs, gain, wg_t, wu_t, wd, next_gain, *deps)


def _ffn_fwd_head(name, xs, gain, wg_t, wu_t, wd, gain_f, target, tm=512):
    rows, d = xs.shape
    f_all = wd.shape[0]
    tm = min(tm, rows)

    def body(x_ref, g_ref, wg_ref, wu_ref, wd_ref, gf_ref, t_ref, dxo_ref, h_ref, gg_ref, uu_ref, dgf_ref, loss_ref):
        xo = _ffn_tile(x_ref, g_ref, wg_ref, wu_ref, wd_ref, h_ref, gg_ref, uu_ref)
        dxo, dgf, loss = _loss_head(xo, gf_ref[...], t_ref[...], d)
        dxo_ref[...] = dxo

        @pl.when(pl.program_id(0) == 0)
        def _():
            dgf_ref[...] = jnp.zeros_like(dgf_ref)
            loss_ref[...] = jnp.zeros_like(loss_ref)

        dgf_ref[...] += dgf
        loss_ref[...] += loss

    tile = pl.BlockSpec((tm, d), lambda i: (i, 0))
    row = pl.BlockSpec((1, d), lambda i: (0, 0))
    wspec = pl.BlockSpec((f_all, d), lambda i: (0, 0), pipeline_mode=pl.Buffered(1))
    hid = pl.BlockSpec((tm, f_all), lambda i: (i, 0))
    return pl.pallas_call(
        body, name=name, grid=(rows // tm,),
        in_specs=[tile, row, wspec, wspec, wspec, row, tile],
        out_specs=[tile, tile, hid, hid, row, pl.BlockSpec((1, 1), lambda i: (0, 0))],
        out_shape=[jax.ShapeDtypeStruct((rows, d), F32), jax.ShapeDtypeStruct((rows, d), BF16),
                   jax.ShapeDtypeStruct((rows, f_all), BF16), jax.ShapeDtypeStruct((rows, f_all), BF16),
                   jax.ShapeDtypeStruct((1, d), F32), jax.ShapeDtypeStruct((1, 1), F32)],
        compiler_params=_cparams("arbitrary"),
    )(xs, gain, wg_t, wu_t, wd, gain_f, target)


def _ffn_bwd(name, dxo, xs, gain, gg_all, uu_all, wg_t, wu_t, wd, tm=256):
    rows, d = xs.shape
    f_all = wd.shape[0]
    tm = min(tm, rows)

    def body(dxo_ref, x_ref, g_ref, gg_ref, uu_ref, wg_ref, wu_ref, wd_ref,
             dx_ref, dgg_ref, duu_ref, act_ref, dgain_ref):
        dxo = dxo_ref[...]
        df = (0.5 * dxo).astype(BF16)
        dh = None
        for cols in _ffn_chunks(f_all):
            gg = gg_ref[:, cols].astype(F32)
            uu = uu_ref[:, cols].astype(F32)
            sg = _sigmoid(gg)
            silu = gg * sg
            dact = _dot_nt(df, wd_ref[cols, :])
            duu = (dact * silu).astype(BF16)
            dgg = (dact * uu * (sg * (1.0 + gg * (1.0 - sg)))).astype(BF16)
            act_ref[:, cols] = (silu * uu).astype(BF16)
            dgg_ref[:, cols] = dgg
            duu_ref[:, cols] = duu
            part = _dot_nn(dgg, wg_ref[cols, :]) + _dot_nn(duu, wu_ref[cols, :])
            dh = part if dh is None else dh + part
        r, xh = _rms_parts(x_ref[...])
        dx_ref[...] = dxo + _rms_bwd_dx(dh, g_ref[...], r, xh)

        @pl.when(pl.program_id(0) == 0)
        def _():
            dgain_ref[...] = jnp.zeros_like(dgain_ref)

        dgain_ref[...] += jnp.sum(dh * xh, axis=0, keepdims=True)

    tile = pl.BlockSpec((tm, d), lambda i: (i, 0))
    row = pl.BlockSpec((1, d), lambda i: (0, 0))
    wspec = pl.BlockSpec((f_all, d), lambda i: (0, 0), pipeline_mode=pl.Buffered(1))
    hid = pl.BlockSpec((tm, f_all), lambda i: (i, 0))
    hid_shape = jax.ShapeDtypeStruct((rows, f_all), BF16)
    return pl.pallas_call(
        body, name=name, grid=(rows // tm,),
        in_specs=[tile, tile, row, hid, hid, wspec, wspec, wspec],
        out_specs=[tile, hid, hid, hid, row],
        out_shape=[jax.ShapeDtypeStruct((rows, d), F32), hid_shape, hid_shape, hid_shape,
                   jax.ShapeDtypeStruct((1, d), F32)],
        compiler_params=_cparams("arbitrary"),
    )(dxo, xs, gain, gg_all, uu_all, wg_t, wu_t, wd)


def _t5_bucket_np(dist):
    max_exact = N_BUCKETS // 2
    dd = np.maximum(dist, 1).astype(np.float32)
    large = max_exact + (np.log(dd / np.float32(max_exact)) / np.float32(math.log(MAX_DISTANCE / max_exact))
                         * np.float32(N_BUCKETS - max_exact)).astype(np.int32)
    large = np.minimum(large, N_BUCKETS - 1)
    return np.where(dist < max_exact, dist, large).astype(np.int32)


def _attn_geometry(g, rows):
    run = rows // 16
    dil = DILATIONS[g]
    if dil == 16:
        bq = BLOCK
        return dict(view=(16, run), block=(None, bq), grid=(16, run // bq), index=lambda r, n: (r, n),
                    pos=np.arange(bq), bq=bq)
    if dil == 4:
        per = BLOCK // 4
        pos = (4 * np.arange(per)[None, :] + np.arange(4)[:, None]).reshape(-1)
        return dict(view=(4, 4, run), block=(4, None, per), grid=(4, run // per), index=lambda r, n: (0, r, n),
                    pos=pos, bq=BLOCK)
    per = 16
    pos = (16 * np.arange(per)[None, :] + np.arange(16)[:, None]).reshape(-1)
    return dict(view=(16, run), block=(16, per), grid=(1, run // per), index=lambda r, n: (0, n),
                pos=pos, bq=16 * per)


def _attn_tables(g, rows):
    geo = _attn_geometry(g, rows)
    pos, bq = geo["pos"], geo["bq"]
    steps = pos[:, None] - np.concatenate([pos - bq, pos])[None, :]
    valid = (steps >= 0) & (steps <= BLOCK)
    bucket = _t5_bucket_np((np.maximum(steps, 0) * DILATIONS[g]).astype(np.int32))
    return bucket, valid.astype(np.int32)


def _bias_fwd(name, bucket, valid, table_t):
    bq = bucket.shape[0]

    def body(bk_ref, ok_ref, tab_ref, o_ref):
        bk = bk_ref[...]
        ok = ok_ref[...] > 0
        accs = [jnp.zeros(bk.shape, F32)] * HEADS_PER_GROUP
        for b in range(N_BUCKETS):
            hit = bk == b
            accs = [jnp.where(hit, tab_ref[h, b], acc) for h, acc in enumerate(accs)]
        for h, acc in enumerate(accs):
            o_ref[h] = jnp.where(ok, acc, NEG_INF)

    vm = pl.BlockSpec(memory_space=pltpu.VMEM)
    return pl.pallas_call(
        body, name=name, in_specs=[vm, vm, pl.BlockSpec(memory_space=pltpu.SMEM)], out_specs=vm,
        out_shape=jax.ShapeDtypeStruct((HEADS_PER_GROUP, bq, 2 * bq), F32),
    )(bucket, valid, table_t)


def _bias_bwd(name, bucket, dbias):
    def body(bk_ref, db_ref, o_ref):
        row_id = lax.broadcasted_iota(jnp.int32, (N_BUCKETS, 128), 0)
        col_id = lax.broadcasted_iota(jnp.int32, (N_BUCKETS, 128), 1)
        bk = bk_ref[...]
        acc = jnp.zeros((N_BUCKETS, 128), F32)
        for h in range(HEADS_PER_GROUP):
            db = db_ref[h]
            for b in range(N_BUCKETS):
                part = jnp.sum(jnp.where(bk == b, db, 0.0), axis=0, keepdims=True)
                tot = jnp.sum(part, axis=1, keepdims=True)
                acc = jnp.where((row_id == b) & (col_id == h), tot, acc)
        o_ref[...] = acc

    vm = pl.BlockSpec(memory_space=pltpu.VMEM)
    return pl.pallas_call(body, name=name, in_specs=[vm, vm], out_specs=vm,
                          out_shape=jax.ShapeDtypeStruct((N_BUCKETS, 128), F32))(bucket, dbias)


def _head_of_lane(nrows):
    return lax.broadcasted_iota(jnp.int32, (nrows, ATTN_OUT), 1) // HEAD_DIM


def _stack_heads(a, lane_head):
    zero = jnp.zeros_like(a)
    return jnp.concatenate([jnp.where(lane_head == h, a, zero) for h in range(HEADS_PER_GROUP)], axis=0)


def _unstack_heads(a4, lane_head, bq):
    out = a4[:bq]
    for h in range(1, HEADS_PER_GROUP):
        out = jnp.where(lane_head == h, a4[h * bq:(h + 1) * bq], out)
    return out


def _attn_specs(geo, cols, col_block, index):
    return pl.BlockSpec(geo["block"] + (cols,), lambda r, n: index(r, n) + (col_block,))


def _attn_fwd(name, qkv, g, bias4):
    rows = qkv.shape[0]
    geo = _attn_geometry(g, rows)
    bq, (nsub, nb), index = geo["bq"], geo["grid"], geo["index"]
    blk_shape = tuple(b for b in geo["block"] if b is not None) + (ATTN_OUT,)

    def body(q_ref, kc_ref, kp_ref, vc_ref, vp_ref, b_ref, o_ref, lse_ref):
        n = pl.program_id(1)
        lane_head = _head_of_lane(bq)
        flat = lambda ref: ref[...].reshape(bq, ATTN_OUT)
        q4 = _stack_heads(flat(q_ref), lane_head)
        k2 = jnp.concatenate([flat(kp_ref), flat(kc_ref)], axis=0)
        v2 = jnp.concatenate([flat(vp_ref), flat(vc_ref)], axis=0)
        s = _dot_nt(q4, k2) + b_ref[...]
        col = lax.broadcasted_iota(jnp.int32, s.shape, 1)
        s = jnp.where((col >= bq) | (n > 0), s, NEG_INF)
        mx = jnp.max(s, axis=-1, keepdims=True)
        p = jnp.exp(s - mx)
        den = jnp.sum(p, axis=-1, keepdims=True)
        o4 = _dot_nn(p.astype(BF16), v2) / den
        lse4 = jnp.broadcast_to(mx + jnp.log(den), (HEADS_PER_GROUP * bq, ATTN_OUT))
        o_ref[...] = _unstack_heads(o4, lane_head, bq).reshape(blk_shape)
        lse_ref[...] = _unstack_heads(lse4, lane_head, bq).reshape(blk_shape)

    prev = lambda r, n: index(r, jnp.maximum(n - 1, 0))
    view = lambda a: a.reshape(geo["view"] + (a.shape[1],))
    qkv_v = view(qkv)
    out_spec = _attn_specs(geo, ATTN_OUT, 0, index)
    out_shape = jax.ShapeDtypeStruct(geo["view"] + (ATTN_OUT,), F32)
    o, lse = pl.pallas_call(
        body, name=name, grid=(nsub, nb),
        in_specs=[_attn_specs(geo, ATTN_OUT, g, index), _attn_specs(geo, ATTN_OUT, 3 + g, index),
                  _attn_specs(geo, ATTN_OUT, 3 + g, prev), _attn_specs(geo, ATTN_OUT, 6 + g, index),
                  _attn_specs(geo, ATTN_OUT, 6 + g, prev), pl.BlockSpec(bias4.shape, lambda r, n: (0, 0))],
        out_specs=[out_spec, out_spec], out_shape=[out_shape, out_shape],
        compiler_params=_cparams("parallel", "arbitrary"),
    )(qkv_v, qkv_v, qkv_v, qkv_v, qkv_v, bias4)
    return o.reshape(rows, ATTN_OUT), lse.reshape(rows, ATTN_OUT)


def _attn_bwd(name, qkv, do, lse, cvec, g, bias4):
    rows = qkv.shape[0]
    geo = _attn_geometry(g, rows)
    bq, (nsub, nb), index = geo["bq"], geo["grid"], geo["index"]
    blk_shape = tuple(b for b in geo["block"] if b is not None) + (ATTN_OUT,)
    nlead = len(blk_shape) - 1

    def body(q_ref, kc_ref, kp_ref, vc_ref, vp_ref, do_ref, lse_ref, c_ref, b_ref,
             dq_ref, dk_ref, dv_ref, db_ref, kcar_ref, vcar_ref):
        r, n = pl.program_id(0), pl.program_id(1)
        valid = n < nb
        lane_head = _head_of_lane(bq)
        flat = lambda ref: ref[...].reshape(bq, ATTN_OUT)

        @pl.when((r == 0) & (n == 0))
        def _():
            kcar_ref[...] = jnp.zeros_like(kcar_ref)
            vcar_ref[...] = jnp.zeros_like(vcar_ref)
            db_ref[...] = jnp.zeros_like(db_ref)

        def column(ref, h):
            lead = (slice(None),) * nlead
            return ref[lead + (pl.ds(h * HEAD_DIM, 1),)].reshape(bq, 1)

        q4 = _stack_heads(flat(q_ref), lane_head)
        do4 = _stack_heads(flat(do_ref), lane_head)
        k2 = jnp.concatenate([flat(kp_ref), flat(kc_ref)], axis=0)
        v2 = jnp.concatenate([flat(vp_ref), flat(vc_ref)], axis=0)
        lse4 = jnp.concatenate([column(lse_ref, h) for h in range(HEADS_PER_GROUP)], axis=0)
        c4 = jnp.concatenate([column(c_ref, h) for h in range(HEADS_PER_GROUP)], axis=0)
        s = _dot_nt(q4, k2) + b_ref[...]
        col = lax.broadcasted_iota(jnp.int32, s.shape, 1)
        keep = ((col >= bq) | (n > 0)) & valid
        p = jnp.where(keep, jnp.exp(s - lse4), 0.0)
        ds = p * (_dot_nt(do4, v2) + c4)
        ds_b = ds.astype(BF16)

        @pl.when(valid)
        def _():
            dq = _unstack_heads(_dot_nn(ds_b, k2), lane_head, bq) * (HEAD_DIM ** -0.5)
            dq_ref[...] = dq.astype(BF16).reshape(blk_shape)

        dk2 = _dot_tn(ds_b, q4)
        dv2 = _dot_tn(p.astype(BF16), do4)
        dk_ref[...] = (kcar_ref[...] + dk2[:bq]).astype(BF16).reshape(blk_shape)
        dv_ref[...] = (vcar_ref[...] + dv2[:bq]).astype(BF16).reshape(blk_shape)
        kcar_ref[...] = dk2[bq:]
        vcar_ref[...] = dv2[bq:]
        db_ref[...] += ds

    cur = lambda r, n: index(r, jnp.minimum(n, nb - 1))
    prev = lambda r, n: index(r, jnp.maximum(jnp.minimum(n, nb - 1) - 1, 0))
    late = lambda r, n: index(r, jnp.maximum(n - 1, 0))
    view = lambda a: a.reshape(geo["view"] + (a.shape[1],))
    qkv_v = view(qkv)
    tile = _attn_specs(geo, ATTN_OUT, 0, cur)
    bias_spec = pl.BlockSpec(bias4.shape, lambda r, n: (0, 0))
    out_shape = jax.ShapeDtypeStruct(geo["view"] + (ATTN_OUT,), BF16)
    dq, dk, dv, db = pl.pallas_call(
        body, name=name, grid=(nsub, nb + 1),
        in_specs=[_attn_specs(geo, ATTN_OUT, g, cur), _attn_specs(geo, ATTN_OUT, 3 + g, cur),
                  _attn_specs(geo, ATTN_OUT, 3 + g, prev), _attn_specs(geo, ATTN_OUT, 6 + g, cur),
                  _attn_specs(geo, ATTN_OUT, 6 + g, prev), tile, tile, tile, bias_spec],
        out_specs=[tile, _attn_specs(geo, ATTN_OUT, 0, late), _attn_specs(geo, ATTN_OUT, 0, late), bias_spec],
        out_shape=[out_shape, out_shape, out_shape, jax.ShapeDtypeStruct(bias4.shape, F32)],
        scratch_shapes=[pltpu.VMEM((bq, ATTN_OUT), F32), pltpu.VMEM((bq, ATTN_OUT), F32)],
        compiler_params=_cparams("arbitrary", "arbitrary"),
    )(qkv_v, qkv_v, qkv_v, qkv_v, qkv_v, view(do), view(lse), view(cvec), bias4)
    return dq.reshape(rows, ATTN_OUT), dk.reshape(rows, ATTN_OUT), dv.reshape(rows, ATTN_OUT), db


def _group_weights(lses):
    mx = jnp.maximum(jnp.maximum(lses[0], lses[1]), lses[2])
    es = [jnp.exp(l - mx) for l in lses]
    den = es[0] + es[1] + es[2]
    return [e / den for e in es]


def _combine_fwd(name, os_, lses):
    def fn(o0, o1, o2, l0, l1, l2):
        ws = _group_weights([l0, l1, l2])
        out = ws[0] * o0 + ws[1] * o1 + ws[2] * o2
        return out, out

    return _ew(name, fn, [*os_, *lses], [ATTN_OUT, ATTN_OUT], [F32, BF16], tm=1024)


def _combine_bwd(name, do, oa, lses):
    def fn(dov, oav, l0, l1, l2):
        head_sum = (lax.broadcasted_iota(jnp.int32, (ATTN_OUT, ATTN_OUT), 0) // HEAD_DIM
                    == lax.broadcasted_iota(jnp.int32, (ATTN_OUT, ATTN_OUT), 1) // HEAD_DIM)
        ws = _group_weights([l0, l1, l2])
        prod = dov * oav
        hi = prod.astype(BF16)
        lo = (prod - hi.astype(F32)).astype(BF16)
        ones = jnp.where(head_sum, 1.0, 0.0).astype(BF16)
        bar = _dot_nn(hi, ones) + _dot_nn(lo, ones)
        return tuple(w * dov for w in ws) + tuple(-w * bar for w in ws)

    return _ew(name, fn, [do, oa, *lses], [ATTN_OUT] * 6, [BF16] * 3 + [F32] * 3, tm=1024)


def _ssm_disc(a_re, a_im, log_dt, b_re, b_im):
    dt = jnp.exp(log_dt)
    mag = jnp.exp(a_re * dt)
    ab_re = mag * jnp.cos(a_im * dt)
    ab_im = mag * jnp.sin(a_im * dt)
    den = a_re * a_re + a_im * a_im
    xr = ab_re - 1.0
    coef_re = (xr * a_re + ab_im * a_im) / den
    coef_im = (ab_im * a_re - xr * a_im) / den
    bb_re = coef_re[None] * b_re - coef_im[None] * b_im
    bb_im = coef_re[None] * b_im + coef_im[None] * b_re
    return ab_re, ab_im, bb_re, bb_im


def _ssm_params_fwd(name, a_re, a_im, log_dt, b_re, b_im):
    pows = jax.ShapeDtypeStruct((SCAN_STEPS,) + a_re.shape, F32)
    cgn = jax.ShapeDtypeStruct(b_re.shape, F32)

    def body(ar, ai, ld, br, bi, o_pr, o_pi, o_bbr, o_bbi):
        ab_re, ab_im, bb_re, bb_im = _ssm_disc(ar[...], ai[...], ld[...], br[...], bi[...])
        pr, pi = ab_re, ab_im
        for j in range(SCAN_STEPS):
            o_pr[j] = pr
            o_pi[j] = pi
            pr, pi = pr * ab_re - pi * ab_im, pr * ab_im + pi * ab_re
        o_bbr[...] = bb_re
        o_bbi[...] = bb_im

    vm = pl.BlockSpec(memory_space=pltpu.VMEM)
    return pl.pallas_call(body, name=name, in_specs=[vm] * 5, out_specs=[vm] * 4,
                          out_shape=[pows, pows, cgn, cgn])(a_re, a_im, log_dt, b_re, b_im)


def _ssm_params_bwd(name, a_re, a_im, log_dt, b_re, b_im, d_ab_re, d_ab_im, d_bb_re, d_bb_im):
    gn = jax.ShapeDtypeStruct(a_re.shape, F32)
    cgn = jax.ShapeDtypeStruct(b_re.shape, F32)

    def body(ar, ai, ld, br, bi, g0, g1, g2, g3, o_ar, o_ai, o_ld, o_br, o_bi):
        _, vjp = jax.vjp(_ssm_disc, ar[...], ai[...], ld[...], br[...], bi[...])
        outs = vjp((g0[...], g1[...], g2[...], g3[...]))
        for o_ref, o in zip((o_ar, o_ai, o_ld, o_br, o_bi), outs):
            o_ref[...] = o

    vm = pl.BlockSpec(memory_space=pltpu.VMEM)
    return pl.pallas_call(body, name=name, in_specs=[vm] * 9, out_specs=[vm] * 5,
                          out_shape=[gn, gn, jax.ShapeDtypeStruct(log_dt.shape, F32), cgn, cgn],
                          )(a_re, a_im, log_dt, b_re, b_im, d_ab_re, d_ab_im, d_bb_re, d_bb_im)


def _scan_block(s_ref, carry_ref, tmp_ref, pw_ref, reverse, sprev=None):
    nl = SSM_LANES
    halves = range(SCAN_COLS // SCAN_SUB)
    zero = jnp.zeros((SCAN_SUB, SCAN_LANES), F32)
    for half in (reversed(halves) if reverse else halves):
        sub_rows = pl.ds(half * SCAN_SUB, SCAN_SUB)
        for lc in range(nl // SCAN_LANES):
            re_l = pl.ds(lc * SCAN_LANES, SCAN_LANES)
            im_l = pl.ds(nl + lc * SCAN_LANES, SCAN_LANES)
            are, aim = pw_ref[0, :, re_l], pw_ref[0, :, im_l]

            def step_of(j):
                return SCAN_STEPS - 1 - j if reverse else j

            def pass1(j, st):
                sr, si = st
                jj = step_of(j)
                nr = are * sr - aim * si + s_ref[jj, sub_rows, re_l]
                ni = are * si + aim * sr + s_ref[jj, sub_rows, im_l]
                s_ref[jj, sub_rows, re_l] = nr
                s_ref[jj, sub_rows, im_l] = ni
                return nr, ni

            er, ei = lax.fori_loop(0, SCAN_STEPS, pass1, (zero, zero), unroll=2)
            tmp_ref[0:SCAN_SUB, re_l] = er
            tmp_ref[0:SCAN_SUB, im_l] = ei
            apr, api = pw_ref[SCAN_STEPS - 1, 0:1, re_l], pw_ref[SCAN_STEPS - 1, 0:1, im_l]
            sr, si = carry_ref[0:1, re_l], carry_ref[0:1, im_l]
            for step in range(SCAN_SUB):
                c = SCAN_SUB - 1 - step if reverse else step
                tmp_ref[SCAN_SUB + c:SCAN_SUB + c + 1, re_l] = sr
                tmp_ref[SCAN_SUB + c:SCAN_SUB + c + 1, im_l] = si
                e_r, e_i = tmp_ref[c:c + 1, re_l], tmp_ref[c:c + 1, im_l]
                sr, si = apr * sr - api * si + e_r, apr * si + api * sr + e_i
            carry_ref[0:1, re_l] = sr
            carry_ref[0:1, im_l] = si
            cr = tmp_ref[SCAN_SUB:2 * SCAN_SUB, re_l]
            ci = tmp_ref[SCAN_SUB:2 * SCAN_SUB, im_l]

            if sprev is None:
                def pass2(j, st):
                    pr, pi = pw_ref[j, :, re_l], pw_ref[j, :, im_l]
                    jj = step_of(j)
                    s_ref[jj, sub_rows, re_l] += pr * cr - pi * ci
                    s_ref[jj, sub_rows, im_l] += pr * ci + pi * cr
                    return st

                lax.fori_loop(0, SCAN_STEPS, pass2, 0, unroll=2)
            else:
                st_ref, prev_ref, have_prev, dab_ref = sprev

                def corrected(jj, pr, pi):
                    gr = s_ref[jj, sub_rows, re_l] + pr * cr - pi * ci
                    gi = s_ref[jj, sub_rows, im_l] + pr * ci + pi * cr
                    s_ref[jj, sub_rows, re_l] = gr
                    s_ref[jj, sub_rows, im_l] = gi
                    return gr, gi

                def pass2(j, st):
                    dr, di = st
                    jj = SCAN_STEPS - 1 - j
                    gr, gi = corrected(jj, pw_ref[j, :, re_l], pw_ref[j, :, im_l])
                    qr, qi = st_ref[jj - 1, sub_rows, re_l], st_ref[jj - 1, sub_rows, im_l]
                    return dr + gr * qr + gi * qi, di + gi * qr - gr * qi

                dr, di = lax.fori_loop(0, SCAN_STEPS - 1, pass2, (zero, zero), unroll=2)
                gr, gi = corrected(0, pw_ref[SCAN_STEPS - 1, :, re_l], pw_ref[SCAN_STEPS - 1, :, im_l])
                sub = lax.broadcasted_iota(jnp.int32, (SCAN_SUB, SCAN_LANES), 0)
                if half == 0:
                    pv_r = prev_ref[SCAN_SUB - 1:SCAN_SUB, re_l] * have_prev
                    pv_i = prev_ref[SCAN_SUB - 1:SCAN_SUB, im_l] * have_prev
                else:
                    before = pl.ds(half * SCAN_SUB - 1, 1)
                    pv_r, pv_i = st_ref[SCAN_STEPS - 1, before, re_l], st_ref[SCAN_STEPS - 1, before, im_l]
                shape = (SCAN_SUB, SCAN_LANES)
                qr = jnp.where(sub == 0, jnp.broadcast_to(pv_r, shape),
                               pltpu.roll(st_ref[SCAN_STEPS - 1, sub_rows, re_l], 1, 0))
                qi = jnp.where(sub == 0, jnp.broadcast_to(pv_i, shape),
                               pltpu.roll(st_ref[SCAN_STEPS - 1, sub_rows, im_l], 1, 0))
                dab_ref[:, re_l] += dr + gr * qr + gi * qi
                dab_ref[:, im_l] += di + gi * qr - gr * qi


def _scan_view(a):
    return a.reshape(16, a.shape[0] // 16, a.shape[1])


def _pair_tile(p):
    start = (p * 2 * SSM_GROUP // PAIR_TILE) * PAIR_TILE
    return slice(start, start + PAIR_TILE)


def _pair_lanes(p):
    return pl.ds(p * PAIR_LANES, PAIR_LANES), pl.ds(SSM_LANES + p * PAIR_LANES, PAIR_LANES)


def _pair_store(s_ref, p, val):
    re_l, im_l = _pair_lanes(p)
    s_ref[:, :, re_l] = val[:, :PAIR_LANES].reshape(16, SCAN_COLS, PAIR_LANES)
    s_ref[:, :, im_l] = val[:, PAIR_LANES:].reshape(16, SCAN_COLS, PAIR_LANES)


def _pair_load(s_ref, p):
    re_l, im_l = _pair_lanes(p)
    parts = [s_ref[:, :, l].reshape(SCAN_BLOCK, PAIR_LANES) for l in (re_l, im_l)]
    return jnp.concatenate(parts, axis=1).astype(BF16)


def _pair_sum(fn):
    per = PAIR_TILE // (2 * SSM_GROUP)
    tiles = []
    for t in range(SSM_PAIRS // per):
        acc = None
        for p in range(t * per, (t + 1) * per):
            part = fn(p)
            acc = part if acc is None else acc + part
        tiles.append(acc)
    return jnp.concatenate(tiles, axis=1)


def _ssm_fwd(name, u, bb_mats, c_mats, pw_rows, d_skip):
    rows = u.shape[0]
    nl2 = 2 * SSM_LANES
    nblk = rows // SCAN_BLOCK

    def body(u_ref, bb_ref, c_ref, pw_ref, d_ref, y_ref, yg_ref, s_ref, carry_ref, tmp_ref):
        @pl.when(pl.program_id(0) == 0)
        def _():
            carry_ref[...] = jnp.zeros_like(carry_ref)

        uv = u_ref[...].reshape(SCAN_BLOCK, SSM_WIDTH)
        ub = uv.astype(BF16)
        for p in range(SSM_PAIRS):
            _pair_store(s_ref, p, _dot_nn(ub[:, _pair_tile(p)], bb_ref[p]))
        _scan_block(s_ref, carry_ref, tmp_ref, pw_ref, reverse=False)
        ys = _pair_sum(lambda p: _dot_nt(_pair_load(s_ref, p), c_ref[p]))
        yv = ys + d_ref[...] * uv
        y_ref[...] = yv.reshape(16, SCAN_COLS, SSM_WIDTH)
        yg_ref[...] = jax.nn.gelu(yv).astype(BF16).reshape(16, SCAN_COLS, SSM_WIDTH)

    const = lambda shape: pl.BlockSpec(shape, lambda i: (0,) * len(shape))
    blk = lambda cols: pl.BlockSpec((16, SCAN_COLS, cols), lambda i: (0, i, 0))
    pair_mats = const((SSM_PAIRS, PAIR_TILE, PAIR_TILE))
    y, yg, s = pl.pallas_call(
        body, name=name, grid=(nblk,),
        in_specs=[blk(SSM_WIDTH), pair_mats, pair_mats, const((SCAN_STEPS, SCAN_SUB, nl2)), const((1, SSM_WIDTH))],
        out_specs=[blk(SSM_WIDTH), blk(SSM_WIDTH), blk(nl2)],
        out_shape=[jax.ShapeDtypeStruct((16, rows // 16, SSM_WIDTH), F32),
                   jax.ShapeDtypeStruct((16, rows // 16, SSM_WIDTH), BF16),
                   jax.ShapeDtypeStruct((16, rows // 16, nl2), F32)],
        scratch_shapes=[pltpu.VMEM((SCAN_SUB, nl2), F32), pltpu.VMEM((2 * SCAN_SUB, nl2), F32)],
        compiler_params=_cparams("arbitrary"),
    )(_scan_view(u), bb_mats, c_mats, pw_rows, d_skip)
    return y.reshape(rows, SSM_WIDTH), yg.reshape(rows, SSM_WIDTH), s.reshape(rows, nl2)


def _ssm_bwd(name, dy, u, states, bb_mats, c_mats, pwc_rows, d_skip):
    rows = u.shape[0]
    nl2 = 2 * SSM_LANES
    nblk = rows // SCAN_BLOCK

    def body(dy_ref, u_ref, st_ref, prev_ref, bb_ref, c_ref, pw_ref, d_ref,
             du_ref, dbb_ref, dc_ref, dab_ref, dd_ref, g_ref, carry_ref, tmp_ref):
        i = pl.program_id(0)

        @pl.when(i == 0)
        def _():
            carry_ref[...] = jnp.zeros_like(carry_ref)
            for ref in (dbb_ref, dc_ref, dab_ref, dd_ref):
                ref[...] = jnp.zeros_like(ref)

        dyv = dy_ref[...].reshape(SCAN_BLOCK, SSM_WIDTH)
        uv = u_ref[...].reshape(SCAN_BLOCK, SSM_WIDTH)
        dyb, ub = dyv.astype(BF16), uv.astype(BF16)
        for p in range(SSM_PAIRS):
            _pair_store(g_ref, p, _dot_nn(dyb[:, _pair_tile(p)], c_ref[p]))
        have_prev = (i < nblk - 1).astype(F32)
        _scan_block(g_ref, carry_ref, tmp_ref, pw_ref, reverse=True,
                    sprev=(st_ref, prev_ref, have_prev, dab_ref))

        def pair_work(p):
            gp = _pair_load(g_ref, p)
            dbb_ref[p] += _dot_tn(ub[:, _pair_tile(p)], gp)
            dc_ref[p] += _dot_tn(dyb[:, _pair_tile(p)], _pair_load(st_ref, p))
            return _dot_nt(gp, bb_ref[p])

        du_ref[...] = (_pair_sum(pair_work) + d_ref[...] * dyv).reshape(16, SCAN_COLS, SSM_WIDTH)
        dd_ref[...] += jnp.sum(dyv * uv, axis=0, keepdims=True)

    const = lambda shape: pl.BlockSpec(shape, lambda i: (0,) * len(shape))
    blk = lambda cols: pl.BlockSpec((16, SCAN_COLS, cols), lambda i: (0, nblk - 1 - i, 0))
    per8 = SCAN_COLS // SCAN_SUB
    prev_spec = pl.BlockSpec((None, SCAN_SUB, nl2), lambda i: (15, jnp.maximum((nblk - 1 - i) * per8 - 1, 0), 0))
    pair_mats = const((SSM_PAIRS, PAIR_TILE, PAIR_TILE))
    pair_shape = jax.ShapeDtypeStruct((SSM_PAIRS, PAIR_TILE, PAIR_TILE), F32)
    sv = _scan_view(states)
    du, dbb, dc, dab, dd = pl.pallas_call(
        body, name=name, grid=(nblk,),
        in_specs=[blk(SSM_WIDTH), blk(SSM_WIDTH), blk(nl2), prev_spec, pair_mats, pair_mats,
                  const((SCAN_STEPS, SCAN_SUB, nl2)), const((1, SSM_WIDTH))],
        out_specs=[blk(SSM_WIDTH), pair_mats, pair_mats, const((SCAN_SUB, nl2)), const((1, SSM_WIDTH))],
        out_shape=[jax.ShapeDtypeStruct((16, rows // 16, SSM_WIDTH), F32), pair_shape, pair_shape,
                   jax.ShapeDtypeStruct((SCAN_SUB, nl2), F32), jax.ShapeDtypeStruct((1, SSM_WIDTH), F32)],
        scratch_shapes=[pltpu.VMEM((16, SCAN_COLS, nl2), F32), pltpu.VMEM((SCAN_SUB, nl2), F32),
                        pltpu.VMEM((2 * SCAN_SUB, nl2), F32)],
        compiler_params=_cparams("arbitrary"),
    )(_scan_view(dy), _scan_view(u), sv, sv, bb_mats, c_mats, pwc_rows, d_skip)
    return du.reshape(rows, SSM_WIDTH), dbb, dc, dab, dd


def _adamw(name, w, m, v, gparts, tr):
    rows, cols = w.shape

    def body(w_ref, m_ref, v_ref, g_ref, og_ref, od_ref, om_ref, ov_ref):
        g = g_ref[0].astype(F32)
        for i in range(1, N_DEV):
            g = g + g_ref[i].astype(F32)
        m_new = B1 * m_ref[...] + (1.0 - B1) * g
        v_new = B2 * v_ref[...] + (1.0 - B2) * (g * g)
        m_hat = m_new / (1.0 - B1 ** STEP)
        v_hat = v_new / (1.0 - B2 ** STEP)
        og_ref[...] = g
        od_ref[...] = -LR * (m_hat / (jnp.sqrt(v_hat) + ADAM_EPS) + WD * w_ref[...])
        om_ref[...] = m_new
        ov_ref[...] = v_new

    spec = pl.BlockSpec((tr, cols), lambda i: (i, 0))
    shape = jax.ShapeDtypeStruct((rows, cols), F32)
    return pl.pallas_call(
        body, name=name, grid=(rows // tr,),
        in_specs=[spec, spec, spec, pl.BlockSpec((N_DEV, tr, cols), lambda i: (0, i, 0))],
        out_specs=[spec] * 4, out_shape=[shape] * 4,
        compiler_params=_cparams("parallel"),
    )(w, m, v, gparts)


_SHARDED = (
    ("ffn1_w_gate", True, (352, 1024)), ("ffn1_w_up", True, (352, 1024)), ("ffn1_w_down", False, (352, 1024)),
    ("w_in", True, (608, 1024)), ("ssm_w_glu", True, (128, 512)), ("w_attn_branch", True, (128, 256)),
    ("w_ssm_branch", True, (128, 512)), ("w_out", False, (128, 1024)),
    ("ffn2_w_gate", True, (352, 1024)), ("ffn2_w_up", True, (352, 1024)), ("ffn2_w_down", False, (352, 1024)),
)
_SMALL = ("ffn1_norm", "mix_norm", "gate_bias", "rel_bias_table", "ssm_a_re", "ssm_a_im", "ssm_log_dt",
          "ssm_b_re", "ssm_b_im", "ssm_c_re", "ssm_c_im", "ssm_d", "ffn2_norm", "final_norm")
_ORDER = ("ffn1_norm", "ffn1_w_gate", "ffn1_w_up", "ffn1_w_down", "mix_norm", "w_in", "gate_bias",
          "rel_bias_table", "ssm_a_re", "ssm_a_im", "ssm_log_dt", "ssm_b_re", "ssm_b_im", "ssm_c_re",
          "ssm_c_im", "ssm_d", "ssm_w_glu", "w_attn_branch", "w_ssm_branch", "w_out", "ffn2_norm",
          "ffn2_w_gate", "ffn2_w_up", "ffn2_w_down", "final_norm")


def _pack_rows(shape):
    return shape[0] * shape[1] // D_MODEL


_SHARD_INFO = {nm: (tr, shape) for nm, tr, shape in _SHARDED}
_PHASES = {
    "f1gu": ("ffn1_w_gate", "ffn1_w_up"), "f1d": ("ffn1_w_down",),
    "mix": ("w_in", "ssm_w_glu", "w_attn_branch", "w_ssm_branch", "w_out"),
    "f2": ("ffn2_w_gate", "ffn2_w_up", "ffn2_w_down"),
}


def _to_rows(a, nm):
    tr, shape = _SHARD_INFO[nm]
    return (a.T if tr else a).reshape(_pack_rows(shape), D_MODEL)


def _from_rows(p, nm):
    tr, shape = _SHARD_INFO[nm]
    a = p.reshape(shape)
    return a.T if tr else a


def _full_weight(gathered, nm):
    _, shape = _SHARD_INFO[nm]
    return gathered.reshape(N_DEV * shape[0], shape[1])


def _grad_blocks(g, nm):
    _, shape = _SHARD_INFO[nm]
    return g.astype(BF16).reshape(N_DEV, _pack_rows(shape), D_MODEL)


_SMALL_TILE = 8 * 128


def _small_rows(a):
    flat = a.reshape(-1)
    return jnp.pad(flat, (0, (-flat.shape[0]) % _SMALL_TILE)).reshape(-1, 128)


def _pack_small(ws, last=None):
    tail = jnp.zeros((), F32) if last is None else last
    return jnp.concatenate([_small_rows(ws[nm]) for nm in _SMALL] + [_small_rows(tail)], axis=0)


def _unpack_small(pack, like):
    out, r0 = {}, 0
    for nm in _SMALL:
        n = like[nm].size
        nr = 8 * -(-n // _SMALL_TILE)
        out[nm] = pack[r0:r0 + nr].reshape(-1)[:n].reshape(like[nm].shape)
        r0 += nr
    return out


def _residue_order(a):
    rows, cols = a.shape
    return a.reshape(rows // 16, 16, cols).transpose(1, 0, 2).reshape(rows, cols)


def _token_order(a):
    rows, cols = a.shape
    return a.reshape(16, rows // 16, cols).transpose(1, 0, 2).reshape(rows, cols)


_PAIRS_PER_TILE = PAIR_TILE // (2 * SSM_GROUP)
_PAIR_AXES = (SSM_PAIRS // _PAIRS_PER_TILE, _PAIRS_PER_TILE, 2)


def _pair_matrices(re, im):
    six = jnp.stack([re, im]).reshape((2,) + _PAIR_AXES + (SSM_GROUP, SSM_STATE))
    eye_j, eye_l = jnp.eye(_PAIRS_PER_TILE, dtype=re.dtype), jnp.eye(2, dtype=re.dtype)
    mats = jnp.einsum("xkjlcn,jJ,lL->kjJLcxln", six, eye_j, eye_l)
    return mats.reshape(SSM_PAIRS, PAIR_TILE, PAIR_TILE).astype(BF16)


def _pair_diagonals(acc):
    k, j, l = _PAIR_AXES
    eight = acc.reshape(k, j, j, l, SSM_GROUP, 2, l, SSM_STATE)
    eye_j, eye_l = jnp.eye(j, dtype=acc.dtype), jnp.eye(l, dtype=acc.dtype)
    own = jnp.einsum("kjJLcxln,jJ,lL->xkjlcn", eight, eye_j, eye_l).reshape(2, SSM_GROUPS, SSM_GROUP, SSM_STATE)
    return own[0], own[1]


def _local_step(xs, target, small, weights_of, send_grads, first_deps=()):
    rows = xs.shape[0]
    gfull, gsmall = {}, {}

    table_t = small["rel_bias_table"].T
    tables, bias4 = [], []
    for g in range(N_GROUPS):
        bucket, valid = [jnp.asarray(t) for t in _attn_tables(g, rows)]
        bias_g = _bias_fwd(f"rel_bias_fwd_{g}", bucket, valid, table_t[g * HEADS_PER_GROUP:(g + 1) * HEADS_PER_GROUP])
        tables.append(bucket)
        bias4.append(bias_g.reshape(-1, bias_g.shape[-1]))
    pw_re, pw_im, bb_re, bb_im = _ssm_params_fwd(
        "ssm_params_fwd", small["ssm_a_re"], small["ssm_a_im"], small["ssm_log_dt"].reshape(SSM_GROUPS, 1),
        small["ssm_b_re"].transpose(2, 0, 1), small["ssm_b_im"].transpose(2, 0, 1))

    def power_rows(sign):
        row = jnp.concatenate([pw_re.reshape(SCAN_STEPS, 1, SSM_LANES), sign * pw_im.reshape(SCAN_STEPS, 1, SSM_LANES)],
                              axis=2)
        return jnp.broadcast_to(row, (SCAN_STEPS, SCAN_SUB, 2 * SSM_LANES))

    bb_mats = _pair_matrices(bb_re.transpose(1, 0, 2), bb_im.transpose(1, 0, 2))
    c_mats = _pair_matrices(small["ssm_c_re"], -small["ssm_c_im"])
    pw_fwd, pw_bwd = power_rows(1.0), power_rows(-1.0)
    d_skip = small["ssm_d"].reshape(1, SSM_WIDTH)
    wf = dict(weights_of("f1", [xs, target, bb_mats, c_mats, pw_fwd, pw_bwd] + bias4))

    x1, h1, gg1, uu1, hmix = _ffn_fwd("ffn1_fwd", xs, small["ffn1_norm"], wf["ffn1_w_gate"], wf["ffn1_w_up"],
                                      wf["ffn1_w_down"], small["mix_norm"], deps=first_deps)
    wf.update(weights_of("mix", x1))
    w_in = wf["w_in"]
    w_qkv, w_u, w_g = w_in[:3 * ATTN_WIDTH], w_in[3 * ATTN_WIDTH:3 * ATTN_WIDTH + SSM_WIDTH], w_in[3 * ATTN_WIDTH + SSM_WIDTH:]
    qscale = jnp.concatenate([jnp.full((1, ATTN_WIDTH), HEAD_DIM ** -0.5, F32), jnp.ones((1, 2 * ATTN_WIDTH), F32)], axis=1)
    qkv, = _mm("in_qkv", [(hmix, w_qkv)], True, 3 * ATTN_WIDTH, [BF16],
               epilogue=lambda acc, sc: (acc * sc,), extras=[(qscale, 0)], tn=ATTN_WIDTH)
    u, = _mm("in_u", [(hmix, w_u)], True, SSM_WIDTH, [F32])
    gates, = _mm("in_gates", [(hmix, w_g)], True, 2 * D_MODEL, [F32],
                 epilogue=lambda acc, b: (_sigmoid(acc + b),), extras=[(small["gate_bias"], 0)])

    o_g, lse_g = [], []
    for g in range(N_GROUPS):
        o, lse = _attn_fwd(f"attn_fwd_{g}", qkv, g, bias4[g])
        o_g.append(o)
        lse_g.append(lse)
    oa_f32, oa = _combine_fwd("attn_combine_fwd", o_g, lse_g)
    y_attn, = _mm("attn_branch", [(oa, wf["w_attn_branch"])], True, D_MODEL, [F32])
    y_raw, ygelu, states = _ssm_fwd("ssm_fwd", u, bb_mats, c_mats, pw_fwd, d_skip)
    glu, ysg = _mm("ssm_glu", [(ygelu, wf["ssm_w_glu"])], True, 2 * SSM_WIDTH, [F32, BF16],
                   epilogue=lambda gv: (gv, gv[:, :SSM_WIDTH] * _sigmoid(gv[:, SSM_WIDTH:])),
                   tn=2 * SSM_WIDTH, out_cols=[2 * SSM_WIDTH, SSM_WIDTH])
    y_ssm, merged = _mm("ssm_branch_merge", [(ysg, wf["w_ssm_branch"])], True, D_MODEL, [F32, BF16],
                        epilogue=lambda acc, ga, gs, ya: (acc, ga * ya + gs * acc),
                        extras=[(gates, 0), (gates, D_MODEL), (y_attn, 0)])
    x2, = _mm("mix_out", [(merged, wf["w_out"])], False, D_MODEL, [F32],
              epilogue=lambda acc, res: (res + acc,), extras=[(x1, 0)])
    wf.update(weights_of("f2", x2))
    dx3, h2, gg2, uu2, gsmall["final_norm"], gsmall["loss"] = _ffn_fwd_head(
        "ffn2_fwd", x2, small["ffn2_norm"], wf["ffn2_w_gate"], wf["ffn2_w_up"], wf["ffn2_w_down"],
        small["final_norm"].reshape(1, D_MODEL), target)

    dx2, dgg2, duu2, act2, gsmall["ffn2_norm"] = _ffn_bwd(
        "ffn2_bwd", dx3, x2, small["ffn2_norm"], gg2, uu2, wf["ffn2_w_gate"], wf["ffn2_w_up"], wf["ffn2_w_down"])
    gfull["ffn2_w_gate"] = _mm_tn("ffn2_dwg", dgg2, h2, out_dtype=BF16)
    gfull["ffn2_w_up"] = _mm_tn("ffn2_dwu", duu2, h2, out_dtype=BF16)
    gfull["ffn2_w_down"] = _mm_tn("ffn2_dwd", act2, dx3, scale=0.5, out_dtype=BF16)
    sent = send_grads("f2", gfull)

    def merge_bwd(dm, ga, gs, ya, ys):
        dza, dzs = dm * ya * ga * (1.0 - ga), dm * ys * gs * (1.0 - gs)
        return (dm * ga, dm * gs, dza, dzs, jnp.sum(dza, axis=0, keepdims=True), jnp.sum(dzs, axis=0, keepdims=True))

    dya, dys, dzga, dzgs, dba, dbs = _mm(
        "mix_out_bwd", [(dx2, wf["w_out"])], True, D_MODEL, [BF16] * 4, epilogue=merge_bwd, row_sums=2,
        extras=[(gates, 0), (gates, D_MODEL), (y_attn, 0), (y_ssm, 0)], deps=sent, tm=512, tn=D_MODEL)
    gfull["w_out"] = _mm_tn("dw_out", merged, dx2, out_dtype=BF16)
    gsmall["gate_bias"] = jnp.concatenate([dba, dbs], axis=1)

    gfull["w_ssm_branch"] = _mm_tn("dw_ssm_branch", dys, ysg, out_dtype=BF16)

    def glu_bwd(dysg, av, bv):
        sb = _sigmoid(bv)
        return (dysg * sb, dysg * av * sb * (1.0 - sb))

    dglu_a, dglu_b = _mm("ssm_branch_bwd", [(dys, wf["w_ssm_branch"])], False, SSM_WIDTH, [BF16, BF16],
                         epilogue=glu_bwd, extras=[(glu, 0), (glu, SSM_WIDTH)])
    w_glu = wf["ssm_w_glu"]
    gfull["ssm_w_glu"] = _mm_tn_stack("dw_glu", [dglu_a, dglu_b], ygelu, out_dtype=BF16)

    def gelu_bwd(acc, yv):
        _, vjp = jax.vjp(jax.nn.gelu, yv)
        return (vjp(acc)[0],)

    dy_raw, = _mm("ssm_glu_bwd", [(dglu_a, w_glu[:SSM_WIDTH]), (dglu_b, w_glu[SSM_WIDTH:])], False, SSM_WIDTH, [F32],
                  epilogue=gelu_bwd, extras=[(y_raw, 0)])
    du, dbb_acc, dc_acc, dab_rows, gsmall_d = _ssm_bwd(
        "ssm_bwd", dy_raw, u, states, bb_mats, c_mats, pw_bwd, d_skip)
    gsmall["ssm_d"] = gsmall_d
    dbb_re, dbb_im = [a.transpose(1, 0, 2) for a in _pair_diagonals(dbb_acc)]
    dc_re, dc_im = _pair_diagonals(dc_acc)
    gsmall["ssm_c_re"], gsmall["ssm_c_im"] = dc_re, -dc_im
    dab = _colsum("ssm_dab", dab_rows)
    d_ar, d_ai, d_ld, d_br, d_bi = _ssm_params_bwd(
        "ssm_params_bwd", small["ssm_a_re"], small["ssm_a_im"], small["ssm_log_dt"].reshape(SSM_GROUPS, 1),
        small["ssm_b_re"].transpose(2, 0, 1), small["ssm_b_im"].transpose(2, 0, 1),
        dab[:, :SSM_LANES].reshape(SSM_GROUPS, SSM_STATE), dab[:, SSM_LANES:].reshape(SSM_GROUPS, SSM_STATE),
        dbb_re, dbb_im)
    gsmall["ssm_a_re"], gsmall["ssm_a_im"], gsmall["ssm_log_dt"] = d_ar, d_ai, d_ld.reshape(SSM_GROUPS)
    gsmall["ssm_b_re"], gsmall["ssm_b_im"] = d_br.transpose(1, 2, 0), d_bi.transpose(1, 2, 0)

    gfull["w_attn_branch"] = _mm_tn("dw_attn_branch", dya, oa, out_dtype=BF16)
    doa, = _mm("attn_branch_bwd", [(dya, wf["w_attn_branch"])], False, ATTN_OUT, [F32])
    dc = _combine_bwd("attn_combine_bwd", doa, oa_f32, lse_g)
    dqkv_cols = [None] * 9
    dtable = []
    for g in range(N_GROUPS):
        dq, dk, dv, db = _attn_bwd(f"attn_bwd_{g}", qkv, dc[g], lse_g[g], dc[3 + g], g, bias4[g])
        dqkv_cols[g], dqkv_cols[3 + g], dqkv_cols[6 + g] = dq, dk, dv
        dt = _bias_bwd(f"rel_bias_bwd_{g}", tables[g], db.reshape(HEADS_PER_GROUP, -1, db.shape[-1]))
        dtable.append(dt[:, :HEADS_PER_GROUP])
    gsmall["rel_bias_table"] = jnp.concatenate(dtable, axis=1)

    gfull["w_in"] = jnp.concatenate([_mm_tn_stack("dw_in_qkv", dqkv_cols, hmix, out_dtype=BF16),
                                     _mm_tn_stack("dw_in_rest", [du, dzga, dzgs], hmix, out_dtype=BF16)], axis=0)
    sent = send_grads("mix", gfull)
    qkv_pairs = [(c, w_qkv[i * ATTN_OUT:(i + 1) * ATTN_OUT]) for i, c in enumerate(dqkv_cols)]

    def mix_norm_bwd(dh, xv, gain, dres):
        r, xh = _rms_parts(xv)
        return dres + _rms_bwd_dx(dh, gain, r, xh), jnp.sum(dh * xh, axis=0, keepdims=True)

    dx1, gsmall["mix_norm"] = _mm(
        "in_bwd", qkv_pairs + [(du, w_u), (dzga, w_g[:D_MODEL]), (dzgs, w_g[D_MODEL:])], False, D_MODEL, [F32],
        epilogue=mix_norm_bwd, row_sums=1, extras=[(x1, 0), (small["mix_norm"], 0), (dx2, 0)], tm=512, tn=D_MODEL,
        deps=sent)

    dx, dgg1, duu1, act1, gsmall["ffn1_norm"] = _ffn_bwd(
        "ffn1_bwd", dx1, xs, small["ffn1_norm"], gg1, uu1, wf["ffn1_w_gate"], wf["ffn1_w_up"], wf["ffn1_w_down"])
    sent = send_grads("small", gsmall)
    gfull["ffn1_w_gate"] = _mm_tn("ffn1_dwg", dgg1, h1, deps=sent, out_dtype=BF16)
    gfull["ffn1_w_up"] = _mm_tn("ffn1_dwu", duu1, h1, out_dtype=BF16)
    sent = send_grads("f1gu", gfull)
    gfull["ffn1_w_down"] = _mm_tn("ffn1_dwd", act1, dx1, scale=0.5, deps=sent, out_dtype=BF16)
    send_grads("f1d", gfull)
    return dx, gsmall


def kernel(x, ffn1_norm, ffn1_w_gate, ffn1_w_up, ffn1_w_down, mix_norm, w_in, gate_bias, rel_bias_table, ssm_a_re, ssm_a_im, ssm_log_dt, ssm_b_re, ssm_b_im, ssm_c_re, ssm_c_im, ssm_d, ssm_w_glu, w_attn_branch, w_ssm_branch, w_out, ffn2_norm, ffn2_w_gate, ffn2_w_up, ffn2_w_down, final_norm, loss_target, m_ffn1_norm, m_ffn1_w_gate, m_ffn1_w_up, m_ffn1_w_down, m_mix_norm, m_w_in, m_gate_bias, m_rel_bias_table, m_ssm_a_re, m_ssm_a_im, m_ssm_log_dt, m_ssm_b_re, m_ssm_b_im, m_ssm_c_re, m_ssm_c_im, m_ssm_d, m_ssm_w_glu, m_w_attn_branch, m_w_ssm_branch, m_w_out, m_ffn2_norm, m_ffn2_w_gate, m_ffn2_w_up, m_ffn2_w_down, m_final_norm, v_ffn1_norm, v_ffn1_w_gate, v_ffn1_w_up, v_ffn1_w_down, v_mix_norm, v_w_in, v_gate_bias, v_rel_bias_table, v_ssm_a_re, v_ssm_a_im, v_ssm_log_dt, v_ssm_b_re, v_ssm_b_im, v_ssm_c_re, v_ssm_c_im, v_ssm_d, v_ssm_w_glu, v_w_attn_branch, v_w_ssm_branch, v_w_out, v_ffn2_norm, v_ffn2_w_gate, v_ffn2_w_up, v_ffn2_w_down, v_final_norm):
    given = dict(locals())
    shapes = {nm: given[nm].shape for nm in _ORDER}

    def strip(a):
        return a[0] if a.ndim >= 2 and a.shape[0] == 1 else a

    w = {nm: strip(given[nm]) for nm in _ORDER}
    m = {nm: strip(given["m_" + nm]) for nm in _ORDER}
    v = {nm: strip(given["v_" + nm]) for nm in _ORDER}
    for d in (w, m, v):
        d["rel_bias_table"] = d["rel_bias_table"].reshape(N_BUCKETS, N_GROUPS * HEADS_PER_GROUP)

    weight_phases = {"f1": _PHASES["f1gu"] + _PHASES["f1d"], "mix": _PHASES["mix"], "f2": _PHASES["f2"]}
    pending_w, w_rows, deps, zero = {}, {}, [], 0.0
    for phase, names in weight_phases.items():
        w_rows.update({nm: _to_rows(w[nm] + zero, nm) for nm in names})
        pending_w[phase] = _exchange_start(f"gather_{phase}_start", [w_rows[nm].astype(BF16) for nm in names],
                                           gather=True, deps=deps)
        deps = [pending_w[phase][4]]
        zero = pending_w["f1"][4][0, 0]
    m_rows = {nm: _to_rows(m[nm] + zero, nm) for nm in _SHARD_INFO}
    v_rows = {nm: _to_rows(v[nm] + zero, nm) for nm in _SHARD_INFO}
    small = {nm: w[nm] for nm in _SMALL}
    small_in = {nm: small[nm] + zero for nm in _SMALL}
    for nm in ("ffn1_norm", "mix_norm", "ffn2_norm", "gate_bias"):
        small_in[nm] = small_in[nm].reshape(1, -1)

    def weights_of(phase, after):
        if phase == "f1":
            after = list(after) + list(m_rows.values()) + list(v_rows.values())
        landed = _exchange_wait(f"gather_{phase}_wait", pending_w[phase], after, gather=True)
        return {nm: _full_weight(got, nm) for nm, got in zip(weight_phases[phase], landed)}

    pending_g = {}

    def send_grads(phase, grads):
        if phase == "small":
            gs_pack = _pack_small({nm: grads[nm].reshape(small[nm].shape) for nm in _SMALL}, last=grads["loss"])
            pending_g[phase] = _exchange_start("gather_small_start", [gs_pack], gather=True)
        else:
            pending_g[phase] = _exchange_start(f"scatter_{phase}_start",
                                               [_grad_blocks(grads[nm], nm) for nm in _PHASES[phase]], gather=False)
        return [pending_g[phase][4]]

    dx, gsmall = _local_step(_residue_order(x[0]), _residue_order(loss_target[0]), small_in, weights_of, send_grads,
                             first_deps=[pending_w["f2"][4]])
    dx = _token_order(dx)

    updated = {}
    after = pending_g["f1d"][4]
    for phase in ("f2", "mix", "small", "f1gu", "f1d"):
        landed = _exchange_wait(f"exchange_{phase}_wait", pending_g[phase], after, gather=phase == "small")
        if phase == "small":
            sm = _adamw("adamw_small", _pack_small(small), _pack_small({nm: m[nm] for nm in _SMALL}),
                        _pack_small({nm: v[nm] for nm in _SMALL}), landed[0], landed[0].shape[1])
            after = sm[0]
            continue
        for nm, recv in zip(_PHASES[phase], landed):
            tr = max(t for t in range(16, 353, 16) if w_rows[nm].shape[0] % t == 0)
            updated[nm] = _adamw(f"adamw_{nm}", w_rows[nm], m_rows[nm], v_rows[nm], recv, tr)
            after = updated[nm][0]

    loss = sm[0][-8, 0]
    outs = []
    for i in range(4):
        sml = _unpack_small(sm[i], small)
        outs.append([(_from_rows(updated[nm][i], nm) if nm in updated else sml[nm]).reshape(shapes[nm])
                     for nm in _ORDER])
    return (loss, dx[None], *outs[0], *outs[1], *outs[2], *outs[3])
```

```python
import math

import numpy as np
import jax
import jax.numpy as jnp
from jax import lax
from jax.experimental import pallas as pl
from jax.experimental.pallas import tpu as pltpu

F32 = jnp.float32
BF16 = jnp.bfloat16

N_DEV = 8
D_MODEL = 1024
D_FF = 2816
HEAD_DIM = 64
HEADS_PER_GROUP = 4
DILATIONS = (1, 4, 16)
N_GROUPS = 3
ATTN_WIDTH = 768
ATTN_OUT = 256
BLOCK = 128
N_BUCKETS = 32
MAX_DISTANCE = 2048
NEG_INF = -1e30
SSM_WIDTH = 512
SSM_GROUPS = 32
SSM_GROUP = 16
SSM_STATE = 64
SSM_LANES = SSM_GROUPS * SSM_STATE
SSM_PAIRS = SSM_GROUPS // 2
PAIR_LANES = 2 * SSM_STATE
PAIR_TILE = 256
EPS = 1e-6
LR, B1, B2, ADAM_EPS, WD, STEP = 0.001, 0.9, 0.999, 1e-08, 0.01, 10

VMEM_LIMIT_BYTES = 56 * 1024 * 1024
FFN_CHUNK = 768
SCAN_BLOCK = 256
SCAN_STEPS = 16
SCAN_COLS = SCAN_BLOCK // SCAN_STEPS
SCAN_SUB = 8
SCAN_LANES = 512

MESH = pl.DeviceIdType.MESH


def _cparams(*sem):
    return pltpu.CompilerParams(dimension_semantics=sem, vmem_limit_bytes=VMEM_LIMIT_BYTES)


def _dot(a, b, dims):
    return lax.dot_general(a, b, (dims, ((), ())), preferred_element_type=F32)


def _dot_nn(a, b):
    return _dot(a, b, ((1,), (0,)))


def _dot_nt(a, b):
    return _dot(a, b, ((1,), (1,)))


def _dot_tn(a, b):
    return _dot(a, b, ((0,), (0,)))


def _sigmoid(x):
    return 1.0 / (1.0 + jnp.exp(-x))


_HBM_SPEC = pl.BlockSpec(memory_space=pltpu.HBM)
_SEM_SPEC = pl.BlockSpec(memory_space=pltpu.SEMAPHORE)
_ANY_SPEC = pl.BlockSpec(memory_space=pl.ANY)
_EFFECT = pltpu.SideEffectType.DATAFLOW_SIDE_EFFECTING


def _peers(x, y, c):
    return [(1 - x if k & 4 else x, 1 - y if k & 2 else y, 1 - c if k & 1 else c) for k in range(1, N_DEV)]


def _exchange_copies(x_refs, land_refs, send_sems, recv_sems, gather):
    x, y, c = lax.axis_index("x"), lax.axis_index("y"), lax.axis_index("c")
    me = 4 * x + 2 * y + c
    copies = []
    for a, (x_ref, land_ref) in enumerate(zip(x_refs, land_refs)):
        for k, (px, py, pc) in enumerate(_peers(x, y, c)):
            src = x_ref if gather else x_ref.at[4 * px + 2 * py + pc]
            copies.append(pltpu.make_async_remote_copy(
                src_ref=src, dst_ref=land_ref.at[me], send_sem=send_sems.at[N_DEV * a + k],
                recv_sem=recv_sems.at[(N_DEV - 1) * a + k], device_id=(px, py, pc), device_id_type=MESH))
    owns = [pltpu.make_async_copy(x_ref if gather else x_ref.at[me], land_ref.at[me],
                                  send_sems.at[N_DEV * a + N_DEV - 1])
            for a, (x_ref, land_ref) in enumerate(zip(x_refs, land_refs))]
    return owns, copies


def _exchange_start(name, xs_list, gather, deps=()):
    n, nd = len(xs_list), len(deps)
    land_shapes = [(N_DEV, *xs.shape) if gather else xs.shape for xs in xs_list]

    def body(*refs):
        x_refs, land_refs = refs[:n], refs[n:2 * n]
        send_sems, recv_sems = refs[2 * n + nd:2 * n + nd + 2]
        token = refs[-1]
        owns, copies = _exchange_copies(x_refs, land_refs, send_sems, recv_sems, gather)
        for cp in copies + owns:
            cp.start()
        token[...] = jnp.zeros_like(token)

    hbm = lambda a: pltpu.with_memory_space_constraint(a, pltpu.HBM)
    outs = pl.pallas_call(
        body, name=name,
        out_shape=(pltpu.SemaphoreType.DMA((n * N_DEV,)), pltpu.SemaphoreType.DMA((n * (N_DEV - 1),)),
                   *[pltpu.HBM(xs.shape, xs.dtype) for xs in xs_list],
                   *[pltpu.HBM(shape, xs.dtype) for shape, xs in zip(land_shapes, xs_list)],
                   jax.ShapeDtypeStruct((8, 128), F32)),
        in_specs=(_HBM_SPEC,) * (2 * n) + (_ANY_SPEC,) * nd,
        out_specs=(_SEM_SPEC, _SEM_SPEC) + (_HBM_SPEC,) * (2 * n) + (pl.BlockSpec(memory_space=pltpu.VMEM),),
        input_output_aliases={i: 2 + i for i in range(2 * n)},
        compiler_params=pltpu.CompilerParams(has_side_effects=_EFFECT),
    )(*[hbm(xs) for xs in xs_list], *[hbm(lax.empty(shape, xs.dtype)) for shape, xs in zip(land_shapes, xs_list)],
      *deps)
    return outs[0], outs[1], list(outs[2:2 + n]), list(outs[2 + n:2 + 2 * n]), outs[-1]


def _exchange_wait(name, handle, after, gather):
    send_sems, recv_sems, xs_thru, lands_thru, _ = handle
    n = len(xs_thru)
    after = list(after) if isinstance(after, (list, tuple)) else [after]

    def body(*refs):
        x_refs, land_refs = refs[:n], refs[n:2 * n]
        send_sems, recv_sems = refs[2 * n:2 * n + 2]
        owns, copies = _exchange_copies(x_refs, land_refs, send_sems, recv_sems, gather)
        for cp in copies:
            cp.wait_send()
            cp.wait_recv()
        for cp in owns:
            cp.wait()

    outs = pl.pallas_call(
        body, name=name,
        out_shape=tuple(pltpu.HBM(a.shape, a.dtype) for a in xs_thru + lands_thru),
        in_specs=(_HBM_SPEC,) * (2 * n) + (_SEM_SPEC, _SEM_SPEC) + (_ANY_SPEC,) * len(after),
        out_specs=(_HBM_SPEC,) * (2 * n), input_output_aliases={i: i for i in range(2 * n)},
        compiler_params=pltpu.CompilerParams(has_side_effects=_EFFECT),
    )(*xs_thru, *lands_thru, send_sems, recv_sems, *after)
    return list(outs[n:])


def _mm(name, pairs, nt, n_cols, out_dtypes, epilogue=None, extras=(), tm=1024, tn=512, deps=(), row_sums=0,
        out_cols=None):
    rows = pairs[0][0].shape[0]
    tm = min(tm, rows)
    tn = min(tn, n_cols)
    na, ne, nd, no = len(pairs), len(extras), len(deps), len(out_dtypes)

    def body(*refs):
        a_refs, w_refs = refs[:na], refs[na:2 * na]
        e_refs, o_refs = refs[2 * na:2 * na + ne], refs[2 * na + ne + nd:]
        acc = None
        for a_ref, w_ref in zip(a_refs, w_refs):
            a = a_ref[...].astype(BF16)
            w = w_ref[...].astype(BF16)
            p = _dot_nt(a, w) if nt else _dot_nn(a, w)
            acc = p if acc is None else acc + p
        outs = (acc,) if epilogue is None else epilogue(acc, *[e[...] for e in e_refs])
        for o_ref, o in zip(o_refs[:no], outs[:no]):
            o_ref[...] = o.astype(o_ref.dtype)
        for r_ref, o in zip(o_refs[no:], outs[no:]):
            @pl.when(pl.program_id(0) == 0)
            def _():
                r_ref[...] = jnp.zeros_like(r_ref)

            r_ref[...] += o

    in_specs = [pl.BlockSpec((tm, a.shape[1]), lambda i, j: (i, 0)) for a, _ in pairs]
    for _, w in pairs:
        if nt:
            in_specs.append(pl.BlockSpec((tn, w.shape[1]), lambda i, j: (j, 0)))
        else:
            in_specs.append(pl.BlockSpec((w.shape[0], tn), lambda i, j: (0, j)))
    for e, col_off in extras:
        off = col_off // tn
        if e.shape[0] == 1:
            in_specs.append(pl.BlockSpec((1, tn), lambda i, j, off=off: (0, j + off)))
        else:
            in_specs.append(pl.BlockSpec((tm, tn), lambda i, j, off=off: (i, j + off)))
    in_specs += [_ANY_SPEC] * nd
    if out_cols is None:
        out_cols = [n_cols] * no
    else:
        assert tn == n_cols, "outputs of other widths need the whole row in one block"
    assert not row_sums or tn == n_cols
    out_specs = [pl.BlockSpec((tm, tn * c // n_cols), lambda i, j: (i, j)) for c in out_cols]
    out_specs += [pl.BlockSpec((1, tn), lambda i, j: (0, j))] * row_sums
    out_shape = [jax.ShapeDtypeStruct((rows, c), dt) for c, dt in zip(out_cols, out_dtypes)]
    out_shape += [jax.ShapeDtypeStruct((1, n_cols), F32)] * row_sums
    outs = pl.pallas_call(
        body, name=name, grid=(rows // tm, n_cols // tn),
        in_specs=in_specs, out_specs=out_specs, out_shape=out_shape,
        compiler_params=_cparams("arbitrary" if row_sums else "parallel", "arbitrary"),
    )(*[a for a, _ in pairs], *[w for _, w in pairs], *[e for e, _ in extras], *deps)
    return outs


def _tn_rows(m):
    return max(b for b in range(128, min(m, 1408) + 1, 128) if m % b == 0)


def _mm_tn(name, a, b, scale=1.0, bm=None, tk=1024, deps=(), out_dtype=F32):
    rows, m = a.shape
    n = b.shape[1]
    bm = _tn_rows(m) if bm is None else bm
    tk = min(tk, rows)
    nk = rows // tk

    def body(a_ref, b_ref, *rest):
        o_ref, acc_ref = rest[-2:]
        k = pl.program_id(1)

        @pl.when(k == 0)
        def _():
            acc_ref[...] = jnp.zeros_like(acc_ref)

        acc_ref[...] += _dot_tn(a_ref[...].astype(BF16), b_ref[...].astype(BF16))

        @pl.when(k == nk - 1)
        def _():
            o_ref[...] = (acc_ref[...] * scale).astype(o_ref.dtype)

    return pl.pallas_call(
        body, name=name, grid=(m // bm, nk),
        in_specs=[pl.BlockSpec((tk, bm), lambda i, k: (k, i)), pl.BlockSpec((tk, n), lambda i, k: (k, 0))]
        + [_ANY_SPEC] * len(deps),
        out_specs=pl.BlockSpec((bm, n), lambda i, k: (i, 0)),
        out_shape=jax.ShapeDtypeStruct((m, n), out_dtype),
        scratch_shapes=[pltpu.VMEM((bm, n), F32)],
        compiler_params=_cparams("parallel", "arbitrary"),
    )(a, b, *deps)


def _mm_tn_stack(name, a_list, b, tk=1024, out_dtype=F32):
    rows, n = b.shape
    ms = [a.shape[1] for a in a_list]
    tk = min(tk, rows)
    nk = rows // tk
    na = len(a_list)

    def body(*refs):
        a_refs, b_ref, o_ref, acc_ref = refs[:na], refs[na], refs[na + 1], refs[na + 2]
        k = pl.program_id(0)

        @pl.when(k == 0)
        def _():
            acc_ref[...] = jnp.zeros_like(acc_ref)

        bv = b_ref[...].astype(BF16)
        r0 = 0
        for a_ref, m in zip(a_refs, ms):
            acc_ref[r0:r0 + m, :] += _dot_tn(a_ref[...].astype(BF16), bv)
            r0 += m

        @pl.when(k == nk - 1)
        def _():
            o_ref[...] = acc_ref[...].astype(o_ref.dtype)

    return pl.pallas_call(
        body, name=name, grid=(nk,),
        in_specs=[pl.BlockSpec((tk, m), lambda k: (k, 0)) for m in ms] + [pl.BlockSpec((tk, n), lambda k: (k, 0))],
        out_specs=pl.BlockSpec((sum(ms), n), lambda k: (0, 0)),
        out_shape=jax.ShapeDtypeStruct((sum(ms), n), out_dtype),
        scratch_shapes=[pltpu.VMEM((sum(ms), n), F32)],
        compiler_params=_cparams("arbitrary"),
    )(*a_list, b)


def _colsum(name, xs, tm=512):
    rows, cols = xs.shape
    tm = min(tm, rows)

    def body(x_ref, o_ref):
        @pl.when(pl.program_id(0) == 0)
        def _():
            o_ref[...] = jnp.zeros_like(o_ref)

        o_ref[...] += jnp.sum(x_ref[...].astype(F32), axis=0, keepdims=True)

    return pl.pallas_call(
        body, name=name, grid=(rows // tm,),
        in_specs=[pl.BlockSpec((tm, cols), lambda i: (i, 0))],
        out_specs=pl.BlockSpec((1, cols), lambda i: (0, 0)),
        out_shape=jax.ShapeDtypeStruct((1, cols), F32),
        compiler_params=_cparams("arbitrary"),
    )(xs)


def _ew(name, fn, ins, out_cols, out_dtypes, tm=512):
    rows = ins[0].shape[0]
    tm = min(tm, rows)
    ni = len(ins)

    def body(*refs):
        outs = fn(*[r[...] for r in refs[:ni]])
        for o_ref, o in zip(refs[ni:], outs):
            o_ref[...] = o.astype(o_ref.dtype)

    def spec(shape):
        if shape[0] == 1:
            return pl.BlockSpec((1, shape[1]), lambda i: (0, 0))
        return pl.BlockSpec((tm, shape[1]), lambda i: (i, 0))

    return pl.pallas_call(
        body, name=name, grid=(rows // tm,),
        in_specs=[spec(a.shape) for a in ins],
        out_specs=[pl.BlockSpec((tm, c), lambda i: (i, 0)) for c in out_cols],
        out_shape=[jax.ShapeDtypeStruct((rows, c), dt) for c, dt in zip(out_cols, out_dtypes)],
        compiler_params=_cparams("parallel"),
    )(*ins)


def _rms_parts(xv):
    r = lax.rsqrt(jnp.mean(xv * xv, axis=-1, keepdims=True) + EPS)
    return r, xv * r


def _rms_bwd_dx(dh, gain, r, xh):
    dxh = dh * gain
    return r * (dxh - xh * jnp.mean(dxh * xh, axis=-1, keepdims=True))


def _ffn_chunks(f_all):
    return [slice(c, min(c + FFN_CHUNK, f_all)) for c in range(0, f_all, FFN_CHUNK)]


def _loss_head(xo, gain_f, target, d):
    r, xh = _rms_parts(xo)
    err = xh * gain_f - target
    dy = err * (1.0 / d)
    per_tok = jnp.mean(err * err, axis=-1, keepdims=True)
    return (_rms_bwd_dx(dy, gain_f, r, xh), jnp.sum(dy * xh, axis=0, keepdims=True),
            0.5 * jnp.sum(per_tok, axis=0, keepdims=True))


def _ffn_tile(x_ref, g_ref, wg_ref, wu_ref, wd_ref, h_ref, gg_ref, uu_ref):
    xv = x_ref[...]
    _, xh = _rms_parts(xv)
    h = (xh * g_ref[...]).astype(BF16)
    h_ref[...] = h
    acc = None
    for cols in _ffn_chunks(wd_ref.shape[0]):
        gg = _dot_nt(h, wg_ref[cols, :])
        uu = _dot_nt(h, wu_ref[cols, :])
        act = gg * _sigmoid(gg) * uu
        part = _dot_nn(act.astype(BF16), wd_ref[cols, :])
        acc = part if acc is None else acc + part
        gg_ref[:, cols] = gg.astype(BF16)
        uu_ref[:, cols] = uu.astype(BF16)
    return xv + 0.5 * acc


def _ffn_fwd(name, xs, gain, wg_t, wu_t, wd, next_gain, tm=512, deps=()):
    rows, d = xs.shape
    f_all = wd.shape[0]
    tm = min(tm, rows)

    def body(x_ref, g_ref, wg_ref, wu_ref, wd_ref, ng_ref, *rest):
        xo_ref, h_ref, gg_ref, uu_ref, hn_ref = rest[-5:]
        xo = _ffn_tile(x_ref, g_ref, wg_ref, wu_ref, wd_ref, h_ref, gg_ref, uu_ref)
        xo_ref[...] = xo
        hn_ref[...] = (_rms_parts(xo)[1] * ng_ref[...]).astype(BF16)

    tile = pl.BlockSpec((tm, d), lambda i: (i, 0))
    row = pl.BlockSpec((1, d), lambda i: (0, 0))
    wspec = pl.BlockSpec((f_all, d), lambda i: (0, 0), pipeline_mode=pl.Buffered(1))
    hid = pl.BlockSpec((tm, f_all), lambda i: (i, 0))
    return pl.pallas_call(
        body, name=name, grid=(rows // tm,),
        in_specs=[tile, row, wspec, wspec, wspec, row] + [_ANY_SPEC] * len(deps),
        out_specs=[tile, tile, hid, hid, tile],
        out_shape=[jax.ShapeDtypeStruct((rows, d), F32), jax.ShapeDtypeStruct((rows, d), BF16),
                   jax.ShapeDtypeStruct((rows, f_all), BF16), jax.ShapeDtypeStruct((rows, f_all), BF16),
                   jax.ShapeDtypeStruct((rows, d), BF16)],
        compiler_params=_cparams("parallel"),
    )(xs, gain, wg_t, wu_t, wd, next_gain, *deps)


def _ffn_fwd_head(name, xs, gain, wg_t, wu_t, wd, gain_f, target, tm=512):
    rows, d = xs.shape
    f_all = wd.shape[0]
    tm = min(tm, rows)

    def body(x_ref, g_ref, wg_ref, wu_ref, wd_ref, gf_ref, t_ref, dxo_ref, h_ref, gg_ref, uu_ref, dgf_ref, loss_ref):
        xo = _ffn_tile(x_ref, g_ref, wg_ref, wu_ref, wd_ref, h_ref, gg_ref, uu_ref)
        dxo, dgf, loss = _loss_head(xo, gf_ref[...], t_ref[...], d)
        dxo_ref[...] = dxo

        @pl.when(pl.program_id(0) == 0)
        def _():
            dgf_ref[...] = jnp.zeros_like(dgf_ref)
            loss_ref[...] = jnp.zeros_like(loss_ref)

        dgf_ref[...] += dgf
        loss_ref[...] += loss

    tile = pl.BlockSpec((tm, d), lambda i: (i, 0))
    row = pl.BlockSpec((1, d), lambda i: (0, 0))
    wspec = pl.BlockSpec((f_all, d), lambda i: (0, 0), pipeline_mode=pl.Buffered(1))
    hid = pl.BlockSpec((tm, f_all), lambda i: (i, 0))
    return pl.pallas_call(
        body, name=name, grid=(rows // tm,),
        in_specs=[tile, row, wspec, wspec, wspec, row, tile],
        out_specs=[tile, tile, hid, hid, row, pl.BlockSpec((1, 1), lambda i: (0, 0))],
        out_shape=[jax.ShapeDtypeStruct((rows, d), F32), jax.ShapeDtypeStruct((rows, d), BF16),
                   jax.ShapeDtypeStruct((rows, f_all), BF16), jax.ShapeDtypeStruct((rows, f_all), BF16),
                   jax.ShapeDtypeStruct((1, d), F32), jax.ShapeDtypeStruct((1, 1), F32)],
        compiler_params=_cparams("arbitrary"),
    )(xs, gain, wg_t, wu_t, wd, gain_f, target)


def _ffn_bwd(name, dxo, xs, gain, gg_all, uu_all, wg_t, wu_t, wd, tm=256):
    rows, d = xs.shape
    f_all = wd.shape[0]
    tm = min(tm, rows)

    def body(dxo_ref, x_ref, g_ref, gg_ref, uu_ref, wg_ref, wu_ref, wd_ref,
             dx_ref, dgg_ref, duu_ref, act_ref, dgain_ref):
        dxo = dxo_ref[...]
        df = (0.5 * dxo).astype(BF16)
        dh = None
        for cols in _ffn_chunks(f_all):
            gg = gg_ref[:, cols].astype(F32)
            uu = uu_ref[:, cols].astype(F32)
            sg = _sigmoid(gg)
            silu = gg * sg
            dact = _dot_nt(df, wd_ref[cols, :])
            duu = (dact * silu).astype(BF16)
            dgg = (dact * uu * (sg * (1.0 + gg * (1.0 - sg)))).astype(BF16)
            act_ref[:, cols] = (silu * uu).astype(BF16)
            dgg_ref[:, cols] = dgg
            duu_ref[:, cols] = duu
            part = _dot_nn(dgg, wg_ref[cols, :]) + _dot_nn(duu, wu_ref[cols, :])
            dh = part if dh is None else dh + part
        r, xh = _rms_parts(x_ref[...])
        dx_ref[...] = dxo + _rms_bwd_dx(dh, g_ref[...], r, xh)

        @pl.when(pl.program_id(0) == 0)
        def _():
            dgain_ref[...] = jnp.zeros_like(dgain_ref)

        dgain_ref[...] += jnp.sum(dh * xh, axis=0, keepdims=True)

    tile = pl.BlockSpec((tm, d), lambda i: (i, 0))
    row = pl.BlockSpec((1, d), lambda i: (0, 0))
    wspec = pl.BlockSpec((f_all, d), lambda i: (0, 0), pipeline_mode=pl.Buffered(1))
    hid = pl.BlockSpec((tm, f_all), lambda i: (i, 0))
    hid_shape = jax.ShapeDtypeStruct((rows, f_all), BF16)
    return pl.pallas_call(
        body, name=name, grid=(rows // tm,),
        in_specs=[tile, tile, row, hid, hid, wspec, wspec, wspec],
        out_specs=[tile, hid, hid, hid, row],
        out_shape=[jax.ShapeDtypeStruct((rows, d), F32), hid_shape, hid_shape, hid_shape,
                   jax.ShapeDtypeStruct((1, d), F32)],
        compiler_params=_cparams("arbitrary"),
    )(dxo, xs, gain, gg_all, uu_all, wg_t, wu_t, wd)


def _t5_bucket_np(dist):
    max_exact = N_BUCKETS // 2
    dd = np.maximum(dist, 1).astype(np.float32)
    large = max_exact + (np.log(dd / np.float32(max_exact)) / np.float32(math.log(MAX_DISTANCE / max_exact))
                         * np.float32(N_BUCKETS - max_exact)).astype(np.int32)
    large = np.minimum(large, N_BUCKETS - 1)
    return np.where(dist < max_exact, dist, large).astype(np.int32)


def _attn_geometry(g, rows):
    run = rows // 16
    dil = DILATIONS[g]
    if dil == 16:
        bq = BLOCK
        return dict(view=(16, run), block=(None, bq), grid=(16, run // bq), index=lambda r, n: (r, n),
                    pos=np.arange(bq), bq=bq)
    if dil == 4:
        per = BLOCK // 4
        pos = (4 * np.arange(per)[None, :] + np.arange(4)[:, None]).reshape(-1)
        return dict(view=(4, 4, run), block=(4, None, per), grid=(4, run // per), index=lambda r, n: (0, r, n),
                    pos=pos, bq=BLOCK)
    per = 16
    pos = (16 * np.arange(per)[None, :] + np.arange(16)[:, None]).reshape(-1)
    return dict(view=(16, run), block=(16, per), grid=(1, run // per), index=lambda r, n: (0, n),
                pos=pos, bq=16 * per)


def _attn_tables(g, rows):
    geo = _attn_geometry(g, rows)
    pos, bq = geo["pos"], geo["bq"]
    steps = pos[:, None] - np.concatenate([pos - bq, pos])[None, :]
    valid = (steps >= 0) & (steps <= BLOCK)
    bucket = _t5_bucket_np((np.maximum(steps, 0) * DILATIONS[g]).astype(np.int32))
    return bucket, valid.astype(np.int32)


def _bias_fwd(name, bucket, valid, table_t):
    bq = bucket.shape[0]

    def body(bk_ref, ok_ref, tab_ref, o_ref):
        bk = bk_ref[...]
        ok = ok_ref[...] > 0
        accs = [jnp.zeros(bk.shape, F32)] * HEADS_PER_GROUP
        for b in range(N_BUCKETS):
            hit = bk == b
            accs = [jnp.where(hit, tab_ref[h, b], acc) for h, acc in enumerate(accs)]
        for h, acc in enumerate(accs):
            o_ref[h] = jnp.where(ok, acc, NEG_INF)

    vm = pl.BlockSpec(memory_space=pltpu.VMEM)
    return pl.pallas_call(
        body, name=name, in_specs=[vm, vm, pl.BlockSpec(memory_space=pltpu.SMEM)], out_specs=vm,
        out_shape=jax.ShapeDtypeStruct((HEADS_PER_GROUP, bq, 2 * bq), F32),
    )(bucket, valid, table_t)


def _bias_bwd(name, bucket, dbias):
    def body(bk_ref, db_ref, o_ref):
        row_id = lax.broadcasted_iota(jnp.int32, (N_BUCKETS, 128), 0)
        col_id = lax.broadcasted_iota(jnp.int32, (N_BUCKETS, 128), 1)
        bk = bk_ref[...]
        acc = jnp.zeros((N_BUCKETS, 128), F32)
        for h in range(HEADS_PER_GROUP):
            db = db_ref[h]
            for b in range(N_BUCKETS):
                part = jnp.sum(jnp.where(bk == b, db, 0.0), axis=0, keepdims=True)
                tot = jnp.sum(part, axis=1, keepdims=True)
                acc = jnp.where((row_id == b) & (col_id == h), tot, acc)
        o_ref[...] = acc

    vm = pl.BlockSpec(memory_space=pltpu.VMEM)
    return pl.pallas_call(body, name=name, in_specs=[vm, vm], out_specs=vm,
                          out_shape=jax.ShapeDtypeStruct((N_BUCKETS, 128), F32))(bucket, dbias)


def _head_of_lane(nrows):
    return lax.broadcasted_iota(jnp.int32, (nrows, ATTN_OUT), 1) // HEAD_DIM


def _stack_heads(a, lane_head):
    zero = jnp.zeros_like(a)
    return jnp.concatenate([jnp.where(lane_head == h, a, zero) for h in range(HEADS_PER_GROUP)], axis=0)


def _unstack_heads(a4, lane_head, bq):
    out = a4[:bq]
    for h in range(1, HEADS_PER_GROUP):
        out = jnp.where(lane_head == h, a4[h * bq:(h + 1) * bq], out)
    return out


def _attn_specs(geo, cols, col_block, index):
    return pl.BlockSpec(geo["block"] + (cols,), lambda r, n: index(r, n) + (col_block,))


def _attn_fwd(name, qkv, g, bias4):
    rows = qkv.shape[0]
    geo = _attn_geometry(g, rows)
    bq, (nsub, nb), index = geo["bq"], geo["grid"], geo["index"]
    blk_shape = tuple(b for b in geo["block"] if b is not None) + (ATTN_OUT,)

    def body(q_ref, kc_ref, kp_ref, vc_ref, vp_ref, b_ref, o_ref, lse_ref):
        n = pl.program_id(1)
        lane_head = _head_of_lane(bq)
        flat = lambda ref: ref[...].reshape(bq, ATTN_OUT)
        q4 = _stack_heads(flat(q_ref), lane_head)
        k2 = jnp.concatenate([flat(kp_ref), flat(kc_ref)], axis=0)
        v2 = jnp.concatenate([flat(vp_ref), flat(vc_ref)], axis=0)
        s = _dot_nt(q4, k2) + b_ref[...]
        col = lax.broadcasted_iota(jnp.int32, s.shape, 1)
        s = jnp.where((col >= bq) | (n > 0), s, NEG_INF)
        mx = jnp.max(s, axis=-1, keepdims=True)
        p = jnp.exp(s - mx)
        den = jnp.sum(p, axis=-1, keepdims=True)
        o4 = _dot_nn(p.astype(BF16), v2) / den
        lse4 = jnp.broadcast_to(mx + jnp.log(den), (HEADS_PER_GROUP * bq, ATTN_OUT))
        o_ref[...] = _unstack_heads(o4, lane_head, bq).reshape(blk_shape)
        lse_ref[...] = _unstack_heads(lse4, lane_head, bq).reshape(blk_shape)

    prev = lambda r, n: index(r, jnp.maximum(n - 1, 0))
    view = lambda a: a.reshape(geo["view"] + (a.shape[1],))
    qkv_v = view(qkv)
    out_spec = _attn_specs(geo, ATTN_OUT, 0, index)
    out_shape = jax.ShapeDtypeStruct(geo["view"] + (ATTN_OUT,), F32)
    o, lse = pl.pallas_call(
        body, name=name, grid=(nsub, nb),
        in_specs=[_attn_specs(geo, ATTN_OUT, g, index), _attn_specs(geo, ATTN_OUT, 3 + g, index),
                  _attn_specs(geo, ATTN_OUT, 3 + g, prev), _attn_specs(geo, ATTN_OUT, 6 + g, index),
                  _attn_specs(geo, ATTN_OUT, 6 + g, prev), pl.BlockSpec(bias4.shape, lambda r, n: (0, 0))],
        out_specs=[out_spec, out_spec], out_shape=[out_shape, out_shape],
        compiler_params=_cparams("parallel", "arbitrary"),
    )(qkv_v, qkv_v, qkv_v, qkv_v, qkv_v, bias4)
    return o.reshape(rows, ATTN_OUT), lse.reshape(rows, ATTN_OUT)


def _attn_bwd(name, qkv, do, lse, cvec, g, bias4):
    rows = qkv.shape[0]
    geo = _attn_geometry(g, rows)
    bq, (nsub, nb), index = geo["bq"], geo["grid"], geo["index"]
    blk_shape = tuple(b for b in geo["block"] if b is not None) + (ATTN_OUT,)
    nlead = len(blk_shape) - 1

    def body(q_ref, kc_ref, kp_ref, vc_ref, vp_ref, do_ref, lse_ref, c_ref, b_ref,
             dq_ref, dk_ref, dv_ref, db_ref, kcar_ref, vcar_ref):
        r, n = pl.program_id(0), pl.program_id(1)
        valid = n < nb
        lane_head = _head_of_lane(bq)
        flat = lambda ref: ref[...].reshape(bq, ATTN_OUT)

        @pl.when((r == 0) & (n == 0))
        def _():
            kcar_ref[...] = jnp.zeros_like(kcar_ref)
            vcar_ref[...] = jnp.zeros_like(vcar_ref)
            db_ref[...] = jnp.zeros_like(db_ref)

        def column(ref, h):
            lead = (slice(None),) * nlead
            return ref[lead + (pl.ds(h * HEAD_DIM, 1),)].reshape(bq, 1)

        q4 = _stack_heads(flat(q_ref), lane_head)
        do4 = _stack_heads(flat(do_ref), lane_head)
        k2 = jnp.concatenate([flat(kp_ref), flat(kc_ref)], axis=0)
        v2 = jnp.concatenate([flat(vp_ref), flat(vc_ref)], axis=0)
        lse4 = jnp.concatenate([column(lse_ref, h) for h in range(HEADS_PER_GROUP)], axis=0)
        c4 = jnp.concatenate([column(c_ref, h) for h in range(HEADS_PER_GROUP)], axis=0)
        s = _dot_nt(q4, k2) + b_ref[...]
        col = lax.broadcasted_iota(jnp.int32, s.shape, 1)
        keep = ((col >= bq) | (n > 0)) & valid
        p = jnp.where(keep, jnp.exp(s - lse4), 0.0)
        ds = p * (_dot_nt(do4, v2) + c4)
        ds_b = ds.astype(BF16)

        @pl.when(valid)
        def _():
            dq = _unstack_heads(_dot_nn(ds_b, k2), lane_head, bq) * (HEAD_DIM ** -0.5)
            dq_ref[...] = dq.astype(BF16).reshape(blk_shape)

        dk2 = _dot_tn(ds_b, q4)
        dv2 = _dot_tn(p.astype(BF16), do4)
        dk_ref[...] = (kcar_ref[...] + dk2[:bq]).astype(BF16).reshape(blk_shape)
        dv_ref[...] = (vcar_ref[...] + dv2[:bq]).astype(BF16).reshape(blk_shape)
        kcar_ref[...] = dk2[bq:]
        vcar_ref[...] = dv2[bq:]
        db_ref[...] += ds

    cur = lambda r, n: index(r, jnp.minimum(n, nb - 1))
    prev = lambda r, n: index(r, jnp.maximum(jnp.minimum(n, nb - 1) - 1, 0))
    late = lambda r, n: index(r, jnp.maximum(n - 1, 0))
    view = lambda a: a.reshape(geo["view"] + (a.shape[1],))
    qkv_v = view(qkv)
    tile = _attn_specs(geo, ATTN_OUT, 0, cur)
    bias_spec = pl.BlockSpec(bias4.shape, lambda r, n: (0, 0))
    out_shape = jax.ShapeDtypeStruct(geo["view"] + (ATTN_OUT,), BF16)
    dq, dk, dv, db = pl.pallas_call(
        body, name=name, grid=(nsub, nb + 1),
        in_specs=[_attn_specs(geo, ATTN_OUT, g, cur), _attn_specs(geo, ATTN_OUT, 3 + g, cur),
                  _attn_specs(geo, ATTN_OUT, 3 + g, prev), _attn_specs(geo, ATTN_OUT, 6 + g, cur),
                  _attn_specs(geo, ATTN_OUT, 6 + g, prev), tile, tile, tile, bias_spec],
        out_specs=[tile, _attn_specs(geo, ATTN_OUT, 0, late), _attn_specs(geo, ATTN_OUT, 0, late), bias_spec],
        out_shape=[out_shape, out_shape, out_shape, jax.ShapeDtypeStruct(bias4.shape, F32)],
        scratch_shapes=[pltpu.VMEM((bq, ATTN_OUT), F32), pltpu.VMEM((bq, ATTN_OUT), F32)],
        compiler_params=_cparams("arbitrary", "arbitrary"),
    )(qkv_v, qkv_v, qkv_v, qkv_v, qkv_v, view(do), view(lse), view(cvec), bias4)
    return dq.reshape(rows, ATTN_OUT), dk.reshape(rows, ATTN_OUT), dv.reshape(rows, ATTN_OUT), db


def _group_weights(lses):
    mx = jnp.maximum(jnp.maximum(lses[0], lses[1]), lses[2])
    es = [jnp.exp(l - mx) for l in lses]
    den = es[0] + es[1] + es[2]
    return [e / den for e in es]


def _combine_fwd(name, os_, lses):
    def fn(o0, o1, o2, l0, l1, l2):
        ws = _group_weights([l0, l1, l2])
        out = ws[0] * o0 + ws[1] * o1 + ws[2] * o2
        return out, out

    return _ew(name, fn, [*os_, *lses], [ATTN_OUT, ATTN_OUT], [F32, BF16], tm=1024)


def _combine_bwd(name, do, oa, lses):
    def fn(dov, oav, l0, l1, l2):
        head_sum = (lax.broadcasted_iota(jnp.int32, (ATTN_OUT, ATTN_OUT), 0) // HEAD_DIM
                    == lax.broadcasted_iota(jnp.int32, (ATTN_OUT, ATTN_OUT), 1) // HEAD_DIM)
        ws = _group_weights([l0, l1, l2])
        prod = dov * oav
        hi = prod.astype(BF16)
        lo = (prod - hi.astype(F32)).astype(BF16)
        ones = jnp.where(head_sum, 1.0, 0.0).astype(BF16)
        bar = _dot_nn(hi, ones) + _dot_nn(lo, ones)
        return tuple(w * dov for w in ws) + tuple(-w * bar for w in ws)

    return _ew(name, fn, [do, oa, *lses], [ATTN_OUT] * 6, [BF16] * 3 + [F32] * 3, tm=1024)


def _ssm_disc(a_re, a_im, log_dt, b_re, b_im):
    dt = jnp.exp(log_dt)
    mag = jnp.exp(a_re * dt)
    ab_re = mag * jnp.cos(a_im * dt)
    ab_im = mag * jnp.sin(a_im * dt)
    den = a_re * a_re + a_im * a_im
    xr = ab_re - 1.0
    coef_re = (xr * a_re + ab_im * a_im) / den
    coef_im = (ab_im * a_re - xr * a_im) / den
    bb_re = coef_re[None] * b_re - coef_im[None] * b_im
    bb_im = coef_re[None] * b_im + coef_im[None] * b_re
    return ab_re, ab_im, bb_re, bb_im


def _ssm_params_fwd(name, a_re, a_im, log_dt, b_re, b_im):
    pows = jax.ShapeDtypeStruct((SCAN_STEPS,) + a_re.shape, F32)
    cgn = jax.ShapeDtypeStruct(b_re.shape, F32)

    def body(ar, ai, ld, br, bi, o_pr, o_pi, o_bbr, o_bbi):
        ab_re, ab_im, bb_re, bb_im = _ssm_disc(ar[...], ai[...], ld[...], br[...], bi[...])
        pr, pi = ab_re, ab_im
        for j in range(SCAN_STEPS):
            o_pr[j] = pr
            o_pi[j] = pi
            pr, pi = pr * ab_re - pi * ab_im, pr * ab_im + pi * ab_re
        o_bbr[...] = bb_re
        o_bbi[...] = bb_im

    vm = pl.BlockSpec(memory_space=pltpu.VMEM)
    return pl.pallas_call(body, name=name, in_specs=[vm] * 5, out_specs=[vm] * 4,
                          out_shape=[pows, pows, cgn, cgn])(a_re, a_im, log_dt, b_re, b_im)


def _ssm_params_bwd(name, a_re, a_im, log_dt, b_re, b_im, d_ab_re, d_ab_im, d_bb_re, d_bb_im):
    gn = jax.ShapeDtypeStruct(a_re.shape, F32)
    cgn = jax.ShapeDtypeStruct(b_re.shape, F32)

    def body(ar, ai, ld, br, bi, g0, g1, g2, g3, o_ar, o_ai, o_ld, o_br, o_bi):
        _, vjp = jax.vjp(_ssm_disc, ar[...], ai[...], ld[...], br[...], bi[...])
        outs = vjp((g0[...], g1[...], g2[...], g3[...]))
        for o_ref, o in zip((o_ar, o_ai, o_ld, o_br, o_bi), outs):
            o_ref[...] = o

    vm = pl.BlockSpec(memory_space=pltpu.VMEM)
    return pl.pallas_call(body, name=name, in_specs=[vm] * 9, out_specs=[vm] * 5,
                          out_shape=[gn, gn, jax.ShapeDtypeStruct(log_dt.shape, F32), cgn, cgn],
                          )(a_re, a_im, log_dt, b_re, b_im, d_ab_re, d_ab_im, d_bb_re, d_bb_im)


def _scan_block(s_ref, carry_ref, tmp_ref, pw_ref, reverse, sprev=None):
    nl = SSM_LANES
    halves = range(SCAN_COLS // SCAN_SUB)
    zero = jnp.zeros((SCAN_SUB, SCAN_LANES), F32)
    for half in (reversed(halves) if reverse else halves):
        sub_rows = pl.ds(half * SCAN_SUB, SCAN_SUB)
        for lc in range(nl // SCAN_LANES):
            re_l = pl.ds(lc * SCAN_LANES, SCAN_LANES)
            im_l = pl.ds(nl + lc * SCAN_LANES, SCAN_LANES)
            are, aim = pw_ref[0, :, re_l], pw_ref[0, :, im_l]

            def step_of(j):
                return SCAN_STEPS - 1 - j if reverse else j

            def pass1(j, st):
                sr, si = st
                jj = step_of(j)
                nr = are * sr - aim * si + s_ref[jj, sub_rows, re_l]
                ni = are * si + aim * sr + s_ref[jj, sub_rows, im_l]
                s_ref[jj, sub_rows, re_l] = nr
                s_ref[jj, sub_rows, im_l] = ni
                return nr, ni

            er, ei = lax.fori_loop(0, SCAN_STEPS, pass1, (zero, zero), unroll=2)
            tmp_ref[0:SCAN_SUB, re_l] = er
            tmp_ref[0:SCAN_SUB, im_l] = ei
            apr, api = pw_ref[SCAN_STEPS - 1, 0:1, re_l], pw_ref[SCAN_STEPS - 1, 0:1, im_l]
            sr, si = carry_ref[0:1, re_l], carry_ref[0:1, im_l]
            for step in range(SCAN_SUB):
                c = SCAN_SUB - 1 - step if reverse else step
                tmp_ref[SCAN_SUB + c:SCAN_SUB + c + 1, re_l] = sr
                tmp_ref[SCAN_SUB + c:SCAN_SUB + c + 1, im_l] = si
                e_r, e_i = tmp_ref[c:c + 1, re_l], tmp_ref[c:c + 1, im_l]
                sr, si = apr * sr - api * si + e_r, apr * si + api * sr + e_i
            carry_ref[0:1, re_l] = sr
            carry_ref[0:1, im_l] = si
            cr = tmp_ref[SCAN_SUB:2 * SCAN_SUB, re_l]
            ci = tmp_ref[SCAN_SUB:2 * SCAN_SUB, im_l]

            if sprev is None:
                def pass2(j, st):
                    pr, pi = pw_ref[j, :, re_l], pw_ref[j, :, im_l]
                    jj = step_of(j)
                    s_ref[jj, sub_rows, re_l] += pr * cr - pi * ci
                    s_ref[jj, sub_rows, im_l] += pr * ci + pi * cr
                    return st

                lax.fori_loop(0, SCAN_STEPS, pass2, 0, unroll=2)
            else:
                st_ref, prev_ref, have_prev, dab_ref = sprev

                def corrected(jj, pr, pi):
                    gr = s_ref[jj, sub_rows, re_l] + pr * cr - pi * ci
                    gi = s_ref[jj, sub_rows, im_l] + pr * ci + pi * cr
                    s_ref[jj, sub_rows, re_l] = gr
                    s_ref[jj, sub_rows, im_l] = gi
                    return gr, gi

                def pass2(j, st):
                    dr, di = st
                    jj = SCAN_STEPS - 1 - j
                    gr, gi = corrected(jj, pw_ref[j, :, re_l], pw_ref[j, :, im_l])
                    qr, qi = st_ref[jj - 1, sub_rows, re_l], st_ref[jj - 1, sub_rows, im_l]
                    return dr + gr * qr + gi * qi, di + gi * qr - gr * qi

                dr, di = lax.fori_loop(0, SCAN_STEPS - 1, pass2, (zero, zero), unroll=2)
                gr, gi = corrected(0, pw_ref[SCAN_STEPS - 1, :, re_l], pw_ref[SCAN_STEPS - 1, :, im_l])
                sub = lax.broadcasted_iota(jnp.int32, (SCAN_SUB, SCAN_LANES), 0)
                if half == 0:
                    pv_r = prev_ref[SCAN_SUB - 1:SCAN_SUB, re_l] * have_prev
                    pv_i = prev_ref[SCAN_SUB - 1:SCAN_SUB, im_l] * have_prev
                else:
                    before = pl.ds(half * SCAN_SUB - 1, 1)
                    pv_r, pv_i = st_ref[SCAN_STEPS - 1, before, re_l], st_ref[SCAN_STEPS - 1, before, im_l]
                shape = (SCAN_SUB, SCAN_LANES)
                qr = jnp.where(sub == 0, jnp.broadcast_to(pv_r, shape),
                               pltpu.roll(st_ref[SCAN_STEPS - 1, sub_rows, re_l], 1, 0))
                qi = jnp.where(sub == 0, jnp.broadcast_to(pv_i, shape),
                               pltpu.roll(st_ref[SCAN_STEPS - 1, sub_rows, im_l], 1, 0))
                dab_ref[:, re_l] += dr + gr * qr + gi * qi
                dab_ref[:, im_l] += di + gi * qr - gr * qi


def _scan_view(a):
    return a.reshape(16, a.shape[0] // 16, a.shape[1])


def _pair_tile(p):
    start = (p * 2 * SSM_GROUP // PAIR_TILE) * PAIR_TILE
    return slice(start, start + PAIR_TILE)


def _pair_lanes(p):
    return pl.ds(p * PAIR_LANES, PAIR_LANES), pl.ds(SSM_LANES + p * PAIR_LANES, PAIR_LANES)


def _pair_store(s_ref, p, val):
    re_l, im_l = _pair_lanes(p)
    s_ref[:, :, re_l] = val[:, :PAIR_LANES].reshape(16, SCAN_COLS, PAIR_LANES)
    s_ref[:, :, im_l] = val[:, PAIR_LANES:].reshape(16, SCAN_COLS, PAIR_LANES)


def _pair_load(s_ref, p):
    re_l, im_l = _pair_lanes(p)
    parts = [s_ref[:, :, l].reshape(SCAN_BLOCK, PAIR_LANES) for l in (re_l, im_l)]
    return jnp.concatenate(parts, axis=1).astype(BF16)


def _pair_sum(fn):
    per = PAIR_TILE // (2 * SSM_GROUP)
    tiles = []
    for t in range(SSM_PAIRS // per):
        acc = None
        for p in range(t * per, (t + 1) * per):
            part = fn(p)
            acc = part if acc is None else acc + part
        tiles.append(acc)
    return jnp.concatenate(tiles, axis=1)


def _ssm_fwd(name, u, bb_mats, c_mats, pw_rows, d_skip):
    rows = u.shape[0]
    nl2 = 2 * SSM_LANES
    nblk = rows // SCAN_BLOCK

    def body(u_ref, bb_ref, c_ref, pw_ref, d_ref, y_ref, yg_ref, s_ref, carry_ref, tmp_ref):
        @pl.when(pl.program_id(0) == 0)
        def _():
            carry_ref[...] = jnp.zeros_like(carry_ref)

        uv = u_ref[...].reshape(SCAN_BLOCK, SSM_WIDTH)
        ub = uv.astype(BF16)
        for p in range(SSM_PAIRS):
            _pair_store(s_ref, p, _dot_nn(ub[:, _pair_tile(p)], bb_ref[p]))
        _scan_block(s_ref, carry_ref, tmp_ref, pw_ref, reverse=False)
        ys = _pair_sum(lambda p: _dot_nt(_pair_load(s_ref, p), c_ref[p]))
        yv = ys + d_ref[...] * uv
        y_ref[...] = yv.reshape(16, SCAN_COLS, SSM_WIDTH)
        yg_ref[...] = jax.nn.gelu(yv).astype(BF16).reshape(16, SCAN_COLS, SSM_WIDTH)

    const = lambda shape: pl.BlockSpec(shape, lambda i: (0,) * len(shape))
    blk = lambda cols: pl.BlockSpec((16, SCAN_COLS, cols), lambda i: (0, i, 0))
    pair_mats = const((SSM_PAIRS, PAIR_TILE, PAIR_TILE))
    y, yg, s = pl.pallas_call(
        body, name=name, grid=(nblk,),
        in_specs=[blk(SSM_WIDTH), pair_mats, pair_mats, const((SCAN_STEPS, SCAN_SUB, nl2)), const((1, SSM_WIDTH))],
        out_specs=[blk(SSM_WIDTH), blk(SSM_WIDTH), blk(nl2)],
        out_shape=[jax.ShapeDtypeStruct((16, rows // 16, SSM_WIDTH), F32),
                   jax.ShapeDtypeStruct((16, rows // 16, SSM_WIDTH), BF16),
                   jax.ShapeDtypeStruct((16, rows // 16, nl2), F32)],
        scratch_shapes=[pltpu.VMEM((SCAN_SUB, nl2), F32), pltpu.VMEM((2 * SCAN_SUB, nl2), F32)],
        compiler_params=_cparams("arbitrary"),
    )(_scan_view(u), bb_mats, c_mats, pw_rows, d_skip)
    return y.reshape(rows, SSM_WIDTH), yg.reshape(rows, SSM_WIDTH), s.reshape(rows, nl2)


def _ssm_bwd(name, dy, u, states, bb_mats, c_mats, pwc_rows, d_skip):
    rows = u.shape[0]
    nl2 = 2 * SSM_LANES
    nblk = rows // SCAN_BLOCK

    def body(dy_ref, u_ref, st_ref, prev_ref, bb_ref, c_ref, pw_ref, d_ref,
             du_ref, dbb_ref, dc_ref, dab_ref, dd_ref, g_ref, carry_ref, tmp_ref):
        i = pl.program_id(0)

        @pl.when(i == 0)
        def _():
            carry_ref[...] = jnp.zeros_like(carry_ref)
            for ref in (dbb_ref, dc_ref, dab_ref, dd_ref):
                ref[...] = jnp.zeros_like(ref)

        dyv = dy_ref[...].reshape(SCAN_BLOCK, SSM_WIDTH)
        uv = u_ref[...].reshape(SCAN_BLOCK, SSM_WIDTH)
        dyb, ub = dyv.astype(BF16), uv.astype(BF16)
        for p in range(SSM_PAIRS):
            _pair_store(g_ref, p, _dot_nn(dyb[:, _pair_tile(p)], c_ref[p]))
        have_prev = (i < nblk - 1).astype(F32)
        _scan_block(g_ref, carry_ref, tmp_ref, pw_ref, reverse=True,
                    sprev=(st_ref, prev_ref, have_prev, dab_ref))

        def pair_work(p):
            gp = _pair_load(g_ref, p)
            dbb_ref[p] += _dot_tn(ub[:, _pair_tile(p)], gp)
            dc_ref[p] += _dot_tn(dyb[:, _pair_tile(p)], _pair_load(st_ref, p))
            return _dot_nt(gp, bb_ref[p])

        du_ref[...] = (_pair_sum(pair_work) + d_ref[...] * dyv).reshape(16, SCAN_COLS, SSM_WIDTH)
        dd_ref[...] += jnp.sum(dyv * uv, axis=0, keepdims=True)

    const = lambda shape: pl.BlockSpec(shape, lambda i: (0,) * len(shape))
    blk = lambda cols: pl.BlockSpec((16, SCAN_COLS, cols), lambda i: (0, nblk - 1 - i, 0))
    per8 = SCAN_COLS // SCAN_SUB
    prev_spec = pl.BlockSpec((None, SCAN_SUB, nl2), lambda i: (15, jnp.maximum((nblk - 1 - i) * per8 - 1, 0), 0))
    pair_mats = const((SSM_PAIRS, PAIR_TILE, PAIR_TILE))
    pair_shape = jax.ShapeDtypeStruct((SSM_PAIRS, PAIR_TILE, PAIR_TILE), F32)
    sv = _scan_view(states)
    du, dbb, dc, dab, dd = pl.pallas_call(
        body, name=name, grid=(nblk,),
        in_specs=[blk(SSM_WIDTH), blk(SSM_WIDTH), blk(nl2), prev_spec, pair_mats, pair_mats,
                  const((SCAN_STEPS, SCAN_SUB, nl2)), const((1, SSM_WIDTH))],
        out_specs=[blk(SSM_WIDTH), pair_mats, pair_mats, const((SCAN_SUB, nl2)), const((1, SSM_WIDTH))],
        out_shape=[jax.ShapeDtypeStruct((16, rows // 16, SSM_WIDTH), F32), pair_shape, pair_shape,
                   jax.ShapeDtypeStruct((SCAN_SUB, nl2), F32), jax.ShapeDtypeStruct((1, SSM_WIDTH), F32)],
        scratch_shapes=[pltpu.VMEM((16, SCAN_COLS, nl2), F32), pltpu.VMEM((SCAN_SUB, nl2), F32),
                        pltpu.VMEM((2 * SCAN_SUB, nl2), F32)],
        compiler_params=_cparams("arbitrary"),
    )(_scan_view(dy), _scan_view(u), sv, sv, bb_mats, c_mats, pwc_rows, d_skip)
    return du.reshape(rows, SSM_WIDTH), dbb, dc, dab, dd


def _adamw(name, w, m, v, gparts, tr):
    rows, cols = w.shape

    def body(w_ref, m_ref, v_ref, g_ref, *out_refs):
        _adam_store(_device_sum(g_ref), w_ref, m_ref, v_ref, out_refs)

    spec = pl.BlockSpec((tr, cols), lambda i: (i, 0))
    shape = jax.ShapeDtypeStruct((rows, cols), F32)
    return pl.pallas_call(
        body, name=name, grid=(rows // tr,),
        in_specs=[spec, spec, spec, pl.BlockSpec((N_DEV, tr, cols), lambda i: (0, i, 0))],
        out_specs=[spec] * 4, out_shape=[shape] * 4,
        compiler_params=_cparams("parallel"),
    )(w, m, v, gparts)


def _device_sum(g_ref):
    g = g_ref[0].astype(F32)
    for i in range(1, N_DEV):
        g = g + g_ref[i].astype(F32)
    return g


def _adam_store(g, w_ref, m_ref, v_ref, out_refs):
    og_ref, od_ref, om_ref, ov_ref = out_refs
    m_new = B1 * m_ref[...] + (1.0 - B1) * g
    v_new = B2 * v_ref[...] + (1.0 - B2) * (g * g)
    m_hat = m_new / (1.0 - B1 ** STEP)
    v_hat = v_new / (1.0 - B2 ** STEP)
    og_ref[...] = g
    od_ref[...] = -LR * (m_hat / (jnp.sqrt(v_hat) + ADAM_EPS) + WD * w_ref[...])
    om_ref[...] = m_new
    ov_ref[...] = v_new


def _adamw_cols(name, w, m, v, gparts):
    k, c = w.shape

    def body(w_ref, m_ref, v_ref, g_ref, *out_refs):
        g_t = _device_sum(g_ref)
        hi = g_t.astype(BF16)
        lo = (g_t - hi.astype(F32)).astype(BF16)
        eye = (lax.broadcasted_iota(jnp.int32, (c, c), 0) == lax.broadcasted_iota(jnp.int32, (c, c), 1)).astype(BF16)
        _adam_store(_dot_tn(hi, eye) + _dot_tn(lo, eye), w_ref, m_ref, v_ref, out_refs)

    vm = pl.BlockSpec(memory_space=pltpu.VMEM)
    return pl.pallas_call(
        body, name=name, in_specs=[vm] * 4, out_specs=[vm] * 4,
        out_shape=[jax.ShapeDtypeStruct((k, c), F32)] * 4,
        compiler_params=pltpu.CompilerParams(vmem_limit_bytes=VMEM_LIMIT_BYTES),
    )(w, m, v, gparts)


_SHARDED = (
    ("ffn1_w_gate", True, (352, 1024)), ("ffn1_w_up", True, (352, 1024)), ("ffn1_w_down", False, (352, 1024)),
    ("w_in", True, (608, 1024)), ("ssm_w_glu", True, (128, 512)), ("w_attn_branch", True, (128, 256)),
    ("w_ssm_branch", True, (128, 512)), ("w_out", False, (128, 1024)),
    ("ffn2_w_gate", True, (352, 1024)), ("ffn2_w_up", True, (352, 1024)), ("ffn2_w_down", False, (352, 1024)),
)
_SMALL = ("ffn1_norm", "mix_norm", "gate_bias", "rel_bias_table", "ssm_a_re", "ssm_a_im", "ssm_log_dt",
          "ssm_b_re", "ssm_b_im", "ssm_c_re", "ssm_c_im", "ssm_d", "ffn2_norm", "final_norm")
_ORDER = ("ffn1_norm", "ffn1_w_gate", "ffn1_w_up", "ffn1_w_down", "mix_norm", "w_in", "gate_bias",
          "rel_bias_table", "ssm_a_re", "ssm_a_im", "ssm_log_dt", "ssm_b_re", "ssm_b_im", "ssm_c_re",
          "ssm_c_im", "ssm_d", "ssm_w_glu", "w_attn_branch", "w_ssm_branch", "w_out", "ffn2_norm",
          "ffn2_w_gate", "ffn2_w_up", "ffn2_w_down", "final_norm")


def _pack_rows(shape):
    return shape[0] * shape[1] // D_MODEL


_SHARD_INFO = {nm: (tr, shape) for nm, tr, shape in _SHARDED}
_PHASES = {
    "f1gu": ("ffn1_w_gate", "ffn1_w_up"), "f1d": ("ffn1_w_down",),
    "mix": ("w_in", "ssm_w_glu", "w_attn_branch", "w_ssm_branch", "w_out"),
    "f2": ("ffn2_w_gate", "ffn2_w_up", "ffn2_w_down"),
}


def _to_rows(a, nm):
    tr, shape = _SHARD_INFO[nm]
    return (a.T if tr else a).reshape(_pack_rows(shape), D_MODEL)


def _full_weight(gathered, nm):
    _, shape = _SHARD_INFO[nm]
    return gathered.reshape(N_DEV * shape[0], shape[1])


def _grad_blocks(g, nm):
    _, shape = _SHARD_INFO[nm]
    return g.astype(BF16).reshape(N_DEV, _pack_rows(shape), D_MODEL)


_SMALL_TILE = 8 * 128


def _small_rows(a):
    flat = a.reshape(-1)
    return jnp.pad(flat, (0, (-flat.shape[0]) % _SMALL_TILE)).reshape(-1, 128)


def _pack_small(ws, last=None):
    tail = jnp.zeros((), F32) if last is None else last
    return jnp.concatenate([_small_rows(ws[nm]) for nm in _SMALL] + [_small_rows(tail)], axis=0)


def _unpack_small(pack, like):
    out, r0 = {}, 0
    for nm in _SMALL:
        n = like[nm].size
        nr = 8 * -(-n // _SMALL_TILE)
        out[nm] = pack[r0:r0 + nr].reshape(-1)[:n].reshape(like[nm].shape)
        r0 += nr
    return out


def _residue_order(a):
    rows, cols = a.shape
    return a.reshape(rows // 16, 16, cols).transpose(1, 0, 2).reshape(rows, cols)


def _token_order(a):
    rows, cols = a.shape
    return a.reshape(16, rows // 16, cols).transpose(1, 0, 2).reshape(rows, cols)


_PAIRS_PER_TILE = PAIR_TILE // (2 * SSM_GROUP)
_PAIR_AXES = (SSM_PAIRS // _PAIRS_PER_TILE, _PAIRS_PER_TILE, 2)


def _pair_matrices(re, im):
    six = jnp.stack([re, im]).reshape((2,) + _PAIR_AXES + (SSM_GROUP, SSM_STATE))
    eye_j, eye_l = jnp.eye(_PAIRS_PER_TILE, dtype=re.dtype), jnp.eye(2, dtype=re.dtype)
    mats = jnp.einsum("xkjlcn,jJ,lL->kjJLcxln", six, eye_j, eye_l)
    return mats.reshape(SSM_PAIRS, PAIR_TILE, PAIR_TILE).astype(BF16)


def _pair_diagonals(acc):
    k, j, l = _PAIR_AXES
    eight = acc.reshape(k, j, j, l, SSM_GROUP, 2, l, SSM_STATE)
    eye_j, eye_l = jnp.eye(j, dtype=acc.dtype), jnp.eye(l, dtype=acc.dtype)
    own = jnp.einsum("kjJLcxln,jJ,lL->xkjlcn", eight, eye_j, eye_l).reshape(2, SSM_GROUPS, SSM_GROUP, SSM_STATE)
    return own[0], own[1]


def _local_step(xs, target, small, weights_of, send_grads, first_deps=()):
    rows = xs.shape[0]
    gfull, gsmall = {}, {}

    table_t = small["rel_bias_table"].T
    tables, bias4 = [], []
    for g in range(N_GROUPS):
        bucket, valid = [jnp.asarray(t) for t in _attn_tables(g, rows)]
        bias_g = _bias_fwd(f"rel_bias_fwd_{g}", bucket, valid, table_t[g * HEADS_PER_GROUP:(g + 1) * HEADS_PER_GROUP])
        tables.append(bucket)
        bias4.append(bias_g.reshape(-1, bias_g.shape[-1]))
    pw_re, pw_im, bb_re, bb_im = _ssm_params_fwd(
        "ssm_params_fwd", small["ssm_a_re"], small["ssm_a_im"], small["ssm_log_dt"].reshape(SSM_GROUPS, 1),
        small["ssm_b_re"].transpose(2, 0, 1), small["ssm_b_im"].transpose(2, 0, 1))

    def power_rows(sign):
        row = jnp.concatenate([pw_re.reshape(SCAN_STEPS, 1, SSM_LANES), sign * pw_im.reshape(SCAN_STEPS, 1, SSM_LANES)],
                              axis=2)
        return jnp.broadcast_to(row, (SCAN_STEPS, SCAN_SUB, 2 * SSM_LANES))

    bb_mats = _pair_matrices(bb_re.transpose(1, 0, 2), bb_im.transpose(1, 0, 2))
    c_mats = _pair_matrices(small["ssm_c_re"], -small["ssm_c_im"])
    pw_fwd, pw_bwd = power_rows(1.0), power_rows(-1.0)
    d_skip = small["ssm_d"].reshape(1, SSM_WIDTH)
    wf = dict(weights_of("f1", [xs, target, bb_mats, c_mats, pw_fwd, pw_bwd] + bias4))

    x1, h1, gg1, uu1, hmix = _ffn_fwd("ffn1_fwd", xs, small["ffn1_norm"], wf["ffn1_w_gate"], wf["ffn1_w_up"],
                                      wf["ffn1_w_down"], small["mix_norm"], deps=first_deps)
    wf.update(weights_of("mix", x1))
    w_in = wf["w_in"]
    w_qkv, w_u, w_g = w_in[:3 * ATTN_WIDTH], w_in[3 * ATTN_WIDTH:3 * ATTN_WIDTH + SSM_WIDTH], w_in[3 * ATTN_WIDTH + SSM_WIDTH:]
    qscale = jnp.concatenate([jnp.full((1, ATTN_WIDTH), HEAD_DIM ** -0.5, F32), jnp.ones((1, 2 * ATTN_WIDTH), F32)], axis=1)
    qkv, = _mm("in_qkv", [(hmix, w_qkv)], True, 3 * ATTN_WIDTH, [BF16],
               epilogue=lambda acc, sc: (acc * sc,), extras=[(qscale, 0)], tn=ATTN_WIDTH)
    u, = _mm("in_u", [(hmix, w_u)], True, SSM_WIDTH, [F32])
    gates, = _mm("in_gates", [(hmix, w_g)], True, 2 * D_MODEL, [F32],
                 epilogue=lambda acc, b: (_sigmoid(acc + b),), extras=[(small["gate_bias"], 0)])

    o_g, lse_g = [], []
    for g in range(N_GROUPS):
        o, lse = _attn_fwd(f"attn_fwd_{g}", qkv, g, bias4[g])
        o_g.append(o)
        lse_g.append(lse)
    oa_f32, oa = _combine_fwd("attn_combine_fwd", o_g, lse_g)
    y_attn, = _mm("attn_branch", [(oa, wf["w_attn_branch"])], True, D_MODEL, [F32])
    y_raw, ygelu, states = _ssm_fwd("ssm_fwd", u, bb_mats, c_mats, pw_fwd, d_skip)
    glu, ysg = _mm("ssm_glu", [(ygelu, wf["ssm_w_glu"])], True, 2 * SSM_WIDTH, [F32, BF16],
                   epilogue=lambda gv: (gv, gv[:, :SSM_WIDTH] * _sigmoid(gv[:, SSM_WIDTH:])),
                   tn=2 * SSM_WIDTH, out_cols=[2 * SSM_WIDTH, SSM_WIDTH])
    y_ssm, merged = _mm("ssm_branch_merge", [(ysg, wf["w_ssm_branch"])], True, D_MODEL, [F32, BF16],
                        epilogue=lambda acc, ga, gs, ya: (acc, ga * ya + gs * acc),
                        extras=[(gates, 0), (gates, D_MODEL), (y_attn, 0)])
    x2, = _mm("mix_out", [(merged, wf["w_out"])], False, D_MODEL, [F32],
              epilogue=lambda acc, res: (res + acc,), extras=[(x1, 0)])
    wf.update(weights_of("f2", x2))
    dx3, h2, gg2, uu2, gsmall["final_norm"], gsmall["loss"] = _ffn_fwd_head(
        "ffn2_fwd", x2, small["ffn2_norm"], wf["ffn2_w_gate"], wf["ffn2_w_up"], wf["ffn2_w_down"],
        small["final_norm"].reshape(1, D_MODEL), target)

    dx2, dgg2, duu2, act2, gsmall["ffn2_norm"] = _ffn_bwd(
        "ffn2_bwd", dx3, x2, small["ffn2_norm"], gg2, uu2, wf["ffn2_w_gate"], wf["ffn2_w_up"], wf["ffn2_w_down"])
    gfull["ffn2_w_gate"] = _mm_tn("ffn2_dwg", dgg2, h2, out_dtype=BF16)
    gfull["ffn2_w_up"] = _mm_tn("ffn2_dwu", duu2, h2, out_dtype=BF16)
    gfull["ffn2_w_down"] = _mm_tn("ffn2_dwd", act2, dx3, scale=0.5, out_dtype=BF16)
    sent = send_grads("f2", gfull)

    def merge_bwd(dm, ga, gs, ya, ys):
        dza, dzs = dm * ya * ga * (1.0 - ga), dm * ys * gs * (1.0 - gs)
        return (dm * ga, dm * gs, dza, dzs, jnp.sum(dza, axis=0, keepdims=True), jnp.sum(dzs, axis=0, keepdims=True))

    dya, dys, dzga, dzgs, dba, dbs = _mm(
        "mix_out_bwd", [(dx2, wf["w_out"])], True, D_MODEL, [BF16] * 4, epilogue=merge_bwd, row_sums=2,
        extras=[(gates, 0), (gates, D_MODEL), (y_attn, 0), (y_ssm, 0)], deps=sent, tm=512, tn=D_MODEL)
    gfull["w_out"] = _mm_tn("dw_out", merged, dx2, out_dtype=BF16)
    gsmall["gate_bias"] = jnp.concatenate([dba, dbs], axis=1)

    gfull["w_ssm_branch"] = _mm_tn("dw_ssm_branch", dys, ysg, out_dtype=BF16)

    def glu_bwd(dysg, av, bv):
        sb = _sigmoid(bv)
        return (dysg * sb, dysg * av * sb * (1.0 - sb))

    dglu_a, dglu_b = _mm("ssm_branch_bwd", [(dys, wf["w_ssm_branch"])], False, SSM_WIDTH, [BF16, BF16],
                         epilogue=glu_bwd, extras=[(glu, 0), (glu, SSM_WIDTH)])
    w_glu = wf["ssm_w_glu"]
    gfull["ssm_w_glu"] = _mm_tn_stack("dw_glu", [dglu_a, dglu_b], ygelu, out_dtype=BF16)

    def gelu_bwd(acc, yv):
        _, vjp = jax.vjp(jax.nn.gelu, yv)
        return (vjp(acc)[0],)

    dy_raw, = _mm("ssm_glu_bwd", [(dglu_a, w_glu[:SSM_WIDTH]), (dglu_b, w_glu[SSM_WIDTH:])], False, SSM_WIDTH, [F32],
                  epilogue=gelu_bwd, extras=[(y_raw, 0)])
    du, dbb_acc, dc_acc, dab_rows, gsmall_d = _ssm_bwd(
        "ssm_bwd", dy_raw, u, states, bb_mats, c_mats, pw_bwd, d_skip)
    gsmall["ssm_d"] = gsmall_d
    dbb_re, dbb_im = [a.transpose(1, 0, 2) for a in _pair_diagonals(dbb_acc)]
    dc_re, dc_im = _pair_diagonals(dc_acc)
    gsmall["ssm_c_re"], gsmall["ssm_c_im"] = dc_re, -dc_im
    dab = _colsum("ssm_dab", dab_rows)
    d_ar, d_ai, d_ld, d_br, d_bi = _ssm_params_bwd(
        "ssm_params_bwd", small["ssm_a_re"], small["ssm_a_im"], small["ssm_log_dt"].reshape(SSM_GROUPS, 1),
        small["ssm_b_re"].transpose(2, 0, 1), small["ssm_b_im"].transpose(2, 0, 1),
        dab[:, :SSM_LANES].reshape(SSM_GROUPS, SSM_STATE), dab[:, SSM_LANES:].reshape(SSM_GROUPS, SSM_STATE),
        dbb_re, dbb_im)
    gsmall["ssm_a_re"], gsmall["ssm_a_im"], gsmall["ssm_log_dt"] = d_ar, d_ai, d_ld.reshape(SSM_GROUPS)
    gsmall["ssm_b_re"], gsmall["ssm_b_im"] = d_br.transpose(1, 2, 0), d_bi.transpose(1, 2, 0)

    gfull["w_attn_branch"] = _mm_tn("dw_attn_branch", dya, oa, out_dtype=BF16)
    doa, = _mm("attn_branch_bwd", [(dya, wf["w_attn_branch"])], False, ATTN_OUT, [F32])
    dc = _combine_bwd("attn_combine_bwd", doa, oa_f32, lse_g)
    dqkv_cols = [None] * 9
    dtable = []
    for g in range(N_GROUPS):
        dq, dk, dv, db = _attn_bwd(f"attn_bwd_{g}", qkv, dc[g], lse_g[g], dc[3 + g], g, bias4[g])
        dqkv_cols[g], dqkv_cols[3 + g], dqkv_cols[6 + g] = dq, dk, dv
        dt = _bias_bwd(f"rel_bias_bwd_{g}", tables[g], db.reshape(HEADS_PER_GROUP, -1, db.shape[-1]))
        dtable.append(dt[:, :HEADS_PER_GROUP])
    gsmall["rel_bias_table"] = jnp.concatenate(dtable, axis=1)

    gfull["w_in"] = jnp.concatenate([_mm_tn_stack("dw_in_qkv", dqkv_cols, hmix, out_dtype=BF16),
                                     _mm_tn_stack("dw_in_rest", [du, dzga, dzgs], hmix, out_dtype=BF16)], axis=0)
    sent = send_grads("mix", gfull)
    qkv_pairs = [(c, w_qkv[i * ATTN_OUT:(i + 1) * ATTN_OUT]) for i, c in enumerate(dqkv_cols)]

    def mix_norm_bwd(dh, xv, gain, dres):
        r, xh = _rms_parts(xv)
        return dres + _rms_bwd_dx(dh, gain, r, xh), jnp.sum(dh * xh, axis=0, keepdims=True)

    dx1, gsmall["mix_norm"] = _mm(
        "in_bwd", qkv_pairs + [(du, w_u), (dzga, w_g[:D_MODEL]), (dzgs, w_g[D_MODEL:])], False, D_MODEL, [F32],
        epilogue=mix_norm_bwd, row_sums=1, extras=[(x1, 0), (small["mix_norm"], 0), (dx2, 0)], tm=512, tn=D_MODEL,
        deps=sent)

    dx, dgg1, duu1, act1, gsmall["ffn1_norm"] = _ffn_bwd(
        "ffn1_bwd", dx1, xs, small["ffn1_norm"], gg1, uu1, wf["ffn1_w_gate"], wf["ffn1_w_up"], wf["ffn1_w_down"])
    sent = send_grads("small", gsmall)
    gfull["ffn1_w_gate"] = _mm_tn("ffn1_dwg", dgg1, h1, deps=sent, out_dtype=BF16)
    gfull["ffn1_w_up"] = _mm_tn("ffn1_dwu", duu1, h1, out_dtype=BF16)
    sent = send_grads("f1gu", gfull)
    gfull["ffn1_w_down"] = _mm_tn("ffn1_dwd", act1, dx1, scale=0.5, deps=sent, out_dtype=BF16)
    send_grads("f1d", gfull)
    return dx, gsmall


def kernel(x, ffn1_norm, ffn1_w_gate, ffn1_w_up, ffn1_w_down, mix_norm, w_in, gate_bias, rel_bias_table, ssm_a_re, ssm_a_im, ssm_log_dt, ssm_b_re, ssm_b_im, ssm_c_re, ssm_c_im, ssm_d, ssm_w_glu, w_attn_branch, w_ssm_branch, w_out, ffn2_norm, ffn2_w_gate, ffn2_w_up, ffn2_w_down, final_norm, loss_target, m_ffn1_norm, m_ffn1_w_gate, m_ffn1_w_up, m_ffn1_w_down, m_mix_norm, m_w_in, m_gate_bias, m_rel_bias_table, m_ssm_a_re, m_ssm_a_im, m_ssm_log_dt, m_ssm_b_re, m_ssm_b_im, m_ssm_c_re, m_ssm_c_im, m_ssm_d, m_ssm_w_glu, m_w_attn_branch, m_w_ssm_branch, m_w_out, m_ffn2_norm, m_ffn2_w_gate, m_ffn2_w_up, m_ffn2_w_down, m_final_norm, v_ffn1_norm, v_ffn1_w_gate, v_ffn1_w_up, v_ffn1_w_down, v_mix_norm, v_w_in, v_gate_bias, v_rel_bias_table, v_ssm_a_re, v_ssm_a_im, v_ssm_log_dt, v_ssm_b_re, v_ssm_b_im, v_ssm_c_re, v_ssm_c_im, v_ssm_d, v_ssm_w_glu, v_w_attn_branch, v_w_ssm_branch, v_w_out, v_ffn2_norm, v_ffn2_w_gate, v_ffn2_w_up, v_ffn2_w_down, v_final_norm):
    given = dict(locals())
    shapes = {nm: given[nm].shape for nm in _ORDER}

    def strip(a):
        return a[0] if a.ndim >= 2 and a.shape[0] == 1 else a

    w = {nm: strip(given[nm]) for nm in _ORDER}
    m = {nm: strip(given["m_" + nm]) for nm in _ORDER}
    v = {nm: strip(given["v_" + nm]) for nm in _ORDER}
    for d in (w, m, v):
        d["rel_bias_table"] = d["rel_bias_table"].reshape(N_BUCKETS, N_GROUPS * HEADS_PER_GROUP)

    weight_phases = {"f1": _PHASES["f1gu"] + _PHASES["f1d"], "mix": _PHASES["mix"], "f2": _PHASES["f2"]}
    pending_w, w_rows, deps, zero = {}, {}, [], 0.0
    for phase, names in weight_phases.items():
        w_rows.update({nm: _to_rows(w[nm] + zero, nm) for nm in names})
        pending_w[phase] = _exchange_start(f"gather_{phase}_start", [w_rows[nm].astype(BF16) for nm in names],
                                           gather=True, deps=deps)
        deps = [pending_w[phase][4]]
        zero = pending_w["f1"][4][0, 0]
    small = {nm: w[nm] for nm in _SMALL}
    small_in = {nm: small[nm] + zero for nm in _SMALL}
    for nm in ("ffn1_norm", "mix_norm", "ffn2_norm", "gate_bias"):
        small_in[nm] = small_in[nm].reshape(1, -1)

    def weights_of(phase, after):
        landed = _exchange_wait(f"gather_{phase}_wait", pending_w[phase], after, gather=True)
        return {nm: _full_weight(got, nm) for nm, got in zip(weight_phases[phase], landed)}

    pending_g = {}

    def send_grads(phase, grads):
        if phase == "small":
            gs_pack = _pack_small({nm: grads[nm].reshape(small[nm].shape) for nm in _SMALL}, last=grads["loss"])
            pending_g[phase] = _exchange_start("gather_small_start", [gs_pack], gather=True)
        else:
            pending_g[phase] = _exchange_start(f"scatter_{phase}_start",
                                               [_grad_blocks(grads[nm], nm) for nm in _PHASES[phase]], gather=False)
        return [pending_g[phase][4]]

    dx, gsmall = _local_step(_residue_order(x[0]), _residue_order(loss_target[0]), small_in, weights_of, send_grads,
                             first_deps=[pending_w["f2"][4]])
    dx = _token_order(dx)

    updated = {}
    after = pending_g["f1d"][4]
    for phase in ("f2", "mix", "small", "f1gu", "f1d"):
        landed = _exchange_wait(f"exchange_{phase}_wait", pending_g[phase], after, gather=phase == "small")
        if phase == "small":
            sm = _adamw("adamw_small", _pack_small(small), _pack_small({nm: m[nm] for nm in _SMALL}),
                        _pack_small({nm: v[nm] for nm in _SMALL}), landed[0], landed[0].shape[1])
            after = sm[0]
            continue
        for nm, recv in zip(_PHASES[phase], landed):
            transposed, shape = _SHARD_INFO[nm]
            if transposed:
                updated[nm] = _adamw_cols(f"adamw_{nm}", w[nm], m[nm], v[nm], recv.reshape((N_DEV,) + shape))
            else:
                tr = max(t for t in range(16, 353, 16) if shape[0] % t == 0)
                updated[nm] = _adamw(f"adamw_{nm}", w[nm], m[nm], v[nm], recv, tr)
            after = updated[nm][0]

    loss = sm[0][-8, 0]
    outs = []
    for i in range(4):
        sml = _unpack_small(sm[i], small)
        outs.append([(updated[nm][i] if nm in updated else sml[nm]).reshape(shapes[nm])
                     for nm in _ORDER])
    return (loss, dx[None], *outs[0], *outs[1], *outs[2], *outs[3])
```

```python
import math

import numpy as np
import jax
import jax.numpy as jnp
from jax import lax
from jax.experimental import pallas as pl
from jax.experimental.pallas import tpu as pltpu

F32 = jnp.float32
BF16 = jnp.bfloat16

N_DEV = 8
D_MODEL = 1024
D_FF = 2816
HEAD_DIM = 64
HEADS_PER_GROUP = 4
DILATIONS = (1, 4, 16)
N_GROUPS = 3
ATTN_WIDTH = 768
ATTN_OUT = 256
BLOCK = 128
N_BUCKETS = 32
MAX_DISTANCE = 2048
NEG_INF = -1e30
SSM_WIDTH = 512
SSM_GROUPS = 32
SSM_GROUP = 16
SSM_STATE = 64
SSM_LANES = SSM_GROUPS * SSM_STATE
SSM_PAIRS = SSM_GROUPS // 2
PAIR_LANES = 2 * SSM_STATE
PAIR_TILE = 256
EPS = 1e-6
LR, B1, B2, ADAM_EPS, WD, STEP = 0.001, 0.9, 0.999, 1e-08, 0.01, 10

VMEM_LIMIT_BYTES = 56 * 1024 * 1024
FFN_CHUNK = 768
SCAN_BLOCK = 256
SCAN_STEPS = 16
SCAN_COLS = SCAN_BLOCK // SCAN_STEPS
SCAN_SUB = 8
SCAN_LANES = 512

MESH = pl.DeviceIdType.MESH


def _cparams(*sem):
    return pltpu.CompilerParams(dimension_semantics=sem, vmem_limit_bytes=VMEM_LIMIT_BYTES)


def _dot(a, b, dims):
    return lax.dot_general(a, b, (dims, ((), ())), preferred_element_type=F32)


def _dot_nn(a, b):
    return _dot(a, b, ((1,), (0,)))


def _dot_nt(a, b):
    return _dot(a, b, ((1,), (1,)))


def _dot_tn(a, b):
    return _dot(a, b, ((0,), (0,)))


def _sigmoid(x):
    return 1.0 / (1.0 + jnp.exp(-x))


_HBM_SPEC = pl.BlockSpec(memory_space=pltpu.HBM)
_SEM_SPEC = pl.BlockSpec(memory_space=pltpu.SEMAPHORE)
_ANY_SPEC = pl.BlockSpec(memory_space=pl.ANY)
_EFFECT = pltpu.SideEffectType.DATAFLOW_SIDE_EFFECTING


def _peers(x, y, c):
    return [(1 - x if k & 4 else x, 1 - y if k & 2 else y, 1 - c if k & 1 else c) for k in range(1, N_DEV)]


_CHIP_PEERS = (0, 1, 3, 5)


def _exchange_copies(x_refs, land_refs, send_sems, recv_sems, gather, peers=None):
    x, y, c = lax.axis_index("x"), lax.axis_index("y"), lax.axis_index("c")
    me = 4 * x + 2 * y + c
    copies = []
    for a, (x_ref, land_ref) in enumerate(zip(x_refs, land_refs)):
        for k, (px, py, pc) in enumerate(_peers(x, y, c)):
            if peers is not None and k not in peers:
                continue
            src = x_ref if gather else x_ref.at[4 * px + 2 * py + pc]
            copies.append(pltpu.make_async_remote_copy(
                src_ref=src, dst_ref=land_ref.at[me], send_sem=send_sems.at[N_DEV * a + k],
                recv_sem=recv_sems.at[(N_DEV - 1) * a + k], device_id=(px, py, pc), device_id_type=MESH))
    owns = [pltpu.make_async_copy(x_ref if gather else x_ref.at[me], land_ref.at[me],
                                  send_sems.at[N_DEV * a + N_DEV - 1])
            for a, (x_ref, land_ref) in enumerate(zip(x_refs, land_refs))]
    return owns, copies


def _exchange_start(name, xs_list, gather, deps=(), peers=None):
    n, nd = len(xs_list), len(deps)
    land_shapes = [(N_DEV, *xs.shape) if gather else xs.shape for xs in xs_list]

    def body(*refs):
        x_refs, land_refs = refs[:n], refs[n:2 * n]
        send_sems, recv_sems = refs[2 * n + nd:2 * n + nd + 2]
        token = refs[-1]
        owns, copies = _exchange_copies(x_refs, land_refs, send_sems, recv_sems, gather, peers)
        for cp in copies + owns:
            cp.start()
        token[...] = jnp.zeros_like(token)

    hbm = lambda a: pltpu.with_memory_space_constraint(a, pltpu.HBM)
    outs = pl.pallas_call(
        body, name=name,
        out_shape=(pltpu.SemaphoreType.DMA((n * N_DEV,)), pltpu.SemaphoreType.DMA((n * (N_DEV - 1),)),
                   *[pltpu.HBM(xs.shape, xs.dtype) for xs in xs_list],
                   *[pltpu.HBM(shape, xs.dtype) for shape, xs in zip(land_shapes, xs_list)],
                   jax.ShapeDtypeStruct((8, 128), F32)),
        in_specs=(_HBM_SPEC,) * (2 * n) + (_ANY_SPEC,) * nd,
        out_specs=(_SEM_SPEC, _SEM_SPEC) + (_HBM_SPEC,) * (2 * n) + (pl.BlockSpec(memory_space=pltpu.VMEM),),
        input_output_aliases={i: 2 + i for i in range(2 * n)},
        compiler_params=pltpu.CompilerParams(has_side_effects=_EFFECT),
    )(*[hbm(xs) for xs in xs_list], *[hbm(lax.empty(shape, xs.dtype)) for shape, xs in zip(land_shapes, xs_list)],
      *deps)
    return outs[0], outs[1], list(outs[2:2 + n]), list(outs[2 + n:2 + 2 * n]), outs[-1]


def _exchange_wait(name, handle, after, gather, peers=None):
    send_sems, recv_sems, xs_thru, lands_thru, _ = handle
    n = len(xs_thru)
    after = list(after) if isinstance(after, (list, tuple)) else [after]

    def body(*refs):
        x_refs, land_refs = refs[:n], refs[n:2 * n]
        send_sems, recv_sems = refs[2 * n:2 * n + 2]
        owns, copies = _exchange_copies(x_refs, land_refs, send_sems, recv_sems, gather, peers)
        for cp in copies:
            cp.wait_send()
            cp.wait_recv()
        for cp in owns:
            cp.wait()

    outs = pl.pallas_call(
        body, name=name,
        out_shape=tuple(pltpu.HBM(a.shape, a.dtype) for a in xs_thru + lands_thru),
        in_specs=(_HBM_SPEC,) * (2 * n) + (_SEM_SPEC, _SEM_SPEC) + (_ANY_SPEC,) * len(after),
        out_specs=(_HBM_SPEC,) * (2 * n), input_output_aliases={i: i for i in range(2 * n)},
        compiler_params=pltpu.CompilerParams(has_side_effects=_EFFECT),
    )(*xs_thru, *lands_thru, send_sems, recv_sems, *after)
    return list(outs[n:])


def _swap_with_sibling(name, lands):
    n = len(lands)

    def body(*refs):
        out_refs = refs[n:2 * n]
        send_sems, recv_sems = refs[2 * n:]
        x, y, c = lax.axis_index("x"), lax.axis_index("y"), lax.axis_index("c")
        copies = []
        for a in range(n):
            for j, (px, py) in enumerate([(1 - x, y), (x, 1 - y), (1 - x, 1 - y)]):
                rows = out_refs[a].at[4 * px + 2 * py + c]
                copies.append(pltpu.make_async_remote_copy(
                    src_ref=rows, dst_ref=rows, send_sem=send_sems.at[3 * a + j], recv_sem=recv_sems.at[3 * a + j],
                    device_id=(x, y, 1 - c), device_id_type=MESH))
        for cp in copies:
            cp.start()
        for cp in copies:
            cp.wait()

    return pl.pallas_call(
        body, name=name,
        out_shape=[jax.ShapeDtypeStruct(a.shape, a.dtype) for a in lands],
        in_specs=[_ANY_SPEC] * n, out_specs=[_ANY_SPEC] * n, input_output_aliases={i: i for i in range(n)},
        scratch_shapes=[pltpu.SemaphoreType.DMA((3 * n,)), pltpu.SemaphoreType.DMA((3 * n,))],
    )(*lands)


def _mm(name, pairs, nt, n_cols, out_dtypes, epilogue=None, extras=(), tm=1024, tn=512, deps=(), row_sums=0,
        out_cols=None):
    rows = pairs[0][0].shape[0]
    tm = min(tm, rows)
    tn = min(tn, n_cols)
    na, ne, nd, no = len(pairs), len(extras), len(deps), len(out_dtypes)

    def body(*refs):
        a_refs, w_refs = refs[:na], refs[na:2 * na]
        e_refs, o_refs = refs[2 * na:2 * na + ne], refs[2 * na + ne + nd:]
        acc = None
        for a_ref, w_ref in zip(a_refs, w_refs):
            a = a_ref[...].astype(BF16)
            w = w_ref[...].astype(BF16)
            p = _dot_nt(a, w) if nt else _dot_nn(a, w)
            acc = p if acc is None else acc + p
        outs = (acc,) if epilogue is None else epilogue(acc, *[e[...] for e in e_refs])
        for o_ref, o in zip(o_refs[:no], outs[:no]):
            o_ref[...] = o.astype(o_ref.dtype)
        for r_ref, o in zip(o_refs[no:], outs[no:]):
            @pl.when(pl.program_id(0) == 0)
            def _():
                r_ref[...] = jnp.zeros_like(r_ref)

            r_ref[...] += o

    in_specs = [pl.BlockSpec((tm, a.shape[1]), lambda i, j: (i, 0)) for a, _ in pairs]
    for _, w in pairs:
        if nt:
            in_specs.append(pl.BlockSpec((tn, w.shape[1]), lambda i, j: (j, 0)))
        else:
            in_specs.append(pl.BlockSpec((w.shape[0], tn), lambda i, j: (0, j)))
    for e, col_off in extras:
        off = col_off // tn
        if e.shape[0] == 1:
            in_specs.append(pl.BlockSpec((1, tn), lambda i, j, off=off: (0, j + off)))
        else:
            in_specs.append(pl.BlockSpec((tm, tn), lambda i, j, off=off: (i, j + off)))
    in_specs += [_ANY_SPEC] * nd
    if out_cols is None:
        out_cols = [n_cols] * no
    else:
        assert tn == n_cols, "outputs of other widths need the whole row in one block"
    assert not row_sums or tn == n_cols
    out_specs = [pl.BlockSpec((tm, tn * c // n_cols), lambda i, j: (i, j)) for c in out_cols]
    out_specs += [pl.BlockSpec((1, tn), lambda i, j: (0, j))] * row_sums
    out_shape = [jax.ShapeDtypeStruct((rows, c), dt) for c, dt in zip(out_cols, out_dtypes)]
    out_shape += [jax.ShapeDtypeStruct((1, n_cols), F32)] * row_sums
    outs = pl.pallas_call(
        body, name=name, grid=(rows // tm, n_cols // tn),
        in_specs=in_specs, out_specs=out_specs, out_shape=out_shape,
        compiler_params=_cparams("arbitrary" if row_sums else "parallel", "arbitrary"),
    )(*[a for a, _ in pairs], *[w for _, w in pairs], *[e for e, _ in extras], *deps)
    return outs


def _tn_rows(m):
    return max(b for b in range(128, min(m, 1408) + 1, 128) if m % b == 0)


def _mm_tn(name, a, b, scale=1.0, bm=None, tk=1024, deps=(), out_dtype=F32):
    rows, m = a.shape
    n = b.shape[1]
    bm = _tn_rows(m) if bm is None else bm
    tk = min(tk, rows)
    nk = rows // tk

    def body(a_ref, b_ref, *rest):
        o_ref, acc_ref = rest[-2:]
        k = pl.program_id(1)

        @pl.when(k == 0)
        def _():
            acc_ref[...] = jnp.zeros_like(acc_ref)

        acc_ref[...] += _dot_tn(a_ref[...].astype(BF16), b_ref[...].astype(BF16))

        @pl.when(k == nk - 1)
        def _():
            o_ref[...] = (acc_ref[...] * scale).astype(o_ref.dtype)

    return pl.pallas_call(
        body, name=name, grid=(m // bm, nk),
        in_specs=[pl.BlockSpec((tk, bm), lambda i, k: (k, i)), pl.BlockSpec((tk, n), lambda i, k: (k, 0))]
        + [_ANY_SPEC] * len(deps),
        out_specs=pl.BlockSpec((bm, n), lambda i, k: (i, 0)),
        out_shape=jax.ShapeDtypeStruct((m, n), out_dtype),
        scratch_shapes=[pltpu.VMEM((bm, n), F32)],
        compiler_params=_cparams("parallel", "arbitrary"),
    )(a, b, *deps)


def _mm_tn_stack(name, a_list, b, tk=1024, out_dtype=F32):
    rows, n = b.shape
    ms = [a.shape[1] for a in a_list]
    tk = min(tk, rows)
    nk = rows // tk
    na = len(a_list)

    def body(*refs):
        a_refs, b_ref, o_ref, acc_ref = refs[:na], refs[na], refs[na + 1], refs[na + 2]
        k = pl.program_id(0)

        @pl.when(k == 0)
        def _():
            acc_ref[...] = jnp.zeros_like(acc_ref)

        bv = b_ref[...].astype(BF16)
        r0 = 0
        for a_ref, m in zip(a_refs, ms):
            acc_ref[r0:r0 + m, :] += _dot_tn(a_ref[...].astype(BF16), bv)
            r0 += m

        @pl.when(k == nk - 1)
        def _():
            o_ref[...] = acc_ref[...].astype(o_ref.dtype)

    return pl.pallas_call(
        body, name=name, grid=(nk,),
        in_specs=[pl.BlockSpec((tk, m), lambda k: (k, 0)) for m in ms] + [pl.BlockSpec((tk, n), lambda k: (k, 0))],
        out_specs=pl.BlockSpec((sum(ms), n), lambda k: (0, 0)),
        out_shape=jax.ShapeDtypeStruct((sum(ms), n), out_dtype),
        scratch_shapes=[pltpu.VMEM((sum(ms), n), F32)],
        compiler_params=_cparams("arbitrary"),
    )(*a_list, b)


def _colsum(name, xs, tm=512):
    rows, cols = xs.shape
    tm = min(tm, rows)

    def body(x_ref, o_ref):
        @pl.when(pl.program_id(0) == 0)
        def _():
            o_ref[...] = jnp.zeros_like(o_ref)

        o_ref[...] += jnp.sum(x_ref[...].astype(F32), axis=0, keepdims=True)

    return pl.pallas_call(
        body, name=name, grid=(rows // tm,),
        in_specs=[pl.BlockSpec((tm, cols), lambda i: (i, 0))],
        out_specs=pl.BlockSpec((1, cols), lambda i: (0, 0)),
        out_shape=jax.ShapeDtypeStruct((1, cols), F32),
        compiler_params=_cparams("arbitrary"),
    )(xs)


def _ew(name, fn, ins, out_cols, out_dtypes, tm=512):
    rows = ins[0].shape[0]
    tm = min(tm, rows)
    ni = len(ins)

    def body(*refs):
        outs = fn(*[r[...] for r in refs[:ni]])
        for o_ref, o in zip(refs[ni:], outs):
            o_ref[...] = o.astype(o_ref.dtype)

    def spec(shape):
        if shape[0] == 1:
            return pl.BlockSpec((1, shape[1]), lambda i: (0, 0))
        return pl.BlockSpec((tm, shape[1]), lambda i: (i, 0))

    return pl.pallas_call(
        body, name=name, grid=(rows // tm,),
        in_specs=[spec(a.shape) for a in ins],
        out_specs=[pl.BlockSpec((tm, c), lambda i: (i, 0)) for c in out_cols],
        out_shape=[jax.ShapeDtypeStruct((rows, c), dt) for c, dt in zip(out_cols, out_dtypes)],
        compiler_params=_cparams("parallel"),
    )(*ins)


def _rms_parts(xv):
    r = lax.rsqrt(jnp.mean(xv * xv, axis=-1, keepdims=True) + EPS)
    return r, xv * r


def _rms_bwd_dx(dh, gain, r, xh):
    dxh = dh * gain
    return r * (dxh - xh * jnp.mean(dxh * xh, axis=-1, keepdims=True))


def _ffn_chunks(f_all):
    return [slice(c, min(c + FFN_CHUNK, f_all)) for c in range(0, f_all, FFN_CHUNK)]


def _loss_head(xo, gain_f, target, d):
    r, xh = _rms_parts(xo)
    err = xh * gain_f - target
    dy = err * (1.0 / d)
    per_tok = jnp.mean(err * err, axis=-1, keepdims=True)
    return (_rms_bwd_dx(dy, gain_f, r, xh), jnp.sum(dy * xh, axis=0, keepdims=True),
            0.5 * jnp.sum(per_tok, axis=0, keepdims=True))


def _ffn_tile(x_ref, g_ref, wg_ref, wu_ref, wd_ref, h_ref, gg_ref, uu_ref):
    xv = x_ref[...]
    _, xh = _rms_parts(xv)
    h = (xh * g_ref[...]).astype(BF16)
    h_ref[...] = h
    acc = None
    for cols in _ffn_chunks(wd_ref.shape[0]):
        gg = _dot_nt(h, wg_ref[cols, :])
        uu = _dot_nt(h, wu_ref[cols, :])
        act = gg * _sigmoid(gg) * uu
        part = _dot_nn(act.astype(BF16), wd_ref[cols, :])
        acc = part if acc is None else acc + part
        gg_ref[:, cols] = gg.astype(BF16)
        uu_ref[:, cols] = uu.astype(BF16)
    return xv + 0.5 * acc


def _ffn_fwd(name, xs, gain, wg_t, wu_t, wd, next_gain, tm=512, deps=()):
    rows, d = xs.shape
    f_all = wd.shape[0]
    tm = min(tm, rows)

    def body(x_ref, g_ref, wg_ref, wu_ref, wd_ref, ng_ref, *rest):
        xo_ref, h_ref, gg_ref, uu_ref, hn_ref = rest[-5:]
        xo = _ffn_tile(x_ref, g_ref, wg_ref, wu_ref, wd_ref, h_ref, gg_ref, uu_ref)
        xo_ref[...] = xo
        hn_ref[...] = (_rms_parts(xo)[1] * ng_ref[...]).astype(BF16)

    tile = pl.BlockSpec((tm, d), lambda i: (i, 0))
    row = pl.BlockSpec((1, d), lambda i: (0, 0))
    wspec = pl.BlockSpec((f_all, d), lambda i: (0, 0), pipeline_mode=pl.Buffered(1))
    hid = pl.BlockSpec((tm, f_all), lambda i: (i, 0))
    return pl.pallas_call(
        body, name=name, grid=(rows // tm,),
        in_specs=[tile, row, wspec, wspec, wspec, row] + [_ANY_SPEC] * len(deps),
        out_specs=[tile, tile, hid, hid, tile],
        out_shape=[jax.ShapeDtypeStruct((rows, d), F32), jax.ShapeDtypeStruct((rows, d), BF16),
                   jax.ShapeDtypeStruct((rows, f_all), BF16), jax.ShapeDtypeStruct((rows, f_all), BF16),
                   jax.ShapeDtypeStruct((rows, d), BF16)],
        compiler_params=_cparams("parallel"),
    )(xs, gain, wg_t, wu_t, wd, next_gain, *deps)


def _ffn_fwd_head(name, xs, gain, wg_t, wu_t, wd, gain_f, target, tm=512):
    rows, d = xs.shape
    f_all = wd.shape[0]
    tm = min(tm, rows)

    def body(x_ref, g_ref, wg_ref, wu_ref, wd_ref, gf_ref, t_ref, dxo_ref, h_ref, gg_ref, uu_ref, dgf_ref, loss_ref):
        xo = _ffn_tile(x_ref, g_ref, wg_ref, wu_ref, wd_ref, h_ref, gg_ref, uu_ref)
        dxo, dgf, loss = _loss_head(xo, gf_ref[...], t_ref[...], d)
        dxo_ref[...] = dxo

        @pl.when(pl.program_id(0) == 0)
        def _():
            dgf_ref[...] = jnp.zeros_like(dgf_ref)
            loss_ref[...] = jnp.zeros_like(loss_ref)

        dgf_ref[...] += dgf
        loss_ref[...] += loss

    tile = pl.BlockSpec((tm, d), lambda i: (i, 0))
    row = pl.BlockSpec((1, d), lambda i: (0, 0))
    wspec = pl.BlockSpec((f_all, d), lambda i: (0, 0), pipeline_mode=pl.Buffered(1))
    hid = pl.BlockSpec((tm, f_all), lambda i: (i, 0))
    return pl.pallas_call(
        body, name=name, grid=(rows // tm,),
        in_specs=[tile, row, wspec, wspec, wspec, row, tile],
        out_specs=[tile, tile, hid, hid, row, pl.BlockSpec((1, 1), lambda i: (0, 0))],
        out_shape=[jax.ShapeDtypeStruct((rows, d), F32), jax.ShapeDtypeStruct((rows, d), BF16),
                   jax.ShapeDtypeStruct((rows, f_all), BF16), jax.ShapeDtypeStruct((rows, f_all), BF16),
                   jax.ShapeDtypeStruct((1, d), F32), jax.ShapeDtypeStruct((1, 1), F32)],
        compiler_params=_cparams("arbitrary"),
    )(xs, gain, wg_t, wu_t, wd, gain_f, target)


def _ffn_bwd(name, dxo, xs, gain, gg_all, uu_all, wg_t, wu_t, wd, tm=256):
    rows, d = xs.shape
    f_all = wd.shape[0]
    tm = min(tm, rows)

    def body(dxo_ref, x_ref, g_ref, gg_ref, uu_ref, wg_ref, wu_ref, wd_ref,
             dx_ref, dgg_ref, duu_ref, act_ref, dgain_ref):
        dxo = dxo_ref[...]
        df = (0.5 * dxo).astype(BF16)
        dh = None
        for cols in _ffn_chunks(f_all):
            gg = gg_ref[:, cols].astype(F32)
            uu = uu_ref[:, cols].astype(F32)
            sg = _sigmoid(gg)
            silu = gg * sg
            dact = _dot_nt(df, wd_ref[cols, :])
            duu = (dact * silu).astype(BF16)
            dgg = (dact * uu * (sg * (1.0 + gg * (1.0 - sg)))).astype(BF16)
            act_ref[:, cols] = (silu * uu).astype(BF16)
            dgg_ref[:, cols] = dgg
            duu_ref[:, cols] = duu
            part = _dot_nn(dgg, wg_ref[cols, :]) + _dot_nn(duu, wu_ref[cols, :])
            dh = part if dh is None else dh + part
        r, xh = _rms_parts(x_ref[...])
        dx_ref[...] = dxo + _rms_bwd_dx(dh, g_ref[...], r, xh)

        @pl.when(pl.program_id(0) == 0)
        def _():
            dgain_ref[...] = jnp.zeros_like(dgain_ref)

        dgain_ref[...] += jnp.sum(dh * xh, axis=0, keepdims=True)

    tile = pl.BlockSpec((tm, d), lambda i: (i, 0))
    row = pl.BlockSpec((1, d), lambda i: (0, 0))
    wspec = pl.BlockSpec((f_all, d), lambda i: (0, 0), pipeline_mode=pl.Buffered(1))
    hid = pl.BlockSpec((tm, f_all), lambda i: (i, 0))
    hid_shape = jax.ShapeDtypeStruct((rows, f_all), BF16)
    return pl.pallas_call(
        body, name=name, grid=(rows // tm,),
        in_specs=[tile, tile, row, hid, hid, wspec, wspec, wspec],
        out_specs=[tile, hid, hid, hid, row],
        out_shape=[jax.ShapeDtypeStruct((rows, d), F32), hid_shape, hid_shape, hid_shape,
                   jax.ShapeDtypeStruct((1, d), F32)],
        compiler_params=_cparams("arbitrary"),
    )(dxo, xs, gain, gg_all, uu_all, wg_t, wu_t, wd)


def _t5_bucket_np(dist):
    max_exact = N_BUCKETS // 2
    dd = np.maximum(dist, 1).astype(np.float32)
    large = max_exact + (np.log(dd / np.float32(max_exact)) / np.float32(math.log(MAX_DISTANCE / max_exact))
                         * np.float32(N_BUCKETS - max_exact)).astype(np.int32)
    large = np.minimum(large, N_BUCKETS - 1)
    return np.where(dist < max_exact, dist, large).astype(np.int32)


def _attn_geometry(g, rows):
    run = rows // 16
    dil = DILATIONS[g]
    if dil == 16:
        bq = BLOCK
        return dict(view=(16, run), block=(None, bq), grid=(16, run // bq), index=lambda r, n: (r, n),
                    pos=np.arange(bq), bq=bq)
    if dil == 4:
        per = BLOCK // 4
        pos = (4 * np.arange(per)[None, :] + np.arange(4)[:, None]).reshape(-1)
        return dict(view=(4, 4, run), block=(4, None, per), grid=(4, run // per), index=lambda r, n: (0, r, n),
                    pos=pos, bq=BLOCK)
    per = 16
    pos = (16 * np.arange(per)[None, :] + np.arange(16)[:, None]).reshape(-1)
    return dict(view=(16, run), block=(16, per), grid=(1, run // per), index=lambda r, n: (0, n),
                pos=pos, bq=16 * per)


def _attn_tables(g, rows):
    geo = _attn_geometry(g, rows)
    pos, bq = geo["pos"], geo["bq"]
    steps = pos[:, None] - np.concatenate([pos - bq, pos])[None, :]
    valid = (steps >= 0) & (steps <= BLOCK)
    bucket = _t5_bucket_np((np.maximum(steps, 0) * DILATIONS[g]).astype(np.int32))
    return bucket, valid.astype(np.int32)


def _bias_fwd(name, bucket, valid, table_t):
    bq = bucket.shape[0]

    def body(bk_ref, ok_ref, tab_ref, o_ref):
        bk = bk_ref[...]
        ok = ok_ref[...] > 0
        accs = [jnp.zeros(bk.shape, F32)] * HEADS_PER_GROUP
        for b in range(N_BUCKETS):
            hit = bk == b
            accs = [jnp.where(hit, tab_ref[h, b], acc) for h, acc in enumerate(accs)]
        for h, acc in enumerate(accs):
            o_ref[h] = jnp.where(ok, acc, NEG_INF)

    vm = pl.BlockSpec(memory_space=pltpu.VMEM)
    return pl.pallas_call(
        body, name=name, in_specs=[vm, vm, pl.BlockSpec(memory_space=pltpu.SMEM)], out_specs=vm,
        out_shape=jax.ShapeDtypeStruct((HEADS_PER_GROUP, bq, 2 * bq), F32),
    )(bucket, valid, table_t)


def _bias_bwd(name, bucket, dbias):
    def body(bk_ref, db_ref, o_ref):
        row_id = lax.broadcasted_iota(jnp.int32, (N_BUCKETS, 128), 0)
        col_id = lax.broadcasted_iota(jnp.int32, (N_BUCKETS, 128), 1)
        bk = bk_ref[...]
        acc = jnp.zeros((N_BUCKETS, 128), F32)
        for h in range(HEADS_PER_GROUP):
            db = db_ref[h]
            for b in range(N_BUCKETS):
                part = jnp.sum(jnp.where(bk == b, db, 0.0), axis=0, keepdims=True)
                tot = jnp.sum(part, axis=1, keepdims=True)
                acc = jnp.where((row_id == b) & (col_id == h), tot, acc)
        o_ref[...] = acc

    vm = pl.BlockSpec(memory_space=pltpu.VMEM)
    return pl.pallas_call(body, name=name, in_specs=[vm, vm], out_specs=vm,
                          out_shape=jax.ShapeDtypeStruct((N_BUCKETS, 128), F32))(bucket, dbias)


def _head_of_lane(nrows):
    return lax.broadcasted_iota(jnp.int32, (nrows, ATTN_OUT), 1) // HEAD_DIM


def _stack_heads(a, lane_head):
    zero = jnp.zeros_like(a)
    return jnp.concatenate([jnp.where(lane_head == h, a, zero) for h in range(HEADS_PER_GROUP)], axis=0)


def _unstack_heads(a4, lane_head, bq):
    out = a4[:bq]
    for h in range(1, HEADS_PER_GROUP):
        out = jnp.where(lane_head == h, a4[h * bq:(h + 1) * bq], out)
    return out


def _attn_specs(geo, cols, col_block, index):
    return pl.BlockSpec(geo["block"] + (cols,), lambda r, n: index(r, n) + (col_block,))


def _attn_fwd(name, qkv, g, bias4):
    rows = qkv.shape[0]
    geo = _attn_geometry(g, rows)
    bq, (nsub, nb), index = geo["bq"], geo["grid"], geo["index"]
    blk_shape = tuple(b for b in geo["block"] if b is not None) + (ATTN_OUT,)

    def body(q_ref, kc_ref, kp_ref, vc_ref, vp_ref, b_ref, o_ref, lse_ref):
        n = pl.program_id(1)
        lane_head = _head_of_lane(bq)
        flat = lambda ref: ref[...].reshape(bq, ATTN_OUT)
        q4 = _stack_heads(flat(q_ref), lane_head)
        k2 = jnp.concatenate([flat(kp_ref), flat(kc_ref)], axis=0)
        v2 = jnp.concatenate([flat(vp_ref), flat(vc_ref)], axis=0)
        s = _dot_nt(q4, k2) + b_ref[...]
        col = lax.broadcasted_iota(jnp.int32, s.shape, 1)
        s = jnp.where((col >= bq) | (n > 0), s, NEG_INF)
        mx = jnp.max(s, axis=-1, keepdims=True)
        p = jnp.exp(s - mx)
        den = jnp.sum(p, axis=-1, keepdims=True)
        o4 = _dot_nn(p.astype(BF16), v2) / den
        lse4 = jnp.broadcast_to(mx + jnp.log(den), (HEADS_PER_GROUP * bq, ATTN_OUT))
        o_ref[...] = _unstack_heads(o4, lane_head, bq).reshape(blk_shape)
        lse_ref[...] = _unstack_heads(lse4, lane_head, bq).reshape(blk_shape)

    prev = lambda r, n: index(r, jnp.maximum(n - 1, 0))
    view = lambda a: a.reshape(geo["view"] + (a.shape[1],))
    qkv_v = view(qkv)
    out_spec = _attn_specs(geo, ATTN_OUT, 0, index)
    out_shape = jax.ShapeDtypeStruct(geo["view"] + (ATTN_OUT,), F32)
    o, lse = pl.pallas_call(
        body, name=name, grid=(nsub, nb),
        in_specs=[_attn_specs(geo, ATTN_OUT, g, index), _attn_specs(geo, ATTN_OUT, 3 + g, index),
                  _attn_specs(geo, ATTN_OUT, 3 + g, prev), _attn_specs(geo, ATTN_OUT, 6 + g, index),
                  _attn_specs(geo, ATTN_OUT, 6 + g, prev), pl.BlockSpec(bias4.shape, lambda r, n: (0, 0))],
        out_specs=[out_spec, out_spec], out_shape=[out_shape, out_shape],
        compiler_params=_cparams("parallel", "arbitrary"),
    )(qkv_v, qkv_v, qkv_v, qkv_v, qkv_v, bias4)
    return o.reshape(rows, ATTN_OUT), lse.reshape(rows, ATTN_OUT)


def _attn_bwd(name, qkv, do, lse, cvec, g, bias4):
    rows = qkv.shape[0]
    geo = _attn_geometry(g, rows)
    bq, (nsub, nb), index = geo["bq"], geo["grid"], geo["index"]
    blk_shape = tuple(b for b in geo["block"] if b is not None) + (ATTN_OUT,)
    nlead = len(blk_shape) - 1

    def body(q_ref, kc_ref, kp_ref, vc_ref, vp_ref, do_ref, lse_ref, c_ref, b_ref,
             dq_ref, dk_ref, dv_ref, db_ref, kcar_ref, vcar_ref):
        r, n = pl.program_id(0), pl.program_id(1)
        valid = n < nb
        lane_head = _head_of_lane(bq)
        flat = lambda ref: ref[...].reshape(bq, ATTN_OUT)

        @pl.when((r == 0) & (n == 0))
        def _():
            kcar_ref[...] = jnp.zeros_like(kcar_ref)
            vcar_ref[...] = jnp.zeros_like(vcar_ref)
            db_ref[...] = jnp.zeros_like(db_ref)

        def column(ref, h):
            lead = (slice(None),) * nlead
            return ref[lead + (pl.ds(h * HEAD_DIM, 1),)].reshape(bq, 1)

        q4 = _stack_heads(flat(q_ref), lane_head)
        do4 = _stack_heads(flat(do_ref), lane_head)
        k2 = jnp.concatenate([flat(kp_ref), flat(kc_ref)], axis=0)
        v2 = jnp.concatenate([flat(vp_ref), flat(vc_ref)], axis=0)
        lse4 = jnp.concatenate([column(lse_ref, h) for h in range(HEADS_PER_GROUP)], axis=0)
        c4 = jnp.concatenate([column(c_ref, h) for h in range(HEADS_PER_GROUP)], axis=0)
        s = _dot_nt(q4, k2) + b_ref[...]
        col = lax.broadcasted_iota(jnp.int32, s.shape, 1)
        keep = ((col >= bq) | (n > 0)) & valid
        p = jnp.where(keep, jnp.exp(s - lse4), 0.0)
        ds = p * (_dot_nt(do4, v2) + c4)
        ds_b = ds.astype(BF16)

        @pl.when(valid)
        def _():
            dq = _unstack_heads(_dot_nn(ds_b, k2), lane_head, bq) * (HEAD_DIM ** -0.5)
            dq_ref[...] = dq.astype(BF16).reshape(blk_shape)

        dk2 = _dot_tn(ds_b, q4)
        dv2 = _dot_tn(p.astype(BF16), do4)
        dk_ref[...] = (kcar_ref[...] + dk2[:bq]).astype(BF16).reshape(blk_shape)
        dv_ref[...] = (vcar_ref[...] + dv2[:bq]).astype(BF16).reshape(blk_shape)
        kcar_ref[...] = dk2[bq:]
        vcar_ref[...] = dv2[bq:]
        db_ref[...] += ds

    cur = lambda r, n: index(r, jnp.minimum(n, nb - 1))
    prev = lambda r, n: index(r, jnp.maximum(jnp.minimum(n, nb - 1) - 1, 0))
    late = lambda r, n: index(r, jnp.maximum(n - 1, 0))
    view = lambda a: a.reshape(geo["view"] + (a.shape[1],))
    qkv_v = view(qkv)
    tile = _attn_specs(geo, ATTN_OUT, 0, cur)
    bias_spec = pl.BlockSpec(bias4.shape, lambda r, n: (0, 0))
    out_shape = jax.ShapeDtypeStruct(geo["view"] + (ATTN_OUT,), BF16)
    dq, dk, dv, db = pl.pallas_call(
        body, name=name, grid=(nsub, nb + 1),
        in_specs=[_attn_specs(geo, ATTN_OUT, g, cur), _attn_specs(geo, ATTN_OUT, 3 + g, cur),
                  _attn_specs(geo, ATTN_OUT, 3 + g, prev), _attn_specs(geo, ATTN_OUT, 6 + g, cur),
                  _attn_specs(geo, ATTN_OUT, 6 + g, prev), tile, tile, tile, bias_spec],
        out_specs=[tile, _attn_specs(geo, ATTN_OUT, 0, late), _attn_specs(geo, ATTN_OUT, 0, late), bias_spec],
        out_shape=[out_shape, out_shape, out_shape, jax.ShapeDtypeStruct(bias4.shape, F32)],
        scratch_shapes=[pltpu.VMEM((bq, ATTN_OUT), F32), pltpu.VMEM((bq, ATTN_OUT), F32)],
        compiler_params=_cparams("arbitrary", "arbitrary"),
    )(qkv_v, qkv_v, qkv_v, qkv_v, qkv_v, view(do), view(lse), view(cvec), bias4)
    return dq.reshape(rows, ATTN_OUT), dk.reshape(rows, ATTN_OUT), dv.reshape(rows, ATTN_OUT), db


def _group_weights(lses):
    mx = jnp.maximum(jnp.maximum(lses[0], lses[1]), lses[2])
    es = [jnp.exp(l - mx) for l in lses]
    den = es[0] + es[1] + es[2]
    return [e / den for e in es]


def _combine_fwd(name, os_, lses):
    def fn(o0, o1, o2, l0, l1, l2):
        ws = _group_weights([l0, l1, l2])
        out = ws[0] * o0 + ws[1] * o1 + ws[2] * o2
        return out, out

    return _ew(name, fn, [*os_, *lses], [ATTN_OUT, ATTN_OUT], [F32, BF16], tm=1024)


def _combine_bwd(name, do, oa, lses):
    def fn(dov, oav, l0, l1, l2):
        head_sum = (lax.broadcasted_iota(jnp.int32, (ATTN_OUT, ATTN_OUT), 0) // HEAD_DIM
                    == lax.broadcasted_iota(jnp.int32, (ATTN_OUT, ATTN_OUT), 1) // HEAD_DIM)
        ws = _group_weights([l0, l1, l2])
        prod = dov * oav
        hi = prod.astype(BF16)
        lo = (prod - hi.astype(F32)).astype(BF16)
        ones = jnp.where(head_sum, 1.0, 0.0).astype(BF16)
        bar = _dot_nn(hi, ones) + _dot_nn(lo, ones)
        return tuple(w * dov for w in ws) + tuple(-w * bar for w in ws)

    return _ew(name, fn, [do, oa, *lses], [ATTN_OUT] * 6, [BF16] * 3 + [F32] * 3, tm=1024)


def _ssm_disc(a_re, a_im, log_dt, b_re, b_im):
    dt = jnp.exp(log_dt)
    mag = jnp.exp(a_re * dt)
    ab_re = mag * jnp.cos(a_im * dt)
    ab_im = mag * jnp.sin(a_im * dt)
    den = a_re * a_re + a_im * a_im
    xr = ab_re - 1.0
    coef_re = (xr * a_re + ab_im * a_im) / den
    coef_im = (ab_im * a_re - xr * a_im) / den
    bb_re = coef_re[None] * b_re - coef_im[None] * b_im
    bb_im = coef_re[None] * b_im + coef_im[None] * b_re
    return ab_re, ab_im, bb_re, bb_im


def _ssm_params_fwd(name, a_re, a_im, log_dt, b_re, b_im):
    pows = jax.ShapeDtypeStruct((SCAN_STEPS,) + a_re.shape, F32)
    cgn = jax.ShapeDtypeStruct(b_re.shape, F32)

    def body(ar, ai, ld, br, bi, o_pr, o_pi, o_bbr, o_bbi):
        ab_re, ab_im, bb_re, bb_im = _ssm_disc(ar[...], ai[...], ld[...], br[...], bi[...])
        pr, pi = ab_re, ab_im
        for j in range(SCAN_STEPS):
            o_pr[j] = pr
            o_pi[j] = pi
            pr, pi = pr * ab_re - pi * ab_im, pr * ab_im + pi * ab_re
        o_bbr[...] = bb_re
        o_bbi[...] = bb_im

    vm = pl.BlockSpec(memory_space=pltpu.VMEM)
    return pl.pallas_call(body, name=name, in_specs=[vm] * 5, out_specs=[vm] * 4,
                          out_shape=[pows, pows, cgn, cgn])(a_re, a_im, log_dt, b_re, b_im)


def _ssm_params_bwd(name, a_re, a_im, log_dt, b_re, b_im, d_ab_re, d_ab_im, d_bb_re, d_bb_im):
    gn = jax.ShapeDtypeStruct(a_re.shape, F32)
    cgn = jax.ShapeDtypeStruct(b_re.shape, F32)

    def body(ar, ai, ld, br, bi, g0, g1, g2, g3, o_ar, o_ai, o_ld, o_br, o_bi):
        _, vjp = jax.vjp(_ssm_disc, ar[...], ai[...], ld[...], br[...], bi[...])
        outs = vjp((g0[...], g1[...], g2[...], g3[...]))
        for o_ref, o in zip((o_ar, o_ai, o_ld, o_br, o_bi), outs):
            o_ref[...] = o

    vm = pl.BlockSpec(memory_space=pltpu.VMEM)
    return pl.pallas_call(body, name=name, in_specs=[vm] * 9, out_specs=[vm] * 5,
                          out_shape=[gn, gn, jax.ShapeDtypeStruct(log_dt.shape, F32), cgn, cgn],
                          )(a_re, a_im, log_dt, b_re, b_im, d_ab_re, d_ab_im, d_bb_re, d_bb_im)


def _scan_block(s_ref, carry_ref, tmp_ref, pw_ref, reverse, sprev=None):
    nl = SSM_LANES
    halves = range(SCAN_COLS // SCAN_SUB)
    zero = jnp.zeros((SCAN_SUB, SCAN_LANES), F32)
    for half in (reversed(halves) if reverse else halves):
        sub_rows = pl.ds(half * SCAN_SUB, SCAN_SUB)
        for lc in range(nl // SCAN_LANES):
            re_l = pl.ds(lc * SCAN_LANES, SCAN_LANES)
            im_l = pl.ds(nl + lc * SCAN_LANES, SCAN_LANES)
            are, aim = pw_ref[0, :, re_l], pw_ref[0, :, im_l]

            def step_of(j):
                return SCAN_STEPS - 1 - j if reverse else j

            def pass1(j, st):
                sr, si = st
                jj = step_of(j)
                nr = are * sr - aim * si + s_ref[jj, sub_rows, re_l]
                ni = are * si + aim * sr + s_ref[jj, sub_rows, im_l]
                s_ref[jj, sub_rows, re_l] = nr
                s_ref[jj, sub_rows, im_l] = ni
                return nr, ni

            er, ei = lax.fori_loop(0, SCAN_STEPS, pass1, (zero, zero), unroll=2)
            tmp_ref[0:SCAN_SUB, re_l] = er
            tmp_ref[0:SCAN_SUB, im_l] = ei
            apr, api = pw_ref[SCAN_STEPS - 1, 0:1, re_l], pw_ref[SCAN_STEPS - 1, 0:1, im_l]
            sr, si = carry_ref[0:1, re_l], carry_ref[0:1, im_l]
            for step in range(SCAN_SUB):
                c = SCAN_SUB - 1 - step if reverse else step
                tmp_ref[SCAN_SUB + c:SCAN_SUB + c + 1, re_l] = sr
                tmp_ref[SCAN_SUB + c:SCAN_SUB + c + 1, im_l] = si
                e_r, e_i = tmp_ref[c:c + 1, re_l], tmp_ref[c:c + 1, im_l]
                sr, si = apr * sr - api * si + e_r, apr * si + api * sr + e_i
            carry_ref[0:1, re_l] = sr
            carry_ref[0:1, im_l] = si
            cr = tmp_ref[SCAN_SUB:2 * SCAN_SUB, re_l]
            ci = tmp_ref[SCAN_SUB:2 * SCAN_SUB, im_l]

            if sprev is None:
                def pass2(j, st):
                    pr, pi = pw_ref[j, :, re_l], pw_ref[j, :, im_l]
                    jj = step_of(j)
                    s_ref[jj, sub_rows, re_l] += pr * cr - pi * ci
                    s_ref[jj, sub_rows, im_l] += pr * ci + pi * cr
                    return st

                lax.fori_loop(0, SCAN_STEPS, pass2, 0, unroll=2)
            else:
                st_ref, prev_ref, have_prev, dab_ref = sprev

                def corrected(jj, pr, pi):
                    gr = s_ref[jj, sub_rows, re_l] + pr * cr - pi * ci
                    gi = s_ref[jj, sub_rows, im_l] + pr * ci + pi * cr
                    s_ref[jj, sub_rows, re_l] = gr
                    s_ref[jj, sub_rows, im_l] = gi
                    return gr, gi

                def pass2(j, st):
                    dr, di = st
                    jj = SCAN_STEPS - 1 - j
                    gr, gi = corrected(jj, pw_ref[j, :, re_l], pw_ref[j, :, im_l])
                    qr, qi = st_ref[jj - 1, sub_rows, re_l], st_ref[jj - 1, sub_rows, im_l]
                    return dr + gr * qr + gi * qi, di + gi * qr - gr * qi

                dr, di = lax.fori_loop(0, SCAN_STEPS - 1, pass2, (zero, zero), unroll=2)
                gr, gi = corrected(0, pw_ref[SCAN_STEPS - 1, :, re_l], pw_ref[SCAN_STEPS - 1, :, im_l])
                sub = lax.broadcasted_iota(jnp.int32, (SCAN_SUB, SCAN_LANES), 0)
                if half == 0:
                    pv_r = prev_ref[SCAN_SUB - 1:SCAN_SUB, re_l] * have_prev
                    pv_i = prev_ref[SCAN_SUB - 1:SCAN_SUB, im_l] * have_prev
                else:
                    before = pl.ds(half * SCAN_SUB - 1, 1)
                    pv_r, pv_i = st_ref[SCAN_STEPS - 1, before, re_l], st_ref[SCAN_STEPS - 1, before, im_l]
                shape = (SCAN_SUB, SCAN_LANES)
                qr = jnp.where(sub == 0, jnp.broadcast_to(pv_r, shape),
                               pltpu.roll(st_ref[SCAN_STEPS - 1, sub_rows, re_l], 1, 0))
                qi = jnp.where(sub == 0, jnp.broadcast_to(pv_i, shape),
                               pltpu.roll(st_ref[SCAN_STEPS - 1, sub_rows, im_l], 1, 0))
                dab_ref[:, re_l] += dr + gr * qr + gi * qi
                dab_ref[:, im_l] += di + gi * qr - gr * qi


def _scan_view(a):
    return a.reshape(16, a.shape[0] // 16, a.shape[1])


def _pair_tile(p):
    start = (p * 2 * SSM_GROUP // PAIR_TILE) * PAIR_TILE
    return slice(start, start + PAIR_TILE)


def _pair_lanes(p):
    return pl.ds(p * PAIR_LANES, PAIR_LANES), pl.ds(SSM_LANES + p * PAIR_LANES, PAIR_LANES)


def _pair_store(s_ref, p, val):
    re_l, im_l = _pair_lanes(p)
    s_ref[:, :, re_l] = val[:, :PAIR_LANES].reshape(16, SCAN_COLS, PAIR_LANES)
    s_ref[:, :, im_l] = val[:, PAIR_LANES:].reshape(16, SCAN_COLS, PAIR_LANES)


def _pair_load(s_ref, p):
    re_l, im_l = _pair_lanes(p)
    parts = [s_ref[:, :, l].reshape(SCAN_BLOCK, PAIR_LANES) for l in (re_l, im_l)]
    return jnp.concatenate(parts, axis=1).astype(BF16)


def _pair_sum(fn):
    per = PAIR_TILE // (2 * SSM_GROUP)
    tiles = []
    for t in range(SSM_PAIRS // per):
        acc = None
        for p in range(t * per, (t + 1) * per):
            part = fn(p)
            acc = part if acc is None else acc + part
        tiles.append(acc)
    return jnp.concatenate(tiles, axis=1)


def _ssm_fwd(name, u, bb_mats, c_mats, pw_rows, d_skip):
    rows = u.shape[0]
    nl2 = 2 * SSM_LANES
    nblk = rows // SCAN_BLOCK

    def body(u_ref, bb_ref, c_ref, pw_ref, d_ref, y_ref, yg_ref, s_ref, carry_ref, tmp_ref):
        @pl.when(pl.program_id(0) == 0)
        def _():
            carry_ref[...] = jnp.zeros_like(carry_ref)

        uv = u_ref[...].reshape(SCAN_BLOCK, SSM_WIDTH)
        ub = uv.astype(BF16)
        for p in range(SSM_PAIRS):
            _pair_store(s_ref, p, _dot_nn(ub[:, _pair_tile(p)], bb_ref[p]))
        _scan_block(s_ref, carry_ref, tmp_ref, pw_ref, reverse=False)
        ys = _pair_sum(lambda p: _dot_nt(_pair_load(s_ref, p), c_ref[p]))
        yv = ys + d_ref[...] * uv
        y_ref[...] = yv.reshape(16, SCAN_COLS, SSM_WIDTH)
        yg_ref[...] = jax.nn.gelu(yv).astype(BF16).reshape(16, SCAN_COLS, SSM_WIDTH)

    const = lambda shape: pl.BlockSpec(shape, lambda i: (0,) * len(shape))
    blk = lambda cols: pl.BlockSpec((16, SCAN_COLS, cols), lambda i: (0, i, 0))
    pair_mats = const((SSM_PAIRS, PAIR_TILE, PAIR_TILE))
    y, yg, s = pl.pallas_call(
        body, name=name, grid=(nblk,),
        in_specs=[blk(SSM_WIDTH), pair_mats, pair_mats, const((SCAN_STEPS, SCAN_SUB, nl2)), const((1, SSM_WIDTH))],
        out_specs=[blk(SSM_WIDTH), blk(SSM_WIDTH), blk(nl2)],
        out_shape=[jax.ShapeDtypeStruct((16, rows // 16, SSM_WIDTH), F32),
                   jax.ShapeDtypeStruct((16, rows // 16, SSM_WIDTH), BF16),
                   jax.ShapeDtypeStruct((16, rows // 16, nl2), F32)],
        scratch_shapes=[pltpu.VMEM((SCAN_SUB, nl2), F32), pltpu.VMEM((2 * SCAN_SUB, nl2), F32)],
        compiler_params=_cparams("arbitrary"),
    )(_scan_view(u), bb_mats, c_mats, pw_rows, d_skip)
    return y.reshape(rows, SSM_WIDTH), yg.reshape(rows, SSM_WIDTH), s.reshape(rows, nl2)


def _ssm_bwd(name, dy, u, states, bb_mats, c_mats, pwc_rows, d_skip):
    rows = u.shape[0]
    nl2 = 2 * SSM_LANES
    nblk = rows // SCAN_BLOCK

    def body(dy_ref, u_ref, st_ref, prev_ref, bb_ref, c_ref, pw_ref, d_ref,
             du_ref, dbb_ref, dc_ref, dab_ref, dd_ref, g_ref, carry_ref, tmp_ref):
        i = pl.program_id(0)

        @pl.when(i == 0)
        def _():
            carry_ref[...] = jnp.zeros_like(carry_ref)
            for ref in (dbb_ref, dc_ref, dab_ref, dd_ref):
                ref[...] = jnp.zeros_like(ref)

        dyv = dy_ref[...].reshape(SCAN_BLOCK, SSM_WIDTH)
        uv = u_ref[...].reshape(SCAN_BLOCK, SSM_WIDTH)
        dyb, ub = dyv.astype(BF16), uv.astype(BF16)
        for p in range(SSM_PAIRS):
            _pair_store(g_ref, p, _dot_nn(dyb[:, _pair_tile(p)], c_ref[p]))
        have_prev = (i < nblk - 1).astype(F32)
        _scan_block(g_ref, carry_ref, tmp_ref, pw_ref, reverse=True,
                    sprev=(st_ref, prev_ref, have_prev, dab_ref))

        def pair_work(p):
            gp = _pair_load(g_ref, p)
            dbb_ref[p] += _dot_tn(ub[:, _pair_tile(p)], gp)
            dc_ref[p] += _dot_tn(dyb[:, _pair_tile(p)], _pair_load(st_ref, p))
            return _dot_nt(gp, bb_ref[p])

        du_ref[...] = (_pair_sum(pair_work) + d_ref[...] * dyv).reshape(16, SCAN_COLS, SSM_WIDTH)
        dd_ref[...] += jnp.sum(dyv * uv, axis=0, keepdims=True)

    const = lambda shape: pl.BlockSpec(shape, lambda i: (0,) * len(shape))
    blk = lambda cols: pl.BlockSpec((16, SCAN_COLS, cols), lambda i: (0, nblk - 1 - i, 0))
    per8 = SCAN_COLS // SCAN_SUB
    prev_spec = pl.BlockSpec((None, SCAN_SUB, nl2), lambda i: (15, jnp.maximum((nblk - 1 - i) * per8 - 1, 0), 0))
    pair_mats = const((SSM_PAIRS, PAIR_TILE, PAIR_TILE))
    pair_shape = jax.ShapeDtypeStruct((SSM_PAIRS, PAIR_TILE, PAIR_TILE), F32)
    sv = _scan_view(states)
    du, dbb, dc, dab, dd = pl.pallas_call(
        body, name=name, grid=(nblk,),
        in_specs=[blk(SSM_WIDTH), blk(SSM_WIDTH), blk(nl2), prev_spec, pair_mats, pair_mats,
                  const((SCAN_STEPS, SCAN_SUB, nl2)), const((1, SSM_WIDTH))],
        out_specs=[blk(SSM_WIDTH), pair_mats, pair_mats, const((SCAN_SUB, nl2)), const((1, SSM_WIDTH))],
        out_shape=[jax.ShapeDtypeStruct((16, rows // 16, SSM_WIDTH), F32), pair_shape, pair_shape,
                   jax.ShapeDtypeStruct((SCAN_SUB, nl2), F32), jax.ShapeDtypeStruct((1, SSM_WIDTH), F32)],
        scratch_shapes=[pltpu.VMEM((16, SCAN_COLS, nl2), F32), pltpu.VMEM((SCAN_SUB, nl2), F32),
                        pltpu.VMEM((2 * SCAN_SUB, nl2), F32)],
        compiler_params=_cparams("arbitrary"),
    )(_scan_view(dy), _scan_view(u), sv, sv, bb_mats, c_mats, pwc_rows, d_skip)
    return du.reshape(rows, SSM_WIDTH), dbb, dc, dab, dd


def _adamw(name, w, m, v, gparts, tr):
    rows, cols = w.shape

    def body(w_ref, m_ref, v_ref, g_ref, og_ref, od_ref, om_ref, ov_ref):
        g = g_ref[0].astype(F32)
        for i in range(1, N_DEV):
            g = g + g_ref[i].astype(F32)
        m_new = B1 * m_ref[...] + (1.0 - B1) * g
        v_new = B2 * v_ref[...] + (1.0 - B2) * (g * g)
        m_hat = m_new / (1.0 - B1 ** STEP)
        v_hat = v_new / (1.0 - B2 ** STEP)
        og_ref[...] = g
        od_ref[...] = -LR * (m_hat / (jnp.sqrt(v_hat) + ADAM_EPS) + WD * w_ref[...])
        om_ref[...] = m_new
        ov_ref[...] = v_new

    spec = pl.BlockSpec((tr, cols), lambda i: (i, 0))
    shape = jax.ShapeDtypeStruct((rows, cols), F32)
    return pl.pallas_call(
        body, name=name, grid=(rows // tr,),
        in_specs=[spec, spec, spec, pl.BlockSpec((N_DEV, tr, cols), lambda i: (0, i, 0))],
        out_specs=[spec] * 4, out_shape=[shape] * 4,
        compiler_params=_cparams("parallel"),
    )(w, m, v, gparts)


_SHARDED = (
    ("ffn1_w_gate", True, (352, 1024)), ("ffn1_w_up", True, (352, 1024)), ("ffn1_w_down", False, (352, 1024)),
    ("w_in", True, (608, 1024)), ("ssm_w_glu", True, (128, 512)), ("w_attn_branch", True, (128, 256)),
    ("w_ssm_branch", True, (128, 512)), ("w_out", False, (128, 1024)),
    ("ffn2_w_gate", True, (352, 1024)), ("ffn2_w_up", True, (352, 1024)), ("ffn2_w_down", False, (352, 1024)),
)
_SMALL = ("ffn1_norm", "mix_norm", "gate_bias", "rel_bias_table", "ssm_a_re", "ssm_a_im", "ssm_log_dt",
          "ssm_b_re", "ssm_b_im", "ssm_c_re", "ssm_c_im", "ssm_d", "ffn2_norm", "final_norm")
_ORDER = ("ffn1_norm", "ffn1_w_gate", "ffn1_w_up", "ffn1_w_down", "mix_norm", "w_in", "gate_bias",
          "rel_bias_table", "ssm_a_re", "ssm_a_im", "ssm_log_dt", "ssm_b_re", "ssm_b_im", "ssm_c_re",
          "ssm_c_im", "ssm_d", "ssm_w_glu", "w_attn_branch", "w_ssm_branch", "w_out", "ffn2_norm",
          "ffn2_w_gate", "ffn2_w_up", "ffn2_w_down", "final_norm")


def _pack_rows(shape):
    return shape[0] * shape[1] // D_MODEL


_SHARD_INFO = {nm: (tr, shape) for nm, tr, shape in _SHARDED}
_PHASES = {
    "f1gu": ("ffn1_w_gate", "ffn1_w_up"), "f1d": ("ffn1_w_down",),
    "mix": ("w_in", "ssm_w_glu", "w_attn_branch", "w_ssm_branch", "w_out"),
    "f2": ("ffn2_w_gate", "ffn2_w_up", "ffn2_w_down"),
}


def _to_rows(a, nm):
    tr, shape = _SHARD_INFO[nm]
    return (a.T if tr else a).reshape(_pack_rows(shape), D_MODEL)


def _from_rows(p, nm):
    tr, shape = _SHARD_INFO[nm]
    a = p.reshape(shape)
    return a.T if tr else a


def _full_weight(gathered, nm):
    _, shape = _SHARD_INFO[nm]
    return gathered.reshape(N_DEV * shape[0], shape[1])


def _grad_blocks(g, nm):
    _, shape = _SHARD_INFO[nm]
    return g.astype(BF16).reshape(N_DEV, _pack_rows(shape), D_MODEL)


_SMALL_TILE = 8 * 128


def _small_rows(a):
    flat = a.reshape(-1)
    return jnp.pad(flat, (0, (-flat.shape[0]) % _SMALL_TILE)).reshape(-1, 128)


def _pack_small(ws, last=None):
    tail = jnp.zeros((), F32) if last is None else last
    return jnp.concatenate([_small_rows(ws[nm]) for nm in _SMALL] + [_small_rows(tail)], axis=0)


def _unpack_small(pack, like):
    out, r0 = {}, 0
    for nm in _SMALL:
        n = like[nm].size
        nr = 8 * -(-n // _SMALL_TILE)
        out[nm] = pack[r0:r0 + nr].reshape(-1)[:n].reshape(like[nm].shape)
        r0 += nr
    return out


def _residue_order(a):
    rows, cols = a.shape
    return a.reshape(rows // 16, 16, cols).transpose(1, 0, 2).reshape(rows, cols)


def _token_order(a):
    rows, cols = a.shape
    return a.reshape(16, rows // 16, cols).transpose(1, 0, 2).reshape(rows, cols)


_PAIRS_PER_TILE = PAIR_TILE // (2 * SSM_GROUP)
_PAIR_AXES = (SSM_PAIRS // _PAIRS_PER_TILE, _PAIRS_PER_TILE, 2)


def _pair_matrices(re, im):
    six = jnp.stack([re, im]).reshape((2,) + _PAIR_AXES + (SSM_GROUP, SSM_STATE))
    eye_j, eye_l = jnp.eye(_PAIRS_PER_TILE, dtype=re.dtype), jnp.eye(2, dtype=re.dtype)
    mats = jnp.einsum("xkjlcn,jJ,lL->kjJLcxln", six, eye_j, eye_l)
    return mats.reshape(SSM_PAIRS, PAIR_TILE, PAIR_TILE).astype(BF16)


def _pair_diagonals(acc):
    k, j, l = _PAIR_AXES
    eight = acc.reshape(k, j, j, l, SSM_GROUP, 2, l, SSM_STATE)
    eye_j, eye_l = jnp.eye(j, dtype=acc.dtype), jnp.eye(l, dtype=acc.dtype)
    own = jnp.einsum("kjJLcxln,jJ,lL->xkjlcn", eight, eye_j, eye_l).reshape(2, SSM_GROUPS, SSM_GROUP, SSM_STATE)
    return own[0], own[1]


def _local_step(xs, target, small, weights_of, send_grads, first_deps=()):
    rows = xs.shape[0]
    gfull, gsmall = {}, {}

    table_t = small["rel_bias_table"].T
    tables, bias4 = [], []
    for g in range(N_GROUPS):
        bucket, valid = [jnp.asarray(t) for t in _attn_tables(g, rows)]
        bias_g = _bias_fwd(f"rel_bias_fwd_{g}", bucket, valid, table_t[g * HEADS_PER_GROUP:(g + 1) * HEADS_PER_GROUP])
        tables.append(bucket)
        bias4.append(bias_g.reshape(-1, bias_g.shape[-1]))
    pw_re, pw_im, bb_re, bb_im = _ssm_params_fwd(
        "ssm_params_fwd", small["ssm_a_re"], small["ssm_a_im"], small["ssm_log_dt"].reshape(SSM_GROUPS, 1),
        small["ssm_b_re"].transpose(2, 0, 1), small["ssm_b_im"].transpose(2, 0, 1))

    def power_rows(sign):
        row = jnp.concatenate([pw_re.reshape(SCAN_STEPS, 1, SSM_LANES), sign * pw_im.reshape(SCAN_STEPS, 1, SSM_LANES)],
                              axis=2)
        return jnp.broadcast_to(row, (SCAN_STEPS, SCAN_SUB, 2 * SSM_LANES))

    bb_mats = _pair_matrices(bb_re.transpose(1, 0, 2), bb_im.transpose(1, 0, 2))
    c_mats = _pair_matrices(small["ssm_c_re"], -small["ssm_c_im"])
    pw_fwd, pw_bwd = power_rows(1.0), power_rows(-1.0)
    d_skip = small["ssm_d"].reshape(1, SSM_WIDTH)
    wf = dict(weights_of("f1", [xs, target, bb_mats, c_mats, pw_fwd, pw_bwd] + bias4))

    x1, h1, gg1, uu1, hmix = _ffn_fwd("ffn1_fwd", xs, small["ffn1_norm"], wf["ffn1_w_gate"], wf["ffn1_w_up"],
                                      wf["ffn1_w_down"], small["mix_norm"], deps=first_deps)
    wf.update(weights_of("mix", x1))
    w_in = wf["w_in"]
    w_qkv, w_u, w_g = w_in[:3 * ATTN_WIDTH], w_in[3 * ATTN_WIDTH:3 * ATTN_WIDTH + SSM_WIDTH], w_in[3 * ATTN_WIDTH + SSM_WIDTH:]
    qscale = jnp.concatenate([jnp.full((1, ATTN_WIDTH), HEAD_DIM ** -0.5, F32), jnp.ones((1, 2 * ATTN_WIDTH), F32)], axis=1)
    qkv, = _mm("in_qkv", [(hmix, w_qkv)], True, 3 * ATTN_WIDTH, [BF16],
               epilogue=lambda acc, sc: (acc * sc,), extras=[(qscale, 0)], tn=ATTN_WIDTH)
    u, = _mm("in_u", [(hmix, w_u)], True, SSM_WIDTH, [F32])
    gates, = _mm("in_gates", [(hmix, w_g)], True, 2 * D_MODEL, [F32],
                 epilogue=lambda acc, b: (_sigmoid(acc + b),), extras=[(small["gate_bias"], 0)])

    o_g, lse_g = [], []
    for g in range(N_GROUPS):
        o, lse = _attn_fwd(f"attn_fwd_{g}", qkv, g, bias4[g])
        o_g.append(o)
        lse_g.append(lse)
    oa_f32, oa = _combine_fwd("attn_combine_fwd", o_g, lse_g)
    y_attn, = _mm("attn_branch", [(oa, wf["w_attn_branch"])], True, D_MODEL, [F32])
    y_raw, ygelu, states = _ssm_fwd("ssm_fwd", u, bb_mats, c_mats, pw_fwd, d_skip)
    glu, ysg = _mm("ssm_glu", [(ygelu, wf["ssm_w_glu"])], True, 2 * SSM_WIDTH, [F32, BF16],
                   epilogue=lambda gv: (gv, gv[:, :SSM_WIDTH] * _sigmoid(gv[:, SSM_WIDTH:])),
                   tn=2 * SSM_WIDTH, out_cols=[2 * SSM_WIDTH, SSM_WIDTH])
    y_ssm, merged = _mm("ssm_branch_merge", [(ysg, wf["w_ssm_branch"])], True, D_MODEL, [F32, BF16],
                        epilogue=lambda acc, ga, gs, ya: (acc, ga * ya + gs * acc),
                        extras=[(gates, 0), (gates, D_MODEL), (y_attn, 0)])
    x2, = _mm("mix_out", [(merged, wf["w_out"])], False, D_MODEL, [F32],
              epilogue=lambda acc, res: (res + acc,), extras=[(x1, 0)])
    wf.update(weights_of("f2", x2))
    dx3, h2, gg2, uu2, gsmall["final_norm"], gsmall["loss"] = _ffn_fwd_head(
        "ffn2_fwd", x2, small["ffn2_norm"], wf["ffn2_w_gate"], wf["ffn2_w_up"], wf["ffn2_w_down"],
        small["final_norm"].reshape(1, D_MODEL), target)

    dx2, dgg2, duu2, act2, gsmall["ffn2_norm"] = _ffn_bwd(
        "ffn2_bwd", dx3, x2, small["ffn2_norm"], gg2, uu2, wf["ffn2_w_gate"], wf["ffn2_w_up"], wf["ffn2_w_down"])
    gfull["ffn2_w_gate"] = _mm_tn("ffn2_dwg", dgg2, h2, out_dtype=BF16)
    gfull["ffn2_w_up"] = _mm_tn("ffn2_dwu", duu2, h2, out_dtype=BF16)
    gfull["ffn2_w_down"] = _mm_tn("ffn2_dwd", act2, dx3, scale=0.5, out_dtype=BF16)
    sent = send_grads("f2", gfull)

    def merge_bwd(dm, ga, gs, ya, ys):
        dza, dzs = dm * ya * ga * (1.0 - ga), dm * ys * gs * (1.0 - gs)
        return (dm * ga, dm * gs, dza, dzs, jnp.sum(dza, axis=0, keepdims=True), jnp.sum(dzs, axis=0, keepdims=True))

    dya, dys, dzga, dzgs, dba, dbs = _mm(
        "mix_out_bwd", [(dx2, wf["w_out"])], True, D_MODEL, [BF16] * 4, epilogue=merge_bwd, row_sums=2,
        extras=[(gates, 0), (gates, D_MODEL), (y_attn, 0), (y_ssm, 0)], deps=sent, tm=512, tn=D_MODEL)
    gfull["w_out"] = _mm_tn("dw_out", merged, dx2, out_dtype=BF16)
    gsmall["gate_bias"] = jnp.concatenate([dba, dbs], axis=1)

    gfull["w_ssm_branch"] = _mm_tn("dw_ssm_branch", dys, ysg, out_dtype=BF16)

    def glu_bwd(dysg, av, bv):
        sb = _sigmoid(bv)
        return (dysg * sb, dysg * av * sb * (1.0 - sb))

    dglu_a, dglu_b = _mm("ssm_branch_bwd", [(dys, wf["w_ssm_branch"])], False, SSM_WIDTH, [BF16, BF16],
                         epilogue=glu_bwd, extras=[(glu, 0), (glu, SSM_WIDTH)])
    w_glu = wf["ssm_w_glu"]
    gfull["ssm_w_glu"] = _mm_tn_stack("dw_glu", [dglu_a, dglu_b], ygelu, out_dtype=BF16)

    def gelu_bwd(acc, yv):
        _, vjp = jax.vjp(jax.nn.gelu, yv)
        return (vjp(acc)[0],)

    dy_raw, = _mm("ssm_glu_bwd", [(dglu_a, w_glu[:SSM_WIDTH]), (dglu_b, w_glu[SSM_WIDTH:])], False, SSM_WIDTH, [F32],
                  epilogue=gelu_bwd, extras=[(y_raw, 0)])
    du, dbb_acc, dc_acc, dab_rows, gsmall_d = _ssm_bwd(
        "ssm_bwd", dy_raw, u, states, bb_mats, c_mats, pw_bwd, d_skip)
    gsmall["ssm_d"] = gsmall_d
    dbb_re, dbb_im = [a.transpose(1, 0, 2) for a in _pair_diagonals(dbb_acc)]
    dc_re, dc_im = _pair_diagonals(dc_acc)
    gsmall["ssm_c_re"], gsmall["ssm_c_im"] = dc_re, -dc_im
    dab = _colsum("ssm_dab", dab_rows)
    d_ar, d_ai, d_ld, d_br, d_bi = _ssm_params_bwd(
        "ssm_params_bwd", small["ssm_a_re"], small["ssm_a_im"], small["ssm_log_dt"].reshape(SSM_GROUPS, 1),
        small["ssm_b_re"].transpose(2, 0, 1), small["ssm_b_im"].transpose(2, 0, 1),
        dab[:, :SSM_LANES].reshape(SSM_GROUPS, SSM_STATE), dab[:, SSM_LANES:].reshape(SSM_GROUPS, SSM_STATE),
        dbb_re, dbb_im)
    gsmall["ssm_a_re"], gsmall["ssm_a_im"], gsmall["ssm_log_dt"] = d_ar, d_ai, d_ld.reshape(SSM_GROUPS)
    gsmall["ssm_b_re"], gsmall["ssm_b_im"] = d_br.transpose(1, 2, 0), d_bi.transpose(1, 2, 0)

    gfull["w_attn_branch"] = _mm_tn("dw_attn_branch", dya, oa, out_dtype=BF16)
    doa, = _mm("attn_branch_bwd", [(dya, wf["w_attn_branch"])], False, ATTN_OUT, [F32])
    dc = _combine_bwd("attn_combine_bwd", doa, oa_f32, lse_g)
    dqkv_cols = [None] * 9
    dtable = []
    for g in range(N_GROUPS):
        dq, dk, dv, db = _attn_bwd(f"attn_bwd_{g}", qkv, dc[g], lse_g[g], dc[3 + g], g, bias4[g])
        dqkv_cols[g], dqkv_cols[3 + g], dqkv_cols[6 + g] = dq, dk, dv
        dt = _bias_bwd(f"rel_bias_bwd_{g}", tables[g], db.reshape(HEADS_PER_GROUP, -1, db.shape[-1]))
        dtable.append(dt[:, :HEADS_PER_GROUP])
    gsmall["rel_bias_table"] = jnp.concatenate(dtable, axis=1)

    gfull["w_in"] = jnp.concatenate([_mm_tn_stack("dw_in_qkv", dqkv_cols, hmix, out_dtype=BF16),
                                     _mm_tn_stack("dw_in_rest", [du, dzga, dzgs], hmix, out_dtype=BF16)], axis=0)
    sent = send_grads("mix", gfull)
    qkv_pairs = [(c, w_qkv[i * ATTN_OUT:(i + 1) * ATTN_OUT]) for i, c in enumerate(dqkv_cols)]

    def mix_norm_bwd(dh, xv, gain, dres):
        r, xh = _rms_parts(xv)
        return dres + _rms_bwd_dx(dh, gain, r, xh), jnp.sum(dh * xh, axis=0, keepdims=True)

    dx1, gsmall["mix_norm"] = _mm(
        "in_bwd", qkv_pairs + [(du, w_u), (dzga, w_g[:D_MODEL]), (dzgs, w_g[D_MODEL:])], False, D_MODEL, [F32],
        epilogue=mix_norm_bwd, row_sums=1, extras=[(x1, 0), (small["mix_norm"], 0), (dx2, 0)], tm=512, tn=D_MODEL,
        deps=sent)

    dx, dgg1, duu1, act1, gsmall["ffn1_norm"] = _ffn_bwd(
        "ffn1_bwd", dx1, xs, small["ffn1_norm"], gg1, uu1, wf["ffn1_w_gate"], wf["ffn1_w_up"], wf["ffn1_w_down"])
    sent = send_grads("small", gsmall)
    gfull["ffn1_w_gate"] = _mm_tn("ffn1_dwg", dgg1, h1, deps=sent, out_dtype=BF16)
    gfull["ffn1_w_up"] = _mm_tn("ffn1_dwu", duu1, h1, out_dtype=BF16)
    sent = send_grads("f1gu", gfull)
    gfull["ffn1_w_down"] = _mm_tn("ffn1_dwd", act1, dx1, scale=0.5, deps=sent, out_dtype=BF16)
    send_grads("f1d", gfull)
    return dx, gsmall


def kernel(x, ffn1_norm, ffn1_w_gate, ffn1_w_up, ffn1_w_down, mix_norm, w_in, gate_bias, rel_bias_table, ssm_a_re, ssm_a_im, ssm_log_dt, ssm_b_re, ssm_b_im, ssm_c_re, ssm_c_im, ssm_d, ssm_w_glu, w_attn_branch, w_ssm_branch, w_out, ffn2_norm, ffn2_w_gate, ffn2_w_up, ffn2_w_down, final_norm, loss_target, m_ffn1_norm, m_ffn1_w_gate, m_ffn1_w_up, m_ffn1_w_down, m_mix_norm, m_w_in, m_gate_bias, m_rel_bias_table, m_ssm_a_re, m_ssm_a_im, m_ssm_log_dt, m_ssm_b_re, m_ssm_b_im, m_ssm_c_re, m_ssm_c_im, m_ssm_d, m_ssm_w_glu, m_w_attn_branch, m_w_ssm_branch, m_w_out, m_ffn2_norm, m_ffn2_w_gate, m_ffn2_w_up, m_ffn2_w_down, m_final_norm, v_ffn1_norm, v_ffn1_w_gate, v_ffn1_w_up, v_ffn1_w_down, v_mix_norm, v_w_in, v_gate_bias, v_rel_bias_table, v_ssm_a_re, v_ssm_a_im, v_ssm_log_dt, v_ssm_b_re, v_ssm_b_im, v_ssm_c_re, v_ssm_c_im, v_ssm_d, v_ssm_w_glu, v_w_attn_branch, v_w_ssm_branch, v_w_out, v_ffn2_norm, v_ffn2_w_gate, v_ffn2_w_up, v_ffn2_w_down, v_final_norm):
    given = dict(locals())
    shapes = {nm: given[nm].shape for nm in _ORDER}

    def strip(a):
        return a[0] if a.ndim >= 2 and a.shape[0] == 1 else a

    w = {nm: strip(given[nm]) for nm in _ORDER}
    m = {nm: strip(given["m_" + nm]) for nm in _ORDER}
    v = {nm: strip(given["v_" + nm]) for nm in _ORDER}
    for d in (w, m, v):
        d["rel_bias_table"] = d["rel_bias_table"].reshape(N_BUCKETS, N_GROUPS * HEADS_PER_GROUP)

    weight_phases = {"f1": _PHASES["f1gu"] + _PHASES["f1d"], "mix": _PHASES["mix"], "f2": _PHASES["f2"]}
    pending_w, w_rows, deps, zero = {}, {}, [], 0.0
    for phase, names in weight_phases.items():
        w_rows.update({nm: _to_rows(w[nm] + zero, nm) for nm in names})
        pending_w[phase] = _exchange_start(f"gather_{phase}_start", [w_rows[nm].astype(BF16) for nm in names],
                                           gather=True, deps=deps, peers=_CHIP_PEERS if phase == "f1" else None)
        deps = [pending_w[phase][4]]
        zero = pending_w["f1"][4][0, 0]
    m_rows = {nm: _to_rows(m[nm] + zero, nm) for nm in _SHARD_INFO}
    v_rows = {nm: _to_rows(v[nm] + zero, nm) for nm in _SHARD_INFO}
    small = {nm: w[nm] for nm in _SMALL}
    small_in = {nm: small[nm] + zero for nm in _SMALL}
    for nm in ("ffn1_norm", "mix_norm", "ffn2_norm", "gate_bias"):
        small_in[nm] = small_in[nm].reshape(1, -1)

    def weights_of(phase, after):
        if phase == "f1":
            after = list(after) + list(m_rows.values()) + list(v_rows.values())
            landed = _exchange_wait("gather_f1_wait", pending_w[phase], after, gather=True, peers=_CHIP_PEERS)
            landed = _swap_with_sibling("gather_f1_swap", landed)
        else:
            landed = _exchange_wait(f"gather_{phase}_wait", pending_w[phase], after, gather=True)
        return {nm: _full_weight(got, nm) for nm, got in zip(weight_phases[phase], landed)}

    pending_g = {}

    def send_grads(phase, grads):
        if phase == "small":
            gs_pack = _pack_small({nm: grads[nm].reshape(small[nm].shape) for nm in _SMALL}, last=grads["loss"])
            pending_g[phase] = _exchange_start("gather_small_start", [gs_pack], gather=True)
        else:
            pending_g[phase] = _exchange_start(f"scatter_{phase}_start",
                                               [_grad_blocks(grads[nm], nm) for nm in _PHASES[phase]], gather=False)
        return [pending_g[phase][4]]

    dx, gsmall = _local_step(_residue_order(x[0]), _residue_order(loss_target[0]), small_in, weights_of, send_grads,
                             first_deps=[pending_w["f2"][4]])
    dx = _token_order(dx)

    updated = {}
    after = pending_g["f1d"][4]
    for phase in ("f2", "mix", "small", "f1gu", "f1d"):
        landed = _exchange_wait(f"exchange_{phase}_wait", pending_g[phase], after, gather=phase == "small")
        if phase == "small":
            sm = _adamw("adamw_small", _pack_small(small), _pack_small({nm: m[nm] for nm in _SMALL}),
                        _pack_small({nm: v[nm] for nm in _SMALL}), landed[0], landed[0].shape[1])
            after = sm[0]
            continue
        for nm, recv in zip(_PHASES[phase], landed):
            tr = max(t for t in range(16, 353, 16) if w_rows[nm].shape[0] % t == 0)
            updated[nm] = _adamw(f"adamw_{nm}", w_rows[nm], m_rows[nm], v_rows[nm], recv, tr)
            after = updated[nm][0]

    loss = sm[0][-8, 0]
    outs = []
    for i in range(4):
        sml = _unpack_small(sm[i], small)
        outs.append([(_from_rows(updated[nm][i], nm) if nm in updated else sml[nm]).reshape(shapes[nm])
                     for nm in _ORDER])
    return (loss, dx[None], *outs[0], *outs[1], *outs[2], *outs[3])
```

```python
import math

import numpy as np
import jax
import jax.numpy as jnp
from jax import lax
from jax.experimental import pallas as pl
from jax.experimental.pallas import tpu as pltpu

F32 = jnp.float32
BF16 = jnp.bfloat16

N_DEV = 8
D_MODEL = 1024
D_FF = 2816
HEAD_DIM = 64
HEADS_PER_GROUP = 4
DILATIONS = (1, 4, 16)
N_GROUPS = 3
ATTN_WIDTH = 768
ATTN_OUT = 256
BLOCK = 128
N_BUCKETS = 32
MAX_DISTANCE = 2048
NEG_INF = -1e30
SSM_WIDTH = 512
SSM_GROUPS = 32
SSM_GROUP = 16
SSM_STATE = 64
SSM_LANES = SSM_GROUPS * SSM_STATE
SSM_PAIRS = SSM_GROUPS // 2
PAIR_LANES = 2 * SSM_STATE
PAIR_TILE = 256
EPS = 1e-6
LR, B1, B2, ADAM_EPS, WD, STEP = 0.001, 0.9, 0.999, 1e-08, 0.01, 10

VMEM_LIMIT_BYTES = 56 * 1024 * 1024
FFN_CHUNK = 768
SCAN_BLOCK = 256
SCAN_STEPS = 16
SCAN_COLS = SCAN_BLOCK // SCAN_STEPS
SCAN_SUB = 8
SCAN_LANES = 512

MESH = pl.DeviceIdType.MESH


def _cparams(*sem):
    return pltpu.CompilerParams(dimension_semantics=sem, vmem_limit_bytes=VMEM_LIMIT_BYTES)


def _dot(a, b, dims):
    return lax.dot_general(a, b, (dims, ((), ())), preferred_element_type=F32)


def _dot_nn(a, b):
    return _dot(a, b, ((1,), (0,)))


def _dot_nt(a, b):
    return _dot(a, b, ((1,), (1,)))


def _dot_tn(a, b):
    return _dot(a, b, ((0,), (0,)))


def _sigmoid(x):
    return 1.0 / (1.0 + jnp.exp(-x))


_HBM_SPEC = pl.BlockSpec(memory_space=pltpu.HBM)
_SEM_SPEC = pl.BlockSpec(memory_space=pltpu.SEMAPHORE)
_ANY_SPEC = pl.BlockSpec(memory_space=pl.ANY)
_EFFECT = pltpu.SideEffectType.DATAFLOW_SIDE_EFFECTING


def _peers(x, y, c):
    return [(1 - x if k & 4 else x, 1 - y if k & 2 else y, 1 - c if k & 1 else c) for k in range(1, N_DEV)]


def _exchange_copies(x_refs, land_refs, send_sems, recv_sems, gather):
    x, y, c = lax.axis_index("x"), lax.axis_index("y"), lax.axis_index("c")
    me = 4 * x + 2 * y + c
    copies = []
    for a, (x_ref, land_ref) in enumerate(zip(x_refs, land_refs)):
        for k, (px, py, pc) in enumerate(_peers(x, y, c)):
            src = x_ref if gather else x_ref.at[4 * px + 2 * py + pc]
            copies.append(pltpu.make_async_remote_copy(
                src_ref=src, dst_ref=land_ref.at[me], send_sem=send_sems.at[N_DEV * a + k],
                recv_sem=recv_sems.at[(N_DEV - 1) * a + k], device_id=(px, py, pc), device_id_type=MESH))
    owns = [pltpu.make_async_copy(x_ref if gather else x_ref.at[me], land_ref.at[me],
                                  send_sems.at[N_DEV * a + N_DEV - 1])
            for a, (x_ref, land_ref) in enumerate(zip(x_refs, land_refs))]
    return owns, copies


def _exchange_start(name, xs_list, gather, deps=()):
    n, nd = len(xs_list), len(deps)
    land_shapes = [(N_DEV, *xs.shape) if gather else xs.shape for xs in xs_list]

    def body(*refs):
        x_refs, land_refs = refs[:n], refs[n:2 * n]
        send_sems, recv_sems = refs[2 * n + nd:2 * n + nd + 2]
        token = refs[-1]
        owns, copies = _exchange_copies(x_refs, land_refs, send_sems, recv_sems, gather)
        for cp in copies + owns:
            cp.start()
        token[...] = jnp.zeros_like(token)

    hbm = lambda a: pltpu.with_memory_space_constraint(a, pltpu.HBM)
    outs = pl.pallas_call(
        body, name=name,
        out_shape=(pltpu.SemaphoreType.DMA((n * N_DEV,)), pltpu.SemaphoreType.DMA((n * (N_DEV - 1),)),
                   *[pltpu.HBM(xs.shape, xs.dtype) for xs in xs_list],
                   *[pltpu.HBM(shape, xs.dtype) for shape, xs in zip(land_shapes, xs_list)],
                   jax.ShapeDtypeStruct((8, 128), F32)),
        in_specs=(_HBM_SPEC,) * (2 * n) + (_ANY_SPEC,) * nd,
        out_specs=(_SEM_SPEC, _SEM_SPEC) + (_HBM_SPEC,) * (2 * n) + (pl.BlockSpec(memory_space=pltpu.VMEM),),
        input_output_aliases={i: 2 + i for i in range(2 * n)},
        compiler_params=pltpu.CompilerParams(has_side_effects=_EFFECT),
    )(*[hbm(xs) for xs in xs_list], *[hbm(lax.empty(shape, xs.dtype)) for shape, xs in zip(land_shapes, xs_list)],
      *deps)
    return outs[0], outs[1], list(outs[2:2 + n]), list(outs[2 + n:2 + 2 * n]), outs[-1]


def _exchange_wait(name, handle, after, gather):
    send_sems, recv_sems, xs_thru, lands_thru, _ = handle
    n = len(xs_thru)
    after = list(after) if isinstance(after, (list, tuple)) else [after]

    def body(*refs):
        x_refs, land_refs = refs[:n], refs[n:2 * n]
        send_sems, recv_sems = refs[2 * n:2 * n + 2]
        owns, copies = _exchange_copies(x_refs, land_refs, send_sems, recv_sems, gather)
        for cp in copies:
            cp.wait_send()
            cp.wait_recv()
        for cp in owns:
            cp.wait()

    outs = pl.pallas_call(
        body, name=name,
        out_shape=tuple(pltpu.HBM(a.shape, a.dtype) for a in xs_thru + lands_thru),
        in_specs=(_HBM_SPEC,) * (2 * n) + (_SEM_SPEC, _SEM_SPEC) + (_ANY_SPEC,) * len(after),
        out_specs=(_HBM_SPEC,) * (2 * n), input_output_aliases={i: i for i in range(2 * n)},
        compiler_params=pltpu.CompilerParams(has_side_effects=_EFFECT),
    )(*xs_thru, *lands_thru, send_sems, recv_sems, *after)
    return list(outs[n:])


def _mm(name, pairs, nt, n_cols, out_dtypes, epilogue=None, extras=(), tm=1024, tn=512, deps=(), row_sums=0,
        out_cols=None):
    rows = pairs[0][0].shape[0]
    tm = min(tm, rows)
    tn = min(tn, n_cols)
    na, ne, nd, no = len(pairs), len(extras), len(deps), len(out_dtypes)

    def body(*refs):
        a_refs, w_refs = refs[:na], refs[na:2 * na]
        e_refs, o_refs = refs[2 * na:2 * na + ne], refs[2 * na + ne + nd:]
        acc = None
        for a_ref, w_ref in zip(a_refs, w_refs):
            a = a_ref[...].astype(BF16)
            w = w_ref[...].astype(BF16)
            p = _dot_nt(a, w) if nt else _dot_nn(a, w)
            acc = p if acc is None else acc + p
        outs = (acc,) if epilogue is None else epilogue(acc, *[e[...] for e in e_refs])
        for o_ref, o in zip(o_refs[:no], outs[:no]):
            o_ref[...] = o.astype(o_ref.dtype)
        for r_ref, o in zip(o_refs[no:], outs[no:]):
            @pl.when(pl.program_id(0) == 0)
            def _():
                r_ref[...] = jnp.zeros_like(r_ref)

            r_ref[...] += o

    in_specs = [pl.BlockSpec((tm, a.shape[1]), lambda i, j: (i, 0)) for a, _ in pairs]
    for _, w in pairs:
        if nt:
            in_specs.append(pl.BlockSpec((tn, w.shape[1]), lambda i, j: (j, 0)))
        else:
            in_specs.append(pl.BlockSpec((w.shape[0], tn), lambda i, j: (0, j)))
    for e, col_off in extras:
        off = col_off // tn
        if e.shape[0] == 1:
            in_specs.append(pl.BlockSpec((1, tn), lambda i, j, off=off: (0, j + off)))
        else:
            in_specs.append(pl.BlockSpec((tm, tn), lambda i, j, off=off: (i, j + off)))
    in_specs += [_ANY_SPEC] * nd
    if out_cols is None:
        out_cols = [n_cols] * no
    else:
        assert tn == n_cols, "outputs of other widths need the whole row in one block"
    assert not row_sums or tn == n_cols
    out_specs = [pl.BlockSpec((tm, tn * c // n_cols), lambda i, j: (i, j)) for c in out_cols]
    out_specs += [pl.BlockSpec((1, tn), lambda i, j: (0, j))] * row_sums
    out_shape = [jax.ShapeDtypeStruct((rows, c), dt) for c, dt in zip(out_cols, out_dtypes)]
    out_shape += [jax.ShapeDtypeStruct((1, n_cols), F32)] * row_sums
    outs = pl.pallas_call(
        body, name=name, grid=(rows // tm, n_cols // tn),
        in_specs=in_specs, out_specs=out_specs, out_shape=out_shape,
        compiler_params=_cparams("arbitrary" if row_sums else "parallel", "arbitrary"),
    )(*[a for a, _ in pairs], *[w for _, w in pairs], *[e for e, _ in extras], *deps)
    return outs


def _tn_rows(m):
    return max(b for b in range(128, min(m, 1408) + 1, 128) if m % b == 0)


def _mm_tn(name, a, b, scale=1.0, bm=None, tk=1024, deps=(), out_dtype=F32):
    rows, m = a.shape
    n = b.shape[1]
    bm = _tn_rows(m) if bm is None else bm
    tk = min(tk, rows)
    nk = rows // tk

    def body(a_ref, b_ref, *rest):
        o_ref, acc_ref = rest[-2:]
        k = pl.program_id(1)

        @pl.when(k == 0)
        def _():
            acc_ref[...] = jnp.zeros_like(acc_ref)

        acc_ref[...] += _dot_tn(a_ref[...].astype(BF16), b_ref[...].astype(BF16))

        @pl.when(k == nk - 1)
        def _():
            o_ref[...] = (acc_ref[...] * scale).astype(o_ref.dtype)

    return pl.pallas_call(
        body, name=name, grid=(m // bm, nk),
        in_specs=[pl.BlockSpec((tk, bm), lambda i, k: (k, i)), pl.BlockSpec((tk, n), lambda i, k: (k, 0))]
        + [_ANY_SPEC] * len(deps),
        out_specs=pl.BlockSpec((bm, n), lambda i, k: (i, 0)),
        out_shape=jax.ShapeDtypeStruct((m, n), out_dtype),
        scratch_shapes=[pltpu.VMEM((bm, n), F32)],
        compiler_params=_cparams("parallel", "arbitrary"),
    )(a, b, *deps)


def _mm_tn_stack(name, a_list, b, tk=1024, out_dtype=F32):
    rows, n = b.shape
    ms = [a.shape[1] for a in a_list]
    tk = min(tk, rows)
    nk = rows // tk
    na = len(a_list)

    def body(*refs):
        a_refs, b_ref, o_ref, acc_ref = refs[:na], refs[na], refs[na + 1], refs[na + 2]
        k = pl.program_id(0)

        @pl.when(k == 0)
        def _():
            acc_ref[...] = jnp.zeros_like(acc_ref)

        bv = b_ref[...].astype(BF16)
        r0 = 0
        for a_ref, m in zip(a_refs, ms):
            acc_ref[r0:r0 + m, :] += _dot_tn(a_ref[...].astype(BF16), bv)
            r0 += m

        @pl.when(k == nk - 1)
        def _():
            o_ref[...] = acc_ref[...].astype(o_ref.dtype)

    return pl.pallas_call(
        body, name=name, grid=(nk,),
        in_specs=[pl.BlockSpec((tk, m), lambda k: (k, 0)) for m in ms] + [pl.BlockSpec((tk, n), lambda k: (k, 0))],
        out_specs=pl.BlockSpec((sum(ms), n), lambda k: (0, 0)),
        out_shape=jax.ShapeDtypeStruct((sum(ms), n), out_dtype),
        scratch_shapes=[pltpu.VMEM((sum(ms), n), F32)],
        compiler_params=_cparams("arbitrary"),
    )(*a_list, b)


def _colsum(name, xs, tm=512):
    rows, cols = xs.shape
    tm = min(tm, rows)

    def body(x_ref, o_ref):
        @pl.when(pl.program_id(0) == 0)
        def _():
            o_ref[...] = jnp.zeros_like(o_ref)

        o_ref[...] += jnp.sum(x_ref[...].astype(F32), axis=0, keepdims=True)

    return pl.pallas_call(
        body, name=name, grid=(rows // tm,),
        in_specs=[pl.BlockSpec((tm, cols), lambda i: (i, 0))],
        out_specs=pl.BlockSpec((1, cols), lambda i: (0, 0)),
        out_shape=jax.ShapeDtypeStruct((1, cols), F32),
        compiler_params=_cparams("arbitrary"),
    )(xs)


def _ew(name, fn, ins, out_cols, out_dtypes, tm=512):
    rows = ins[0].shape[0]
    tm = min(tm, rows)
    ni = len(ins)

    def body(*refs):
        outs = fn(*[r[...] for r in refs[:ni]])
        for o_ref, o in zip(refs[ni:], outs):
            o_ref[...] = o.astype(o_ref.dtype)

    def spec(shape):
        if shape[0] == 1:
            return pl.BlockSpec((1, shape[1]), lambda i: (0, 0))
        return pl.BlockSpec((tm, shape[1]), lambda i: (i, 0))

    return pl.pallas_call(
        body, name=name, grid=(rows // tm,),
        in_specs=[spec(a.shape) for a in ins],
        out_specs=[pl.BlockSpec((tm, c), lambda i: (i, 0)) for c in out_cols],
        out_shape=[jax.ShapeDtypeStruct((rows, c), dt) for c, dt in zip(out_cols, out_dtypes)],
        compiler_params=_cparams("parallel"),
    )(*ins)


def _rms_parts(xv):
    r = lax.rsqrt(jnp.mean(xv * xv, axis=-1, keepdims=True) + EPS)
    return r, xv * r


def _rms_bwd_dx(dh, gain, r, xh):
    dxh = dh * gain
    return r * (dxh - xh * jnp.mean(dxh * xh, axis=-1, keepdims=True))


def _ffn_chunks(f_all):
    return [slice(c, min(c + FFN_CHUNK, f_all)) for c in range(0, f_all, FFN_CHUNK)]


def _loss_head(xo, gain_f, target, d):
    r, xh = _rms_parts(xo)
    err = xh * gain_f - target
    dy = err * (1.0 / d)
    per_tok = jnp.mean(err * err, axis=-1, keepdims=True)
    return (_rms_bwd_dx(dy, gain_f, r, xh), jnp.sum(dy * xh, axis=0, keepdims=True),
            0.5 * jnp.sum(per_tok, axis=0, keepdims=True))


def _ffn_tile(x_ref, g_ref, wg_ref, wu_ref, wd_ref, h_ref, gg_ref, uu_ref):
    xv = x_ref[...]
    _, xh = _rms_parts(xv)
    h = (xh * g_ref[...]).astype(BF16)
    h_ref[...] = h
    acc = None
    for cols in _ffn_chunks(wd_ref.shape[0]):
        gg = _dot_nt(h, wg_ref[cols, :])
        uu = _dot_nt(h, wu_ref[cols, :])
        act = gg * _sigmoid(gg) * uu
        part = _dot_nn(act.astype(BF16), wd_ref[cols, :])
        acc = part if acc is None else acc + part
        gg_ref[:, cols] = gg.astype(BF16)
        uu_ref[:, cols] = uu.astype(BF16)
    return xv + 0.5 * acc


def _ffn_fwd(name, xs, gain, wg_t, wu_t, wd, next_gain, tm=512, deps=()):
    rows, d = xs.shape
    f_all = wd.shape[0]
    tm = min(tm, rows)

    def body(x_ref, g_ref, wg_ref, wu_ref, wd_ref, ng_ref, *rest):
        xo_ref, h_ref, gg_ref, uu_ref, hn_ref = rest[-5:]
        xo = _ffn_tile(x_ref, g_ref, wg_ref, wu_ref, wd_ref, h_ref, gg_ref, uu_ref)
        xo_ref[...] = xo
        hn_ref[...] = (_rms_parts(xo)[1] * ng_ref[...]).astype(BF16)

    tile = pl.BlockSpec((tm, d), lambda i: (i, 0))
    row = pl.BlockSpec((1, d), lambda i: (0, 0))
    wspec = pl.BlockSpec((f_all, d), lambda i: (0, 0), pipeline_mode=pl.Buffered(1))
    hid = pl.BlockSpec((tm, f_all), lambda i: (i, 0))
    return pl.pallas_call(
        body, name=name, grid=(rows // tm,),
        in_specs=[tile, row, wspec, wspec, wspec, row] + [_ANY_SPEC] * len(deps),
        out_specs=[tile, tile, hid, hid, tile],
        out_shape=[jax.ShapeDtypeStruct((rows, d), F32), jax.ShapeDtypeStruct((rows, d), BF16),
                   jax.ShapeDtypeStruct((rows, f_all), BF16), jax.ShapeDtypeStruct((rows, f_all), BF16),
                   jax.ShapeDtypeStruct((rows, d), BF16)],
        compiler_params=_cparams("parallel"),
    )(xs, gain, wg_t, wu_t, wd, next_gain, *deps)


def _ffn_fwd_head(name, xs, gain, wg_t, wu_t, wd, gain_f, target, tm=512):
    rows, d = xs.shape
    f_all = wd.shape[0]
    tm = min(tm, rows)

    def body(x_ref, g_ref, wg_ref, wu_ref, wd_ref, gf_ref, t_ref, dxo_ref, h_ref, gg_ref, uu_ref, dgf_ref, loss_ref):
        xo = _ffn_tile(x_ref, g_ref, wg_ref, wu_ref, wd_ref, h_ref, gg_ref, uu_ref)
        dxo, dgf, loss = _loss_head(xo, gf_ref[...], t_ref[...], d)
        dxo_ref[...] = dxo

        @pl.when(pl.program_id(0) == 0)
        def _():
            dgf_ref[...] = jnp.zeros_like(dgf_ref)
            loss_ref[...] = jnp.zeros_like(loss_ref)

        dgf_ref[...] += dgf
        loss_ref[...] += loss

    tile = pl.BlockSpec((tm, d), lambda i: (i, 0))
    row = pl.BlockSpec((1, d), lambda i: (0, 0))
    wspec = pl.BlockSpec((f_all, d), lambda i: (0, 0), pipeline_mode=pl.Buffered(1))
    hid = pl.BlockSpec((tm, f_all), lambda i: (i, 0))
    return pl.pallas_call(
        body, name=name, grid=(rows // tm,),
        in_specs=[tile, row, wspec, wspec, wspec, row, tile],
        out_specs=[tile, tile, hid, hid, row, pl.BlockSpec((1, 1), lambda i: (0, 0))],
        out_shape=[jax.ShapeDtypeStruct((rows, d), F32), jax.ShapeDtypeStruct((rows, d), BF16),
                   jax.ShapeDtypeStruct((rows, f_all), BF16), jax.ShapeDtypeStruct((rows, f_all), BF16),
                   jax.ShapeDtypeStruct((1, d), F32), jax.ShapeDtypeStruct((1, 1), F32)],
        compiler_params=_cparams("arbitrary"),
    )(xs, gain, wg_t, wu_t, wd, gain_f, target)


def _ffn_bwd(name, dxo, xs, gain, gg_all, uu_all, wg_t, wu_t, wd, tm=256):
    rows, d = xs.shape
    f_all = wd.shape[0]
    tm = min(tm, rows)

    def body(dxo_ref, x_ref, g_ref, gg_ref, uu_ref, wg_ref, wu_ref, wd_ref,
             dx_ref, dgg_ref, duu_ref, act_ref, dgain_ref):
        dxo = dxo_ref[...]
        df = (0.5 * dxo).astype(BF16)
        dh = None
        for cols in _ffn_chunks(f_all):
            gg = gg_ref[:, cols].astype(F32)
            uu = uu_ref[:, cols].astype(F32)
            sg = _sigmoid(gg)
            silu = gg * sg
            dact = _dot_nt(df, wd_ref[cols, :])
            duu = (dact * silu).astype(BF16)
            dgg = (dact * uu * (sg * (1.0 + gg * (1.0 - sg)))).astype(BF16)
            act_ref[:, cols] = (silu * uu).astype(BF16)
            dgg_ref[:, cols] = dgg
            duu_ref[:, cols] = duu
            part = _dot_nn(dgg, wg_ref[cols, :]) + _dot_nn(duu, wu_ref[cols, :])
            dh = part if dh is None else dh + part
        r, xh = _rms_parts(x_ref[...])
        dx_ref[...] = dxo + _rms_bwd_dx(dh, g_ref[...], r, xh)

        @pl.when(pl.program_id(0) == 0)
        def _():
            dgain_ref[...] = jnp.zeros_like(dgain_ref)

        dgain_ref[...] += jnp.sum(dh * xh, axis=0, keepdims=True)

    tile = pl.BlockSpec((tm, d), lambda i: (i, 0))
    row = pl.BlockSpec((1, d), lambda i: (0, 0))
    wspec = pl.BlockSpec((f_all, d), lambda i: (0, 0), pipeline_mode=pl.Buffered(1))
    hid = pl.BlockSpec((tm, f_all), lambda i: (i, 0))
    hid_shape = jax.ShapeDtypeStruct((rows, f_all), BF16)
    return pl.pallas_call(
        body, name=name, grid=(rows // tm,),
        in_specs=[tile, tile, row, hid, hid, wspec, wspec, wspec],
        out_specs=[tile, hid, hid, hid, row],
        out_shape=[jax.ShapeDtypeStruct((rows, d), F32), hid_shape, hid_shape, hid_shape,
                   jax.ShapeDtypeStruct((1, d), F32)],
        compiler_params=_cparams("arbitrary"),
    )(dxo, xs, gain, gg_all, uu_all, wg_t, wu_t, wd)


def _t5_bucket_np(dist):
    max_exact = N_BUCKETS // 2
    dd = np.maximum(dist, 1).astype(np.float32)
    large = max_exact + (np.log(dd / np.float32(max_exact)) / np.float32(math.log(MAX_DISTANCE / max_exact))
                         * np.float32(N_BUCKETS - max_exact)).astype(np.int32)
    large = np.minimum(large, N_BUCKETS - 1)
    return np.where(dist < max_exact, dist, large).astype(np.int32)


def _attn_geometry(g, rows):
    run = rows // 16
    dil = DILATIONS[g]
    if dil == 16:
        bq = BLOCK
        return dict(view=(16, run), block=(None, bq), grid=(16, run // bq), index=lambda r, n: (r, n),
                    pos=np.arange(bq), bq=bq)
    if dil == 4:
        per = BLOCK // 4
        pos = (4 * np.arange(per)[None, :] + np.arange(4)[:, None]).reshape(-1)
        return dict(view=(4, 4, run), block=(4, None, per), grid=(4, run // per), index=lambda r, n: (0, r, n),
                    pos=pos, bq=BLOCK)
    per = 16
    pos = (16 * np.arange(per)[None, :] + np.arange(16)[:, None]).reshape(-1)
    return dict(view=(16, run), block=(16, per), grid=(1, run // per), index=lambda r, n: (0, n),
                pos=pos, bq=16 * per)


def _attn_tables(g, rows):
    geo = _attn_geometry(g, rows)
    pos, bq = geo["pos"], geo["bq"]
    steps = pos[:, None] - np.concatenate([pos - bq, pos])[None, :]
    valid = (steps >= 0) & (steps <= BLOCK)
    bucket = _t5_bucket_np((np.maximum(steps, 0) * DILATIONS[g]).astype(np.int32))
    return bucket, valid.astype(np.int32)


def _bias_fwd(name, bucket, valid, table_t):
    bq = bucket.shape[0]

    def body(bk_ref, ok_ref, tab_ref, o_ref):
        bk = bk_ref[...]
        ok = ok_ref[...] > 0
        accs = [jnp.zeros(bk.shape, F32)] * HEADS_PER_GROUP
        for b in range(N_BUCKETS):
            hit = bk == b
            accs = [jnp.where(hit, tab_ref[h, b], acc) for h, acc in enumerate(accs)]
        for h, acc in enumerate(accs):
            o_ref[h] = jnp.where(ok, acc, NEG_INF)

    vm = pl.BlockSpec(memory_space=pltpu.VMEM)
    return pl.pallas_call(
        body, name=name, in_specs=[vm, vm, pl.BlockSpec(memory_space=pltpu.SMEM)], out_specs=vm,
        out_shape=jax.ShapeDtypeStruct((HEADS_PER_GROUP, bq, 2 * bq), F32),
    )(bucket, valid, table_t)


def _bias_bwd(name, bucket, dbias):
    def body(bk_ref, db_ref, o_ref):
        row_id = lax.broadcasted_iota(jnp.int32, (N_BUCKETS, 128), 0)
        col_id = lax.broadcasted_iota(jnp.int32, (N_BUCKETS, 128), 1)
        bk = bk_ref[...]
        acc = jnp.zeros((N_BUCKETS, 128), F32)
        for h in range(HEADS_PER_GROUP):
            db = db_ref[h]
            for b in range(N_BUCKETS):
                part = jnp.sum(jnp.where(bk == b, db, 0.0), axis=0, keepdims=True)
                tot = jnp.sum(part, axis=1, keepdims=True)
                acc = jnp.where((row_id == b) & (col_id == h), tot, acc)
        o_ref[...] = acc

    vm = pl.BlockSpec(memory_space=pltpu.VMEM)
    return pl.pallas_call(body, name=name, in_specs=[vm, vm], out_specs=vm,
                          out_shape=jax.ShapeDtypeStruct((N_BUCKETS, 128), F32))(bucket, dbias)


def _head_of_lane(nrows):
    return lax.broadcasted_iota(jnp.int32, (nrows, ATTN_OUT), 1) // HEAD_DIM


def _stack_heads(a, lane_head):
    zero = jnp.zeros_like(a)
    return jnp.concatenate([jnp.where(lane_head == h, a, zero) for h in range(HEADS_PER_GROUP)], axis=0)


def _unstack_heads(a4, lane_head, bq):
    out = a4[:bq]
    for h in range(1, HEADS_PER_GROUP):
        out = jnp.where(lane_head == h, a4[h * bq:(h + 1) * bq], out)
    return out


def _attn_specs(geo, cols, col_block, index):
    return pl.BlockSpec(geo["block"] + (cols,), lambda r, n: index(r, n) + (col_block,))


def _attn_fwd(name, qkv, g, bias4):
    rows = qkv.shape[0]
    geo = _attn_geometry(g, rows)
    bq, (nsub, nb), index = geo["bq"], geo["grid"], geo["index"]
    blk_shape = tuple(b for b in geo["block"] if b is not None) + (ATTN_OUT,)

    def body(q_ref, kc_ref, kp_ref, vc_ref, vp_ref, b_ref, o_ref, lse_ref):
        n = pl.program_id(1)
        lane_head = _head_of_lane(bq)
        flat = lambda ref: ref[...].reshape(bq, ATTN_OUT)
        q4 = _stack_heads(flat(q_ref), lane_head)
        k2 = jnp.concatenate([flat(kp_ref), flat(kc_ref)], axis=0)
        v2 = jnp.concatenate([flat(vp_ref), flat(vc_ref)], axis=0)
        s = _dot_nt(q4, k2) + b_ref[...]
        col = lax.broadcasted_iota(jnp.int32, s.shape, 1)
        s = jnp.where((col >= bq) | (n > 0), s, NEG_INF)
        mx = jnp.max(s, axis=-1, keepdims=True)
        p = jnp.exp(s - mx)
        den = jnp.sum(p, axis=-1, keepdims=True)
        o4 = _dot_nn(p.astype(BF16), v2) / den
        lse4 = jnp.broadcast_to(mx + jnp.log(den), (HEADS_PER_GROUP * bq, ATTN_OUT))
        o_ref[...] = _unstack_heads(o4, lane_head, bq).reshape(blk_shape)
        lse_ref[...] = _unstack_heads(lse4, lane_head, bq).reshape(blk_shape)

    prev = lambda r, n: index(r, jnp.maximum(n - 1, 0))
    view = lambda a: a.reshape(geo["view"] + (a.shape[1],))
    qkv_v = view(qkv)
    out_spec = _attn_specs(geo, ATTN_OUT, 0, index)
    out_shape = jax.ShapeDtypeStruct(geo["view"] + (ATTN_OUT,), F32)
    o, lse = pl.pallas_call(
        body, name=name, grid=(nsub, nb),
        in_specs=[_attn_specs(geo, ATTN_OUT, g, index), _attn_specs(geo, ATTN_OUT, 3 + g, index),
                  _attn_specs(geo, ATTN_OUT, 3 + g, prev), _attn_specs(geo, ATTN_OUT, 6 + g, index),
                  _attn_specs(geo, ATTN_OUT, 6 + g, prev), pl.BlockSpec(bias4.shape, lambda r, n: (0, 0))],
        out_specs=[out_spec, out_spec], out_shape=[out_shape, out_shape],
        compiler_params=_cparams("parallel", "arbitrary"),
    )(qkv_v, qkv_v, qkv_v, qkv_v, qkv_v, bias4)
    return o.reshape(rows, ATTN_OUT), lse.reshape(rows, ATTN_OUT)


def _attn_bwd(name, qkv, do, lse, cvec, g, bias4):
    rows = qkv.shape[0]
    geo = _attn_geometry(g, rows)
    bq, (nsub, nb), index = geo["bq"], geo["grid"], geo["index"]
    blk_shape = tuple(b for b in geo["block"] if b is not None) + (ATTN_OUT,)
    nlead = len(blk_shape) - 1

    def body(q_ref, kc_ref, kp_ref, vc_ref, vp_ref, do_ref, lse_ref, c_ref, b_ref,
             dq_ref, dk_ref, dv_ref, db_ref, kcar_ref, vcar_ref):
        r, n = pl.program_id(0), pl.program_id(1)
        valid = n < nb
        lane_head = _head_of_lane(bq)
        flat = lambda ref: ref[...].reshape(bq, ATTN_OUT)

        @pl.when((r == 0) & (n == 0))
        def _():
            kcar_ref[...] = jnp.zeros_like(kcar_ref)
            vcar_ref[...] = jnp.zeros_like(vcar_ref)
            db_ref[...] = jnp.zeros_like(db_ref)

        def column(ref, h):
            lead = (slice(None),) * nlead
            return ref[lead + (pl.ds(h * HEAD_DIM, 1),)].reshape(bq, 1)

        q4 = _stack_heads(flat(q_ref), lane_head)
        do4 = _stack_heads(flat(do_ref), lane_head)
        k2 = jnp.concatenate([flat(kp_ref), flat(kc_ref)], axis=0)
        v2 = jnp.concatenate([flat(vp_ref), flat(vc_ref)], axis=0)
        lse4 = jnp.concatenate([column(lse_ref, h) for h in range(HEADS_PER_GROUP)], axis=0)
        c4 = jnp.concatenate([column(c_ref, h) for h in range(HEADS_PER_GROUP)], axis=0)
        s = _dot_nt(q4, k2) + b_ref[...]
        col = lax.broadcasted_iota(jnp.int32, s.shape, 1)
        keep = ((col >= bq) | (n > 0)) & valid
        p = jnp.where(keep, jnp.exp(s - lse4), 0.0)
        ds = p * (_dot_nt(do4, v2) + c4)
        ds_b = ds.astype(BF16)

        @pl.when(valid)
        def _():
            dq = _unstack_heads(_dot_nn(ds_b, k2), lane_head, bq) * (HEAD_DIM ** -0.5)
            dq_ref[...] = dq.astype(BF16).reshape(blk_shape)

        dk2 = _dot_tn(ds_b, q4)
        dv2 = _dot_tn(p.astype(BF16), do4)
        dk_ref[...] = (kcar_ref[...] + dk2[:bq]).astype(BF16).reshape(blk_shape)
        dv_ref[...] = (vcar_ref[...] + dv2[:bq]).astype(BF16).reshape(blk_shape)
        kcar_ref[...] = dk2[bq:]
        vcar_ref[...] = dv2[bq:]
        db_ref[...] += ds

    cur = lambda r, n: index(r, jnp.minimum(n, nb - 1))
    prev = lambda r, n: index(r, jnp.maximum(jnp.minimum(n, nb - 1) - 1, 0))
    late = lambda r, n: index(r, jnp.maximum(n - 1, 0))
    view = lambda a: a.reshape(geo["view"] + (a.shape[1],))
    qkv_v = view(qkv)
    tile = _attn_specs(geo, ATTN_OUT, 0, cur)
    bias_spec = pl.BlockSpec(bias4.shape, lambda r, n: (0, 0))
    out_shape = jax.ShapeDtypeStruct(geo["view"] + (ATTN_OUT,), BF16)
    dq, dk, dv, db = pl.pallas_call(
        body, name=name, grid=(nsub, nb + 1),
        in_specs=[_attn_specs(geo, ATTN_OUT, g, cur), _attn_specs(geo, ATTN_OUT, 3 + g, cur),
                  _attn_specs(geo, ATTN_OUT, 3 + g, prev), _attn_specs(geo, ATTN_OUT, 6 + g, cur),
                  _attn_specs(geo, ATTN_OUT, 6 + g, prev), tile, tile, tile, bias_spec],
        out_specs=[tile, _attn_specs(geo, ATTN_OUT, 0, late), _attn_specs(geo, ATTN_OUT, 0, late), bias_spec],
        out_shape=[out_shape, out_shape, out_shape, jax.ShapeDtypeStruct(bias4.shape, F32)],
        scratch_shapes=[pltpu.VMEM((bq, ATTN_OUT), F32), pltpu.VMEM((bq, ATTN_OUT), F32)],
        compiler_params=_cparams("arbitrary", "arbitrary"),
    )(qkv_v, qkv_v, qkv_v, qkv_v, qkv_v, view(do), view(lse), view(cvec), bias4)
    return dq.reshape(rows, ATTN_OUT), dk.reshape(rows, ATTN_OUT), dv.reshape(rows, ATTN_OUT), db


def _group_weights(lses):
    mx = jnp.maximum(jnp.maximum(lses[0], lses[1]), lses[2])
    es = [jnp.exp(l - mx) for l in lses]
    den = es[0] + es[1] + es[2]
    return [e / den for e in es]


def _combine_fwd(name, os_, lses):
    def fn(o0, o1, o2, l0, l1, l2):
        ws = _group_weights([l0, l1, l2])
        out = ws[0] * o0 + ws[1] * o1 + ws[2] * o2
        return out, out

    return _ew(name, fn, [*os_, *lses], [ATTN_OUT, ATTN_OUT], [F32, BF16], tm=1024)


def _combine_bwd(name, do, oa, lses):
    def fn(dov, oav, l0, l1, l2):
        head_sum = (lax.broadcasted_iota(jnp.int32, (ATTN_OUT, ATTN_OUT), 0) // HEAD_DIM
                    == lax.broadcasted_iota(jnp.int32, (ATTN_OUT, ATTN_OUT), 1) // HEAD_DIM)
        ws = _group_weights([l0, l1, l2])
        prod = dov * oav
        hi = prod.astype(BF16)
        lo = (prod - hi.astype(F32)).astype(BF16)
        ones = jnp.where(head_sum, 1.0, 0.0).astype(BF16)
        bar = _dot_nn(hi, ones) + _dot_nn(lo, ones)
        return tuple(w * dov for w in ws) + tuple(-w * bar for w in ws)

    return _ew(name, fn, [do, oa, *lses], [ATTN_OUT] * 6, [BF16] * 3 + [F32] * 3, tm=1024)


def _ssm_disc(a_re, a_im, log_dt, b_re, b_im):
    dt = jnp.exp(log_dt)
    mag = jnp.exp(a_re * dt)
    ab_re = mag * jnp.cos(a_im * dt)
    ab_im = mag * jnp.sin(a_im * dt)
    den = a_re * a_re + a_im * a_im
    xr = ab_re - 1.0
    coef_re = (xr * a_re + ab_im * a_im) / den
    coef_im = (ab_im * a_re - xr * a_im) / den
    bb_re = coef_re[None] * b_re - coef_im[None] * b_im
    bb_im = coef_re[None] * b_im + coef_im[None] * b_re
    return ab_re, ab_im, bb_re, bb_im


def _ssm_params_fwd(name, a_re, a_im, log_dt, b_re, b_im, c_re, c_im):
    pows = jax.ShapeDtypeStruct((SCAN_STEPS,) + a_re.shape, F32)
    mats = jax.ShapeDtypeStruct((SSM_PAIRS, PAIR_TILE, PAIR_TILE), BF16)
    per_tile = PAIR_TILE // (2 * SSM_GROUP)

    def body(ar, ai, ld, br, bi, cr, ci, o_pr, o_pi, o_bb, o_c, bbr_ref, bbi_ref, wide_ref):
        ab_re, ab_im, bb_re, bb_im = _ssm_disc(ar[...], ai[...], ld[...], br[...], bi[...])
        pr, pi = ab_re, ab_im
        for j in range(SCAN_STEPS):
            o_pr[j] = pr
            o_pi[j] = pi
            pr, pi = pr * ab_re - pi * ab_im, pr * ab_im + pi * ab_re
        bbr_ref[...] = bb_re
        bbi_ref[...] = bb_im

        def place(out_ref, block):
            wide_ref[...] = jnp.zeros_like(wide_ref)
            for g in range(SSM_GROUPS):
                p, l = divmod(g, 2)
                rows = pl.ds((p % per_tile) * 2 * SSM_GROUP + l * SSM_GROUP, SSM_GROUP)
                re, im = block(g)
                wide_ref[p, rows, pl.ds(l * SSM_STATE, SSM_STATE)] = re
                wide_ref[p, rows, pl.ds(PAIR_LANES + l * SSM_STATE, SSM_STATE)] = im
            out_ref[...] = wide_ref[...].astype(BF16)

        place(o_bb, lambda g: (bbr_ref[:, g, :], bbi_ref[:, g, :]))
        place(o_c, lambda g: (cr[g], -ci[g]))

    vm = pl.BlockSpec(memory_space=pltpu.VMEM)
    return pl.pallas_call(
        body, name=name, in_specs=[vm] * 7, out_specs=[vm] * 4, out_shape=[pows, pows, mats, mats],
        scratch_shapes=[pltpu.VMEM(b_re.shape, F32), pltpu.VMEM(b_re.shape, F32),
                        pltpu.VMEM((SSM_PAIRS, PAIR_TILE, PAIR_TILE), F32)],
    )(a_re, a_im, log_dt, b_re, b_im, c_re, c_im)


def _ssm_params_bwd(name, a_re, a_im, log_dt, b_re, b_im, d_ab_re, d_ab_im, d_bb_re, d_bb_im):
    gn = jax.ShapeDtypeStruct(a_re.shape, F32)
    cgn = jax.ShapeDtypeStruct(b_re.shape, F32)

    def body(ar, ai, ld, br, bi, g0, g1, g2, g3, o_ar, o_ai, o_ld, o_br, o_bi):
        _, vjp = jax.vjp(_ssm_disc, ar[...], ai[...], ld[...], br[...], bi[...])
        outs = vjp((g0[...], g1[...], g2[...], g3[...]))
        for o_ref, o in zip((o_ar, o_ai, o_ld, o_br, o_bi), outs):
            o_ref[...] = o

    vm = pl.BlockSpec(memory_space=pltpu.VMEM)
    return pl.pallas_call(body, name=name, in_specs=[vm] * 9, out_specs=[vm] * 5,
                          out_shape=[gn, gn, jax.ShapeDtypeStruct(log_dt.shape, F32), cgn, cgn],
                          )(a_re, a_im, log_dt, b_re, b_im, d_ab_re, d_ab_im, d_bb_re, d_bb_im)


def _scan_block(s_ref, carry_ref, tmp_ref, pw_ref, reverse, sprev=None):
    nl = SSM_LANES
    halves = range(SCAN_COLS // SCAN_SUB)
    zero = jnp.zeros((SCAN_SUB, SCAN_LANES), F32)
    for half in (reversed(halves) if reverse else halves):
        sub_rows = pl.ds(half * SCAN_SUB, SCAN_SUB)
        for lc in range(nl // SCAN_LANES):
            re_l = pl.ds(lc * SCAN_LANES, SCAN_LANES)
            im_l = pl.ds(nl + lc * SCAN_LANES, SCAN_LANES)
            are, aim = pw_ref[0, :, re_l], pw_ref[0, :, im_l]

            def step_of(j):
                return SCAN_STEPS - 1 - j if reverse else j

            def pass1(j, st):
                sr, si = st
                jj = step_of(j)
                nr = are * sr - aim * si + s_ref[jj, sub_rows, re_l]
                ni = are * si + aim * sr + s_ref[jj, sub_rows, im_l]
                s_ref[jj, sub_rows, re_l] = nr
                s_ref[jj, sub_rows, im_l] = ni
                return nr, ni

            er, ei = lax.fori_loop(0, SCAN_STEPS, pass1, (zero, zero), unroll=2)
            tmp_ref[0:SCAN_SUB, re_l] = er
            tmp_ref[0:SCAN_SUB, im_l] = ei
            apr, api = pw_ref[SCAN_STEPS - 1, 0:1, re_l], pw_ref[SCAN_STEPS - 1, 0:1, im_l]
            sr, si = carry_ref[0:1, re_l], carry_ref[0:1, im_l]
            for step in range(SCAN_SUB):
                c = SCAN_SUB - 1 - step if reverse else step
                tmp_ref[SCAN_SUB + c:SCAN_SUB + c + 1, re_l] = sr
                tmp_ref[SCAN_SUB + c:SCAN_SUB + c + 1, im_l] = si
                e_r, e_i = tmp_ref[c:c + 1, re_l], tmp_ref[c:c + 1, im_l]
                sr, si = apr * sr - api * si + e_r, apr * si + api * sr + e_i
            carry_ref[0:1, re_l] = sr
            carry_ref[0:1, im_l] = si
            cr = tmp_ref[SCAN_SUB:2 * SCAN_SUB, re_l]
            ci = tmp_ref[SCAN_SUB:2 * SCAN_SUB, im_l]

            if sprev is None:
                def pass2(j, st):
                    pr, pi = pw_ref[j, :, re_l], pw_ref[j, :, im_l]
                    jj = step_of(j)
                    s_ref[jj, sub_rows, re_l] += pr * cr - pi * ci
                    s_ref[jj, sub_rows, im_l] += pr * ci + pi * cr
                    return st

                lax.fori_loop(0, SCAN_STEPS, pass2, 0, unroll=2)
            else:
                st_ref, prev_ref, have_prev, dab_ref = sprev

                def corrected(jj, pr, pi):
                    gr = s_ref[jj, sub_rows, re_l] + pr * cr - pi * ci
                    gi = s_ref[jj, sub_rows, im_l] + pr * ci + pi * cr
                    s_ref[jj, sub_rows, re_l] = gr
                    s_ref[jj, sub_rows, im_l] = gi
                    return gr, gi

                def pass2(j, st):
                    dr, di = st
                    jj = SCAN_STEPS - 1 - j
                    gr, gi = corrected(jj, pw_ref[j, :, re_l], pw_ref[j, :, im_l])
                    qr, qi = st_ref[jj - 1, sub_rows, re_l], st_ref[jj - 1, sub_rows, im_l]
                    return dr + gr * qr + gi * qi, di + gi * qr - gr * qi

                dr, di = lax.fori_loop(0, SCAN_STEPS - 1, pass2, (zero, zero), unroll=2)
                gr, gi = corrected(0, pw_ref[SCAN_STEPS - 1, :, re_l], pw_ref[SCAN_STEPS - 1, :, im_l])
                sub = lax.broadcasted_iota(jnp.int32, (SCAN_SUB, SCAN_LANES), 0)
                if half == 0:
                    pv_r = prev_ref[SCAN_SUB - 1:SCAN_SUB, re_l] * have_prev
                    pv_i = prev_ref[SCAN_SUB - 1:SCAN_SUB, im_l] * have_prev
                else:
                    before = pl.ds(half * SCAN_SUB - 1, 1)
                    pv_r, pv_i = st_ref[SCAN_STEPS - 1, before, re_l], st_ref[SCAN_STEPS - 1, before, im_l]
                shape = (SCAN_SUB, SCAN_LANES)
                qr = jnp.where(sub == 0, jnp.broadcast_to(pv_r, shape),
                               pltpu.roll(st_ref[SCAN_STEPS - 1, sub_rows, re_l], 1, 0))
                qi = jnp.where(sub == 0, jnp.broadcast_to(pv_i, shape),
                               pltpu.roll(st_ref[SCAN_STEPS - 1, sub_rows, im_l], 1, 0))
                dab_ref[:, re_l] += dr + gr * qr + gi * qi
                dab_ref[:, im_l] += di + gi * qr - gr * qi


def _scan_view(a):
    return a.reshape(16, a.shape[0] // 16, a.shape[1])


def _pair_tile(p):
    start = (p * 2 * SSM_GROUP // PAIR_TILE) * PAIR_TILE
    return slice(start, start + PAIR_TILE)


def _pair_lanes(p):
    return pl.ds(p * PAIR_LANES, PAIR_LANES), pl.ds(SSM_LANES + p * PAIR_LANES, PAIR_LANES)


def _pair_store(s_ref, p, val):
    re_l, im_l = _pair_lanes(p)
    s_ref[:, :, re_l] = val[:, :PAIR_LANES].reshape(16, SCAN_COLS, PAIR_LANES)
    s_ref[:, :, im_l] = val[:, PAIR_LANES:].reshape(16, SCAN_COLS, PAIR_LANES)


def _pair_load(s_ref, p):
    re_l, im_l = _pair_lanes(p)
    parts = [s_ref[:, :, l].reshape(SCAN_BLOCK, PAIR_LANES) for l in (re_l, im_l)]
    return jnp.concatenate(parts, axis=1).astype(BF16)


def _pair_sum(fn):
    per = PAIR_TILE // (2 * SSM_GROUP)
    tiles = []
    for t in range(SSM_PAIRS // per):
        acc = None
        for p in range(t * per, (t + 1) * per):
            part = fn(p)
            acc = part if acc is None else acc + part
        tiles.append(acc)
    return jnp.concatenate(tiles, axis=1)


def _ssm_fwd(name, u, bb_mats, c_mats, pw_rows, d_skip):
    rows = u.shape[0]
    nl2 = 2 * SSM_LANES
    nblk = rows // SCAN_BLOCK

    def body(u_ref, bb_ref, c_ref, pw_ref, d_ref, y_ref, yg_ref, s_ref, carry_ref, tmp_ref):
        @pl.when(pl.program_id(0) == 0)
        def _():
            carry_ref[...] = jnp.zeros_like(carry_ref)

        uv = u_ref[...].reshape(SCAN_BLOCK, SSM_WIDTH)
        ub = uv.astype(BF16)
        for p in range(SSM_PAIRS):
            _pair_store(s_ref, p, _dot_nn(ub[:, _pair_tile(p)], bb_ref[p]))
        _scan_block(s_ref, carry_ref, tmp_ref, pw_ref, reverse=False)
        ys = _pair_sum(lambda p: _dot_nt(_pair_load(s_ref, p), c_ref[p]))
        yv = ys + d_ref[...] * uv
        y_ref[...] = yv.reshape(16, SCAN_COLS, SSM_WIDTH)
        yg_ref[...] = jax.nn.gelu(yv).astype(BF16).reshape(16, SCAN_COLS, SSM_WIDTH)

    const = lambda shape: pl.BlockSpec(shape, lambda i: (0,) * len(shape))
    blk = lambda cols: pl.BlockSpec((16, SCAN_COLS, cols), lambda i: (0, i, 0))
    pair_mats = const((SSM_PAIRS, PAIR_TILE, PAIR_TILE))
    y, yg, s = pl.pallas_call(
        body, name=name, grid=(nblk,),
        in_specs=[blk(SSM_WIDTH), pair_mats, pair_mats, const((SCAN_STEPS, SCAN_SUB, nl2)), const((1, SSM_WIDTH))],
        out_specs=[blk(SSM_WIDTH), blk(SSM_WIDTH), blk(nl2)],
        out_shape=[jax.ShapeDtypeStruct((16, rows // 16, SSM_WIDTH), F32),
                   jax.ShapeDtypeStruct((16, rows // 16, SSM_WIDTH), BF16),
                   jax.ShapeDtypeStruct((16, rows // 16, nl2), F32)],
        scratch_shapes=[pltpu.VMEM((SCAN_SUB, nl2), F32), pltpu.VMEM((2 * SCAN_SUB, nl2), F32)],
        compiler_params=_cparams("arbitrary"),
    )(_scan_view(u), bb_mats, c_mats, pw_rows, d_skip)
    return y.reshape(rows, SSM_WIDTH), yg.reshape(rows, SSM_WIDTH), s.reshape(rows, nl2)


def _ssm_bwd(name, dy, u, states, bb_mats, c_mats, pwc_rows, d_skip):
    rows = u.shape[0]
    nl2 = 2 * SSM_LANES
    nblk = rows // SCAN_BLOCK

    def body(dy_ref, u_ref, st_ref, prev_ref, bb_ref, c_ref, pw_ref, d_ref,
             du_ref, dbb_ref, dc_ref, dab_ref, dd_ref, g_ref, carry_ref, tmp_ref):
        i = pl.program_id(0)

        @pl.when(i == 0)
        def _():
            carry_ref[...] = jnp.zeros_like(carry_ref)
            for ref in (dbb_ref, dc_ref, dab_ref, dd_ref):
                ref[...] = jnp.zeros_like(ref)

        dyv = dy_ref[...].reshape(SCAN_BLOCK, SSM_WIDTH)
        uv = u_ref[...].reshape(SCAN_BLOCK, SSM_WIDTH)
        dyb, ub = dyv.astype(BF16), uv.astype(BF16)
        for p in range(SSM_PAIRS):
            _pair_store(g_ref, p, _dot_nn(dyb[:, _pair_tile(p)], c_ref[p]))
        have_prev = (i < nblk - 1).astype(F32)
        _scan_block(g_ref, carry_ref, tmp_ref, pw_ref, reverse=True,
                    sprev=(st_ref, prev_ref, have_prev, dab_ref))

        def pair_work(p):
            gp = _pair_load(g_ref, p)
            dbb_ref[p] += _dot_tn(ub[:, _pair_tile(p)], gp)
            dc_ref[p] += _dot_tn(dyb[:, _pair_tile(p)], _pair_load(st_ref, p))
            return _dot_nt(gp, bb_ref[p])

        du_ref[...] = (_pair_sum(pair_work) + d_ref[...] * dyv).reshape(16, SCAN_COLS, SSM_WIDTH)
        dd_ref[...] += jnp.sum(dyv * uv, axis=0, keepdims=True)

    const = lambda shape: pl.BlockSpec(shape, lambda i: (0,) * len(shape))
    blk = lambda cols: pl.BlockSpec((16, SCAN_COLS, cols), lambda i: (0, nblk - 1 - i, 0))
    per8 = SCAN_COLS // SCAN_SUB
    prev_spec = pl.BlockSpec((None, SCAN_SUB, nl2), lambda i: (15, jnp.maximum((nblk - 1 - i) * per8 - 1, 0), 0))
    pair_mats = const((SSM_PAIRS, PAIR_TILE, PAIR_TILE))
    pair_shape = jax.ShapeDtypeStruct((SSM_PAIRS, PAIR_TILE, PAIR_TILE), F32)
    sv = _scan_view(states)
    du, dbb, dc, dab, dd = pl.pallas_call(
        body, name=name, grid=(nblk,),
        in_specs=[blk(SSM_WIDTH), blk(SSM_WIDTH), blk(nl2), prev_spec, pair_mats, pair_mats,
                  const((SCAN_STEPS, SCAN_SUB, nl2)), const((1, SSM_WIDTH))],
        out_specs=[blk(SSM_WIDTH), pair_mats, pair_mats, const((SCAN_SUB, nl2)), const((1, SSM_WIDTH))],
        out_shape=[jax.ShapeDtypeStruct((16, rows // 16, SSM_WIDTH), F32), pair_shape, pair_shape,
                   jax.ShapeDtypeStruct((SCAN_SUB, nl2), F32), jax.ShapeDtypeStruct((1, SSM_WIDTH), F32)],
        scratch_shapes=[pltpu.VMEM((16, SCAN_COLS, nl2), F32), pltpu.VMEM((SCAN_SUB, nl2), F32),
                        pltpu.VMEM((2 * SCAN_SUB, nl2), F32)],
        compiler_params=_cparams("arbitrary"),
    )(_scan_view(dy), _scan_view(u), sv, sv, bb_mats, c_mats, pwc_rows, d_skip)
    return du.reshape(rows, SSM_WIDTH), dbb, dc, dab, dd


def _adamw(name, w, m, v, gparts, tr):
    rows, cols = w.shape

    def body(w_ref, m_ref, v_ref, g_ref, og_ref, od_ref, om_ref, ov_ref):
        g = g_ref[0].astype(F32)
        for i in range(1, N_DEV):
            g = g + g_ref[i].astype(F32)
        m_new = B1 * m_ref[...] + (1.0 - B1) * g
        v_new = B2 * v_ref[...] + (1.0 - B2) * (g * g)
        m_hat = m_new / (1.0 - B1 ** STEP)
        v_hat = v_new / (1.0 - B2 ** STEP)
        og_ref[...] = g
        od_ref[...] = -LR * (m_hat / (jnp.sqrt(v_hat) + ADAM_EPS) + WD * w_ref[...])
        om_ref[...] = m_new
        ov_ref[...] = v_new

    spec = pl.BlockSpec((tr, cols), lambda i: (i, 0))
    shape = jax.ShapeDtypeStruct((rows, cols), F32)
    return pl.pallas_call(
        body, name=name, grid=(rows // tr,),
        in_specs=[spec, spec, spec, pl.BlockSpec((N_DEV, tr, cols), lambda i: (0, i, 0))],
        out_specs=[spec] * 4, out_shape=[shape] * 4,
        compiler_params=_cparams("parallel"),
    )(w, m, v, gparts)


_SHARDED = (
    ("ffn1_w_gate", True, (352, 1024)), ("ffn1_w_up", True, (352, 1024)), ("ffn1_w_down", False, (352, 1024)),
    ("w_in", True, (608, 1024)), ("ssm_w_glu", True, (128, 512)), ("w_attn_branch", True, (128, 256)),
    ("w_ssm_branch", True, (128, 512)), ("w_out", False, (128, 1024)),
    ("ffn2_w_gate", True, (352, 1024)), ("ffn2_w_up", True, (352, 1024)), ("ffn2_w_down", False, (352, 1024)),
)
_SMALL = ("ffn1_norm", "mix_norm", "gate_bias", "rel_bias_table", "ssm_a_re", "ssm_a_im", "ssm_log_dt",
          "ssm_b_re", "ssm_b_im", "ssm_c_re", "ssm_c_im", "ssm_d", "ffn2_norm", "final_norm")
_ORDER = ("ffn1_norm", "ffn1_w_gate", "ffn1_w_up", "ffn1_w_down", "mix_norm", "w_in", "gate_bias",
          "rel_bias_table", "ssm_a_re", "ssm_a_im", "ssm_log_dt", "ssm_b_re", "ssm_b_im", "ssm_c_re",
          "ssm_c_im", "ssm_d", "ssm_w_glu", "w_attn_branch", "w_ssm_branch", "w_out", "ffn2_norm",
          "ffn2_w_gate", "ffn2_w_up", "ffn2_w_down", "final_norm")


def _pack_rows(shape):
    return shape[0] * shape[1] // D_MODEL


_SHARD_INFO = {nm: (tr, shape) for nm, tr, shape in _SHARDED}
_PHASES = {
    "f1gu": ("ffn1_w_gate", "ffn1_w_up"), "f1d": ("ffn1_w_down",),
    "mix": ("w_in", "ssm_w_glu", "w_attn_branch", "w_ssm_branch", "w_out"),
    "f2": ("ffn2_w_gate", "ffn2_w_up", "ffn2_w_down"),
}


def _to_rows(a, nm):
    tr, shape = _SHARD_INFO[nm]
    return (a.T if tr else a).reshape(_pack_rows(shape), D_MODEL)


def _from_rows(p, nm):
    tr, shape = _SHARD_INFO[nm]
    a = p.reshape(shape)
    return a.T if tr else a


def _full_weight(gathered, nm):
    _, shape = _SHARD_INFO[nm]
    return gathered.reshape(N_DEV * shape[0], shape[1])


def _grad_blocks(g, nm):
    _, shape = _SHARD_INFO[nm]
    return g.astype(BF16).reshape(N_DEV, _pack_rows(shape), D_MODEL)


_SMALL_TILE = 8 * 128


def _small_rows(a):
    flat = a.reshape(-1)
    return jnp.pad(flat, (0, (-flat.shape[0]) % _SMALL_TILE)).reshape(-1, 128)


def _pack_small(ws, last=None):
    tail = jnp.zeros((), F32) if last is None else last
    return jnp.concatenate([_small_rows(ws[nm]) for nm in _SMALL] + [_small_rows(tail)], axis=0)


def _unpack_small(pack, like):
    out, r0 = {}, 0
    for nm in _SMALL:
        n = like[nm].size
        nr = 8 * -(-n // _SMALL_TILE)
        out[nm] = pack[r0:r0 + nr].reshape(-1)[:n].reshape(like[nm].shape)
        r0 += nr
    return out


def _residue_order(a):
    rows, cols = a.shape
    return a.reshape(rows // 16, 16, cols).transpose(1, 0, 2).reshape(rows, cols)


def _token_order(a):
    rows, cols = a.shape
    return a.reshape(16, rows // 16, cols).transpose(1, 0, 2).reshape(rows, cols)


_PAIRS_PER_TILE = PAIR_TILE // (2 * SSM_GROUP)
_PAIR_AXES = (SSM_PAIRS // _PAIRS_PER_TILE, _PAIRS_PER_TILE, 2)


def _pair_diagonals(acc):
    k, j, l = _PAIR_AXES
    eight = acc.reshape(k, j, j, l, SSM_GROUP, 2, l, SSM_STATE)
    eye_j, eye_l = jnp.eye(j, dtype=acc.dtype), jnp.eye(l, dtype=acc.dtype)
    own = jnp.einsum("kjJLcxln,jJ,lL->xkjlcn", eight, eye_j, eye_l).reshape(2, SSM_GROUPS, SSM_GROUP, SSM_STATE)
    return own[0], own[1]


def _local_step(xs, target, small, weights_of, send_grads, first_deps=()):
    rows = xs.shape[0]
    gfull, gsmall = {}, {}

    table_t = small["rel_bias_table"].T
    tables, bias4 = [], []
    for g in range(N_GROUPS):
        bucket, valid = [jnp.asarray(t) for t in _attn_tables(g, rows)]
        bias_g = _bias_fwd(f"rel_bias_fwd_{g}", bucket, valid, table_t[g * HEADS_PER_GROUP:(g + 1) * HEADS_PER_GROUP])
        tables.append(bucket)
        bias4.append(bias_g.reshape(-1, bias_g.shape[-1]))
    pw_re, pw_im, bb_mats, c_mats = _ssm_params_fwd(
        "ssm_params_fwd", small["ssm_a_re"], small["ssm_a_im"], small["ssm_log_dt"].reshape(SSM_GROUPS, 1),
        small["ssm_b_re"].transpose(2, 0, 1), small["ssm_b_im"].transpose(2, 0, 1), small["ssm_c_re"], small["ssm_c_im"])

    def power_rows(sign):
        row = jnp.concatenate([pw_re.reshape(SCAN_STEPS, 1, SSM_LANES), sign * pw_im.reshape(SCAN_STEPS, 1, SSM_LANES)],
                              axis=2)
        return jnp.broadcast_to(row, (SCAN_STEPS, SCAN_SUB, 2 * SSM_LANES))

    pw_fwd, pw_bwd = power_rows(1.0), power_rows(-1.0)
    d_skip = small["ssm_d"].reshape(1, SSM_WIDTH)
    wf = dict(weights_of("f1", [xs, target, bb_mats, c_mats, pw_fwd, pw_bwd] + bias4))

    x1, h1, gg1, uu1, hmix = _ffn_fwd("ffn1_fwd", xs, small["ffn1_norm"], wf["ffn1_w_gate"], wf["ffn1_w_up"],
                                      wf["ffn1_w_down"], small["mix_norm"], deps=first_deps)
    wf.update(weights_of("mix", x1))
    w_in = wf["w_in"]
    w_qkv, w_u, w_g = w_in[:3 * ATTN_WIDTH], w_in[3 * ATTN_WIDTH:3 * ATTN_WIDTH + SSM_WIDTH], w_in[3 * ATTN_WIDTH + SSM_WIDTH:]
    qscale = jnp.concatenate([jnp.full((1, ATTN_WIDTH), HEAD_DIM ** -0.5, F32), jnp.ones((1, 2 * ATTN_WIDTH), F32)], axis=1)
    qkv, = _mm("in_qkv", [(hmix, w_qkv)], True, 3 * ATTN_WIDTH, [BF16],
               epilogue=lambda acc, sc: (acc * sc,), extras=[(qscale, 0)], tn=ATTN_WIDTH)
    u, = _mm("in_u", [(hmix, w_u)], True, SSM_WIDTH, [F32])
    gates, = _mm("in_gates", [(hmix, w_g)], True, 2 * D_MODEL, [F32],
                 epilogue=lambda acc, b: (_sigmoid(acc + b),), extras=[(small["gate_bias"], 0)])

    o_g, lse_g = [], []
    for g in range(N_GROUPS):
        o, lse = _attn_fwd(f"attn_fwd_{g}", qkv, g, bias4[g])
        o_g.append(o)
        lse_g.append(lse)
    oa_f32, oa = _combine_fwd("attn_combine_fwd", o_g, lse_g)
    y_attn, = _mm("attn_branch", [(oa, wf["w_attn_branch"])], True, D_MODEL, [F32])
    y_raw, ygelu, states = _ssm_fwd("ssm_fwd", u, bb_mats, c_mats, pw_fwd, d_skip)
    glu, ysg = _mm("ssm_glu", [(ygelu, wf["ssm_w_glu"])], True, 2 * SSM_WIDTH, [F32, BF16],
                   epilogue=lambda gv: (gv, gv[:, :SSM_WIDTH] * _sigmoid(gv[:, SSM_WIDTH:])),
                   tn=2 * SSM_WIDTH, out_cols=[2 * SSM_WIDTH, SSM_WIDTH])
    y_ssm, merged = _mm("ssm_branch_merge", [(ysg, wf["w_ssm_branch"])], True, D_MODEL, [F32, BF16],
                        epilogue=lambda acc, ga, gs, ya: (acc, ga * ya + gs * acc),
                        extras=[(gates, 0), (gates, D_MODEL), (y_attn, 0)])
    x2, = _mm("mix_out", [(merged, wf["w_out"])], False, D_MODEL, [F32],
              epilogue=lambda acc, res: (res + acc,), extras=[(x1, 0)])
    wf.update(weights_of("f2", x2))
    dx3, h2, gg2, uu2, gsmall["final_norm"], gsmall["loss"] = _ffn_fwd_head(
        "ffn2_fwd", x2, small["ffn2_norm"], wf["ffn2_w_gate"], wf["ffn2_w_up"], wf["ffn2_w_down"],
        small["final_norm"].reshape(1, D_MODEL), target)

    dx2, dgg2, duu2, act2, gsmall["ffn2_norm"] = _ffn_bwd(
        "ffn2_bwd", dx3, x2, small["ffn2_norm"], gg2, uu2, wf["ffn2_w_gate"], wf["ffn2_w_up"], wf["ffn2_w_down"])
    gfull["ffn2_w_gate"] = _mm_tn("ffn2_dwg", dgg2, h2, out_dtype=BF16)
    gfull["ffn2_w_up"] = _mm_tn("ffn2_dwu", duu2, h2, out_dtype=BF16)
    gfull["ffn2_w_down"] = _mm_tn("ffn2_dwd", act2, dx3, scale=0.5, out_dtype=BF16)
    sent = send_grads("f2", gfull)

    def merge_bwd(dm, ga, gs, ya, ys):
        dza, dzs = dm * ya * ga * (1.0 - ga), dm * ys * gs * (1.0 - gs)
        return (dm * ga, dm * gs, dza, dzs, jnp.sum(dza, axis=0, keepdims=True), jnp.sum(dzs, axis=0, keepdims=True))

    dya, dys, dzga, dzgs, dba, dbs = _mm(
        "mix_out_bwd", [(dx2, wf["w_out"])], True, D_MODEL, [BF16] * 4, epilogue=merge_bwd, row_sums=2,
        extras=[(gates, 0), (gates, D_MODEL), (y_attn, 0), (y_ssm, 0)], deps=sent, tm=512, tn=D_MODEL)
    gfull["w_out"] = _mm_tn("dw_out", merged, dx2, out_dtype=BF16)
    gsmall["gate_bias"] = jnp.concatenate([dba, dbs], axis=1)

    gfull["w_ssm_branch"] = _mm_tn("dw_ssm_branch", dys, ysg, out_dtype=BF16)

    def glu_bwd(dysg, av, bv):
        sb = _sigmoid(bv)
        return (dysg * sb, dysg * av * sb * (1.0 - sb))

    dglu_a, dglu_b = _mm("ssm_branch_bwd", [(dys, wf["w_ssm_branch"])], False, SSM_WIDTH, [BF16, BF16],
                         epilogue=glu_bwd, extras=[(glu, 0), (glu, SSM_WIDTH)])
    w_glu = wf["ssm_w_glu"]
    gfull["ssm_w_glu"] = _mm_tn_stack("dw_glu", [dglu_a, dglu_b], ygelu, out_dtype=BF16)

    def gelu_bwd(acc, yv):
        _, vjp = jax.vjp(jax.nn.gelu, yv)
        return (vjp(acc)[0],)

    dy_raw, = _mm("ssm_glu_bwd", [(dglu_a, w_glu[:SSM_WIDTH]), (dglu_b, w_glu[SSM_WIDTH:])], False, SSM_WIDTH, [F32],
                  epilogue=gelu_bwd, extras=[(y_raw, 0)])
    du, dbb_acc, dc_acc, dab_rows, gsmall_d = _ssm_bwd(
        "ssm_bwd", dy_raw, u, states, bb_mats, c_mats, pw_bwd, d_skip)
    gsmall["ssm_d"] = gsmall_d
    dbb_re, dbb_im = [a.transpose(1, 0, 2) for a in _pair_diagonals(dbb_acc)]
    dc_re, dc_im = _pair_diagonals(dc_acc)
    gsmall["ssm_c_re"], gsmall["ssm_c_im"] = dc_re, -dc_im
    dab = _colsum("ssm_dab", dab_rows)
    d_ar, d_ai, d_ld, d_br, d_bi = _ssm_params_bwd(
        "ssm_params_bwd", small["ssm_a_re"], small["ssm_a_im"], small["ssm_log_dt"].reshape(SSM_GROUPS, 1),
        small["ssm_b_re"].transpose(2, 0, 1), small["ssm_b_im"].transpose(2, 0, 1),
        dab[:, :SSM_LANES].reshape(SSM_GROUPS, SSM_STATE), dab[:, SSM_LANES:].reshape(SSM_GROUPS, SSM_STATE),
        dbb_re, dbb_im)
    gsmall["ssm_a_re"], gsmall["ssm_a_im"], gsmall["ssm_log_dt"] = d_ar, d_ai, d_ld.reshape(SSM_GROUPS)
    gsmall["ssm_b_re"], gsmall["ssm_b_im"] = d_br.transpose(1, 2, 0), d_bi.transpose(1, 2, 0)

    gfull["w_attn_branch"] = _mm_tn("dw_attn_branch", dya, oa, out_dtype=BF16)
    doa, = _mm("attn_branch_bwd", [(dya, wf["w_attn_branch"])], False, ATTN_OUT, [F32])
    dc = _combine_bwd("attn_combine_bwd", doa, oa_f32, lse_g)
    dqkv_cols = [None] * 9
    dtable = []
    for g in range(N_GROUPS):
        dq, dk, dv, db = _attn_bwd(f"attn_bwd_{g}", qkv, dc[g], lse_g[g], dc[3 + g], g, bias4[g])
        dqkv_cols[g], dqkv_cols[3 + g], dqkv_cols[6 + g] = dq, dk, dv
        dt = _bias_bwd(f"rel_bias_bwd_{g}", tables[g], db.reshape(HEADS_PER_GROUP, -1, db.shape[-1]))
        dtable.append(dt[:, :HEADS_PER_GROUP])
    gsmall["rel_bias_table"] = jnp.concatenate(dtable, axis=1)

    gfull["w_in"] = jnp.concatenate([_mm_tn_stack("dw_in_qkv", dqkv_cols, hmix, out_dtype=BF16),
                                     _mm_tn_stack("dw_in_rest", [du, dzga, dzgs], hmix, out_dtype=BF16)], axis=0)
    sent = send_grads("mix", gfull)
    qkv_pairs = [(c, w_qkv[i * ATTN_OUT:(i + 1) * ATTN_OUT]) for i, c in enumerate(dqkv_cols)]

    def mix_norm_bwd(dh, xv, gain, dres):
        r, xh = _rms_parts(xv)
        return dres + _rms_bwd_dx(dh, gain, r, xh), jnp.sum(dh * xh, axis=0, keepdims=True)

    dx1, gsmall["mix_norm"] = _mm(
        "in_bwd", qkv_pairs + [(du, w_u), (dzga, w_g[:D_MODEL]), (dzgs, w_g[D_MODEL:])], False, D_MODEL, [F32],
        epilogue=mix_norm_bwd, row_sums=1, extras=[(x1, 0), (small["mix_norm"], 0), (dx2, 0)], tm=512, tn=D_MODEL,
        deps=sent)

    dx, dgg1, duu1, act1, gsmall["ffn1_norm"] = _ffn_bwd(
        "ffn1_bwd", dx1, xs, small["ffn1_norm"], gg1, uu1, wf["ffn1_w_gate"], wf["ffn1_w_up"], wf["ffn1_w_down"])
    sent = send_grads("small", gsmall)
    gfull["ffn1_w_gate"] = _mm_tn("ffn1_dwg", dgg1, h1, deps=sent, out_dtype=BF16)
    gfull["ffn1_w_up"] = _mm_tn("ffn1_dwu", duu1, h1, out_dtype=BF16)
    sent = send_grads("f1gu", gfull)
    gfull["ffn1_w_down"] = _mm_tn("ffn1_dwd", act1, dx1, scale=0.5, deps=sent, out_dtype=BF16)
    send_grads("f1d", gfull)
    return dx, gsmall


def kernel(x, ffn1_norm, ffn1_w_gate, ffn1_w_up, ffn1_w_down, mix_norm, w_in, gate_bias, rel_bias_table, ssm_a_re, ssm_a_im, ssm_log_dt, ssm_b_re, ssm_b_im, ssm_c_re, ssm_c_im, ssm_d, ssm_w_glu, w_attn_branch, w_ssm_branch, w_out, ffn2_norm, ffn2_w_gate, ffn2_w_up, ffn2_w_down, final_norm, loss_target, m_ffn1_norm, m_ffn1_w_gate, m_ffn1_w_up, m_ffn1_w_down, m_mix_norm, m_w_in, m_gate_bias, m_rel_bias_table, m_ssm_a_re, m_ssm_a_im, m_ssm_log_dt, m_ssm_b_re, m_ssm_b_im, m_ssm_c_re, m_ssm_c_im, m_ssm_d, m_ssm_w_glu, m_w_attn_branch, m_w_ssm_branch, m_w_out, m_ffn2_norm, m_ffn2_w_gate, m_ffn2_w_up, m_ffn2_w_down, m_final_norm, v_ffn1_norm, v_ffn1_w_gate, v_ffn1_w_up, v_ffn1_w_down, v_mix_norm, v_w_in, v_gate_bias, v_rel_bias_table, v_ssm_a_re, v_ssm_a_im, v_ssm_log_dt, v_ssm_b_re, v_ssm_b_im, v_ssm_c_re, v_ssm_c_im, v_ssm_d, v_ssm_w_glu, v_w_attn_branch, v_w_ssm_branch, v_w_out, v_ffn2_norm, v_ffn2_w_gate, v_ffn2_w_up, v_ffn2_w_down, v_final_norm):
    given = dict(locals())
    shapes = {nm: given[nm].shape for nm in _ORDER}

    def strip(a):
        return a[0] if a.ndim >= 2 and a.shape[0] == 1 else a

    w = {nm: strip(given[nm]) for nm in _ORDER}
    m = {nm: strip(given["m_" + nm]) for nm in _ORDER}
    v = {nm: strip(given["v_" + nm]) for nm in _ORDER}
    for d in (w, m, v):
        d["rel_bias_table"] = d["rel_bias_table"].reshape(N_BUCKETS, N_GROUPS * HEADS_PER_GROUP)

    weight_phases = {"f1": _PHASES["f1gu"] + _PHASES["f1d"], "mix": _PHASES["mix"], "f2": _PHASES["f2"]}
    pending_w, w_rows, deps, zero = {}, {}, [], 0.0
    for phase, names in weight_phases.items():
        w_rows.update({nm: _to_rows(w[nm] + zero, nm) for nm in names})
        pending_w[phase] = _exchange_start(f"gather_{phase}_start", [w_rows[nm].astype(BF16) for nm in names],
                                           gather=True, deps=deps)
        deps = [pending_w[phase][4]]
        zero = pending_w["f1"][4][0, 0]
    m_rows = {nm: _to_rows(m[nm] + zero, nm) for nm in _SHARD_INFO}
    v_rows = {nm: _to_rows(v[nm] + zero, nm) for nm in _SHARD_INFO}
    small = {nm: w[nm] for nm in _SMALL}
    small_in = {nm: small[nm] + zero for nm in _SMALL}
    for nm in ("ffn1_norm", "mix_norm", "ffn2_norm", "gate_bias"):
        small_in[nm] = small_in[nm].reshape(1, -1)

    def weights_of(phase, after):
        if phase == "f1":
            after = list(after) + list(m_rows.values()) + list(v_rows.values())
        landed = _exchange_wait(f"gather_{phase}_wait", pending_w[phase], after, gather=True)
        return {nm: _full_weight(got, nm) for nm, got in zip(weight_phases[phase], landed)}

    pending_g = {}

    def send_grads(phase, grads):
        if phase == "small":
            gs_pack = _pack_small({nm: grads[nm].reshape(small[nm].shape) for nm in _SMALL}, last=grads["loss"])
            pending_g[phase] = _exchange_start("gather_small_start", [gs_pack], gather=True)
        else:
            pending_g[phase] = _exchange_start(f"scatter_{phase}_start",
                                               [_grad_blocks(grads[nm], nm) for nm in _PHASES[phase]], gather=False)
        return [pending_g[phase][4]]

    dx, gsmall = _local_step(_residue_order(x[0]), _residue_order(loss_target[0]), small_in, weights_of, send_grads,
                             first_deps=[pending_w["f2"][4]])
    dx = _token_order(dx)

    updated = {}
    after = pending_g["f1d"][4]
    for phase in ("f2", "mix", "small", "f1gu", "f1d"):
        landed = _exchange_wait(f"exchange_{phase}_wait", pending_g[phase], after, gather=phase == "small")
        if phase == "small":
            sm = _adamw("adamw_small", _pack_small(small), _pack_small({nm: m[nm] for nm in _SMALL}),
                        _pack_small({nm: v[nm] for nm in _SMALL}), landed[0], landed[0].shape[1])
            after = sm[0]
            continue
        for nm, recv in zip(_PHASES[phase], landed):
            tr = max(t for t in range(16, 353, 16) if w_rows[nm].shape[0] % t == 0)
            updated[nm] = _adamw(f"adamw_{nm}", w_rows[nm], m_rows[nm], v_rows[nm], recv, tr)
            after = updated[nm][0]

    loss = sm[0][-8, 0]
    outs = []
    for i in range(4):
        sml = _unpack_small(sm[i], small)
        outs.append([(_from_rows(updated[nm][i], nm) if nm in updated else sml[nm]).reshape(shapes[nm])
                     for nm in _ORDER])
    return (loss, dx[None], *outs[0], *outs[1], *outs[2], *outs[3])
```

```python
import math

import numpy as np
import jax
import jax.numpy as jnp
from jax import lax
from jax.experimental import pallas as pl
from jax.experimental.pallas import tpu as pltpu

F32 = jnp.float32
BF16 = jnp.bfloat16

N_DEV = 8
D_MODEL = 1024
HEAD_DIM = 64
HEADS_PER_GROUP = 4
DILATIONS = (1, 4, 16)
N_GROUPS = 3
ATTN_WIDTH = 768
ATTN_OUT = 256
BLOCK = 128
N_BUCKETS = 32
MAX_DISTANCE = 2048
NEG_INF = -1e30
SSM_WIDTH = 512
SSM_GROUPS = 32
SSM_GROUP = 16
SSM_STATE = 64
SSM_LANES = SSM_GROUPS * SSM_STATE
SSM_PAIRS = SSM_GROUPS // 2
PAIR_LANES = 2 * SSM_STATE
PAIR_TILE = 256
EPS = 1e-6
LR, B1, B2, ADAM_EPS, WD, STEP = 0.001, 0.9, 0.999, 1e-08, 0.01, 10

VMEM_LIMIT_BYTES = 56 * 1024 * 1024
FFN_CHUNK = 768
SCAN_BLOCK = 256
SCAN_STEPS = 16
SCAN_COLS = SCAN_BLOCK // SCAN_STEPS
SCAN_SUB = 8
SCAN_LANES = 512

MESH = pl.DeviceIdType.MESH


def _cparams(*sem):
    return pltpu.CompilerParams(dimension_semantics=sem, vmem_limit_bytes=VMEM_LIMIT_BYTES)


def _dot(a, b, dims):
    return lax.dot_general(a, b, (dims, ((), ())), preferred_element_type=F32)


def _dot_nn(a, b):
    return _dot(a, b, ((1,), (0,)))


def _dot_nt(a, b):
    return _dot(a, b, ((1,), (1,)))


def _dot_tn(a, b):
    return _dot(a, b, ((0,), (0,)))


def _sigmoid(x):
    return 1.0 / (1.0 + jnp.exp(-x))


_HBM_SPEC = pl.BlockSpec(memory_space=pltpu.HBM)
_SEM_SPEC = pl.BlockSpec(memory_space=pltpu.SEMAPHORE)
_ANY_SPEC = pl.BlockSpec(memory_space=pl.ANY)
_EFFECT = pltpu.SideEffectType.DATAFLOW_SIDE_EFFECTING


def _peers(x, y, c):
    return [(1 - x if k & 4 else x, 1 - y if k & 2 else y, 1 - c if k & 1 else c) for k in range(1, N_DEV)]


def _exchange_copies(x_refs, land_refs, send_sems, recv_sems, gather):
    x, y, c = lax.axis_index("x"), lax.axis_index("y"), lax.axis_index("c")
    me = 4 * x + 2 * y + c
    copies = []
    for a, (x_ref, land_ref) in enumerate(zip(x_refs, land_refs)):
        for k, (px, py, pc) in enumerate(_peers(x, y, c)):
            src = x_ref if gather else x_ref.at[4 * px + 2 * py + pc]
            copies.append(pltpu.make_async_remote_copy(
                src_ref=src, dst_ref=land_ref.at[me], send_sem=send_sems.at[N_DEV * a + k],
                recv_sem=recv_sems.at[(N_DEV - 1) * a + k], device_id=(px, py, pc), device_id_type=MESH))
    owns = [pltpu.make_async_copy(x_ref if gather else x_ref.at[me], land_ref.at[me],
                                  send_sems.at[N_DEV * a + N_DEV - 1])
            for a, (x_ref, land_ref) in enumerate(zip(x_refs, land_refs))]
    return owns, copies


def _exchange_start(name, xs_list, gather, deps=()):
    n, nd = len(xs_list), len(deps)
    land_shapes = [(N_DEV, *xs.shape) if gather else xs.shape for xs in xs_list]

    def body(*refs):
        x_refs, land_refs = refs[:n], refs[n:2 * n]
        send_sems, recv_sems = refs[2 * n + nd:2 * n + nd + 2]
        token = refs[-1]
        owns, copies = _exchange_copies(x_refs, land_refs, send_sems, recv_sems, gather)
        for cp in copies + owns:
            cp.start()
        token[...] = jnp.zeros_like(token)

    hbm = lambda a: pltpu.with_memory_space_constraint(a, pltpu.HBM)
    outs = pl.pallas_call(
        body, name=name,
        out_shape=(pltpu.SemaphoreType.DMA((n * N_DEV,)), pltpu.SemaphoreType.DMA((n * (N_DEV - 1),)),
                   *[pltpu.HBM(xs.shape, xs.dtype) for xs in xs_list],
                   *[pltpu.HBM(shape, xs.dtype) for shape, xs in zip(land_shapes, xs_list)],
                   jax.ShapeDtypeStruct((8, 128), F32)),
        in_specs=(_HBM_SPEC,) * (2 * n) + (_ANY_SPEC,) * nd,
        out_specs=(_SEM_SPEC, _SEM_SPEC) + (_HBM_SPEC,) * (2 * n) + (pl.BlockSpec(memory_space=pltpu.VMEM),),
        input_output_aliases={i: 2 + i for i in range(2 * n)},
        compiler_params=pltpu.CompilerParams(has_side_effects=_EFFECT),
    )(*[hbm(xs) for xs in xs_list], *[hbm(lax.empty(shape, xs.dtype)) for shape, xs in zip(land_shapes, xs_list)],
      *deps)
    return outs[0], outs[1], list(outs[2:2 + n]), list(outs[2 + n:2 + 2 * n]), outs[-1]


def _exchange_wait(name, handle, after, gather):
    send_sems, recv_sems, xs_thru, lands_thru, _ = handle
    n = len(xs_thru)
    after = list(after) if isinstance(after, (list, tuple)) else [after]

    def body(*refs):
        x_refs, land_refs = refs[:n], refs[n:2 * n]
        send_sems, recv_sems = refs[2 * n:2 * n + 2]
        owns, copies = _exchange_copies(x_refs, land_refs, send_sems, recv_sems, gather)
        for cp in copies:
            cp.wait_send()
            cp.wait_recv()
        for cp in owns:
            cp.wait()

    outs = pl.pallas_call(
        body, name=name,
        out_shape=tuple(pltpu.HBM(a.shape, a.dtype) for a in xs_thru + lands_thru),
        in_specs=(_HBM_SPEC,) * (2 * n) + (_SEM_SPEC, _SEM_SPEC) + (_ANY_SPEC,) * len(after),
        out_specs=(_HBM_SPEC,) * (2 * n), input_output_aliases={i: i for i in range(2 * n)},
        compiler_params=pltpu.CompilerParams(has_side_effects=_EFFECT),
    )(*xs_thru, *lands_thru, send_sems, recv_sems, *after)
    return list(outs[n:])


def _mm(name, pairs, nt, n_cols, out_dtypes, epilogue=None, extras=(), tm=1024, tn=512, deps=(), row_sums=0,
        out_cols=None):
    rows = pairs[0][0].shape[0]
    tm = min(tm, rows)
    tn = min(tn, n_cols)
    na, ne, nd, no = len(pairs), len(extras), len(deps), len(out_dtypes)

    def body(*refs):
        a_refs, w_refs = refs[:na], refs[na:2 * na]
        e_refs, o_refs = refs[2 * na:2 * na + ne], refs[2 * na + ne + nd:]
        acc = None
        for a_ref, w_ref in zip(a_refs, w_refs):
            a = a_ref[...].astype(BF16)
            w = w_ref[...].astype(BF16)
            p = _dot_nt(a, w) if nt else _dot_nn(a, w)
            acc = p if acc is None else acc + p
        outs = (acc,) if epilogue is None else epilogue(acc, *[e[...].astype(F32) for e in e_refs])
        for o_ref, o in zip(o_refs[:no], outs[:no]):
            o_ref[...] = o.astype(o_ref.dtype)
        for r_ref, o in zip(o_refs[no:], outs[no:]):
            @pl.when(pl.program_id(0) == 0)
            def _():
                r_ref[...] = jnp.zeros_like(r_ref)

            r_ref[...] += o

    in_specs = [pl.BlockSpec((tm, a.shape[1]), lambda i, j: (i, 0)) for a, _ in pairs]
    for _, w in pairs:
        if nt:
            in_specs.append(pl.BlockSpec((tn, w.shape[1]), lambda i, j: (j, 0)))
        else:
            in_specs.append(pl.BlockSpec((w.shape[0], tn), lambda i, j: (0, j)))
    for e, col_off in extras:
        off = col_off // tn
        if e.shape[0] == 1:
            in_specs.append(pl.BlockSpec((1, tn), lambda i, j, off=off: (0, j + off)))
        else:
            in_specs.append(pl.BlockSpec((tm, tn), lambda i, j, off=off: (i, j + off)))
    in_specs += [_ANY_SPEC] * nd
    if out_cols is None:
        out_cols = [n_cols] * no
    else:
        assert tn == n_cols, "outputs of other widths need the whole row in one block"
    assert not row_sums or tn == n_cols
    out_specs = [pl.BlockSpec((tm, tn * c // n_cols), lambda i, j: (i, j)) for c in out_cols]
    out_specs += [pl.BlockSpec((1, tn), lambda i, j: (0, j))] * row_sums
    out_shape = [jax.ShapeDtypeStruct((rows, c), dt) for c, dt in zip(out_cols, out_dtypes)]
    out_shape += [jax.ShapeDtypeStruct((1, n_cols), F32)] * row_sums
    outs = pl.pallas_call(
        body, name=name, grid=(rows // tm, n_cols // tn),
        in_specs=in_specs, out_specs=out_specs, out_shape=out_shape,
        compiler_params=_cparams("arbitrary" if row_sums else "parallel", "arbitrary"),
    )(*[a for a, _ in pairs], *[w for _, w in pairs], *[e for e, _ in extras], *deps)
    return outs


def _tn_rows(m):
    return max(b for b in range(128, min(m, 1408) + 1, 128) if m % b == 0)


def _mm_tn(name, a, b, scale=1.0, bm=None, tk=1024, deps=(), out_dtype=F32):
    rows, m = a.shape
    n = b.shape[1]
    bm = _tn_rows(m) if bm is None else bm
    tk = min(tk, rows)
    nk = rows // tk

    def body(a_ref, b_ref, *rest):
        o_ref, acc_ref = rest[-2:]
        k = pl.program_id(1)

        @pl.when(k == 0)
        def _():
            acc_ref[...] = jnp.zeros_like(acc_ref)

        acc_ref[...] += _dot_tn(a_ref[...].astype(BF16), b_ref[...].astype(BF16))

        @pl.when(k == nk - 1)
        def _():
            o_ref[...] = (acc_ref[...] * scale).astype(o_ref.dtype)

    return pl.pallas_call(
        body, name=name, grid=(m // bm, nk),
        in_specs=[pl.BlockSpec((tk, bm), lambda i, k: (k, i)), pl.BlockSpec((tk, n), lambda i, k: (k, 0))]
        + [_ANY_SPEC] * len(deps),
        out_specs=pl.BlockSpec((bm, n), lambda i, k: (i, 0)),
        out_shape=jax.ShapeDtypeStruct((m, n), out_dtype),
        scratch_shapes=[pltpu.VMEM((bm, n), F32)],
        compiler_params=_cparams("parallel", "arbitrary"),
    )(a, b, *deps)


def _mm_tn_stack(name, a_list, b, tk=1024, out_dtype=F32):
    rows, n = b.shape
    ms = [a.shape[1] for a in a_list]
    tk = min(tk, rows)
    nk = rows // tk
    na = len(a_list)

    def body(*refs):
        a_refs, b_ref, o_ref, acc_ref = refs[:na], refs[na], refs[na + 1], refs[na + 2]
        k = pl.program_id(0)

        @pl.when(k == 0)
        def _():
            acc_ref[...] = jnp.zeros_like(acc_ref)

        bv = b_ref[...].astype(BF16)
        r0 = 0
        for a_ref, m in zip(a_refs, ms):
            acc_ref[r0:r0 + m, :] += _dot_tn(a_ref[...].astype(BF16), bv)
            r0 += m

        @pl.when(k == nk - 1)
        def _():
            o_ref[...] = acc_ref[...].astype(o_ref.dtype)

    return pl.pallas_call(
        body, name=name, grid=(nk,),
        in_specs=[pl.BlockSpec((tk, m), lambda k: (k, 0)) for m in ms] + [pl.BlockSpec((tk, n), lambda k: (k, 0))],
        out_specs=pl.BlockSpec((sum(ms), n), lambda k: (0, 0)),
        out_shape=jax.ShapeDtypeStruct((sum(ms), n), out_dtype),
        scratch_shapes=[pltpu.VMEM((sum(ms), n), F32)],
        compiler_params=_cparams("arbitrary"),
    )(*a_list, b)


def _colsum(name, xs, tm=512):
    rows, cols = xs.shape
    tm = min(tm, rows)

    def body(x_ref, o_ref):
        @pl.when(pl.program_id(0) == 0)
        def _():
            o_ref[...] = jnp.zeros_like(o_ref)

        o_ref[...] += jnp.sum(x_ref[...].astype(F32), axis=0, keepdims=True)

    return pl.pallas_call(
        body, name=name, grid=(rows // tm,),
        in_specs=[pl.BlockSpec((tm, cols), lambda i: (i, 0))],
        out_specs=pl.BlockSpec((1, cols), lambda i: (0, 0)),
        out_shape=jax.ShapeDtypeStruct((1, cols), F32),
        compiler_params=_cparams("arbitrary"),
    )(xs)


def _ew(name, fn, ins, out_cols, out_dtypes, tm=512):
    rows = ins[0].shape[0]
    tm = min(tm, rows)
    ni = len(ins)

    def body(*refs):
        outs = fn(*[r[...] for r in refs[:ni]])
        for o_ref, o in zip(refs[ni:], outs):
            o_ref[...] = o.astype(o_ref.dtype)

    def spec(shape):
        if shape[0] == 1:
            return pl.BlockSpec((1, shape[1]), lambda i: (0, 0))
        return pl.BlockSpec((tm, shape[1]), lambda i: (i, 0))

    return pl.pallas_call(
        body, name=name, grid=(rows // tm,),
        in_specs=[spec(a.shape) for a in ins],
        out_specs=[pl.BlockSpec((tm, c), lambda i: (i, 0)) for c in out_cols],
        out_shape=[jax.ShapeDtypeStruct((rows, c), dt) for c, dt in zip(out_cols, out_dtypes)],
        compiler_params=_cparams("parallel"),
    )(*ins)


def _rms_parts(xv):
    r = lax.rsqrt(jnp.mean(xv * xv, axis=-1, keepdims=True) + EPS)
    return r, xv * r


def _rms_bwd_dx(dh, gain, r, xh):
    dxh = dh * gain
    return r * (dxh - xh * jnp.mean(dxh * xh, axis=-1, keepdims=True))


def _ffn_chunks(f_all):
    return [slice(c, min(c + FFN_CHUNK, f_all)) for c in range(0, f_all, FFN_CHUNK)]


def _loss_head(xo, gain_f, target, d):
    r, xh = _rms_parts(xo)
    err = xh * gain_f - target
    dy = err * (1.0 / d)
    per_tok = jnp.mean(err * err, axis=-1, keepdims=True)
    return (_rms_bwd_dx(dy, gain_f, r, xh), jnp.sum(dy * xh, axis=0, keepdims=True),
            0.5 * jnp.sum(per_tok, axis=0, keepdims=True))


def _ffn_tile(x_ref, g_ref, wg_ref, wu_ref, wd_ref, h_ref, gg_ref, uu_ref):
    xv = x_ref[...]
    _, xh = _rms_parts(xv)
    h = (xh * g_ref[...]).astype(BF16)
    h_ref[...] = h
    acc = None
    for cols in _ffn_chunks(wd_ref.shape[0]):
        gg = _dot_nt(h, wg_ref[cols, :])
        uu = _dot_nt(h, wu_ref[cols, :])
        act = gg * _sigmoid(gg) * uu
        part = _dot_nn(act.astype(BF16), wd_ref[cols, :])
        acc = part if acc is None else acc + part
        gg_ref[:, cols] = gg.astype(BF16)
        uu_ref[:, cols] = uu.astype(BF16)
    return xv + 0.5 * acc


def _ffn_fwd(name, xs, gain, wg_t, wu_t, wd, next_gain, tm=512, deps=()):
    rows, d = xs.shape
    f_all = wd.shape[0]
    tm = min(tm, rows)

    def body(x_ref, g_ref, wg_ref, wu_ref, wd_ref, ng_ref, *rest):
        xo_ref, h_ref, gg_ref, uu_ref, hn_ref = rest[-5:]
        xo = _ffn_tile(x_ref, g_ref, wg_ref, wu_ref, wd_ref, h_ref, gg_ref, uu_ref)
        xo_ref[...] = xo
        hn_ref[...] = (_rms_parts(xo)[1] * ng_ref[...]).astype(BF16)

    tile = pl.BlockSpec((tm, d), lambda i: (i, 0))
    row = pl.BlockSpec((1, d), lambda i: (0, 0))
    wspec = pl.BlockSpec((f_all, d), lambda i: (0, 0), pipeline_mode=pl.Buffered(1))
    hid = pl.BlockSpec((tm, f_all), lambda i: (i, 0))
    return pl.pallas_call(
        body, name=name, grid=(rows // tm,),
        in_specs=[tile, row, wspec, wspec, wspec, row] + [_ANY_SPEC] * len(deps),
        out_specs=[tile, tile, hid, hid, tile],
        out_shape=[jax.ShapeDtypeStruct((rows, d), F32), jax.ShapeDtypeStruct((rows, d), BF16),
                   jax.ShapeDtypeStruct((rows, f_all), BF16), jax.ShapeDtypeStruct((rows, f_all), BF16),
                   jax.ShapeDtypeStruct((rows, d), BF16)],
        compiler_params=_cparams("parallel"),
    )(xs, gain, wg_t, wu_t, wd, next_gain, *deps)


def _ffn_fwd_head(name, xs, gain, wg_t, wu_t, wd, gain_f, target, tm=512):
    rows, d = xs.shape
    f_all = wd.shape[0]
    tm = min(tm, rows)

    def body(x_ref, g_ref, wg_ref, wu_ref, wd_ref, gf_ref, t_ref, dxo_ref, h_ref, gg_ref, uu_ref, dgf_ref, loss_ref):
        xo = _ffn_tile(x_ref, g_ref, wg_ref, wu_ref, wd_ref, h_ref, gg_ref, uu_ref)
        dxo, dgf, loss = _loss_head(xo, gf_ref[...], t_ref[...], d)
        dxo_ref[...] = dxo

        @pl.when(pl.program_id(0) == 0)
        def _():
            dgf_ref[...] = jnp.zeros_like(dgf_ref)
            loss_ref[...] = jnp.zeros_like(loss_ref)

        dgf_ref[...] += dgf
        loss_ref[...] += loss

    tile = pl.BlockSpec((tm, d), lambda i: (i, 0))
    row = pl.BlockSpec((1, d), lambda i: (0, 0))
    wspec = pl.BlockSpec((f_all, d), lambda i: (0, 0), pipeline_mode=pl.Buffered(1))
    hid = pl.BlockSpec((tm, f_all), lambda i: (i, 0))
    return pl.pallas_call(
        body, name=name, grid=(rows // tm,),
        in_specs=[tile, row, wspec, wspec, wspec, row, tile],
        out_specs=[tile, tile, hid, hid, row, pl.BlockSpec((1, 1), lambda i: (0, 0))],
        out_shape=[jax.ShapeDtypeStruct((rows, d), F32), jax.ShapeDtypeStruct((rows, d), BF16),
                   jax.ShapeDtypeStruct((rows, f_all), BF16), jax.ShapeDtypeStruct((rows, f_all), BF16),
                   jax.ShapeDtypeStruct((1, d), F32), jax.ShapeDtypeStruct((1, 1), F32)],
        compiler_params=_cparams("arbitrary"),
    )(xs, gain, wg_t, wu_t, wd, gain_f, target)


def _ffn_bwd(name, dxo, xs, gain, gg_all, uu_all, wg_t, wu_t, wd, tm=256):
    rows, d = xs.shape
    f_all = wd.shape[0]
    tm = min(tm, rows)

    def body(dxo_ref, x_ref, g_ref, gg_ref, uu_ref, wg_ref, wu_ref, wd_ref,
             dx_ref, dgg_ref, duu_ref, act_ref, dgain_ref):
        dxo = dxo_ref[...]
        df = (0.5 * dxo).astype(BF16)
        dh = None
        for cols in _ffn_chunks(f_all):
            gg = gg_ref[:, cols].astype(F32)
            uu = uu_ref[:, cols].astype(F32)
            sg = _sigmoid(gg)
            silu = gg * sg
            dact = _dot_nt(df, wd_ref[cols, :])
            duu = (dact * silu).astype(BF16)
            dgg = (dact * uu * (sg * (1.0 + gg * (1.0 - sg)))).astype(BF16)
            act_ref[:, cols] = (silu * uu).astype(BF16)
            dgg_ref[:, cols] = dgg
            duu_ref[:, cols] = duu
            part = _dot_nn(dgg, wg_ref[cols, :]) + _dot_nn(duu, wu_ref[cols, :])
            dh = part if dh is None else dh + part
        r, xh = _rms_parts(x_ref[...])
        dx_ref[...] = dxo + _rms_bwd_dx(dh, g_ref[...], r, xh)

        @pl.when(pl.program_id(0) == 0)
        def _():
            dgain_ref[...] = jnp.zeros_like(dgain_ref)

        dgain_ref[...] += jnp.sum(dh * xh, axis=0, keepdims=True)

    tile = pl.BlockSpec((tm, d), lambda i: (i, 0))
    row = pl.BlockSpec((1, d), lambda i: (0, 0))
    wspec = pl.BlockSpec((f_all, d), lambda i: (0, 0), pipeline_mode=pl.Buffered(1))
    hid = pl.BlockSpec((tm, f_all), lambda i: (i, 0))
    hid_shape = jax.ShapeDtypeStruct((rows, f_all), BF16)
    return pl.pallas_call(
        body, name=name, grid=(rows // tm,),
        in_specs=[tile, tile, row, hid, hid, wspec, wspec, wspec],
        out_specs=[tile, hid, hid, hid, row],
        out_shape=[jax.ShapeDtypeStruct((rows, d), F32), hid_shape, hid_shape, hid_shape,
                   jax.ShapeDtypeStruct((1, d), F32)],
        compiler_params=_cparams("arbitrary"),
    )(dxo, xs, gain, gg_all, uu_all, wg_t, wu_t, wd)


def _t5_bucket_np(dist):
    max_exact = N_BUCKETS // 2
    dd = np.maximum(dist, 1).astype(np.float32)
    large = max_exact + (np.log(dd / np.float32(max_exact)) / np.float32(math.log(MAX_DISTANCE / max_exact))
                         * np.float32(N_BUCKETS - max_exact)).astype(np.int32)
    large = np.minimum(large, N_BUCKETS - 1)
    return np.where(dist < max_exact, dist, large).astype(np.int32)


def _attn_geometry(g, rows):
    run = rows // 16
    dil = DILATIONS[g]
    if dil == 16:
        bq = BLOCK
        return dict(view=(16, run), block=(None, bq), grid=(16, run // bq), index=lambda r, n: (r, n),
                    pos=np.arange(bq), bq=bq)
    if dil == 4:
        per = BLOCK // 4
        pos = (4 * np.arange(per)[None, :] + np.arange(4)[:, None]).reshape(-1)
        return dict(view=(4, 4, run), block=(4, None, per), grid=(4, run // per), index=lambda r, n: (0, r, n),
                    pos=pos, bq=BLOCK)
    per = 16
    pos = (16 * np.arange(per)[None, :] + np.arange(16)[:, None]).reshape(-1)
    return dict(view=(16, run), block=(16, per), grid=(1, run // per), index=lambda r, n: (0, n),
                pos=pos, bq=16 * per)


def _attn_tables(g, rows):
    geo = _attn_geometry(g, rows)
    pos, bq = geo["pos"], geo["bq"]
    steps = pos[:, None] - np.concatenate([pos - bq, pos])[None, :]
    valid = (steps >= 0) & (steps <= BLOCK)
    bucket = _t5_bucket_np((np.maximum(steps, 0) * DILATIONS[g]).astype(np.int32))
    return bucket, valid.astype(np.int32)


def _bias_fwd(name, bucket, valid, table_t):
    bq = bucket.shape[0]

    def body(bk_ref, ok_ref, tab_ref, o_ref):
        bk = bk_ref[...]
        ok = ok_ref[...] > 0
        accs = [jnp.zeros(bk.shape, F32)] * HEADS_PER_GROUP
        for b in range(N_BUCKETS):
            hit = bk == b
            accs = [jnp.where(hit, tab_ref[h, b], acc) for h, acc in enumerate(accs)]
        for h, acc in enumerate(accs):
            o_ref[h] = jnp.where(ok, acc, NEG_INF)

    vm = pl.BlockSpec(memory_space=pltpu.VMEM)
    return pl.pallas_call(
        body, name=name, in_specs=[vm, vm, pl.BlockSpec(memory_space=pltpu.SMEM)], out_specs=vm,
        out_shape=jax.ShapeDtypeStruct((HEADS_PER_GROUP, bq, 2 * bq), F32),
    )(bucket, valid, table_t)


def _bias_bwd(name, bucket, dbias):
    def body(bk_ref, db_ref, o_ref):
        row_id = lax.broadcasted_iota(jnp.int32, (N_BUCKETS, 128), 0)
        col_id = lax.broadcasted_iota(jnp.int32, (N_BUCKETS, 128), 1)
        bk = bk_ref[...]
        acc = jnp.zeros((N_BUCKETS, 128), F32)
        for h in range(HEADS_PER_GROUP):
            db = db_ref[h]
            for b in range(N_BUCKETS):
                part = jnp.sum(jnp.where(bk == b, db, 0.0), axis=0, keepdims=True)
                tot = jnp.sum(part, axis=1, keepdims=True)
                acc = jnp.where((row_id == b) & (col_id == h), tot, acc)
        o_ref[...] = acc

    vm = pl.BlockSpec(memory_space=pltpu.VMEM)
    return pl.pallas_call(body, name=name, in_specs=[vm, vm], out_specs=vm,
                          out_shape=jax.ShapeDtypeStruct((N_BUCKETS, 128), F32))(bucket, dbias)


def _head_of_lane(nrows):
    return lax.broadcasted_iota(jnp.int32, (nrows, ATTN_OUT), 1) // HEAD_DIM


def _stack_heads(a, lane_head):
    zero = jnp.zeros_like(a)
    return jnp.concatenate([jnp.where(lane_head == h, a, zero) for h in range(HEADS_PER_GROUP)], axis=0)


def _unstack_heads(a4, lane_head, bq):
    out = a4[:bq]
    for h in range(1, HEADS_PER_GROUP):
        out = jnp.where(lane_head == h, a4[h * bq:(h + 1) * bq], out)
    return out


def _attn_specs(geo, cols, col_block, index):
    return pl.BlockSpec(geo["block"] + (cols,), lambda r, n: index(r, n) + (col_block,))


def _attn_fwd(name, qkv, g, bias4):
    rows = qkv.shape[0]
    geo = _attn_geometry(g, rows)
    bq, (nsub, nb), index = geo["bq"], geo["grid"], geo["index"]
    blk_shape = tuple(b for b in geo["block"] if b is not None) + (ATTN_OUT,)

    def body(q_ref, kc_ref, kp_ref, vc_ref, vp_ref, b_ref, o_ref, lse_ref):
        n = pl.program_id(1)
        lane_head = _head_of_lane(bq)
        flat = lambda ref: ref[...].reshape(bq, ATTN_OUT)
        q4 = _stack_heads(flat(q_ref), lane_head)
        k2 = jnp.concatenate([flat(kp_ref), flat(kc_ref)], axis=0)
        v2 = jnp.concatenate([flat(vp_ref), flat(vc_ref)], axis=0)
        s = _dot_nt(q4, k2) + b_ref[...]
        col = lax.broadcasted_iota(jnp.int32, s.shape, 1)
        s = jnp.where((col >= bq) | (n > 0), s, NEG_INF)
        mx = jnp.max(s, axis=-1, keepdims=True)
        p = jnp.exp(s - mx)
        den = jnp.sum(p, axis=-1, keepdims=True)
        o4 = _dot_nn(p.astype(BF16), v2) / den
        lse4 = jnp.broadcast_to(mx + jnp.log(den), (HEADS_PER_GROUP * bq, ATTN_OUT))
        o_ref[...] = _unstack_heads(o4, lane_head, bq).reshape(blk_shape)
        lse_ref[...] = _unstack_heads(lse4, lane_head, bq).reshape(blk_shape)

    prev = lambda r, n: index(r, jnp.maximum(n - 1, 0))
    view = lambda a: a.reshape(geo["view"] + (a.shape[1],))
    qkv_v = view(qkv)
    out_spec = _attn_specs(geo, ATTN_OUT, 0, index)
    out_shape = jax.ShapeDtypeStruct(geo["view"] + (ATTN_OUT,), F32)
    o, lse = pl.pallas_call(
        body, name=name, grid=(nsub, nb),
        in_specs=[_attn_specs(geo, ATTN_OUT, g, index), _attn_specs(geo, ATTN_OUT, 3 + g, index),
                  _attn_specs(geo, ATTN_OUT, 3 + g, prev), _attn_specs(geo, ATTN_OUT, 6 + g, index),
                  _attn_specs(geo, ATTN_OUT, 6 + g, prev), pl.BlockSpec(bias4.shape, lambda r, n: (0, 0))],
        out_specs=[out_spec, out_spec], out_shape=[out_shape, out_shape],
        compiler_params=_cparams("parallel", "arbitrary"),
    )(qkv_v, qkv_v, qkv_v, qkv_v, qkv_v, bias4)
    return o.reshape(rows, ATTN_OUT), lse.reshape(rows, ATTN_OUT)


def _attn_bwd(name, qkv, do, lse, cvec, g, bias4):
    rows = qkv.shape[0]
    geo = _attn_geometry(g, rows)
    bq, (nsub, nb), index = geo["bq"], geo["grid"], geo["index"]
    blk_shape = tuple(b for b in geo["block"] if b is not None) + (ATTN_OUT,)
    nlead = len(blk_shape) - 1

    def body(q_ref, kc_ref, kp_ref, vc_ref, vp_ref, do_ref, lse_ref, c_ref, b_ref,
             dq_ref, dk_ref, dv_ref, db_ref, kcar_ref, vcar_ref):
        r, n = pl.program_id(0), pl.program_id(1)
        valid = n < nb
        lane_head = _head_of_lane(bq)
        flat = lambda ref: ref[...].reshape(bq, ATTN_OUT)

        @pl.when((r == 0) & (n == 0))
        def _():
            kcar_ref[...] = jnp.zeros_like(kcar_ref)
            vcar_ref[...] = jnp.zeros_like(vcar_ref)
            db_ref[...] = jnp.zeros_like(db_ref)

        def column(ref, h):
            lead = (slice(None),) * nlead
            return ref[lead + (pl.ds(h * HEAD_DIM, 1),)].reshape(bq, 1)

        q4 = _stack_heads(flat(q_ref), lane_head)
        do4 = _stack_heads(flat(do_ref), lane_head)
        k2 = jnp.concatenate([flat(kp_ref), flat(kc_ref)], axis=0)
        v2 = jnp.concatenate([flat(vp_ref), flat(vc_ref)], axis=0)
        lse4 = jnp.concatenate([column(lse_ref, h) for h in range(HEADS_PER_GROUP)], axis=0)
        c4 = jnp.concatenate([column(c_ref, h) for h in range(HEADS_PER_GROUP)], axis=0)
        s = _dot_nt(q4, k2) + b_ref[...]
        col = lax.broadcasted_iota(jnp.int32, s.shape, 1)
        keep = ((col >= bq) | (n > 0)) & valid
        p = jnp.where(keep, jnp.exp(s - lse4), 0.0)
        ds = p * (_dot_nt(do4, v2) + c4)
        ds_b = ds.astype(BF16)

        @pl.when(valid)
        def _():
            dq = _unstack_heads(_dot_nn(ds_b, k2), lane_head, bq) * (HEAD_DIM ** -0.5)
            dq_ref[...] = dq.astype(BF16).reshape(blk_shape)

        dk2 = _dot_tn(ds_b, q4)
        dv2 = _dot_tn(p.astype(BF16), do4)
        dk_ref[...] = (kcar_ref[...] + dk2[:bq]).astype(BF16).reshape(blk_shape)
        dv_ref[...] = (vcar_ref[...] + dv2[:bq]).astype(BF16).reshape(blk_shape)
        kcar_ref[...] = dk2[bq:]
        vcar_ref[...] = dv2[bq:]
        db_ref[...] += ds

    cur = lambda r, n: index(r, jnp.minimum(n, nb - 1))
    prev = lambda r, n: index(r, jnp.maximum(jnp.minimum(n, nb - 1) - 1, 0))
    late = lambda r, n: index(r, jnp.maximum(n - 1, 0))
    view = lambda a: a.reshape(geo["view"] + (a.shape[1],))
    qkv_v = view(qkv)
    tile = _attn_specs(geo, ATTN_OUT, 0, cur)
    bias_spec = pl.BlockSpec(bias4.shape, lambda r, n: (0, 0))
    out_shape = jax.ShapeDtypeStruct(geo["view"] + (ATTN_OUT,), BF16)
    dq, dk, dv, db = pl.pallas_call(
        body, name=name, grid=(nsub, nb + 1),
        in_specs=[_attn_specs(geo, ATTN_OUT, g, cur), _attn_specs(geo, ATTN_OUT, 3 + g, cur),
                  _attn_specs(geo, ATTN_OUT, 3 + g, prev), _attn_specs(geo, ATTN_OUT, 6 + g, cur),
                  _attn_specs(geo, ATTN_OUT, 6 + g, prev), tile, tile, tile, bias_spec],
        out_specs=[tile, _attn_specs(geo, ATTN_OUT, 0, late), _attn_specs(geo, ATTN_OUT, 0, late), bias_spec],
        out_shape=[out_shape, out_shape, out_shape, jax.ShapeDtypeStruct(bias4.shape, F32)],
        scratch_shapes=[pltpu.VMEM((bq, ATTN_OUT), F32), pltpu.VMEM((bq, ATTN_OUT), F32)],
        compiler_params=_cparams("arbitrary", "arbitrary"),
    )(qkv_v, qkv_v, qkv_v, qkv_v, qkv_v, view(do), view(lse), view(cvec), bias4)
    return dq.reshape(rows, ATTN_OUT), dk.reshape(rows, ATTN_OUT), dv.reshape(rows, ATTN_OUT), db


def _group_weights(lses):
    mx = jnp.maximum(jnp.maximum(lses[0], lses[1]), lses[2])
    es = [jnp.exp(l - mx) for l in lses]
    den = es[0] + es[1] + es[2]
    return [e / den for e in es]


def _combine_fwd(name, os_, lses):
    def fn(o0, o1, o2, l0, l1, l2):
        ws = _group_weights([l0, l1, l2])
        out = ws[0] * o0 + ws[1] * o1 + ws[2] * o2
        return out, out

    return _ew(name, fn, [*os_, *lses], [ATTN_OUT, ATTN_OUT], [F32, BF16], tm=1024)


def _combine_bwd(name, do, oa, lses):
    def fn(dov, oav, l0, l1, l2):
        head_sum = (lax.broadcasted_iota(jnp.int32, (ATTN_OUT, ATTN_OUT), 0) // HEAD_DIM
                    == lax.broadcasted_iota(jnp.int32, (ATTN_OUT, ATTN_OUT), 1) // HEAD_DIM)
        ws = _group_weights([l0, l1, l2])
        prod = dov * oav
        hi = prod.astype(BF16)
        lo = (prod - hi.astype(F32)).astype(BF16)
        ones = jnp.where(head_sum, 1.0, 0.0).astype(BF16)
        bar = _dot_nn(hi, ones) + _dot_nn(lo, ones)
        return tuple(w * dov for w in ws) + tuple(-w * bar for w in ws)

    return _ew(name, fn, [do, oa, *lses], [ATTN_OUT] * 6, [BF16] * 3 + [F32] * 3, tm=1024)


def _ssm_disc(a_re, a_im, log_dt, b_re, b_im):
    dt = jnp.exp(log_dt)
    mag = jnp.exp(a_re * dt)
    ab_re = mag * jnp.cos(a_im * dt)
    ab_im = mag * jnp.sin(a_im * dt)
    den = a_re * a_re + a_im * a_im
    xr = ab_re - 1.0
    coef_re = (xr * a_re + ab_im * a_im) / den
    coef_im = (ab_im * a_re - xr * a_im) / den
    bb_re = coef_re[None] * b_re - coef_im[None] * b_im
    bb_im = coef_re[None] * b_im + coef_im[None] * b_re
    return ab_re, ab_im, bb_re, bb_im


def _ssm_params_fwd(name, a_re, a_im, log_dt, b_re, b_im, c_re, c_im):
    pows = jax.ShapeDtypeStruct((SCAN_STEPS,) + a_re.shape, F32)
    mats = jax.ShapeDtypeStruct((SSM_PAIRS, PAIR_TILE, PAIR_TILE), BF16)
    per_tile = PAIR_TILE // (2 * SSM_GROUP)

    def body(ar, ai, ld, br, bi, cr, ci, o_pr, o_pi, o_bb, o_c, bbr_ref, bbi_ref, wide_ref):
        ab_re, ab_im, bb_re, bb_im = _ssm_disc(ar[...], ai[...], ld[...], br[...], bi[...])
        pr, pi = ab_re, ab_im
        for j in range(SCAN_STEPS):
            o_pr[j] = pr
            o_pi[j] = pi
            pr, pi = pr * ab_re - pi * ab_im, pr * ab_im + pi * ab_re
        bbr_ref[...] = bb_re
        bbi_ref[...] = bb_im

        def place(out_ref, block):
            wide_ref[...] = jnp.zeros_like(wide_ref)
            for g in range(SSM_GROUPS):
                p, l = divmod(g, 2)
                rows = pl.ds((p % per_tile) * 2 * SSM_GROUP + l * SSM_GROUP, SSM_GROUP)
                re, im = block(g)
                wide_ref[p, rows, pl.ds(l * SSM_STATE, SSM_STATE)] = re
                wide_ref[p, rows, pl.ds(PAIR_LANES + l * SSM_STATE, SSM_STATE)] = im
            out_ref[...] = wide_ref[...].astype(BF16)

        place(o_bb, lambda g: (bbr_ref[:, g, :], bbi_ref[:, g, :]))
        place(o_c, lambda g: (cr[g], -ci[g]))

    vm = pl.BlockSpec(memory_space=pltpu.VMEM)
    return pl.pallas_call(
        body, name=name, in_specs=[vm] * 7, out_specs=[vm] * 4, out_shape=[pows, pows, mats, mats],
        scratch_shapes=[pltpu.VMEM(b_re.shape, F32), pltpu.VMEM(b_re.shape, F32),
                        pltpu.VMEM((SSM_PAIRS, PAIR_TILE, PAIR_TILE), F32)],
    )(a_re, a_im, log_dt, b_re, b_im, c_re, c_im)


def _ssm_params_bwd(name, a_re, a_im, log_dt, b_re, b_im, d_ab_re, d_ab_im, d_bb_re, d_bb_im):
    gn = jax.ShapeDtypeStruct(a_re.shape, F32)
    cgn = jax.ShapeDtypeStruct(b_re.shape, F32)

    def body(ar, ai, ld, br, bi, g0, g1, g2, g3, o_ar, o_ai, o_ld, o_br, o_bi):
        _, vjp = jax.vjp(_ssm_disc, ar[...], ai[...], ld[...], br[...], bi[...])
        outs = vjp((g0[...], g1[...], g2[...], g3[...]))
        for o_ref, o in zip((o_ar, o_ai, o_ld, o_br, o_bi), outs):
            o_ref[...] = o

    vm = pl.BlockSpec(memory_space=pltpu.VMEM)
    return pl.pallas_call(body, name=name, in_specs=[vm] * 9, out_specs=[vm] * 5,
                          out_shape=[gn, gn, jax.ShapeDtypeStruct(log_dt.shape, F32), cgn, cgn],
                          )(a_re, a_im, log_dt, b_re, b_im, d_ab_re, d_ab_im, d_bb_re, d_bb_im)


def _scan_block(s_ref, carry_ref, tmp_ref, pw_ref, reverse, sprev=None):
    nl = SSM_LANES
    halves = range(SCAN_COLS // SCAN_SUB)
    zero = jnp.zeros((SCAN_SUB, SCAN_LANES), F32)
    for half in (reversed(halves) if reverse else halves):
        sub_rows = pl.ds(half * SCAN_SUB, SCAN_SUB)
        for lc in range(nl // SCAN_LANES):
            re_l = pl.ds(lc * SCAN_LANES, SCAN_LANES)
            im_l = pl.ds(nl + lc * SCAN_LANES, SCAN_LANES)
            are, aim = pw_ref[0, :, re_l], pw_ref[0, :, im_l]

            def step_of(j):
                return SCAN_STEPS - 1 - j if reverse else j

            def pass1(j, st):
                sr, si = st
                jj = step_of(j)
                nr = are * sr - aim * si + s_ref[jj, sub_rows, re_l]
                ni = are * si + aim * sr + s_ref[jj, sub_rows, im_l]
                s_ref[jj, sub_rows, re_l] = nr
                s_ref[jj, sub_rows, im_l] = ni
                return nr, ni

            er, ei = lax.fori_loop(0, SCAN_STEPS, pass1, (zero, zero), unroll=2)
            tmp_ref[0:SCAN_SUB, re_l] = er
            tmp_ref[0:SCAN_SUB, im_l] = ei
            apr, api = pw_ref[SCAN_STEPS - 1, 0:1, re_l], pw_ref[SCAN_STEPS - 1, 0:1, im_l]
            sr, si = carry_ref[0:1, re_l], carry_ref[0:1, im_l]
            for step in range(SCAN_SUB):
                c = SCAN_SUB - 1 - step if reverse else step
                tmp_ref[SCAN_SUB + c:SCAN_SUB + c + 1, re_l] = sr
                tmp_ref[SCAN_SUB + c:SCAN_SUB + c + 1, im_l] = si
                e_r, e_i = tmp_ref[c:c + 1, re_l], tmp_ref[c:c + 1, im_l]
                sr, si = apr * sr - api * si + e_r, apr * si + api * sr + e_i
            carry_ref[0:1, re_l] = sr
            carry_ref[0:1, im_l] = si
            cr = tmp_ref[SCAN_SUB:2 * SCAN_SUB, re_l]
            ci = tmp_ref[SCAN_SUB:2 * SCAN_SUB, im_l]

            if sprev is None:
                def pass2(j, st):
                    pr, pi = pw_ref[j, :, re_l], pw_ref[j, :, im_l]
                    jj = step_of(j)
                    s_ref[jj, sub_rows, re_l] += pr * cr - pi * ci
                    s_ref[jj, sub_rows, im_l] += pr * ci + pi * cr
                    return st

                lax.fori_loop(0, SCAN_STEPS, pass2, 0, unroll=2)
            else:
                st_ref, prev_ref, have_prev, dab_ref = sprev

                def corrected(jj, pr, pi):
                    gr = s_ref[jj, sub_rows, re_l] + pr * cr - pi * ci
                    gi = s_ref[jj, sub_rows, im_l] + pr * ci + pi * cr
                    s_ref[jj, sub_rows, re_l] = gr
                    s_ref[jj, sub_rows, im_l] = gi
                    return gr, gi

                def pass2(j, st):
                    dr, di = st
                    jj = SCAN_STEPS - 1 - j
                    gr, gi = corrected(jj, pw_ref[j, :, re_l], pw_ref[j, :, im_l])
                    qr, qi = st_ref[jj - 1, sub_rows, re_l], st_ref[jj - 1, sub_rows, im_l]
                    return dr + gr * qr + gi * qi, di + gi * qr - gr * qi

                dr, di = lax.fori_loop(0, SCAN_STEPS - 1, pass2, (zero, zero), unroll=2)
                gr, gi = corrected(0, pw_ref[SCAN_STEPS - 1, :, re_l], pw_ref[SCAN_STEPS - 1, :, im_l])
                sub = lax.broadcasted_iota(jnp.int32, (SCAN_SUB, SCAN_LANES), 0)
                if half == 0:
                    pv_r = prev_ref[SCAN_SUB - 1:SCAN_SUB, re_l] * have_prev
                    pv_i = prev_ref[SCAN_SUB - 1:SCAN_SUB, im_l] * have_prev
                else:
                    before = pl.ds(half * SCAN_SUB - 1, 1)
                    pv_r, pv_i = st_ref[SCAN_STEPS - 1, before, re_l], st_ref[SCAN_STEPS - 1, before, im_l]
                shape = (SCAN_SUB, SCAN_LANES)
                qr = jnp.where(sub == 0, jnp.broadcast_to(pv_r, shape),
                               pltpu.roll(st_ref[SCAN_STEPS - 1, sub_rows, re_l], 1, 0))
                qi = jnp.where(sub == 0, jnp.broadcast_to(pv_i, shape),
                               pltpu.roll(st_ref[SCAN_STEPS - 1, sub_rows, im_l], 1, 0))
                dab_ref[:, re_l] += dr + gr * qr + gi * qi
                dab_ref[:, im_l] += di + gi * qr - gr * qi


def _scan_view(a):
    return a.reshape(16, a.shape[0] // 16, a.shape[1])


def _pair_tile(p):
    start = (p * 2 * SSM_GROUP // PAIR_TILE) * PAIR_TILE
    return slice(start, start + PAIR_TILE)


def _pair_lanes(p):
    return pl.ds(p * PAIR_LANES, PAIR_LANES), pl.ds(SSM_LANES + p * PAIR_LANES, PAIR_LANES)


def _pair_store(s_ref, p, val):
    re_l, im_l = _pair_lanes(p)
    s_ref[:, :, re_l] = val[:, :PAIR_LANES].reshape(16, SCAN_COLS, PAIR_LANES)
    s_ref[:, :, im_l] = val[:, PAIR_LANES:].reshape(16, SCAN_COLS, PAIR_LANES)


def _pair_load(s_ref, p):
    re_l, im_l = _pair_lanes(p)
    parts = [s_ref[:, :, l].reshape(SCAN_BLOCK, PAIR_LANES) for l in (re_l, im_l)]
    return jnp.concatenate(parts, axis=1).astype(BF16)


def _pair_sum(fn):
    per = PAIR_TILE // (2 * SSM_GROUP)
    tiles = []
    for t in range(SSM_PAIRS // per):
        acc = None
        for p in range(t * per, (t + 1) * per):
            part = fn(p)
            acc = part if acc is None else acc + part
        tiles.append(acc)
    return jnp.concatenate(tiles, axis=1)


def _ssm_fwd(name, u, bb_mats, c_mats, pw_rows, d_skip):
    rows = u.shape[0]
    nl2 = 2 * SSM_LANES
    nblk = rows // SCAN_BLOCK

    def body(u_ref, bb_ref, c_ref, pw_ref, d_ref, y_ref, yg_ref, s_ref, carry_ref, tmp_ref):
        @pl.when(pl.program_id(0) == 0)
        def _():
            carry_ref[...] = jnp.zeros_like(carry_ref)

        uv = u_ref[...].reshape(SCAN_BLOCK, SSM_WIDTH)
        ub = uv.astype(BF16)
        for p in range(SSM_PAIRS):
            _pair_store(s_ref, p, _dot_nn(ub[:, _pair_tile(p)], bb_ref[p]))
        _scan_block(s_ref, carry_ref, tmp_ref, pw_ref, reverse=False)
        ys = _pair_sum(lambda p: _dot_nt(_pair_load(s_ref, p), c_ref[p]))
        yv = ys + d_ref[...] * uv
        y_ref[...] = yv.reshape(16, SCAN_COLS, SSM_WIDTH)
        yg_ref[...] = jax.nn.gelu(yv).astype(BF16).reshape(16, SCAN_COLS, SSM_WIDTH)

    const = lambda shape: pl.BlockSpec(shape, lambda i: (0,) * len(shape))
    blk = lambda cols: pl.BlockSpec((16, SCAN_COLS, cols), lambda i: (0, i, 0))
    pair_mats = const((SSM_PAIRS, PAIR_TILE, PAIR_TILE))
    y, yg, s = pl.pallas_call(
        body, name=name, grid=(nblk,),
        in_specs=[blk(SSM_WIDTH), pair_mats, pair_mats, const((SCAN_STEPS, SCAN_SUB, nl2)), const((1, SSM_WIDTH))],
        out_specs=[blk(SSM_WIDTH), blk(SSM_WIDTH), blk(nl2)],
        out_shape=[jax.ShapeDtypeStruct((16, rows // 16, SSM_WIDTH), F32),
                   jax.ShapeDtypeStruct((16, rows // 16, SSM_WIDTH), BF16),
                   jax.ShapeDtypeStruct((16, rows // 16, nl2), F32)],
        scratch_shapes=[pltpu.VMEM((SCAN_SUB, nl2), F32), pltpu.VMEM((2 * SCAN_SUB, nl2), F32)],
        compiler_params=_cparams("arbitrary"),
    )(_scan_view(u), bb_mats, c_mats, pw_rows, d_skip)
    return y.reshape(rows, SSM_WIDTH), yg.reshape(rows, SSM_WIDTH), s.reshape(rows, nl2)


def _ssm_bwd(name, dy, u, states, bb_mats, c_mats, pwc_rows, d_skip):
    rows = u.shape[0]
    nl2 = 2 * SSM_LANES
    nblk = rows // SCAN_BLOCK

    def body(dy_ref, u_ref, st_ref, prev_ref, bb_ref, c_ref, pw_ref, d_ref,
             du_ref, dbb_ref, dc_ref, dab_ref, dd_ref, g_ref, carry_ref, tmp_ref):
        i = pl.program_id(0)

        @pl.when(i == 0)
        def _():
            carry_ref[...] = jnp.zeros_like(carry_ref)
            for ref in (dbb_ref, dc_ref, dab_ref, dd_ref):
                ref[...] = jnp.zeros_like(ref)

        dyv = dy_ref[...].reshape(SCAN_BLOCK, SSM_WIDTH)
        uv = u_ref[...].reshape(SCAN_BLOCK, SSM_WIDTH)
        dyb, ub = dyv.astype(BF16), uv.astype(BF16)
        for p in range(SSM_PAIRS):
            _pair_store(g_ref, p, _dot_nn(dyb[:, _pair_tile(p)], c_ref[p]))
        have_prev = (i < nblk - 1).astype(F32)
        _scan_block(g_ref, carry_ref, tmp_ref, pw_ref, reverse=True,
                    sprev=(st_ref, prev_ref, have_prev, dab_ref))

        def pair_work(p):
            gp = _pair_load(g_ref, p)
            dbb_ref[p] += _dot_tn(ub[:, _pair_tile(p)], gp)
            dc_ref[p] += _dot_tn(dyb[:, _pair_tile(p)], _pair_load(st_ref, p))
            return _dot_nt(gp, bb_ref[p])

        du_ref[...] = (_pair_sum(pair_work) + d_ref[...] * dyv).reshape(16, SCAN_COLS, SSM_WIDTH)
        dd_ref[...] += jnp.sum(dyv * uv, axis=0, keepdims=True)

    const = lambda shape: pl.BlockSpec(shape, lambda i: (0,) * len(shape))
    blk = lambda cols: pl.BlockSpec((16, SCAN_COLS, cols), lambda i: (0, nblk - 1 - i, 0))
    per8 = SCAN_COLS // SCAN_SUB
    prev_spec = pl.BlockSpec((None, SCAN_SUB, nl2), lambda i: (15, jnp.maximum((nblk - 1 - i) * per8 - 1, 0), 0))
    pair_mats = const((SSM_PAIRS, PAIR_TILE, PAIR_TILE))
    pair_shape = jax.ShapeDtypeStruct((SSM_PAIRS, PAIR_TILE, PAIR_TILE), F32)
    sv = _scan_view(states)
    du, dbb, dc, dab, dd = pl.pallas_call(
        body, name=name, grid=(nblk,),
        in_specs=[blk(SSM_WIDTH), blk(SSM_WIDTH), blk(nl2), prev_spec, pair_mats, pair_mats,
                  const((SCAN_STEPS, SCAN_SUB, nl2)), const((1, SSM_WIDTH))],
        out_specs=[blk(SSM_WIDTH), pair_mats, pair_mats, const((SCAN_SUB, nl2)), const((1, SSM_WIDTH))],
        out_shape=[jax.ShapeDtypeStruct((16, rows // 16, SSM_WIDTH), F32), pair_shape, pair_shape,
                   jax.ShapeDtypeStruct((SCAN_SUB, nl2), F32), jax.ShapeDtypeStruct((1, SSM_WIDTH), F32)],
        scratch_shapes=[pltpu.VMEM((16, SCAN_COLS, nl2), F32), pltpu.VMEM((SCAN_SUB, nl2), F32),
                        pltpu.VMEM((2 * SCAN_SUB, nl2), F32)],
        compiler_params=_cparams("arbitrary"),
    )(_scan_view(dy), _scan_view(u), sv, sv, bb_mats, c_mats, pwc_rows, d_skip)
    return du.reshape(rows, SSM_WIDTH), dbb, dc, dab, dd


def _adamw(name, w, m, v, gparts, tr):
    rows, cols = w.shape

    def body(w_ref, m_ref, v_ref, g_ref, og_ref, od_ref, om_ref, ov_ref):
        g = g_ref[0].astype(F32)
        for i in range(1, N_DEV):
            g = g + g_ref[i].astype(F32)
        m_new = B1 * m_ref[...] + (1.0 - B1) * g
        v_new = B2 * v_ref[...] + (1.0 - B2) * (g * g)
        m_hat = m_new / (1.0 - B1 ** STEP)
        v_hat = v_new / (1.0 - B2 ** STEP)
        og_ref[...] = g
        od_ref[...] = -LR * (m_hat / (jnp.sqrt(v_hat) + ADAM_EPS) + WD * w_ref[...])
        om_ref[...] = m_new
        ov_ref[...] = v_new

    spec = pl.BlockSpec((tr, cols), lambda i: (i, 0))
    shape = jax.ShapeDtypeStruct((rows, cols), F32)
    return pl.pallas_call(
        body, name=name, grid=(rows // tr,),
        in_specs=[spec, spec, spec, pl.BlockSpec((N_DEV, tr, cols), lambda i: (0, i, 0))],
        out_specs=[spec] * 4, out_shape=[shape] * 4,
        compiler_params=_cparams("parallel"),
    )(w, m, v, gparts)


_SHARDED = (
    ("ffn1_w_gate", True, (352, 1024)), ("ffn1_w_up", True, (352, 1024)), ("ffn1_w_down", False, (352, 1024)),
    ("w_in", True, (608, 1024)), ("ssm_w_glu", True, (128, 512)), ("w_attn_branch", True, (128, 256)),
    ("w_ssm_branch", True, (128, 512)), ("w_out", False, (128, 1024)),
    ("ffn2_w_gate", True, (352, 1024)), ("ffn2_w_up", True, (352, 1024)), ("ffn2_w_down", False, (352, 1024)),
)
_SMALL = ("ffn1_norm", "mix_norm", "gate_bias", "rel_bias_table", "ssm_a_re", "ssm_a_im", "ssm_log_dt",
          "ssm_b_re", "ssm_b_im", "ssm_c_re", "ssm_c_im", "ssm_d", "ffn2_norm", "final_norm")
_ORDER = ("ffn1_norm", "ffn1_w_gate", "ffn1_w_up", "ffn1_w_down", "mix_norm", "w_in", "gate_bias",
          "rel_bias_table", "ssm_a_re", "ssm_a_im", "ssm_log_dt", "ssm_b_re", "ssm_b_im", "ssm_c_re",
          "ssm_c_im", "ssm_d", "ssm_w_glu", "w_attn_branch", "w_ssm_branch", "w_out", "ffn2_norm",
          "ffn2_w_gate", "ffn2_w_up", "ffn2_w_down", "final_norm")


def _pack_rows(shape):
    return shape[0] * shape[1] // D_MODEL


_SHARD_INFO = {nm: (tr, shape) for nm, tr, shape in _SHARDED}
_PHASES = {
    "f1gu": ("ffn1_w_gate", "ffn1_w_up"), "f1d": ("ffn1_w_down",),
    "mix": ("w_in", "ssm_w_glu", "w_attn_branch", "w_ssm_branch", "w_out"),
    "f2": ("ffn2_w_gate", "ffn2_w_up", "ffn2_w_down"),
}


def _to_rows(a, nm):
    tr, shape = _SHARD_INFO[nm]
    return (a.T if tr else a).reshape(_pack_rows(shape), D_MODEL)


def _from_rows(p, nm):
    tr, shape = _SHARD_INFO[nm]
    a = p.reshape(shape)
    return a.T if tr else a


def _full_weight(gathered, nm):
    _, shape = _SHARD_INFO[nm]
    return gathered.reshape(N_DEV * shape[0], shape[1])


def _grad_blocks(g, nm):
    _, shape = _SHARD_INFO[nm]
    return g.astype(BF16).reshape(N_DEV, _pack_rows(shape), D_MODEL)


_SMALL_TILE = 8 * 128


def _small_rows(a):
    flat = a.reshape(-1)
    return jnp.pad(flat, (0, (-flat.shape[0]) % _SMALL_TILE)).reshape(-1, 128)


def _pack_small(ws, last=None):
    tail = jnp.zeros((), F32) if last is None else last
    return jnp.concatenate([_small_rows(ws[nm]) for nm in _SMALL] + [_small_rows(tail)], axis=0)


def _unpack_small(pack, like):
    out, r0 = {}, 0
    for nm in _SMALL:
        n = like[nm].size
        nr = 8 * -(-n // _SMALL_TILE)
        out[nm] = pack[r0:r0 + nr].reshape(-1)[:n].reshape(like[nm].shape)
        r0 += nr
    return out


def _residue_order(a):
    rows, cols = a.shape
    return a.reshape(rows // 16, 16, cols).transpose(1, 0, 2).reshape(rows, cols)


def _token_order(a):
    rows, cols = a.shape
    return a.reshape(16, rows // 16, cols).transpose(1, 0, 2).reshape(rows, cols)


_PAIRS_PER_TILE = PAIR_TILE // (2 * SSM_GROUP)
_PAIR_AXES = (SSM_PAIRS // _PAIRS_PER_TILE, _PAIRS_PER_TILE, 2)


def _pair_diagonals(acc):
    k, j, l = _PAIR_AXES
    eight = acc.reshape(k, j, j, l, SSM_GROUP, 2, l, SSM_STATE)
    eye_j, eye_l = jnp.eye(j, dtype=acc.dtype), jnp.eye(l, dtype=acc.dtype)
    own = jnp.einsum("kjJLcxln,jJ,lL->xkjlcn", eight, eye_j, eye_l).reshape(2, SSM_GROUPS, SSM_GROUP, SSM_STATE)
    return own[0], own[1]


def _local_step(xs, target, small, weights_of, send_grads, first_deps=()):
    rows = xs.shape[0]
    gfull, gsmall = {}, {}

    table_t = small["rel_bias_table"].T
    tables, bias4 = [], []
    for g in range(N_GROUPS):
        bucket, valid = [jnp.asarray(t) for t in _attn_tables(g, rows)]
        bias_g = _bias_fwd(f"rel_bias_fwd_{g}", bucket, valid, table_t[g * HEADS_PER_GROUP:(g + 1) * HEADS_PER_GROUP])
        tables.append(bucket)
        bias4.append(bias_g.reshape(-1, bias_g.shape[-1]))
    pw_re, pw_im, bb_mats, c_mats = _ssm_params_fwd(
        "ssm_params_fwd", small["ssm_a_re"], small["ssm_a_im"], small["ssm_log_dt"].reshape(SSM_GROUPS, 1),
        small["ssm_b_re"].transpose(2, 0, 1), small["ssm_b_im"].transpose(2, 0, 1), small["ssm_c_re"], small["ssm_c_im"])

    def power_rows(sign):
        row = jnp.concatenate([pw_re.reshape(SCAN_STEPS, 1, SSM_LANES), sign * pw_im.reshape(SCAN_STEPS, 1, SSM_LANES)],
                              axis=2)
        return jnp.broadcast_to(row, (SCAN_STEPS, SCAN_SUB, 2 * SSM_LANES))

    pw_fwd, pw_bwd = power_rows(1.0), power_rows(-1.0)
    d_skip = small["ssm_d"].reshape(1, SSM_WIDTH)
    wf = dict(weights_of("f1", [xs, target, bb_mats, c_mats, pw_fwd, pw_bwd] + bias4))

    x1, h1, gg1, uu1, hmix = _ffn_fwd("ffn1_fwd", xs, small["ffn1_norm"], wf["ffn1_w_gate"], wf["ffn1_w_up"],
                                      wf["ffn1_w_down"], small["mix_norm"], deps=first_deps)
    wf.update(weights_of("mix", x1))
    w_in = wf["w_in"]
    w_qkv, w_u, w_g = w_in[:3 * ATTN_WIDTH], w_in[3 * ATTN_WIDTH:3 * ATTN_WIDTH + SSM_WIDTH], w_in[3 * ATTN_WIDTH + SSM_WIDTH:]
    qscale = jnp.concatenate([jnp.full((1, ATTN_WIDTH), HEAD_DIM ** -0.5, F32), jnp.ones((1, 2 * ATTN_WIDTH), F32)], axis=1)
    qkv, = _mm("in_qkv", [(hmix, w_qkv)], True, 3 * ATTN_WIDTH, [BF16],
               epilogue=lambda acc, sc: (acc * sc,), extras=[(qscale, 0)], tn=ATTN_WIDTH)
    u, = _mm("in_u", [(hmix, w_u)], True, SSM_WIDTH, [F32])
    gates, = _mm("in_gates", [(hmix, w_g)], True, 2 * D_MODEL, [BF16],
                 epilogue=lambda acc, b: (_sigmoid(acc + b),), extras=[(small["gate_bias"], 0)])

    o_g, lse_g = [], []
    for g in range(N_GROUPS):
        o, lse = _attn_fwd(f"attn_fwd_{g}", qkv, g, bias4[g])
        o_g.append(o)
        lse_g.append(lse)
    oa_f32, oa = _combine_fwd("attn_combine_fwd", o_g, lse_g)
    y_attn, = _mm("attn_branch", [(oa, wf["w_attn_branch"])], True, D_MODEL, [BF16])
    y_raw, ygelu, states = _ssm_fwd("ssm_fwd", u, bb_mats, c_mats, pw_fwd, d_skip)
    glu, ysg = _mm("ssm_glu", [(ygelu, wf["ssm_w_glu"])], True, 2 * SSM_WIDTH, [F32, BF16],
                   epilogue=lambda gv: (gv, gv[:, :SSM_WIDTH] * _sigmoid(gv[:, SSM_WIDTH:])),
                   tn=2 * SSM_WIDTH, out_cols=[2 * SSM_WIDTH, SSM_WIDTH])
    y_ssm, merged = _mm("ssm_branch_merge", [(ysg, wf["w_ssm_branch"])], True, D_MODEL, [BF16, BF16],
                        epilogue=lambda acc, ga, gs, ya: (acc, ga * ya + gs * acc),
                        extras=[(gates, 0), (gates, D_MODEL), (y_attn, 0)])
    x2, = _mm("mix_out", [(merged, wf["w_out"])], False, D_MODEL, [F32],
              epilogue=lambda acc, res: (res + acc,), extras=[(x1, 0)])
    wf.update(weights_of("f2", x2))
    dx3, h2, gg2, uu2, gsmall["final_norm"], gsmall["loss"] = _ffn_fwd_head(
        "ffn2_fwd", x2, small["ffn2_norm"], wf["ffn2_w_gate"], wf["ffn2_w_up"], wf["ffn2_w_down"],
        small["final_norm"].reshape(1, D_MODEL), target)

    dx2, dgg2, duu2, act2, gsmall["ffn2_norm"] = _ffn_bwd(
        "ffn2_bwd", dx3, x2, small["ffn2_norm"], gg2, uu2, wf["ffn2_w_gate"], wf["ffn2_w_up"], wf["ffn2_w_down"])
    gfull["ffn2_w_gate"] = _mm_tn("ffn2_dwg", dgg2, h2, out_dtype=BF16)
    gfull["ffn2_w_up"] = _mm_tn("ffn2_dwu", duu2, h2, out_dtype=BF16)
    gfull["ffn2_w_down"] = _mm_tn("ffn2_dwd", act2, dx3, scale=0.5, out_dtype=BF16)
    sent = send_grads("f2", gfull)

    def merge_bwd(dm, ga, gs, ya, ys):
        dza, dzs = dm * ya * ga * (1.0 - ga), dm * ys * gs * (1.0 - gs)
        return (dm * ga, dm * gs, dza, dzs, jnp.sum(dza, axis=0, keepdims=True), jnp.sum(dzs, axis=0, keepdims=True))

    dya, dys, dzga, dzgs, dba, dbs = _mm(
        "mix_out_bwd", [(dx2, wf["w_out"])], True, D_MODEL, [BF16] * 4, epilogue=merge_bwd, row_sums=2,
        extras=[(gates, 0), (gates, D_MODEL), (y_attn, 0), (y_ssm, 0)], deps=sent, tm=512, tn=D_MODEL)
    gfull["w_out"] = _mm_tn("dw_out", merged, dx2, out_dtype=BF16)
    gsmall["gate_bias"] = jnp.concatenate([dba, dbs], axis=1)

    gfull["w_ssm_branch"] = _mm_tn("dw_ssm_branch", dys, ysg, out_dtype=BF16)

    def glu_bwd(dysg, av, bv):
        sb = _sigmoid(bv)
        return (dysg * sb, dysg * av * sb * (1.0 - sb))

    dglu_a, dglu_b = _mm("ssm_branch_bwd", [(dys, wf["w_ssm_branch"])], False, SSM_WIDTH, [BF16, BF16],
                         epilogue=glu_bwd, extras=[(glu, 0), (glu, SSM_WIDTH)])
    w_glu = wf["ssm_w_glu"]
    gfull["ssm_w_glu"] = _mm_tn_stack("dw_glu", [dglu_a, dglu_b], ygelu, out_dtype=BF16)

    def gelu_bwd(acc, yv):
        _, vjp = jax.vjp(jax.nn.gelu, yv)
        return (vjp(acc)[0],)

    dy_raw, = _mm("ssm_glu_bwd", [(dglu_a, w_glu[:SSM_WIDTH]), (dglu_b, w_glu[SSM_WIDTH:])], False, SSM_WIDTH, [F32],
                  epilogue=gelu_bwd, extras=[(y_raw, 0)])
    du, dbb_acc, dc_acc, dab_rows, gsmall_d = _ssm_bwd(
        "ssm_bwd", dy_raw, u, states, bb_mats, c_mats, pw_bwd, d_skip)
    gsmall["ssm_d"] = gsmall_d
    dbb_re, dbb_im = [a.transpose(1, 0, 2) for a in _pair_diagonals(dbb_acc)]
    dc_re, dc_im = _pair_diagonals(dc_acc)
    gsmall["ssm_c_re"], gsmall["ssm_c_im"] = dc_re, -dc_im
    dab = _colsum("ssm_dab", dab_rows)
    d_ar, d_ai, d_ld, d_br, d_bi = _ssm_params_bwd(
        "ssm_params_bwd", small["ssm_a_re"], small["ssm_a_im"], small["ssm_log_dt"].reshape(SSM_GROUPS, 1),
        small["ssm_b_re"].transpose(2, 0, 1), small["ssm_b_im"].transpose(2, 0, 1),
        dab[:, :SSM_LANES].reshape(SSM_GROUPS, SSM_STATE), dab[:, SSM_LANES:].reshape(SSM_GROUPS, SSM_STATE),
        dbb_re, dbb_im)
    gsmall["ssm_a_re"], gsmall["ssm_a_im"], gsmall["ssm_log_dt"] = d_ar, d_ai, d_ld.reshape(SSM_GROUPS)
    gsmall["ssm_b_re"], gsmall["ssm_b_im"] = d_br.transpose(1, 2, 0), d_bi.transpose(1, 2, 0)

    gfull["w_attn_branch"] = _mm_tn("dw_attn_branch", dya, oa, out_dtype=BF16)
    doa, = _mm("attn_branch_bwd", [(dya, wf["w_attn_branch"])], False, ATTN_OUT, [F32])
    dc = _combine_bwd("attn_combine_bwd", doa, oa_f32, lse_g)
    dqkv_cols = [None] * 9
    dtable = []
    for g in range(N_GROUPS):
        dq, dk, dv, db = _attn_bwd(f"attn_bwd_{g}", qkv, dc[g], lse_g[g], dc[3 + g], g, bias4[g])
        dqkv_cols[g], dqkv_cols[3 + g], dqkv_cols[6 + g] = dq, dk, dv
        dt = _bias_bwd(f"rel_bias_bwd_{g}", tables[g], db.reshape(HEADS_PER_GROUP, -1, db.shape[-1]))
        dtable.append(dt[:, :HEADS_PER_GROUP])
    gsmall["rel_bias_table"] = jnp.concatenate(dtable, axis=1)

    gfull["w_in"] = jnp.concatenate([_mm_tn_stack("dw_in_qkv", dqkv_cols, hmix, out_dtype=BF16),
                                     _mm_tn_stack("dw_in_rest", [du, dzga, dzgs], hmix, out_dtype=BF16)], axis=0)
    sent = send_grads("mix", gfull)
    qkv_pairs = [(c, w_qkv[i * ATTN_OUT:(i + 1) * ATTN_OUT]) for i, c in enumerate(dqkv_cols)]

    def mix_norm_bwd(dh, xv, gain, dres):
        r, xh = _rms_parts(xv)
        return dres + _rms_bwd_dx(dh, gain, r, xh), jnp.sum(dh * xh, axis=0, keepdims=True)

    dx1, gsmall["mix_norm"] = _mm(
        "in_bwd", qkv_pairs + [(du, w_u), (dzga, w_g[:D_MODEL]), (dzgs, w_g[D_MODEL:])], False, D_MODEL, [F32],
        epilogue=mix_norm_bwd, row_sums=1, extras=[(x1, 0), (small["mix_norm"], 0), (dx2, 0)], tm=512, tn=D_MODEL,
        deps=sent)

    dx, dgg1, duu1, act1, gsmall["ffn1_norm"] = _ffn_bwd(
        "ffn1_bwd", dx1, xs, small["ffn1_norm"], gg1, uu1, wf["ffn1_w_gate"], wf["ffn1_w_up"], wf["ffn1_w_down"])
    sent = send_grads("small", gsmall)
    gfull["ffn1_w_gate"] = _mm_tn("ffn1_dwg", dgg1, h1, deps=sent, out_dtype=BF16)
    gfull["ffn1_w_up"] = _mm_tn("ffn1_dwu", duu1, h1, out_dtype=BF16)
    sent = send_grads("f1gu", gfull)
    gfull["ffn1_w_down"] = _mm_tn("ffn1_dwd", act1, dx1, scale=0.5, deps=sent, out_dtype=BF16)
    send_grads("f1d", gfull)
    return dx, gsmall


def kernel(x, ffn1_norm, ffn1_w_gate, ffn1_w_up, ffn1_w_down, mix_norm, w_in, gate_bias, rel_bias_table, ssm_a_re, ssm_a_im, ssm_log_dt, ssm_b_re, ssm_b_im, ssm_c_re, ssm_c_im, ssm_d, ssm_w_glu, w_attn_branch, w_ssm_branch, w_out, ffn2_norm, ffn2_w_gate, ffn2_w_up, ffn2_w_down, final_norm, loss_target, m_ffn1_norm, m_ffn1_w_gate, m_ffn1_w_up, m_ffn1_w_down, m_mix_norm, m_w_in, m_gate_bias, m_rel_bias_table, m_ssm_a_re, m_ssm_a_im, m_ssm_log_dt, m_ssm_b_re, m_ssm_b_im, m_ssm_c_re, m_ssm_c_im, m_ssm_d, m_ssm_w_glu, m_w_attn_branch, m_w_ssm_branch, m_w_out, m_ffn2_norm, m_ffn2_w_gate, m_ffn2_w_up, m_ffn2_w_down, m_final_norm, v_ffn1_norm, v_ffn1_w_gate, v_ffn1_w_up, v_ffn1_w_down, v_mix_norm, v_w_in, v_gate_bias, v_rel_bias_table, v_ssm_a_re, v_ssm_a_im, v_ssm_log_dt, v_ssm_b_re, v_ssm_b_im, v_ssm_c_re, v_ssm_c_im, v_ssm_d, v_ssm_w_glu, v_w_attn_branch, v_w_ssm_branch, v_w_out, v_ffn2_norm, v_ffn2_w_gate, v_ffn2_w_up, v_ffn2_w_down, v_final_norm):
    given = dict(locals())
    shapes = {nm: given[nm].shape for nm in _ORDER}

    def strip(a):
        return a[0] if a.ndim >= 2 and a.shape[0] == 1 else a

    w = {nm: strip(given[nm]) for nm in _ORDER}
    m = {nm: strip(given["m_" + nm]) for nm in _ORDER}
    v = {nm: strip(given["v_" + nm]) for nm in _ORDER}
    for d in (w, m, v):
        d["rel_bias_table"] = d["rel_bias_table"].reshape(N_BUCKETS, N_GROUPS * HEADS_PER_GROUP)

    weight_phases = {"f1": _PHASES["f1gu"] + _PHASES["f1d"], "mix": _PHASES["mix"], "f2": _PHASES["f2"]}
    pending_w, w_rows, deps, zero = {}, {}, [], 0.0
    for phase, names in weight_phases.items():
        w_rows.update({nm: _to_rows(w[nm] + zero, nm) for nm in names})
        pending_w[phase] = _exchange_start(f"gather_{phase}_start", [w_rows[nm].astype(BF16) for nm in names],
                                           gather=True, deps=deps)
        deps = [pending_w[phase][4]]
        zero = pending_w["f1"][4][0, 0]
    m_rows = {nm: _to_rows(m[nm] + zero, nm) for nm in _SHARD_INFO}
    v_rows = {nm: _to_rows(v[nm] + zero, nm) for nm in _SHARD_INFO}
    small = {nm: w[nm] for nm in _SMALL}
    small_in = {nm: small[nm] + zero for nm in _SMALL}
    for nm in ("ffn1_norm", "mix_norm", "ffn2_norm", "gate_bias"):
        small_in[nm] = small_in[nm].reshape(1, -1)

    def weights_of(phase, after):
        if phase == "f1":
            after = list(after) + list(m_rows.values()) + list(v_rows.values())
        landed = _exchange_wait(f"gather_{phase}_wait", pending_w[phase], after, gather=True)
        return {nm: _full_weight(got, nm) for nm, got in zip(weight_phases[phase], landed)}

    pending_g = {}

    def send_grads(phase, grads):
        if phase == "small":
            gs_pack = _pack_small({nm: grads[nm].reshape(small[nm].shape) for nm in _SMALL}, last=grads["loss"])
            pending_g[phase] = _exchange_start("gather_small_start", [gs_pack], gather=True)
        else:
            pending_g[phase] = _exchange_start(f"scatter_{phase}_start",
                                               [_grad_blocks(grads[nm], nm) for nm in _PHASES[phase]], gather=False)
        return [pending_g[phase][4]]

    dx, gsmall = _local_step(_residue_order(x[0]), _residue_order(loss_target[0]), small_in, weights_of, send_grads,
                             first_deps=[pending_w["f2"][4]])
    dx = _token_order(dx)

    updated = {}
    after = pending_g["f1d"][4]
    for phase in ("f2", "mix", "small", "f1gu", "f1d"):
        landed = _exchange_wait(f"exchange_{phase}_wait", pending_g[phase], after, gather=phase == "small")
        if phase == "small":
            sm = _adamw("adamw_small", _pack_small(small), _pack_small({nm: m[nm] for nm in _SMALL}),
                        _pack_small({nm: v[nm] for nm in _SMALL}), landed[0], landed[0].shape[1])
            after = sm[0]
            continue
        for nm, recv in zip(_PHASES[phase], landed):
            tr = max(t for t in range(16, 353, 16) if w_rows[nm].shape[0] % t == 0)
            updated[nm] = _adamw(f"adamw_{nm}", w_rows[nm], m_rows[nm], v_rows[nm], recv, tr)
            after = updated[nm][0]

    loss = sm[0][-8, 0]
    outs = []
    for i in range(4):
        sml = _unpack_small(sm[i], small)
        outs.append([(_from_rows(updated[nm][i], nm) if nm in updated else sml[nm]).reshape(shapes[nm])
                     for nm in _ORDER])
    return (loss, dx[None], *outs[0], *outs[1], *outs[2], *outs[3])
```

```python
import math

import numpy as np
import jax
import jax.numpy as jnp
from jax import lax
from jax.experimental import pallas as pl
from jax.experimental.pallas import tpu as pltpu

F32 = jnp.float32
BF16 = jnp.bfloat16

N_DEV = 8
D_MODEL = 1024
HEAD_DIM = 64
HEADS_PER_GROUP = 4
DILATIONS = (1, 4, 16)
N_GROUPS = 3
ATTN_WIDTH = 768
ATTN_OUT = 256
BLOCK = 128
N_BUCKETS = 32
MAX_DISTANCE = 2048
NEG_INF = -1e30
SSM_WIDTH = 512
SSM_GROUPS = 32
SSM_GROUP = 16
SSM_STATE = 64
SSM_LANES = SSM_GROUPS * SSM_STATE
SSM_PAIRS = SSM_GROUPS // 2
PAIR_LANES = 2 * SSM_STATE
PAIR_TILE = 256
EPS = 1e-6
LR, B1, B2, ADAM_EPS, WD, STEP = 0.001, 0.9, 0.999, 1e-08, 0.01, 10

VMEM_LIMIT_BYTES = 56 * 1024 * 1024
FFN_CHUNK = 768
FFN_DW_ROWS = 2048
SCAN_BLOCK = 256
SCAN_STEPS = 16
SCAN_COLS = SCAN_BLOCK // SCAN_STEPS
SCAN_SUB = 8
SCAN_LANES = 512

MESH = pl.DeviceIdType.MESH


def _cparams(*sem):
    return pltpu.CompilerParams(dimension_semantics=sem, vmem_limit_bytes=VMEM_LIMIT_BYTES)


def _dot(a, b, dims):
    return lax.dot_general(a, b, (dims, ((), ())), preferred_element_type=F32)


def _dot_nn(a, b):
    return _dot(a, b, ((1,), (0,)))


def _dot_nt(a, b):
    return _dot(a, b, ((1,), (1,)))


def _dot_tn(a, b):
    return _dot(a, b, ((0,), (0,)))


def _sigmoid(x):
    return 1.0 / (1.0 + jnp.exp(-x))


_HBM_SPEC = pl.BlockSpec(memory_space=pltpu.HBM)
_SEM_SPEC = pl.BlockSpec(memory_space=pltpu.SEMAPHORE)
_ANY_SPEC = pl.BlockSpec(memory_space=pl.ANY)
_EFFECT = pltpu.SideEffectType.DATAFLOW_SIDE_EFFECTING


def _peers(x, y, c):
    return [(1 - x if k & 4 else x, 1 - y if k & 2 else y, 1 - c if k & 1 else c) for k in range(1, N_DEV)]


def _exchange_copies(x_refs, land_refs, send_sems, recv_sems, gather):
    x, y, c = lax.axis_index("x"), lax.axis_index("y"), lax.axis_index("c")
    me = 4 * x + 2 * y + c
    copies = []
    for a, (x_ref, land_ref) in enumerate(zip(x_refs, land_refs)):
        for k, (px, py, pc) in enumerate(_peers(x, y, c)):
            src = x_ref if gather else x_ref.at[4 * px + 2 * py + pc]
            copies.append(pltpu.make_async_remote_copy(
                src_ref=src, dst_ref=land_ref.at[me], send_sem=send_sems.at[N_DEV * a + k],
                recv_sem=recv_sems.at[(N_DEV - 1) * a + k], device_id=(px, py, pc), device_id_type=MESH))
    owns = [pltpu.make_async_copy(x_ref if gather else x_ref.at[me], land_ref.at[me],
                                  send_sems.at[N_DEV * a + N_DEV - 1])
            for a, (x_ref, land_ref) in enumerate(zip(x_refs, land_refs))]
    return owns, copies


def _exchange_start(name, xs_list, gather, deps=()):
    n, nd = len(xs_list), len(deps)
    land_shapes = [(N_DEV, *xs.shape) if gather else xs.shape for xs in xs_list]

    def body(*refs):
        x_refs, land_refs = refs[:n], refs[n:2 * n]
        send_sems, recv_sems = refs[2 * n + nd:2 * n + nd + 2]
        token = refs[-1]
        owns, copies = _exchange_copies(x_refs, land_refs, send_sems, recv_sems, gather)
        for cp in copies + owns:
            cp.start()
        token[...] = jnp.zeros_like(token)

    hbm = lambda a: pltpu.with_memory_space_constraint(a, pltpu.HBM)
    outs = pl.pallas_call(
        body, name=name,
        out_shape=(pltpu.SemaphoreType.DMA((n * N_DEV,)), pltpu.SemaphoreType.DMA((n * (N_DEV - 1),)),
                   *[pltpu.HBM(xs.shape, xs.dtype) for xs in xs_list],
                   *[pltpu.HBM(shape, xs.dtype) for shape, xs in zip(land_shapes, xs_list)],
                   jax.ShapeDtypeStruct((8, 128), F32)),
        in_specs=(_HBM_SPEC,) * (2 * n) + (_ANY_SPEC,) * nd,
        out_specs=(_SEM_SPEC, _SEM_SPEC) + (_HBM_SPEC,) * (2 * n) + (pl.BlockSpec(memory_space=pltpu.VMEM),),
        input_output_aliases={i: 2 + i for i in range(2 * n)},
        compiler_params=pltpu.CompilerParams(has_side_effects=_EFFECT),
    )(*[hbm(xs) for xs in xs_list], *[hbm(lax.empty(shape, xs.dtype)) for shape, xs in zip(land_shapes, xs_list)],
      *deps)
    return outs[0], outs[1], list(outs[2:2 + n]), list(outs[2 + n:2 + 2 * n]), outs[-1]


def _exchange_wait(name, handle, after, gather):
    send_sems, recv_sems, xs_thru, lands_thru, _ = handle
    n = len(xs_thru)
    after = list(after) if isinstance(after, (list, tuple)) else [after]

    def body(*refs):
        x_refs, land_refs = refs[:n], refs[n:2 * n]
        send_sems, recv_sems = refs[2 * n:2 * n + 2]
        owns, copies = _exchange_copies(x_refs, land_refs, send_sems, recv_sems, gather)
        for cp in copies:
            cp.wait_send()
            cp.wait_recv()
        for cp in owns:
            cp.wait()

    outs = pl.pallas_call(
        body, name=name,
        out_shape=tuple(pltpu.HBM(a.shape, a.dtype) for a in xs_thru + lands_thru),
        in_specs=(_HBM_SPEC,) * (2 * n) + (_SEM_SPEC, _SEM_SPEC) + (_ANY_SPEC,) * len(after),
        out_specs=(_HBM_SPEC,) * (2 * n), input_output_aliases={i: i for i in range(2 * n)},
        compiler_params=pltpu.CompilerParams(has_side_effects=_EFFECT),
    )(*xs_thru, *lands_thru, send_sems, recv_sems, *after)
    return list(outs[n:])


def _mm(name, pairs, nt, n_cols, out_dtypes, epilogue=None, extras=(), tm=1024, tn=512, deps=(), row_sums=0,
        out_cols=None):
    rows = pairs[0][0].shape[0]
    tm = min(tm, rows)
    tn = min(tn, n_cols)
    na, ne, nd, no = len(pairs), len(extras), len(deps), len(out_dtypes)

    def body(*refs):
        a_refs, w_refs = refs[:na], refs[na:2 * na]
        e_refs, o_refs = refs[2 * na:2 * na + ne], refs[2 * na + ne + nd:]
        acc = None
        for a_ref, w_ref in zip(a_refs, w_refs):
            a = a_ref[...].astype(BF16)
            w = w_ref[...].astype(BF16)
            p = _dot_nt(a, w) if nt else _dot_nn(a, w)
            acc = p if acc is None else acc + p
        outs = (acc,) if epilogue is None else epilogue(acc, *[e[...].astype(F32) for e in e_refs])
        for o_ref, o in zip(o_refs[:no], outs[:no]):
            o_ref[...] = o.astype(o_ref.dtype)
        for r_ref, o in zip(o_refs[no:], outs[no:]):
            @pl.when(pl.program_id(0) == 0)
            def _():
                r_ref[...] = jnp.zeros_like(r_ref)

            r_ref[...] += o

    in_specs = [pl.BlockSpec((tm, a.shape[1]), lambda i, j: (i, 0)) for a, _ in pairs]
    for _, w in pairs:
        if nt:
            in_specs.append(pl.BlockSpec((tn, w.shape[1]), lambda i, j: (j, 0)))
        else:
            in_specs.append(pl.BlockSpec((w.shape[0], tn), lambda i, j: (0, j)))
    for e, col_off in extras:
        off = col_off // tn
        if e.shape[0] == 1:
            in_specs.append(pl.BlockSpec((1, tn), lambda i, j, off=off: (0, j + off)))
        else:
            in_specs.append(pl.BlockSpec((tm, tn), lambda i, j, off=off: (i, j + off)))
    in_specs += [_ANY_SPEC] * nd
    if out_cols is None:
        out_cols = [n_cols] * no
    else:
        assert tn == n_cols, "outputs of other widths need the whole row in one block"
    assert not row_sums or tn == n_cols
    out_specs = [pl.BlockSpec((tm, tn * c // n_cols), lambda i, j: (i, j)) for c in out_cols]
    out_specs += [pl.BlockSpec((1, tn), lambda i, j: (0, j))] * row_sums
    out_shape = [jax.ShapeDtypeStruct((rows, c), dt) for c, dt in zip(out_cols, out_dtypes)]
    out_shape += [jax.ShapeDtypeStruct((1, n_cols), F32)] * row_sums
    outs = pl.pallas_call(
        body, name=name, grid=(rows // tm, n_cols // tn),
        in_specs=in_specs, out_specs=out_specs, out_shape=out_shape,
        compiler_params=_cparams("arbitrary" if row_sums else "parallel", "arbitrary"),
    )(*[a for a, _ in pairs], *[w for _, w in pairs], *[e for e, _ in extras], *deps)
    return outs


def _tn_rows(m):
    return max(b for b in range(128, min(m, 1408) + 1, 128) if m % b == 0)


def _mm_tn(name, a, b, scale=1.0, bm=None, tk=1024, deps=(), out_dtype=F32):
    rows, m = a.shape
    n = b.shape[1]
    bm = _tn_rows(m) if bm is None else bm
    tk = min(tk, rows)
    nk = rows // tk

    def body(a_ref, b_ref, *rest):
        o_ref, acc_ref = rest[-2:]
        k = pl.program_id(1)

        @pl.when(k == 0)
        def _():
            acc_ref[...] = jnp.zeros_like(acc_ref)

        acc_ref[...] += _dot_tn(a_ref[...].astype(BF16), b_ref[...].astype(BF16))

        @pl.when(k == nk - 1)
        def _():
            o_ref[...] = (acc_ref[...] * scale).astype(o_ref.dtype)

    return pl.pallas_call(
        body, name=name, grid=(m // bm, nk),
        in_specs=[pl.BlockSpec((tk, bm), lambda i, k: (k, i)), pl.BlockSpec((tk, n), lambda i, k: (k, 0))]
        + [_ANY_SPEC] * len(deps),
        out_specs=pl.BlockSpec((bm, n), lambda i, k: (i, 0)),
        out_shape=jax.ShapeDtypeStruct((m, n), out_dtype),
        scratch_shapes=[pltpu.VMEM((bm, n), F32)],
        compiler_params=_cparams("parallel", "arbitrary"),
    )(a, b, *deps)


def _mm_tn_stack(name, a_list, b, tk=1024, out_dtype=F32):
    rows, n = b.shape
    ms = [a.shape[1] for a in a_list]
    tk = min(tk, rows)
    nk = rows // tk
    na = len(a_list)

    def body(*refs):
        a_refs, b_ref, o_ref, acc_ref = refs[:na], refs[na], refs[na + 1], refs[na + 2]
        k = pl.program_id(0)

        @pl.when(k == 0)
        def _():
            acc_ref[...] = jnp.zeros_like(acc_ref)

        bv = b_ref[...].astype(BF16)
        r0 = 0
        for a_ref, m in zip(a_refs, ms):
            acc_ref[r0:r0 + m, :] += _dot_tn(a_ref[...].astype(BF16), bv)
            r0 += m

        @pl.when(k == nk - 1)
        def _():
            o_ref[...] = acc_ref[...].astype(o_ref.dtype)

    return pl.pallas_call(
        body, name=name, grid=(nk,),
        in_specs=[pl.BlockSpec((tk, m), lambda k: (k, 0)) for m in ms] + [pl.BlockSpec((tk, n), lambda k: (k, 0))],
        out_specs=pl.BlockSpec((sum(ms), n), lambda k: (0, 0)),
        out_shape=jax.ShapeDtypeStruct((sum(ms), n), out_dtype),
        scratch_shapes=[pltpu.VMEM((sum(ms), n), F32)],
        compiler_params=_cparams("arbitrary"),
    )(*a_list, b)


def _colsum(name, xs, tm=512):
    rows, cols = xs.shape
    tm = min(tm, rows)

    def body(x_ref, o_ref):
        @pl.when(pl.program_id(0) == 0)
        def _():
            o_ref[...] = jnp.zeros_like(o_ref)

        o_ref[...] += jnp.sum(x_ref[...].astype(F32), axis=0, keepdims=True)

    return pl.pallas_call(
        body, name=name, grid=(rows // tm,),
        in_specs=[pl.BlockSpec((tm, cols), lambda i: (i, 0))],
        out_specs=pl.BlockSpec((1, cols), lambda i: (0, 0)),
        out_shape=jax.ShapeDtypeStruct((1, cols), F32),
        compiler_params=_cparams("arbitrary"),
    )(xs)


def _ew(name, fn, ins, out_cols, out_dtypes, tm=512):
    rows = ins[0].shape[0]
    tm = min(tm, rows)
    ni = len(ins)

    def body(*refs):
        outs = fn(*[r[...] for r in refs[:ni]])
        for o_ref, o in zip(refs[ni:], outs):
            o_ref[...] = o.astype(o_ref.dtype)

    def spec(shape):
        if shape[0] == 1:
            return pl.BlockSpec((1, shape[1]), lambda i: (0, 0))
        return pl.BlockSpec((tm, shape[1]), lambda i: (i, 0))

    return pl.pallas_call(
        body, name=name, grid=(rows // tm,),
        in_specs=[spec(a.shape) for a in ins],
        out_specs=[pl.BlockSpec((tm, c), lambda i: (i, 0)) for c in out_cols],
        out_shape=[jax.ShapeDtypeStruct((rows, c), dt) for c, dt in zip(out_cols, out_dtypes)],
        compiler_params=_cparams("parallel"),
    )(*ins)


def _rms_parts(xv):
    r = lax.rsqrt(jnp.mean(xv * xv, axis=-1, keepdims=True) + EPS)
    return r, xv * r


def _rms_bwd_dx(dh, gain, r, xh):
    dxh = dh * gain
    return r * (dxh - xh * jnp.mean(dxh * xh, axis=-1, keepdims=True))


def _ffn_chunks(f_all):
    return [slice(c, min(c + FFN_CHUNK, f_all)) for c in range(0, f_all, FFN_CHUNK)]


def _loss_head(xo, gain_f, target, d):
    r, xh = _rms_parts(xo)
    err = xh * gain_f - target
    dy = err * (1.0 / d)
    per_tok = jnp.mean(err * err, axis=-1, keepdims=True)
    return (_rms_bwd_dx(dy, gain_f, r, xh), jnp.sum(dy * xh, axis=0, keepdims=True),
            0.5 * jnp.sum(per_tok, axis=0, keepdims=True))


def _ffn_tile(x_ref, g_ref, wg_ref, wu_ref, wd_ref, h_ref, gg_ref, uu_ref):
    xv = x_ref[...]
    _, xh = _rms_parts(xv)
    h = (xh * g_ref[...]).astype(BF16)
    h_ref[...] = h
    acc = None
    for cols in _ffn_chunks(wd_ref.shape[0]):
        gg = _dot_nt(h, wg_ref[cols, :])
        uu = _dot_nt(h, wu_ref[cols, :])
        act = gg * _sigmoid(gg) * uu
        part = _dot_nn(act.astype(BF16), wd_ref[cols, :])
        acc = part if acc is None else acc + part
        gg_ref[:, cols] = gg.astype(BF16)
        uu_ref[:, cols] = uu.astype(BF16)
    return xv + 0.5 * acc


def _ffn_fwd(name, xs, gain, wg_t, wu_t, wd, next_gain, tm=512, deps=()):
    rows, d = xs.shape
    f_all = wd.shape[0]
    tm = min(tm, rows)

    def body(x_ref, g_ref, wg_ref, wu_ref, wd_ref, ng_ref, *rest):
        xo_ref, h_ref, gg_ref, uu_ref, hn_ref = rest[-5:]
        xo = _ffn_tile(x_ref, g_ref, wg_ref, wu_ref, wd_ref, h_ref, gg_ref, uu_ref)
        xo_ref[...] = xo
        hn_ref[...] = (_rms_parts(xo)[1] * ng_ref[...]).astype(BF16)

    tile = pl.BlockSpec((tm, d), lambda i: (i, 0))
    row = pl.BlockSpec((1, d), lambda i: (0, 0))
    wspec = pl.BlockSpec((f_all, d), lambda i: (0, 0), pipeline_mode=pl.Buffered(1))
    hid = pl.BlockSpec((tm, f_all), lambda i: (i, 0))
    return pl.pallas_call(
        body, name=name, grid=(rows // tm,),
        in_specs=[tile, row, wspec, wspec, wspec, row] + [_ANY_SPEC] * len(deps),
        out_specs=[tile, tile, hid, hid, tile],
        out_shape=[jax.ShapeDtypeStruct((rows, d), F32), jax.ShapeDtypeStruct((rows, d), BF16),
                   jax.ShapeDtypeStruct((rows, f_all), BF16), jax.ShapeDtypeStruct((rows, f_all), BF16),
                   jax.ShapeDtypeStruct((rows, d), BF16)],
        compiler_params=_cparams("parallel"),
    )(xs, gain, wg_t, wu_t, wd, next_gain, *deps)


def _ffn_fwd_head(name, xs, gain, wg_t, wu_t, wd, gain_f, target, tm=512):
    rows, d = xs.shape
    f_all = wd.shape[0]
    tm = min(tm, rows)

    def body(x_ref, g_ref, wg_ref, wu_ref, wd_ref, gf_ref, t_ref, dxo_ref, h_ref, gg_ref, uu_ref, dgf_ref, loss_ref):
        xo = _ffn_tile(x_ref, g_ref, wg_ref, wu_ref, wd_ref, h_ref, gg_ref, uu_ref)
        dxo, dgf, loss = _loss_head(xo, gf_ref[...], t_ref[...], d)
        dxo_ref[...] = dxo

        @pl.when(pl.program_id(0) == 0)
        def _():
            dgf_ref[...] = jnp.zeros_like(dgf_ref)
            loss_ref[...] = jnp.zeros_like(loss_ref)

        dgf_ref[...] += dgf
        loss_ref[...] += loss

    tile = pl.BlockSpec((tm, d), lambda i: (i, 0))
    row = pl.BlockSpec((1, d), lambda i: (0, 0))
    wspec = pl.BlockSpec((f_all, d), lambda i: (0, 0), pipeline_mode=pl.Buffered(1))
    hid = pl.BlockSpec((tm, f_all), lambda i: (i, 0))
    return pl.pallas_call(
        body, name=name, grid=(rows // tm,),
        in_specs=[tile, row, wspec, wspec, wspec, row, tile],
        out_specs=[tile, tile, hid, hid, row, pl.BlockSpec((1, 1), lambda i: (0, 0))],
        out_shape=[jax.ShapeDtypeStruct((rows, d), F32), jax.ShapeDtypeStruct((rows, d), BF16),
                   jax.ShapeDtypeStruct((rows, f_all), BF16), jax.ShapeDtypeStruct((rows, f_all), BF16),
                   jax.ShapeDtypeStruct((1, d), F32), jax.ShapeDtypeStruct((1, 1), F32)],
        compiler_params=_cparams("arbitrary"),
    )(xs, gain, wg_t, wu_t, wd, gain_f, target)


def _ffn_bwd(name, dxo, xs, gain, gg_all, uu_all, wg_t, wu_t, wd, tm=256):
    rows, d = xs.shape
    f_all = wd.shape[0]
    tm = min(tm, rows)

    def body(dxo_ref, x_ref, g_ref, gg_ref, uu_ref, wg_ref, wu_ref, wd_ref,
             dx_ref, dgg_ref, duu_ref, act_ref, dgain_ref):
        dxo = dxo_ref[...]
        df = (0.5 * dxo).astype(BF16)
        dh = None
        for cols in _ffn_chunks(f_all):
            gg = gg_ref[:, cols].astype(F32)
            uu = uu_ref[:, cols].astype(F32)
            sg = _sigmoid(gg)
            silu = gg * sg
            dact = _dot_nt(df, wd_ref[cols, :])
            duu = (dact * silu).astype(BF16)
            dgg = (dact * uu * (sg * (1.0 + gg * (1.0 - sg)))).astype(BF16)
            act_ref[:, cols] = (silu * uu).astype(BF16)
            dgg_ref[:, cols] = dgg
            duu_ref[:, cols] = duu
            part = _dot_nn(dgg, wg_ref[cols, :]) + _dot_nn(duu, wu_ref[cols, :])
            dh = part if dh is None else dh + part
        r, xh = _rms_parts(x_ref[...])
        dx_ref[...] = dxo + _rms_bwd_dx(dh, g_ref[...], r, xh)

        @pl.when(pl.program_id(0) == 0)
        def _():
            dgain_ref[...] = jnp.zeros_like(dgain_ref)

        dgain_ref[...] += jnp.sum(dh * xh, axis=0, keepdims=True)

    tile = pl.BlockSpec((tm, d), lambda i: (i, 0))
    row = pl.BlockSpec((1, d), lambda i: (0, 0))
    wspec = pl.BlockSpec((f_all, d), lambda i: (0, 0), pipeline_mode=pl.Buffered(1))
    hid = pl.BlockSpec((tm, f_all), lambda i: (i, 0))
    hid_shape = jax.ShapeDtypeStruct((rows, f_all), BF16)
    return pl.pallas_call(
        body, name=name, grid=(rows // tm,),
        in_specs=[tile, tile, row, hid, hid, wspec, wspec, wspec],
        out_specs=[tile, hid, hid, hid, row],
        out_shape=[jax.ShapeDtypeStruct((rows, d), F32), hid_shape, hid_shape, hid_shape,
                   jax.ShapeDtypeStruct((1, d), F32)],
        compiler_params=_cparams("arbitrary"),
    )(dxo, xs, gain, gg_all, uu_all, wg_t, wu_t, wd)


def _t5_bucket_np(dist):
    max_exact = N_BUCKETS // 2
    dd = np.maximum(dist, 1).astype(np.float32)
    large = max_exact + (np.log(dd / np.float32(max_exact)) / np.float32(math.log(MAX_DISTANCE / max_exact))
                         * np.float32(N_BUCKETS - max_exact)).astype(np.int32)
    large = np.minimum(large, N_BUCKETS - 1)
    return np.where(dist < max_exact, dist, large).astype(np.int32)


def _attn_geometry(g, rows):
    run = rows // 16
    dil = DILATIONS[g]
    if dil == 16:
        bq = BLOCK
        return dict(view=(16, run), block=(None, bq), grid=(16, run // bq), index=lambda r, n: (r, n),
                    pos=np.arange(bq), bq=bq)
    if dil == 4:
        per = BLOCK // 4
        pos = (4 * np.arange(per)[None, :] + np.arange(4)[:, None]).reshape(-1)
        return dict(view=(4, 4, run), block=(4, None, per), grid=(4, run // per), index=lambda r, n: (0, r, n),
                    pos=pos, bq=BLOCK)
    per = 16
    pos = (16 * np.arange(per)[None, :] + np.arange(16)[:, None]).reshape(-1)
    return dict(view=(16, run), block=(16, per), grid=(1, run // per), index=lambda r, n: (0, n),
                pos=pos, bq=16 * per)


def _attn_tables(g, rows):
    geo = _attn_geometry(g, rows)
    pos, bq = geo["pos"], geo["bq"]
    steps = pos[:, None] - np.concatenate([pos - bq, pos])[None, :]
    valid = (steps >= 0) & (steps <= BLOCK)
    bucket = _t5_bucket_np((np.maximum(steps, 0) * DILATIONS[g]).astype(np.int32))
    return bucket, valid.astype(np.int32)


def _bias_fwd(name, bucket, valid, table_t):
    bq = bucket.shape[0]

    def body(bk_ref, ok_ref, tab_ref, o_ref):
        bk = bk_ref[...]
        ok = ok_ref[...] > 0
        accs = [jnp.zeros(bk.shape, F32)] * HEADS_PER_GROUP
        for b in range(N_BUCKETS):
            hit = bk == b
            accs = [jnp.where(hit, tab_ref[h, b], acc) for h, acc in enumerate(accs)]
        for h, acc in enumerate(accs):
            o_ref[h] = jnp.where(ok, acc, NEG_INF)

    vm = pl.BlockSpec(memory_space=pltpu.VMEM)
    return pl.pallas_call(
        body, name=name, in_specs=[vm, vm, pl.BlockSpec(memory_space=pltpu.SMEM)], out_specs=vm,
        out_shape=jax.ShapeDtypeStruct((HEADS_PER_GROUP, bq, 2 * bq), F32),
    )(bucket, valid, table_t)


def _bias_bwd(name, bucket, dbias):
    def body(bk_ref, db_ref, o_ref):
        row_id = lax.broadcasted_iota(jnp.int32, (N_BUCKETS, 128), 0)
        col_id = lax.broadcasted_iota(jnp.int32, (N_BUCKETS, 128), 1)
        bk = bk_ref[...]
        acc = jnp.zeros((N_BUCKETS, 128), F32)
        for h in range(HEADS_PER_GROUP):
            db = db_ref[h]
            for b in range(N_BUCKETS):
                part = jnp.sum(jnp.where(bk == b, db, 0.0), axis=0, keepdims=True)
                tot = jnp.sum(part, axis=1, keepdims=True)
                acc = jnp.where((row_id == b) & (col_id == h), tot, acc)
        o_ref[...] = acc

    vm = pl.BlockSpec(memory_space=pltpu.VMEM)
    return pl.pallas_call(body, name=name, in_specs=[vm, vm], out_specs=vm,
                          out_shape=jax.ShapeDtypeStruct((N_BUCKETS, 128), F32))(bucket, dbias)


def _head_of_lane(nrows):
    return lax.broadcasted_iota(jnp.int32, (nrows, ATTN_OUT), 1) // HEAD_DIM


def _stack_heads(a, lane_head):
    zero = jnp.zeros_like(a)
    return jnp.concatenate([jnp.where(lane_head == h, a, zero) for h in range(HEADS_PER_GROUP)], axis=0)


def _unstack_heads(a4, lane_head, bq):
    out = a4[:bq]
    for h in range(1, HEADS_PER_GROUP):
        out = jnp.where(lane_head == h, a4[h * bq:(h + 1) * bq], out)
    return out


def _attn_specs(geo, cols, col_block, index):
    return pl.BlockSpec(geo["block"] + (cols,), lambda r, n: index(r, n) + (col_block,))


def _attn_fwd(name, qkv, g, bias4):
    rows = qkv.shape[0]
    geo = _attn_geometry(g, rows)
    bq, (nsub, nb), index = geo["bq"], geo["grid"], geo["index"]
    blk_shape = tuple(b for b in geo["block"] if b is not None) + (ATTN_OUT,)

    def body(q_ref, kc_ref, kp_ref, vc_ref, vp_ref, b_ref, o_ref, lse_ref):
        n = pl.program_id(1)
        lane_head = _head_of_lane(bq)
        flat = lambda ref: ref[...].reshape(bq, ATTN_OUT)
        q4 = _stack_heads(flat(q_ref), lane_head)
        k2 = jnp.concatenate([flat(kp_ref), flat(kc_ref)], axis=0)
        v2 = jnp.concatenate([flat(vp_ref), flat(vc_ref)], axis=0)
        s = _dot_nt(q4, k2) + b_ref[...]
        col = lax.broadcasted_iota(jnp.int32, s.shape, 1)
        s = jnp.where((col >= bq) | (n > 0), s, NEG_INF)
        mx = jnp.max(s, axis=-1, keepdims=True)
        p = jnp.exp(s - mx)
        den = jnp.sum(p, axis=-1, keepdims=True)
        o4 = _dot_nn(p.astype(BF16), v2) / den
        lse4 = jnp.broadcast_to(mx + jnp.log(den), (HEADS_PER_GROUP * bq, ATTN_OUT))
        o_ref[...] = _unstack_heads(o4, lane_head, bq).reshape(blk_shape)
        lse_ref[...] = _unstack_heads(lse4, lane_head, bq).reshape(blk_shape)

    prev = lambda r, n: index(r, jnp.maximum(n - 1, 0))
    view = lambda a: a.reshape(geo["view"] + (a.shape[1],))
    qkv_v = view(qkv)
    out_spec = _attn_specs(geo, ATTN_OUT, 0, index)
    out_shape = jax.ShapeDtypeStruct(geo["view"] + (ATTN_OUT,), F32)
    o, lse = pl.pallas_call(
        body, name=name, grid=(nsub, nb),
        in_specs=[_attn_specs(geo, ATTN_OUT, g, index), _attn_specs(geo, ATTN_OUT, 3 + g, index),
                  _attn_specs(geo, ATTN_OUT, 3 + g, prev), _attn_specs(geo, ATTN_OUT, 6 + g, index),
                  _attn_specs(geo, ATTN_OUT, 6 + g, prev), pl.BlockSpec(bias4.shape, lambda r, n: (0, 0))],
        out_specs=[out_spec, out_spec], out_shape=[out_shape, out_shape],
        compiler_params=_cparams("parallel", "arbitrary"),
    )(qkv_v, qkv_v, qkv_v, qkv_v, qkv_v, bias4)
    return o.reshape(rows, ATTN_OUT), lse.reshape(rows, ATTN_OUT)


def _attn_bwd(name, qkv, do, lse, cvec, g, bias4):
    rows = qkv.shape[0]
    geo = _attn_geometry(g, rows)
    bq, (nsub, nb), index = geo["bq"], geo["grid"], geo["index"]
    blk_shape = tuple(b for b in geo["block"] if b is not None) + (ATTN_OUT,)
    nlead = len(blk_shape) - 1

    def body(q_ref, kc_ref, kp_ref, vc_ref, vp_ref, do_ref, lse_ref, c_ref, b_ref,
             dq_ref, dk_ref, dv_ref, db_ref, kcar_ref, vcar_ref):
        r, n = pl.program_id(0), pl.program_id(1)
        valid = n < nb
        lane_head = _head_of_lane(bq)
        flat = lambda ref: ref[...].reshape(bq, ATTN_OUT)

        @pl.when((r == 0) & (n == 0))
        def _():
            kcar_ref[...] = jnp.zeros_like(kcar_ref)
            vcar_ref[...] = jnp.zeros_like(vcar_ref)
            db_ref[...] = jnp.zeros_like(db_ref)

        def column(ref, h):
            lead = (slice(None),) * nlead
            return ref[lead + (pl.ds(h * HEAD_DIM, 1),)].reshape(bq, 1)

        q4 = _stack_heads(flat(q_ref), lane_head)
        do4 = _stack_heads(flat(do_ref), lane_head)
        k2 = jnp.concatenate([flat(kp_ref), flat(kc_ref)], axis=0)
        v2 = jnp.concatenate([flat(vp_ref), flat(vc_ref)], axis=0)
        lse4 = jnp.concatenate([column(lse_ref, h) for h in range(HEADS_PER_GROUP)], axis=0)
        c4 = jnp.concatenate([column(c_ref, h) for h in range(HEADS_PER_GROUP)], axis=0)
        s = _dot_nt(q4, k2) + b_ref[...]
        col = lax.broadcasted_iota(jnp.int32, s.shape, 1)
        keep = ((col >= bq) | (n > 0)) & valid
        p = jnp.where(keep, jnp.exp(s - lse4), 0.0)
        ds = p * (_dot_nt(do4, v2) + c4)
        ds_b = ds.astype(BF16)

        @pl.when(valid)
        def _():
            dq = _unstack_heads(_dot_nn(ds_b, k2), lane_head, bq) * (HEAD_DIM ** -0.5)
            dq_ref[...] = dq.astype(BF16).reshape(blk_shape)

        dk2 = _dot_tn(ds_b, q4)
        dv2 = _dot_tn(p.astype(BF16), do4)
        dk_ref[...] = (kcar_ref[...] + dk2[:bq]).astype(BF16).reshape(blk_shape)
        dv_ref[...] = (vcar_ref[...] + dv2[:bq]).astype(BF16).reshape(blk_shape)
        kcar_ref[...] = dk2[bq:]
        vcar_ref[...] = dv2[bq:]
        db_ref[...] += ds

    cur = lambda r, n: index(r, jnp.minimum(n, nb - 1))
    prev = lambda r, n: index(r, jnp.maximum(jnp.minimum(n, nb - 1) - 1, 0))
    late = lambda r, n: index(r, jnp.maximum(n - 1, 0))
    view = lambda a: a.reshape(geo["view"] + (a.shape[1],))
    qkv_v = view(qkv)
    tile = _attn_specs(geo, ATTN_OUT, 0, cur)
    bias_spec = pl.BlockSpec(bias4.shape, lambda r, n: (0, 0))
    out_shape = jax.ShapeDtypeStruct(geo["view"] + (ATTN_OUT,), BF16)
    dq, dk, dv, db = pl.pallas_call(
        body, name=name, grid=(nsub, nb + 1),
        in_specs=[_attn_specs(geo, ATTN_OUT, g, cur), _attn_specs(geo, ATTN_OUT, 3 + g, cur),
                  _attn_specs(geo, ATTN_OUT, 3 + g, prev), _attn_specs(geo, ATTN_OUT, 6 + g, cur),
                  _attn_specs(geo, ATTN_OUT, 6 + g, prev), tile, tile, tile, bias_spec],
        out_specs=[tile, _attn_specs(geo, ATTN_OUT, 0, late), _attn_specs(geo, ATTN_OUT, 0, late), bias_spec],
        out_shape=[out_shape, out_shape, out_shape, jax.ShapeDtypeStruct(bias4.shape, F32)],
        scratch_shapes=[pltpu.VMEM((bq, ATTN_OUT), F32), pltpu.VMEM((bq, ATTN_OUT), F32)],
        compiler_params=_cparams("arbitrary", "arbitrary"),
    )(qkv_v, qkv_v, qkv_v, qkv_v, qkv_v, view(do), view(lse), view(cvec), bias4)
    return dq.reshape(rows, ATTN_OUT), dk.reshape(rows, ATTN_OUT), dv.reshape(rows, ATTN_OUT), db


def _group_weights(lses):
    mx = jnp.maximum(jnp.maximum(lses[0], lses[1]), lses[2])
    es = [jnp.exp(l - mx) for l in lses]
    den = es[0] + es[1] + es[2]
    return [e / den for e in es]


def _combine_fwd(name, os_, lses):
    def fn(o0, o1, o2, l0, l1, l2):
        ws = _group_weights([l0, l1, l2])
        out = ws[0] * o0 + ws[1] * o1 + ws[2] * o2
        return out, out

    return _ew(name, fn, [*os_, *lses], [ATTN_OUT, ATTN_OUT], [F32, BF16], tm=1024)


def _combine_bwd(name, do, oa, lses):
    def fn(dov, oav, l0, l1, l2):
        head_sum = (lax.broadcasted_iota(jnp.int32, (ATTN_OUT, ATTN_OUT), 0) // HEAD_DIM
                    == lax.broadcasted_iota(jnp.int32, (ATTN_OUT, ATTN_OUT), 1) // HEAD_DIM)
        ws = _group_weights([l0, l1, l2])
        prod = dov * oav
        hi = prod.astype(BF16)
        lo = (prod - hi.astype(F32)).astype(BF16)
        ones = jnp.where(head_sum, 1.0, 0.0).astype(BF16)
        bar = _dot_nn(hi, ones) + _dot_nn(lo, ones)
        return tuple(w * dov for w in ws) + tuple(-w * bar for w in ws)

    return _ew(name, fn, [do, oa, *lses], [ATTN_OUT] * 6, [BF16] * 3 + [F32] * 3, tm=1024)


def _ssm_disc(a_re, a_im, log_dt, b_re, b_im):
    dt = jnp.exp(log_dt)
    mag = jnp.exp(a_re * dt)
    ab_re = mag * jnp.cos(a_im * dt)
    ab_im = mag * jnp.sin(a_im * dt)
    den = a_re * a_re + a_im * a_im
    xr = ab_re - 1.0
    coef_re = (xr * a_re + ab_im * a_im) / den
    coef_im = (ab_im * a_re - xr * a_im) / den
    bb_re = coef_re[None] * b_re - coef_im[None] * b_im
    bb_im = coef_re[None] * b_im + coef_im[None] * b_re
    return ab_re, ab_im, bb_re, bb_im


def _ssm_params_fwd(name, a_re, a_im, log_dt, b_re, b_im, c_re, c_im):
    pows = jax.ShapeDtypeStruct((SCAN_STEPS,) + a_re.shape, F32)
    mats = jax.ShapeDtypeStruct((SSM_PAIRS, PAIR_TILE, PAIR_TILE), BF16)
    per_tile = PAIR_TILE // (2 * SSM_GROUP)

    def body(ar, ai, ld, br, bi, cr, ci, o_pr, o_pi, o_bb, o_c, bbr_ref, bbi_ref, wide_ref):
        ab_re, ab_im, bb_re, bb_im = _ssm_disc(ar[...], ai[...], ld[...], br[...], bi[...])
        pr, pi = ab_re, ab_im
        for j in range(SCAN_STEPS):
            o_pr[j] = pr
            o_pi[j] = pi
            pr, pi = pr * ab_re - pi * ab_im, pr * ab_im + pi * ab_re
        bbr_ref[...] = bb_re
        bbi_ref[...] = bb_im

        def place(out_ref, block):
            wide_ref[...] = jnp.zeros_like(wide_ref)
            for g in range(SSM_GROUPS):
                p, l = divmod(g, 2)
                rows = pl.ds((p % per_tile) * 2 * SSM_GROUP + l * SSM_GROUP, SSM_GROUP)
                re, im = block(g)
                wide_ref[p, rows, pl.ds(l * SSM_STATE, SSM_STATE)] = re
                wide_ref[p, rows, pl.ds(PAIR_LANES + l * SSM_STATE, SSM_STATE)] = im
            out_ref[...] = wide_ref[...].astype(BF16)

        place(o_bb, lambda g: (bbr_ref[:, g, :], bbi_ref[:, g, :]))
        place(o_c, lambda g: (cr[g], -ci[g]))

    vm = pl.BlockSpec(memory_space=pltpu.VMEM)
    return pl.pallas_call(
        body, name=name, in_specs=[vm] * 7, out_specs=[vm] * 4, out_shape=[pows, pows, mats, mats],
        scratch_shapes=[pltpu.VMEM(b_re.shape, F32), pltpu.VMEM(b_re.shape, F32),
                        pltpu.VMEM((SSM_PAIRS, PAIR_TILE, PAIR_TILE), F32)],
    )(a_re, a_im, log_dt, b_re, b_im, c_re, c_im)


def _ssm_params_bwd(name, a_re, a_im, log_dt, b_re, b_im, d_ab_re, d_ab_im, d_bb_re, d_bb_im):
    gn = jax.ShapeDtypeStruct(a_re.shape, F32)
    cgn = jax.ShapeDtypeStruct(b_re.shape, F32)

    def body(ar, ai, ld, br, bi, g0, g1, g2, g3, o_ar, o_ai, o_ld, o_br, o_bi):
        _, vjp = jax.vjp(_ssm_disc, ar[...], ai[...], ld[...], br[...], bi[...])
        outs = vjp((g0[...], g1[...], g2[...], g3[...]))
        for o_ref, o in zip((o_ar, o_ai, o_ld, o_br, o_bi), outs):
            o_ref[...] = o

    vm = pl.BlockSpec(memory_space=pltpu.VMEM)
    return pl.pallas_call(body, name=name, in_specs=[vm] * 9, out_specs=[vm] * 5,
                          out_shape=[gn, gn, jax.ShapeDtypeStruct(log_dt.shape, F32), cgn, cgn],
                          )(a_re, a_im, log_dt, b_re, b_im, d_ab_re, d_ab_im, d_bb_re, d_bb_im)


def _scan_block(s_ref, carry_ref, tmp_ref, pw_ref, reverse, sprev=None):
    nl = SSM_LANES
    halves = range(SCAN_COLS // SCAN_SUB)
    zero = jnp.zeros((SCAN_SUB, SCAN_LANES), F32)
    for half in (reversed(halves) if reverse else halves):
        sub_rows = pl.ds(half * SCAN_SUB, SCAN_SUB)
        for lc in range(nl // SCAN_LANES):
            re_l = pl.ds(lc * SCAN_LANES, SCAN_LANES)
            im_l = pl.ds(nl + lc * SCAN_LANES, SCAN_LANES)
            are, aim = pw_ref[0, :, re_l], pw_ref[0, :, im_l]

            def step_of(j):
                return SCAN_STEPS - 1 - j if reverse else j

            def pass1(j, st):
                sr, si = st
                jj = step_of(j)
                nr = are * sr - aim * si + s_ref[jj, sub_rows, re_l]
                ni = are * si + aim * sr + s_ref[jj, sub_rows, im_l]
                s_ref[jj, sub_rows, re_l] = nr
                s_ref[jj, sub_rows, im_l] = ni
                return nr, ni

            er, ei = lax.fori_loop(0, SCAN_STEPS, pass1, (zero, zero), unroll=2)
            tmp_ref[0:SCAN_SUB, re_l] = er
            tmp_ref[0:SCAN_SUB, im_l] = ei
            apr, api = pw_ref[SCAN_STEPS - 1, 0:1, re_l], pw_ref[SCAN_STEPS - 1, 0:1, im_l]
            sr, si = carry_ref[0:1, re_l], carry_ref[0:1, im_l]
            for step in range(SCAN_SUB):
                c = SCAN_SUB - 1 - step if reverse else step
                tmp_ref[SCAN_SUB + c:SCAN_SUB + c + 1, re_l] = sr
                tmp_ref[SCAN_SUB + c:SCAN_SUB + c + 1, im_l] = si
                e_r, e_i = tmp_ref[c:c + 1, re_l], tmp_ref[c:c + 1, im_l]
                sr, si = apr * sr - api * si + e_r, apr * si + api * sr + e_i
            carry_ref[0:1, re_l] = sr
            carry_ref[0:1, im_l] = si
            cr = tmp_ref[SCAN_SUB:2 * SCAN_SUB, re_l]
            ci = tmp_ref[SCAN_SUB:2 * SCAN_SUB, im_l]

            if sprev is None:
                def pass2(j, st):
                    pr, pi = pw_ref[j, :, re_l], pw_ref[j, :, im_l]
                    jj = step_of(j)
                    s_ref[jj, sub_rows, re_l] += pr * cr - pi * ci
                    s_ref[jj, sub_rows, im_l] += pr * ci + pi * cr
                    return st

                lax.fori_loop(0, SCAN_STEPS, pass2, 0, unroll=2)
            else:
                st_ref, prev_ref, have_prev, dab_ref = sprev

                def corrected(jj, pr, pi):
                    gr = s_ref[jj, sub_rows, re_l] + pr * cr - pi * ci
                    gi = s_ref[jj, sub_rows, im_l] + pr * ci + pi * cr
                    s_ref[jj, sub_rows, re_l] = gr
                    s_ref[jj, sub_rows, im_l] = gi
                    return gr, gi

                def pass2(j, st):
                    dr, di = st
                    jj = SCAN_STEPS - 1 - j
                    gr, gi = corrected(jj, pw_ref[j, :, re_l], pw_ref[j, :, im_l])
                    qr, qi = st_ref[jj - 1, sub_rows, re_l], st_ref[jj - 1, sub_rows, im_l]
                    return dr + gr * qr + gi * qi, di + gi * qr - gr * qi

                dr, di = lax.fori_loop(0, SCAN_STEPS - 1, pass2, (zero, zero), unroll=2)
                gr, gi = corrected(0, pw_ref[SCAN_STEPS - 1, :, re_l], pw_ref[SCAN_STEPS - 1, :, im_l])
                sub = lax.broadcasted_iota(jnp.int32, (SCAN_SUB, SCAN_LANES), 0)
                if half == 0:
                    pv_r = prev_ref[SCAN_SUB - 1:SCAN_SUB, re_l] * have_prev
                    pv_i = prev_ref[SCAN_SUB - 1:SCAN_SUB, im_l] * have_prev
                else:
                    before = pl.ds(half * SCAN_SUB - 1, 1)
                    pv_r, pv_i = st_ref[SCAN_STEPS - 1, before, re_l], st_ref[SCAN_STEPS - 1, before, im_l]
                shape = (SCAN_SUB, SCAN_LANES)
                qr = jnp.where(sub == 0, jnp.broadcast_to(pv_r, shape),
                               pltpu.roll(st_ref[SCAN_STEPS - 1, sub_rows, re_l], 1, 0))
                qi = jnp.where(sub == 0, jnp.broadcast_to(pv_i, shape),
                               pltpu.roll(st_ref[SCAN_STEPS - 1, sub_rows, im_l], 1, 0))
                dab_ref[:, re_l] += dr + gr * qr + gi * qi
                dab_ref[:, im_l] += di + gi * qr - gr * qi


def _scan_view(a):
    return a.reshape(16, a.shape[0] // 16, a.shape[1])


def _pair_tile(p):
    start = (p * 2 * SSM_GROUP // PAIR_TILE) * PAIR_TILE
    return slice(start, start + PAIR_TILE)


def _pair_lanes(p):
    return pl.ds(p * PAIR_LANES, PAIR_LANES), pl.ds(SSM_LANES + p * PAIR_LANES, PAIR_LANES)


def _pair_store(s_ref, p, val):
    re_l, im_l = _pair_lanes(p)
    s_ref[:, :, re_l] = val[:, :PAIR_LANES].reshape(16, SCAN_COLS, PAIR_LANES)
    s_ref[:, :, im_l] = val[:, PAIR_LANES:].reshape(16, SCAN_COLS, PAIR_LANES)


def _pair_load(s_ref, p):
    re_l, im_l = _pair_lanes(p)
    parts = [s_ref[:, :, l].reshape(SCAN_BLOCK, PAIR_LANES) for l in (re_l, im_l)]
    return jnp.concatenate(parts, axis=1).astype(BF16)


def _pair_sum(fn):
    per = PAIR_TILE // (2 * SSM_GROUP)
    tiles = []
    for t in range(SSM_PAIRS // per):
        acc = None
        for p in range(t * per, (t + 1) * per):
            part = fn(p)
            acc = part if acc is None else acc + part
        tiles.append(acc)
    return jnp.concatenate(tiles, axis=1)


def _ssm_fwd(name, u, bb_mats, c_mats, pw_rows, d_skip):
    rows = u.shape[0]
    nl2 = 2 * SSM_LANES
    nblk = rows // SCAN_BLOCK

    def body(u_ref, bb_ref, c_ref, pw_ref, d_ref, y_ref, yg_ref, s_ref, carry_ref, tmp_ref):
        @pl.when(pl.program_id(0) == 0)
        def _():
            carry_ref[...] = jnp.zeros_like(carry_ref)

        uv = u_ref[...].reshape(SCAN_BLOCK, SSM_WIDTH)
        ub = uv.astype(BF16)
        for p in range(SSM_PAIRS):
            _pair_store(s_ref, p, _dot_nn(ub[:, _pair_tile(p)], bb_ref[p]))
        _scan_block(s_ref, carry_ref, tmp_ref, pw_ref, reverse=False)
        ys = _pair_sum(lambda p: _dot_nt(_pair_load(s_ref, p), c_ref[p]))
        yv = ys + d_ref[...] * uv
        y_ref[...] = yv.reshape(16, SCAN_COLS, SSM_WIDTH)
        yg_ref[...] = jax.nn.gelu(yv).astype(BF16).reshape(16, SCAN_COLS, SSM_WIDTH)

    const = lambda shape: pl.BlockSpec(shape, lambda i: (0,) * len(shape))
    blk = lambda cols: pl.BlockSpec((16, SCAN_COLS, cols), lambda i: (0, i, 0))
    pair_mats = const((SSM_PAIRS, PAIR_TILE, PAIR_TILE))
    y, yg, s = pl.pallas_call(
        body, name=name, grid=(nblk,),
        in_specs=[blk(SSM_WIDTH), pair_mats, pair_mats, const((SCAN_STEPS, SCAN_SUB, nl2)), const((1, SSM_WIDTH))],
        out_specs=[blk(SSM_WIDTH), blk(SSM_WIDTH), blk(nl2)],
        out_shape=[jax.ShapeDtypeStruct((16, rows // 16, SSM_WIDTH), F32),
                   jax.ShapeDtypeStruct((16, rows // 16, SSM_WIDTH), BF16),
                   jax.ShapeDtypeStruct((16, rows // 16, nl2), F32)],
        scratch_shapes=[pltpu.VMEM((SCAN_SUB, nl2), F32), pltpu.VMEM((2 * SCAN_SUB, nl2), F32)],
        compiler_params=_cparams("arbitrary"),
    )(_scan_view(u), bb_mats, c_mats, pw_rows, d_skip)
    return y.reshape(rows, SSM_WIDTH), yg.reshape(rows, SSM_WIDTH), s.reshape(rows, nl2)


def _ssm_bwd(name, dy, u, states, bb_mats, c_mats, pwc_rows, d_skip):
    rows = u.shape[0]
    nl2 = 2 * SSM_LANES
    nblk = rows // SCAN_BLOCK

    def body(dy_ref, u_ref, st_ref, prev_ref, bb_ref, c_ref, pw_ref, d_ref,
             du_ref, dbb_ref, dc_ref, dab_ref, dd_ref, g_ref, carry_ref, tmp_ref):
        i = pl.program_id(0)

        @pl.when(i == 0)
        def _():
            carry_ref[...] = jnp.zeros_like(carry_ref)
            for ref in (dbb_ref, dc_ref, dab_ref, dd_ref):
                ref[...] = jnp.zeros_like(ref)

        dyv = dy_ref[...].reshape(SCAN_BLOCK, SSM_WIDTH)
        uv = u_ref[...].reshape(SCAN_BLOCK, SSM_WIDTH)
        dyb, ub = dyv.astype(BF16), uv.astype(BF16)
        for p in range(SSM_PAIRS):
            _pair_store(g_ref, p, _dot_nn(dyb[:, _pair_tile(p)], c_ref[p]))
        have_prev = (i < nblk - 1).astype(F32)
        _scan_block(g_ref, carry_ref, tmp_ref, pw_ref, reverse=True,
                    sprev=(st_ref, prev_ref, have_prev, dab_ref))

        def pair_work(p):
            gp = _pair_load(g_ref, p)
            dbb_ref[p] += _dot_tn(ub[:, _pair_tile(p)], gp)
            dc_ref[p] += _dot_tn(dyb[:, _pair_tile(p)], _pair_load(st_ref, p))
            return _dot_nt(gp, bb_ref[p])

        du_ref[...] = (_pair_sum(pair_work) + d_ref[...] * dyv).reshape(16, SCAN_COLS, SSM_WIDTH)
        dd_ref[...] += jnp.sum(dyv * uv, axis=0, keepdims=True)

    const = lambda shape: pl.BlockSpec(shape, lambda i: (0,) * len(shape))
    blk = lambda cols: pl.BlockSpec((16, SCAN_COLS, cols), lambda i: (0, nblk - 1 - i, 0))
    per8 = SCAN_COLS // SCAN_SUB
    prev_spec = pl.BlockSpec((None, SCAN_SUB, nl2), lambda i: (15, jnp.maximum((nblk - 1 - i) * per8 - 1, 0), 0))
    pair_mats = const((SSM_PAIRS, PAIR_TILE, PAIR_TILE))
    pair_shape = jax.ShapeDtypeStruct((SSM_PAIRS, PAIR_TILE, PAIR_TILE), F32)
    sv = _scan_view(states)
    du, dbb, dc, dab, dd = pl.pallas_call(
        body, name=name, grid=(nblk,),
        in_specs=[blk(SSM_WIDTH), blk(SSM_WIDTH), blk(nl2), prev_spec, pair_mats, pair_mats,
                  const((SCAN_STEPS, SCAN_SUB, nl2)), const((1, SSM_WIDTH))],
        out_specs=[blk(SSM_WIDTH), pair_mats, pair_mats, const((SCAN_SUB, nl2)), const((1, SSM_WIDTH))],
        out_shape=[jax.ShapeDtypeStruct((16, rows // 16, SSM_WIDTH), F32), pair_shape, pair_shape,
                   jax.ShapeDtypeStruct((SCAN_SUB, nl2), F32), jax.ShapeDtypeStruct((1, SSM_WIDTH), F32)],
        scratch_shapes=[pltpu.VMEM((16, SCAN_COLS, nl2), F32), pltpu.VMEM((SCAN_SUB, nl2), F32),
                        pltpu.VMEM((2 * SCAN_SUB, nl2), F32)],
        compiler_params=_cparams("arbitrary"),
    )(_scan_view(dy), _scan_view(u), sv, sv, bb_mats, c_mats, pwc_rows, d_skip)
    return du.reshape(rows, SSM_WIDTH), dbb, dc, dab, dd


def _adamw(name, w, m, v, gparts, tr):
    rows, cols = w.shape

    def body(w_ref, m_ref, v_ref, g_ref, og_ref, od_ref, om_ref, ov_ref):
        g = g_ref[0].astype(F32)
        for i in range(1, N_DEV):
            g = g + g_ref[i].astype(F32)
        m_new = B1 * m_ref[...] + (1.0 - B1) * g
        v_new = B2 * v_ref[...] + (1.0 - B2) * (g * g)
        m_hat = m_new / (1.0 - B1 ** STEP)
        v_hat = v_new / (1.0 - B2 ** STEP)
        og_ref[...] = g
        od_ref[...] = -LR * (m_hat / (jnp.sqrt(v_hat) + ADAM_EPS) + WD * w_ref[...])
        om_ref[...] = m_new
        ov_ref[...] = v_new

    spec = pl.BlockSpec((tr, cols), lambda i: (i, 0))
    shape = jax.ShapeDtypeStruct((rows, cols), F32)
    return pl.pallas_call(
        body, name=name, grid=(rows // tr,),
        in_specs=[spec, spec, spec, pl.BlockSpec((N_DEV, tr, cols), lambda i: (0, i, 0))],
        out_specs=[spec] * 4, out_shape=[shape] * 4,
        compiler_params=_cparams("parallel"),
    )(w, m, v, gparts)


_SHARDED = (
    ("ffn1_w_gate", True, (352, 1024)), ("ffn1_w_up", True, (352, 1024)), ("ffn1_w_down", False, (352, 1024)),
    ("w_in", True, (608, 1024)), ("ssm_w_glu", True, (128, 512)), ("w_attn_branch", True, (128, 256)),
    ("w_ssm_branch", True, (128, 512)), ("w_out", False, (128, 1024)),
    ("ffn2_w_gate", True, (352, 1024)), ("ffn2_w_up", True, (352, 1024)), ("ffn2_w_down", False, (352, 1024)),
)
_SMALL = ("ffn1_norm", "mix_norm", "gate_bias", "rel_bias_table", "ssm_a_re", "ssm_a_im", "ssm_log_dt",
          "ssm_b_re", "ssm_b_im", "ssm_c_re", "ssm_c_im", "ssm_d", "ffn2_norm", "final_norm")
_ORDER = ("ffn1_norm", "ffn1_w_gate", "ffn1_w_up", "ffn1_w_down", "mix_norm", "w_in", "gate_bias",
          "rel_bias_table", "ssm_a_re", "ssm_a_im", "ssm_log_dt", "ssm_b_re", "ssm_b_im", "ssm_c_re",
          "ssm_c_im", "ssm_d", "ssm_w_glu", "w_attn_branch", "w_ssm_branch", "w_out", "ffn2_norm",
          "ffn2_w_gate", "ffn2_w_up", "ffn2_w_down", "final_norm")


def _pack_rows(shape):
    return shape[0] * shape[1] // D_MODEL


_SHARD_INFO = {nm: (tr, shape) for nm, tr, shape in _SHARDED}
_PHASES = {
    "f1gu": ("ffn1_w_gate", "ffn1_w_up"), "f1d": ("ffn1_w_down",),
    "mix": ("w_in", "ssm_w_glu", "w_attn_branch", "w_ssm_branch", "w_out"),
    "f2": ("ffn2_w_gate", "ffn2_w_up", "ffn2_w_down"),
}


def _to_rows(a, nm):
    tr, shape = _SHARD_INFO[nm]
    return (a.T if tr else a).reshape(_pack_rows(shape), D_MODEL)


def _from_rows(p, nm):
    tr, shape = _SHARD_INFO[nm]
    a = p.reshape(shape)
    return a.T if tr else a


def _full_weight(gathered, nm):
    _, shape = _SHARD_INFO[nm]
    return gathered.reshape(N_DEV * shape[0], shape[1])


def _grad_blocks(g, nm):
    _, shape = _SHARD_INFO[nm]
    return g.astype(BF16).reshape(N_DEV, _pack_rows(shape), D_MODEL)


_SMALL_TILE = 8 * 128


def _small_rows(a):
    flat = a.reshape(-1)
    return jnp.pad(flat, (0, (-flat.shape[0]) % _SMALL_TILE)).reshape(-1, 128)


def _pack_small(ws, last=None):
    tail = jnp.zeros((), F32) if last is None else last
    return jnp.concatenate([_small_rows(ws[nm]) for nm in _SMALL] + [_small_rows(tail)], axis=0)


def _unpack_small(pack, like):
    out, r0 = {}, 0
    for nm in _SMALL:
        n = like[nm].size
        nr = 8 * -(-n // _SMALL_TILE)
        out[nm] = pack[r0:r0 + nr].reshape(-1)[:n].reshape(like[nm].shape)
        r0 += nr
    return out


def _residue_order(a):
    rows, cols = a.shape
    return a.reshape(rows // 16, 16, cols).transpose(1, 0, 2).reshape(rows, cols)


def _token_order(a):
    rows, cols = a.shape
    return a.reshape(16, rows // 16, cols).transpose(1, 0, 2).reshape(rows, cols)


_PAIRS_PER_TILE = PAIR_TILE // (2 * SSM_GROUP)
_PAIR_AXES = (SSM_PAIRS // _PAIRS_PER_TILE, _PAIRS_PER_TILE, 2)


def _pair_diagonals(acc):
    k, j, l = _PAIR_AXES
    eight = acc.reshape(k, j, j, l, SSM_GROUP, 2, l, SSM_STATE)
    eye_j, eye_l = jnp.eye(j, dtype=acc.dtype), jnp.eye(l, dtype=acc.dtype)
    own = jnp.einsum("kjJLcxln,jJ,lL->xkjlcn", eight, eye_j, eye_l).reshape(2, SSM_GROUPS, SSM_GROUP, SSM_STATE)
    return own[0], own[1]


def _local_step(xs, target, small, weights_of, send_grads, first_deps=()):
    rows = xs.shape[0]
    gfull, gsmall = {}, {}

    table_t = small["rel_bias_table"].T
    tables, bias4 = [], []
    for g in range(N_GROUPS):
        bucket, valid = [jnp.asarray(t) for t in _attn_tables(g, rows)]
        bias_g = _bias_fwd(f"rel_bias_fwd_{g}", bucket, valid, table_t[g * HEADS_PER_GROUP:(g + 1) * HEADS_PER_GROUP])
        tables.append(bucket)
        bias4.append(bias_g.reshape(-1, bias_g.shape[-1]))
    pw_re, pw_im, bb_mats, c_mats = _ssm_params_fwd(
        "ssm_params_fwd", small["ssm_a_re"], small["ssm_a_im"], small["ssm_log_dt"].reshape(SSM_GROUPS, 1),
        small["ssm_b_re"].transpose(2, 0, 1), small["ssm_b_im"].transpose(2, 0, 1), small["ssm_c_re"], small["ssm_c_im"])

    def power_rows(sign):
        row = jnp.concatenate([pw_re.reshape(SCAN_STEPS, 1, SSM_LANES), sign * pw_im.reshape(SCAN_STEPS, 1, SSM_LANES)],
                              axis=2)
        return jnp.broadcast_to(row, (SCAN_STEPS, SCAN_SUB, 2 * SSM_LANES))

    pw_fwd, pw_bwd = power_rows(1.0), power_rows(-1.0)
    d_skip = small["ssm_d"].reshape(1, SSM_WIDTH)
    wf = dict(weights_of("f1", [xs, target, bb_mats, c_mats, pw_fwd, pw_bwd] + bias4))

    x1, h1, gg1, uu1, hmix = _ffn_fwd("ffn1_fwd", xs, small["ffn1_norm"], wf["ffn1_w_gate"], wf["ffn1_w_up"],
                                      wf["ffn1_w_down"], small["mix_norm"], deps=first_deps)
    wf.update(weights_of("mix", x1))
    w_in = wf["w_in"]
    w_qkv, w_u, w_g = w_in[:3 * ATTN_WIDTH], w_in[3 * ATTN_WIDTH:3 * ATTN_WIDTH + SSM_WIDTH], w_in[3 * ATTN_WIDTH + SSM_WIDTH:]
    qscale = jnp.concatenate([jnp.full((1, ATTN_WIDTH), HEAD_DIM ** -0.5, F32), jnp.ones((1, 2 * ATTN_WIDTH), F32)], axis=1)
    qkv, = _mm("in_qkv", [(hmix, w_qkv)], True, 3 * ATTN_WIDTH, [BF16],
               epilogue=lambda acc, sc: (acc * sc,), extras=[(qscale, 0)], tn=ATTN_WIDTH)
    u, = _mm("in_u", [(hmix, w_u)], True, SSM_WIDTH, [F32])
    gates, = _mm("in_gates", [(hmix, w_g)], True, 2 * D_MODEL, [BF16],
                 epilogue=lambda acc, b: (_sigmoid(acc + b),), extras=[(small["gate_bias"], 0)])

    o_g, lse_g = [], []
    for g in range(N_GROUPS):
        o, lse = _attn_fwd(f"attn_fwd_{g}", qkv, g, bias4[g])
        o_g.append(o)
        lse_g.append(lse)
    oa_f32, oa = _combine_fwd("attn_combine_fwd", o_g, lse_g)
    y_attn, = _mm("attn_branch", [(oa, wf["w_attn_branch"])], True, D_MODEL, [BF16])
    y_raw, ygelu, states = _ssm_fwd("ssm_fwd", u, bb_mats, c_mats, pw_fwd, d_skip)
    glu, ysg = _mm("ssm_glu", [(ygelu, wf["ssm_w_glu"])], True, 2 * SSM_WIDTH, [F32, BF16],
                   epilogue=lambda gv: (gv, gv[:, :SSM_WIDTH] * _sigmoid(gv[:, SSM_WIDTH:])),
                   tn=2 * SSM_WIDTH, out_cols=[2 * SSM_WIDTH, SSM_WIDTH])
    y_ssm, merged = _mm("ssm_branch_merge", [(ysg, wf["w_ssm_branch"])], True, D_MODEL, [BF16, BF16],
                        epilogue=lambda acc, ga, gs, ya: (acc, ga * ya + gs * acc),
                        extras=[(gates, 0), (gates, D_MODEL), (y_attn, 0)])
    x2, = _mm("mix_out", [(merged, wf["w_out"])], False, D_MODEL, [F32],
              epilogue=lambda acc, res: (res + acc,), extras=[(x1, 0)])
    wf.update(weights_of("f2", x2))
    dx3, h2, gg2, uu2, gsmall["final_norm"], gsmall["loss"] = _ffn_fwd_head(
        "ffn2_fwd", x2, small["ffn2_norm"], wf["ffn2_w_gate"], wf["ffn2_w_up"], wf["ffn2_w_down"],
        small["final_norm"].reshape(1, D_MODEL), target)

    dx2, dgg2, duu2, act2, gsmall["ffn2_norm"] = _ffn_bwd(
        "ffn2_bwd", dx3, x2, small["ffn2_norm"], gg2, uu2, wf["ffn2_w_gate"], wf["ffn2_w_up"], wf["ffn2_w_down"])
    gfull["ffn2_w_gate"] = _mm_tn("ffn2_dwg", dgg2, h2, out_dtype=BF16, tk=FFN_DW_ROWS)
    gfull["ffn2_w_up"] = _mm_tn("ffn2_dwu", duu2, h2, out_dtype=BF16, tk=FFN_DW_ROWS)
    gfull["ffn2_w_down"] = _mm_tn("ffn2_dwd", act2, dx3, scale=0.5, out_dtype=BF16, tk=FFN_DW_ROWS)
    sent = send_grads("f2", gfull)

    def merge_bwd(dm, ga, gs, ya, ys):
        dza, dzs = dm * ya * ga * (1.0 - ga), dm * ys * gs * (1.0 - gs)
        return (dm * ga, dm * gs, dza, dzs, jnp.sum(dza, axis=0, keepdims=True), jnp.sum(dzs, axis=0, keepdims=True))

    dya, dys, dzga, dzgs, dba, dbs = _mm(
        "mix_out_bwd", [(dx2, wf["w_out"])], True, D_MODEL, [BF16] * 4, epilogue=merge_bwd, row_sums=2,
        extras=[(gates, 0), (gates, D_MODEL), (y_attn, 0), (y_ssm, 0)], deps=sent, tm=512, tn=D_MODEL)
    gfull["w_out"] = _mm_tn("dw_out", merged, dx2, out_dtype=BF16)
    gsmall["gate_bias"] = jnp.concatenate([dba, dbs], axis=1)

    gfull["w_ssm_branch"] = _mm_tn("dw_ssm_branch", dys, ysg, out_dtype=BF16)

    def glu_bwd(dysg, av, bv):
        sb = _sigmoid(bv)
        return (dysg * sb, dysg * av * sb * (1.0 - sb))

    dglu_a, dglu_b = _mm("ssm_branch_bwd", [(dys, wf["w_ssm_branch"])], False, SSM_WIDTH, [BF16, BF16],
                         epilogue=glu_bwd, extras=[(glu, 0), (glu, SSM_WIDTH)])
    w_glu = wf["ssm_w_glu"]
    gfull["ssm_w_glu"] = _mm_tn_stack("dw_glu", [dglu_a, dglu_b], ygelu, out_dtype=BF16)

    def gelu_bwd(acc, yv):
        _, vjp = jax.vjp(jax.nn.gelu, yv)
        return (vjp(acc)[0],)

    dy_raw, = _mm("ssm_glu_bwd", [(dglu_a, w_glu[:SSM_WIDTH]), (dglu_b, w_glu[SSM_WIDTH:])], False, SSM_WIDTH, [F32],
                  epilogue=gelu_bwd, extras=[(y_raw, 0)])
    du, dbb_acc, dc_acc, dab_rows, gsmall_d = _ssm_bwd(
        "ssm_bwd", dy_raw, u, states, bb_mats, c_mats, pw_bwd, d_skip)
    gsmall["ssm_d"] = gsmall_d
    dbb_re, dbb_im = [a.transpose(1, 0, 2) for a in _pair_diagonals(dbb_acc)]
    dc_re, dc_im = _pair_diagonals(dc_acc)
    gsmall["ssm_c_re"], gsmall["ssm_c_im"] = dc_re, -dc_im
    dab = _colsum("ssm_dab", dab_rows)
    d_ar, d_ai, d_ld, d_br, d_bi = _ssm_params_bwd(
        "ssm_params_bwd", small["ssm_a_re"], small["ssm_a_im"], small["ssm_log_dt"].reshape(SSM_GROUPS, 1),
        small["ssm_b_re"].transpose(2, 0, 1), small["ssm_b_im"].transpose(2, 0, 1),
        dab[:, :SSM_LANES].reshape(SSM_GROUPS, SSM_STATE), dab[:, SSM_LANES:].reshape(SSM_GROUPS, SSM_STATE),
        dbb_re, dbb_im)
    gsmall["ssm_a_re"], gsmall["ssm_a_im"], gsmall["ssm_log_dt"] = d_ar, d_ai, d_ld.reshape(SSM_GROUPS)
    gsmall["ssm_b_re"], gsmall["ssm_b_im"] = d_br.transpose(1, 2, 0), d_bi.transpose(1, 2, 0)

    gfull["w_attn_branch"] = _mm_tn("dw_attn_branch", dya, oa, out_dtype=BF16)
    doa, = _mm("attn_branch_bwd", [(dya, wf["w_attn_branch"])], False, ATTN_OUT, [F32])
    dc = _combine_bwd("attn_combine_bwd", doa, oa_f32, lse_g)
    dqkv_cols = [None] * 9
    dtable = []
    for g in range(N_GROUPS):
        dq, dk, dv, db = _attn_bwd(f"attn_bwd_{g}", qkv, dc[g], lse_g[g], dc[3 + g], g, bias4[g])
        dqkv_cols[g], dqkv_cols[3 + g], dqkv_cols[6 + g] = dq, dk, dv
        dt = _bias_bwd(f"rel_bias_bwd_{g}", tables[g], db.reshape(HEADS_PER_GROUP, -1, db.shape[-1]))
        dtable.append(dt[:, :HEADS_PER_GROUP])
    gsmall["rel_bias_table"] = jnp.concatenate(dtable, axis=1)

    gfull["w_in"] = jnp.concatenate([_mm_tn_stack("dw_in_qkv", dqkv_cols, hmix, out_dtype=BF16),
                                     _mm_tn_stack("dw_in_rest", [du, dzga, dzgs], hmix, out_dtype=BF16)], axis=0)
    sent = send_grads("mix", gfull)
    qkv_pairs = [(c, w_qkv[i * ATTN_OUT:(i + 1) * ATTN_OUT]) for i, c in enumerate(dqkv_cols)]

    def mix_norm_bwd(dh, xv, gain, dres):
        r, xh = _rms_parts(xv)
        return dres + _rms_bwd_dx(dh, gain, r, xh), jnp.sum(dh * xh, axis=0, keepdims=True)

    dx1, gsmall["mix_norm"] = _mm(
        "in_bwd", qkv_pairs + [(du, w_u), (dzga, w_g[:D_MODEL]), (dzgs, w_g[D_MODEL:])], False, D_MODEL, [F32],
        epilogue=mix_norm_bwd, row_sums=1, extras=[(x1, 0), (small["mix_norm"], 0), (dx2, 0)], tm=512, tn=D_MODEL,
        deps=sent)

    dx, dgg1, duu1, act1, gsmall["ffn1_norm"] = _ffn_bwd(
        "ffn1_bwd", dx1, xs, small["ffn1_norm"], gg1, uu1, wf["ffn1_w_gate"], wf["ffn1_w_up"], wf["ffn1_w_down"])
    sent = send_grads("small", gsmall)
    gfull["ffn1_w_gate"] = _mm_tn("ffn1_dwg", dgg1, h1, deps=sent, out_dtype=BF16, tk=FFN_DW_ROWS)
    gfull["ffn1_w_up"] = _mm_tn("ffn1_dwu", duu1, h1, out_dtype=BF16, tk=FFN_DW_ROWS)
    sent = send_grads("f1gu", gfull)
    gfull["ffn1_w_down"] = _mm_tn("ffn1_dwd", act1, dx1, scale=0.5, deps=sent, out_dtype=BF16, tk=FFN_DW_ROWS)
    send_grads("f1d", gfull)
    return dx, gsmall


def kernel(x, ffn1_norm, ffn1_w_gate, ffn1_w_up, ffn1_w_down, mix_norm, w_in, gate_bias, rel_bias_table, ssm_a_re, ssm_a_im, ssm_log_dt, ssm_b_re, ssm_b_im, ssm_c_re, ssm_c_im, ssm_d, ssm_w_glu, w_attn_branch, w_ssm_branch, w_out, ffn2_norm, ffn2_w_gate, ffn2_w_up, ffn2_w_down, final_norm, loss_target, m_ffn1_norm, m_ffn1_w_gate, m_ffn1_w_up, m_ffn1_w_down, m_mix_norm, m_w_in, m_gate_bias, m_rel_bias_table, m_ssm_a_re, m_ssm_a_im, m_ssm_log_dt, m_ssm_b_re, m_ssm_b_im, m_ssm_c_re, m_ssm_c_im, m_ssm_d, m_ssm_w_glu, m_w_attn_branch, m_w_ssm_branch, m_w_out, m_ffn2_norm, m_ffn2_w_gate, m_ffn2_w_up, m_ffn2_w_down, m_final_norm, v_ffn1_norm, v_ffn1_w_gate, v_ffn1_w_up, v_ffn1_w_down, v_mix_norm, v_w_in, v_gate_bias, v_rel_bias_table, v_ssm_a_re, v_ssm_a_im, v_ssm_log_dt, v_ssm_b_re, v_ssm_b_im, v_ssm_c_re, v_ssm_c_im, v_ssm_d, v_ssm_w_glu, v_w_attn_branch, v_w_ssm_branch, v_w_out, v_ffn2_norm, v_ffn2_w_gate, v_ffn2_w_up, v_ffn2_w_down, v_final_norm):
    given = dict(locals())
    shapes = {nm: given[nm].shape for nm in _ORDER}

    def strip(a):
        return a[0] if a.ndim >= 2 and a.shape[0] == 1 else a

    w = {nm: strip(given[nm]) for nm in _ORDER}
    m = {nm: strip(given["m_" + nm]) for nm in _ORDER}
    v = {nm: strip(given["v_" + nm]) for nm in _ORDER}
    for d in (w, m, v):
        d["rel_bias_table"] = d["rel_bias_table"].reshape(N_BUCKETS, N_GROUPS * HEADS_PER_GROUP)

    weight_phases = {"f1": _PHASES["f1gu"] + _PHASES["f1d"], "mix": _PHASES["mix"], "f2": _PHASES["f2"]}
    pending_w, w_rows, deps, zero = {}, {}, [], 0.0
    for phase, names in weight_phases.items():
        w_rows.update({nm: _to_rows(w[nm] + zero, nm) for nm in names})
        pending_w[phase] = _exchange_start(f"gather_{phase}_start", [w_rows[nm].astype(BF16) for nm in names],
                                           gather=True, deps=deps)
        deps = [pending_w[phase][4]]
        zero = pending_w["f1"][4][0, 0]
    m_rows = {nm: _to_rows(m[nm] + zero, nm) for nm in _SHARD_INFO}
    v_rows = {nm: _to_rows(v[nm] + zero, nm) for nm in _SHARD_INFO}
    small = {nm: w[nm] for nm in _SMALL}
    small_in = {nm: small[nm] + zero for nm in _SMALL}
    for nm in ("ffn1_norm", "mix_norm", "ffn2_norm", "gate_bias"):
        small_in[nm] = small_in[nm].reshape(1, -1)

    def weights_of(phase, after):
        if phase == "f1":
            after = list(after) + list(m_rows.values()) + list(v_rows.values())
        landed = _exchange_wait(f"gather_{phase}_wait", pending_w[phase], after, gather=True)
        return {nm: _full_weight(got, nm) for nm, got in zip(weight_phases[phase], landed)}

    pending_g = {}

    def send_grads(phase, grads):
        if phase == "small":
            gs_pack = _pack_small({nm: grads[nm].reshape(small[nm].shape) for nm in _SMALL}, last=grads["loss"])
            pending_g[phase] = _exchange_start("gather_small_start", [gs_pack], gather=True)
        else:
            pending_g[phase] = _exchange_start(f"scatter_{phase}_start",
                                               [_grad_blocks(grads[nm], nm) for nm in _PHASES[phase]], gather=False)
        return [pending_g[phase][4]]

    dx, gsmall = _local_step(_residue_order(x[0]), _residue_order(loss_target[0]), small_in, weights_of, send_grads,
                             first_deps=[pending_w["f2"][4]])
    dx = _token_order(dx)

    updated = {}
    after = pending_g["f1d"][4]
    for phase in ("f2", "mix", "small", "f1gu", "f1d"):
        landed = _exchange_wait(f"exchange_{phase}_wait", pending_g[phase], after, gather=phase == "small")
        if phase == "small":
            sm = _adamw("adamw_small", _pack_small(small), _pack_small({nm: m[nm] for nm in _SMALL}),
                        _pack_small({nm: v[nm] for nm in _SMALL}), landed[0], landed[0].shape[1])
            after = sm[0]
            continue
        for nm, recv in zip(_PHASES[phase], landed):
            tr = max(t for t in range(16, 353, 16) if w_rows[nm].shape[0] % t == 0)
            updated[nm] = _adamw(f"adamw_{nm}", w_rows[nm], m_rows[nm], v_rows[nm], recv, tr)
            after = updated[nm][0]

    loss = sm[0][-8, 0]
    outs = []
    for i in range(4):
        sml = _unpack_small(sm[i], small)
        outs.append([(_from_rows(updated[nm][i], nm) if nm in updated else sml[nm]).reshape(shapes[nm])
                     for nm in _ORDER])
    return (loss, dx[None], *outs[0], *outs[1], *outs[2], *outs[3])
```

```python
import math

import numpy as np
import jax
import jax.numpy as jnp
from jax import lax
from jax.experimental import pallas as pl
from jax.experimental.pallas import tpu as pltpu

F32 = jnp.float32
BF16 = jnp.bfloat16

N_DEV = 8
D_MODEL = 1024
HEAD_DIM = 64
HEADS_PER_GROUP = 4
DILATIONS = (1, 4, 16)
N_GROUPS = 3
ATTN_WIDTH = 768
ATTN_OUT = 256
BLOCK = 128
N_BUCKETS = 32
MAX_DISTANCE = 2048
NEG_INF = -1e30
SSM_WIDTH = 512
SSM_GROUPS = 32
SSM_GROUP = 16
SSM_STATE = 64
SSM_LANES = SSM_GROUPS * SSM_STATE
SSM_PAIRS = SSM_GROUPS // 2
PAIR_LANES = 2 * SSM_STATE
PAIR_TILE = 256
EPS = 1e-6
LR, B1, B2, ADAM_EPS, WD, STEP = 0.001, 0.9, 0.999, 1e-08, 0.01, 10

VMEM_LIMIT_BYTES = 56 * 1024 * 1024
FFN_CHUNK = 768
FFN_DW_ROWS = 2048
SCAN_BLOCK = 256
SCAN_STEPS = 16
SCAN_COLS = SCAN_BLOCK // SCAN_STEPS
SCAN_SUB = 8
SCAN_LANES = 512

MESH = pl.DeviceIdType.MESH


def _cparams(*sem):
    return pltpu.CompilerParams(dimension_semantics=sem, vmem_limit_bytes=VMEM_LIMIT_BYTES)


def _dot(a, b, dims):
    return lax.dot_general(a, b, (dims, ((), ())), preferred_element_type=F32)


def _dot_nn(a, b):
    return _dot(a, b, ((1,), (0,)))


def _dot_nt(a, b):
    return _dot(a, b, ((1,), (1,)))


def _dot_tn(a, b):
    return _dot(a, b, ((0,), (0,)))


def _sigmoid(x):
    return 1.0 / (1.0 + jnp.exp(-x))


_HBM_SPEC = pl.BlockSpec(memory_space=pltpu.HBM)
_SEM_SPEC = pl.BlockSpec(memory_space=pltpu.SEMAPHORE)
_ANY_SPEC = pl.BlockSpec(memory_space=pl.ANY)
_EFFECT = pltpu.SideEffectType.DATAFLOW_SIDE_EFFECTING


def _peers(x, y, c):
    return [(1 - x if k & 4 else x, 1 - y if k & 2 else y, 1 - c if k & 1 else c) for k in range(1, N_DEV)]


def _exchange_copies(x_refs, land_refs, send_sems, recv_sems, gather):
    x, y, c = lax.axis_index("x"), lax.axis_index("y"), lax.axis_index("c")
    me = 4 * x + 2 * y + c
    copies = []
    for a, (x_ref, land_ref) in enumerate(zip(x_refs, land_refs)):
        for k, (px, py, pc) in enumerate(_peers(x, y, c)):
            src = x_ref if gather else x_ref.at[4 * px + 2 * py + pc]
            copies.append(pltpu.make_async_remote_copy(
                src_ref=src, dst_ref=land_ref.at[me], send_sem=send_sems.at[N_DEV * a + k],
                recv_sem=recv_sems.at[(N_DEV - 1) * a + k], device_id=(px, py, pc), device_id_type=MESH))
    owns = [pltpu.make_async_copy(x_ref if gather else x_ref.at[me], land_ref.at[me],
                                  send_sems.at[N_DEV * a + N_DEV - 1])
            for a, (x_ref, land_ref) in enumerate(zip(x_refs, land_refs))]
    return owns, copies


def _exchange_start(name, xs_list, gather, deps=()):
    n, nd = len(xs_list), len(deps)
    land_shapes = [(N_DEV, *xs.shape) if gather else xs.shape for xs in xs_list]

    def body(*refs):
        x_refs, land_refs = refs[:n], refs[n:2 * n]
        send_sems, recv_sems = refs[2 * n + nd:2 * n + nd + 2]
        token = refs[-1]
        owns, copies = _exchange_copies(x_refs, land_refs, send_sems, recv_sems, gather)
        for cp in copies + owns:
            cp.start()
        token[...] = jnp.zeros_like(token)

    hbm = lambda a: pltpu.with_memory_space_constraint(a, pltpu.HBM)
    outs = pl.pallas_call(
        body, name=name,
        out_shape=(pltpu.SemaphoreType.DMA((n * N_DEV,)), pltpu.SemaphoreType.DMA((n * (N_DEV - 1),)),
                   *[pltpu.HBM(xs.shape, xs.dtype) for xs in xs_list],
                   *[pltpu.HBM(shape, xs.dtype) for shape, xs in zip(land_shapes, xs_list)],
                   jax.ShapeDtypeStruct((8, 128), F32)),
        in_specs=(_HBM_SPEC,) * (2 * n) + (_ANY_SPEC,) * nd,
        out_specs=(_SEM_SPEC, _SEM_SPEC) + (_HBM_SPEC,) * (2 * n) + (pl.BlockSpec(memory_space=pltpu.VMEM),),
        input_output_aliases={i: 2 + i for i in range(2 * n)},
        compiler_params=pltpu.CompilerParams(has_side_effects=_EFFECT),
    )(*[hbm(xs) for xs in xs_list], *[hbm(lax.empty(shape, xs.dtype)) for shape, xs in zip(land_shapes, xs_list)],
      *deps)
    return outs[0], outs[1], list(outs[2:2 + n]), list(outs[2 + n:2 + 2 * n]), outs[-1]


def _exchange_wait(name, handle, after, gather):
    send_sems, recv_sems, xs_thru, lands_thru, _ = handle
    n = len(xs_thru)
    after = list(after) if isinstance(after, (list, tuple)) else [after]

    def body(*refs):
        x_refs, land_refs = refs[:n], refs[n:2 * n]
        send_sems, recv_sems = refs[2 * n:2 * n + 2]
        owns, copies = _exchange_copies(x_refs, land_refs, send_sems, recv_sems, gather)
        for cp in copies:
            cp.wait_send()
            cp.wait_recv()
        for cp in owns:
            cp.wait()

    outs = pl.pallas_call(
        body, name=name,
        out_shape=tuple(pltpu.HBM(a.shape, a.dtype) for a in xs_thru + lands_thru),
        in_specs=(_HBM_SPEC,) * (2 * n) + (_SEM_SPEC, _SEM_SPEC) + (_ANY_SPEC,) * len(after),
        out_specs=(_HBM_SPEC,) * (2 * n), input_output_aliases={i: i for i in range(2 * n)},
        compiler_params=pltpu.CompilerParams(has_side_effects=_EFFECT),
    )(*xs_thru, *lands_thru, send_sems, recv_sems, *after)
    return list(outs[n:])


def _mm(name, pairs, nt, n_cols, out_dtypes, epilogue=None, extras=(), tm=1024, tn=512, deps=(), row_sums=0,
        out_cols=None):
    rows = pairs[0][0].shape[0]
    tm = min(tm, rows)
    tn = min(tn, n_cols)
    na, ne, nd, no = len(pairs), len(extras), len(deps), len(out_dtypes)

    def body(*refs):
        a_refs, w_refs = refs[:na], refs[na:2 * na]
        e_refs, o_refs = refs[2 * na:2 * na + ne], refs[2 * na + ne + nd:]
        acc = None
        for a_ref, w_ref in zip(a_refs, w_refs):
            a = a_ref[...].astype(BF16)
            w = w_ref[...].astype(BF16)
            p = _dot_nt(a, w) if nt else _dot_nn(a, w)
            acc = p if acc is None else acc + p
        outs = (acc,) if epilogue is None else epilogue(acc, *[e[...].astype(F32) for e in e_refs])
        for o_ref, o in zip(o_refs[:no], outs[:no]):
            o_ref[...] = o.astype(o_ref.dtype)
        for r_ref, o in zip(o_refs[no:], outs[no:]):
            @pl.when(pl.program_id(0) == 0)
            def _():
                r_ref[...] = jnp.zeros_like(r_ref)

            r_ref[...] += o

    in_specs = [pl.BlockSpec((tm, a.shape[1]), lambda i, j: (i, 0)) for a, _ in pairs]
    for _, w in pairs:
        if nt:
            in_specs.append(pl.BlockSpec((tn, w.shape[1]), lambda i, j: (j, 0)))
        else:
            in_specs.append(pl.BlockSpec((w.shape[0], tn), lambda i, j: (0, j)))
    for e, col_off in extras:
        off = col_off // tn
        if e.shape[0] == 1:
            in_specs.append(pl.BlockSpec((1, tn), lambda i, j, off=off: (0, j + off)))
        else:
            in_specs.append(pl.BlockSpec((tm, tn), lambda i, j, off=off: (i, j + off)))
    in_specs += [_ANY_SPEC] * nd
    if out_cols is None:
        out_cols = [n_cols] * no
    else:
        assert tn == n_cols, "outputs of other widths need the whole row in one block"
    assert not row_sums or tn == n_cols
    out_specs = [pl.BlockSpec((tm, tn * c // n_cols), lambda i, j: (i, j)) for c in out_cols]
    out_specs += [pl.BlockSpec((1, tn), lambda i, j: (0, j))] * row_sums
    out_shape = [jax.ShapeDtypeStruct((rows, c), dt) for c, dt in zip(out_cols, out_dtypes)]
    out_shape += [jax.ShapeDtypeStruct((1, n_cols), F32)] * row_sums
    outs = pl.pallas_call(
        body, name=name, grid=(rows // tm, n_cols // tn),
        in_specs=in_specs, out_specs=out_specs, out_shape=out_shape,
        compiler_params=_cparams("arbitrary" if row_sums else "parallel", "arbitrary"),
    )(*[a for a, _ in pairs], *[w for _, w in pairs], *[e for e, _ in extras], *deps)
    return outs


def _tn_rows(m):
    return max(b for b in range(128, min(m, 1408) + 1, 128) if m % b == 0)


def _mm_tn(name, a, b, scale=1.0, bm=None, tk=1024, deps=(), out_dtype=F32):
    rows, m = a.shape
    n = b.shape[1]
    bm = _tn_rows(m) if bm is None else bm
    tk = min(tk, rows)
    nk = rows // tk

    def body(a_ref, b_ref, *rest):
        o_ref, acc_ref = rest[-2:]
        k = pl.program_id(1)

        @pl.when(k == 0)
        def _():
            acc_ref[...] = jnp.zeros_like(acc_ref)

        acc_ref[...] += _dot_tn(a_ref[...].astype(BF16), b_ref[...].astype(BF16))

        @pl.when(k == nk - 1)
        def _():
            o_ref[...] = (acc_ref[...] * scale).astype(o_ref.dtype)

    return pl.pallas_call(
        body, name=name, grid=(m // bm, nk),
        in_specs=[pl.BlockSpec((tk, bm), lambda i, k: (k, i)), pl.BlockSpec((tk, n), lambda i, k: (k, 0))]
        + [_ANY_SPEC] * len(deps),
        out_specs=pl.BlockSpec((bm, n), lambda i, k: (i, 0)),
        out_shape=jax.ShapeDtypeStruct((m, n), out_dtype),
        scratch_shapes=[pltpu.VMEM((bm, n), F32)],
        compiler_params=_cparams("parallel", "arbitrary"),
    )(a, b, *deps)


def _mm_tn_stack(name, a_list, b, tk=1024, out_dtype=F32):
    rows, n = b.shape
    ms = [a.shape[1] for a in a_list]
    tk = min(tk, rows)
    nk = rows // tk
    na = len(a_list)

    def body(*refs):
        a_refs, b_ref, o_ref, acc_ref = refs[:na], refs[na], refs[na + 1], refs[na + 2]
        k = pl.program_id(0)

        @pl.when(k == 0)
        def _():
            acc_ref[...] = jnp.zeros_like(acc_ref)

        bv = b_ref[...].astype(BF16)
        r0 = 0
        for a_ref, m in zip(a_refs, ms):
            acc_ref[r0:r0 + m, :] += _dot_tn(a_ref[...].astype(BF16), bv)
            r0 += m

        @pl.when(k == nk - 1)
        def _():
            o_ref[...] = acc_ref[...].astype(o_ref.dtype)

    return pl.pallas_call(
        body, name=name, grid=(nk,),
        in_specs=[pl.BlockSpec((tk, m), lambda k: (k, 0)) for m in ms] + [pl.BlockSpec((tk, n), lambda k: (k, 0))],
        out_specs=pl.BlockSpec((sum(ms), n), lambda k: (0, 0)),
        out_shape=jax.ShapeDtypeStruct((sum(ms), n), out_dtype),
        scratch_shapes=[pltpu.VMEM((sum(ms), n), F32)],
        compiler_params=_cparams("arbitrary"),
    )(*a_list, b)


def _colsum(name, xs, tm=512):
    rows, cols = xs.shape
    tm = min(tm, rows)

    def body(x_ref, o_ref):
        @pl.when(pl.program_id(0) == 0)
        def _():
            o_ref[...] = jnp.zeros_like(o_ref)

        o_ref[...] += jnp.sum(x_ref[...].astype(F32), axis=0, keepdims=True)

    return pl.pallas_call(
        body, name=name, grid=(rows // tm,),
        in_specs=[pl.BlockSpec((tm, cols), lambda i: (i, 0))],
        out_specs=pl.BlockSpec((1, cols), lambda i: (0, 0)),
        out_shape=jax.ShapeDtypeStruct((1, cols), F32),
        compiler_params=_cparams("arbitrary"),
    )(xs)


def _ew(name, fn, ins, out_cols, out_dtypes, tm=512):
    rows = ins[0].shape[0]
    tm = min(tm, rows)
    ni = len(ins)

    def body(*refs):
        outs = fn(*[r[...] for r in refs[:ni]])
        for o_ref, o in zip(refs[ni:], outs):
            o_ref[...] = o.astype(o_ref.dtype)

    def spec(shape):
        if shape[0] == 1:
            return pl.BlockSpec((1, shape[1]), lambda i: (0, 0))
        return pl.BlockSpec((tm, shape[1]), lambda i: (i, 0))

    return pl.pallas_call(
        body, name=name, grid=(rows // tm,),
        in_specs=[spec(a.shape) for a in ins],
        out_specs=[pl.BlockSpec((tm, c), lambda i: (i, 0)) for c in out_cols],
        out_shape=[jax.ShapeDtypeStruct((rows, c), dt) for c, dt in zip(out_cols, out_dtypes)],
        compiler_params=_cparams("parallel"),
    )(*ins)


def _rms_parts(xv):
    r = lax.rsqrt(jnp.mean(xv * xv, axis=-1, keepdims=True) + EPS)
    return r, xv * r


def _rms_bwd_dx(dh, gain, r, xh):
    dxh = dh * gain
    return r * (dxh - xh * jnp.mean(dxh * xh, axis=-1, keepdims=True))


def _ffn_chunks(f_all):
    return [slice(c, min(c + FFN_CHUNK, f_all)) for c in range(0, f_all, FFN_CHUNK)]


def _loss_head(xo, gain_f, target, d):
    r, xh = _rms_parts(xo)
    err = xh * gain_f - target
    dy = err * (1.0 / d)
    per_tok = jnp.mean(err * err, axis=-1, keepdims=True)
    return (_rms_bwd_dx(dy, gain_f, r, xh), jnp.sum(dy * xh, axis=0, keepdims=True),
            0.5 * jnp.sum(per_tok, axis=0, keepdims=True))


def _ffn_tile(x_ref, g_ref, wg_ref, wu_ref, wd_ref, h_ref, gg_ref, uu_ref):
    xv = x_ref[...]
    _, xh = _rms_parts(xv)
    h = (xh * g_ref[...]).astype(BF16)
    h_ref[...] = h
    acc = None
    for cols in _ffn_chunks(wd_ref.shape[0]):
        gg = _dot_nt(h, wg_ref[cols, :])
        uu = _dot_nt(h, wu_ref[cols, :])
        act = gg * _sigmoid(gg) * uu
        part = _dot_nn(act.astype(BF16), wd_ref[cols, :])
        acc = part if acc is None else acc + part
        gg_ref[:, cols] = gg.astype(BF16)
        uu_ref[:, cols] = uu.astype(BF16)
    return xv + 0.5 * acc


def _ffn_fwd(name, xs, gain, wg_t, wu_t, wd, next_gain, tm=512, deps=()):
    rows, d = xs.shape
    f_all = wd.shape[0]
    tm = min(tm, rows)

    def body(x_ref, g_ref, wg_ref, wu_ref, wd_ref, ng_ref, *rest):
        xo_ref, h_ref, gg_ref, uu_ref, hn_ref = rest[-5:]
        xo = _ffn_tile(x_ref, g_ref, wg_ref, wu_ref, wd_ref, h_ref, gg_ref, uu_ref)
        xo_ref[...] = xo
        hn_ref[...] = (_rms_parts(xo)[1] * ng_ref[...]).astype(BF16)

    tile = pl.BlockSpec((tm, d), lambda i: (i, 0))
    row = pl.BlockSpec((1, d), lambda i: (0, 0))
    wspec = pl.BlockSpec((f_all, d), lambda i: (0, 0), pipeline_mode=pl.Buffered(1))
    hid = pl.BlockSpec((tm, f_all), lambda i: (i, 0))
    return pl.pallas_call(
        body, name=name, grid=(rows // tm,),
        in_specs=[tile, row, wspec, wspec, wspec, row] + [_ANY_SPEC] * len(deps),
        out_specs=[tile, tile, hid, hid, tile],
        out_shape=[jax.ShapeDtypeStruct((rows, d), F32), jax.ShapeDtypeStruct((rows, d), BF16),
                   jax.ShapeDtypeStruct((rows, f_all), BF16), jax.ShapeDtypeStruct((rows, f_all), BF16),
                   jax.ShapeDtypeStruct((rows, d), BF16)],
        compiler_params=_cparams("parallel"),
    )(xs, gain, wg_t, wu_t, wd, next_gain, *deps)


def _ffn_fwd_head(name, xs, gain, wg_t, wu_t, wd, gain_f, target, tm=512):
    rows, d = xs.shape
    f_all = wd.shape[0]
    tm = min(tm, rows)

    def body(x_ref, g_ref, wg_ref, wu_ref, wd_ref, gf_ref, t_ref, dxo_ref, h_ref, gg_ref, uu_ref, dgf_ref, loss_ref):
        xo = _ffn_tile(x_ref, g_ref, wg_ref, wu_ref, wd_ref, h_ref, gg_ref, uu_ref)
        dxo, dgf, loss = _loss_head(xo, gf_ref[...], t_ref[...], d)
        dxo_ref[...] = dxo

        @pl.when(pl.program_id(0) == 0)
        def _():
            dgf_ref[...] = jnp.zeros_like(dgf_ref)
            loss_ref[...] = jnp.zeros_like(loss_ref)

        dgf_ref[...] += dgf
        loss_ref[...] += loss

    tile = pl.BlockSpec((tm, d), lambda i: (i, 0))
    row = pl.BlockSpec((1, d), lambda i: (0, 0))
    wspec = pl.BlockSpec((f_all, d), lambda i: (0, 0), pipeline_mode=pl.Buffered(1))
    hid = pl.BlockSpec((tm, f_all), lambda i: (i, 0))
    return pl.pallas_call(
        body, name=name, grid=(rows // tm,),
        in_specs=[tile, row, wspec, wspec, wspec, row, tile],
        out_specs=[tile, tile, hid, hid, row, pl.BlockSpec((1, 1), lambda i: (0, 0))],
        out_shape=[jax.ShapeDtypeStruct((rows, d), F32), jax.ShapeDtypeStruct((rows, d), BF16),
                   jax.ShapeDtypeStruct((rows, f_all), BF16), jax.ShapeDtypeStruct((rows, f_all), BF16),
                   jax.ShapeDtypeStruct((1, d), F32), jax.ShapeDtypeStruct((1, 1), F32)],
        compiler_params=_cparams("arbitrary"),
    )(xs, gain, wg_t, wu_t, wd, gain_f, target)


def _ffn_bwd(name, dxo, xs, gain, gg_all, uu_all, wg_t, wu_t, wd, tm=256):
    rows, d = xs.shape
    f_all = wd.shape[0]
    tm = min(tm, rows)

    def body(dxo_ref, x_ref, g_ref, gg_ref, uu_ref, wg_ref, wu_ref, wd_ref,
             dx_ref, dgg_ref, duu_ref, act_ref, dgain_ref):
        dxo = dxo_ref[...]
        df = (0.5 * dxo).astype(BF16)
        dh = None
        for cols in _ffn_chunks(f_all):
            gg = gg_ref[:, cols].astype(F32)
            uu = uu_ref[:, cols].astype(F32)
            sg = _sigmoid(gg)
            silu = gg * sg
            dact = _dot_nt(df, wd_ref[cols, :])
            duu = (dact * silu).astype(BF16)
            dgg = (dact * uu * (sg * (1.0 + gg * (1.0 - sg)))).astype(BF16)
            act_ref[:, cols] = (silu * uu).astype(BF16)
            dgg_ref[:, cols] = dgg
            duu_ref[:, cols] = duu
            part = _dot_nn(dgg, wg_ref[cols, :]) + _dot_nn(duu, wu_ref[cols, :])
            dh = part if dh is None else dh + part
        r, xh = _rms_parts(x_ref[...])
        dx_ref[...] = dxo + _rms_bwd_dx(dh, g_ref[...], r, xh)

        @pl.when(pl.program_id(0) == 0)
        def _():
            dgain_ref[...] = jnp.zeros_like(dgain_ref)

        dgain_ref[...] += jnp.sum(dh * xh, axis=0, keepdims=True)

    tile = pl.BlockSpec((tm, d), lambda i: (i, 0))
    row = pl.BlockSpec((1, d), lambda i: (0, 0))
    wspec = pl.BlockSpec((f_all, d), lambda i: (0, 0), pipeline_mode=pl.Buffered(1))
    hid = pl.BlockSpec((tm, f_all), lambda i: (i, 0))
    hid_shape = jax.ShapeDtypeStruct((rows, f_all), BF16)
    return pl.pallas_call(
        body, name=name, grid=(rows // tm,),
        in_specs=[tile, tile, row, hid, hid, wspec, wspec, wspec],
        out_specs=[tile, hid, hid, hid, row],
        out_shape=[jax.ShapeDtypeStruct((rows, d), F32), hid_shape, hid_shape, hid_shape,
                   jax.ShapeDtypeStruct((1, d), F32)],
        compiler_params=_cparams("arbitrary"),
    )(dxo, xs, gain, gg_all, uu_all, wg_t, wu_t, wd)


def _t5_bucket_np(dist):
    max_exact = N_BUCKETS // 2
    dd = np.maximum(dist, 1).astype(np.float32)
    large = max_exact + (np.log(dd / np.float32(max_exact)) / np.float32(math.log(MAX_DISTANCE / max_exact))
                         * np.float32(N_BUCKETS - max_exact)).astype(np.int32)
    large = np.minimum(large, N_BUCKETS - 1)
    return np.where(dist < max_exact, dist, large).astype(np.int32)


def _attn_geometry(g, rows):
    run = rows // 16
    dil = DILATIONS[g]
    if dil == 16:
        bq = BLOCK
        return dict(view=(16, run), block=(None, bq), grid=(16, run // bq), index=lambda r, n: (r, n),
                    pos=np.arange(bq), bq=bq)
    if dil == 4:
        per = BLOCK // 4
        pos = (4 * np.arange(per)[None, :] + np.arange(4)[:, None]).reshape(-1)
        return dict(view=(4, 4, run), block=(4, None, per), grid=(4, run // per), index=lambda r, n: (0, r, n),
                    pos=pos, bq=BLOCK)
    per = 16
    pos = (16 * np.arange(per)[None, :] + np.arange(16)[:, None]).reshape(-1)
    return dict(view=(16, run), block=(16, per), grid=(1, run // per), index=lambda r, n: (0, n),
                pos=pos, bq=16 * per)


def _attn_tables(g, rows):
    geo = _attn_geometry(g, rows)
    pos, bq = geo["pos"], geo["bq"]
    steps = pos[:, None] - np.concatenate([pos - bq, pos])[None, :]
    valid = (steps >= 0) & (steps <= BLOCK)
    bucket = _t5_bucket_np((np.maximum(steps, 0) * DILATIONS[g]).astype(np.int32))
    return bucket, valid.astype(np.int32)


def _bias_fwd(name, bucket, valid, table_t):
    bq = bucket.shape[0]

    def body(bk_ref, ok_ref, tab_ref, o_ref):
        bk = bk_ref[...]
        ok = ok_ref[...] > 0
        accs = [jnp.zeros(bk.shape, F32)] * HEADS_PER_GROUP
        for b in range(N_BUCKETS):
            hit = bk == b
            accs = [jnp.where(hit, tab_ref[h, b], acc) for h, acc in enumerate(accs)]
        for h, acc in enumerate(accs):
            o_ref[h] = jnp.where(ok, acc, NEG_INF)

    vm = pl.BlockSpec(memory_space=pltpu.VMEM)
    return pl.pallas_call(
        body, name=name, in_specs=[vm, vm, pl.BlockSpec(memory_space=pltpu.SMEM)], out_specs=vm,
        out_shape=jax.ShapeDtypeStruct((HEADS_PER_GROUP, bq, 2 * bq), F32),
    )(bucket, valid, table_t)


def _bias_bwd(name, bucket, dbias):
    def body(bk_ref, db_ref, o_ref):
        row_id = lax.broadcasted_iota(jnp.int32, (N_BUCKETS, 128), 0)
        col_id = lax.broadcasted_iota(jnp.int32, (N_BUCKETS, 128), 1)
        bk = bk_ref[...]
        acc = jnp.zeros((N_BUCKETS, 128), F32)
        for h in range(HEADS_PER_GROUP):
            db = db_ref[h]
            for b in range(N_BUCKETS):
                part = jnp.sum(jnp.where(bk == b, db, 0.0), axis=0, keepdims=True)
                tot = jnp.sum(part, axis=1, keepdims=True)
                acc = jnp.where((row_id == b) & (col_id == h), tot, acc)
        o_ref[...] = acc

    vm = pl.BlockSpec(memory_space=pltpu.VMEM)
    return pl.pallas_call(body, name=name, in_specs=[vm, vm], out_specs=vm,
                          out_shape=jax.ShapeDtypeStruct((N_BUCKETS, 128), F32))(bucket, dbias)


def _head_of_lane(nrows):
    return lax.broadcasted_iota(jnp.int32, (nrows, ATTN_OUT), 1) // HEAD_DIM


def _stack_heads(a, lane_head):
    zero = jnp.zeros_like(a)
    return jnp.concatenate([jnp.where(lane_head == h, a, zero) for h in range(HEADS_PER_GROUP)], axis=0)


def _unstack_heads(a4, lane_head, bq):
    out = a4[:bq]
    for h in range(1, HEADS_PER_GROUP):
        out = jnp.where(lane_head == h, a4[h * bq:(h + 1) * bq], out)
    return out


def _attn_specs(geo, cols, col_block, index):
    return pl.BlockSpec(geo["block"] + (cols,), lambda r, n: index(r, n) + (col_block,))


def _attn_fwd(name, qkv, g, bias4):
    rows = qkv.shape[0]
    geo = _attn_geometry(g, rows)
    bq, (nsub, nb), index = geo["bq"], geo["grid"], geo["index"]
    blk_shape = tuple(b for b in geo["block"] if b is not None) + (ATTN_OUT,)

    def body(q_ref, kc_ref, kp_ref, vc_ref, vp_ref, b_ref, o_ref, lse_ref):
        n = pl.program_id(1)
        lane_head = _head_of_lane(bq)
        flat = lambda ref: ref[...].reshape(bq, ATTN_OUT)
        q4 = _stack_heads(flat(q_ref), lane_head)
        k2 = jnp.concatenate([flat(kp_ref), flat(kc_ref)], axis=0)
        v2 = jnp.concatenate([flat(vp_ref), flat(vc_ref)], axis=0)
        s = _dot_nt(q4, k2) + b_ref[...]
        col = lax.broadcasted_iota(jnp.int32, s.shape, 1)
        s = jnp.where((col >= bq) | (n > 0), s, NEG_INF)
        mx = jnp.max(s, axis=-1, keepdims=True)
        p = jnp.exp(s - mx)
        den = jnp.sum(p, axis=-1, keepdims=True)
        o4 = _dot_nn(p.astype(BF16), v2) / den
        lse4 = jnp.broadcast_to(mx + jnp.log(den), (HEADS_PER_GROUP * bq, ATTN_OUT))
        o_ref[...] = _unstack_heads(o4, lane_head, bq).reshape(blk_shape)
        lse_ref[...] = _unstack_heads(lse4, lane_head, bq).reshape(blk_shape)

    prev = lambda r, n: index(r, jnp.maximum(n - 1, 0))
    view = lambda a: a.reshape(geo["view"] + (a.shape[1],))
    qkv_v = view(qkv)
    out_spec = _attn_specs(geo, ATTN_OUT, 0, index)
    out_shape = jax.ShapeDtypeStruct(geo["view"] + (ATTN_OUT,), F32)
    o, lse = pl.pallas_call(
        body, name=name, grid=(nsub, nb),
        in_specs=[_attn_specs(geo, ATTN_OUT, g, index), _attn_specs(geo, ATTN_OUT, 3 + g, index),
                  _attn_specs(geo, ATTN_OUT, 3 + g, prev), _attn_specs(geo, ATTN_OUT, 6 + g, index),
                  _attn_specs(geo, ATTN_OUT, 6 + g, prev), pl.BlockSpec(bias4.shape, lambda r, n: (0, 0))],
        out_specs=[out_spec, out_spec], out_shape=[out_shape, out_shape],
        compiler_params=_cparams("parallel", "arbitrary"),
    )(qkv_v, qkv_v, qkv_v, qkv_v, qkv_v, bias4)
    return o.reshape(rows, ATTN_OUT), lse.reshape(rows, ATTN_OUT)


def _attn_bwd(name, qkv, do, lse, cvec, g, bias4):
    rows = qkv.shape[0]
    geo = _attn_geometry(g, rows)
    bq, (nsub, nb), index = geo["bq"], geo["grid"], geo["index"]
    blk_shape = tuple(b for b in geo["block"] if b is not None) + (ATTN_OUT,)
    nlead = len(blk_shape) - 1

    def body(q_ref, kc_ref, kp_ref, vc_ref, vp_ref, do_ref, lse_ref, c_ref, b_ref,
             dq_ref, dk_ref, dv_ref, db_ref, kcar_ref, vcar_ref):
        r, n = pl.program_id(0), pl.program_id(1)
        valid = n < nb
        lane_head = _head_of_lane(bq)
        flat = lambda ref: ref[...].reshape(bq, ATTN_OUT)

        @pl.when((r == 0) & (n == 0))
        def _():
            kcar_ref[...] = jnp.zeros_like(kcar_ref)
            vcar_ref[...] = jnp.zeros_like(vcar_ref)
            db_ref[...] = jnp.zeros_like(db_ref)

        def column(ref, h):
            lead = (slice(None),) * nlead
            return ref[lead + (pl.ds(h * HEAD_DIM, 1),)].reshape(bq, 1)

        q4 = _stack_heads(flat(q_ref), lane_head)
        do4 = _stack_heads(flat(do_ref), lane_head)
        k2 = jnp.concatenate([flat(kp_ref), flat(kc_ref)], axis=0)
        v2 = jnp.concatenate([flat(vp_ref), flat(vc_ref)], axis=0)
        lse4 = jnp.concatenate([column(lse_ref, h) for h in range(HEADS_PER_GROUP)], axis=0)
        c4 = jnp.concatenate([column(c_ref, h) for h in range(HEADS_PER_GROUP)], axis=0)
        s = _dot_nt(q4, k2) + b_ref[...]
        col = lax.broadcasted_iota(jnp.int32, s.shape, 1)
        keep = ((col >= bq) | (n > 0)) & valid
        p = jnp.where(keep, jnp.exp(s - lse4), 0.0)
        ds = p * (_dot_nt(do4, v2) + c4)
        ds_b = ds.astype(BF16)

        @pl.when(valid)
        def _():
            dq = _unstack_heads(_dot_nn(ds_b, k2), lane_head, bq) * (HEAD_DIM ** -0.5)
            dq_ref[...] = dq.astype(BF16).reshape(blk_shape)

        dk2 = _dot_tn(ds_b, q4)
        dv2 = _dot_tn(p.astype(BF16), do4)
        dk_ref[...] = (kcar_ref[...] + dk2[:bq]).astype(BF16).reshape(blk_shape)
        dv_ref[...] = (vcar_ref[...] + dv2[:bq]).astype(BF16).reshape(blk_shape)
        kcar_ref[...] = dk2[bq:]
        vcar_ref[...] = dv2[bq:]
        db_ref[...] += ds

    cur = lambda r, n: index(r, jnp.minimum(n, nb - 1))
    prev = lambda r, n: index(r, jnp.maximum(jnp.minimum(n, nb - 1) - 1, 0))
    late = lambda r, n: index(r, jnp.maximum(n - 1, 0))
    view = lambda a: a.reshape(geo["view"] + (a.shape[1],))
    qkv_v = view(qkv)
    tile = _attn_specs(geo, ATTN_OUT, 0, cur)
    bias_spec = pl.BlockSpec(bias4.shape, lambda r, n: (0, 0))
    out_shape = jax.ShapeDtypeStruct(geo["view"] + (ATTN_OUT,), BF16)
    dq, dk, dv, db = pl.pallas_call(
        body, name=name, grid=(nsub, nb + 1),
        in_specs=[_attn_specs(geo, ATTN_OUT, g, cur), _attn_specs(geo, ATTN_OUT, 3 + g, cur),
                  _attn_specs(geo, ATTN_OUT, 3 + g, prev), _attn_specs(geo, ATTN_OUT, 6 + g, cur),
                  _attn_specs(geo, ATTN_OUT, 6 + g, prev), tile, tile, tile, bias_spec],
        out_specs=[tile, _attn_specs(geo, ATTN_OUT, 0, late), _attn_specs(geo, ATTN_OUT, 0, late), bias_spec],
        out_shape=[out_shape, out_shape, out_shape, jax.ShapeDtypeStruct(bias4.shape, F32)],
        scratch_shapes=[pltpu.VMEM((bq, ATTN_OUT), F32), pltpu.VMEM((bq, ATTN_OUT), F32)],
        compiler_params=_cparams("arbitrary", "arbitrary"),
    )(qkv_v, qkv_v, qkv_v, qkv_v, qkv_v, view(do), view(lse), view(cvec), bias4)
    return dq.reshape(rows, ATTN_OUT), dk.reshape(rows, ATTN_OUT), dv.reshape(rows, ATTN_OUT), db


def _group_weights(lses):
    mx = jnp.maximum(jnp.maximum(lses[0], lses[1]), lses[2])
    es = [jnp.exp(l - mx) for l in lses]
    den = es[0] + es[1] + es[2]
    return [e / den for e in es]


def _combine_fwd(name, os_, lses):
    def fn(o0, o1, o2, l0, l1, l2):
        ws = _group_weights([l0, l1, l2])
        out = ws[0] * o0 + ws[1] * o1 + ws[2] * o2
        return out, out

    return _ew(name, fn, [*os_, *lses], [ATTN_OUT, ATTN_OUT], [F32, BF16], tm=1024)


def _combine_bwd(name, do, oa, lses):
    def fn(dov, oav, l0, l1, l2):
        head_sum = (lax.broadcasted_iota(jnp.int32, (ATTN_OUT, ATTN_OUT), 0) // HEAD_DIM
                    == lax.broadcasted_iota(jnp.int32, (ATTN_OUT, ATTN_OUT), 1) // HEAD_DIM)
        ws = _group_weights([l0, l1, l2])
        prod = dov * oav
        hi = prod.astype(BF16)
        lo = (prod - hi.astype(F32)).astype(BF16)
        ones = jnp.where(head_sum, 1.0, 0.0).astype(BF16)
        bar = _dot_nn(hi, ones) + _dot_nn(lo, ones)
        return tuple(w * dov for w in ws) + tuple(-w * bar for w in ws)

    return _ew(name, fn, [do, oa, *lses], [ATTN_OUT] * 6, [BF16] * 3 + [F32] * 3, tm=1024)


def _ssm_disc(a_re, a_im, log_dt, b_re, b_im):
    dt = jnp.exp(log_dt)
    mag = jnp.exp(a_re * dt)
    ab_re = mag * jnp.cos(a_im * dt)
    ab_im = mag * jnp.sin(a_im * dt)
    den = a_re * a_re + a_im * a_im
    xr = ab_re - 1.0
    coef_re = (xr * a_re + ab_im * a_im) / den
    coef_im = (ab_im * a_re - xr * a_im) / den
    bb_re = coef_re[None] * b_re - coef_im[None] * b_im
    bb_im = coef_re[None] * b_im + coef_im[None] * b_re
    return ab_re, ab_im, bb_re, bb_im


def _ssm_params_fwd(name, a_re, a_im, log_dt, b_re, b_im, c_re, c_im):
    pows = jax.ShapeDtypeStruct((SCAN_STEPS,) + a_re.shape, F32)
    mats = jax.ShapeDtypeStruct((SSM_PAIRS, PAIR_TILE, PAIR_TILE), BF16)
    per_tile = PAIR_TILE // (2 * SSM_GROUP)

    def body(ar, ai, ld, br, bi, cr, ci, o_pr, o_pi, o_bb, o_c, bbr_ref, bbi_ref, wide_ref):
        ab_re, ab_im, bb_re, bb_im = _ssm_disc(ar[...], ai[...], ld[...], br[...], bi[...])
        pr, pi = ab_re, ab_im
        for j in range(SCAN_STEPS):
            o_pr[j] = pr
            o_pi[j] = pi
            pr, pi = pr * ab_re - pi * ab_im, pr * ab_im + pi * ab_re
        bbr_ref[...] = bb_re
        bbi_ref[...] = bb_im

        def place(out_ref, block):
            wide_ref[...] = jnp.zeros_like(wide_ref)
            for g in range(SSM_GROUPS):
                p, l = divmod(g, 2)
                rows = pl.ds((p % per_tile) * 2 * SSM_GROUP + l * SSM_GROUP, SSM_GROUP)
                re, im = block(g)
                wide_ref[p, rows, pl.ds(l * SSM_STATE, SSM_STATE)] = re
                wide_ref[p, rows, pl.ds(PAIR_LANES + l * SSM_STATE, SSM_STATE)] = im
            out_ref[...] = wide_ref[...].astype(BF16)

        place(o_bb, lambda g: (bbr_ref[:, g, :], bbi_ref[:, g, :]))
        place(o_c, lambda g: (cr[g], -ci[g]))

    vm = pl.BlockSpec(memory_space=pltpu.VMEM)
    return pl.pallas_call(
        body, name=name, in_specs=[vm] * 7, out_specs=[vm] * 4, out_shape=[pows, pows, mats, mats],
        scratch_shapes=[pltpu.VMEM(b_re.shape, F32), pltpu.VMEM(b_re.shape, F32),
                        pltpu.VMEM((SSM_PAIRS, PAIR_TILE, PAIR_TILE), F32)],
    )(a_re, a_im, log_dt, b_re, b_im, c_re, c_im)


def _ssm_params_bwd(name, a_re, a_im, log_dt, b_re, b_im, d_ab_re, d_ab_im, d_bb_re, d_bb_im):
    gn = jax.ShapeDtypeStruct(a_re.shape, F32)
    cgn = jax.ShapeDtypeStruct(b_re.shape, F32)

    def body(ar, ai, ld, br, bi, g0, g1, g2, g3, o_ar, o_ai, o_ld, o_br, o_bi):
        _, vjp = jax.vjp(_ssm_disc, ar[...], ai[...], ld[...], br[...], bi[...])
        outs = vjp((g0[...], g1[...], g2[...], g3[...]))
        for o_ref, o in zip((o_ar, o_ai, o_ld, o_br, o_bi), outs):
            o_ref[...] = o

    vm = pl.BlockSpec(memory_space=pltpu.VMEM)
    return pl.pallas_call(body, name=name, in_specs=[vm] * 9, out_specs=[vm] * 5,
                          out_shape=[gn, gn, jax.ShapeDtypeStruct(log_dt.shape, F32), cgn, cgn],
                          )(a_re, a_im, log_dt, b_re, b_im, d_ab_re, d_ab_im, d_bb_re, d_bb_im)


def _scan_block(s_ref, carry_ref, tmp_ref, pw_ref, reverse, sprev=None):
    nl = SSM_LANES
    halves = range(SCAN_COLS // SCAN_SUB)
    zero = jnp.zeros((SCAN_SUB, SCAN_LANES), F32)
    for half in (reversed(halves) if reverse else halves):
        sub_rows = pl.ds(half * SCAN_SUB, SCAN_SUB)
        for lc in range(nl // SCAN_LANES):
            re_l = pl.ds(lc * SCAN_LANES, SCAN_LANES)
            im_l = pl.ds(nl + lc * SCAN_LANES, SCAN_LANES)
            are, aim = pw_ref[0, :, re_l], pw_ref[0, :, im_l]

            def step_of(j):
                return SCAN_STEPS - 1 - j if reverse else j

            def pass1(j, st):
                sr, si = st
                jj = step_of(j)
                nr = are * sr - aim * si + s_ref[jj, sub_rows, re_l]
                ni = are * si + aim * sr + s_ref[jj, sub_rows, im_l]
                s_ref[jj, sub_rows, re_l] = nr
                s_ref[jj, sub_rows, im_l] = ni
                return nr, ni

            er, ei = lax.fori_loop(0, SCAN_STEPS, pass1, (zero, zero), unroll=2)
            tmp_ref[0:SCAN_SUB, re_l] = er
            tmp_ref[0:SCAN_SUB, im_l] = ei
            apr, api = pw_ref[SCAN_STEPS - 1, 0:1, re_l], pw_ref[SCAN_STEPS - 1, 0:1, im_l]
            sr, si = carry_ref[0:1, re_l], carry_ref[0:1, im_l]
            for step in range(SCAN_SUB):
                c = SCAN_SUB - 1 - step if reverse else step
                tmp_ref[SCAN_SUB + c:SCAN_SUB + c + 1, re_l] = sr
                tmp_ref[SCAN_SUB + c:SCAN_SUB + c + 1, im_l] = si
                e_r, e_i = tmp_ref[c:c + 1, re_l], tmp_ref[c:c + 1, im_l]
                sr, si = apr * sr - api * si + e_r, apr * si + api * sr + e_i
            carry_ref[0:1, re_l] = sr
            carry_ref[0:1, im_l] = si
            cr = tmp_ref[SCAN_SUB:2 * SCAN_SUB, re_l]
            ci = tmp_ref[SCAN_SUB:2 * SCAN_SUB, im_l]

            if sprev is None:
                def pass2(j, st):
                    pr, pi = pw_ref[j, :, re_l], pw_ref[j, :, im_l]
                    jj = step_of(j)
                    s_ref[jj, sub_rows, re_l] += pr * cr - pi * ci
                    s_ref[jj, sub_rows, im_l] += pr * ci + pi * cr
                    return st

                lax.fori_loop(0, SCAN_STEPS, pass2, 0, unroll=2)
            else:
                st_ref, prev_ref, have_prev, dab_ref = sprev

                def corrected(jj, pr, pi):
                    gr = s_ref[jj, sub_rows, re_l] + pr * cr - pi * ci
                    gi = s_ref[jj, sub_rows, im_l] + pr * ci + pi * cr
                    s_ref[jj, sub_rows, re_l] = gr
                    s_ref[jj, sub_rows, im_l] = gi
                    return gr, gi

                def pass2(j, st):
                    dr, di = st
                    jj = SCAN_STEPS - 1 - j
                    gr, gi = corrected(jj, pw_ref[j, :, re_l], pw_ref[j, :, im_l])
                    qr, qi = st_ref[jj - 1, sub_rows, re_l], st_ref[jj - 1, sub_rows, im_l]
                    return dr + gr * qr + gi * qi, di + gi * qr - gr * qi

                dr, di = lax.fori_loop(0, SCAN_STEPS - 1, pass2, (zero, zero), unroll=2)
                gr, gi = corrected(0, pw_ref[SCAN_STEPS - 1, :, re_l], pw_ref[SCAN_STEPS - 1, :, im_l])
                sub = lax.broadcasted_iota(jnp.int32, (SCAN_SUB, SCAN_LANES), 0)
                if half == 0:
                    pv_r = prev_ref[SCAN_SUB - 1:SCAN_SUB, re_l] * have_prev
                    pv_i = prev_ref[SCAN_SUB - 1:SCAN_SUB, im_l] * have_prev
                else:
                    before = pl.ds(half * SCAN_SUB - 1, 1)
                    pv_r, pv_i = st_ref[SCAN_STEPS - 1, before, re_l], st_ref[SCAN_STEPS - 1, before, im_l]
                shape = (SCAN_SUB, SCAN_LANES)
                qr = jnp.where(sub == 0, jnp.broadcast_to(pv_r, shape),
                               pltpu.roll(st_ref[SCAN_STEPS - 1, sub_rows, re_l], 1, 0))
                qi = jnp.where(sub == 0, jnp.broadcast_to(pv_i, shape),
                               pltpu.roll(st_ref[SCAN_STEPS - 1, sub_rows, im_l], 1, 0))
                dab_ref[:, re_l] += dr + gr * qr + gi * qi
                dab_ref[:, im_l] += di + gi * qr - gr * qi


def _scan_view(a):
    return a.reshape(16, a.shape[0] // 16, a.shape[1])


def _pair_tile(p):
    start = (p * 2 * SSM_GROUP // PAIR_TILE) * PAIR_TILE
    return slice(start, start + PAIR_TILE)


def _pair_lanes(p):
    return pl.ds(p * PAIR_LANES, PAIR_LANES), pl.ds(SSM_LANES + p * PAIR_LANES, PAIR_LANES)


def _pair_store(s_ref, p, val):
    re_l, im_l = _pair_lanes(p)
    s_ref[:, :, re_l] = val[:, :PAIR_LANES].reshape(16, SCAN_COLS, PAIR_LANES)
    s_ref[:, :, im_l] = val[:, PAIR_LANES:].reshape(16, SCAN_COLS, PAIR_LANES)


def _pair_load(s_ref, p):
    re_l, im_l = _pair_lanes(p)
    parts = [s_ref[:, :, l].reshape(SCAN_BLOCK, PAIR_LANES) for l in (re_l, im_l)]
    return jnp.concatenate(parts, axis=1).astype(BF16)


def _pair_sum(fn):
    per = PAIR_TILE // (2 * SSM_GROUP)
    tiles = []
    for t in range(SSM_PAIRS // per):
        acc = None
        for p in range(t * per, (t + 1) * per):
            part = fn(p)
            acc = part if acc is None else acc + part
        tiles.append(acc)
    return jnp.concatenate(tiles, axis=1)


def _ssm_fwd(name, u, bb_mats, c_mats, pw_rows, d_skip):
    rows = u.shape[0]
    nl2 = 2 * SSM_LANES
    nblk = rows // SCAN_BLOCK

    def body(u_ref, bb_ref, c_ref, pw_ref, d_ref, y_ref, yg_ref, s_ref, carry_ref, tmp_ref):
        @pl.when(pl.program_id(0) == 0)
        def _():
            carry_ref[...] = jnp.zeros_like(carry_ref)

        uv = u_ref[...].reshape(SCAN_BLOCK, SSM_WIDTH)
        ub = uv.astype(BF16)
        for p in range(SSM_PAIRS):
            _pair_store(s_ref, p, _dot_nn(ub[:, _pair_tile(p)], bb_ref[p]))
        _scan_block(s_ref, carry_ref, tmp_ref, pw_ref, reverse=False)
        ys = _pair_sum(lambda p: _dot_nt(_pair_load(s_ref, p), c_ref[p]))
        yv = ys + d_ref[...] * uv
        y_ref[...] = yv.reshape(16, SCAN_COLS, SSM_WIDTH)
        yg_ref[...] = jax.nn.gelu(yv).astype(BF16).reshape(16, SCAN_COLS, SSM_WIDTH)

    const = lambda shape: pl.BlockSpec(shape, lambda i: (0,) * len(shape))
    blk = lambda cols: pl.BlockSpec((16, SCAN_COLS, cols), lambda i: (0, i, 0))
    pair_mats = const((SSM_PAIRS, PAIR_TILE, PAIR_TILE))
    y, yg, s = pl.pallas_call(
        body, name=name, grid=(nblk,),
        in_specs=[blk(SSM_WIDTH), pair_mats, pair_mats, const((SCAN_STEPS, SCAN_SUB, nl2)), const((1, SSM_WIDTH))],
        out_specs=[blk(SSM_WIDTH), blk(SSM_WIDTH), blk(nl2)],
        out_shape=[jax.ShapeDtypeStruct((16, rows // 16, SSM_WIDTH), F32),
                   jax.ShapeDtypeStruct((16, rows // 16, SSM_WIDTH), BF16),
                   jax.ShapeDtypeStruct((16, rows // 16, nl2), F32)],
        scratch_shapes=[pltpu.VMEM((SCAN_SUB, nl2), F32), pltpu.VMEM((2 * SCAN_SUB, nl2), F32)],
        compiler_params=_cparams("arbitrary"),
    )(_scan_view(u), bb_mats, c_mats, pw_rows, d_skip)
    return y.reshape(rows, SSM_WIDTH), yg.reshape(rows, SSM_WIDTH), s.reshape(rows, nl2)


def _ssm_bwd(name, dy, u, states, bb_mats, c_mats, pwc_rows, d_skip):
    rows = u.shape[0]
    nl2 = 2 * SSM_LANES
    nblk = rows // SCAN_BLOCK

    def body(dy_ref, u_ref, st_ref, prev_ref, bb_ref, c_ref, pw_ref, d_ref,
             du_ref, dbb_ref, dc_ref, dab_ref, dd_ref, g_ref, carry_ref, tmp_ref):
        i = pl.program_id(0)

        @pl.when(i == 0)
        def _():
            carry_ref[...] = jnp.zeros_like(carry_ref)
            for ref in (dbb_ref, dc_ref, dab_ref, dd_ref):
                ref[...] = jnp.zeros_like(ref)

        dyv = dy_ref[...].reshape(SCAN_BLOCK, SSM_WIDTH)
        uv = u_ref[...].reshape(SCAN_BLOCK, SSM_WIDTH)
        dyb, ub = dyv.astype(BF16), uv.astype(BF16)
        for p in range(SSM_PAIRS):
            _pair_store(g_ref, p, _dot_nn(dyb[:, _pair_tile(p)], c_ref[p]))
        have_prev = (i < nblk - 1).astype(F32)
        _scan_block(g_ref, carry_ref, tmp_ref, pw_ref, reverse=True,
                    sprev=(st_ref, prev_ref, have_prev, dab_ref))

        def pair_work(p):
            gp = _pair_load(g_ref, p)
            dbb_ref[p] += _dot_tn(ub[:, _pair_tile(p)], gp)
            dc_ref[p] += _dot_tn(dyb[:, _pair_tile(p)], _pair_load(st_ref, p))
            return _dot_nt(gp, bb_ref[p])

        du_ref[...] = (_pair_sum(pair_work) + d_ref[...] * dyv).reshape(16, SCAN_COLS, SSM_WIDTH)
        dd_ref[...] += jnp.sum(dyv * uv, axis=0, keepdims=True)

    const = lambda shape: pl.BlockSpec(shape, lambda i: (0,) * len(shape))
    blk = lambda cols: pl.BlockSpec((16, SCAN_COLS, cols), lambda i: (0, nblk - 1 - i, 0))
    per8 = SCAN_COLS // SCAN_SUB
    prev_spec = pl.BlockSpec((None, SCAN_SUB, nl2), lambda i: (15, jnp.maximum((nblk - 1 - i) * per8 - 1, 0), 0))
    pair_mats = const((SSM_PAIRS, PAIR_TILE, PAIR_TILE))
    pair_shape = jax.ShapeDtypeStruct((SSM_PAIRS, PAIR_TILE, PAIR_TILE), F32)
    sv = _scan_view(states)
    du, dbb, dc, dab, dd = pl.pallas_call(
        body, name=name, grid=(nblk,),
        in_specs=[blk(SSM_WIDTH), blk(SSM_WIDTH), blk(nl2), prev_spec, pair_mats, pair_mats,
                  const((SCAN_STEPS, SCAN_SUB, nl2)), const((1, SSM_WIDTH))],
        out_specs=[blk(SSM_WIDTH), pair_mats, pair_mats, const((SCAN_SUB, nl2)), const((1, SSM_WIDTH))],
        out_shape=[jax.ShapeDtypeStruct((16, rows // 16, SSM_WIDTH), F32), pair_shape, pair_shape,
                   jax.ShapeDtypeStruct((SCAN_SUB, nl2), F32), jax.ShapeDtypeStruct((1, SSM_WIDTH), F32)],
        scratch_shapes=[pltpu.VMEM((16, SCAN_COLS, nl2), F32), pltpu.VMEM((SCAN_SUB, nl2), F32),
                        pltpu.VMEM((2 * SCAN_SUB, nl2), F32)],
        compiler_params=_cparams("arbitrary"),
    )(_scan_view(dy), _scan_view(u), sv, sv, bb_mats, c_mats, pwc_rows, d_skip)
    return du.reshape(rows, SSM_WIDTH), dbb, dc, dab, dd


def _adamw(name, w, m, v, gparts, tr):
    rows, cols = w.shape

    def body(w_ref, m_ref, v_ref, g_ref, og_ref, od_ref, om_ref, ov_ref):
        g = g_ref[0].astype(F32)
        for i in range(1, N_DEV):
            g = g + g_ref[i].astype(F32)
        m_new = B1 * m_ref[...] + (1.0 - B1) * g
        v_new = B2 * v_ref[...] + (1.0 - B2) * (g * g)
        m_hat = m_new / (1.0 - B1 ** STEP)
        v_hat = v_new / (1.0 - B2 ** STEP)
        og_ref[...] = g
        od_ref[...] = -LR * (m_hat / (jnp.sqrt(v_hat) + ADAM_EPS) + WD * w_ref[...])
        om_ref[...] = m_new
        ov_ref[...] = v_new

    spec = pl.BlockSpec((tr, cols), lambda i: (i, 0))
    shape = jax.ShapeDtypeStruct((rows, cols), F32)
    return pl.pallas_call(
        body, name=name, grid=(rows // tr,),
        in_specs=[spec, spec, spec, pl.BlockSpec((N_DEV, tr, cols), lambda i: (0, i, 0))],
        out_specs=[spec] * 4, out_shape=[shape] * 4,
        compiler_params=_cparams("parallel"),
    )(w, m, v, gparts)


_SHARDED = (
    ("ffn1_w_gate", True, (352, 1024)), ("ffn1_w_up", True, (352, 1024)), ("ffn1_w_down", False, (352, 1024)),
    ("w_in", True, (608, 1024)), ("ssm_w_glu", True, (128, 512)), ("w_attn_branch", True, (128, 256)),
    ("w_ssm_branch", True, (128, 512)), ("w_out", False, (128, 1024)),
    ("ffn2_w_gate", True, (352, 1024)), ("ffn2_w_up", True, (352, 1024)), ("ffn2_w_down", False, (352, 1024)),
)
_SMALL = ("ffn1_norm", "mix_norm", "gate_bias", "rel_bias_table", "ssm_a_re", "ssm_a_im", "ssm_log_dt",
          "ssm_b_re", "ssm_b_im", "ssm_c_re", "ssm_c_im", "ssm_d", "ffn2_norm", "final_norm")
_ORDER = ("ffn1_norm", "ffn1_w_gate", "ffn1_w_up", "ffn1_w_down", "mix_norm", "w_in", "gate_bias",
          "rel_bias_table", "ssm_a_re", "ssm_a_im", "ssm_log_dt", "ssm_b_re", "ssm_b_im", "ssm_c_re",
          "ssm_c_im", "ssm_d", "ssm_w_glu", "w_attn_branch", "w_ssm_branch", "w_out", "ffn2_norm",
          "ffn2_w_gate", "ffn2_w_up", "ffn2_w_down", "final_norm")


def _pack_rows(shape):
    return shape[0] * shape[1] // D_MODEL


_SHARD_INFO = {nm: (tr, shape) for nm, tr, shape in _SHARDED}
_PHASES = {
    "f1gu": ("ffn1_w_gate", "ffn1_w_up"), "f1d": ("ffn1_w_down",),
    "mix": ("w_in", "ssm_w_glu", "w_attn_branch", "w_ssm_branch", "w_out"),
    "f2": ("ffn2_w_gate", "ffn2_w_up", "ffn2_w_down"),
}


def _to_rows(a, nm):
    tr, shape = _SHARD_INFO[nm]
    return (a.T if tr else a).reshape(_pack_rows(shape), D_MODEL)


def _from_rows(p, nm):
    tr, shape = _SHARD_INFO[nm]
    a = p.reshape(shape)
    return a.T if tr else a


def _full_weight(gathered, nm):
    _, shape = _SHARD_INFO[nm]
    return gathered.reshape(N_DEV * shape[0], shape[1])


def _grad_blocks(g, nm):
    _, shape = _SHARD_INFO[nm]
    return g.astype(BF16).reshape(N_DEV, _pack_rows(shape), D_MODEL)


_SMALL_TILE = 8 * 128


def _small_rows(a):
    flat = a.reshape(-1)
    return jnp.pad(flat, (0, (-flat.shape[0]) % _SMALL_TILE)).reshape(-1, 128)


def _pack_small(ws, last=None):
    tail = jnp.zeros((), F32) if last is None else last
    return jnp.concatenate([_small_rows(ws[nm]) for nm in _SMALL] + [_small_rows(tail)], axis=0)


def _unpack_small(pack, like):
    out, r0 = {}, 0
    for nm in _SMALL:
        n = like[nm].size
        nr = 8 * -(-n // _SMALL_TILE)
        out[nm] = pack[r0:r0 + nr].reshape(-1)[:n].reshape(like[nm].shape)
        r0 += nr
    return out


def _residue_order(a):
    rows, cols = a.shape
    return a.reshape(rows // 16, 16, cols).transpose(1, 0, 2).reshape(rows, cols)


def _token_order(a):
    rows, cols = a.shape
    return a.reshape(16, rows // 16, cols).transpose(1, 0, 2).reshape(rows, cols)


_PAIRS_PER_TILE = PAIR_TILE // (2 * SSM_GROUP)
_PAIR_AXES = (SSM_PAIRS // _PAIRS_PER_TILE, _PAIRS_PER_TILE, 2)


def _pair_diagonals(acc):
    k, j, l = _PAIR_AXES
    eight = acc.reshape(k, j, j, l, SSM_GROUP, 2, l, SSM_STATE)
    eye_j, eye_l = jnp.eye(j, dtype=acc.dtype), jnp.eye(l, dtype=acc.dtype)
    own = jnp.einsum("kjJLcxln,jJ,lL->xkjlcn", eight, eye_j, eye_l).reshape(2, SSM_GROUPS, SSM_GROUP, SSM_STATE)
    return own[0], own[1]


def _local_step(xs, target, small, weights_of, send_grads, first_deps=()):
    rows = xs.shape[0]
    gfull, gsmall = {}, {}

    table_t = small["rel_bias_table"].T
    tables, bias4 = [], []
    for g in range(N_GROUPS):
        bucket, valid = [jnp.asarray(t) for t in _attn_tables(g, rows)]
        bias_g = _bias_fwd(f"rel_bias_fwd_{g}", bucket, valid, table_t[g * HEADS_PER_GROUP:(g + 1) * HEADS_PER_GROUP])
        tables.append(bucket)
        bias4.append(bias_g.reshape(-1, bias_g.shape[-1]))
    pw_re, pw_im, bb_mats, c_mats = _ssm_params_fwd(
        "ssm_params_fwd", small["ssm_a_re"], small["ssm_a_im"], small["ssm_log_dt"].reshape(SSM_GROUPS, 1),
        small["ssm_b_re"].transpose(2, 0, 1), small["ssm_b_im"].transpose(2, 0, 1), small["ssm_c_re"], small["ssm_c_im"])

    def power_rows(sign):
        row = jnp.concatenate([pw_re.reshape(SCAN_STEPS, 1, SSM_LANES), sign * pw_im.reshape(SCAN_STEPS, 1, SSM_LANES)],
                              axis=2)
        return jnp.broadcast_to(row, (SCAN_STEPS, SCAN_SUB, 2 * SSM_LANES))

    pw_fwd, pw_bwd = power_rows(1.0), power_rows(-1.0)
    d_skip = small["ssm_d"].reshape(1, SSM_WIDTH)
    wf = dict(weights_of("f1", [xs, target, bb_mats, c_mats, pw_fwd, pw_bwd] + bias4))

    x1, h1, gg1, uu1, hmix = _ffn_fwd("ffn1_fwd", xs, small["ffn1_norm"], wf["ffn1_w_gate"], wf["ffn1_w_up"],
                                      wf["ffn1_w_down"], small["mix_norm"], deps=first_deps)
    wf.update(weights_of("mix", x1))
    w_in = wf["w_in"]
    w_qkv, w_u, w_g = w_in[:3 * ATTN_WIDTH], w_in[3 * ATTN_WIDTH:3 * ATTN_WIDTH + SSM_WIDTH], w_in[3 * ATTN_WIDTH + SSM_WIDTH:]
    qscale = jnp.concatenate([jnp.full((1, ATTN_WIDTH), HEAD_DIM ** -0.5, F32), jnp.ones((1, 2 * ATTN_WIDTH), F32)], axis=1)
    qkv, = _mm("in_qkv", [(hmix, w_qkv)], True, 3 * ATTN_WIDTH, [BF16],
               epilogue=lambda acc, sc: (acc * sc,), extras=[(qscale, 0)], tn=ATTN_WIDTH)
    u, = _mm("in_u", [(hmix, w_u)], True, SSM_WIDTH, [F32])
    gates, = _mm("in_gates", [(hmix, w_g)], True, 2 * D_MODEL, [BF16],
                 epilogue=lambda acc, b: (_sigmoid(acc + b),), extras=[(small["gate_bias"], 0)])

    o_g, lse_g = [], []
    for g in range(N_GROUPS):
        o, lse = _attn_fwd(f"attn_fwd_{g}", qkv, g, bias4[g])
        o_g.append(o)
        lse_g.append(lse)
    oa_f32, oa = _combine_fwd("attn_combine_fwd", o_g, lse_g)
    y_attn, = _mm("attn_branch", [(oa, wf["w_attn_branch"])], True, D_MODEL, [BF16])
    y_raw, ygelu, states = _ssm_fwd("ssm_fwd", u, bb_mats, c_mats, pw_fwd, d_skip)
    glu, ysg = _mm("ssm_glu", [(ygelu, wf["ssm_w_glu"])], True, 2 * SSM_WIDTH, [BF16, BF16],
                   epilogue=lambda gv: (gv, gv[:, :SSM_WIDTH] * _sigmoid(gv[:, SSM_WIDTH:])),
                   tn=2 * SSM_WIDTH, out_cols=[2 * SSM_WIDTH, SSM_WIDTH])
    y_ssm, merged = _mm("ssm_branch_merge", [(ysg, wf["w_ssm_branch"])], True, D_MODEL, [BF16, BF16],
                        epilogue=lambda acc, ga, gs, ya: (acc, ga * ya + gs * acc),
                        extras=[(gates, 0), (gates, D_MODEL), (y_attn, 0)])
    x2, = _mm("mix_out", [(merged, wf["w_out"])], False, D_MODEL, [F32],
              epilogue=lambda acc, res: (res + acc,), extras=[(x1, 0)])
    wf.update(weights_of("f2", x2))
    dx3, h2, gg2, uu2, gsmall["final_norm"], gsmall["loss"] = _ffn_fwd_head(
        "ffn2_fwd", x2, small["ffn2_norm"], wf["ffn2_w_gate"], wf["ffn2_w_up"], wf["ffn2_w_down"],
        small["final_norm"].reshape(1, D_MODEL), target)

    dx2, dgg2, duu2, act2, gsmall["ffn2_norm"] = _ffn_bwd(
        "ffn2_bwd", dx3, x2, small["ffn2_norm"], gg2, uu2, wf["ffn2_w_gate"], wf["ffn2_w_up"], wf["ffn2_w_down"])
    gfull["ffn2_w_gate"] = _mm_tn("ffn2_dwg", dgg2, h2, out_dtype=BF16, tk=FFN_DW_ROWS)
    gfull["ffn2_w_up"] = _mm_tn("ffn2_dwu", duu2, h2, out_dtype=BF16, tk=FFN_DW_ROWS)
    gfull["ffn2_w_down"] = _mm_tn("ffn2_dwd", act2, dx3, scale=0.5, out_dtype=BF16, tk=FFN_DW_ROWS)
    sent = send_grads("f2", gfull)

    def merge_bwd(dm, ga, gs, ya, ys):
        dza, dzs = dm * ya * ga * (1.0 - ga), dm * ys * gs * (1.0 - gs)
        return (dm * ga, dm * gs, dza, dzs, jnp.sum(dza, axis=0, keepdims=True), jnp.sum(dzs, axis=0, keepdims=True))

    dya, dys, dzga, dzgs, dba, dbs = _mm(
        "mix_out_bwd", [(dx2, wf["w_out"])], True, D_MODEL, [BF16] * 4, epilogue=merge_bwd, row_sums=2,
        extras=[(gates, 0), (gates, D_MODEL), (y_attn, 0), (y_ssm, 0)], deps=sent, tm=512, tn=D_MODEL)
    gfull["w_out"] = _mm_tn("dw_out", merged, dx2, out_dtype=BF16)
    gsmall["gate_bias"] = jnp.concatenate([dba, dbs], axis=1)

    gfull["w_ssm_branch"] = _mm_tn("dw_ssm_branch", dys, ysg, out_dtype=BF16)

    def glu_bwd(dysg, av, bv):
        sb = _sigmoid(bv)
        return (dysg * sb, dysg * av * sb * (1.0 - sb))

    dglu_a, dglu_b = _mm("ssm_branch_bwd", [(dys, wf["w_ssm_branch"])], False, SSM_WIDTH, [BF16, BF16],
                         epilogue=glu_bwd, extras=[(glu, 0), (glu, SSM_WIDTH)])
    w_glu = wf["ssm_w_glu"]
    gfull["ssm_w_glu"] = _mm_tn_stack("dw_glu", [dglu_a, dglu_b], ygelu, out_dtype=BF16)

    def gelu_bwd(acc, yv):
        _, vjp = jax.vjp(jax.nn.gelu, yv)
        return (vjp(acc)[0],)

    dy_raw, = _mm("ssm_glu_bwd", [(dglu_a, w_glu[:SSM_WIDTH]), (dglu_b, w_glu[SSM_WIDTH:])], False, SSM_WIDTH, [F32],
                  epilogue=gelu_bwd, extras=[(y_raw, 0)])
    du, dbb_acc, dc_acc, dab_rows, gsmall_d = _ssm_bwd(
        "ssm_bwd", dy_raw, u, states, bb_mats, c_mats, pw_bwd, d_skip)
    gsmall["ssm_d"] = gsmall_d
    dbb_re, dbb_im = [a.transpose(1, 0, 2) for a in _pair_diagonals(dbb_acc)]
    dc_re, dc_im = _pair_diagonals(dc_acc)
    gsmall["ssm_c_re"], gsmall["ssm_c_im"] = dc_re, -dc_im
    dab = _colsum("ssm_dab", dab_rows)
    d_ar, d_ai, d_ld, d_br, d_bi = _ssm_params_bwd(
        "ssm_params_bwd", small["ssm_a_re"], small["ssm_a_im"], small["ssm_log_dt"].reshape(SSM_GROUPS, 1),
        small["ssm_b_re"].transpose(2, 0, 1), small["ssm_b_im"].transpose(2, 0, 1),
        dab[:, :SSM_LANES].reshape(SSM_GROUPS, SSM_STATE), dab[:, SSM_LANES:].reshape(SSM_GROUPS, SSM_STATE),
        dbb_re, dbb_im)
    gsmall["ssm_a_re"], gsmall["ssm_a_im"], gsmall["ssm_log_dt"] = d_ar, d_ai, d_ld.reshape(SSM_GROUPS)
    gsmall["ssm_b_re"], gsmall["ssm_b_im"] = d_br.transpose(1, 2, 0), d_bi.transpose(1, 2, 0)

    gfull["w_attn_branch"] = _mm_tn("dw_attn_branch", dya, oa, out_dtype=BF16)
    doa, = _mm("attn_branch_bwd", [(dya, wf["w_attn_branch"])], False, ATTN_OUT, [F32])
    dc = _combine_bwd("attn_combine_bwd", doa, oa_f32, lse_g)
    dqkv_cols = [None] * 9
    dtable = []
    for g in range(N_GROUPS):
        dq, dk, dv, db = _attn_bwd(f"attn_bwd_{g}", qkv, dc[g], lse_g[g], dc[3 + g], g, bias4[g])
        dqkv_cols[g], dqkv_cols[3 + g], dqkv_cols[6 + g] = dq, dk, dv
        dt = _bias_bwd(f"rel_bias_bwd_{g}", tables[g], db.reshape(HEADS_PER_GROUP, -1, db.shape[-1]))
        dtable.append(dt[:, :HEADS_PER_GROUP])
    gsmall["rel_bias_table"] = jnp.concatenate(dtable, axis=1)

    gfull["w_in"] = jnp.concatenate([_mm_tn_stack("dw_in_qkv", dqkv_cols, hmix, out_dtype=BF16),
                                     _mm_tn_stack("dw_in_rest", [du, dzga, dzgs], hmix, out_dtype=BF16)], axis=0)
    sent = send_grads("mix", gfull)
    qkv_pairs = [(c, w_qkv[i * ATTN_OUT:(i + 1) * ATTN_OUT]) for i, c in enumerate(dqkv_cols)]

    def mix_norm_bwd(dh, xv, gain, dres):
        r, xh = _rms_parts(xv)
        return dres + _rms_bwd_dx(dh, gain, r, xh), jnp.sum(dh * xh, axis=0, keepdims=True)

    dx1, gsmall["mix_norm"] = _mm(
        "in_bwd", qkv_pairs + [(du, w_u), (dzga, w_g[:D_MODEL]), (dzgs, w_g[D_MODEL:])], False, D_MODEL, [F32],
        epilogue=mix_norm_bwd, row_sums=1, extras=[(x1, 0), (small["mix_norm"], 0), (dx2, 0)], tm=512, tn=D_MODEL,
        deps=sent)

    dx, dgg1, duu1, act1, gsmall["ffn1_norm"] = _ffn_bwd(
        "ffn1_bwd", dx1, xs, small["ffn1_norm"], gg1, uu1, wf["ffn1_w_gate"], wf["ffn1_w_up"], wf["ffn1_w_down"])
    sent = send_grads("small", gsmall)
    gfull["ffn1_w_gate"] = _mm_tn("ffn1_dwg", dgg1, h1, deps=sent, out_dtype=BF16, tk=FFN_DW_ROWS)
    gfull["ffn1_w_up"] = _mm_tn("ffn1_dwu", duu1, h1, out_dtype=BF16, tk=FFN_DW_ROWS)
    sent = send_grads("f1gu", gfull)
    gfull["ffn1_w_down"] = _mm_tn("ffn1_dwd", act1, dx1, scale=0.5, deps=sent, out_dtype=BF16, tk=FFN_DW_ROWS)
    send_grads("f1d", gfull)
    return dx, gsmall


def kernel(x, ffn1_norm, ffn1_w_gate, ffn1_w_up, ffn1_w_down, mix_norm, w_in, gate_bias, rel_bias_table, ssm_a_re, ssm_a_im, ssm_log_dt, ssm_b_re, ssm_b_im, ssm_c_re, ssm_c_im, ssm_d, ssm_w_glu, w_attn_branch, w_ssm_branch, w_out, ffn2_norm, ffn2_w_gate, ffn2_w_up, ffn2_w_down, final_norm, loss_target, m_ffn1_norm, m_ffn1_w_gate, m_ffn1_w_up, m_ffn1_w_down, m_mix_norm, m_w_in, m_gate_bias, m_rel_bias_table, m_ssm_a_re, m_ssm_a_im, m_ssm_log_dt, m_ssm_b_re, m_ssm_b_im, m_ssm_c_re, m_ssm_c_im, m_ssm_d, m_ssm_w_glu, m_w_attn_branch, m_w_ssm_branch, m_w_out, m_ffn2_norm, m_ffn2_w_gate, m_ffn2_w_up, m_ffn2_w_down, m_final_norm, v_ffn1_norm, v_ffn1_w_gate, v_ffn1_w_up, v_ffn1_w_down, v_mix_norm, v_w_in, v_gate_bias, v_rel_bias_table, v_ssm_a_re, v_ssm_a_im, v_ssm_log_dt, v_ssm_b_re, v_ssm_b_im, v_ssm_c_re, v_ssm_c_im, v_ssm_d, v_ssm_w_glu, v_w_attn_branch, v_w_ssm_branch, v_w_out, v_ffn2_norm, v_ffn2_w_gate, v_ffn2_w_up, v_ffn2_w_down, v_final_norm):
    given = dict(locals())
    shapes = {nm: given[nm].shape for nm in _ORDER}

    def strip(a):
        return a[0] if a.ndim >= 2 and a.shape[0] == 1 else a

    w = {nm: strip(given[nm]) for nm in _ORDER}
    m = {nm: strip(given["m_" + nm]) for nm in _ORDER}
    v = {nm: strip(given["v_" + nm]) for nm in _ORDER}
    for d in (w, m, v):
        d["rel_bias_table"] = d["rel_bias_table"].reshape(N_BUCKETS, N_GROUPS * HEADS_PER_GROUP)

    weight_phases = {"f1": _PHASES["f1gu"] + _PHASES["f1d"], "mix": _PHASES["mix"], "f2": _PHASES["f2"]}
    pending_w, w_rows, deps, zero = {}, {}, [], 0.0
    for phase, names in weight_phases.items():
        w_rows.update({nm: _to_rows(w[nm] + zero, nm) for nm in names})
        pending_w[phase] = _exchange_start(f"gather_{phase}_start", [w_rows[nm].astype(BF16) for nm in names],
                                           gather=True, deps=deps)
        deps = [pending_w[phase][4]]
        zero = pending_w["f1"][4][0, 0]
    m_rows = {nm: _to_rows(m[nm] + zero, nm) for nm in _SHARD_INFO}
    v_rows = {nm: _to_rows(v[nm] + zero, nm) for nm in _SHARD_INFO}
    small = {nm: w[nm] for nm in _SMALL}
    small_in = {nm: small[nm] + zero for nm in _SMALL}
    for nm in ("ffn1_norm", "mix_norm", "ffn2_norm", "gate_bias"):
        small_in[nm] = small_in[nm].reshape(1, -1)

    def weights_of(phase, after):
        if phase == "f1":
            after = list(after) + list(m_rows.values()) + list(v_rows.values())
        landed = _exchange_wait(f"gather_{phase}_wait", pending_w[phase], after, gather=True)
        return {nm: _full_weight(got, nm) for nm, got in zip(weight_phases[phase], landed)}

    pending_g = {}

    def send_grads(phase, grads):
        if phase == "small":
            gs_pack = _pack_small({nm: grads[nm].reshape(small[nm].shape) for nm in _SMALL}, last=grads["loss"])
            pending_g[phase] = _exchange_start("gather_small_start", [gs_pack], gather=True)
        else:
            pending_g[phase] = _exchange_start(f"scatter_{phase}_start",
                                               [_grad_blocks(grads[nm], nm) for nm in _PHASES[phase]], gather=False)
        return [pending_g[phase][4]]

    dx, gsmall = _local_step(_residue_order(x[0]), _residue_order(loss_target[0]), small_in, weights_of, send_grads,
                             first_deps=[pending_w["f2"][4]])
    dx = _token_order(dx)

    updated = {}
    after = pending_g["f1d"][4]
    for phase in ("f2", "mix", "small", "f1gu", "f1d"):
        landed = _exchange_wait(f"exchange_{phase}_wait", pending_g[phase], after, gather=phase == "small")
        if phase == "small":
            sm = _adamw("adamw_small", _pack_small(small), _pack_small({nm: m[nm] for nm in _SMALL}),
                        _pack_small({nm: v[nm] for nm in _SMALL}), landed[0], landed[0].shape[1])
            after = sm[0]
            continue
        for nm, recv in zip(_PHASES[phase], landed):
            tr = max(t for t in range(16, 353, 16) if w_rows[nm].shape[0] % t == 0)
            updated[nm] = _adamw(f"adamw_{nm}", w_rows[nm], m_rows[nm], v_rows[nm], recv, tr)
            after = updated[nm][0]

    loss = sm[0][-8, 0]
    outs = []
    for i in range(4):
        sml = _unpack_small(sm[i], small)
        outs.append([(_from_rows(updated[nm][i], nm) if nm in updated else sml[nm]).reshape(shapes[nm])
                     for nm in _ORDER])
    return (loss, dx[None], *outs[0], *outs[1], *outs[2], *outs[3])
```

```python
import math

import numpy as np
import jax
import jax.numpy as jnp
from jax import lax
from jax.experimental import pallas as pl
from jax.experimental.pallas import tpu as pltpu

F32 = jnp.float32
BF16 = jnp.bfloat16

N_DEV = 8
D_MODEL = 1024
HEAD_DIM = 64
HEADS_PER_GROUP = 4
DILATIONS = (1, 4, 16)
N_GROUPS = 3
ATTN_WIDTH = 768
ATTN_OUT = 256
BLOCK = 128
N_BUCKETS = 32
MAX_DISTANCE = 2048
NEG_INF = -1e30
SSM_WIDTH = 512
SSM_GROUPS = 32
SSM_GROUP = 16
SSM_STATE = 64
SSM_LANES = SSM_GROUPS * SSM_STATE
SSM_PAIRS = SSM_GROUPS // 2
PAIR_LANES = 2 * SSM_STATE
PAIR_TILE = 256
EPS = 1e-6
LR, B1, B2, ADAM_EPS, WD, STEP = 0.001, 0.9, 0.999, 1e-08, 0.01, 10

VMEM_LIMIT_BYTES = 56 * 1024 * 1024
FFN_CHUNK = 768
FFN_DW_ROWS = 2048
SCAN_BLOCK = 256
SCAN_STEPS = 16
SCAN_COLS = SCAN_BLOCK // SCAN_STEPS
SCAN_SUB = 8
SCAN_LANES = 512

MESH = pl.DeviceIdType.MESH


def _cparams(*sem):
    return pltpu.CompilerParams(dimension_semantics=sem, vmem_limit_bytes=VMEM_LIMIT_BYTES)


def _dot(a, b, dims):
    return lax.dot_general(a, b, (dims, ((), ())), preferred_element_type=F32)


def _dot_nn(a, b):
    return _dot(a, b, ((1,), (0,)))


def _dot_nt(a, b):
    return _dot(a, b, ((1,), (1,)))


def _dot_tn(a, b):
    return _dot(a, b, ((0,), (0,)))


def _sigmoid(x):
    return 1.0 / (1.0 + jnp.exp(-x))


_HBM_SPEC = pl.BlockSpec(memory_space=pltpu.HBM)
_SEM_SPEC = pl.BlockSpec(memory_space=pltpu.SEMAPHORE)
_ANY_SPEC = pl.BlockSpec(memory_space=pl.ANY)
_EFFECT = pltpu.SideEffectType.DATAFLOW_SIDE_EFFECTING


def _peers(x, y, c):
    return [(1 - x if k & 4 else x, 1 - y if k & 2 else y, 1 - c if k & 1 else c) for k in range(1, N_DEV)]


def _exchange_copies(x_refs, land_refs, send_sems, recv_sems, gather):
    x, y, c = lax.axis_index("x"), lax.axis_index("y"), lax.axis_index("c")
    me = 4 * x + 2 * y + c
    copies = []
    for a, (x_ref, land_ref) in enumerate(zip(x_refs, land_refs)):
        for k, (px, py, pc) in enumerate(_peers(x, y, c)):
            src = x_ref if gather else x_ref.at[4 * px + 2 * py + pc]
            copies.append(pltpu.make_async_remote_copy(
                src_ref=src, dst_ref=land_ref.at[me], send_sem=send_sems.at[N_DEV * a + k],
                recv_sem=recv_sems.at[(N_DEV - 1) * a + k], device_id=(px, py, pc), device_id_type=MESH))
    owns = [pltpu.make_async_copy(x_ref if gather else x_ref.at[me], land_ref.at[me],
                                  send_sems.at[N_DEV * a + N_DEV - 1])
            for a, (x_ref, land_ref) in enumerate(zip(x_refs, land_refs))]
    return owns, copies


def _exchange_start(name, xs_list, gather, deps=()):
    n, nd = len(xs_list), len(deps)
    land_shapes = [(N_DEV, *xs.shape) if gather else xs.shape for xs in xs_list]

    def body(*refs):
        x_refs, land_refs = refs[:n], refs[n:2 * n]
        send_sems, recv_sems = refs[2 * n + nd:2 * n + nd + 2]
        token = refs[-1]
        owns, copies = _exchange_copies(x_refs, land_refs, send_sems, recv_sems, gather)
        for cp in copies + owns:
            cp.start()
        token[...] = jnp.zeros_like(token)

    hbm = lambda a: pltpu.with_memory_space_constraint(a, pltpu.HBM)
    outs = pl.pallas_call(
        body, name=name,
        out_shape=(pltpu.SemaphoreType.DMA((n * N_DEV,)), pltpu.SemaphoreType.DMA((n * (N_DEV - 1),)),
                   *[pltpu.HBM(xs.shape, xs.dtype) for xs in xs_list],
                   *[pltpu.HBM(shape, xs.dtype) for shape, xs in zip(land_shapes, xs_list)],
                   jax.ShapeDtypeStruct((8, 128), F32)),
        in_specs=(_HBM_SPEC,) * (2 * n) + (_ANY_SPEC,) * nd,
        out_specs=(_SEM_SPEC, _SEM_SPEC) + (_HBM_SPEC,) * (2 * n) + (pl.BlockSpec(memory_space=pltpu.VMEM),),
        input_output_aliases={i: 2 + i for i in range(2 * n)},
        compiler_params=pltpu.CompilerParams(has_side_effects=_EFFECT),
    )(*[hbm(xs) for xs in xs_list], *[hbm(lax.empty(shape, xs.dtype)) for shape, xs in zip(land_shapes, xs_list)],
      *deps)
    return outs[0], outs[1], list(outs[2:2 + n]), list(outs[2 + n:2 + 2 * n]), outs[-1]


def _exchange_wait(name, handle, after, gather):
    send_sems, recv_sems, xs_thru, lands_thru, _ = handle
    n = len(xs_thru)
    after = list(after) if isinstance(after, (list, tuple)) else [after]

    def body(*refs):
        x_refs, land_refs = refs[:n], refs[n:2 * n]
        send_sems, recv_sems = refs[2 * n:2 * n + 2]
        owns, copies = _exchange_copies(x_refs, land_refs, send_sems, recv_sems, gather)
        for cp in copies:
            cp.wait_send()
            cp.wait_recv()
        for cp in owns:
            cp.wait()

    outs = pl.pallas_call(
        body, name=name,
        out_shape=tuple(pltpu.HBM(a.shape, a.dtype) for a in xs_thru + lands_thru),
        in_specs=(_HBM_SPEC,) * (2 * n) + (_SEM_SPEC, _SEM_SPEC) + (_ANY_SPEC,) * len(after),
        out_specs=(_HBM_SPEC,) * (2 * n), input_output_aliases={i: i for i in range(2 * n)},
        compiler_params=pltpu.CompilerParams(has_side_effects=_EFFECT),
    )(*xs_thru, *lands_thru, send_sems, recv_sems, *after)
    return list(outs[n:])


def _mm(name, pairs, nt, n_cols, out_dtypes, epilogue=None, extras=(), tm=1024, tn=512, deps=(), row_sums=0,
        out_cols=None):
    rows = pairs[0][0].shape[0]
    tm = min(tm, rows)
    tn = min(tn, n_cols)
    na, ne, nd, no = len(pairs), len(extras), len(deps), len(out_dtypes)

    def body(*refs):
        a_refs, w_refs = refs[:na], refs[na:2 * na]
        e_refs, o_refs = refs[2 * na:2 * na + ne], refs[2 * na + ne + nd:]
        acc = None
        for a_ref, w_ref in zip(a_refs, w_refs):
            a = a_ref[...].astype(BF16)
            w = w_ref[...].astype(BF16)
            p = _dot_nt(a, w) if nt else _dot_nn(a, w)
            acc = p if acc is None else acc + p
        outs = (acc,) if epilogue is None else epilogue(acc, *[e[...].astype(F32) for e in e_refs])
        for o_ref, o in zip(o_refs[:no], outs[:no]):
            o_ref[...] = o.astype(o_ref.dtype)
        for r_ref, o in zip(o_refs[no:], outs[no:]):
            @pl.when(pl.program_id(0) == 0)
            def _():
                r_ref[...] = jnp.zeros_like(r_ref)

            r_ref[...] += o

    in_specs = [pl.BlockSpec((tm, a.shape[1]), lambda i, j: (i, 0)) for a, _ in pairs]
    for _, w in pairs:
        if nt:
            in_specs.append(pl.BlockSpec((tn, w.shape[1]), lambda i, j: (j, 0)))
        else:
            in_specs.append(pl.BlockSpec((w.shape[0], tn), lambda i, j: (0, j)))
    for e, col_off in extras:
        off = col_off // tn
        if e.shape[0] == 1:
            in_specs.append(pl.BlockSpec((1, tn), lambda i, j, off=off: (0, j + off)))
        else:
            in_specs.append(pl.BlockSpec((tm, tn), lambda i, j, off=off: (i, j + off)))
    in_specs += [_ANY_SPEC] * nd
    if out_cols is None:
        out_cols = [n_cols] * no
    else:
        assert tn == n_cols, "outputs of other widths need the whole row in one block"
    assert not row_sums or tn == n_cols
    out_specs = [pl.BlockSpec((tm, tn * c // n_cols), lambda i, j: (i, j)) for c in out_cols]
    out_specs += [pl.BlockSpec((1, tn), lambda i, j: (0, j))] * row_sums
    out_shape = [jax.ShapeDtypeStruct((rows, c), dt) for c, dt in zip(out_cols, out_dtypes)]
    out_shape += [jax.ShapeDtypeStruct((1, n_cols), F32)] * row_sums
    outs = pl.pallas_call(
        body, name=name, grid=(rows // tm, n_cols // tn),
        in_specs=in_specs, out_specs=out_specs, out_shape=out_shape,
        compiler_params=_cparams("arbitrary" if row_sums else "parallel", "arbitrary"),
    )(*[a for a, _ in pairs], *[w for _, w in pairs], *[e for e, _ in extras], *deps)
    return outs


def _tn_rows(m):
    return max(b for b in range(128, min(m, 1408) + 1, 128) if m % b == 0)


def _mm_tn(name, a, b, scale=1.0, bm=None, tk=1024, deps=(), out_dtype=F32):
    rows, m = a.shape
    n = b.shape[1]
    bm = _tn_rows(m) if bm is None else bm
    tk = min(tk, rows)
    nk = rows // tk

    def body(a_ref, b_ref, *rest):
        o_ref, acc_ref = rest[-2:]
        k = pl.program_id(1)

        @pl.when(k == 0)
        def _():
            acc_ref[...] = jnp.zeros_like(acc_ref)

        acc_ref[...] += _dot_tn(a_ref[...].astype(BF16), b_ref[...].astype(BF16))

        @pl.when(k == nk - 1)
        def _():
            o_ref[...] = (acc_ref[...] * scale).astype(o_ref.dtype)

    return pl.pallas_call(
        body, name=name, grid=(m // bm, nk),
        in_specs=[pl.BlockSpec((tk, bm), lambda i, k: (k, i)), pl.BlockSpec((tk, n), lambda i, k: (k, 0))]
        + [_ANY_SPEC] * len(deps),
        out_specs=pl.BlockSpec((bm, n), lambda i, k: (i, 0)),
        out_shape=jax.ShapeDtypeStruct((m, n), out_dtype),
        scratch_shapes=[pltpu.VMEM((bm, n), F32)],
        compiler_params=_cparams("parallel", "arbitrary"),
    )(a, b, *deps)


def _mm_tn_stack(name, a_list, b, tk=1024, out_dtype=F32):
    rows, n = b.shape
    ms = [a.shape[1] for a in a_list]
    tk = min(tk, rows)
    nk = rows // tk
    na = len(a_list)

    def body(*refs):
        a_refs, b_ref, o_ref, acc_ref = refs[:na], refs[na], refs[na + 1], refs[na + 2]
        k = pl.program_id(0)

        @pl.when(k == 0)
        def _():
            acc_ref[...] = jnp.zeros_like(acc_ref)

        bv = b_ref[...].astype(BF16)
        r0 = 0
        for a_ref, m in zip(a_refs, ms):
            acc_ref[r0:r0 + m, :] += _dot_tn(a_ref[...].astype(BF16), bv)
            r0 += m

        @pl.when(k == nk - 1)
        def _():
            o_ref[...] = acc_ref[...].astype(o_ref.dtype)

    return pl.pallas_call(
        body, name=name, grid=(nk,),
        in_specs=[pl.BlockSpec((tk, m), lambda k: (k, 0)) for m in ms] + [pl.BlockSpec((tk, n), lambda k: (k, 0))],
        out_specs=pl.BlockSpec((sum(ms), n), lambda k: (0, 0)),
        out_shape=jax.ShapeDtypeStruct((sum(ms), n), out_dtype),
        scratch_shapes=[pltpu.VMEM((sum(ms), n), F32)],
        compiler_params=_cparams("arbitrary"),
    )(*a_list, b)


def _colsum(name, xs, tm=512):
    rows, cols = xs.shape
    tm = min(tm, rows)

    def body(x_ref, o_ref):
        @pl.when(pl.program_id(0) == 0)
        def _():
            o_ref[...] = jnp.zeros_like(o_ref)

        o_ref[...] += jnp.sum(x_ref[...].astype(F32), axis=0, keepdims=True)

    return pl.pallas_call(
        body, name=name, grid=(rows // tm,),
        in_specs=[pl.BlockSpec((tm, cols), lambda i: (i, 0))],
        out_specs=pl.BlockSpec((1, cols), lambda i: (0, 0)),
        out_shape=jax.ShapeDtypeStruct((1, cols), F32),
        compiler_params=_cparams("arbitrary"),
    )(xs)


def _ew(name, fn, ins, out_cols, out_dtypes, tm=512):
    rows = ins[0].shape[0]
    tm = min(tm, rows)
    ni = len(ins)

    def body(*refs):
        outs = fn(*[r[...] for r in refs[:ni]])
        for o_ref, o in zip(refs[ni:], outs):
            o_ref[...] = o.astype(o_ref.dtype)

    def spec(shape):
        if shape[0] == 1:
            return pl.BlockSpec((1, shape[1]), lambda i: (0, 0))
        return pl.BlockSpec((tm, shape[1]), lambda i: (i, 0))

    return pl.pallas_call(
        body, name=name, grid=(rows // tm,),
        in_specs=[spec(a.shape) for a in ins],
        out_specs=[pl.BlockSpec((tm, c), lambda i: (i, 0)) for c in out_cols],
        out_shape=[jax.ShapeDtypeStruct((rows, c), dt) for c, dt in zip(out_cols, out_dtypes)],
        compiler_params=_cparams("parallel"),
    )(*ins)


def _rms_parts(xv):
    r = lax.rsqrt(jnp.mean(xv * xv, axis=-1, keepdims=True) + EPS)
    return r, xv * r


def _rms_bwd_dx(dh, gain, r, xh):
    dxh = dh * gain
    return r * (dxh - xh * jnp.mean(dxh * xh, axis=-1, keepdims=True))


def _ffn_chunks(f_all):
    return [slice(c, min(c + FFN_CHUNK, f_all)) for c in range(0, f_all, FFN_CHUNK)]


def _loss_head(xo, gain_f, target, d):
    r, xh = _rms_parts(xo)
    err = xh * gain_f - target
    dy = err * (1.0 / d)
    per_tok = jnp.mean(err * err, axis=-1, keepdims=True)
    return (_rms_bwd_dx(dy, gain_f, r, xh), jnp.sum(dy * xh, axis=0, keepdims=True),
            0.5 * jnp.sum(per_tok, axis=0, keepdims=True))


def _ffn_tile(x_ref, g_ref, wg_ref, wu_ref, wd_ref, h_ref, gg_ref, uu_ref):
    xv = x_ref[...]
    _, xh = _rms_parts(xv)
    h = (xh * g_ref[...]).astype(BF16)
    h_ref[...] = h
    acc = None
    for cols in _ffn_chunks(wd_ref.shape[0]):
        gg = _dot_nt(h, wg_ref[cols, :])
        uu = _dot_nt(h, wu_ref[cols, :])
        act = gg * _sigmoid(gg) * uu
        part = _dot_nn(act.astype(BF16), wd_ref[cols, :])
        acc = part if acc is None else acc + part
        gg_ref[:, cols] = gg.astype(BF16)
        uu_ref[:, cols] = uu.astype(BF16)
    return xv + 0.5 * acc


def _ffn_fwd(name, xs, gain, wg_t, wu_t, wd, next_gain, tm=512, deps=()):
    rows, d = xs.shape
    f_all = wd.shape[0]
    tm = min(tm, rows)

    def body(x_ref, g_ref, wg_ref, wu_ref, wd_ref, ng_ref, *rest):
        xo_ref, h_ref, gg_ref, uu_ref, hn_ref = rest[-5:]
        xo = _ffn_tile(x_ref, g_ref, wg_ref, wu_ref, wd_ref, h_ref, gg_ref, uu_ref)
        xo_ref[...] = xo
        hn_ref[...] = (_rms_parts(xo)[1] * ng_ref[...]).astype(BF16)

    tile = pl.BlockSpec((tm, d), lambda i: (i, 0))
    row = pl.BlockSpec((1, d), lambda i: (0, 0))
    wspec = pl.BlockSpec((f_all, d), lambda i: (0, 0), pipeline_mode=pl.Buffered(1))
    hid = pl.BlockSpec((tm, f_all), lambda i: (i, 0))
    return pl.pallas_call(
        body, name=name, grid=(rows // tm,),
        in_specs=[tile, row, wspec, wspec, wspec, row] + [_ANY_SPEC] * len(deps),
        out_specs=[tile, tile, hid, hid, tile],
        out_shape=[jax.ShapeDtypeStruct((rows, d), F32), jax.ShapeDtypeStruct((rows, d), BF16),
                   jax.ShapeDtypeStruct((rows, f_all), BF16), jax.ShapeDtypeStruct((rows, f_all), BF16),
                   jax.ShapeDtypeStruct((rows, d), BF16)],
        compiler_params=_cparams("parallel"),
    )(xs, gain, wg_t, wu_t, wd, next_gain, *deps)


def _ffn_fwd_head(name, xs, gain, wg_t, wu_t, wd, gain_f, target, tm=512):
    rows, d = xs.shape
    f_all = wd.shape[0]
    tm = min(tm, rows)

    def body(x_ref, g_ref, wg_ref, wu_ref, wd_ref, gf_ref, t_ref, dxo_ref, h_ref, gg_ref, uu_ref, dgf_ref, loss_ref):
        xo = _ffn_tile(x_ref, g_ref, wg_ref, wu_ref, wd_ref, h_ref, gg_ref, uu_ref)
        dxo, dgf, loss = _loss_head(xo, gf_ref[...], t_ref[...], d)
        dxo_ref[...] = dxo

        @pl.when(pl.program_id(0) == 0)
        def _():
            dgf_ref[...] = jnp.zeros_like(dgf_ref)
            loss_ref[...] = jnp.zeros_like(loss_ref)

        dgf_ref[...] += dgf
        loss_ref[...] += loss

    tile = pl.BlockSpec((tm, d), lambda i: (i, 0))
    row = pl.BlockSpec((1, d), lambda i: (0, 0))
    wspec = pl.BlockSpec((f_all, d), lambda i: (0, 0), pipeline_mode=pl.Buffered(1))
    hid = pl.BlockSpec((tm, f_all), lambda i: (i, 0))
    return pl.pallas_call(
        body, name=name, grid=(rows // tm,),
        in_specs=[tile, row, wspec, wspec, wspec, row, tile],
        out_specs=[tile, tile, hid, hid, row, pl.BlockSpec((1, 1), lambda i: (0, 0))],
        out_shape=[jax.ShapeDtypeStruct((rows, d), F32), jax.ShapeDtypeStruct((rows, d), BF16),
                   jax.ShapeDtypeStruct((rows, f_all), BF16), jax.ShapeDtypeStruct((rows, f_all), BF16),
                   jax.ShapeDtypeStruct((1, d), F32), jax.ShapeDtypeStruct((1, 1), F32)],
        compiler_params=_cparams("arbitrary"),
    )(xs, gain, wg_t, wu_t, wd, gain_f, target)


def _ffn_bwd(name, dxo, xs, gain, gg_all, uu_all, wg_t, wu_t, wd, tm=256):
    rows, d = xs.shape
    f_all = wd.shape[0]
    tm = min(tm, rows)

    def body(dxo_ref, x_ref, g_ref, gg_ref, uu_ref, wg_ref, wu_ref, wd_ref,
             dx_ref, dgg_ref, duu_ref, act_ref, dgain_ref):
        dxo = dxo_ref[...]
        df = (0.5 * dxo).astype(BF16)
        dh = None
        for cols in _ffn_chunks(f_all):
            gg = gg_ref[:, cols].astype(F32)
            uu = uu_ref[:, cols].astype(F32)
            sg = _sigmoid(gg)
            silu = gg * sg
            dact = _dot_nt(df, wd_ref[cols, :])
            duu = (dact * silu).astype(BF16)
            dgg = (dact * uu * (sg * (1.0 + gg * (1.0 - sg)))).astype(BF16)
            act_ref[:, cols] = (silu * uu).astype(BF16)
            dgg_ref[:, cols] = dgg
            duu_ref[:, cols] = duu
            part = _dot_nn(dgg, wg_ref[cols, :]) + _dot_nn(duu, wu_ref[cols, :])
            dh = part if dh is None else dh + part
        r, xh = _rms_parts(x_ref[...])
        dx_ref[...] = dxo + _rms_bwd_dx(dh, g_ref[...], r, xh)

        @pl.when(pl.program_id(0) == 0)
        def _():
            dgain_ref[...] = jnp.zeros_like(dgain_ref)

        dgain_ref[...] += jnp.sum(dh * xh, axis=0, keepdims=True)

    tile = pl.BlockSpec((tm, d), lambda i: (i, 0))
    row = pl.BlockSpec((1, d), lambda i: (0, 0))
    wspec = pl.BlockSpec((f_all, d), lambda i: (0, 0), pipeline_mode=pl.Buffered(1))
    hid = pl.BlockSpec((tm, f_all), lambda i: (i, 0))
    hid_shape = jax.ShapeDtypeStruct((rows, f_all), BF16)
    return pl.pallas_call(
        body, name=name, grid=(rows // tm,),
        in_specs=[tile, tile, row, hid, hid, wspec, wspec, wspec],
        out_specs=[tile, hid, hid, hid, row],
        out_shape=[jax.ShapeDtypeStruct((rows, d), F32), hid_shape, hid_shape, hid_shape,
                   jax.ShapeDtypeStruct((1, d), F32)],
        compiler_params=_cparams("arbitrary"),
    )(dxo, xs, gain, gg_all, uu_all, wg_t, wu_t, wd)


def _t5_bucket_np(dist):
    max_exact = N_BUCKETS // 2
    dd = np.maximum(dist, 1).astype(np.float32)
    large = max_exact + (np.log(dd / np.float32(max_exact)) / np.float32(math.log(MAX_DISTANCE / max_exact))
                         * np.float32(N_BUCKETS - max_exact)).astype(np.int32)
    large = np.minimum(large, N_BUCKETS - 1)
    return np.where(dist < max_exact, dist, large).astype(np.int32)


def _attn_geometry(g, rows):
    run = rows // 16
    dil = DILATIONS[g]
    if dil == 16:
        bq = BLOCK
        return dict(view=(16, run), block=(None, bq), grid=(16, run // bq), index=lambda r, n: (r, n),
                    pos=np.arange(bq), bq=bq)
    if dil == 4:
        per = BLOCK // 4
        pos = (4 * np.arange(per)[None, :] + np.arange(4)[:, None]).reshape(-1)
        return dict(view=(4, 4, run), block=(4, None, per), grid=(4, run // per), index=lambda r, n: (0, r, n),
                    pos=pos, bq=BLOCK)
    per = 16
    pos = (16 * np.arange(per)[None, :] + np.arange(16)[:, None]).reshape(-1)
    return dict(view=(16, run), block=(16, per), grid=(1, run // per), index=lambda r, n: (0, n),
                pos=pos, bq=16 * per)


def _attn_tables(g, rows):
    geo = _attn_geometry(g, rows)
    pos, bq = geo["pos"], geo["bq"]
    steps = pos[:, None] - np.concatenate([pos - bq, pos])[None, :]
    valid = (steps >= 0) & (steps <= BLOCK)
    bucket = _t5_bucket_np((np.maximum(steps, 0) * DILATIONS[g]).astype(np.int32))
    return bucket, valid.astype(np.int32)


def _bias_fwd(name, bucket, valid, table_t):
    bq = bucket.shape[0]

    def body(bk_ref, ok_ref, tab_ref, o_ref):
        bk = bk_ref[...]
        ok = ok_ref[...] > 0
        accs = [jnp.zeros(bk.shape, F32)] * HEADS_PER_GROUP
        for b in range(N_BUCKETS):
            hit = bk == b
            accs = [jnp.where(hit, tab_ref[h, b], acc) for h, acc in enumerate(accs)]
        for h, acc in enumerate(accs):
            o_ref[h] = jnp.where(ok, acc, NEG_INF)

    vm = pl.BlockSpec(memory_space=pltpu.VMEM)
    return pl.pallas_call(
        body, name=name, in_specs=[vm, vm, pl.BlockSpec(memory_space=pltpu.SMEM)], out_specs=vm,
        out_shape=jax.ShapeDtypeStruct((HEADS_PER_GROUP, bq, 2 * bq), F32),
    )(bucket, valid, table_t)


def _bias_bwd(name, bucket, dbias):
    def body(bk_ref, db_ref, o_ref):
        row_id = lax.broadcasted_iota(jnp.int32, (N_BUCKETS, 128), 0)
        col_id = lax.broadcasted_iota(jnp.int32, (N_BUCKETS, 128), 1)
        bk = bk_ref[...]
        acc = jnp.zeros((N_BUCKETS, 128), F32)
        for h in range(HEADS_PER_GROUP):
            db = db_ref[h]
            for b in range(N_BUCKETS):
                part = jnp.sum(jnp.where(bk == b, db, 0.0), axis=0, keepdims=True)
                tot = jnp.sum(part, axis=1, keepdims=True)
                acc = jnp.where((row_id == b) & (col_id == h), tot, acc)
        o_ref[...] = acc

    vm = pl.BlockSpec(memory_space=pltpu.VMEM)
    return pl.pallas_call(body, name=name, in_specs=[vm, vm], out_specs=vm,
                          out_shape=jax.ShapeDtypeStruct((N_BUCKETS, 128), F32))(bucket, dbias)


def _head_of_lane(nrows):
    return lax.broadcasted_iota(jnp.int32, (nrows, ATTN_OUT), 1) // HEAD_DIM


def _stack_heads(a, lane_head):
    zero = jnp.zeros_like(a)
    return jnp.concatenate([jnp.where(lane_head == h, a, zero) for h in range(HEADS_PER_GROUP)], axis=0)


def _unstack_heads(a4, lane_head, bq):
    out = a4[:bq]
    for h in range(1, HEADS_PER_GROUP):
        out = jnp.where(lane_head == h, a4[h * bq:(h + 1) * bq], out)
    return out


def _attn_specs(geo, cols, col_block, index):
    return pl.BlockSpec(geo["block"] + (cols,), lambda r, n: index(r, n) + (col_block,))


def _attn_fwd(name, qkv, g, bias4):
    rows = qkv.shape[0]
    geo = _attn_geometry(g, rows)
    bq, (nsub, nb), index = geo["bq"], geo["grid"], geo["index"]
    blk_shape = tuple(b for b in geo["block"] if b is not None) + (ATTN_OUT,)

    def body(q_ref, kc_ref, kp_ref, vc_ref, vp_ref, b_ref, o_ref, lse_ref):
        n = pl.program_id(1)
        lane_head = _head_of_lane(bq)
        flat = lambda ref: ref[...].reshape(bq, ATTN_OUT)
        q4 = _stack_heads(flat(q_ref), lane_head)
        k2 = jnp.concatenate([flat(kp_ref), flat(kc_ref)], axis=0)
        v2 = jnp.concatenate([flat(vp_ref), flat(vc_ref)], axis=0)
        s = _dot_nt(q4, k2) + b_ref[...]
        col = lax.broadcasted_iota(jnp.int32, s.shape, 1)
        s = jnp.where((col >= bq) | (n > 0), s, NEG_INF)
        mx = jnp.max(s, axis=-1, keepdims=True)
        p = jnp.exp(s - mx)
        den = jnp.sum(p, axis=-1, keepdims=True)
        o4 = _dot_nn(p.astype(BF16), v2) / den
        lse4 = jnp.broadcast_to(mx + jnp.log(den), (HEADS_PER_GROUP * bq, ATTN_OUT))
        o_ref[...] = _unstack_heads(o4, lane_head, bq).reshape(blk_shape)
        lse_ref[...] = _unstack_heads(lse4, lane_head, bq).reshape(blk_shape)

    prev = lambda r, n: index(r, jnp.maximum(n - 1, 0))
    view = lambda a: a.reshape(geo["view"] + (a.shape[1],))
    qkv_v = view(qkv)
    out_spec = _attn_specs(geo, ATTN_OUT, 0, index)
    out_shape = jax.ShapeDtypeStruct(geo["view"] + (ATTN_OUT,), F32)
    o, lse = pl.pallas_call(
        body, name=name, grid=(nsub, nb),
        in_specs=[_attn_specs(geo, ATTN_OUT, g, index), _attn_specs(geo, ATTN_OUT, 3 + g, index),
                  _attn_specs(geo, ATTN_OUT, 3 + g, prev), _attn_specs(geo, ATTN_OUT, 6 + g, index),
                  _attn_specs(geo, ATTN_OUT, 6 + g, prev), pl.BlockSpec(bias4.shape, lambda r, n: (0, 0))],
        out_specs=[out_spec, out_spec], out_shape=[out_shape, out_shape],
        compiler_params=_cparams("parallel", "arbitrary"),
    )(qkv_v, qkv_v, qkv_v, qkv_v, qkv_v, bias4)
    return o.reshape(rows, ATTN_OUT), lse.reshape(rows, ATTN_OUT)


def _attn_bwd(name, qkv, do, lse, cvec, g, bias4):
    rows = qkv.shape[0]
    geo = _attn_geometry(g, rows)
    bq, (nsub, nb), index = geo["bq"], geo["grid"], geo["index"]
    blk_shape = tuple(b for b in geo["block"] if b is not None) + (ATTN_OUT,)
    nlead = len(blk_shape) - 1

    def body(q_ref, kc_ref, kp_ref, vc_ref, vp_ref, do_ref, lse_ref, c_ref, b_ref,
             dq_ref, dk_ref, dv_ref, db_ref, kcar_ref, vcar_ref):
        r, n = pl.program_id(0), pl.program_id(1)
        valid = n < nb
        lane_head = _head_of_lane(bq)
        flat = lambda ref: ref[...].reshape(bq, ATTN_OUT)

        @pl.when((r == 0) & (n == 0))
        def _():
            kcar_ref[...] = jnp.zeros_like(kcar_ref)
            vcar_ref[...] = jnp.zeros_like(vcar_ref)
            db_ref[...] = jnp.zeros_like(db_ref)

        def column(ref, h):
            lead = (slice(None),) * nlead
            return ref[lead + (pl.ds(h * HEAD_DIM, 1),)].reshape(bq, 1)

        q4 = _stack_heads(flat(q_ref), lane_head)
        do4 = _stack_heads(flat(do_ref), lane_head)
        k2 = jnp.concatenate([flat(kp_ref), flat(kc_ref)], axis=0)
        v2 = jnp.concatenate([flat(vp_ref), flat(vc_ref)], axis=0)
        lse4 = jnp.concatenate([column(lse_ref, h) for h in range(HEADS_PER_GROUP)], axis=0)
        c4 = jnp.concatenate([column(c_ref, h) for h in range(HEADS_PER_GROUP)], axis=0)
        s = _dot_nt(q4, k2) + b_ref[...]
        col = lax.broadcasted_iota(jnp.int32, s.shape, 1)
        keep = ((col >= bq) | (n > 0)) & valid
        p = jnp.where(keep, jnp.exp(s - lse4), 0.0)
        ds = p * (_dot_nt(do4, v2) + c4)
        ds_b = ds.astype(BF16)

        @pl.when(valid)
        def _():
            dq = _unstack_heads(_dot_nn(ds_b, k2), lane_head, bq) * (HEAD_DIM ** -0.5)
            dq_ref[...] = dq.astype(BF16).reshape(blk_shape)

        dk2 = _dot_tn(ds_b, q4)
        dv2 = _dot_tn(p.astype(BF16), do4)
        dk_ref[...] = (kcar_ref[...] + dk2[:bq]).astype(BF16).reshape(blk_shape)
        dv_ref[...] = (vcar_ref[...] + dv2[:bq]).astype(BF16).reshape(blk_shape)
        kcar_ref[...] = dk2[bq:]
        vcar_ref[...] = dv2[bq:]
        db_ref[...] += ds

    cur = lambda r, n: index(r, jnp.minimum(n, nb - 1))
    prev = lambda r, n: index(r, jnp.maximum(jnp.minimum(n, nb - 1) - 1, 0))
    late = lambda r, n: index(r, jnp.maximum(n - 1, 0))
    view = lambda a: a.reshape(geo["view"] + (a.shape[1],))
    qkv_v = view(qkv)
    tile = _attn_specs(geo, ATTN_OUT, 0, cur)
    bias_spec = pl.BlockSpec(bias4.shape, lambda r, n: (0, 0))
    out_shape = jax.ShapeDtypeStruct(geo["view"] + (ATTN_OUT,), BF16)
    dq, dk, dv, db = pl.pallas_call(
        body, name=name, grid=(nsub, nb + 1),
        in_specs=[_attn_specs(geo, ATTN_OUT, g, cur), _attn_specs(geo, ATTN_OUT, 3 + g, cur),
                  _attn_specs(geo, ATTN_OUT, 3 + g, prev), _attn_specs(geo, ATTN_OUT, 6 + g, cur),
                  _attn_specs(geo, ATTN_OUT, 6 + g, prev), tile, tile, tile, bias_spec],
        out_specs=[tile, _attn_specs(geo, ATTN_OUT, 0, late), _attn_specs(geo, ATTN_OUT, 0, late), bias_spec],
        out_shape=[out_shape, out_shape, out_shape, jax.ShapeDtypeStruct(bias4.shape, F32)],
        scratch_shapes=[pltpu.VMEM((bq, ATTN_OUT), F32), pltpu.VMEM((bq, ATTN_OUT), F32)],
        compiler_params=_cparams("arbitrary", "arbitrary"),
    )(qkv_v, qkv_v, qkv_v, qkv_v, qkv_v, view(do), view(lse), view(cvec), bias4)
    return dq.reshape(rows, ATTN_OUT), dk.reshape(rows, ATTN_OUT), dv.reshape(rows, ATTN_OUT), db


def _group_weights(lses):
    mx = jnp.maximum(jnp.maximum(lses[0], lses[1]), lses[2])
    es = [jnp.exp(l - mx) for l in lses]
    den = es[0] + es[1] + es[2]
    return [e / den for e in es]


def _combine_fwd(name, os_, lses):
    def fn(o0, o1, o2, l0, l1, l2):
        ws = _group_weights([l0, l1, l2])
        out = ws[0] * o0 + ws[1] * o1 + ws[2] * o2
        return out, out

    return _ew(name, fn, [*os_, *lses], [ATTN_OUT, ATTN_OUT], [F32, BF16], tm=1024)


def _combine_bwd(name, do, oa, lses):
    def fn(dov, oav, l0, l1, l2):
        head_sum = (lax.broadcasted_iota(jnp.int32, (ATTN_OUT, ATTN_OUT), 0) // HEAD_DIM
                    == lax.broadcasted_iota(jnp.int32, (ATTN_OUT, ATTN_OUT), 1) // HEAD_DIM)
        ws = _group_weights([l0, l1, l2])
        prod = dov * oav
        hi = prod.astype(BF16)
        lo = (prod - hi.astype(F32)).astype(BF16)
        ones = jnp.where(head_sum, 1.0, 0.0).astype(BF16)
        bar = _dot_nn(hi, ones) + _dot_nn(lo, ones)
        return tuple(w * dov for w in ws) + tuple(-w * bar for w in ws)

    return _ew(name, fn, [do, oa, *lses], [ATTN_OUT] * 6, [BF16] * 3 + [F32] * 3, tm=1024)


def _ssm_disc(a_re, a_im, log_dt, b_re, b_im):
    dt = jnp.exp(log_dt)
    mag = jnp.exp(a_re * dt)
    ab_re = mag * jnp.cos(a_im * dt)
    ab_im = mag * jnp.sin(a_im * dt)
    den = a_re * a_re + a_im * a_im
    xr = ab_re - 1.0
    coef_re = (xr * a_re + ab_im * a_im) / den
    coef_im = (ab_im * a_re - xr * a_im) / den
    bb_re = coef_re[None] * b_re - coef_im[None] * b_im
    bb_im = coef_re[None] * b_im + coef_im[None] * b_re
    return ab_re, ab_im, bb_re, bb_im


def _ssm_params_fwd(name, a_re, a_im, log_dt, b_re, b_im, c_re, c_im):
    pows = jax.ShapeDtypeStruct((SCAN_STEPS,) + a_re.shape, F32)
    mats = jax.ShapeDtypeStruct((SSM_PAIRS, PAIR_TILE, PAIR_TILE), BF16)
    per_tile = PAIR_TILE // (2 * SSM_GROUP)

    def body(ar, ai, ld, br, bi, cr, ci, o_pr, o_pi, o_bb, o_c, bbr_ref, bbi_ref, wide_ref):
        ab_re, ab_im, bb_re, bb_im = _ssm_disc(ar[...], ai[...], ld[...], br[...], bi[...])
        pr, pi = ab_re, ab_im
        for j in range(SCAN_STEPS):
            o_pr[j] = pr
            o_pi[j] = pi
            pr, pi = pr * ab_re - pi * ab_im, pr * ab_im + pi * ab_re
        bbr_ref[...] = bb_re
        bbi_ref[...] = bb_im

        def place(out_ref, block):
            wide_ref[...] = jnp.zeros_like(wide_ref)
            for g in range(SSM_GROUPS):
                p, l = divmod(g, 2)
                rows = pl.ds((p % per_tile) * 2 * SSM_GROUP + l * SSM_GROUP, SSM_GROUP)
                re, im = block(g)
                wide_ref[p, rows, pl.ds(l * SSM_STATE, SSM_STATE)] = re
                wide_ref[p, rows, pl.ds(PAIR_LANES + l * SSM_STATE, SSM_STATE)] = im
            out_ref[...] = wide_ref[...].astype(BF16)

        place(o_bb, lambda g: (bbr_ref[:, g, :], bbi_ref[:, g, :]))
        place(o_c, lambda g: (cr[g], -ci[g]))

    vm = pl.BlockSpec(memory_space=pltpu.VMEM)
    return pl.pallas_call(
        body, name=name, in_specs=[vm] * 7, out_specs=[vm] * 4, out_shape=[pows, pows, mats, mats],
        scratch_shapes=[pltpu.VMEM(b_re.shape, F32), pltpu.VMEM(b_re.shape, F32),
                        pltpu.VMEM((SSM_PAIRS, PAIR_TILE, PAIR_TILE), F32)],
    )(a_re, a_im, log_dt, b_re, b_im, c_re, c_im)


def _ssm_params_bwd(name, a_re, a_im, log_dt, b_re, b_im, d_ab_re, d_ab_im, d_bb_re, d_bb_im):
    gn = jax.ShapeDtypeStruct(a_re.shape, F32)
    cgn = jax.ShapeDtypeStruct(b_re.shape, F32)

    def body(ar, ai, ld, br, bi, g0, g1, g2, g3, o_ar, o_ai, o_ld, o_br, o_bi):
        _, vjp = jax.vjp(_ssm_disc, ar[...], ai[...], ld[...], br[...], bi[...])
        outs = vjp((g0[...], g1[...], g2[...], g3[...]))
        for o_ref, o in zip((o_ar, o_ai, o_ld, o_br, o_bi), outs):
            o_ref[...] = o

    vm = pl.BlockSpec(memory_space=pltpu.VMEM)
    return pl.pallas_call(body, name=name, in_specs=[vm] * 9, out_specs=[vm] * 5,
                          out_shape=[gn, gn, jax.ShapeDtypeStruct(log_dt.shape, F32), cgn, cgn],
                          )(a_re, a_im, log_dt, b_re, b_im, d_ab_re, d_ab_im, d_bb_re, d_bb_im)


def _scan_block(s_ref, carry_ref, tmp_ref, pw_ref, reverse, sprev=None):
    nl = SSM_LANES
    halves = range(SCAN_COLS // SCAN_SUB)
    zero = jnp.zeros((SCAN_SUB, SCAN_LANES), F32)
    for half in (reversed(halves) if reverse else halves):
        sub_rows = pl.ds(half * SCAN_SUB, SCAN_SUB)
        for lc in range(nl // SCAN_LANES):
            re_l = pl.ds(lc * SCAN_LANES, SCAN_LANES)
            im_l = pl.ds(nl + lc * SCAN_LANES, SCAN_LANES)
            are, aim = pw_ref[0, :, re_l], pw_ref[0, :, im_l]

            def step_of(j):
                return SCAN_STEPS - 1 - j if reverse else j

            def pass1(j, st):
                sr, si = st
                jj = step_of(j)
                nr = are * sr - aim * si + s_ref[jj, sub_rows, re_l]
                ni = are * si + aim * sr + s_ref[jj, sub_rows, im_l]
                s_ref[jj, sub_rows, re_l] = nr
                s_ref[jj, sub_rows, im_l] = ni
                return nr, ni

            er, ei = lax.fori_loop(0, SCAN_STEPS, pass1, (zero, zero), unroll=2)
            tmp_ref[0:SCAN_SUB, re_l] = er
            tmp_ref[0:SCAN_SUB, im_l] = ei
            apr, api = pw_ref[SCAN_STEPS - 1, 0:1, re_l], pw_ref[SCAN_STEPS - 1, 0:1, im_l]
            sr, si = carry_ref[0:1, re_l], carry_ref[0:1, im_l]
            for step in range(SCAN_SUB):
                c = SCAN_SUB - 1 - step if reverse else step
                tmp_ref[SCAN_SUB + c:SCAN_SUB + c + 1, re_l] = sr
                tmp_ref[SCAN_SUB + c:SCAN_SUB + c + 1, im_l] = si
                e_r, e_i = tmp_ref[c:c + 1, re_l], tmp_ref[c:c + 1, im_l]
                sr, si = apr * sr - api * si + e_r, apr * si + api * sr + e_i
            carry_ref[0:1, re_l] = sr
            carry_ref[0:1, im_l] = si
            cr = tmp_ref[SCAN_SUB:2 * SCAN_SUB, re_l]
            ci = tmp_ref[SCAN_SUB:2 * SCAN_SUB, im_l]

            if sprev is None:
                def pass2(j, st):
                    pr, pi = pw_ref[j, :, re_l], pw_ref[j, :, im_l]
                    jj = step_of(j)
                    s_ref[jj, sub_rows, re_l] += pr * cr - pi * ci
                    s_ref[jj, sub_rows, im_l] += pr * ci + pi * cr
                    return st

                lax.fori_loop(0, SCAN_STEPS, pass2, 0, unroll=2)
            else:
                st_ref, prev_ref, have_prev, dab_ref = sprev

                def corrected(jj, pr, pi):
                    gr = s_ref[jj, sub_rows, re_l] + pr * cr - pi * ci
                    gi = s_ref[jj, sub_rows, im_l] + pr * ci + pi * cr
                    s_ref[jj, sub_rows, re_l] = gr
                    s_ref[jj, sub_rows, im_l] = gi
                    return gr, gi

                def pass2(j, st):
                    dr, di = st
                    jj = SCAN_STEPS - 1 - j
                    gr, gi = corrected(jj, pw_ref[j, :, re_l], pw_ref[j, :, im_l])
                    qr, qi = st_ref[jj - 1, sub_rows, re_l], st_ref[jj - 1, sub_rows, im_l]
                    return dr + gr * qr + gi * qi, di + gi * qr - gr * qi

                dr, di = lax.fori_loop(0, SCAN_STEPS - 1, pass2, (zero, zero), unroll=2)
                gr, gi = corrected(0, pw_ref[SCAN_STEPS - 1, :, re_l], pw_ref[SCAN_STEPS - 1, :, im_l])
                sub = lax.broadcasted_iota(jnp.int32, (SCAN_SUB, SCAN_LANES), 0)
                if half == 0:
                    pv_r = prev_ref[SCAN_SUB - 1:SCAN_SUB, re_l] * have_prev
                    pv_i = prev_ref[SCAN_SUB - 1:SCAN_SUB, im_l] * have_prev
                else:
                    before = pl.ds(half * SCAN_SUB - 1, 1)
                    pv_r, pv_i = st_ref[SCAN_STEPS - 1, before, re_l], st_ref[SCAN_STEPS - 1, before, im_l]
                shape = (SCAN_SUB, SCAN_LANES)
                qr = jnp.where(sub == 0, jnp.broadcast_to(pv_r, shape),
                               pltpu.roll(st_ref[SCAN_STEPS - 1, sub_rows, re_l], 1, 0))
                qi = jnp.where(sub == 0, jnp.broadcast_to(pv_i, shape),
                               pltpu.roll(st_ref[SCAN_STEPS - 1, sub_rows, im_l], 1, 0))
                dab_ref[:, re_l] += dr + gr * qr + gi * qi
                dab_ref[:, im_l] += di + gi * qr - gr * qi


def _scan_view(a):
    return a.reshape(16, a.shape[0] // 16, a.shape[1])


def _pair_tile(p):
    start = (p * 2 * SSM_GROUP // PAIR_TILE) * PAIR_TILE
    return slice(start, start + PAIR_TILE)


def _pair_lanes(p):
    return pl.ds(p * PAIR_LANES, PAIR_LANES), pl.ds(SSM_LANES + p * PAIR_LANES, PAIR_LANES)


def _pair_store(s_ref, p, val):
    re_l, im_l = _pair_lanes(p)
    s_ref[:, :, re_l] = val[:, :PAIR_LANES].reshape(16, SCAN_COLS, PAIR_LANES)
    s_ref[:, :, im_l] = val[:, PAIR_LANES:].reshape(16, SCAN_COLS, PAIR_LANES)


def _pair_load(s_ref, p):
    re_l, im_l = _pair_lanes(p)
    parts = [s_ref[:, :, l].reshape(SCAN_BLOCK, PAIR_LANES) for l in (re_l, im_l)]
    return jnp.concatenate(parts, axis=1).astype(BF16)


def _pair_sum(fn):
    per = PAIR_TILE // (2 * SSM_GROUP)
    tiles = []
    for t in range(SSM_PAIRS // per):
        acc = None
        for p in range(t * per, (t + 1) * per):
            part = fn(p)
            acc = part if acc is None else acc + part
        tiles.append(acc)
    return jnp.concatenate(tiles, axis=1)


def _ssm_fwd(name, u, bb_mats, c_mats, pw_rows, d_skip):
    rows = u.shape[0]
    nl2 = 2 * SSM_LANES
    nblk = rows // SCAN_BLOCK

    def body(u_ref, bb_ref, c_ref, pw_ref, d_ref, y_ref, yg_ref, s_ref, carry_ref, tmp_ref):
        @pl.when(pl.program_id(0) == 0)
        def _():
            carry_ref[...] = jnp.zeros_like(carry_ref)

        uv = u_ref[...].reshape(SCAN_BLOCK, SSM_WIDTH)
        ub = uv.astype(BF16)
        for p in range(SSM_PAIRS):
            _pair_store(s_ref, p, _dot_nn(ub[:, _pair_tile(p)], bb_ref[p]))
        _scan_block(s_ref, carry_ref, tmp_ref, pw_ref, reverse=False)
        ys = _pair_sum(lambda p: _dot_nt(_pair_load(s_ref, p), c_ref[p]))
        yv = ys + d_ref[...] * uv
        y_ref[...] = yv.reshape(16, SCAN_COLS, SSM_WIDTH)
        yg_ref[...] = jax.nn.gelu(yv).astype(BF16).reshape(16, SCAN_COLS, SSM_WIDTH)

    const = lambda shape: pl.BlockSpec(shape, lambda i: (0,) * len(shape))
    blk = lambda cols: pl.BlockSpec((16, SCAN_COLS, cols), lambda i: (0, i, 0))
    pair_mats = const((SSM_PAIRS, PAIR_TILE, PAIR_TILE))
    y, yg, s = pl.pallas_call(
        body, name=name, grid=(nblk,),
        in_specs=[blk(SSM_WIDTH), pair_mats, pair_mats, const((SCAN_STEPS, SCAN_SUB, nl2)), const((1, SSM_WIDTH))],
        out_specs=[blk(SSM_WIDTH), blk(SSM_WIDTH), blk(nl2)],
        out_shape=[jax.ShapeDtypeStruct((16, rows // 16, SSM_WIDTH), F32),
                   jax.ShapeDtypeStruct((16, rows // 16, SSM_WIDTH), BF16),
                   jax.ShapeDtypeStruct((16, rows // 16, nl2), F32)],
        scratch_shapes=[pltpu.VMEM((SCAN_SUB, nl2), F32), pltpu.VMEM((2 * SCAN_SUB, nl2), F32)],
        compiler_params=_cparams("arbitrary"),
    )(_scan_view(u), bb_mats, c_mats, pw_rows, d_skip)
    return y.reshape(rows, SSM_WIDTH), yg.reshape(rows, SSM_WIDTH), s.reshape(rows, nl2)


def _ssm_bwd(name, dy, u, states, bb_mats, c_mats, pwc_rows, d_skip):
    rows = u.shape[0]
    nl2 = 2 * SSM_LANES
    nblk = rows // SCAN_BLOCK

    def body(dy_ref, u_ref, st_ref, prev_ref, bb_ref, c_ref, pw_ref, d_ref,
             du_ref, dbb_ref, dc_ref, dab_ref, dd_ref, g_ref, carry_ref, tmp_ref):
        i = pl.program_id(0)

        @pl.when(i == 0)
        def _():
            carry_ref[...] = jnp.zeros_like(carry_ref)
            for ref in (dbb_ref, dc_ref, dab_ref, dd_ref):
                ref[...] = jnp.zeros_like(ref)

        dyv = dy_ref[...].reshape(SCAN_BLOCK, SSM_WIDTH)
        uv = u_ref[...].reshape(SCAN_BLOCK, SSM_WIDTH)
        dyb, ub = dyv.astype(BF16), uv.astype(BF16)
        for p in range(SSM_PAIRS):
            _pair_store(g_ref, p, _dot_nn(dyb[:, _pair_tile(p)], c_ref[p]))
        have_prev = (i < nblk - 1).astype(F32)
        _scan_block(g_ref, carry_ref, tmp_ref, pw_ref, reverse=True,
                    sprev=(st_ref, prev_ref, have_prev, dab_ref))

        def pair_work(p):
            gp = _pair_load(g_ref, p)
            dbb_ref[p] += _dot_tn(ub[:, _pair_tile(p)], gp)
            dc_ref[p] += _dot_tn(dyb[:, _pair_tile(p)], _pair_load(st_ref, p))
            return _dot_nt(gp, bb_ref[p])

        du_ref[...] = (_pair_sum(pair_work) + d_ref[...] * dyv).reshape(16, SCAN_COLS, SSM_WIDTH)
        dd_ref[...] += jnp.sum(dyv * uv, axis=0, keepdims=True)

    const = lambda shape: pl.BlockSpec(shape, lambda i: (0,) * len(shape))
    blk = lambda cols: pl.BlockSpec((16, SCAN_COLS, cols), lambda i: (0, nblk - 1 - i, 0))
    per8 = SCAN_COLS // SCAN_SUB
    prev_spec = pl.BlockSpec((None, SCAN_SUB, nl2), lambda i: (15, jnp.maximum((nblk - 1 - i) * per8 - 1, 0), 0))
    pair_mats = const((SSM_PAIRS, PAIR_TILE, PAIR_TILE))
    pair_shape = jax.ShapeDtypeStruct((SSM_PAIRS, PAIR_TILE, PAIR_TILE), F32)
    sv = _scan_view(states)
    du, dbb, dc, dab, dd = pl.pallas_call(
        body, name=name, grid=(nblk,),
        in_specs=[blk(SSM_WIDTH), blk(SSM_WIDTH), blk(nl2), prev_spec, pair_mats, pair_mats,
                  const((SCAN_STEPS, SCAN_SUB, nl2)), const((1, SSM_WIDTH))],
        out_specs=[blk(SSM_WIDTH), pair_mats, pair_mats, const((SCAN_SUB, nl2)), const((1, SSM_WIDTH))],
        out_shape=[jax.ShapeDtypeStruct((16, rows // 16, SSM_WIDTH), F32), pair_shape, pair_shape,
                   jax.ShapeDtypeStruct((SCAN_SUB, nl2), F32), jax.ShapeDtypeStruct((1, SSM_WIDTH), F32)],
        scratch_shapes=[pltpu.VMEM((16, SCAN_COLS, nl2), F32), pltpu.VMEM((SCAN_SUB, nl2), F32),
                        pltpu.VMEM((2 * SCAN_SUB, nl2), F32)],
        compiler_params=_cparams("arbitrary"),
    )(_scan_view(dy), _scan_view(u), sv, sv, bb_mats, c_mats, pwc_rows, d_skip)
    return du.reshape(rows, SSM_WIDTH), dbb, dc, dab, dd


def _adamw(name, w, m, v, gparts, tr):
    rows, cols = w.shape

    def body(w_ref, m_ref, v_ref, g_ref, og_ref, od_ref, om_ref, ov_ref):
        g = g_ref[0].astype(F32)
        for i in range(1, N_DEV):
            g = g + g_ref[i].astype(F32)
        m_new = B1 * m_ref[...] + (1.0 - B1) * g
        v_new = B2 * v_ref[...] + (1.0 - B2) * (g * g)
        m_hat = m_new / (1.0 - B1 ** STEP)
        v_hat = v_new / (1.0 - B2 ** STEP)
        og_ref[...] = g
        od_ref[...] = -LR * (m_hat / (jnp.sqrt(v_hat) + ADAM_EPS) + WD * w_ref[...])
        om_ref[...] = m_new
        ov_ref[...] = v_new

    spec = pl.BlockSpec((tr, cols), lambda i: (i, 0))
    shape = jax.ShapeDtypeStruct((rows, cols), F32)
    return pl.pallas_call(
        body, name=name, grid=(rows // tr,),
        in_specs=[spec, spec, spec, pl.BlockSpec((N_DEV, tr, cols), lambda i: (0, i, 0))],
        out_specs=[spec] * 4, out_shape=[shape] * 4,
        compiler_params=_cparams("parallel"),
    )(w, m, v, gparts)


_SHARDED = (
    ("ffn1_w_gate", True, (352, 1024)), ("ffn1_w_up", True, (352, 1024)), ("ffn1_w_down", False, (352, 1024)),
    ("w_in", True, (608, 1024)), ("ssm_w_glu", True, (128, 512)), ("w_attn_branch", True, (128, 256)),
    ("w_ssm_branch", True, (128, 512)), ("w_out", False, (128, 1024)),
    ("ffn2_w_gate", True, (352, 1024)), ("ffn2_w_up", True, (352, 1024)), ("ffn2_w_down", False, (352, 1024)),
)
_SMALL = ("ffn1_norm", "mix_norm", "gate_bias", "rel_bias_table", "ssm_a_re", "ssm_a_im", "ssm_log_dt",
          "ssm_b_re", "ssm_b_im", "ssm_c_re", "ssm_c_im", "ssm_d", "ffn2_norm", "final_norm")
_ORDER = ("ffn1_norm", "ffn1_w_gate", "ffn1_w_up", "ffn1_w_down", "mix_norm", "w_in", "gate_bias",
          "rel_bias_table", "ssm_a_re", "ssm_a_im", "ssm_log_dt", "ssm_b_re", "ssm_b_im", "ssm_c_re",
          "ssm_c_im", "ssm_d", "ssm_w_glu", "w_attn_branch", "w_ssm_branch", "w_out", "ffn2_norm",
          "ffn2_w_gate", "ffn2_w_up", "ffn2_w_down", "final_norm")


def _pack_rows(shape):
    return shape[0] * shape[1] // D_MODEL


_SHARD_INFO = {nm: (tr, shape) for nm, tr, shape in _SHARDED}
_PHASES = {
    "f1gu": ("ffn1_w_gate", "ffn1_w_up"), "f1d": ("ffn1_w_down",),
    "mix": ("w_in", "ssm_w_glu", "w_attn_branch", "w_ssm_branch", "w_out"),
    "f2": ("ffn2_w_gate", "ffn2_w_up", "ffn2_w_down"),
}


def _to_rows(a, nm):
    tr, shape = _SHARD_INFO[nm]
    return (a.T if tr else a).reshape(_pack_rows(shape), D_MODEL)


def _from_rows(p, nm):
    tr, shape = _SHARD_INFO[nm]
    a = p.reshape(shape)
    return a.T if tr else a


def _full_weight(gathered, nm):
    _, shape = _SHARD_INFO[nm]
    return gathered.reshape(N_DEV * shape[0], shape[1])


def _grad_blocks(g, nm):
    _, shape = _SHARD_INFO[nm]
    return g.astype(BF16).reshape(N_DEV, _pack_rows(shape), D_MODEL)


_SMALL_TILE = 8 * 128


def _small_rows(a):
    flat = a.reshape(-1)
    return jnp.pad(flat, (0, (-flat.shape[0]) % _SMALL_TILE)).reshape(-1, 128)


def _pack_small(ws, last=None):
    tail = jnp.zeros((), F32) if last is None else last
    return jnp.concatenate([_small_rows(ws[nm]) for nm in _SMALL] + [_small_rows(tail)], axis=0)


def _unpack_small(pack, like):
    out, r0 = {}, 0
    for nm in _SMALL:
        n = like[nm].size
        nr = 8 * -(-n // _SMALL_TILE)
        out[nm] = pack[r0:r0 + nr].reshape(-1)[:n].reshape(like[nm].shape)
        r0 += nr
    return out


def _residue_order(a):
    rows, cols = a.shape
    return a.reshape(rows // 16, 16, cols).transpose(1, 0, 2).reshape(rows, cols)


def _token_order(a):
    rows, cols = a.shape
    return a.reshape(16, rows // 16, cols).transpose(1, 0, 2).reshape(rows, cols)


_PAIRS_PER_TILE = PAIR_TILE // (2 * SSM_GROUP)
_PAIR_AXES = (SSM_PAIRS // _PAIRS_PER_TILE, _PAIRS_PER_TILE, 2)


def _pair_diagonals(acc):
    k, j, l = _PAIR_AXES
    eight = acc.reshape(k, j, j, l, SSM_GROUP, 2, l, SSM_STATE)
    eye_j, eye_l = jnp.eye(j, dtype=acc.dtype), jnp.eye(l, dtype=acc.dtype)
    own = jnp.einsum("kjJLcxln,jJ,lL->xkjlcn", eight, eye_j, eye_l).reshape(2, SSM_GROUPS, SSM_GROUP, SSM_STATE)
    return own[0], own[1]


def _local_step(xs, target, small, weights_of, send_grads, first_deps=()):
    rows = xs.shape[0]
    gfull, gsmall = {}, {}

    table_t = small["rel_bias_table"].T
    tables, bias4 = [], []
    for g in range(N_GROUPS):
        bucket, valid = [jnp.asarray(t) for t in _attn_tables(g, rows)]
        bias_g = _bias_fwd(f"rel_bias_fwd_{g}", bucket, valid, table_t[g * HEADS_PER_GROUP:(g + 1) * HEADS_PER_GROUP])
        tables.append(bucket)
        bias4.append(bias_g.reshape(-1, bias_g.shape[-1]))
    pw_re, pw_im, bb_mats, c_mats = _ssm_params_fwd(
        "ssm_params_fwd", small["ssm_a_re"], small["ssm_a_im"], small["ssm_log_dt"].reshape(SSM_GROUPS, 1),
        small["ssm_b_re"].transpose(2, 0, 1), small["ssm_b_im"].transpose(2, 0, 1), small["ssm_c_re"], small["ssm_c_im"])

    def power_rows(sign):
        row = jnp.concatenate([pw_re.reshape(SCAN_STEPS, 1, SSM_LANES), sign * pw_im.reshape(SCAN_STEPS, 1, SSM_LANES)],
                              axis=2)
        return jnp.broadcast_to(row, (SCAN_STEPS, SCAN_SUB, 2 * SSM_LANES))

    pw_fwd, pw_bwd = power_rows(1.0), power_rows(-1.0)
    d_skip = small["ssm_d"].reshape(1, SSM_WIDTH)
    wf = dict(weights_of("f1", [xs, target, bb_mats, c_mats, pw_fwd, pw_bwd] + bias4))

    x1, h1, gg1, uu1, hmix = _ffn_fwd("ffn1_fwd", xs, small["ffn1_norm"], wf["ffn1_w_gate"], wf["ffn1_w_up"],
                                      wf["ffn1_w_down"], small["mix_norm"], deps=first_deps)
    wf.update(weights_of("mix", x1))
    w_in = wf["w_in"]
    w_qkv, w_u, w_g = w_in[:3 * ATTN_WIDTH], w_in[3 * ATTN_WIDTH:3 * ATTN_WIDTH + SSM_WIDTH], w_in[3 * ATTN_WIDTH + SSM_WIDTH:]
    qscale = jnp.concatenate([jnp.full((1, ATTN_WIDTH), HEAD_DIM ** -0.5, F32), jnp.ones((1, 2 * ATTN_WIDTH), F32)], axis=1)
    qkv, = _mm("in_qkv", [(hmix, w_qkv)], True, 3 * ATTN_WIDTH, [BF16],
               epilogue=lambda acc, sc: (acc * sc,), extras=[(qscale, 0)], tn=ATTN_WIDTH)
    u, = _mm("in_u", [(hmix, w_u)], True, SSM_WIDTH, [F32])
    gates, = _mm("in_gates", [(hmix, w_g)], True, 2 * D_MODEL, [BF16],
                 epilogue=lambda acc, b: (_sigmoid(acc + b),), extras=[(small["gate_bias"], 0)])

    o_g, lse_g = [], []
    for g in range(N_GROUPS):
        o, lse = _attn_fwd(f"attn_fwd_{g}", qkv, g, bias4[g])
        o_g.append(o)
        lse_g.append(lse)
    oa_f32, oa = _combine_fwd("attn_combine_fwd", o_g, lse_g)
    y_attn, = _mm("attn_branch", [(oa, wf["w_attn_branch"])], True, D_MODEL, [BF16])
    y_raw, ygelu, states = _ssm_fwd("ssm_fwd", u, bb_mats, c_mats, pw_fwd, d_skip)
    glu, ysg = _mm("ssm_glu", [(ygelu, wf["ssm_w_glu"])], True, 2 * SSM_WIDTH, [BF16, BF16],
                   epilogue=lambda gv: (gv, gv[:, :SSM_WIDTH] * _sigmoid(gv[:, SSM_WIDTH:])),
                   tn=2 * SSM_WIDTH, out_cols=[2 * SSM_WIDTH, SSM_WIDTH])
    y_ssm, merged = _mm("ssm_branch_merge", [(ysg, wf["w_ssm_branch"])], True, D_MODEL, [BF16, BF16],
                        epilogue=lambda acc, ga, gs, ya: (acc, ga * ya + gs * acc),
                        extras=[(gates, 0), (gates, D_MODEL), (y_attn, 0)])
    x2, = _mm("mix_out", [(merged, wf["w_out"])], False, D_MODEL, [F32],
              epilogue=lambda acc, res: (res + acc,), extras=[(x1, 0)])
    wf.update(weights_of("f2", x2))
    dx3, h2, gg2, uu2, gsmall["final_norm"], gsmall["loss"] = _ffn_fwd_head(
        "ffn2_fwd", x2, small["ffn2_norm"], wf["ffn2_w_gate"], wf["ffn2_w_up"], wf["ffn2_w_down"],
        small["final_norm"].reshape(1, D_MODEL), target)

    dx2, dgg2, duu2, act2, gsmall["ffn2_norm"] = _ffn_bwd(
        "ffn2_bwd", dx3, x2, small["ffn2_norm"], gg2, uu2, wf["ffn2_w_gate"], wf["ffn2_w_up"], wf["ffn2_w_down"])
    gfull["ffn2_w_gate"] = _mm_tn("ffn2_dwg", dgg2, h2, out_dtype=BF16, tk=FFN_DW_ROWS)
    gfull["ffn2_w_up"] = _mm_tn("ffn2_dwu", duu2, h2, out_dtype=BF16, tk=FFN_DW_ROWS)
    gfull["ffn2_w_down"] = _mm_tn("ffn2_dwd", act2, dx3, scale=0.5, out_dtype=BF16, tk=FFN_DW_ROWS)
    sent = send_grads("f2", gfull)

    def merge_bwd(dm, ga, gs, ya, ys):
        dza, dzs = dm * ya * ga * (1.0 - ga), dm * ys * gs * (1.0 - gs)
        return (dm * ga, dm * gs, dza, dzs, jnp.sum(dza, axis=0, keepdims=True), jnp.sum(dzs, axis=0, keepdims=True))

    dya, dys, dzga, dzgs, dba, dbs = _mm(
        "mix_out_bwd", [(dx2, wf["w_out"])], True, D_MODEL, [BF16] * 4, epilogue=merge_bwd, row_sums=2,
        extras=[(gates, 0), (gates, D_MODEL), (y_attn, 0), (y_ssm, 0)], deps=sent, tm=512, tn=D_MODEL)
    gfull["w_out"] = _mm_tn("dw_out", merged, dx2, out_dtype=BF16)
    gsmall["gate_bias"] = jnp.concatenate([dba, dbs], axis=1)

    gfull["w_ssm_branch"] = _mm_tn("dw_ssm_branch", dys, ysg, out_dtype=BF16)

    def glu_bwd(dysg, av, bv):
        sb = _sigmoid(bv)
        return (dysg * sb, dysg * av * sb * (1.0 - sb))

    dglu_a, dglu_b = _mm("ssm_branch_bwd", [(dys, wf["w_ssm_branch"])], False, SSM_WIDTH, [BF16, BF16],
                         epilogue=glu_bwd, extras=[(glu, 0), (glu, SSM_WIDTH)])
    w_glu = wf["ssm_w_glu"]
    gfull["ssm_w_glu"] = _mm_tn_stack("dw_glu", [dglu_a, dglu_b], ygelu, out_dtype=BF16)

    def gelu_bwd(acc, yv):
        _, vjp = jax.vjp(jax.nn.gelu, yv)
        return (vjp(acc)[0],)

    dy_raw, = _mm("ssm_glu_bwd", [(dglu_a, w_glu[:SSM_WIDTH]), (dglu_b, w_glu[SSM_WIDTH:])], False, SSM_WIDTH, [F32],
                  epilogue=gelu_bwd, extras=[(y_raw, 0)])
    du, dbb_acc, dc_acc, dab_rows, gsmall_d = _ssm_bwd(
        "ssm_bwd", dy_raw, u, states, bb_mats, c_mats, pw_bwd, d_skip)
    gsmall["ssm_d"] = gsmall_d
    dbb_re, dbb_im = [a.transpose(1, 0, 2) for a in _pair_diagonals(dbb_acc)]
    dc_re, dc_im = _pair_diagonals(dc_acc)
    gsmall["ssm_c_re"], gsmall["ssm_c_im"] = dc_re, -dc_im
    dab = _colsum("ssm_dab", dab_rows)
    d_ar, d_ai, d_ld, d_br, d_bi = _ssm_params_bwd(
        "ssm_params_bwd", small["ssm_a_re"], small["ssm_a_im"], small["ssm_log_dt"].reshape(SSM_GROUPS, 1),
        small["ssm_b_re"].transpose(2, 0, 1), small["ssm_b_im"].transpose(2, 0, 1),
        dab[:, :SSM_LANES].reshape(SSM_GROUPS, SSM_STATE), dab[:, SSM_LANES:].reshape(SSM_GROUPS, SSM_STATE),
        dbb_re, dbb_im)
    gsmall["ssm_a_re"], gsmall["ssm_a_im"], gsmall["ssm_log_dt"] = d_ar, d_ai, d_ld.reshape(SSM_GROUPS)
    gsmall["ssm_b_re"], gsmall["ssm_b_im"] = d_br.transpose(1, 2, 0), d_bi.transpose(1, 2, 0)

    gfull["w_attn_branch"] = _mm_tn("dw_attn_branch", dya, oa, out_dtype=BF16)
    doa, = _mm("attn_branch_bwd", [(dya, wf["w_attn_branch"])], False, ATTN_OUT, [BF16])
    dc = _combine_bwd("attn_combine_bwd", doa, oa_f32, lse_g)
    dqkv_cols = [None] * 9
    dtable = []
    for g in range(N_GROUPS):
        dq, dk, dv, db = _attn_bwd(f"attn_bwd_{g}", qkv, dc[g], lse_g[g], dc[3 + g], g, bias4[g])
        dqkv_cols[g], dqkv_cols[3 + g], dqkv_cols[6 + g] = dq, dk, dv
        dt = _bias_bwd(f"rel_bias_bwd_{g}", tables[g], db.reshape(HEADS_PER_GROUP, -1, db.shape[-1]))
        dtable.append(dt[:, :HEADS_PER_GROUP])
    gsmall["rel_bias_table"] = jnp.concatenate(dtable, axis=1)

    gfull["w_in"] = jnp.concatenate([_mm_tn_stack("dw_in_qkv", dqkv_cols, hmix, out_dtype=BF16),
                                     _mm_tn_stack("dw_in_rest", [du, dzga, dzgs], hmix, out_dtype=BF16)], axis=0)
    sent = send_grads("mix", gfull)
    qkv_pairs = [(c, w_qkv[i * ATTN_OUT:(i + 1) * ATTN_OUT]) for i, c in enumerate(dqkv_cols)]

    def mix_norm_bwd(dh, xv, gain, dres):
        r, xh = _rms_parts(xv)
        return dres + _rms_bwd_dx(dh, gain, r, xh), jnp.sum(dh * xh, axis=0, keepdims=True)

    dx1, gsmall["mix_norm"] = _mm(
        "in_bwd", qkv_pairs + [(du, w_u), (dzga, w_g[:D_MODEL]), (dzgs, w_g[D_MODEL:])], False, D_MODEL, [F32],
        epilogue=mix_norm_bwd, row_sums=1, extras=[(x1, 0), (small["mix_norm"], 0), (dx2, 0)], tm=512, tn=D_MODEL,
        deps=sent)

    dx, dgg1, duu1, act1, gsmall["ffn1_norm"] = _ffn_bwd(
        "ffn1_bwd", dx1, xs, small["ffn1_norm"], gg1, uu1, wf["ffn1_w_gate"], wf["ffn1_w_up"], wf["ffn1_w_down"])
    sent = send_grads("small", gsmall)
    gfull["ffn1_w_gate"] = _mm_tn("ffn1_dwg", dgg1, h1, deps=sent, out_dtype=BF16, tk=FFN_DW_ROWS)
    gfull["ffn1_w_up"] = _mm_tn("ffn1_dwu", duu1, h1, out_dtype=BF16, tk=FFN_DW_ROWS)
    sent = send_grads("f1gu", gfull)
    gfull["ffn1_w_down"] = _mm_tn("ffn1_dwd", act1, dx1, scale=0.5, deps=sent, out_dtype=BF16, tk=FFN_DW_ROWS)
    send_grads("f1d", gfull)
    return dx, gsmall


def kernel(x, ffn1_norm, ffn1_w_gate, ffn1_w_up, ffn1_w_down, mix_norm, w_in, gate_bias, rel_bias_table, ssm_a_re, ssm_a_im, ssm_log_dt, ssm_b_re, ssm_b_im, ssm_c_re, ssm_c_im, ssm_d, ssm_w_glu, w_attn_branch, w_ssm_branch, w_out, ffn2_norm, ffn2_w_gate, ffn2_w_up, ffn2_w_down, final_norm, loss_target, m_ffn1_norm, m_ffn1_w_gate, m_ffn1_w_up, m_ffn1_w_down, m_mix_norm, m_w_in, m_gate_bias, m_rel_bias_table, m_ssm_a_re, m_ssm_a_im, m_ssm_log_dt, m_ssm_b_re, m_ssm_b_im, m_ssm_c_re, m_ssm_c_im, m_ssm_d, m_ssm_w_glu, m_w_attn_branch, m_w_ssm_branch, m_w_out, m_ffn2_norm, m_ffn2_w_gate, m_ffn2_w_up, m_ffn2_w_down, m_final_norm, v_ffn1_norm, v_ffn1_w_gate, v_ffn1_w_up, v_ffn1_w_down, v_mix_norm, v_w_in, v_gate_bias, v_rel_bias_table, v_ssm_a_re, v_ssm_a_im, v_ssm_log_dt, v_ssm_b_re, v_ssm_b_im, v_ssm_c_re, v_ssm_c_im, v_ssm_d, v_ssm_w_glu, v_w_attn_branch, v_w_ssm_branch, v_w_out, v_ffn2_norm, v_ffn2_w_gate, v_ffn2_w_up, v_ffn2_w_down, v_final_norm):
    given = dict(locals())
    shapes = {nm: given[nm].shape for nm in _ORDER}

    def strip(a):
        return a[0] if a.ndim >= 2 and a.shape[0] == 1 else a

    w = {nm: strip(given[nm]) for nm in _ORDER}
    m = {nm: strip(given["m_" + nm]) for nm in _ORDER}
    v = {nm: strip(given["v_" + nm]) for nm in _ORDER}
    for d in (w, m, v):
        d["rel_bias_table"] = d["rel_bias_table"].reshape(N_BUCKETS, N_GROUPS * HEADS_PER_GROUP)

    weight_phases = {"f1": _PHASES["f1gu"] + _PHASES["f1d"], "mix": _PHASES["mix"], "f2": _PHASES["f2"]}
    pending_w, w_rows, deps, zero = {}, {}, [], 0.0
    for phase, names in weight_phases.items():
        w_rows.update({nm: _to_rows(w[nm] + zero, nm) for nm in names})
        pending_w[phase] = _exchange_start(f"gather_{phase}_start", [w_rows[nm].astype(BF16) for nm in names],
                                           gather=True, deps=deps)
        deps = [pending_w[phase][4]]
        zero = pending_w["f1"][4][0, 0]
    m_rows = {nm: _to_rows(m[nm] + zero, nm) for nm in _SHARD_INFO}
    v_rows = {nm: _to_rows(v[nm] + zero, nm) for nm in _SHARD_INFO}
    small = {nm: w[nm] for nm in _SMALL}
    small_in = {nm: small[nm] + zero for nm in _SMALL}
    for nm in ("ffn1_norm", "mix_norm", "ffn2_norm", "gate_bias"):
        small_in[nm] = small_in[nm].reshape(1, -1)

    def weights_of(phase, after):
        if phase == "f1":
            after = list(after) + list(m_rows.values()) + list(v_rows.values())
        landed = _exchange_wait(f"gather_{phase}_wait", pending_w[phase], after, gather=True)
        return {nm: _full_weight(got, nm) for nm, got in zip(weight_phases[phase], landed)}

    pending_g = {}

    def send_grads(phase, grads):
        if phase == "small":
            gs_pack = _pack_small({nm: grads[nm].reshape(small[nm].shape) for nm in _SMALL}, last=grads["loss"])
            pending_g[phase] = _exchange_start("gather_small_start", [gs_pack], gather=True)
        else:
            pending_g[phase] = _exchange_start(f"scatter_{phase}_start",
                                               [_grad_blocks(grads[nm], nm) for nm in _PHASES[phase]], gather=False)
        return [pending_g[phase][4]]

    dx, gsmall = _local_step(_residue_order(x[0]), _residue_order(loss_target[0]), small_in, weights_of, send_grads,
                             first_deps=[pending_w["f2"][4]])
    dx = _token_order(dx)

    updated = {}
    after = pending_g["f1d"][4]
    for phase in ("f2", "mix", "small", "f1gu", "f1d"):
        landed = _exchange_wait(f"exchange_{phase}_wait", pending_g[phase], after, gather=phase == "small")
        if phase == "small":
            sm = _adamw("adamw_small", _pack_small(small), _pack_small({nm: m[nm] for nm in _SMALL}),
                        _pack_small({nm: v[nm] for nm in _SMALL}), landed[0], landed[0].shape[1])
            after = sm[0]
            continue
        for nm, recv in zip(_PHASES[phase], landed):
            tr = max(t for t in range(16, 353, 16) if w_rows[nm].shape[0] % t == 0)
            updated[nm] = _adamw(f"adamw_{nm}", w_rows[nm], m_rows[nm], v_rows[nm], recv, tr)
            after = updated[nm][0]

    loss = sm[0][-8, 0]
    outs = []
    for i in range(4):
        sml = _unpack_small(sm[i], small)
        outs.append([(_from_rows(updated[nm][i], nm) if nm in updated else sml[nm]).reshape(shapes[nm])
                     for nm in _ORDER])
    return (loss, dx[None], *outs[0], *outs[1], *outs[2], *outs[3])
```

```python
import math

import numpy as np
import jax
import jax.numpy as jnp
from jax import lax
from jax.experimental import pallas as pl
from jax.experimental.pallas import tpu as pltpu

F32 = jnp.float32
BF16 = jnp.bfloat16

N_DEV = 8
D_MODEL = 1024
HEAD_DIM = 64
HEADS_PER_GROUP = 4
DILATIONS = (1, 4, 16)
N_GROUPS = 3
ATTN_WIDTH = 768
ATTN_OUT = 256
BLOCK = 128
N_BUCKETS = 32
MAX_DISTANCE = 2048
NEG_INF = -1e30
SSM_WIDTH = 512
SSM_GROUPS = 32
SSM_GROUP = 16
SSM_STATE = 64
SSM_LANES = SSM_GROUPS * SSM_STATE
SSM_PAIRS = SSM_GROUPS // 2
PAIR_LANES = 2 * SSM_STATE
PAIR_TILE = 256
EPS = 1e-6
LR, B1, B2, ADAM_EPS, WD, STEP = 0.001, 0.9, 0.999, 1e-08, 0.01, 10

VMEM_LIMIT_BYTES = 56 * 1024 * 1024
FFN_CHUNK = 768
FFN_DW_ROWS = 2048
SCAN_BLOCK = 256
SCAN_STEPS = 16
SCAN_COLS = SCAN_BLOCK // SCAN_STEPS
SCAN_SUB = 8
SCAN_LANES = 512

MESH = pl.DeviceIdType.MESH


def _cparams(*sem):
    return pltpu.CompilerParams(dimension_semantics=sem, vmem_limit_bytes=VMEM_LIMIT_BYTES)


def _dot(a, b, dims):
    return lax.dot_general(a, b, (dims, ((), ())), preferred_element_type=F32)


def _dot_nn(a, b):
    return _dot(a, b, ((1,), (0,)))


def _dot_nt(a, b):
    return _dot(a, b, ((1,), (1,)))


def _dot_tn(a, b):
    return _dot(a, b, ((0,), (0,)))


def _sigmoid(x):
    return 1.0 / (1.0 + jnp.exp(-x))


_HBM_SPEC = pl.BlockSpec(memory_space=pltpu.HBM)
_SEM_SPEC = pl.BlockSpec(memory_space=pltpu.SEMAPHORE)
_ANY_SPEC = pl.BlockSpec(memory_space=pl.ANY)
_EFFECT = pltpu.SideEffectType.DATAFLOW_SIDE_EFFECTING


def _peers(x, y, c):
    return [(1 - x if k & 4 else x, 1 - y if k & 2 else y, 1 - c if k & 1 else c) for k in range(1, N_DEV)]


def _exchange_copies(x_refs, land_refs, send_sems, recv_sems, gather):
    x, y, c = lax.axis_index("x"), lax.axis_index("y"), lax.axis_index("c")
    me = 4 * x + 2 * y + c
    copies = []
    for a, (x_ref, land_ref) in enumerate(zip(x_refs, land_refs)):
        for k, (px, py, pc) in enumerate(_peers(x, y, c)):
            src = x_ref if gather else x_ref.at[4 * px + 2 * py + pc]
            copies.append(pltpu.make_async_remote_copy(
                src_ref=src, dst_ref=land_ref.at[me], send_sem=send_sems.at[N_DEV * a + k],
                recv_sem=recv_sems.at[(N_DEV - 1) * a + k], device_id=(px, py, pc), device_id_type=MESH))
    owns = [pltpu.make_async_copy(x_ref if gather else x_ref.at[me], land_ref.at[me],
                                  send_sems.at[N_DEV * a + N_DEV - 1])
            for a, (x_ref, land_ref) in enumerate(zip(x_refs, land_refs))]
    return owns, copies


def _exchange_start(name, xs_list, gather, deps=()):
    n, nd = len(xs_list), len(deps)
    land_shapes = [(N_DEV, *xs.shape) if gather else xs.shape for xs in xs_list]

    def body(*refs):
        x_refs, land_refs = refs[:n], refs[n:2 * n]
        send_sems, recv_sems = refs[2 * n + nd:2 * n + nd + 2]
        token = refs[-1]
        owns, copies = _exchange_copies(x_refs, land_refs, send_sems, recv_sems, gather)
        for cp in copies + owns:
            cp.start()
        token[...] = jnp.zeros_like(token)

    hbm = lambda a: pltpu.with_memory_space_constraint(a, pltpu.HBM)
    outs = pl.pallas_call(
        body, name=name,
        out_shape=(pltpu.SemaphoreType.DMA((n * N_DEV,)), pltpu.SemaphoreType.DMA((n * (N_DEV - 1),)),
                   *[pltpu.HBM(xs.shape, xs.dtype) for xs in xs_list],
                   *[pltpu.HBM(shape, xs.dtype) for shape, xs in zip(land_shapes, xs_list)],
                   jax.ShapeDtypeStruct((8, 128), F32)),
        in_specs=(_HBM_SPEC,) * (2 * n) + (_ANY_SPEC,) * nd,
        out_specs=(_SEM_SPEC, _SEM_SPEC) + (_HBM_SPEC,) * (2 * n) + (pl.BlockSpec(memory_space=pltpu.VMEM),),
        input_output_aliases={i: 2 + i for i in range(2 * n)},
        compiler_params=pltpu.CompilerParams(has_side_effects=_EFFECT),
    )(*[hbm(xs) for xs in xs_list], *[hbm(lax.empty(shape, xs.dtype)) for shape, xs in zip(land_shapes, xs_list)],
      *deps)
    return outs[0], outs[1], list(outs[2:2 + n]), list(outs[2 + n:2 + 2 * n]), outs[-1]


def _exchange_wait(name, handle, after, gather):
    send_sems, recv_sems, xs_thru, lands_thru, _ = handle
    n = len(xs_thru)
    after = list(after) if isinstance(after, (list, tuple)) else [after]

    def body(*refs):
        x_refs, land_refs = refs[:n], refs[n:2 * n]
        send_sems, recv_sems = refs[2 * n:2 * n + 2]
        owns, copies = _exchange_copies(x_refs, land_refs, send_sems, recv_sems, gather)
        for cp in copies:
            cp.wait_send()
            cp.wait_recv()
        for cp in owns:
            cp.wait()

    outs = pl.pallas_call(
        body, name=name,
        out_shape=tuple(pltpu.HBM(a.shape, a.dtype) for a in xs_thru + lands_thru),
        in_specs=(_HBM_SPEC,) * (2 * n) + (_SEM_SPEC, _SEM_SPEC) + (_ANY_SPEC,) * len(after),
        out_specs=(_HBM_SPEC,) * (2 * n), input_output_aliases={i: i for i in range(2 * n)},
        compiler_params=pltpu.CompilerParams(has_side_effects=_EFFECT),
    )(*xs_thru, *lands_thru, send_sems, recv_sems, *after)
    return list(outs[n:])


def _mm(name, pairs, nt, n_cols, out_dtypes, epilogue=None, extras=(), tm=1024, tn=512, deps=(), row_sums=0,
        out_cols=None):
    rows = pairs[0][0].shape[0]
    tm = min(tm, rows)
    tn = min(tn, n_cols)
    na, ne, nd, no = len(pairs), len(extras), len(deps), len(out_dtypes)

    def body(*refs):
        a_refs, w_refs = refs[:na], refs[na:2 * na]
        e_refs, o_refs = refs[2 * na:2 * na + ne], refs[2 * na + ne + nd:]
        acc = None
        for a_ref, w_ref in zip(a_refs, w_refs):
            a = a_ref[...].astype(BF16)
            w = w_ref[...].astype(BF16)
            p = _dot_nt(a, w) if nt else _dot_nn(a, w)
            acc = p if acc is None else acc + p
        outs = (acc,) if epilogue is None else epilogue(acc, *[e[...].astype(F32) for e in e_refs])
        for o_ref, o in zip(o_refs[:no], outs[:no]):
            o_ref[...] = o.astype(o_ref.dtype)
        for r_ref, o in zip(o_refs[no:], outs[no:]):
            @pl.when(pl.program_id(0) == 0)
            def _():
                r_ref[...] = jnp.zeros_like(r_ref)

            r_ref[...] += o

    in_specs = [pl.BlockSpec((tm, a.shape[1]), lambda i, j: (i, 0)) for a, _ in pairs]
    for _, w in pairs:
        if nt:
            in_specs.append(pl.BlockSpec((tn, w.shape[1]), lambda i, j: (j, 0)))
        else:
            in_specs.append(pl.BlockSpec((w.shape[0], tn), lambda i, j: (0, j)))
    for e, col_off in extras:
        off = col_off // tn
        if e.shape[0] == 1:
            width = tn if out_cols is None else e.shape[1]
            in_specs.append(pl.BlockSpec((1, width), lambda i, j, off=off: (0, j + off)))
        else:
            in_specs.append(pl.BlockSpec((tm, tn), lambda i, j, off=off: (i, j + off)))
    in_specs += [_ANY_SPEC] * nd
    if out_cols is None:
        out_cols = [n_cols] * no
    else:
        assert tn == n_cols, "outputs of other widths need the whole row in one block"
    assert not row_sums or tn == n_cols
    out_specs = [pl.BlockSpec((tm, tn * c // n_cols), lambda i, j: (i, j)) for c in out_cols]
    out_specs += [pl.BlockSpec((1, tn), lambda i, j: (0, j))] * row_sums
    out_shape = [jax.ShapeDtypeStruct((rows, c), dt) for c, dt in zip(out_cols, out_dtypes)]
    out_shape += [jax.ShapeDtypeStruct((1, n_cols), F32)] * row_sums
    outs = pl.pallas_call(
        body, name=name, grid=(rows // tm, n_cols // tn),
        in_specs=in_specs, out_specs=out_specs, out_shape=out_shape,
        compiler_params=_cparams("arbitrary" if row_sums else "parallel", "arbitrary"),
    )(*[a for a, _ in pairs], *[w for _, w in pairs], *[e for e, _ in extras], *deps)
    return outs


def _tn_rows(m):
    return max(b for b in range(128, min(m, 1408) + 1, 128) if m % b == 0)


def _mm_tn(name, a, b, scale=1.0, bm=None, tk=1024, deps=(), out_dtype=F32):
    rows, m = a.shape
    n = b.shape[1]
    bm = _tn_rows(m) if bm is None else bm
    tk = min(tk, rows)
    nk = rows // tk

    def body(a_ref, b_ref, *rest):
        o_ref, acc_ref = rest[-2:]
        k = pl.program_id(1)

        @pl.when(k == 0)
        def _():
            acc_ref[...] = jnp.zeros_like(acc_ref)

        acc_ref[...] += _dot_tn(a_ref[...].astype(BF16), b_ref[...].astype(BF16))

        @pl.when(k == nk - 1)
        def _():
            o_ref[...] = (acc_ref[...] * scale).astype(o_ref.dtype)

    return pl.pallas_call(
        body, name=name, grid=(m // bm, nk),
        in_specs=[pl.BlockSpec((tk, bm), lambda i, k: (k, i)), pl.BlockSpec((tk, n), lambda i, k: (k, 0))]
        + [_ANY_SPEC] * len(deps),
        out_specs=pl.BlockSpec((bm, n), lambda i, k: (i, 0)),
        out_shape=jax.ShapeDtypeStruct((m, n), out_dtype),
        scratch_shapes=[pltpu.VMEM((bm, n), F32)],
        compiler_params=_cparams("parallel", "arbitrary"),
    )(a, b, *deps)


def _mm_tn_stack(name, a_list, b, tk=1024, out_dtype=F32):
    rows, n = b.shape
    ms = [a.shape[1] for a in a_list]
    tk = min(tk, rows)
    nk = rows // tk
    na = len(a_list)

    def body(*refs):
        a_refs, b_ref, o_ref, acc_ref = refs[:na], refs[na], refs[na + 1], refs[na + 2]
        k = pl.program_id(0)

        @pl.when(k == 0)
        def _():
            acc_ref[...] = jnp.zeros_like(acc_ref)

        bv = b_ref[...].astype(BF16)
        r0 = 0
        for a_ref, m in zip(a_refs, ms):
            acc_ref[r0:r0 + m, :] += _dot_tn(a_ref[...].astype(BF16), bv)
            r0 += m

        @pl.when(k == nk - 1)
        def _():
            o_ref[...] = acc_ref[...].astype(o_ref.dtype)

    return pl.pallas_call(
        body, name=name, grid=(nk,),
        in_specs=[pl.BlockSpec((tk, m), lambda k: (k, 0)) for m in ms] + [pl.BlockSpec((tk, n), lambda k: (k, 0))],
        out_specs=pl.BlockSpec((sum(ms), n), lambda k: (0, 0)),
        out_shape=jax.ShapeDtypeStruct((sum(ms), n), out_dtype),
        scratch_shapes=[pltpu.VMEM((sum(ms), n), F32)],
        compiler_params=_cparams("arbitrary"),
    )(*a_list, b)


def _colsum(name, xs, tm=512):
    rows, cols = xs.shape
    tm = min(tm, rows)

    def body(x_ref, o_ref):
        @pl.when(pl.program_id(0) == 0)
        def _():
            o_ref[...] = jnp.zeros_like(o_ref)

        o_ref[...] += jnp.sum(x_ref[...].astype(F32), axis=0, keepdims=True)

    return pl.pallas_call(
        body, name=name, grid=(rows // tm,),
        in_specs=[pl.BlockSpec((tm, cols), lambda i: (i, 0))],
        out_specs=pl.BlockSpec((1, cols), lambda i: (0, 0)),
        out_shape=jax.ShapeDtypeStruct((1, cols), F32),
        compiler_params=_cparams("arbitrary"),
    )(xs)


def _ew(name, fn, ins, out_cols, out_dtypes, tm=512):
    rows = ins[0].shape[0]
    tm = min(tm, rows)
    ni = len(ins)

    def body(*refs):
        outs = fn(*[r[...] for r in refs[:ni]])
        for o_ref, o in zip(refs[ni:], outs):
            o_ref[...] = o.astype(o_ref.dtype)

    def spec(shape):
        if shape[0] == 1:
            return pl.BlockSpec((1, shape[1]), lambda i: (0, 0))
        return pl.BlockSpec((tm, shape[1]), lambda i: (i, 0))

    return pl.pallas_call(
        body, name=name, grid=(rows // tm,),
        in_specs=[spec(a.shape) for a in ins],
        out_specs=[pl.BlockSpec((tm, c), lambda i: (i, 0)) for c in out_cols],
        out_shape=[jax.ShapeDtypeStruct((rows, c), dt) for c, dt in zip(out_cols, out_dtypes)],
        compiler_params=_cparams("parallel"),
    )(*ins)


def _rms_parts(xv):
    r = lax.rsqrt(jnp.mean(xv * xv, axis=-1, keepdims=True) + EPS)
    return r, xv * r


def _rms_bwd_dx(dh, gain, r, xh):
    dxh = dh * gain
    return r * (dxh - xh * jnp.mean(dxh * xh, axis=-1, keepdims=True))


def _ffn_chunks(f_all):
    return [slice(c, min(c + FFN_CHUNK, f_all)) for c in range(0, f_all, FFN_CHUNK)]


def _loss_head(xo, gain_f, target, d):
    r, xh = _rms_parts(xo)
    err = xh * gain_f - target
    dy = err * (1.0 / d)
    per_tok = jnp.mean(err * err, axis=-1, keepdims=True)
    return (_rms_bwd_dx(dy, gain_f, r, xh), jnp.sum(dy * xh, axis=0, keepdims=True),
            0.5 * jnp.sum(per_tok, axis=0, keepdims=True))


def _ffn_tile(x_ref, g_ref, wg_ref, wu_ref, wd_ref, h_ref, gg_ref, uu_ref):
    xv = x_ref[...]
    _, xh = _rms_parts(xv)
    h = (xh * g_ref[...]).astype(BF16)
    h_ref[...] = h
    acc = None
    for cols in _ffn_chunks(wd_ref.shape[0]):
        gg = _dot_nt(h, wg_ref[cols, :])
        uu = _dot_nt(h, wu_ref[cols, :])
        act = gg * _sigmoid(gg) * uu
        part = _dot_nn(act.astype(BF16), wd_ref[cols, :])
        acc = part if acc is None else acc + part
        gg_ref[:, cols] = gg.astype(BF16)
        uu_ref[:, cols] = uu.astype(BF16)
    return xv + 0.5 * acc


def _ffn_fwd(name, xs, gain, wg_t, wu_t, wd, next_gain, tm=512, deps=()):
    rows, d = xs.shape
    f_all = wd.shape[0]
    tm = min(tm, rows)

    def body(x_ref, g_ref, wg_ref, wu_ref, wd_ref, ng_ref, *rest):
        xo_ref, h_ref, gg_ref, uu_ref, hn_ref = rest[-5:]
        xo = _ffn_tile(x_ref, g_ref, wg_ref, wu_ref, wd_ref, h_ref, gg_ref, uu_ref)
        xo_ref[...] = xo
        hn_ref[...] = (_rms_parts(xo)[1] * ng_ref[...]).astype(BF16)

    tile = pl.BlockSpec((tm, d), lambda i: (i, 0))
    row = pl.BlockSpec((1, d), lambda i: (0, 0))
    wspec = pl.BlockSpec((f_all, d), lambda i: (0, 0), pipeline_mode=pl.Buffered(1))
    hid = pl.BlockSpec((tm, f_all), lambda i: (i, 0))
    return pl.pallas_call(
        body, name=name, grid=(rows // tm,),
        in_specs=[tile, row, wspec, wspec, wspec, row] + [_ANY_SPEC] * len(deps),
        out_specs=[tile, tile, hid, hid, tile],
        out_shape=[jax.ShapeDtypeStruct((rows, d), F32), jax.ShapeDtypeStruct((rows, d), BF16),
                   jax.ShapeDtypeStruct((rows, f_all), BF16), jax.ShapeDtypeStruct((rows, f_all), BF16),
                   jax.ShapeDtypeStruct((rows, d), BF16)],
        compiler_params=_cparams("parallel"),
    )(xs, gain, wg_t, wu_t, wd, next_gain, *deps)


def _ffn_fwd_head(name, xs, gain, wg_t, wu_t, wd, gain_f, target, tm=512):
    rows, d = xs.shape
    f_all = wd.shape[0]
    tm = min(tm, rows)

    def body(x_ref, g_ref, wg_ref, wu_ref, wd_ref, gf_ref, t_ref, dxo_ref, h_ref, gg_ref, uu_ref, dgf_ref, loss_ref):
        xo = _ffn_tile(x_ref, g_ref, wg_ref, wu_ref, wd_ref, h_ref, gg_ref, uu_ref)
        dxo, dgf, loss = _loss_head(xo, gf_ref[...], t_ref[...], d)
        dxo_ref[...] = dxo

        @pl.when(pl.program_id(0) == 0)
        def _():
            dgf_ref[...] = jnp.zeros_like(dgf_ref)
            loss_ref[...] = jnp.zeros_like(loss_ref)

        dgf_ref[...] += dgf
        loss_ref[...] += loss

    tile = pl.BlockSpec((tm, d), lambda i: (i, 0))
    row = pl.BlockSpec((1, d), lambda i: (0, 0))
    wspec = pl.BlockSpec((f_all, d), lambda i: (0, 0), pipeline_mode=pl.Buffered(1))
    hid = pl.BlockSpec((tm, f_all), lambda i: (i, 0))
    return pl.pallas_call(
        body, name=name, grid=(rows // tm,),
        in_specs=[tile, row, wspec, wspec, wspec, row, tile],
        out_specs=[tile, tile, hid, hid, row, pl.BlockSpec((1, 1), lambda i: (0, 0))],
        out_shape=[jax.ShapeDtypeStruct((rows, d), F32), jax.ShapeDtypeStruct((rows, d), BF16),
                   jax.ShapeDtypeStruct((rows, f_all), BF16), jax.ShapeDtypeStruct((rows, f_all), BF16),
                   jax.ShapeDtypeStruct((1, d), F32), jax.ShapeDtypeStruct((1, 1), F32)],
        compiler_params=_cparams("arbitrary"),
    )(xs, gain, wg_t, wu_t, wd, gain_f, target)


def _ffn_bwd(name, dxo, xs, gain, gg_all, uu_all, wg_t, wu_t, wd, tm=256):
    rows, d = xs.shape
    f_all = wd.shape[0]
    tm = min(tm, rows)

    def body(dxo_ref, x_ref, g_ref, gg_ref, uu_ref, wg_ref, wu_ref, wd_ref,
             dx_ref, dgg_ref, duu_ref, act_ref, dgain_ref):
        dxo = dxo_ref[...]
        df = (0.5 * dxo).astype(BF16)
        dh = None
        for cols in _ffn_chunks(f_all):
            gg = gg_ref[:, cols].astype(F32)
            uu = uu_ref[:, cols].astype(F32)
            sg = _sigmoid(gg)
            silu = gg * sg
            dact = _dot_nt(df, wd_ref[cols, :])
            duu = (dact * silu).astype(BF16)
            dgg = (dact * uu * (sg * (1.0 + gg * (1.0 - sg)))).astype(BF16)
            act_ref[:, cols] = (silu * uu).astype(BF16)
            dgg_ref[:, cols] = dgg
            duu_ref[:, cols] = duu
            part = _dot_nn(dgg, wg_ref[cols, :]) + _dot_nn(duu, wu_ref[cols, :])
            dh = part if dh is None else dh + part
        r, xh = _rms_parts(x_ref[...])
        dx_ref[...] = dxo + _rms_bwd_dx(dh, g_ref[...], r, xh)

        @pl.when(pl.program_id(0) == 0)
        def _():
            dgain_ref[...] = jnp.zeros_like(dgain_ref)

        dgain_ref[...] += jnp.sum(dh * xh, axis=0, keepdims=True)

    tile = pl.BlockSpec((tm, d), lambda i: (i, 0))
    row = pl.BlockSpec((1, d), lambda i: (0, 0))
    wspec = pl.BlockSpec((f_all, d), lambda i: (0, 0), pipeline_mode=pl.Buffered(1))
    hid = pl.BlockSpec((tm, f_all), lambda i: (i, 0))
    hid_shape = jax.ShapeDtypeStruct((rows, f_all), BF16)
    return pl.pallas_call(
        body, name=name, grid=(rows // tm,),
        in_specs=[tile, tile, row, hid, hid, wspec, wspec, wspec],
        out_specs=[tile, hid, hid, hid, row],
        out_shape=[jax.ShapeDtypeStruct((rows, d), F32), hid_shape, hid_shape, hid_shape,
                   jax.ShapeDtypeStruct((1, d), F32)],
        compiler_params=_cparams("arbitrary"),
    )(dxo, xs, gain, gg_all, uu_all, wg_t, wu_t, wd)


def _t5_bucket_np(dist):
    max_exact = N_BUCKETS // 2
    dd = np.maximum(dist, 1).astype(np.float32)
    large = max_exact + (np.log(dd / np.float32(max_exact)) / np.float32(math.log(MAX_DISTANCE / max_exact))
                         * np.float32(N_BUCKETS - max_exact)).astype(np.int32)
    large = np.minimum(large, N_BUCKETS - 1)
    return np.where(dist < max_exact, dist, large).astype(np.int32)


def _attn_geometry(g, rows):
    run = rows // 16
    dil = DILATIONS[g]
    if dil == 16:
        bq = BLOCK
        return dict(view=(16, run), block=(None, bq), grid=(16, run // bq), index=lambda r, n: (r, n),
                    pos=np.arange(bq), bq=bq)
    if dil == 4:
        per = BLOCK // 4
        pos = (4 * np.arange(per)[None, :] + np.arange(4)[:, None]).reshape(-1)
        return dict(view=(4, 4, run), block=(4, None, per), grid=(4, run // per), index=lambda r, n: (0, r, n),
                    pos=pos, bq=BLOCK)
    per = 16
    pos = (16 * np.arange(per)[None, :] + np.arange(16)[:, None]).reshape(-1)
    return dict(view=(16, run), block=(16, per), grid=(1, run // per), index=lambda r, n: (0, n),
                pos=pos, bq=16 * per)


def _attn_tables(g, rows):
    geo = _attn_geometry(g, rows)
    pos, bq = geo["pos"], geo["bq"]
    steps = pos[:, None] - np.concatenate([pos - bq, pos])[None, :]
    valid = (steps >= 0) & (steps <= BLOCK)
    bucket = _t5_bucket_np((np.maximum(steps, 0) * DILATIONS[g]).astype(np.int32))
    return bucket, valid.astype(np.int32)


def _bias_fwd(name, bucket, valid, table_t):
    bq = bucket.shape[0]

    def body(bk_ref, ok_ref, tab_ref, o_ref):
        bk = bk_ref[...]
        ok = ok_ref[...] > 0
        accs = [jnp.zeros(bk.shape, F32)] * HEADS_PER_GROUP
        for b in range(N_BUCKETS):
            hit = bk == b
            accs = [jnp.where(hit, tab_ref[h, b], acc) for h, acc in enumerate(accs)]
        for h, acc in enumerate(accs):
            o_ref[h] = jnp.where(ok, acc, NEG_INF)

    vm = pl.BlockSpec(memory_space=pltpu.VMEM)
    return pl.pallas_call(
        body, name=name, in_specs=[vm, vm, pl.BlockSpec(memory_space=pltpu.SMEM)], out_specs=vm,
        out_shape=jax.ShapeDtypeStruct((HEADS_PER_GROUP, bq, 2 * bq), F32),
    )(bucket, valid, table_t)


def _bias_bwd(name, bucket, dbias):
    def body(bk_ref, db_ref, o_ref):
        row_id = lax.broadcasted_iota(jnp.int32, (N_BUCKETS, 128), 0)
        col_id = lax.broadcasted_iota(jnp.int32, (N_BUCKETS, 128), 1)
        bk = bk_ref[...]
        acc = jnp.zeros((N_BUCKETS, 128), F32)
        for h in range(HEADS_PER_GROUP):
            db = db_ref[h]
            for b in range(N_BUCKETS):
                part = jnp.sum(jnp.where(bk == b, db, 0.0), axis=0, keepdims=True)
                tot = jnp.sum(part, axis=1, keepdims=True)
                acc = jnp.where((row_id == b) & (col_id == h), tot, acc)
        o_ref[...] = acc

    vm = pl.BlockSpec(memory_space=pltpu.VMEM)
    return pl.pallas_call(body, name=name, in_specs=[vm, vm], out_specs=vm,
                          out_shape=jax.ShapeDtypeStruct((N_BUCKETS, 128), F32))(bucket, dbias)


def _head_of_lane(nrows):
    return lax.broadcasted_iota(jnp.int32, (nrows, ATTN_OUT), 1) // HEAD_DIM


def _stack_heads(a, lane_head):
    zero = jnp.zeros_like(a)
    return jnp.concatenate([jnp.where(lane_head == h, a, zero) for h in range(HEADS_PER_GROUP)], axis=0)


def _unstack_heads(a4, lane_head, bq):
    out = a4[:bq]
    for h in range(1, HEADS_PER_GROUP):
        out = jnp.where(lane_head == h, a4[h * bq:(h + 1) * bq], out)
    return out


def _attn_specs(geo, cols, col_block, index):
    return pl.BlockSpec(geo["block"] + (cols,), lambda r, n: index(r, n) + (col_block,))


def _attn_fwd(name, qkv, g, bias4):
    rows = qkv.shape[0]
    geo = _attn_geometry(g, rows)
    bq, (nsub, nb), index = geo["bq"], geo["grid"], geo["index"]
    blk_shape = tuple(b for b in geo["block"] if b is not None) + (ATTN_OUT,)

    def body(q_ref, kc_ref, kp_ref, vc_ref, vp_ref, b_ref, o_ref, lse_ref):
        n = pl.program_id(1)
        lane_head = _head_of_lane(bq)
        flat = lambda ref: ref[...].reshape(bq, ATTN_OUT)
        q4 = _stack_heads(flat(q_ref), lane_head)
        k2 = jnp.concatenate([flat(kp_ref), flat(kc_ref)], axis=0)
        v2 = jnp.concatenate([flat(vp_ref), flat(vc_ref)], axis=0)
        s = _dot_nt(q4, k2) + b_ref[...]
        col = lax.broadcasted_iota(jnp.int32, s.shape, 1)
        s = jnp.where((col >= bq) | (n > 0), s, NEG_INF)
        mx = jnp.max(s, axis=-1, keepdims=True)
        p = jnp.exp(s - mx)
        den = jnp.sum(p, axis=-1, keepdims=True)
        o4 = _dot_nn(p.astype(BF16), v2) / den
        lse4 = jnp.broadcast_to(mx + jnp.log(den), (HEADS_PER_GROUP * bq, ATTN_OUT))
        o_ref[...] = _unstack_heads(o4, lane_head, bq).reshape(blk_shape)
        lse_ref[...] = _unstack_heads(lse4, lane_head, bq).reshape(blk_shape)

    prev = lambda r, n: index(r, jnp.maximum(n - 1, 0))
    view = lambda a: a.reshape(geo["view"] + (a.shape[1],))
    qkv_v = view(qkv)
    out_spec = _attn_specs(geo, ATTN_OUT, 0, index)
    out_shape = jax.ShapeDtypeStruct(geo["view"] + (ATTN_OUT,), F32)
    o, lse = pl.pallas_call(
        body, name=name, grid=(nsub, nb),
        in_specs=[_attn_specs(geo, ATTN_OUT, g, index), _attn_specs(geo, ATTN_OUT, 3 + g, index),
                  _attn_specs(geo, ATTN_OUT, 3 + g, prev), _attn_specs(geo, ATTN_OUT, 6 + g, index),
                  _attn_specs(geo, ATTN_OUT, 6 + g, prev), pl.BlockSpec(bias4.shape, lambda r, n: (0, 0))],
        out_specs=[out_spec, out_spec], out_shape=[out_shape, out_shape],
        compiler_params=_cparams("parallel", "arbitrary"),
    )(qkv_v, qkv_v, qkv_v, qkv_v, qkv_v, bias4)
    return o.reshape(rows, ATTN_OUT), lse.reshape(rows, ATTN_OUT)


def _attn_bwd(name, qkv, do, lse, cvec, g, bias4):
    rows = qkv.shape[0]
    geo = _attn_geometry(g, rows)
    bq, (nsub, nb), index = geo["bq"], geo["grid"], geo["index"]
    blk_shape = tuple(b for b in geo["block"] if b is not None) + (ATTN_OUT,)
    nlead = len(blk_shape) - 1

    def body(q_ref, kc_ref, kp_ref, vc_ref, vp_ref, do_ref, lse_ref, c_ref, b_ref,
             dq_ref, dk_ref, dv_ref, db_ref, kcar_ref, vcar_ref):
        r, n = pl.program_id(0), pl.program_id(1)
        valid = n < nb
        lane_head = _head_of_lane(bq)
        flat = lambda ref: ref[...].reshape(bq, ATTN_OUT)

        @pl.when((r == 0) & (n == 0))
        def _():
            kcar_ref[...] = jnp.zeros_like(kcar_ref)
            vcar_ref[...] = jnp.zeros_like(vcar_ref)
            db_ref[...] = jnp.zeros_like(db_ref)

        def column(ref, h):
            lead = (slice(None),) * nlead
            return ref[lead + (pl.ds(h * HEAD_DIM, 1),)].reshape(bq, 1)

        q4 = _stack_heads(flat(q_ref), lane_head)
        do4 = _stack_heads(flat(do_ref), lane_head)
        k2 = jnp.concatenate([flat(kp_ref), flat(kc_ref)], axis=0)
        v2 = jnp.concatenate([flat(vp_ref), flat(vc_ref)], axis=0)
        lse4 = jnp.concatenate([column(lse_ref, h) for h in range(HEADS_PER_GROUP)], axis=0)
        c4 = jnp.concatenate([column(c_ref, h) for h in range(HEADS_PER_GROUP)], axis=0)
        s = _dot_nt(q4, k2) + b_ref[...]
        col = lax.broadcasted_iota(jnp.int32, s.shape, 1)
        keep = ((col >= bq) | (n > 0)) & valid
        p = jnp.where(keep, jnp.exp(s - lse4), 0.0)
        ds = p * (_dot_nt(do4, v2) + c4)
        ds_b = ds.astype(BF16)

        @pl.when(valid)
        def _():
            dq = _unstack_heads(_dot_nn(ds_b, k2), lane_head, bq) * (HEAD_DIM ** -0.5)
            dq_ref[...] = dq.astype(BF16).reshape(blk_shape)

        dk2 = _dot_tn(ds_b, q4)
        dv2 = _dot_tn(p.astype(BF16), do4)
        dk_ref[...] = (kcar_ref[...] + dk2[:bq]).astype(BF16).reshape(blk_shape)
        dv_ref[...] = (vcar_ref[...] + dv2[:bq]).astype(BF16).reshape(blk_shape)
        kcar_ref[...] = dk2[bq:]
        vcar_ref[...] = dv2[bq:]
        db_ref[...] += ds

    cur = lambda r, n: index(r, jnp.minimum(n, nb - 1))
    prev = lambda r, n: index(r, jnp.maximum(jnp.minimum(n, nb - 1) - 1, 0))
    late = lambda r, n: index(r, jnp.maximum(n - 1, 0))
    view = lambda a: a.reshape(geo["view"] + (a.shape[1],))
    qkv_v = view(qkv)
    tile = _attn_specs(geo, ATTN_OUT, 0, cur)
    bias_spec = pl.BlockSpec(bias4.shape, lambda r, n: (0, 0))
    out_shape = jax.ShapeDtypeStruct(geo["view"] + (ATTN_OUT,), BF16)
    dq, dk, dv, db = pl.pallas_call(
        body, name=name, grid=(nsub, nb + 1),
        in_specs=[_attn_specs(geo, ATTN_OUT, g, cur), _attn_specs(geo, ATTN_OUT, 3 + g, cur),
                  _attn_specs(geo, ATTN_OUT, 3 + g, prev), _attn_specs(geo, ATTN_OUT, 6 + g, cur),
                  _attn_specs(geo, ATTN_OUT, 6 + g, prev), tile, tile, tile, bias_spec],
        out_specs=[tile, _attn_specs(geo, ATTN_OUT, 0, late), _attn_specs(geo, ATTN_OUT, 0, late), bias_spec],
        out_shape=[out_shape, out_shape, out_shape, jax.ShapeDtypeStruct(bias4.shape, F32)],
        scratch_shapes=[pltpu.VMEM((bq, ATTN_OUT), F32), pltpu.VMEM((bq, ATTN_OUT), F32)],
        compiler_params=_cparams("arbitrary", "arbitrary"),
    )(qkv_v, qkv_v, qkv_v, qkv_v, qkv_v, view(do), view(lse), view(cvec), bias4)
    return dq.reshape(rows, ATTN_OUT), dk.reshape(rows, ATTN_OUT), dv.reshape(rows, ATTN_OUT), db


def _group_weights(lses):
    mx = jnp.maximum(jnp.maximum(lses[0], lses[1]), lses[2])
    es = [jnp.exp(l - mx) for l in lses]
    den = es[0] + es[1] + es[2]
    return [e / den for e in es]


def _combine_fwd(name, os_, lses):
    def fn(o0, o1, o2, l0, l1, l2):
        ws = _group_weights([l0, l1, l2])
        out = ws[0] * o0 + ws[1] * o1 + ws[2] * o2
        return out, out

    return _ew(name, fn, [*os_, *lses], [ATTN_OUT, ATTN_OUT], [F32, BF16], tm=1024)


def _combine_bwd(name, do, oa, lses):
    def fn(dov, oav, l0, l1, l2):
        head_sum = (lax.broadcasted_iota(jnp.int32, (ATTN_OUT, ATTN_OUT), 0) // HEAD_DIM
                    == lax.broadcasted_iota(jnp.int32, (ATTN_OUT, ATTN_OUT), 1) // HEAD_DIM)
        ws = _group_weights([l0, l1, l2])
        prod = dov * oav
        hi = prod.astype(BF16)
        lo = (prod - hi.astype(F32)).astype(BF16)
        ones = jnp.where(head_sum, 1.0, 0.0).astype(BF16)
        bar = _dot_nn(hi, ones) + _dot_nn(lo, ones)
        return tuple(w * dov for w in ws) + tuple(-w * bar for w in ws)

    return _ew(name, fn, [do, oa, *lses], [ATTN_OUT] * 6, [BF16] * 3 + [F32] * 3, tm=1024)


def _ssm_disc(a_re, a_im, log_dt, b_re, b_im):
    dt = jnp.exp(log_dt)
    mag = jnp.exp(a_re * dt)
    ab_re = mag * jnp.cos(a_im * dt)
    ab_im = mag * jnp.sin(a_im * dt)
    den = a_re * a_re + a_im * a_im
    xr = ab_re - 1.0
    coef_re = (xr * a_re + ab_im * a_im) / den
    coef_im = (ab_im * a_re - xr * a_im) / den
    bb_re = coef_re[None] * b_re - coef_im[None] * b_im
    bb_im = coef_re[None] * b_im + coef_im[None] * b_re
    return ab_re, ab_im, bb_re, bb_im


def _ssm_params_fwd(name, a_re, a_im, log_dt, b_re, b_im, c_re, c_im):
    pows = jax.ShapeDtypeStruct((SCAN_STEPS,) + a_re.shape, F32)
    mats = jax.ShapeDtypeStruct((SSM_PAIRS, PAIR_TILE, PAIR_TILE), BF16)
    per_tile = PAIR_TILE // (2 * SSM_GROUP)

    def body(ar, ai, ld, br, bi, cr, ci, o_pr, o_pi, o_bb, o_c, bbr_ref, bbi_ref, wide_ref):
        ab_re, ab_im, bb_re, bb_im = _ssm_disc(ar[...], ai[...], ld[...], br[...], bi[...])
        pr, pi = ab_re, ab_im
        for j in range(SCAN_STEPS):
            o_pr[j] = pr
            o_pi[j] = pi
            pr, pi = pr * ab_re - pi * ab_im, pr * ab_im + pi * ab_re
        bbr_ref[...] = bb_re
        bbi_ref[...] = bb_im

        def place(out_ref, block):
            wide_ref[...] = jnp.zeros_like(wide_ref)
            for g in range(SSM_GROUPS):
                p, l = divmod(g, 2)
                rows = pl.ds((p % per_tile) * 2 * SSM_GROUP + l * SSM_GROUP, SSM_GROUP)
                re, im = block(g)
                wide_ref[p, rows, pl.ds(l * SSM_STATE, SSM_STATE)] = re
                wide_ref[p, rows, pl.ds(PAIR_LANES + l * SSM_STATE, SSM_STATE)] = im
            out_ref[...] = wide_ref[...].astype(BF16)

        place(o_bb, lambda g: (bbr_ref[:, g, :], bbi_ref[:, g, :]))
        place(o_c, lambda g: (cr[g], -ci[g]))

    vm = pl.BlockSpec(memory_space=pltpu.VMEM)
    return pl.pallas_call(
        body, name=name, in_specs=[vm] * 7, out_specs=[vm] * 4, out_shape=[pows, pows, mats, mats],
        scratch_shapes=[pltpu.VMEM(b_re.shape, F32), pltpu.VMEM(b_re.shape, F32),
                        pltpu.VMEM((SSM_PAIRS, PAIR_TILE, PAIR_TILE), F32)],
    )(a_re, a_im, log_dt, b_re, b_im, c_re, c_im)


def _ssm_params_bwd(name, a_re, a_im, log_dt, b_re, b_im, d_ab_re, d_ab_im, d_bb_re, d_bb_im):
    gn = jax.ShapeDtypeStruct(a_re.shape, F32)
    cgn = jax.ShapeDtypeStruct(b_re.shape, F32)

    def body(ar, ai, ld, br, bi, g0, g1, g2, g3, o_ar, o_ai, o_ld, o_br, o_bi):
        _, vjp = jax.vjp(_ssm_disc, ar[...], ai[...], ld[...], br[...], bi[...])
        outs = vjp((g0[...], g1[...], g2[...], g3[...]))
        for o_ref, o in zip((o_ar, o_ai, o_ld, o_br, o_bi), outs):
            o_ref[...] = o

    vm = pl.BlockSpec(memory_space=pltpu.VMEM)
    return pl.pallas_call(body, name=name, in_specs=[vm] * 9, out_specs=[vm] * 5,
                          out_shape=[gn, gn, jax.ShapeDtypeStruct(log_dt.shape, F32), cgn, cgn],
                          )(a_re, a_im, log_dt, b_re, b_im, d_ab_re, d_ab_im, d_bb_re, d_bb_im)


def _scan_block(s_ref, carry_ref, tmp_ref, pw_ref, reverse, sprev=None):
    nl = SSM_LANES
    halves = range(SCAN_COLS // SCAN_SUB)
    zero = jnp.zeros((SCAN_SUB, SCAN_LANES), F32)
    for half in (reversed(halves) if reverse else halves):
        sub_rows = pl.ds(half * SCAN_SUB, SCAN_SUB)
        for lc in range(nl // SCAN_LANES):
            re_l = pl.ds(lc * SCAN_LANES, SCAN_LANES)
            im_l = pl.ds(nl + lc * SCAN_LANES, SCAN_LANES)
            are, aim = pw_ref[0, :, re_l], pw_ref[0, :, im_l]

            def step_of(j):
                return SCAN_STEPS - 1 - j if reverse else j

            def pass1(j, st):
                sr, si = st
                jj = step_of(j)
                nr = are * sr - aim * si + s_ref[jj, sub_rows, re_l]
                ni = are * si + aim * sr + s_ref[jj, sub_rows, im_l]
                s_ref[jj, sub_rows, re_l] = nr
                s_ref[jj, sub_rows, im_l] = ni
                return nr, ni

            er, ei = lax.fori_loop(0, SCAN_STEPS, pass1, (zero, zero), unroll=2)
            tmp_ref[0:SCAN_SUB, re_l] = er
            tmp_ref[0:SCAN_SUB, im_l] = ei
            apr, api = pw_ref[SCAN_STEPS - 1, 0:1, re_l], pw_ref[SCAN_STEPS - 1, 0:1, im_l]
            sr, si = carry_ref[0:1, re_l], carry_ref[0:1, im_l]
            for step in range(SCAN_SUB):
                c = SCAN_SUB - 1 - step if reverse else step
                tmp_ref[SCAN_SUB + c:SCAN_SUB + c + 1, re_l] = sr
                tmp_ref[SCAN_SUB + c:SCAN_SUB + c + 1, im_l] = si
                e_r, e_i = tmp_ref[c:c + 1, re_l], tmp_ref[c:c + 1, im_l]
                sr, si = apr * sr - api * si + e_r, apr * si + api * sr + e_i
            carry_ref[0:1, re_l] = sr
            carry_ref[0:1, im_l] = si
            cr = tmp_ref[SCAN_SUB:2 * SCAN_SUB, re_l]
            ci = tmp_ref[SCAN_SUB:2 * SCAN_SUB, im_l]

            if sprev is None:
                def pass2(j, st):
                    pr, pi = pw_ref[j, :, re_l], pw_ref[j, :, im_l]
                    jj = step_of(j)
                    s_ref[jj, sub_rows, re_l] += pr * cr - pi * ci
                    s_ref[jj, sub_rows, im_l] += pr * ci + pi * cr
                    return st

                lax.fori_loop(0, SCAN_STEPS, pass2, 0, unroll=2)
            else:
                st_ref, prev_ref, have_prev, dab_ref = sprev

                def corrected(jj, pr, pi):
                    gr = s_ref[jj, sub_rows, re_l] + pr * cr - pi * ci
                    gi = s_ref[jj, sub_rows, im_l] + pr * ci + pi * cr
                    s_ref[jj, sub_rows, re_l] = gr
                    s_ref[jj, sub_rows, im_l] = gi
                    return gr, gi

                def pass2(j, st):
                    dr, di = st
                    jj = SCAN_STEPS - 1 - j
                    gr, gi = corrected(jj, pw_ref[j, :, re_l], pw_ref[j, :, im_l])
                    qr, qi = st_ref[jj - 1, sub_rows, re_l], st_ref[jj - 1, sub_rows, im_l]
                    return dr + gr * qr + gi * qi, di + gi * qr - gr * qi

                dr, di = lax.fori_loop(0, SCAN_STEPS - 1, pass2, (zero, zero), unroll=2)
                gr, gi = corrected(0, pw_ref[SCAN_STEPS - 1, :, re_l], pw_ref[SCAN_STEPS - 1, :, im_l])
                sub = lax.broadcasted_iota(jnp.int32, (SCAN_SUB, SCAN_LANES), 0)
                if half == 0:
                    pv_r = prev_ref[SCAN_SUB - 1:SCAN_SUB, re_l] * have_prev
                    pv_i = prev_ref[SCAN_SUB - 1:SCAN_SUB, im_l] * have_prev
                else:
                    before = pl.ds(half * SCAN_SUB - 1, 1)
                    pv_r, pv_i = st_ref[SCAN_STEPS - 1, before, re_l], st_ref[SCAN_STEPS - 1, before, im_l]
                shape = (SCAN_SUB, SCAN_LANES)
                qr = jnp.where(sub == 0, jnp.broadcast_to(pv_r, shape),
                               pltpu.roll(st_ref[SCAN_STEPS - 1, sub_rows, re_l], 1, 0))
                qi = jnp.where(sub == 0, jnp.broadcast_to(pv_i, shape),
                               pltpu.roll(st_ref[SCAN_STEPS - 1, sub_rows, im_l], 1, 0))
                dab_ref[:, re_l] += dr + gr * qr + gi * qi
                dab_ref[:, im_l] += di + gi * qr - gr * qi


def _scan_view(a):
    return a.reshape(16, a.shape[0] // 16, a.shape[1])


def _pair_tile(p):
    start = (p * 2 * SSM_GROUP // PAIR_TILE) * PAIR_TILE
    return slice(start, start + PAIR_TILE)


def _pair_lanes(p):
    return pl.ds(p * PAIR_LANES, PAIR_LANES), pl.ds(SSM_LANES + p * PAIR_LANES, PAIR_LANES)


def _pair_store(s_ref, p, val):
    re_l, im_l = _pair_lanes(p)
    s_ref[:, :, re_l] = val[:, :PAIR_LANES].reshape(16, SCAN_COLS, PAIR_LANES)
    s_ref[:, :, im_l] = val[:, PAIR_LANES:].reshape(16, SCAN_COLS, PAIR_LANES)


def _pair_load(s_ref, p):
    re_l, im_l = _pair_lanes(p)
    parts = [s_ref[:, :, l].reshape(SCAN_BLOCK, PAIR_LANES) for l in (re_l, im_l)]
    return jnp.concatenate(parts, axis=1).astype(BF16)


def _pair_sum(fn):
    per = PAIR_TILE // (2 * SSM_GROUP)
    tiles = []
    for t in range(SSM_PAIRS // per):
        acc = None
        for p in range(t * per, (t + 1) * per):
            part = fn(p)
            acc = part if acc is None else acc + part
        tiles.append(acc)
    return jnp.concatenate(tiles, axis=1)


def _ssm_fwd(name, u, bb_mats, c_mats, pw_rows, d_skip):
    rows = u.shape[0]
    nl2 = 2 * SSM_LANES
    nblk = rows // SCAN_BLOCK

    def body(u_ref, bb_ref, c_ref, pw_ref, d_ref, y_ref, yg_ref, s_ref, carry_ref, tmp_ref):
        @pl.when(pl.program_id(0) == 0)
        def _():
            carry_ref[...] = jnp.zeros_like(carry_ref)

        uv = u_ref[...].reshape(SCAN_BLOCK, SSM_WIDTH)
        ub = uv.astype(BF16)
        for p in range(SSM_PAIRS):
            _pair_store(s_ref, p, _dot_nn(ub[:, _pair_tile(p)], bb_ref[p]))
        _scan_block(s_ref, carry_ref, tmp_ref, pw_ref, reverse=False)
        ys = _pair_sum(lambda p: _dot_nt(_pair_load(s_ref, p), c_ref[p]))
        yv = ys + d_ref[...] * uv
        y_ref[...] = yv.reshape(16, SCAN_COLS, SSM_WIDTH)
        yg_ref[...] = jax.nn.gelu(yv).astype(BF16).reshape(16, SCAN_COLS, SSM_WIDTH)

    const = lambda shape: pl.BlockSpec(shape, lambda i: (0,) * len(shape))
    blk = lambda cols: pl.BlockSpec((16, SCAN_COLS, cols), lambda i: (0, i, 0))
    pair_mats = const((SSM_PAIRS, PAIR_TILE, PAIR_TILE))
    y, yg, s = pl.pallas_call(
        body, name=name, grid=(nblk,),
        in_specs=[blk(SSM_WIDTH), pair_mats, pair_mats, const((SCAN_STEPS, SCAN_SUB, nl2)), const((1, SSM_WIDTH))],
        out_specs=[blk(SSM_WIDTH), blk(SSM_WIDTH), blk(nl2)],
        out_shape=[jax.ShapeDtypeStruct((16, rows // 16, SSM_WIDTH), F32),
                   jax.ShapeDtypeStruct((16, rows // 16, SSM_WIDTH), BF16),
                   jax.ShapeDtypeStruct((16, rows // 16, nl2), F32)],
        scratch_shapes=[pltpu.VMEM((SCAN_SUB, nl2), F32), pltpu.VMEM((2 * SCAN_SUB, nl2), F32)],
        compiler_params=_cparams("arbitrary"),
    )(_scan_view(u), bb_mats, c_mats, pw_rows, d_skip)
    return y.reshape(rows, SSM_WIDTH), yg.reshape(rows, SSM_WIDTH), s.reshape(rows, nl2)


def _ssm_bwd(name, dy, u, states, bb_mats, c_mats, pwc_rows, d_skip):
    rows = u.shape[0]
    nl2 = 2 * SSM_LANES
    nblk = rows // SCAN_BLOCK

    def body(dy_ref, u_ref, st_ref, prev_ref, bb_ref, c_ref, pw_ref, d_ref,
             du_ref, dbb_ref, dc_ref, dab_ref, dd_ref, g_ref, carry_ref, tmp_ref):
        i = pl.program_id(0)

        @pl.when(i == 0)
        def _():
            carry_ref[...] = jnp.zeros_like(carry_ref)
            for ref in (dbb_ref, dc_ref, dab_ref, dd_ref):
                ref[...] = jnp.zeros_like(ref)

        dyv = dy_ref[...].reshape(SCAN_BLOCK, SSM_WIDTH)
        uv = u_ref[...].reshape(SCAN_BLOCK, SSM_WIDTH)
        dyb, ub = dyv.astype(BF16), uv.astype(BF16)
        for p in range(SSM_PAIRS):
            _pair_store(g_ref, p, _dot_nn(dyb[:, _pair_tile(p)], c_ref[p]))
        have_prev = (i < nblk - 1).astype(F32)
        _scan_block(g_ref, carry_ref, tmp_ref, pw_ref, reverse=True,
                    sprev=(st_ref, prev_ref, have_prev, dab_ref))

        def pair_work(p):
            gp = _pair_load(g_ref, p)
            dbb_ref[p] += _dot_tn(ub[:, _pair_tile(p)], gp)
            dc_ref[p] += _dot_tn(dyb[:, _pair_tile(p)], _pair_load(st_ref, p))
            return _dot_nt(gp, bb_ref[p])

        du_ref[...] = (_pair_sum(pair_work) + d_ref[...] * dyv).reshape(16, SCAN_COLS, SSM_WIDTH)
        dd_ref[...] += jnp.sum(dyv * uv, axis=0, keepdims=True)

    const = lambda shape: pl.BlockSpec(shape, lambda i: (0,) * len(shape))
    blk = lambda cols: pl.BlockSpec((16, SCAN_COLS, cols), lambda i: (0, nblk - 1 - i, 0))
    per8 = SCAN_COLS // SCAN_SUB
    prev_spec = pl.BlockSpec((None, SCAN_SUB, nl2), lambda i: (15, jnp.maximum((nblk - 1 - i) * per8 - 1, 0), 0))
    pair_mats = const((SSM_PAIRS, PAIR_TILE, PAIR_TILE))
    pair_shape = jax.ShapeDtypeStruct((SSM_PAIRS, PAIR_TILE, PAIR_TILE), F32)
    sv = _scan_view(states)
    du, dbb, dc, dab, dd = pl.pallas_call(
        body, name=name, grid=(nblk,),
        in_specs=[blk(SSM_WIDTH), blk(SSM_WIDTH), blk(nl2), prev_spec, pair_mats, pair_mats,
                  const((SCAN_STEPS, SCAN_SUB, nl2)), const((1, SSM_WIDTH))],
        out_specs=[blk(SSM_WIDTH), pair_mats, pair_mats, const((SCAN_SUB, nl2)), const((1, SSM_WIDTH))],
        out_shape=[jax.ShapeDtypeStruct((16, rows // 16, SSM_WIDTH), F32), pair_shape, pair_shape,
                   jax.ShapeDtypeStruct((SCAN_SUB, nl2), F32), jax.ShapeDtypeStruct((1, SSM_WIDTH), F32)],
        scratch_shapes=[pltpu.VMEM((16, SCAN_COLS, nl2), F32), pltpu.VMEM((SCAN_SUB, nl2), F32),
                        pltpu.VMEM((2 * SCAN_SUB, nl2), F32)],
        compiler_params=_cparams("arbitrary"),
    )(_scan_view(dy), _scan_view(u), sv, sv, bb_mats, c_mats, pwc_rows, d_skip)
    return du.reshape(rows, SSM_WIDTH), dbb, dc, dab, dd


def _adamw(name, w, m, v, gparts, tr):
    rows, cols = w.shape

    def body(w_ref, m_ref, v_ref, g_ref, og_ref, od_ref, om_ref, ov_ref):
        g = g_ref[0].astype(F32)
        for i in range(1, N_DEV):
            g = g + g_ref[i].astype(F32)
        m_new = B1 * m_ref[...] + (1.0 - B1) * g
        v_new = B2 * v_ref[...] + (1.0 - B2) * (g * g)
        m_hat = m_new / (1.0 - B1 ** STEP)
        v_hat = v_new / (1.0 - B2 ** STEP)
        og_ref[...] = g
        od_ref[...] = -LR * (m_hat / (jnp.sqrt(v_hat) + ADAM_EPS) + WD * w_ref[...])
        om_ref[...] = m_new
        ov_ref[...] = v_new

    spec = pl.BlockSpec((tr, cols), lambda i: (i, 0))
    shape = jax.ShapeDtypeStruct((rows, cols), F32)
    return pl.pallas_call(
        body, name=name, grid=(rows // tr,),
        in_specs=[spec, spec, spec, pl.BlockSpec((N_DEV, tr, cols), lambda i: (0, i, 0))],
        out_specs=[spec] * 4, out_shape=[shape] * 4,
        compiler_params=_cparams("parallel"),
    )(w, m, v, gparts)


_SHARDED = (
    ("ffn1_w_gate", True, (352, 1024)), ("ffn1_w_up", True, (352, 1024)), ("ffn1_w_down", False, (352, 1024)),
    ("w_in", True, (608, 1024)), ("ssm_w_glu", True, (128, 512)), ("w_attn_branch", True, (128, 256)),
    ("w_ssm_branch", True, (128, 512)), ("w_out", False, (128, 1024)),
    ("ffn2_w_gate", True, (352, 1024)), ("ffn2_w_up", True, (352, 1024)), ("ffn2_w_down", False, (352, 1024)),
)
_SMALL = ("ffn1_norm", "mix_norm", "gate_bias", "rel_bias_table", "ssm_a_re", "ssm_a_im", "ssm_log_dt",
          "ssm_b_re", "ssm_b_im", "ssm_c_re", "ssm_c_im", "ssm_d", "ffn2_norm", "final_norm")
_ORDER = ("ffn1_norm", "ffn1_w_gate", "ffn1_w_up", "ffn1_w_down", "mix_norm", "w_in", "gate_bias",
          "rel_bias_table", "ssm_a_re", "ssm_a_im", "ssm_log_dt", "ssm_b_re", "ssm_b_im", "ssm_c_re",
          "ssm_c_im", "ssm_d", "ssm_w_glu", "w_attn_branch", "w_ssm_branch", "w_out", "ffn2_norm",
          "ffn2_w_gate", "ffn2_w_up", "ffn2_w_down", "final_norm")


def _pack_rows(shape):
    return shape[0] * shape[1] // D_MODEL


_SHARD_INFO = {nm: (tr, shape) for nm, tr, shape in _SHARDED}
_PHASES = {
    "f1gu": ("ffn1_w_gate", "ffn1_w_up"), "f1d": ("ffn1_w_down",),
    "mix": ("w_in", "ssm_w_glu", "w_attn_branch", "w_ssm_branch", "w_out"),
    "f2": ("ffn2_w_gate", "ffn2_w_up", "ffn2_w_down"),
}


def _to_rows(a, nm):
    tr, shape = _SHARD_INFO[nm]
    return (a.T if tr else a).reshape(_pack_rows(shape), D_MODEL)


def _from_rows(p, nm):
    tr, shape = _SHARD_INFO[nm]
    a = p.reshape(shape)
    return a.T if tr else a


def _full_weight(gathered, nm):
    _, shape = _SHARD_INFO[nm]
    return gathered.reshape(N_DEV * shape[0], shape[1])


def _grad_blocks(g, nm):
    _, shape = _SHARD_INFO[nm]
    return g.astype(BF16).reshape(N_DEV, _pack_rows(shape), D_MODEL)


_SMALL_TILE = 8 * 128


def _small_rows(a):
    flat = a.reshape(-1)
    return jnp.pad(flat, (0, (-flat.shape[0]) % _SMALL_TILE)).reshape(-1, 128)


def _pack_small(ws, last=None):
    tail = jnp.zeros((), F32) if last is None else last
    return jnp.concatenate([_small_rows(ws[nm]) for nm in _SMALL] + [_small_rows(tail)], axis=0)


def _unpack_small(pack, like):
    out, r0 = {}, 0
    for nm in _SMALL:
        n = like[nm].size
        nr = 8 * -(-n // _SMALL_TILE)
        out[nm] = pack[r0:r0 + nr].reshape(-1)[:n].reshape(like[nm].shape)
        r0 += nr
    return out


def _residue_order(a):
    rows, cols = a.shape
    return a.reshape(rows // 16, 16, cols).transpose(1, 0, 2).reshape(rows, cols)


def _token_order(a):
    rows, cols = a.shape
    return a.reshape(16, rows // 16, cols).transpose(1, 0, 2).reshape(rows, cols)


_PAIRS_PER_TILE = PAIR_TILE // (2 * SSM_GROUP)
_PAIR_AXES = (SSM_PAIRS // _PAIRS_PER_TILE, _PAIRS_PER_TILE, 2)


def _pair_diagonals(acc):
    k, j, l = _PAIR_AXES
    eight = acc.reshape(k, j, j, l, SSM_GROUP, 2, l, SSM_STATE)
    eye_j, eye_l = jnp.eye(j, dtype=acc.dtype), jnp.eye(l, dtype=acc.dtype)
    own = jnp.einsum("kjJLcxln,jJ,lL->xkjlcn", eight, eye_j, eye_l).reshape(2, SSM_GROUPS, SSM_GROUP, SSM_STATE)
    return own[0], own[1]


def _local_step(xs, target, small, weights_of, send_grads, first_deps=()):
    rows = xs.shape[0]
    gfull, gsmall = {}, {}

    table_t = small["rel_bias_table"].T
    tables, bias4 = [], []
    for g in range(N_GROUPS):
        bucket, valid = [jnp.asarray(t) for t in _attn_tables(g, rows)]
        bias_g = _bias_fwd(f"rel_bias_fwd_{g}", bucket, valid, table_t[g * HEADS_PER_GROUP:(g + 1) * HEADS_PER_GROUP])
        tables.append(bucket)
        bias4.append(bias_g.reshape(-1, bias_g.shape[-1]))
    pw_re, pw_im, bb_mats, c_mats = _ssm_params_fwd(
        "ssm_params_fwd", small["ssm_a_re"], small["ssm_a_im"], small["ssm_log_dt"].reshape(SSM_GROUPS, 1),
        small["ssm_b_re"].transpose(2, 0, 1), small["ssm_b_im"].transpose(2, 0, 1), small["ssm_c_re"], small["ssm_c_im"])

    def power_rows(sign):
        row = jnp.concatenate([pw_re.reshape(SCAN_STEPS, 1, SSM_LANES), sign * pw_im.reshape(SCAN_STEPS, 1, SSM_LANES)],
                              axis=2)
        return jnp.broadcast_to(row, (SCAN_STEPS, SCAN_SUB, 2 * SSM_LANES))

    pw_fwd, pw_bwd = power_rows(1.0), power_rows(-1.0)
    d_skip = small["ssm_d"].reshape(1, SSM_WIDTH)
    wf = dict(weights_of("f1", [xs, target, bb_mats, c_mats, pw_fwd, pw_bwd] + bias4))

    x1, h1, gg1, uu1, hmix = _ffn_fwd("ffn1_fwd", xs, small["ffn1_norm"], wf["ffn1_w_gate"], wf["ffn1_w_up"],
                                      wf["ffn1_w_down"], small["mix_norm"], deps=first_deps)
    wf.update(weights_of("mix", x1))
    w_in = wf["w_in"]
    w_qkv, w_u, w_g = w_in[:3 * ATTN_WIDTH], w_in[3 * ATTN_WIDTH:3 * ATTN_WIDTH + SSM_WIDTH], w_in[3 * ATTN_WIDTH + SSM_WIDTH:]
    qscale = jnp.concatenate([jnp.full((1, ATTN_WIDTH), HEAD_DIM ** -0.5, F32), jnp.ones((1, 2 * ATTN_WIDTH), F32)], axis=1)
    u_col, g_col = 3 * ATTN_WIDTH, 3 * ATTN_WIDTH + SSM_WIDTH

    def in_split(acc, sc, b):
        return acc[:, :u_col] * sc, acc[:, u_col:g_col], _sigmoid(acc[:, g_col:] + b)

    qkv, u, gates = _mm("in_proj", [(hmix, w_in)], True, w_in.shape[0], [BF16, F32, BF16], epilogue=in_split,
                        extras=[(qscale, 0), (small["gate_bias"], 0)], tm=512, tn=w_in.shape[0],
                        out_cols=[3 * ATTN_WIDTH, SSM_WIDTH, 2 * D_MODEL])

    o_g, lse_g = [], []
    for g in range(N_GROUPS):
        o, lse = _attn_fwd(f"attn_fwd_{g}", qkv, g, bias4[g])
        o_g.append(o)
        lse_g.append(lse)
    oa_f32, oa = _combine_fwd("attn_combine_fwd", o_g, lse_g)
    y_attn, = _mm("attn_branch", [(oa, wf["w_attn_branch"])], True, D_MODEL, [BF16])
    y_raw, ygelu, states = _ssm_fwd("ssm_fwd", u, bb_mats, c_mats, pw_fwd, d_skip)
    glu, ysg = _mm("ssm_glu", [(ygelu, wf["ssm_w_glu"])], True, 2 * SSM_WIDTH, [BF16, BF16],
                   epilogue=lambda gv: (gv, gv[:, :SSM_WIDTH] * _sigmoid(gv[:, SSM_WIDTH:])),
                   tn=2 * SSM_WIDTH, out_cols=[2 * SSM_WIDTH, SSM_WIDTH])
    y_ssm, merged = _mm("ssm_branch_merge", [(ysg, wf["w_ssm_branch"])], True, D_MODEL, [BF16, BF16],
                        epilogue=lambda acc, ga, gs, ya: (acc, ga * ya + gs * acc),
                        extras=[(gates, 0), (gates, D_MODEL), (y_attn, 0)])
    x2, = _mm("mix_out", [(merged, wf["w_out"])], False, D_MODEL, [F32],
              epilogue=lambda acc, res: (res + acc,), extras=[(x1, 0)])
    wf.update(weights_of("f2", x2))
    dx3, h2, gg2, uu2, gsmall["final_norm"], gsmall["loss"] = _ffn_fwd_head(
        "ffn2_fwd", x2, small["ffn2_norm"], wf["ffn2_w_gate"], wf["ffn2_w_up"], wf["ffn2_w_down"],
        small["final_norm"].reshape(1, D_MODEL), target)

    dx2, dgg2, duu2, act2, gsmall["ffn2_norm"] = _ffn_bwd(
        "ffn2_bwd", dx3, x2, small["ffn2_norm"], gg2, uu2, wf["ffn2_w_gate"], wf["ffn2_w_up"], wf["ffn2_w_down"])
    gfull["ffn2_w_gate"] = _mm_tn("ffn2_dwg", dgg2, h2, out_dtype=BF16, tk=FFN_DW_ROWS)
    gfull["ffn2_w_up"] = _mm_tn("ffn2_dwu", duu2, h2, out_dtype=BF16, tk=FFN_DW_ROWS)
    gfull["ffn2_w_down"] = _mm_tn("ffn2_dwd", act2, dx3, scale=0.5, out_dtype=BF16, tk=FFN_DW_ROWS)
    sent = send_grads("f2", gfull)

    def merge_bwd(dm, ga, gs, ya, ys):
        dza, dzs = dm * ya * ga * (1.0 - ga), dm * ys * gs * (1.0 - gs)
        return (dm * ga, dm * gs, dza, dzs, jnp.sum(dza, axis=0, keepdims=True), jnp.sum(dzs, axis=0, keepdims=True))

    dya, dys, dzga, dzgs, dba, dbs = _mm(
        "mix_out_bwd", [(dx2, wf["w_out"])], True, D_MODEL, [BF16] * 4, epilogue=merge_bwd, row_sums=2,
        extras=[(gates, 0), (gates, D_MODEL), (y_attn, 0), (y_ssm, 0)], deps=sent, tm=512, tn=D_MODEL)
    gfull["w_out"] = _mm_tn("dw_out", merged, dx2, out_dtype=BF16)
    gsmall["gate_bias"] = jnp.concatenate([dba, dbs], axis=1)

    gfull["w_ssm_branch"] = _mm_tn("dw_ssm_branch", dys, ysg, out_dtype=BF16)

    def glu_bwd(dysg, av, bv):
        sb = _sigmoid(bv)
        return (dysg * sb, dysg * av * sb * (1.0 - sb))

    dglu_a, dglu_b = _mm("ssm_branch_bwd", [(dys, wf["w_ssm_branch"])], False, SSM_WIDTH, [BF16, BF16],
                         epilogue=glu_bwd, extras=[(glu, 0), (glu, SSM_WIDTH)])
    w_glu = wf["ssm_w_glu"]
    gfull["ssm_w_glu"] = _mm_tn_stack("dw_glu", [dglu_a, dglu_b], ygelu, out_dtype=BF16)

    def gelu_bwd(acc, yv):
        _, vjp = jax.vjp(jax.nn.gelu, yv)
        return (vjp(acc)[0],)

    dy_raw, = _mm("ssm_glu_bwd", [(dglu_a, w_glu[:SSM_WIDTH]), (dglu_b, w_glu[SSM_WIDTH:])], False, SSM_WIDTH, [F32],
                  epilogue=gelu_bwd, extras=[(y_raw, 0)])
    du, dbb_acc, dc_acc, dab_rows, gsmall_d = _ssm_bwd(
        "ssm_bwd", dy_raw, u, states, bb_mats, c_mats, pw_bwd, d_skip)
    gsmall["ssm_d"] = gsmall_d
    dbb_re, dbb_im = [a.transpose(1, 0, 2) for a in _pair_diagonals(dbb_acc)]
    dc_re, dc_im = _pair_diagonals(dc_acc)
    gsmall["ssm_c_re"], gsmall["ssm_c_im"] = dc_re, -dc_im
    dab = _colsum("ssm_dab", dab_rows)
    d_ar, d_ai, d_ld, d_br, d_bi = _ssm_params_bwd(
        "ssm_params_bwd", small["ssm_a_re"], small["ssm_a_im"], small["ssm_log_dt"].reshape(SSM_GROUPS, 1),
        small["ssm_b_re"].transpose(2, 0, 1), small["ssm_b_im"].transpose(2, 0, 1),
        dab[:, :SSM_LANES].reshape(SSM_GROUPS, SSM_STATE), dab[:, SSM_LANES:].reshape(SSM_GROUPS, SSM_STATE),
        dbb_re, dbb_im)
    gsmall["ssm_a_re"], gsmall["ssm_a_im"], gsmall["ssm_log_dt"] = d_ar, d_ai, d_ld.reshape(SSM_GROUPS)
    gsmall["ssm_b_re"], gsmall["ssm_b_im"] = d_br.transpose(1, 2, 0), d_bi.transpose(1, 2, 0)

    gfull["w_attn_branch"] = _mm_tn("dw_attn_branch", dya, oa, out_dtype=BF16)
    doa, = _mm("attn_branch_bwd", [(dya, wf["w_attn_branch"])], False, ATTN_OUT, [F32])
    dc = _combine_bwd("attn_combine_bwd", doa, oa_f32, lse_g)
    dqkv_cols = [None] * 9
    dtable = []
    for g in range(N_GROUPS):
        dq, dk, dv, db = _attn_bwd(f"attn_bwd_{g}", qkv, dc[g], lse_g[g], dc[3 + g], g, bias4[g])
        dqkv_cols[g], dqkv_cols[3 + g], dqkv_cols[6 + g] = dq, dk, dv
        dt = _bias_bwd(f"rel_bias_bwd_{g}", tables[g], db.reshape(HEADS_PER_GROUP, -1, db.shape[-1]))
        dtable.append(dt[:, :HEADS_PER_GROUP])
    gsmall["rel_bias_table"] = jnp.concatenate(dtable, axis=1)

    gfull["w_in"] = jnp.concatenate([_mm_tn_stack("dw_in_qkv", dqkv_cols, hmix, out_dtype=BF16),
                                     _mm_tn_stack("dw_in_rest", [du, dzga, dzgs], hmix, out_dtype=BF16)], axis=0)
    sent = send_grads("mix", gfull)
    qkv_pairs = [(c, w_qkv[i * ATTN_OUT:(i + 1) * ATTN_OUT]) for i, c in enumerate(dqkv_cols)]

    def mix_norm_bwd(dh, xv, gain, dres):
        r, xh = _rms_parts(xv)
        return dres + _rms_bwd_dx(dh, gain, r, xh), jnp.sum(dh * xh, axis=0, keepdims=True)

    dx1, gsmall["mix_norm"] = _mm(
        "in_bwd", qkv_pairs + [(du, w_u), (dzga, w_g[:D_MODEL]), (dzgs, w_g[D_MODEL:])], False, D_MODEL, [F32],
        epilogue=mix_norm_bwd, row_sums=1, extras=[(x1, 0), (small["mix_norm"], 0), (dx2, 0)], tm=512, tn=D_MODEL,
        deps=sent)

    dx, dgg1, duu1, act1, gsmall["ffn1_norm"] = _ffn_bwd(
        "ffn1_bwd", dx1, xs, small["ffn1_norm"], gg1, uu1, wf["ffn1_w_gate"], wf["ffn1_w_up"], wf["ffn1_w_down"])
    sent = send_grads("small", gsmall)
    gfull["ffn1_w_gate"] = _mm_tn("ffn1_dwg", dgg1, h1, deps=sent, out_dtype=BF16, tk=FFN_DW_ROWS)
    gfull["ffn1_w_up"] = _mm_tn("ffn1_dwu", duu1, h1, out_dtype=BF16, tk=FFN_DW_ROWS)
    sent = send_grads("f1gu", gfull)
    gfull["ffn1_w_down"] = _mm_tn("ffn1_dwd", act1, dx1, scale=0.5, deps=sent, out_dtype=BF16, tk=FFN_DW_ROWS)
    send_grads("f1d", gfull)
    return dx, gsmall


def kernel(x, ffn1_norm, ffn1_w_gate, ffn1_w_up, ffn1_w_down, mix_norm, w_in, gate_bias, rel_bias_table, ssm_a_re, ssm_a_im, ssm_log_dt, ssm_b_re, ssm_b_im, ssm_c_re, ssm_c_im, ssm_d, ssm_w_glu, w_attn_branch, w_ssm_branch, w_out, ffn2_norm, ffn2_w_gate, ffn2_w_up, ffn2_w_down, final_norm, loss_target, m_ffn1_norm, m_ffn1_w_gate, m_ffn1_w_up, m_ffn1_w_down, m_mix_norm, m_w_in, m_gate_bias, m_rel_bias_table, m_ssm_a_re, m_ssm_a_im, m_ssm_log_dt, m_ssm_b_re, m_ssm_b_im, m_ssm_c_re, m_ssm_c_im, m_ssm_d, m_ssm_w_glu, m_w_attn_branch, m_w_ssm_branch, m_w_out, m_ffn2_norm, m_ffn2_w_gate, m_ffn2_w_up, m_ffn2_w_down, m_final_norm, v_ffn1_norm, v_ffn1_w_gate, v_ffn1_w_up, v_ffn1_w_down, v_mix_norm, v_w_in, v_gate_bias, v_rel_bias_table, v_ssm_a_re, v_ssm_a_im, v_ssm_log_dt, v_ssm_b_re, v_ssm_b_im, v_ssm_c_re, v_ssm_c_im, v_ssm_d, v_ssm_w_glu, v_w_attn_branch, v_w_ssm_branch, v_w_out, v_ffn2_norm, v_ffn2_w_gate, v_ffn2_w_up, v_ffn2_w_down, v_final_norm):
    given = dict(locals())
    shapes = {nm: given[nm].shape for nm in _ORDER}

    def strip(a):
        return a[0] if a.ndim >= 2 and a.shape[0] == 1 else a

    w = {nm: strip(given[nm]) for nm in _ORDER}
    m = {nm: strip(given["m_" + nm]) for nm in _ORDER}
    v = {nm: strip(given["v_" + nm]) for nm in _ORDER}
    for d in (w, m, v):
        d["rel_bias_table"] = d["rel_bias_table"].reshape(N_BUCKETS, N_GROUPS * HEADS_PER_GROUP)

    weight_phases = {"f1": _PHASES["f1gu"] + _PHASES["f1d"], "mix": _PHASES["mix"], "f2": _PHASES["f2"]}
    pending_w, w_rows, deps, zero = {}, {}, [], 0.0
    for phase, names in weight_phases.items():
        w_rows.update({nm: _to_rows(w[nm] + zero, nm) for nm in names})
        pending_w[phase] = _exchange_start(f"gather_{phase}_start", [w_rows[nm].astype(BF16) for nm in names],
                                           gather=True, deps=deps)
        deps = [pending_w[phase][4]]
        zero = pending_w["f1"][4][0, 0]
    m_rows = {nm: _to_rows(m[nm] + zero, nm) for nm in _SHARD_INFO}
    v_rows = {nm: _to_rows(v[nm] + zero, nm) for nm in _SHARD_INFO}
    small = {nm: w[nm] for nm in _SMALL}
    small_in = {nm: small[nm] + zero for nm in _SMALL}
    for nm in ("ffn1_norm", "mix_norm", "ffn2_norm", "gate_bias"):
        small_in[nm] = small_in[nm].reshape(1, -1)

    def weights_of(phase, after):
        if phase == "f1":
            after = list(after) + list(m_rows.values()) + list(v_rows.values())
        landed = _exchange_wait(f"gather_{phase}_wait", pending_w[phase], after, gather=True)
        return {nm: _full_weight(got, nm) for nm, got in zip(weight_phases[phase], landed)}

    pending_g = {}

    def send_grads(phase, grads):
        if phase == "small":
            gs_pack = _pack_small({nm: grads[nm].reshape(small[nm].shape) for nm in _SMALL}, last=grads["loss"])
            pending_g[phase] = _exchange_start("gather_small_start", [gs_pack], gather=True)
        else:
            pending_g[phase] = _exchange_start(f"scatter_{phase}_start",
                                               [_grad_blocks(grads[nm], nm) for nm in _PHASES[phase]], gather=False)
        return [pending_g[phase][4]]

    dx, gsmall = _local_step(_residue_order(x[0]), _residue_order(loss_target[0]), small_in, weights_of, send_grads,
                             first_deps=[pending_w["f2"][4]])
    dx = _token_order(dx)

    updated = {}
    after = pending_g["f1d"][4]
    for phase in ("f2", "mix", "small", "f1gu", "f1d"):
        landed = _exchange_wait(f"exchange_{phase}_wait", pending_g[phase], after, gather=phase == "small")
        if phase == "small":
            sm = _adamw("adamw_small", _pack_small(small), _pack_small({nm: m[nm] for nm in _SMALL}),
                        _pack_small({nm: v[nm] for nm in _SMALL}), landed[0], landed[0].shape[1])
            after = sm[0]
            continue
        for nm, recv in zip(_PHASES[phase], landed):
            tr = max(t for t in range(16, 353, 16) if w_rows[nm].shape[0] % t == 0)
            updated[nm] = _adamw(f"adamw_{nm}", w_rows[nm], m_rows[nm], v_rows[nm], recv, tr)
            after = updated[nm][0]

    loss = sm[0][-8, 0]
    outs = []
    for i in range(4):
        sml = _unpack_small(sm[i], small)
        outs.append([(_from_rows(updated[nm][i], nm) if nm in updated else sml[nm]).reshape(shapes[nm])
                     for nm in _ORDER])
    return (loss, dx[None], *outs[0], *outs[1], *outs[2], *outs[3])
```

```python
import math

import numpy as np
import jax
import jax.numpy as jnp
from jax import lax
from jax.experimental import pallas as pl
from jax.experimental.pallas import tpu as pltpu

F32 = jnp.float32
BF16 = jnp.bfloat16

N_DEV = 8
D_MODEL = 1024
HEAD_DIM = 64
HEADS_PER_GROUP = 4
DILATIONS = (1, 4, 16)
N_GROUPS = 3
ATTN_WIDTH = 768
ATTN_OUT = 256
BLOCK = 128
N_BUCKETS = 32
MAX_DISTANCE = 2048
NEG_INF = -1e30
SSM_WIDTH = 512
SSM_GROUPS = 32
SSM_GROUP = 16
SSM_STATE = 64
SSM_LANES = SSM_GROUPS * SSM_STATE
SSM_PAIRS = SSM_GROUPS // 2
PAIR_LANES = 2 * SSM_STATE
PAIR_TILE = 256
EPS = 1e-6
LR, B1, B2, ADAM_EPS, WD, STEP = 0.001, 0.9, 0.999, 1e-08, 0.01, 10

VMEM_LIMIT_BYTES = 56 * 1024 * 1024
FFN_CHUNK = 768
FFN_DW_ROWS = 2048
SCAN_BLOCK = 256
SCAN_STEPS = 16
SCAN_COLS = SCAN_BLOCK // SCAN_STEPS
SCAN_SUB = 8
SCAN_LANES = 512

MESH = pl.DeviceIdType.MESH


def _cparams(*sem):
    return pltpu.CompilerParams(dimension_semantics=sem, vmem_limit_bytes=VMEM_LIMIT_BYTES)


def _dot(a, b, dims):
    return lax.dot_general(a, b, (dims, ((), ())), preferred_element_type=F32)


def _dot_nn(a, b):
    return _dot(a, b, ((1,), (0,)))


def _dot_nt(a, b):
    return _dot(a, b, ((1,), (1,)))


def _dot_tn(a, b):
    return _dot(a, b, ((0,), (0,)))


def _sigmoid(x):
    return 1.0 / (1.0 + jnp.exp(-x))


_HBM_SPEC = pl.BlockSpec(memory_space=pltpu.HBM)
_SEM_SPEC = pl.BlockSpec(memory_space=pltpu.SEMAPHORE)
_ANY_SPEC = pl.BlockSpec(memory_space=pl.ANY)
_EFFECT = pltpu.SideEffectType.DATAFLOW_SIDE_EFFECTING


def _peers(x, y, c):
    return [(1 - x if k & 4 else x, 1 - y if k & 2 else y, 1 - c if k & 1 else c) for k in range(1, N_DEV)]


def _exchange_copies(x_refs, land_refs, send_sems, recv_sems, gather):
    x, y, c = lax.axis_index("x"), lax.axis_index("y"), lax.axis_index("c")
    me = 4 * x + 2 * y + c
    copies = []
    for a, (x_ref, land_ref) in enumerate(zip(x_refs, land_refs)):
        for k, (px, py, pc) in enumerate(_peers(x, y, c)):
            src = x_ref if gather else x_ref.at[4 * px + 2 * py + pc]
            copies.append(pltpu.make_async_remote_copy(
                src_ref=src, dst_ref=land_ref.at[me], send_sem=send_sems.at[N_DEV * a + k],
                recv_sem=recv_sems.at[(N_DEV - 1) * a + k], device_id=(px, py, pc), device_id_type=MESH))
    owns = [pltpu.make_async_copy(x_ref if gather else x_ref.at[me], land_ref.at[me],
                                  send_sems.at[N_DEV * a + N_DEV - 1])
            for a, (x_ref, land_ref) in enumerate(zip(x_refs, land_refs))]
    return owns, copies


def _exchange_start(name, xs_list, gather, deps=()):
    n, nd = len(xs_list), len(deps)
    land_shapes = [(N_DEV, *xs.shape) if gather else xs.shape for xs in xs_list]

    def body(*refs):
        x_refs, land_refs = refs[:n], refs[n:2 * n]
        send_sems, recv_sems = refs[2 * n + nd:2 * n + nd + 2]
        token = refs[-1]
        owns, copies = _exchange_copies(x_refs, land_refs, send_sems, recv_sems, gather)
        for cp in copies + owns:
            cp.start()
        token[...] = jnp.zeros_like(token)

    hbm = lambda a: pltpu.with_memory_space_constraint(a, pltpu.HBM)
    outs = pl.pallas_call(
        body, name=name,
        out_shape=(pltpu.SemaphoreType.DMA((n * N_DEV,)), pltpu.SemaphoreType.DMA((n * (N_DEV - 1),)),
                   *[pltpu.HBM(xs.shape, xs.dtype) for xs in xs_list],
                   *[pltpu.HBM(shape, xs.dtype) for shape, xs in zip(land_shapes, xs_list)],
                   jax.ShapeDtypeStruct((8, 128), F32)),
        in_specs=(_HBM_SPEC,) * (2 * n) + (_ANY_SPEC,) * nd,
        out_specs=(_SEM_SPEC, _SEM_SPEC) + (_HBM_SPEC,) * (2 * n) + (pl.BlockSpec(memory_space=pltpu.VMEM),),
        input_output_aliases={i: 2 + i for i in range(2 * n)},
        compiler_params=pltpu.CompilerParams(has_side_effects=_EFFECT),
    )(*[hbm(xs) for xs in xs_list], *[hbm(lax.empty(shape, xs.dtype)) for shape, xs in zip(land_shapes, xs_list)],
      *deps)
    return outs[0], outs[1], list(outs[2:2 + n]), list(outs[2 + n:2 + 2 * n]), outs[-1]


def _exchange_wait(name, handle, after, gather):
    send_sems, recv_sems, xs_thru, lands_thru, _ = handle
    n = len(xs_thru)
    after = list(after) if isinstance(after, (list, tuple)) else [after]

    def body(*refs):
        x_refs, land_refs = refs[:n], refs[n:2 * n]
        send_sems, recv_sems = refs[2 * n:2 * n + 2]
        owns, copies = _exchange_copies(x_refs, land_refs, send_sems, recv_sems, gather)
        for cp in copies:
            cp.wait_send()
            cp.wait_recv()
        for cp in owns:
            cp.wait()

    outs = pl.pallas_call(
        body, name=name,
        out_shape=tuple(pltpu.HBM(a.shape, a.dtype) for a in xs_thru + lands_thru),
        in_specs=(_HBM_SPEC,) * (2 * n) + (_SEM_SPEC, _SEM_SPEC) + (_ANY_SPEC,) * len(after),
        out_specs=(_HBM_SPEC,) * (2 * n), input_output_aliases={i: i for i in range(2 * n)},
        compiler_params=pltpu.CompilerParams(has_side_effects=_EFFECT),
    )(*xs_thru, *lands_thru, send_sems, recv_sems, *after)
    return list(outs[n:])


def _mm(name, pairs, nt, n_cols, out_dtypes, epilogue=None, extras=(), tm=1024, tn=1024, deps=(), row_sums=0,
        out_cols=None):
    rows = pairs[0][0].shape[0]
    tm = min(tm, rows)
    tn = min(tn, n_cols)
    na, ne, nd, no = len(pairs), len(extras), len(deps), len(out_dtypes)

    def body(*refs):
        a_refs, w_refs = refs[:na], refs[na:2 * na]
        e_refs, o_refs = refs[2 * na:2 * na + ne], refs[2 * na + ne + nd:]
        acc = None
        for a_ref, w_ref in zip(a_refs, w_refs):
            a = a_ref[...].astype(BF16)
            w = w_ref[...].astype(BF16)
            p = _dot_nt(a, w) if nt else _dot_nn(a, w)
            acc = p if acc is None else acc + p
        outs = (acc,) if epilogue is None else epilogue(acc, *[e[...].astype(F32) for e in e_refs])
        for o_ref, o in zip(o_refs[:no], outs[:no]):
            o_ref[...] = o.astype(o_ref.dtype)
        for r_ref, o in zip(o_refs[no:], outs[no:]):
            @pl.when(pl.program_id(0) == 0)
            def _():
                r_ref[...] = jnp.zeros_like(r_ref)

            r_ref[...] += o

    in_specs = [pl.BlockSpec((tm, a.shape[1]), lambda i, j: (i, 0)) for a, _ in pairs]
    for _, w in pairs:
        if nt:
            in_specs.append(pl.BlockSpec((tn, w.shape[1]), lambda i, j: (j, 0)))
        else:
            in_specs.append(pl.BlockSpec((w.shape[0], tn), lambda i, j: (0, j)))
    for e, col_off in extras:
        off = col_off // tn
        if e.shape[0] == 1:
            width = tn if out_cols is None else e.shape[1]
            in_specs.append(pl.BlockSpec((1, width), lambda i, j, off=off: (0, j + off)))
        else:
            in_specs.append(pl.BlockSpec((tm, tn), lambda i, j, off=off: (i, j + off)))
    in_specs += [_ANY_SPEC] * nd
    if out_cols is None:
        out_cols = [n_cols] * no
    else:
        assert tn == n_cols, "outputs of other widths need the whole row in one block"
    assert not row_sums or tn == n_cols
    out_specs = [pl.BlockSpec((tm, tn * c // n_cols), lambda i, j: (i, j)) for c in out_cols]
    out_specs += [pl.BlockSpec((1, tn), lambda i, j: (0, j))] * row_sums
    out_shape = [jax.ShapeDtypeStruct((rows, c), dt) for c, dt in zip(out_cols, out_dtypes)]
    out_shape += [jax.ShapeDtypeStruct((1, n_cols), F32)] * row_sums
    outs = pl.pallas_call(
        body, name=name, grid=(rows // tm, n_cols // tn),
        in_specs=in_specs, out_specs=out_specs, out_shape=out_shape,
        compiler_params=_cparams("arbitrary" if row_sums else "parallel", "arbitrary"),
    )(*[a for a, _ in pairs], *[w for _, w in pairs], *[e for e, _ in extras], *deps)
    return outs


def _tn_rows(m):
    return max(b for b in range(128, min(m, 1408) + 1, 128) if m % b == 0)


def _mm_tn(name, a, b, scale=1.0, bm=None, tk=1024, deps=(), out_dtype=F32):
    rows, m = a.shape
    n = b.shape[1]
    bm = _tn_rows(m) if bm is None else bm
    tk = min(tk, rows)
    nk = rows // tk

    def body(a_ref, b_ref, *rest):
        o_ref, acc_ref = rest[-2:]
        k = pl.program_id(1)

        @pl.when(k == 0)
        def _():
            acc_ref[...] = jnp.zeros_like(acc_ref)

        acc_ref[...] += _dot_tn(a_ref[...].astype(BF16), b_ref[...].astype(BF16))

        @pl.when(k == nk - 1)
        def _():
            o_ref[...] = (acc_ref[...] * scale).astype(o_ref.dtype)

    return pl.pallas_call(
        body, name=name, grid=(m // bm, nk),
        in_specs=[pl.BlockSpec((tk, bm), lambda i, k: (k, i)), pl.BlockSpec((tk, n), lambda i, k: (k, 0))]
        + [_ANY_SPEC] * len(deps),
        out_specs=pl.BlockSpec((bm, n), lambda i, k: (i, 0)),
        out_shape=jax.ShapeDtypeStruct((m, n), out_dtype),
        scratch_shapes=[pltpu.VMEM((bm, n), F32)],
        compiler_params=_cparams("parallel", "arbitrary"),
    )(a, b, *deps)


def _mm_tn_stack(name, a_list, b, tk=1024, out_dtype=F32):
    rows, n = b.shape
    ms = [a.shape[1] for a in a_list]
    tk = min(tk, rows)
    nk = rows // tk
    na = len(a_list)

    def body(*refs):
        a_refs, b_ref, o_ref, acc_ref = refs[:na], refs[na], refs[na + 1], refs[na + 2]
        k = pl.program_id(0)

        @pl.when(k == 0)
        def _():
            acc_ref[...] = jnp.zeros_like(acc_ref)

        bv = b_ref[...].astype(BF16)
        r0 = 0
        for a_ref, m in zip(a_refs, ms):
            acc_ref[r0:r0 + m, :] += _dot_tn(a_ref[...].astype(BF16), bv)
            r0 += m

        @pl.when(k == nk - 1)
        def _():
            o_ref[...] = acc_ref[...].astype(o_ref.dtype)

    return pl.pallas_call(
        body, name=name, grid=(nk,),
        in_specs=[pl.BlockSpec((tk, m), lambda k: (k, 0)) for m in ms] + [pl.BlockSpec((tk, n), lambda k: (k, 0))],
        out_specs=pl.BlockSpec((sum(ms), n), lambda k: (0, 0)),
        out_shape=jax.ShapeDtypeStruct((sum(ms), n), out_dtype),
        scratch_shapes=[pltpu.VMEM((sum(ms), n), F32)],
        compiler_params=_cparams("arbitrary"),
    )(*a_list, b)


def _colsum(name, xs, tm=512):
    rows, cols = xs.shape
    tm = min(tm, rows)

    def body(x_ref, o_ref):
        @pl.when(pl.program_id(0) == 0)
        def _():
            o_ref[...] = jnp.zeros_like(o_ref)

        o_ref[...] += jnp.sum(x_ref[...].astype(F32), axis=0, keepdims=True)

    return pl.pallas_call(
        body, name=name, grid=(rows // tm,),
        in_specs=[pl.BlockSpec((tm, cols), lambda i: (i, 0))],
        out_specs=pl.BlockSpec((1, cols), lambda i: (0, 0)),
        out_shape=jax.ShapeDtypeStruct((1, cols), F32),
        compiler_params=_cparams("arbitrary"),
    )(xs)


def _ew(name, fn, ins, out_cols, out_dtypes, tm=512):
    rows = ins[0].shape[0]
    tm = min(tm, rows)
    ni = len(ins)

    def body(*refs):
        outs = fn(*[r[...] for r in refs[:ni]])
        for o_ref, o in zip(refs[ni:], outs):
            o_ref[...] = o.astype(o_ref.dtype)

    def spec(shape):
        if shape[0] == 1:
            return pl.BlockSpec((1, shape[1]), lambda i: (0, 0))
        return pl.BlockSpec((tm, shape[1]), lambda i: (i, 0))

    return pl.pallas_call(
        body, name=name, grid=(rows // tm,),
        in_specs=[spec(a.shape) for a in ins],
        out_specs=[pl.BlockSpec((tm, c), lambda i: (i, 0)) for c in out_cols],
        out_shape=[jax.ShapeDtypeStruct((rows, c), dt) for c, dt in zip(out_cols, out_dtypes)],
        compiler_params=_cparams("parallel"),
    )(*ins)


def _rms_parts(xv):
    r = lax.rsqrt(jnp.mean(xv * xv, axis=-1, keepdims=True) + EPS)
    return r, xv * r


def _rms_bwd_dx(dh, gain, r, xh):
    dxh = dh * gain
    return r * (dxh - xh * jnp.mean(dxh * xh, axis=-1, keepdims=True))


def _ffn_chunks(f_all):
    return [slice(c, min(c + FFN_CHUNK, f_all)) for c in range(0, f_all, FFN_CHUNK)]


def _loss_head(xo, gain_f, target, d):
    r, xh = _rms_parts(xo)
    err = xh * gain_f - target
    dy = err * (1.0 / d)
    per_tok = jnp.mean(err * err, axis=-1, keepdims=True)
    return (_rms_bwd_dx(dy, gain_f, r, xh), jnp.sum(dy * xh, axis=0, keepdims=True),
            0.5 * jnp.sum(per_tok, axis=0, keepdims=True))


def _ffn_tile(x_ref, g_ref, wg_ref, wu_ref, wd_ref, h_ref, gg_ref, uu_ref):
    xv = x_ref[...]
    _, xh = _rms_parts(xv)
    h = (xh * g_ref[...]).astype(BF16)
    h_ref[...] = h
    acc = None
    for cols in _ffn_chunks(wd_ref.shape[0]):
        gg = _dot_nt(h, wg_ref[cols, :])
        uu = _dot_nt(h, wu_ref[cols, :])
        act = gg * _sigmoid(gg) * uu
        part = _dot_nn(act.astype(BF16), wd_ref[cols, :])
        acc = part if acc is None else acc + part
        gg_ref[:, cols] = gg.astype(BF16)
        uu_ref[:, cols] = uu.astype(BF16)
    return xv + 0.5 * acc


def _ffn_fwd(name, xs, gain, wg_t, wu_t, wd, next_gain, tm=512, deps=()):
    rows, d = xs.shape
    f_all = wd.shape[0]
    tm = min(tm, rows)

    def body(x_ref, g_ref, wg_ref, wu_ref, wd_ref, ng_ref, *rest):
        xo_ref, h_ref, gg_ref, uu_ref, hn_ref = rest[-5:]
        xo = _ffn_tile(x_ref, g_ref, wg_ref, wu_ref, wd_ref, h_ref, gg_ref, uu_ref)
        xo_ref[...] = xo
        hn_ref[...] = (_rms_parts(xo)[1] * ng_ref[...]).astype(BF16)

    tile = pl.BlockSpec((tm, d), lambda i: (i, 0))
    row = pl.BlockSpec((1, d), lambda i: (0, 0))
    wspec = pl.BlockSpec((f_all, d), lambda i: (0, 0), pipeline_mode=pl.Buffered(1))
    hid = pl.BlockSpec((tm, f_all), lambda i: (i, 0))
    return pl.pallas_call(
        body, name=name, grid=(rows // tm,),
        in_specs=[tile, row, wspec, wspec, wspec, row] + [_ANY_SPEC] * len(deps),
        out_specs=[tile, tile, hid, hid, tile],
        out_shape=[jax.ShapeDtypeStruct((rows, d), F32), jax.ShapeDtypeStruct((rows, d), BF16),
                   jax.ShapeDtypeStruct((rows, f_all), BF16), jax.ShapeDtypeStruct((rows, f_all), BF16),
                   jax.ShapeDtypeStruct((rows, d), BF16)],
        compiler_params=_cparams("parallel"),
    )(xs, gain, wg_t, wu_t, wd, next_gain, *deps)


def _ffn_fwd_head(name, xs, gain, wg_t, wu_t, wd, gain_f, target, tm=512):
    rows, d = xs.shape
    f_all = wd.shape[0]
    tm = min(tm, rows)

    def body(x_ref, g_ref, wg_ref, wu_ref, wd_ref, gf_ref, t_ref, dxo_ref, h_ref, gg_ref, uu_ref, dgf_ref, loss_ref):
        xo = _ffn_tile(x_ref, g_ref, wg_ref, wu_ref, wd_ref, h_ref, gg_ref, uu_ref)
        dxo, dgf, loss = _loss_head(xo, gf_ref[...], t_ref[...], d)
        dxo_ref[...] = dxo

        @pl.when(pl.program_id(0) == 0)
        def _():
            dgf_ref[...] = jnp.zeros_like(dgf_ref)
            loss_ref[...] = jnp.zeros_like(loss_ref)

        dgf_ref[...] += dgf
        loss_ref[...] += loss

    tile = pl.BlockSpec((tm, d), lambda i: (i, 0))
    row = pl.BlockSpec((1, d), lambda i: (0, 0))
    wspec = pl.BlockSpec((f_all, d), lambda i: (0, 0), pipeline_mode=pl.Buffered(1))
    hid = pl.BlockSpec((tm, f_all), lambda i: (i, 0))
    return pl.pallas_call(
        body, name=name, grid=(rows // tm,),
        in_specs=[tile, row, wspec, wspec, wspec, row, tile],
        out_specs=[tile, tile, hid, hid, row, pl.BlockSpec((1, 1), lambda i: (0, 0))],
        out_shape=[jax.ShapeDtypeStruct((rows, d), F32), jax.ShapeDtypeStruct((rows, d), BF16),
                   jax.ShapeDtypeStruct((rows, f_all), BF16), jax.ShapeDtypeStruct((rows, f_all), BF16),
                   jax.ShapeDtypeStruct((1, d), F32), jax.ShapeDtypeStruct((1, 1), F32)],
        compiler_params=_cparams("arbitrary"),
    )(xs, gain, wg_t, wu_t, wd, gain_f, target)


def _ffn_bwd(name, dxo, xs, gain, gg_all, uu_all, wg_t, wu_t, wd, tm=256):
    rows, d = xs.shape
    f_all = wd.shape[0]
    tm = min(tm, rows)

    def body(dxo_ref, x_ref, g_ref, gg_ref, uu_ref, wg_ref, wu_ref, wd_ref,
             dx_ref, dgg_ref, duu_ref, act_ref, dgain_ref):
        dxo = dxo_ref[...]
        df = (0.5 * dxo).astype(BF16)
        dh = None
        for cols in _ffn_chunks(f_all):
            gg = gg_ref[:, cols].astype(F32)
            uu = uu_ref[:, cols].astype(F32)
            sg = _sigmoid(gg)
            silu = gg * sg
            dact = _dot_nt(df, wd_ref[cols, :])
            duu = (dact * silu).astype(BF16)
            dgg = (dact * uu * (sg * (1.0 + gg * (1.0 - sg)))).astype(BF16)
            act_ref[:, cols] = (silu * uu).astype(BF16)
            dgg_ref[:, cols] = dgg
            duu_ref[:, cols] = duu
            part = _dot_nn(dgg, wg_ref[cols, :]) + _dot_nn(duu, wu_ref[cols, :])
            dh = part if dh is None else dh + part
        r, xh = _rms_parts(x_ref[...])
        dx_ref[...] = dxo + _rms_bwd_dx(dh, g_ref[...], r, xh)

        @pl.when(pl.program_id(0) == 0)
        def _():
            dgain_ref[...] = jnp.zeros_like(dgain_ref)

        dgain_ref[...] += jnp.sum(dh * xh, axis=0, keepdims=True)

    tile = pl.BlockSpec((tm, d), lambda i: (i, 0))
    row = pl.BlockSpec((1, d), lambda i: (0, 0))
    wspec = pl.BlockSpec((f_all, d), lambda i: (0, 0), pipeline_mode=pl.Buffered(1))
    hid = pl.BlockSpec((tm, f_all), lambda i: (i, 0))
    hid_shape = jax.ShapeDtypeStruct((rows, f_all), BF16)
    return pl.pallas_call(
        body, name=name, grid=(rows // tm,),
        in_specs=[tile, tile, row, hid, hid, wspec, wspec, wspec],
        out_specs=[tile, hid, hid, hid, row],
        out_shape=[jax.ShapeDtypeStruct((rows, d), F32), hid_shape, hid_shape, hid_shape,
                   jax.ShapeDtypeStruct((1, d), F32)],
        compiler_params=_cparams("arbitrary"),
    )(dxo, xs, gain, gg_all, uu_all, wg_t, wu_t, wd)


def _t5_bucket_np(dist):
    max_exact = N_BUCKETS // 2
    dd = np.maximum(dist, 1).astype(np.float32)
    large = max_exact + (np.log(dd / np.float32(max_exact)) / np.float32(math.log(MAX_DISTANCE / max_exact))
                         * np.float32(N_BUCKETS - max_exact)).astype(np.int32)
    large = np.minimum(large, N_BUCKETS - 1)
    return np.where(dist < max_exact, dist, large).astype(np.int32)


def _attn_geometry(g, rows):
    run = rows // 16
    dil = DILATIONS[g]
    if dil == 16:
        bq = BLOCK
        return dict(view=(16, run), block=(None, bq), grid=(16, run // bq), index=lambda r, n: (r, n),
                    pos=np.arange(bq), bq=bq)
    if dil == 4:
        per = BLOCK // 4
        pos = (4 * np.arange(per)[None, :] + np.arange(4)[:, None]).reshape(-1)
        return dict(view=(4, 4, run), block=(4, None, per), grid=(4, run // per), index=lambda r, n: (0, r, n),
                    pos=pos, bq=BLOCK)
    per = 16
    pos = (16 * np.arange(per)[None, :] + np.arange(16)[:, None]).reshape(-1)
    return dict(view=(16, run), block=(16, per), grid=(1, run // per), index=lambda r, n: (0, n),
                pos=pos, bq=16 * per)


def _attn_tables(g, rows):
    geo = _attn_geometry(g, rows)
    pos, bq = geo["pos"], geo["bq"]
    steps = pos[:, None] - np.concatenate([pos - bq, pos])[None, :]
    valid = (steps >= 0) & (steps <= BLOCK)
    bucket = _t5_bucket_np((np.maximum(steps, 0) * DILATIONS[g]).astype(np.int32))
    return bucket, valid.astype(np.int32)


def _bias_fwd(name, bucket, valid, table_t):
    bq = bucket.shape[0]

    def body(bk_ref, ok_ref, tab_ref, o_ref):
        bk = bk_ref[...]
        ok = ok_ref[...] > 0
        accs = [jnp.zeros(bk.shape, F32)] * HEADS_PER_GROUP
        for b in range(N_BUCKETS):
            hit = bk == b
            accs = [jnp.where(hit, tab_ref[h, b], acc) for h, acc in enumerate(accs)]
        for h, acc in enumerate(accs):
            o_ref[h] = jnp.where(ok, acc, NEG_INF)

    vm = pl.BlockSpec(memory_space=pltpu.VMEM)
    return pl.pallas_call(
        body, name=name, in_specs=[vm, vm, pl.BlockSpec(memory_space=pltpu.SMEM)], out_specs=vm,
        out_shape=jax.ShapeDtypeStruct((HEADS_PER_GROUP, bq, 2 * bq), F32),
    )(bucket, valid, table_t)


def _bias_bwd(name, bucket, dbias):
    def body(bk_ref, db_ref, o_ref):
        row_id = lax.broadcasted_iota(jnp.int32, (N_BUCKETS, 128), 0)
        col_id = lax.broadcasted_iota(jnp.int32, (N_BUCKETS, 128), 1)
        bk = bk_ref[...]
        acc = jnp.zeros((N_BUCKETS, 128), F32)
        for h in range(HEADS_PER_GROUP):
            db = db_ref[h]
            for b in range(N_BUCKETS):
                part = jnp.sum(jnp.where(bk == b, db, 0.0), axis=0, keepdims=True)
                tot = jnp.sum(part, axis=1, keepdims=True)
                acc = jnp.where((row_id == b) & (col_id == h), tot, acc)
        o_ref[...] = acc

    vm = pl.BlockSpec(memory_space=pltpu.VMEM)
    return pl.pallas_call(body, name=name, in_specs=[vm, vm], out_specs=vm,
                          out_shape=jax.ShapeDtypeStruct((N_BUCKETS, 128), F32))(bucket, dbias)


def _head_of_lane(nrows):
    return lax.broadcasted_iota(jnp.int32, (nrows, ATTN_OUT), 1) // HEAD_DIM


def _stack_heads(a, lane_head):
    zero = jnp.zeros_like(a)
    return jnp.concatenate([jnp.where(lane_head == h, a, zero) for h in range(HEADS_PER_GROUP)], axis=0)


def _unstack_heads(a4, lane_head, bq):
    out = a4[:bq]
    for h in range(1, HEADS_PER_GROUP):
        out = jnp.where(lane_head == h, a4[h * bq:(h + 1) * bq], out)
    return out


def _attn_specs(geo, cols, col_block, index):
    return pl.BlockSpec(geo["block"] + (cols,), lambda r, n: index(r, n) + (col_block,))


def _attn_fwd(name, qkv, g, bias4):
    rows = qkv.shape[0]
    geo = _attn_geometry(g, rows)
    bq, (nsub, nb), index = geo["bq"], geo["grid"], geo["index"]
    blk_shape = tuple(b for b in geo["block"] if b is not None) + (ATTN_OUT,)

    def body(q_ref, kc_ref, kp_ref, vc_ref, vp_ref, b_ref, o_ref, lse_ref):
        n = pl.program_id(1)
        lane_head = _head_of_lane(bq)
        flat = lambda ref: ref[...].reshape(bq, ATTN_OUT)
        q4 = _stack_heads(flat(q_ref), lane_head)
        k2 = jnp.concatenate([flat(kp_ref), flat(kc_ref)], axis=0)
        v2 = jnp.concatenate([flat(vp_ref), flat(vc_ref)], axis=0)
        s = _dot_nt(q4, k2) + b_ref[...]
        col = lax.broadcasted_iota(jnp.int32, s.shape, 1)
        s = jnp.where((col >= bq) | (n > 0), s, NEG_INF)
        mx = jnp.max(s, axis=-1, keepdims=True)
        p = jnp.exp(s - mx)
        den = jnp.sum(p, axis=-1, keepdims=True)
        o4 = _dot_nn(p.astype(BF16), v2) / den
        lse4 = jnp.broadcast_to(mx + jnp.log(den), (HEADS_PER_GROUP * bq, ATTN_OUT))
        o_ref[...] = _unstack_heads(o4, lane_head, bq).reshape(blk_shape)
        lse_ref[...] = _unstack_heads(lse4, lane_head, bq).reshape(blk_shape)

    prev = lambda r, n: index(r, jnp.maximum(n - 1, 0))
    view = lambda a: a.reshape(geo["view"] + (a.shape[1],))
    qkv_v = view(qkv)
    out_spec = _attn_specs(geo, ATTN_OUT, 0, index)
    out_shape = jax.ShapeDtypeStruct(geo["view"] + (ATTN_OUT,), F32)
    o, lse = pl.pallas_call(
        body, name=name, grid=(nsub, nb),
        in_specs=[_attn_specs(geo, ATTN_OUT, g, index), _attn_specs(geo, ATTN_OUT, 3 + g, index),
                  _attn_specs(geo, ATTN_OUT, 3 + g, prev), _attn_specs(geo, ATTN_OUT, 6 + g, index),
                  _attn_specs(geo, ATTN_OUT, 6 + g, prev), pl.BlockSpec(bias4.shape, lambda r, n: (0, 0))],
        out_specs=[out_spec, out_spec], out_shape=[out_shape, out_shape],
        compiler_params=_cparams("parallel", "arbitrary"),
    )(qkv_v, qkv_v, qkv_v, qkv_v, qkv_v, bias4)
    return o.reshape(rows, ATTN_OUT), lse.reshape(rows, ATTN_OUT)


def _attn_bwd(name, qkv, do, lse, cvec, g, bias4):
    rows = qkv.shape[0]
    geo = _attn_geometry(g, rows)
    bq, (nsub, nb), index = geo["bq"], geo["grid"], geo["index"]
    blk_shape = tuple(b for b in geo["block"] if b is not None) + (ATTN_OUT,)
    nlead = len(blk_shape) - 1

    def body(q_ref, kc_ref, kp_ref, vc_ref, vp_ref, do_ref, lse_ref, c_ref, b_ref,
             dq_ref, dk_ref, dv_ref, db_ref, kcar_ref, vcar_ref):
        r, n = pl.program_id(0), pl.program_id(1)
        valid = n < nb
        lane_head = _head_of_lane(bq)
        flat = lambda ref: ref[...].reshape(bq, ATTN_OUT)

        @pl.when((r == 0) & (n == 0))
        def _():
            kcar_ref[...] = jnp.zeros_like(kcar_ref)
            vcar_ref[...] = jnp.zeros_like(vcar_ref)
            db_ref[...] = jnp.zeros_like(db_ref)

        def column(ref, h):
            lead = (slice(None),) * nlead
            return ref[lead + (pl.ds(h * HEAD_DIM, 1),)].reshape(bq, 1)

        q4 = _stack_heads(flat(q_ref), lane_head)
        do4 = _stack_heads(flat(do_ref), lane_head)
        k2 = jnp.concatenate([flat(kp_ref), flat(kc_ref)], axis=0)
        v2 = jnp.concatenate([flat(vp_ref), flat(vc_ref)], axis=0)
        lse4 = jnp.concatenate([column(lse_ref, h) for h in range(HEADS_PER_GROUP)], axis=0)
        c4 = jnp.concatenate([column(c_ref, h) for h in range(HEADS_PER_GROUP)], axis=0)
        s = _dot_nt(q4, k2) + b_ref[...]
        col = lax.broadcasted_iota(jnp.int32, s.shape, 1)
        keep = ((col >= bq) | (n > 0)) & valid
        p = jnp.where(keep, jnp.exp(s - lse4), 0.0)
        ds = p * (_dot_nt(do4, v2) + c4)
        ds_b = ds.astype(BF16)

        @pl.when(valid)
        def _():
            dq = _unstack_heads(_dot_nn(ds_b, k2), lane_head, bq) * (HEAD_DIM ** -0.5)
            dq_ref[...] = dq.astype(BF16).reshape(blk_shape)

        dk2 = _dot_tn(ds_b, q4)
        dv2 = _dot_tn(p.astype(BF16), do4)
        dk_ref[...] = (kcar_ref[...] + dk2[:bq]).astype(BF16).reshape(blk_shape)
        dv_ref[...] = (vcar_ref[...] + dv2[:bq]).astype(BF16).reshape(blk_shape)
        kcar_ref[...] = dk2[bq:]
        vcar_ref[...] = dv2[bq:]
        db_ref[...] += ds

    cur = lambda r, n: index(r, jnp.minimum(n, nb - 1))
    prev = lambda r, n: index(r, jnp.maximum(jnp.minimum(n, nb - 1) - 1, 0))
    late = lambda r, n: index(r, jnp.maximum(n - 1, 0))
    view = lambda a: a.reshape(geo["view"] + (a.shape[1],))
    qkv_v = view(qkv)
    tile = _attn_specs(geo, ATTN_OUT, 0, cur)
    bias_spec = pl.BlockSpec(bias4.shape, lambda r, n: (0, 0))
    out_shape = jax.ShapeDtypeStruct(geo["view"] + (ATTN_OUT,), BF16)
    dq, dk, dv, db = pl.pallas_call(
        body, name=name, grid=(nsub, nb + 1),
        in_specs=[_attn_specs(geo, ATTN_OUT, g, cur), _attn_specs(geo, ATTN_OUT, 3 + g, cur),
                  _attn_specs(geo, ATTN_OUT, 3 + g, prev), _attn_specs(geo, ATTN_OUT, 6 + g, cur),
                  _attn_specs(geo, ATTN_OUT, 6 + g, prev), tile, tile, tile, bias_spec],
        out_specs=[tile, _attn_specs(geo, ATTN_OUT, 0, late), _attn_specs(geo, ATTN_OUT, 0, late), bias_spec],
        out_shape=[out_shape, out_shape, out_shape, jax.ShapeDtypeStruct(bias4.shape, F32)],
        scratch_shapes=[pltpu.VMEM((bq, ATTN_OUT), F32), pltpu.VMEM((bq, ATTN_OUT), F32)],
        compiler_params=_cparams("arbitrary", "arbitrary"),
    )(qkv_v, qkv_v, qkv_v, qkv_v, qkv_v, view(do), view(lse), view(cvec), bias4)
    return dq.reshape(rows, ATTN_OUT), dk.reshape(rows, ATTN_OUT), dv.reshape(rows, ATTN_OUT), db


def _group_weights(lses):
    mx = jnp.maximum(jnp.maximum(lses[0], lses[1]), lses[2])
    es = [jnp.exp(l - mx) for l in lses]
    den = es[0] + es[1] + es[2]
    return [e / den for e in es]


def _combine_fwd(name, os_, lses):
    def fn(o0, o1, o2, l0, l1, l2):
        ws = _group_weights([l0, l1, l2])
        out = ws[0] * o0 + ws[1] * o1 + ws[2] * o2
        return out, out

    return _ew(name, fn, [*os_, *lses], [ATTN_OUT, ATTN_OUT], [F32, BF16], tm=1024)


def _combine_bwd(name, do, oa, lses):
    def fn(dov, oav, l0, l1, l2):
        head_sum = (lax.broadcasted_iota(jnp.int32, (ATTN_OUT, ATTN_OUT), 0) // HEAD_DIM
                    == lax.broadcasted_iota(jnp.int32, (ATTN_OUT, ATTN_OUT), 1) // HEAD_DIM)
        ws = _group_weights([l0, l1, l2])
        prod = dov * oav
        hi = prod.astype(BF16)
        lo = (prod - hi.astype(F32)).astype(BF16)
        ones = jnp.where(head_sum, 1.0, 0.0).astype(BF16)
        bar = _dot_nn(hi, ones) + _dot_nn(lo, ones)
        return tuple(w * dov for w in ws) + tuple(-w * bar for w in ws)

    return _ew(name, fn, [do, oa, *lses], [ATTN_OUT] * 6, [BF16] * 3 + [F32] * 3, tm=1024)


def _ssm_disc(a_re, a_im, log_dt, b_re, b_im):
    dt = jnp.exp(log_dt)
    mag = jnp.exp(a_re * dt)
    ab_re = mag * jnp.cos(a_im * dt)
    ab_im = mag * jnp.sin(a_im * dt)
    den = a_re * a_re + a_im * a_im
    xr = ab_re - 1.0
    coef_re = (xr * a_re + ab_im * a_im) / den
    coef_im = (ab_im * a_re - xr * a_im) / den
    bb_re = coef_re[None] * b_re - coef_im[None] * b_im
    bb_im = coef_re[None] * b_im + coef_im[None] * b_re
    return ab_re, ab_im, bb_re, bb_im


def _ssm_params_fwd(name, a_re, a_im, log_dt, b_re, b_im, c_re, c_im):
    pows = jax.ShapeDtypeStruct((SCAN_STEPS,) + a_re.shape, F32)
    mats = jax.ShapeDtypeStruct((SSM_PAIRS, PAIR_TILE, PAIR_TILE), BF16)
    per_tile = PAIR_TILE // (2 * SSM_GROUP)

    def body(ar, ai, ld, br, bi, cr, ci, o_pr, o_pi, o_bb, o_c, bbr_ref, bbi_ref, wide_ref):
        ab_re, ab_im, bb_re, bb_im = _ssm_disc(ar[...], ai[...], ld[...], br[...], bi[...])
        pr, pi = ab_re, ab_im
        for j in range(SCAN_STEPS):
            o_pr[j] = pr
            o_pi[j] = pi
            pr, pi = pr * ab_re - pi * ab_im, pr * ab_im + pi * ab_re
        bbr_ref[...] = bb_re
        bbi_ref[...] = bb_im

        def place(out_ref, block):
            wide_ref[...] = jnp.zeros_like(wide_ref)
            for g in range(SSM_GROUPS):
                p, l = divmod(g, 2)
                rows = pl.ds((p % per_tile) * 2 * SSM_GROUP + l * SSM_GROUP, SSM_GROUP)
                re, im = block(g)
                wide_ref[p, rows, pl.ds(l * SSM_STATE, SSM_STATE)] = re
                wide_ref[p, rows, pl.ds(PAIR_LANES + l * SSM_STATE, SSM_STATE)] = im
            out_ref[...] = wide_ref[...].astype(BF16)

        place(o_bb, lambda g: (bbr_ref[:, g, :], bbi_ref[:, g, :]))
        place(o_c, lambda g: (cr[g], -ci[g]))

    vm = pl.BlockSpec(memory_space=pltpu.VMEM)
    return pl.pallas_call(
        body, name=name, in_specs=[vm] * 7, out_specs=[vm] * 4, out_shape=[pows, pows, mats, mats],
        scratch_shapes=[pltpu.VMEM(b_re.shape, F32), pltpu.VMEM(b_re.shape, F32),
                        pltpu.VMEM((SSM_PAIRS, PAIR_TILE, PAIR_TILE), F32)],
    )(a_re, a_im, log_dt, b_re, b_im, c_re, c_im)


def _ssm_params_bwd(name, a_re, a_im, log_dt, b_re, b_im, d_ab_re, d_ab_im, d_bb_re, d_bb_im):
    gn = jax.ShapeDtypeStruct(a_re.shape, F32)
    cgn = jax.ShapeDtypeStruct(b_re.shape, F32)

    def body(ar, ai, ld, br, bi, g0, g1, g2, g3, o_ar, o_ai, o_ld, o_br, o_bi):
        _, vjp = jax.vjp(_ssm_disc, ar[...], ai[...], ld[...], br[...], bi[...])
        outs = vjp((g0[...], g1[...], g2[...], g3[...]))
        for o_ref, o in zip((o_ar, o_ai, o_ld, o_br, o_bi), outs):
            o_ref[...] = o

    vm = pl.BlockSpec(memory_space=pltpu.VMEM)
    return pl.pallas_call(body, name=name, in_specs=[vm] * 9, out_specs=[vm] * 5,
                          out_shape=[gn, gn, jax.ShapeDtypeStruct(log_dt.shape, F32), cgn, cgn],
                          )(a_re, a_im, log_dt, b_re, b_im, d_ab_re, d_ab_im, d_bb_re, d_bb_im)


def _scan_block(s_ref, carry_ref, tmp_ref, pw_ref, reverse, sprev=None):
    nl = SSM_LANES
    halves = range(SCAN_COLS // SCAN_SUB)
    zero = jnp.zeros((SCAN_SUB, SCAN_LANES), F32)
    for half in (reversed(halves) if reverse else halves):
        sub_rows = pl.ds(half * SCAN_SUB, SCAN_SUB)
        for lc in range(nl // SCAN_LANES):
            re_l = pl.ds(lc * SCAN_LANES, SCAN_LANES)
            im_l = pl.ds(nl + lc * SCAN_LANES, SCAN_LANES)
            are, aim = pw_ref[0, :, re_l], pw_ref[0, :, im_l]

            def step_of(j):
                return SCAN_STEPS - 1 - j if reverse else j

            def pass1(j, st):
                sr, si = st
                jj = step_of(j)
                nr = are * sr - aim * si + s_ref[jj, sub_rows, re_l]
                ni = are * si + aim * sr + s_ref[jj, sub_rows, im_l]
                s_ref[jj, sub_rows, re_l] = nr
                s_ref[jj, sub_rows, im_l] = ni
                return nr, ni

            er, ei = lax.fori_loop(0, SCAN_STEPS, pass1, (zero, zero), unroll=2)
            tmp_ref[0:SCAN_SUB, re_l] = er
            tmp_ref[0:SCAN_SUB, im_l] = ei
            apr, api = pw_ref[SCAN_STEPS - 1, 0:1, re_l], pw_ref[SCAN_STEPS - 1, 0:1, im_l]
            sr, si = carry_ref[0:1, re_l], carry_ref[0:1, im_l]
            for step in range(SCAN_SUB):
                c = SCAN_SUB - 1 - step if reverse else step
                tmp_ref[SCAN_SUB + c:SCAN_SUB + c + 1, re_l] = sr
                tmp_ref[SCAN_SUB + c:SCAN_SUB + c + 1, im_l] = si
                e_r, e_i = tmp_ref[c:c + 1, re_l], tmp_ref[c:c + 1, im_l]
                sr, si = apr * sr - api * si + e_r, apr * si + api * sr + e_i
            carry_ref[0:1, re_l] = sr
            carry_ref[0:1, im_l] = si
            cr = tmp_ref[SCAN_SUB:2 * SCAN_SUB, re_l]
            ci = tmp_ref[SCAN_SUB:2 * SCAN_SUB, im_l]

            if sprev is None:
                def pass2(j, st):
                    pr, pi = pw_ref[j, :, re_l], pw_ref[j, :, im_l]
                    jj = step_of(j)
                    s_ref[jj, sub_rows, re_l] += pr * cr - pi * ci
                    s_ref[jj, sub_rows, im_l] += pr * ci + pi * cr
                    return st

                lax.fori_loop(0, SCAN_STEPS, pass2, 0, unroll=2)
            else:
                st_ref, prev_ref, have_prev, dab_ref = sprev

                def corrected(jj, pr, pi):
                    gr = s_ref[jj, sub_rows, re_l] + pr * cr - pi * ci
                    gi = s_ref[jj, sub_rows, im_l] + pr * ci + pi * cr
                    s_ref[jj, sub_rows, re_l] = gr
                    s_ref[jj, sub_rows, im_l] = gi
                    return gr, gi

                def pass2(j, st):
                    dr, di = st
                    jj = SCAN_STEPS - 1 - j
                    gr, gi = corrected(jj, pw_ref[j, :, re_l], pw_ref[j, :, im_l])
                    qr, qi = st_ref[jj - 1, sub_rows, re_l], st_ref[jj - 1, sub_rows, im_l]
                    return dr + gr * qr + gi * qi, di + gi * qr - gr * qi

                dr, di = lax.fori_loop(0, SCAN_STEPS - 1, pass2, (zero, zero), unroll=2)
                gr, gi = corrected(0, pw_ref[SCAN_STEPS - 1, :, re_l], pw_ref[SCAN_STEPS - 1, :, im_l])
                sub = lax.broadcasted_iota(jnp.int32, (SCAN_SUB, SCAN_LANES), 0)
                if half == 0:
                    pv_r = prev_ref[SCAN_SUB - 1:SCAN_SUB, re_l] * have_prev
                    pv_i = prev_ref[SCAN_SUB - 1:SCAN_SUB, im_l] * have_prev
                else:
                    before = pl.ds(half * SCAN_SUB - 1, 1)
                    pv_r, pv_i = st_ref[SCAN_STEPS - 1, before, re_l], st_ref[SCAN_STEPS - 1, before, im_l]
                shape = (SCAN_SUB, SCAN_LANES)
                qr = jnp.where(sub == 0, jnp.broadcast_to(pv_r, shape),
                               pltpu.roll(st_ref[SCAN_STEPS - 1, sub_rows, re_l], 1, 0))
                qi = jnp.where(sub == 0, jnp.broadcast_to(pv_i, shape),
                               pltpu.roll(st_ref[SCAN_STEPS - 1, sub_rows, im_l], 1, 0))
                dab_ref[:, re_l] += dr + gr * qr + gi * qi
                dab_ref[:, im_l] += di + gi * qr - gr * qi


def _scan_view(a):
    return a.reshape(16, a.shape[0] // 16, a.shape[1])


def _pair_tile(p):
    start = (p * 2 * SSM_GROUP // PAIR_TILE) * PAIR_TILE
    return slice(start, start + PAIR_TILE)


def _pair_lanes(p):
    return pl.ds(p * PAIR_LANES, PAIR_LANES), pl.ds(SSM_LANES + p * PAIR_LANES, PAIR_LANES)


def _pair_store(s_ref, p, val):
    re_l, im_l = _pair_lanes(p)
    s_ref[:, :, re_l] = val[:, :PAIR_LANES].reshape(16, SCAN_COLS, PAIR_LANES)
    s_ref[:, :, im_l] = val[:, PAIR_LANES:].reshape(16, SCAN_COLS, PAIR_LANES)


def _pair_load(s_ref, p):
    re_l, im_l = _pair_lanes(p)
    parts = [s_ref[:, :, l].reshape(SCAN_BLOCK, PAIR_LANES) for l in (re_l, im_l)]
    return jnp.concatenate(parts, axis=1).astype(BF16)


def _pair_sum(fn):
    per = PAIR_TILE // (2 * SSM_GROUP)
    tiles = []
    for t in range(SSM_PAIRS // per):
        acc = None
        for p in range(t * per, (t + 1) * per):
            part = fn(p)
            acc = part if acc is None else acc + part
        tiles.append(acc)
    return jnp.concatenate(tiles, axis=1)


def _ssm_fwd(name, u, bb_mats, c_mats, pw_rows, d_skip):
    rows = u.shape[0]
    nl2 = 2 * SSM_LANES
    nblk = rows // SCAN_BLOCK

    def body(u_ref, bb_ref, c_ref, pw_ref, d_ref, y_ref, yg_ref, s_ref, carry_ref, tmp_ref):
        @pl.when(pl.program_id(0) == 0)
        def _():
            carry_ref[...] = jnp.zeros_like(carry_ref)

        uv = u_ref[...].reshape(SCAN_BLOCK, SSM_WIDTH)
        ub = uv.astype(BF16)
        for p in range(SSM_PAIRS):
            _pair_store(s_ref, p, _dot_nn(ub[:, _pair_tile(p)], bb_ref[p]))
        _scan_block(s_ref, carry_ref, tmp_ref, pw_ref, reverse=False)
        ys = _pair_sum(lambda p: _dot_nt(_pair_load(s_ref, p), c_ref[p]))
        yv = ys + d_ref[...] * uv
        y_ref[...] = yv.reshape(16, SCAN_COLS, SSM_WIDTH)
        yg_ref[...] = jax.nn.gelu(yv).astype(BF16).reshape(16, SCAN_COLS, SSM_WIDTH)

    const = lambda shape: pl.BlockSpec(shape, lambda i: (0,) * len(shape))
    blk = lambda cols: pl.BlockSpec((16, SCAN_COLS, cols), lambda i: (0, i, 0))
    pair_mats = const((SSM_PAIRS, PAIR_TILE, PAIR_TILE))
    y, yg, s = pl.pallas_call(
        body, name=name, grid=(nblk,),
        in_specs=[blk(SSM_WIDTH), pair_mats, pair_mats, const((SCAN_STEPS, SCAN_SUB, nl2)), const((1, SSM_WIDTH))],
        out_specs=[blk(SSM_WIDTH), blk(SSM_WIDTH), blk(nl2)],
        out_shape=[jax.ShapeDtypeStruct((16, rows // 16, SSM_WIDTH), F32),
                   jax.ShapeDtypeStruct((16, rows // 16, SSM_WIDTH), BF16),
                   jax.ShapeDtypeStruct((16, rows // 16, nl2), F32)],
        scratch_shapes=[pltpu.VMEM((SCAN_SUB, nl2), F32), pltpu.VMEM((2 * SCAN_SUB, nl2), F32)],
        compiler_params=_cparams("arbitrary"),
    )(_scan_view(u), bb_mats, c_mats, pw_rows, d_skip)
    return y.reshape(rows, SSM_WIDTH), yg.reshape(rows, SSM_WIDTH), s.reshape(rows, nl2)


def _ssm_bwd(name, dy, u, states, bb_mats, c_mats, pwc_rows, d_skip):
    rows = u.shape[0]
    nl2 = 2 * SSM_LANES
    nblk = rows // SCAN_BLOCK

    def body(dy_ref, u_ref, st_ref, prev_ref, bb_ref, c_ref, pw_ref, d_ref,
             du_ref, dbb_ref, dc_ref, dab_ref, dd_ref, g_ref, carry_ref, tmp_ref):
        i = pl.program_id(0)

        @pl.when(i == 0)
        def _():
            carry_ref[...] = jnp.zeros_like(carry_ref)
            for ref in (dbb_ref, dc_ref, dab_ref, dd_ref):
                ref[...] = jnp.zeros_like(ref)

        dyv = dy_ref[...].reshape(SCAN_BLOCK, SSM_WIDTH)
        uv = u_ref[...].reshape(SCAN_BLOCK, SSM_WIDTH)
        dyb, ub = dyv.astype(BF16), uv.astype(BF16)
        for p in range(SSM_PAIRS):
            _pair_store(g_ref, p, _dot_nn(dyb[:, _pair_tile(p)], c_ref[p]))
        have_prev = (i < nblk - 1).astype(F32)
        _scan_block(g_ref, carry_ref, tmp_ref, pw_ref, reverse=True,
                    sprev=(st_ref, prev_ref, have_prev, dab_ref))

        def pair_work(p):
            gp = _pair_load(g_ref, p)
            dbb_ref[p] += _dot_tn(ub[:, _pair_tile(p)], gp)
            dc_ref[p] += _dot_tn(dyb[:, _pair_tile(p)], _pair_load(st_ref, p))
            return _dot_nt(gp, bb_ref[p])

        du_ref[...] = (_pair_sum(pair_work) + d_ref[...] * dyv).reshape(16, SCAN_COLS, SSM_WIDTH)
        dd_ref[...] += jnp.sum(dyv * uv, axis=0, keepdims=True)

    const = lambda shape: pl.BlockSpec(shape, lambda i: (0,) * len(shape))
    blk = lambda cols: pl.BlockSpec((16, SCAN_COLS, cols), lambda i: (0, nblk - 1 - i, 0))
    per8 = SCAN_COLS // SCAN_SUB
    prev_spec = pl.BlockSpec((None, SCAN_SUB, nl2), lambda i: (15, jnp.maximum((nblk - 1 - i) * per8 - 1, 0), 0))
    pair_mats = const((SSM_PAIRS, PAIR_TILE, PAIR_TILE))
    pair_shape = jax.ShapeDtypeStruct((SSM_PAIRS, PAIR_TILE, PAIR_TILE), F32)
    sv = _scan_view(states)
    du, dbb, dc, dab, dd = pl.pallas_call(
        body, name=name, grid=(nblk,),
        in_specs=[blk(SSM_WIDTH), blk(SSM_WIDTH), blk(nl2), prev_spec, pair_mats, pair_mats,
                  const((SCAN_STEPS, SCAN_SUB, nl2)), const((1, SSM_WIDTH))],
        out_specs=[blk(SSM_WIDTH), pair_mats, pair_mats, const((SCAN_SUB, nl2)), const((1, SSM_WIDTH))],
        out_shape=[jax.ShapeDtypeStruct((16, rows // 16, SSM_WIDTH), F32), pair_shape, pair_shape,
                   jax.ShapeDtypeStruct((SCAN_SUB, nl2), F32), jax.ShapeDtypeStruct((1, SSM_WIDTH), F32)],
        scratch_shapes=[pltpu.VMEM((16, SCAN_COLS, nl2), F32), pltpu.VMEM((SCAN_SUB, nl2), F32),
                        pltpu.VMEM((2 * SCAN_SUB, nl2), F32)],
        compiler_params=_cparams("arbitrary"),
    )(_scan_view(dy), _scan_view(u), sv, sv, bb_mats, c_mats, pwc_rows, d_skip)
    return du.reshape(rows, SSM_WIDTH), dbb, dc, dab, dd


def _adamw(name, w, m, v, gparts, tr):
    rows, cols = w.shape

    def body(w_ref, m_ref, v_ref, g_ref, og_ref, od_ref, om_ref, ov_ref):
        g = g_ref[0].astype(F32)
        for i in range(1, N_DEV):
            g = g + g_ref[i].astype(F32)
        m_new = B1 * m_ref[...] + (1.0 - B1) * g
        v_new = B2 * v_ref[...] + (1.0 - B2) * (g * g)
        m_hat = m_new / (1.0 - B1 ** STEP)
        v_hat = v_new / (1.0 - B2 ** STEP)
        og_ref[...] = g
        od_ref[...] = -LR * (m_hat / (jnp.sqrt(v_hat) + ADAM_EPS) + WD * w_ref[...])
        om_ref[...] = m_new
        ov_ref[...] = v_new

    spec = pl.BlockSpec((tr, cols), lambda i: (i, 0))
    shape = jax.ShapeDtypeStruct((rows, cols), F32)
    return pl.pallas_call(
        body, name=name, grid=(rows // tr,),
        in_specs=[spec, spec, spec, pl.BlockSpec((N_DEV, tr, cols), lambda i: (0, i, 0))],
        out_specs=[spec] * 4, out_shape=[shape] * 4,
        compiler_params=_cparams("parallel"),
    )(w, m, v, gparts)


_SHARDED = (
    ("ffn1_w_gate", True, (352, 1024)), ("ffn1_w_up", True, (352, 1024)), ("ffn1_w_down", False, (352, 1024)),
    ("w_in", True, (608, 1024)), ("ssm_w_glu", True, (128, 512)), ("w_attn_branch", True, (128, 256)),
    ("w_ssm_branch", True, (128, 512)), ("w_out", False, (128, 1024)),
    ("ffn2_w_gate", True, (352, 1024)), ("ffn2_w_up", True, (352, 1024)), ("ffn2_w_down", False, (352, 1024)),
)
_SMALL = ("ffn1_norm", "mix_norm", "gate_bias", "rel_bias_table", "ssm_a_re", "ssm_a_im", "ssm_log_dt",
          "ssm_b_re", "ssm_b_im", "ssm_c_re", "ssm_c_im", "ssm_d", "ffn2_norm", "final_norm")
_ORDER = ("ffn1_norm", "ffn1_w_gate", "ffn1_w_up", "ffn1_w_down", "mix_norm", "w_in", "gate_bias",
          "rel_bias_table", "ssm_a_re", "ssm_a_im", "ssm_log_dt", "ssm_b_re", "ssm_b_im", "ssm_c_re",
          "ssm_c_im", "ssm_d", "ssm_w_glu", "w_attn_branch", "w_ssm_branch", "w_out", "ffn2_norm",
          "ffn2_w_gate", "ffn2_w_up", "ffn2_w_down", "final_norm")


def _pack_rows(shape):
    return shape[0] * shape[1] // D_MODEL


_SHARD_INFO = {nm: (tr, shape) for nm, tr, shape in _SHARDED}
_PHASES = {
    "f1gu": ("ffn1_w_gate", "ffn1_w_up"), "f1d": ("ffn1_w_down",),
    "mix": ("w_in", "ssm_w_glu", "w_attn_branch", "w_ssm_branch", "w_out"),
    "f2": ("ffn2_w_gate", "ffn2_w_up", "ffn2_w_down"),
}


def _to_rows(a, nm):
    tr, shape = _SHARD_INFO[nm]
    return (a.T if tr else a).reshape(_pack_rows(shape), D_MODEL)


def _from_rows(p, nm):
    tr, shape = _SHARD_INFO[nm]
    a = p.reshape(shape)
    return a.T if tr else a


def _full_weight(gathered, nm):
    _, shape = _SHARD_INFO[nm]
    return gathered.reshape(N_DEV * shape[0], shape[1])


def _grad_blocks(g, nm):
    _, shape = _SHARD_INFO[nm]
    return g.astype(BF16).reshape(N_DEV, _pack_rows(shape), D_MODEL)


_SMALL_TILE = 8 * 128


def _small_rows(a):
    flat = a.reshape(-1)
    return jnp.pad(flat, (0, (-flat.shape[0]) % _SMALL_TILE)).reshape(-1, 128)


def _pack_small(ws, last=None):
    tail = jnp.zeros((), F32) if last is None else last
    return jnp.concatenate([_small_rows(ws[nm]) for nm in _SMALL] + [_small_rows(tail)], axis=0)


def _unpack_small(pack, like):
    out, r0 = {}, 0
    for nm in _SMALL:
        n = like[nm].size
        nr = 8 * -(-n // _SMALL_TILE)
        out[nm] = pack[r0:r0 + nr].reshape(-1)[:n].reshape(like[nm].shape)
        r0 += nr
    return out


def _residue_order(a):
    rows, cols = a.shape
    return a.reshape(rows // 16, 16, cols).transpose(1, 0, 2).reshape(rows, cols)


def _token_order(a):
    rows, cols = a.shape
    return a.reshape(16, rows // 16, cols).transpose(1, 0, 2).reshape(rows, cols)


_PAIRS_PER_TILE = PAIR_TILE // (2 * SSM_GROUP)
_PAIR_AXES = (SSM_PAIRS // _PAIRS_PER_TILE, _PAIRS_PER_TILE, 2)


def _pair_diagonals(acc):
    k, j, l = _PAIR_AXES
    eight = acc.reshape(k, j, j, l, SSM_GROUP, 2, l, SSM_STATE)
    eye_j, eye_l = jnp.eye(j, dtype=acc.dtype), jnp.eye(l, dtype=acc.dtype)
    own = jnp.einsum("kjJLcxln,jJ,lL->xkjlcn", eight, eye_j, eye_l).reshape(2, SSM_GROUPS, SSM_GROUP, SSM_STATE)
    return own[0], own[1]


def _local_step(xs, target, small, weights_of, send_grads, first_deps=()):
    rows = xs.shape[0]
    gfull, gsmall = {}, {}

    table_t = small["rel_bias_table"].T
    tables, bias4 = [], []
    for g in range(N_GROUPS):
        bucket, valid = [jnp.asarray(t) for t in _attn_tables(g, rows)]
        bias_g = _bias_fwd(f"rel_bias_fwd_{g}", bucket, valid, table_t[g * HEADS_PER_GROUP:(g + 1) * HEADS_PER_GROUP])
        tables.append(bucket)
        bias4.append(bias_g.reshape(-1, bias_g.shape[-1]))
    pw_re, pw_im, bb_mats, c_mats = _ssm_params_fwd(
        "ssm_params_fwd", small["ssm_a_re"], small["ssm_a_im"], small["ssm_log_dt"].reshape(SSM_GROUPS, 1),
        small["ssm_b_re"].transpose(2, 0, 1), small["ssm_b_im"].transpose(2, 0, 1), small["ssm_c_re"], small["ssm_c_im"])

    def power_rows(sign):
        row = jnp.concatenate([pw_re.reshape(SCAN_STEPS, 1, SSM_LANES), sign * pw_im.reshape(SCAN_STEPS, 1, SSM_LANES)],
                              axis=2)
        return jnp.broadcast_to(row, (SCAN_STEPS, SCAN_SUB, 2 * SSM_LANES))

    pw_fwd, pw_bwd = power_rows(1.0), power_rows(-1.0)
    d_skip = small["ssm_d"].reshape(1, SSM_WIDTH)
    wf = dict(weights_of("f1", [xs, target, bb_mats, c_mats, pw_fwd, pw_bwd] + bias4))

    x1, h1, gg1, uu1, hmix = _ffn_fwd("ffn1_fwd", xs, small["ffn1_norm"], wf["ffn1_w_gate"], wf["ffn1_w_up"],
                                      wf["ffn1_w_down"], small["mix_norm"], deps=first_deps)
    wf.update(weights_of("mix", x1))
    w_in = wf["w_in"]
    w_qkv, w_u, w_g = w_in[:3 * ATTN_WIDTH], w_in[3 * ATTN_WIDTH:3 * ATTN_WIDTH + SSM_WIDTH], w_in[3 * ATTN_WIDTH + SSM_WIDTH:]
    qscale = jnp.concatenate([jnp.full((1, ATTN_WIDTH), HEAD_DIM ** -0.5, F32), jnp.ones((1, 2 * ATTN_WIDTH), F32)], axis=1)
    u_col, g_col = 3 * ATTN_WIDTH, 3 * ATTN_WIDTH + SSM_WIDTH

    def in_split(acc, sc, b):
        return acc[:, :u_col] * sc, acc[:, u_col:g_col], _sigmoid(acc[:, g_col:] + b)

    qkv, u, gates = _mm("in_proj", [(hmix, w_in)], True, w_in.shape[0], [BF16, F32, BF16], epilogue=in_split,
                        extras=[(qscale, 0), (small["gate_bias"], 0)], tm=512, tn=w_in.shape[0],
                        out_cols=[3 * ATTN_WIDTH, SSM_WIDTH, 2 * D_MODEL])

    o_g, lse_g = [], []
    for g in range(N_GROUPS):
        o, lse = _attn_fwd(f"attn_fwd_{g}", qkv, g, bias4[g])
        o_g.append(o)
        lse_g.append(lse)
    oa_f32, oa = _combine_fwd("attn_combine_fwd", o_g, lse_g)
    y_attn, = _mm("attn_branch", [(oa, wf["w_attn_branch"])], True, D_MODEL, [BF16])
    y_raw, ygelu, states = _ssm_fwd("ssm_fwd", u, bb_mats, c_mats, pw_fwd, d_skip)
    glu, ysg = _mm("ssm_glu", [(ygelu, wf["ssm_w_glu"])], True, 2 * SSM_WIDTH, [BF16, BF16],
                   epilogue=lambda gv: (gv, gv[:, :SSM_WIDTH] * _sigmoid(gv[:, SSM_WIDTH:])),
                   tn=2 * SSM_WIDTH, out_cols=[2 * SSM_WIDTH, SSM_WIDTH])
    y_ssm, merged = _mm("ssm_branch_merge", [(ysg, wf["w_ssm_branch"])], True, D_MODEL, [BF16, BF16],
                        epilogue=lambda acc, ga, gs, ya: (acc, ga * ya + gs * acc),
                        extras=[(gates, 0), (gates, D_MODEL), (y_attn, 0)])
    x2, = _mm("mix_out", [(merged, wf["w_out"])], False, D_MODEL, [F32],
              epilogue=lambda acc, res: (res + acc,), extras=[(x1, 0)])
    wf.update(weights_of("f2", x2))
    dx3, h2, gg2, uu2, gsmall["final_norm"], gsmall["loss"] = _ffn_fwd_head(
        "ffn2_fwd", x2, small["ffn2_norm"], wf["ffn2_w_gate"], wf["ffn2_w_up"], wf["ffn2_w_down"],
        small["final_norm"].reshape(1, D_MODEL), target)

    dx2, dgg2, duu2, act2, gsmall["ffn2_norm"] = _ffn_bwd(
        "ffn2_bwd", dx3, x2, small["ffn2_norm"], gg2, uu2, wf["ffn2_w_gate"], wf["ffn2_w_up"], wf["ffn2_w_down"])
    gfull["ffn2_w_gate"] = _mm_tn("ffn2_dwg", dgg2, h2, out_dtype=BF16, tk=FFN_DW_ROWS)
    gfull["ffn2_w_up"] = _mm_tn("ffn2_dwu", duu2, h2, out_dtype=BF16, tk=FFN_DW_ROWS)
    gfull["ffn2_w_down"] = _mm_tn("ffn2_dwd", act2, dx3, scale=0.5, out_dtype=BF16, tk=FFN_DW_ROWS)
    sent = send_grads("f2", gfull)

    def merge_bwd(dm, ga, gs, ya, ys):
        dza, dzs = dm * ya * ga * (1.0 - ga), dm * ys * gs * (1.0 - gs)
        return (dm * ga, dm * gs, dza, dzs, jnp.sum(dza, axis=0, keepdims=True), jnp.sum(dzs, axis=0, keepdims=True))

    dya, dys, dzga, dzgs, dba, dbs = _mm(
        "mix_out_bwd", [(dx2, wf["w_out"])], True, D_MODEL, [BF16] * 4, epilogue=merge_bwd, row_sums=2,
        extras=[(gates, 0), (gates, D_MODEL), (y_attn, 0), (y_ssm, 0)], deps=sent, tm=512, tn=D_MODEL)
    gfull["w_out"] = _mm_tn("dw_out", merged, dx2, out_dtype=BF16)
    gsmall["gate_bias"] = jnp.concatenate([dba, dbs], axis=1)

    gfull["w_ssm_branch"] = _mm_tn("dw_ssm_branch", dys, ysg, out_dtype=BF16)

    def glu_bwd(dysg, av, bv):
        sb = _sigmoid(bv)
        return (dysg * sb, dysg * av * sb * (1.0 - sb))

    dglu_a, dglu_b = _mm("ssm_branch_bwd", [(dys, wf["w_ssm_branch"])], False, SSM_WIDTH, [BF16, BF16],
                         epilogue=glu_bwd, extras=[(glu, 0), (glu, SSM_WIDTH)])
    w_glu = wf["ssm_w_glu"]
    gfull["ssm_w_glu"] = _mm_tn_stack("dw_glu", [dglu_a, dglu_b], ygelu, out_dtype=BF16)

    def gelu_bwd(acc, yv):
        _, vjp = jax.vjp(jax.nn.gelu, yv)
        return (vjp(acc)[0],)

    dy_raw, = _mm("ssm_glu_bwd", [(dglu_a, w_glu[:SSM_WIDTH]), (dglu_b, w_glu[SSM_WIDTH:])], False, SSM_WIDTH, [F32],
                  epilogue=gelu_bwd, extras=[(y_raw, 0)])
    du, dbb_acc, dc_acc, dab_rows, gsmall_d = _ssm_bwd(
        "ssm_bwd", dy_raw, u, states, bb_mats, c_mats, pw_bwd, d_skip)
    gsmall["ssm_d"] = gsmall_d
    dbb_re, dbb_im = [a.transpose(1, 0, 2) for a in _pair_diagonals(dbb_acc)]
    dc_re, dc_im = _pair_diagonals(dc_acc)
    gsmall["ssm_c_re"], gsmall["ssm_c_im"] = dc_re, -dc_im
    dab = _colsum("ssm_dab", dab_rows)
    d_ar, d_ai, d_ld, d_br, d_bi = _ssm_params_bwd(
        "ssm_params_bwd", small["ssm_a_re"], small["ssm_a_im"], small["ssm_log_dt"].reshape(SSM_GROUPS, 1),
        small["ssm_b_re"].transpose(2, 0, 1), small["ssm_b_im"].transpose(2, 0, 1),
        dab[:, :SSM_LANES].reshape(SSM_GROUPS, SSM_STATE), dab[:, SSM_LANES:].reshape(SSM_GROUPS, SSM_STATE),
        dbb_re, dbb_im)
    gsmall["ssm_a_re"], gsmall["ssm_a_im"], gsmall["ssm_log_dt"] = d_ar, d_ai, d_ld.reshape(SSM_GROUPS)
    gsmall["ssm_b_re"], gsmall["ssm_b_im"] = d_br.transpose(1, 2, 0), d_bi.transpose(1, 2, 0)

    gfull["w_attn_branch"] = _mm_tn("dw_attn_branch", dya, oa, out_dtype=BF16)
    doa, = _mm("attn_branch_bwd", [(dya, wf["w_attn_branch"])], False, ATTN_OUT, [F32])
    dc = _combine_bwd("attn_combine_bwd", doa, oa_f32, lse_g)
    dqkv_cols = [None] * 9
    dtable = []
    for g in range(N_GROUPS):
        dq, dk, dv, db = _attn_bwd(f"attn_bwd_{g}", qkv, dc[g], lse_g[g], dc[3 + g], g, bias4[g])
        dqkv_cols[g], dqkv_cols[3 + g], dqkv_cols[6 + g] = dq, dk, dv
        dt = _bias_bwd(f"rel_bias_bwd_{g}", tables[g], db.reshape(HEADS_PER_GROUP, -1, db.shape[-1]))
        dtable.append(dt[:, :HEADS_PER_GROUP])
    gsmall["rel_bias_table"] = jnp.concatenate(dtable, axis=1)

    gfull["w_in"] = jnp.concatenate([_mm_tn_stack("dw_in_qkv", dqkv_cols, hmix, out_dtype=BF16),
                                     _mm_tn_stack("dw_in_rest", [du, dzga, dzgs], hmix, out_dtype=BF16)], axis=0)
    sent = send_grads("mix", gfull)
    qkv_pairs = [(c, w_qkv[i * ATTN_OUT:(i + 1) * ATTN_OUT]) for i, c in enumerate(dqkv_cols)]

    def mix_norm_bwd(dh, xv, gain, dres):
        r, xh = _rms_parts(xv)
        return dres + _rms_bwd_dx(dh, gain, r, xh), jnp.sum(dh * xh, axis=0, keepdims=True)

    dx1, gsmall["mix_norm"] = _mm(
        "in_bwd", qkv_pairs + [(du, w_u), (dzga, w_g[:D_MODEL]), (dzgs, w_g[D_MODEL:])], False, D_MODEL, [F32],
        epilogue=mix_norm_bwd, row_sums=1, extras=[(x1, 0), (small["mix_norm"], 0), (dx2, 0)], tm=512, tn=D_MODEL,
        deps=sent)

    dx, dgg1, duu1, act1, gsmall["ffn1_norm"] = _ffn_bwd(
        "ffn1_bwd", dx1, xs, small["ffn1_norm"], gg1, uu1, wf["ffn1_w_gate"], wf["ffn1_w_up"], wf["ffn1_w_down"])
    sent = send_grads("small", gsmall)
    gfull["ffn1_w_gate"] = _mm_tn("ffn1_dwg", dgg1, h1, deps=sent, out_dtype=BF16, tk=FFN_DW_ROWS)
    gfull["ffn1_w_up"] = _mm_tn("ffn1_dwu", duu1, h1, out_dtype=BF16, tk=FFN_DW_ROWS)
    sent = send_grads("f1gu", gfull)
    gfull["ffn1_w_down"] = _mm_tn("ffn1_dwd", act1, dx1, scale=0.5, deps=sent, out_dtype=BF16, tk=FFN_DW_ROWS)
    send_grads("f1d", gfull)
    return dx, gsmall


def kernel(x, ffn1_norm, ffn1_w_gate, ffn1_w_up, ffn1_w_down, mix_norm, w_in, gate_bias, rel_bias_table, ssm_a_re, ssm_a_im, ssm_log_dt, ssm_b_re, ssm_b_im, ssm_c_re, ssm_c_im, ssm_d, ssm_w_glu, w_attn_branch, w_ssm_branch, w_out, ffn2_norm, ffn2_w_gate, ffn2_w_up, ffn2_w_down, final_norm, loss_target, m_ffn1_norm, m_ffn1_w_gate, m_ffn1_w_up, m_ffn1_w_down, m_mix_norm, m_w_in, m_gate_bias, m_rel_bias_table, m_ssm_a_re, m_ssm_a_im, m_ssm_log_dt, m_ssm_b_re, m_ssm_b_im, m_ssm_c_re, m_ssm_c_im, m_ssm_d, m_ssm_w_glu, m_w_attn_branch, m_w_ssm_branch, m_w_out, m_ffn2_norm, m_ffn2_w_gate, m_ffn2_w_up, m_ffn2_w_down, m_final_norm, v_ffn1_norm, v_ffn1_w_gate, v_ffn1_w_up, v_ffn1_w_down, v_mix_norm, v_w_in, v_gate_bias, v_rel_bias_table, v_ssm_a_re, v_ssm_a_im, v_ssm_log_dt, v_ssm_b_re, v_ssm_b_im, v_ssm_c_re, v_ssm_c_im, v_ssm_d, v_ssm_w_glu, v_w_attn_branch, v_w_ssm_branch, v_w_out, v_ffn2_norm, v_ffn2_w_gate, v_ffn2_w_up, v_ffn2_w_down, v_final_norm):
    given = dict(locals())
    shapes = {nm: given[nm].shape for nm in _ORDER}

    def strip(a):
        return a[0] if a.ndim >= 2 and a.shape[0] == 1 else a

    w = {nm: strip(given[nm]) for nm in _ORDER}
    m = {nm: strip(given["m_" + nm]) for nm in _ORDER}
    v = {nm: strip(given["v_" + nm]) for nm in _ORDER}
    for d in (w, m, v):
        d["rel_bias_table"] = d["rel_bias_table"].reshape(N_BUCKETS, N_GROUPS * HEADS_PER_GROUP)

    weight_phases = {"f1": _PHASES["f1gu"] + _PHASES["f1d"], "mix": _PHASES["mix"], "f2": _PHASES["f2"]}
    pending_w, w_rows, deps, zero = {}, {}, [], 0.0
    for phase, names in weight_phases.items():
        w_rows.update({nm: _to_rows(w[nm] + zero, nm) for nm in names})
        pending_w[phase] = _exchange_start(f"gather_{phase}_start", [w_rows[nm].astype(BF16) for nm in names],
                                           gather=True, deps=deps)
        deps = [pending_w[phase][4]]
        zero = pending_w["f1"][4][0, 0]
    m_rows = {nm: _to_rows(m[nm] + zero, nm) for nm in _SHARD_INFO}
    v_rows = {nm: _to_rows(v[nm] + zero, nm) for nm in _SHARD_INFO}
    small = {nm: w[nm] for nm in _SMALL}
    small_in = {nm: small[nm] + zero for nm in _SMALL}
    for nm in ("ffn1_norm", "mix_norm", "ffn2_norm", "gate_bias"):
        small_in[nm] = small_in[nm].reshape(1, -1)

    def weights_of(phase, after):
        if phase == "f1":
            after = list(after) + list(m_rows.values()) + list(v_rows.values())
        landed = _exchange_wait(f"gather_{phase}_wait", pending_w[phase], after, gather=True)
        return {nm: _full_weight(got, nm) for nm, got in zip(weight_phases[phase], landed)}

    pending_g = {}

    def send_grads(phase, grads):
        if phase == "small":
            gs_pack = _pack_small({nm: grads[nm].reshape(small[nm].shape) for nm in _SMALL}, last=grads["loss"])
            pending_g[phase] = _exchange_start("gather_small_start", [gs_pack], gather=True)
        else:
            pending_g[phase] = _exchange_start(f"scatter_{phase}_start",
                                               [_grad_blocks(grads[nm], nm) for nm in _PHASES[phase]], gather=False)
        return [pending_g[phase][4]]

    dx, gsmall = _local_step(_residue_order(x[0]), _residue_order(loss_target[0]), small_in, weights_of, send_grads,
                             first_deps=[pending_w["f2"][4]])
    dx = _token_order(dx)

    updated = {}
    after = pending_g["f1d"][4]
    for phase in ("f2", "mix", "small", "f1gu", "f1d"):
        landed = _exchange_wait(f"exchange_{phase}_wait", pending_g[phase], after, gather=phase == "small")
        if phase == "small":
            sm = _adamw("adamw_small", _pack_small(small), _pack_small({nm: m[nm] for nm in _SMALL}),
                        _pack_small({nm: v[nm] for nm in _SMALL}), landed[0], landed[0].shape[1])
            after = sm[0]
            continue
        for nm, recv in zip(_PHASES[phase], landed):
            tr = max(t for t in range(16, 353, 16) if w_rows[nm].shape[0] % t == 0)
            updated[nm] = _adamw(f"adamw_{nm}", w_rows[nm], m_rows[nm], v_rows[nm], recv, tr)
            after = updated[nm][0]

    loss = sm[0][-8, 0]
    outs = []
    for i in range(4):
        sml = _unpack_small(sm[i], small)
        outs.append([(_from_rows(updated[nm][i], nm) if nm in updated else sml[nm]).reshape(shapes[nm])
                     for nm in _ORDER])
    return (loss, dx[None], *outs[0], *outs[1], *outs[2], *outs[3])
```
